```python
import jax, jax.numpy as jnp
from jax import lax
import numpy as np

D_MODEL = 1024
BATCH = 8
SEQ = 2048
DEPTH = 2

GRID_W = 64
CTX_LEN = 256
N_MIXERS = 2
SSD_EXPAND = 2
SSD_D_INNER = SSD_EXPAND * D_MODEL
SSD_HEAD_DIM = 64
SSD_HEADS = SSD_D_INNER // SSD_HEAD_DIM
SSD_GROUPS = 8
SSD_HPG = SSD_HEADS // SSD_GROUPS
SSD_STATE = 128
SSD_CONV = 5
SSD_CHUNK = 128
SSD_CONV_DIM = SSD_D_INNER + 2 * SSD_GROUPS * SSD_STATE
SSD_IN_DIM = SSD_D_INNER + SSD_CONV_DIM + 2 * SSD_HEADS
CONF_KERNEL = 31
FFN_HIDDEN = ((8 * D_MODEL // 3 + 255) // 256) * 256
FFN_CONV = 3
N_SSD_LAYERS = (DEPTH + 1) // 2
N_CONF_LAYERS = DEPTH // 2
EPS = 1e-6

kernel_name = 'hybrid_ssd_conformer_dit_ctx_prefix'


def rmsnorm(h, w):
    hf = h.astype(jnp.float32)
    y = hf * lax.rsqrt(jnp.mean(hf * hf, axis=-1, keepdims=True) + EPS)
    return (y * w.astype(jnp.float32)).astype(h.dtype)


def layernorm(h, w, b):
    hf = h.astype(jnp.float32)
    mu = jnp.mean(hf, axis=-1, keepdims=True)
    d = hf - mu
    y = d * lax.rsqrt(jnp.mean(d * d, axis=-1, keepdims=True) + EPS)
    return (y * w.astype(jnp.float32) + b.astype(jnp.float32)).astype(h.dtype)


def modulate(h, g, shift, scale):
    return rmsnorm(h, g) * (1 + scale) + shift


def ada_params(cond, w, b):
    m = jax.nn.silu(cond) @ w + b
    return jnp.split(m, 6, axis=-1)


def dwconv1d(u, w, b):
    k, ch = w.shape
    pad = k // 2
    y = lax.conv_general_dilated(u, w[:, None, :].astype(u.dtype), window_strides=(1,),
                                 padding=[(pad, pad)], dimension_numbers=('NWC', 'WIO', 'NWC'),
                                 feature_group_count=ch)
    return y + b


def dwconv2d_grid(u, w, b):
    bsz, l, ch = u.shape
    rows = l // GRID_W
    u4 = u.reshape(bsz, rows, GRID_W, ch)
    kh, kw, _ = w.shape
    y = lax.conv_general_dilated(u4, w[:, :, None, :].astype(u.dtype), window_strides=(1, 1),
                                 padding=[(kh // 2, kh // 2), (kw // 2, kw // 2)],
                                 dimension_numbers=('NHWC', 'HWIO', 'NHWC'),
                                 feature_group_count=ch)
    return y.reshape(bsz, l, ch) + b


def ssd_scan(x, dt, A, B, C, s0):
    bsz, l, g, r, p = x.shape
    n = B.shape[-1]
    q = SSD_CHUNK
    nc = l // q
    x = x.astype(jnp.float32).reshape(bsz, nc, q, g, r, p)
    dt = dt.reshape(bsz, nc, q, g, r)
    B = B.astype(jnp.float32).reshape(bsz, nc, q, g, n)
    C = C.astype(jnp.float32).reshape(bsz, nc, q, g, n)
    acum = jnp.cumsum(dt * A, axis=2)
    xdt = x * dt[..., None]
    seg = acum[:, :, :, None] - acum[:, :, None, :]
    mask = jnp.tril(jnp.ones((q, q), dtype=bool))[:, :, None, None]
    decay = jnp.exp(jnp.where(mask, seg, -jnp.inf))
    cb = jnp.einsum('bcign,bcjgn->bcijg', C, B)
    y_diag = jnp.einsum('bcijgr,bcjgrp->bcigrp', cb[..., None] * decay, xdt)
    decay_to_end = jnp.exp(acum[:, :, -1:] - acum)
    chunk_states = jnp.einsum('bcjgn,bcjgrp->bcgrpn', B, xdt * decay_to_end[..., None])
    chunk_decay = jnp.exp(acum[:, :, -1])

    def step(s, inp):
        dec, st = inp
        return dec[..., None, None] * s + st, s

    final, entering = lax.scan(step, s0.astype(jnp.float32),
                               (jnp.moveaxis(chunk_decay, 1, 0), jnp.moveaxis(chunk_states, 1, 0)))
    entering = jnp.moveaxis(entering, 0, 1)
    y_off = jnp.einsum('bcign,bcgrpn->bcigrp', C, entering) * jnp.exp(acum)[..., None]
    y = (y_diag + y_off).reshape(bsz, l, g, r, p)
    return y, final


def ssd_mixer(u, w_in, conv_w, conv_b, dt_bias, a_log, d_skip, norm_w, w_out, s0_fwd, s0_bwd):
    bsz, l, _ = u.shape
    di, gn = SSD_D_INNER, SSD_GROUPS * SSD_STATE
    proj = u @ w_in
    z = proj[..., :di]
    xbc = jax.nn.silu(dwconv1d(proj[..., di:di + SSD_CONV_DIM], conv_w, conv_b))
    dt_raw = proj[..., di + SSD_CONV_DIM:]
    xs = xbc[..., :di].reshape(bsz, l, SSD_GROUPS, SSD_HPG, SSD_HEAD_DIM)
    Bm = xbc[..., di:di + gn].reshape(bsz, l, SSD_GROUPS, SSD_STATE)
    Cm = xbc[..., di + gn:].reshape(bsz, l, SSD_GROUPS, SSD_STATE)
    dt = jax.nn.softplus(dt_raw.astype(jnp.float32).reshape(bsz, l, 2, SSD_GROUPS, SSD_HPG)
                         + dt_bias.astype(jnp.float32).reshape(2, SSD_GROUPS, SSD_HPG))
    A = -jnp.exp(a_log.astype(jnp.float32)).reshape(2, SSD_GROUPS, SSD_HPG)
    y_f, s_f = ssd_scan(xs, dt[:, :, 0], A[0], Bm, Cm, s0_fwd)
    y_b, s_b = ssd_scan(jnp.flip(xs, 1), jnp.flip(dt[:, :, 1], 1), A[1],
                        jnp.flip(Bm, 1), jnp.flip(Cm, 1), s0_bwd)
    y = y_f + jnp.flip(y_b, 1) + d_skip.astype(jnp.float32).reshape(SSD_GROUPS, SSD_HPG)[:, :, None] * xs.astype(jnp.float32)
    y = y.reshape(bsz, l, di) * jax.nn.silu(z.astype(jnp.float32))
    out = rmsnorm(y, norm_w).astype(u.dtype) @ w_out
    return out, (s_f, s_b)


def conformer_conv(u, w1, b1, w_dw, b_dw, ln_w, ln_b, w2, b2):
    h = u @ w1 + b1
    a, g = jnp.split(h, 2, axis=-1)
    h = a * jax.nn.sigmoid(g)
    h = dwconv1d(h, w_dw, b_dw)
    h = jax.nn.silu(layernorm(h, ln_w, ln_b))
    return h @ w2 + b2


def conv_ffn(u, w_up, conv_w, conv_b, w_down, on_grid):
    h = u @ w_up
    val, gate = jnp.split(h, 2, axis=-1)
    if on_grid:
        gate = dwconv2d_grid(gate, conv_w, conv_b)
    else:
        gate = dwconv1d(gate, conv_w[FFN_CONV // 2], conv_b)
    return (jax.nn.silu(gate) * val) @ w_down


def _fwd_setup_inputs(seed: int = 0) -> dict:
    key = jax.random.key(seed)
    ks = jax.random.split(key, 32)
    f32 = jnp.float32

    def nrm(k, shape, scale):
        return jax.random.normal(k, shape, f32) * scale

    D = D_MODEL
    u = jax.random.uniform(ks[10], (N_SSD_LAYERS, 2, SSD_HEADS), f32)
    dt0 = jnp.exp(u * (np.log(0.1) - np.log(0.001)) + np.log(0.001)).astype(f32)
    dt_bias = dt0 + jnp.log(-jnp.expm1(-dt0))
    a_log = jnp.log(jax.random.uniform(ks[11], (N_SSD_LAYERS, 2, SSD_HEADS), f32, 1.0, 16.0))
    return {
        'x': nrm(ks[0], (BATCH, SEQ, D), 1.0),
        'c': nrm(ks[1], (BATCH, D), 1.0),
        'ctx': nrm(ks[2], (BATCH, CTX_LEN, D), 1.0),
        'c_ctx': nrm(ks[3], (D,), 1.0),
        'mod_w': nrm(ks[4], (DEPTH, D, 6 * D), 0.5 * D ** -0.5),
        'mod_b': nrm(ks[5], (DEPTH, 6 * D), 0.02),
        'norm1_w': 1.0 + nrm(ks[6], (DEPTH, D), 0.02),
        'norm2_w': 1.0 + nrm(ks[7], (DEPTH, D), 0.02),
        'ssd_w_in': nrm(ks[8], (N_SSD_LAYERS, D, SSD_IN_DIM), D ** -0.5),
        'ssd_conv_w': nrm(ks[9], (N_SSD_LAYERS, SSD_CONV, SSD_CONV_DIM), SSD_CONV ** -0.5),
        'ssd_conv_b': nrm(ks[12], (N_SSD_LAYERS, SSD_CONV_DIM), 0.02),
        'ssd_dt_bias': dt_bias,
        'ssd_a_log': a_log,
        'ssd_d': 1.0 + nrm(ks[13], (N_SSD_LAYERS, SSD_HEADS), 0.1),
        'ssd_norm_w': 1.0 + nrm(ks[14], (N_SSD_LAYERS, SSD_D_INNER), 0.02),
        'ssd_w_out': nrm(ks[15], (N_SSD_LAYERS, SSD_D_INNER, D), SSD_D_INNER ** -0.5),
        'conf_w_pw1': nrm(ks[16], (N_CONF_LAYERS, D, 2 * D), D ** -0.5),
        'conf_b_pw1': nrm(ks[17], (N_CONF_LAYERS, 2 * D), 0.02),
        'conf_w_dw': nrm(ks[18], (N_CONF_LAYERS, CONF_KERNEL, D), CONF_KERNEL ** -0.5),
        'conf_b_dw': nrm(ks[19], (N_CONF_LAYERS, D), 0.02),
        'conf_ln_w': 1.0 + nrm(ks[20], (N_CONF_LAYERS, D), 0.02),
        'conf_ln_b': nrm(ks[21], (N_CONF_LAYERS, D), 0.02),
        'conf_w_pw2': nrm(ks[22], (N_CONF_LAYERS, D, D), D ** -0.5),
        'conf_b_pw2': nrm(ks[23], (N_CONF_LAYERS, D), 0.02),
        'ffn_w_up': nrm(ks[24], (DEPTH, D, 2 * FFN_HIDDEN), D ** -0.5),
        'ffn_conv_w': nrm(ks[25], (DEPTH, FFN_CONV, FFN_CONV, FFN_HIDDEN), (FFN_CONV * FFN_CONV) ** -0.5),
        'ffn_conv_b': nrm(ks[26], (DEPTH, FFN_HIDDEN), 0.02),
        'ffn_w_down': nrm(ks[27], (DEPTH, FFN_HIDDEN, D), FFN_HIDDEN ** -0.5),
        'final_norm_w': 1.0 + nrm(ks[28], (D,), 0.02),
    }


def _fwd_reference(x, c, ctx, c_ctx, mod_w, mod_b, norm1_w, norm2_w,
              ssd_w_in, ssd_conv_w, ssd_conv_b, ssd_dt_bias, ssd_a_log, ssd_d, ssd_norm_w, ssd_w_out,
              conf_w_pw1, conf_b_pw1, conf_w_dw, conf_b_dw, conf_ln_w, conf_ln_b, conf_w_pw2, conf_b_pw2,
              ffn_w_up, ffn_conv_w, ffn_conv_b, ffn_w_down, final_norm_w):
    h, hc = x, ctx
    bsz = x.shape[0]
    for i in range(DEPTH):
        kind = i % N_MIXERS
        j = i // N_MIXERS
        last = i == DEPTH - 1
        need_ctx = (not last) or kind == 0
        sh1, sc1, g1, sh2, sc2, g2 = ada_params(c[:, None, :], mod_w[i], mod_b[i])
        a = modulate(h, norm1_w[i], sh1, sc1)
        if need_ctx:
            csh1, csc1, cg1, csh2, csc2, cg2 = ada_params(c_ctx, mod_w[i], mod_b[i])
            ac = modulate(hc, norm1_w[i], csh1, csc1)
        if kind == 0:
            ssd_p = (ssd_w_in[j], ssd_conv_w[j], ssd_conv_b[j], ssd_dt_bias[j], ssd_a_log[j],
                     ssd_d[j], ssd_norm_w[j], ssd_w_out[j])
            zeros = jnp.zeros((bsz, SSD_GROUPS, SSD_HPG, SSD_HEAD_DIM, SSD_STATE), jnp.float32)
            yc, (s_f, s_b) = ssd_mixer(ac, *ssd_p, zeros, zeros)
            y, _ = ssd_mixer(a, *ssd_p, s_f, s_b)
        else:
            conf_p = (conf_w_pw1[j], conf_b_pw1[j], conf_w_dw[j], conf_b_dw[j],
                      conf_ln_w[j], conf_ln_b[j], conf_w_pw2[j], conf_b_pw2[j])
            y = conformer_conv(a, *conf_p)
            if not last:
                yc = conformer_conv(ac, *conf_p)
        h = h + g1 * y
        h = h + g2 * conv_ffn(modulate(h, norm2_w[i], sh2, sc2), ffn_w_up[i], ffn_conv_w[i],
                              ffn_conv_b[i], ffn_w_down[i], True)
        if not last:
            hc = hc + cg1 * yc
            hc = hc + cg2 * conv_ffn(modulate(hc, norm2_w[i], csh2, csc2), ffn_w_up[i], ffn_conv_w[i],
                                     ffn_conv_b[i], ffn_w_down[i], False)
    return rmsnorm(h, final_norm_w)


import jax as _jax
import jax.numpy as _jnp

TWIN_FORMAT = 'train_step'
FWD_PARAMS = ['x', 'c', 'ctx', 'c_ctx', 'mod_w', 'mod_b', 'norm1_w', 'norm2_w', 'ssd_w_in', 'ssd_conv_w', 'ssd_conv_b', 'ssd_dt_bias', 'ssd_a_log', 'ssd_d', 'ssd_norm_w', 'ssd_w_out', 'conf_w_pw1', 'conf_b_pw1', 'conf_w_dw', 'conf_b_dw', 'conf_ln_w', 'conf_ln_b', 'conf_w_pw2', 'conf_b_pw2', 'ffn_w_up', 'ffn_conv_w', 'ffn_conv_b', 'ffn_w_down', 'final_norm_w']
TWIN_WEIGHTS = ['c_ctx', 'mod_w', 'mod_b', 'norm1_w', 'norm2_w', 'ssd_w_in', 'ssd_conv_w', 'ssd_conv_b', 'ssd_dt_bias', 'ssd_a_log', 'ssd_d', 'ssd_norm_w', 'ssd_w_out', 'conf_w_pw1', 'conf_b_pw1', 'conf_w_dw', 'conf_b_dw', 'conf_ln_w', 'conf_ln_b', 'conf_w_pw2', 'conf_b_pw2', 'ffn_w_up', 'ffn_conv_w', 'ffn_conv_b', 'ffn_w_down', 'final_norm_w']
TWIN_DIFF_INPUT = 'x'
TWIN_INPUTS = ['x', 'c', 'ctx', 'c_ctx', 'mod_w', 'mod_b', 'norm1_w', 'norm2_w', 'ssd_w_in', 'ssd_conv_w', 'ssd_conv_b', 'ssd_dt_bias', 'ssd_a_log', 'ssd_d', 'ssd_norm_w', 'ssd_w_out', 'conf_w_pw1', 'conf_b_pw1', 'conf_w_dw', 'conf_b_dw', 'conf_ln_w', 'conf_ln_b', 'conf_w_pw2', 'conf_b_pw2', 'ffn_w_up', 'ffn_conv_w', 'ffn_conv_b', 'ffn_w_down', 'final_norm_w', 'loss_target', 'm_c_ctx', 'm_mod_w', 'm_mod_b', 'm_norm1_w', 'm_norm2_w', 'm_ssd_w_in', 'm_ssd_conv_w', 'm_ssd_conv_b', 'm_ssd_dt_bias', 'm_ssd_a_log', 'm_ssd_d', 'm_ssd_norm_w', 'm_ssd_w_out', 'm_conf_w_pw1', 'm_conf_b_pw1', 'm_conf_w_dw', 'm_conf_b_dw', 'm_conf_ln_w', 'm_conf_ln_b', 'm_conf_w_pw2', 'm_conf_b_pw2', 'm_ffn_w_up', 'm_ffn_conv_w', 'm_ffn_conv_b', 'm_ffn_w_down', 'm_final_norm_w', 'v_c_ctx', 'v_mod_w', 'v_mod_b', 'v_norm1_w', 'v_norm2_w', 'v_ssd_w_in', 'v_ssd_conv_w', 'v_ssd_conv_b', 'v_ssd_dt_bias', 'v_ssd_a_log', 'v_ssd_d', 'v_ssd_norm_w', 'v_ssd_w_out', 'v_conf_w_pw1', 'v_conf_b_pw1', 'v_conf_w_dw', 'v_conf_b_dw', 'v_conf_ln_w', 'v_conf_ln_b', 'v_conf_w_pw2', 'v_conf_b_pw2', 'v_ffn_w_up', 'v_ffn_conv_w', 'v_ffn_conv_b', 'v_ffn_w_down', 'v_final_norm_w']
TWIN_OUTPUTS = ['loss', 'grad_x', 'grad_c_ctx', 'grad_mod_w', 'grad_mod_b', 'grad_norm1_w', 'grad_norm2_w', 'grad_ssd_w_in', 'grad_ssd_conv_w', 'grad_ssd_conv_b', 'grad_ssd_dt_bias', 'grad_ssd_a_log', 'grad_ssd_d', 'grad_ssd_norm_w', 'grad_ssd_w_out', 'grad_conf_w_pw1', 'grad_conf_b_pw1', 'grad_conf_w_dw', 'grad_conf_b_dw', 'grad_conf_ln_w', 'grad_conf_ln_b', 'grad_conf_w_pw2', 'grad_conf_b_pw2', 'grad_ffn_w_up', 'grad_ffn_conv_w', 'grad_ffn_conv_b', 'grad_ffn_w_down', 'grad_final_norm_w', 'delta_c_ctx', 'delta_mod_w', 'delta_mod_b', 'delta_norm1_w', 'delta_norm2_w', 'delta_ssd_w_in', 'delta_ssd_conv_w', 'delta_ssd_conv_b', 'delta_ssd_dt_bias', 'delta_ssd_a_log', 'delta_ssd_d', 'delta_ssd_norm_w', 'delta_ssd_w_out', 'delta_conf_w_pw1', 'delta_conf_b_pw1', 'delta_conf_w_dw', 'delta_conf_b_dw', 'delta_conf_ln_w', 'delta_conf_ln_b', 'delta_conf_w_pw2', 'delta_conf_b_pw2', 'delta_ffn_w_up', 'delta_ffn_conv_w', 'delta_ffn_conv_b', 'delta_ffn_w_down', 'delta_final_norm_w', 'new_m_c_ctx', 'new_m_mod_w', 'new_m_mod_b', 'new_m_norm1_w', 'new_m_norm2_w', 'new_m_ssd_w_in', 'new_m_ssd_conv_w', 'new_m_ssd_conv_b', 'new_m_ssd_dt_bias', 'new_m_ssd_a_log', 'new_m_ssd_d', 'new_m_ssd_norm_w', 'new_m_ssd_w_out', 'new_m_conf_w_pw1', 'new_m_conf_b_pw1', 'new_m_conf_w_dw', 'new_m_conf_b_dw', 'new_m_conf_ln_w', 'new_m_conf_ln_b', 'new_m_conf_w_pw2', 'new_m_conf_b_pw2', 'new_m_ffn_w_up', 'new_m_ffn_conv_w', 'new_m_ffn_conv_b', 'new_m_ffn_w_down', 'new_m_final_norm_w', 'new_v_c_ctx', 'new_v_mod_w', 'new_v_mod_b', 'new_v_norm1_w', 'new_v_norm2_w', 'new_v_ssd_w_in', 'new_v_ssd_conv_w', 'new_v_ssd_conv_b', 'new_v_ssd_dt_bias', 'new_v_ssd_a_log', 'new_v_ssd_d', 'new_v_ssd_norm_w', 'new_v_ssd_w_out', 'new_v_conf_w_pw1', 'new_v_conf_b_pw1', 'new_v_conf_w_dw', 'new_v_conf_b_dw', 'new_v_conf_ln_w', 'new_v_conf_ln_b', 'new_v_conf_w_pw2', 'new_v_conf_b_pw2', 'new_v_ffn_w_up', 'new_v_ffn_conv_w', 'new_v_ffn_conv_b', 'new_v_ffn_w_down', 'new_v_final_norm_w']
TWIN_LEAF_KINDS = {'loss': 'loss', 'grad_x': 'grad_x', 'grad_c_ctx': 'grad_w', 'grad_mod_w': 'grad_w', 'grad_mod_b': 'grad_w', 'grad_norm1_w': 'grad_w', 'grad_norm2_w': 'grad_w', 'grad_ssd_w_in': 'grad_w', 'grad_ssd_conv_w': 'grad_w', 'grad_ssd_conv_b': 'grad_w', 'grad_ssd_dt_bias': 'grad_w', 'grad_ssd_a_log': 'grad_w', 'grad_ssd_d': 'grad_w', 'grad_ssd_norm_w': 'grad_w', 'grad_ssd_w_out': 'grad_w', 'grad_conf_w_pw1': 'grad_w', 'grad_conf_b_pw1': 'grad_w', 'grad_conf_w_dw': 'grad_w', 'grad_conf_b_dw': 'grad_w', 'grad_conf_ln_w': 'grad_w', 'grad_conf_ln_b': 'grad_w', 'grad_conf_w_pw2': 'grad_w', 'grad_conf_b_pw2': 'grad_w', 'grad_ffn_w_up': 'grad_w', 'grad_ffn_conv_w': 'grad_w', 'grad_ffn_conv_b': 'grad_w', 'grad_ffn_w_down': 'grad_w', 'grad_final_norm_w': 'grad_w', 'delta_c_ctx': 'delta_w', 'delta_mod_w': 'delta_w', 'delta_mod_b': 'delta_w', 'delta_norm1_w': 'delta_w', 'delta_norm2_w': 'delta_w', 'delta_ssd_w_in': 'delta_w', 'delta_ssd_conv_w': 'delta_w', 'delta_ssd_conv_b': 'delta_w', 'delta_ssd_dt_bias': 'delta_w', 'delta_ssd_a_log': 'delta_w', 'delta_ssd_d': 'delta_w', 'delta_ssd_norm_w': 'delta_w', 'delta_ssd_w_out': 'delta_w', 'delta_conf_w_pw1': 'delta_w', 'delta_conf_b_pw1': 'delta_w', 'delta_conf_w_dw': 'delta_w', 'delta_conf_b_dw': 'delta_w', 'delta_conf_ln_w': 'delta_w', 'delta_conf_ln_b': 'delta_w', 'delta_conf_w_pw2': 'delta_w', 'delta_conf_b_pw2': 'delta_w', 'delta_ffn_w_up': 'delta_w', 'delta_ffn_conv_w': 'delta_w', 'delta_ffn_conv_b': 'delta_w', 'delta_ffn_w_down': 'delta_w', 'delta_final_norm_w': 'delta_w', 'new_m_c_ctx': 'new_m', 'new_m_mod_w': 'new_m', 'new_m_mod_b': 'new_m', 'new_m_norm1_w': 'new_m', 'new_m_norm2_w': 'new_m', 'new_m_ssd_w_in': 'new_m', 'new_m_ssd_conv_w': 'new_m', 'new_m_ssd_conv_b': 'new_m', 'new_m_ssd_dt_bias': 'new_m', 'new_m_ssd_a_log': 'new_m', 'new_m_ssd_d': 'new_m', 'new_m_ssd_norm_w': 'new_m', 'new_m_ssd_w_out': 'new_m', 'new_m_conf_w_pw1': 'new_m', 'new_m_conf_b_pw1': 'new_m', 'new_m_conf_w_dw': 'new_m', 'new_m_conf_b_dw': 'new_m', 'new_m_conf_ln_w': 'new_m', 'new_m_conf_ln_b': 'new_m', 'new_m_conf_w_pw2': 'new_m', 'new_m_conf_b_pw2': 'new_m', 'new_m_ffn_w_up': 'new_m', 'new_m_ffn_conv_w': 'new_m', 'new_m_ffn_conv_b': 'new_m', 'new_m_ffn_w_down': 'new_m', 'new_m_final_norm_w': 'new_m', 'new_v_c_ctx': 'new_v', 'new_v_mod_w': 'new_v', 'new_v_mod_b': 'new_v', 'new_v_norm1_w': 'new_v', 'new_v_norm2_w': 'new_v', 'new_v_ssd_w_in': 'new_v', 'new_v_ssd_conv_w': 'new_v', 'new_v_ssd_conv_b': 'new_v', 'new_v_ssd_dt_bias': 'new_v', 'new_v_ssd_a_log': 'new_v', 'new_v_ssd_d': 'new_v', 'new_v_ssd_norm_w': 'new_v', 'new_v_ssd_w_out': 'new_v', 'new_v_conf_w_pw1': 'new_v', 'new_v_conf_b_pw1': 'new_v', 'new_v_conf_w_dw': 'new_v', 'new_v_conf_b_dw': 'new_v', 'new_v_conf_ln_w': 'new_v', 'new_v_conf_ln_b': 'new_v', 'new_v_conf_w_pw2': 'new_v', 'new_v_conf_b_pw2': 'new_v', 'new_v_ffn_w_up': 'new_v', 'new_v_ffn_conv_w': 'new_v', 'new_v_ffn_conv_b': 'new_v', 'new_v_ffn_w_down': 'new_v', 'new_v_final_norm_w': 'new_v'}


def _forward(args):
    return _fwd_reference(*[args[k] for k in FWD_PARAMS])


def _output_shape():
    out = _jax.eval_shape(lambda: _forward(_fwd_setup_inputs(0)))
    return out.shape, out.dtype

N_MICROBATCH = 1
ADAM_LR = 0.001
ADAM_B1 = 0.9
ADAM_B2 = 0.999
ADAM_EPS = 1e-08
ADAM_WD = 0.01
ADAM_STEP = 10
PER_EXAMPLE_BATCH_AXIS = {'x': 0, 'c': 0, 'ctx': 0, 'loss_target': 0}
SHARED_INPUTS = []
_WEIGHT_DTYPES = {'c_ctx': _jnp.float32, 'mod_w': _jnp.float32, 'mod_b': _jnp.float32, 'norm1_w': _jnp.float32, 'norm2_w': _jnp.float32, 'ssd_w_in': _jnp.float32, 'ssd_conv_w': _jnp.float32, 'ssd_conv_b': _jnp.float32, 'ssd_dt_bias': _jnp.float32, 'ssd_a_log': _jnp.float32, 'ssd_d': _jnp.float32, 'ssd_norm_w': _jnp.float32, 'ssd_w_out': _jnp.float32, 'conf_w_pw1': _jnp.float32, 'conf_b_pw1': _jnp.float32, 'conf_w_dw': _jnp.float32, 'conf_b_dw': _jnp.float32, 'conf_ln_w': _jnp.float32, 'conf_ln_b': _jnp.float32, 'conf_w_pw2': _jnp.float32, 'conf_b_pw2': _jnp.float32, 'ffn_w_up': _jnp.float32, 'ffn_conv_w': _jnp.float32, 'ffn_conv_b': _jnp.float32, 'ffn_w_down': _jnp.float32, 'final_norm_w': _jnp.float32}
MOMENT_SCALE = {'c_ctx': 3.504371e-03, 'mod_w': 4.251505e-02, 'mod_b': 6.966845e-02, 'norm1_w': 4.336284e-02, 'norm2_w': 3.669195e-02, 'ssd_w_in': 2.351113e-02, 'ssd_conv_w': 2.049040e-02, 'ssd_conv_b': 2.759416e-02, 'ssd_dt_bias': 6.292049e-02, 'ssd_a_log': 7.646961e-02, 'ssd_d': 9.224342e-02, 'ssd_norm_w': 2.705220e-02, 'ssd_w_out': 3.856893e-02, 'conf_w_pw1': 1.720752e-02, 'conf_b_pw1': 1.715704e-02, 'conf_w_dw': 2.267638e-02, 'conf_b_dw': 3.882408e-02, 'conf_ln_w': 2.611184e-02, 'conf_ln_b': 2.336109e-02, 'conf_w_pw2': 2.213982e-02, 'conf_b_pw2': 4.127119e-02, 'ffn_w_up': 1.703710e-02, 'ffn_conv_w': 1.729645e-02, 'ffn_conv_b': 1.506495e-02, 'ffn_w_down': 2.773936e-02, 'final_norm_w': 1.603038e+01}


def _to_microbatches(a, axis):
    t = _jnp.moveaxis(a, axis, 0)
    t = t.reshape((N_MICROBATCH, t.shape[0] // N_MICROBATCH) + t.shape[1:])
    return _jnp.moveaxis(t, 1, axis + 1)


def setup_inputs(seed: int = 0) -> dict:
    inp = _fwd_setup_inputs(seed)
    key = _jax.random.fold_in(_jax.random.key(seed), 7919)
    shape, _ = _output_shape()
    out = dict(inp)
    out["loss_target"] = _jax.random.normal(_jax.random.fold_in(key, 0), shape, _jnp.float32)
    for i, name in enumerate(TWIN_WEIGHTS):
        w = inp[name].astype(_jnp.float32)
        if MOMENT_SCALE is None:
            s = _jnp.sqrt(_jnp.mean(_jnp.square(w)) + 1e-30)
        else:
            s = MOMENT_SCALE[name]
        km, kv = _jax.random.split(_jax.random.fold_in(key, i + 1))
        out[name] = w
        out["m_" + name] = s * _jax.random.normal(km, w.shape, _jnp.float32)
        out["v_" + name] = (s * s) * _jax.random.uniform(kv, w.shape, _jnp.float32, 0.5, 1.5)
    if N_MICROBATCH > 1:
        for name, axis in PER_EXAMPLE_BATCH_AXIS.items():
            out[name] = _to_microbatches(out[name], axis)
    return {'x': out['x'], 'c': out['c'], 'ctx': out['ctx'], 'c_ctx': out['c_ctx'], 'mod_w': out['mod_w'], 'mod_b': out['mod_b'], 'norm1_w': out['norm1_w'], 'norm2_w': out['norm2_w'], 'ssd_w_in': out['ssd_w_in'], 'ssd_conv_w': out['ssd_conv_w'], 'ssd_conv_b': out['ssd_conv_b'], 'ssd_dt_bias': out['ssd_dt_bias'], 'ssd_a_log': out['ssd_a_log'], 'ssd_d': out['ssd_d'], 'ssd_norm_w': out['ssd_norm_w'], 'ssd_w_out': out['ssd_w_out'], 'conf_w_pw1': out['conf_w_pw1'], 'conf_b_pw1': out['conf_b_pw1'], 'conf_w_dw': out['conf_w_dw'], 'conf_b_dw': out['conf_b_dw'], 'conf_ln_w': out['conf_ln_w'], 'conf_ln_b': out['conf_ln_b'], 'conf_w_pw2': out['conf_w_pw2'], 'conf_b_pw2': out['conf_b_pw2'], 'ffn_w_up': out['ffn_w_up'], 'ffn_conv_w': out['ffn_conv_w'], 'ffn_conv_b': out['ffn_conv_b'], 'ffn_w_down': out['ffn_w_down'], 'final_norm_w': out['final_norm_w'], 'loss_target': out['loss_target'], 'm_c_ctx': out['m_c_ctx'], 'm_mod_w': out['m_mod_w'], 'm_mod_b': out['m_mod_b'], 'm_norm1_w': out['m_norm1_w'], 'm_norm2_w': out['m_norm2_w'], 'm_ssd_w_in': out['m_ssd_w_in'], 'm_ssd_conv_w': out['m_ssd_conv_w'], 'm_ssd_conv_b': out['m_ssd_conv_b'], 'm_ssd_dt_bias': out['m_ssd_dt_bias'], 'm_ssd_a_log': out['m_ssd_a_log'], 'm_ssd_d': out['m_ssd_d'], 'm_ssd_norm_w': out['m_ssd_norm_w'], 'm_ssd_w_out': out['m_ssd_w_out'], 'm_conf_w_pw1': out['m_conf_w_pw1'], 'm_conf_b_pw1': out['m_conf_b_pw1'], 'm_conf_w_dw': out['m_conf_w_dw'], 'm_conf_b_dw': out['m_conf_b_dw'], 'm_conf_ln_w': out['m_conf_ln_w'], 'm_conf_ln_b': out['m_conf_ln_b'], 'm_conf_w_pw2': out['m_conf_w_pw2'], 'm_conf_b_pw2': out['m_conf_b_pw2'], 'm_ffn_w_up': out['m_ffn_w_up'], 'm_ffn_conv_w': out['m_ffn_conv_w'], 'm_ffn_conv_b': out['m_ffn_conv_b'], 'm_ffn_w_down': out['m_ffn_w_down'], 'm_final_norm_w': out['m_final_norm_w'], 'v_c_ctx': out['v_c_ctx'], 'v_mod_w': out['v_mod_w'], 'v_mod_b': out['v_mod_b'], 'v_norm1_w': out['v_norm1_w'], 'v_norm2_w': out['v_norm2_w'], 'v_ssd_w_in': out['v_ssd_w_in'], 'v_ssd_conv_w': out['v_ssd_conv_w'], 'v_ssd_conv_b': out['v_ssd_conv_b'], 'v_ssd_dt_bias': out['v_ssd_dt_bias'], 'v_ssd_a_log': out['v_ssd_a_log'], 'v_ssd_d': out['v_ssd_d'], 'v_ssd_norm_w': out['v_ssd_norm_w'], 'v_ssd_w_out': out['v_ssd_w_out'], 'v_conf_w_pw1': out['v_conf_w_pw1'], 'v_conf_b_pw1': out['v_conf_b_pw1'], 'v_conf_w_dw': out['v_conf_w_dw'], 'v_conf_b_dw': out['v_conf_b_dw'], 'v_conf_ln_w': out['v_conf_ln_w'], 'v_conf_ln_b': out['v_conf_ln_b'], 'v_conf_w_pw2': out['v_conf_w_pw2'], 'v_conf_b_pw2': out['v_conf_b_pw2'], 'v_ffn_w_up': out['v_ffn_w_up'], 'v_ffn_conv_w': out['v_ffn_conv_w'], 'v_ffn_conv_b': out['v_ffn_conv_b'], 'v_ffn_w_down': out['v_ffn_w_down'], 'v_final_norm_w': out['v_final_norm_w']}


def _loss(weights, diff, rest, loss_target):
    with _jax.named_scope("forward"):
        args = {**rest, TWIN_DIFF_INPUT: diff, **{k: w.astype(_WEIGHT_DTYPES[k]) for k, w in weights.items()}}
        y = _forward(args)
    with _jax.named_scope("loss_head"):
        err = _jnp.square(y.astype(_jnp.float32) - loss_target)
        return 0.5 * _jnp.sum(_jnp.mean(err, axis=-1)) if err.ndim else 0.5 * err


def _adamw(w, g, m, v):
    m = ADAM_B1 * m + (1.0 - ADAM_B1) * g
    v = ADAM_B2 * v + (1.0 - ADAM_B2) * _jnp.square(g)
    m_hat = m / (1.0 - ADAM_B1 ** ADAM_STEP)
    v_hat = v / (1.0 - ADAM_B2 ** ADAM_STEP)
    delta = -ADAM_LR * (m_hat / (_jnp.sqrt(v_hat) + ADAM_EPS) + ADAM_WD * w)
    return delta, m, v


def reference(x, c, ctx, c_ctx, mod_w, mod_b, norm1_w, norm2_w, ssd_w_in, ssd_conv_w, ssd_conv_b, ssd_dt_bias, ssd_a_log, ssd_d, ssd_norm_w, ssd_w_out, conf_w_pw1, conf_b_pw1, conf_w_dw, conf_b_dw, conf_ln_w, conf_ln_b, conf_w_pw2, conf_b_pw2, ffn_w_up, ffn_conv_w, ffn_conv_b, ffn_w_down, final_norm_w, loss_target, m_c_ctx, m_mod_w, m_mod_b, m_norm1_w, m_norm2_w, m_ssd_w_in, m_ssd_conv_w, m_ssd_conv_b, m_ssd_dt_bias, m_ssd_a_log, m_ssd_d, m_ssd_norm_w, m_ssd_w_out, m_conf_w_pw1, m_conf_b_pw1, m_conf_w_dw, m_conf_b_dw, m_conf_ln_w, m_conf_ln_b, m_conf_w_pw2, m_conf_b_pw2, m_ffn_w_up, m_ffn_conv_w, m_ffn_conv_b, m_ffn_w_down, m_final_norm_w, v_c_ctx, v_mod_w, v_mod_b, v_norm1_w, v_norm2_w, v_ssd_w_in, v_ssd_conv_w, v_ssd_conv_b, v_ssd_dt_bias, v_ssd_a_log, v_ssd_d, v_ssd_norm_w, v_ssd_w_out, v_conf_w_pw1, v_conf_b_pw1, v_conf_w_dw, v_conf_b_dw, v_conf_ln_w, v_conf_ln_b, v_conf_w_pw2, v_conf_b_pw2, v_ffn_w_up, v_ffn_conv_w, v_ffn_conv_b, v_ffn_w_down, v_final_norm_w):
    given = dict(x=x, c=c, ctx=ctx, c_ctx=c_ctx, mod_w=mod_w, mod_b=mod_b, norm1_w=norm1_w, norm2_w=norm2_w, ssd_w_in=ssd_w_in, ssd_conv_w=ssd_conv_w, ssd_conv_b=ssd_conv_b, ssd_dt_bias=ssd_dt_bias, ssd_a_log=ssd_a_log, ssd_d=ssd_d, ssd_norm_w=ssd_norm_w, ssd_w_out=ssd_w_out, conf_w_pw1=conf_w_pw1, conf_b_pw1=conf_b_pw1, conf_w_dw=conf_w_dw, conf_b_dw=conf_b_dw, conf_ln_w=conf_ln_w, conf_ln_b=conf_ln_b, conf_w_pw2=conf_w_pw2, conf_b_pw2=conf_b_pw2, ffn_w_up=ffn_w_up, ffn_conv_w=ffn_conv_w, ffn_conv_b=ffn_conv_b, ffn_w_down=ffn_w_down, final_norm_w=final_norm_w, loss_target=loss_target, m_c_ctx=m_c_ctx, m_mod_w=m_mod_w, m_mod_b=m_mod_b, m_norm1_w=m_norm1_w, m_norm2_w=m_norm2_w, m_ssd_w_in=m_ssd_w_in, m_ssd_conv_w=m_ssd_conv_w, m_ssd_conv_b=m_ssd_conv_b, m_ssd_dt_bias=m_ssd_dt_bias, m_ssd_a_log=m_ssd_a_log, m_ssd_d=m_ssd_d, m_ssd_norm_w=m_ssd_norm_w, m_ssd_w_out=m_ssd_w_out, m_conf_w_pw1=m_conf_w_pw1, m_conf_b_pw1=m_conf_b_pw1, m_conf_w_dw=m_conf_w_dw, m_conf_b_dw=m_conf_b_dw, m_conf_ln_w=m_conf_ln_w, m_conf_ln_b=m_conf_ln_b, m_conf_w_pw2=m_conf_w_pw2, m_conf_b_pw2=m_conf_b_pw2, m_ffn_w_up=m_ffn_w_up, m_ffn_conv_w=m_ffn_conv_w, m_ffn_conv_b=m_ffn_conv_b, m_ffn_w_down=m_ffn_w_down, m_final_norm_w=m_final_norm_w, v_c_ctx=v_c_ctx, v_mod_w=v_mod_w, v_mod_b=v_mod_b, v_norm1_w=v_norm1_w, v_norm2_w=v_norm2_w, v_ssd_w_in=v_ssd_w_in, v_ssd_conv_w=v_ssd_conv_w, v_ssd_conv_b=v_ssd_conv_b, v_ssd_dt_bias=v_ssd_dt_bias, v_ssd_a_log=v_ssd_a_log, v_ssd_d=v_ssd_d, v_ssd_norm_w=v_ssd_norm_w, v_ssd_w_out=v_ssd_w_out, v_conf_w_pw1=v_conf_w_pw1, v_conf_b_pw1=v_conf_b_pw1, v_conf_w_dw=v_conf_w_dw, v_conf_b_dw=v_conf_b_dw, v_conf_ln_w=v_conf_ln_w, v_conf_ln_b=v_conf_ln_b, v_conf_w_pw2=v_conf_w_pw2, v_conf_b_pw2=v_conf_b_pw2, v_ffn_w_up=v_ffn_w_up, v_ffn_conv_w=v_ffn_conv_w, v_ffn_conv_b=v_ffn_conv_b, v_ffn_w_down=v_ffn_w_down, v_final_norm_w=v_final_norm_w)
    weights = {n: given[n] for n in TWIN_WEIGHTS}
    shared = {n: given[n] for n in SHARED_INPUTS}
    per_example = {n: given[n] for n in ['x', 'c', 'ctx']}
    grad_fn = _jax.value_and_grad(_loss, argnums=(0, 1))

    def one_microbatch(ex, loss_target):
        ex = dict(ex)
        diff = ex.pop(TWIN_DIFF_INPUT)
        return grad_fn(weights, diff, {**shared, **ex}, loss_target)

    if N_MICROBATCH == 1:
        loss, (grad_w, grad_x) = one_microbatch(per_example, given["loss_target"])
    else:
        def body(carry, xs):
            loss_sum, grad_sum = carry
            l_k, (gw_k, gx_k) = one_microbatch(xs[0], xs[1])
            with _jax.named_scope("update"):
                return (loss_sum + l_k, _jax.tree.map(_jnp.add, grad_sum, gw_k)), gx_k

        init = (_jnp.zeros((), _jnp.float32), _jax.tree.map(_jnp.zeros_like, weights))
        (loss, grad_w), grad_x = _jax.lax.scan(body, init, (per_example, given["loss_target"]))
    with _jax.named_scope("update"):
        delta_w, new_m, new_v = {}, {}, {}
        for n in TWIN_WEIGHTS:
            delta_w[n], new_m[n], new_v[n] = _adamw(weights[n], grad_w[n], given["m_" + n], given["v_" + n])
    return (loss, grad_x, *[grad_w[n] for n in TWIN_WEIGHTS], *[delta_w[n] for n in TWIN_WEIGHTS],
            *[new_m[n] for n in TWIN_WEIGHTS], *[new_v[n] for n in TWIN_WEIGHTS])
```

```python
import functools

import jax
import jax.numpy as jnp
from jax import lax
from jax.experimental import pallas as pl
from jax.experimental.pallas import tpu as pltpu

F32 = jnp.float32
BF16 = jnp.bfloat16
MESH = pl.DeviceIdType.MESH
HIGHEST = lax.Precision.HIGHEST

VMEM_LIMIT_BYTES = 48 * 1024 * 1024
LANE = 128
SUBLANE = 8

SSD_STATE = 128
SSD_CHUNK = 128
GRID_W = 64
EPS = 1e-6
N_CHIPS = 4
N_DEV = 8

ADAM_LR = 0.001
ADAM_B1 = 0.9
ADAM_B2 = 0.999
ADAM_EPS = 1e-08
ADAM_WD = 0.01
ADAM_STEP = 10


def _pcall(body, **kw):
    return pl.pallas_call(body, **kw)


def _cparams(n_grid):
    return pltpu.CompilerParams(dimension_semantics=("arbitrary",) * n_grid, vmem_limit_bytes=VMEM_LIMIT_BYTES)


def _cdiv(a, b):
    return -(-a // b)


def _round_up(a, b):
    return _cdiv(a, b) * b


def _tile(n, cap):
    if n <= cap:
        return n
    best = None
    for t in range(LANE, cap + 1, LANE):
        if n % t == 0:
            best = t
    if best is None:
        npad = _round_up(n, LANE)
        for t in range(LANE, cap + 1, LANE):
            if npad % t == 0:
                best = t
    return best


def _row_tile(n, cap, also=()):
    best = None
    for t in range(SUBLANE, min(cap, n) + 1, SUBLANE):
        if n % t == 0 and all(a % t == 0 for a in also):
            best = t
    assert best is not None, (n, cap, also)
    return best


def _silu(v):
    return v * jax.nn.sigmoid(v)


def _mm(a, b, *, name, ta=False, tb=False, precision=None, cap=1024):
    M, K = (a.shape[1], a.shape[0]) if ta else a.shape
    N = b.shape[0] if tb else b.shape[1]
    assert K == (b.shape[1] if tb else b.shape[0]), (a.shape, b.shape, ta, tb)
    tm, tn, tk = _tile(M, cap), _tile(N, cap), _tile(K, cap)
    nm, nn, nk = _cdiv(M, tm), _cdiv(N, tn), _cdiv(K, tk)
    k_tail = K % tk
    exact = precision is not None

    def body(a_ref, b_ref, o_ref, acc_ref):
        k = pl.program_id(2)

        @pl.when(k == 0)
        def _():
            acc_ref[...] = jnp.zeros_like(acc_ref)

        av = a_ref[...]
        bv = b_ref[...]
        if k_tail:
            lim = K - k * tk
            ka = lax.broadcasted_iota(jnp.int32, av.shape, 0 if ta else 1)
            kb = lax.broadcasted_iota(jnp.int32, bv.shape, 1 if tb else 0)
            av = jnp.where(ka < lim, av, jnp.zeros_like(av))
            bv = jnp.where(kb < lim, bv, jnp.zeros_like(bv))
        if exact:
            av = av.astype(F32)
            bv = bv.astype(F32)
        else:
            av = av.astype(BF16)
            bv = bv.astype(BF16)
        dn = (((0 if ta else 1,), (1 if tb else 0,)), ((), ()))
        acc_ref[...] += lax.dot_general(av, bv, dn, preferred_element_type=F32, precision=precision)

        @pl.when(k == nk - 1)
        def _():
            o_ref[...] = acc_ref[...]

    a_spec = pl.BlockSpec((tk, tm), lambda i, j, k: (k, i)) if ta else pl.BlockSpec((tm, tk), lambda i, j, k: (i, k))
    b_spec = pl.BlockSpec((tn, tk), lambda i, j, k: (j, k)) if tb else pl.BlockSpec((tk, tn), lambda i, j, k: (k, j))
    return _pcall(
        body, name=name, grid=(nm, nn, nk), in_specs=[a_spec, b_spec],
        out_specs=pl.BlockSpec((tm, tn), lambda i, j, k: (i, j)),
        out_shape=jax.ShapeDtypeStruct((M, N), F32),
        scratch_shapes=[pltpu.VMEM((tm, tn), F32)], compiler_params=_cparams(3),
    )(a, b)


def _rw_plan(rows, pars, seg_rows, col_tile, tm_cap):
    T = rows[0][0].shape[0]
    widths = [(r[2] if r[2] is not None else r[0].shape[1]) for r in rows]
    wmax = max(widths + [p.shape[-1] for p in pars] + [1])
    if col_tile is not None:
        assert all(w == widths[0] for w in widths) and all(p.shape[-1] == widths[0] for p in pars)
        ncol = widths[0] // col_tile
        assert ncol * col_tile == widths[0]
        wmax = col_tile
    else:
        ncol = 1
    cap = tm_cap if tm_cap is not None else max(SUBLANE, min(256, (256 * 1024) // wmax))
    tm = _row_tile(T, cap, also=seg_rows)
    bounds = tuple(s // tm for s in seg_rows)
    return T, widths, ncol, tm, bounds


def _norm_rows(rows):
    out = []
    for r in rows:
        if isinstance(r, tuple):
            arr, off, width = (r + (0, None))[:3] if len(r) < 3 else r
            out.append((arr, off, width))
        else:
            out.append((r, 0, None))
    return out


def _rw_specs(rows, pars, widths, ncol, tm, bounds, col_tile):
    def seg(i):
        s = 0
        for b in bounds:
            s = s + (i >= b).astype(jnp.int32)
        return s

    specs = []
    for (arr, off, width), w in zip(rows, widths):
        bw = col_tile if col_tile is not None else w
        assert off % bw == 0, (off, bw)
        ob = off // bw
        specs.append(pl.BlockSpec((tm, bw), functools.partial(lambda j, i, ob: (i, ob + j), ob=ob)))
    for p in pars:
        bw = col_tile if col_tile is not None else p.shape[-1]
        if p.shape[0] > 1:
            specs.append(pl.BlockSpec((None, 1, bw), lambda j, i: (seg(i), 0, j)))
        else:
            specs.append(pl.BlockSpec((None, 1, bw), lambda j, i: (0, 0, j)))
    return specs, seg


def _rw_fwd(name, f, rows, pars, out_widths, *, seg_rows=(), col_tile=None, tm_cap=None):
    rows = _norm_rows(rows)
    T, widths, ncol, tm, bounds = _rw_plan(rows, pars, seg_rows, col_tile, tm_cap)
    in_specs, _ = _rw_specs(rows, pars, widths, ncol, tm, bounds, col_tile)
    nr, npar, nout = len(rows), len(pars), len(out_widths)

    def body(*refs):
        vals = [r[...] for r in refs[:nr + npar]]
        outs = f(*vals)
        if not isinstance(outs, (tuple, list)):
            outs = (outs,)
        for o_ref, o in zip(refs[nr + npar:], outs):
            o_ref[...] = o.astype(o_ref.dtype)

    out_specs = [pl.BlockSpec((tm, col_tile if col_tile is not None else w), lambda j, i: (i, j)) for w in out_widths]
    res = _pcall(
        body, name=name, grid=(ncol, T // tm), in_specs=in_specs, out_specs=out_specs,
        out_shape=[jax.ShapeDtypeStruct((T, w), F32) for w in out_widths], compiler_params=_cparams(2),
    )(*[r[0] for r in rows], *pars)
    return res if nout > 1 else res[0]


def _rw_bwd(name, f, rows, pars, cots, *, row_grad, par_grad, seg_rows=(), col_tile=None, tm_cap=None, add=None):
    rows = _norm_rows(rows)
    cots = _norm_rows(cots)
    T, widths, ncol, tm, bounds = _rw_plan(rows + cots, pars, seg_rows, col_tile, tm_cap)
    extra = ([(add, 0, None)] if add is not None else [])
    all_rows = rows + cots + extra
    all_widths = widths + [e[0].shape[1] for e in extra]
    in_specs, seg = _rw_specs(all_rows, pars, all_widths, ncol, tm, bounds, col_tile)
    nr, nc, ne, npar = len(rows), len(cots), len(extra), len(pars)
    row_idx = [k for k in range(nr) if row_grad[k]]
    par_idx = [k for k in range(npar) if par_grad[k]]

    def body(*refs):
        i = pl.program_id(1)
        row_vals = [r[...] for r in refs[:nr]]
        cot_vals = [r[...] for r in refs[nr:nr + nc]]
        add_vals = [r[...] for r in refs[nr + nc:nr + nc + ne]]
        par_vals = [r[...] for r in refs[nr + nc + ne:nr + nc + ne + npar]]
        out_refs = refs[nr + nc + ne + npar:]
        outs, vjp = jax.vjp(f, *row_vals, *par_vals)
        if isinstance(outs, (tuple, list)):
            grads = vjp(tuple(c.astype(o.dtype) for c, o in zip(cot_vals, outs)))
        else:
            grads = vjp(cot_vals[0].astype(outs.dtype))
        first_seg = i == 0
        for b in bounds:
            first_seg = first_seg | (i == b)
        for n, k in enumerate(row_idx):
            g = grads[k]
            if n == 0 and add_vals:
                g = g + add_vals[0]
            out_refs[n][...] = g
        for n, k in enumerate(par_idx):
            g = grads[nr + k]
            o_ref = out_refs[len(row_idx) + n]
            first = first_seg if pars[k].shape[0] > 1 else (i == 0)

            @pl.when(first)
            def _(o_ref=o_ref, g=g):
                o_ref[...] = g

            @pl.when(jnp.logical_not(first))
            def _(o_ref=o_ref, g=g):
                o_ref[...] += g

    out_specs, out_shape = [], []
    for k in row_idx:
        w = widths[k]
        out_specs.append(pl.BlockSpec((tm, col_tile if col_tile is not None else w), lambda j, i: (i, j)))
        out_shape.append(jax.ShapeDtypeStruct((T, w), F32))
    for k in par_idx:
        p = pars[k]
        bw = col_tile if col_tile is not None else p.shape[-1]
        if p.shape[0] > 1:
            out_specs.append(pl.BlockSpec((None, 1, bw), lambda j, i: (seg(i), 0, j)))
        else:
            out_specs.append(pl.BlockSpec((None, 1, bw), lambda j, i: (0, 0, j)))
        out_shape.append(jax.ShapeDtypeStruct(p.shape, F32))
    res = _pcall(
        body, name=name, grid=(ncol, T // tm), in_specs=in_specs, out_specs=out_specs, out_shape=out_shape,
        compiler_params=_cparams(2),
    )(*[r[0] for r in all_rows], *pars)
    return list(res[:len(row_idx)]), list(res[len(row_idx):])


def _f_modnorm(h, w, sc, sh):
    y = h * lax.rsqrt(jnp.mean(h * h, axis=-1, keepdims=True) + EPS)
    return (y * w) * (1.0 + sc) + sh


def _f_gate_res(h, y, g):
    return h + g * y


def _f_gate_res_bias(h, y, g, b):
    return h + g * (y + b)


def _f_ffn_act(val, gate):
    return _silu(gate) * val


def _f_softplus(raw, bias):
    v = raw + bias
    return jnp.maximum(v, 0.0) + jnp.log(1.0 + jnp.exp(-jnp.abs(v)))


def _f_ssd_gate(yf, yb, xs, z, d_rep, nw):
    y = (yf + yb + d_rep * xs) * _silu(z)
    return (y * lax.rsqrt(jnp.mean(y * y, axis=-1, keepdims=True) + EPS)) * nw


def _f_glu(a, g, ba, bg):
    return (a + ba) * jax.nn.sigmoid(g + bg)


def _f_ln_silu(h, w, b):
    mu = jnp.mean(h, axis=-1, keepdims=True)
    d = h - mu
    y = d * lax.rsqrt(jnp.mean(d * d, axis=-1, keepdims=True) + EPS)
    return _silu(y * w + b)


def _f_loss_rows(h, t, w):
    y = (h * lax.rsqrt(jnp.mean(h * h, axis=-1, keepdims=True) + EPS)) * w
    e = y - t
    return 0.5 * jnp.mean(e * e, axis=-1, keepdims=True)


def _f_adamw(w, m, v, ga, gb):
    g = ga + gb
    m = ADAM_B1 * m + (1.0 - ADAM_B1) * g
    v = ADAM_B2 * v + (1.0 - ADAM_B2) * (g * g)
    m_hat = m / (1.0 - ADAM_B1 ** ADAM_STEP)
    v_hat = v / (1.0 - ADAM_B2 ** ADAM_STEP)
    delta = -ADAM_LR * (m_hat / (jnp.sqrt(v_hat) + ADAM_EPS) + ADAM_WD * w)
    return g, delta, m, v


def _adamw(name, w, m, v, ga, gb):
    shape = w.shape
    c = shape[-1]
    two_d = [t.reshape(-1, c) for t in (w, m, v, ga, gb)]
    rows = two_d[0].shape[0]
    pad = _round_up(rows, SUBLANE) - rows
    if pad:
        two_d = [jnp.pad(t, ((0, pad), (0, 0))) for t in two_d]
    outs = _rw_fwd(name, _f_adamw, two_d, [], [c] * 4)
    return tuple(o[:rows].reshape(shape) for o in outs)


def _sum_leading(name, x, idxs):
    _, R, C = x.shape
    tm = _row_tile(R, max(SUBLANE, min(512, (512 * 1024) // C)))

    def body(x_ref, o_ref):
        acc = x_ref[idxs[0]].astype(F32)
        for k in idxs[1:]:
            acc = acc + x_ref[k].astype(F32)
        o_ref[...] = acc

    return _pcall(
        body, name=name, grid=(R // tm,), in_specs=[pl.BlockSpec((x.shape[0], tm, C), lambda i: (0, i, 0))],
        out_specs=pl.BlockSpec((tm, C), lambda i: (i, 0)), out_shape=jax.ShapeDtypeStruct((R, C), F32),
        compiler_params=_cparams(1),
    )(x)


def _loss_fwd(h, t, w):
    T, D = h.shape
    tm = _row_tile(T, 256)

    def body(h_ref, t_ref, w_ref, o_ref):
        i = pl.program_id(0)
        part = jnp.sum(_f_loss_rows(h_ref[...], t_ref[...], w_ref[...]), axis=0, keepdims=True)
        part = jnp.broadcast_to(part, (1, LANE))

        @pl.when(i == 0)
        def _():
            o_ref[...] = part

        @pl.when(i > 0)
        def _():
            o_ref[...] += part

    return _pcall(
        body, name="loss_fwd", grid=(T // tm,),
        in_specs=[pl.BlockSpec((tm, D), lambda i: (i, 0)), pl.BlockSpec((tm, D), lambda i: (i, 0)),
                  pl.BlockSpec((1, D), lambda i: (0, 0))],
        out_specs=pl.BlockSpec((1, LANE), lambda i: (0, 0)), out_shape=jax.ShapeDtypeStruct((1, LANE), F32),
        compiler_params=_cparams(1),
    )(h, t, w)


CONV_ROWS = 256
CONV_SUB = 64


def _tap_mask(mask, t, s):
    if mask is None:
        return None
    kind, arg = mask
    if kind == "seg":
        if s == 0:
            return None
        return (t >= arg) == ((t + s) >= arg)
    col = jnp.bitwise_and(t, GRID_W - 1)
    return (col != 0) if arg < 0 else (col != GRID_W - 1)


def _conv_plan(T, C):
    rc = CONV_ROWS if T % CONV_ROWS == 0 else LANE
    assert T % rc == 0
    ct = LANE if C % LANE == 0 else C
    return rc, ct, T // rc, C // ct


def _halo_specs(rc, ct, nrc, off_blocks):
    prev = pl.BlockSpec((rc, ct), lambda j, i: (jnp.maximum(i - 1, 0), off_blocks + j))
    cur = pl.BlockSpec((rc, ct), lambda j, i: (i, off_blocks + j))
    nxt = pl.BlockSpec((rc, ct), lambda j, i: (jnp.minimum(i + 1, nrc - 1), off_blocks + j))
    return [prev, cur, nxt]


def _fill_halo(pad_ref, p_ref, c_ref, n_ref, i, nrc, rc):
    pad_ref[0:rc, :] = jnp.where(i > 0, p_ref[...], 0.0)
    pad_ref[rc:2 * rc, :] = c_ref[...]
    pad_ref[2 * rc:3 * rc, :] = jnp.where(i < nrc - 1, n_ref[...], 0.0)


def _conv_fwd(name, u, col_off, C, w, b, taps, act=False):
    T = u.shape[0]
    rc, ct, nrc, ncc = _conv_plan(T, C)
    assert col_off % ct == 0 and max(abs(s) for s, _ in taps) <= rc
    K = len(taps)
    sub = min(CONV_SUB, rc)

    def body(up, uc, un, w_ref, b_ref, *rest):
        y_ref = rest[0]
        pad_ref = rest[-1]
        i = pl.program_id(1)
        _fill_halo(pad_ref, up, uc, un, i, nrc, rc)
        for r0 in range(0, rc, sub):
            t = i * rc + r0 + lax.broadcasted_iota(jnp.int32, (sub, 1), 0)
            acc = jnp.broadcast_to(b_ref[...], (sub, ct))
            for k, (s, mask) in enumerate(taps):
                v = pad_ref[rc + r0 + s:rc + r0 + s + sub, :]
                m = _tap_mask(mask, t, s)
                if m is not None:
                    v = jnp.where(m, v, 0.0)
                acc = acc + w_ref[k:k + 1, :] * v
            y_ref[r0:r0 + sub, :] = acc
            if act:
                rest[1][r0:r0 + sub, :] = _silu(acc)

    n_out = 2 if act else 1
    res = _pcall(
        body, name=name, grid=(ncc, nrc),
        in_specs=_halo_specs(rc, ct, nrc, col_off // ct) + [pl.BlockSpec((K, ct), lambda j, i: (0, j)),
                                                             pl.BlockSpec((1, ct), lambda j, i: (0, j))],
        out_specs=[pl.BlockSpec((rc, ct), lambda j, i: (i, j))] * n_out,
        out_shape=[jax.ShapeDtypeStruct((T, C), F32)] * n_out,
        scratch_shapes=[pltpu.VMEM((3 * rc, ct), F32)], compiler_params=_cparams(2),
    )(u, u, u, w, b)
    return res if act else res[0]


def _conv_bwd(name, u, col_off, C, w, g, taps):
    T = u.shape[0]
    rc, ct, nrc, ncc = _conv_plan(T, C)
    K = len(taps)
    sub = min(CONV_SUB, rc)

    def body(up, uc, un, gp, gc, gn, w_ref, du_ref, dw_ref, db_ref, upad, gpad):
        i = pl.program_id(1)
        _fill_halo(upad, up, uc, un, i, nrc, rc)
        _fill_halo(gpad, gp, gc, gn, i, nrc, rc)

        @pl.when(i == 0)
        def _():
            dw_ref[...] = jnp.zeros_like(dw_ref)
            db_ref[...] = jnp.zeros_like(db_ref)

        dws = [jnp.zeros((1, ct), F32) for _ in range(K)]
        dbs = jnp.zeros((1, ct), F32)
        for r0 in range(0, rc, sub):
            t = i * rc + r0 + lax.broadcasted_iota(jnp.int32, (sub, 1), 0)
            gv = gpad[rc + r0:rc + r0 + sub, :]
            dbs = dbs + jnp.sum(gv, axis=0, keepdims=True)
            acc = jnp.zeros((sub, ct), F32)
            for k, (s, mask) in enumerate(taps):
                gs = gpad[rc + r0 - s:rc + r0 - s + sub, :]
                m = _tap_mask(mask, t - s, s)
                if m is not None:
                    gs = jnp.where(m, gs, 0.0)
                acc = acc + w_ref[k:k + 1, :] * gs
                uv = upad[rc + r0 + s:rc + r0 + s + sub, :]
                m = _tap_mask(mask, t, s)
                prod = gv * uv
                if m is not None:
                    prod = jnp.where(m, prod, 0.0)
                dws[k] = dws[k] + jnp.sum(prod, axis=0, keepdims=True)
            du_ref[r0:r0 + sub, :] = acc
        for k in range(K):
            dw_ref[k:k + 1, :] += dws[k]
        db_ref[...] += dbs

    halo_u = _halo_specs(rc, ct, nrc, col_off // ct)
    halo_g = _halo_specs(rc, ct, nrc, 0)
    return _pcall(
        body, name=name, grid=(ncc, nrc),
        in_specs=halo_u + halo_g + [pl.BlockSpec((K, ct), lambda j, i: (0, j))],
        out_specs=[pl.BlockSpec((rc, ct), lambda j, i: (i, j)), pl.BlockSpec((K, ct), lambda j, i: (0, j)),
                   pl.BlockSpec((1, ct), lambda j, i: (0, j))],
        out_shape=[jax.ShapeDtypeStruct((T, C), F32), jax.ShapeDtypeStruct((K, C), F32),
                   jax.ShapeDtypeStruct((1, C), F32)],
        scratch_shapes=[pltpu.VMEM((3 * rc, ct), F32), pltpu.VMEM((3 * rc, ct), F32)], compiler_params=_cparams(2),
    )(u, u, u, g, g, g, w)


def _ssd_chunk(x, dtc, dtr, bm, cm, a, s_in, sgn):
    q = x.shape[0]
    ii = lax.broadcasted_iota(jnp.int32, (q, q), 0)
    jj = lax.broadcasted_iota(jnp.int32, (q, q), 1)
    causal = ((jj - ii) * sgn) <= 0
    causal_t = ((ii - jj) * sgn) <= 0
    dac = dtc * a
    dar = dtr * a
    acum_c = jnp.sum(jnp.where(causal, dar, 0.0), axis=1, keepdims=True)
    acum_r = jnp.sum(jnp.where(causal_t, dac, 0.0), axis=0, keepdims=True)
    seg = acum_c - acum_r
    decay = jnp.where(causal, jnp.exp(jnp.where(causal, seg, 0.0)), 0.0)
    nt = (((1,), (1,)), ((), ()))
    tn = (((0,), (0,)), ((), ()))
    cb = lax.dot_general(cm.astype(BF16), bm.astype(BF16), nt, preferred_element_type=F32)
    xdt = x * dtc
    y = jnp.dot((cb * decay).astype(BF16), xdt.astype(BF16), preferred_element_type=F32)
    tot = jnp.sum(dac, axis=0, keepdims=True)
    dte = jnp.exp(tot - acum_c)
    cs = lax.dot_general((xdt * dte).astype(BF16), bm.astype(BF16), tn, preferred_element_type=F32)
    y = y + lax.dot_general(cm.astype(BF16), s_in.astype(BF16), nt, preferred_element_type=F32) * jnp.exp(acum_c)
    s_out = jnp.exp(tot) * s_in + cs
    return y, s_out


def _ssd_dims(xh, dtc, xbc, b_off):
    H, T, P = xh.shape
    N, Q = SSD_STATE, SSD_CHUNK
    return H, T, P, N, Q, T // Q


def _ssd_index_maps(H, R, NC, ncc, reverse_steps):
    def chunk(d, s):
        if reverse_steps:
            s = NC - 1 - s
        return jnp.where(d == 0, s, jnp.where(s < ncc, ncc - 1 - s, NC - 1 - s + ncc))

    def step(s):
        return NC - 1 - s if reverse_steps else s

    return chunk, step


def _ssd_fwd(xh, dtc, dtr, xbc, b_off, c_off, a, G, ncc):
    H, T, P, N, Q, NC = _ssd_dims(xh, dtc, xbc, b_off)
    R = H // G
    chunk, step = _ssd_index_maps(H, R, NC, ncc, False)
    bo, co = b_off // N, c_off // N

    def body(x_ref, dtc_ref, dtr_ref, b_ref, c_ref, a_ref, y_ref, se_ref, s_ref):
        d, s, r = pl.program_id(0), pl.program_id(2), pl.program_id(3)

        @pl.when(s == 0)
        def _():
            s_ref[r] = jnp.zeros((P, N), F32)

        s_in = s_ref[r]
        se_ref[...] = s_in
        y, s_out = _ssd_chunk(x_ref[...], dtc_ref[...], dtr_ref[...], b_ref[...], c_ref[...], a_ref[...], s_in,
                              1 - 2 * d)
        y_ref[...] = y
        s_ref[r] = s_out

    return _pcall(
        body, name="ssd_fwd", grid=(2, G, NC, R),
        in_specs=[
            pl.BlockSpec((None, Q, P), lambda d, g, s, r: (g * R + r, chunk(d, s), 0)),
            pl.BlockSpec((None, Q, 1), lambda d, g, s, r: (d * H + g * R + r, chunk(d, s), 0)),
            pl.BlockSpec((None, 1, Q), lambda d, g, s, r: (d * H + g * R + r, 0, chunk(d, s))),
            pl.BlockSpec((Q, N), lambda d, g, s, r: (chunk(d, s), bo + g)),
            pl.BlockSpec((Q, N), lambda d, g, s, r: (chunk(d, s), co + g)),
            pl.BlockSpec((None, 1, 1), lambda d, g, s, r: (d * H + g * R + r, 0, 0)),
        ],
        out_specs=[
            pl.BlockSpec((None, None, Q, P), lambda d, g, s, r: (d, g * R + r, chunk(d, s), 0)),
            pl.BlockSpec((None, None, None, P, N), lambda d, g, s, r: (d, g * R + r, s, 0, 0)),
        ],
        out_shape=[jax.ShapeDtypeStruct((2, H, T, P), F32), jax.ShapeDtypeStruct((2, H, NC, P, N), F32)],
        scratch_shapes=[pltpu.VMEM((R, P, N), F32)], compiler_params=_cparams(4),
    )(xh, dtc, dtr, xbc, xbc, a)


def _ssd_bwd(xh, dtc, dtr, xbc, b_off, c_off, a, s_enter, dyh, G, ncc):
    H, T, P, N, Q, NC = _ssd_dims(xh, dtc, xbc, b_off)
    R = H // G
    chunk, step = _ssd_index_maps(H, R, NC, ncc, True)
    bo, co = b_off // N, c_off // N

    def body(x_ref, dtc_ref, dtr_ref, b_ref, c_ref, a_ref, se_ref, dy_ref,
             dx_ref, ddtc_ref, ddtr_ref, db_ref, dc_ref, da_ref, ds_ref):
        d, s, r = pl.program_id(0), pl.program_id(2), pl.program_id(3)

        @pl.when(s == 0)
        def _():
            ds_ref[r] = jnp.zeros((P, N), F32)

        sgn = 1 - 2 * d
        f = functools.partial(_ssd_chunk, sgn=sgn)
        _, vjp = jax.vjp(f, x_ref[...], dtc_ref[...], dtr_ref[...], b_ref[...], c_ref[...], a_ref[...], se_ref[...])
        dx, ddtc, ddtr, db, dc, da, ds = vjp((dy_ref[...], ds_ref[r]))
        dx_ref[...] = dx
        ddtc_ref[...] = ddtc
        ddtr_ref[...] = ddtr
        da_ref[...] = jnp.broadcast_to(da, (SUBLANE, LANE))
        ds_ref[r] = ds

        @pl.when(r == 0)
        def _():
            db_ref[...] = db
            dc_ref[...] = dc

        @pl.when(r > 0)
        def _():
            db_ref[...] += db
            dc_ref[...] += dc

    return _pcall(
        body, name="ssd_bwd", grid=(2, G, NC, R),
        in_specs=[
            pl.BlockSpec((None, Q, P), lambda d, g, s, r: (g * R + r, chunk(d, s), 0)),
            pl.BlockSpec((None, Q, 1), lambda d, g, s, r: (d * H + g * R + r, chunk(d, s), 0)),
            pl.BlockSpec((None, 1, Q), lambda d, g, s, r: (d * H + g * R + r, 0, chunk(d, s))),
            pl.BlockSpec((Q, N), lambda d, g, s, r: (chunk(d, s), bo + g)),
            pl.BlockSpec((Q, N), lambda d, g, s, r: (chunk(d, s), co + g)),
            pl.BlockSpec((None, 1, 1), lambda d, g, s, r: (d * H + g * R + r, 0, 0)),
            pl.BlockSpec((None, None, None, P, N), lambda d, g, s, r: (d, g * R + r, step(s), 0, 0)),
            pl.BlockSpec((None, Q, P), lambda d, g, s, r: (g * R + r, chunk(d, s), 0)),
        ],
        out_specs=[
            pl.BlockSpec((None, None, Q, P), lambda d, g, s, r: (d, g * R + r, chunk(d, s), 0)),
            pl.BlockSpec((None, Q, 1), lambda d, g, s, r: (d * H + g * R + r, chunk(d, s), 0)),
            pl.BlockSpec((None, 1, Q), lambda d, g, s, r: (d * H + g * R + r, 0, chunk(d, s))),
            pl.BlockSpec((None, Q, N), lambda d, g, s, r: (d, chunk(d, s), g)),
            pl.BlockSpec((None, Q, N), lambda d, g, s, r: (d, chunk(d, s), g)),
            pl.BlockSpec((None, SUBLANE, LANE), lambda d, g, s, r: (((d * G + g) * NC + s) * R + r, 0, 0)),
        ],
        out_shape=[
            jax.ShapeDtypeStruct((2, H, T, P), F32), jax.ShapeDtypeStruct((2 * H, T, 1), F32),
            jax.ShapeDtypeStruct((2 * H, 1, T), F32), jax.ShapeDtypeStruct((2, T, G * N), F32),
            jax.ShapeDtypeStruct((2, T, G * N), F32), jax.ShapeDtypeStruct((2 * G * NC * R, SUBLANE, LANE), F32),
        ],
        scratch_shapes=[pltpu.VMEM((R, P, N), F32)], compiler_params=_cparams(4),
    )(xh, dtc, dtr, xbc, xbc, a, s_enter, dyh)


def _allgather8(name, v):
    R, C = v.shape

    def body(x_ref, out_ref, send_sems, recv_sems, local_sem):
        x, y, c = lax.axis_index("x"), lax.axis_index("y"), lax.axis_index("c")
        me, sibling = (x, y, c), (x, y, 1 - c)
        chips = [(1 - x, y), (x, 1 - y), (1 - x, 1 - y)]

        def slot(px, py, pc):
            return out_ref.at[4 * px + 2 * py + pc]

        def copy(k, block, to, src=None):
            return pltpu.make_async_remote_copy(
                src_ref=slot(*block) if src is None else src, dst_ref=slot(*block),
                send_sem=send_sems.at[k], recv_sem=recv_sems.at[k], device_id=to, device_id_type=MESH)

        mine = pltpu.make_async_copy(x_ref, slot(*me), local_sem)
        mine.start()
        first = [copy(0, me, sibling, src=x_ref)]
        first += [copy(1 + j, me, (*chip, c), src=x_ref) for j, chip in enumerate(chips)]
        for cp in first:
            cp.start()
        passed = [copy(4 + j, (*chip, c), sibling) for j, chip in enumerate(chips)]
        for j, chip in enumerate(chips):
            copy(1 + j, (*chip, c), me).wait_recv()
            passed[j].start()
        copy(0, sibling, me).wait_recv()
        for j, chip in enumerate(chips):
            copy(4 + j, (*chip, 1 - c), me).wait_recv()
        for cp in first + passed:
            cp.wait_send()
        mine.wait()

    return _pcall(
        body, name=name, out_shape=jax.ShapeDtypeStruct((N_DEV, R, C), v.dtype),
        in_specs=[pl.BlockSpec(memory_space=pltpu.VMEM)], out_specs=pl.BlockSpec(memory_space=pltpu.VMEM),
        scratch_shapes=[pltpu.SemaphoreType.DMA((7,)), pltpu.SemaphoreType.DMA((7,)), pltpu.SemaphoreType.DMA],
        compiler_params=pltpu.CompilerParams(vmem_limit_bytes=VMEM_LIMIT_BYTES),
    )(v)


def _exchange4(name, srcs, bcast):
    n = len(srcs)
    out_shape = [jax.ShapeDtypeStruct(((N_CHIPS,) + s.shape) if bcast else s.shape, s.dtype) for s in srcs]

    def body(*refs):
        src, out = refs[:n], refs[n:2 * n]
        send_sems, recv_sems, local_sems = refs[2 * n:]
        x, y, c = lax.axis_index("x"), lax.axis_index("y"), lax.axis_index("c")
        me = 2 * x + y
        peers = [(1 - x, y), (x, 1 - y), (1 - x, 1 - y)]
        copies = []
        for a in range(n):
            local = pltpu.make_async_copy(src[a] if bcast else src[a].at[me], out[a].at[me], local_sems.at[a])
            local.start()
            copies.append(local)
            for j, (px, py) in enumerate(peers):
                rc = pltpu.make_async_remote_copy(
                    src_ref=src[a] if bcast else src[a].at[2 * px + py], dst_ref=out[a].at[me],
                    send_sem=send_sems.at[a, j], recv_sem=recv_sems.at[a, j], device_id=(px, py, c),
                    device_id_type=MESH)
                rc.start()
                copies.append(rc)
        for cp in copies:
            cp.wait()

    any_spec = pl.BlockSpec(memory_space=pl.ANY)
    return _pcall(
        body, name=name, out_shape=out_shape, in_specs=[any_spec] * n, out_specs=[any_spec] * n,
        scratch_shapes=[pltpu.SemaphoreType.DMA((n, 3)), pltpu.SemaphoreType.DMA((n, 3)),
                        pltpu.SemaphoreType.DMA((n,))],
    )(*srcs)


def _swap_sibling(name, srcs):
    n = len(srcs)

    def body(*refs):
        src, out = refs[:n], refs[n:2 * n]
        send_sems, recv_sems = refs[2 * n:]
        x, y, c = lax.axis_index("x"), lax.axis_index("y"), lax.axis_index("c")
        copies = []
        for a in range(n):
            rc = pltpu.make_async_remote_copy(
                src_ref=src[a], dst_ref=out[a], send_sem=send_sems.at[a], recv_sem=recv_sems.at[a],
                device_id=(x, y, 1 - c), device_id_type=MESH)
            rc.start()
            copies.append(rc)
        for cp in copies:
            cp.wait()

    any_spec = pl.BlockSpec(memory_space=pl.ANY)
    return _pcall(
        body, name=name, out_shape=[jax.ShapeDtypeStruct(s.shape, s.dtype) for s in srcs],
        in_specs=[any_spec] * n, out_specs=[any_spec] * n,
        scratch_shapes=[pltpu.SemaphoreType.DMA((n,)), pltpu.SemaphoreType.DMA((n,))],
    )(*srcs)


def _mod_fwd(c16, mod_w, mod_b_shard):
    nl, D, S = mod_w.shape

    def body(c_ref, w_ref, b_ref, o_ref):
        s = _silu(c_ref[...]).astype(BF16)
        o_ref[...] = jnp.dot(s, w_ref[...].astype(BF16), preferred_element_type=F32) + b_ref[...]

    return _pcall(
        body, name="mod_fwd", grid=(nl,),
        in_specs=[pl.BlockSpec((16, D), lambda l: (0, 0)), pl.BlockSpec((None, D, S), lambda l: (l, 0, 0)),
                  pl.BlockSpec((None, 1, S), lambda l: (l, 0, 0))],
        out_specs=pl.BlockSpec((None, 16, S), lambda l: (l, 0, 0)),
        out_shape=jax.ShapeDtypeStruct((nl, 16, S), F32), compiler_params=_cparams(1),
    )(c16, mod_w, mod_b_shard)


def _mod_w_update(s16t, dm16, w, m, v):
    nl, D, S = w.shape
    tm = _row_tile(D, 256)

    def body(s_ref, dm_ref, w_ref, m_ref, v_ref, g_ref, dl_ref, nm_ref, nv_ref):
        g = jnp.dot(s_ref[...], dm_ref[...], preferred_element_type=F32, precision=HIGHEST)
        g, dl, nm, nv = _f_adamw(w_ref[...], m_ref[...], v_ref[...], g, jnp.zeros_like(g))
        g_ref[...] = g
        dl_ref[...] = dl
        nm_ref[...] = nm
        nv_ref[...] = nv

    big = pl.BlockSpec((None, tm, S), lambda l, i: (l, i, 0))
    return _pcall(
        body, name="mod_w_update", grid=(nl, D // tm),
        in_specs=[pl.BlockSpec((tm, 16), lambda l, i: (i, 0)), pl.BlockSpec((None, 16, S), lambda l, i: (l, 0, 0)),
                  big, big, big],
        out_specs=[big] * 4, out_shape=[jax.ShapeDtypeStruct(w.shape, F32)] * 4, compiler_params=_cparams(2),
    )(s16t, dm16, w, m, v)


def _pack(arrs):
    flat = jnp.concatenate([a.reshape(-1).astype(F32) for a in arrs])
    n = flat.shape[0]
    rows = _round_up(_cdiv(n, LANE), SUBLANE)
    return jnp.pad(flat, (0, rows * LANE - n)).reshape(rows, LANE)


def _unpack(buf, shapes):
    flat = buf.reshape(-1)
    out, pos = [], 0
    for s in shapes:
        n = 1
        for d in s:
            n *= d
        out.append(flat[pos:pos + n].reshape(s))
        pos += n
    return out


SHARD_AXIS = {
    "mod_w": 2, "ssd_w_in": 2, "ssd_conv_w": 2, "ssd_w_out": 1, "conf_w_pw1": 2, "conf_b_pw1": 1, "conf_w_dw": 2,
    "conf_b_dw": 1, "conf_ln_w": 1, "conf_ln_b": 1, "conf_w_pw2": 1, "conf_b_pw2": 1, "ffn_w_up": 2,
    "ffn_conv_w": 3, "ffn_w_down": 1,
}
BIG = ("ssd_w_in", "ssd_w_out", "conf_w_pw1", "conf_w_pw2", "ffn_w_up", "ffn_w_down")
WEIGHTS = ("c_ctx", "mod_w", "mod_b", "norm1_w", "norm2_w", "ssd_w_in", "ssd_conv_w", "ssd_conv_b", "ssd_dt_bias",
           "ssd_a_log", "ssd_d", "ssd_norm_w", "ssd_w_out", "conf_w_pw1", "conf_b_pw1", "conf_w_dw", "conf_b_dw",
           "conf_ln_w", "conf_ln_b", "conf_w_pw2", "conf_b_pw2", "ffn_w_up", "ffn_conv_w", "ffn_conv_b",
           "ffn_w_down", "final_norm_w")
SMALL = tuple(n for n in WEIGHTS if n not in BIG and n != "mod_w")
SMALL_SHARDED = tuple(n for n in SMALL if n in SHARD_AXIS)


def _unshard(stacked, axis):
    return jnp.concatenate([stacked[k] for k in range(N_CHIPS)], axis=axis)


def _to_blocks(full, axis):
    return jnp.stack(jnp.split(full, N_CHIPS, axis=axis))


def _par(v):
    v = v.reshape(-1, v.shape[-1])
    return v[:, None, :]


def kernel(x, c, ctx, c_ctx, mod_w, mod_b, norm1_w, norm2_w, ssd_w_in, ssd_conv_w, ssd_conv_b, ssd_dt_bias, ssd_a_log, ssd_d, ssd_norm_w, ssd_w_out, conf_w_pw1, conf_b_pw1, conf_w_dw, conf_b_dw, conf_ln_w, conf_ln_b, conf_w_pw2, conf_b_pw2, ffn_w_up, ffn_conv_w, ffn_conv_b, ffn_w_down, final_norm_w, loss_target, m_c_ctx, m_mod_w, m_mod_b, m_norm1_w, m_norm2_w, m_ssd_w_in, m_ssd_conv_w, m_ssd_conv_b, m_ssd_dt_bias, m_ssd_a_log, m_ssd_d, m_ssd_norm_w, m_ssd_w_out, m_conf_w_pw1, m_conf_b_pw1, m_conf_w_dw, m_conf_b_dw, m_conf_ln_w, m_conf_ln_b, m_conf_w_pw2, m_conf_b_pw2, m_ffn_w_up, m_ffn_conv_w, m_ffn_conv_b, m_ffn_w_down, m_final_norm_w, v_c_ctx, v_mod_w, v_mod_b, v_norm1_w, v_norm2_w, v_ssd_w_in, v_ssd_conv_w, v_ssd_conv_b, v_ssd_dt_bias, v_ssd_a_log, v_ssd_d, v_ssd_norm_w, v_ssd_w_out, v_conf_w_pw1, v_conf_b_pw1, v_conf_w_dw, v_conf_b_dw, v_conf_ln_w, v_conf_ln_b, v_conf_w_pw2, v_conf_b_pw2, v_ffn_w_up, v_ffn_conv_w, v_ffn_conv_b, v_ffn_w_down, v_final_norm_w):
    given = dict(locals())
    W = {n: given[n] for n in WEIGHTS}
    Mo = {n: given["m_" + n] for n in WEIGHTS}
    Vo = {n: given["v_" + n] for n in WEIGHTS}

    ax, ay, ac = lax.axis_index("x"), lax.axis_index("y"), lax.axis_index("c")
    chip = 2 * ax + ay
    dev = 4 * ax + 2 * ay + ac

    D = x.shape[-1]
    L, Lc = x.shape[1], ctx.shape[1]
    T0 = L + Lc
    H = ssd_a_log.shape[-1]
    DI = ssd_norm_w.shape[-1]
    P = DI // H
    CD = ssd_conv_b.shape[-1]
    N = SSD_STATE
    G = (CD - DI) // (2 * N)
    FH = ffn_conv_b.shape[-1]
    KS = ssd_conv_w.shape[1]
    KC = conf_w_dw.shape[1]
    ncc = Lc // SSD_CHUNK

    small_shard_shapes = [W[n].shape for n in SMALL_SHARDED]
    f1 = _allgather8("gather_small", _pack([c] + [W[n] for n in SMALL_SHARDED]))
    c_rows, full_small = [], {n: [] for n in SMALL_SHARDED}
    for k in range(N_DEV):
        parts = _unpack(f1[k], [c.shape] + small_shard_shapes)
        c_rows.append(parts[0])
        if k % 2 == 0:
            for n, p in zip(SMALL_SHARDED, parts[1:]):
                full_small[n].append(p)
    Wf = dict(W)
    for n in SMALL_SHARDED:
        Wf[n] = jnp.concatenate(full_small[n], axis=SHARD_AXIS[n])
    c16 = jnp.concatenate(c_rows + [c_ctx[None, :], jnp.zeros((16 - N_DEV - 1, D), F32)], axis=0)

    S_mod = mod_w.shape[-1]
    mod_b_shard = lax.dynamic_slice_in_dim(mod_b, chip * S_mod, S_mod, axis=1)[:, None, :]
    mod_part = _mod_fwd(c16, mod_w, mod_b_shard)
    f2 = _allgather8("gather_mod", mod_part.reshape(2 * 16, S_mod))
    mods = jnp.concatenate([f2[2 * k].reshape(2, 16, S_mod) for k in range(N_CHIPS)], axis=-1)
    my = lax.dynamic_slice_in_dim(mods, dev, 1, axis=1)[:, 0]
    sh1, sc1, g1, sh2, sc2, g2 = [[my[l, k * D:(k + 1) * D] for l in range(2)] for k in range(6)]
    csh1, csc1 = mods[0, N_DEV, 0:D], mods[0, N_DEV, D:2 * D]

    gathered = _exchange4("gather_weights", [W[n].astype(BF16) for n in BIG], True)
    Wb = {n: _unshard(g, SHARD_AXIS[n]) for n, g in zip(BIG, gathered)}
    w_in, w_out = Wb["ssd_w_in"][0], Wb["ssd_w_out"][0]
    w_pw1, w_pw2 = Wb["conf_w_pw1"][0], Wb["conf_w_pw2"][0]
    w_up, w_dn = Wb["ffn_w_up"], Wb["ffn_w_down"]

    xl = x[0]
    hcat = jnp.concatenate([ctx[0], xl], axis=0)
    n1w0, n2w0, n1w1, n2w1 = _par(norm1_w[0]), _par(norm2_w[0]), _par(norm1_w[1]), _par(norm2_w[1])
    sc_seg = jnp.stack([csc1, sc1[0]])[:, None, :]
    sh_seg = jnp.stack([csh1, sh1[0]])[:, None, :]

    a0 = _rw_fwd("l0_modnorm1", _f_modnorm, [hcat], [n1w0, sc_seg, sh_seg], [D], seg_rows=(Lc,))
    proj = _mm(a0, w_in, name="l0_w_in")
    seg_taps = [(k - KS // 2, ("seg", Lc)) for k in range(KS)]
    xbc_pre, xbc = _conv_fwd("l0_conv", proj, DI, CD, Wf["ssd_conv_w"][0], ssd_conv_b, seg_taps, act=True)
    dt_raw = proj[:, DI + CD:]
    dt_bias = _par(ssd_dt_bias.reshape(1, 2 * H))
    dt = _rw_fwd("l0_softplus", _f_softplus, [dt_raw], [dt_bias], [2 * H])
    dt_t = dt.T
    dtc, dtr = dt_t[:, :, None], dt_t[:, None, :]
    a_neg = -jnp.exp(ssd_a_log.reshape(2 * H, 1, 1))
    xh = xbc[:, :DI].reshape(T0, H, P).transpose(1, 0, 2)
    yh, s_enter = _ssd_fwd(xh, dtc, dtr, xbc, DI, DI + G * N, a_neg, G, ncc)
    y_dirs = yh[:, :, Lc:, :].transpose(0, 2, 1, 3).reshape(2, L, DI)
    xs_lat, z_lat = xbc[Lc:, :DI], proj[Lc:, :DI]
    d_rep = _par(jnp.repeat(ssd_d[0], P))
    ssd_nw = _par(ssd_norm_w[0])
    yn = _rw_fwd("l0_ssd_gate", _f_ssd_gate, [y_dirs[0], y_dirs[1], xs_lat, z_lat], [d_rep, ssd_nw], [DI])
    mix0 = _mm(yn, w_out, name="l0_w_out")
    g1_0, g2_0, g1_1, g2_1 = _par(g1[0]), _par(g2[0]), _par(g1[1]), _par(g2[1])
    h1 = _rw_fwd("l0_res1", _f_gate_res, [xl, mix0], [g1_0], [D])

    grid_taps = [((i - 1) * GRID_W + (j - 1), (None if j == 1 else ("col", j - 1))) for i in range(3) for j in range(3)]

    def ffn_fwd(l, h, tag):
        a = _rw_fwd(tag + "_modnorm2", _f_modnorm, [h], [_par(norm2_w[l]), _par(sc2[l]), _par(sh2[l])], [D])
        hh = _mm(a, w_up[l], name=tag + "_w_up")
        gc = _conv_fwd(tag + "_conv", hh, FH, FH, Wf["ffn_conv_w"][l].reshape(9, FH), ffn_conv_b[l][None, :],
                       grid_taps)
        act = _rw_fwd(tag + "_act", _f_ffn_act, [(hh, 0, FH), gc], [], [FH], col_tile=_tile(FH, 1536))
        dn = _mm(act, w_dn[l], name=tag + "_w_down")
        return a, hh, gc, act, dn

    a1, hh0, gc0, act0, dn0 = ffn_fwd(0, h1, "l0")
    h2 = _rw_fwd("l0_res2", _f_gate_res, [h1, dn0], [g2_0], [D])

    a2 = _rw_fwd("l1_modnorm1", _f_modnorm, [h2], [n1w1, _par(sc1[1]), _par(sh1[1])], [D])
    pw = _mm(a2, w_pw1, name="l1_pw1")
    b_pw1 = Wf["conf_b_pw1"][0]
    glu = _rw_fwd("l1_glu", _f_glu, [(pw, 0, D), (pw, D, D)], [_par(b_pw1[:D]), _par(b_pw1[D:])], [D])
    conf_taps = [(k - KC // 2, None) for k in range(KC)]
    cv = _conv_fwd("l1_conv", glu, 0, D, Wf["conf_w_dw"][0], Wf["conf_b_dw"], conf_taps)
    ln_w, ln_b = _par(Wf["conf_ln_w"][0]), _par(Wf["conf_ln_b"][0])
    ls = _rw_fwd("l1_ln_silu", _f_ln_silu, [cv], [ln_w, ln_b], [D])
    p2 = _mm(ls, w_pw2, name="l1_pw2")
    b_pw2 = _par(Wf["conf_b_pw2"][0])
    h3 = _rw_fwd("l1_res1", _f_gate_res_bias, [h2, p2], [g1_1, b_pw2], [D])
    a3, hh1, gc1, act1, dn1 = ffn_fwd(1, h3, "l1")
    h4 = _rw_fwd("l1_res2", _f_gate_res, [h3, dn1], [g2_1], [D])

    fnw = final_norm_w[None, :]
    tgt = loss_target[0]
    loss_local = _loss_fwd(h4, tgt, fnw)[0, 0]
    loss = lax.psum(loss_local, ("x", "y", "c"))

    G_full = {}
    ones = jnp.ones((L, 1), F32)
    (dh4,), (dfnw,) = _rw_bwd("loss_bwd", _f_loss_rows, [h4, tgt], [_par(final_norm_w)], [ones],
                              row_grad=[True, False], par_grad=[True])
    G_full["final_norm_w"] = dfnw.reshape(D)

    def ffn_bwd(l, h, saved, g2_l, dh_out, tag):
        a, hh, gc, act, dn = saved
        (ddn,), (dg2,) = _rw_bwd(tag + "_res2_bwd", _f_gate_res, [h, dn], [g2_l], [dh_out],
                                 row_grad=[False, True], par_grad=[True])
        dact = _mm(ddn, w_dn[l], tb=True, name=tag + "_w_down_dx")
        dwdn = _mm(act, ddn, ta=True, name=tag + "_w_down_dw")
        (dval, dgc), _ = _rw_bwd(tag + "_act_bwd", _f_ffn_act, [(hh, 0, FH), gc], [], [dact],
                                 row_grad=[True, True], par_grad=[], col_tile=_tile(FH, 1536))
        dgin, dcw, dcb = _conv_bwd(tag + "_conv_bwd", hh, FH, FH, Wf["ffn_conv_w"][l].reshape(9, FH), dgc, grid_taps)
        dhh = jnp.concatenate([dval, dgin], axis=1)
        da = _mm(dhh, w_up[l], tb=True, name=tag + "_w_up_dx")
        dwup = _mm(a, dhh, ta=True, name=tag + "_w_up_dw")
        (dh,), (dn2w, dsc2, dsh2) = _rw_bwd(
            tag + "_modnorm2_bwd", _f_modnorm, [h], [_par(norm2_w[l]), _par(sc2[l]), _par(sh2[l])], [da],
            row_grad=[True], par_grad=[True, True, True], add=dh_out)
        return dh, dict(w_down=dwdn, w_up=dwup, conv_w=dcw.reshape(3, 3, FH), conv_b=dcb.reshape(FH),
                        n2w=dn2w.reshape(D), sc2=dsc2.reshape(D), sh2=dsh2.reshape(D), g2=dg2.reshape(D))

    dh3, gf1 = ffn_bwd(1, h3, (a3, hh1, gc1, act1, dn1), g2_1, dh4, "l1")
    (dp2,), (dg1_1, db_pw2) = _rw_bwd("l1_res1_bwd", _f_gate_res_bias, [h2, p2], [g1_1, b_pw2], [dh3],
                                      row_grad=[False, True], par_grad=[True, True])
    dls = _mm(dp2, w_pw2, tb=True, name="l1_pw2_dx")
    G_full["conf_w_pw2"] = _mm(ls, dp2, ta=True, name="l1_pw2_dw")[None]
    (dcv,), (dln_w, dln_b) = _rw_bwd("l1_ln_silu_bwd", _f_ln_silu, [cv], [ln_w, ln_b], [dls],
                                     row_grad=[True], par_grad=[True, True])
    dglu, dw_dw, db_dw = _conv_bwd("l1_conv_bwd", glu, 0, D, Wf["conf_w_dw"][0], dcv, conf_taps)
    (dpa, dpg), (dba, dbg) = _rw_bwd("l1_glu_bwd", _f_glu, [(pw, 0, D), (pw, D, D)],
                                     [_par(b_pw1[:D]), _par(b_pw1[D:])], [dglu],
                                     row_grad=[True, True], par_grad=[True, True])
    dpw = jnp.concatenate([dpa, dpg], axis=1)
    da2 = _mm(dpw, w_pw1, tb=True, name="l1_pw1_dx")
    G_full["conf_w_pw1"] = _mm(a2, dpw, ta=True, name="l1_pw1_dw")[None]
    (dh2,), (dn1w1, dsc1_1, dsh1_1) = _rw_bwd(
        "l1_modnorm1_bwd", _f_modnorm, [h2], [n1w1, _par(sc1[1]), _par(sh1[1])], [da2],
        row_grad=[True], par_grad=[True, True, True], add=dh3)
    G_full["conf_b_pw2"] = db_pw2.reshape(1, D)
    G_full["conf_ln_w"], G_full["conf_ln_b"] = dln_w.reshape(1, D), dln_b.reshape(1, D)
    G_full["conf_w_dw"], G_full["conf_b_dw"] = dw_dw[None], db_dw.reshape(1, D)
    G_full["conf_b_pw1"] = jnp.concatenate([dba.reshape(1, D), dbg.reshape(1, D)], axis=1)

    dh1, gf0 = ffn_bwd(0, h1, (a1, hh0, gc0, act0, dn0), g2_0, dh2, "l0")
    G_full["ffn_w_up"] = jnp.stack([gf0["w_up"], gf1["w_up"]])
    G_full["ffn_w_down"] = jnp.stack([gf0["w_down"], gf1["w_down"]])
    G_full["ffn_conv_w"] = jnp.stack([gf0["conv_w"], gf1["conv_w"]])
    G_full["ffn_conv_b"] = jnp.stack([gf0["conv_b"], gf1["conv_b"]])

    (dmix,), (dg1_0,) = _rw_bwd("l0_res1_bwd", _f_gate_res, [xl, mix0], [g1_0], [dh1],
                                row_grad=[False, True], par_grad=[True])
    dyn = _mm(dmix, w_out, tb=True, name="l0_w_out_dx")
    G_full["ssd_w_out"] = _mm(yn, dmix, ta=True, name="l0_w_out_dw")[None]
    (dy_lat, dxs_gate, dz_lat), (dd_rep, dssd_nw) = _rw_bwd(
        "l0_ssd_gate_bwd", _f_ssd_gate, [y_dirs[0], y_dirs[1], xs_lat, z_lat], [d_rep, ssd_nw], [dyn],
        row_grad=[True, False, True, True], par_grad=[True, True])
    dyh = jnp.pad(dy_lat, ((Lc, 0), (0, 0))).reshape(T0, H, P).transpose(1, 0, 2)
    dxh, ddtc, ddtr, dB, dC, da_parts = _ssd_bwd(xh, dtc, dtr, xbc, DI, DI + G * N, a_neg, s_enter, dyh, G, ncc)
    dxs = (dxh[0] + dxh[1]).transpose(1, 0, 2).reshape(T0, DI) + jnp.pad(dxs_gate, ((Lc, 0), (0, 0)))
    dxbc = jnp.concatenate([dxs, dB[0] + dB[1], dC[0] + dC[1]], axis=1)
    (dxbc_pre,), _ = _rw_bwd("l0_silu_bwd", _silu, [xbc_pre], [], [dxbc], row_grad=[True], par_grad=[],
                             col_tile=_tile(CD, 1024))
    dconv_in, dcw0, dcb0 = _conv_bwd("l0_conv_bwd", proj, DI, CD, Wf["ssd_conv_w"][0], dxbc_pre, seg_taps)
    ddt = ddtc[:, :, 0].T + ddtr[:, 0, :].T
    (ddt_raw,), (ddt_bias,) = _rw_bwd("l0_softplus_bwd", _f_softplus, [dt_raw], [dt_bias], [ddt],
                                      row_grad=[True], par_grad=[True])
    dproj = jnp.concatenate([jnp.pad(dz_lat, ((Lc, 0), (0, 0))), dconv_in, ddt_raw], axis=1)
    da0 = _mm(dproj, w_in, tb=True, name="l0_w_in_dx")
    G_full["ssd_w_in"] = _mm(a0, dproj, ta=True, name="l0_w_in_dw")[None]
    (dhcat,), (dn1w0, dsc_seg, dsh_seg) = _rw_bwd(
        "l0_modnorm1_bwd", _f_modnorm, [hcat], [n1w0, sc_seg, sh_seg], [da0],
        row_grad=[True], par_grad=[True, True, True], seg_rows=(Lc,))
    grad_x = (dhcat[Lc:] + dh1)[None]

    da_heads = da_parts[:, 0, 0].reshape(2, G, T0 // SSD_CHUNK, H // G).sum(axis=2).reshape(1, 2, H)
    G_full["ssd_a_log"] = da_heads * (-jnp.exp(ssd_a_log))
    G_full["ssd_dt_bias"] = ddt_bias.reshape(1, 2, H)
    G_full["ssd_d"] = dd_rep.reshape(H, P).sum(axis=1)[None]
    G_full["ssd_norm_w"] = dssd_nw.reshape(1, DI)
    G_full["ssd_conv_w"], G_full["ssd_conv_b"] = dcw0[None], dcb0.reshape(1, CD)
    G_full["norm1_w"] = jnp.stack([dn1w0.reshape(D), dn1w1.reshape(D)])
    G_full["norm2_w"] = jnp.stack([gf0["n2w"], gf1["n2w"]])

    zD = jnp.zeros((D,), F32)
    dm_own = jnp.stack([
        jnp.concatenate([dsh_seg[1, 0], dsc_seg[1, 0], dg1_0.reshape(D), gf0["sh2"], gf0["sc2"], gf0["g2"]]),
        jnp.concatenate([dsh1_1.reshape(D), dsc1_1.reshape(D), dg1_1.reshape(D), gf1["sh2"], gf1["sc2"], gf1["g2"]]),
    ])
    dmc_own = jnp.concatenate([dsh_seg[0, 0], dsc_seg[0, 0], zD, zD, zD, zD])

    small_sum_names = [n for n in SMALL if n not in ("c_ctx", "mod_b")]
    sum_part = [G_full[n] for n in small_sum_names] + [dmc_own]
    n_sum = sum(int(a.size) for a in sum_part)
    packed = _pack(sum_part + [dm_own])
    gat = _allgather8("gather_small_grads", packed)
    total = _sum_leading("sum_small_grads", gat, tuple(range(N_DEV)))
    summed = _unpack(total, [a.shape for a in sum_part])
    Gs = dict(zip(small_sum_names, summed[:-1]))
    dmc_tot = summed[-1]
    dm_all = jnp.stack([gat[k].reshape(-1)[n_sum:n_sum + 2 * 6 * D].reshape(2, 6 * D) for k in range(N_DEV)], axis=1)
    dm16 = jnp.concatenate([dm_all, jnp.stack([dmc_tot, jnp.zeros_like(dmc_tot)])[:, None, :],
                            jnp.zeros((2, 16 - N_DEV - 1, 6 * D), F32)], axis=1)
    Gs["mod_b"] = _sum_leading("sum_mod_b", dm16.transpose(1, 0, 2).reshape(16, 2 * 6 * D // LANE, LANE),
                               tuple(range(N_DEV + 1))).reshape(2, 6 * D)

    dm16_shard = lax.dynamic_slice_in_dim(dm16, chip * S_mod, S_mod, axis=2)
    ds16 = _mm(dm16_shard[0], mod_w[0], tb=True, precision=HIGHEST, name="c_ctx_dx")
    sig = jax.nn.sigmoid(c_ctx)
    dcc_part = ds16[N_DEV] * (sig * (1.0 + c_ctx * (1.0 - sig)))
    gat_cc = _allgather8("gather_c_ctx_grad", _pack([dcc_part]))
    Gs["c_ctx"] = _sum_leading("sum_c_ctx_grad", gat_cc, (0, 2, 4, 6)).reshape(-1)[:D]

    s16t = _silu(c16).T
    out = {}
    out["mod_w"] = _mod_w_update(s16t, dm16_shard, mod_w, m_mod_w, v_mod_w)

    blocks = [_to_blocks(G_full[n], SHARD_AXIS[n]).astype(BF16) for n in BIG]
    landed = _exchange4("reduce_grads", blocks, False)
    partial = []
    for n, blk in zip(BIG, landed):
        r = blk.reshape(N_CHIPS, -1, blk.shape[-1])
        partial.append(_sum_leading("sum4_" + n, r, (0, 1, 2, 3)).reshape(W[n].shape))
    sibling = _swap_sibling("swap_grads", partial)
    for n, mine, sib in zip(BIG, partial, sibling):
        out[n] = _adamw("adamw_" + n, W[n], Mo[n], Vo[n], mine, sib)

    def own(n, full):
        if n in SHARD_AXIS:
            size = W[n].shape[SHARD_AXIS[n]]
            return lax.dynamic_slice_in_dim(full, chip * size, size, axis=SHARD_AXIS[n])
        return full

    g_small = [own(n, Gs[n].reshape(Wf[n].shape)) for n in SMALL]
    shapes = [W[n].shape for n in SMALL]
    pk = [_pack([W[n] for n in SMALL]), _pack([Mo[n] for n in SMALL]), _pack([Vo[n] for n in SMALL]), _pack(g_small)]
    res = _adamw("adamw_small", pk[0], pk[1], pk[2], pk[3], jnp.zeros_like(pk[3]))
    unpacked = [_unpack(r, shapes) for r in res]
    for k, n in enumerate(SMALL):
        out[n] = tuple(u[k] for u in unpacked)

    grads = [out[n][0] for n in WEIGHTS]
    deltas = [out[n][1] for n in WEIGHTS]
    new_m = [out[n][2] for n in WEIGHTS]
    new_v = [out[n][3] for n in WEIGHTS]
    return (loss, grad_x, *grads, *deltas, *new_m, *new_v)
```

```python
import functools

import jax
import jax.numpy as jnp
from jax import lax
from jax.experimental import pallas as pl
from jax.experimental.pallas import tpu as pltpu

F32 = jnp.float32
BF16 = jnp.bfloat16
MESH = pl.DeviceIdType.MESH
HIGHEST = lax.Precision.HIGHEST

VMEM_LIMIT_BYTES = 48 * 1024 * 1024
LANE = 128
SUBLANE = 8

SSD_STATE = 128
SSD_CHUNK = 128
GRID_W = 64
EPS = 1e-6
N_CHIPS = 4
N_DEV = 8

ADAM_LR = 0.001
ADAM_B1 = 0.9
ADAM_B2 = 0.999
ADAM_EPS = 1e-08
ADAM_WD = 0.01
ADAM_STEP = 10


def _pcall(body, **kw):
    return pl.pallas_call(body, **kw)


def _cparams(n_grid):
    return pltpu.CompilerParams(dimension_semantics=("arbitrary",) * n_grid, vmem_limit_bytes=VMEM_LIMIT_BYTES)


def _cdiv(a, b):
    return -(-a // b)


def _round_up(a, b):
    return _cdiv(a, b) * b


def _tile(n, cap):
    if n <= cap:
        return n
    best = None
    for t in range(LANE, cap + 1, LANE):
        if n % t == 0:
            best = t
    if best is None:
        npad = _round_up(n, LANE)
        for t in range(LANE, cap + 1, LANE):
            if npad % t == 0:
                best = t
    return best


def _row_tile(n, cap, also=()):
    best = None
    for t in range(SUBLANE, min(cap, n) + 1, SUBLANE):
        if n % t == 0 and all(a % t == 0 for a in also):
            best = t
    assert best is not None, (n, cap, also)
    return best


def _silu(v):
    return v * jax.nn.sigmoid(v)


def _mm(a, b, *, name, ta=False, tb=False, precision=None, cap=1024):
    M, K = (a.shape[1], a.shape[0]) if ta else a.shape
    N = b.shape[0] if tb else b.shape[1]
    assert K == (b.shape[1] if tb else b.shape[0]), (a.shape, b.shape, ta, tb)
    tm, tn, tk = _tile(M, cap), _tile(N, cap), _tile(K, cap)
    nm, nn, nk = _cdiv(M, tm), _cdiv(N, tn), _cdiv(K, tk)
    k_tail = K % tk
    exact = precision is not None

    def body(a_ref, b_ref, o_ref, acc_ref):
        k = pl.program_id(2)

        @pl.when(k == 0)
        def _():
            acc_ref[...] = jnp.zeros_like(acc_ref)

        av = a_ref[...]
        bv = b_ref[...]
        if k_tail:
            lim = K - k * tk
            ka = lax.broadcasted_iota(jnp.int32, av.shape, 0 if ta else 1)
            kb = lax.broadcasted_iota(jnp.int32, bv.shape, 1 if tb else 0)
            av = jnp.where(ka < lim, av, jnp.zeros_like(av))
            bv = jnp.where(kb < lim, bv, jnp.zeros_like(bv))
        if exact:
            av = av.astype(F32)
            bv = bv.astype(F32)
        else:
            av = av.astype(BF16)
            bv = bv.astype(BF16)
        dn = (((0 if ta else 1,), (1 if tb else 0,)), ((), ()))
        acc_ref[...] += lax.dot_general(av, bv, dn, preferred_element_type=F32, precision=precision)

        @pl.when(k == nk - 1)
        def _():
            o_ref[...] = acc_ref[...]

    a_spec = pl.BlockSpec((tk, tm), lambda i, j, k: (k, i)) if ta else pl.BlockSpec((tm, tk), lambda i, j, k: (i, k))
    b_spec = pl.BlockSpec((tn, tk), lambda i, j, k: (j, k)) if tb else pl.BlockSpec((tk, tn), lambda i, j, k: (k, j))
    return _pcall(
        body, name=name, grid=(nm, nn, nk), in_specs=[a_spec, b_spec],
        out_specs=pl.BlockSpec((tm, tn), lambda i, j, k: (i, j)),
        out_shape=jax.ShapeDtypeStruct((M, N), F32),
        scratch_shapes=[pltpu.VMEM((tm, tn), F32)], compiler_params=_cparams(3),
    )(a, b)


def _norm_rows(rows):
    out = []
    for r in rows:
        if not isinstance(r, tuple):
            r = (r,)
        arr, off, width, roff = (r + (0, None, 0)[len(r) - 1:])
        out.append((arr, off, width if width is not None else arr.shape[1], roff))
    return out


def _rw_plan(T, rows, pars, seg_rows, col_tile, tm_cap):
    widths = [r[2] for r in rows]
    wmax = max(widths + [p.shape[-1] for p in pars] + [1])
    if col_tile is not None:
        assert all(w == widths[0] for w in widths) and all(p.shape[-1] == widths[0] for p in pars)
        ncol = widths[0] // col_tile
        assert ncol * col_tile == widths[0]
        wmax = col_tile
    else:
        ncol = 1
    cap = tm_cap if tm_cap is not None else max(SUBLANE, min(256, (256 * 1024) // wmax))
    tm = _row_tile(T, cap, also=tuple(seg_rows) + tuple(r[3] for r in rows if r[3]))
    bounds = tuple(s // tm for s in seg_rows)
    return widths, ncol, tm, bounds


def _rw_specs(rows, pars, ncol, tm, bounds, col_tile):
    def seg(i):
        s = 0
        for b in bounds:
            s = s + (i >= b).astype(jnp.int32)
        return s

    specs = []
    for arr, off, w, roff in rows:
        bw = col_tile if col_tile is not None else w
        assert off % bw == 0 and roff % tm == 0, (off, bw, roff, tm)
        specs.append(pl.BlockSpec((tm, bw), functools.partial(lambda j, i, ob, rb: (i + rb, ob + j),
                                                              ob=off // bw, rb=roff // tm)))
    for p in pars:
        bw = col_tile if col_tile is not None else p.shape[-1]
        if p.shape[0] > 1:
            specs.append(pl.BlockSpec((None, 1, bw), lambda j, i: (seg(i), 0, j)))
        else:
            specs.append(pl.BlockSpec((None, 1, bw), lambda j, i: (0, 0, j)))
    return specs, seg


def _rw_fwd(name, f, rows, pars, out_widths, *, T=None, seg_rows=(), col_tile=None, tm_cap=None):
    rows = _norm_rows(rows)
    T = rows[0][0].shape[0] if T is None else T
    widths, ncol, tm, bounds = _rw_plan(T, rows, pars, seg_rows, col_tile, tm_cap)
    in_specs, _ = _rw_specs(rows, pars, ncol, tm, bounds, col_tile)
    nr, npar, nout = len(rows), len(pars), len(out_widths)

    def body(*refs):
        vals = [r[...] for r in refs[:nr + npar]]
        outs = f(*vals)
        if not isinstance(outs, (tuple, list)):
            outs = (outs,)
        for o_ref, o in zip(refs[nr + npar:], outs):
            o_ref[...] = o.astype(o_ref.dtype)

    out_specs = [pl.BlockSpec((tm, col_tile if col_tile is not None else w), lambda j, i: (i, j)) for w in out_widths]
    res = _pcall(
        body, name=name, grid=(ncol, T // tm), in_specs=in_specs, out_specs=out_specs,
        out_shape=[jax.ShapeDtypeStruct((T, w), F32) for w in out_widths], compiler_params=_cparams(2),
    )(*[r[0] for r in rows], *pars)
    return res if nout > 1 else res[0]


def _rw_bwd(name, f, rows, pars, cots, *, row_grad, par_grad, T=None, seg_rows=(), col_tile=None, tm_cap=None,
            add=None, cot_fn=None):
    rows = _norm_rows(rows)
    cots = _norm_rows(cots)
    T = rows[0][0].shape[0] if T is None else T
    extra = _norm_rows([add]) if add is not None else []
    all_rows = rows + cots + extra
    widths, ncol, tm, bounds = _rw_plan(T, all_rows, pars, seg_rows, col_tile, tm_cap)
    in_specs, seg = _rw_specs(all_rows, pars, ncol, tm, bounds, col_tile)
    nr, nc, ne, npar = len(rows), len(cots), len(extra), len(pars)
    row_idx = [k for k in range(nr) if row_grad[k]]
    par_idx = [k for k in range(npar) if par_grad[k]]

    def body(*refs):
        i = pl.program_id(1)
        row_vals = [r[...] for r in refs[:nr]]
        cot_vals = [r[...] for r in refs[nr:nr + nc]]
        add_vals = [r[...] for r in refs[nr + nc:nr + nc + ne]]
        par_vals = [r[...] for r in refs[nr + nc + ne:nr + nc + ne + npar]]
        out_refs = refs[nr + nc + ne + npar:]
        outs, vjp = jax.vjp(f, *row_vals, *par_vals)
        if cot_fn is not None:
            cot_vals = cot_fn(*cot_vals)
            if not isinstance(cot_vals, (tuple, list)):
                cot_vals = (cot_vals,)
        if isinstance(outs, (tuple, list)):
            grads = vjp(tuple(c.astype(o.dtype) for c, o in zip(cot_vals, outs)))
        else:
            grads = vjp(cot_vals[0].astype(outs.dtype))
        first_seg = i == 0
        for b in bounds:
            first_seg = first_seg | (i == b)
        for n, k in enumerate(row_idx):
            g = grads[k]
            if n == 0 and add_vals:
                g = g + add_vals[0]
            out_refs[n][...] = g
        for n, k in enumerate(par_idx):
            g = grads[nr + k]
            o_ref = out_refs[len(row_idx) + n]
            first = first_seg if pars[k].shape[0] > 1 else (i == 0)

            @pl.when(first)
            def _(o_ref=o_ref, g=g):
                o_ref[...] = g

            @pl.when(jnp.logical_not(first))
            def _(o_ref=o_ref, g=g):
                o_ref[...] += g

    out_specs, out_shape = [], []
    for k in row_idx:
        w = widths[k]
        out_specs.append(pl.BlockSpec((tm, col_tile if col_tile is not None else w), lambda j, i: (i, j)))
        out_shape.append(jax.ShapeDtypeStruct((T, w), F32))
    for k in par_idx:
        p = pars[k]
        bw = col_tile if col_tile is not None else p.shape[-1]
        if p.shape[0] > 1:
            out_specs.append(pl.BlockSpec((None, 1, bw), lambda j, i: (seg(i), 0, j)))
        else:
            out_specs.append(pl.BlockSpec((None, 1, bw), lambda j, i: (0, 0, j)))
        out_shape.append(jax.ShapeDtypeStruct(p.shape, F32))
    res = _pcall(
        body, name=name, grid=(ncol, T // tm), in_specs=in_specs, out_specs=out_specs, out_shape=out_shape,
        compiler_params=_cparams(2),
    )(*[r[0] for r in all_rows], *pars)
    return list(res[:len(row_idx)]), list(res[len(row_idx):])


def _f_modnorm(h, w, sc, sh):
    y = h * lax.rsqrt(jnp.mean(h * h, axis=-1, keepdims=True) + EPS)
    return (y * w) * (1.0 + sc) + sh


def _f_gate_res(h, y, g):
    return h + g * y


def _f_gate_res_bias(h, y, g, b):
    return h + g * (y + b)


def _f_ffn_act(val, gate):
    return _silu(gate) * val


def _f_softplus(raw, bias):
    v = raw + bias
    return jnp.maximum(v, 0.0) + jnp.log(1.0 + jnp.exp(-jnp.abs(v)))


def _f_ssd_gate(yf, yb, xs, z, d_rep, nw):
    y = (yf + yb + d_rep * xs) * _silu(z)
    return (y * lax.rsqrt(jnp.mean(y * y, axis=-1, keepdims=True) + EPS)) * nw


def _f_glu(a, g, ba, bg):
    return (a + ba) * jax.nn.sigmoid(g + bg)


def _f_ln_silu(h, w, b):
    mu = jnp.mean(h, axis=-1, keepdims=True)
    d = h - mu
    y = d * lax.rsqrt(jnp.mean(d * d, axis=-1, keepdims=True) + EPS)
    return _silu(y * w + b)


def _f_loss_rows(h, t, w):
    y = (h * lax.rsqrt(jnp.mean(h * h, axis=-1, keepdims=True) + EPS)) * w
    e = y - t
    return 0.5 * jnp.mean(e * e, axis=-1, keepdims=True)


def _f_adamw(w, m, v, ga, gb):
    g = ga + gb
    m = ADAM_B1 * m + (1.0 - ADAM_B1) * g
    v = ADAM_B2 * v + (1.0 - ADAM_B2) * (g * g)
    m_hat = m / (1.0 - ADAM_B1 ** ADAM_STEP)
    v_hat = v / (1.0 - ADAM_B2 ** ADAM_STEP)
    delta = -ADAM_LR * (m_hat / (jnp.sqrt(v_hat) + ADAM_EPS) + ADAM_WD * w)
    return g, delta, m, v


def _adamw(name, w, m, v, ga, gb):
    shape = w.shape
    c = shape[-1]
    two_d = [t.reshape(-1, c) for t in (w, m, v, ga, gb)]
    rows = two_d[0].shape[0]
    pad = _round_up(rows, SUBLANE) - rows
    if pad:
        two_d = [jnp.pad(t, ((0, pad), (0, 0))) for t in two_d]
    outs = _rw_fwd(name, _f_adamw, two_d, [], [c] * 4)
    return tuple(o[:rows].reshape(shape) for o in outs)


def _sum_leading(name, x, idxs):
    _, R, C = x.shape
    tm = _row_tile(R, max(SUBLANE, min(512, (512 * 1024) // C)))

    def body(x_ref, o_ref):
        acc = x_ref[idxs[0]].astype(F32)
        for k in idxs[1:]:
            acc = acc + x_ref[k].astype(F32)
        o_ref[...] = acc

    return _pcall(
        body, name=name, grid=(R // tm,), in_specs=[pl.BlockSpec((x.shape[0], tm, C), lambda i: (0, i, 0))],
        out_specs=pl.BlockSpec((tm, C), lambda i: (i, 0)), out_shape=jax.ShapeDtypeStruct((R, C), F32),
        compiler_params=_cparams(1),
    )(x)


def _loss_fwd(h, t, w):
    T, D = h.shape
    tm = _row_tile(T, 256)

    def body(h_ref, t_ref, w_ref, o_ref):
        i = pl.program_id(0)
        part = jnp.sum(_f_loss_rows(h_ref[...], t_ref[...], w_ref[...]), axis=0, keepdims=True)
        part = jnp.broadcast_to(part, (1, LANE))

        @pl.when(i == 0)
        def _():
            o_ref[...] = part

        @pl.when(i > 0)
        def _():
            o_ref[...] += part

    return _pcall(
        body, name="loss_fwd", grid=(T // tm,),
        in_specs=[pl.BlockSpec((tm, D), lambda i: (i, 0)), pl.BlockSpec((tm, D), lambda i: (i, 0)),
                  pl.BlockSpec((1, D), lambda i: (0, 0))],
        out_specs=pl.BlockSpec((1, LANE), lambda i: (0, 0)), out_shape=jax.ShapeDtypeStruct((1, LANE), F32),
        compiler_params=_cparams(1),
    )(h, t, w)


CONV_ROWS = 256
CONV_ACC_ELEMS = 16384


def _tap_mask(mask, t, s):
    if mask is None:
        return None
    kind, arg = mask
    if kind == "seg":
        if s == 0:
            return None
        return (t >= arg) == ((t + s) >= arg)
    col = jnp.bitwise_and(t, GRID_W - 1)
    return (col != 0) if arg < 0 else (col != GRID_W - 1)


def _conv_plan(T, C, taps):
    rc = CONV_ROWS if T % CONV_ROWS == 0 else LANE
    assert T % rc == 0
    ct = next((t for t in (512, 256, LANE) if C % t == 0), C)
    reach = max(abs(s) for s, _ in taps)
    hb = next(h for h in (8, 16, 32, 64, 128, 256) if h >= reach and rc % h == 0)
    sub = max(SUBLANE, min(rc, CONV_ACC_ELEMS // ct))
    return rc, ct, hb, sub, T // rc, C // ct


def _halo_specs(rc, ct, hb, T, off_blocks):
    per = rc // hb
    last = T // hb - 1
    prev = pl.BlockSpec((hb, ct), lambda j, i: (jnp.maximum(i * per - 1, 0), off_blocks + j))
    cur = pl.BlockSpec((rc, ct), lambda j, i: (i, off_blocks + j))
    nxt = pl.BlockSpec((hb, ct), lambda j, i: (jnp.minimum((i + 1) * per, last), off_blocks + j))
    return [prev, cur, nxt]


def _fill_halo(pad_ref, p_ref, c_ref, n_ref, i, nrc, rc, hb):
    pad_ref[0:hb, :] = jnp.where(i > 0, p_ref[...], 0.0)
    pad_ref[hb:hb + rc, :] = c_ref[...]
    pad_ref[hb + rc:hb + rc + hb, :] = jnp.where(i < nrc - 1, n_ref[...], 0.0)


def _conv_fwd(name, u, col_off, C, w, b, taps, act=False):
    T = u.shape[0]
    rc, ct, hb, sub, nrc, ncc = _conv_plan(T, C, taps)
    assert col_off % ct == 0
    K = len(taps)

    def body(up, uc, un, w_ref, b_ref, *rest):
        y_ref = rest[0]
        pad_ref = rest[-1]
        i = pl.program_id(1)
        _fill_halo(pad_ref, up, uc, un, i, nrc, rc, hb)
        for r0 in range(0, rc, sub):
            t = i * rc + r0 + lax.broadcasted_iota(jnp.int32, (sub, 1), 0)
            acc = jnp.broadcast_to(b_ref[...], (sub, ct))
            for k, (s, mask) in enumerate(taps):
                v = pad_ref[hb + r0 + s:hb + r0 + s + sub, :]
                m = _tap_mask(mask, t, s)
                if m is not None:
                    v = jnp.where(m, v, 0.0)
                acc = acc + w_ref[k:k + 1, :] * v
            y_ref[r0:r0 + sub, :] = acc
            if act:
                rest[1][r0:r0 + sub, :] = _silu(acc)

    n_out = 2 if act else 1
    res = _pcall(
        body, name=name, grid=(ncc, nrc),
        in_specs=_halo_specs(rc, ct, hb, T, col_off // ct) + [pl.BlockSpec((K, ct), lambda j, i: (0, j)),
                                                              pl.BlockSpec((1, ct), lambda j, i: (0, j))],
        out_specs=[pl.BlockSpec((rc, ct), lambda j, i: (i, j))] * n_out,
        out_shape=[jax.ShapeDtypeStruct((T, C), F32)] * n_out,
        scratch_shapes=[pltpu.VMEM((rc + 2 * hb, ct), F32)], compiler_params=_cparams(2),
    )(u, u, u, w, b)
    return res if act else res[0]


def _conv_bwd(name, u, col_off, C, w, g, taps):
    T = u.shape[0]
    rc, ct, hb, sub, nrc, ncc = _conv_plan(T, C, taps)
    K = len(taps)

    def body(up, uc, un, gp, gc, gn, w_ref, du_ref, dw_ref, db_ref, upad, gpad):
        i = pl.program_id(1)
        _fill_halo(upad, up, uc, un, i, nrc, rc, hb)
        _fill_halo(gpad, gp, gc, gn, i, nrc, rc, hb)

        @pl.when(i == 0)
        def _():
            dw_ref[...] = jnp.zeros_like(dw_ref)
            db_ref[...] = jnp.zeros_like(db_ref)

        def fold(v):
            return jnp.sum(v.reshape(sub // SUBLANE, SUBLANE, ct), axis=0)

        dws = [jnp.zeros((SUBLANE, ct), F32) for _ in range(K)]
        dbs = jnp.zeros((SUBLANE, ct), F32)
        for r0 in range(0, rc, sub):
            t = i * rc + r0 + lax.broadcasted_iota(jnp.int32, (sub, 1), 0)
            gv = gpad[hb + r0:hb + r0 + sub, :]
            dbs = dbs + fold(gv)
            acc = jnp.zeros((sub, ct), F32)
            for k, (s, mask) in enumerate(taps):
                gs = gpad[hb + r0 - s:hb + r0 - s + sub, :]
                m = _tap_mask(mask, t - s, s)
                if m is not None:
                    gs = jnp.where(m, gs, 0.0)
                acc = acc + w_ref[k:k + 1, :] * gs
                uv = upad[hb + r0 + s:hb + r0 + s + sub, :]
                m = _tap_mask(mask, t, s)
                prod = gv * uv
                if m is not None:
                    prod = jnp.where(m, prod, 0.0)
                dws[k] = dws[k] + fold(prod)
            du_ref[r0:r0 + sub, :] = acc
        for k in range(K):
            dw_ref[k:k + 1, :] += jnp.sum(dws[k], axis=0, keepdims=True)
        db_ref[...] += jnp.sum(dbs, axis=0, keepdims=True)

    halo_u = _halo_specs(rc, ct, hb, T, col_off // ct)
    halo_g = _halo_specs(rc, ct, hb, T, 0)
    return _pcall(
        body, name=name, grid=(ncc, nrc),
        in_specs=halo_u + halo_g + [pl.BlockSpec((K, ct), lambda j, i: (0, j))],
        out_specs=[pl.BlockSpec((rc, ct), lambda j, i: (i, j)), pl.BlockSpec((K, ct), lambda j, i: (0, j)),
                   pl.BlockSpec((1, ct), lambda j, i: (0, j))],
        out_shape=[jax.ShapeDtypeStruct((T, C), F32), jax.ShapeDtypeStruct((K, C), F32),
                   jax.ShapeDtypeStruct((1, C), F32)],
        scratch_shapes=[pltpu.VMEM((rc + 2 * hb, ct), F32), pltpu.VMEM((rc + 2 * hb, ct), F32)],
        compiler_params=_cparams(2),
    )(u, u, u, g, g, g, w)


def _ssd_group(xg, bm, cm, s_in, *per_head, sgn, P):
    R = len(per_head) // 3
    dtcs, dtrs, a_s = per_head[:R], per_head[R:2 * R], per_head[2 * R:]
    q, rp = xg.shape
    ii = lax.broadcasted_iota(jnp.int32, (q, q), 0)
    jj = lax.broadcasted_iota(jnp.int32, (q, q), 1)
    causal = ((jj - ii) * sgn) <= 0
    causal_t = ((ii - jj) * sgn) <= 0
    lane = lax.broadcasted_iota(jnp.int32, (1, rp), 1)
    row = lax.broadcasted_iota(jnp.int32, (rp, 1), 0)
    nt = (((1,), (1,)), ((), ()))
    tn = (((0,), (0,)), ((), ()))
    cb = lax.dot_general(cm.astype(BF16), bm.astype(BF16), nt, preferred_element_type=F32)
    dt_x = jnp.zeros((q, rp), F32)
    acum_x = jnp.zeros((q, rp), F32)
    tot_row = jnp.zeros((1, rp), F32)
    tot_col = jnp.zeros((rp, 1), F32)
    wts, lane_masks = [], []
    for r in range(R):
        hm = (lane >= r * P) & (lane < (r + 1) * P)
        hc = (row >= r * P) & (row < (r + 1) * P)
        dac = dtcs[r] * a_s[r]
        dar = dtrs[r] * a_s[r]
        acum_c = jnp.sum(jnp.where(causal, dar, 0.0), axis=1, keepdims=True)
        acum_r = jnp.sum(jnp.where(causal_t, dac, 0.0), axis=0, keepdims=True)
        decay = jnp.where(causal, jnp.exp(jnp.where(causal, acum_c - acum_r, 0.0)), 0.0)
        tot = jnp.sum(dac, axis=0, keepdims=True)
        dt_x = jnp.where(hm, dtcs[r], dt_x)
        acum_x = jnp.where(hm, acum_c, acum_x)
        tot_row = jnp.where(hm, tot, tot_row)
        tot_col = jnp.where(hc, tot, tot_col)
        wts.append((cb * decay).astype(BF16))
        lane_masks.append(hm)
    xdt = xg * dt_x
    xdt_b = xdt.astype(BF16)
    y = jnp.zeros((q, rp), F32)
    for r in range(R):
        y = jnp.where(lane_masks[r], jnp.dot(wts[r], xdt_b, preferred_element_type=F32), y)
    dte = jnp.exp(tot_row - acum_x)
    cs = lax.dot_general((xdt * dte).astype(BF16), bm.astype(BF16), tn, preferred_element_type=F32)
    y = y + lax.dot_general(cm.astype(BF16), s_in.astype(BF16), nt, preferred_element_type=F32) * jnp.exp(acum_x)
    s_out = jnp.exp(tot_col) * s_in + cs
    return y, s_out


def _ssd_maps(NC, ncc, reverse_steps):
    def chunk(d, s):
        if reverse_steps:
            s = NC - 1 - s
        return jnp.where(d == 0, s, jnp.where(s < ncc, ncc - 1 - s, NC - 1 - s + ncc))

    def lat_chunk(d, s):
        c = chunk(d, s) - ncc
        return jnp.where(c < 0, jnp.where(d == 0, 0, NC - ncc - 1), c)

    def step(s):
        return NC - 1 - s if reverse_steps else s

    return chunk, lat_chunk, step


def _ssd_specs(chunk, H, R, Q, N, RP, bo, co):
    return [
        pl.BlockSpec((Q, RP), lambda d, g, s: (chunk(d, s), g)),
        pl.BlockSpec((Q, N), lambda d, g, s: (chunk(d, s), bo + g)),
        pl.BlockSpec((Q, N), lambda d, g, s: (chunk(d, s), co + g)),
        pl.BlockSpec((R, Q, 1), lambda d, g, s: (d * (H // R) + g, chunk(d, s), 0)),
        pl.BlockSpec((R, 1, Q), lambda d, g, s: (d * (H // R) + g, 0, chunk(d, s))),
        pl.BlockSpec((R, 1, 1), lambda d, g, s: (d * (H // R) + g, 0, 0)),
    ]


def _ssd_fwd(xbc, b_off, c_off, dtc, dtr, a, H, P, ncc):
    T = xbc.shape[0]
    N, Q = SSD_STATE, SSD_CHUNK
    NC = T // Q
    G = (c_off - b_off) // N
    R = H // G
    RP = R * P
    chunk, lat_chunk, _ = _ssd_maps(NC, ncc, False)

    def body(x_ref, b_ref, c_ref, dtc_ref, dtr_ref, a_ref, y_ref, se_ref, s_ref):
        d, s = pl.program_id(0), pl.program_id(2)

        @pl.when(s == 0)
        def _():
            s_ref[...] = jnp.zeros_like(s_ref)

        s_in = s_ref[...]
        se_ref[...] = s_in
        per_head = ([dtc_ref[r] for r in range(R)] + [dtr_ref[r] for r in range(R)] + [a_ref[r] for r in range(R)])
        y, s_out = _ssd_group(x_ref[...], b_ref[...], c_ref[...], s_in, *per_head, sgn=1 - 2 * d, P=P)
        y_ref[...] = y
        s_ref[...] = s_out

    return _pcall(
        body, name="ssd_fwd", grid=(2, G, NC),
        in_specs=_ssd_specs(chunk, H, R, Q, N, RP, b_off // N, c_off // N),
        out_specs=[
            pl.BlockSpec((None, Q, RP), lambda d, g, s: (d, lat_chunk(d, s), g)),
            pl.BlockSpec((None, None, None, RP, N), lambda d, g, s: (d, g, s, 0, 0)),
        ],
        out_shape=[jax.ShapeDtypeStruct((2, T - ncc * Q, H * P), F32),
                   jax.ShapeDtypeStruct((2, G, NC, RP, N), F32)],
        scratch_shapes=[pltpu.VMEM((RP, N), F32)], compiler_params=_cparams(3),
    )(xbc, xbc, xbc, dtc, dtr, a)


def _ssd_bwd(xbc, b_off, c_off, dtc, dtr, a, s_enter, dy, H, P, ncc):
    T = xbc.shape[0]
    N, Q = SSD_STATE, SSD_CHUNK
    NC = T // Q
    G = (c_off - b_off) // N
    R = H // G
    RP = R * P
    chunk, lat_chunk, step = _ssd_maps(NC, ncc, True)

    def body(x_ref, b_ref, c_ref, dtc_ref, dtr_ref, a_ref, se_ref, dy_ref,
             dx_ref, db_ref, dc_ref, ddtc_ref, ddtr_ref, da_ref, ds_ref):
        d, s = pl.program_id(0), pl.program_id(2)

        @pl.when(s == 0)
        def _():
            ds_ref[...] = jnp.zeros_like(ds_ref)

        per_head = ([dtc_ref[r] for r in range(R)] + [dtr_ref[r] for r in range(R)] + [a_ref[r] for r in range(R)])
        f = functools.partial(_ssd_group, sgn=1 - 2 * d, P=P)
        _, vjp = jax.vjp(f, x_ref[...], b_ref[...], c_ref[...], se_ref[...], *per_head)
        is_latent = chunk(d, s) >= ncc
        dy_v = jnp.where(is_latent, dy_ref[...], 0.0)
        grads = vjp((dy_v, ds_ref[...]))
        dx_ref[...] = grads[0]
        db_ref[...] = grads[1]
        dc_ref[...] = grads[2]
        ds_ref[...] = grads[3]
        for r in range(R):
            ddtc_ref[r] = grads[4 + r]
            ddtr_ref[r] = grads[4 + R + r]
            da_ref[r] = jnp.broadcast_to(grads[4 + 2 * R + r], (SUBLANE, LANE))

    return _pcall(
        body, name="ssd_bwd", grid=(2, G, NC),
        in_specs=_ssd_specs(chunk, H, R, Q, N, RP, b_off // N, c_off // N) + [
            pl.BlockSpec((None, None, None, RP, N), lambda d, g, s: (d, g, step(s), 0, 0)),
            pl.BlockSpec((Q, RP), lambda d, g, s: (lat_chunk(d, s), g)),
        ],
        out_specs=[
            pl.BlockSpec((None, Q, RP), lambda d, g, s: (d, chunk(d, s), g)),
            pl.BlockSpec((None, Q, N), lambda d, g, s: (d, chunk(d, s), g)),
            pl.BlockSpec((None, Q, N), lambda d, g, s: (d, chunk(d, s), g)),
            pl.BlockSpec((R, Q, 1), lambda d, g, s: (d * G + g, chunk(d, s), 0)),
            pl.BlockSpec((R, 1, Q), lambda d, g, s: (d * G + g, 0, chunk(d, s))),
            pl.BlockSpec((R, SUBLANE, LANE), lambda d, g, s: ((d * G + g) * NC + s, 0, 0)),
        ],
        out_shape=[
            jax.ShapeDtypeStruct((2, T, H * P), F32), jax.ShapeDtypeStruct((2, T, G * N), F32),
            jax.ShapeDtypeStruct((2, T, G * N), F32), jax.ShapeDtypeStruct((2 * H, T, 1), F32),
            jax.ShapeDtypeStruct((2 * H, 1, T), F32), jax.ShapeDtypeStruct((2 * G * NC * R, SUBLANE, LANE), F32),
        ],
        scratch_shapes=[pltpu.VMEM((RP, N), F32)], compiler_params=_cparams(3),
    )(xbc, xbc, xbc, dtc, dtr, a, s_enter, dy)


def _allgather8(name, v):
    R, C = v.shape

    def body(x_ref, out_ref, send_sems, recv_sems, local_sem):
        x, y, c = lax.axis_index("x"), lax.axis_index("y"), lax.axis_index("c")
        me, sibling = (x, y, c), (x, y, 1 - c)
        chips = [(1 - x, y), (x, 1 - y), (1 - x, 1 - y)]

        def slot(px, py, pc):
            return out_ref.at[4 * px + 2 * py + pc]

        def copy(k, block, to, src=None):
            return pltpu.make_async_remote_copy(
                src_ref=slot(*block) if src is None else src, dst_ref=slot(*block),
                send_sem=send_sems.at[k], recv_sem=recv_sems.at[k], device_id=to, device_id_type=MESH)

        mine = pltpu.make_async_copy(x_ref, slot(*me), local_sem)
        mine.start()
        first = [copy(0, me, sibling, src=x_ref)]
        first += [copy(1 + j, me, (*chip, c), src=x_ref) for j, chip in enumerate(chips)]
        for cp in first:
            cp.start()
        passed = [copy(4 + j, (*chip, c), sibling) for j, chip in enumerate(chips)]
        for j, chip in enumerate(chips):
            copy(1 + j, (*chip, c), me).wait_recv()
            passed[j].start()
        copy(0, sibling, me).wait_recv()
        for j, chip in enumerate(chips):
            copy(4 + j, (*chip, 1 - c), me).wait_recv()
        for cp in first + passed:
            cp.wait_send()
        mine.wait()

    return _pcall(
        body, name=name, out_shape=jax.ShapeDtypeStruct((N_DEV, R, C), v.dtype),
        in_specs=[pl.BlockSpec(memory_space=pltpu.VMEM)], out_specs=pl.BlockSpec(memory_space=pltpu.VMEM),
        scratch_shapes=[pltpu.SemaphoreType.DMA((7,)), pltpu.SemaphoreType.DMA((7,)), pltpu.SemaphoreType.DMA],
        compiler_params=pltpu.CompilerParams(vmem_limit_bytes=VMEM_LIMIT_BYTES),
    )(v)


def _exchange4(name, srcs, bcast):
    n = len(srcs)
    out_shape = [jax.ShapeDtypeStruct(((N_CHIPS,) + s.shape) if bcast else s.shape, s.dtype) for s in srcs]

    def body(*refs):
        src, out = refs[:n], refs[n:2 * n]
        send_sems, recv_sems, local_sems = refs[2 * n:]
        x, y, c = lax.axis_index("x"), lax.axis_index("y"), lax.axis_index("c")
        me = 2 * x + y
        peers = [(1 - x, y), (x, 1 - y), (1 - x, 1 - y)]
        copies = []
        for a in range(n):
            local = pltpu.make_async_copy(src[a] if bcast else src[a].at[me], out[a].at[me], local_sems.at[a])
            local.start()
            copies.append(local)
            for j, (px, py) in enumerate(peers):
                rc = pltpu.make_async_remote_copy(
                    src_ref=src[a] if bcast else src[a].at[2 * px + py], dst_ref=out[a].at[me],
                    send_sem=send_sems.at[a, j], recv_sem=recv_sems.at[a, j], device_id=(px, py, c),
                    device_id_type=MESH)
                rc.start()
                copies.append(rc)
        for cp in copies:
            cp.wait()

    any_spec = pl.BlockSpec(memory_space=pl.ANY)
    return _pcall(
        body, name=name, out_shape=out_shape, in_specs=[any_spec] * n, out_specs=[any_spec] * n,
        scratch_shapes=[pltpu.SemaphoreType.DMA((n, 3)), pltpu.SemaphoreType.DMA((n, 3)),
                        pltpu.SemaphoreType.DMA((n,))],
    )(*srcs)


def _swap_sibling(name, srcs):
    n = len(srcs)

    def body(*refs):
        src, out = refs[:n], refs[n:2 * n]
        send_sems, recv_sems = refs[2 * n:]
        x, y, c = lax.axis_index("x"), lax.axis_index("y"), lax.axis_index("c")
        copies = []
        for a in range(n):
            rc = pltpu.make_async_remote_copy(
                src_ref=src[a], dst_ref=out[a], send_sem=send_sems.at[a], recv_sem=recv_sems.at[a],
                device_id=(x, y, 1 - c), device_id_type=MESH)
            rc.start()
            copies.append(rc)
        for cp in copies:
            cp.wait()

    any_spec = pl.BlockSpec(memory_space=pl.ANY)
    return _pcall(
        body, name=name, out_shape=[jax.ShapeDtypeStruct(s.shape, s.dtype) for s in srcs],
        in_specs=[any_spec] * n, out_specs=[any_spec] * n,
        scratch_shapes=[pltpu.SemaphoreType.DMA((n,)), pltpu.SemaphoreType.DMA((n,))],
    )(*srcs)


def _mod_fwd(c16, mod_w, mod_b_shard):
    nl, D, S = mod_w.shape

    def body(c_ref, w_ref, b_ref, o_ref):
        s = _silu(c_ref[...]).astype(BF16)
        o_ref[...] = jnp.dot(s, w_ref[...].astype(BF16), preferred_element_type=F32) + b_ref[...]

    return _pcall(
        body, name="mod_fwd", grid=(nl,),
        in_specs=[pl.BlockSpec((16, D), lambda l: (0, 0)), pl.BlockSpec((None, D, S), lambda l: (l, 0, 0)),
                  pl.BlockSpec((None, 1, S), lambda l: (l, 0, 0))],
        out_specs=pl.BlockSpec((None, 16, S), lambda l: (l, 0, 0)),
        out_shape=jax.ShapeDtypeStruct((nl, 16, S), F32), compiler_params=_cparams(1),
    )(c16, mod_w, mod_b_shard)


def _mod_w_update(s16t, dm16, w, m, v):
    nl, D, S = w.shape
    tm = _row_tile(D, 256)

    def body(s_ref, dm_ref, w_ref, m_ref, v_ref, g_ref, dl_ref, nm_ref, nv_ref):
        g = jnp.dot(s_ref[...], dm_ref[...], preferred_element_type=F32, precision=HIGHEST)
        g, dl, nm, nv = _f_adamw(w_ref[...], m_ref[...], v_ref[...], g, jnp.zeros_like(g))
        g_ref[...] = g
        dl_ref[...] = dl
        nm_ref[...] = nm
        nv_ref[...] = nv

    big = pl.BlockSpec((None, tm, S), lambda l, i: (l, i, 0))
    return _pcall(
        body, name="mod_w_update", grid=(nl, D // tm),
        in_specs=[pl.BlockSpec((tm, 16), lambda l, i: (i, 0)), pl.BlockSpec((None, 16, S), lambda l, i: (l, 0, 0)),
                  big, big, big],
        out_specs=[big] * 4, out_shape=[jax.ShapeDtypeStruct(w.shape, F32)] * 4, compiler_params=_cparams(2),
    )(s16t, dm16, w, m, v)


def _pack(arrs):
    flat = jnp.concatenate([a.reshape(-1).astype(F32) for a in arrs])
    n = flat.shape[0]
    rows = _round_up(_cdiv(n, LANE), SUBLANE)
    return jnp.pad(flat, (0, rows * LANE - n)).reshape(rows, LANE)


def _unpack(buf, shapes):
    flat = buf.reshape(-1)
    out, pos = [], 0
    for s in shapes:
        n = 1
        for d in s:
            n *= d
        out.append(flat[pos:pos + n].reshape(s))
        pos += n
    return out


SHARD_AXIS = {
    "mod_w": 2, "ssd_w_in": 2, "ssd_conv_w": 2, "ssd_w_out": 1, "conf_w_pw1": 2, "conf_b_pw1": 1, "conf_w_dw": 2,
    "conf_b_dw": 1, "conf_ln_w": 1, "conf_ln_b": 1, "conf_w_pw2": 1, "conf_b_pw2": 1, "ffn_w_up": 2,
    "ffn_conv_w": 3, "ffn_w_down": 1,
}
BIG = ("ssd_w_in", "ssd_w_out", "conf_w_pw1", "conf_w_pw2", "ffn_w_up", "ffn_w_down")
WEIGHTS = ("c_ctx", "mod_w", "mod_b", "norm1_w", "norm2_w", "ssd_w_in", "ssd_conv_w", "ssd_conv_b", "ssd_dt_bias",
           "ssd_a_log", "ssd_d", "ssd_norm_w", "ssd_w_out", "conf_w_pw1", "conf_b_pw1", "conf_w_dw", "conf_b_dw",
           "conf_ln_w", "conf_ln_b", "conf_w_pw2", "conf_b_pw2", "ffn_w_up", "ffn_conv_w", "ffn_conv_b",
           "ffn_w_down", "final_norm_w")
SMALL = tuple(n for n in WEIGHTS if n not in BIG and n != "mod_w")
SMALL_SHARDED = tuple(n for n in SMALL if n in SHARD_AXIS)


def _unshard(stacked, axis):
    return jnp.concatenate([stacked[k] for k in range(N_CHIPS)], axis=axis)


def _to_blocks(full, axis):
    return jnp.stack(jnp.split(full, N_CHIPS, axis=axis))


def _par(v):
    v = v.reshape(-1, v.shape[-1])
    return v[:, None, :]


def kernel(x, c, ctx, c_ctx, mod_w, mod_b, norm1_w, norm2_w, ssd_w_in, ssd_conv_w, ssd_conv_b, ssd_dt_bias, ssd_a_log, ssd_d, ssd_norm_w, ssd_w_out, conf_w_pw1, conf_b_pw1, conf_w_dw, conf_b_dw, conf_ln_w, conf_ln_b, conf_w_pw2, conf_b_pw2, ffn_w_up, ffn_conv_w, ffn_conv_b, ffn_w_down, final_norm_w, loss_target, m_c_ctx, m_mod_w, m_mod_b, m_norm1_w, m_norm2_w, m_ssd_w_in, m_ssd_conv_w, m_ssd_conv_b, m_ssd_dt_bias, m_ssd_a_log, m_ssd_d, m_ssd_norm_w, m_ssd_w_out, m_conf_w_pw1, m_conf_b_pw1, m_conf_w_dw, m_conf_b_dw, m_conf_ln_w, m_conf_ln_b, m_conf_w_pw2, m_conf_b_pw2, m_ffn_w_up, m_ffn_conv_w, m_ffn_conv_b, m_ffn_w_down, m_final_norm_w, v_c_ctx, v_mod_w, v_mod_b, v_norm1_w, v_norm2_w, v_ssd_w_in, v_ssd_conv_w, v_ssd_conv_b, v_ssd_dt_bias, v_ssd_a_log, v_ssd_d, v_ssd_norm_w, v_ssd_w_out, v_conf_w_pw1, v_conf_b_pw1, v_conf_w_dw, v_conf_b_dw, v_conf_ln_w, v_conf_ln_b, v_conf_w_pw2, v_conf_b_pw2, v_ffn_w_up, v_ffn_conv_w, v_ffn_conv_b, v_ffn_w_down, v_final_norm_w):
    given = dict(locals())
    W = {n: given[n] for n in WEIGHTS}
    Mo = {n: given["m_" + n] for n in WEIGHTS}
    Vo = {n: given["v_" + n] for n in WEIGHTS}

    ax, ay, ac = lax.axis_index("x"), lax.axis_index("y"), lax.axis_index("c")
    chip = 2 * ax + ay
    dev = 4 * ax + 2 * ay + ac

    D = x.shape[-1]
    L, Lc = x.shape[1], ctx.shape[1]
    T0 = L + Lc
    H = ssd_a_log.shape[-1]
    DI = ssd_norm_w.shape[-1]
    P = DI // H
    CD = ssd_conv_b.shape[-1]
    N = SSD_STATE
    G = (CD - DI) // (2 * N)
    FH = ffn_conv_b.shape[-1]
    KS = ssd_conv_w.shape[1]
    KC = conf_w_dw.shape[1]
    ncc = Lc // SSD_CHUNK

    small_shard_shapes = [W[n].shape for n in SMALL_SHARDED]
    f1 = _allgather8("gather_small", _pack([c] + [W[n] for n in SMALL_SHARDED]))
    c_rows, full_small = [], {n: [] for n in SMALL_SHARDED}
    for k in range(N_DEV):
        parts = _unpack(f1[k], [c.shape] + small_shard_shapes)
        c_rows.append(parts[0])
        if k % 2 == 0:
            for n, p in zip(SMALL_SHARDED, parts[1:]):
                full_small[n].append(p)
    Wf = dict(W)
    for n in SMALL_SHARDED:
        Wf[n] = jnp.concatenate(full_small[n], axis=SHARD_AXIS[n])
    c16 = jnp.concatenate(c_rows + [c_ctx[None, :], jnp.zeros((16 - N_DEV - 1, D), F32)], axis=0)

    S_mod = mod_w.shape[-1]
    mod_b_shard = lax.dynamic_slice_in_dim(mod_b, chip * S_mod, S_mod, axis=1)[:, None, :]
    mod_part = _mod_fwd(c16, mod_w, mod_b_shard)
    f2 = _allgather8("gather_mod", mod_part.reshape(2 * 16, S_mod))
    mods = jnp.concatenate([f2[2 * k].reshape(2, 16, S_mod) for k in range(N_CHIPS)], axis=-1)
    my = lax.dynamic_slice_in_dim(mods, dev, 1, axis=1)[:, 0]
    sh1, sc1, g1, sh2, sc2, g2 = [[my[l, k * D:(k + 1) * D] for l in range(2)] for k in range(6)]
    csh1, csc1 = mods[0, N_DEV, 0:D], mods[0, N_DEV, D:2 * D]

    gathered = _exchange4("gather_weights", [W[n].astype(BF16) for n in BIG], True)
    Wb = {n: _unshard(g, SHARD_AXIS[n]) for n, g in zip(BIG, gathered)}
    w_in, w_out = Wb["ssd_w_in"][0], Wb["ssd_w_out"][0]
    w_pw1, w_pw2 = Wb["conf_w_pw1"][0], Wb["conf_w_pw2"][0]
    w_up, w_dn = Wb["ffn_w_up"], Wb["ffn_w_down"]

    xl = x[0]
    hcat = jnp.concatenate([ctx[0], xl], axis=0)
    n1w0, n2w0, n1w1, n2w1 = _par(norm1_w[0]), _par(norm2_w[0]), _par(norm1_w[1]), _par(norm2_w[1])
    sc_seg = jnp.stack([csc1, sc1[0]])[:, None, :]
    sh_seg = jnp.stack([csh1, sh1[0]])[:, None, :]

    a0 = _rw_fwd("l0_modnorm1", _f_modnorm, [hcat], [n1w0, sc_seg, sh_seg], [D], seg_rows=(Lc,))
    proj = _mm(a0, w_in, name="l0_w_in")
    seg_taps = [(k - KS // 2, ("seg", Lc)) for k in range(KS)]
    xbc_pre, xbc = _conv_fwd("l0_conv", proj, DI, CD, Wf["ssd_conv_w"][0], ssd_conv_b, seg_taps, act=True)
    dt_raw = proj[:, DI + CD:]
    dt_bias = _par(ssd_dt_bias.reshape(1, 2 * H))
    dt = _rw_fwd("l0_softplus", _f_softplus, [dt_raw], [dt_bias], [2 * H])
    dt_t = dt.T
    dtc, dtr = dt_t[:, :, None], dt_t[:, None, :]
    a_neg = -jnp.exp(ssd_a_log.reshape(2 * H, 1, 1))
    y2, s_enter = _ssd_fwd(xbc, DI, DI + G * N, dtc, dtr, a_neg, H, P, ncc)
    y2 = y2.reshape(2 * L, DI)
    gate_rows = [(y2, 0, DI, 0), (y2, 0, DI, L), (xbc, 0, DI, Lc), (proj, 0, DI, Lc)]
    d_rep = _par(jnp.repeat(ssd_d[0], P))
    ssd_nw = _par(ssd_norm_w[0])
    yn = _rw_fwd("l0_ssd_gate", _f_ssd_gate, gate_rows, [d_rep, ssd_nw], [DI], T=L)
    mix0 = _mm(yn, w_out, name="l0_w_out")
    g1_0, g2_0, g1_1, g2_1 = _par(g1[0]), _par(g2[0]), _par(g1[1]), _par(g2[1])
    h1 = _rw_fwd("l0_res1", _f_gate_res, [xl, mix0], [g1_0], [D])

    grid_taps = [((i - 1) * GRID_W + (j - 1), (None if j == 1 else ("col", j - 1))) for i in range(3) for j in range(3)]

    def ffn_fwd(l, h, tag):
        a = _rw_fwd(tag + "_modnorm2", _f_modnorm, [h], [_par(norm2_w[l]), _par(sc2[l]), _par(sh2[l])], [D])
        hh = _mm(a, w_up[l], name=tag + "_w_up")
        gc = _conv_fwd(tag + "_ffn_conv", hh, FH, FH, Wf["ffn_conv_w"][l].reshape(9, FH), ffn_conv_b[l][None, :],
                       grid_taps)
        act = _rw_fwd(tag + "_act", _f_ffn_act, [(hh, 0, FH), gc], [], [FH], col_tile=_tile(FH, 1536))
        dn = _mm(act, w_dn[l], name=tag + "_w_down")
        return a, hh, gc, act, dn

    a1, hh0, gc0, act0, dn0 = ffn_fwd(0, h1, "l0")
    h2 = _rw_fwd("l0_res2", _f_gate_res, [h1, dn0], [g2_0], [D])

    a2 = _rw_fwd("l1_modnorm1", _f_modnorm, [h2], [n1w1, _par(sc1[1]), _par(sh1[1])], [D])
    pw = _mm(a2, w_pw1, name="l1_pw1")
    b_pw1 = Wf["conf_b_pw1"][0]
    glu = _rw_fwd("l1_glu", _f_glu, [(pw, 0, D), (pw, D, D)], [_par(b_pw1[:D]), _par(b_pw1[D:])], [D])
    conf_taps = [(k - KC // 2, None) for k in range(KC)]
    cv = _conv_fwd("l1_conv", glu, 0, D, Wf["conf_w_dw"][0], Wf["conf_b_dw"], conf_taps)
    ln_w, ln_b = _par(Wf["conf_ln_w"][0]), _par(Wf["conf_ln_b"][0])
    ls = _rw_fwd("l1_ln_silu", _f_ln_silu, [cv], [ln_w, ln_b], [D])
    p2 = _mm(ls, w_pw2, name="l1_pw2")
    b_pw2 = _par(Wf["conf_b_pw2"][0])
    h3 = _rw_fwd("l1_res1", _f_gate_res_bias, [h2, p2], [g1_1, b_pw2], [D])
    a3, hh1, gc1, act1, dn1 = ffn_fwd(1, h3, "l1")
    h4 = _rw_fwd("l1_res2", _f_gate_res, [h3, dn1], [g2_1], [D])

    fnw = final_norm_w[None, :]
    tgt = loss_target[0]
    loss_local = _loss_fwd(h4, tgt, fnw)[0, 0]
    loss = lax.psum(loss_local, ("x", "y", "c"))

    G_full = {}
    ones = jnp.ones((L, 1), F32)
    (dh4,), (dfnw,) = _rw_bwd("loss_bwd", _f_loss_rows, [h4, tgt], [_par(final_norm_w)], [ones],
                              row_grad=[True, False], par_grad=[True])
    G_full["final_norm_w"] = dfnw.reshape(D)

    def ffn_bwd(l, h, saved, g2_l, dh_out, tag):
        a, hh, gc, act, dn = saved
        (ddn,), (dg2,) = _rw_bwd(tag + "_res2_bwd", _f_gate_res, [h, dn], [g2_l], [dh_out],
                                 row_grad=[False, True], par_grad=[True])
        dact = _mm(ddn, w_dn[l], tb=True, name=tag + "_w_down_dx")
        dwdn = _mm(act, ddn, ta=True, name=tag + "_w_down_dw")
        (dval, dgc), _ = _rw_bwd(tag + "_act_bwd", _f_ffn_act, [(hh, 0, FH), gc], [], [dact],
                                 row_grad=[True, True], par_grad=[], col_tile=_tile(FH, 1536))
        dgin, dcw, dcb = _conv_bwd(tag + "_ffn_conv_bwd", hh, FH, FH, Wf["ffn_conv_w"][l].reshape(9, FH), dgc, grid_taps)
        dhh = jnp.concatenate([dval, dgin], axis=1)
        da = _mm(dhh, w_up[l], tb=True, name=tag + "_w_up_dx")
        dwup = _mm(a, dhh, ta=True, name=tag + "_w_up_dw")
        (dh,), (dn2w, dsc2, dsh2) = _rw_bwd(
            tag + "_modnorm2_bwd", _f_modnorm, [h], [_par(norm2_w[l]), _par(sc2[l]), _par(sh2[l])], [da],
            row_grad=[True], par_grad=[True, True, True], add=dh_out)
        return dh, dict(w_down=dwdn, w_up=dwup, conv_w=dcw.reshape(3, 3, FH), conv_b=dcb.reshape(FH),
                        n2w=dn2w.reshape(D), sc2=dsc2.reshape(D), sh2=dsh2.reshape(D), g2=dg2.reshape(D))

    dh3, gf1 = ffn_bwd(1, h3, (a3, hh1, gc1, act1, dn1), g2_1, dh4, "l1")
    (dp2,), (dg1_1, db_pw2) = _rw_bwd("l1_res1_bwd", _f_gate_res_bias, [h2, p2], [g1_1, b_pw2], [dh3],
                                      row_grad=[False, True], par_grad=[True, True])
    dls = _mm(dp2, w_pw2, tb=True, name="l1_pw2_dx")
    G_full["conf_w_pw2"] = _mm(ls, dp2, ta=True, name="l1_pw2_dw")[None]
    (dcv,), (dln_w, dln_b) = _rw_bwd("l1_ln_silu_bwd", _f_ln_silu, [cv], [ln_w, ln_b], [dls],
                                     row_grad=[True], par_grad=[True, True])
    dglu, dw_dw, db_dw = _conv_bwd("l1_conv_bwd", glu, 0, D, Wf["conf_w_dw"][0], dcv, conf_taps)
    (dpa, dpg), (dba, dbg) = _rw_bwd("l1_glu_bwd", _f_glu, [(pw, 0, D), (pw, D, D)],
                                     [_par(b_pw1[:D]), _par(b_pw1[D:])], [dglu],
                                     row_grad=[True, True], par_grad=[True, True])
    dpw = jnp.concatenate([dpa, dpg], axis=1)
    da2 = _mm(dpw, w_pw1, tb=True, name="l1_pw1_dx")
    G_full["conf_w_pw1"] = _mm(a2, dpw, ta=True, name="l1_pw1_dw")[None]
    (dh2,), (dn1w1, dsc1_1, dsh1_1) = _rw_bwd(
        "l1_modnorm1_bwd", _f_modnorm, [h2], [n1w1, _par(sc1[1]), _par(sh1[1])], [da2],
        row_grad=[True], par_grad=[True, True, True], add=dh3)
    G_full["conf_b_pw2"] = db_pw2.reshape(1, D)
    G_full["conf_ln_w"], G_full["conf_ln_b"] = dln_w.reshape(1, D), dln_b.reshape(1, D)
    G_full["conf_w_dw"], G_full["conf_b_dw"] = dw_dw[None], db_dw.reshape(1, D)
    G_full["conf_b_pw1"] = jnp.concatenate([dba.reshape(1, D), dbg.reshape(1, D)], axis=1)

    dh1, gf0 = ffn_bwd(0, h1, (a1, hh0, gc0, act0, dn0), g2_0, dh2, "l0")
    G_full["ffn_w_up"] = jnp.stack([gf0["w_up"], gf1["w_up"]])
    G_full["ffn_w_down"] = jnp.stack([gf0["w_down"], gf1["w_down"]])
    G_full["ffn_conv_w"] = jnp.stack([gf0["conv_w"], gf1["conv_w"]])
    G_full["ffn_conv_b"] = jnp.stack([gf0["conv_b"], gf1["conv_b"]])

    (dmix,), (dg1_0,) = _rw_bwd("l0_res1_bwd", _f_gate_res, [xl, mix0], [g1_0], [dh1],
                                row_grad=[False, True], par_grad=[True])
    dyn = _mm(dmix, w_out, tb=True, name="l0_w_out_dx")
    G_full["ssd_w_out"] = _mm(yn, dmix, ta=True, name="l0_w_out_dw")[None]
    (dy_lat, dxs_gate, dz_lat), (dd_rep, dssd_nw) = _rw_bwd(
        "l0_ssd_gate_bwd", _f_ssd_gate, gate_rows, [d_rep, ssd_nw], [dyn],
        row_grad=[True, False, True, True], par_grad=[True, True], T=L)
    dx2, dB, dC, ddtc, ddtr, da_parts = _ssd_bwd(xbc, DI, DI + G * N, dtc, dtr, a_neg, s_enter, dy_lat, H, P, ncc)
    dxs_gate_all = jnp.pad(dxs_gate, ((Lc, 0), (0, 0)))
    dx2, dB, dC = dx2.reshape(2 * T0, DI), dB.reshape(2 * T0, G * N), dC.reshape(2 * T0, G * N)
    silu_bwd = functools.partial(_rw_bwd, f=_silu, pars=[], row_grad=[True], par_grad=[], T=T0)
    (dxs_pre,), _ = silu_bwd("l0_silu_bwd_x", rows=[(xbc_pre, 0, DI)], cot_fn=lambda p, q, r: p + q + r,
                             cots=[(dx2, 0, DI, 0), (dx2, 0, DI, T0), dxs_gate_all], col_tile=_tile(DI, 1024))
    (db_pre,), _ = silu_bwd("l0_silu_bwd_b", rows=[(xbc_pre, DI, G * N)], cot_fn=lambda p, q: p + q,
                            cots=[(dB, 0, G * N, 0), (dB, 0, G * N, T0)], col_tile=_tile(G * N, 1024))
    (dc_pre,), _ = silu_bwd("l0_silu_bwd_c", rows=[(xbc_pre, DI + G * N, G * N)], cot_fn=lambda p, q: p + q,
                            cots=[(dC, 0, G * N, 0), (dC, 0, G * N, T0)], col_tile=_tile(G * N, 1024))
    dxbc_pre = jnp.concatenate([dxs_pre, db_pre, dc_pre], axis=1)
    dconv_in, dcw0, dcb0 = _conv_bwd("l0_conv_bwd", proj, DI, CD, Wf["ssd_conv_w"][0], dxbc_pre, seg_taps)
    ddt = ddtc[:, :, 0].T + ddtr[:, 0, :].T
    (ddt_raw,), (ddt_bias,) = _rw_bwd("l0_softplus_bwd", _f_softplus, [dt_raw], [dt_bias], [ddt],
                                      row_grad=[True], par_grad=[True])
    dproj = jnp.concatenate([jnp.pad(dz_lat, ((Lc, 0), (0, 0))), dconv_in, ddt_raw], axis=1)
    da0 = _mm(dproj, w_in, tb=True, name="l0_w_in_dx")
    G_full["ssd_w_in"] = _mm(a0, dproj, ta=True, name="l0_w_in_dw")[None]
    (dhcat,), (dn1w0, dsc_seg, dsh_seg) = _rw_bwd(
        "l0_modnorm1_bwd", _f_modnorm, [hcat], [n1w0, sc_seg, sh_seg], [da0],
        row_grad=[True], par_grad=[True, True, True], seg_rows=(Lc,))
    grad_x = (dhcat[Lc:] + dh1)[None]

    da_heads = da_parts[:, 0, 0].reshape(2, G, T0 // SSD_CHUNK, H // G).sum(axis=2).reshape(1, 2, H)
    G_full["ssd_a_log"] = da_heads * (-jnp.exp(ssd_a_log))
    G_full["ssd_dt_bias"] = ddt_bias.reshape(1, 2, H)
    G_full["ssd_d"] = dd_rep.reshape(H, P).sum(axis=1)[None]
    G_full["ssd_norm_w"] = dssd_nw.reshape(1, DI)
    G_full["ssd_conv_w"], G_full["ssd_conv_b"] = dcw0[None], dcb0.reshape(1, CD)
    G_full["norm1_w"] = jnp.stack([dn1w0.reshape(D), dn1w1.reshape(D)])
    G_full["norm2_w"] = jnp.stack([gf0["n2w"], gf1["n2w"]])

    zD = jnp.zeros((D,), F32)
    dm_own = jnp.stack([
        jnp.concatenate([dsh_seg[1, 0], dsc_seg[1, 0], dg1_0.reshape(D), gf0["sh2"], gf0["sc2"], gf0["g2"]]),
        jnp.concatenate([dsh1_1.reshape(D), dsc1_1.reshape(D), dg1_1.reshape(D), gf1["sh2"], gf1["sc2"], gf1["g2"]]),
    ])
    dmc_own = jnp.concatenate([dsh_seg[0, 0], dsc_seg[0, 0], zD, zD, zD, zD])

    small_sum_names = [n for n in SMALL if n not in ("c_ctx", "mod_b")]
    sum_part = [G_full[n] for n in small_sum_names] + [dmc_own]
    n_sum = sum(int(a.size) for a in sum_part)
    packed = _pack(sum_part + [dm_own])
    gat = _allgather8("gather_small_grads", packed)
    total = _sum_leading("sum_small_grads", gat, tuple(range(N_DEV)))
    summed = _unpack(total, [a.shape for a in sum_part])
    Gs = dict(zip(small_sum_names, summed[:-1]))
    dmc_tot = summed[-1]
    dm_all = jnp.stack([gat[k].reshape(-1)[n_sum:n_sum + 2 * 6 * D].reshape(2, 6 * D) for k in range(N_DEV)], axis=1)
    dm16 = jnp.concatenate([dm_all, jnp.stack([dmc_tot, jnp.zeros_like(dmc_tot)])[:, None, :],
                            jnp.zeros((2, 16 - N_DEV - 1, 6 * D), F32)], axis=1)
    Gs["mod_b"] = _sum_leading("sum_mod_b", dm16.transpose(1, 0, 2).reshape(16, 2 * 6 * D // LANE, LANE),
                               tuple(range(N_DEV + 1))).reshape(2, 6 * D)

    dm16_shard = lax.dynamic_slice_in_dim(dm16, chip * S_mod, S_mod, axis=2)
    ds16 = _mm(dm16_shard[0], mod_w[0], tb=True, precision=HIGHEST, name="c_ctx_dx")
    sig = jax.nn.sigmoid(c_ctx)
    dcc_part = ds16[N_DEV] * (sig * (1.0 + c_ctx * (1.0 - sig)))
    gat_cc = _allgather8("gather_c_ctx_grad", _pack([dcc_part]))
    Gs["c_ctx"] = _sum_leading("sum_c_ctx_grad", gat_cc, (0, 2, 4, 6)).reshape(-1)[:D]

    s16t = _silu(c16).T
    out = {}
    out["mod_w"] = _mod_w_update(s16t, dm16_shard, mod_w, m_mod_w, v_mod_w)

    blocks = [_to_blocks(G_full[n], SHARD_AXIS[n]).astype(BF16) for n in BIG]
    landed = _exchange4("reduce_grads", blocks, False)
    partial = []
    for n, blk in zip(BIG, landed):
        r = blk.reshape(N_CHIPS, -1, blk.shape[-1])
        partial.append(_sum_leading("sum4_" + n, r, (0, 1, 2, 3)).reshape(W[n].shape))
    sibling = _swap_sibling("swap_grads", partial)
    for n, mine, sib in zip(BIG, partial, sibling):
        out[n] = _adamw("adamw_" + n, W[n], Mo[n], Vo[n], mine, sib)

    def own(n, full):
        if n in SHARD_AXIS:
            size = W[n].shape[SHARD_AXIS[n]]
            return lax.dynamic_slice_in_dim(full, chip * size, size, axis=SHARD_AXIS[n])
        return full

    g_small = [own(n, Gs[n].reshape(Wf[n].shape)) for n in SMALL]
    shapes = [W[n].shape for n in SMALL]
    pk = [_pack([W[n] for n in SMALL]), _pack([Mo[n] for n in SMALL]), _pack([Vo[n] for n in SMALL]), _pack(g_small)]
    res = _adamw("adamw_small", pk[0], pk[1], pk[2], pk[3], jnp.zeros_like(pk[3]))
    unpacked = [_unpack(r, shapes) for r in res]
    for k, n in enumerate(SMALL):
        out[n] = tuple(u[k] for u in unpacked)

    grads = [out[n][0] for n in WEIGHTS]
    deltas = [out[n][1] for n in WEIGHTS]
    new_m = [out[n][2] for n in WEIGHTS]
    new_v = [out[n][3] for n in WEIGHTS]
    return (loss, grad_x, *grads, *deltas, *new_m, *new_v)
```

```python
import functools

import jax
import jax.numpy as jnp
from jax import lax
from jax.experimental import pallas as pl
from jax.experimental.pallas import tpu as pltpu

F32 = jnp.float32
BF16 = jnp.bfloat16
MESH = pl.DeviceIdType.MESH
HIGHEST = lax.Precision.HIGHEST

VMEM_LIMIT_BYTES = 48 * 1024 * 1024
LANE = 128
SUBLANE = 8

SSD_STATE = 128
SSD_CHUNK = 128
GRID_W = 64
EPS = 1e-6
N_CHIPS = 4
N_DEV = 8

ADAM_LR = 0.001
ADAM_B1 = 0.9
ADAM_B2 = 0.999
ADAM_EPS = 1e-08
ADAM_WD = 0.01
ADAM_STEP = 10


def _pcall(body, **kw):
    return pl.pallas_call(body, **kw)


def _cparams(n_grid):
    return pltpu.CompilerParams(dimension_semantics=("arbitrary",) * n_grid, vmem_limit_bytes=VMEM_LIMIT_BYTES)


def _cdiv(a, b):
    return -(-a // b)


def _round_up(a, b):
    return _cdiv(a, b) * b


def _tile(n, cap):
    if n <= cap:
        return n
    best = None
    for t in range(LANE, cap + 1, LANE):
        if n % t == 0:
            best = t
    if best is None:
        npad = _round_up(n, LANE)
        for t in range(LANE, cap + 1, LANE):
            if npad % t == 0:
                best = t
    return best


def _row_tile(n, cap, also=()):
    best = None
    for t in range(SUBLANE, min(cap, n) + 1, SUBLANE):
        if n % t == 0 and all(a % t == 0 for a in also):
            best = t
    assert best is not None, (n, cap, also)
    return best


def _silu(v):
    return v * jax.nn.sigmoid(v)


def _mm(a, b, *, name, ta=False, tb=False, precision=None, cap=1024):
    M, K = (a.shape[1], a.shape[0]) if ta else a.shape
    N = b.shape[0] if tb else b.shape[1]
    assert K == (b.shape[1] if tb else b.shape[0]), (a.shape, b.shape, ta, tb)
    tm, tn, tk = _tile(M, cap), _tile(N, cap), _tile(K, cap)
    nm, nn, nk = _cdiv(M, tm), _cdiv(N, tn), _cdiv(K, tk)
    k_tail = K % tk
    exact = precision is not None

    def body(a_ref, b_ref, o_ref, acc_ref):
        k = pl.program_id(2)

        @pl.when(k == 0)
        def _():
            acc_ref[...] = jnp.zeros_like(acc_ref)

        av = a_ref[...]
        bv = b_ref[...]
        if k_tail:
            lim = K - k * tk
            ka = lax.broadcasted_iota(jnp.int32, av.shape, 0 if ta else 1)
            kb = lax.broadcasted_iota(jnp.int32, bv.shape, 1 if tb else 0)
            av = jnp.where(ka < lim, av, jnp.zeros_like(av))
            bv = jnp.where(kb < lim, bv, jnp.zeros_like(bv))
        if exact:
            av = av.astype(F32)
            bv = bv.astype(F32)
        else:
            av = av.astype(BF16)
            bv = bv.astype(BF16)
        dn = (((0 if ta else 1,), (1 if tb else 0,)), ((), ()))
        acc_ref[...] += lax.dot_general(av, bv, dn, preferred_element_type=F32, precision=precision)

        @pl.when(k == nk - 1)
        def _():
            o_ref[...] = acc_ref[...]

    a_spec = pl.BlockSpec((tk, tm), lambda i, j, k: (k, i)) if ta else pl.BlockSpec((tm, tk), lambda i, j, k: (i, k))
    b_spec = pl.BlockSpec((tn, tk), lambda i, j, k: (j, k)) if tb else pl.BlockSpec((tk, tn), lambda i, j, k: (k, j))
    return _pcall(
        body, name=name, grid=(nm, nn, nk), in_specs=[a_spec, b_spec],
        out_specs=pl.BlockSpec((tm, tn), lambda i, j, k: (i, j)),
        out_shape=jax.ShapeDtypeStruct((M, N), F32),
        scratch_shapes=[pltpu.VMEM((tm, tn), F32)], compiler_params=_cparams(3),
    )(a, b)


def _norm_rows(rows):
    out = []
    for r in rows:
        if not isinstance(r, tuple):
            r = (r,)
        arr, off, width, roff = (r + (0, None, 0)[len(r) - 1:])
        out.append((arr, off, width if width is not None else arr.shape[1], roff))
    return out


def _rw_plan(T, rows, pars, seg_rows, col_tile, tm_cap):
    widths = [r[2] for r in rows]
    wmax = max(widths + [p.shape[-1] for p in pars] + [1])
    if col_tile is not None:
        assert all(w == widths[0] for w in widths) and all(p.shape[-1] == widths[0] for p in pars)
        ncol = widths[0] // col_tile
        assert ncol * col_tile == widths[0]
        wmax = col_tile
    else:
        ncol = 1
    cap = tm_cap if tm_cap is not None else max(SUBLANE, min(256, (256 * 1024) // wmax))
    tm = _row_tile(T, cap, also=tuple(seg_rows) + tuple(r[3] for r in rows if r[3]))
    bounds = tuple(s // tm for s in seg_rows)
    return widths, ncol, tm, bounds


def _rw_specs(rows, pars, ncol, tm, bounds, col_tile):
    def seg(i):
        s = 0
        for b in bounds:
            s = s + (i >= b).astype(jnp.int32)
        return s

    specs = []
    for arr, off, w, roff in rows:
        bw = col_tile if col_tile is not None else w
        assert off % bw == 0 and roff % tm == 0, (off, bw, roff, tm)
        specs.append(pl.BlockSpec((tm, bw), functools.partial(lambda j, i, ob, rb: (i + rb, ob + j),
                                                              ob=off // bw, rb=roff // tm)))
    for p in pars:
        bw = col_tile if col_tile is not None else p.shape[-1]
        if p.shape[0] > 1:
            specs.append(pl.BlockSpec((None, 1, bw), lambda j, i: (seg(i), 0, j)))
        else:
            specs.append(pl.BlockSpec((None, 1, bw), lambda j, i: (0, 0, j)))
    return specs, seg


def _rw_fwd(name, f, rows, pars, out_widths, *, T=None, seg_rows=(), col_tile=None, tm_cap=None):
    rows = _norm_rows(rows)
    T = rows[0][0].shape[0] if T is None else T
    widths, ncol, tm, bounds = _rw_plan(T, rows, pars, seg_rows, col_tile, tm_cap)
    in_specs, _ = _rw_specs(rows, pars, ncol, tm, bounds, col_tile)
    nr, npar, nout = len(rows), len(pars), len(out_widths)

    def body(*refs):
        vals = [r[...] for r in refs[:nr + npar]]
        outs = f(*vals)
        if not isinstance(outs, (tuple, list)):
            outs = (outs,)
        for o_ref, o in zip(refs[nr + npar:], outs):
            o_ref[...] = o.astype(o_ref.dtype)

    out_specs = [pl.BlockSpec((tm, col_tile if col_tile is not None else w), lambda j, i: (i, j)) for w in out_widths]
    res = _pcall(
        body, name=name, grid=(ncol, T // tm), in_specs=in_specs, out_specs=out_specs,
        out_shape=[jax.ShapeDtypeStruct((T, w), F32) for w in out_widths], compiler_params=_cparams(2),
    )(*[r[0] for r in rows], *pars)
    return res if nout > 1 else res[0]


def _rw_bwd(name, f, rows, pars, cots, *, row_grad, par_grad, T=None, seg_rows=(), col_tile=None, tm_cap=None,
            add=None, cot_fn=None):
    rows = _norm_rows(rows)
    cots = _norm_rows(cots)
    T = rows[0][0].shape[0] if T is None else T
    extra = _norm_rows([add]) if add is not None else []
    all_rows = rows + cots + extra
    widths, ncol, tm, bounds = _rw_plan(T, all_rows, pars, seg_rows, col_tile, tm_cap)
    in_specs, seg = _rw_specs(all_rows, pars, ncol, tm, bounds, col_tile)
    nr, nc, ne, npar = len(rows), len(cots), len(extra), len(pars)
    row_idx = [k for k in range(nr) if row_grad[k]]
    par_idx = [k for k in range(npar) if par_grad[k]]

    def body(*refs):
        i = pl.program_id(1)
        row_vals = [r[...] for r in refs[:nr]]
        cot_vals = [r[...] for r in refs[nr:nr + nc]]
        add_vals = [r[...] for r in refs[nr + nc:nr + nc + ne]]
        par_vals = [r[...] for r in refs[nr + nc + ne:nr + nc + ne + npar]]
        out_refs = refs[nr + nc + ne + npar:]
        outs, vjp = jax.vjp(f, *row_vals, *par_vals)
        if cot_fn is not None:
            cot_vals = cot_fn(*cot_vals)
            if not isinstance(cot_vals, (tuple, list)):
                cot_vals = (cot_vals,)
        if isinstance(outs, (tuple, list)):
            grads = vjp(tuple(c.astype(o.dtype) for c, o in zip(cot_vals, outs)))
        else:
            grads = vjp(cot_vals[0].astype(outs.dtype))
        first_seg = i == 0
        for b in bounds:
            first_seg = first_seg | (i == b)
        for n, k in enumerate(row_idx):
            g = grads[k]
            if n == 0 and add_vals:
                g = g + add_vals[0]
            out_refs[n][...] = g
        for n, k in enumerate(par_idx):
            g = grads[nr + k]
            o_ref = out_refs[len(row_idx) + n]
            first = first_seg if pars[k].shape[0] > 1 else (i == 0)

            @pl.when(first)
            def _(o_ref=o_ref, g=g):
                o_ref[...] = g

            @pl.when(jnp.logical_not(first))
            def _(o_ref=o_ref, g=g):
                o_ref[...] += g

    out_specs, out_shape = [], []
    for k in row_idx:
        w = widths[k]
        out_specs.append(pl.BlockSpec((tm, col_tile if col_tile is not None else w), lambda j, i: (i, j)))
        out_shape.append(jax.ShapeDtypeStruct((T, w), F32))
    for k in par_idx:
        p = pars[k]
        bw = col_tile if col_tile is not None else p.shape[-1]
        if p.shape[0] > 1:
            out_specs.append(pl.BlockSpec((None, 1, bw), lambda j, i: (seg(i), 0, j)))
        else:
            out_specs.append(pl.BlockSpec((None, 1, bw), lambda j, i: (0, 0, j)))
        out_shape.append(jax.ShapeDtypeStruct(p.shape, F32))
    res = _pcall(
        body, name=name, grid=(ncol, T // tm), in_specs=in_specs, out_specs=out_specs, out_shape=out_shape,
        compiler_params=_cparams(2),
    )(*[r[0] for r in all_rows], *pars)
    return list(res[:len(row_idx)]), list(res[len(row_idx):])


def _f_modnorm(h, w, sc, sh):
    y = h * lax.rsqrt(jnp.mean(h * h, axis=-1, keepdims=True) + EPS)
    return (y * w) * (1.0 + sc) + sh


def _f_gate_res(h, y, g):
    return h + g * y


def _f_gate_res_bias(h, y, g, b):
    return h + g * (y + b)


def _f_ffn_act(val, gate):
    return _silu(gate) * val


def _f_softplus(raw, bias):
    v = raw + bias
    return jnp.maximum(v, 0.0) + jnp.log(1.0 + jnp.exp(-jnp.abs(v)))


def _f_ssd_gate(yf, yb, xs, z, d_rep, nw):
    y = (yf + yb + d_rep * xs) * _silu(z)
    return (y * lax.rsqrt(jnp.mean(y * y, axis=-1, keepdims=True) + EPS)) * nw


def _f_glu(a, g, ba, bg):
    return (a + ba) * jax.nn.sigmoid(g + bg)


def _f_ln_silu(h, w, b):
    mu = jnp.mean(h, axis=-1, keepdims=True)
    d = h - mu
    y = d * lax.rsqrt(jnp.mean(d * d, axis=-1, keepdims=True) + EPS)
    return _silu(y * w + b)


def _f_loss_rows(h, t, w):
    y = (h * lax.rsqrt(jnp.mean(h * h, axis=-1, keepdims=True) + EPS)) * w
    e = y - t
    return 0.5 * jnp.mean(e * e, axis=-1, keepdims=True)


def _f_adamw(w, m, v, ga, gb):
    g = ga + gb
    m = ADAM_B1 * m + (1.0 - ADAM_B1) * g
    v = ADAM_B2 * v + (1.0 - ADAM_B2) * (g * g)
    m_hat = m / (1.0 - ADAM_B1 ** ADAM_STEP)
    v_hat = v / (1.0 - ADAM_B2 ** ADAM_STEP)
    delta = -ADAM_LR * (m_hat / (jnp.sqrt(v_hat) + ADAM_EPS) + ADAM_WD * w)
    return g, delta, m, v


def _adamw(name, w, m, v, ga, gb):
    shape = w.shape
    c = shape[-1]
    two_d = [t.reshape(-1, c) for t in (w, m, v, ga, gb)]
    rows = two_d[0].shape[0]
    pad = _round_up(rows, SUBLANE) - rows
    if pad:
        two_d = [jnp.pad(t, ((0, pad), (0, 0))) for t in two_d]
    outs = _rw_fwd(name, _f_adamw, two_d, [], [c] * 4)
    return tuple(o[:rows].reshape(shape) for o in outs)


def _sum_leading(name, x, idxs):
    _, R, C = x.shape
    tm = _row_tile(R, max(SUBLANE, min(512, (512 * 1024) // C)))

    def body(x_ref, o_ref):
        acc = x_ref[idxs[0]].astype(F32)
        for k in idxs[1:]:
            acc = acc + x_ref[k].astype(F32)
        o_ref[...] = acc

    return _pcall(
        body, name=name, grid=(R // tm,), in_specs=[pl.BlockSpec((x.shape[0], tm, C), lambda i: (0, i, 0))],
        out_specs=pl.BlockSpec((tm, C), lambda i: (i, 0)), out_shape=jax.ShapeDtypeStruct((R, C), F32),
        compiler_params=_cparams(1),
    )(x)


def _loss_fwd(h, t, w):
    T, D = h.shape
    tm = _row_tile(T, 256)

    def body(h_ref, t_ref, w_ref, o_ref):
        i = pl.program_id(0)
        part = jnp.sum(_f_loss_rows(h_ref[...], t_ref[...], w_ref[...]), axis=0, keepdims=True)
        part = jnp.broadcast_to(part, (1, LANE))

        @pl.when(i == 0)
        def _():
            o_ref[...] = part

        @pl.when(i > 0)
        def _():
            o_ref[...] += part

    return _pcall(
        body, name="loss_fwd", grid=(T // tm,),
        in_specs=[pl.BlockSpec((tm, D), lambda i: (i, 0)), pl.BlockSpec((tm, D), lambda i: (i, 0)),
                  pl.BlockSpec((1, D), lambda i: (0, 0))],
        out_specs=pl.BlockSpec((1, LANE), lambda i: (0, 0)), out_shape=jax.ShapeDtypeStruct((1, LANE), F32),
        compiler_params=_cparams(1),
    )(h, t, w)


CONV_ROWS = 256
CONV_ACC_ELEMS = 16384


def _tap_mask(mask, t, s):
    if mask is None:
        return None
    kind, arg = mask
    if kind == "seg":
        if s == 0:
            return None
        return (t >= arg) == ((t + s) >= arg)
    col = jnp.bitwise_and(t, GRID_W - 1)
    return (col != 0) if arg < 0 else (col != GRID_W - 1)


def _conv_plan(T, C, taps):
    rc = CONV_ROWS if T % CONV_ROWS == 0 else LANE
    assert T % rc == 0
    ct = next((t for t in (512, 256, LANE) if C % t == 0), C)
    reach = max(abs(s) for s, _ in taps)
    hb = next(h for h in (8, 16, 32, 64, 128, 256) if h >= reach and rc % h == 0)
    sub = max(SUBLANE, min(rc, CONV_ACC_ELEMS // ct))
    return rc, ct, hb, sub, T // rc, C // ct


def _halo_specs(rc, ct, hb, T, off_blocks):
    per = rc // hb
    last = T // hb - 1
    prev = pl.BlockSpec((hb, ct), lambda j, i: (jnp.maximum(i * per - 1, 0), off_blocks + j))
    cur = pl.BlockSpec((rc, ct), lambda j, i: (i, off_blocks + j))
    nxt = pl.BlockSpec((hb, ct), lambda j, i: (jnp.minimum((i + 1) * per, last), off_blocks + j))
    return [prev, cur, nxt]


def _fill_halo(pad_ref, p_ref, c_ref, n_ref, i, nrc, rc, hb):
    pad_ref[0:hb, :] = jnp.where(i > 0, p_ref[...], 0.0)
    pad_ref[hb:hb + rc, :] = c_ref[...]
    pad_ref[hb + rc:hb + rc + hb, :] = jnp.where(i < nrc - 1, n_ref[...], 0.0)


def _conv_fwd(name, u, col_off, C, w, b, taps, act=False):
    T = u.shape[0]
    rc, ct, hb, sub, nrc, ncc = _conv_plan(T, C, taps)
    assert col_off % ct == 0
    K = len(taps)

    def body(up, uc, un, w_ref, b_ref, *rest):
        y_ref = rest[0]
        pad_ref = rest[-1]
        i = pl.program_id(1)
        _fill_halo(pad_ref, up, uc, un, i, nrc, rc, hb)
        for r0 in range(0, rc, sub):
            t = i * rc + r0 + lax.broadcasted_iota(jnp.int32, (sub, 1), 0)
            acc = jnp.broadcast_to(b_ref[...], (sub, ct))
            for k, (s, mask) in enumerate(taps):
                v = pad_ref[hb + r0 + s:hb + r0 + s + sub, :]
                m = _tap_mask(mask, t, s)
                if m is not None:
                    v = jnp.where(m, v, 0.0)
                acc = acc + w_ref[k:k + 1, :] * v
            y_ref[r0:r0 + sub, :] = acc
            if act:
                rest[1][r0:r0 + sub, :] = _silu(acc)

    n_out = 2 if act else 1
    res = _pcall(
        body, name=name, grid=(ncc, nrc),
        in_specs=_halo_specs(rc, ct, hb, T, col_off // ct) + [pl.BlockSpec((K, ct), lambda j, i: (0, j)),
                                                              pl.BlockSpec((1, ct), lambda j, i: (0, j))],
        out_specs=[pl.BlockSpec((rc, ct), lambda j, i: (i, j))] * n_out,
        out_shape=[jax.ShapeDtypeStruct((T, C), F32)] * n_out,
        scratch_shapes=[pltpu.VMEM((rc + 2 * hb, ct), F32)], compiler_params=_cparams(2),
    )(u, u, u, w, b)
    return res if act else res[0]


def _conv_bwd(name, u, col_off, C, w, g, taps):
    T = u.shape[0]
    rc, ct, hb, sub, nrc, ncc = _conv_plan(T, C, taps)
    K = len(taps)

    def body(up, uc, un, gp, gc, gn, w_ref, du_ref, dw_ref, db_ref, upad, gpad):
        i = pl.program_id(1)
        _fill_halo(upad, up, uc, un, i, nrc, rc, hb)
        _fill_halo(gpad, gp, gc, gn, i, nrc, rc, hb)

        @pl.when(i == 0)
        def _():
            dw_ref[...] = jnp.zeros_like(dw_ref)
            db_ref[...] = jnp.zeros_like(db_ref)

        def fold(v):
            return jnp.sum(v.reshape(sub // SUBLANE, SUBLANE, ct), axis=0)

        dws = [jnp.zeros((SUBLANE, ct), F32) for _ in range(K)]
        dbs = jnp.zeros((SUBLANE, ct), F32)
        for r0 in range(0, rc, sub):
            t = i * rc + r0 + lax.broadcasted_iota(jnp.int32, (sub, 1), 0)
            gv = gpad[hb + r0:hb + r0 + sub, :]
            dbs = dbs + fold(gv)
            acc = jnp.zeros((sub, ct), F32)
            for k, (s, mask) in enumerate(taps):
                gs = gpad[hb + r0 - s:hb + r0 - s + sub, :]
                m = _tap_mask(mask, t - s, s)
                if m is not None:
                    gs = jnp.where(m, gs, 0.0)
                acc = acc + w_ref[k:k + 1, :] * gs
                uv = upad[hb + r0 + s:hb + r0 + s + sub, :]
                m = _tap_mask(mask, t, s)
                prod = gv * uv
                if m is not None:
                    prod = jnp.where(m, prod, 0.0)
                dws[k] = dws[k] + fold(prod)
            du_ref[r0:r0 + sub, :] = acc
        for k in range(K):
            dw_ref[k:k + 1, :] += jnp.sum(dws[k], axis=0, keepdims=True)
        db_ref[...] += jnp.sum(dbs, axis=0, keepdims=True)

    halo_u = _halo_specs(rc, ct, hb, T, col_off // ct)
    halo_g = _halo_specs(rc, ct, hb, T, 0)
    return _pcall(
        body, name=name, grid=(ncc, nrc),
        in_specs=halo_u + halo_g + [pl.BlockSpec((K, ct), lambda j, i: (0, j))],
        out_specs=[pl.BlockSpec((rc, ct), lambda j, i: (i, j)), pl.BlockSpec((K, ct), lambda j, i: (0, j)),
                   pl.BlockSpec((1, ct), lambda j, i: (0, j))],
        out_shape=[jax.ShapeDtypeStruct((T, C), F32), jax.ShapeDtypeStruct((K, C), F32),
                   jax.ShapeDtypeStruct((1, C), F32)],
        scratch_shapes=[pltpu.VMEM((rc + 2 * hb, ct), F32), pltpu.VMEM((rc + 2 * hb, ct), F32)],
        compiler_params=_cparams(2),
    )(u, u, u, g, g, g, w)


def _ssd_group(xg, bm, cm, s_in, *per_head, sgn, P):
    R = len(per_head) // 3
    dtcs, dtrs, a_s = per_head[:R], per_head[R:2 * R], per_head[2 * R:]
    q, rp = xg.shape
    ii = lax.broadcasted_iota(jnp.int32, (q, q), 0)
    jj = lax.broadcasted_iota(jnp.int32, (q, q), 1)
    causal = ((jj - ii) * sgn) <= 0
    causal_t = ((ii - jj) * sgn) <= 0
    lane = lax.broadcasted_iota(jnp.int32, (1, rp), 1)
    row = lax.broadcasted_iota(jnp.int32, (rp, 1), 0)
    nt = (((1,), (1,)), ((), ()))
    tn = (((0,), (0,)), ((), ()))
    cb = lax.dot_general(cm.astype(BF16), bm.astype(BF16), nt, preferred_element_type=F32)
    dt_x = jnp.zeros((q, rp), F32)
    acum_x = jnp.zeros((q, rp), F32)
    tot_row = jnp.zeros((1, rp), F32)
    tot_col = jnp.zeros((rp, 1), F32)
    wts, lane_masks = [], []
    for r in range(R):
        hm = (lane >= r * P) & (lane < (r + 1) * P)
        hc = (row >= r * P) & (row < (r + 1) * P)
        dac = dtcs[r] * a_s[r]
        dar = dtrs[r] * a_s[r]
        acum_c = jnp.sum(jnp.where(causal, dar, 0.0), axis=1, keepdims=True)
        acum_r = jnp.sum(jnp.where(causal_t, dac, 0.0), axis=0, keepdims=True)
        decay = jnp.where(causal, jnp.exp(jnp.where(causal, acum_c - acum_r, 0.0)), 0.0)
        tot = jnp.sum(dac, axis=0, keepdims=True)
        dt_x = jnp.where(hm, dtcs[r], dt_x)
        acum_x = jnp.where(hm, acum_c, acum_x)
        tot_row = jnp.where(hm, tot, tot_row)
        tot_col = jnp.where(hc, tot, tot_col)
        wts.append((cb * decay).astype(BF16))
        lane_masks.append(hm)
    xdt = xg * dt_x
    xdt_b = xdt.astype(BF16)
    y = jnp.zeros((q, rp), F32)
    for r in range(R):
        y = jnp.where(lane_masks[r], jnp.dot(wts[r], xdt_b, preferred_element_type=F32), y)
    dte = jnp.exp(tot_row - acum_x)
    cs = lax.dot_general((xdt * dte).astype(BF16), bm.astype(BF16), tn, preferred_element_type=F32)
    y = y + lax.dot_general(cm.astype(BF16), s_in.astype(BF16), nt, preferred_element_type=F32) * jnp.exp(acum_x)
    s_out = jnp.exp(tot_col) * s_in + cs
    return y, s_out


def _ssd_maps(NC, ncc, reverse_steps):
    def chunk(d, s):
        if reverse_steps:
            s = NC - 1 - s
        return jnp.where(d == 0, s, jnp.where(s < ncc, ncc - 1 - s, NC - 1 - s + ncc))

    def lat_chunk(d, s):
        c = chunk(d, s) - ncc
        return jnp.where(c < 0, jnp.where(d == 0, 0, NC - ncc - 1), c)

    def step(s):
        return NC - 1 - s if reverse_steps else s

    return chunk, lat_chunk, step


def _ssd_specs(chunk, H, R, Q, N, RP, bo, co):
    return [
        pl.BlockSpec((Q, RP), lambda d, g, s: (chunk(d, s), g)),
        pl.BlockSpec((Q, N), lambda d, g, s: (chunk(d, s), bo + g)),
        pl.BlockSpec((Q, N), lambda d, g, s: (chunk(d, s), co + g)),
        pl.BlockSpec((R, Q, 1), lambda d, g, s: (d * (H // R) + g, chunk(d, s), 0)),
        pl.BlockSpec((R, 1, Q), lambda d, g, s: (d * (H // R) + g, 0, chunk(d, s))),
        pl.BlockSpec((R, 1, 1), lambda d, g, s: (d * (H // R) + g, 0, 0)),
    ]


def _ssd_fwd(xbc, b_off, c_off, dtc, dtr, a, H, P, ncc):
    T = xbc.shape[0]
    N, Q = SSD_STATE, SSD_CHUNK
    NC = T // Q
    G = (c_off - b_off) // N
    R = H // G
    RP = R * P
    chunk, lat_chunk, _ = _ssd_maps(NC, ncc, False)

    def body(x_ref, b_ref, c_ref, dtc_ref, dtr_ref, a_ref, y_ref, se_ref, s_ref):
        d, s = pl.program_id(0), pl.program_id(2)

        @pl.when(s == 0)
        def _():
            s_ref[...] = jnp.zeros_like(s_ref)

        s_in = s_ref[...]
        se_ref[...] = s_in
        per_head = ([dtc_ref[r] for r in range(R)] + [dtr_ref[r] for r in range(R)] + [a_ref[r] for r in range(R)])
        y, s_out = _ssd_group(x_ref[...], b_ref[...], c_ref[...], s_in, *per_head, sgn=1 - 2 * d, P=P)
        y_ref[...] = y
        s_ref[...] = s_out

    return _pcall(
        body, name="ssd_fwd", grid=(2, G, NC),
        in_specs=_ssd_specs(chunk, H, R, Q, N, RP, b_off // N, c_off // N),
        out_specs=[
            pl.BlockSpec((None, Q, RP), lambda d, g, s: (d, lat_chunk(d, s), g)),
            pl.BlockSpec((None, None, None, RP, N), lambda d, g, s: (d, g, s, 0, 0)),
        ],
        out_shape=[jax.ShapeDtypeStruct((2, T - ncc * Q, H * P), F32),
                   jax.ShapeDtypeStruct((2, G, NC, RP, N), F32)],
        scratch_shapes=[pltpu.VMEM((RP, N), F32)], compiler_params=_cparams(3),
    )(xbc, xbc, xbc, dtc, dtr, a)


def _ssd_bwd(xbc, b_off, c_off, dtc, dtr, a, s_enter, dy, H, P, ncc):
    T = xbc.shape[0]
    N, Q = SSD_STATE, SSD_CHUNK
    NC = T // Q
    G = (c_off - b_off) // N
    R = H // G
    RP = R * P
    chunk, lat_chunk, step = _ssd_maps(NC, ncc, True)

    def body(x_ref, b_ref, c_ref, dtc_ref, dtr_ref, a_ref, se_ref, dy_ref,
             dx_ref, db_ref, dc_ref, ddtc_ref, ddtr_ref, da_ref, ds_ref):
        d, s = pl.program_id(0), pl.program_id(2)

        @pl.when(s == 0)
        def _():
            ds_ref[...] = jnp.zeros_like(ds_ref)

        per_head = ([dtc_ref[r] for r in range(R)] + [dtr_ref[r] for r in range(R)] + [a_ref[r] for r in range(R)])
        f = functools.partial(_ssd_group, sgn=1 - 2 * d, P=P)
        _, vjp = jax.vjp(f, x_ref[...], b_ref[...], c_ref[...], se_ref[...], *per_head)
        is_latent = chunk(d, s) >= ncc
        dy_v = jnp.where(is_latent, dy_ref[...], 0.0)
        grads = vjp((dy_v, ds_ref[...]))
        dx_ref[...] = grads[0]
        db_ref[...] = grads[1]
        dc_ref[...] = grads[2]
        ds_ref[...] = grads[3]
        for r in range(R):
            ddtc_ref[r] = grads[4 + r]
            ddtr_ref[r] = grads[4 + R + r]
            da_ref[r] = jnp.broadcast_to(grads[4 + 2 * R + r], (SUBLANE, LANE))

    return _pcall(
        body, name="ssd_bwd", grid=(2, G, NC),
        in_specs=_ssd_specs(chunk, H, R, Q, N, RP, b_off // N, c_off // N) + [
            pl.BlockSpec((None, None, None, RP, N), lambda d, g, s: (d, g, step(s), 0, 0)),
            pl.BlockSpec((Q, RP), lambda d, g, s: (lat_chunk(d, s), g)),
        ],
        out_specs=[
            pl.BlockSpec((None, Q, RP), lambda d, g, s: (d, chunk(d, s), g)),
            pl.BlockSpec((None, Q, N), lambda d, g, s: (d, chunk(d, s), g)),
            pl.BlockSpec((None, Q, N), lambda d, g, s: (d, chunk(d, s), g)),
            pl.BlockSpec((R, Q, 1), lambda d, g, s: (d * G + g, chunk(d, s), 0)),
            pl.BlockSpec((R, 1, Q), lambda d, g, s: (d * G + g, 0, chunk(d, s))),
            pl.BlockSpec((R, SUBLANE, LANE), lambda d, g, s: ((d * G + g) * NC + s, 0, 0)),
        ],
        out_shape=[
            jax.ShapeDtypeStruct((2, T, H * P), F32), jax.ShapeDtypeStruct((2, T, G * N), F32),
            jax.ShapeDtypeStruct((2, T, G * N), F32), jax.ShapeDtypeStruct((2 * H, T, 1), F32),
            jax.ShapeDtypeStruct((2 * H, 1, T), F32), jax.ShapeDtypeStruct((2 * G * NC * R, SUBLANE, LANE), F32),
        ],
        scratch_shapes=[pltpu.VMEM((RP, N), F32)], compiler_params=_cparams(3),
    )(xbc, xbc, xbc, dtc, dtr, a, s_enter, dy)


def _allgather8(name, v):
    R, C = v.shape

    def body(x_ref, out_ref, send_sems, recv_sems, local_sem):
        x, y, c = lax.axis_index("x"), lax.axis_index("y"), lax.axis_index("c")
        me, sibling = (x, y, c), (x, y, 1 - c)
        chips = [(1 - x, y), (x, 1 - y), (1 - x, 1 - y)]

        def slot(px, py, pc):
            return out_ref.at[4 * px + 2 * py + pc]

        def copy(k, block, to, src=None):
            return pltpu.make_async_remote_copy(
                src_ref=slot(*block) if src is None else src, dst_ref=slot(*block),
                send_sem=send_sems.at[k], recv_sem=recv_sems.at[k], device_id=to, device_id_type=MESH)

        mine = pltpu.make_async_copy(x_ref, slot(*me), local_sem)
        mine.start()
        first = [copy(0, me, sibling, src=x_ref)]
        first += [copy(1 + j, me, (*chip, c), src=x_ref) for j, chip in enumerate(chips)]
        for cp in first:
            cp.start()
        passed = [copy(4 + j, (*chip, c), sibling) for j, chip in enumerate(chips)]
        for j, chip in enumerate(chips):
            copy(1 + j, (*chip, c), me).wait_recv()
            passed[j].start()
        copy(0, sibling, me).wait_recv()
        for j, chip in enumerate(chips):
            copy(4 + j, (*chip, 1 - c), me).wait_recv()
        for cp in first + passed:
            cp.wait_send()
        mine.wait()

    return _pcall(
        body, name=name, out_shape=jax.ShapeDtypeStruct((N_DEV, R, C), v.dtype),
        in_specs=[pl.BlockSpec(memory_space=pltpu.VMEM)], out_specs=pl.BlockSpec(memory_space=pltpu.VMEM),
        scratch_shapes=[pltpu.SemaphoreType.DMA((7,)), pltpu.SemaphoreType.DMA((7,)), pltpu.SemaphoreType.DMA],
        compiler_params=pltpu.CompilerParams(vmem_limit_bytes=VMEM_LIMIT_BYTES),
    )(v)


def _exchange4_start(name, srcs, bcast, dep):
    n = len(srcs)
    lands = [lax.empty(((N_CHIPS,) + s.shape) if bcast else s.shape, s.dtype) for s in srcs]

    def body(*refs):
        src, land = refs[:n], refs[n:2 * n]
        send_sems, recv_sems = refs[2 * n + 1], refs[2 * n + 2]
        token = refs[-1]
        x, y, c = lax.axis_index("x"), lax.axis_index("y"), lax.axis_index("c")
        me = 2 * x + y
        for a in range(n):
            for j, (px, py) in enumerate([(1 - x, y), (x, 1 - y), (1 - x, 1 - y)]):
                pltpu.make_async_remote_copy(
                    src_ref=src[a] if bcast else src[a].at[2 * px + py], dst_ref=land[a].at[me],
                    send_sem=send_sems.at[3 * a + j], recv_sem=recv_sems.at[3 * a + j], device_id=(px, py, c),
                    device_id_type=MESH).start()
        token[...] = jnp.zeros_like(token)

    hbm = pl.BlockSpec(memory_space=pltpu.HBM)
    sem = pl.BlockSpec(memory_space=pltpu.SEMAPHORE)
    outs = _pcall(
        body, name=name,
        out_shape=(pltpu.SemaphoreType.DMA((3 * n,)), pltpu.SemaphoreType.DMA((3 * n,)),
                   *[pltpu.HBM(s.shape, s.dtype) for s in srcs], *[pltpu.HBM(l.shape, l.dtype) for l in lands],
                   jax.ShapeDtypeStruct((SUBLANE, LANE), F32)),
        in_specs=[hbm] * (2 * n) + [pl.BlockSpec(memory_space=pl.ANY)],
        out_specs=(sem, sem, *[hbm] * (2 * n), pl.BlockSpec(memory_space=pltpu.VMEM)),
        input_output_aliases={k: 2 + k for k in range(2 * n)},
        compiler_params=pltpu.CompilerParams(has_side_effects=pltpu.SideEffectType.DATAFLOW_SIDE_EFFECTING),
    )(*[pltpu.with_memory_space_constraint(s, pltpu.HBM) for s in srcs],
      *[pltpu.with_memory_space_constraint(l, pltpu.HBM) for l in lands], dep)
    return (n, bcast, outs[0], outs[1], outs[2:2 + n], outs[2 + n:2 + 2 * n]), outs[-1]


def _exchange4_wait(name, handle, after):
    n, bcast, send_sems, recv_sems, src_thru, land_thru = handle

    def body(*refs):
        src, land = refs[:n], refs[n:2 * n]
        send_sems, recv_sems = refs[2 * n], refs[2 * n + 1]
        x, y, c = lax.axis_index("x"), lax.axis_index("y"), lax.axis_index("c")
        for a in range(n):
            for j, (px, py) in enumerate([(1 - x, y), (x, 1 - y), (1 - x, 1 - y)]):
                pk = 2 * px + py
                copy = pltpu.make_async_remote_copy(
                    src_ref=src[a] if bcast else src[a].at[pk], dst_ref=land[a].at[pk],
                    send_sem=send_sems.at[3 * a + j], recv_sem=recv_sems.at[3 * a + j], device_id=(px, py, c),
                    device_id_type=MESH)
                copy.wait_send()
                copy.wait_recv()

    hbm = pl.BlockSpec(memory_space=pltpu.HBM)
    sem = pl.BlockSpec(memory_space=pltpu.SEMAPHORE)
    outs = _pcall(
        body, name=name,
        out_shape=tuple(pltpu.HBM(t.shape, t.dtype) for t in (*src_thru, *land_thru)),
        in_specs=[hbm] * (2 * n) + [sem, sem, pl.BlockSpec(memory_space=pl.ANY)], out_specs=tuple([hbm] * (2 * n)),
        input_output_aliases={k: k for k in range(2 * n)},
        compiler_params=pltpu.CompilerParams(has_side_effects=pltpu.SideEffectType.DATAFLOW_SIDE_EFFECTING),
    )(*src_thru, *land_thru, send_sems, recv_sems, after)
    return list(outs[n:])


def _fill_own(landed, own, me, bcast):
    blk = own if bcast else lax.dynamic_index_in_dim(own, me, 0, keepdims=False)
    return lax.dynamic_update_index_in_dim(landed, blk, me, 0)


def _swap_sibling(name, srcs):
    n = len(srcs)

    def body(*refs):
        src, out = refs[:n], refs[n:2 * n]
        send_sems, recv_sems = refs[2 * n:]
        x, y, c = lax.axis_index("x"), lax.axis_index("y"), lax.axis_index("c")
        copies = []
        for a in range(n):
            rc = pltpu.make_async_remote_copy(
                src_ref=src[a], dst_ref=out[a], send_sem=send_sems.at[a], recv_sem=recv_sems.at[a],
                device_id=(x, y, 1 - c), device_id_type=MESH)
            rc.start()
            copies.append(rc)
        for cp in copies:
            cp.wait()

    any_spec = pl.BlockSpec(memory_space=pl.ANY)
    return _pcall(
        body, name=name, out_shape=[jax.ShapeDtypeStruct(s.shape, s.dtype) for s in srcs],
        in_specs=[any_spec] * n, out_specs=[any_spec] * n,
        scratch_shapes=[pltpu.SemaphoreType.DMA((n,)), pltpu.SemaphoreType.DMA((n,))],
    )(*srcs)


def _mod_fwd(c16, mod_w, mod_b_shard):
    nl, D, S = mod_w.shape

    def body(c_ref, w_ref, b_ref, o_ref):
        s = _silu(c_ref[...]).astype(BF16)
        o_ref[...] = jnp.dot(s, w_ref[...].astype(BF16), preferred_element_type=F32) + b_ref[...]

    return _pcall(
        body, name="mod_fwd", grid=(nl,),
        in_specs=[pl.BlockSpec((16, D), lambda l: (0, 0)), pl.BlockSpec((None, D, S), lambda l: (l, 0, 0)),
                  pl.BlockSpec((None, 1, S), lambda l: (l, 0, 0))],
        out_specs=pl.BlockSpec((None, 16, S), lambda l: (l, 0, 0)),
        out_shape=jax.ShapeDtypeStruct((nl, 16, S), F32), compiler_params=_cparams(1),
    )(c16, mod_w, mod_b_shard)


def _mod_w_update(s16t, dm16, w, m, v):
    nl, D, S = w.shape
    tm = _row_tile(D, 256)

    def body(s_ref, dm_ref, w_ref, m_ref, v_ref, g_ref, dl_ref, nm_ref, nv_ref):
        g = jnp.dot(s_ref[...], dm_ref[...], preferred_element_type=F32, precision=HIGHEST)
        g, dl, nm, nv = _f_adamw(w_ref[...], m_ref[...], v_ref[...], g, jnp.zeros_like(g))
        g_ref[...] = g
        dl_ref[...] = dl
        nm_ref[...] = nm
        nv_ref[...] = nv

    big = pl.BlockSpec((None, tm, S), lambda l, i: (l, i, 0))
    return _pcall(
        body, name="mod_w_update", grid=(nl, D // tm),
        in_specs=[pl.BlockSpec((tm, 16), lambda l, i: (i, 0)), pl.BlockSpec((None, 16, S), lambda l, i: (l, 0, 0)),
                  big, big, big],
        out_specs=[big] * 4, out_shape=[jax.ShapeDtypeStruct(w.shape, F32)] * 4, compiler_params=_cparams(2),
    )(s16t, dm16, w, m, v)


def _pack(arrs):
    flat = jnp.concatenate([a.reshape(-1).astype(F32) for a in arrs])
    n = flat.shape[0]
    rows = _round_up(_cdiv(n, LANE), SUBLANE)
    return jnp.pad(flat, (0, rows * LANE - n)).reshape(rows, LANE)


def _unpack(buf, shapes):
    flat = buf.reshape(-1)
    out, pos = [], 0
    for s in shapes:
        n = 1
        for d in s:
            n *= d
        out.append(flat[pos:pos + n].reshape(s))
        pos += n
    return out


SHARD_AXIS = {
    "mod_w": 2, "ssd_w_in": 2, "ssd_conv_w": 2, "ssd_w_out": 1, "conf_w_pw1": 2, "conf_b_pw1": 1, "conf_w_dw": 2,
    "conf_b_dw": 1, "conf_ln_w": 1, "conf_ln_b": 1, "conf_w_pw2": 1, "conf_b_pw2": 1, "ffn_w_up": 2,
    "ffn_conv_w": 3, "ffn_w_down": 1,
}
BIG = ("ssd_w_in", "ssd_w_out", "conf_w_pw1", "conf_w_pw2", "ffn_w_up", "ffn_w_down")
WEIGHTS = ("c_ctx", "mod_w", "mod_b", "norm1_w", "norm2_w", "ssd_w_in", "ssd_conv_w", "ssd_conv_b", "ssd_dt_bias",
           "ssd_a_log", "ssd_d", "ssd_norm_w", "ssd_w_out", "conf_w_pw1", "conf_b_pw1", "conf_w_dw", "conf_b_dw",
           "conf_ln_w", "conf_ln_b", "conf_w_pw2", "conf_b_pw2", "ffn_w_up", "ffn_conv_w", "ffn_conv_b",
           "ffn_w_down", "final_norm_w")
SMALL = tuple(n for n in WEIGHTS if n not in BIG and n != "mod_w")
SMALL_SHARDED = tuple(n for n in SMALL if n in SHARD_AXIS)


def _unshard(stacked, axis):
    return jnp.concatenate([stacked[k] for k in range(N_CHIPS)], axis=axis)


def _to_blocks(full, axis):
    return jnp.stack(jnp.split(full, N_CHIPS, axis=axis))


def _par(v):
    v = v.reshape(-1, v.shape[-1])
    return v[:, None, :]


def kernel(x, c, ctx, c_ctx, mod_w, mod_b, norm1_w, norm2_w, ssd_w_in, ssd_conv_w, ssd_conv_b, ssd_dt_bias, ssd_a_log, ssd_d, ssd_norm_w, ssd_w_out, conf_w_pw1, conf_b_pw1, conf_w_dw, conf_b_dw, conf_ln_w, conf_ln_b, conf_w_pw2, conf_b_pw2, ffn_w_up, ffn_conv_w, ffn_conv_b, ffn_w_down, final_norm_w, loss_target, m_c_ctx, m_mod_w, m_mod_b, m_norm1_w, m_norm2_w, m_ssd_w_in, m_ssd_conv_w, m_ssd_conv_b, m_ssd_dt_bias, m_ssd_a_log, m_ssd_d, m_ssd_norm_w, m_ssd_w_out, m_conf_w_pw1, m_conf_b_pw1, m_conf_w_dw, m_conf_b_dw, m_conf_ln_w, m_conf_ln_b, m_conf_w_pw2, m_conf_b_pw2, m_ffn_w_up, m_ffn_conv_w, m_ffn_conv_b, m_ffn_w_down, m_final_norm_w, v_c_ctx, v_mod_w, v_mod_b, v_norm1_w, v_norm2_w, v_ssd_w_in, v_ssd_conv_w, v_ssd_conv_b, v_ssd_dt_bias, v_ssd_a_log, v_ssd_d, v_ssd_norm_w, v_ssd_w_out, v_conf_w_pw1, v_conf_b_pw1, v_conf_w_dw, v_conf_b_dw, v_conf_ln_w, v_conf_ln_b, v_conf_w_pw2, v_conf_b_pw2, v_ffn_w_up, v_ffn_conv_w, v_ffn_conv_b, v_ffn_w_down, v_final_norm_w):
    given = dict(locals())
    W = {n: given[n] for n in WEIGHTS}
    Mo = {n: given["m_" + n] for n in WEIGHTS}
    Vo = {n: given["v_" + n] for n in WEIGHTS}

    ax, ay, ac = lax.axis_index("x"), lax.axis_index("y"), lax.axis_index("c")
    chip = 2 * ax + ay
    dev = 4 * ax + 2 * ay + ac

    D = x.shape[-1]
    L, Lc = x.shape[1], ctx.shape[1]
    T0 = L + Lc
    H = ssd_a_log.shape[-1]
    DI = ssd_norm_w.shape[-1]
    P = DI // H
    CD = ssd_conv_b.shape[-1]
    N = SSD_STATE
    G = (CD - DI) // (2 * N)
    FH = ffn_conv_b.shape[-1]
    KS = ssd_conv_w.shape[1]
    KC = conf_w_dw.shape[1]
    ncc = Lc // SSD_CHUNK

    shard_b = {n: W[n].astype(BF16) for n in BIG}
    gather_a, token = _exchange4_start("gather_w_in_start", [shard_b["ssd_w_in"]], True, x)
    c = lax.optimization_barrier((c, token))[0]

    small_shard_shapes = [W[n].shape for n in SMALL_SHARDED]
    f1 = _allgather8("gather_small", _pack([c] + [W[n] for n in SMALL_SHARDED]))
    c_rows, full_small = [], {n: [] for n in SMALL_SHARDED}
    for k in range(N_DEV):
        parts = _unpack(f1[k], [c.shape] + small_shard_shapes)
        c_rows.append(parts[0])
        if k % 2 == 0:
            for n, p in zip(SMALL_SHARDED, parts[1:]):
                full_small[n].append(p)
    Wf = dict(W)
    for n in SMALL_SHARDED:
        Wf[n] = jnp.concatenate(full_small[n], axis=SHARD_AXIS[n])
    c16 = jnp.concatenate(c_rows + [c_ctx[None, :], jnp.zeros((16 - N_DEV - 1, D), F32)], axis=0)

    S_mod = mod_w.shape[-1]
    mod_b_shard = lax.dynamic_slice_in_dim(mod_b, chip * S_mod, S_mod, axis=1)[:, None, :]
    mod_part = _mod_fwd(c16, mod_w, mod_b_shard)
    f2 = _allgather8("gather_mod", mod_part.reshape(2 * 16, S_mod))
    mods = jnp.concatenate([f2[2 * k].reshape(2, 16, S_mod) for k in range(N_CHIPS)], axis=-1)
    my = lax.dynamic_slice_in_dim(mods, dev, 1, axis=1)[:, 0]
    sh1, sc1, g1, sh2, sc2, g2 = [[my[l, k * D:(k + 1) * D] for l in range(2)] for k in range(6)]
    csh1, csc1 = mods[0, N_DEV, 0:D], mods[0, N_DEV, D:2 * D]

    def full_weight(n, landed):
        return _unshard(_fill_own(landed, shard_b[n], chip, True), SHARD_AXIS[n])

    xl = x[0]
    hcat = jnp.concatenate([ctx[0], xl], axis=0)
    n1w0, n2w0, n1w1, n2w1 = _par(norm1_w[0]), _par(norm2_w[0]), _par(norm1_w[1]), _par(norm2_w[1])
    sc_seg = jnp.stack([csc1, sc1[0]])[:, None, :]
    sh_seg = jnp.stack([csh1, sh1[0]])[:, None, :]

    a0 = _rw_fwd("l0_modnorm1", _f_modnorm, [hcat], [n1w0, sc_seg, sh_seg], [D], seg_rows=(Lc,))
    (landed_in,) = _exchange4_wait("gather_w_in_wait", gather_a, a0)
    w_in = full_weight("ssd_w_in", landed_in)[0]
    rest = [n for n in BIG if n != "ssd_w_in"]
    gather_b, token = _exchange4_start("gather_rest_start", [shard_b[n] for n in rest], True, landed_in)
    a0 = lax.optimization_barrier((a0, token))[0]
    proj = _mm(a0, w_in, name="l0_w_in")
    seg_taps = [(k - KS // 2, ("seg", Lc)) for k in range(KS)]
    xbc_pre, xbc = _conv_fwd("l0_conv", proj, DI, CD, Wf["ssd_conv_w"][0], ssd_conv_b, seg_taps, act=True)
    dt_raw = proj[:, DI + CD:]
    dt_bias = _par(ssd_dt_bias.reshape(1, 2 * H))
    dt = _rw_fwd("l0_softplus", _f_softplus, [dt_raw], [dt_bias], [2 * H])
    dt_t = dt.T
    dtc, dtr = dt_t[:, :, None], dt_t[:, None, :]
    a_neg = -jnp.exp(ssd_a_log.reshape(2 * H, 1, 1))
    y2, s_enter = _ssd_fwd(xbc, DI, DI + G * N, dtc, dtr, a_neg, H, P, ncc)
    y2 = y2.reshape(2 * L, DI)
    gate_rows = [(y2, 0, DI, 0), (y2, 0, DI, L), (xbc, 0, DI, Lc), (proj, 0, DI, Lc)]
    d_rep = _par(jnp.repeat(ssd_d[0], P))
    ssd_nw = _par(ssd_norm_w[0])
    yn = _rw_fwd("l0_ssd_gate", _f_ssd_gate, gate_rows, [d_rep, ssd_nw], [DI], T=L)
    Wb = {n: full_weight(n, g) for n, g in zip(rest, _exchange4_wait("gather_rest_wait", gather_b, yn))}
    w_out, w_pw1, w_pw2 = Wb["ssd_w_out"][0], Wb["conf_w_pw1"][0], Wb["conf_w_pw2"][0]
    w_up, w_dn = Wb["ffn_w_up"], Wb["ffn_w_down"]
    mix0 = _mm(yn, w_out, name="l0_w_out")
    g1_0, g2_0, g1_1, g2_1 = _par(g1[0]), _par(g2[0]), _par(g1[1]), _par(g2[1])
    h1 = _rw_fwd("l0_res1", _f_gate_res, [xl, mix0], [g1_0], [D])

    grid_taps = [((i - 1) * GRID_W + (j - 1), (None if j == 1 else ("col", j - 1))) for i in range(3) for j in range(3)]

    def ffn_fwd(l, h, tag):
        a = _rw_fwd(tag + "_modnorm2", _f_modnorm, [h], [_par(norm2_w[l]), _par(sc2[l]), _par(sh2[l])], [D])
        hh = _mm(a, w_up[l], name=tag + "_w_up")
        gc = _conv_fwd(tag + "_ffn_conv", hh, FH, FH, Wf["ffn_conv_w"][l].reshape(9, FH), ffn_conv_b[l][None, :],
                       grid_taps)
        act = _rw_fwd(tag + "_act", _f_ffn_act, [(hh, 0, FH), gc], [], [FH], col_tile=_tile(FH, 1536))
        dn = _mm(act, w_dn[l], name=tag + "_w_down")
        return a, hh, gc, act, dn

    a1, hh0, gc0, act0, dn0 = ffn_fwd(0, h1, "l0")
    h2 = _rw_fwd("l0_res2", _f_gate_res, [h1, dn0], [g2_0], [D])

    a2 = _rw_fwd("l1_modnorm1", _f_modnorm, [h2], [n1w1, _par(sc1[1]), _par(sh1[1])], [D])
    pw = _mm(a2, w_pw1, name="l1_pw1")
    b_pw1 = Wf["conf_b_pw1"][0]
    glu = _rw_fwd("l1_glu", _f_glu, [(pw, 0, D), (pw, D, D)], [_par(b_pw1[:D]), _par(b_pw1[D:])], [D])
    conf_taps = [(k - KC // 2, None) for k in range(KC)]
    cv = _conv_fwd("l1_conv", glu, 0, D, Wf["conf_w_dw"][0], Wf["conf_b_dw"], conf_taps)
    ln_w, ln_b = _par(Wf["conf_ln_w"][0]), _par(Wf["conf_ln_b"][0])
    ls = _rw_fwd("l1_ln_silu", _f_ln_silu, [cv], [ln_w, ln_b], [D])
    p2 = _mm(ls, w_pw2, name="l1_pw2")
    b_pw2 = _par(Wf["conf_b_pw2"][0])
    h3 = _rw_fwd("l1_res1", _f_gate_res_bias, [h2, p2], [g1_1, b_pw2], [D])
    a3, hh1, gc1, act1, dn1 = ffn_fwd(1, h3, "l1")
    h4 = _rw_fwd("l1_res2", _f_gate_res, [h3, dn1], [g2_1], [D])

    fnw = final_norm_w[None, :]
    tgt = loss_target[0]
    loss_local = _loss_fwd(h4, tgt, fnw)[0, 0]
    loss = lax.psum(loss_local, ("x", "y", "c"))

    G_full = {}
    reduces = {}

    def start_reduce(tag, items, dep):
        blocks = [_to_blocks(g, ax).astype(BF16) for _, g, ax in items]
        handle, tok = _exchange4_start("reduce_" + tag + "_start", blocks, False, dep)
        reduces[tag] = ([n for n, _, _ in items], handle, blocks)
        return tok
    ones = jnp.ones((L, 1), F32)
    (dh4,), (dfnw,) = _rw_bwd("loss_bwd", _f_loss_rows, [h4, tgt], [_par(final_norm_w)], [ones],
                              row_grad=[True, False], par_grad=[True])
    G_full["final_norm_w"] = dfnw.reshape(D)

    def ffn_bwd(l, h, saved, g2_l, dh_out, tag):
        a, hh, gc, act, dn = saved
        (ddn,), (dg2,) = _rw_bwd(tag + "_res2_bwd", _f_gate_res, [h, dn], [g2_l], [dh_out],
                                 row_grad=[False, True], par_grad=[True])
        dact = _mm(ddn, w_dn[l], tb=True, name=tag + "_w_down_dx")
        dwdn = _mm(act, ddn, ta=True, name=tag + "_w_down_dw")
        (dval, dgc), _ = _rw_bwd(tag + "_act_bwd", _f_ffn_act, [(hh, 0, FH), gc], [], [dact],
                                 row_grad=[True, True], par_grad=[], col_tile=_tile(FH, 1536))
        dgin, dcw, dcb = _conv_bwd(tag + "_ffn_conv_bwd", hh, FH, FH, Wf["ffn_conv_w"][l].reshape(9, FH), dgc, grid_taps)
        dhh = jnp.concatenate([dval, dgin], axis=1)
        da = _mm(dhh, w_up[l], tb=True, name=tag + "_w_up_dx")
        dwup = _mm(a, dhh, ta=True, name=tag + "_w_up_dw")
        (dh,), (dn2w, dsc2, dsh2) = _rw_bwd(
            tag + "_modnorm2_bwd", _f_modnorm, [h], [_par(norm2_w[l]), _par(sc2[l]), _par(sh2[l])], [da],
            row_grad=[True], par_grad=[True, True, True], add=dh_out)
        return dh, dict(w_down=dwdn, w_up=dwup, conv_w=dcw.reshape(3, 3, FH), conv_b=dcb.reshape(FH),
                        n2w=dn2w.reshape(D), sc2=dsc2.reshape(D), sh2=dsh2.reshape(D), g2=dg2.reshape(D))

    dh3, gf1 = ffn_bwd(1, h3, (a3, hh1, gc1, act1, dn1), g2_1, dh4, "l1")
    (dp2,), (dg1_1, db_pw2) = _rw_bwd("l1_res1_bwd", _f_gate_res_bias, [h2, p2], [g1_1, b_pw2], [dh3],
                                      row_grad=[False, True], par_grad=[True, True])
    dls = _mm(dp2, w_pw2, tb=True, name="l1_pw2_dx")
    dw_pw2 = _mm(ls, dp2, ta=True, name="l1_pw2_dw")
    (dcv,), (dln_w, dln_b) = _rw_bwd("l1_ln_silu_bwd", _f_ln_silu, [cv], [ln_w, ln_b], [dls],
                                     row_grad=[True], par_grad=[True, True])
    dglu, dw_dw, db_dw = _conv_bwd("l1_conv_bwd", glu, 0, D, Wf["conf_w_dw"][0], dcv, conf_taps)
    (dpa, dpg), (dba, dbg) = _rw_bwd("l1_glu_bwd", _f_glu, [(pw, 0, D), (pw, D, D)],
                                     [_par(b_pw1[:D]), _par(b_pw1[D:])], [dglu],
                                     row_grad=[True, True], par_grad=[True, True])
    dpw = jnp.concatenate([dpa, dpg], axis=1)
    da2 = _mm(dpw, w_pw1, tb=True, name="l1_pw1_dx")
    dw_pw1 = _mm(a2, dpw, ta=True, name="l1_pw1_dw")
    (dh2,), (dn1w1, dsc1_1, dsh1_1) = _rw_bwd(
        "l1_modnorm1_bwd", _f_modnorm, [h2], [n1w1, _par(sc1[1]), _par(sh1[1])], [da2],
        row_grad=[True], par_grad=[True, True, True], add=dh3)
    G_full["conf_b_pw2"] = db_pw2.reshape(1, D)
    G_full["conf_ln_w"], G_full["conf_ln_b"] = dln_w.reshape(1, D), dln_b.reshape(1, D)
    G_full["conf_w_dw"], G_full["conf_b_dw"] = dw_dw[None], db_dw.reshape(1, D)
    G_full["conf_b_pw1"] = jnp.concatenate([dba.reshape(1, D), dbg.reshape(1, D)], axis=1)

    token = start_reduce("l1", [("conf_w_pw2", dw_pw2, 0), ("conf_w_pw1", dw_pw1, 1), ("ffn_w_up1", gf1["w_up"], 1),
                                ("ffn_w_down1", gf1["w_down"], 0)], dh2)
    dh2 = lax.optimization_barrier((dh2, token))[0]
    dh1, gf0 = ffn_bwd(0, h1, (a1, hh0, gc0, act0, dn0), g2_0, dh2, "l0")
    G_full["ffn_conv_w"] = jnp.stack([gf0["conv_w"], gf1["conv_w"]])
    G_full["ffn_conv_b"] = jnp.stack([gf0["conv_b"], gf1["conv_b"]])

    (dmix,), (dg1_0,) = _rw_bwd("l0_res1_bwd", _f_gate_res, [xl, mix0], [g1_0], [dh1],
                                row_grad=[False, True], par_grad=[True])
    dyn = _mm(dmix, w_out, tb=True, name="l0_w_out_dx")
    dw_out = _mm(yn, dmix, ta=True, name="l0_w_out_dw")
    token = start_reduce("l0", [("ffn_w_up0", gf0["w_up"], 1), ("ffn_w_down0", gf0["w_down"], 0),
                                ("ssd_w_out", dw_out, 0)], dmix)
    dyn = lax.optimization_barrier((dyn, token))[0]
    (dy_lat, dxs_gate, dz_lat), (dd_rep, dssd_nw) = _rw_bwd(
        "l0_ssd_gate_bwd", _f_ssd_gate, gate_rows, [d_rep, ssd_nw], [dyn],
        row_grad=[True, False, True, True], par_grad=[True, True], T=L)
    dx2, dB, dC, ddtc, ddtr, da_parts = _ssd_bwd(xbc, DI, DI + G * N, dtc, dtr, a_neg, s_enter, dy_lat, H, P, ncc)
    dxs_gate_all = jnp.pad(dxs_gate, ((Lc, 0), (0, 0)))
    dx2, dB, dC = dx2.reshape(2 * T0, DI), dB.reshape(2 * T0, G * N), dC.reshape(2 * T0, G * N)
    silu_bwd = functools.partial(_rw_bwd, f=_silu, pars=[], row_grad=[True], par_grad=[], T=T0)
    (dxs_pre,), _ = silu_bwd("l0_silu_bwd_x", rows=[(xbc_pre, 0, DI)], cot_fn=lambda p, q, r: p + q + r,
                             cots=[(dx2, 0, DI, 0), (dx2, 0, DI, T0), dxs_gate_all], col_tile=_tile(DI, 1024))
    (db_pre,), _ = silu_bwd("l0_silu_bwd_b", rows=[(xbc_pre, DI, G * N)], cot_fn=lambda p, q: p + q,
                            cots=[(dB, 0, G * N, 0), (dB, 0, G * N, T0)], col_tile=_tile(G * N, 1024))
    (dc_pre,), _ = silu_bwd("l0_silu_bwd_c", rows=[(xbc_pre, DI + G * N, G * N)], cot_fn=lambda p, q: p + q,
                            cots=[(dC, 0, G * N, 0), (dC, 0, G * N, T0)], col_tile=_tile(G * N, 1024))
    dxbc_pre = jnp.concatenate([dxs_pre, db_pre, dc_pre], axis=1)
    dconv_in, dcw0, dcb0 = _conv_bwd("l0_conv_bwd", proj, DI, CD, Wf["ssd_conv_w"][0], dxbc_pre, seg_taps)
    ddt = ddtc[:, :, 0].T + ddtr[:, 0, :].T
    (ddt_raw,), (ddt_bias,) = _rw_bwd("l0_softplus_bwd", _f_softplus, [dt_raw], [dt_bias], [ddt],
                                      row_grad=[True], par_grad=[True])
    dproj = jnp.concatenate([jnp.pad(dz_lat, ((Lc, 0), (0, 0))), dconv_in, ddt_raw], axis=1)
    da0 = _mm(dproj, w_in, tb=True, name="l0_w_in_dx")
    dw_in = _mm(a0, dproj, ta=True, name="l0_w_in_dw")
    token = start_reduce("in", [("ssd_w_in", dw_in, 1)], da0)
    da0 = lax.optimization_barrier((da0, token))[0]
    (dhcat,), (dn1w0, dsc_seg, dsh_seg) = _rw_bwd(
        "l0_modnorm1_bwd", _f_modnorm, [hcat], [n1w0, sc_seg, sh_seg], [da0],
        row_grad=[True], par_grad=[True, True, True], seg_rows=(Lc,))
    grad_x = (dhcat[Lc:] + dh1)[None]

    da_heads = da_parts[:, 0, 0].reshape(2, G, T0 // SSD_CHUNK, H // G).sum(axis=2).reshape(1, 2, H)
    G_full["ssd_a_log"] = da_heads * (-jnp.exp(ssd_a_log))
    G_full["ssd_dt_bias"] = ddt_bias.reshape(1, 2, H)
    G_full["ssd_d"] = dd_rep.reshape(H, P).sum(axis=1)[None]
    G_full["ssd_norm_w"] = dssd_nw.reshape(1, DI)
    G_full["ssd_conv_w"], G_full["ssd_conv_b"] = dcw0[None], dcb0.reshape(1, CD)
    G_full["norm1_w"] = jnp.stack([dn1w0.reshape(D), dn1w1.reshape(D)])
    G_full["norm2_w"] = jnp.stack([gf0["n2w"], gf1["n2w"]])

    zD = jnp.zeros((D,), F32)
    dm_own = jnp.stack([
        jnp.concatenate([dsh_seg[1, 0], dsc_seg[1, 0], dg1_0.reshape(D), gf0["sh2"], gf0["sc2"], gf0["g2"]]),
        jnp.concatenate([dsh1_1.reshape(D), dsc1_1.reshape(D), dg1_1.reshape(D), gf1["sh2"], gf1["sc2"], gf1["g2"]]),
    ])
    dmc_own = jnp.concatenate([dsh_seg[0, 0], dsc_seg[0, 0], zD, zD, zD, zD])

    small_sum_names = [n for n in SMALL if n not in ("c_ctx", "mod_b")]
    sum_part = [G_full[n] for n in small_sum_names] + [dmc_own]
    n_sum = sum(int(a.size) for a in sum_part)
    packed = _pack(sum_part + [dm_own])
    gat = _allgather8("gather_small_grads", packed)
    total = _sum_leading("sum_small_grads", gat, tuple(range(N_DEV)))
    summed = _unpack(total, [a.shape for a in sum_part])
    Gs = dict(zip(small_sum_names, summed[:-1]))
    dmc_tot = summed[-1]
    dm_all = jnp.stack([gat[k].reshape(-1)[n_sum:n_sum + 2 * 6 * D].reshape(2, 6 * D) for k in range(N_DEV)], axis=1)
    dm16 = jnp.concatenate([dm_all, jnp.stack([dmc_tot, jnp.zeros_like(dmc_tot)])[:, None, :],
                            jnp.zeros((2, 16 - N_DEV - 1, 6 * D), F32)], axis=1)
    Gs["mod_b"] = _sum_leading("sum_mod_b", dm16.transpose(1, 0, 2).reshape(16, 2 * 6 * D // LANE, LANE),
                               tuple(range(N_DEV + 1))).reshape(2, 6 * D)

    dm16_shard = lax.dynamic_slice_in_dim(dm16, chip * S_mod, S_mod, axis=2)
    ds16 = _mm(dm16_shard[0], mod_w[0], tb=True, precision=HIGHEST, name="c_ctx_dx")
    sig = jax.nn.sigmoid(c_ctx)
    dcc_part = ds16[N_DEV] * (sig * (1.0 + c_ctx * (1.0 - sig)))
    gat_cc = _allgather8("gather_c_ctx_grad", _pack([dcc_part]))
    Gs["c_ctx"] = _sum_leading("sum_c_ctx_grad", gat_cc, (0, 2, 4, 6)).reshape(-1)[:D]

    s16t = _silu(c16).T
    out = {}
    out["mod_w"] = _mod_w_update(s16t, dm16_shard, mod_w, m_mod_w, v_mod_w)

    late = out["mod_w"][0]
    partial = {}
    for tag, (names, handle, blocks) in reduces.items():
        landed = _exchange4_wait("reduce_" + tag + "_wait", handle, late)
        for n, blk, own in zip(names, landed, blocks):
            r = _fill_own(blk, own, chip, False)
            partial[n] = _sum_leading("sum4_" + n, r.reshape(N_CHIPS, -1, r.shape[-1]), (0, 1, 2, 3)).reshape(r.shape[1:])
    for n in ("ffn_w_up", "ffn_w_down"):
        partial[n] = jnp.stack([partial.pop(n + "0"), partial.pop(n + "1")])
    partial = [partial[n].reshape(W[n].shape) for n in BIG]
    sibling = _swap_sibling("swap_grads", partial)
    for n, mine, sib in zip(BIG, partial, sibling):
        out[n] = _adamw("adamw_" + n, W[n], Mo[n], Vo[n], mine, sib)

    def own(n, full):
        if n in SHARD_AXIS:
            size = W[n].shape[SHARD_AXIS[n]]
            return lax.dynamic_slice_in_dim(full, chip * size, size, axis=SHARD_AXIS[n])
        return full

    g_small = [own(n, Gs[n].reshape(Wf[n].shape)) for n in SMALL]
    shapes = [W[n].shape for n in SMALL]
    pk = [_pack([W[n] for n in SMALL]), _pack([Mo[n] for n in SMALL]), _pack([Vo[n] for n in SMALL]), _pack(g_small)]
    res = _adamw("adamw_small", pk[0], pk[1], pk[2], pk[3], jnp.zeros_like(pk[3]))
    unpacked = [_unpack(r, shapes) for r in res]
    for k, n in enumerate(SMALL):
        out[n] = tuple(u[k] for u in unpacked)

    grads = [out[n][0] for n in WEIGHTS]
    deltas = [out[n][1] for n in WEIGHTS]
    new_m = [out[n][2] for n in WEIGHTS]
    new_v = [out[n][3] for n in WEIGHTS]
    return (loss, grad_x, *grads, *deltas, *new_m, *new_v)
```

```python
import functools

import jax
import jax.numpy as jnp
from jax import lax
from jax.experimental import pallas as pl
from jax.experimental.pallas import tpu as pltpu

F32 = jnp.float32
BF16 = jnp.bfloat16
MESH = pl.DeviceIdType.MESH
HIGHEST = lax.Precision.HIGHEST

VMEM_LIMIT_BYTES = 48 * 1024 * 1024
LANE = 128
SUBLANE = 8

SSD_STATE = 128
SSD_CHUNK = 128
GRID_W = 64
EPS = 1e-6
N_CHIPS = 4
N_DEV = 8

ADAM_LR = 0.001
ADAM_B1 = 0.9
ADAM_B2 = 0.999
ADAM_EPS = 1e-08
ADAM_WD = 0.01
ADAM_STEP = 10


def _pcall(body, **kw):
    return pl.pallas_call(body, **kw)


def _cparams(n_grid):
    return pltpu.CompilerParams(dimension_semantics=("arbitrary",) * n_grid, vmem_limit_bytes=VMEM_LIMIT_BYTES)


def _cdiv(a, b):
    return -(-a // b)


def _round_up(a, b):
    return _cdiv(a, b) * b


def _tile(n, cap):
    if n <= cap:
        return n
    best = None
    for t in range(LANE, cap + 1, LANE):
        if n % t == 0:
            best = t
    if best is None:
        npad = _round_up(n, LANE)
        for t in range(LANE, cap + 1, LANE):
            if npad % t == 0:
                best = t
    return best


def _row_tile(n, cap, also=()):
    best = None
    for t in range(SUBLANE, min(cap, n) + 1, SUBLANE):
        if n % t == 0 and all(a % t == 0 for a in also):
            best = t
    assert best is not None, (n, cap, also)
    return best


def _silu(v):
    return v * jax.nn.sigmoid(v)


def _mm(a, b, *, name, ta=False, tb=False, precision=None, cap=1024):
    M, K = (a.shape[1], a.shape[0]) if ta else a.shape
    N = b.shape[0] if tb else b.shape[1]
    assert K == (b.shape[1] if tb else b.shape[0]), (a.shape, b.shape, ta, tb)
    tm, tn, tk = _tile(M, cap), _tile(N, cap), _tile(K, cap)
    nm, nn, nk = _cdiv(M, tm), _cdiv(N, tn), _cdiv(K, tk)
    k_tail = K % tk
    exact = precision is not None

    def body(a_ref, b_ref, o_ref, acc_ref):
        k = pl.program_id(2)

        @pl.when(k == 0)
        def _():
            acc_ref[...] = jnp.zeros_like(acc_ref)

        av = a_ref[...]
        bv = b_ref[...]
        if k_tail:
            lim = K - k * tk
            ka = lax.broadcasted_iota(jnp.int32, av.shape, 0 if ta else 1)
            kb = lax.broadcasted_iota(jnp.int32, bv.shape, 1 if tb else 0)
            av = jnp.where(ka < lim, av, jnp.zeros_like(av))
            bv = jnp.where(kb < lim, bv, jnp.zeros_like(bv))
        if exact:
            av = av.astype(F32)
            bv = bv.astype(F32)
        else:
            av = av.astype(BF16)
            bv = bv.astype(BF16)
        dn = (((0 if ta else 1,), (1 if tb else 0,)), ((), ()))
        acc_ref[...] += lax.dot_general(av, bv, dn, preferred_element_type=F32, precision=precision)

        @pl.when(k == nk - 1)
        def _():
            o_ref[...] = acc_ref[...]

    a_spec = pl.BlockSpec((tk, tm), lambda i, j, k: (k, i)) if ta else pl.BlockSpec((tm, tk), lambda i, j, k: (i, k))
    b_spec = pl.BlockSpec((tn, tk), lambda i, j, k: (j, k)) if tb else pl.BlockSpec((tk, tn), lambda i, j, k: (k, j))
    return _pcall(
        body, name=name, grid=(nm, nn, nk), in_specs=[a_spec, b_spec],
        out_specs=pl.BlockSpec((tm, tn), lambda i, j, k: (i, j)),
        out_shape=jax.ShapeDtypeStruct((M, N), F32),
        scratch_shapes=[pltpu.VMEM((tm, tn), F32)], compiler_params=_cparams(3),
    )(a, b)


def _norm_rows(rows):
    out = []
    for r in rows:
        if not isinstance(r, tuple):
            r = (r,)
        arr, off, width, roff = (r + (0, None, 0)[len(r) - 1:])
        out.append((arr, off, width if width is not None else arr.shape[1], roff))
    return out


def _rw_plan(T, rows, pars, seg_rows, col_tile, tm_cap):
    widths = [r[2] for r in rows]
    wmax = max(widths + [p.shape[-1] for p in pars] + [1])
    if col_tile is not None:
        assert all(w == widths[0] for w in widths) and all(p.shape[-1] == widths[0] for p in pars)
        ncol = widths[0] // col_tile
        assert ncol * col_tile == widths[0]
        wmax = col_tile
    else:
        ncol = 1
    cap = tm_cap if tm_cap is not None else max(SUBLANE, min(256, (256 * 1024) // wmax))
    tm = _row_tile(T, cap, also=tuple(seg_rows) + tuple(r[3] for r in rows if r[3]))
    bounds = tuple(s // tm for s in seg_rows)
    return widths, ncol, tm, bounds


def _rw_specs(rows, pars, ncol, tm, bounds, col_tile):
    def seg(i):
        s = 0
        for b in bounds:
            s = s + (i >= b).astype(jnp.int32)
        return s

    specs = []
    for arr, off, w, roff in rows:
        bw = col_tile if col_tile is not None else w
        assert off % bw == 0 and roff % tm == 0, (off, bw, roff, tm)
        specs.append(pl.BlockSpec((tm, bw), functools.partial(lambda j, i, ob, rb: (i + rb, ob + j),
                                                              ob=off // bw, rb=roff // tm)))
    for p in pars:
        bw = col_tile if col_tile is not None else p.shape[-1]
        if p.shape[0] > 1:
            specs.append(pl.BlockSpec((None, 1, bw), lambda j, i: (seg(i), 0, j)))
        else:
            specs.append(pl.BlockSpec((None, 1, bw), lambda j, i: (0, 0, j)))
    return specs, seg


def _rw_fwd(name, f, rows, pars, out_widths, *, T=None, seg_rows=(), col_tile=None, tm_cap=None):
    rows = _norm_rows(rows)
    T = rows[0][0].shape[0] if T is None else T
    widths, ncol, tm, bounds = _rw_plan(T, rows, pars, seg_rows, col_tile, tm_cap)
    in_specs, _ = _rw_specs(rows, pars, ncol, tm, bounds, col_tile)
    nr, npar, nout = len(rows), len(pars), len(out_widths)

    def body(*refs):
        vals = [r[...] for r in refs[:nr + npar]]
        outs = f(*vals)
        if not isinstance(outs, (tuple, list)):
            outs = (outs,)
        for o_ref, o in zip(refs[nr + npar:], outs):
            o_ref[...] = o.astype(o_ref.dtype)

    out_specs = [pl.BlockSpec((tm, col_tile if col_tile is not None else w), lambda j, i: (i, j)) for w in out_widths]
    res = _pcall(
        body, name=name, grid=(ncol, T // tm), in_specs=in_specs, out_specs=out_specs,
        out_shape=[jax.ShapeDtypeStruct((T, w), F32) for w in out_widths], compiler_params=_cparams(2),
    )(*[r[0] for r in rows], *pars)
    return res if nout > 1 else res[0]


def _rw_bwd(name, f, rows, pars, cots, *, row_grad, par_grad, T=None, seg_rows=(), col_tile=None, tm_cap=None,
            add=None, cot_fn=None):
    rows = _norm_rows(rows)
    cots = _norm_rows(cots)
    T = rows[0][0].shape[0] if T is None else T
    extra = _norm_rows([add]) if add is not None else []
    all_rows = rows + cots + extra
    widths, ncol, tm, bounds = _rw_plan(T, all_rows, pars, seg_rows, col_tile, tm_cap)
    in_specs, seg = _rw_specs(all_rows, pars, ncol, tm, bounds, col_tile)
    nr, nc, ne, npar = len(rows), len(cots), len(extra), len(pars)
    row_idx = [k for k in range(nr) if row_grad[k]]
    par_idx = [k for k in range(npar) if par_grad[k]]

    def body(*refs):
        i = pl.program_id(1)
        row_vals = [r[...] for r in refs[:nr]]
        cot_vals = [r[...] for r in refs[nr:nr + nc]]
        add_vals = [r[...] for r in refs[nr + nc:nr + nc + ne]]
        par_vals = [r[...] for r in refs[nr + nc + ne:nr + nc + ne + npar]]
        out_refs = refs[nr + nc + ne + npar:]
        outs, vjp = jax.vjp(f, *row_vals, *par_vals)
        if cot_fn is not None:
            cot_vals = cot_fn(*cot_vals)
            if not isinstance(cot_vals, (tuple, list)):
                cot_vals = (cot_vals,)
        if isinstance(outs, (tuple, list)):
            grads = vjp(tuple(c.astype(o.dtype) for c, o in zip(cot_vals, outs)))
        else:
            grads = vjp(cot_vals[0].astype(outs.dtype))
        first_seg = i == 0
        for b in bounds:
            first_seg = first_seg | (i == b)
        for n, k in enumerate(row_idx):
            g = grads[k]
            if n == 0 and add_vals:
                g = g + add_vals[0]
            out_refs[n][...] = g
        for n, k in enumerate(par_idx):
            g = grads[nr + k]
            o_ref = out_refs[len(row_idx) + n]
            first = first_seg if pars[k].shape[0] > 1 else (i == 0)

            @pl.when(first)
            def _(o_ref=o_ref, g=g):
                o_ref[...] = g

            @pl.when(jnp.logical_not(first))
            def _(o_ref=o_ref, g=g):
                o_ref[...] += g

    out_specs, out_shape = [], []
    for k in row_idx:
        w = widths[k]
        out_specs.append(pl.BlockSpec((tm, col_tile if col_tile is not None else w), lambda j, i: (i, j)))
        out_shape.append(jax.ShapeDtypeStruct((T, w), F32))
    for k in par_idx:
        p = pars[k]
        bw = col_tile if col_tile is not None else p.shape[-1]
        if p.shape[0] > 1:
            out_specs.append(pl.BlockSpec((None, 1, bw), lambda j, i: (seg(i), 0, j)))
        else:
            out_specs.append(pl.BlockSpec((None, 1, bw), lambda j, i: (0, 0, j)))
        out_shape.append(jax.ShapeDtypeStruct(p.shape, F32))
    res = _pcall(
        body, name=name, grid=(ncol, T // tm), in_specs=in_specs, out_specs=out_specs, out_shape=out_shape,
        compiler_params=_cparams(2),
    )(*[r[0] for r in all_rows], *pars)
    return list(res[:len(row_idx)]), list(res[len(row_idx):])


def _f_modnorm(h, w, sc, sh):
    y = h * lax.rsqrt(jnp.mean(h * h, axis=-1, keepdims=True) + EPS)
    return (y * w) * (1.0 + sc) + sh


def _f_gate_res(h, y, g):
    return h + g * y


def _f_gate_res_bias(h, y, g, b):
    return h + g * (y + b)


def _f_ffn_act(val, gate):
    return _silu(gate) * val


def _f_softplus(raw, bias):
    v = raw + bias
    return jnp.maximum(v, 0.0) + jnp.log(1.0 + jnp.exp(-jnp.abs(v)))


def _f_ssd_gate(yf, yb, xs, z, d_rep, nw):
    y = (yf + yb + d_rep * xs) * _silu(z)
    return (y * lax.rsqrt(jnp.mean(y * y, axis=-1, keepdims=True) + EPS)) * nw


def _f_glu(a, g, ba, bg):
    return (a + ba) * jax.nn.sigmoid(g + bg)


def _f_ln_silu(h, w, b):
    mu = jnp.mean(h, axis=-1, keepdims=True)
    d = h - mu
    y = d * lax.rsqrt(jnp.mean(d * d, axis=-1, keepdims=True) + EPS)
    return _silu(y * w + b)


def _f_loss_rows(h, t, w):
    y = (h * lax.rsqrt(jnp.mean(h * h, axis=-1, keepdims=True) + EPS)) * w
    e = y - t
    return 0.5 * jnp.mean(e * e, axis=-1, keepdims=True)


def _f_adamw(w, m, v, ga, gb):
    g = ga + gb
    m = ADAM_B1 * m + (1.0 - ADAM_B1) * g
    v = ADAM_B2 * v + (1.0 - ADAM_B2) * (g * g)
    m_hat = m / (1.0 - ADAM_B1 ** ADAM_STEP)
    v_hat = v / (1.0 - ADAM_B2 ** ADAM_STEP)
    delta = -ADAM_LR * (m_hat / (jnp.sqrt(v_hat) + ADAM_EPS) + ADAM_WD * w)
    return g, delta, m, v


def _adamw(name, w, m, v, ga, gb):
    shape = w.shape
    c = shape[-1]
    two_d = [t.reshape(-1, c) for t in (w, m, v, ga, gb)]
    rows = two_d[0].shape[0]
    pad = _round_up(rows, SUBLANE) - rows
    if pad:
        two_d = [jnp.pad(t, ((0, pad), (0, 0))) for t in two_d]
    outs = _rw_fwd(name, _f_adamw, two_d, [], [c] * 4)
    return tuple(o[:rows].reshape(shape) for o in outs)


def _sum_leading(name, x, idxs):
    _, R, C = x.shape
    tm = _row_tile(R, max(SUBLANE, min(512, (512 * 1024) // C)))

    def body(x_ref, o_ref):
        acc = x_ref[idxs[0]].astype(F32)
        for k in idxs[1:]:
            acc = acc + x_ref[k].astype(F32)
        o_ref[...] = acc

    return _pcall(
        body, name=name, grid=(R // tm,), in_specs=[pl.BlockSpec((x.shape[0], tm, C), lambda i: (0, i, 0))],
        out_specs=pl.BlockSpec((tm, C), lambda i: (i, 0)), out_shape=jax.ShapeDtypeStruct((R, C), F32),
        compiler_params=_cparams(1),
    )(x)


def _loss_fwd(h, t, w):
    T, D = h.shape
    tm = _row_tile(T, 256)

    def body(h_ref, t_ref, w_ref, o_ref):
        i = pl.program_id(0)
        part = jnp.sum(_f_loss_rows(h_ref[...], t_ref[...], w_ref[...]), axis=0, keepdims=True)
        part = jnp.broadcast_to(part, (1, LANE))

        @pl.when(i == 0)
        def _():
            o_ref[...] = part

        @pl.when(i > 0)
        def _():
            o_ref[...] += part

    return _pcall(
        body, name="loss_fwd", grid=(T // tm,),
        in_specs=[pl.BlockSpec((tm, D), lambda i: (i, 0)), pl.BlockSpec((tm, D), lambda i: (i, 0)),
                  pl.BlockSpec((1, D), lambda i: (0, 0))],
        out_specs=pl.BlockSpec((1, LANE), lambda i: (0, 0)), out_shape=jax.ShapeDtypeStruct((1, LANE), F32),
        compiler_params=_cparams(1),
    )(h, t, w)


CONV_ROWS = 256
CONV_ACC_ELEMS = 16384


def _tap_mask(mask, t, s):
    if mask is None:
        return None
    kind, arg = mask
    if kind == "seg":
        if s == 0:
            return None
        return (t >= arg) == ((t + s) >= arg)
    col = jnp.bitwise_and(t, GRID_W - 1)
    return (col != 0) if arg < 0 else (col != GRID_W - 1)


def _conv_plan(T, C, taps):
    rc = CONV_ROWS if T % CONV_ROWS == 0 else LANE
    assert T % rc == 0
    ct = next((t for t in (512, 256, LANE) if C % t == 0), C)
    reach = max(abs(s) for s, _ in taps)
    hb = next(h for h in (8, 16, 32, 64, 128, 256) if h >= reach and rc % h == 0)
    sub = max(SUBLANE, min(rc, CONV_ACC_ELEMS // ct))
    return rc, ct, hb, sub, T // rc, C // ct


def _halo_specs(rc, ct, hb, T, off_blocks):
    per = rc // hb
    last = T // hb - 1
    prev = pl.BlockSpec((hb, ct), lambda j, i: (jnp.maximum(i * per - 1, 0), off_blocks + j))
    cur = pl.BlockSpec((rc, ct), lambda j, i: (i, off_blocks + j))
    nxt = pl.BlockSpec((hb, ct), lambda j, i: (jnp.minimum((i + 1) * per, last), off_blocks + j))
    return [prev, cur, nxt]


def _fill_halo(pad_ref, p_ref, c_ref, n_ref, i, nrc, rc, hb):
    pad_ref[0:hb, :] = jnp.where(i > 0, p_ref[...], 0.0)
    pad_ref[hb:hb + rc, :] = c_ref[...]
    pad_ref[hb + rc:hb + rc + hb, :] = jnp.where(i < nrc - 1, n_ref[...], 0.0)


def _conv_fwd(name, u, col_off, C, w, b, taps, act=False):
    T = u.shape[0]
    rc, ct, hb, sub, nrc, ncc = _conv_plan(T, C, taps)
    assert col_off % ct == 0
    K = len(taps)

    def body(up, uc, un, w_ref, b_ref, *rest):
        y_ref = rest[0]
        pad_ref = rest[-1]
        i = pl.program_id(1)
        _fill_halo(pad_ref, up, uc, un, i, nrc, rc, hb)
        for r0 in range(0, rc, sub):
            t = i * rc + r0 + lax.broadcasted_iota(jnp.int32, (sub, 1), 0)
            acc = jnp.broadcast_to(b_ref[...], (sub, ct))
            for k, (s, mask) in enumerate(taps):
                v = pad_ref[hb + r0 + s:hb + r0 + s + sub, :]
                m = _tap_mask(mask, t, s)
                if m is not None:
                    v = jnp.where(m, v, 0.0)
                acc = acc + w_ref[k:k + 1, :] * v
            y_ref[r0:r0 + sub, :] = acc
            if act:
                rest[1][r0:r0 + sub, :] = _silu(acc)

    n_out = 2 if act else 1
    res = _pcall(
        body, name=name, grid=(ncc, nrc),
        in_specs=_halo_specs(rc, ct, hb, T, col_off // ct) + [pl.BlockSpec((K, ct), lambda j, i: (0, j)),
                                                              pl.BlockSpec((1, ct), lambda j, i: (0, j))],
        out_specs=[pl.BlockSpec((rc, ct), lambda j, i: (i, j))] * n_out,
        out_shape=[jax.ShapeDtypeStruct((T, C), F32)] * n_out,
        scratch_shapes=[pltpu.VMEM((rc + 2 * hb, ct), F32)], compiler_params=_cparams(2),
    )(u, u, u, w, b)
    return res if act else res[0]


def _conv_bwd(name, u, col_off, C, w, g, taps):
    T = u.shape[0]
    rc, ct, hb, sub, nrc, ncc = _conv_plan(T, C, taps)
    K = len(taps)

    def body(up, uc, un, gp, gc, gn, w_ref, du_ref, dw_ref, db_ref, upad, gpad):
        i = pl.program_id(1)
        _fill_halo(upad, up, uc, un, i, nrc, rc, hb)
        _fill_halo(gpad, gp, gc, gn, i, nrc, rc, hb)

        @pl.when(i == 0)
        def _():
            dw_ref[...] = jnp.zeros_like(dw_ref)
            db_ref[...] = jnp.zeros_like(db_ref)

        def fold(v):
            return jnp.sum(v.reshape(sub // SUBLANE, SUBLANE, ct), axis=0)

        dws = [jnp.zeros((SUBLANE, ct), F32) for _ in range(K)]
        dbs = jnp.zeros((SUBLANE, ct), F32)
        for r0 in range(0, rc, sub):
            t = i * rc + r0 + lax.broadcasted_iota(jnp.int32, (sub, 1), 0)
            gv = gpad[hb + r0:hb + r0 + sub, :]
            dbs = dbs + fold(gv)
            acc = jnp.zeros((sub, ct), F32)
            for k, (s, mask) in enumerate(taps):
                gs = gpad[hb + r0 - s:hb + r0 - s + sub, :]
                m = _tap_mask(mask, t - s, s)
                if m is not None:
                    gs = jnp.where(m, gs, 0.0)
                acc = acc + w_ref[k:k + 1, :] * gs
                uv = upad[hb + r0 + s:hb + r0 + s + sub, :]
                m = _tap_mask(mask, t, s)
                prod = gv * uv
                if m is not None:
                    prod = jnp.where(m, prod, 0.0)
                dws[k] = dws[k] + fold(prod)
            du_ref[r0:r0 + sub, :] = acc
        for k in range(K):
            dw_ref[k:k + 1, :] += jnp.sum(dws[k], axis=0, keepdims=True)
        db_ref[...] += jnp.sum(dbs, axis=0, keepdims=True)

    halo_u = _halo_specs(rc, ct, hb, T, col_off // ct)
    halo_g = _halo_specs(rc, ct, hb, T, 0)
    return _pcall(
        body, name=name, grid=(ncc, nrc),
        in_specs=halo_u + halo_g + [pl.BlockSpec((K, ct), lambda j, i: (0, j))],
        out_specs=[pl.BlockSpec((rc, ct), lambda j, i: (i, j)), pl.BlockSpec((K, ct), lambda j, i: (0, j)),
                   pl.BlockSpec((1, ct), lambda j, i: (0, j))],
        out_shape=[jax.ShapeDtypeStruct((T, C), F32), jax.ShapeDtypeStruct((K, C), F32),
                   jax.ShapeDtypeStruct((1, C), F32)],
        scratch_shapes=[pltpu.VMEM((rc + 2 * hb, ct), F32), pltpu.VMEM((rc + 2 * hb, ct), F32)],
        compiler_params=_cparams(2),
    )(u, u, u, g, g, g, w)


def _ssd_group(xg, bm, cm, s_in, *per_head, sgn, P):
    R = len(per_head) // 3
    dtcs, dtrs, a_s = per_head[:R], per_head[R:2 * R], per_head[2 * R:]
    q, rp = xg.shape
    ii = lax.broadcasted_iota(jnp.int32, (q, q), 0)
    jj = lax.broadcasted_iota(jnp.int32, (q, q), 1)
    causal = ((jj - ii) * sgn) <= 0
    causal_t = ((ii - jj) * sgn) <= 0
    lane = lax.broadcasted_iota(jnp.int32, (1, rp), 1)
    row = lax.broadcasted_iota(jnp.int32, (rp, 1), 0)
    nt = (((1,), (1,)), ((), ()))
    tn = (((0,), (0,)), ((), ()))
    cb = lax.dot_general(cm.astype(BF16), bm.astype(BF16), nt, preferred_element_type=F32)
    dt_x = jnp.zeros((q, rp), F32)
    acum_x = jnp.zeros((q, rp), F32)
    tot_row = jnp.zeros((1, rp), F32)
    tot_col = jnp.zeros((rp, 1), F32)
    wts, lane_masks = [], []
    for r in range(R):
        hm = (lane >= r * P) & (lane < (r + 1) * P)
        hc = (row >= r * P) & (row < (r + 1) * P)
        dac = dtcs[r] * a_s[r]
        dar = dtrs[r] * a_s[r]
        acum_c = jnp.sum(jnp.where(causal, dar, 0.0), axis=1, keepdims=True)
        acum_r = jnp.sum(jnp.where(causal_t, dac, 0.0), axis=0, keepdims=True)
        decay = jnp.where(causal, jnp.exp(jnp.where(causal, acum_c - acum_r, 0.0)), 0.0)
        tot = jnp.sum(dac, axis=0, keepdims=True)
        dt_x = jnp.where(hm, dtcs[r], dt_x)
        acum_x = jnp.where(hm, acum_c, acum_x)
        tot_row = jnp.where(hm, tot, tot_row)
        tot_col = jnp.where(hc, tot, tot_col)
        wts.append((cb * decay).astype(BF16))
        lane_masks.append(hm)
    xdt = xg * dt_x
    xdt_b = xdt.astype(BF16)
    y = jnp.zeros((q, rp), F32)
    for r in range(R):
        y = jnp.where(lane_masks[r], jnp.dot(wts[r], xdt_b, preferred_element_type=F32), y)
    dte = jnp.exp(tot_row - acum_x)
    cs = lax.dot_general((xdt * dte).astype(BF16), bm.astype(BF16), tn, preferred_element_type=F32)
    y = y + lax.dot_general(cm.astype(BF16), s_in.astype(BF16), nt, preferred_element_type=F32) * jnp.exp(acum_x)
    s_out = jnp.exp(tot_col) * s_in + cs
    return y, s_out


def _ssd_maps(NC, ncc, reverse_steps):
    def chunk(d, s):
        if reverse_steps:
            s = NC - 1 - s
        return jnp.where(d == 0, s, jnp.where(s < ncc, ncc - 1 - s, NC - 1 - s + ncc))

    def lat_chunk(d, s):
        c = chunk(d, s) - ncc
        return jnp.where(c < 0, jnp.where(d == 0, 0, NC - ncc - 1), c)

    def step(s):
        return NC - 1 - s if reverse_steps else s

    return chunk, lat_chunk, step


def _ssd_specs(chunk, H, R, Q, N, RP, bo, co):
    return [
        pl.BlockSpec((Q, RP), lambda d, g, s: (chunk(d, s), g)),
        pl.BlockSpec((Q, N), lambda d, g, s: (chunk(d, s), bo + g)),
        pl.BlockSpec((Q, N), lambda d, g, s: (chunk(d, s), co + g)),
        pl.BlockSpec((R, Q, 1), lambda d, g, s: (d * (H // R) + g, chunk(d, s), 0)),
        pl.BlockSpec((R, 1, Q), lambda d, g, s: (d * (H // R) + g, 0, chunk(d, s))),
        pl.BlockSpec((R, 1, 1), lambda d, g, s: (d * (H // R) + g, 0, 0)),
    ]


def _ssd_fwd(xbc, b_off, c_off, dtc, dtr, a, H, P, ncc):
    T = xbc.shape[0]
    N, Q = SSD_STATE, SSD_CHUNK
    NC = T // Q
    G = (c_off - b_off) // N
    R = H // G
    RP = R * P
    chunk, lat_chunk, _ = _ssd_maps(NC, ncc, False)

    def body(x_ref, b_ref, c_ref, dtc_ref, dtr_ref, a_ref, y_ref, se_ref, s_ref):
        d, s = pl.program_id(0), pl.program_id(2)

        @pl.when(s == 0)
        def _():
            s_ref[...] = jnp.zeros_like(s_ref)

        s_in = s_ref[...]
        se_ref[...] = s_in
        per_head = ([dtc_ref[r] for r in range(R)] + [dtr_ref[r] for r in range(R)] + [a_ref[r] for r in range(R)])
        y, s_out = _ssd_group(x_ref[...], b_ref[...], c_ref[...], s_in, *per_head, sgn=1 - 2 * d, P=P)
        y_ref[...] = y
        s_ref[...] = s_out

    return _pcall(
        body, name="ssd_fwd", grid=(2, G, NC),
        in_specs=_ssd_specs(chunk, H, R, Q, N, RP, b_off // N, c_off // N),
        out_specs=[
            pl.BlockSpec((None, Q, RP), lambda d, g, s: (d, lat_chunk(d, s), g)),
            pl.BlockSpec((None, None, None, RP, N), lambda d, g, s: (d, g, s, 0, 0)),
        ],
        out_shape=[jax.ShapeDtypeStruct((2, T - ncc * Q, H * P), F32),
                   jax.ShapeDtypeStruct((2, G, NC, RP, N), F32)],
        scratch_shapes=[pltpu.VMEM((RP, N), F32)], compiler_params=_cparams(3),
    )(xbc, xbc, xbc, dtc, dtr, a)


def _ssd_bwd(xbc, b_off, c_off, dtc, dtr, a, s_enter, dy, H, P, ncc):
    T = xbc.shape[0]
    N, Q = SSD_STATE, SSD_CHUNK
    NC = T // Q
    G = (c_off - b_off) // N
    R = H // G
    RP = R * P
    chunk, lat_chunk, step = _ssd_maps(NC, ncc, True)

    def body(x_ref, b_ref, c_ref, dtc_ref, dtr_ref, a_ref, se_ref, dy_ref,
             dx_ref, db_ref, dc_ref, ddtc_ref, ddtr_ref, da_ref, ds_ref):
        d, s = pl.program_id(0), pl.program_id(2)

        @pl.when(s == 0)
        def _():
            ds_ref[...] = jnp.zeros_like(ds_ref)

        per_head = ([dtc_ref[r] for r in range(R)] + [dtr_ref[r] for r in range(R)] + [a_ref[r] for r in range(R)])
        f = functools.partial(_ssd_group, sgn=1 - 2 * d, P=P)
        _, vjp = jax.vjp(f, x_ref[...], b_ref[...], c_ref[...], se_ref[...], *per_head)
        is_latent = chunk(d, s) >= ncc
        dy_v = jnp.where(is_latent, dy_ref[...], 0.0)
        grads = vjp((dy_v, ds_ref[...]))
        dx_ref[...] = grads[0]
        db_ref[...] = grads[1]
        dc_ref[...] = grads[2]
        ds_ref[...] = grads[3]
        for r in range(R):
            ddtc_ref[r] = grads[4 + r]
            ddtr_ref[r] = grads[4 + R + r]
            da_ref[r] = jnp.broadcast_to(grads[4 + 2 * R + r], (SUBLANE, LANE))

    return _pcall(
        body, name="ssd_bwd", grid=(2, G, NC),
        in_specs=_ssd_specs(chunk, H, R, Q, N, RP, b_off // N, c_off // N) + [
            pl.BlockSpec((None, None, None, RP, N), lambda d, g, s: (d, g, step(s), 0, 0)),
            pl.BlockSpec((Q, RP), lambda d, g, s: (lat_chunk(d, s), g)),
        ],
        out_specs=[
            pl.BlockSpec((None, Q, RP), lambda d, g, s: (d, chunk(d, s), g)),
            pl.BlockSpec((None, Q, N), lambda d, g, s: (d, chunk(d, s), g)),
            pl.BlockSpec((None, Q, N), lambda d, g, s: (d, chunk(d, s), g)),
            pl.BlockSpec((R, Q, 1), lambda d, g, s: (d * G + g, chunk(d, s), 0)),
            pl.BlockSpec((R, 1, Q), lambda d, g, s: (d * G + g, 0, chunk(d, s))),
            pl.BlockSpec((R, SUBLANE, LANE), lambda d, g, s: ((d * G + g) * NC + s, 0, 0)),
        ],
        out_shape=[
            jax.ShapeDtypeStruct((2, T, H * P), F32), jax.ShapeDtypeStruct((2, T, G * N), F32),
            jax.ShapeDtypeStruct((2, T, G * N), F32), jax.ShapeDtypeStruct((2 * H, T, 1), F32),
            jax.ShapeDtypeStruct((2 * H, 1, T), F32), jax.ShapeDtypeStruct((2 * G * NC * R, SUBLANE, LANE), F32),
        ],
        scratch_shapes=[pltpu.VMEM((RP, N), F32)], compiler_params=_cparams(3),
    )(xbc, xbc, xbc, dtc, dtr, a, s_enter, dy)


def _allgather8(name, v):
    R, C = v.shape

    def body(x_ref, out_ref, send_sems, recv_sems, local_sem):
        x, y, c = lax.axis_index("x"), lax.axis_index("y"), lax.axis_index("c")
        me, sibling = (x, y, c), (x, y, 1 - c)
        chips = [(1 - x, y), (x, 1 - y), (1 - x, 1 - y)]

        def slot(px, py, pc):
            return out_ref.at[4 * px + 2 * py + pc]

        def copy(k, block, to, src=None):
            return pltpu.make_async_remote_copy(
                src_ref=slot(*block) if src is None else src, dst_ref=slot(*block),
                send_sem=send_sems.at[k], recv_sem=recv_sems.at[k], device_id=to, device_id_type=MESH)

        mine = pltpu.make_async_copy(x_ref, slot(*me), local_sem)
        mine.start()
        first = [copy(0, me, sibling, src=x_ref)]
        first += [copy(1 + j, me, (*chip, c), src=x_ref) for j, chip in enumerate(chips)]
        for cp in first:
            cp.start()
        passed = [copy(4 + j, (*chip, c), sibling) for j, chip in enumerate(chips)]
        for j, chip in enumerate(chips):
            copy(1 + j, (*chip, c), me).wait_recv()
            passed[j].start()
        copy(0, sibling, me).wait_recv()
        for j, chip in enumerate(chips):
            copy(4 + j, (*chip, 1 - c), me).wait_recv()
        for cp in first + passed:
            cp.wait_send()
        mine.wait()

    return _pcall(
        body, name=name, out_shape=jax.ShapeDtypeStruct((N_DEV, R, C), v.dtype),
        in_specs=[pl.BlockSpec(memory_space=pltpu.VMEM)], out_specs=pl.BlockSpec(memory_space=pltpu.VMEM),
        scratch_shapes=[pltpu.SemaphoreType.DMA((7,)), pltpu.SemaphoreType.DMA((7,)), pltpu.SemaphoreType.DMA],
        compiler_params=pltpu.CompilerParams(vmem_limit_bytes=VMEM_LIMIT_BYTES),
    )(v)


def _exchange4_start(name, srcs, bcast, dep):
    n = len(srcs)
    lands = [lax.empty(((N_CHIPS,) + s.shape) if bcast else s.shape, s.dtype) for s in srcs]

    def body(*refs):
        src, land = refs[:n], refs[n:2 * n]
        send_sems, recv_sems = refs[2 * n + 1], refs[2 * n + 2]
        token = refs[-1]
        x, y, c = lax.axis_index("x"), lax.axis_index("y"), lax.axis_index("c")
        me = 2 * x + y
        for a in range(n):
            for j, (px, py) in enumerate([(1 - x, y), (x, 1 - y), (1 - x, 1 - y)]):
                pltpu.make_async_remote_copy(
                    src_ref=src[a] if bcast else src[a].at[2 * px + py], dst_ref=land[a].at[me],
                    send_sem=send_sems.at[3 * a + j], recv_sem=recv_sems.at[3 * a + j], device_id=(px, py, c),
                    device_id_type=MESH).start()
        token[...] = jnp.zeros_like(token)

    hbm = pl.BlockSpec(memory_space=pltpu.HBM)
    sem = pl.BlockSpec(memory_space=pltpu.SEMAPHORE)
    outs = _pcall(
        body, name=name,
        out_shape=(pltpu.SemaphoreType.DMA((3 * n,)), pltpu.SemaphoreType.DMA((3 * n,)),
                   *[pltpu.HBM(s.shape, s.dtype) for s in srcs], *[pltpu.HBM(l.shape, l.dtype) for l in lands],
                   jax.ShapeDtypeStruct((SUBLANE, LANE), F32)),
        in_specs=[hbm] * (2 * n) + [pl.BlockSpec(memory_space=pl.ANY)],
        out_specs=(sem, sem, *[hbm] * (2 * n), pl.BlockSpec(memory_space=pltpu.VMEM)),
        input_output_aliases={k: 2 + k for k in range(2 * n)},
        compiler_params=pltpu.CompilerParams(has_side_effects=pltpu.SideEffectType.DATAFLOW_SIDE_EFFECTING),
    )(*[pltpu.with_memory_space_constraint(s, pltpu.HBM) for s in srcs],
      *[pltpu.with_memory_space_constraint(l, pltpu.HBM) for l in lands], dep)
    return (n, bcast, outs[0], outs[1], outs[2:2 + n], outs[2 + n:2 + 2 * n]), outs[-1]


def _exchange4_wait(name, handle, after):
    n, bcast, send_sems, recv_sems, src_thru, land_thru = handle

    def body(*refs):
        src, land = refs[:n], refs[n:2 * n]
        send_sems, recv_sems = refs[2 * n], refs[2 * n + 1]
        x, y, c = lax.axis_index("x"), lax.axis_index("y"), lax.axis_index("c")
        for a in range(n):
            for j, (px, py) in enumerate([(1 - x, y), (x, 1 - y), (1 - x, 1 - y)]):
                pk = 2 * px + py
                copy = pltpu.make_async_remote_copy(
                    src_ref=src[a] if bcast else src[a].at[pk], dst_ref=land[a].at[pk],
                    send_sem=send_sems.at[3 * a + j], recv_sem=recv_sems.at[3 * a + j], device_id=(px, py, c),
                    device_id_type=MESH)
                copy.wait_send()
                copy.wait_recv()

    hbm = pl.BlockSpec(memory_space=pltpu.HBM)
    sem = pl.BlockSpec(memory_space=pltpu.SEMAPHORE)
    outs = _pcall(
        body, name=name,
        out_shape=tuple(pltpu.HBM(t.shape, t.dtype) for t in (*src_thru, *land_thru)),
        in_specs=[hbm] * (2 * n) + [sem, sem, pl.BlockSpec(memory_space=pl.ANY)], out_specs=tuple([hbm] * (2 * n)),
        input_output_aliases={k: k for k in range(2 * n)},
        compiler_params=pltpu.CompilerParams(has_side_effects=pltpu.SideEffectType.DATAFLOW_SIDE_EFFECTING),
    )(*src_thru, *land_thru, send_sems, recv_sems, after)
    return list(outs[n:])


def _tie(name, v, token):
    def body(v_ref, token_ref, o_ref):
        del v_ref, token_ref, o_ref

    any_spec = pl.BlockSpec(memory_space=pl.ANY)
    return _pcall(body, name=name, out_shape=jax.ShapeDtypeStruct(v.shape, v.dtype), in_specs=[any_spec, any_spec],
                  out_specs=any_spec, input_output_aliases={0: 0})(v, token)


def _fill_own(landed, own, me, bcast):
    blk = own if bcast else lax.dynamic_index_in_dim(own, me, 0, keepdims=False)
    return lax.dynamic_update_index_in_dim(landed, blk, me, 0)


def _swap_sibling(name, srcs):
    n = len(srcs)

    def body(*refs):
        src, out = refs[:n], refs[n:2 * n]
        send_sems, recv_sems = refs[2 * n:]
        x, y, c = lax.axis_index("x"), lax.axis_index("y"), lax.axis_index("c")
        copies = []
        for a in range(n):
            rc = pltpu.make_async_remote_copy(
                src_ref=src[a], dst_ref=out[a], send_sem=send_sems.at[a], recv_sem=recv_sems.at[a],
                device_id=(x, y, 1 - c), device_id_type=MESH)
            rc.start()
            copies.append(rc)
        for cp in copies:
            cp.wait()

    any_spec = pl.BlockSpec(memory_space=pl.ANY)
    return _pcall(
        body, name=name, out_shape=[jax.ShapeDtypeStruct(s.shape, s.dtype) for s in srcs],
        in_specs=[any_spec] * n, out_specs=[any_spec] * n,
        scratch_shapes=[pltpu.SemaphoreType.DMA((n,)), pltpu.SemaphoreType.DMA((n,))],
    )(*srcs)


def _mod_fwd(c16, mod_w, mod_b_shard):
    nl, D, S = mod_w.shape

    def body(c_ref, w_ref, b_ref, o_ref):
        s = _silu(c_ref[...]).astype(BF16)
        o_ref[...] = jnp.dot(s, w_ref[...].astype(BF16), preferred_element_type=F32) + b_ref[...]

    return _pcall(
        body, name="mod_fwd", grid=(nl,),
        in_specs=[pl.BlockSpec((16, D), lambda l: (0, 0)), pl.BlockSpec((None, D, S), lambda l: (l, 0, 0)),
                  pl.BlockSpec((None, 1, S), lambda l: (l, 0, 0))],
        out_specs=pl.BlockSpec((None, 16, S), lambda l: (l, 0, 0)),
        out_shape=jax.ShapeDtypeStruct((nl, 16, S), F32), compiler_params=_cparams(1),
    )(c16, mod_w, mod_b_shard)


def _mod_w_update(s16t, dm16, w, m, v):
    nl, D, S = w.shape
    tm = _row_tile(D, 256)

    def body(s_ref, dm_ref, w_ref, m_ref, v_ref, g_ref, dl_ref, nm_ref, nv_ref):
        g = jnp.dot(s_ref[...], dm_ref[...], preferred_element_type=F32, precision=HIGHEST)
        g, dl, nm, nv = _f_adamw(w_ref[...], m_ref[...], v_ref[...], g, jnp.zeros_like(g))
        g_ref[...] = g
        dl_ref[...] = dl
        nm_ref[...] = nm
        nv_ref[...] = nv

    big = pl.BlockSpec((None, tm, S), lambda l, i: (l, i, 0))
    return _pcall(
        body, name="mod_w_update", grid=(nl, D // tm),
        in_specs=[pl.BlockSpec((tm, 16), lambda l, i: (i, 0)), pl.BlockSpec((None, 16, S), lambda l, i: (l, 0, 0)),
                  big, big, big],
        out_specs=[big] * 4, out_shape=[jax.ShapeDtypeStruct(w.shape, F32)] * 4, compiler_params=_cparams(2),
    )(s16t, dm16, w, m, v)


def _pack(arrs):
    flat = jnp.concatenate([a.reshape(-1).astype(F32) for a in arrs])
    n = flat.shape[0]
    rows = _round_up(_cdiv(n, LANE), SUBLANE)
    return jnp.pad(flat, (0, rows * LANE - n)).reshape(rows, LANE)


def _unpack(buf, shapes):
    flat = buf.reshape(-1)
    out, pos = [], 0
    for s in shapes:
        n = 1
        for d in s:
            n *= d
        out.append(flat[pos:pos + n].reshape(s))
        pos += n
    return out


SHARD_AXIS = {
    "mod_w": 2, "ssd_w_in": 2, "ssd_conv_w": 2, "ssd_w_out": 1, "conf_w_pw1": 2, "conf_b_pw1": 1, "conf_w_dw": 2,
    "conf_b_dw": 1, "conf_ln_w": 1, "conf_ln_b": 1, "conf_w_pw2": 1, "conf_b_pw2": 1, "ffn_w_up": 2,
    "ffn_conv_w": 3, "ffn_w_down": 1,
}
BIG = ("ssd_w_in", "ssd_w_out", "conf_w_pw1", "conf_w_pw2", "ffn_w_up", "ffn_w_down")
WEIGHTS = ("c_ctx", "mod_w", "mod_b", "norm1_w", "norm2_w", "ssd_w_in", "ssd_conv_w", "ssd_conv_b", "ssd_dt_bias",
           "ssd_a_log", "ssd_d", "ssd_norm_w", "ssd_w_out", "conf_w_pw1", "conf_b_pw1", "conf_w_dw", "conf_b_dw",
           "conf_ln_w", "conf_ln_b", "conf_w_pw2", "conf_b_pw2", "ffn_w_up", "ffn_conv_w", "ffn_conv_b",
           "ffn_w_down", "final_norm_w")
SMALL = tuple(n for n in WEIGHTS if n not in BIG and n != "mod_w")
SMALL_SHARDED = tuple(n for n in SMALL if n in SHARD_AXIS)


def _unshard(stacked, axis):
    return jnp.concatenate([stacked[k] for k in range(N_CHIPS)], axis=axis)


def _to_blocks(full, axis):
    return jnp.stack(jnp.split(full, N_CHIPS, axis=axis))


def _par(v):
    v = v.reshape(-1, v.shape[-1])
    return v[:, None, :]


def kernel(x, c, ctx, c_ctx, mod_w, mod_b, norm1_w, norm2_w, ssd_w_in, ssd_conv_w, ssd_conv_b, ssd_dt_bias, ssd_a_log, ssd_d, ssd_norm_w, ssd_w_out, conf_w_pw1, conf_b_pw1, conf_w_dw, conf_b_dw, conf_ln_w, conf_ln_b, conf_w_pw2, conf_b_pw2, ffn_w_up, ffn_conv_w, ffn_conv_b, ffn_w_down, final_norm_w, loss_target, m_c_ctx, m_mod_w, m_mod_b, m_norm1_w, m_norm2_w, m_ssd_w_in, m_ssd_conv_w, m_ssd_conv_b, m_ssd_dt_bias, m_ssd_a_log, m_ssd_d, m_ssd_norm_w, m_ssd_w_out, m_conf_w_pw1, m_conf_b_pw1, m_conf_w_dw, m_conf_b_dw, m_conf_ln_w, m_conf_ln_b, m_conf_w_pw2, m_conf_b_pw2, m_ffn_w_up, m_ffn_conv_w, m_ffn_conv_b, m_ffn_w_down, m_final_norm_w, v_c_ctx, v_mod_w, v_mod_b, v_norm1_w, v_norm2_w, v_ssd_w_in, v_ssd_conv_w, v_ssd_conv_b, v_ssd_dt_bias, v_ssd_a_log, v_ssd_d, v_ssd_norm_w, v_ssd_w_out, v_conf_w_pw1, v_conf_b_pw1, v_conf_w_dw, v_conf_b_dw, v_conf_ln_w, v_conf_ln_b, v_conf_w_pw2, v_conf_b_pw2, v_ffn_w_up, v_ffn_conv_w, v_ffn_conv_b, v_ffn_w_down, v_final_norm_w):
    given = dict(locals())
    W = {n: given[n] for n in WEIGHTS}
    Mo = {n: given["m_" + n] for n in WEIGHTS}
    Vo = {n: given["v_" + n] for n in WEIGHTS}

    ax, ay, ac = lax.axis_index("x"), lax.axis_index("y"), lax.axis_index("c")
    chip = 2 * ax + ay
    dev = 4 * ax + 2 * ay + ac

    D = x.shape[-1]
    L, Lc = x.shape[1], ctx.shape[1]
    T0 = L + Lc
    H = ssd_a_log.shape[-1]
    DI = ssd_norm_w.shape[-1]
    P = DI // H
    CD = ssd_conv_b.shape[-1]
    N = SSD_STATE
    G = (CD - DI) // (2 * N)
    FH = ffn_conv_b.shape[-1]
    KS = ssd_conv_w.shape[1]
    KC = conf_w_dw.shape[1]
    ncc = Lc // SSD_CHUNK

    shard_b = {n: W[n].astype(BF16) for n in BIG}
    gather_a, token = _exchange4_start("gather_w_in_start", [shard_b["ssd_w_in"]], True, x)
    c = _tie("tie_gather_w_in", c, token)

    small_shard_shapes = [W[n].shape for n in SMALL_SHARDED]
    f1 = _allgather8("gather_small", _pack([c] + [W[n] for n in SMALL_SHARDED]))
    c_rows, full_small = [], {n: [] for n in SMALL_SHARDED}
    for k in range(N_DEV):
        parts = _unpack(f1[k], [c.shape] + small_shard_shapes)
        c_rows.append(parts[0])
        if k % 2 == 0:
            for n, p in zip(SMALL_SHARDED, parts[1:]):
                full_small[n].append(p)
    Wf = dict(W)
    for n in SMALL_SHARDED:
        Wf[n] = jnp.concatenate(full_small[n], axis=SHARD_AXIS[n])
    c16 = jnp.concatenate(c_rows + [c_ctx[None, :], jnp.zeros((16 - N_DEV - 1, D), F32)], axis=0)

    S_mod = mod_w.shape[-1]
    mod_b_shard = lax.dynamic_slice_in_dim(mod_b, chip * S_mod, S_mod, axis=1)[:, None, :]
    mod_part = _mod_fwd(c16, mod_w, mod_b_shard)
    f2 = _allgather8("gather_mod", mod_part.reshape(2 * 16, S_mod))
    mods = jnp.concatenate([f2[2 * k].reshape(2, 16, S_mod) for k in range(N_CHIPS)], axis=-1)
    my = lax.dynamic_slice_in_dim(mods, dev, 1, axis=1)[:, 0]
    sh1, sc1, g1, sh2, sc2, g2 = [[my[l, k * D:(k + 1) * D] for l in range(2)] for k in range(6)]
    csh1, csc1 = mods[0, N_DEV, 0:D], mods[0, N_DEV, D:2 * D]

    def full_weight(n, landed):
        return _unshard(_fill_own(landed, shard_b[n], chip, True), SHARD_AXIS[n])

    xl = x[0]
    hcat = jnp.concatenate([ctx[0], xl], axis=0)
    n1w0, n2w0, n1w1, n2w1 = _par(norm1_w[0]), _par(norm2_w[0]), _par(norm1_w[1]), _par(norm2_w[1])
    sc_seg = jnp.stack([csc1, sc1[0]])[:, None, :]
    sh_seg = jnp.stack([csh1, sh1[0]])[:, None, :]

    a0 = _rw_fwd("l0_modnorm1", _f_modnorm, [hcat], [n1w0, sc_seg, sh_seg], [D], seg_rows=(Lc,))
    (landed_in,) = _exchange4_wait("gather_w_in_wait", gather_a, a0)
    w_in = full_weight("ssd_w_in", landed_in)[0]
    rest = [n for n in BIG if n != "ssd_w_in"]
    gather_b, token = _exchange4_start("gather_rest_start", [shard_b[n] for n in rest], True, landed_in)
    a0 = _tie("tie_gather_rest", a0, token)
    proj = _mm(a0, w_in, name="l0_w_in")
    seg_taps = [(k - KS // 2, ("seg", Lc)) for k in range(KS)]
    xbc_pre, xbc = _conv_fwd("l0_conv", proj, DI, CD, Wf["ssd_conv_w"][0], ssd_conv_b, seg_taps, act=True)
    dt_raw = proj[:, DI + CD:]
    dt_bias = _par(ssd_dt_bias.reshape(1, 2 * H))
    dt = _rw_fwd("l0_softplus", _f_softplus, [dt_raw], [dt_bias], [2 * H])
    dt_t = dt.T
    dtc, dtr = dt_t[:, :, None], dt_t[:, None, :]
    a_neg = -jnp.exp(ssd_a_log.reshape(2 * H, 1, 1))
    y2, s_enter = _ssd_fwd(xbc, DI, DI + G * N, dtc, dtr, a_neg, H, P, ncc)
    y2 = y2.reshape(2 * L, DI)
    gate_rows = [(y2, 0, DI, 0), (y2, 0, DI, L), (xbc, 0, DI, Lc), (proj, 0, DI, Lc)]
    d_rep = _par(jnp.repeat(ssd_d[0], P))
    ssd_nw = _par(ssd_norm_w[0])
    yn = _rw_fwd("l0_ssd_gate", _f_ssd_gate, gate_rows, [d_rep, ssd_nw], [DI], T=L)
    Wb = {n: full_weight(n, g) for n, g in zip(rest, _exchange4_wait("gather_rest_wait", gather_b, yn))}
    w_out, w_pw1, w_pw2 = Wb["ssd_w_out"][0], Wb["conf_w_pw1"][0], Wb["conf_w_pw2"][0]
    w_up, w_dn = Wb["ffn_w_up"], Wb["ffn_w_down"]
    mix0 = _mm(yn, w_out, name="l0_w_out")
    g1_0, g2_0, g1_1, g2_1 = _par(g1[0]), _par(g2[0]), _par(g1[1]), _par(g2[1])
    h1 = _rw_fwd("l0_res1", _f_gate_res, [xl, mix0], [g1_0], [D])

    grid_taps = [((i - 1) * GRID_W + (j - 1), (None if j == 1 else ("col", j - 1))) for i in range(3) for j in range(3)]

    def ffn_fwd(l, h, tag):
        a = _rw_fwd(tag + "_modnorm2", _f_modnorm, [h], [_par(norm2_w[l]), _par(sc2[l]), _par(sh2[l])], [D])
        hh = _mm(a, w_up[l], name=tag + "_w_up")
        gc = _conv_fwd(tag + "_ffn_conv", hh, FH, FH, Wf["ffn_conv_w"][l].reshape(9, FH), ffn_conv_b[l][None, :],
                       grid_taps)
        act = _rw_fwd(tag + "_act", _f_ffn_act, [(hh, 0, FH), gc], [], [FH], col_tile=_tile(FH, 1536))
        dn = _mm(act, w_dn[l], name=tag + "_w_down")
        return a, hh, gc, act, dn

    a1, hh0, gc0, act0, dn0 = ffn_fwd(0, h1, "l0")
    h2 = _rw_fwd("l0_res2", _f_gate_res, [h1, dn0], [g2_0], [D])

    a2 = _rw_fwd("l1_modnorm1", _f_modnorm, [h2], [n1w1, _par(sc1[1]), _par(sh1[1])], [D])
    pw = _mm(a2, w_pw1, name="l1_pw1")
    b_pw1 = Wf["conf_b_pw1"][0]
    glu = _rw_fwd("l1_glu", _f_glu, [(pw, 0, D), (pw, D, D)], [_par(b_pw1[:D]), _par(b_pw1[D:])], [D])
    conf_taps = [(k - KC // 2, None) for k in range(KC)]
    cv = _conv_fwd("l1_conv", glu, 0, D, Wf["conf_w_dw"][0], Wf["conf_b_dw"], conf_taps)
    ln_w, ln_b = _par(Wf["conf_ln_w"][0]), _par(Wf["conf_ln_b"][0])
    ls = _rw_fwd("l1_ln_silu", _f_ln_silu, [cv], [ln_w, ln_b], [D])
    p2 = _mm(ls, w_pw2, name="l1_pw2")
    b_pw2 = _par(Wf["conf_b_pw2"][0])
    h3 = _rw_fwd("l1_res1", _f_gate_res_bias, [h2, p2], [g1_1, b_pw2], [D])
    a3, hh1, gc1, act1, dn1 = ffn_fwd(1, h3, "l1")
    h4 = _rw_fwd("l1_res2", _f_gate_res, [h3, dn1], [g2_1], [D])

    fnw = final_norm_w[None, :]
    tgt = loss_target[0]
    loss_local = _loss_fwd(h4, tgt, fnw)[0, 0]
    loss = lax.psum(loss_local, ("x", "y", "c"))

    G_full = {}
    reduces = {}

    def start_reduce(tag, items, dep):
        blocks = [_to_blocks(g, ax).astype(BF16) for _, g, ax in items]
        handle, tok = _exchange4_start("reduce_" + tag + "_start", blocks, False, dep)
        reduces[tag] = ([n for n, _, _ in items], handle, blocks)
        return tok
    ones = jnp.ones((L, 1), F32)
    (dh4,), (dfnw,) = _rw_bwd("loss_bwd", _f_loss_rows, [h4, tgt], [_par(final_norm_w)], [ones],
                              row_grad=[True, False], par_grad=[True])
    G_full["final_norm_w"] = dfnw.reshape(D)

    def ffn_bwd(l, h, saved, g2_l, dh_out, tag):
        a, hh, gc, act, dn = saved
        (ddn,), (dg2,) = _rw_bwd(tag + "_res2_bwd", _f_gate_res, [h, dn], [g2_l], [dh_out],
                                 row_grad=[False, True], par_grad=[True])
        dact = _mm(ddn, w_dn[l], tb=True, name=tag + "_w_down_dx")
        dwdn = _mm(act, ddn, ta=True, name=tag + "_w_down_dw")
        (dval, dgc), _ = _rw_bwd(tag + "_act_bwd", _f_ffn_act, [(hh, 0, FH), gc], [], [dact],
                                 row_grad=[True, True], par_grad=[], col_tile=_tile(FH, 1536))
        dgin, dcw, dcb = _conv_bwd(tag + "_ffn_conv_bwd", hh, FH, FH, Wf["ffn_conv_w"][l].reshape(9, FH), dgc, grid_taps)
        dhh = jnp.concatenate([dval, dgin], axis=1)
        da = _mm(dhh, w_up[l], tb=True, name=tag + "_w_up_dx")
        dwup = _mm(a, dhh, ta=True, name=tag + "_w_up_dw")
        (dh,), (dn2w, dsc2, dsh2) = _rw_bwd(
            tag + "_modnorm2_bwd", _f_modnorm, [h], [_par(norm2_w[l]), _par(sc2[l]), _par(sh2[l])], [da],
            row_grad=[True], par_grad=[True, True, True], add=dh_out)
        return dh, dict(w_down=dwdn, w_up=dwup, conv_w=dcw.reshape(3, 3, FH), conv_b=dcb.reshape(FH),
                        n2w=dn2w.reshape(D), sc2=dsc2.reshape(D), sh2=dsh2.reshape(D), g2=dg2.reshape(D))

    dh3, gf1 = ffn_bwd(1, h3, (a3, hh1, gc1, act1, dn1), g2_1, dh4, "l1")
    (dp2,), (dg1_1, db_pw2) = _rw_bwd("l1_res1_bwd", _f_gate_res_bias, [h2, p2], [g1_1, b_pw2], [dh3],
                                      row_grad=[False, True], par_grad=[True, True])
    dls = _mm(dp2, w_pw2, tb=True, name="l1_pw2_dx")
    dw_pw2 = _mm(ls, dp2, ta=True, name="l1_pw2_dw")
    (dcv,), (dln_w, dln_b) = _rw_bwd("l1_ln_silu_bwd", _f_ln_silu, [cv], [ln_w, ln_b], [dls],
                                     row_grad=[True], par_grad=[True, True])
    dglu, dw_dw, db_dw = _conv_bwd("l1_conv_bwd", glu, 0, D, Wf["conf_w_dw"][0], dcv, conf_taps)
    (dpa, dpg), (dba, dbg) = _rw_bwd("l1_glu_bwd", _f_glu, [(pw, 0, D), (pw, D, D)],
                                     [_par(b_pw1[:D]), _par(b_pw1[D:])], [dglu],
                                     row_grad=[True, True], par_grad=[True, True])
    dpw = jnp.concatenate([dpa, dpg], axis=1)
    da2 = _mm(dpw, w_pw1, tb=True, name="l1_pw1_dx")
    dw_pw1 = _mm(a2, dpw, ta=True, name="l1_pw1_dw")
    (dh2,), (dn1w1, dsc1_1, dsh1_1) = _rw_bwd(
        "l1_modnorm1_bwd", _f_modnorm, [h2], [n1w1, _par(sc1[1]), _par(sh1[1])], [da2],
        row_grad=[True], par_grad=[True, True, True], add=dh3)
    G_full["conf_b_pw2"] = db_pw2.reshape(1, D)
    G_full["conf_ln_w"], G_full["conf_ln_b"] = dln_w.reshape(1, D), dln_b.reshape(1, D)
    G_full["conf_w_dw"], G_full["conf_b_dw"] = dw_dw[None], db_dw.reshape(1, D)
    G_full["conf_b_pw1"] = jnp.concatenate([dba.reshape(1, D), dbg.reshape(1, D)], axis=1)

    token = start_reduce("l1", [("conf_w_pw2", dw_pw2, 0), ("conf_w_pw1", dw_pw1, 1), ("ffn_w_up1", gf1["w_up"], 1),
                                ("ffn_w_down1", gf1["w_down"], 0)], dw_pw2)
    dh2 = _tie("tie_reduce_l1", dh2, token)
    dh1, gf0 = ffn_bwd(0, h1, (a1, hh0, gc0, act0, dn0), g2_0, dh2, "l0")
    G_full["ffn_conv_w"] = jnp.stack([gf0["conv_w"], gf1["conv_w"]])
    G_full["ffn_conv_b"] = jnp.stack([gf0["conv_b"], gf1["conv_b"]])

    (dmix,), (dg1_0,) = _rw_bwd("l0_res1_bwd", _f_gate_res, [xl, mix0], [g1_0], [dh1],
                                row_grad=[False, True], par_grad=[True])
    dyn = _mm(dmix, w_out, tb=True, name="l0_w_out_dx")
    dw_out = _mm(yn, dmix, ta=True, name="l0_w_out_dw")
    token = start_reduce("l0", [("ffn_w_up0", gf0["w_up"], 1), ("ffn_w_down0", gf0["w_down"], 0),
                                ("ssd_w_out", dw_out, 0)], dw_out)
    dyn = _tie("tie_reduce_l0", dyn, token)
    (dy_lat, dxs_gate, dz_lat), (dd_rep, dssd_nw) = _rw_bwd(
        "l0_ssd_gate_bwd", _f_ssd_gate, gate_rows, [d_rep, ssd_nw], [dyn],
        row_grad=[True, False, True, True], par_grad=[True, True], T=L)
    dx2, dB, dC, ddtc, ddtr, da_parts = _ssd_bwd(xbc, DI, DI + G * N, dtc, dtr, a_neg, s_enter, dy_lat, H, P, ncc)
    dxs_gate_all = jnp.pad(dxs_gate, ((Lc, 0), (0, 0)))
    dx2, dB, dC = dx2.reshape(2 * T0, DI), dB.reshape(2 * T0, G * N), dC.reshape(2 * T0, G * N)
    silu_bwd = functools.partial(_rw_bwd, f=_silu, pars=[], row_grad=[True], par_grad=[], T=T0)
    (dxs_pre,), _ = silu_bwd("l0_silu_bwd_x", rows=[(xbc_pre, 0, DI)], cot_fn=lambda p, q, r: p + q + r,
                             cots=[(dx2, 0, DI, 0), (dx2, 0, DI, T0), dxs_gate_all], col_tile=_tile(DI, 1024))
    (db_pre,), _ = silu_bwd("l0_silu_bwd_b", rows=[(xbc_pre, DI, G * N)], cot_fn=lambda p, q: p + q,
                            cots=[(dB, 0, G * N, 0), (dB, 0, G * N, T0)], col_tile=_tile(G * N, 1024))
    (dc_pre,), _ = silu_bwd("l0_silu_bwd_c", rows=[(xbc_pre, DI + G * N, G * N)], cot_fn=lambda p, q: p + q,
                            cots=[(dC, 0, G * N, 0), (dC, 0, G * N, T0)], col_tile=_tile(G * N, 1024))
    dxbc_pre = jnp.concatenate([dxs_pre, db_pre, dc_pre], axis=1)
    dconv_in, dcw0, dcb0 = _conv_bwd("l0_conv_bwd", proj, DI, CD, Wf["ssd_conv_w"][0], dxbc_pre, seg_taps)
    ddt = ddtc[:, :, 0].T + ddtr[:, 0, :].T
    (ddt_raw,), (ddt_bias,) = _rw_bwd("l0_softplus_bwd", _f_softplus, [dt_raw], [dt_bias], [ddt],
                                      row_grad=[True], par_grad=[True])
    dproj = jnp.concatenate([jnp.pad(dz_lat, ((Lc, 0), (0, 0))), dconv_in, ddt_raw], axis=1)
    da0 = _mm(dproj, w_in, tb=True, name="l0_w_in_dx")
    dw_in = _mm(a0, dproj, ta=True, name="l0_w_in_dw")
    token = start_reduce("in", [("ssd_w_in", dw_in, 1)], dw_in)
    da0 = _tie("tie_reduce_in", da0, token)
    (dhcat,), (dn1w0, dsc_seg, dsh_seg) = _rw_bwd(
        "l0_modnorm1_bwd", _f_modnorm, [hcat], [n1w0, sc_seg, sh_seg], [da0],
        row_grad=[True], par_grad=[True, True, True], seg_rows=(Lc,))
    grad_x = (dhcat[Lc:] + dh1)[None]

    da_heads = da_parts[:, 0, 0].reshape(2, G, T0 // SSD_CHUNK, H // G).sum(axis=2).reshape(1, 2, H)
    G_full["ssd_a_log"] = da_heads * (-jnp.exp(ssd_a_log))
    G_full["ssd_dt_bias"] = ddt_bias.reshape(1, 2, H)
    G_full["ssd_d"] = dd_rep.reshape(H, P).sum(axis=1)[None]
    G_full["ssd_norm_w"] = dssd_nw.reshape(1, DI)
    G_full["ssd_conv_w"], G_full["ssd_conv_b"] = dcw0[None], dcb0.reshape(1, CD)
    G_full["norm1_w"] = jnp.stack([dn1w0.reshape(D), dn1w1.reshape(D)])
    G_full["norm2_w"] = jnp.stack([gf0["n2w"], gf1["n2w"]])

    zD = jnp.zeros((D,), F32)
    dm_own = jnp.stack([
        jnp.concatenate([dsh_seg[1, 0], dsc_seg[1, 0], dg1_0.reshape(D), gf0["sh2"], gf0["sc2"], gf0["g2"]]),
        jnp.concatenate([dsh1_1.reshape(D), dsc1_1.reshape(D), dg1_1.reshape(D), gf1["sh2"], gf1["sc2"], gf1["g2"]]),
    ])
    dmc_own = jnp.concatenate([dsh_seg[0, 0], dsc_seg[0, 0], zD, zD, zD, zD])

    small_sum_names = [n for n in SMALL if n not in ("c_ctx", "mod_b")]
    sum_part = [G_full[n] for n in small_sum_names] + [dmc_own]
    n_sum = sum(int(a.size) for a in sum_part)
    packed = _pack(sum_part + [dm_own])
    gat = _allgather8("gather_small_grads", packed)
    total = _sum_leading("sum_small_grads", gat, tuple(range(N_DEV)))
    summed = _unpack(total, [a.shape for a in sum_part])
    Gs = dict(zip(small_sum_names, summed[:-1]))
    dmc_tot = summed[-1]
    dm_all = jnp.stack([gat[k].reshape(-1)[n_sum:n_sum + 2 * 6 * D].reshape(2, 6 * D) for k in range(N_DEV)], axis=1)
    dm16 = jnp.concatenate([dm_all, jnp.stack([dmc_tot, jnp.zeros_like(dmc_tot)])[:, None, :],
                            jnp.zeros((2, 16 - N_DEV - 1, 6 * D), F32)], axis=1)
    Gs["mod_b"] = _sum_leading("sum_mod_b", dm16.transpose(1, 0, 2).reshape(16, 2 * 6 * D // LANE, LANE),
                               tuple(range(N_DEV + 1))).reshape(2, 6 * D)

    dm16_shard = lax.dynamic_slice_in_dim(dm16, chip * S_mod, S_mod, axis=2)
    ds16 = _mm(dm16_shard[0], mod_w[0], tb=True, precision=HIGHEST, name="c_ctx_dx")
    sig = jax.nn.sigmoid(c_ctx)
    dcc_part = ds16[N_DEV] * (sig * (1.0 + c_ctx * (1.0 - sig)))
    gat_cc = _allgather8("gather_c_ctx_grad", _pack([dcc_part]))
    Gs["c_ctx"] = _sum_leading("sum_c_ctx_grad", gat_cc, (0, 2, 4, 6)).reshape(-1)[:D]

    s16t = _silu(c16).T
    out = {}
    out["mod_w"] = _mod_w_update(s16t, dm16_shard, mod_w, m_mod_w, v_mod_w)

    late = out["mod_w"][0]
    partial = {}
    for tag, (names, handle, blocks) in reduces.items():
        landed = _exchange4_wait("reduce_" + tag + "_wait", handle, late)
        for n, blk, own in zip(names, landed, blocks):
            r = _fill_own(blk, own, chip, False)
            partial[n] = _sum_leading("sum4_" + n, r.reshape(N_CHIPS, -1, r.shape[-1]), (0, 1, 2, 3)).reshape(r.shape[1:])
    for n in ("ffn_w_up", "ffn_w_down"):
        partial[n] = jnp.stack([partial.pop(n + "0"), partial.pop(n + "1")])
    partial = [partial[n].reshape(W[n].shape) for n in BIG]
    sibling = _swap_sibling("swap_grads", partial)
    for n, mine, sib in zip(BIG, partial, sibling):
        out[n] = _adamw("adamw_" + n, W[n], Mo[n], Vo[n], mine, sib)

    def own(n, full):
        if n in SHARD_AXIS:
            size = W[n].shape[SHARD_AXIS[n]]
            return lax.dynamic_slice_in_dim(full, chip * size, size, axis=SHARD_AXIS[n])
        return full

    g_small = [own(n, Gs[n].reshape(Wf[n].shape)) for n in SMALL]
    shapes = [W[n].shape for n in SMALL]
    pk = [_pack([W[n] for n in SMALL]), _pack([Mo[n] for n in SMALL]), _pack([Vo[n] for n in SMALL]), _pack(g_small)]
    res = _adamw("adamw_small", pk[0], pk[1], pk[2], pk[3], jnp.zeros_like(pk[3]))
    unpacked = [_unpack(r, shapes) for r in res]
    for k, n in enumerate(SMALL):
        out[n] = tuple(u[k] for u in unpacked)

    grads = [out[n][0] for n in WEIGHTS]
    deltas = [out[n][1] for n in WEIGHTS]
    new_m = [out[n][2] for n in WEIGHTS]
    new_v = [out[n][3] for n in WEIGHTS]
    return (loss, grad_x, *grads, *deltas, *new_m, *new_v)
```

```python
import functools

import jax
import jax.numpy as jnp
from jax import lax
from jax.experimental import pallas as pl
from jax.experimental.pallas import tpu as pltpu

F32 = jnp.float32
BF16 = jnp.bfloat16
MESH = pl.DeviceIdType.MESH
HIGHEST = lax.Precision.HIGHEST

VMEM_LIMIT_BYTES = 48 * 1024 * 1024
LANE = 128
SUBLANE = 8

SSD_STATE = 128
SSD_CHUNK = 128
GRID_W = 64
EPS = 1e-6
N_CHIPS = 4
N_DEV = 8

ADAM_LR = 0.001
ADAM_B1 = 0.9
ADAM_B2 = 0.999
ADAM_EPS = 1e-08
ADAM_WD = 0.01
ADAM_STEP = 10


def _pcall(body, **kw):
    return pl.pallas_call(body, **kw)


def _cparams(n_grid):
    return pltpu.CompilerParams(dimension_semantics=("arbitrary",) * n_grid, vmem_limit_bytes=VMEM_LIMIT_BYTES)


def _cdiv(a, b):
    return -(-a // b)


def _round_up(a, b):
    return _cdiv(a, b) * b


def _tile(n, cap):
    if n <= cap:
        return n
    best = None
    for t in range(LANE, cap + 1, LANE):
        if n % t == 0:
            best = t
    if best is None:
        npad = _round_up(n, LANE)
        for t in range(LANE, cap + 1, LANE):
            if npad % t == 0:
                best = t
    return best


def _row_tile(n, cap, also=()):
    best = None
    for t in range(SUBLANE, min(cap, n) + 1, SUBLANE):
        if n % t == 0 and all(a % t == 0 for a in also):
            best = t
    assert best is not None, (n, cap, also)
    return best


def _silu(v):
    return v * jax.nn.sigmoid(v)


def _mm(a, b, *, name, ta=False, tb=False, precision=None, cap=1024):
    M, K = (a.shape[1], a.shape[0]) if ta else a.shape
    N = b.shape[0] if tb else b.shape[1]
    assert K == (b.shape[1] if tb else b.shape[0]), (a.shape, b.shape, ta, tb)
    tm, tn, tk = _tile(M, cap), _tile(N, cap), _tile(K, cap)
    nm, nn, nk = _cdiv(M, tm), _cdiv(N, tn), _cdiv(K, tk)
    k_tail = K % tk
    exact = precision is not None

    def body(a_ref, b_ref, o_ref, acc_ref):
        k = pl.program_id(2)

        @pl.when(k == 0)
        def _():
            acc_ref[...] = jnp.zeros_like(acc_ref)

        av = a_ref[...]
        bv = b_ref[...]
        if k_tail:
            lim = K - k * tk
            ka = lax.broadcasted_iota(jnp.int32, av.shape, 0 if ta else 1)
            kb = lax.broadcasted_iota(jnp.int32, bv.shape, 1 if tb else 0)
            av = jnp.where(ka < lim, av, jnp.zeros_like(av))
            bv = jnp.where(kb < lim, bv, jnp.zeros_like(bv))
        if exact:
            av = av.astype(F32)
            bv = bv.astype(F32)
        else:
            av = av.astype(BF16)
            bv = bv.astype(BF16)
        dn = (((0 if ta else 1,), (1 if tb else 0,)), ((), ()))
        acc_ref[...] += lax.dot_general(av, bv, dn, preferred_element_type=F32, precision=precision)

        @pl.when(k == nk - 1)
        def _():
            o_ref[...] = acc_ref[...]

    a_spec = pl.BlockSpec((tk, tm), lambda i, j, k: (k, i)) if ta else pl.BlockSpec((tm, tk), lambda i, j, k: (i, k))
    b_spec = pl.BlockSpec((tn, tk), lambda i, j, k: (j, k)) if tb else pl.BlockSpec((tk, tn), lambda i, j, k: (k, j))
    return _pcall(
        body, name=name, grid=(nm, nn, nk), in_specs=[a_spec, b_spec],
        out_specs=pl.BlockSpec((tm, tn), lambda i, j, k: (i, j)),
        out_shape=jax.ShapeDtypeStruct((M, N), F32),
        scratch_shapes=[pltpu.VMEM((tm, tn), F32)], compiler_params=_cparams(3),
    )(a, b)


def _norm_rows(rows):
    out = []
    for r in rows:
        if not isinstance(r, tuple):
            r = (r,)
        arr, off, width, roff = (r + (0, None, 0)[len(r) - 1:])
        out.append((arr, off, width if width is not None else arr.shape[1], roff))
    return out


def _rw_plan(T, rows, pars, seg_rows, col_tile, tm_cap):
    widths = [r[2] for r in rows]
    wmax = max(widths + [p.shape[-1] for p in pars] + [1])
    if col_tile is not None:
        assert all(w == widths[0] for w in widths) and all(p.shape[-1] == widths[0] for p in pars)
        ncol = widths[0] // col_tile
        assert ncol * col_tile == widths[0]
        wmax = col_tile
    else:
        ncol = 1
    cap = tm_cap if tm_cap is not None else max(SUBLANE, min(256, (256 * 1024) // wmax))
    tm = _row_tile(T, cap, also=tuple(seg_rows) + tuple(r[3] for r in rows if r[3]))
    bounds = tuple(s // tm for s in seg_rows)
    return widths, ncol, tm, bounds


def _rw_specs(rows, pars, ncol, tm, bounds, col_tile):
    def seg(i):
        s = 0
        for b in bounds:
            s = s + (i >= b).astype(jnp.int32)
        return s

    specs = []
    for arr, off, w, roff in rows:
        bw = col_tile if col_tile is not None else w
        assert off % bw == 0 and roff % tm == 0, (off, bw, roff, tm)
        specs.append(pl.BlockSpec((tm, bw), functools.partial(lambda j, i, ob, rb: (i + rb, ob + j),
                                                              ob=off // bw, rb=roff // tm)))
    for p in pars:
        bw = col_tile if col_tile is not None else p.shape[-1]
        if p.shape[0] > 1:
            specs.append(pl.BlockSpec((None, 1, bw), lambda j, i: (seg(i), 0, j)))
        else:
            specs.append(pl.BlockSpec((None, 1, bw), lambda j, i: (0, 0, j)))
    return specs, seg


def _rw_fwd(name, f, rows, pars, out_widths, *, T=None, seg_rows=(), col_tile=None, tm_cap=None):
    rows = _norm_rows(rows)
    T = rows[0][0].shape[0] if T is None else T
    widths, ncol, tm, bounds = _rw_plan(T, rows, pars, seg_rows, col_tile, tm_cap)
    in_specs, _ = _rw_specs(rows, pars, ncol, tm, bounds, col_tile)
    nr, npar, nout = len(rows), len(pars), len(out_widths)

    def body(*refs):
        vals = [r[...] for r in refs[:nr + npar]]
        outs = f(*vals)
        if not isinstance(outs, (tuple, list)):
            outs = (outs,)
        for o_ref, o in zip(refs[nr + npar:], outs):
            o_ref[...] = o.astype(o_ref.dtype)

    out_specs = [pl.BlockSpec((tm, col_tile if col_tile is not None else w), lambda j, i: (i, j)) for w in out_widths]
    res = _pcall(
        body, name=name, grid=(ncol, T // tm), in_specs=in_specs, out_specs=out_specs,
        out_shape=[jax.ShapeDtypeStruct((T, w), F32) for w in out_widths], compiler_params=_cparams(2),
    )(*[r[0] for r in rows], *pars)
    return res if nout > 1 else res[0]


def _rw_bwd(name, f, rows, pars, cots, *, row_grad, par_grad, T=None, seg_rows=(), col_tile=None, tm_cap=None,
            add=None, cot_fn=None):
    rows = _norm_rows(rows)
    cots = _norm_rows(cots)
    T = rows[0][0].shape[0] if T is None else T
    extra = _norm_rows([add]) if add is not None else []
    all_rows = rows + cots + extra
    widths, ncol, tm, bounds = _rw_plan(T, all_rows, pars, seg_rows, col_tile, tm_cap)
    in_specs, seg = _rw_specs(all_rows, pars, ncol, tm, bounds, col_tile)
    nr, nc, ne, npar = len(rows), len(cots), len(extra), len(pars)
    row_idx = [k for k in range(nr) if row_grad[k]]
    par_idx = [k for k in range(npar) if par_grad[k]]

    def body(*refs):
        i = pl.program_id(1)
        row_vals = [r[...] for r in refs[:nr]]
        cot_vals = [r[...] for r in refs[nr:nr + nc]]
        add_vals = [r[...] for r in refs[nr + nc:nr + nc + ne]]
        par_vals = [r[...] for r in refs[nr + nc + ne:nr + nc + ne + npar]]
        out_refs = refs[nr + nc + ne + npar:]
        outs, vjp = jax.vjp(f, *row_vals, *par_vals)
        if cot_fn is not None:
            cot_vals = cot_fn(*cot_vals)
            if not isinstance(cot_vals, (tuple, list)):
                cot_vals = (cot_vals,)
        if isinstance(outs, (tuple, list)):
            grads = vjp(tuple(c.astype(o.dtype) for c, o in zip(cot_vals, outs)))
        else:
            grads = vjp(cot_vals[0].astype(outs.dtype))
        first_seg = i == 0
        for b in bounds:
            first_seg = first_seg | (i == b)
        for n, k in enumerate(row_idx):
            g = grads[k]
            if n == 0 and add_vals:
                g = g + add_vals[0]
            out_refs[n][...] = g
        for n, k in enumerate(par_idx):
            g = grads[nr + k]
            o_ref = out_refs[len(row_idx) + n]
            first = first_seg if pars[k].shape[0] > 1 else (i == 0)

            @pl.when(first)
            def _(o_ref=o_ref, g=g):
                o_ref[...] = g

            @pl.when(jnp.logical_not(first))
            def _(o_ref=o_ref, g=g):
                o_ref[...] += g

    out_specs, out_shape = [], []
    for k in row_idx:
        w = widths[k]
        out_specs.append(pl.BlockSpec((tm, col_tile if col_tile is not None else w), lambda j, i: (i, j)))
        out_shape.append(jax.ShapeDtypeStruct((T, w), F32))
    for k in par_idx:
        p = pars[k]
        bw = col_tile if col_tile is not None else p.shape[-1]
        if p.shape[0] > 1:
            out_specs.append(pl.BlockSpec((None, 1, bw), lambda j, i: (seg(i), 0, j)))
        else:
            out_specs.append(pl.BlockSpec((None, 1, bw), lambda j, i: (0, 0, j)))
        out_shape.append(jax.ShapeDtypeStruct(p.shape, F32))
    res = _pcall(
        body, name=name, grid=(ncol, T // tm), in_specs=in_specs, out_specs=out_specs, out_shape=out_shape,
        compiler_params=_cparams(2),
    )(*[r[0] for r in all_rows], *pars)
    return list(res[:len(row_idx)]), list(res[len(row_idx):])


def _f_modnorm(h, w, sc, sh):
    y = h * lax.rsqrt(jnp.mean(h * h, axis=-1, keepdims=True) + EPS)
    return (y * w) * (1.0 + sc) + sh


def _f_gate_res(h, y, g):
    return h + g * y


def _f_gate_res_bias(h, y, g, b):
    return h + g * (y + b)


def _f_ffn_act(val, gate):
    return _silu(gate) * val


def _f_softplus(raw, bias):
    v = raw + bias
    return jnp.maximum(v, 0.0) + jnp.log(1.0 + jnp.exp(-jnp.abs(v)))


def _f_ssd_gate(yf, yb, xs, z, d_rep, nw):
    y = (yf + yb + d_rep * xs) * _silu(z)
    return (y * lax.rsqrt(jnp.mean(y * y, axis=-1, keepdims=True) + EPS)) * nw


def _f_glu(a, g, ba, bg):
    return (a + ba) * jax.nn.sigmoid(g + bg)


def _f_ln_silu(h, w, b):
    mu = jnp.mean(h, axis=-1, keepdims=True)
    d = h - mu
    y = d * lax.rsqrt(jnp.mean(d * d, axis=-1, keepdims=True) + EPS)
    return _silu(y * w + b)


def _f_loss_rows(h, t, w):
    y = (h * lax.rsqrt(jnp.mean(h * h, axis=-1, keepdims=True) + EPS)) * w
    e = y - t
    return 0.5 * jnp.mean(e * e, axis=-1, keepdims=True)


def _f_adamw(w, m, v, ga, gb):
    g = ga + gb
    m = ADAM_B1 * m + (1.0 - ADAM_B1) * g
    v = ADAM_B2 * v + (1.0 - ADAM_B2) * (g * g)
    m_hat = m / (1.0 - ADAM_B1 ** ADAM_STEP)
    v_hat = v / (1.0 - ADAM_B2 ** ADAM_STEP)
    delta = -ADAM_LR * (m_hat / (jnp.sqrt(v_hat) + ADAM_EPS) + ADAM_WD * w)
    return g, delta, m, v


def _adamw(name, w, m, v, ga, gb):
    shape = w.shape
    c = shape[-1]
    two_d = [t.reshape(-1, c) for t in (w, m, v, ga, gb)]
    rows = two_d[0].shape[0]
    pad = _round_up(rows, SUBLANE) - rows
    if pad:
        two_d = [jnp.pad(t, ((0, pad), (0, 0))) for t in two_d]
    outs = _rw_fwd(name, _f_adamw, two_d, [], [c] * 4)
    return tuple(o[:rows].reshape(shape) for o in outs)


def _sum_leading(name, x, idxs):
    _, R, C = x.shape
    tm = _row_tile(R, max(SUBLANE, min(512, (512 * 1024) // C)))

    def body(x_ref, o_ref):
        acc = x_ref[idxs[0]].astype(F32)
        for k in idxs[1:]:
            acc = acc + x_ref[k].astype(F32)
        o_ref[...] = acc

    return _pcall(
        body, name=name, grid=(R // tm,), in_specs=[pl.BlockSpec((x.shape[0], tm, C), lambda i: (0, i, 0))],
        out_specs=pl.BlockSpec((tm, C), lambda i: (i, 0)), out_shape=jax.ShapeDtypeStruct((R, C), F32),
        compiler_params=_cparams(1),
    )(x)


def _loss_fwd(h, t, w):
    T, D = h.shape
    tm = _row_tile(T, 256)

    def body(h_ref, t_ref, w_ref, o_ref):
        i = pl.program_id(0)
        part = jnp.sum(_f_loss_rows(h_ref[...], t_ref[...], w_ref[...]), axis=0, keepdims=True)
        part = jnp.broadcast_to(part, (1, LANE))

        @pl.when(i == 0)
        def _():
            o_ref[...] = part

        @pl.when(i > 0)
        def _():
            o_ref[...] += part

    return _pcall(
        body, name="loss_fwd", grid=(T // tm,),
        in_specs=[pl.BlockSpec((tm, D), lambda i: (i, 0)), pl.BlockSpec((tm, D), lambda i: (i, 0)),
                  pl.BlockSpec((1, D), lambda i: (0, 0))],
        out_specs=pl.BlockSpec((1, LANE), lambda i: (0, 0)), out_shape=jax.ShapeDtypeStruct((1, LANE), F32),
        compiler_params=_cparams(1),
    )(h, t, w)


CONV_ROWS = 256
CONV_ACC_ELEMS = 16384


def _tap_mask(mask, t, s):
    if mask is None:
        return None
    kind, arg = mask
    if kind == "seg":
        if s == 0:
            return None
        return (t >= arg) == ((t + s) >= arg)
    col = jnp.bitwise_and(t, GRID_W - 1)
    return (col != 0) if arg < 0 else (col != GRID_W - 1)


def _conv_plan(T, C, taps):
    rc = CONV_ROWS if T % CONV_ROWS == 0 else LANE
    assert T % rc == 0
    ct = next((t for t in (512, 256, LANE) if C % t == 0), C)
    reach = max(abs(s) for s, _ in taps)
    hb = next(h for h in (8, 16, 32, 64, 128, 256) if h >= reach and rc % h == 0)
    sub = max(SUBLANE, min(rc, CONV_ACC_ELEMS // ct))
    return rc, ct, hb, sub, T // rc, C // ct


def _halo_specs(rc, ct, hb, T, off_blocks):
    per = rc // hb
    last = T // hb - 1
    prev = pl.BlockSpec((hb, ct), lambda j, i: (jnp.maximum(i * per - 1, 0), off_blocks + j))
    cur = pl.BlockSpec((rc, ct), lambda j, i: (i, off_blocks + j))
    nxt = pl.BlockSpec((hb, ct), lambda j, i: (jnp.minimum((i + 1) * per, last), off_blocks + j))
    return [prev, cur, nxt]


def _fill_halo(pad_ref, p_ref, c_ref, n_ref, i, nrc, rc, hb):
    pad_ref[0:hb, :] = jnp.where(i > 0, p_ref[...], 0.0)
    pad_ref[hb:hb + rc, :] = c_ref[...]
    pad_ref[hb + rc:hb + rc + hb, :] = jnp.where(i < nrc - 1, n_ref[...], 0.0)


def _conv_fwd(name, u, col_off, C, w, b, taps, act=False):
    T = u.shape[0]
    rc, ct, hb, sub, nrc, ncc = _conv_plan(T, C, taps)
    assert col_off % ct == 0
    K = len(taps)

    def body(up, uc, un, w_ref, b_ref, *rest):
        y_ref = rest[0]
        pad_ref = rest[-1]
        i = pl.program_id(1)
        _fill_halo(pad_ref, up, uc, un, i, nrc, rc, hb)
        for r0 in range(0, rc, sub):
            t = i * rc + r0 + lax.broadcasted_iota(jnp.int32, (sub, 1), 0)
            acc = jnp.broadcast_to(b_ref[...], (sub, ct))
            for k, (s, mask) in enumerate(taps):
                v = pad_ref[hb + r0 + s:hb + r0 + s + sub, :]
                m = _tap_mask(mask, t, s)
                if m is not None:
                    v = jnp.where(m, v, 0.0)
                acc = acc + w_ref[k:k + 1, :] * v
            y_ref[r0:r0 + sub, :] = acc
            if act:
                rest[1][r0:r0 + sub, :] = _silu(acc)

    n_out = 2 if act else 1
    res = _pcall(
        body, name=name, grid=(ncc, nrc),
        in_specs=_halo_specs(rc, ct, hb, T, col_off // ct) + [pl.BlockSpec((K, ct), lambda j, i: (0, j)),
                                                              pl.BlockSpec((1, ct), lambda j, i: (0, j))],
        out_specs=[pl.BlockSpec((rc, ct), lambda j, i: (i, j))] * n_out,
        out_shape=[jax.ShapeDtypeStruct((T, C), F32)] * n_out,
        scratch_shapes=[pltpu.VMEM((rc + 2 * hb, ct), F32)], compiler_params=_cparams(2),
    )(u, u, u, w, b)
    return res if act else res[0]


def _conv_bwd(name, u, col_off, C, w, g, taps):
    T = u.shape[0]
    rc, ct, hb, sub, nrc, ncc = _conv_plan(T, C, taps)
    K = len(taps)

    def body(up, uc, un, gp, gc, gn, w_ref, du_ref, dw_ref, db_ref, upad, gpad):
        i = pl.program_id(1)
        _fill_halo(upad, up, uc, un, i, nrc, rc, hb)
        _fill_halo(gpad, gp, gc, gn, i, nrc, rc, hb)

        @pl.when(i == 0)
        def _():
            dw_ref[...] = jnp.zeros_like(dw_ref)
            db_ref[...] = jnp.zeros_like(db_ref)

        def fold(v):
            return jnp.sum(v.reshape(sub // SUBLANE, SUBLANE, ct), axis=0)

        dws = [jnp.zeros((SUBLANE, ct), F32) for _ in range(K)]
        dbs = jnp.zeros((SUBLANE, ct), F32)
        for r0 in range(0, rc, sub):
            t = i * rc + r0 + lax.broadcasted_iota(jnp.int32, (sub, 1), 0)
            gv = gpad[hb + r0:hb + r0 + sub, :]
            dbs = dbs + fold(gv)
            acc = jnp.zeros((sub, ct), F32)
            for k, (s, mask) in enumerate(taps):
                gs = gpad[hb + r0 - s:hb + r0 - s + sub, :]
                m = _tap_mask(mask, t - s, s)
                if m is not None:
                    gs = jnp.where(m, gs, 0.0)
                acc = acc + w_ref[k:k + 1, :] * gs
                uv = upad[hb + r0 + s:hb + r0 + s + sub, :]
                m = _tap_mask(mask, t, s)
                prod = gv * uv
                if m is not None:
                    prod = jnp.where(m, prod, 0.0)
                dws[k] = dws[k] + fold(prod)
            du_ref[r0:r0 + sub, :] = acc
        for k in range(K):
            dw_ref[k:k + 1, :] += jnp.sum(dws[k], axis=0, keepdims=True)
        db_ref[...] += jnp.sum(dbs, axis=0, keepdims=True)

    halo_u = _halo_specs(rc, ct, hb, T, col_off // ct)
    halo_g = _halo_specs(rc, ct, hb, T, 0)
    return _pcall(
        body, name=name, grid=(ncc, nrc),
        in_specs=halo_u + halo_g + [pl.BlockSpec((K, ct), lambda j, i: (0, j))],
        out_specs=[pl.BlockSpec((rc, ct), lambda j, i: (i, j)), pl.BlockSpec((K, ct), lambda j, i: (0, j)),
                   pl.BlockSpec((1, ct), lambda j, i: (0, j))],
        out_shape=[jax.ShapeDtypeStruct((T, C), F32), jax.ShapeDtypeStruct((K, C), F32),
                   jax.ShapeDtypeStruct((1, C), F32)],
        scratch_shapes=[pltpu.VMEM((rc + 2 * hb, ct), F32), pltpu.VMEM((rc + 2 * hb, ct), F32)],
        compiler_params=_cparams(2),
    )(u, u, u, g, g, g, w)


def _ssd_group(xg, bm, cm, s_in, *per_head, reverse, P):
    R = len(per_head) // 2
    dtrs, a_s = per_head[:R], per_head[R:]
    q, rp = xg.shape
    ii = lax.broadcasted_iota(jnp.int32, (q, q), 0)
    jj = lax.broadcasted_iota(jnp.int32, (q, q), 1)
    causal = (jj >= ii) if reverse else (jj <= ii)
    causal_t = (ii >= jj) if reverse else (ii <= jj)
    eye = ii == jj
    lane = lax.broadcasted_iota(jnp.int32, (1, rp), 1)
    row = lax.broadcasted_iota(jnp.int32, (rp, 1), 0)
    nt = (((1,), (1,)), ((), ()))
    tn = (((0,), (0,)), ((), ()))
    cb = lax.dot_general(cm.astype(BF16), bm.astype(BF16), nt, preferred_element_type=F32)
    dt_x = jnp.zeros((q, rp), F32)
    acum_x = jnp.zeros((q, rp), F32)
    tot_row = jnp.zeros((1, rp), F32)
    tot_col = jnp.zeros((rp, 1), F32)
    wts, lane_masks = [], []
    for r in range(R):
        hm = (lane >= r * P) & (lane < (r + 1) * P)
        hc = (row >= r * P) & (row < (r + 1) * P)
        dt_c = jnp.sum(jnp.where(eye, dtrs[r], 0.0), axis=1, keepdims=True)
        dac = dt_c * a_s[r]
        dar = dtrs[r] * a_s[r]
        acum_c = jnp.sum(jnp.where(causal, dar, 0.0), axis=1, keepdims=True)
        acum_r = jnp.sum(jnp.where(causal_t, dac, 0.0), axis=0, keepdims=True)
        decay = jnp.where(causal, jnp.exp(jnp.where(causal, acum_c - acum_r, 0.0)), 0.0)
        tot = jnp.sum(dac, axis=0, keepdims=True)
        dt_x = jnp.where(hm, dt_c, dt_x)
        acum_x = jnp.where(hm, acum_c, acum_x)
        tot_row = jnp.where(hm, tot, tot_row)
        tot_col = jnp.where(hc, tot, tot_col)
        wts.append((cb * decay).astype(BF16))
        lane_masks.append(hm)
    xdt = xg * dt_x
    xdt_b = xdt.astype(BF16)
    y = jnp.zeros((q, rp), F32)
    for r in range(R):
        y = jnp.where(lane_masks[r], jnp.dot(wts[r], xdt_b, preferred_element_type=F32), y)
    dte = jnp.exp(tot_row - acum_x)
    cs = lax.dot_general((xdt * dte).astype(BF16), bm.astype(BF16), tn, preferred_element_type=F32)
    y = y + lax.dot_general(cm.astype(BF16), s_in.astype(BF16), nt, preferred_element_type=F32) * jnp.exp(acum_x)
    s_out = jnp.exp(tot_col) * s_in + cs
    return y, s_out


def _ssd_maps(NC, ncc, reverse_steps):
    def chunk(d, s):
        if reverse_steps:
            s = NC - 1 - s
        return s if d == 0 else jnp.where(s < ncc, ncc - 1 - s, NC - 1 - s + ncc)

    def lat_chunk(d, s):
        c = chunk(d, s) - ncc
        return jnp.where(c < 0, 0 if d == 0 else NC - ncc - 1, c)

    def step(s):
        return NC - 1 - s if reverse_steps else s

    return chunk, lat_chunk, step


def _ssd_specs(chunk, d, R, Q, N, RP, bo, co):
    return [
        pl.BlockSpec((Q, RP), lambda g, s: (chunk(d, s), g)),
        pl.BlockSpec((Q, N), lambda g, s: (chunk(d, s), bo + g)),
        pl.BlockSpec((Q, N), lambda g, s: (chunk(d, s), co + g)),
        pl.BlockSpec((R, 1, Q), lambda g, s: (g, 0, chunk(d, s))),
        pl.BlockSpec((R, 1, 1), lambda g, s: (g, 0, 0)),
    ]


def _ssd_fwd(xbc, b_off, c_off, dtr, a, P, ncc):
    T = xbc.shape[0]
    H = dtr[0].shape[0]
    N, Q = SSD_STATE, SSD_CHUNK
    NC = T // Q
    G = (c_off - b_off) // N
    R = H // G
    RP = R * P
    chunk, lat_chunk, _ = _ssd_maps(NC, ncc, False)

    def body(*refs):
        s = pl.program_id(1)
        s_ref = refs[-1]

        @pl.when(s == 0)
        def _():
            s_ref[...] = jnp.zeros_like(s_ref)

        for d in range(2):
            x_ref, b_ref, c_ref, dtr_ref, a_ref = refs[5 * d:5 * d + 5]
            y_ref, se_ref = refs[10 + 2 * d:12 + 2 * d]
            s_in = s_ref[d]
            se_ref[...] = s_in
            per_head = [dtr_ref[r] for r in range(R)] + [a_ref[r] for r in range(R)]
            y, s_out = _ssd_group(x_ref[...], b_ref[...], c_ref[...], s_in, *per_head, reverse=d == 1, P=P)
            y_ref[...] = y
            s_ref[d] = s_out

    in_specs, out_specs, out_shape, operands = [], [], [], []
    for d in range(2):
        in_specs += _ssd_specs(chunk, d, R, Q, N, RP, b_off // N, c_off // N)
        operands += [xbc, xbc, xbc, dtr[d], a[d]]
        out_specs += [pl.BlockSpec((Q, RP), functools.partial(lambda g, s, d: (lat_chunk(d, s), g), d=d)),
                      pl.BlockSpec((None, None, RP, N), lambda g, s: (g, s, 0, 0))]
        out_shape += [jax.ShapeDtypeStruct((T - ncc * Q, H * P), F32), jax.ShapeDtypeStruct((G, NC, RP, N), F32)]
    y_f, se_f, y_b, se_b = _pcall(
        body, name="ssd_fwd", grid=(G, NC), in_specs=in_specs, out_specs=out_specs, out_shape=out_shape,
        scratch_shapes=[pltpu.VMEM((2, RP, N), F32)], compiler_params=_cparams(2),
    )(*operands)
    return (y_f, y_b), (se_f, se_b)


def _ssd_bwd(xbc, b_off, c_off, dtr, a, s_enter, dy, P, ncc):
    T = xbc.shape[0]
    H = dtr[0].shape[0]
    N, Q = SSD_STATE, SSD_CHUNK
    NC = T // Q
    G = (c_off - b_off) // N
    R = H // G
    RP = R * P
    chunk, lat_chunk, step = _ssd_maps(NC, ncc, True)
    n_in, n_out = 7, 5

    def body(*refs):
        s = pl.program_id(1)
        ds_ref = refs[-1]

        @pl.when(s == 0)
        def _():
            ds_ref[...] = jnp.zeros_like(ds_ref)

        for d in range(2):
            x_ref, b_ref, c_ref, dtr_ref, a_ref, se_ref, dy_ref = refs[n_in * d:n_in * (d + 1)]
            dx_ref, db_ref, dc_ref, ddtr_ref, da_ref = refs[2 * n_in + n_out * d:2 * n_in + n_out * (d + 1)]
            per_head = [dtr_ref[r] for r in range(R)] + [a_ref[r] for r in range(R)]
            f = functools.partial(_ssd_group, reverse=d == 1, P=P)
            _, vjp = jax.vjp(f, x_ref[...], b_ref[...], c_ref[...], se_ref[...], *per_head)
            is_latent = chunk(d, s) >= ncc
            dy_v = jnp.where(is_latent, dy_ref[...], 0.0)
            grads = vjp((dy_v, ds_ref[d]))
            dx_ref[...] = grads[0]
            db_ref[...] = grads[1]
            dc_ref[...] = grads[2]
            ds_ref[d] = grads[3]
            for r in range(R):
                ddtr_ref[r] = grads[4 + r]
                da_ref[r] = jnp.broadcast_to(grads[4 + R + r], (SUBLANE, LANE))

    in_specs, out_specs, out_shape, operands = [], [], [], []
    for d in range(2):
        in_specs += _ssd_specs(chunk, d, R, Q, N, RP, b_off // N, c_off // N) + [
            pl.BlockSpec((None, None, RP, N), lambda g, s: (g, step(s), 0, 0)),
            pl.BlockSpec((Q, RP), functools.partial(lambda g, s, d: (lat_chunk(d, s), g), d=d)),
        ]
        operands += [xbc, xbc, xbc, dtr[d], a[d], s_enter[d], dy]
    for d in range(2):
        at_chunk = functools.partial(lambda g, s, d: (chunk(d, s), g), d=d)
        out_specs += [
            pl.BlockSpec((Q, RP), at_chunk), pl.BlockSpec((Q, N), at_chunk), pl.BlockSpec((Q, N), at_chunk),
            pl.BlockSpec((R, 1, Q), functools.partial(lambda g, s, d: (g, 0, chunk(d, s)), d=d)),
            pl.BlockSpec((R, SUBLANE, LANE), lambda g, s: (g * NC + s, 0, 0)),
        ]
        out_shape += [
            jax.ShapeDtypeStruct((T, H * P), F32), jax.ShapeDtypeStruct((T, G * N), F32),
            jax.ShapeDtypeStruct((T, G * N), F32), jax.ShapeDtypeStruct((H, 1, T), F32),
            jax.ShapeDtypeStruct((G * NC * R, SUBLANE, LANE), F32),
        ]
    res = _pcall(
        body, name="ssd_bwd", grid=(G, NC), in_specs=in_specs, out_specs=out_specs, out_shape=out_shape,
        scratch_shapes=[pltpu.VMEM((2, RP, N), F32)], compiler_params=_cparams(2),
    )(*operands)
    return res[:n_out], res[n_out:]


def _allgather8(name, v):
    R, C = v.shape

    def body(x_ref, out_ref, send_sems, recv_sems, local_sem):
        x, y, c = lax.axis_index("x"), lax.axis_index("y"), lax.axis_index("c")
        me, sibling = (x, y, c), (x, y, 1 - c)
        chips = [(1 - x, y), (x, 1 - y), (1 - x, 1 - y)]

        def slot(px, py, pc):
            return out_ref.at[4 * px + 2 * py + pc]

        def copy(k, block, to, src=None):
            return pltpu.make_async_remote_copy(
                src_ref=slot(*block) if src is None else src, dst_ref=slot(*block),
                send_sem=send_sems.at[k], recv_sem=recv_sems.at[k], device_id=to, device_id_type=MESH)

        mine = pltpu.make_async_copy(x_ref, slot(*me), local_sem)
        mine.start()
        first = [copy(0, me, sibling, src=x_ref)]
        first += [copy(1 + j, me, (*chip, c), src=x_ref) for j, chip in enumerate(chips)]
        for cp in first:
            cp.start()
        passed = [copy(4 + j, (*chip, c), sibling) for j, chip in enumerate(chips)]
        for j, chip in enumerate(chips):
            copy(1 + j, (*chip, c), me).wait_recv()
            passed[j].start()
        copy(0, sibling, me).wait_recv()
        for j, chip in enumerate(chips):
            copy(4 + j, (*chip, 1 - c), me).wait_recv()
        for cp in first + passed:
            cp.wait_send()
        mine.wait()

    return _pcall(
        body, name=name, out_shape=jax.ShapeDtypeStruct((N_DEV, R, C), v.dtype),
        in_specs=[pl.BlockSpec(memory_space=pltpu.VMEM)], out_specs=pl.BlockSpec(memory_space=pltpu.VMEM),
        scratch_shapes=[pltpu.SemaphoreType.DMA((7,)), pltpu.SemaphoreType.DMA((7,)), pltpu.SemaphoreType.DMA],
        compiler_params=pltpu.CompilerParams(vmem_limit_bytes=VMEM_LIMIT_BYTES),
    )(v)


def _exchange4_start(name, srcs, bcast, dep):
    n = len(srcs)
    lands = [lax.empty(((N_CHIPS,) + s.shape) if bcast else s.shape, s.dtype) for s in srcs]

    def body(*refs):
        src, land = refs[:n], refs[n:2 * n]
        send_sems, recv_sems = refs[2 * n + 1], refs[2 * n + 2]
        token = refs[-1]
        x, y, c = lax.axis_index("x"), lax.axis_index("y"), lax.axis_index("c")
        me = 2 * x + y
        for a in range(n):
            for j, (px, py) in enumerate([(1 - x, y), (x, 1 - y), (1 - x, 1 - y)]):
                pltpu.make_async_remote_copy(
                    src_ref=src[a] if bcast else src[a].at[2 * px + py], dst_ref=land[a].at[me],
                    send_sem=send_sems.at[3 * a + j], recv_sem=recv_sems.at[3 * a + j], device_id=(px, py, c),
                    device_id_type=MESH).start()
        token[...] = jnp.zeros_like(token)

    hbm = pl.BlockSpec(memory_space=pltpu.HBM)
    sem = pl.BlockSpec(memory_space=pltpu.SEMAPHORE)
    outs = _pcall(
        body, name=name,
        out_shape=(pltpu.SemaphoreType.DMA((3 * n,)), pltpu.SemaphoreType.DMA((3 * n,)),
                   *[pltpu.HBM(s.shape, s.dtype) for s in srcs], *[pltpu.HBM(l.shape, l.dtype) for l in lands],
                   jax.ShapeDtypeStruct((SUBLANE, LANE), F32)),
        in_specs=[hbm] * (2 * n) + [pl.BlockSpec(memory_space=pl.ANY)],
        out_specs=(sem, sem, *[hbm] * (2 * n), pl.BlockSpec(memory_space=pltpu.VMEM)),
        input_output_aliases={k: 2 + k for k in range(2 * n)},
        compiler_params=pltpu.CompilerParams(has_side_effects=pltpu.SideEffectType.DATAFLOW_SIDE_EFFECTING),
    )(*[pltpu.with_memory_space_constraint(s, pltpu.HBM) for s in srcs],
      *[pltpu.with_memory_space_constraint(l, pltpu.HBM) for l in lands], dep)
    return (n, bcast, outs[0], outs[1], outs[2:2 + n], outs[2 + n:2 + 2 * n]), outs[-1]


def _exchange4_wait(name, handle, after):
    n, bcast, send_sems, recv_sems, src_thru, land_thru = handle

    def body(*refs):
        src, land = refs[:n], refs[n:2 * n]
        send_sems, recv_sems = refs[2 * n], refs[2 * n + 1]
        x, y, c = lax.axis_index("x"), lax.axis_index("y"), lax.axis_index("c")
        for a in range(n):
            for j, (px, py) in enumerate([(1 - x, y), (x, 1 - y), (1 - x, 1 - y)]):
                pk = 2 * px + py
                copy = pltpu.make_async_remote_copy(
                    src_ref=src[a] if bcast else src[a].at[pk], dst_ref=land[a].at[pk],
                    send_sem=send_sems.at[3 * a + j], recv_sem=recv_sems.at[3 * a + j], device_id=(px, py, c),
                    device_id_type=MESH)
                copy.wait_send()
                copy.wait_recv()

    hbm = pl.BlockSpec(memory_space=pltpu.HBM)
    sem = pl.BlockSpec(memory_space=pltpu.SEMAPHORE)
    outs = _pcall(
        body, name=name,
        out_shape=tuple(pltpu.HBM(t.shape, t.dtype) for t in (*src_thru, *land_thru)),
        in_specs=[hbm] * (2 * n) + [sem, sem, pl.BlockSpec(memory_space=pl.ANY)], out_specs=tuple([hbm] * (2 * n)),
        input_output_aliases={k: k for k in range(2 * n)},
        compiler_params=pltpu.CompilerParams(has_side_effects=pltpu.SideEffectType.DATAFLOW_SIDE_EFFECTING),
    )(*src_thru, *land_thru, send_sems, recv_sems, after)
    return list(outs[n:])


def _tie(name, v, token):
    def body(v_ref, token_ref, o_ref):
        del v_ref, token_ref, o_ref

    any_spec = pl.BlockSpec(memory_space=pl.ANY)
    return _pcall(body, name=name, out_shape=jax.ShapeDtypeStruct(v.shape, v.dtype), in_specs=[any_spec, any_spec],
                  out_specs=any_spec, input_output_aliases={0: 0})(v, token)


def _fill_own(name, landed, own, bcast):
    def body(landed_ref, own_ref, out_ref, sem):
        del landed_ref
        me = 2 * lax.axis_index("x") + lax.axis_index("y")
        copy = pltpu.make_async_copy(own_ref if bcast else own_ref.at[me], out_ref.at[me], sem)
        copy.start()
        copy.wait()

    any_spec = pl.BlockSpec(memory_space=pl.ANY)
    return _pcall(body, name=name, out_shape=jax.ShapeDtypeStruct(landed.shape, landed.dtype),
                  in_specs=[any_spec, any_spec], out_specs=any_spec, input_output_aliases={0: 0},
                  scratch_shapes=[pltpu.SemaphoreType.DMA])(landed, own)


def _swap_sibling(name, srcs):
    n = len(srcs)

    def body(*refs):
        src, out = refs[:n], refs[n:2 * n]
        send_sems, recv_sems = refs[2 * n:]
        x, y, c = lax.axis_index("x"), lax.axis_index("y"), lax.axis_index("c")
        copies = []
        for a in range(n):
            rc = pltpu.make_async_remote_copy(
                src_ref=src[a], dst_ref=out[a], send_sem=send_sems.at[a], recv_sem=recv_sems.at[a],
                device_id=(x, y, 1 - c), device_id_type=MESH)
            rc.start()
            copies.append(rc)
        for cp in copies:
            cp.wait()

    any_spec = pl.BlockSpec(memory_space=pl.ANY)
    return _pcall(
        body, name=name, out_shape=[jax.ShapeDtypeStruct(s.shape, s.dtype) for s in srcs],
        in_specs=[any_spec] * n, out_specs=[any_spec] * n,
        scratch_shapes=[pltpu.SemaphoreType.DMA((n,)), pltpu.SemaphoreType.DMA((n,))],
    )(*srcs)


def _mod_fwd(c16, mod_w, mod_b_shard):
    nl, D, S = mod_w.shape

    def body(c_ref, w_ref, b_ref, o_ref):
        s = _silu(c_ref[...]).astype(BF16)
        o_ref[...] = jnp.dot(s, w_ref[...].astype(BF16), preferred_element_type=F32) + b_ref[...]

    return _pcall(
        body, name="mod_fwd", grid=(nl,),
        in_specs=[pl.BlockSpec((16, D), lambda l: (0, 0)), pl.BlockSpec((None, D, S), lambda l: (l, 0, 0)),
                  pl.BlockSpec((None, 1, S), lambda l: (l, 0, 0))],
        out_specs=pl.BlockSpec((None, 16, S), lambda l: (l, 0, 0)),
        out_shape=jax.ShapeDtypeStruct((nl, 16, S), F32), compiler_params=_cparams(1),
    )(c16, mod_w, mod_b_shard)


def _mod_w_update(s16t, dm16, w, m, v):
    nl, D, S = w.shape
    tm = _row_tile(D, 256)

    def body(s_ref, dm_ref, w_ref, m_ref, v_ref, g_ref, dl_ref, nm_ref, nv_ref):
        g = jnp.dot(s_ref[...], dm_ref[...], preferred_element_type=F32, precision=HIGHEST)
        g, dl, nm, nv = _f_adamw(w_ref[...], m_ref[...], v_ref[...], g, jnp.zeros_like(g))
        g_ref[...] = g
        dl_ref[...] = dl
        nm_ref[...] = nm
        nv_ref[...] = nv

    big = pl.BlockSpec((None, tm, S), lambda l, i: (l, i, 0))
    return _pcall(
        body, name="mod_w_update", grid=(nl, D // tm),
        in_specs=[pl.BlockSpec((tm, 16), lambda l, i: (i, 0)), pl.BlockSpec((None, 16, S), lambda l, i: (l, 0, 0)),
                  big, big, big],
        out_specs=[big] * 4, out_shape=[jax.ShapeDtypeStruct(w.shape, F32)] * 4, compiler_params=_cparams(2),
    )(s16t, dm16, w, m, v)


def _pack(arrs):
    flat = jnp.concatenate([a.reshape(-1).astype(F32) for a in arrs])
    n = flat.shape[0]
    rows = _round_up(_cdiv(n, LANE), SUBLANE)
    return jnp.pad(flat, (0, rows * LANE - n)).reshape(rows, LANE)


def _unpack(buf, shapes):
    flat = buf.reshape(-1)
    out, pos = [], 0
    for s in shapes:
        n = 1
        for d in s:
            n *= d
        out.append(flat[pos:pos + n].reshape(s))
        pos += n
    return out


SHARD_AXIS = {
    "mod_w": 2, "ssd_w_in": 2, "ssd_conv_w": 2, "ssd_w_out": 1, "conf_w_pw1": 2, "conf_b_pw1": 1, "conf_w_dw": 2,
    "conf_b_dw": 1, "conf_ln_w": 1, "conf_ln_b": 1, "conf_w_pw2": 1, "conf_b_pw2": 1, "ffn_w_up": 2,
    "ffn_conv_w": 3, "ffn_w_down": 1,
}
BIG = ("ssd_w_in", "ssd_w_out", "conf_w_pw1", "conf_w_pw2", "ffn_w_up", "ffn_w_down")
WEIGHTS = ("c_ctx", "mod_w", "mod_b", "norm1_w", "norm2_w", "ssd_w_in", "ssd_conv_w", "ssd_conv_b", "ssd_dt_bias",
           "ssd_a_log", "ssd_d", "ssd_norm_w", "ssd_w_out", "conf_w_pw1", "conf_b_pw1", "conf_w_dw", "conf_b_dw",
           "conf_ln_w", "conf_ln_b", "conf_w_pw2", "conf_b_pw2", "ffn_w_up", "ffn_conv_w", "ffn_conv_b",
           "ffn_w_down", "final_norm_w")
SMALL = tuple(n for n in WEIGHTS if n not in BIG and n != "mod_w")
SMALL_SHARDED = tuple(n for n in SMALL if n in SHARD_AXIS)


def _unshard(stacked, axis):
    return jnp.concatenate([stacked[k] for k in range(N_CHIPS)], axis=axis)


def _to_blocks(full, axis):
    return jnp.stack(jnp.split(full, N_CHIPS, axis=axis))


def _par(v):
    v = v.reshape(-1, v.shape[-1])
    return v[:, None, :]


def kernel(x, c, ctx, c_ctx, mod_w, mod_b, norm1_w, norm2_w, ssd_w_in, ssd_conv_w, ssd_conv_b, ssd_dt_bias, ssd_a_log, ssd_d, ssd_norm_w, ssd_w_out, conf_w_pw1, conf_b_pw1, conf_w_dw, conf_b_dw, conf_ln_w, conf_ln_b, conf_w_pw2, conf_b_pw2, ffn_w_up, ffn_conv_w, ffn_conv_b, ffn_w_down, final_norm_w, loss_target, m_c_ctx, m_mod_w, m_mod_b, m_norm1_w, m_norm2_w, m_ssd_w_in, m_ssd_conv_w, m_ssd_conv_b, m_ssd_dt_bias, m_ssd_a_log, m_ssd_d, m_ssd_norm_w, m_ssd_w_out, m_conf_w_pw1, m_conf_b_pw1, m_conf_w_dw, m_conf_b_dw, m_conf_ln_w, m_conf_ln_b, m_conf_w_pw2, m_conf_b_pw2, m_ffn_w_up, m_ffn_conv_w, m_ffn_conv_b, m_ffn_w_down, m_final_norm_w, v_c_ctx, v_mod_w, v_mod_b, v_norm1_w, v_norm2_w, v_ssd_w_in, v_ssd_conv_w, v_ssd_conv_b, v_ssd_dt_bias, v_ssd_a_log, v_ssd_d, v_ssd_norm_w, v_ssd_w_out, v_conf_w_pw1, v_conf_b_pw1, v_conf_w_dw, v_conf_b_dw, v_conf_ln_w, v_conf_ln_b, v_conf_w_pw2, v_conf_b_pw2, v_ffn_w_up, v_ffn_conv_w, v_ffn_conv_b, v_ffn_w_down, v_final_norm_w):
    given = dict(locals())
    W = {n: given[n] for n in WEIGHTS}
    Mo = {n: given["m_" + n] for n in WEIGHTS}
    Vo = {n: given["v_" + n] for n in WEIGHTS}

    ax, ay, ac = lax.axis_index("x"), lax.axis_index("y"), lax.axis_index("c")
    chip = 2 * ax + ay
    dev = 4 * ax + 2 * ay + ac

    D = x.shape[-1]
    L, Lc = x.shape[1], ctx.shape[1]
    T0 = L + Lc
    H = ssd_a_log.shape[-1]
    DI = ssd_norm_w.shape[-1]
    P = DI // H
    CD = ssd_conv_b.shape[-1]
    N = SSD_STATE
    G = (CD - DI) // (2 * N)
    FH = ffn_conv_b.shape[-1]
    KS = ssd_conv_w.shape[1]
    KC = conf_w_dw.shape[1]
    ncc = Lc // SSD_CHUNK

    shard_b = {n: W[n].astype(BF16) for n in BIG}
    gather_a, token = _exchange4_start("gather_w_in_start", [shard_b["ssd_w_in"]], True, x)
    c = _tie("tie_gather_w_in", c, token)

    small_shard_shapes = [W[n].shape for n in SMALL_SHARDED]
    f1 = _allgather8("gather_small", _pack([c] + [W[n] for n in SMALL_SHARDED]))
    c_rows, full_small = [], {n: [] for n in SMALL_SHARDED}
    for k in range(N_DEV):
        parts = _unpack(f1[k], [c.shape] + small_shard_shapes)
        c_rows.append(parts[0])
        if k % 2 == 0:
            for n, p in zip(SMALL_SHARDED, parts[1:]):
                full_small[n].append(p)
    Wf = dict(W)
    for n in SMALL_SHARDED:
        Wf[n] = jnp.concatenate(full_small[n], axis=SHARD_AXIS[n])
    c16 = jnp.concatenate(c_rows + [c_ctx[None, :], jnp.zeros((16 - N_DEV - 1, D), F32)], axis=0)

    S_mod = mod_w.shape[-1]
    mod_b_shard = lax.dynamic_slice_in_dim(mod_b, chip * S_mod, S_mod, axis=1)[:, None, :]
    mod_part = _mod_fwd(c16, mod_w, mod_b_shard)
    f2 = _allgather8("gather_mod", mod_part.reshape(2 * 16, S_mod))
    mods = jnp.concatenate([f2[2 * k].reshape(2, 16, S_mod) for k in range(N_CHIPS)], axis=-1)
    my = lax.dynamic_slice_in_dim(mods, dev, 1, axis=1)[:, 0]
    sh1, sc1, g1, sh2, sc2, g2 = [[my[l, k * D:(k + 1) * D] for l in range(2)] for k in range(6)]
    csh1, csc1 = mods[0, N_DEV, 0:D], mods[0, N_DEV, D:2 * D]

    def full_weight(n, landed):
        return _unshard(_fill_own("own_" + n, landed, shard_b[n], True), SHARD_AXIS[n])

    xl = x[0]
    hcat = jnp.concatenate([ctx[0], xl], axis=0)
    n1w0, n2w0, n1w1, n2w1 = _par(norm1_w[0]), _par(norm2_w[0]), _par(norm1_w[1]), _par(norm2_w[1])
    sc_seg = jnp.stack([csc1, sc1[0]])[:, None, :]
    sh_seg = jnp.stack([csh1, sh1[0]])[:, None, :]

    a0 = _rw_fwd("l0_modnorm1", _f_modnorm, [hcat], [n1w0, sc_seg, sh_seg], [D], seg_rows=(Lc,))
    (landed_in,) = _exchange4_wait("gather_w_in_wait", gather_a, a0)
    w_in = full_weight("ssd_w_in", landed_in)[0]
    rest = [n for n in BIG if n != "ssd_w_in"]
    gather_b, token = _exchange4_start("gather_rest_start", [shard_b[n] for n in rest], True, landed_in)
    a0 = _tie("tie_gather_rest", a0, token)
    proj = _mm(a0, w_in, name="l0_w_in")
    seg_taps = [(k - KS // 2, ("seg", Lc)) for k in range(KS)]
    xbc_pre, xbc = _conv_fwd("l0_conv", proj, DI, CD, Wf["ssd_conv_w"][0], ssd_conv_b, seg_taps, act=True)
    dt_raw = proj[:, DI + CD:]
    dt_bias = _par(ssd_dt_bias.reshape(1, 2 * H))
    dt = _rw_fwd("l0_softplus", _f_softplus, [dt_raw], [dt_bias], [2 * H])
    dt_t = dt.T
    dtr = (dt_t[:H, None, :], dt_t[H:, None, :])
    a_all = -jnp.exp(ssd_a_log.reshape(2, H, 1, 1))
    a_neg = (a_all[0], a_all[1])
    (y_f, y_b), s_enter = _ssd_fwd(xbc, DI, DI + G * N, dtr, a_neg, P, ncc)
    gate_rows = [y_f, y_b, (xbc, 0, DI, Lc), (proj, 0, DI, Lc)]
    d_rep = _par(jnp.repeat(ssd_d[0], P))
    ssd_nw = _par(ssd_norm_w[0])
    yn = _rw_fwd("l0_ssd_gate", _f_ssd_gate, gate_rows, [d_rep, ssd_nw], [DI], T=L)
    Wb = {n: full_weight(n, g) for n, g in zip(rest, _exchange4_wait("gather_rest_wait", gather_b, yn))}
    w_out, w_pw1, w_pw2 = Wb["ssd_w_out"][0], Wb["conf_w_pw1"][0], Wb["conf_w_pw2"][0]
    w_up, w_dn = Wb["ffn_w_up"], Wb["ffn_w_down"]
    mix0 = _mm(yn, w_out, name="l0_w_out")
    g1_0, g2_0, g1_1, g2_1 = _par(g1[0]), _par(g2[0]), _par(g1[1]), _par(g2[1])
    h1 = _rw_fwd("l0_res1", _f_gate_res, [xl, mix0], [g1_0], [D])

    grid_taps = [((i - 1) * GRID_W + (j - 1), (None if j == 1 else ("col", j - 1))) for i in range(3) for j in range(3)]

    def ffn_fwd(l, h, tag):
        a = _rw_fwd(tag + "_modnorm2", _f_modnorm, [h], [_par(norm2_w[l]), _par(sc2[l]), _par(sh2[l])], [D])
        hh = _mm(a, w_up[l], name=tag + "_w_up")
        gc = _conv_fwd(tag + "_ffn_conv", hh, FH, FH, Wf["ffn_conv_w"][l].reshape(9, FH), ffn_conv_b[l][None, :],
                       grid_taps)
        act = _rw_fwd(tag + "_act", _f_ffn_act, [(hh, 0, FH), gc], [], [FH], col_tile=_tile(FH, 1536))
        dn = _mm(act, w_dn[l], name=tag + "_w_down")
        return a, hh, gc, act, dn

    a1, hh0, gc0, act0, dn0 = ffn_fwd(0, h1, "l0")
    h2 = _rw_fwd("l0_res2", _f_gate_res, [h1, dn0], [g2_0], [D])

    a2 = _rw_fwd("l1_modnorm1", _f_modnorm, [h2], [n1w1, _par(sc1[1]), _par(sh1[1])], [D])
    pw = _mm(a2, w_pw1, name="l1_pw1")
    b_pw1 = Wf["conf_b_pw1"][0]
    glu = _rw_fwd("l1_glu", _f_glu, [(pw, 0, D), (pw, D, D)], [_par(b_pw1[:D]), _par(b_pw1[D:])], [D])
    conf_taps = [(k - KC // 2, None) for k in range(KC)]
    cv = _conv_fwd("l1_conv", glu, 0, D, Wf["conf_w_dw"][0], Wf["conf_b_dw"], conf_taps)
    ln_w, ln_b = _par(Wf["conf_ln_w"][0]), _par(Wf["conf_ln_b"][0])
    ls = _rw_fwd("l1_ln_silu", _f_ln_silu, [cv], [ln_w, ln_b], [D])
    p2 = _mm(ls, w_pw2, name="l1_pw2")
    b_pw2 = _par(Wf["conf_b_pw2"][0])
    h3 = _rw_fwd("l1_res1", _f_gate_res_bias, [h2, p2], [g1_1, b_pw2], [D])
    a3, hh1, gc1, act1, dn1 = ffn_fwd(1, h3, "l1")
    h4 = _rw_fwd("l1_res2", _f_gate_res, [h3, dn1], [g2_1], [D])

    fnw = final_norm_w[None, :]
    tgt = loss_target[0]
    loss_local = _loss_fwd(h4, tgt, fnw)[0, 0]
    loss = lax.psum(loss_local, ("x", "y", "c"))

    G_full = {}
    reduces = {}

    def start_reduce(tag, items, dep):
        blocks = [_to_blocks(g, ax).astype(BF16) for _, g, ax in items]
        handle, tok = _exchange4_start("reduce_" + tag + "_start", blocks, False, dep)
        reduces[tag] = ([n for n, _, _ in items], handle, blocks)
        return tok
    ones = jnp.ones((L, 1), F32)
    (dh4,), (dfnw,) = _rw_bwd("loss_bwd", _f_loss_rows, [h4, tgt], [_par(final_norm_w)], [ones],
                              row_grad=[True, False], par_grad=[True])
    G_full["final_norm_w"] = dfnw.reshape(D)

    def ffn_bwd(l, h, saved, g2_l, dh_out, tag):
        a, hh, gc, act, dn = saved
        (ddn,), (dg2,) = _rw_bwd(tag + "_res2_bwd", _f_gate_res, [h, dn], [g2_l], [dh_out],
                                 row_grad=[False, True], par_grad=[True])
        dact = _mm(ddn, w_dn[l], tb=True, name=tag + "_w_down_dx")
        dwdn = _mm(act, ddn, ta=True, name=tag + "_w_down_dw")
        (dval, dgc), _ = _rw_bwd(tag + "_act_bwd", _f_ffn_act, [(hh, 0, FH), gc], [], [dact],
                                 row_grad=[True, True], par_grad=[], col_tile=_tile(FH, 1536))
        dgin, dcw, dcb = _conv_bwd(tag + "_ffn_conv_bwd", hh, FH, FH, Wf["ffn_conv_w"][l].reshape(9, FH), dgc, grid_taps)
        dhh = jnp.concatenate([dval, dgin], axis=1)
        da = _mm(dhh, w_up[l], tb=True, name=tag + "_w_up_dx")
        dwup = _mm(a, dhh, ta=True, name=tag + "_w_up_dw")
        (dh,), (dn2w, dsc2, dsh2) = _rw_bwd(
            tag + "_modnorm2_bwd", _f_modnorm, [h], [_par(norm2_w[l]), _par(sc2[l]), _par(sh2[l])], [da],
            row_grad=[True], par_grad=[True, True, True], add=dh_out)
        return dh, dict(w_down=dwdn, w_up=dwup, conv_w=dcw.reshape(3, 3, FH), conv_b=dcb.reshape(FH),
                        n2w=dn2w.reshape(D), sc2=dsc2.reshape(D), sh2=dsh2.reshape(D), g2=dg2.reshape(D))

    dh3, gf1 = ffn_bwd(1, h3, (a3, hh1, gc1, act1, dn1), g2_1, dh4, "l1")
    (dp2,), (dg1_1, db_pw2) = _rw_bwd("l1_res1_bwd", _f_gate_res_bias, [h2, p2], [g1_1, b_pw2], [dh3],
                                      row_grad=[False, True], par_grad=[True, True])
    dls = _mm(dp2, w_pw2, tb=True, name="l1_pw2_dx")
    dw_pw2 = _mm(ls, dp2, ta=True, name="l1_pw2_dw")
    (dcv,), (dln_w, dln_b) = _rw_bwd("l1_ln_silu_bwd", _f_ln_silu, [cv], [ln_w, ln_b], [dls],
                                     row_grad=[True], par_grad=[True, True])
    dglu, dw_dw, db_dw = _conv_bwd("l1_conv_bwd", glu, 0, D, Wf["conf_w_dw"][0], dcv, conf_taps)
    (dpa, dpg), (dba, dbg) = _rw_bwd("l1_glu_bwd", _f_glu, [(pw, 0, D), (pw, D, D)],
                                     [_par(b_pw1[:D]), _par(b_pw1[D:])], [dglu],
                                     row_grad=[True, True], par_grad=[True, True])
    dpw = jnp.concatenate([dpa, dpg], axis=1)
    da2 = _mm(dpw, w_pw1, tb=True, name="l1_pw1_dx")
    dw_pw1 = _mm(a2, dpw, ta=True, name="l1_pw1_dw")
    (dh2,), (dn1w1, dsc1_1, dsh1_1) = _rw_bwd(
        "l1_modnorm1_bwd", _f_modnorm, [h2], [n1w1, _par(sc1[1]), _par(sh1[1])], [da2],
        row_grad=[True], par_grad=[True, True, True], add=dh3)
    G_full["conf_b_pw2"] = db_pw2.reshape(1, D)
    G_full["conf_ln_w"], G_full["conf_ln_b"] = dln_w.reshape(1, D), dln_b.reshape(1, D)
    G_full["conf_w_dw"], G_full["conf_b_dw"] = dw_dw[None], db_dw.reshape(1, D)
    G_full["conf_b_pw1"] = jnp.concatenate([dba.reshape(1, D), dbg.reshape(1, D)], axis=1)

    token = start_reduce("l1", [("conf_w_pw2", dw_pw2, 0), ("conf_w_pw1", dw_pw1, 1), ("ffn_w_up1", gf1["w_up"], 1),
                                ("ffn_w_down1", gf1["w_down"], 0)], dw_pw2)
    dh2 = _tie("tie_reduce_l1", dh2, token)
    dh1, gf0 = ffn_bwd(0, h1, (a1, hh0, gc0, act0, dn0), g2_0, dh2, "l0")
    G_full["ffn_conv_w"] = jnp.stack([gf0["conv_w"], gf1["conv_w"]])
    G_full["ffn_conv_b"] = jnp.stack([gf0["conv_b"], gf1["conv_b"]])

    (dmix,), (dg1_0,) = _rw_bwd("l0_res1_bwd", _f_gate_res, [xl, mix0], [g1_0], [dh1],
                                row_grad=[False, True], par_grad=[True])
    dyn = _mm(dmix, w_out, tb=True, name="l0_w_out_dx")
    dw_out = _mm(yn, dmix, ta=True, name="l0_w_out_dw")
    token = start_reduce("l0", [("ffn_w_up0", gf0["w_up"], 1), ("ffn_w_down0", gf0["w_down"], 0),
                                ("ssd_w_out", dw_out, 0)], dw_out)
    dyn = _tie("tie_reduce_l0", dyn, token)
    (dy_lat, dxs_gate, dz_lat), (dd_rep, dssd_nw) = _rw_bwd(
        "l0_ssd_gate_bwd", _f_ssd_gate, gate_rows, [d_rep, ssd_nw], [dyn],
        row_grad=[True, False, True, True], par_grad=[True, True], T=L)
    g_f, g_b = _ssd_bwd(xbc, DI, DI + G * N, dtr, a_neg, s_enter, dy_lat, P, ncc)
    dxs_gate_all = jnp.pad(dxs_gate, ((Lc, 0), (0, 0)))
    silu_bwd = functools.partial(_rw_bwd, f=_silu, pars=[], row_grad=[True], par_grad=[], T=T0)
    (dxs_pre,), _ = silu_bwd("l0_silu_bwd_x", rows=[(xbc_pre, 0, DI)], cot_fn=lambda p, q, r: p + q + r,
                             cots=[g_f[0], g_b[0], dxs_gate_all], col_tile=_tile(DI, 1024))
    (db_pre,), _ = silu_bwd("l0_silu_bwd_b", rows=[(xbc_pre, DI, G * N)], cot_fn=lambda p, q: p + q,
                            cots=[g_f[1], g_b[1]], col_tile=_tile(G * N, 1024))
    (dc_pre,), _ = silu_bwd("l0_silu_bwd_c", rows=[(xbc_pre, DI + G * N, G * N)], cot_fn=lambda p, q: p + q,
                            cots=[g_f[2], g_b[2]], col_tile=_tile(G * N, 1024))
    dxbc_pre = jnp.concatenate([dxs_pre, db_pre, dc_pre], axis=1)
    dconv_in, dcw0, dcb0 = _conv_bwd("l0_conv_bwd", proj, DI, CD, Wf["ssd_conv_w"][0], dxbc_pre, seg_taps)
    ddt = jnp.concatenate([g_f[3][:, 0, :].T, g_b[3][:, 0, :].T], axis=1)
    (ddt_raw,), (ddt_bias,) = _rw_bwd("l0_softplus_bwd", _f_softplus, [dt_raw], [dt_bias], [ddt],
                                      row_grad=[True], par_grad=[True])
    dproj = jnp.concatenate([jnp.pad(dz_lat, ((Lc, 0), (0, 0))), dconv_in, ddt_raw], axis=1)
    da0 = _mm(dproj, w_in, tb=True, name="l0_w_in_dx")
    dw_in = _mm(a0, dproj, ta=True, name="l0_w_in_dw")
    token = start_reduce("in", [("ssd_w_in", dw_in, 1)], dw_in)
    da0 = _tie("tie_reduce_in", da0, token)
    (dhcat,), (dn1w0, dsc_seg, dsh_seg) = _rw_bwd(
        "l0_modnorm1_bwd", _f_modnorm, [hcat], [n1w0, sc_seg, sh_seg], [da0],
        row_grad=[True], par_grad=[True, True, True], seg_rows=(Lc,))
    grad_x = (dhcat[Lc:] + dh1)[None]

    da_heads = jnp.stack([g[4][:, 0, 0].reshape(G, T0 // SSD_CHUNK, H // G).sum(axis=1).reshape(H)
                          for g in (g_f, g_b)])[None]
    G_full["ssd_a_log"] = da_heads * (-jnp.exp(ssd_a_log))
    G_full["ssd_dt_bias"] = ddt_bias.reshape(1, 2, H)
    G_full["ssd_d"] = dd_rep.reshape(H, P).sum(axis=1)[None]
    G_full["ssd_norm_w"] = dssd_nw.reshape(1, DI)
    G_full["ssd_conv_w"], G_full["ssd_conv_b"] = dcw0[None], dcb0.reshape(1, CD)
    G_full["norm1_w"] = jnp.stack([dn1w0.reshape(D), dn1w1.reshape(D)])
    G_full["norm2_w"] = jnp.stack([gf0["n2w"], gf1["n2w"]])

    zD = jnp.zeros((D,), F32)
    dm_own = jnp.stack([
        jnp.concatenate([dsh_seg[1, 0], dsc_seg[1, 0], dg1_0.reshape(D), gf0["sh2"], gf0["sc2"], gf0["g2"]]),
        jnp.concatenate([dsh1_1.reshape(D), dsc1_1.reshape(D), dg1_1.reshape(D), gf1["sh2"], gf1["sc2"], gf1["g2"]]),
    ])
    dmc_own = jnp.concatenate([dsh_seg[0, 0], dsc_seg[0, 0], zD, zD, zD, zD])

    small_sum_names = [n for n in SMALL if n not in ("c_ctx", "mod_b")]
    sum_part = [G_full[n] for n in small_sum_names] + [dmc_own]
    n_sum = sum(int(a.size) for a in sum_part)
    packed = _pack(sum_part + [dm_own])
    gat = _allgather8("gather_small_grads", packed)
    total = _sum_leading("sum_small_grads", gat, tuple(range(N_DEV)))
    summed = _unpack(total, [a.shape for a in sum_part])
    Gs = dict(zip(small_sum_names, summed[:-1]))
    dmc_tot = summed[-1]
    dm_all = jnp.stack([gat[k].reshape(-1)[n_sum:n_sum + 2 * 6 * D].reshape(2, 6 * D) for k in range(N_DEV)], axis=1)
    dm16 = jnp.concatenate([dm_all, jnp.stack([dmc_tot, jnp.zeros_like(dmc_tot)])[:, None, :],
                            jnp.zeros((2, 16 - N_DEV - 1, 6 * D), F32)], axis=1)
    Gs["mod_b"] = _sum_leading("sum_mod_b", dm16.transpose(1, 0, 2).reshape(16, 2 * 6 * D // LANE, LANE),
                               tuple(range(N_DEV + 1))).reshape(2, 6 * D)

    dm16_shard = lax.dynamic_slice_in_dim(dm16, chip * S_mod, S_mod, axis=2)
    ds16 = _mm(dm16_shard[0], mod_w[0], tb=True, precision=HIGHEST, name="c_ctx_dx")
    sig = jax.nn.sigmoid(c_ctx)
    dcc_part = ds16[N_DEV] * (sig * (1.0 + c_ctx * (1.0 - sig)))
    gat_cc = _allgather8("gather_c_ctx_grad", _pack([dcc_part]))
    Gs["c_ctx"] = _sum_leading("sum_c_ctx_grad", gat_cc, (0, 2, 4, 6)).reshape(-1)[:D]

    s16t = _silu(c16).T
    out = {}
    out["mod_w"] = _mod_w_update(s16t, dm16_shard, mod_w, m_mod_w, v_mod_w)

    late = out["mod_w"][0]
    partial = {}
    for tag, (names, handle, blocks) in reduces.items():
        landed = _exchange4_wait("reduce_" + tag + "_wait", handle, late)
        for n, blk, own in zip(names, landed, blocks):
            r = _fill_own("own_grad_" + n, blk, own, False)
            partial[n] = _sum_leading("sum4_" + n, r.reshape(N_CHIPS, -1, r.shape[-1]), (0, 1, 2, 3)).reshape(r.shape[1:])
    for n in ("ffn_w_up", "ffn_w_down"):
        partial[n] = jnp.stack([partial.pop(n + "0"), partial.pop(n + "1")])
    partial = [partial[n].reshape(W[n].shape) for n in BIG]
    sibling = _swap_sibling("swap_grads", partial)
    for n, mine, sib in zip(BIG, partial, sibling):
        out[n] = _adamw("adamw_" + n, W[n], Mo[n], Vo[n], mine, sib)

    def own(n, full):
        if n in SHARD_AXIS:
            size = W[n].shape[SHARD_AXIS[n]]
            return lax.dynamic_slice_in_dim(full, chip * size, size, axis=SHARD_AXIS[n])
        return full

    g_small = [own(n, Gs[n].reshape(Wf[n].shape)) for n in SMALL]
    shapes = [W[n].shape for n in SMALL]
    pk = [_pack([W[n] for n in SMALL]), _pack([Mo[n] for n in SMALL]), _pack([Vo[n] for n in SMALL]), _pack(g_small)]
    res = _adamw("adamw_small", pk[0], pk[1], pk[2], pk[3], jnp.zeros_like(pk[3]))
    unpacked = [_unpack(r, shapes) for r in res]
    for k, n in enumerate(SMALL):
        out[n] = tuple(u[k] for u in unpacked)

    grads = [out[n][0] for n in WEIGHTS]
    deltas = [out[n][1] for n in WEIGHTS]
    new_m = [out[n][2] for n in WEIGHTS]
    new_v = [out[n][3] for n in WEIGHTS]
    return (loss, grad_x, *grads, *deltas, *new_m, *new_v)
```

```python
import functools

import jax
import jax.numpy as jnp
from jax import lax
from jax.experimental import pallas as pl
from jax.experimental.pallas import tpu as pltpu

F32 = jnp.float32
BF16 = jnp.bfloat16
MESH = pl.DeviceIdType.MESH
HIGHEST = lax.Precision.HIGHEST

VMEM_LIMIT_BYTES = 48 * 1024 * 1024
LANE = 128
SUBLANE = 8

SSD_STATE = 128
SSD_CHUNK = 128
GRID_W = 64
EPS = 1e-6
N_CHIPS = 4
N_DEV = 8

ADAM_LR = 0.001
ADAM_B1 = 0.9
ADAM_B2 = 0.999
ADAM_EPS = 1e-08
ADAM_WD = 0.01
ADAM_STEP = 10


def _pcall(body, **kw):
    return pl.pallas_call(body, **kw)


def _cparams(n_grid):
    return pltpu.CompilerParams(dimension_semantics=("arbitrary",) * n_grid, vmem_limit_bytes=VMEM_LIMIT_BYTES)


def _cdiv(a, b):
    return -(-a // b)


def _round_up(a, b):
    return _cdiv(a, b) * b


def _tile(n, cap):
    if n <= cap:
        return n
    best = None
    for t in range(LANE, cap + 1, LANE):
        if n % t == 0:
            best = t
    if best is None:
        npad = _round_up(n, LANE)
        for t in range(LANE, cap + 1, LANE):
            if npad % t == 0:
                best = t
    return best


def _row_tile(n, cap, also=()):
    best = None
    for step in (2 * SUBLANE, SUBLANE):
        for t in range(step, min(cap, n) + 1, step):
            if n % t == 0 and all(a % t == 0 for a in also):
                best = t
        if best is not None:
            break
    assert best is not None, (n, cap, also)
    return best


def _silu(v):
    return v * jax.nn.sigmoid(v)


def _mm(a, b, *, name, ta=False, tb=False, precision=None, cap=1024, out_dtype=F32, col_blocks=None):
    M, K = (a.shape[1], a.shape[0]) if ta else a.shape
    N = b.shape[0] if tb else b.shape[1]
    assert K == (b.shape[1] if tb else b.shape[0]), (a.shape, b.shape, ta, tb)
    tm, tk = _tile(M, cap), _tile(K, cap)
    tn = _tile(N, cap) if col_blocks is None else _tile(N // col_blocks, cap + cap // 2)
    nm, nn, nk = _cdiv(M, tm), _cdiv(N, tn), _cdiv(K, tk)
    k_tail = K % tk
    exact = precision is not None

    def body(a_ref, b_ref, o_ref, acc_ref):
        k = pl.program_id(2)

        @pl.when(k == 0)
        def _():
            acc_ref[...] = jnp.zeros_like(acc_ref)

        av = a_ref[...]
        bv = b_ref[...]
        if k_tail:
            lim = K - k * tk
            ka = lax.broadcasted_iota(jnp.int32, av.shape, 0 if ta else 1)
            kb = lax.broadcasted_iota(jnp.int32, bv.shape, 1 if tb else 0)
            av = jnp.where(ka < lim, av, jnp.zeros_like(av))
            bv = jnp.where(kb < lim, bv, jnp.zeros_like(bv))
        if exact:
            av = av.astype(F32)
            bv = bv.astype(F32)
        else:
            av = av.astype(BF16)
            bv = bv.astype(BF16)
        dn = (((0 if ta else 1,), (1 if tb else 0,)), ((), ()))
        acc_ref[...] += lax.dot_general(av, bv, dn, preferred_element_type=F32, precision=precision)

        @pl.when(k == nk - 1)
        def _():
            o_ref[...] = acc_ref[...].astype(o_ref.dtype)

    a_spec = pl.BlockSpec((tk, tm), lambda i, j, k: (k, i)) if ta else pl.BlockSpec((tm, tk), lambda i, j, k: (i, k))
    b_spec = pl.BlockSpec((tn, tk), lambda i, j, k: (j, k)) if tb else pl.BlockSpec((tk, tn), lambda i, j, k: (k, j))
    if col_blocks is None:
        out_spec = pl.BlockSpec((tm, tn), lambda i, j, k: (i, j))
        out_shape = jax.ShapeDtypeStruct((M, N), out_dtype)
    else:
        per = (N // col_blocks) // tn
        assert per * tn * col_blocks == N, (N, col_blocks, tn)
        out_spec = pl.BlockSpec((None, tm, tn), lambda i, j, k: (j // per, i, j % per))
        out_shape = jax.ShapeDtypeStruct((col_blocks, M, N // col_blocks), out_dtype)
    return _pcall(
        body, name=name, grid=(nm, nn, nk), in_specs=[a_spec, b_spec], out_specs=out_spec, out_shape=out_shape,
        scratch_shapes=[pltpu.VMEM((tm, tn), F32)], compiler_params=_cparams(3),
    )(a, b)


def _norm_rows(rows):
    out = []
    for r in rows:
        if not isinstance(r, tuple):
            r = (r,)
        arr, off, width, roff = (r + (0, None, 0)[len(r) - 1:])
        out.append((arr, off, width if width is not None else arr.shape[1], roff))
    return out


def _rw_plan(T, rows, pars, seg_rows, col_tile, tm_cap):
    widths = [r[2] for r in rows]
    wmax = max(widths + [p.shape[-1] for p in pars] + [1])
    if col_tile is not None:
        assert all(w == widths[0] for w in widths) and all(p.shape[-1] == widths[0] for p in pars)
        ncol = widths[0] // col_tile
        assert ncol * col_tile == widths[0]
        wmax = col_tile
    else:
        ncol = 1
    cap = tm_cap if tm_cap is not None else max(SUBLANE, min(256, (256 * 1024) // wmax))
    tm = _row_tile(T, cap, also=tuple(seg_rows) + tuple(r[3] for r in rows if r[3]))
    bounds = tuple(s // tm for s in seg_rows)
    return widths, ncol, tm, bounds


def _rw_specs(rows, pars, ncol, tm, bounds, col_tile):
    def seg(i):
        s = 0
        for b in bounds:
            s = s + (i >= b).astype(jnp.int32)
        return s

    specs = []
    for arr, off, w, roff in rows:
        bw = col_tile if col_tile is not None else w
        assert off % bw == 0 and roff % tm == 0, (off, bw, roff, tm)
        specs.append(pl.BlockSpec((tm, bw), functools.partial(lambda j, i, ob, rb: (i + rb, ob + j),
                                                              ob=off // bw, rb=roff // tm)))
    for p in pars:
        bw = col_tile if col_tile is not None else p.shape[-1]
        if p.shape[0] > 1:
            specs.append(pl.BlockSpec((None, 1, bw), lambda j, i: (seg(i), 0, j)))
        else:
            specs.append(pl.BlockSpec((None, 1, bw), lambda j, i: (0, 0, j)))
    return specs, seg


def _rw_fwd(name, f, rows, pars, out_widths, *, T=None, seg_rows=(), col_tile=None, tm_cap=None, out_dtypes=None):
    rows = _norm_rows(rows)
    T = rows[0][0].shape[0] if T is None else T
    widths, ncol, tm, bounds = _rw_plan(T, rows, pars, seg_rows, col_tile, tm_cap)
    in_specs, _ = _rw_specs(rows, pars, ncol, tm, bounds, col_tile)
    nr, npar, nout = len(rows), len(pars), len(out_widths)

    def body(*refs):
        vals = [r[...] for r in refs[:nr + npar]]
        outs = f(*vals)
        if not isinstance(outs, (tuple, list)):
            outs = (outs,)
        for o_ref, o in zip(refs[nr + npar:], outs):
            o_ref[...] = o.astype(o_ref.dtype)

    out_specs = [pl.BlockSpec((tm, col_tile if col_tile is not None else w), lambda j, i: (i, j)) for w in out_widths]
    res = _pcall(
        body, name=name, grid=(ncol, T // tm), in_specs=in_specs, out_specs=out_specs,
        out_shape=[jax.ShapeDtypeStruct((T, w), dt) for w, dt in zip(out_widths, out_dtypes or [F32] * nout)],
        compiler_params=_cparams(2),
    )(*[r[0] for r in rows], *pars)
    return res if nout > 1 else res[0]


def _rw_bwd(name, f, rows, pars, cots, *, row_grad, par_grad, T=None, seg_rows=(), col_tile=None, tm_cap=None,
            add=None, cot_fn=None, row_dtypes=None):
    rows = _norm_rows(rows)
    cots = _norm_rows(cots)
    T = rows[0][0].shape[0] if T is None else T
    extra = _norm_rows([add]) if add is not None else []
    all_rows = rows + cots + extra
    widths, ncol, tm, bounds = _rw_plan(T, all_rows, pars, seg_rows, col_tile, tm_cap)
    in_specs, seg = _rw_specs(all_rows, pars, ncol, tm, bounds, col_tile)
    nr, nc, ne, npar = len(rows), len(cots), len(extra), len(pars)
    row_idx = [k for k in range(nr) if row_grad[k]]
    par_idx = [k for k in range(npar) if par_grad[k]]

    def body(*refs):
        i = pl.program_id(1)
        row_vals = [r[...] for r in refs[:nr]]
        cot_vals = [r[...] for r in refs[nr:nr + nc]]
        add_vals = [r[...] for r in refs[nr + nc:nr + nc + ne]]
        par_vals = [r[...] for r in refs[nr + nc + ne:nr + nc + ne + npar]]
        out_refs = refs[nr + nc + ne + npar:]
        outs, vjp = jax.vjp(f, *row_vals, *par_vals)
        if cot_fn is not None:
            cot_vals = cot_fn(*cot_vals)
            if not isinstance(cot_vals, (tuple, list)):
                cot_vals = (cot_vals,)
        if isinstance(outs, (tuple, list)):
            grads = vjp(tuple(c.astype(o.dtype) for c, o in zip(cot_vals, outs)))
        else:
            grads = vjp(cot_vals[0].astype(outs.dtype))
        first_seg = i == 0
        for b in bounds:
            first_seg = first_seg | (i == b)
        for n, k in enumerate(row_idx):
            g = grads[k]
            if n == 0 and add_vals:
                g = g + add_vals[0]
            out_refs[n][...] = g.astype(out_refs[n].dtype)
        for n, k in enumerate(par_idx):
            g = grads[nr + k]
            o_ref = out_refs[len(row_idx) + n]
            first = first_seg if pars[k].shape[0] > 1 else (i == 0)

            @pl.when(first)
            def _(o_ref=o_ref, g=g):
                o_ref[...] = g

            @pl.when(jnp.logical_not(first))
            def _(o_ref=o_ref, g=g):
                o_ref[...] += g

    out_specs, out_shape = [], []
    for k in row_idx:
        w = widths[k]
        out_specs.append(pl.BlockSpec((tm, col_tile if col_tile is not None else w), lambda j, i: (i, j)))
        out_shape.append(jax.ShapeDtypeStruct((T, w), row_dtypes[len(out_shape)] if row_dtypes else F32))
    for k in par_idx:
        p = pars[k]
        bw = col_tile if col_tile is not None else p.shape[-1]
        if p.shape[0] > 1:
            out_specs.append(pl.BlockSpec((None, 1, bw), lambda j, i: (seg(i), 0, j)))
        else:
            out_specs.append(pl.BlockSpec((None, 1, bw), lambda j, i: (0, 0, j)))
        out_shape.append(jax.ShapeDtypeStruct(p.shape, F32))
    res = _pcall(
        body, name=name, grid=(ncol, T // tm), in_specs=in_specs, out_specs=out_specs, out_shape=out_shape,
        compiler_params=_cparams(2),
    )(*[r[0] for r in all_rows], *pars)
    return list(res[:len(row_idx)]), list(res[len(row_idx):])


def _f_modnorm(h, w, sc, sh):
    y = h * lax.rsqrt(jnp.mean(h * h, axis=-1, keepdims=True) + EPS)
    return (y * w) * (1.0 + sc) + sh


def _f_gate_res(h, y, g):
    return h + g * y


def _f_gate_res_bias(h, y, g, b):
    return h + g * (y + b)


def _f_ffn_act(val, gate):
    return _silu(gate) * val


def _f_softplus(raw, bias):
    v = raw + bias
    return jnp.maximum(v, 0.0) + jnp.log(1.0 + jnp.exp(-jnp.abs(v)))


def _f_ssd_gate(yf, yb, xs, z, d_rep, nw):
    y = (yf + yb + d_rep * xs) * _silu(z)
    return (y * lax.rsqrt(jnp.mean(y * y, axis=-1, keepdims=True) + EPS)) * nw


def _f_glu(a, g, ba, bg):
    return (a + ba) * jax.nn.sigmoid(g + bg)


def _f_ln_silu(h, w, b):
    mu = jnp.mean(h, axis=-1, keepdims=True)
    d = h - mu
    y = d * lax.rsqrt(jnp.mean(d * d, axis=-1, keepdims=True) + EPS)
    return _silu(y * w + b)


def _f_loss_rows(h, t, w):
    y = (h * lax.rsqrt(jnp.mean(h * h, axis=-1, keepdims=True) + EPS)) * w
    e = y - t
    return 0.5 * jnp.mean(e * e, axis=-1, keepdims=True)


def _f_adamw(w, m, v, ga, gb):
    g = ga + gb
    m = ADAM_B1 * m + (1.0 - ADAM_B1) * g
    v = ADAM_B2 * v + (1.0 - ADAM_B2) * (g * g)
    m_hat = m / (1.0 - ADAM_B1 ** ADAM_STEP)
    v_hat = v / (1.0 - ADAM_B2 ** ADAM_STEP)
    delta = -ADAM_LR * (m_hat / (jnp.sqrt(v_hat) + ADAM_EPS) + ADAM_WD * w)
    return g, delta, m, v


def _adamw(name, w, m, v, ga, gb):
    shape = w.shape
    c = shape[-1]
    two_d = [t.reshape(-1, c) for t in (w, m, v, ga, gb)]
    rows = two_d[0].shape[0]
    pad = _round_up(rows, SUBLANE) - rows
    if pad:
        two_d = [jnp.pad(t, ((0, pad), (0, 0))) for t in two_d]
    outs = _rw_fwd(name, _f_adamw, two_d, [], [c] * 4)
    return tuple(o[:rows].reshape(shape) for o in outs)


def _sum_leading(name, x, idxs):
    _, R, C = x.shape
    tm = _row_tile(R, max(SUBLANE, min(512, (512 * 1024) // C)))

    def body(x_ref, o_ref):
        acc = x_ref[idxs[0]].astype(F32)
        for k in idxs[1:]:
            acc = acc + x_ref[k].astype(F32)
        o_ref[...] = acc

    return _pcall(
        body, name=name, grid=(R // tm,), in_specs=[pl.BlockSpec((x.shape[0], tm, C), lambda i: (0, i, 0))],
        out_specs=pl.BlockSpec((tm, C), lambda i: (i, 0)), out_shape=jax.ShapeDtypeStruct((R, C), F32),
        compiler_params=_cparams(1),
    )(x)


def _loss_fwd(h, t, w):
    T, D = h.shape
    tm = _row_tile(T, 256)

    def body(h_ref, t_ref, w_ref, o_ref):
        i = pl.program_id(0)
        part = jnp.sum(_f_loss_rows(h_ref[...], t_ref[...], w_ref[...]), axis=0, keepdims=True)
        part = jnp.broadcast_to(part, (1, LANE))

        @pl.when(i == 0)
        def _():
            o_ref[...] = part

        @pl.when(i > 0)
        def _():
            o_ref[...] += part

    return _pcall(
        body, name="loss_fwd", grid=(T // tm,),
        in_specs=[pl.BlockSpec((tm, D), lambda i: (i, 0)), pl.BlockSpec((tm, D), lambda i: (i, 0)),
                  pl.BlockSpec((1, D), lambda i: (0, 0))],
        out_specs=pl.BlockSpec((1, LANE), lambda i: (0, 0)), out_shape=jax.ShapeDtypeStruct((1, LANE), F32),
        compiler_params=_cparams(1),
    )(h, t, w)


CONV_ROWS = 256
CONV_ACC_ELEMS = 16384


def _tap_mask(mask, t, s):
    if mask is None:
        return None
    kind, arg = mask
    if kind == "seg":
        if s == 0:
            return None
        return (t >= arg) == ((t + s) >= arg)
    col = jnp.bitwise_and(t, GRID_W - 1)
    return (col != 0) if arg < 0 else (col != GRID_W - 1)


def _conv_plan(T, C, taps):
    rc = CONV_ROWS if T % CONV_ROWS == 0 else LANE
    assert T % rc == 0
    ct = next((t for t in (512, 256, LANE) if C % t == 0), C)
    reach = max(abs(s) for s, _ in taps)
    hb = next(h for h in (8, 16, 32, 64, 128, 256) if h >= reach and rc % h == 0)
    sub = max(SUBLANE, min(rc, CONV_ACC_ELEMS // ct))
    return rc, ct, hb, sub, T // rc, C // ct


def _halo_specs(rc, ct, hb, T, off_blocks):
    per = rc // hb
    last = T // hb - 1
    prev = pl.BlockSpec((hb, ct), lambda j, i: (jnp.maximum(i * per - 1, 0), off_blocks + j))
    cur = pl.BlockSpec((rc, ct), lambda j, i: (i, off_blocks + j))
    nxt = pl.BlockSpec((hb, ct), lambda j, i: (jnp.minimum((i + 1) * per, last), off_blocks + j))
    return [prev, cur, nxt]


def _fill_halo(pad_ref, p_ref, c_ref, n_ref, i, nrc, rc, hb):
    pad_ref[0:hb, :] = jnp.where(i > 0, p_ref[...], 0.0)
    pad_ref[hb:hb + rc, :] = c_ref[...]
    pad_ref[hb + rc:hb + rc + hb, :] = jnp.where(i < nrc - 1, n_ref[...], 0.0)


def _conv_fwd(name, u, col_off, C, w, b, taps, act=False):
    T = u.shape[0]
    rc, ct, hb, sub, nrc, ncc = _conv_plan(T, C, taps)
    assert col_off % ct == 0
    K = len(taps)

    def body(up, uc, un, w_ref, b_ref, *rest):
        y_ref = rest[0]
        pad_ref = rest[-1]
        i = pl.program_id(1)
        _fill_halo(pad_ref, up, uc, un, i, nrc, rc, hb)
        for r0 in range(0, rc, sub):
            t = i * rc + r0 + lax.broadcasted_iota(jnp.int32, (sub, 1), 0)
            acc = jnp.broadcast_to(b_ref[...], (sub, ct))
            for k, (s, mask) in enumerate(taps):
                v = pad_ref[hb + r0 + s:hb + r0 + s + sub, :]
                m = _tap_mask(mask, t, s)
                if m is not None:
                    v = jnp.where(m, v, 0.0)
                acc = acc + w_ref[k:k + 1, :] * v
            y_ref[r0:r0 + sub, :] = acc
            if act:
                rest[1][r0:r0 + sub, :] = _silu(acc)

    n_out = 2 if act else 1
    res = _pcall(
        body, name=name, grid=(ncc, nrc),
        in_specs=_halo_specs(rc, ct, hb, T, col_off // ct) + [pl.BlockSpec((K, ct), lambda j, i: (0, j)),
                                                              pl.BlockSpec((1, ct), lambda j, i: (0, j))],
        out_specs=[pl.BlockSpec((rc, ct), lambda j, i: (i, j))] * n_out,
        out_shape=[jax.ShapeDtypeStruct((T, C), F32)] * n_out,
        scratch_shapes=[pltpu.VMEM((rc + 2 * hb, ct), F32)], compiler_params=_cparams(2),
    )(u, u, u, w, b)
    return res if act else res[0]


def _conv_bwd(name, u, col_off, C, w, g, taps, du_dtype=F32):
    T = u.shape[0]
    rc, ct, hb, sub, nrc, ncc = _conv_plan(T, C, taps)
    K = len(taps)

    def body(up, uc, un, gp, gc, gn, w_ref, du_ref, dw_ref, db_ref, upad, gpad):
        i = pl.program_id(1)
        _fill_halo(upad, up, uc, un, i, nrc, rc, hb)
        _fill_halo(gpad, gp, gc, gn, i, nrc, rc, hb)

        @pl.when(i == 0)
        def _():
            dw_ref[...] = jnp.zeros_like(dw_ref)
            db_ref[...] = jnp.zeros_like(db_ref)

        def fold(v):
            return jnp.sum(v.reshape(sub // SUBLANE, SUBLANE, ct), axis=0)

        dws = [jnp.zeros((SUBLANE, ct), F32) for _ in range(K)]
        dbs = jnp.zeros((SUBLANE, ct), F32)
        for r0 in range(0, rc, sub):
            t = i * rc + r0 + lax.broadcasted_iota(jnp.int32, (sub, 1), 0)
            gv = gpad[hb + r0:hb + r0 + sub, :]
            dbs = dbs + fold(gv)
            acc = jnp.zeros((sub, ct), F32)
            for k, (s, mask) in enumerate(taps):
                gs = gpad[hb + r0 - s:hb + r0 - s + sub, :]
                m = _tap_mask(mask, t - s, s)
                if m is not None:
                    gs = jnp.where(m, gs, 0.0)
                acc = acc + w_ref[k:k + 1, :] * gs
                uv = upad[hb + r0 + s:hb + r0 + s + sub, :]
                m = _tap_mask(mask, t, s)
                prod = gv * uv
                if m is not None:
                    prod = jnp.where(m, prod, 0.0)
                dws[k] = dws[k] + fold(prod)
            du_ref[r0:r0 + sub, :] = acc.astype(du_ref.dtype)
        for k in range(K):
            dw_ref[k:k + 1, :] += jnp.sum(dws[k], axis=0, keepdims=True)
        db_ref[...] += jnp.sum(dbs, axis=0, keepdims=True)

    halo_u = _halo_specs(rc, ct, hb, T, col_off // ct)
    halo_g = _halo_specs(rc, ct, hb, T, 0)
    return _pcall(
        body, name=name, grid=(ncc, nrc),
        in_specs=halo_u + halo_g + [pl.BlockSpec((K, ct), lambda j, i: (0, j))],
        out_specs=[pl.BlockSpec((rc, ct), lambda j, i: (i, j)), pl.BlockSpec((K, ct), lambda j, i: (0, j)),
                   pl.BlockSpec((1, ct), lambda j, i: (0, j))],
        out_shape=[jax.ShapeDtypeStruct((T, C), du_dtype), jax.ShapeDtypeStruct((K, C), F32),
                   jax.ShapeDtypeStruct((1, C), F32)],
        scratch_shapes=[pltpu.VMEM((rc + 2 * hb, ct), F32), pltpu.VMEM((rc + 2 * hb, ct), F32)],
        compiler_params=_cparams(2),
    )(u, u, u, g, g, g, w)


def _ssd_group(xg, bm, cm, s_in, *per_head, reverse, P):
    R = len(per_head) // 2
    dtrs, a_s = per_head[:R], per_head[R:]
    q, rp = xg.shape
    ii = lax.broadcasted_iota(jnp.int32, (q, q), 0)
    jj = lax.broadcasted_iota(jnp.int32, (q, q), 1)
    causal = (jj >= ii) if reverse else (jj <= ii)
    causal_t = (ii >= jj) if reverse else (ii <= jj)
    eye = ii == jj
    lane = lax.broadcasted_iota(jnp.int32, (1, rp), 1)
    row = lax.broadcasted_iota(jnp.int32, (rp, 1), 0)
    nt = (((1,), (1,)), ((), ()))
    tn = (((0,), (0,)), ((), ()))
    cb = lax.dot_general(cm.astype(BF16), bm.astype(BF16), nt, preferred_element_type=F32)
    dt_x = jnp.zeros((q, rp), F32)
    acum_x = jnp.zeros((q, rp), F32)
    tot_row = jnp.zeros((1, rp), F32)
    tot_col = jnp.zeros((rp, 1), F32)
    wts, lane_masks = [], []
    for r in range(R):
        hm = (lane >= r * P) & (lane < (r + 1) * P)
        hc = (row >= r * P) & (row < (r + 1) * P)
        dt_c = jnp.sum(jnp.where(eye, dtrs[r], 0.0), axis=1, keepdims=True)
        dac = dt_c * a_s[r]
        dar = dtrs[r] * a_s[r]
        acum_c = jnp.sum(jnp.where(causal, dar, 0.0), axis=1, keepdims=True)
        acum_r = jnp.sum(jnp.where(causal_t, dac, 0.0), axis=0, keepdims=True)
        decay = jnp.where(causal, jnp.exp(jnp.where(causal, acum_c - acum_r, 0.0)), 0.0)
        tot = jnp.sum(dac, axis=0, keepdims=True)
        dt_x = jnp.where(hm, dt_c, dt_x)
        acum_x = jnp.where(hm, acum_c, acum_x)
        tot_row = jnp.where(hm, tot, tot_row)
        tot_col = jnp.where(hc, tot, tot_col)
        wts.append((cb * decay).astype(BF16))
        lane_masks.append(hm)
    xdt = xg * dt_x
    xdt_b = xdt.astype(BF16)
    y = jnp.zeros((q, rp), F32)
    for r in range(R):
        y = jnp.where(lane_masks[r], jnp.dot(wts[r], xdt_b, preferred_element_type=F32), y)
    dte = jnp.exp(tot_row - acum_x)
    cs = lax.dot_general((xdt * dte).astype(BF16), bm.astype(BF16), tn, preferred_element_type=F32)
    y = y + lax.dot_general(cm.astype(BF16), s_in.astype(BF16), nt, preferred_element_type=F32) * jnp.exp(acum_x)
    s_out = jnp.exp(tot_col) * s_in + cs
    return y, s_out


def _ssd_maps(NC, ncc, reverse_steps):
    def chunk(d, s):
        if reverse_steps:
            s = NC - 1 - s
        return s if d == 0 else jnp.where(s < ncc, ncc - 1 - s, NC - 1 - s + ncc)

    def lat_chunk(d, s):
        c = chunk(d, s) - ncc
        return jnp.where(c < 0, 0 if d == 0 else NC - ncc - 1, c)

    def step(s):
        return NC - 1 - s if reverse_steps else s

    return chunk, lat_chunk, step


def _ssd_specs(chunk, d, R, Q, N, RP, bo, co):
    return [
        pl.BlockSpec((Q, RP), lambda g, s: (chunk(d, s), g)),
        pl.BlockSpec((Q, N), lambda g, s: (chunk(d, s), bo + g)),
        pl.BlockSpec((Q, N), lambda g, s: (chunk(d, s), co + g)),
        pl.BlockSpec((R, 1, Q), lambda g, s: (g, 0, chunk(d, s))),
        pl.BlockSpec((R, 1, 1), lambda g, s: (g, 0, 0)),
    ]


def _ssd_fwd(xbc, b_off, c_off, dtr, a, P, ncc):
    T = xbc.shape[0]
    H = dtr[0].shape[0]
    N, Q = SSD_STATE, SSD_CHUNK
    NC = T // Q
    G = (c_off - b_off) // N
    R = H // G
    RP = R * P
    chunk, lat_chunk, _ = _ssd_maps(NC, ncc, False)

    def body(*refs):
        s = pl.program_id(1)
        s_ref = refs[-1]

        @pl.when(s == 0)
        def _():
            s_ref[...] = jnp.zeros_like(s_ref)

        for d in range(2):
            x_ref, b_ref, c_ref, dtr_ref, a_ref = refs[5 * d:5 * d + 5]
            y_ref, se_ref = refs[10 + 2 * d:12 + 2 * d]
            s_in = s_ref[d]
            se_ref[...] = s_in
            per_head = [dtr_ref[r] for r in range(R)] + [a_ref[r] for r in range(R)]
            y, s_out = _ssd_group(x_ref[...], b_ref[...], c_ref[...], s_in, *per_head, reverse=d == 1, P=P)
            y_ref[...] = y
            s_ref[d] = s_out

    in_specs, out_specs, out_shape, operands = [], [], [], []
    for d in range(2):
        in_specs += _ssd_specs(chunk, d, R, Q, N, RP, b_off // N, c_off // N)
        operands += [xbc, xbc, xbc, dtr[d], a[d]]
        out_specs += [pl.BlockSpec((Q, RP), functools.partial(lambda g, s, d: (lat_chunk(d, s), g), d=d)),
                      pl.BlockSpec((None, None, RP, N), lambda g, s: (g, s, 0, 0))]
        out_shape += [jax.ShapeDtypeStruct((T - ncc * Q, H * P), F32), jax.ShapeDtypeStruct((G, NC, RP, N), F32)]
    y_f, se_f, y_b, se_b = _pcall(
        body, name="ssd_fwd", grid=(G, NC), in_specs=in_specs, out_specs=out_specs, out_shape=out_shape,
        scratch_shapes=[pltpu.VMEM((2, RP, N), F32)], compiler_params=_cparams(2),
    )(*operands)
    return (y_f, y_b), (se_f, se_b)


def _ssd_bwd(xbc, b_off, c_off, dtr, a, s_enter, dy, P, ncc):
    T = xbc.shape[0]
    H = dtr[0].shape[0]
    N, Q = SSD_STATE, SSD_CHUNK
    NC = T // Q
    G = (c_off - b_off) // N
    R = H // G
    RP = R * P
    chunk, lat_chunk, step = _ssd_maps(NC, ncc, True)
    n_in, n_out = 7, 5

    def body(*refs):
        s = pl.program_id(1)
        ds_ref = refs[-1]

        @pl.when(s == 0)
        def _():
            ds_ref[...] = jnp.zeros_like(ds_ref)

        for d in range(2):
            x_ref, b_ref, c_ref, dtr_ref, a_ref, se_ref, dy_ref = refs[n_in * d:n_in * (d + 1)]
            dx_ref, db_ref, dc_ref, ddtr_ref, da_ref = refs[2 * n_in + n_out * d:2 * n_in + n_out * (d + 1)]
            per_head = [dtr_ref[r] for r in range(R)] + [a_ref[r] for r in range(R)]
            f = functools.partial(_ssd_group, reverse=d == 1, P=P)
            _, vjp = jax.vjp(f, x_ref[...], b_ref[...], c_ref[...], se_ref[...], *per_head)
            is_latent = chunk(d, s) >= ncc
            dy_v = jnp.where(is_latent, dy_ref[...], 0.0)
            grads = vjp((dy_v, ds_ref[d]))
            dx_ref[...] = grads[0]
            db_ref[...] = grads[1]
            dc_ref[...] = grads[2]
            ds_ref[d] = grads[3]
            for r in range(R):
                ddtr_ref[r] = grads[4 + r]
                da_ref[r] = jnp.broadcast_to(grads[4 + R + r], (SUBLANE, LANE))

    in_specs, out_specs, out_shape, operands = [], [], [], []
    for d in range(2):
        in_specs += _ssd_specs(chunk, d, R, Q, N, RP, b_off // N, c_off // N) + [
            pl.BlockSpec((None, None, RP, N), lambda g, s: (g, step(s), 0, 0)),
            pl.BlockSpec((Q, RP), functools.partial(lambda g, s, d: (lat_chunk(d, s), g), d=d)),
        ]
        operands += [xbc, xbc, xbc, dtr[d], a[d], s_enter[d], dy]
    for d in range(2):
        at_chunk = functools.partial(lambda g, s, d: (chunk(d, s), g), d=d)
        out_specs += [
            pl.BlockSpec((Q, RP), at_chunk), pl.BlockSpec((Q, N), at_chunk), pl.BlockSpec((Q, N), at_chunk),
            pl.BlockSpec((R, 1, Q), functools.partial(lambda g, s, d: (g, 0, chunk(d, s)), d=d)),
            pl.BlockSpec((R, SUBLANE, LANE), lambda g, s: (g * NC + s, 0, 0)),
        ]
        out_shape += [
            jax.ShapeDtypeStruct((T, H * P), F32), jax.ShapeDtypeStruct((T, G * N), F32),
            jax.ShapeDtypeStruct((T, G * N), F32), jax.ShapeDtypeStruct((H, 1, T), F32),
            jax.ShapeDtypeStruct((G * NC * R, SUBLANE, LANE), F32),
        ]
    res = _pcall(
        body, name="ssd_bwd", grid=(G, NC), in_specs=in_specs, out_specs=out_specs, out_shape=out_shape,
        scratch_shapes=[pltpu.VMEM((2, RP, N), F32)], compiler_params=_cparams(2),
    )(*operands)
    return res[:n_out], res[n_out:]


def _allgather8(name, v):
    R, C = v.shape

    def body(x_ref, out_ref, send_sems, recv_sems, local_sem):
        x, y, c = lax.axis_index("x"), lax.axis_index("y"), lax.axis_index("c")
        me, sibling = (x, y, c), (x, y, 1 - c)
        chips = [(1 - x, y), (x, 1 - y), (1 - x, 1 - y)]

        def slot(px, py, pc):
            return out_ref.at[4 * px + 2 * py + pc]

        def copy(k, block, to, src=None):
            return pltpu.make_async_remote_copy(
                src_ref=slot(*block) if src is None else src, dst_ref=slot(*block),
                send_sem=send_sems.at[k], recv_sem=recv_sems.at[k], device_id=to, device_id_type=MESH)

        mine = pltpu.make_async_copy(x_ref, slot(*me), local_sem)
        mine.start()
        first = [copy(0, me, sibling, src=x_ref)]
        first += [copy(1 + j, me, (*chip, c), src=x_ref) for j, chip in enumerate(chips)]
        for cp in first:
            cp.start()
        passed = [copy(4 + j, (*chip, c), sibling) for j, chip in enumerate(chips)]
        for j, chip in enumerate(chips):
            copy(1 + j, (*chip, c), me).wait_recv()
            passed[j].start()
        copy(0, sibling, me).wait_recv()
        for j, chip in enumerate(chips):
            copy(4 + j, (*chip, 1 - c), me).wait_recv()
        for cp in first + passed:
            cp.wait_send()
        mine.wait()

    return _pcall(
        body, name=name, out_shape=jax.ShapeDtypeStruct((N_DEV, R, C), v.dtype),
        in_specs=[pl.BlockSpec(memory_space=pltpu.VMEM)], out_specs=pl.BlockSpec(memory_space=pltpu.VMEM),
        scratch_shapes=[pltpu.SemaphoreType.DMA((7,)), pltpu.SemaphoreType.DMA((7,)), pltpu.SemaphoreType.DMA],
        compiler_params=pltpu.CompilerParams(vmem_limit_bytes=VMEM_LIMIT_BYTES),
    )(v)


def _exchange4_start(name, srcs, bcast, dep):
    n = len(srcs)
    lands = [lax.empty(((N_CHIPS,) + s.shape) if bcast else s.shape, s.dtype) for s in srcs]

    def body(*refs):
        src, land = refs[:n], refs[n:2 * n]
        send_sems, recv_sems = refs[2 * n + 1], refs[2 * n + 2]
        token = refs[-1]
        x, y, c = lax.axis_index("x"), lax.axis_index("y"), lax.axis_index("c")
        me = 2 * x + y
        for a in range(n):
            for j, (px, py) in enumerate([(1 - x, y), (x, 1 - y), (1 - x, 1 - y)]):
                pltpu.make_async_remote_copy(
                    src_ref=src[a] if bcast else src[a].at[2 * px + py], dst_ref=land[a].at[me],
                    send_sem=send_sems.at[3 * a + j], recv_sem=recv_sems.at[3 * a + j], device_id=(px, py, c),
                    device_id_type=MESH).start()
        token[...] = jnp.zeros_like(token)

    hbm = pl.BlockSpec(memory_space=pltpu.HBM)
    sem = pl.BlockSpec(memory_space=pltpu.SEMAPHORE)
    outs = _pcall(
        body, name=name,
        out_shape=(pltpu.SemaphoreType.DMA((3 * n,)), pltpu.SemaphoreType.DMA((3 * n,)),
                   *[pltpu.HBM(s.shape, s.dtype) for s in srcs], *[pltpu.HBM(l.shape, l.dtype) for l in lands],
                   jax.ShapeDtypeStruct((SUBLANE, LANE), F32)),
        in_specs=[hbm] * (2 * n) + [pl.BlockSpec(memory_space=pl.ANY)],
        out_specs=(sem, sem, *[hbm] * (2 * n), pl.BlockSpec(memory_space=pltpu.VMEM)),
        input_output_aliases={k: 2 + k for k in range(2 * n)},
        compiler_params=pltpu.CompilerParams(has_side_effects=pltpu.SideEffectType.DATAFLOW_SIDE_EFFECTING),
    )(*[pltpu.with_memory_space_constraint(s, pltpu.HBM) for s in srcs],
      *[pltpu.with_memory_space_constraint(l, pltpu.HBM) for l in lands], dep)
    return (n, bcast, outs[0], outs[1], outs[2:2 + n], outs[2 + n:2 + 2 * n]), outs[-1]


def _exchange4_wait(name, handle, after):
    n, bcast, send_sems, recv_sems, src_thru, land_thru = handle

    def body(*refs):
        src, land = refs[:n], refs[n:2 * n]
        send_sems, recv_sems = refs[2 * n], refs[2 * n + 1]
        x, y, c = lax.axis_index("x"), lax.axis_index("y"), lax.axis_index("c")
        for a in range(n):
            for j, (px, py) in enumerate([(1 - x, y), (x, 1 - y), (1 - x, 1 - y)]):
                pk = 2 * px + py
                copy = pltpu.make_async_remote_copy(
                    src_ref=src[a] if bcast else src[a].at[pk], dst_ref=land[a].at[pk],
                    send_sem=send_sems.at[3 * a + j], recv_sem=recv_sems.at[3 * a + j], device_id=(px, py, c),
                    device_id_type=MESH)
                copy.wait_send()
                copy.wait_recv()

    hbm = pl.BlockSpec(memory_space=pltpu.HBM)
    sem = pl.BlockSpec(memory_space=pltpu.SEMAPHORE)
    outs = _pcall(
        body, name=name,
        out_shape=tuple(pltpu.HBM(t.shape, t.dtype) for t in (*src_thru, *land_thru)),
        in_specs=[hbm] * (2 * n) + [sem, sem, pl.BlockSpec(memory_space=pl.ANY)], out_specs=tuple([hbm] * (2 * n)),
        input_output_aliases={k: k for k in range(2 * n)},
        compiler_params=pltpu.CompilerParams(has_side_effects=pltpu.SideEffectType.DATAFLOW_SIDE_EFFECTING),
    )(*src_thru, *land_thru, send_sems, recv_sems, after)
    return list(outs[n:])


def _tie(name, v, token):
    def body(v_ref, token_ref, o_ref):
        del v_ref, token_ref, o_ref

    any_spec = pl.BlockSpec(memory_space=pl.ANY)
    return _pcall(body, name=name, out_shape=jax.ShapeDtypeStruct(v.shape, v.dtype), in_specs=[any_spec, any_spec],
                  out_specs=any_spec, input_output_aliases={0: 0})(v, token)


def _fill_own(landed, own, me, bcast):
    blk = own if bcast else lax.dynamic_index_in_dim(own, me, 0, keepdims=False)
    return lax.dynamic_update_index_in_dim(landed, blk, me, 0)


def _swap_sibling(name, srcs):
    n = len(srcs)

    def body(*refs):
        src, out = refs[:n], refs[n:2 * n]
        send_sems, recv_sems = refs[2 * n:]
        x, y, c = lax.axis_index("x"), lax.axis_index("y"), lax.axis_index("c")
        copies = []
        for a in range(n):
            rc = pltpu.make_async_remote_copy(
                src_ref=src[a], dst_ref=out[a], send_sem=send_sems.at[a], recv_sem=recv_sems.at[a],
                device_id=(x, y, 1 - c), device_id_type=MESH)
            rc.start()
            copies.append(rc)
        for cp in copies:
            cp.wait()

    any_spec = pl.BlockSpec(memory_space=pl.ANY)
    return _pcall(
        body, name=name, out_shape=[jax.ShapeDtypeStruct(s.shape, s.dtype) for s in srcs],
        in_specs=[any_spec] * n, out_specs=[any_spec] * n,
        scratch_shapes=[pltpu.SemaphoreType.DMA((n,)), pltpu.SemaphoreType.DMA((n,))],
    )(*srcs)


def _mod_fwd(c16, mod_w, mod_b_shard):
    nl, D, S = mod_w.shape

    def body(c_ref, w_ref, b_ref, o_ref):
        s = _silu(c_ref[...]).astype(BF16)
        o_ref[...] = jnp.dot(s, w_ref[...].astype(BF16), preferred_element_type=F32) + b_ref[...]

    return _pcall(
        body, name="mod_fwd", grid=(nl,),
        in_specs=[pl.BlockSpec((16, D), lambda l: (0, 0)), pl.BlockSpec((None, D, S), lambda l: (l, 0, 0)),
                  pl.BlockSpec((None, 1, S), lambda l: (l, 0, 0))],
        out_specs=pl.BlockSpec((None, 16, S), lambda l: (l, 0, 0)),
        out_shape=jax.ShapeDtypeStruct((nl, 16, S), F32), compiler_params=_cparams(1),
    )(c16, mod_w, mod_b_shard)


def _mod_w_update(s16t, dm16, w, m, v):
    nl, D, S = w.shape
    tm = _row_tile(D, 256)

    def body(s_ref, dm_ref, w_ref, m_ref, v_ref, g_ref, dl_ref, nm_ref, nv_ref):
        g = jnp.dot(s_ref[...], dm_ref[...], preferred_element_type=F32, precision=HIGHEST)
        g, dl, nm, nv = _f_adamw(w_ref[...], m_ref[...], v_ref[...], g, jnp.zeros_like(g))
        g_ref[...] = g
        dl_ref[...] = dl
        nm_ref[...] = nm
        nv_ref[...] = nv

    big = pl.BlockSpec((None, tm, S), lambda l, i: (l, i, 0))
    return _pcall(
        body, name="mod_w_update", grid=(nl, D // tm),
        in_specs=[pl.BlockSpec((tm, 16), lambda l, i: (i, 0)), pl.BlockSpec((None, 16, S), lambda l, i: (l, 0, 0)),
                  big, big, big],
        out_specs=[big] * 4, out_shape=[jax.ShapeDtypeStruct(w.shape, F32)] * 4, compiler_params=_cparams(2),
    )(s16t, dm16, w, m, v)


def _pack(arrs):
    flat = jnp.concatenate([a.reshape(-1).astype(F32) for a in arrs])
    n = flat.shape[0]
    rows = _round_up(_cdiv(n, LANE), SUBLANE)
    return jnp.pad(flat, (0, rows * LANE - n)).reshape(rows, LANE)


def _unpack(buf, shapes):
    flat = buf.reshape(-1)
    out, pos = [], 0
    for s in shapes:
        n = 1
        for d in s:
            n *= d
        out.append(flat[pos:pos + n].reshape(s))
        pos += n
    return out


SHARD_AXIS = {
    "mod_w": 2, "ssd_w_in": 2, "ssd_conv_w": 2, "ssd_w_out": 1, "conf_w_pw1": 2, "conf_b_pw1": 1, "conf_w_dw": 2,
    "conf_b_dw": 1, "conf_ln_w": 1, "conf_ln_b": 1, "conf_w_pw2": 1, "conf_b_pw2": 1, "ffn_w_up": 2,
    "ffn_conv_w": 3, "ffn_w_down": 1,
}
BIG = ("ssd_w_in", "ssd_w_out", "conf_w_pw1", "conf_w_pw2", "ffn_w_up", "ffn_w_down")
WEIGHTS = ("c_ctx", "mod_w", "mod_b", "norm1_w", "norm2_w", "ssd_w_in", "ssd_conv_w", "ssd_conv_b", "ssd_dt_bias",
           "ssd_a_log", "ssd_d", "ssd_norm_w", "ssd_w_out", "conf_w_pw1", "conf_b_pw1", "conf_w_dw", "conf_b_dw",
           "conf_ln_w", "conf_ln_b", "conf_w_pw2", "conf_b_pw2", "ffn_w_up", "ffn_conv_w", "ffn_conv_b",
           "ffn_w_down", "final_norm_w")
SMALL = tuple(n for n in WEIGHTS if n not in BIG and n != "mod_w")
SMALL_SHARDED = tuple(n for n in SMALL if n in SHARD_AXIS)


def _unshard(stacked, axis):
    return jnp.concatenate([stacked[k] for k in range(N_CHIPS)], axis=axis)


def _to_blocks(full, axis):
    return jnp.stack(jnp.split(full, N_CHIPS, axis=axis))


def _par(v):
    v = v.reshape(-1, v.shape[-1])
    return v[:, None, :]


def kernel(x, c, ctx, c_ctx, mod_w, mod_b, norm1_w, norm2_w, ssd_w_in, ssd_conv_w, ssd_conv_b, ssd_dt_bias, ssd_a_log, ssd_d, ssd_norm_w, ssd_w_out, conf_w_pw1, conf_b_pw1, conf_w_dw, conf_b_dw, conf_ln_w, conf_ln_b, conf_w_pw2, conf_b_pw2, ffn_w_up, ffn_conv_w, ffn_conv_b, ffn_w_down, final_norm_w, loss_target, m_c_ctx, m_mod_w, m_mod_b, m_norm1_w, m_norm2_w, m_ssd_w_in, m_ssd_conv_w, m_ssd_conv_b, m_ssd_dt_bias, m_ssd_a_log, m_ssd_d, m_ssd_norm_w, m_ssd_w_out, m_conf_w_pw1, m_conf_b_pw1, m_conf_w_dw, m_conf_b_dw, m_conf_ln_w, m_conf_ln_b, m_conf_w_pw2, m_conf_b_pw2, m_ffn_w_up, m_ffn_conv_w, m_ffn_conv_b, m_ffn_w_down, m_final_norm_w, v_c_ctx, v_mod_w, v_mod_b, v_norm1_w, v_norm2_w, v_ssd_w_in, v_ssd_conv_w, v_ssd_conv_b, v_ssd_dt_bias, v_ssd_a_log, v_ssd_d, v_ssd_norm_w, v_ssd_w_out, v_conf_w_pw1, v_conf_b_pw1, v_conf_w_dw, v_conf_b_dw, v_conf_ln_w, v_conf_ln_b, v_conf_w_pw2, v_conf_b_pw2, v_ffn_w_up, v_ffn_conv_w, v_ffn_conv_b, v_ffn_w_down, v_final_norm_w):
    given = dict(locals())
    W = {n: given[n] for n in WEIGHTS}
    Mo = {n: given["m_" + n] for n in WEIGHTS}
    Vo = {n: given["v_" + n] for n in WEIGHTS}

    ax, ay, ac = lax.axis_index("x"), lax.axis_index("y"), lax.axis_index("c")
    chip = 2 * ax + ay
    dev = 4 * ax + 2 * ay + ac

    D = x.shape[-1]
    L, Lc = x.shape[1], ctx.shape[1]
    T0 = L + Lc
    H = ssd_a_log.shape[-1]
    DI = ssd_norm_w.shape[-1]
    P = DI // H
    CD = ssd_conv_b.shape[-1]
    N = SSD_STATE
    G = (CD - DI) // (2 * N)
    FH = ffn_conv_b.shape[-1]
    KS = ssd_conv_w.shape[1]
    KC = conf_w_dw.shape[1]
    ncc = Lc // SSD_CHUNK

    shard_b = {n: W[n].astype(BF16) for n in BIG}
    gather_a, token = _exchange4_start("gather_w_in_start", [shard_b["ssd_w_in"]], True, x)
    c = _tie("tie_gather_w_in", c, token)

    small_shard_shapes = [W[n].shape for n in SMALL_SHARDED]
    f1 = _allgather8("gather_small", _pack([c] + [W[n] for n in SMALL_SHARDED]))
    c_rows, full_small = [], {n: [] for n in SMALL_SHARDED}
    for k in range(N_DEV):
        parts = _unpack(f1[k], [c.shape] + small_shard_shapes)
        c_rows.append(parts[0])
        if k % 2 == 0:
            for n, p in zip(SMALL_SHARDED, parts[1:]):
                full_small[n].append(p)
    Wf = dict(W)
    for n in SMALL_SHARDED:
        Wf[n] = jnp.concatenate(full_small[n], axis=SHARD_AXIS[n])
    c16 = jnp.concatenate(c_rows + [c_ctx[None, :], jnp.zeros((16 - N_DEV - 1, D), F32)], axis=0)

    S_mod = mod_w.shape[-1]
    mod_b_shard = lax.dynamic_slice_in_dim(mod_b, chip * S_mod, S_mod, axis=1)[:, None, :]
    mod_part = _mod_fwd(c16, mod_w, mod_b_shard)
    f2 = _allgather8("gather_mod", mod_part.reshape(2 * 16, S_mod))
    mods = jnp.concatenate([f2[2 * k].reshape(2, 16, S_mod) for k in range(N_CHIPS)], axis=-1)
    my = lax.dynamic_slice_in_dim(mods, dev, 1, axis=1)[:, 0]
    sh1, sc1, g1, sh2, sc2, g2 = [[my[l, k * D:(k + 1) * D] for l in range(2)] for k in range(6)]
    csh1, csc1 = mods[0, N_DEV, 0:D], mods[0, N_DEV, D:2 * D]

    def full_weight(n, landed):
        return _unshard(_fill_own(landed, shard_b[n], chip, True), SHARD_AXIS[n])

    xl = x[0]
    hcat = jnp.concatenate([ctx[0], xl], axis=0)
    n1w0, n2w0, n1w1, n2w1 = _par(norm1_w[0]), _par(norm2_w[0]), _par(norm1_w[1]), _par(norm2_w[1])
    sc_seg = jnp.stack([csc1, sc1[0]])[:, None, :]
    sh_seg = jnp.stack([csh1, sh1[0]])[:, None, :]

    a0 = _rw_fwd("l0_modnorm1", _f_modnorm, [hcat], [n1w0, sc_seg, sh_seg], [D], seg_rows=(Lc,), out_dtypes=[BF16])
    (landed_in,) = _exchange4_wait("gather_w_in_wait", gather_a, a0)
    w_in = full_weight("ssd_w_in", landed_in)[0]
    rest = [n for n in BIG if n != "ssd_w_in"]
    gather_b, token = _exchange4_start("gather_rest_start", [shard_b[n] for n in rest], True, landed_in)
    a0 = _tie("tie_gather_rest", a0, token)
    proj = _mm(a0, w_in, name="l0_w_in")
    seg_taps = [(k - KS // 2, ("seg", Lc)) for k in range(KS)]
    xbc_pre, xbc = _conv_fwd("l0_conv", proj, DI, CD, Wf["ssd_conv_w"][0], ssd_conv_b, seg_taps, act=True)
    dt_raw = proj[:, DI + CD:]
    dt_bias = _par(ssd_dt_bias.reshape(1, 2 * H))
    dt = _rw_fwd("l0_softplus", _f_softplus, [dt_raw], [dt_bias], [2 * H])
    dt_t = dt.T
    dtr = (dt_t[:H, None, :], dt_t[H:, None, :])
    a_all = -jnp.exp(ssd_a_log.reshape(2, H, 1, 1))
    a_neg = (a_all[0], a_all[1])
    (y_f, y_b), s_enter = _ssd_fwd(xbc, DI, DI + G * N, dtr, a_neg, P, ncc)
    gate_rows = [y_f, y_b, (xbc, 0, DI, Lc), (proj, 0, DI, Lc)]
    d_rep = _par(jnp.repeat(ssd_d[0], P))
    ssd_nw = _par(ssd_norm_w[0])
    yn = _rw_fwd("l0_ssd_gate", _f_ssd_gate, gate_rows, [d_rep, ssd_nw], [DI], T=L, out_dtypes=[BF16])
    Wb = {n: full_weight(n, g) for n, g in zip(rest, _exchange4_wait("gather_rest_wait", gather_b, yn))}
    w_out, w_pw1, w_pw2 = Wb["ssd_w_out"][0], Wb["conf_w_pw1"][0], Wb["conf_w_pw2"][0]
    w_up, w_dn = Wb["ffn_w_up"], Wb["ffn_w_down"]
    mix0 = _mm(yn, w_out, name="l0_w_out")
    g1_0, g2_0, g1_1, g2_1 = _par(g1[0]), _par(g2[0]), _par(g1[1]), _par(g2[1])
    h1 = _rw_fwd("l0_res1", _f_gate_res, [xl, mix0], [g1_0], [D])

    grid_taps = [((i - 1) * GRID_W + (j - 1), (None if j == 1 else ("col", j - 1))) for i in range(3) for j in range(3)]

    def ffn_fwd(l, h, tag):
        a = _rw_fwd(tag + "_modnorm2", _f_modnorm, [h], [_par(norm2_w[l]), _par(sc2[l]), _par(sh2[l])], [D],
                    out_dtypes=[BF16])
        hh = _mm(a, w_up[l], name=tag + "_w_up")
        gc = _conv_fwd(tag + "_ffn_conv", hh, FH, FH, Wf["ffn_conv_w"][l].reshape(9, FH), ffn_conv_b[l][None, :],
                       grid_taps)
        act = _rw_fwd(tag + "_act", _f_ffn_act, [(hh, 0, FH), gc], [], [FH], col_tile=_tile(FH, 1536),
                      out_dtypes=[BF16])
        dn = _mm(act, w_dn[l], name=tag + "_w_down")
        return a, hh, gc, act, dn

    a1, hh0, gc0, act0, dn0 = ffn_fwd(0, h1, "l0")
    h2 = _rw_fwd("l0_res2", _f_gate_res, [h1, dn0], [g2_0], [D])

    a2 = _rw_fwd("l1_modnorm1", _f_modnorm, [h2], [n1w1, _par(sc1[1]), _par(sh1[1])], [D], out_dtypes=[BF16])
    pw = _mm(a2, w_pw1, name="l1_pw1")
    b_pw1 = Wf["conf_b_pw1"][0]
    glu = _rw_fwd("l1_glu", _f_glu, [(pw, 0, D), (pw, D, D)], [_par(b_pw1[:D]), _par(b_pw1[D:])], [D])
    conf_taps = [(k - KC // 2, None) for k in range(KC)]
    cv = _conv_fwd("l1_conv", glu, 0, D, Wf["conf_w_dw"][0], Wf["conf_b_dw"], conf_taps)
    ln_w, ln_b = _par(Wf["conf_ln_w"][0]), _par(Wf["conf_ln_b"][0])
    ls = _rw_fwd("l1_ln_silu", _f_ln_silu, [cv], [ln_w, ln_b], [D], out_dtypes=[BF16])
    p2 = _mm(ls, w_pw2, name="l1_pw2")
    b_pw2 = _par(Wf["conf_b_pw2"][0])
    h3 = _rw_fwd("l1_res1", _f_gate_res_bias, [h2, p2], [g1_1, b_pw2], [D])
    a3, hh1, gc1, act1, dn1 = ffn_fwd(1, h3, "l1")
    h4 = _rw_fwd("l1_res2", _f_gate_res, [h3, dn1], [g2_1], [D])

    fnw = final_norm_w[None, :]
    tgt = loss_target[0]
    loss_local = _loss_fwd(h4, tgt, fnw)[0, 0]
    loss = lax.psum(loss_local, ("x", "y", "c"))

    G_full = {}
    reduces = {}

    def start_reduce(tag, items, dep):
        def blocks_of(g, ax):
            if g.ndim == 3:
                return g
            return g.reshape(N_CHIPS, g.shape[0] // N_CHIPS, g.shape[1]) if ax == 0 else _to_blocks(g, ax)

        blocks = [blocks_of(g, ax).astype(BF16) for _, g, ax in items]
        handle, tok = _exchange4_start("reduce_" + tag + "_start", blocks, False, dep)
        reduces[tag] = ([n for n, _, _ in items], handle, blocks)
        return tok
    ones = jnp.ones((L, 1), F32)
    (dh4,), (dfnw,) = _rw_bwd("loss_bwd", _f_loss_rows, [h4, tgt], [_par(final_norm_w)], [ones],
                              row_grad=[True, False], par_grad=[True])
    G_full["final_norm_w"] = dfnw.reshape(D)

    def ffn_bwd(l, h, saved, g2_l, dh_out, tag):
        a, hh, gc, act, dn = saved
        (ddn,), (dg2,) = _rw_bwd(tag + "_res2_bwd", _f_gate_res, [h, dn], [g2_l], [dh_out],
                                 row_grad=[False, True], par_grad=[True], row_dtypes=[BF16])
        dact = _mm(ddn, w_dn[l], tb=True, name=tag + "_w_down_dx")
        dwdn = _mm(act, ddn, ta=True, name=tag + "_w_down_dw", out_dtype=BF16)
        (dval, dgc), _ = _rw_bwd(tag + "_act_bwd", _f_ffn_act, [(hh, 0, FH), gc], [], [dact],
                                 row_grad=[True, True], par_grad=[], col_tile=_tile(FH, 1536), row_dtypes=[BF16, F32])
        dgin, dcw, dcb = _conv_bwd(tag + "_ffn_conv_bwd", hh, FH, FH, Wf["ffn_conv_w"][l].reshape(9, FH), dgc,
                                   grid_taps, du_dtype=BF16)
        dhh = jnp.concatenate([dval, dgin], axis=1)
        da = _mm(dhh, w_up[l], tb=True, name=tag + "_w_up_dx")
        dwup = _mm(a, dhh, ta=True, name=tag + "_w_up_dw", out_dtype=BF16, col_blocks=N_CHIPS)
        (dh,), (dn2w, dsc2, dsh2) = _rw_bwd(
            tag + "_modnorm2_bwd", _f_modnorm, [h], [_par(norm2_w[l]), _par(sc2[l]), _par(sh2[l])], [da],
            row_grad=[True], par_grad=[True, True, True], add=dh_out)
        return dh, dict(w_down=dwdn, w_up=dwup, conv_w=dcw.reshape(3, 3, FH), conv_b=dcb.reshape(FH),
                        n2w=dn2w.reshape(D), sc2=dsc2.reshape(D), sh2=dsh2.reshape(D), g2=dg2.reshape(D))

    dh3, gf1 = ffn_bwd(1, h3, (a3, hh1, gc1, act1, dn1), g2_1, dh4, "l1")
    (dp2,), (dg1_1, db_pw2) = _rw_bwd("l1_res1_bwd", _f_gate_res_bias, [h2, p2], [g1_1, b_pw2], [dh3],
                                      row_grad=[False, True], par_grad=[True, True], row_dtypes=[BF16])
    dls = _mm(dp2, w_pw2, tb=True, name="l1_pw2_dx")
    dw_pw2 = _mm(ls, dp2, ta=True, name="l1_pw2_dw", out_dtype=BF16)
    (dcv,), (dln_w, dln_b) = _rw_bwd("l1_ln_silu_bwd", _f_ln_silu, [cv], [ln_w, ln_b], [dls],
                                     row_grad=[True], par_grad=[True, True])
    dglu, dw_dw, db_dw = _conv_bwd("l1_conv_bwd", glu, 0, D, Wf["conf_w_dw"][0], dcv, conf_taps)
    (dpa, dpg), (dba, dbg) = _rw_bwd("l1_glu_bwd", _f_glu, [(pw, 0, D), (pw, D, D)],
                                     [_par(b_pw1[:D]), _par(b_pw1[D:])], [dglu],
                                     row_grad=[True, True], par_grad=[True, True], row_dtypes=[BF16, BF16])
    dpw = jnp.concatenate([dpa, dpg], axis=1)
    da2 = _mm(dpw, w_pw1, tb=True, name="l1_pw1_dx")
    dw_pw1 = _mm(a2, dpw, ta=True, name="l1_pw1_dw", out_dtype=BF16, col_blocks=N_CHIPS)
    (dh2,), (dn1w1, dsc1_1, dsh1_1) = _rw_bwd(
        "l1_modnorm1_bwd", _f_modnorm, [h2], [n1w1, _par(sc1[1]), _par(sh1[1])], [da2],
        row_grad=[True], par_grad=[True, True, True], add=dh3)
    G_full["conf_b_pw2"] = db_pw2.reshape(1, D)
    G_full["conf_ln_w"], G_full["conf_ln_b"] = dln_w.reshape(1, D), dln_b.reshape(1, D)
    G_full["conf_w_dw"], G_full["conf_b_dw"] = dw_dw[None], db_dw.reshape(1, D)
    G_full["conf_b_pw1"] = jnp.concatenate([dba.reshape(1, D), dbg.reshape(1, D)], axis=1)

    token = start_reduce("l1", [("conf_w_pw2", dw_pw2, 0), ("conf_w_pw1", dw_pw1, 1), ("ffn_w_up1", gf1["w_up"], 1),
                                ("ffn_w_down1", gf1["w_down"], 0)], dw_pw2)
    dh2 = _tie("tie_reduce_l1", dh2, token)
    dh1, gf0 = ffn_bwd(0, h1, (a1, hh0, gc0, act0, dn0), g2_0, dh2, "l0")
    G_full["ffn_conv_w"] = jnp.stack([gf0["conv_w"], gf1["conv_w"]])
    G_full["ffn_conv_b"] = jnp.stack([gf0["conv_b"], gf1["conv_b"]])

    (dmix,), (dg1_0,) = _rw_bwd("l0_res1_bwd", _f_gate_res, [xl, mix0], [g1_0], [dh1],
                                row_grad=[False, True], par_grad=[True], row_dtypes=[BF16])
    dyn = _mm(dmix, w_out, tb=True, name="l0_w_out_dx")
    dw_out = _mm(yn, dmix, ta=True, name="l0_w_out_dw", out_dtype=BF16)
    token = start_reduce("l0", [("ffn_w_up0", gf0["w_up"], 1), ("ffn_w_down0", gf0["w_down"], 0),
                                ("ssd_w_out", dw_out, 0)], dw_out)
    dyn = _tie("tie_reduce_l0", dyn, token)
    (dy_lat, dxs_gate, dz_lat), (dd_rep, dssd_nw) = _rw_bwd(
        "l0_ssd_gate_bwd", _f_ssd_gate, gate_rows, [d_rep, ssd_nw], [dyn],
        row_grad=[True, False, True, True], par_grad=[True, True], T=L, row_dtypes=[F32, F32, BF16])
    g_f, g_b = _ssd_bwd(xbc, DI, DI + G * N, dtr, a_neg, s_enter, dy_lat, P, ncc)
    dxs_gate_all = jnp.pad(dxs_gate, ((Lc, 0), (0, 0)))
    silu_bwd = functools.partial(_rw_bwd, f=_silu, pars=[], row_grad=[True], par_grad=[], T=T0)
    (dxs_pre,), _ = silu_bwd("l0_silu_bwd_x", rows=[(xbc_pre, 0, DI)], cot_fn=lambda p, q, r: p + q + r,
                             cots=[g_f[0], g_b[0], dxs_gate_all], col_tile=_tile(DI, 1024))
    (db_pre,), _ = silu_bwd("l0_silu_bwd_b", rows=[(xbc_pre, DI, G * N)], cot_fn=lambda p, q: p + q,
                            cots=[g_f[1], g_b[1]], col_tile=_tile(G * N, 1024))
    (dc_pre,), _ = silu_bwd("l0_silu_bwd_c", rows=[(xbc_pre, DI + G * N, G * N)], cot_fn=lambda p, q: p + q,
                            cots=[g_f[2], g_b[2]], col_tile=_tile(G * N, 1024))
    dxbc_pre = jnp.concatenate([dxs_pre, db_pre, dc_pre], axis=1)
    dconv_in, dcw0, dcb0 = _conv_bwd("l0_conv_bwd", proj, DI, CD, Wf["ssd_conv_w"][0], dxbc_pre, seg_taps,
                                     du_dtype=BF16)
    ddt = jnp.concatenate([g_f[3][:, 0, :].T, g_b[3][:, 0, :].T], axis=1)
    (ddt_raw,), (ddt_bias,) = _rw_bwd("l0_softplus_bwd", _f_softplus, [dt_raw], [dt_bias], [ddt],
                                      row_grad=[True], par_grad=[True], row_dtypes=[BF16])
    dproj = jnp.concatenate([jnp.pad(dz_lat, ((Lc, 0), (0, 0))), dconv_in, ddt_raw], axis=1)
    da0 = _mm(dproj, w_in, tb=True, name="l0_w_in_dx")
    dw_in = _mm(a0, dproj, ta=True, name="l0_w_in_dw", out_dtype=BF16)
    token = start_reduce("in", [("ssd_w_in", dw_in, 1)], dw_in)
    da0 = _tie("tie_reduce_in", da0, token)
    (dhcat,), (dn1w0, dsc_seg, dsh_seg) = _rw_bwd(
        "l0_modnorm1_bwd", _f_modnorm, [hcat], [n1w0, sc_seg, sh_seg], [da0],
        row_grad=[True], par_grad=[True, True, True], seg_rows=(Lc,))
    grad_x = (dhcat[Lc:] + dh1)[None]

    da_heads = jnp.stack([g[4][:, 0, 0].reshape(G, T0 // SSD_CHUNK, H // G).sum(axis=1).reshape(H)
                          for g in (g_f, g_b)])[None]
    G_full["ssd_a_log"] = da_heads * (-jnp.exp(ssd_a_log))
    G_full["ssd_dt_bias"] = ddt_bias.reshape(1, 2, H)
    G_full["ssd_d"] = dd_rep.reshape(H, P).sum(axis=1)[None]
    G_full["ssd_norm_w"] = dssd_nw.reshape(1, DI)
    G_full["ssd_conv_w"], G_full["ssd_conv_b"] = dcw0[None], dcb0.reshape(1, CD)
    G_full["norm1_w"] = jnp.stack([dn1w0.reshape(D), dn1w1.reshape(D)])
    G_full["norm2_w"] = jnp.stack([gf0["n2w"], gf1["n2w"]])

    zD = jnp.zeros((D,), F32)
    dm_own = jnp.stack([
        jnp.concatenate([dsh_seg[1, 0], dsc_seg[1, 0], dg1_0.reshape(D), gf0["sh2"], gf0["sc2"], gf0["g2"]]),
        jnp.concatenate([dsh1_1.reshape(D), dsc1_1.reshape(D), dg1_1.reshape(D), gf1["sh2"], gf1["sc2"], gf1["g2"]]),
    ])
    dmc_own = jnp.concatenate([dsh_seg[0, 0], dsc_seg[0, 0], zD, zD, zD, zD])

    small_sum_names = [n for n in SMALL if n not in ("c_ctx", "mod_b")]
    sum_part = [G_full[n] for n in small_sum_names] + [dmc_own]
    n_sum = sum(int(a.size) for a in sum_part)
    packed = _pack(sum_part + [dm_own])
    gat = _allgather8("gather_small_grads", packed)
    total = _sum_leading("sum_small_grads", gat, tuple(range(N_DEV)))
    summed = _unpack(total, [a.shape for a in sum_part])
    Gs = dict(zip(small_sum_names, summed[:-1]))
    dmc_tot = summed[-1]
    dm_all = jnp.stack([gat[k].reshape(-1)[n_sum:n_sum + 2 * 6 * D].reshape(2, 6 * D) for k in range(N_DEV)], axis=1)
    dm16 = jnp.concatenate([dm_all, jnp.stack([dmc_tot, jnp.zeros_like(dmc_tot)])[:, None, :],
                            jnp.zeros((2, 16 - N_DEV - 1, 6 * D), F32)], axis=1)
    Gs["mod_b"] = _sum_leading("sum_mod_b", dm16.transpose(1, 0, 2).reshape(16, 2 * 6 * D // LANE, LANE),
                               tuple(range(N_DEV + 1))).reshape(2, 6 * D)

    dm16_shard = lax.dynamic_slice_in_dim(dm16, chip * S_mod, S_mod, axis=2)
    ds16 = _mm(dm16_shard[0], mod_w[0], tb=True, precision=HIGHEST, name="c_ctx_dx")
    sig = jax.nn.sigmoid(c_ctx)
    dcc_part = ds16[N_DEV] * (sig * (1.0 + c_ctx * (1.0 - sig)))
    gat_cc = _allgather8("gather_c_ctx_grad", _pack([dcc_part]))
    Gs["c_ctx"] = _sum_leading("sum_c_ctx_grad", gat_cc, (0, 2, 4, 6)).reshape(-1)[:D]

    s16t = _silu(c16).T
    out = {}
    out["mod_w"] = _mod_w_update(s16t, dm16_shard, mod_w, m_mod_w, v_mod_w)

    late = out["mod_w"][0]
    partial = {}
    for tag, (names, handle, blocks) in reduces.items():
        landed = _exchange4_wait("reduce_" + tag + "_wait", handle, late)
        for n, blk, own in zip(names, landed, blocks):
            r = _fill_own(blk, own, chip, False)
            partial[n] = _sum_leading("sum4_" + n, r.reshape(N_CHIPS, -1, r.shape[-1]), (0, 1, 2, 3)).reshape(r.shape[1:])
    for n in ("ffn_w_up", "ffn_w_down"):
        partial[n] = jnp.stack([partial.pop(n + "0"), partial.pop(n + "1")])
    partial = [partial[n].reshape(W[n].shape) for n in BIG]
    sibling = _swap_sibling("swap_grads", partial)
    for n, mine, sib in zip(BIG, partial, sibling):
        out[n] = _adamw("adamw_" + n, W[n], Mo[n], Vo[n], mine, sib)

    def own(n, full):
        if n in SHARD_AXIS:
            size = W[n].shape[SHARD_AXIS[n]]
            return lax.dynamic_slice_in_dim(full, chip * size, size, axis=SHARD_AXIS[n])
        return full

    g_small = [own(n, Gs[n].reshape(Wf[n].shape)) for n in SMALL]
    shapes = [W[n].shape for n in SMALL]
    pk = [_pack([W[n] for n in SMALL]), _pack([Mo[n] for n in SMALL]), _pack([Vo[n] for n in SMALL]), _pack(g_small)]
    res = _adamw("adamw_small", pk[0], pk[1], pk[2], pk[3], jnp.zeros_like(pk[3]))
    unpacked = [_unpack(r, shapes) for r in res]
    for k, n in enumerate(SMALL):
        out[n] = tuple(u[k] for u in unpacked)

    grads = [out[n][0] for n in WEIGHTS]
    deltas = [out[n][1] for n in WEIGHTS]
    new_m = [out[n][2] for n in WEIGHTS]
    new_v = [out[n][3] for n in WEIGHTS]
    return (loss, grad_x, *grads, *deltas, *new_m, *new_v)
```

```python
import functools

import jax
import jax.numpy as jnp
from jax import lax
from jax.experimental import pallas as pl
from jax.experimental.pallas import tpu as pltpu

F32 = jnp.float32
BF16 = jnp.bfloat16
MESH = pl.DeviceIdType.MESH
HIGHEST = lax.Precision.HIGHEST

VMEM_LIMIT_BYTES = 48 * 1024 * 1024
LANE = 128
SUBLANE = 8

SSD_STATE = 128
SSD_CHUNK = 128
GRID_W = 64
EPS = 1e-6
N_CHIPS = 4
N_DEV = 8

ADAM_LR = 0.001
ADAM_B1 = 0.9
ADAM_B2 = 0.999
ADAM_EPS = 1e-08
ADAM_WD = 0.01
ADAM_STEP = 10


def _pcall(body, **kw):
    return pl.pallas_call(body, **kw)


def _cparams(n_grid):
    return pltpu.CompilerParams(dimension_semantics=("arbitrary",) * n_grid, vmem_limit_bytes=VMEM_LIMIT_BYTES)


def _cdiv(a, b):
    return -(-a // b)


def _round_up(a, b):
    return _cdiv(a, b) * b


def _tile(n, cap):
    if n <= cap:
        return n
    best = None
    for t in range(LANE, cap + 1, LANE):
        if n % t == 0:
            best = t
    if best is None:
        npad = _round_up(n, LANE)
        for t in range(LANE, cap + 1, LANE):
            if npad % t == 0:
                best = t
    return best


def _row_tile(n, cap, also=()):
    best = None
    for step in (2 * SUBLANE, SUBLANE):
        for t in range(step, min(cap, n) + 1, step):
            if n % t == 0 and all(a % t == 0 for a in also):
                best = t
        if best is not None:
            break
    assert best is not None, (n, cap, also)
    return best


def _silu(v):
    return v * jax.nn.sigmoid(v)


def _mm(a, b, *, name, ta=False, tb=False, precision=None, cap=1024, out_dtype=F32, col_blocks=None):
    M, K = (a.shape[1], a.shape[0]) if ta else a.shape
    N = b.shape[0] if tb else b.shape[1]
    assert K == (b.shape[1] if tb else b.shape[0]), (a.shape, b.shape, ta, tb)
    tm, tk = _tile(M, cap), _tile(K, cap)
    tn = _tile(N, cap) if col_blocks is None else _tile(N // col_blocks, cap + cap // 2)
    nm, nn, nk = _cdiv(M, tm), _cdiv(N, tn), _cdiv(K, tk)
    k_tail = K % tk
    exact = precision is not None

    def body(a_ref, b_ref, o_ref, acc_ref):
        k = pl.program_id(2)

        @pl.when(k == 0)
        def _():
            acc_ref[...] = jnp.zeros_like(acc_ref)

        av = a_ref[...]
        bv = b_ref[...]
        if k_tail:
            lim = K - k * tk
            ka = lax.broadcasted_iota(jnp.int32, av.shape, 0 if ta else 1)
            kb = lax.broadcasted_iota(jnp.int32, bv.shape, 1 if tb else 0)
            av = jnp.where(ka < lim, av, jnp.zeros_like(av))
            bv = jnp.where(kb < lim, bv, jnp.zeros_like(bv))
        if exact:
            av = av.astype(F32)
            bv = bv.astype(F32)
        else:
            av = av.astype(BF16)
            bv = bv.astype(BF16)
        dn = (((0 if ta else 1,), (1 if tb else 0,)), ((), ()))
        acc_ref[...] += lax.dot_general(av, bv, dn, preferred_element_type=F32, precision=precision)

        @pl.when(k == nk - 1)
        def _():
            o_ref[...] = acc_ref[...].astype(o_ref.dtype)

    a_spec = pl.BlockSpec((tk, tm), lambda i, j, k: (k, i)) if ta else pl.BlockSpec((tm, tk), lambda i, j, k: (i, k))
    b_spec = pl.BlockSpec((tn, tk), lambda i, j, k: (j, k)) if tb else pl.BlockSpec((tk, tn), lambda i, j, k: (k, j))
    if col_blocks is None:
        out_spec = pl.BlockSpec((tm, tn), lambda i, j, k: (i, j))
        out_shape = jax.ShapeDtypeStruct((M, N), out_dtype)
    else:
        per = (N // col_blocks) // tn
        assert per * tn * col_blocks == N, (N, col_blocks, tn)
        out_spec = pl.BlockSpec((None, tm, tn), lambda i, j, k: (j // per, i, j % per))
        out_shape = jax.ShapeDtypeStruct((col_blocks, M, N // col_blocks), out_dtype)
    return _pcall(
        body, name=name, grid=(nm, nn, nk), in_specs=[a_spec, b_spec], out_specs=out_spec, out_shape=out_shape,
        scratch_shapes=[pltpu.VMEM((tm, tn), F32)], compiler_params=_cparams(3),
    )(a, b)


def _norm_rows(rows):
    out = []
    for r in rows:
        if not isinstance(r, tuple):
            r = (r,)
        arr, off, width, roff = (r + (0, None, 0)[len(r) - 1:])
        out.append((arr, off, width if width is not None else arr.shape[1], roff))
    return out


def _rw_plan(T, rows, pars, seg_rows, col_tile, tm_cap):
    widths = [r[2] for r in rows]
    wmax = max(widths + [p.shape[-1] for p in pars] + [1])
    if col_tile is not None:
        assert all(w == widths[0] for w in widths) and all(p.shape[-1] == widths[0] for p in pars)
        ncol = widths[0] // col_tile
        assert ncol * col_tile == widths[0]
        wmax = col_tile
    else:
        ncol = 1
    cap = tm_cap if tm_cap is not None else max(SUBLANE, min(256, (256 * 1024) // wmax))
    tm = _row_tile(T, cap, also=tuple(seg_rows) + tuple(r[3] for r in rows if r[3]))
    bounds = tuple(s // tm for s in seg_rows)
    return widths, ncol, tm, bounds


def _rw_specs(rows, pars, ncol, tm, bounds, col_tile):
    def seg(i):
        s = 0
        for b in bounds:
            s = s + (i >= b).astype(jnp.int32)
        return s

    specs = []
    for arr, off, w, roff in rows:
        bw = col_tile if col_tile is not None else w
        assert off % bw == 0 and roff % tm == 0, (off, bw, roff, tm)
        specs.append(pl.BlockSpec((tm, bw), functools.partial(lambda j, i, ob, rb: (i + rb, ob + j),
                                                              ob=off // bw, rb=roff // tm)))
    for p in pars:
        bw = col_tile if col_tile is not None else p.shape[-1]
        if p.shape[0] > 1:
            specs.append(pl.BlockSpec((None, 1, bw), lambda j, i: (seg(i), 0, j)))
        else:
            specs.append(pl.BlockSpec((None, 1, bw), lambda j, i: (0, 0, j)))
    return specs, seg


def _rw_fwd(name, f, rows, pars, out_widths, *, T=None, seg_rows=(), col_tile=None, tm_cap=None, out_dtypes=None):
    rows = _norm_rows(rows)
    T = rows[0][0].shape[0] if T is None else T
    widths, ncol, tm, bounds = _rw_plan(T, rows, pars, seg_rows, col_tile, tm_cap)
    in_specs, _ = _rw_specs(rows, pars, ncol, tm, bounds, col_tile)
    nr, npar, nout = len(rows), len(pars), len(out_widths)

    def body(*refs):
        vals = [r[...] for r in refs[:nr + npar]]
        outs = f(*vals)
        if not isinstance(outs, (tuple, list)):
            outs = (outs,)
        for o_ref, o in zip(refs[nr + npar:], outs):
            o_ref[...] = o.astype(o_ref.dtype)

    out_specs = [pl.BlockSpec((tm, col_tile if col_tile is not None else w), lambda j, i: (i, j)) for w in out_widths]
    res = _pcall(
        body, name=name, grid=(ncol, T // tm), in_specs=in_specs, out_specs=out_specs,
        out_shape=[jax.ShapeDtypeStruct((T, w), dt) for w, dt in zip(out_widths, out_dtypes or [F32] * nout)],
        compiler_params=_cparams(2),
    )(*[r[0] for r in rows], *pars)
    return res if nout > 1 else res[0]


def _rw_bwd(name, f, rows, pars, cots, *, row_grad, par_grad, T=None, seg_rows=(), col_tile=None, tm_cap=None,
            add=None, cot_fn=None, row_dtypes=None):
    rows = _norm_rows(rows)
    cots = _norm_rows(cots)
    T = rows[0][0].shape[0] if T is None else T
    extra = _norm_rows([add]) if add is not None else []
    all_rows = rows + cots + extra
    widths, ncol, tm, bounds = _rw_plan(T, all_rows, pars, seg_rows, col_tile, tm_cap)
    in_specs, seg = _rw_specs(all_rows, pars, ncol, tm, bounds, col_tile)
    nr, nc, ne, npar = len(rows), len(cots), len(extra), len(pars)
    row_idx = [k for k in range(nr) if row_grad[k]]
    par_idx = [k for k in range(npar) if par_grad[k]]

    def body(*refs):
        i = pl.program_id(1)
        row_vals = [r[...] for r in refs[:nr]]
        cot_vals = [r[...] for r in refs[nr:nr + nc]]
        add_vals = [r[...] for r in refs[nr + nc:nr + nc + ne]]
        par_vals = [r[...] for r in refs[nr + nc + ne:nr + nc + ne + npar]]
        out_refs = refs[nr + nc + ne + npar:]
        outs, vjp = jax.vjp(f, *row_vals, *par_vals)
        if cot_fn is not None:
            cot_vals = cot_fn(*cot_vals)
            if not isinstance(cot_vals, (tuple, list)):
                cot_vals = (cot_vals,)
        if isinstance(outs, (tuple, list)):
            grads = vjp(tuple(c.astype(o.dtype) for c, o in zip(cot_vals, outs)))
        else:
            grads = vjp(cot_vals[0].astype(outs.dtype))
        first_seg = i == 0
        for b in bounds:
            first_seg = first_seg | (i == b)
        for n, k in enumerate(row_idx):
            g = grads[k]
            if n == 0 and add_vals:
                g = g + add_vals[0]
            out_refs[n][...] = g.astype(out_refs[n].dtype)
        for n, k in enumerate(par_idx):
            g = grads[nr + k]
            o_ref = out_refs[len(row_idx) + n]
            first = first_seg if pars[k].shape[0] > 1 else (i == 0)

            @pl.when(first)
            def _(o_ref=o_ref, g=g):
                o_ref[...] = g

            @pl.when(jnp.logical_not(first))
            def _(o_ref=o_ref, g=g):
                o_ref[...] += g

    out_specs, out_shape = [], []
    for k in row_idx:
        w = widths[k]
        out_specs.append(pl.BlockSpec((tm, col_tile if col_tile is not None else w), lambda j, i: (i, j)))
        out_shape.append(jax.ShapeDtypeStruct((T, w), row_dtypes[len(out_shape)] if row_dtypes else F32))
    for k in par_idx:
        p = pars[k]
        bw = col_tile if col_tile is not None else p.shape[-1]
        if p.shape[0] > 1:
            out_specs.append(pl.BlockSpec((None, 1, bw), lambda j, i: (seg(i), 0, j)))
        else:
            out_specs.append(pl.BlockSpec((None, 1, bw), lambda j, i: (0, 0, j)))
        out_shape.append(jax.ShapeDtypeStruct(p.shape, F32))
    res = _pcall(
        body, name=name, grid=(ncol, T // tm), in_specs=in_specs, out_specs=out_specs, out_shape=out_shape,
        compiler_params=_cparams(2),
    )(*[r[0] for r in all_rows], *pars)
    return list(res[:len(row_idx)]), list(res[len(row_idx):])


def _f_modnorm(h, w, sc, sh):
    y = h * lax.rsqrt(jnp.mean(h * h, axis=-1, keepdims=True) + EPS)
    return (y * w) * (1.0 + sc) + sh


def _f_gate_res(h, y, g):
    return h + g * y


def _f_gate_res_bias(h, y, g, b):
    return h + g * (y + b)


def _f_ffn_act(val, gate):
    return _silu(gate) * val


def _f_softplus(raw, bias):
    v = raw + bias
    return jnp.maximum(v, 0.0) + jnp.log(1.0 + jnp.exp(-jnp.abs(v)))


def _f_ssd_gate(yf, yb, xs, z, d_rep, nw):
    y = (yf + yb + d_rep * xs) * _silu(z)
    return (y * lax.rsqrt(jnp.mean(y * y, axis=-1, keepdims=True) + EPS)) * nw


def _f_glu(a, g, ba, bg):
    return (a + ba) * jax.nn.sigmoid(g + bg)


def _f_ln_silu(h, w, b):
    mu = jnp.mean(h, axis=-1, keepdims=True)
    d = h - mu
    y = d * lax.rsqrt(jnp.mean(d * d, axis=-1, keepdims=True) + EPS)
    return _silu(y * w + b)


def _f_loss_rows(h, t, w):
    y = (h * lax.rsqrt(jnp.mean(h * h, axis=-1, keepdims=True) + EPS)) * w
    e = y - t
    return 0.5 * jnp.mean(e * e, axis=-1, keepdims=True)


def _f_adamw(w, m, v, ga, gb):
    g = ga + gb
    m = ADAM_B1 * m + (1.0 - ADAM_B1) * g
    v = ADAM_B2 * v + (1.0 - ADAM_B2) * (g * g)
    m_hat = m / (1.0 - ADAM_B1 ** ADAM_STEP)
    v_hat = v / (1.0 - ADAM_B2 ** ADAM_STEP)
    delta = -ADAM_LR * (m_hat / (jnp.sqrt(v_hat) + ADAM_EPS) + ADAM_WD * w)
    return g, delta, m, v


def _adamw(name, w, m, v, ga, gb):
    shape = w.shape
    c = shape[-1]
    two_d = [t.reshape(-1, c) for t in (w, m, v, ga, gb)]
    rows = two_d[0].shape[0]
    pad = _round_up(rows, SUBLANE) - rows
    if pad:
        two_d = [jnp.pad(t, ((0, pad), (0, 0))) for t in two_d]
    outs = _rw_fwd(name, _f_adamw, two_d, [], [c] * 4)
    return tuple(o[:rows].reshape(shape) for o in outs)


def _sum_leading(name, x, idxs):
    _, R, C = x.shape
    tm = _row_tile(R, max(SUBLANE, min(512, (512 * 1024) // C)))

    def body(x_ref, o_ref):
        acc = x_ref[idxs[0]].astype(F32)
        for k in idxs[1:]:
            acc = acc + x_ref[k].astype(F32)
        o_ref[...] = acc

    return _pcall(
        body, name=name, grid=(R // tm,), in_specs=[pl.BlockSpec((x.shape[0], tm, C), lambda i: (0, i, 0))],
        out_specs=pl.BlockSpec((tm, C), lambda i: (i, 0)), out_shape=jax.ShapeDtypeStruct((R, C), F32),
        compiler_params=_cparams(1),
    )(x)


def _loss_fwd(h, t, w):
    T, D = h.shape
    tm = _row_tile(T, 256)

    def body(h_ref, t_ref, w_ref, o_ref):
        i = pl.program_id(0)
        part = jnp.sum(_f_loss_rows(h_ref[...], t_ref[...], w_ref[...]), axis=0, keepdims=True)
        part = jnp.broadcast_to(part, (1, LANE))

        @pl.when(i == 0)
        def _():
            o_ref[...] = part

        @pl.when(i > 0)
        def _():
            o_ref[...] += part

    return _pcall(
        body, name="loss_fwd", grid=(T // tm,),
        in_specs=[pl.BlockSpec((tm, D), lambda i: (i, 0)), pl.BlockSpec((tm, D), lambda i: (i, 0)),
                  pl.BlockSpec((1, D), lambda i: (0, 0))],
        out_specs=pl.BlockSpec((1, LANE), lambda i: (0, 0)), out_shape=jax.ShapeDtypeStruct((1, LANE), F32),
        compiler_params=_cparams(1),
    )(h, t, w)


CONV_ROWS = 256
CONV_ACC_ELEMS = 16384


def _col_mask(arg, t):
    col = jnp.bitwise_and(t, GRID_W - 1)
    return (col != 0) if arg < 0 else (col != GRID_W - 1)


def _conv_plan(T, C, taps):
    rc = CONV_ROWS if T % CONV_ROWS == 0 else LANE
    assert T % rc == 0
    ct = next((t for t in (512, 256, LANE) if C % t == 0), C)
    reach = max(abs(s) for s, _ in taps)
    hb = next(h for h in (8, 16, 32, 64, 128, 256) if h >= reach and rc % h == 0)
    sub = max(2 * SUBLANE, min(rc, CONV_ACC_ELEMS // ct))
    seg = [m[1] for _, m in taps if m is not None and m[0] == "seg"]
    boundary = seg[0] if seg else None
    assert boundary is None or boundary % rc == 0
    taps = [(s, None if (m is None or m[0] == "seg") else m[1]) for s, m in taps]
    return rc, ct, hb, sub, T // rc, C // ct, boundary, taps


def _halo_specs(rc, ct, hb, T, off_blocks):
    per = rc // hb
    last = T // hb - 1
    prev = pl.BlockSpec((hb, ct), lambda j, i: (jnp.maximum(i * per - 1, 0), off_blocks + j))
    cur = pl.BlockSpec((rc, ct), lambda j, i: (i, off_blocks + j))
    nxt = pl.BlockSpec((hb, ct), lambda j, i: (jnp.minimum((i + 1) * per, last), off_blocks + j))
    return [prev, cur, nxt]


def _fill_halo(pad_ref, p_ref, c_ref, n_ref, i, nrc, rc, hb, boundary):
    has_prev = i > 0
    has_next = i < nrc - 1
    if boundary is not None:
        has_prev = has_prev & (i * rc != boundary)
        has_next = has_next & ((i + 1) * rc != boundary)
    pad_ref[0:hb, :] = jnp.where(has_prev, p_ref[...], 0.0)
    pad_ref[hb:hb + rc, :] = c_ref[...]
    pad_ref[hb + rc:hb + rc + hb, :] = jnp.where(has_next, n_ref[...], 0.0)


def _shift_plan(keys):
    count = {}
    for s, m in keys:
        k = (s % SUBLANE, m)
        count[k] = count.get(k, 0) + 1
    slots = {}
    for k, n in sorted(count.items(), key=lambda kv: (kv[0][0], str(kv[0][1]))):
        if k != (0, None) and (n >= 2 or k[1] is not None):
            slots[k] = len(slots)
    return slots


def _build_shifted(copies_ref, slots, pad_ref, keys, i, rc, hb, sub):
    for (r, m), slot in slots.items():
        qs = [s - r for s, mk in keys if (s % SUBLANE, mk) == (r, m)]
        lo, hi = hb + min(qs), hb + rc + max(qs)
        for p in range(lo, hi, sub):
            n = min(sub, hi - p)
            v = pad_ref[p + r:p + r + n, :]
            if m is not None:
                t = i * rc - hb + p + r + lax.broadcasted_iota(jnp.int32, (n, 1), 0)
                v = jnp.where(_col_mask(m, t), v, 0.0)
            copies_ref[slot, p:p + n, :] = v


def _read(copies_ref, slots, pad_ref, s, m, row, n):
    k = (s % SUBLANE, m)
    if k in slots:
        q = s - k[0]
        return copies_ref[slots[k], row + q:row + q + n, :]
    return pad_ref[row + s:row + s + n, :]


def _conv_fwd(name, u, col_off, C, w, b, taps, act=False):
    T = u.shape[0]
    rc, ct, hb, sub, nrc, ncc, boundary, taps = _conv_plan(T, C, taps)
    assert col_off % ct == 0
    K = len(taps)
    keys = [(s, None) for s, _ in taps]
    slots = _shift_plan(keys)
    dirs = sorted({m for _, m in taps if m is not None})

    def body(up, uc, un, w_ref, b_ref, *rest):
        y_ref = rest[0]
        pad_ref, copies_ref = rest[-2], rest[-1]
        i = pl.program_id(1)
        _fill_halo(pad_ref, up, uc, un, i, nrc, rc, hb, boundary)
        _build_shifted(copies_ref, slots, pad_ref, keys, i, rc, hb, sub)
        for r0 in range(0, rc, sub):
            acc = jnp.broadcast_to(b_ref[...], (sub, ct))
            for m in [None] + dirs:
                part = None
                for k, (s, mk) in enumerate(taps):
                    if mk != m:
                        continue
                    term = w_ref[k:k + 1, :] * _read(copies_ref, slots, pad_ref, s, None, hb + r0, sub)
                    part = term if part is None else part + term
                if part is None:
                    continue
                if m is not None:
                    t = i * rc + r0 + lax.broadcasted_iota(jnp.int32, (sub, 1), 0)
                    part = jnp.where(_col_mask(m, t), part, 0.0)
                acc = acc + part
            y_ref[r0:r0 + sub, :] = acc
            if act:
                rest[1][r0:r0 + sub, :] = _silu(acc)

    n_out = 2 if act else 1
    res = _pcall(
        body, name=name, grid=(ncc, nrc),
        in_specs=_halo_specs(rc, ct, hb, T, col_off // ct) + [pl.BlockSpec((K, ct), lambda j, i: (0, j)),
                                                              pl.BlockSpec((1, ct), lambda j, i: (0, j))],
        out_specs=[pl.BlockSpec((rc, ct), lambda j, i: (i, j))] * n_out,
        out_shape=[jax.ShapeDtypeStruct((T, C), F32)] * n_out,
        scratch_shapes=[pltpu.VMEM((rc + 2 * hb, ct), F32), pltpu.VMEM((max(len(slots), 1), rc + 2 * hb, ct), F32)],
        compiler_params=_cparams(2),
    )(u, u, u, w, b)
    return res if act else res[0]


def _conv_bwd(name, u, col_off, C, w, g, taps, du_dtype=F32):
    T = u.shape[0]
    rc, ct, hb, sub, nrc, ncc, boundary, taps = _conv_plan(T, C, taps)
    K = len(taps)
    u_keys = [(s, None) for s, _ in taps]
    dirs = sorted({m for _, m in taps if m is not None})
    g_keys = [(-s, m) for s, m in taps] + [(0, m) for m in dirs]
    u_slots, g_slots = _shift_plan(u_keys), _shift_plan(g_keys)

    def body(up, uc, un, gp, gc, gn, w_ref, du_ref, dw_ref, db_ref, upad, gpad, ucopies, gcopies):
        i = pl.program_id(1)
        _fill_halo(upad, up, uc, un, i, nrc, rc, hb, boundary)
        _fill_halo(gpad, gp, gc, gn, i, nrc, rc, hb, boundary)
        _build_shifted(ucopies, u_slots, upad, u_keys, i, rc, hb, sub)
        _build_shifted(gcopies, g_slots, gpad, g_keys, i, rc, hb, sub)

        @pl.when(i == 0)
        def _():
            dw_ref[...] = jnp.zeros_like(dw_ref)
            db_ref[...] = jnp.zeros_like(db_ref)

        def fold(v):
            return jnp.sum(v.reshape(sub // SUBLANE, SUBLANE, ct), axis=0)

        dbs = jnp.zeros((SUBLANE, ct), F32)
        for r0 in range(0, rc, sub):
            dbs = dbs + fold(gpad[hb + r0:hb + r0 + sub, :])
            acc = jnp.zeros((sub, ct), F32)
            for k, (s, m) in enumerate(taps):
                acc = acc + w_ref[k:k + 1, :] * _read(gcopies, g_slots, gpad, -s, m, hb + r0, sub)
            du_ref[r0:r0 + sub, :] = acc.astype(du_ref.dtype)
        db_ref[...] += jnp.sum(dbs, axis=0, keepdims=True)
        for k, (s, m) in enumerate(taps):
            part = jnp.zeros((SUBLANE, ct), F32)
            for r0 in range(0, rc, sub):
                part = part + fold(_read(gcopies, g_slots, gpad, 0, m, hb + r0, sub)
                                   * _read(ucopies, u_slots, upad, s, None, hb + r0, sub))
            dw_ref[k:k + 1, :] += jnp.sum(part, axis=0, keepdims=True)

    halo_u = _halo_specs(rc, ct, hb, T, col_off // ct)
    halo_g = _halo_specs(rc, ct, hb, T, 0)
    rows = rc + 2 * hb
    return _pcall(
        body, name=name, grid=(ncc, nrc),
        in_specs=halo_u + halo_g + [pl.BlockSpec((K, ct), lambda j, i: (0, j))],
        out_specs=[pl.BlockSpec((rc, ct), lambda j, i: (i, j)), pl.BlockSpec((K, ct), lambda j, i: (0, j)),
                   pl.BlockSpec((1, ct), lambda j, i: (0, j))],
        out_shape=[jax.ShapeDtypeStruct((T, C), du_dtype), jax.ShapeDtypeStruct((K, C), F32),
                   jax.ShapeDtypeStruct((1, C), F32)],
        scratch_shapes=[pltpu.VMEM((rows, ct), F32), pltpu.VMEM((rows, ct), F32),
                        pltpu.VMEM((max(len(u_slots), 1), rows, ct), F32),
                        pltpu.VMEM((max(len(g_slots), 1), rows, ct), F32)],
        compiler_params=_cparams(2),
    )(u, u, u, g, g, g, w)


def _ssd_group(xg, bm, cm, s_in, *per_head, reverse, P):
    R = len(per_head) // 2
    dtrs, a_s = per_head[:R], per_head[R:]
    q, rp = xg.shape
    ii = lax.broadcasted_iota(jnp.int32, (q, q), 0)
    jj = lax.broadcasted_iota(jnp.int32, (q, q), 1)
    causal = (jj >= ii) if reverse else (jj <= ii)
    causal_t = (ii >= jj) if reverse else (ii <= jj)
    eye = ii == jj
    lane = lax.broadcasted_iota(jnp.int32, (1, rp), 1)
    row = lax.broadcasted_iota(jnp.int32, (rp, 1), 0)
    nt = (((1,), (1,)), ((), ()))
    tn = (((0,), (0,)), ((), ()))
    cb = lax.dot_general(cm.astype(BF16), bm.astype(BF16), nt, preferred_element_type=F32)
    dt_x = jnp.zeros((q, rp), F32)
    acum_x = jnp.zeros((q, rp), F32)
    tot_row = jnp.zeros((1, rp), F32)
    tot_col = jnp.zeros((rp, 1), F32)
    wts, lane_masks = [], []
    for r in range(R):
        hm = (lane >= r * P) & (lane < (r + 1) * P)
        hc = (row >= r * P) & (row < (r + 1) * P)
        dt_c = jnp.sum(jnp.where(eye, dtrs[r], 0.0), axis=1, keepdims=True)
        dac = dt_c * a_s[r]
        dar = dtrs[r] * a_s[r]
        acum_c = jnp.sum(jnp.where(causal, dar, 0.0), axis=1, keepdims=True)
        acum_r = jnp.sum(jnp.where(causal_t, dac, 0.0), axis=0, keepdims=True)
        decay = jnp.where(causal, jnp.exp(jnp.where(causal, acum_c - acum_r, 0.0)), 0.0)
        tot = jnp.sum(dac, axis=0, keepdims=True)
        dt_x = jnp.where(hm, dt_c, dt_x)
        acum_x = jnp.where(hm, acum_c, acum_x)
        tot_row = jnp.where(hm, tot, tot_row)
        tot_col = jnp.where(hc, tot, tot_col)
        wts.append((cb * decay).astype(BF16))
        lane_masks.append(hm)
    xdt = xg * dt_x
    xdt_b = xdt.astype(BF16)
    y = jnp.zeros((q, rp), F32)
    for r in range(R):
        y = jnp.where(lane_masks[r], jnp.dot(wts[r], xdt_b, preferred_element_type=F32), y)
    dte = jnp.exp(tot_row - acum_x)
    cs = lax.dot_general((xdt * dte).astype(BF16), bm.astype(BF16), tn, preferred_element_type=F32)
    y = y + lax.dot_general(cm.astype(BF16), s_in.astype(BF16), nt, preferred_element_type=F32) * jnp.exp(acum_x)
    s_out = jnp.exp(tot_col) * s_in + cs
    return y, s_out


def _ssd_maps(NC, ncc, reverse_steps):
    def chunk(d, s):
        if reverse_steps:
            s = NC - 1 - s
        return s if d == 0 else jnp.where(s < ncc, ncc - 1 - s, NC - 1 - s + ncc)

    def lat_chunk(d, s):
        c = chunk(d, s) - ncc
        return jnp.where(c < 0, 0 if d == 0 else NC - ncc - 1, c)

    def step(s):
        return NC - 1 - s if reverse_steps else s

    return chunk, lat_chunk, step


def _ssd_specs(chunk, d, R, Q, N, RP, bo, co):
    return [
        pl.BlockSpec((Q, RP), lambda g, s: (chunk(d, s), g)),
        pl.BlockSpec((Q, N), lambda g, s: (chunk(d, s), bo + g)),
        pl.BlockSpec((Q, N), lambda g, s: (chunk(d, s), co + g)),
        pl.BlockSpec((R, 1, Q), lambda g, s: (g, 0, chunk(d, s))),
        pl.BlockSpec((R, 1, 1), lambda g, s: (g, 0, 0)),
    ]


def _ssd_fwd(xbc, b_off, c_off, dtr, a, P, ncc):
    T = xbc.shape[0]
    H = dtr[0].shape[0]
    N, Q = SSD_STATE, SSD_CHUNK
    NC = T // Q
    G = (c_off - b_off) // N
    R = H // G
    RP = R * P
    chunk, lat_chunk, _ = _ssd_maps(NC, ncc, False)

    def body(*refs):
        s = pl.program_id(1)
        s_ref = refs[-1]

        @pl.when(s == 0)
        def _():
            s_ref[...] = jnp.zeros_like(s_ref)

        for d in range(2):
            x_ref, b_ref, c_ref, dtr_ref, a_ref = refs[5 * d:5 * d + 5]
            y_ref, se_ref = refs[10 + 2 * d:12 + 2 * d]
            s_in = s_ref[d]
            se_ref[...] = s_in
            per_head = [dtr_ref[r] for r in range(R)] + [a_ref[r] for r in range(R)]
            y, s_out = _ssd_group(x_ref[...], b_ref[...], c_ref[...], s_in, *per_head, reverse=d == 1, P=P)
            y_ref[...] = y
            s_ref[d] = s_out

    in_specs, out_specs, out_shape, operands = [], [], [], []
    for d in range(2):
        in_specs += _ssd_specs(chunk, d, R, Q, N, RP, b_off // N, c_off // N)
        operands += [xbc, xbc, xbc, dtr[d], a[d]]
        out_specs += [pl.BlockSpec((Q, RP), functools.partial(lambda g, s, d: (lat_chunk(d, s), g), d=d)),
                      pl.BlockSpec((None, None, RP, N), lambda g, s: (g, s, 0, 0))]
        out_shape += [jax.ShapeDtypeStruct((T - ncc * Q, H * P), F32), jax.ShapeDtypeStruct((G, NC, RP, N), F32)]
    y_f, se_f, y_b, se_b = _pcall(
        body, name="ssd_fwd", grid=(G, NC), in_specs=in_specs, out_specs=out_specs, out_shape=out_shape,
        scratch_shapes=[pltpu.VMEM((2, RP, N), F32)], compiler_params=_cparams(2),
    )(*operands)
    return (y_f, y_b), (se_f, se_b)


def _ssd_bwd(xbc, b_off, c_off, dtr, a, s_enter, dy, P, ncc):
    T = xbc.shape[0]
    H = dtr[0].shape[0]
    N, Q = SSD_STATE, SSD_CHUNK
    NC = T // Q
    G = (c_off - b_off) // N
    R = H // G
    RP = R * P
    chunk, lat_chunk, step = _ssd_maps(NC, ncc, True)
    n_in, n_out = 7, 5

    def body(*refs):
        s = pl.program_id(1)
        ds_ref = refs[-1]

        @pl.when(s == 0)
        def _():
            ds_ref[...] = jnp.zeros_like(ds_ref)

        for d in range(2):
            x_ref, b_ref, c_ref, dtr_ref, a_ref, se_ref, dy_ref = refs[n_in * d:n_in * (d + 1)]
            dx_ref, db_ref, dc_ref, ddtr_ref, da_ref = refs[2 * n_in + n_out * d:2 * n_in + n_out * (d + 1)]
            per_head = [dtr_ref[r] for r in range(R)] + [a_ref[r] for r in range(R)]
            f = functools.partial(_ssd_group, reverse=d == 1, P=P)
            _, vjp = jax.vjp(f, x_ref[...], b_ref[...], c_ref[...], se_ref[...], *per_head)
            is_latent = chunk(d, s) >= ncc
            dy_v = jnp.where(is_latent, dy_ref[...], 0.0)
            grads = vjp((dy_v, ds_ref[d]))
            dx_ref[...] = grads[0]
            db_ref[...] = grads[1]
            dc_ref[...] = grads[2]
            ds_ref[d] = grads[3]
            for r in range(R):
                ddtr_ref[r] = grads[4 + r]
                da_ref[r] = jnp.broadcast_to(grads[4 + R + r], (SUBLANE, LANE))

    in_specs, out_specs, out_shape, operands = [], [], [], []
    for d in range(2):
        in_specs += _ssd_specs(chunk, d, R, Q, N, RP, b_off // N, c_off // N) + [
            pl.BlockSpec((None, None, RP, N), lambda g, s: (g, step(s), 0, 0)),
            pl.BlockSpec((Q, RP), functools.partial(lambda g, s, d: (lat_chunk(d, s), g), d=d)),
        ]
        operands += [xbc, xbc, xbc, dtr[d], a[d], s_enter[d], dy]
    for d in range(2):
        at_chunk = functools.partial(lambda g, s, d: (chunk(d, s), g), d=d)
        out_specs += [
            pl.BlockSpec((Q, RP), at_chunk), pl.BlockSpec((Q, N), at_chunk), pl.BlockSpec((Q, N), at_chunk),
            pl.BlockSpec((R, 1, Q), functools.partial(lambda g, s, d: (g, 0, chunk(d, s)), d=d)),
            pl.BlockSpec((R, SUBLANE, LANE), lambda g, s: (g * NC + s, 0, 0)),
        ]
        out_shape += [
            jax.ShapeDtypeStruct((T, H * P), F32), jax.ShapeDtypeStruct((T, G * N), F32),
            jax.ShapeDtypeStruct((T, G * N), F32), jax.ShapeDtypeStruct((H, 1, T), F32),
            jax.ShapeDtypeStruct((G * NC * R, SUBLANE, LANE), F32),
        ]
    res = _pcall(
        body, name="ssd_bwd", grid=(G, NC), in_specs=in_specs, out_specs=out_specs, out_shape=out_shape,
        scratch_shapes=[pltpu.VMEM((2, RP, N), F32)], compiler_params=_cparams(2),
    )(*operands)
    return res[:n_out], res[n_out:]


def _allgather8(name, v):
    R, C = v.shape

    def body(x_ref, out_ref, send_sems, recv_sems, local_sem):
        x, y, c = lax.axis_index("x"), lax.axis_index("y"), lax.axis_index("c")
        me, sibling = (x, y, c), (x, y, 1 - c)
        chips = [(1 - x, y), (x, 1 - y), (1 - x, 1 - y)]

        def slot(px, py, pc):
            return out_ref.at[4 * px + 2 * py + pc]

        def copy(k, block, to, src=None):
            return pltpu.make_async_remote_copy(
                src_ref=slot(*block) if src is None else src, dst_ref=slot(*block),
                send_sem=send_sems.at[k], recv_sem=recv_sems.at[k], device_id=to, device_id_type=MESH)

        mine = pltpu.make_async_copy(x_ref, slot(*me), local_sem)
        mine.start()
        first = [copy(0, me, sibling, src=x_ref)]
        first += [copy(1 + j, me, (*chip, c), src=x_ref) for j, chip in enumerate(chips)]
        for cp in first:
            cp.start()
        passed = [copy(4 + j, (*chip, c), sibling) for j, chip in enumerate(chips)]
        for j, chip in enumerate(chips):
            copy(1 + j, (*chip, c), me).wait_recv()
            passed[j].start()
        copy(0, sibling, me).wait_recv()
        for j, chip in enumerate(chips):
            copy(4 + j, (*chip, 1 - c), me).wait_recv()
        for cp in first + passed:
            cp.wait_send()
        mine.wait()

    return _pcall(
        body, name=name, out_shape=jax.ShapeDtypeStruct((N_DEV, R, C), v.dtype),
        in_specs=[pl.BlockSpec(memory_space=pltpu.VMEM)], out_specs=pl.BlockSpec(memory_space=pltpu.VMEM),
        scratch_shapes=[pltpu.SemaphoreType.DMA((7,)), pltpu.SemaphoreType.DMA((7,)), pltpu.SemaphoreType.DMA],
        compiler_params=pltpu.CompilerParams(vmem_limit_bytes=VMEM_LIMIT_BYTES),
    )(v)


def _exchange4_start(name, srcs, bcast, dep):
    n = len(srcs)
    lands = [lax.empty(((N_CHIPS,) + s.shape) if bcast else s.shape, s.dtype) for s in srcs]

    def body(*refs):
        src, land = refs[:n], refs[n:2 * n]
        send_sems, recv_sems = refs[2 * n + 1], refs[2 * n + 2]
        token = refs[-1]
        x, y, c = lax.axis_index("x"), lax.axis_index("y"), lax.axis_index("c")
        me = 2 * x + y
        for a in range(n):
            for j, (px, py) in enumerate([(1 - x, y), (x, 1 - y), (1 - x, 1 - y)]):
                pltpu.make_async_remote_copy(
                    src_ref=src[a] if bcast else src[a].at[2 * px + py], dst_ref=land[a].at[me],
                    send_sem=send_sems.at[3 * a + j], recv_sem=recv_sems.at[3 * a + j], device_id=(px, py, c),
                    device_id_type=MESH).start()
        token[...] = jnp.zeros_like(token)

    hbm = pl.BlockSpec(memory_space=pltpu.HBM)
    sem = pl.BlockSpec(memory_space=pltpu.SEMAPHORE)
    outs = _pcall(
        body, name=name,
        out_shape=(pltpu.SemaphoreType.DMA((3 * n,)), pltpu.SemaphoreType.DMA((3 * n,)),
                   *[pltpu.HBM(s.shape, s.dtype) for s in srcs], *[pltpu.HBM(l.shape, l.dtype) for l in lands],
                   jax.ShapeDtypeStruct((SUBLANE, LANE), F32)),
        in_specs=[hbm] * (2 * n) + [pl.BlockSpec(memory_space=pl.ANY)],
        out_specs=(sem, sem, *[hbm] * (2 * n), pl.BlockSpec(memory_space=pltpu.VMEM)),
        input_output_aliases={k: 2 + k for k in range(2 * n)},
        compiler_params=pltpu.CompilerParams(has_side_effects=pltpu.SideEffectType.DATAFLOW_SIDE_EFFECTING),
    )(*[pltpu.with_memory_space_constraint(s, pltpu.HBM) for s in srcs],
      *[pltpu.with_memory_space_constraint(l, pltpu.HBM) for l in lands], dep)
    return (n, bcast, outs[0], outs[1], outs[2:2 + n], outs[2 + n:2 + 2 * n]), outs[-1]


def _exchange4_wait(name, handle, after):
    n, bcast, send_sems, recv_sems, src_thru, land_thru = handle

    def body(*refs):
        src, land = refs[:n], refs[n:2 * n]
        send_sems, recv_sems = refs[2 * n], refs[2 * n + 1]
        x, y, c = lax.axis_index("x"), lax.axis_index("y"), lax.axis_index("c")
        for a in range(n):
            for j, (px, py) in enumerate([(1 - x, y), (x, 1 - y), (1 - x, 1 - y)]):
                pk = 2 * px + py
                copy = pltpu.make_async_remote_copy(
                    src_ref=src[a] if bcast else src[a].at[pk], dst_ref=land[a].at[pk],
                    send_sem=send_sems.at[3 * a + j], recv_sem=recv_sems.at[3 * a + j], device_id=(px, py, c),
                    device_id_type=MESH)
                copy.wait_send()
                copy.wait_recv()

    hbm = pl.BlockSpec(memory_space=pltpu.HBM)
    sem = pl.BlockSpec(memory_space=pltpu.SEMAPHORE)
    outs = _pcall(
        body, name=name,
        out_shape=tuple(pltpu.HBM(t.shape, t.dtype) for t in (*src_thru, *land_thru)),
        in_specs=[hbm] * (2 * n) + [sem, sem, pl.BlockSpec(memory_space=pl.ANY)], out_specs=tuple([hbm] * (2 * n)),
        input_output_aliases={k: k for k in range(2 * n)},
        compiler_params=pltpu.CompilerParams(has_side_effects=pltpu.SideEffectType.DATAFLOW_SIDE_EFFECTING),
    )(*src_thru, *land_thru, send_sems, recv_sems, after)
    return list(outs[n:])


def _tie(name, v, token):
    def body(v_ref, token_ref, o_ref):
        del v_ref, token_ref, o_ref

    any_spec = pl.BlockSpec(memory_space=pl.ANY)
    return _pcall(body, name=name, out_shape=jax.ShapeDtypeStruct(v.shape, v.dtype), in_specs=[any_spec, any_spec],
                  out_specs=any_spec, input_output_aliases={0: 0})(v, token)


def _fill_own(landed, own, me, bcast):
    blk = own if bcast else lax.dynamic_index_in_dim(own, me, 0, keepdims=False)
    return lax.dynamic_update_index_in_dim(landed, blk, me, 0)


def _swap_sibling(name, srcs):
    n = len(srcs)

    def body(*refs):
        src, out = refs[:n], refs[n:2 * n]
        send_sems, recv_sems = refs[2 * n:]
        x, y, c = lax.axis_index("x"), lax.axis_index("y"), lax.axis_index("c")
        copies = []
        for a in range(n):
            rc = pltpu.make_async_remote_copy(
                src_ref=src[a], dst_ref=out[a], send_sem=send_sems.at[a], recv_sem=recv_sems.at[a],
                device_id=(x, y, 1 - c), device_id_type=MESH)
            rc.start()
            copies.append(rc)
        for cp in copies:
            cp.wait()

    any_spec = pl.BlockSpec(memory_space=pl.ANY)
    return _pcall(
        body, name=name, out_shape=[jax.ShapeDtypeStruct(s.shape, s.dtype) for s in srcs],
        in_specs=[any_spec] * n, out_specs=[any_spec] * n,
        scratch_shapes=[pltpu.SemaphoreType.DMA((n,)), pltpu.SemaphoreType.DMA((n,))],
    )(*srcs)


def _mod_fwd(c16, mod_w, mod_b_shard):
    nl, D, S = mod_w.shape

    def body(c_ref, w_ref, b_ref, o_ref):
        s = _silu(c_ref[...]).astype(BF16)
        o_ref[...] = jnp.dot(s, w_ref[...].astype(BF16), preferred_element_type=F32) + b_ref[...]

    return _pcall(
        body, name="mod_fwd", grid=(nl,),
        in_specs=[pl.BlockSpec((16, D), lambda l: (0, 0)), pl.BlockSpec((None, D, S), lambda l: (l, 0, 0)),
                  pl.BlockSpec((None, 1, S), lambda l: (l, 0, 0))],
        out_specs=pl.BlockSpec((None, 16, S), lambda l: (l, 0, 0)),
        out_shape=jax.ShapeDtypeStruct((nl, 16, S), F32), compiler_params=_cparams(1),
    )(c16, mod_w, mod_b_shard)


def _mod_w_update(s16t, dm16, w, m, v):
    nl, D, S = w.shape
    tm = _row_tile(D, 256)

    def body(s_ref, dm_ref, w_ref, m_ref, v_ref, g_ref, dl_ref, nm_ref, nv_ref):
        g = jnp.dot(s_ref[...], dm_ref[...], preferred_element_type=F32, precision=HIGHEST)
        g, dl, nm, nv = _f_adamw(w_ref[...], m_ref[...], v_ref[...], g, jnp.zeros_like(g))
        g_ref[...] = g
        dl_ref[...] = dl
        nm_ref[...] = nm
        nv_ref[...] = nv

    big = pl.BlockSpec((None, tm, S), lambda l, i: (l, i, 0))
    return _pcall(
        body, name="mod_w_update", grid=(nl, D // tm),
        in_specs=[pl.BlockSpec((tm, 16), lambda l, i: (i, 0)), pl.BlockSpec((None, 16, S), lambda l, i: (l, 0, 0)),
                  big, big, big],
        out_specs=[big] * 4, out_shape=[jax.ShapeDtypeStruct(w.shape, F32)] * 4, compiler_params=_cparams(2),
    )(s16t, dm16, w, m, v)


def _pack(arrs):
    flat = jnp.concatenate([a.reshape(-1).astype(F32) for a in arrs])
    n = flat.shape[0]
    rows = _round_up(_cdiv(n, LANE), SUBLANE)
    return jnp.pad(flat, (0, rows * LANE - n)).reshape(rows, LANE)


def _unpack(buf, shapes):
    flat = buf.reshape(-1)
    out, pos = [], 0
    for s in shapes:
        n = 1
        for d in s:
            n *= d
        out.append(flat[pos:pos + n].reshape(s))
        pos += n
    return out


SHARD_AXIS = {
    "mod_w": 2, "ssd_w_in": 2, "ssd_conv_w": 2, "ssd_w_out": 1, "conf_w_pw1": 2, "conf_b_pw1": 1, "conf_w_dw": 2,
    "conf_b_dw": 1, "conf_ln_w": 1, "conf_ln_b": 1, "conf_w_pw2": 1, "conf_b_pw2": 1, "ffn_w_up": 2,
    "ffn_conv_w": 3, "ffn_w_down": 1,
}
BIG = ("ssd_w_in", "ssd_w_out", "conf_w_pw1", "conf_w_pw2", "ffn_w_up", "ffn_w_down")
WEIGHTS = ("c_ctx", "mod_w", "mod_b", "norm1_w", "norm2_w", "ssd_w_in", "ssd_conv_w", "ssd_conv_b", "ssd_dt_bias",
           "ssd_a_log", "ssd_d", "ssd_norm_w", "ssd_w_out", "conf_w_pw1", "conf_b_pw1", "conf_w_dw", "conf_b_dw",
           "conf_ln_w", "conf_ln_b", "conf_w_pw2", "conf_b_pw2", "ffn_w_up", "ffn_conv_w", "ffn_conv_b",
           "ffn_w_down", "final_norm_w")
SMALL = tuple(n for n in WEIGHTS if n not in BIG and n != "mod_w")
SMALL_SHARDED = tuple(n for n in SMALL if n in SHARD_AXIS)


def _unshard(stacked, axis):
    return jnp.concatenate([stacked[k] for k in range(N_CHIPS)], axis=axis)


def _to_blocks(full, axis):
    return jnp.stack(jnp.split(full, N_CHIPS, axis=axis))


def _par(v):
    v = v.reshape(-1, v.shape[-1])
    return v[:, None, :]


def kernel(x, c, ctx, c_ctx, mod_w, mod_b, norm1_w, norm2_w, ssd_w_in, ssd_conv_w, ssd_conv_b, ssd_dt_bias, ssd_a_log, ssd_d, ssd_norm_w, ssd_w_out, conf_w_pw1, conf_b_pw1, conf_w_dw, conf_b_dw, conf_ln_w, conf_ln_b, conf_w_pw2, conf_b_pw2, ffn_w_up, ffn_conv_w, ffn_conv_b, ffn_w_down, final_norm_w, loss_target, m_c_ctx, m_mod_w, m_mod_b, m_norm1_w, m_norm2_w, m_ssd_w_in, m_ssd_conv_w, m_ssd_conv_b, m_ssd_dt_bias, m_ssd_a_log, m_ssd_d, m_ssd_norm_w, m_ssd_w_out, m_conf_w_pw1, m_conf_b_pw1, m_conf_w_dw, m_conf_b_dw, m_conf_ln_w, m_conf_ln_b, m_conf_w_pw2, m_conf_b_pw2, m_ffn_w_up, m_ffn_conv_w, m_ffn_conv_b, m_ffn_w_down, m_final_norm_w, v_c_ctx, v_mod_w, v_mod_b, v_norm1_w, v_norm2_w, v_ssd_w_in, v_ssd_conv_w, v_ssd_conv_b, v_ssd_dt_bias, v_ssd_a_log, v_ssd_d, v_ssd_norm_w, v_ssd_w_out, v_conf_w_pw1, v_conf_b_pw1, v_conf_w_dw, v_conf_b_dw, v_conf_ln_w, v_conf_ln_b, v_conf_w_pw2, v_conf_b_pw2, v_ffn_w_up, v_ffn_conv_w, v_ffn_conv_b, v_ffn_w_down, v_final_norm_w):
    given = dict(locals())
    W = {n: given[n] for n in WEIGHTS}
    Mo = {n: given["m_" + n] for n in WEIGHTS}
    Vo = {n: given["v_" + n] for n in WEIGHTS}

    ax, ay, ac = lax.axis_index("x"), lax.axis_index("y"), lax.axis_index("c")
    chip = 2 * ax + ay
    dev = 4 * ax + 2 * ay + ac

    D = x.shape[-1]
    L, Lc = x.shape[1], ctx.shape[1]
    T0 = L + Lc
    H = ssd_a_log.shape[-1]
    DI = ssd_norm_w.shape[-1]
    P = DI // H
    CD = ssd_conv_b.shape[-1]
    N = SSD_STATE
    G = (CD - DI) // (2 * N)
    FH = ffn_conv_b.shape[-1]
    KS = ssd_conv_w.shape[1]
    KC = conf_w_dw.shape[1]
    ncc = Lc // SSD_CHUNK

    shard_b = {n: W[n].astype(BF16) for n in BIG}
    gather_a, token = _exchange4_start("gather_w_in_start", [shard_b["ssd_w_in"]], True, x)
    c = _tie("tie_gather_w_in", c, token)

    small_shard_shapes = [W[n].shape for n in SMALL_SHARDED]
    f1 = _allgather8("gather_small", _pack([c] + [W[n] for n in SMALL_SHARDED]))
    c_rows, full_small = [], {n: [] for n in SMALL_SHARDED}
    for k in range(N_DEV):
        parts = _unpack(f1[k], [c.shape] + small_shard_shapes)
        c_rows.append(parts[0])
        if k % 2 == 0:
            for n, p in zip(SMALL_SHARDED, parts[1:]):
                full_small[n].append(p)
    Wf = dict(W)
    for n in SMALL_SHARDED:
        Wf[n] = jnp.concatenate(full_small[n], axis=SHARD_AXIS[n])
    c16 = jnp.concatenate(c_rows + [c_ctx[None, :], jnp.zeros((16 - N_DEV - 1, D), F32)], axis=0)

    S_mod = mod_w.shape[-1]
    mod_b_shard = lax.dynamic_slice_in_dim(mod_b, chip * S_mod, S_mod, axis=1)[:, None, :]
    mod_part = _mod_fwd(c16, mod_w, mod_b_shard)
    f2 = _allgather8("gather_mod", mod_part.reshape(2 * 16, S_mod))
    mods = jnp.concatenate([f2[2 * k].reshape(2, 16, S_mod) for k in range(N_CHIPS)], axis=-1)
    my = lax.dynamic_slice_in_dim(mods, dev, 1, axis=1)[:, 0]
    sh1, sc1, g1, sh2, sc2, g2 = [[my[l, k * D:(k + 1) * D] for l in range(2)] for k in range(6)]
    csh1, csc1 = mods[0, N_DEV, 0:D], mods[0, N_DEV, D:2 * D]

    def full_weight(n, landed):
        return _unshard(_fill_own(landed, shard_b[n], chip, True), SHARD_AXIS[n])

    xl = x[0]
    hcat = jnp.concatenate([ctx[0], xl], axis=0)
    n1w0, n2w0, n1w1, n2w1 = _par(norm1_w[0]), _par(norm2_w[0]), _par(norm1_w[1]), _par(norm2_w[1])
    sc_seg = jnp.stack([csc1, sc1[0]])[:, None, :]
    sh_seg = jnp.stack([csh1, sh1[0]])[:, None, :]

    a0 = _rw_fwd("l0_modnorm1", _f_modnorm, [hcat], [n1w0, sc_seg, sh_seg], [D], seg_rows=(Lc,), out_dtypes=[BF16])
    (landed_in,) = _exchange4_wait("gather_w_in_wait", gather_a, a0)
    w_in = full_weight("ssd_w_in", landed_in)[0]
    rest = [n for n in BIG if n != "ssd_w_in"]
    gather_b, token = _exchange4_start("gather_rest_start", [shard_b[n] for n in rest], True, landed_in)
    a0 = _tie("tie_gather_rest", a0, token)
    proj = _mm(a0, w_in, name="l0_w_in")
    seg_taps = [(k - KS // 2, ("seg", Lc)) for k in range(KS)]
    xbc_pre, xbc = _conv_fwd("l0_conv", proj, DI, CD, Wf["ssd_conv_w"][0], ssd_conv_b, seg_taps, act=True)
    dt_raw = proj[:, DI + CD:]
    dt_bias = _par(ssd_dt_bias.reshape(1, 2 * H))
    dt = _rw_fwd("l0_softplus", _f_softplus, [dt_raw], [dt_bias], [2 * H])
    dt_t = dt.T
    dtr = (dt_t[:H, None, :], dt_t[H:, None, :])
    a_all = -jnp.exp(ssd_a_log.reshape(2, H, 1, 1))
    a_neg = (a_all[0], a_all[1])
    (y_f, y_b), s_enter = _ssd_fwd(xbc, DI, DI + G * N, dtr, a_neg, P, ncc)
    gate_rows = [y_f, y_b, (xbc, 0, DI, Lc), (proj, 0, DI, Lc)]
    d_rep = _par(jnp.repeat(ssd_d[0], P))
    ssd_nw = _par(ssd_norm_w[0])
    yn = _rw_fwd("l0_ssd_gate", _f_ssd_gate, gate_rows, [d_rep, ssd_nw], [DI], T=L, out_dtypes=[BF16])
    Wb = {n: full_weight(n, g) for n, g in zip(rest, _exchange4_wait("gather_rest_wait", gather_b, yn))}
    w_out, w_pw1, w_pw2 = Wb["ssd_w_out"][0], Wb["conf_w_pw1"][0], Wb["conf_w_pw2"][0]
    w_up, w_dn = Wb["ffn_w_up"], Wb["ffn_w_down"]
    mix0 = _mm(yn, w_out, name="l0_w_out")
    g1_0, g2_0, g1_1, g2_1 = _par(g1[0]), _par(g2[0]), _par(g1[1]), _par(g2[1])
    h1 = _rw_fwd("l0_res1", _f_gate_res, [xl, mix0], [g1_0], [D])

    grid_taps = [((i - 1) * GRID_W + (j - 1), (None if j == 1 else ("col", j - 1))) for i in range(3) for j in range(3)]

    def ffn_fwd(l, h, tag):
        a = _rw_fwd(tag + "_modnorm2", _f_modnorm, [h], [_par(norm2_w[l]), _par(sc2[l]), _par(sh2[l])], [D],
                    out_dtypes=[BF16])
        hh = _mm(a, w_up[l], name=tag + "_w_up")
        gc = _conv_fwd(tag + "_ffn_conv", hh, FH, FH, Wf["ffn_conv_w"][l].reshape(9, FH), ffn_conv_b[l][None, :],
                       grid_taps)
        act = _rw_fwd(tag + "_act", _f_ffn_act, [(hh, 0, FH), gc], [], [FH], col_tile=_tile(FH, 1536),
                      out_dtypes=[BF16])
        dn = _mm(act, w_dn[l], name=tag + "_w_down")
        return a, hh, gc, act, dn

    a1, hh0, gc0, act0, dn0 = ffn_fwd(0, h1, "l0")
    h2 = _rw_fwd("l0_res2", _f_gate_res, [h1, dn0], [g2_0], [D])

    a2 = _rw_fwd("l1_modnorm1", _f_modnorm, [h2], [n1w1, _par(sc1[1]), _par(sh1[1])], [D], out_dtypes=[BF16])
    pw = _mm(a2, w_pw1, name="l1_pw1")
    b_pw1 = Wf["conf_b_pw1"][0]
    glu = _rw_fwd("l1_glu", _f_glu, [(pw, 0, D), (pw, D, D)], [_par(b_pw1[:D]), _par(b_pw1[D:])], [D])
    conf_taps = [(k - KC // 2, None) for k in range(KC)]
    cv = _conv_fwd("l1_conv", glu, 0, D, Wf["conf_w_dw"][0], Wf["conf_b_dw"], conf_taps)
    ln_w, ln_b = _par(Wf["conf_ln_w"][0]), _par(Wf["conf_ln_b"][0])
    ls = _rw_fwd("l1_ln_silu", _f_ln_silu, [cv], [ln_w, ln_b], [D], out_dtypes=[BF16])
    p2 = _mm(ls, w_pw2, name="l1_pw2")
    b_pw2 = _par(Wf["conf_b_pw2"][0])
    h3 = _rw_fwd("l1_res1", _f_gate_res_bias, [h2, p2], [g1_1, b_pw2], [D])
    a3, hh1, gc1, act1, dn1 = ffn_fwd(1, h3, "l1")
    h4 = _rw_fwd("l1_res2", _f_gate_res, [h3, dn1], [g2_1], [D])

    fnw = final_norm_w[None, :]
    tgt = loss_target[0]
    loss_local = _loss_fwd(h4, tgt, fnw)[0, 0]
    loss = lax.psum(loss_local, ("x", "y", "c"))

    G_full = {}
    reduces = {}

    def start_reduce(tag, items, dep):
        def blocks_of(g, ax):
            if g.ndim == 3:
                return g
            return g.reshape(N_CHIPS, g.shape[0] // N_CHIPS, g.shape[1]) if ax == 0 else _to_blocks(g, ax)

        blocks = [blocks_of(g, ax).astype(BF16) for _, g, ax in items]
        handle, tok = _exchange4_start("reduce_" + tag + "_start", blocks, False, dep)
        reduces[tag] = ([n for n, _, _ in items], handle, blocks)
        return tok
    ones = jnp.ones((L, 1), F32)
    (dh4,), (dfnw,) = _rw_bwd("loss_bwd", _f_loss_rows, [h4, tgt], [_par(final_norm_w)], [ones],
                              row_grad=[True, False], par_grad=[True])
    G_full["final_norm_w"] = dfnw.reshape(D)

    def ffn_bwd(l, h, saved, g2_l, dh_out, tag):
        a, hh, gc, act, dn = saved
        (ddn,), (dg2,) = _rw_bwd(tag + "_res2_bwd", _f_gate_res, [h, dn], [g2_l], [dh_out],
                                 row_grad=[False, True], par_grad=[True], row_dtypes=[BF16])
        dact = _mm(ddn, w_dn[l], tb=True, name=tag + "_w_down_dx")
        dwdn = _mm(act, ddn, ta=True, name=tag + "_w_down_dw", out_dtype=BF16)
        (dval, dgc), _ = _rw_bwd(tag + "_act_bwd", _f_ffn_act, [(hh, 0, FH), gc], [], [dact],
                                 row_grad=[True, True], par_grad=[], col_tile=_tile(FH, 1536), row_dtypes=[BF16, F32])
        dgin, dcw, dcb = _conv_bwd(tag + "_ffn_conv_bwd", hh, FH, FH, Wf["ffn_conv_w"][l].reshape(9, FH), dgc,
                                   grid_taps, du_dtype=BF16)
        dhh = jnp.concatenate([dval, dgin], axis=1)
        da = _mm(dhh, w_up[l], tb=True, name=tag + "_w_up_dx")
        dwup = _mm(a, dhh, ta=True, name=tag + "_w_up_dw", out_dtype=BF16, col_blocks=N_CHIPS)
        (dh,), (dn2w, dsc2, dsh2) = _rw_bwd(
            tag + "_modnorm2_bwd", _f_modnorm, [h], [_par(norm2_w[l]), _par(sc2[l]), _par(sh2[l])], [da],
            row_grad=[True], par_grad=[True, True, True], add=dh_out)
        return dh, dict(w_down=dwdn, w_up=dwup, conv_w=dcw.reshape(3, 3, FH), conv_b=dcb.reshape(FH),
                        n2w=dn2w.reshape(D), sc2=dsc2.reshape(D), sh2=dsh2.reshape(D), g2=dg2.reshape(D))

    dh3, gf1 = ffn_bwd(1, h3, (a3, hh1, gc1, act1, dn1), g2_1, dh4, "l1")
    (dp2,), (dg1_1, db_pw2) = _rw_bwd("l1_res1_bwd", _f_gate_res_bias, [h2, p2], [g1_1, b_pw2], [dh3],
                                      row_grad=[False, True], par_grad=[True, True], row_dtypes=[BF16])
    dls = _mm(dp2, w_pw2, tb=True, name="l1_pw2_dx")
    dw_pw2 = _mm(ls, dp2, ta=True, name="l1_pw2_dw", out_dtype=BF16)
    (dcv,), (dln_w, dln_b) = _rw_bwd("l1_ln_silu_bwd", _f_ln_silu, [cv], [ln_w, ln_b], [dls],
                                     row_grad=[True], par_grad=[True, True])
    dglu, dw_dw, db_dw = _conv_bwd("l1_conv_bwd", glu, 0, D, Wf["conf_w_dw"][0], dcv, conf_taps)
    (dpa, dpg), (dba, dbg) = _rw_bwd("l1_glu_bwd", _f_glu, [(pw, 0, D), (pw, D, D)],
                                     [_par(b_pw1[:D]), _par(b_pw1[D:])], [dglu],
                                     row_grad=[True, True], par_grad=[True, True], row_dtypes=[BF16, BF16])
    dpw = jnp.concatenate([dpa, dpg], axis=1)
    da2 = _mm(dpw, w_pw1, tb=True, name="l1_pw1_dx")
    dw_pw1 = _mm(a2, dpw, ta=True, name="l1_pw1_dw", out_dtype=BF16, col_blocks=N_CHIPS)
    (dh2,), (dn1w1, dsc1_1, dsh1_1) = _rw_bwd(
        "l1_modnorm1_bwd", _f_modnorm, [h2], [n1w1, _par(sc1[1]), _par(sh1[1])], [da2],
        row_grad=[True], par_grad=[True, True, True], add=dh3)
    G_full["conf_b_pw2"] = db_pw2.reshape(1, D)
    G_full["conf_ln_w"], G_full["conf_ln_b"] = dln_w.reshape(1, D), dln_b.reshape(1, D)
    G_full["conf_w_dw"], G_full["conf_b_dw"] = dw_dw[None], db_dw.reshape(1, D)
    G_full["conf_b_pw1"] = jnp.concatenate([dba.reshape(1, D), dbg.reshape(1, D)], axis=1)

    token = start_reduce("l1", [("conf_w_pw2", dw_pw2, 0), ("conf_w_pw1", dw_pw1, 1), ("ffn_w_up1", gf1["w_up"], 1),
                                ("ffn_w_down1", gf1["w_down"], 0)], dw_pw2)
    dh2 = _tie("tie_reduce_l1", dh2, token)
    dh1, gf0 = ffn_bwd(0, h1, (a1, hh0, gc0, act0, dn0), g2_0, dh2, "l0")
    G_full["ffn_conv_w"] = jnp.stack([gf0["conv_w"], gf1["conv_w"]])
    G_full["ffn_conv_b"] = jnp.stack([gf0["conv_b"], gf1["conv_b"]])

    (dmix,), (dg1_0,) = _rw_bwd("l0_res1_bwd", _f_gate_res, [xl, mix0], [g1_0], [dh1],
                                row_grad=[False, True], par_grad=[True], row_dtypes=[BF16])
    dyn = _mm(dmix, w_out, tb=True, name="l0_w_out_dx")
    dw_out = _mm(yn, dmix, ta=True, name="l0_w_out_dw", out_dtype=BF16)
    token = start_reduce("l0", [("ffn_w_up0", gf0["w_up"], 1), ("ffn_w_down0", gf0["w_down"], 0),
                                ("ssd_w_out", dw_out, 0)], dw_out)
    dyn = _tie("tie_reduce_l0", dyn, token)
    (dy_lat, dxs_gate, dz_lat), (dd_rep, dssd_nw) = _rw_bwd(
        "l0_ssd_gate_bwd", _f_ssd_gate, gate_rows, [d_rep, ssd_nw], [dyn],
        row_grad=[True, False, True, True], par_grad=[True, True], T=L, row_dtypes=[F32, F32, BF16])
    g_f, g_b = _ssd_bwd(xbc, DI, DI + G * N, dtr, a_neg, s_enter, dy_lat, P, ncc)
    dxs_gate_all = jnp.pad(dxs_gate, ((Lc, 0), (0, 0)))
    silu_bwd = functools.partial(_rw_bwd, f=_silu, pars=[], row_grad=[True], par_grad=[], T=T0)
    (dxs_pre,), _ = silu_bwd("l0_silu_bwd_x", rows=[(xbc_pre, 0, DI)], cot_fn=lambda p, q, r: p + q + r,
                             cots=[g_f[0], g_b[0], dxs_gate_all], col_tile=_tile(DI, 1024))
    (db_pre,), _ = silu_bwd("l0_silu_bwd_b", rows=[(xbc_pre, DI, G * N)], cot_fn=lambda p, q: p + q,
                            cots=[g_f[1], g_b[1]], col_tile=_tile(G * N, 1024))
    (dc_pre,), _ = silu_bwd("l0_silu_bwd_c", rows=[(xbc_pre, DI + G * N, G * N)], cot_fn=lambda p, q: p + q,
                            cots=[g_f[2], g_b[2]], col_tile=_tile(G * N, 1024))
    conv_w0 = Wf["ssd_conv_w"][0]
    pieces = []
    for tag, off, width, g_pre in (("x", 0, DI, dxs_pre), ("b", DI, G * N, db_pre), ("c", DI + G * N, G * N, dc_pre)):
        pieces.append(_conv_bwd("l0_conv_bwd_" + tag, proj, DI + off, width, conv_w0[:, off:off + width], g_pre,
                                seg_taps, du_dtype=BF16))
    dconv_in = [p[0] for p in pieces]
    dcw0 = jnp.concatenate([p[1] for p in pieces], axis=1)
    dcb0 = jnp.concatenate([p[2] for p in pieces], axis=1)
    ddt = jnp.concatenate([g_f[3][:, 0, :].T, g_b[3][:, 0, :].T], axis=1)
    (ddt_raw,), (ddt_bias,) = _rw_bwd("l0_softplus_bwd", _f_softplus, [dt_raw], [dt_bias], [ddt],
                                      row_grad=[True], par_grad=[True], row_dtypes=[BF16])
    dproj = jnp.concatenate([jnp.pad(dz_lat, ((Lc, 0), (0, 0))), *dconv_in, ddt_raw], axis=1)
    da0 = _mm(dproj, w_in, tb=True, name="l0_w_in_dx")
    dw_in = _mm(a0, dproj, ta=True, name="l0_w_in_dw", out_dtype=BF16)
    token = start_reduce("in", [("ssd_w_in", dw_in, 1)], dw_in)
    da0 = _tie("tie_reduce_in", da0, token)
    (dhcat,), (dn1w0, dsc_seg, dsh_seg) = _rw_bwd(
        "l0_modnorm1_bwd", _f_modnorm, [hcat], [n1w0, sc_seg, sh_seg], [da0],
        row_grad=[True], par_grad=[True, True, True], seg_rows=(Lc,))
    grad_x = (dhcat[Lc:] + dh1)[None]

    da_heads = jnp.stack([g[4][:, 0, 0].reshape(G, T0 // SSD_CHUNK, H // G).sum(axis=1).reshape(H)
                          for g in (g_f, g_b)])[None]
    G_full["ssd_a_log"] = da_heads * (-jnp.exp(ssd_a_log))
    G_full["ssd_dt_bias"] = ddt_bias.reshape(1, 2, H)
    G_full["ssd_d"] = dd_rep.reshape(H, P).sum(axis=1)[None]
    G_full["ssd_norm_w"] = dssd_nw.reshape(1, DI)
    G_full["ssd_conv_w"], G_full["ssd_conv_b"] = dcw0[None], dcb0.reshape(1, CD)
    G_full["norm1_w"] = jnp.stack([dn1w0.reshape(D), dn1w1.reshape(D)])
    G_full["norm2_w"] = jnp.stack([gf0["n2w"], gf1["n2w"]])

    zD = jnp.zeros((D,), F32)
    dm_own = jnp.stack([
        jnp.concatenate([dsh_seg[1, 0], dsc_seg[1, 0], dg1_0.reshape(D), gf0["sh2"], gf0["sc2"], gf0["g2"]]),
        jnp.concatenate([dsh1_1.reshape(D), dsc1_1.reshape(D), dg1_1.reshape(D), gf1["sh2"], gf1["sc2"], gf1["g2"]]),
    ])
    dmc_own = jnp.concatenate([dsh_seg[0, 0], dsc_seg[0, 0], zD, zD, zD, zD])

    small_sum_names = [n for n in SMALL if n not in ("c_ctx", "mod_b")]
    sum_part = [G_full[n] for n in small_sum_names] + [dmc_own]
    n_sum = sum(int(a.size) for a in sum_part)
    packed = _pack(sum_part + [dm_own])
    gat = _allgather8("gather_small_grads", packed)
    total = _sum_leading("sum_small_grads", gat, tuple(range(N_DEV)))
    summed = _unpack(total, [a.shape for a in sum_part])
    Gs = dict(zip(small_sum_names, summed[:-1]))
    dmc_tot = summed[-1]
    dm_all = jnp.stack([gat[k].reshape(-1)[n_sum:n_sum + 2 * 6 * D].reshape(2, 6 * D) for k in range(N_DEV)], axis=1)
    dm16 = jnp.concatenate([dm_all, jnp.stack([dmc_tot, jnp.zeros_like(dmc_tot)])[:, None, :],
                            jnp.zeros((2, 16 - N_DEV - 1, 6 * D), F32)], axis=1)
    Gs["mod_b"] = _sum_leading("sum_mod_b", dm16.transpose(1, 0, 2).reshape(16, 2 * 6 * D // LANE, LANE),
                               tuple(range(N_DEV + 1))).reshape(2, 6 * D)

    dm16_shard = lax.dynamic_slice_in_dim(dm16, chip * S_mod, S_mod, axis=2)
    ds16 = _mm(dm16_shard[0], mod_w[0], tb=True, precision=HIGHEST, name="c_ctx_dx")
    sig = jax.nn.sigmoid(c_ctx)
    dcc_part = ds16[N_DEV] * (sig * (1.0 + c_ctx * (1.0 - sig)))
    gat_cc = _allgather8("gather_c_ctx_grad", _pack([dcc_part]))
    Gs["c_ctx"] = _sum_leading("sum_c_ctx_grad", gat_cc, (0, 2, 4, 6)).reshape(-1)[:D]

    s16t = _silu(c16).T
    out = {}
    out["mod_w"] = _mod_w_update(s16t, dm16_shard, mod_w, m_mod_w, v_mod_w)

    late = out["mod_w"][0]
    partial = {}
    for tag, (names, handle, blocks) in reduces.items():
        landed = _exchange4_wait("reduce_" + tag + "_wait", handle, late)
        for n, blk, own in zip(names, landed, blocks):
            r = _fill_own(blk, own, chip, False)
            partial[n] = _sum_leading("sum4_" + n, r.reshape(N_CHIPS, -1, r.shape[-1]), (0, 1, 2, 3)).reshape(r.shape[1:])
    for n in ("ffn_w_up", "ffn_w_down"):
        partial[n] = jnp.stack([partial.pop(n + "0"), partial.pop(n + "1")])
    partial = [partial[n].reshape(W[n].shape) for n in BIG]
    sibling = _swap_sibling("swap_grads", partial)
    for n, mine, sib in zip(BIG, partial, sibling):
        out[n] = _adamw("adamw_" + n, W[n], Mo[n], Vo[n], mine, sib)

    def own(n, full):
        if n in SHARD_AXIS:
            size = W[n].shape[SHARD_AXIS[n]]
            return lax.dynamic_slice_in_dim(full, chip * size, size, axis=SHARD_AXIS[n])
        return full

    g_small = [own(n, Gs[n].reshape(Wf[n].shape)) for n in SMALL]
    shapes = [W[n].shape for n in SMALL]
    pk = [_pack([W[n] for n in SMALL]), _pack([Mo[n] for n in SMALL]), _pack([Vo[n] for n in SMALL]), _pack(g_small)]
    res = _adamw("adamw_small", pk[0], pk[1], pk[2], pk[3], jnp.zeros_like(pk[3]))
    unpacked = [_unpack(r, shapes) for r in res]
    for k, n in enumerate(SMALL):
        out[n] = tuple(u[k] for u in unpacked)

    grads = [out[n][0] for n in WEIGHTS]
    deltas = [out[n][1] for n in WEIGHTS]
    new_m = [out[n][2] for n in WEIGHTS]
    new_v = [out[n][3] for n in WEIGHTS]
    return (loss, grad_x, *grads, *deltas, *new_m, *new_v)
```

```python
import functools

import jax
import jax.numpy as jnp
from jax import lax
from jax.experimental import pallas as pl
from jax.experimental.pallas import tpu as pltpu

F32 = jnp.float32
BF16 = jnp.bfloat16
MESH = pl.DeviceIdType.MESH
HIGHEST = lax.Precision.HIGHEST

VMEM_LIMIT_BYTES = 48 * 1024 * 1024
LANE = 128
SUBLANE = 8

SSD_STATE = 128
SSD_CHUNK = 128
GRID_W = 64
EPS = 1e-6
N_CHIPS = 4
N_DEV = 8

ADAM_LR = 0.001
ADAM_B1 = 0.9
ADAM_B2 = 0.999
ADAM_EPS = 1e-08
ADAM_WD = 0.01
ADAM_STEP = 10


def _pcall(body, **kw):
    return pl.pallas_call(body, **kw)


def _cparams(n_grid):
    return pltpu.CompilerParams(dimension_semantics=("arbitrary",) * n_grid, vmem_limit_bytes=VMEM_LIMIT_BYTES)


def _cdiv(a, b):
    return -(-a // b)


def _round_up(a, b):
    return _cdiv(a, b) * b


def _tile(n, cap):
    if n <= cap:
        return n
    best = None
    for t in range(LANE, cap + 1, LANE):
        if n % t == 0:
            best = t
    if best is None:
        npad = _round_up(n, LANE)
        for t in range(LANE, cap + 1, LANE):
            if npad % t == 0:
                best = t
    return best


def _row_tile(n, cap, also=()):
    best = None
    for step in (2 * SUBLANE, SUBLANE):
        for t in range(step, min(cap, n) + 1, step):
            if n % t == 0 and all(a % t == 0 for a in also):
                best = t
        if best is not None:
            break
    assert best is not None, (n, cap, also)
    return best


def _silu(v):
    return v * jax.nn.sigmoid(v)


def _mm(a, b, *, name, ta=False, tb=False, precision=None, cap=1024, out_dtype=F32, col_blocks=None):
    M, K = (a.shape[1], a.shape[0]) if ta else a.shape
    N = b.shape[0] if tb else b.shape[1]
    assert K == (b.shape[1] if tb else b.shape[0]), (a.shape, b.shape, ta, tb)
    tm, tk = _tile(M, cap), _tile(K, cap)
    tn = _tile(N, cap) if col_blocks is None else _tile(N // col_blocks, cap + cap // 2)
    nm, nn, nk = _cdiv(M, tm), _cdiv(N, tn), _cdiv(K, tk)
    k_tail = K % tk
    exact = precision is not None

    def body(a_ref, b_ref, o_ref, acc_ref):
        k = pl.program_id(2)

        @pl.when(k == 0)
        def _():
            acc_ref[...] = jnp.zeros_like(acc_ref)

        av = a_ref[...]
        bv = b_ref[...]
        if k_tail:
            lim = K - k * tk
            ka = lax.broadcasted_iota(jnp.int32, av.shape, 0 if ta else 1)
            kb = lax.broadcasted_iota(jnp.int32, bv.shape, 1 if tb else 0)
            av = jnp.where(ka < lim, av, jnp.zeros_like(av))
            bv = jnp.where(kb < lim, bv, jnp.zeros_like(bv))
        if exact:
            av = av.astype(F32)
            bv = bv.astype(F32)
        else:
            av = av.astype(BF16)
            bv = bv.astype(BF16)
        dn = (((0 if ta else 1,), (1 if tb else 0,)), ((), ()))
        acc_ref[...] += lax.dot_general(av, bv, dn, preferred_element_type=F32, precision=precision)

        @pl.when(k == nk - 1)
        def _():
            o_ref[...] = acc_ref[...].astype(o_ref.dtype)

    a_spec = pl.BlockSpec((tk, tm), lambda i, j, k: (k, i)) if ta else pl.BlockSpec((tm, tk), lambda i, j, k: (i, k))
    b_spec = pl.BlockSpec((tn, tk), lambda i, j, k: (j, k)) if tb else pl.BlockSpec((tk, tn), lambda i, j, k: (k, j))
    if col_blocks is None:
        out_spec = pl.BlockSpec((tm, tn), lambda i, j, k: (i, j))
        out_shape = jax.ShapeDtypeStruct((M, N), out_dtype)
    else:
        per = (N // col_blocks) // tn
        assert per * tn * col_blocks == N, (N, col_blocks, tn)
        out_spec = pl.BlockSpec((None, tm, tn), lambda i, j, k: (j // per, i, j % per))
        out_shape = jax.ShapeDtypeStruct((col_blocks, M, N // col_blocks), out_dtype)
    return _pcall(
        body, name=name, grid=(nm, nn, nk), in_specs=[a_spec, b_spec], out_specs=out_spec, out_shape=out_shape,
        scratch_shapes=[pltpu.VMEM((tm, tn), F32)], compiler_params=_cparams(3),
    )(a, b)


def _norm_rows(rows):
    out = []
    for r in rows:
        if not isinstance(r, tuple):
            r = (r,)
        arr, off, width, roff = (r + (0, None, 0)[len(r) - 1:])
        out.append((arr, off, width if width is not None else arr.shape[1], roff))
    return out


def _rw_plan(T, rows, pars, seg_rows, col_tile, tm_cap):
    widths = [r[2] for r in rows]
    wmax = max(widths + [p.shape[-1] for p in pars] + [1])
    if col_tile is not None:
        assert all(w == widths[0] for w in widths) and all(p.shape[-1] == widths[0] for p in pars)
        ncol = widths[0] // col_tile
        assert ncol * col_tile == widths[0]
        wmax = col_tile
    else:
        ncol = 1
    cap = tm_cap if tm_cap is not None else max(SUBLANE, min(256, (256 * 1024) // wmax))
    tm = _row_tile(T, cap, also=tuple(seg_rows) + tuple(r[3] for r in rows if r[3]))
    bounds = tuple(s // tm for s in seg_rows)
    return widths, ncol, tm, bounds


def _rw_specs(rows, pars, ncol, tm, bounds, col_tile):
    def seg(i):
        s = 0
        for b in bounds:
            s = s + (i >= b).astype(jnp.int32)
        return s

    specs = []
    for arr, off, w, roff in rows:
        bw = col_tile if col_tile is not None else w
        assert off % bw == 0 and roff % tm == 0, (off, bw, roff, tm)
        specs.append(pl.BlockSpec((tm, bw), functools.partial(lambda j, i, ob, rb: (i + rb, ob + j),
                                                              ob=off // bw, rb=roff // tm)))
    for p in pars:
        bw = col_tile if col_tile is not None else p.shape[-1]
        if p.shape[0] > 1:
            specs.append(pl.BlockSpec((None, 1, bw), lambda j, i: (seg(i), 0, j)))
        else:
            specs.append(pl.BlockSpec((None, 1, bw), lambda j, i: (0, 0, j)))
    return specs, seg


def _rw_fwd(name, f, rows, pars, out_widths, *, T=None, seg_rows=(), col_tile=None, tm_cap=None, out_dtypes=None):
    rows = _norm_rows(rows)
    T = rows[0][0].shape[0] if T is None else T
    widths, ncol, tm, bounds = _rw_plan(T, rows, pars, seg_rows, col_tile, tm_cap)
    in_specs, _ = _rw_specs(rows, pars, ncol, tm, bounds, col_tile)
    nr, npar, nout = len(rows), len(pars), len(out_widths)

    def body(*refs):
        vals = [r[...] for r in refs[:nr + npar]]
        outs = f(*vals)
        if not isinstance(outs, (tuple, list)):
            outs = (outs,)
        for o_ref, o in zip(refs[nr + npar:], outs):
            o_ref[...] = o.astype(o_ref.dtype)

    out_specs = [pl.BlockSpec((tm, col_tile if col_tile is not None else w), lambda j, i: (i, j)) for w in out_widths]
    res = _pcall(
        body, name=name, grid=(ncol, T // tm), in_specs=in_specs, out_specs=out_specs,
        out_shape=[jax.ShapeDtypeStruct((T, w), dt) for w, dt in zip(out_widths, out_dtypes or [F32] * nout)],
        compiler_params=_cparams(2),
    )(*[r[0] for r in rows], *pars)
    return res if nout > 1 else res[0]


def _rw_bwd(name, f, rows, pars, cots, *, row_grad, par_grad, T=None, seg_rows=(), col_tile=None, tm_cap=None,
            add=None, cot_fn=None, row_dtypes=None):
    rows = _norm_rows(rows)
    cots = _norm_rows(cots)
    T = rows[0][0].shape[0] if T is None else T
    extra = _norm_rows([add]) if add is not None else []
    all_rows = rows + cots + extra
    widths, ncol, tm, bounds = _rw_plan(T, all_rows, pars, seg_rows, col_tile, tm_cap)
    in_specs, seg = _rw_specs(all_rows, pars, ncol, tm, bounds, col_tile)
    nr, nc, ne, npar = len(rows), len(cots), len(extra), len(pars)
    row_idx = [k for k in range(nr) if row_grad[k]]
    par_idx = [k for k in range(npar) if par_grad[k]]

    def body(*refs):
        i = pl.program_id(1)
        row_vals = [r[...] for r in refs[:nr]]
        cot_vals = [r[...] for r in refs[nr:nr + nc]]
        add_vals = [r[...] for r in refs[nr + nc:nr + nc + ne]]
        par_vals = [r[...] for r in refs[nr + nc + ne:nr + nc + ne + npar]]
        out_refs = refs[nr + nc + ne + npar:]
        outs, vjp = jax.vjp(f, *row_vals, *par_vals)
        if cot_fn is not None:
            cot_vals = cot_fn(*cot_vals)
            if not isinstance(cot_vals, (tuple, list)):
                cot_vals = (cot_vals,)
        if isinstance(outs, (tuple, list)):
            grads = vjp(tuple(c.astype(o.dtype) for c, o in zip(cot_vals, outs)))
        else:
            grads = vjp(cot_vals[0].astype(outs.dtype))
        first_seg = i == 0
        for b in bounds:
            first_seg = first_seg | (i == b)
        for n, k in enumerate(row_idx):
            g = grads[k]
            if n == 0 and add_vals:
                g = g + add_vals[0]
            out_refs[n][...] = g.astype(out_refs[n].dtype)
        for n, k in enumerate(par_idx):
            g = grads[nr + k]
            o_ref = out_refs[len(row_idx) + n]
            first = first_seg if pars[k].shape[0] > 1 else (i == 0)

            @pl.when(first)
            def _(o_ref=o_ref, g=g):
                o_ref[...] = g

            @pl.when(jnp.logical_not(first))
            def _(o_ref=o_ref, g=g):
                o_ref[...] += g

    out_specs, out_shape = [], []
    for k in row_idx:
        w = widths[k]
        out_specs.append(pl.BlockSpec((tm, col_tile if col_tile is not None else w), lambda j, i: (i, j)))
        out_shape.append(jax.ShapeDtypeStruct((T, w), row_dtypes[len(out_shape)] if row_dtypes else F32))
    for k in par_idx:
        p = pars[k]
        bw = col_tile if col_tile is not None else p.shape[-1]
        if p.shape[0] > 1:
            out_specs.append(pl.BlockSpec((None, 1, bw), lambda j, i: (seg(i), 0, j)))
        else:
            out_specs.append(pl.BlockSpec((None, 1, bw), lambda j, i: (0, 0, j)))
        out_shape.append(jax.ShapeDtypeStruct(p.shape, F32))
    res = _pcall(
        body, name=name, grid=(ncol, T // tm), in_specs=in_specs, out_specs=out_specs, out_shape=out_shape,
        compiler_params=_cparams(2),
    )(*[r[0] for r in all_rows], *pars)
    return list(res[:len(row_idx)]), list(res[len(row_idx):])


def _f_modnorm(h, w, sc, sh):
    y = h * lax.rsqrt(jnp.mean(h * h, axis=-1, keepdims=True) + EPS)
    return (y * w) * (1.0 + sc) + sh


def _f_gate_res(h, y, g):
    return h + g * y


def _f_gate_res_bias(h, y, g, b):
    return h + g * (y + b)


def _f_ffn_act(val, gate):
    return _silu(gate) * val


def _f_softplus(raw, bias):
    v = raw + bias
    return jnp.maximum(v, 0.0) + jnp.log(1.0 + jnp.exp(-jnp.abs(v)))


def _f_ssd_gate(yf, yb, xs, z, d_rep, nw):
    y = (yf + yb + d_rep * xs) * _silu(z)
    return (y * lax.rsqrt(jnp.mean(y * y, axis=-1, keepdims=True) + EPS)) * nw


def _f_glu(a, g, ba, bg):
    return (a + ba) * jax.nn.sigmoid(g + bg)


def _f_ln_silu(h, w, b):
    mu = jnp.mean(h, axis=-1, keepdims=True)
    d = h - mu
    y = d * lax.rsqrt(jnp.mean(d * d, axis=-1, keepdims=True) + EPS)
    return _silu(y * w + b)


def _f_loss_rows(h, t, w):
    y = (h * lax.rsqrt(jnp.mean(h * h, axis=-1, keepdims=True) + EPS)) * w
    e = y - t
    return 0.5 * jnp.mean(e * e, axis=-1, keepdims=True)


def _f_adamw(w, m, v, ga, gb):
    g = ga + gb
    m = ADAM_B1 * m + (1.0 - ADAM_B1) * g
    v = ADAM_B2 * v + (1.0 - ADAM_B2) * (g * g)
    m_hat = m / (1.0 - ADAM_B1 ** ADAM_STEP)
    v_hat = v / (1.0 - ADAM_B2 ** ADAM_STEP)
    delta = -ADAM_LR * (m_hat / (jnp.sqrt(v_hat) + ADAM_EPS) + ADAM_WD * w)
    return g, delta, m, v


def _adamw(name, w, m, v, ga, gb):
    shape = w.shape
    c = shape[-1]
    two_d = [t.reshape(-1, c) for t in (w, m, v, ga, gb)]
    rows = two_d[0].shape[0]
    pad = _round_up(rows, SUBLANE) - rows
    if pad:
        two_d = [jnp.pad(t, ((0, pad), (0, 0))) for t in two_d]
    outs = _rw_fwd(name, _f_adamw, two_d, [], [c] * 4)
    return tuple(o[:rows].reshape(shape) for o in outs)


def _sum_leading(name, x, idxs):
    _, R, C = x.shape
    tm = _row_tile(R, max(SUBLANE, min(512, (512 * 1024) // C)))

    def body(x_ref, o_ref):
        acc = x_ref[idxs[0]].astype(F32)
        for k in idxs[1:]:
            acc = acc + x_ref[k].astype(F32)
        o_ref[...] = acc

    return _pcall(
        body, name=name, grid=(R // tm,), in_specs=[pl.BlockSpec((x.shape[0], tm, C), lambda i: (0, i, 0))],
        out_specs=pl.BlockSpec((tm, C), lambda i: (i, 0)), out_shape=jax.ShapeDtypeStruct((R, C), F32),
        compiler_params=_cparams(1),
    )(x)


def _loss_fwd(h, t, w):
    T, D = h.shape
    tm = _row_tile(T, 256)

    def body(h_ref, t_ref, w_ref, o_ref):
        i = pl.program_id(0)
        part = jnp.sum(_f_loss_rows(h_ref[...], t_ref[...], w_ref[...]), axis=0, keepdims=True)
        part = jnp.broadcast_to(part, (1, LANE))

        @pl.when(i == 0)
        def _():
            o_ref[...] = part

        @pl.when(i > 0)
        def _():
            o_ref[...] += part

    return _pcall(
        body, name="loss_fwd", grid=(T // tm,),
        in_specs=[pl.BlockSpec((tm, D), lambda i: (i, 0)), pl.BlockSpec((tm, D), lambda i: (i, 0)),
                  pl.BlockSpec((1, D), lambda i: (0, 0))],
        out_specs=pl.BlockSpec((1, LANE), lambda i: (0, 0)), out_shape=jax.ShapeDtypeStruct((1, LANE), F32),
        compiler_params=_cparams(1),
    )(h, t, w)


CONV_ROWS = 256
CONV_ROWS_FEW_TAPS = 1024
CONV_ACC_ELEMS = 16384


def _col_mask(arg, t):
    col = jnp.bitwise_and(t, GRID_W - 1)
    return (col != 0) if arg < 0 else (col != GRID_W - 1)


def _conv_plan(T, C, taps):
    seg = [m[1] for _, m in taps if m is not None and m[0] == "seg"]
    boundary = seg[0] if seg else None
    cap = CONV_ROWS_FEW_TAPS if len(taps) <= 9 else CONV_ROWS
    rc = next(r for r in (1024, 768, 512, 256, LANE)
              if r <= cap and T % r == 0 and (boundary is None or boundary % r == 0))
    ct = next((t for t in (512, 256, LANE) if C % t == 0), C)
    reach = max(abs(s) for s, _ in taps)
    hb = next(h for h in (8, 16, 32, 64, 128, 256) if h >= reach and rc % h == 0)
    sub = max(2 * SUBLANE, min(rc, CONV_ACC_ELEMS // ct))
    taps = [(s, None if (m is None or m[0] == "seg") else m[1]) for s, m in taps]
    return rc, ct, hb, sub, T // rc, C // ct, boundary, taps


def _halo_specs(rc, ct, hb, T, off_blocks):
    per = rc // hb
    last = T // hb - 1
    prev = pl.BlockSpec((hb, ct), lambda j, i: (jnp.maximum(i * per - 1, 0), off_blocks + j))
    cur = pl.BlockSpec((rc, ct), lambda j, i: (i, off_blocks + j))
    nxt = pl.BlockSpec((hb, ct), lambda j, i: (jnp.minimum((i + 1) * per, last), off_blocks + j))
    return [prev, cur, nxt]


def _fill_halo(pad_ref, p_ref, c_ref, n_ref, i, nrc, rc, hb, boundary):
    has_prev = i > 0
    has_next = i < nrc - 1
    if boundary is not None:
        has_prev = has_prev & (i * rc != boundary)
        has_next = has_next & ((i + 1) * rc != boundary)
    pad_ref[0:hb, :] = jnp.where(has_prev, p_ref[...], 0.0)
    pad_ref[hb:hb + rc, :] = c_ref[...]
    pad_ref[hb + rc:hb + rc + hb, :] = jnp.where(has_next, n_ref[...], 0.0)


def _shift_plan(keys):
    count = {}
    for s, m in keys:
        k = (s % SUBLANE, m)
        count[k] = count.get(k, 0) + 1
    slots = {}
    for k, n in sorted(count.items(), key=lambda kv: (kv[0][0], str(kv[0][1]))):
        if k != (0, None) and (n >= 2 or k[1] is not None):
            slots[k] = len(slots)
    return slots


def _build_shifted(copies_ref, slots, pad_ref, keys, i, rc, hb, sub):
    for (r, m), slot in slots.items():
        qs = [s - r for s, mk in keys if (s % SUBLANE, mk) == (r, m)]
        lo, hi = hb + min(qs), hb + rc + max(qs)
        for p in range(lo, hi, sub):
            n = min(sub, hi - p)
            v = pad_ref[p + r:p + r + n, :]
            if m is not None:
                t = i * rc - hb + p + r + lax.broadcasted_iota(jnp.int32, (n, 1), 0)
                v = jnp.where(_col_mask(m, t), v, 0.0)
            copies_ref[slot, p:p + n, :] = v


def _read(copies_ref, slots, pad_ref, s, m, row, n):
    k = (s % SUBLANE, m)
    if k in slots:
        q = s - k[0]
        return copies_ref[slots[k], row + q:row + q + n, :]
    return pad_ref[row + s:row + s + n, :]


def _conv_fwd(name, u, col_off, C, w, b, taps, act=False):
    T = u.shape[0]
    rc, ct, hb, sub, nrc, ncc, boundary, taps = _conv_plan(T, C, taps)
    assert col_off % ct == 0
    K = len(taps)
    keys = [(s, None) for s, _ in taps]
    slots = _shift_plan(keys)
    dirs = sorted({m for _, m in taps if m is not None})

    def body(up, uc, un, w_ref, b_ref, *rest):
        y_ref = rest[0]
        pad_ref, copies_ref = rest[-2], rest[-1]
        i = pl.program_id(1)
        _fill_halo(pad_ref, up, uc, un, i, nrc, rc, hb, boundary)
        _build_shifted(copies_ref, slots, pad_ref, keys, i, rc, hb, sub)
        for r0 in range(0, rc, sub):
            acc = jnp.broadcast_to(b_ref[...], (sub, ct))
            for m in [None] + dirs:
                part = None
                for k, (s, mk) in enumerate(taps):
                    if mk != m:
                        continue
                    term = w_ref[k:k + 1, :] * _read(copies_ref, slots, pad_ref, s, None, hb + r0, sub)
                    part = term if part is None else part + term
                if part is None:
                    continue
                if m is not None:
                    t = i * rc + r0 + lax.broadcasted_iota(jnp.int32, (sub, 1), 0)
                    part = jnp.where(_col_mask(m, t), part, 0.0)
                acc = acc + part
            y_ref[r0:r0 + sub, :] = acc
            if act:
                rest[1][r0:r0 + sub, :] = _silu(acc)

    n_out = 2 if act else 1
    res = _pcall(
        body, name=name, grid=(ncc, nrc),
        in_specs=_halo_specs(rc, ct, hb, T, col_off // ct) + [pl.BlockSpec((K, ct), lambda j, i: (0, j)),
                                                              pl.BlockSpec((1, ct), lambda j, i: (0, j))],
        out_specs=[pl.BlockSpec((rc, ct), lambda j, i: (i, j))] * n_out,
        out_shape=[jax.ShapeDtypeStruct((T, C), F32)] * n_out,
        scratch_shapes=[pltpu.VMEM((rc + 2 * hb, ct), F32), pltpu.VMEM((max(len(slots), 1), rc + 2 * hb, ct), F32)],
        compiler_params=_cparams(2),
    )(u, u, u, w, b)
    return res if act else res[0]


def _conv_bwd(name, u, col_off, C, w, g, taps, du_dtype=F32):
    T = u.shape[0]
    rc, ct, hb, sub, nrc, ncc, boundary, taps = _conv_plan(T, C, taps)
    K = len(taps)
    u_keys = [(s, None) for s, _ in taps]
    dirs = sorted({m for _, m in taps if m is not None})
    g_keys = [(-s, m) for s, m in taps] + [(0, m) for m in dirs]
    u_slots, g_slots = _shift_plan(u_keys), _shift_plan(g_keys)

    def body(up, uc, un, gp, gc, gn, w_ref, du_ref, dw_ref, db_ref, upad, gpad, ucopies, gcopies):
        i = pl.program_id(1)
        _fill_halo(upad, up, uc, un, i, nrc, rc, hb, boundary)
        _fill_halo(gpad, gp, gc, gn, i, nrc, rc, hb, boundary)
        _build_shifted(ucopies, u_slots, upad, u_keys, i, rc, hb, sub)
        _build_shifted(gcopies, g_slots, gpad, g_keys, i, rc, hb, sub)

        @pl.when(i == 0)
        def _():
            dw_ref[...] = jnp.zeros_like(dw_ref)
            db_ref[...] = jnp.zeros_like(db_ref)

        def fold(v):
            return jnp.sum(v.reshape(sub // SUBLANE, SUBLANE, ct), axis=0)

        dbs = jnp.zeros((SUBLANE, ct), F32)
        for r0 in range(0, rc, sub):
            dbs = dbs + fold(gpad[hb + r0:hb + r0 + sub, :])
            acc = jnp.zeros((sub, ct), F32)
            for k, (s, m) in enumerate(taps):
                acc = acc + w_ref[k:k + 1, :] * _read(gcopies, g_slots, gpad, -s, m, hb + r0, sub)
            du_ref[r0:r0 + sub, :] = acc.astype(du_ref.dtype)
        db_ref[...] += jnp.sum(dbs, axis=0, keepdims=True)
        for k, (s, m) in enumerate(taps):
            part = jnp.zeros((SUBLANE, ct), F32)
            for r0 in range(0, rc, sub):
                part = part + fold(_read(gcopies, g_slots, gpad, 0, m, hb + r0, sub)
                                   * _read(ucopies, u_slots, upad, s, None, hb + r0, sub))
            dw_ref[k:k + 1, :] += jnp.sum(part, axis=0, keepdims=True)

    halo_u = _halo_specs(rc, ct, hb, T, col_off // ct)
    halo_g = _halo_specs(rc, ct, hb, T, 0)
    rows = rc + 2 * hb
    return _pcall(
        body, name=name, grid=(ncc, nrc),
        in_specs=halo_u + halo_g + [pl.BlockSpec((K, ct), lambda j, i: (0, j))],
        out_specs=[pl.BlockSpec((rc, ct), lambda j, i: (i, j)), pl.BlockSpec((K, ct), lambda j, i: (0, j)),
                   pl.BlockSpec((1, ct), lambda j, i: (0, j))],
        out_shape=[jax.ShapeDtypeStruct((T, C), du_dtype), jax.ShapeDtypeStruct((K, C), F32),
                   jax.ShapeDtypeStruct((1, C), F32)],
        scratch_shapes=[pltpu.VMEM((rows, ct), F32), pltpu.VMEM((rows, ct), F32),
                        pltpu.VMEM((max(len(u_slots), 1), rows, ct), F32),
                        pltpu.VMEM((max(len(g_slots), 1), rows, ct), F32)],
        compiler_params=_cparams(2),
    )(u, u, u, g, g, g, w)


def _ssd_group(xg, bm, cm, s_in, *per_head, reverse, P):
    R = len(per_head) // 2
    dtrs, a_s = per_head[:R], per_head[R:]
    q, rp = xg.shape
    ii = lax.broadcasted_iota(jnp.int32, (q, q), 0)
    jj = lax.broadcasted_iota(jnp.int32, (q, q), 1)
    causal = (jj >= ii) if reverse else (jj <= ii)
    causal_t = (ii >= jj) if reverse else (ii <= jj)
    eye = ii == jj
    lane = lax.broadcasted_iota(jnp.int32, (1, rp), 1)
    row = lax.broadcasted_iota(jnp.int32, (rp, 1), 0)
    nt = (((1,), (1,)), ((), ()))
    tn = (((0,), (0,)), ((), ()))
    cb = lax.dot_general(cm.astype(BF16), bm.astype(BF16), nt, preferred_element_type=F32)
    dt_x = jnp.zeros((q, rp), F32)
    acum_x = jnp.zeros((q, rp), F32)
    tot_row = jnp.zeros((1, rp), F32)
    tot_col = jnp.zeros((rp, 1), F32)
    wts, lane_masks = [], []
    for r in range(R):
        hm = (lane >= r * P) & (lane < (r + 1) * P)
        hc = (row >= r * P) & (row < (r + 1) * P)
        dt_c = jnp.sum(jnp.where(eye, dtrs[r], 0.0), axis=1, keepdims=True)
        dac = dt_c * a_s[r]
        dar = dtrs[r] * a_s[r]
        acum_c = jnp.sum(jnp.where(causal, dar, 0.0), axis=1, keepdims=True)
        acum_r = jnp.sum(jnp.where(causal_t, dac, 0.0), axis=0, keepdims=True)
        decay = jnp.where(causal, jnp.exp(jnp.where(causal, acum_c - acum_r, 0.0)), 0.0)
        tot = jnp.sum(dac, axis=0, keepdims=True)
        dt_x = jnp.where(hm, dt_c, dt_x)
        acum_x = jnp.where(hm, acum_c, acum_x)
        tot_row = jnp.where(hm, tot, tot_row)
        tot_col = jnp.where(hc, tot, tot_col)
        wts.append((cb * decay).astype(BF16))
        lane_masks.append(hm)
    xdt = xg * dt_x
    xdt_b = xdt.astype(BF16)
    y = jnp.zeros((q, rp), F32)
    for r in range(R):
        y = jnp.where(lane_masks[r], jnp.dot(wts[r], xdt_b, preferred_element_type=F32), y)
    dte = jnp.exp(tot_row - acum_x)
    cs = lax.dot_general((xdt * dte).astype(BF16), bm.astype(BF16), tn, preferred_element_type=F32)
    y = y + lax.dot_general(cm.astype(BF16), s_in.astype(BF16), nt, preferred_element_type=F32) * jnp.exp(acum_x)
    s_out = jnp.exp(tot_col) * s_in + cs
    return y, s_out


def _ssd_maps(NC, ncc, reverse_steps):
    def chunk(d, s):
        if reverse_steps:
            s = NC - 1 - s
        return s if d == 0 else jnp.where(s < ncc, ncc - 1 - s, NC - 1 - s + ncc)

    def lat_chunk(d, s):
        c = chunk(d, s) - ncc
        return jnp.where(c < 0, 0 if d == 0 else NC - ncc - 1, c)

    def step(s):
        return NC - 1 - s if reverse_steps else s

    return chunk, lat_chunk, step


def _ssd_specs(chunk, d, R, Q, N, RP, bo, co):
    return [
        pl.BlockSpec((Q, RP), lambda g, s: (chunk(d, s), g)),
        pl.BlockSpec((Q, N), lambda g, s: (chunk(d, s), bo + g)),
        pl.BlockSpec((Q, N), lambda g, s: (chunk(d, s), co + g)),
        pl.BlockSpec((R, 1, Q), lambda g, s: (g, 0, chunk(d, s))),
        pl.BlockSpec((R, 1, 1), lambda g, s: (g, 0, 0)),
    ]


def _ssd_fwd(xbc, b_off, c_off, dtr, a, P, ncc):
    T = xbc.shape[0]
    H = dtr[0].shape[0]
    N, Q = SSD_STATE, SSD_CHUNK
    NC = T // Q
    G = (c_off - b_off) // N
    R = H // G
    RP = R * P
    chunk, lat_chunk, _ = _ssd_maps(NC, ncc, False)

    def body(*refs):
        s = pl.program_id(1)
        s_ref = refs[-1]

        @pl.when(s == 0)
        def _():
            s_ref[...] = jnp.zeros_like(s_ref)

        for d in range(2):
            x_ref, b_ref, c_ref, dtr_ref, a_ref = refs[5 * d:5 * d + 5]
            y_ref, se_ref = refs[10 + 2 * d:12 + 2 * d]
            s_in = s_ref[d]
            se_ref[...] = s_in
            per_head = [dtr_ref[r] for r in range(R)] + [a_ref[r] for r in range(R)]
            y, s_out = _ssd_group(x_ref[...], b_ref[...], c_ref[...], s_in, *per_head, reverse=d == 1, P=P)
            y_ref[...] = y
            s_ref[d] = s_out

    in_specs, out_specs, out_shape, operands = [], [], [], []
    for d in range(2):
        in_specs += _ssd_specs(chunk, d, R, Q, N, RP, b_off // N, c_off // N)
        operands += [xbc, xbc, xbc, dtr[d], a[d]]
        out_specs += [pl.BlockSpec((Q, RP), functools.partial(lambda g, s, d: (lat_chunk(d, s), g), d=d)),
                      pl.BlockSpec((None, None, RP, N), lambda g, s: (g, s, 0, 0))]
        out_shape += [jax.ShapeDtypeStruct((T - ncc * Q, H * P), F32), jax.ShapeDtypeStruct((G, NC, RP, N), F32)]
    y_f, se_f, y_b, se_b = _pcall(
        body, name="ssd_fwd", grid=(G, NC), in_specs=in_specs, out_specs=out_specs, out_shape=out_shape,
        scratch_shapes=[pltpu.VMEM((2, RP, N), F32)], compiler_params=_cparams(2),
    )(*operands)
    return (y_f, y_b), (se_f, se_b)


def _ssd_bwd(xbc, b_off, c_off, dtr, a, s_enter, dy, P, ncc):
    T = xbc.shape[0]
    H = dtr[0].shape[0]
    N, Q = SSD_STATE, SSD_CHUNK
    NC = T // Q
    G = (c_off - b_off) // N
    R = H // G
    RP = R * P
    chunk, lat_chunk, step = _ssd_maps(NC, ncc, True)
    n_in, n_out = 7, 5

    def body(*refs):
        s = pl.program_id(1)
        ds_ref = refs[-1]

        @pl.when(s == 0)
        def _():
            ds_ref[...] = jnp.zeros_like(ds_ref)

        for d in range(2):
            x_ref, b_ref, c_ref, dtr_ref, a_ref, se_ref, dy_ref = refs[n_in * d:n_in * (d + 1)]
            dx_ref, db_ref, dc_ref, ddtr_ref, da_ref = refs[2 * n_in + n_out * d:2 * n_in + n_out * (d + 1)]
            per_head = [dtr_ref[r] for r in range(R)] + [a_ref[r] for r in range(R)]
            f = functools.partial(_ssd_group, reverse=d == 1, P=P)
            _, vjp = jax.vjp(f, x_ref[...], b_ref[...], c_ref[...], se_ref[...], *per_head)
            is_latent = chunk(d, s) >= ncc
            dy_v = jnp.where(is_latent, dy_ref[...], 0.0)
            grads = vjp((dy_v, ds_ref[d]))
            dx_ref[...] = grads[0]
            db_ref[...] = grads[1]
            dc_ref[...] = grads[2]
            ds_ref[d] = grads[3]
            for r in range(R):
                ddtr_ref[r] = grads[4 + r]
                da_ref[r] = jnp.broadcast_to(grads[4 + R + r], (SUBLANE, LANE))

    in_specs, out_specs, out_shape, operands = [], [], [], []
    for d in range(2):
        in_specs += _ssd_specs(chunk, d, R, Q, N, RP, b_off // N, c_off // N) + [
            pl.BlockSpec((None, None, RP, N), lambda g, s: (g, step(s), 0, 0)),
            pl.BlockSpec((Q, RP), functools.partial(lambda g, s, d: (lat_chunk(d, s), g), d=d)),
        ]
        operands += [xbc, xbc, xbc, dtr[d], a[d], s_enter[d], dy]
    for d in range(2):
        at_chunk = functools.partial(lambda g, s, d: (chunk(d, s), g), d=d)
        out_specs += [
            pl.BlockSpec((Q, RP), at_chunk), pl.BlockSpec((Q, N), at_chunk), pl.BlockSpec((Q, N), at_chunk),
            pl.BlockSpec((R, 1, Q), functools.partial(lambda g, s, d: (g, 0, chunk(d, s)), d=d)),
            pl.BlockSpec((R, SUBLANE, LANE), lambda g, s: (g * NC + s, 0, 0)),
        ]
        out_shape += [
            jax.ShapeDtypeStruct((T, H * P), F32), jax.ShapeDtypeStruct((T, G * N), F32),
            jax.ShapeDtypeStruct((T, G * N), F32), jax.ShapeDtypeStruct((H, 1, T), F32),
            jax.ShapeDtypeStruct((G * NC * R, SUBLANE, LANE), F32),
        ]
    res = _pcall(
        body, name="ssd_bwd", grid=(G, NC), in_specs=in_specs, out_specs=out_specs, out_shape=out_shape,
        scratch_shapes=[pltpu.VMEM((2, RP, N), F32)], compiler_params=_cparams(2),
    )(*operands)
    return res[:n_out], res[n_out:]


def _allgather8(name, v):
    R, C = v.shape

    def body(x_ref, out_ref, send_sems, recv_sems, local_sem):
        x, y, c = lax.axis_index("x"), lax.axis_index("y"), lax.axis_index("c")
        me, sibling = (x, y, c), (x, y, 1 - c)
        chips = [(1 - x, y), (x, 1 - y), (1 - x, 1 - y)]

        def slot(px, py, pc):
            return out_ref.at[4 * px + 2 * py + pc]

        def copy(k, block, to, src=None):
            return pltpu.make_async_remote_copy(
                src_ref=slot(*block) if src is None else src, dst_ref=slot(*block),
                send_sem=send_sems.at[k], recv_sem=recv_sems.at[k], device_id=to, device_id_type=MESH)

        mine = pltpu.make_async_copy(x_ref, slot(*me), local_sem)
        mine.start()
        first = [copy(0, me, sibling, src=x_ref)]
        first += [copy(1 + j, me, (*chip, c), src=x_ref) for j, chip in enumerate(chips)]
        for cp in first:
            cp.start()
        passed = [copy(4 + j, (*chip, c), sibling) for j, chip in enumerate(chips)]
        for j, chip in enumerate(chips):
            copy(1 + j, (*chip, c), me).wait_recv()
            passed[j].start()
        copy(0, sibling, me).wait_recv()
        for j, chip in enumerate(chips):
            copy(4 + j, (*chip, 1 - c), me).wait_recv()
        for cp in first + passed:
            cp.wait_send()
        mine.wait()

    return _pcall(
        body, name=name, out_shape=jax.ShapeDtypeStruct((N_DEV, R, C), v.dtype),
        in_specs=[pl.BlockSpec(memory_space=pltpu.VMEM)], out_specs=pl.BlockSpec(memory_space=pltpu.VMEM),
        scratch_shapes=[pltpu.SemaphoreType.DMA((7,)), pltpu.SemaphoreType.DMA((7,)), pltpu.SemaphoreType.DMA],
        compiler_params=pltpu.CompilerParams(vmem_limit_bytes=VMEM_LIMIT_BYTES),
    )(v)


def _exchange4_start(name, srcs, bcast, dep):
    n = len(srcs)
    lands = [lax.empty(((N_CHIPS,) + s.shape) if bcast else s.shape, s.dtype) for s in srcs]

    def body(*refs):
        src, land = refs[:n], refs[n:2 * n]
        send_sems, recv_sems = refs[2 * n + 1], refs[2 * n + 2]
        token = refs[-1]
        x, y, c = lax.axis_index("x"), lax.axis_index("y"), lax.axis_index("c")
        me = 2 * x + y
        for a in range(n):
            for j, (px, py) in enumerate([(1 - x, y), (x, 1 - y), (1 - x, 1 - y)]):
                pltpu.make_async_remote_copy(
                    src_ref=src[a] if bcast else src[a].at[2 * px + py], dst_ref=land[a].at[me],
                    send_sem=send_sems.at[3 * a + j], recv_sem=recv_sems.at[3 * a + j], device_id=(px, py, c),
                    device_id_type=MESH).start()
        token[...] = jnp.zeros_like(token)

    hbm = pl.BlockSpec(memory_space=pltpu.HBM)
    sem = pl.BlockSpec(memory_space=pltpu.SEMAPHORE)
    outs = _pcall(
        body, name=name,
        out_shape=(pltpu.SemaphoreType.DMA((3 * n,)), pltpu.SemaphoreType.DMA((3 * n,)),
                   *[pltpu.HBM(s.shape, s.dtype) for s in srcs], *[pltpu.HBM(l.shape, l.dtype) for l in lands],
                   jax.ShapeDtypeStruct((SUBLANE, LANE), F32)),
        in_specs=[hbm] * (2 * n) + [pl.BlockSpec(memory_space=pl.ANY)],
        out_specs=(sem, sem, *[hbm] * (2 * n), pl.BlockSpec(memory_space=pltpu.VMEM)),
        input_output_aliases={k: 2 + k for k in range(2 * n)},
        compiler_params=pltpu.CompilerParams(has_side_effects=pltpu.SideEffectType.DATAFLOW_SIDE_EFFECTING),
    )(*[pltpu.with_memory_space_constraint(s, pltpu.HBM) for s in srcs],
      *[pltpu.with_memory_space_constraint(l, pltpu.HBM) for l in lands], dep)
    return (n, bcast, outs[0], outs[1], outs[2:2 + n], outs[2 + n:2 + 2 * n]), outs[-1]


def _exchange4_wait(name, handle, after):
    n, bcast, send_sems, recv_sems, src_thru, land_thru = handle

    def body(*refs):
        src, land = refs[:n], refs[n:2 * n]
        send_sems, recv_sems = refs[2 * n], refs[2 * n + 1]
        x, y, c = lax.axis_index("x"), lax.axis_index("y"), lax.axis_index("c")
        for a in range(n):
            for j, (px, py) in enumerate([(1 - x, y), (x, 1 - y), (1 - x, 1 - y)]):
                pk = 2 * px + py
                copy = pltpu.make_async_remote_copy(
                    src_ref=src[a] if bcast else src[a].at[pk], dst_ref=land[a].at[pk],
                    send_sem=send_sems.at[3 * a + j], recv_sem=recv_sems.at[3 * a + j], device_id=(px, py, c),
                    device_id_type=MESH)
                copy.wait_send()
                copy.wait_recv()

    hbm = pl.BlockSpec(memory_space=pltpu.HBM)
    sem = pl.BlockSpec(memory_space=pltpu.SEMAPHORE)
    outs = _pcall(
        body, name=name,
        out_shape=tuple(pltpu.HBM(t.shape, t.dtype) for t in (*src_thru, *land_thru)),
        in_specs=[hbm] * (2 * n) + [sem, sem, pl.BlockSpec(memory_space=pl.ANY)], out_specs=tuple([hbm] * (2 * n)),
        input_output_aliases={k: k for k in range(2 * n)},
        compiler_params=pltpu.CompilerParams(has_side_effects=pltpu.SideEffectType.DATAFLOW_SIDE_EFFECTING),
    )(*src_thru, *land_thru, send_sems, recv_sems, after)
    return list(outs[n:])


def _tie(name, v, token):
    def body(v_ref, token_ref, o_ref):
        del v_ref, token_ref, o_ref

    any_spec = pl.BlockSpec(memory_space=pl.ANY)
    return _pcall(body, name=name, out_shape=jax.ShapeDtypeStruct(v.shape, v.dtype), in_specs=[any_spec, any_spec],
                  out_specs=any_spec, input_output_aliases={0: 0})(v, token)


def _fill_own(landed, own, me, bcast):
    blk = own if bcast else lax.dynamic_index_in_dim(own, me, 0, keepdims=False)
    return lax.dynamic_update_index_in_dim(landed, blk, me, 0)


def _swap_sibling(name, srcs):
    n = len(srcs)

    def body(*refs):
        src, out = refs[:n], refs[n:2 * n]
        send_sems, recv_sems = refs[2 * n:]
        x, y, c = lax.axis_index("x"), lax.axis_index("y"), lax.axis_index("c")
        copies = []
        for a in range(n):
            rc = pltpu.make_async_remote_copy(
                src_ref=src[a], dst_ref=out[a], send_sem=send_sems.at[a], recv_sem=recv_sems.at[a],
                device_id=(x, y, 1 - c), device_id_type=MESH)
            rc.start()
            copies.append(rc)
        for cp in copies:
            cp.wait()

    any_spec = pl.BlockSpec(memory_space=pl.ANY)
    return _pcall(
        body, name=name, out_shape=[jax.ShapeDtypeStruct(s.shape, s.dtype) for s in srcs],
        in_specs=[any_spec] * n, out_specs=[any_spec] * n,
        scratch_shapes=[pltpu.SemaphoreType.DMA((n,)), pltpu.SemaphoreType.DMA((n,))],
    )(*srcs)


def _mod_fwd(c16, mod_w, mod_b_shard):
    nl, D, S = mod_w.shape

    def body(c_ref, w_ref, b_ref, o_ref):
        s = _silu(c_ref[...]).astype(BF16)
        o_ref[...] = jnp.dot(s, w_ref[...].astype(BF16), preferred_element_type=F32) + b_ref[...]

    return _pcall(
        body, name="mod_fwd", grid=(nl,),
        in_specs=[pl.BlockSpec((16, D), lambda l: (0, 0)), pl.BlockSpec((None, D, S), lambda l: (l, 0, 0)),
                  pl.BlockSpec((None, 1, S), lambda l: (l, 0, 0))],
        out_specs=pl.BlockSpec((None, 16, S), lambda l: (l, 0, 0)),
        out_shape=jax.ShapeDtypeStruct((nl, 16, S), F32), compiler_params=_cparams(1),
    )(c16, mod_w, mod_b_shard)


def _mod_w_update(s16t, dm16, w, m, v):
    nl, D, S = w.shape
    tm = _row_tile(D, 256)

    def body(s_ref, dm_ref, w_ref, m_ref, v_ref, g_ref, dl_ref, nm_ref, nv_ref):
        g = jnp.dot(s_ref[...], dm_ref[...], preferred_element_type=F32, precision=HIGHEST)
        g, dl, nm, nv = _f_adamw(w_ref[...], m_ref[...], v_ref[...], g, jnp.zeros_like(g))
        g_ref[...] = g
        dl_ref[...] = dl
        nm_ref[...] = nm
        nv_ref[...] = nv

    big = pl.BlockSpec((None, tm, S), lambda l, i: (l, i, 0))
    return _pcall(
        body, name="mod_w_update", grid=(nl, D // tm),
        in_specs=[pl.BlockSpec((tm, 16), lambda l, i: (i, 0)), pl.BlockSpec((None, 16, S), lambda l, i: (l, 0, 0)),
                  big, big, big],
        out_specs=[big] * 4, out_shape=[jax.ShapeDtypeStruct(w.shape, F32)] * 4, compiler_params=_cparams(2),
    )(s16t, dm16, w, m, v)


def _pack(arrs):
    flat = jnp.concatenate([a.reshape(-1).astype(F32) for a in arrs])
    n = flat.shape[0]
    rows = _round_up(_cdiv(n, LANE), SUBLANE)
    return jnp.pad(flat, (0, rows * LANE - n)).reshape(rows, LANE)


def _unpack(buf, shapes):
    flat = buf.reshape(-1)
    out, pos = [], 0
    for s in shapes:
        n = 1
        for d in s:
            n *= d
        out.append(flat[pos:pos + n].reshape(s))
        pos += n
    return out


SHARD_AXIS = {
    "mod_w": 2, "ssd_w_in": 2, "ssd_conv_w": 2, "ssd_w_out": 1, "conf_w_pw1": 2, "conf_b_pw1": 1, "conf_w_dw": 2,
    "conf_b_dw": 1, "conf_ln_w": 1, "conf_ln_b": 1, "conf_w_pw2": 1, "conf_b_pw2": 1, "ffn_w_up": 2,
    "ffn_conv_w": 3, "ffn_w_down": 1,
}
BIG = ("ssd_w_in", "ssd_w_out", "conf_w_pw1", "conf_w_pw2", "ffn_w_up", "ffn_w_down")
WEIGHTS = ("c_ctx", "mod_w", "mod_b", "norm1_w", "norm2_w", "ssd_w_in", "ssd_conv_w", "ssd_conv_b", "ssd_dt_bias",
           "ssd_a_log", "ssd_d", "ssd_norm_w", "ssd_w_out", "conf_w_pw1", "conf_b_pw1", "conf_w_dw", "conf_b_dw",
           "conf_ln_w", "conf_ln_b", "conf_w_pw2", "conf_b_pw2", "ffn_w_up", "ffn_conv_w", "ffn_conv_b",
           "ffn_w_down", "final_norm_w")
SMALL = tuple(n for n in WEIGHTS if n not in BIG and n != "mod_w")
SMALL_SHARDED = tuple(n for n in SMALL if n in SHARD_AXIS)


def _unshard(stacked, axis):
    return jnp.concatenate([stacked[k] for k in range(N_CHIPS)], axis=axis)


def _to_blocks(full, axis):
    return jnp.stack(jnp.split(full, N_CHIPS, axis=axis))


def _par(v):
    v = v.reshape(-1, v.shape[-1])
    return v[:, None, :]


def kernel(x, c, ctx, c_ctx, mod_w, mod_b, norm1_w, norm2_w, ssd_w_in, ssd_conv_w, ssd_conv_b, ssd_dt_bias, ssd_a_log, ssd_d, ssd_norm_w, ssd_w_out, conf_w_pw1, conf_b_pw1, conf_w_dw, conf_b_dw, conf_ln_w, conf_ln_b, conf_w_pw2, conf_b_pw2, ffn_w_up, ffn_conv_w, ffn_conv_b, ffn_w_down, final_norm_w, loss_target, m_c_ctx, m_mod_w, m_mod_b, m_norm1_w, m_norm2_w, m_ssd_w_in, m_ssd_conv_w, m_ssd_conv_b, m_ssd_dt_bias, m_ssd_a_log, m_ssd_d, m_ssd_norm_w, m_ssd_w_out, m_conf_w_pw1, m_conf_b_pw1, m_conf_w_dw, m_conf_b_dw, m_conf_ln_w, m_conf_ln_b, m_conf_w_pw2, m_conf_b_pw2, m_ffn_w_up, m_ffn_conv_w, m_ffn_conv_b, m_ffn_w_down, m_final_norm_w, v_c_ctx, v_mod_w, v_mod_b, v_norm1_w, v_norm2_w, v_ssd_w_in, v_ssd_conv_w, v_ssd_conv_b, v_ssd_dt_bias, v_ssd_a_log, v_ssd_d, v_ssd_norm_w, v_ssd_w_out, v_conf_w_pw1, v_conf_b_pw1, v_conf_w_dw, v_conf_b_dw, v_conf_ln_w, v_conf_ln_b, v_conf_w_pw2, v_conf_b_pw2, v_ffn_w_up, v_ffn_conv_w, v_ffn_conv_b, v_ffn_w_down, v_final_norm_w):
    given = dict(locals())
    W = {n: given[n] for n in WEIGHTS}
    Mo = {n: given["m_" + n] for n in WEIGHTS}
    Vo = {n: given["v_" + n] for n in WEIGHTS}

    ax, ay, ac = lax.axis_index("x"), lax.axis_index("y"), lax.axis_index("c")
    chip = 2 * ax + ay
    dev = 4 * ax + 2 * ay + ac

    D = x.shape[-1]
    L, Lc = x.shape[1], ctx.shape[1]
    T0 = L + Lc
    H = ssd_a_log.shape[-1]
    DI = ssd_norm_w.shape[-1]
    P = DI // H
    CD = ssd_conv_b.shape[-1]
    N = SSD_STATE
    G = (CD - DI) // (2 * N)
    FH = ffn_conv_b.shape[-1]
    KS = ssd_conv_w.shape[1]
    KC = conf_w_dw.shape[1]
    ncc = Lc // SSD_CHUNK

    shard_b = {n: W[n].astype(BF16) for n in BIG}
    gather_a, token = _exchange4_start("gather_w_in_start", [shard_b["ssd_w_in"]], True, x)
    c = _tie("tie_gather_w_in", c, token)

    small_shard_shapes = [W[n].shape for n in SMALL_SHARDED]
    f1 = _allgather8("gather_small", _pack([c] + [W[n] for n in SMALL_SHARDED]))
    c_rows, full_small = [], {n: [] for n in SMALL_SHARDED}
    for k in range(N_DEV):
        parts = _unpack(f1[k], [c.shape] + small_shard_shapes)
        c_rows.append(parts[0])
        if k % 2 == 0:
            for n, p in zip(SMALL_SHARDED, parts[1:]):
                full_small[n].append(p)
    Wf = dict(W)
    for n in SMALL_SHARDED:
        Wf[n] = jnp.concatenate(full_small[n], axis=SHARD_AXIS[n])
    c16 = jnp.concatenate(c_rows + [c_ctx[None, :], jnp.zeros((16 - N_DEV - 1, D), F32)], axis=0)

    S_mod = mod_w.shape[-1]
    mod_b_shard = lax.dynamic_slice_in_dim(mod_b, chip * S_mod, S_mod, axis=1)[:, None, :]
    mod_part = _mod_fwd(c16, mod_w, mod_b_shard)
    f2 = _allgather8("gather_mod", mod_part.reshape(2 * 16, S_mod))
    mods = jnp.concatenate([f2[2 * k].reshape(2, 16, S_mod) for k in range(N_CHIPS)], axis=-1)
    my = lax.dynamic_slice_in_dim(mods, dev, 1, axis=1)[:, 0]
    sh1, sc1, g1, sh2, sc2, g2 = [[my[l, k * D:(k + 1) * D] for l in range(2)] for k in range(6)]
    csh1, csc1 = mods[0, N_DEV, 0:D], mods[0, N_DEV, D:2 * D]

    def full_weight(n, landed):
        return _unshard(_fill_own(landed, shard_b[n], chip, True), SHARD_AXIS[n])

    xl = x[0]
    hcat = jnp.concatenate([ctx[0], xl], axis=0)
    n1w0, n2w0, n1w1, n2w1 = _par(norm1_w[0]), _par(norm2_w[0]), _par(norm1_w[1]), _par(norm2_w[1])
    sc_seg = jnp.stack([csc1, sc1[0]])[:, None, :]
    sh_seg = jnp.stack([csh1, sh1[0]])[:, None, :]

    a0 = _rw_fwd("l0_modnorm1", _f_modnorm, [hcat], [n1w0, sc_seg, sh_seg], [D], seg_rows=(Lc,), out_dtypes=[BF16])
    (landed_in,) = _exchange4_wait("gather_w_in_wait", gather_a, a0)
    w_in = full_weight("ssd_w_in", landed_in)[0]
    rest = [n for n in BIG if n != "ssd_w_in"]
    gather_b, token = _exchange4_start("gather_rest_start", [shard_b[n] for n in rest], True, landed_in)
    a0 = _tie("tie_gather_rest", a0, token)
    proj = _mm(a0, w_in, name="l0_w_in")
    seg_taps = [(k - KS // 2, ("seg", Lc)) for k in range(KS)]
    xbc_pre, xbc = _conv_fwd("l0_conv", proj, DI, CD, Wf["ssd_conv_w"][0], ssd_conv_b, seg_taps, act=True)
    dt_raw = proj[:, DI + CD:]
    dt_bias = _par(ssd_dt_bias.reshape(1, 2 * H))
    dt = _rw_fwd("l0_softplus", _f_softplus, [dt_raw], [dt_bias], [2 * H])
    dt_t = dt.T
    dtr = (dt_t[:H, None, :], dt_t[H:, None, :])
    a_all = -jnp.exp(ssd_a_log.reshape(2, H, 1, 1))
    a_neg = (a_all[0], a_all[1])
    (y_f, y_b), s_enter = _ssd_fwd(xbc, DI, DI + G * N, dtr, a_neg, P, ncc)
    gate_rows = [y_f, y_b, (xbc, 0, DI, Lc), (proj, 0, DI, Lc)]
    d_rep = _par(jnp.repeat(ssd_d[0], P))
    ssd_nw = _par(ssd_norm_w[0])
    yn = _rw_fwd("l0_ssd_gate", _f_ssd_gate, gate_rows, [d_rep, ssd_nw], [DI], T=L, out_dtypes=[BF16])
    Wb = {n: full_weight(n, g) for n, g in zip(rest, _exchange4_wait("gather_rest_wait", gather_b, yn))}
    w_out, w_pw1, w_pw2 = Wb["ssd_w_out"][0], Wb["conf_w_pw1"][0], Wb["conf_w_pw2"][0]
    w_up, w_dn = Wb["ffn_w_up"], Wb["ffn_w_down"]
    mix0 = _mm(yn, w_out, name="l0_w_out")
    g1_0, g2_0, g1_1, g2_1 = _par(g1[0]), _par(g2[0]), _par(g1[1]), _par(g2[1])
    h1 = _rw_fwd("l0_res1", _f_gate_res, [xl, mix0], [g1_0], [D])

    grid_taps = [((i - 1) * GRID_W + (j - 1), (None if j == 1 else ("col", j - 1))) for i in range(3) for j in range(3)]

    def ffn_fwd(l, h, tag):
        a = _rw_fwd(tag + "_modnorm2", _f_modnorm, [h], [_par(norm2_w[l]), _par(sc2[l]), _par(sh2[l])], [D],
                    out_dtypes=[BF16])
        hh = _mm(a, w_up[l], name=tag + "_w_up")
        gc = _conv_fwd(tag + "_ffn_conv", hh, FH, FH, Wf["ffn_conv_w"][l].reshape(9, FH), ffn_conv_b[l][None, :],
                       grid_taps)
        act = _rw_fwd(tag + "_act", _f_ffn_act, [(hh, 0, FH), gc], [], [FH], col_tile=_tile(FH, 1536),
                      out_dtypes=[BF16])
        dn = _mm(act, w_dn[l], name=tag + "_w_down")
        return a, hh, gc, act, dn

    a1, hh0, gc0, act0, dn0 = ffn_fwd(0, h1, "l0")
    h2 = _rw_fwd("l0_res2", _f_gate_res, [h1, dn0], [g2_0], [D])

    a2 = _rw_fwd("l1_modnorm1", _f_modnorm, [h2], [n1w1, _par(sc1[1]), _par(sh1[1])], [D], out_dtypes=[BF16])
    pw = _mm(a2, w_pw1, name="l1_pw1")
    b_pw1 = Wf["conf_b_pw1"][0]
    glu = _rw_fwd("l1_glu", _f_glu, [(pw, 0, D), (pw, D, D)], [_par(b_pw1[:D]), _par(b_pw1[D:])], [D])
    conf_taps = [(k - KC // 2, None) for k in range(KC)]
    cv = _conv_fwd("l1_conv", glu, 0, D, Wf["conf_w_dw"][0], Wf["conf_b_dw"], conf_taps)
    ln_w, ln_b = _par(Wf["conf_ln_w"][0]), _par(Wf["conf_ln_b"][0])
    ls = _rw_fwd("l1_ln_silu", _f_ln_silu, [cv], [ln_w, ln_b], [D], out_dtypes=[BF16])
    p2 = _mm(ls, w_pw2, name="l1_pw2")
    b_pw2 = _par(Wf["conf_b_pw2"][0])
    h3 = _rw_fwd("l1_res1", _f_gate_res_bias, [h2, p2], [g1_1, b_pw2], [D])
    a3, hh1, gc1, act1, dn1 = ffn_fwd(1, h3, "l1")
    h4 = _rw_fwd("l1_res2", _f_gate_res, [h3, dn1], [g2_1], [D])

    fnw = final_norm_w[None, :]
    tgt = loss_target[0]
    loss_local = _loss_fwd(h4, tgt, fnw)[0, 0]
    loss = lax.psum(loss_local, ("x", "y", "c"))

    G_full = {}
    reduces = {}

    def start_reduce(tag, items, dep):
        def blocks_of(g, ax):
            if g.ndim == 3:
                return g
            return g.reshape(N_CHIPS, g.shape[0] // N_CHIPS, g.shape[1]) if ax == 0 else _to_blocks(g, ax)

        blocks = [blocks_of(g, ax).astype(BF16) for _, g, ax in items]
        handle, tok = _exchange4_start("reduce_" + tag + "_start", blocks, False, dep)
        reduces[tag] = ([n for n, _, _ in items], handle, blocks)
        return tok
    ones = jnp.ones((L, 1), F32)
    (dh4,), (dfnw,) = _rw_bwd("loss_bwd", _f_loss_rows, [h4, tgt], [_par(final_norm_w)], [ones],
                              row_grad=[True, False], par_grad=[True])
    G_full["final_norm_w"] = dfnw.reshape(D)

    def ffn_bwd(l, h, saved, g2_l, dh_out, tag):
        a, hh, gc, act, dn = saved
        (ddn,), (dg2,) = _rw_bwd(tag + "_res2_bwd", _f_gate_res, [h, dn], [g2_l], [dh_out],
                                 row_grad=[False, True], par_grad=[True], row_dtypes=[BF16])
        dact = _mm(ddn, w_dn[l], tb=True, name=tag + "_w_down_dx")
        dwdn = _mm(act, ddn, ta=True, name=tag + "_w_down_dw", out_dtype=BF16)
        (dval, dgc), _ = _rw_bwd(tag + "_act_bwd", _f_ffn_act, [(hh, 0, FH), gc], [], [dact],
                                 row_grad=[True, True], par_grad=[], col_tile=_tile(FH, 1536), row_dtypes=[BF16, F32])
        dgin, dcw, dcb = _conv_bwd(tag + "_ffn_conv_bwd", hh, FH, FH, Wf["ffn_conv_w"][l].reshape(9, FH), dgc,
                                   grid_taps, du_dtype=BF16)
        dhh = jnp.concatenate([dval, dgin], axis=1)
        da = _mm(dhh, w_up[l], tb=True, name=tag + "_w_up_dx")
        dwup = _mm(a, dhh, ta=True, name=tag + "_w_up_dw", out_dtype=BF16, col_blocks=N_CHIPS)
        (dh,), (dn2w, dsc2, dsh2) = _rw_bwd(
            tag + "_modnorm2_bwd", _f_modnorm, [h], [_par(norm2_w[l]), _par(sc2[l]), _par(sh2[l])], [da],
            row_grad=[True], par_grad=[True, True, True], add=dh_out)
        return dh, dict(w_down=dwdn, w_up=dwup, conv_w=dcw.reshape(3, 3, FH), conv_b=dcb.reshape(FH),
                        n2w=dn2w.reshape(D), sc2=dsc2.reshape(D), sh2=dsh2.reshape(D), g2=dg2.reshape(D))

    dh3, gf1 = ffn_bwd(1, h3, (a3, hh1, gc1, act1, dn1), g2_1, dh4, "l1")
    (dp2,), (dg1_1, db_pw2) = _rw_bwd("l1_res1_bwd", _f_gate_res_bias, [h2, p2], [g1_1, b_pw2], [dh3],
                                      row_grad=[False, True], par_grad=[True, True], row_dtypes=[BF16])
    dls = _mm(dp2, w_pw2, tb=True, name="l1_pw2_dx")
    dw_pw2 = _mm(ls, dp2, ta=True, name="l1_pw2_dw", out_dtype=BF16)
    (dcv,), (dln_w, dln_b) = _rw_bwd("l1_ln_silu_bwd", _f_ln_silu, [cv], [ln_w, ln_b], [dls],
                                     row_grad=[True], par_grad=[True, True])
    dglu, dw_dw, db_dw = _conv_bwd("l1_conv_bwd", glu, 0, D, Wf["conf_w_dw"][0], dcv, conf_taps)
    (dpa, dpg), (dba, dbg) = _rw_bwd("l1_glu_bwd", _f_glu, [(pw, 0, D), (pw, D, D)],
                                     [_par(b_pw1[:D]), _par(b_pw1[D:])], [dglu],
                                     row_grad=[True, True], par_grad=[True, True], row_dtypes=[BF16, BF16])
    dpw = jnp.concatenate([dpa, dpg], axis=1)
    da2 = _mm(dpw, w_pw1, tb=True, name="l1_pw1_dx")
    dw_pw1 = _mm(a2, dpw, ta=True, name="l1_pw1_dw", out_dtype=BF16, col_blocks=N_CHIPS)
    (dh2,), (dn1w1, dsc1_1, dsh1_1) = _rw_bwd(
        "l1_modnorm1_bwd", _f_modnorm, [h2], [n1w1, _par(sc1[1]), _par(sh1[1])], [da2],
        row_grad=[True], par_grad=[True, True, True], add=dh3)
    G_full["conf_b_pw2"] = db_pw2.reshape(1, D)
    G_full["conf_ln_w"], G_full["conf_ln_b"] = dln_w.reshape(1, D), dln_b.reshape(1, D)
    G_full["conf_w_dw"], G_full["conf_b_dw"] = dw_dw[None], db_dw.reshape(1, D)
    G_full["conf_b_pw1"] = jnp.concatenate([dba.reshape(1, D), dbg.reshape(1, D)], axis=1)

    token = start_reduce("l1", [("conf_w_pw2", dw_pw2, 0), ("conf_w_pw1", dw_pw1, 1), ("ffn_w_up1", gf1["w_up"], 1),
                                ("ffn_w_down1", gf1["w_down"], 0)], dw_pw2)
    dh2 = _tie("tie_reduce_l1", dh2, token)
    dh1, gf0 = ffn_bwd(0, h1, (a1, hh0, gc0, act0, dn0), g2_0, dh2, "l0")
    G_full["ffn_conv_w"] = jnp.stack([gf0["conv_w"], gf1["conv_w"]])
    G_full["ffn_conv_b"] = jnp.stack([gf0["conv_b"], gf1["conv_b"]])

    (dmix,), (dg1_0,) = _rw_bwd("l0_res1_bwd", _f_gate_res, [xl, mix0], [g1_0], [dh1],
                                row_grad=[False, True], par_grad=[True], row_dtypes=[BF16])
    dyn = _mm(dmix, w_out, tb=True, name="l0_w_out_dx")
    dw_out = _mm(yn, dmix, ta=True, name="l0_w_out_dw", out_dtype=BF16)
    token = start_reduce("l0", [("ffn_w_up0", gf0["w_up"], 1), ("ffn_w_down0", gf0["w_down"], 0),
                                ("ssd_w_out", dw_out, 0)], dw_out)
    dyn = _tie("tie_reduce_l0", dyn, token)
    (dy_lat, dxs_gate, dz_lat), (dd_rep, dssd_nw) = _rw_bwd(
        "l0_ssd_gate_bwd", _f_ssd_gate, gate_rows, [d_rep, ssd_nw], [dyn],
        row_grad=[True, False, True, True], par_grad=[True, True], T=L, row_dtypes=[F32, F32, BF16])
    g_f, g_b = _ssd_bwd(xbc, DI, DI + G * N, dtr, a_neg, s_enter, dy_lat, P, ncc)
    dxs_gate_all = jnp.pad(dxs_gate, ((Lc, 0), (0, 0)))
    silu_bwd = functools.partial(_rw_bwd, f=_silu, pars=[], row_grad=[True], par_grad=[], T=T0)
    (dxs_pre,), _ = silu_bwd("l0_silu_bwd_x", rows=[(xbc_pre, 0, DI)], cot_fn=lambda p, q, r: p + q + r,
                             cots=[g_f[0], g_b[0], dxs_gate_all], col_tile=_tile(DI, 1024))
    (db_pre,), _ = silu_bwd("l0_silu_bwd_b", rows=[(xbc_pre, DI, G * N)], cot_fn=lambda p, q: p + q,
                            cots=[g_f[1], g_b[1]], col_tile=_tile(G * N, 1024))
    (dc_pre,), _ = silu_bwd("l0_silu_bwd_c", rows=[(xbc_pre, DI + G * N, G * N)], cot_fn=lambda p, q: p + q,
                            cots=[g_f[2], g_b[2]], col_tile=_tile(G * N, 1024))
    conv_w0 = Wf["ssd_conv_w"][0]
    pieces = []
    for tag, off, width, g_pre in (("x", 0, DI, dxs_pre), ("b", DI, G * N, db_pre), ("c", DI + G * N, G * N, dc_pre)):
        pieces.append(_conv_bwd("l0_conv_bwd_" + tag, proj, DI + off, width, conv_w0[:, off:off + width], g_pre,
                                seg_taps, du_dtype=BF16))
    dconv_in = [p[0] for p in pieces]
    dcw0 = jnp.concatenate([p[1] for p in pieces], axis=1)
    dcb0 = jnp.concatenate([p[2] for p in pieces], axis=1)
    ddt = jnp.concatenate([g_f[3][:, 0, :].T, g_b[3][:, 0, :].T], axis=1)
    (ddt_raw,), (ddt_bias,) = _rw_bwd("l0_softplus_bwd", _f_softplus, [dt_raw], [dt_bias], [ddt],
                                      row_grad=[True], par_grad=[True], row_dtypes=[BF16])
    dproj = jnp.concatenate([jnp.pad(dz_lat, ((Lc, 0), (0, 0))), *dconv_in, ddt_raw], axis=1)
    da0 = _mm(dproj, w_in, tb=True, name="l0_w_in_dx")
    dw_in = _mm(a0, dproj, ta=True, name="l0_w_in_dw", out_dtype=BF16)
    token = start_reduce("in", [("ssd_w_in", dw_in, 1)], dw_in)
    da0 = _tie("tie_reduce_in", da0, token)
    (dhcat,), (dn1w0, dsc_seg, dsh_seg) = _rw_bwd(
        "l0_modnorm1_bwd", _f_modnorm, [hcat], [n1w0, sc_seg, sh_seg], [da0],
        row_grad=[True], par_grad=[True, True, True], seg_rows=(Lc,))
    grad_x = (dhcat[Lc:] + dh1)[None]

    da_heads = jnp.stack([g[4][:, 0, 0].reshape(G, T0 // SSD_CHUNK, H // G).sum(axis=1).reshape(H)
                          for g in (g_f, g_b)])[None]
    G_full["ssd_a_log"] = da_heads * (-jnp.exp(ssd_a_log))
    G_full["ssd_dt_bias"] = ddt_bias.reshape(1, 2, H)
    G_full["ssd_d"] = dd_rep.reshape(H, P).sum(axis=1)[None]
    G_full["ssd_norm_w"] = dssd_nw.reshape(1, DI)
    G_full["ssd_conv_w"], G_full["ssd_conv_b"] = dcw0[None], dcb0.reshape(1, CD)
    G_full["norm1_w"] = jnp.stack([dn1w0.reshape(D), dn1w1.reshape(D)])
    G_full["norm2_w"] = jnp.stack([gf0["n2w"], gf1["n2w"]])

    zD = jnp.zeros((D,), F32)
    dm_own = jnp.stack([
        jnp.concatenate([dsh_seg[1, 0], dsc_seg[1, 0], dg1_0.reshape(D), gf0["sh2"], gf0["sc2"], gf0["g2"]]),
        jnp.concatenate([dsh1_1.reshape(D), dsc1_1.reshape(D), dg1_1.reshape(D), gf1["sh2"], gf1["sc2"], gf1["g2"]]),
    ])
    dmc_own = jnp.concatenate([dsh_seg[0, 0], dsc_seg[0, 0], zD, zD, zD, zD])

    out = {}

    def finish_reduce(tags, after, swap_name):
        partial = {}
        for tag in tags:
            names, handle, blocks = reduces[tag]
            landed = _exchange4_wait("reduce_" + tag + "_wait", handle, after)
            for n, blk, own in zip(names, landed, blocks):
                r = _fill_own(blk, own, chip, False)
                partial[n] = _sum_leading("sum4_" + n, r.reshape(N_CHIPS, -1, r.shape[-1]),
                                          (0, 1, 2, 3)).reshape(r.shape[1:])
        for n in ("ffn_w_up", "ffn_w_down"):
            if n + "0" in partial:
                partial[n] = jnp.stack([partial.pop(n + "0"), partial.pop(n + "1")])
        names = [n for n in BIG if n in partial]
        mine = [partial[n].reshape(W[n].shape) for n in names]
        for n, own, sib in zip(names, mine, _swap_sibling(swap_name, mine)):
            out[n] = _adamw("adamw_" + n, W[n], Mo[n], Vo[n], own, sib)
        return names

    early = finish_reduce(["l1", "l0"], dhcat, "swap_grads_early")

    small_sum_names = [n for n in SMALL if n not in ("c_ctx", "mod_b")]
    sum_part = [G_full[n] for n in small_sum_names] + [dmc_own]
    n_sum = sum(int(a.size) for a in sum_part)
    packed = _tie("tie_small_grads", _pack(sum_part + [dm_own]), out[early[-1]][1])
    gat = _allgather8("gather_small_grads", packed)
    total = _sum_leading("sum_small_grads", gat, tuple(range(N_DEV)))
    summed = _unpack(total, [a.shape for a in sum_part])
    Gs = dict(zip(small_sum_names, summed[:-1]))
    dmc_tot = summed[-1]
    dm_all = jnp.stack([gat[k].reshape(-1)[n_sum:n_sum + 2 * 6 * D].reshape(2, 6 * D) for k in range(N_DEV)], axis=1)
    dm16 = jnp.concatenate([dm_all, jnp.stack([dmc_tot, jnp.zeros_like(dmc_tot)])[:, None, :],
                            jnp.zeros((2, 16 - N_DEV - 1, 6 * D), F32)], axis=1)
    Gs["mod_b"] = _sum_leading("sum_mod_b", dm16.transpose(1, 0, 2).reshape(16, 2 * 6 * D // LANE, LANE),
                               tuple(range(N_DEV + 1))).reshape(2, 6 * D)

    dm16_shard = lax.dynamic_slice_in_dim(dm16, chip * S_mod, S_mod, axis=2)
    ds16 = _mm(dm16_shard[0], mod_w[0], tb=True, precision=HIGHEST, name="c_ctx_dx")
    sig = jax.nn.sigmoid(c_ctx)
    dcc_part = ds16[N_DEV] * (sig * (1.0 + c_ctx * (1.0 - sig)))
    gat_cc = _allgather8("gather_c_ctx_grad", _pack([dcc_part]))
    Gs["c_ctx"] = _sum_leading("sum_c_ctx_grad", gat_cc, (0, 2, 4, 6)).reshape(-1)[:D]

    s16t = _silu(c16).T
    out["mod_w"] = _mod_w_update(s16t, dm16_shard, mod_w, m_mod_w, v_mod_w)
    finish_reduce(["in"], out["mod_w"][0], "swap_grads_late")

    def own(n, full):
        if n in SHARD_AXIS:
            size = W[n].shape[SHARD_AXIS[n]]
            return lax.dynamic_slice_in_dim(full, chip * size, size, axis=SHARD_AXIS[n])
        return full

    g_small = [own(n, Gs[n].reshape(Wf[n].shape)) for n in SMALL]
    shapes = [W[n].shape for n in SMALL]
    pk = [_pack([W[n] for n in SMALL]), _pack([Mo[n] for n in SMALL]), _pack([Vo[n] for n in SMALL]), _pack(g_small)]
    res = _adamw("adamw_small", pk[0], pk[1], pk[2], pk[3], jnp.zeros_like(pk[3]))
    unpacked = [_unpack(r, shapes) for r in res]
    for k, n in enumerate(SMALL):
        out[n] = tuple(u[k] for u in unpacked)

    grads = [out[n][0] for n in WEIGHTS]
    deltas = [out[n][1] for n in WEIGHTS]
    new_m = [out[n][2] for n in WEIGHTS]
    new_v = [out[n][3] for n in WEIGHTS]
    return (loss, grad_x, *grads, *deltas, *new_m, *new_v)
```

```python
import functools

import jax
import jax.numpy as jnp
from jax import lax
from jax.experimental import pallas as pl
from jax.experimental.pallas import tpu as pltpu

F32 = jnp.float32
BF16 = jnp.bfloat16
MESH = pl.DeviceIdType.MESH
HIGHEST = lax.Precision.HIGHEST

VMEM_LIMIT_BYTES = 48 * 1024 * 1024
LANE = 128
SUBLANE = 8

SSD_STATE = 128
SSD_CHUNK = 128
GRID_W = 64
EPS = 1e-6
N_CHIPS = 4
N_DEV = 8

ADAM_LR = 0.001
ADAM_B1 = 0.9
ADAM_B2 = 0.999
ADAM_EPS = 1e-08
ADAM_WD = 0.01
ADAM_STEP = 10


def _pcall(body, **kw):
    return pl.pallas_call(body, **kw)


def _cparams(n_grid):
    return pltpu.CompilerParams(dimension_semantics=("arbitrary",) * n_grid, vmem_limit_bytes=VMEM_LIMIT_BYTES)


def _cdiv(a, b):
    return -(-a // b)


def _round_up(a, b):
    return _cdiv(a, b) * b


def _tile(n, cap):
    if n <= cap:
        return n
    best = None
    for t in range(LANE, cap + 1, LANE):
        if n % t == 0:
            best = t
    if best is None:
        npad = _round_up(n, LANE)
        for t in range(LANE, cap + 1, LANE):
            if npad % t == 0:
                best = t
    return best


def _row_tile(n, cap, also=()):
    best = None
    for step in (2 * SUBLANE, SUBLANE):
        for t in range(step, min(cap, n) + 1, step):
            if n % t == 0 and all(a % t == 0 for a in also):
                best = t
        if best is not None:
            break
    assert best is not None, (n, cap, also)
    return best


def _silu(v):
    return v * jax.nn.sigmoid(v)


def _mm(a, b, *, name, ta=False, tb=False, precision=None, cap=1024, out_dtype=F32, col_blocks=None):
    M, K = (a.shape[1], a.shape[0]) if ta else a.shape
    N = b.shape[0] if tb else b.shape[1]
    assert K == (b.shape[1] if tb else b.shape[0]), (a.shape, b.shape, ta, tb)
    tm, tk = _tile(M, cap), _tile(K, cap + cap // 2)
    tn = _tile(N if col_blocks is None else N // col_blocks, cap + cap // 2)
    nm, nn, nk = _cdiv(M, tm), _cdiv(N, tn), _cdiv(K, tk)
    k_tail = K % tk
    exact = precision is not None

    def body(a_ref, b_ref, o_ref, acc_ref):
        k = pl.program_id(2)

        @pl.when(k == 0)
        def _():
            acc_ref[...] = jnp.zeros_like(acc_ref)

        av = a_ref[...]
        bv = b_ref[...]
        if k_tail:
            lim = K - k * tk
            ka = lax.broadcasted_iota(jnp.int32, av.shape, 0 if ta else 1)
            kb = lax.broadcasted_iota(jnp.int32, bv.shape, 1 if tb else 0)
            av = jnp.where(ka < lim, av, jnp.zeros_like(av))
            bv = jnp.where(kb < lim, bv, jnp.zeros_like(bv))
        if exact:
            av = av.astype(F32)
            bv = bv.astype(F32)
        else:
            av = av.astype(BF16)
            bv = bv.astype(BF16)
        dn = (((0 if ta else 1,), (1 if tb else 0,)), ((), ()))
        acc_ref[...] += lax.dot_general(av, bv, dn, preferred_element_type=F32, precision=precision)

        @pl.when(k == nk - 1)
        def _():
            o_ref[...] = acc_ref[...].astype(o_ref.dtype)

    a_spec = pl.BlockSpec((tk, tm), lambda i, j, k: (k, i)) if ta else pl.BlockSpec((tm, tk), lambda i, j, k: (i, k))
    b_spec = pl.BlockSpec((tn, tk), lambda i, j, k: (j, k)) if tb else pl.BlockSpec((tk, tn), lambda i, j, k: (k, j))
    if col_blocks is None:
        out_spec = pl.BlockSpec((tm, tn), lambda i, j, k: (i, j))
        out_shape = jax.ShapeDtypeStruct((M, N), out_dtype)
    else:
        per = (N // col_blocks) // tn
        assert per * tn * col_blocks == N, (N, col_blocks, tn)
        out_spec = pl.BlockSpec((None, tm, tn), lambda i, j, k: (j // per, i, j % per))
        out_shape = jax.ShapeDtypeStruct((col_blocks, M, N // col_blocks), out_dtype)
    return _pcall(
        body, name=name, grid=(nm, nn, nk), in_specs=[a_spec, b_spec], out_specs=out_spec, out_shape=out_shape,
        scratch_shapes=[pltpu.VMEM((tm, tn), F32)], compiler_params=_cparams(3),
    )(a, b)


def _norm_rows(rows):
    out = []
    for r in rows:
        if not isinstance(r, tuple):
            r = (r,)
        arr, off, width, roff = (r + (0, None, 0)[len(r) - 1:])
        out.append((arr, off, width if width is not None else arr.shape[1], roff))
    return out


def _rw_plan(T, rows, pars, seg_rows, col_tile, tm_cap):
    widths = [r[2] for r in rows]
    wmax = max(widths + [p.shape[-1] for p in pars] + [1])
    if col_tile is not None:
        assert all(w == widths[0] for w in widths) and all(p.shape[-1] == widths[0] for p in pars)
        ncol = widths[0] // col_tile
        assert ncol * col_tile == widths[0]
        wmax = col_tile
    else:
        ncol = 1
    cap = tm_cap if tm_cap is not None else max(SUBLANE, min(256, (256 * 1024) // wmax))
    tm = _row_tile(T, cap, also=tuple(seg_rows) + tuple(abs(r[3]) for r in rows if r[3]))
    bounds = tuple(s // tm for s in seg_rows)
    return widths, ncol, tm, bounds


def _rw_specs(rows, pars, ncol, tm, bounds, col_tile):
    def seg(i):
        s = 0
        for b in bounds:
            s = s + (i >= b).astype(jnp.int32)
        return s

    specs = []
    for arr, off, w, roff in rows:
        bw = col_tile if col_tile is not None else w
        assert off % bw == 0 and roff % tm == 0, (off, bw, roff, tm)
        specs.append(pl.BlockSpec((tm, bw), functools.partial(lambda j, i, ob, rb: (jnp.maximum(i + rb, 0), ob + j),
                                                              ob=off // bw, rb=roff // tm)))
    for p in pars:
        bw = col_tile if col_tile is not None else p.shape[-1]
        if p.shape[0] > 1:
            specs.append(pl.BlockSpec((None, 1, bw), lambda j, i: (seg(i), 0, j)))
        else:
            specs.append(pl.BlockSpec((None, 1, bw), lambda j, i: (0, 0, j)))
    return specs, seg


def _rw_fwd(name, f, rows, pars, out_widths, *, T=None, seg_rows=(), col_tile=None, tm_cap=None, out_dtypes=None):
    rows = _norm_rows(rows)
    T = rows[0][0].shape[0] if T is None else T
    widths, ncol, tm, bounds = _rw_plan(T, rows, pars, seg_rows, col_tile, tm_cap)
    in_specs, _ = _rw_specs(rows, pars, ncol, tm, bounds, col_tile)
    nr, npar, nout = len(rows), len(pars), len(out_widths)

    def body(*refs):
        vals = [r[...] for r in refs[:nr + npar]]
        outs = f(*vals)
        if not isinstance(outs, (tuple, list)):
            outs = (outs,)
        for o_ref, o in zip(refs[nr + npar:], outs):
            o_ref[...] = o.astype(o_ref.dtype)

    out_specs = [pl.BlockSpec((tm, col_tile if col_tile is not None else w), lambda j, i: (i, j)) for w in out_widths]
    res = _pcall(
        body, name=name, grid=(ncol, T // tm), in_specs=in_specs, out_specs=out_specs,
        out_shape=[jax.ShapeDtypeStruct((T, w), dt) for w, dt in zip(out_widths, out_dtypes or [F32] * nout)],
        compiler_params=_cparams(2),
    )(*[r[0] for r in rows], *pars)
    return res if nout > 1 else res[0]


def _rw_bwd(name, f, rows, pars, cots, *, row_grad, par_grad, T=None, seg_rows=(), col_tile=None, tm_cap=None,
            add=None, cot_fn=None, row_dtypes=None):
    rows = _norm_rows(rows)
    cots = _norm_rows(cots)
    T = rows[0][0].shape[0] if T is None else T
    extra = _norm_rows([add]) if add is not None else []
    all_rows = rows + cots + extra
    widths, ncol, tm, bounds = _rw_plan(T, all_rows, pars, seg_rows, col_tile, tm_cap)
    in_specs, seg = _rw_specs(all_rows, pars, ncol, tm, bounds, col_tile)
    nr, nc, ne, npar = len(rows), len(cots), len(extra), len(pars)
    row_idx = [k for k in range(nr) if row_grad[k]]
    par_idx = [k for k in range(npar) if par_grad[k]]

    def body(*refs):
        i = pl.program_id(1)
        row_vals = [r[...] for r in refs[:nr]]
        cot_vals = [r[...] for r in refs[nr:nr + nc]]
        cot_vals = [jnp.where(i + c[3] // tm >= 0, v, jnp.zeros_like(v)) if c[3] < 0 else v
                    for v, c in zip(cot_vals, cots)]
        add_vals = [r[...] for r in refs[nr + nc:nr + nc + ne]]
        par_vals = [r[...] for r in refs[nr + nc + ne:nr + nc + ne + npar]]
        out_refs = refs[nr + nc + ne + npar:]
        outs, vjp = jax.vjp(f, *row_vals, *par_vals)
        if cot_fn is not None:
            cot_vals = cot_fn(*cot_vals)
            if not isinstance(cot_vals, (tuple, list)):
                cot_vals = (cot_vals,)
        if isinstance(outs, (tuple, list)):
            grads = vjp(tuple(c.astype(o.dtype) for c, o in zip(cot_vals, outs)))
        else:
            grads = vjp(cot_vals[0].astype(outs.dtype))
        first_seg = i == 0
        for b in bounds:
            first_seg = first_seg | (i == b)
        for n, k in enumerate(row_idx):
            g = grads[k]
            if n == 0 and add_vals:
                g = g + add_vals[0]
            out_refs[n][...] = g.astype(out_refs[n].dtype)
        for n, k in enumerate(par_idx):
            g = grads[nr + k]
            o_ref = out_refs[len(row_idx) + n]
            first = first_seg if pars[k].shape[0] > 1 else (i == 0)

            @pl.when(first)
            def _(o_ref=o_ref, g=g):
                o_ref[...] = g

            @pl.when(jnp.logical_not(first))
            def _(o_ref=o_ref, g=g):
                o_ref[...] += g

    out_specs, out_shape = [], []
    for k in row_idx:
        w = widths[k]
        out_specs.append(pl.BlockSpec((tm, col_tile if col_tile is not None else w), lambda j, i: (i, j)))
        out_shape.append(jax.ShapeDtypeStruct((T, w), row_dtypes[len(out_shape)] if row_dtypes else F32))
    for k in par_idx:
        p = pars[k]
        bw = col_tile if col_tile is not None else p.shape[-1]
        if p.shape[0] > 1:
            out_specs.append(pl.BlockSpec((None, 1, bw), lambda j, i: (seg(i), 0, j)))
        else:
            out_specs.append(pl.BlockSpec((None, 1, bw), lambda j, i: (0, 0, j)))
        out_shape.append(jax.ShapeDtypeStruct(p.shape, F32))
    res = _pcall(
        body, name=name, grid=(ncol, T // tm), in_specs=in_specs, out_specs=out_specs, out_shape=out_shape,
        compiler_params=_cparams(2),
    )(*[r[0] for r in all_rows], *pars)
    return list(res[:len(row_idx)]), list(res[len(row_idx):])


def _f_modnorm(h, w, sc, sh):
    y = h * lax.rsqrt(jnp.mean(h * h, axis=-1, keepdims=True) + EPS)
    return (y * w) * (1.0 + sc) + sh


def _f_gate_res(h, y, g):
    return h + g * y


def _f_gate_res_bias(h, y, g, b):
    return h + g * (y + b)


def _f_ffn_act(val, gate):
    return _silu(gate) * val


def _f_softplus(raw, bias):
    v = raw + bias
    return jnp.maximum(v, 0.0) + jnp.log(1.0 + jnp.exp(-jnp.abs(v)))


def _f_ssd_gate(yf, yb, xs, z, d_rep, nw):
    y = (yf + yb + d_rep * xs) * _silu(z)
    return (y * lax.rsqrt(jnp.mean(y * y, axis=-1, keepdims=True) + EPS)) * nw


def _f_glu(a, g, ba, bg):
    return (a + ba) * jax.nn.sigmoid(g + bg)


def _f_ln_silu(h, w, b):
    mu = jnp.mean(h, axis=-1, keepdims=True)
    d = h - mu
    y = d * lax.rsqrt(jnp.mean(d * d, axis=-1, keepdims=True) + EPS)
    return _silu(y * w + b)


def _f_loss_rows(h, t, w):
    y = (h * lax.rsqrt(jnp.mean(h * h, axis=-1, keepdims=True) + EPS)) * w
    e = y - t
    return 0.5 * jnp.mean(e * e, axis=-1, keepdims=True)


def _f_adamw(w, m, v, ga, gb):
    g = ga + gb
    m = ADAM_B1 * m + (1.0 - ADAM_B1) * g
    v = ADAM_B2 * v + (1.0 - ADAM_B2) * (g * g)
    m_hat = m / (1.0 - ADAM_B1 ** ADAM_STEP)
    v_hat = v / (1.0 - ADAM_B2 ** ADAM_STEP)
    delta = -ADAM_LR * (m_hat / (jnp.sqrt(v_hat) + ADAM_EPS) + ADAM_WD * w)
    return g, delta, m, v


def _adamw(name, w, m, v, ga, gb):
    shape = w.shape
    c = shape[-1]
    two_d = [t.reshape(-1, c) for t in (w, m, v, ga, gb)]
    rows = two_d[0].shape[0]
    pad = _round_up(rows, SUBLANE) - rows
    if pad:
        two_d = [jnp.pad(t, ((0, pad), (0, 0))) for t in two_d]
    outs = _rw_fwd(name, _f_adamw, two_d, [], [c] * 4)
    return tuple(o[:rows].reshape(shape) for o in outs)


def _sum_leading(name, x, idxs):
    _, R, C = x.shape
    tm = _row_tile(R, max(SUBLANE, min(512, (512 * 1024) // C)))

    def body(x_ref, o_ref):
        acc = x_ref[idxs[0]].astype(F32)
        for k in idxs[1:]:
            acc = acc + x_ref[k].astype(F32)
        o_ref[...] = acc

    return _pcall(
        body, name=name, grid=(R // tm,), in_specs=[pl.BlockSpec((x.shape[0], tm, C), lambda i: (0, i, 0))],
        out_specs=pl.BlockSpec((tm, C), lambda i: (i, 0)), out_shape=jax.ShapeDtypeStruct((R, C), F32),
        compiler_params=_cparams(1),
    )(x)


def _loss_fwd(h, t, w):
    T, D = h.shape
    tm = _row_tile(T, 256)

    def body(h_ref, t_ref, w_ref, o_ref):
        i = pl.program_id(0)
        part = jnp.sum(_f_loss_rows(h_ref[...], t_ref[...], w_ref[...]), axis=0, keepdims=True)
        part = jnp.broadcast_to(part, (1, LANE))

        @pl.when(i == 0)
        def _():
            o_ref[...] = part

        @pl.when(i > 0)
        def _():
            o_ref[...] += part

    return _pcall(
        body, name="loss_fwd", grid=(T // tm,),
        in_specs=[pl.BlockSpec((tm, D), lambda i: (i, 0)), pl.BlockSpec((tm, D), lambda i: (i, 0)),
                  pl.BlockSpec((1, D), lambda i: (0, 0))],
        out_specs=pl.BlockSpec((1, LANE), lambda i: (0, 0)), out_shape=jax.ShapeDtypeStruct((1, LANE), F32),
        compiler_params=_cparams(1),
    )(h, t, w)


CONV_ROWS = 256
CONV_ROWS_FEW_TAPS = 1024
CONV_ACC_ELEMS = 16384


def _col_mask(arg, t):
    col = jnp.bitwise_and(t, GRID_W - 1)
    return (col != 0) if arg < 0 else (col != GRID_W - 1)


def _conv_plan(T, C, taps):
    seg = [m[1] for _, m in taps if m is not None and m[0] == "seg"]
    boundary = seg[0] if seg else None
    cap = CONV_ROWS_FEW_TAPS if len(taps) <= 9 else CONV_ROWS
    rc = next(r for r in (1024, 768, 512, 256, LANE)
              if r <= cap and T % r == 0 and (boundary is None or boundary % r == 0))
    ct = next((t for t in (512, 256, LANE) if C % t == 0), C)
    reach = max(abs(s) for s, _ in taps)
    hb = next(h for h in (8, 16, 32, 64, 128, 256) if h >= reach and rc % h == 0)
    sub = max(2 * SUBLANE, min(rc, CONV_ACC_ELEMS // ct))
    taps = [(s, None if (m is None or m[0] == "seg") else m[1]) for s, m in taps]
    return rc, ct, hb, sub, T // rc, C // ct, boundary, taps


def _halo_specs(rc, ct, hb, T, off_blocks):
    per = rc // hb
    last = T // hb - 1
    prev = pl.BlockSpec((hb, ct), lambda j, i: (jnp.maximum(i * per - 1, 0), off_blocks + j))
    cur = pl.BlockSpec((rc, ct), lambda j, i: (i, off_blocks + j))
    nxt = pl.BlockSpec((hb, ct), lambda j, i: (jnp.minimum((i + 1) * per, last), off_blocks + j))
    return [prev, cur, nxt]


def _fill_halo(pad_ref, p_ref, c_ref, n_ref, i, nrc, rc, hb, boundary):
    has_prev = i > 0
    has_next = i < nrc - 1
    if boundary is not None:
        has_prev = has_prev & (i * rc != boundary)
        has_next = has_next & ((i + 1) * rc != boundary)
    pad_ref[0:hb, :] = jnp.where(has_prev, p_ref[...], 0.0)
    pad_ref[hb:hb + rc, :] = c_ref[...]
    pad_ref[hb + rc:hb + rc + hb, :] = jnp.where(has_next, n_ref[...], 0.0)


def _shift_plan(keys):
    count = {}
    for s, m in keys:
        k = (s % SUBLANE, m)
        count[k] = count.get(k, 0) + 1
    slots = {}
    for k, n in sorted(count.items(), key=lambda kv: (kv[0][0], str(kv[0][1]))):
        if k != (0, None) and (n >= 2 or k[1] is not None):
            slots[k] = len(slots)
    return slots


def _build_shifted(copies_ref, slots, pad_ref, keys, i, rc, hb, sub):
    for (r, m), slot in slots.items():
        qs = [s - r for s, mk in keys if (s % SUBLANE, mk) == (r, m)]
        lo, hi = hb + min(qs), hb + rc + max(qs)
        for p in range(lo, hi, sub):
            n = min(sub, hi - p)
            v = pad_ref[p + r:p + r + n, :]
            if m is not None:
                t = i * rc - hb + p + r + lax.broadcasted_iota(jnp.int32, (n, 1), 0)
                v = jnp.where(_col_mask(m, t), v, 0.0)
            copies_ref[slot, p:p + n, :] = v


def _read(copies_ref, slots, pad_ref, s, m, row, n):
    k = (s % SUBLANE, m)
    if k in slots:
        q = s - k[0]
        return copies_ref[slots[k], row + q:row + q + n, :]
    return pad_ref[row + s:row + s + n, :]


def _conv_fwd(name, u, col_off, C, w, b, taps, act=False):
    T = u.shape[0]
    rc, ct, hb, sub, nrc, ncc, boundary, taps = _conv_plan(T, C, taps)
    assert col_off % ct == 0
    K = len(taps)
    keys = [(s, None) for s, _ in taps]
    slots = _shift_plan(keys)
    dirs = sorted({m for _, m in taps if m is not None})

    def body(up, uc, un, w_ref, b_ref, *rest):
        y_ref = rest[0]
        pad_ref, copies_ref = rest[-2], rest[-1]
        i = pl.program_id(1)
        _fill_halo(pad_ref, up, uc, un, i, nrc, rc, hb, boundary)
        _build_shifted(copies_ref, slots, pad_ref, keys, i, rc, hb, sub)
        for r0 in range(0, rc, sub):
            acc = jnp.broadcast_to(b_ref[...], (sub, ct))
            for m in [None] + dirs:
                part = None
                for k, (s, mk) in enumerate(taps):
                    if mk != m:
                        continue
                    term = w_ref[k:k + 1, :] * _read(copies_ref, slots, pad_ref, s, None, hb + r0, sub)
                    part = term if part is None else part + term
                if part is None:
                    continue
                if m is not None:
                    t = i * rc + r0 + lax.broadcasted_iota(jnp.int32, (sub, 1), 0)
                    part = jnp.where(_col_mask(m, t), part, 0.0)
                acc = acc + part
            y_ref[r0:r0 + sub, :] = acc
            if act:
                rest[1][r0:r0 + sub, :] = _silu(acc)

    n_out = 2 if act else 1
    res = _pcall(
        body, name=name, grid=(ncc, nrc),
        in_specs=_halo_specs(rc, ct, hb, T, col_off // ct) + [pl.BlockSpec((K, ct), lambda j, i: (0, j)),
                                                              pl.BlockSpec((1, ct), lambda j, i: (0, j))],
        out_specs=[pl.BlockSpec((rc, ct), lambda j, i: (i, j))] * n_out,
        out_shape=[jax.ShapeDtypeStruct((T, C), F32)] * n_out,
        scratch_shapes=[pltpu.VMEM((rc + 2 * hb, ct), F32), pltpu.VMEM((max(len(slots), 1), rc + 2 * hb, ct), F32)],
        compiler_params=_cparams(2),
    )(u, u, u, w, b)
    return res if act else res[0]


def _conv_bwd(name, u, col_off, C, w, g, taps, du_dtype=F32):
    T = u.shape[0]
    rc, ct, hb, sub, nrc, ncc, boundary, taps = _conv_plan(T, C, taps)
    K = len(taps)
    u_keys = [(s, None) for s, _ in taps]
    dirs = sorted({m for _, m in taps if m is not None})
    g_keys = [(-s, m) for s, m in taps] + [(0, m) for m in dirs]
    u_slots, g_slots = _shift_plan(u_keys), _shift_plan(g_keys)

    def body(up, uc, un, gp, gc, gn, w_ref, du_ref, dw_ref, db_ref, upad, gpad, ucopies, gcopies):
        i = pl.program_id(1)
        _fill_halo(upad, up, uc, un, i, nrc, rc, hb, boundary)
        _fill_halo(gpad, gp, gc, gn, i, nrc, rc, hb, boundary)
        _build_shifted(ucopies, u_slots, upad, u_keys, i, rc, hb, sub)
        _build_shifted(gcopies, g_slots, gpad, g_keys, i, rc, hb, sub)

        @pl.when(i == 0)
        def _():
            dw_ref[...] = jnp.zeros_like(dw_ref)
            db_ref[...] = jnp.zeros_like(db_ref)

        def fold(v):
            return jnp.sum(v.reshape(sub // SUBLANE, SUBLANE, ct), axis=0)

        dbs = jnp.zeros((SUBLANE, ct), F32)
        for r0 in range(0, rc, sub):
            dbs = dbs + fold(gpad[hb + r0:hb + r0 + sub, :])
            acc = jnp.zeros((sub, ct), F32)
            for k, (s, m) in enumerate(taps):
                acc = acc + w_ref[k:k + 1, :] * _read(gcopies, g_slots, gpad, -s, m, hb + r0, sub)
            du_ref[r0:r0 + sub, :] = acc.astype(du_ref.dtype)
        db_ref[...] += jnp.sum(dbs, axis=0, keepdims=True)
        for k, (s, m) in enumerate(taps):
            part = jnp.zeros((SUBLANE, ct), F32)
            for r0 in range(0, rc, sub):
                part = part + fold(_read(gcopies, g_slots, gpad, 0, m, hb + r0, sub)
                                   * _read(ucopies, u_slots, upad, s, None, hb + r0, sub))
            dw_ref[k:k + 1, :] += jnp.sum(part, axis=0, keepdims=True)

    halo_u = _halo_specs(rc, ct, hb, T, col_off // ct)
    halo_g = _halo_specs(rc, ct, hb, T, 0)
    rows = rc + 2 * hb
    return _pcall(
        body, name=name, grid=(ncc, nrc),
        in_specs=halo_u + halo_g + [pl.BlockSpec((K, ct), lambda j, i: (0, j))],
        out_specs=[pl.BlockSpec((rc, ct), lambda j, i: (i, j)), pl.BlockSpec((K, ct), lambda j, i: (0, j)),
                   pl.BlockSpec((1, ct), lambda j, i: (0, j))],
        out_shape=[jax.ShapeDtypeStruct((T, C), du_dtype), jax.ShapeDtypeStruct((K, C), F32),
                   jax.ShapeDtypeStruct((1, C), F32)],
        scratch_shapes=[pltpu.VMEM((rows, ct), F32), pltpu.VMEM((rows, ct), F32),
                        pltpu.VMEM((max(len(u_slots), 1), rows, ct), F32),
                        pltpu.VMEM((max(len(g_slots), 1), rows, ct), F32)],
        compiler_params=_cparams(2),
    )(u, u, u, g, g, g, w)


def _ssd_group(xg, bm, cm, s_in, *per_head, reverse, P):
    R = len(per_head) // 2
    dtrs, a_s = per_head[:R], per_head[R:]
    q, rp = xg.shape
    ii = lax.broadcasted_iota(jnp.int32, (q, q), 0)
    jj = lax.broadcasted_iota(jnp.int32, (q, q), 1)
    causal = (jj >= ii) if reverse else (jj <= ii)
    causal_t = (ii >= jj) if reverse else (ii <= jj)
    eye = ii == jj
    lane = lax.broadcasted_iota(jnp.int32, (1, rp), 1)
    row = lax.broadcasted_iota(jnp.int32, (rp, 1), 0)
    nt = (((1,), (1,)), ((), ()))
    tn = (((0,), (0,)), ((), ()))
    cb = lax.dot_general(cm.astype(BF16), bm.astype(BF16), nt, preferred_element_type=F32)
    dt_x = jnp.zeros((q, rp), F32)
    acum_x = jnp.zeros((q, rp), F32)
    tot_row = jnp.zeros((1, rp), F32)
    tot_col = jnp.zeros((rp, 1), F32)
    wts, lane_masks = [], []
    for r in range(R):
        hm = (lane >= r * P) & (lane < (r + 1) * P)
        hc = (row >= r * P) & (row < (r + 1) * P)
        dt_c = jnp.sum(jnp.where(eye, dtrs[r], 0.0), axis=1, keepdims=True)
        dac = dt_c * a_s[r]
        dar = dtrs[r] * a_s[r]
        acum_c = jnp.sum(jnp.where(causal, dar, 0.0), axis=1, keepdims=True)
        acum_r = jnp.sum(jnp.where(causal_t, dac, 0.0), axis=0, keepdims=True)
        decay = jnp.where(causal, jnp.exp(jnp.where(causal, acum_c - acum_r, 0.0)), 0.0)
        tot = jnp.sum(dac, axis=0, keepdims=True)
        dt_x = jnp.where(hm, dt_c, dt_x)
        acum_x = jnp.where(hm, acum_c, acum_x)
        tot_row = jnp.where(hm, tot, tot_row)
        tot_col = jnp.where(hc, tot, tot_col)
        wts.append((cb * decay).astype(BF16))
        lane_masks.append(hm)
    xdt = xg * dt_x
    xdt_b = xdt.astype(BF16)
    y = jnp.zeros((q, rp), F32)
    for r in range(R):
        y = jnp.where(lane_masks[r], jnp.dot(wts[r], xdt_b, preferred_element_type=F32), y)
    dte = jnp.exp(tot_row - acum_x)
    cs = lax.dot_general((xdt * dte).astype(BF16), bm.astype(BF16), tn, preferred_element_type=F32)
    y = y + lax.dot_general(cm.astype(BF16), s_in.astype(BF16), nt, preferred_element_type=F32) * jnp.exp(acum_x)
    s_out = jnp.exp(tot_col) * s_in + cs
    return y, s_out


def _ssd_maps(NC, ncc, reverse_steps):
    def chunk(d, s):
        if reverse_steps:
            s = NC - 1 - s
        return s if d == 0 else jnp.where(s < ncc, ncc - 1 - s, NC - 1 - s + ncc)

    def lat_chunk(d, s):
        c = chunk(d, s) - ncc
        return jnp.where(c < 0, 0 if d == 0 else NC - ncc - 1, c)

    def step(s):
        return NC - 1 - s if reverse_steps else s

    return chunk, lat_chunk, step


def _ssd_specs(chunk, d, R, Q, N, RP, bo, co):
    return [
        pl.BlockSpec((Q, RP), lambda g, s: (chunk(d, s), g)),
        pl.BlockSpec((Q, N), lambda g, s: (chunk(d, s), bo + g)),
        pl.BlockSpec((Q, N), lambda g, s: (chunk(d, s), co + g)),
        pl.BlockSpec((R, 1, Q), lambda g, s: (g, 0, chunk(d, s))),
        pl.BlockSpec((R, 1, 1), lambda g, s: (g, 0, 0)),
    ]


def _ssd_fwd(xbc, b_off, c_off, dtr, a, P, ncc):
    T = xbc.shape[0]
    H = dtr[0].shape[0]
    N, Q = SSD_STATE, SSD_CHUNK
    NC = T // Q
    G = (c_off - b_off) // N
    R = H // G
    RP = R * P
    chunk, lat_chunk, _ = _ssd_maps(NC, ncc, False)

    def body(*refs):
        s = pl.program_id(1)
        s_ref = refs[-1]

        @pl.when(s == 0)
        def _():
            s_ref[...] = jnp.zeros_like(s_ref)

        for d in range(2):
            x_ref, b_ref, c_ref, dtr_ref, a_ref = refs[5 * d:5 * d + 5]
            y_ref, se_ref = refs[10 + 2 * d:12 + 2 * d]
            s_in = s_ref[d]
            se_ref[...] = s_in
            per_head = [dtr_ref[r] for r in range(R)] + [a_ref[r] for r in range(R)]
            y, s_out = _ssd_group(x_ref[...], b_ref[...], c_ref[...], s_in, *per_head, reverse=d == 1, P=P)
            y_ref[...] = y
            s_ref[d] = s_out

    in_specs, out_specs, out_shape, operands = [], [], [], []
    for d in range(2):
        in_specs += _ssd_specs(chunk, d, R, Q, N, RP, b_off // N, c_off // N)
        operands += [xbc, xbc, xbc, dtr[d], a[d]]
        out_specs += [pl.BlockSpec((Q, RP), functools.partial(lambda g, s, d: (lat_chunk(d, s), g), d=d)),
                      pl.BlockSpec((None, None, RP, N), lambda g, s: (g, s, 0, 0))]
        out_shape += [jax.ShapeDtypeStruct((T - ncc * Q, H * P), F32), jax.ShapeDtypeStruct((G, NC, RP, N), F32)]
    y_f, se_f, y_b, se_b = _pcall(
        body, name="ssd_fwd", grid=(G, NC), in_specs=in_specs, out_specs=out_specs, out_shape=out_shape,
        scratch_shapes=[pltpu.VMEM((2, RP, N), F32)], compiler_params=_cparams(2),
    )(*operands)
    return (y_f, y_b), (se_f, se_b)


def _ssd_bwd(xbc, b_off, c_off, dtr, a, s_enter, dy, P, ncc):
    T = xbc.shape[0]
    H = dtr[0].shape[0]
    N, Q = SSD_STATE, SSD_CHUNK
    NC = T // Q
    G = (c_off - b_off) // N
    R = H // G
    RP = R * P
    chunk, lat_chunk, step = _ssd_maps(NC, ncc, True)
    n_in, n_out = 7, 5

    def body(*refs):
        s = pl.program_id(1)
        ds_ref = refs[-1]

        @pl.when(s == 0)
        def _():
            ds_ref[...] = jnp.zeros_like(ds_ref)

        for d in range(2):
            x_ref, b_ref, c_ref, dtr_ref, a_ref, se_ref, dy_ref = refs[n_in * d:n_in * (d + 1)]
            dx_ref, db_ref, dc_ref, ddtr_ref, da_ref = refs[2 * n_in + n_out * d:2 * n_in + n_out * (d + 1)]
            per_head = [dtr_ref[r] for r in range(R)] + [a_ref[r] for r in range(R)]
            f = functools.partial(_ssd_group, reverse=d == 1, P=P)
            _, vjp = jax.vjp(f, x_ref[...], b_ref[...], c_ref[...], se_ref[...], *per_head)
            is_latent = chunk(d, s) >= ncc
            dy_v = jnp.where(is_latent, dy_ref[...], 0.0)
            grads = vjp((dy_v, ds_ref[d]))
            dx_ref[...] = grads[0]
            db_ref[...] = grads[1]
            dc_ref[...] = grads[2]
            ds_ref[d] = grads[3]
            for r in range(R):
                ddtr_ref[r] = grads[4 + r]
                da_ref[r] = jnp.broadcast_to(grads[4 + R + r], (SUBLANE, LANE))

    in_specs, out_specs, out_shape, operands = [], [], [], []
    for d in range(2):
        in_specs += _ssd_specs(chunk, d, R, Q, N, RP, b_off // N, c_off // N) + [
            pl.BlockSpec((None, None, RP, N), lambda g, s: (g, step(s), 0, 0)),
            pl.BlockSpec((Q, RP), functools.partial(lambda g, s, d: (lat_chunk(d, s), g), d=d)),
        ]
        operands += [xbc, xbc, xbc, dtr[d], a[d], s_enter[d], dy]
    for d in range(2):
        at_chunk = functools.partial(lambda g, s, d: (chunk(d, s), g), d=d)
        out_specs += [
            pl.BlockSpec((Q, RP), at_chunk), pl.BlockSpec((Q, N), at_chunk), pl.BlockSpec((Q, N), at_chunk),
            pl.BlockSpec((R, 1, Q), functools.partial(lambda g, s, d: (g, 0, chunk(d, s)), d=d)),
            pl.BlockSpec((R, SUBLANE, LANE), lambda g, s: (g * NC + s, 0, 0)),
        ]
        out_shape += [
            jax.ShapeDtypeStruct((T, H * P), F32), jax.ShapeDtypeStruct((T, G * N), F32),
            jax.ShapeDtypeStruct((T, G * N), F32), jax.ShapeDtypeStruct((H, 1, T), F32),
            jax.ShapeDtypeStruct((G * NC * R, SUBLANE, LANE), F32),
        ]
    res = _pcall(
        body, name="ssd_bwd", grid=(G, NC), in_specs=in_specs, out_specs=out_specs, out_shape=out_shape,
        scratch_shapes=[pltpu.VMEM((2, RP, N), F32)], compiler_params=_cparams(2),
    )(*operands)
    return res[:n_out], res[n_out:]


def _allgather8(name, v):
    R, C = v.shape

    def body(x_ref, out_ref, send_sems, recv_sems, local_sem):
        x, y, c = lax.axis_index("x"), lax.axis_index("y"), lax.axis_index("c")
        me, sibling = (x, y, c), (x, y, 1 - c)
        chips = [(1 - x, y), (x, 1 - y), (1 - x, 1 - y)]

        def slot(px, py, pc):
            return out_ref.at[4 * px + 2 * py + pc]

        def copy(k, block, to, src=None):
            return pltpu.make_async_remote_copy(
                src_ref=slot(*block) if src is None else src, dst_ref=slot(*block),
                send_sem=send_sems.at[k], recv_sem=recv_sems.at[k], device_id=to, device_id_type=MESH)

        mine = pltpu.make_async_copy(x_ref, slot(*me), local_sem)
        mine.start()
        first = [copy(0, me, sibling, src=x_ref)]
        first += [copy(1 + j, me, (*chip, c), src=x_ref) for j, chip in enumerate(chips)]
        for cp in first:
            cp.start()
        passed = [copy(4 + j, (*chip, c), sibling) for j, chip in enumerate(chips)]
        for j, chip in enumerate(chips):
            copy(1 + j, (*chip, c), me).wait_recv()
            passed[j].start()
        copy(0, sibling, me).wait_recv()
        for j, chip in enumerate(chips):
            copy(4 + j, (*chip, 1 - c), me).wait_recv()
        for cp in first + passed:
            cp.wait_send()
        mine.wait()

    return _pcall(
        body, name=name, out_shape=jax.ShapeDtypeStruct((N_DEV, R, C), v.dtype),
        in_specs=[pl.BlockSpec(memory_space=pltpu.VMEM)], out_specs=pl.BlockSpec(memory_space=pltpu.VMEM),
        scratch_shapes=[pltpu.SemaphoreType.DMA((7,)), pltpu.SemaphoreType.DMA((7,)), pltpu.SemaphoreType.DMA],
        compiler_params=pltpu.CompilerParams(vmem_limit_bytes=VMEM_LIMIT_BYTES),
    )(v)


def _exchange4_start(name, srcs, bcast, dep):
    n = len(srcs)
    lands = [lax.empty(((N_CHIPS,) + s.shape) if bcast else s.shape, s.dtype) for s in srcs]

    def body(*refs):
        src, land = refs[:n], refs[n:2 * n]
        send_sems, recv_sems = refs[2 * n + 1], refs[2 * n + 2]
        token = refs[-1]
        x, y, c = lax.axis_index("x"), lax.axis_index("y"), lax.axis_index("c")
        me = 2 * x + y
        for a in range(n):
            for j, (px, py) in enumerate([(1 - x, y), (x, 1 - y), (1 - x, 1 - y)]):
                pltpu.make_async_remote_copy(
                    src_ref=src[a] if bcast else src[a].at[2 * px + py], dst_ref=land[a].at[me],
                    send_sem=send_sems.at[3 * a + j], recv_sem=recv_sems.at[3 * a + j], device_id=(px, py, c),
                    device_id_type=MESH).start()
        token[...] = jnp.zeros_like(token)

    hbm = pl.BlockSpec(memory_space=pltpu.HBM)
    sem = pl.BlockSpec(memory_space=pltpu.SEMAPHORE)
    outs = _pcall(
        body, name=name,
        out_shape=(pltpu.SemaphoreType.DMA((3 * n,)), pltpu.SemaphoreType.DMA((3 * n,)),
                   *[pltpu.HBM(s.shape, s.dtype) for s in srcs], *[pltpu.HBM(l.shape, l.dtype) for l in lands],
                   jax.ShapeDtypeStruct((SUBLANE, LANE), F32)),
        in_specs=[hbm] * (2 * n) + [pl.BlockSpec(memory_space=pl.ANY)],
        out_specs=(sem, sem, *[hbm] * (2 * n), pl.BlockSpec(memory_space=pltpu.VMEM)),
        input_output_aliases={k: 2 + k for k in range(2 * n)},
        compiler_params=pltpu.CompilerParams(has_side_effects=pltpu.SideEffectType.DATAFLOW_SIDE_EFFECTING),
    )(*[pltpu.with_memory_space_constraint(s, pltpu.HBM) for s in srcs],
      *[pltpu.with_memory_space_constraint(l, pltpu.HBM) for l in lands], dep)
    return (n, bcast, outs[0], outs[1], outs[2:2 + n], outs[2 + n:2 + 2 * n]), outs[-1]


def _exchange4_wait(name, handle, after):
    n, bcast, send_sems, recv_sems, src_thru, land_thru = handle

    def body(*refs):
        src, land = refs[:n], refs[n:2 * n]
        send_sems, recv_sems = refs[2 * n], refs[2 * n + 1]
        x, y, c = lax.axis_index("x"), lax.axis_index("y"), lax.axis_index("c")
        for a in range(n):
            for j, (px, py) in enumerate([(1 - x, y), (x, 1 - y), (1 - x, 1 - y)]):
                pk = 2 * px + py
                copy = pltpu.make_async_remote_copy(
                    src_ref=src[a] if bcast else src[a].at[pk], dst_ref=land[a].at[pk],
                    send_sem=send_sems.at[3 * a + j], recv_sem=recv_sems.at[3 * a + j], device_id=(px, py, c),
                    device_id_type=MESH)
                copy.wait_send()
                copy.wait_recv()

    hbm = pl.BlockSpec(memory_space=pltpu.HBM)
    sem = pl.BlockSpec(memory_space=pltpu.SEMAPHORE)
    outs = _pcall(
        body, name=name,
        out_shape=tuple(pltpu.HBM(t.shape, t.dtype) for t in (*src_thru, *land_thru)),
        in_specs=[hbm] * (2 * n) + [sem, sem, pl.BlockSpec(memory_space=pl.ANY)], out_specs=tuple([hbm] * (2 * n)),
        input_output_aliases={k: k for k in range(2 * n)},
        compiler_params=pltpu.CompilerParams(has_side_effects=pltpu.SideEffectType.DATAFLOW_SIDE_EFFECTING),
    )(*src_thru, *land_thru, send_sems, recv_sems, after)
    return list(outs[:n]), list(outs[n:])


def _tie(name, v, token):
    def body(v_ref, token_ref, o_ref):
        del v_ref, token_ref, o_ref

    any_spec = pl.BlockSpec(memory_space=pl.ANY)
    return _pcall(body, name=name, out_shape=jax.ShapeDtypeStruct(v.shape, v.dtype), in_specs=[any_spec, any_spec],
                  out_specs=any_spec, input_output_aliases={0: 0})(v, token)


def _fill_own(landed, own, me, bcast):
    blk = own if bcast else lax.dynamic_index_in_dim(own, me, 0, keepdims=False)
    return lax.dynamic_update_index_in_dim(landed, blk, me, 0)


def _swap_sibling(name, srcs):
    n = len(srcs)

    def body(*refs):
        src, out = refs[:n], refs[n:2 * n]
        send_sems, recv_sems = refs[2 * n:]
        x, y, c = lax.axis_index("x"), lax.axis_index("y"), lax.axis_index("c")
        copies = []
        for a in range(n):
            rc = pltpu.make_async_remote_copy(
                src_ref=src[a], dst_ref=out[a], send_sem=send_sems.at[a], recv_sem=recv_sems.at[a],
                device_id=(x, y, 1 - c), device_id_type=MESH)
            rc.start()
            copies.append(rc)
        for cp in copies:
            cp.wait()

    any_spec = pl.BlockSpec(memory_space=pl.ANY)
    return _pcall(
        body, name=name, out_shape=[jax.ShapeDtypeStruct(s.shape, s.dtype) for s in srcs],
        in_specs=[any_spec] * n, out_specs=[any_spec] * n,
        scratch_shapes=[pltpu.SemaphoreType.DMA((n,)), pltpu.SemaphoreType.DMA((n,))],
    )(*srcs)


def _mod_fwd(c16, mod_w, mod_b_shard):
    nl, D, S = mod_w.shape

    def body(c_ref, w_ref, b_ref, o_ref):
        s = _silu(c_ref[...]).astype(BF16)
        o_ref[...] = jnp.dot(s, w_ref[...].astype(BF16), preferred_element_type=F32) + b_ref[...]

    return _pcall(
        body, name="mod_fwd", grid=(nl,),
        in_specs=[pl.BlockSpec((16, D), lambda l: (0, 0)), pl.BlockSpec((None, D, S), lambda l: (l, 0, 0)),
                  pl.BlockSpec((None, 1, S), lambda l: (l, 0, 0))],
        out_specs=pl.BlockSpec((None, 16, S), lambda l: (l, 0, 0)),
        out_shape=jax.ShapeDtypeStruct((nl, 16, S), F32), compiler_params=_cparams(1),
    )(c16, mod_w, mod_b_shard)


def _mod_w_update(s16t, dm16, w, m, v):
    nl, D, S = w.shape
    tm = _row_tile(D, 256)

    def body(s_ref, dm_ref, w_ref, m_ref, v_ref, g_ref, dl_ref, nm_ref, nv_ref):
        g = jnp.dot(s_ref[...], dm_ref[...], preferred_element_type=F32, precision=HIGHEST)
        g, dl, nm, nv = _f_adamw(w_ref[...], m_ref[...], v_ref[...], g, jnp.zeros_like(g))
        g_ref[...] = g
        dl_ref[...] = dl
        nm_ref[...] = nm
        nv_ref[...] = nv

    big = pl.BlockSpec((None, tm, S), lambda l, i: (l, i, 0))
    return _pcall(
        body, name="mod_w_update", grid=(nl, D // tm),
        in_specs=[pl.BlockSpec((tm, 16), lambda l, i: (i, 0)), pl.BlockSpec((None, 16, S), lambda l, i: (l, 0, 0)),
                  big, big, big],
        out_specs=[big] * 4, out_shape=[jax.ShapeDtypeStruct(w.shape, F32)] * 4, compiler_params=_cparams(2),
    )(s16t, dm16, w, m, v)


def _pack(arrs):
    flat = jnp.concatenate([a.reshape(-1).astype(F32) for a in arrs])
    n = flat.shape[0]
    rows = _round_up(_cdiv(n, LANE), SUBLANE)
    return jnp.pad(flat, (0, rows * LANE - n)).reshape(rows, LANE)


def _unpack(buf, shapes):
    flat = buf.reshape(-1)
    out, pos = [], 0
    for s in shapes:
        n = 1
        for d in s:
            n *= d
        out.append(flat[pos:pos + n].reshape(s))
        pos += n
    return out


SHARD_AXIS = {
    "mod_w": 2, "ssd_w_in": 2, "ssd_conv_w": 2, "ssd_w_out": 1, "conf_w_pw1": 2, "conf_b_pw1": 1, "conf_w_dw": 2,
    "conf_b_dw": 1, "conf_ln_w": 1, "conf_ln_b": 1, "conf_w_pw2": 1, "conf_b_pw2": 1, "ffn_w_up": 2,
    "ffn_conv_w": 3, "ffn_w_down": 1,
}
BIG = ("ssd_w_in", "ssd_w_out", "conf_w_pw1", "conf_w_pw2", "ffn_w_up", "ffn_w_down")
WEIGHTS = ("c_ctx", "mod_w", "mod_b", "norm1_w", "norm2_w", "ssd_w_in", "ssd_conv_w", "ssd_conv_b", "ssd_dt_bias",
           "ssd_a_log", "ssd_d", "ssd_norm_w", "ssd_w_out", "conf_w_pw1", "conf_b_pw1", "conf_w_dw", "conf_b_dw",
           "conf_ln_w", "conf_ln_b", "conf_w_pw2", "conf_b_pw2", "ffn_w_up", "ffn_conv_w", "ffn_conv_b",
           "ffn_w_down", "final_norm_w")
SMALL = tuple(n for n in WEIGHTS if n not in BIG and n != "mod_w")
SMALL_SHARDED = tuple(n for n in SMALL if n in SHARD_AXIS)


def _unshard(stacked, axis):
    return jnp.concatenate([stacked[k] for k in range(N_CHIPS)], axis=axis)


def _to_blocks(full, axis):
    return jnp.stack(jnp.split(full, N_CHIPS, axis=axis))


def _par(v):
    v = v.reshape(-1, v.shape[-1])
    return v[:, None, :]


def kernel(x, c, ctx, c_ctx, mod_w, mod_b, norm1_w, norm2_w, ssd_w_in, ssd_conv_w, ssd_conv_b, ssd_dt_bias, ssd_a_log, ssd_d, ssd_norm_w, ssd_w_out, conf_w_pw1, conf_b_pw1, conf_w_dw, conf_b_dw, conf_ln_w, conf_ln_b, conf_w_pw2, conf_b_pw2, ffn_w_up, ffn_conv_w, ffn_conv_b, ffn_w_down, final_norm_w, loss_target, m_c_ctx, m_mod_w, m_mod_b, m_norm1_w, m_norm2_w, m_ssd_w_in, m_ssd_conv_w, m_ssd_conv_b, m_ssd_dt_bias, m_ssd_a_log, m_ssd_d, m_ssd_norm_w, m_ssd_w_out, m_conf_w_pw1, m_conf_b_pw1, m_conf_w_dw, m_conf_b_dw, m_conf_ln_w, m_conf_ln_b, m_conf_w_pw2, m_conf_b_pw2, m_ffn_w_up, m_ffn_conv_w, m_ffn_conv_b, m_ffn_w_down, m_final_norm_w, v_c_ctx, v_mod_w, v_mod_b, v_norm1_w, v_norm2_w, v_ssd_w_in, v_ssd_conv_w, v_ssd_conv_b, v_ssd_dt_bias, v_ssd_a_log, v_ssd_d, v_ssd_norm_w, v_ssd_w_out, v_conf_w_pw1, v_conf_b_pw1, v_conf_w_dw, v_conf_b_dw, v_conf_ln_w, v_conf_ln_b, v_conf_w_pw2, v_conf_b_pw2, v_ffn_w_up, v_ffn_conv_w, v_ffn_conv_b, v_ffn_w_down, v_final_norm_w):
    given = dict(locals())
    W = {n: given[n] for n in WEIGHTS}
    Mo = {n: given["m_" + n] for n in WEIGHTS}
    Vo = {n: given["v_" + n] for n in WEIGHTS}

    ax, ay, ac = lax.axis_index("x"), lax.axis_index("y"), lax.axis_index("c")
    chip = 2 * ax + ay
    dev = 4 * ax + 2 * ay + ac

    D = x.shape[-1]
    L, Lc = x.shape[1], ctx.shape[1]
    T0 = L + Lc
    H = ssd_a_log.shape[-1]
    DI = ssd_norm_w.shape[-1]
    P = DI // H
    CD = ssd_conv_b.shape[-1]
    N = SSD_STATE
    G = (CD - DI) // (2 * N)
    FH = ffn_conv_b.shape[-1]
    KS = ssd_conv_w.shape[1]
    KC = conf_w_dw.shape[1]
    ncc = Lc // SSD_CHUNK

    shard_b = {n: W[n].astype(BF16) for n in BIG}
    gather_a, token = _exchange4_start("gather_w_in_start", [shard_b["ssd_w_in"]], True, x)
    c = _tie("tie_gather_w_in", c, token)

    small_shard_shapes = [W[n].shape for n in SMALL_SHARDED]
    f1 = _allgather8("gather_small", _pack([c] + [W[n] for n in SMALL_SHARDED]))
    c_rows, full_small = [], {n: [] for n in SMALL_SHARDED}
    for k in range(N_DEV):
        parts = _unpack(f1[k], [c.shape] + small_shard_shapes)
        c_rows.append(parts[0])
        if k % 2 == 0:
            for n, p in zip(SMALL_SHARDED, parts[1:]):
                full_small[n].append(p)
    Wf = dict(W)
    for n in SMALL_SHARDED:
        Wf[n] = jnp.concatenate(full_small[n], axis=SHARD_AXIS[n])
    c16 = jnp.concatenate(c_rows + [c_ctx[None, :], jnp.zeros((16 - N_DEV - 1, D), F32)], axis=0)

    S_mod = mod_w.shape[-1]
    mod_b_shard = lax.dynamic_slice_in_dim(mod_b, chip * S_mod, S_mod, axis=1)[:, None, :]
    mod_part = _mod_fwd(c16, mod_w, mod_b_shard)
    f2 = _allgather8("gather_mod", mod_part.reshape(2 * 16, S_mod))
    mods = jnp.concatenate([f2[2 * k].reshape(2, 16, S_mod) for k in range(N_CHIPS)], axis=-1)
    my = lax.dynamic_slice_in_dim(mods, dev, 1, axis=1)[:, 0]
    sh1, sc1, g1, sh2, sc2, g2 = [[my[l, k * D:(k + 1) * D] for l in range(2)] for k in range(6)]
    csh1, csc1 = mods[0, N_DEV, 0:D], mods[0, N_DEV, D:2 * D]

    def full_weight(n, own, landed):
        return _unshard(_fill_own(landed, own, chip, True), SHARD_AXIS[n])

    xl = x[0]
    hcat = jnp.concatenate([ctx[0], xl], axis=0)
    n1w0, n2w0, n1w1, n2w1 = _par(norm1_w[0]), _par(norm2_w[0]), _par(norm1_w[1]), _par(norm2_w[1])
    sc_seg = jnp.stack([csc1, sc1[0]])[:, None, :]
    sh_seg = jnp.stack([csh1, sh1[0]])[:, None, :]

    a0 = _rw_fwd("l0_modnorm1", _f_modnorm, [hcat], [n1w0, sc_seg, sh_seg], [D], seg_rows=(Lc,), out_dtypes=[BF16])
    (own_in,), (landed_in,) = _exchange4_wait("gather_w_in_wait", gather_a, a0)
    w_in = full_weight("ssd_w_in", own_in, landed_in)[0]
    rest = [n for n in BIG if n != "ssd_w_in"]
    gather_b, token = _exchange4_start("gather_rest_start", [shard_b[n] for n in rest], True, landed_in)
    a0 = _tie("tie_gather_rest", a0, token)
    proj = _mm(a0, w_in, name="l0_w_in")
    seg_taps = [(k - KS // 2, ("seg", Lc)) for k in range(KS)]
    xbc_pre, xbc = _conv_fwd("l0_conv", proj, DI, CD, Wf["ssd_conv_w"][0], ssd_conv_b, seg_taps, act=True)
    dt_raw = proj[:, DI + CD:]
    dt_bias = _par(ssd_dt_bias.reshape(1, 2 * H))
    dt = _rw_fwd("l0_softplus", _f_softplus, [dt_raw], [dt_bias], [2 * H])
    dt_t = dt.T
    dtr = (dt_t[:H, None, :], dt_t[H:, None, :])
    a_all = -jnp.exp(ssd_a_log.reshape(2, H, 1, 1))
    a_neg = (a_all[0], a_all[1])
    (y_f, y_b), s_enter = _ssd_fwd(xbc, DI, DI + G * N, dtr, a_neg, P, ncc)
    gate_rows = [y_f, y_b, (xbc, 0, DI, Lc), (proj, 0, DI, Lc)]
    d_rep = _par(jnp.repeat(ssd_d[0], P))
    ssd_nw = _par(ssd_norm_w[0])
    yn = _rw_fwd("l0_ssd_gate", _f_ssd_gate, gate_rows, [d_rep, ssd_nw], [DI], T=L, out_dtypes=[BF16])
    Wb = {n: full_weight(n, own, g) for n, own, g in zip(rest, *_exchange4_wait("gather_rest_wait", gather_b, yn))}
    w_out, w_pw1, w_pw2 = Wb["ssd_w_out"][0], Wb["conf_w_pw1"][0], Wb["conf_w_pw2"][0]
    w_up, w_dn = Wb["ffn_w_up"], Wb["ffn_w_down"]
    mix0 = _mm(yn, w_out, name="l0_w_out")
    g1_0, g2_0, g1_1, g2_1 = _par(g1[0]), _par(g2[0]), _par(g1[1]), _par(g2[1])
    h1 = _rw_fwd("l0_res1", _f_gate_res, [xl, mix0], [g1_0], [D])

    grid_taps = [((i - 1) * GRID_W + (j - 1), (None if j == 1 else ("col", j - 1))) for i in range(3) for j in range(3)]

    def ffn_fwd(l, h, tag):
        a = _rw_fwd(tag + "_modnorm2", _f_modnorm, [h], [_par(norm2_w[l]), _par(sc2[l]), _par(sh2[l])], [D],
                    out_dtypes=[BF16])
        hh = _mm(a, w_up[l], name=tag + "_w_up")
        gc = _conv_fwd(tag + "_ffn_conv", hh, FH, FH, Wf["ffn_conv_w"][l].reshape(9, FH), ffn_conv_b[l][None, :],
                       grid_taps)
        act = _rw_fwd(tag + "_act", _f_ffn_act, [(hh, 0, FH), gc], [], [FH], col_tile=_tile(FH, 1536),
                      out_dtypes=[BF16])
        dn = _mm(act, w_dn[l], name=tag + "_w_down")
        return a, hh, gc, act, dn

    a1, hh0, gc0, act0, dn0 = ffn_fwd(0, h1, "l0")
    h2 = _rw_fwd("l0_res2", _f_gate_res, [h1, dn0], [g2_0], [D])

    a2 = _rw_fwd("l1_modnorm1", _f_modnorm, [h2], [n1w1, _par(sc1[1]), _par(sh1[1])], [D], out_dtypes=[BF16])
    pw = _mm(a2, w_pw1, name="l1_pw1")
    b_pw1 = Wf["conf_b_pw1"][0]
    glu = _rw_fwd("l1_glu", _f_glu, [(pw, 0, D), (pw, D, D)], [_par(b_pw1[:D]), _par(b_pw1[D:])], [D])
    conf_taps = [(k - KC // 2, None) for k in range(KC)]
    cv = _conv_fwd("l1_conv", glu, 0, D, Wf["conf_w_dw"][0], Wf["conf_b_dw"], conf_taps)
    ln_w, ln_b = _par(Wf["conf_ln_w"][0]), _par(Wf["conf_ln_b"][0])
    ls = _rw_fwd("l1_ln_silu", _f_ln_silu, [cv], [ln_w, ln_b], [D], out_dtypes=[BF16])
    p2 = _mm(ls, w_pw2, name="l1_pw2")
    b_pw2 = _par(Wf["conf_b_pw2"][0])
    h3 = _rw_fwd("l1_res1", _f_gate_res_bias, [h2, p2], [g1_1, b_pw2], [D])
    a3, hh1, gc1, act1, dn1 = ffn_fwd(1, h3, "l1")
    h4 = _rw_fwd("l1_res2", _f_gate_res, [h3, dn1], [g2_1], [D])

    fnw = final_norm_w[None, :]
    tgt = loss_target[0]
    loss_local = _loss_fwd(h4, tgt, fnw)[0, 0]
    loss = lax.psum(loss_local, ("x", "y", "c"))

    G_full = {}
    reduces = {}

    def start_reduce(tag, items, dep):
        def blocks_of(g, ax):
            if g.ndim == 3:
                return g
            return g.reshape(N_CHIPS, g.shape[0] // N_CHIPS, g.shape[1]) if ax == 0 else _to_blocks(g, ax)

        blocks = [blocks_of(g, ax).astype(BF16) for _, g, ax in items]
        handle, tok = _exchange4_start("reduce_" + tag + "_start", blocks, False, dep)
        reduces[tag] = ([n for n, _, _ in items], handle)
        return tok
    ones = jnp.ones((L, 1), F32)
    (dh4,), (dfnw,) = _rw_bwd("loss_bwd", _f_loss_rows, [h4, tgt], [_par(final_norm_w)], [ones],
                              row_grad=[True, False], par_grad=[True])
    G_full["final_norm_w"] = dfnw.reshape(D)

    def ffn_bwd(l, h, saved, g2_l, dh_out, tag):
        a, hh, gc, act, dn = saved
        (ddn,), (dg2,) = _rw_bwd(tag + "_res2_bwd", _f_gate_res, [h, dn], [g2_l], [dh_out],
                                 row_grad=[False, True], par_grad=[True], row_dtypes=[BF16])
        dact = _mm(ddn, w_dn[l], tb=True, name=tag + "_w_down_dx")
        dwdn = _mm(act, ddn, ta=True, name=tag + "_w_down_dw", out_dtype=BF16)
        (dval, dgc), _ = _rw_bwd(tag + "_act_bwd", _f_ffn_act, [(hh, 0, FH), gc], [], [dact],
                                 row_grad=[True, True], par_grad=[], col_tile=_tile(FH, 1536), row_dtypes=[BF16, F32])
        dgin, dcw, dcb = _conv_bwd(tag + "_ffn_conv_bwd", hh, FH, FH, Wf["ffn_conv_w"][l].reshape(9, FH), dgc,
                                   grid_taps, du_dtype=BF16)
        dhh = jnp.concatenate([dval, dgin], axis=1)
        da = _mm(dhh, w_up[l], tb=True, name=tag + "_w_up_dx")
        dwup = _mm(a, dhh, ta=True, name=tag + "_w_up_dw", out_dtype=BF16, col_blocks=N_CHIPS)
        (dh,), (dn2w, dsc2, dsh2) = _rw_bwd(
            tag + "_modnorm2_bwd", _f_modnorm, [h], [_par(norm2_w[l]), _par(sc2[l]), _par(sh2[l])], [da],
            row_grad=[True], par_grad=[True, True, True], add=dh_out)
        return dh, dict(w_down=dwdn, w_up=dwup, conv_w=dcw.reshape(3, 3, FH), conv_b=dcb.reshape(FH),
                        n2w=dn2w.reshape(D), sc2=dsc2.reshape(D), sh2=dsh2.reshape(D), g2=dg2.reshape(D))

    dh3, gf1 = ffn_bwd(1, h3, (a3, hh1, gc1, act1, dn1), g2_1, dh4, "l1")
    (dp2,), (dg1_1, db_pw2) = _rw_bwd("l1_res1_bwd", _f_gate_res_bias, [h2, p2], [g1_1, b_pw2], [dh3],
                                      row_grad=[False, True], par_grad=[True, True], row_dtypes=[BF16])
    dls = _mm(dp2, w_pw2, tb=True, name="l1_pw2_dx")
    dw_pw2 = _mm(ls, dp2, ta=True, name="l1_pw2_dw", out_dtype=BF16)
    (dcv,), (dln_w, dln_b) = _rw_bwd("l1_ln_silu_bwd", _f_ln_silu, [cv], [ln_w, ln_b], [dls],
                                     row_grad=[True], par_grad=[True, True])
    dglu, dw_dw, db_dw = _conv_bwd("l1_conv_bwd", glu, 0, D, Wf["conf_w_dw"][0], dcv, conf_taps)
    (dpa, dpg), (dba, dbg) = _rw_bwd("l1_glu_bwd", _f_glu, [(pw, 0, D), (pw, D, D)],
                                     [_par(b_pw1[:D]), _par(b_pw1[D:])], [dglu],
                                     row_grad=[True, True], par_grad=[True, True], row_dtypes=[BF16, BF16])
    dpw = jnp.concatenate([dpa, dpg], axis=1)
    da2 = _mm(dpw, w_pw1, tb=True, name="l1_pw1_dx")
    dw_pw1 = _mm(a2, dpw, ta=True, name="l1_pw1_dw", out_dtype=BF16, col_blocks=N_CHIPS)
    (dh2,), (dn1w1, dsc1_1, dsh1_1) = _rw_bwd(
        "l1_modnorm1_bwd", _f_modnorm, [h2], [n1w1, _par(sc1[1]), _par(sh1[1])], [da2],
        row_grad=[True], par_grad=[True, True, True], add=dh3)
    G_full["conf_b_pw2"] = db_pw2.reshape(1, D)
    G_full["conf_ln_w"], G_full["conf_ln_b"] = dln_w.reshape(1, D), dln_b.reshape(1, D)
    G_full["conf_w_dw"], G_full["conf_b_dw"] = dw_dw[None], db_dw.reshape(1, D)
    G_full["conf_b_pw1"] = jnp.concatenate([dba.reshape(1, D), dbg.reshape(1, D)], axis=1)

    token = start_reduce("l1", [("conf_w_pw2", dw_pw2, 0), ("conf_w_pw1", dw_pw1, 1), ("ffn_w_up1", gf1["w_up"], 1),
                                ("ffn_w_down1", gf1["w_down"], 0)], dw_pw2)
    dh2 = _tie("tie_reduce_l1", dh2, token)
    dh1, gf0 = ffn_bwd(0, h1, (a1, hh0, gc0, act0, dn0), g2_0, dh2, "l0")
    G_full["ffn_conv_w"] = jnp.stack([gf0["conv_w"], gf1["conv_w"]])
    G_full["ffn_conv_b"] = jnp.stack([gf0["conv_b"], gf1["conv_b"]])

    (dmix,), (dg1_0,) = _rw_bwd("l0_res1_bwd", _f_gate_res, [xl, mix0], [g1_0], [dh1],
                                row_grad=[False, True], par_grad=[True], row_dtypes=[BF16])
    dyn = _mm(dmix, w_out, tb=True, name="l0_w_out_dx")
    dw_out = _mm(yn, dmix, ta=True, name="l0_w_out_dw", out_dtype=BF16)
    token = start_reduce("l0", [("ffn_w_up0", gf0["w_up"], 1), ("ffn_w_down0", gf0["w_down"], 0),
                                ("ssd_w_out", dw_out, 0)], dw_out)
    dyn = _tie("tie_reduce_l0", dyn, token)
    (dy_lat, dxs_gate, dz_lat), (dd_rep, dssd_nw) = _rw_bwd(
        "l0_ssd_gate_bwd", _f_ssd_gate, gate_rows, [d_rep, ssd_nw], [dyn],
        row_grad=[True, False, True, True], par_grad=[True, True], T=L, row_dtypes=[F32, F32, BF16])
    g_f, g_b = _ssd_bwd(xbc, DI, DI + G * N, dtr, a_neg, s_enter, dy_lat, P, ncc)
    silu_bwd = functools.partial(_rw_bwd, f=_silu, pars=[], row_grad=[True], par_grad=[], T=T0)
    (dxs_pre,), _ = silu_bwd("l0_silu_bwd_x", rows=[(xbc_pre, 0, DI)], cot_fn=lambda p, q, r: p + q + r,
                             cots=[g_f[0], g_b[0], (dxs_gate, 0, DI, -Lc)],
                             col_tile=_tile(DI, 1024))
    (db_pre,), _ = silu_bwd("l0_silu_bwd_b", rows=[(xbc_pre, DI, G * N)], cot_fn=lambda p, q: p + q,
                            cots=[g_f[1], g_b[1]], col_tile=_tile(G * N, 1024))
    (dc_pre,), _ = silu_bwd("l0_silu_bwd_c", rows=[(xbc_pre, DI + G * N, G * N)], cot_fn=lambda p, q: p + q,
                            cots=[g_f[2], g_b[2]], col_tile=_tile(G * N, 1024))
    conv_w0 = Wf["ssd_conv_w"][0]
    pieces = []
    for tag, off, width, g_pre in (("x", 0, DI, dxs_pre), ("b", DI, G * N, db_pre), ("c", DI + G * N, G * N, dc_pre)):
        pieces.append(_conv_bwd("l0_conv_bwd_" + tag, proj, DI + off, width, conv_w0[:, off:off + width], g_pre,
                                seg_taps, du_dtype=BF16))
    dconv_in = [p[0] for p in pieces]
    dcw0 = jnp.concatenate([p[1] for p in pieces], axis=1)
    dcb0 = jnp.concatenate([p[2] for p in pieces], axis=1)
    ddt = jnp.concatenate([g_f[3][:, 0, :].T, g_b[3][:, 0, :].T], axis=1)
    (ddt_raw,), (ddt_bias,) = _rw_bwd("l0_softplus_bwd", _f_softplus, [dt_raw], [dt_bias], [ddt],
                                      row_grad=[True], par_grad=[True], row_dtypes=[BF16])
    dproj = jnp.concatenate([jnp.pad(dz_lat, ((Lc, 0), (0, 0))), *dconv_in, ddt_raw], axis=1)
    da0 = _mm(dproj, w_in, tb=True, name="l0_w_in_dx")
    dw_in = _mm(a0, dproj, ta=True, name="l0_w_in_dw", out_dtype=BF16)
    token = start_reduce("in", [("ssd_w_in", dw_in, 1)], dw_in)
    da0 = _tie("tie_reduce_in", da0, token)
    (dhcat,), (dn1w0, dsc_seg, dsh_seg) = _rw_bwd(
        "l0_modnorm1_bwd", _f_modnorm, [hcat], [n1w0, sc_seg, sh_seg], [da0],
        row_grad=[True], par_grad=[True, True, True], seg_rows=(Lc,))
    grad_x = (dhcat[Lc:] + dh1)[None]

    da_heads = jnp.stack([g[4][:, 0, 0].reshape(G, T0 // SSD_CHUNK, H // G).sum(axis=1).reshape(H)
                          for g in (g_f, g_b)])[None]
    G_full["ssd_a_log"] = da_heads * (-jnp.exp(ssd_a_log))
    G_full["ssd_dt_bias"] = ddt_bias.reshape(1, 2, H)
    G_full["ssd_d"] = dd_rep.reshape(H, P).sum(axis=1)[None]
    G_full["ssd_norm_w"] = dssd_nw.reshape(1, DI)
    G_full["ssd_conv_w"], G_full["ssd_conv_b"] = dcw0[None], dcb0.reshape(1, CD)
    G_full["norm1_w"] = jnp.stack([dn1w0.reshape(D), dn1w1.reshape(D)])
    G_full["norm2_w"] = jnp.stack([gf0["n2w"], gf1["n2w"]])

    zD = jnp.zeros((D,), F32)
    dm_own = jnp.stack([
        jnp.concatenate([dsh_seg[1, 0], dsc_seg[1, 0], dg1_0.reshape(D), gf0["sh2"], gf0["sc2"], gf0["g2"]]),
        jnp.concatenate([dsh1_1.reshape(D), dsc1_1.reshape(D), dg1_1.reshape(D), gf1["sh2"], gf1["sc2"], gf1["g2"]]),
    ])
    dmc_own = jnp.concatenate([dsh_seg[0, 0], dsc_seg[0, 0], zD, zD, zD, zD])

    out = {}

    def finish_reduce(tags, after, swap_name):
        partial = {}
        for tag in tags:
            names, handle = reduces[tag]
            blocks, landed = _exchange4_wait("reduce_" + tag + "_wait", handle, after)
            for n, blk, own in zip(names, landed, blocks):
                r = _fill_own(blk, own, chip, False)
                partial[n] = _sum_leading("sum4_" + n, r.reshape(N_CHIPS, -1, r.shape[-1]),
                                          (0, 1, 2, 3)).reshape(r.shape[1:])
        for n in ("ffn_w_up", "ffn_w_down"):
            if n + "0" in partial:
                partial[n] = jnp.stack([partial.pop(n + "0"), partial.pop(n + "1")])
        names = [n for n in BIG if n in partial]
        mine = [partial[n].reshape(W[n].shape) for n in names]
        for n, own, sib in zip(names, mine, _swap_sibling(swap_name, mine)):
            out[n] = _adamw("adamw_" + n, W[n], Mo[n], Vo[n], own, sib)
        return names

    early = finish_reduce(["l1", "l0"], dhcat, "swap_grads_early")

    small_sum_names = [n for n in SMALL if n not in ("c_ctx", "mod_b")]
    sum_part = [G_full[n] for n in small_sum_names] + [dmc_own]
    n_sum = sum(int(a.size) for a in sum_part)
    packed = _tie("tie_small_grads", _pack(sum_part + [dm_own]), out[early[-1]][1])
    gat = _allgather8("gather_small_grads", packed)
    total = _sum_leading("sum_small_grads", gat, tuple(range(N_DEV)))
    summed = _unpack(total, [a.shape for a in sum_part])
    Gs = dict(zip(small_sum_names, summed[:-1]))
    dmc_tot = summed[-1]
    dm_all = jnp.stack([gat[k].reshape(-1)[n_sum:n_sum + 2 * 6 * D].reshape(2, 6 * D) for k in range(N_DEV)], axis=1)
    dm16 = jnp.concatenate([dm_all, jnp.stack([dmc_tot, jnp.zeros_like(dmc_tot)])[:, None, :],
                            jnp.zeros((2, 16 - N_DEV - 1, 6 * D), F32)], axis=1)
    Gs["mod_b"] = _sum_leading("sum_mod_b", dm16.transpose(1, 0, 2).reshape(16, 2 * 6 * D // LANE, LANE),
                               tuple(range(N_DEV + 1))).reshape(2, 6 * D)

    dm16_shard = lax.dynamic_slice_in_dim(dm16, chip * S_mod, S_mod, axis=2)
    ds16 = _mm(dm16_shard[0], mod_w[0], tb=True, precision=HIGHEST, name="c_ctx_dx")
    sig = jax.nn.sigmoid(c_ctx)
    dcc_part = ds16[N_DEV] * (sig * (1.0 + c_ctx * (1.0 - sig)))
    gat_cc = _allgather8("gather_c_ctx_grad", _pack([dcc_part]))
    Gs["c_ctx"] = _sum_leading("sum_c_ctx_grad", gat_cc, (0, 2, 4, 6)).reshape(-1)[:D]

    s16t = _silu(c16).T
    out["mod_w"] = _mod_w_update(s16t, dm16_shard, mod_w, m_mod_w, v_mod_w)
    finish_reduce(["in"], out["mod_w"][0], "swap_grads_late")

    def own(n, full):
        if n in SHARD_AXIS:
            size = W[n].shape[SHARD_AXIS[n]]
            return lax.dynamic_slice_in_dim(full, chip * size, size, axis=SHARD_AXIS[n])
        return full

    g_small = [own(n, Gs[n].reshape(Wf[n].shape)) for n in SMALL]
    shapes = [W[n].shape for n in SMALL]
    pk = [_pack([W[n] for n in SMALL]), _pack([Mo[n] for n in SMALL]), _pack([Vo[n] for n in SMALL]), _pack(g_small)]
    res = _adamw("adamw_small", pk[0], pk[1], pk[2], pk[3], jnp.zeros_like(pk[3]))
    unpacked = [_unpack(r, shapes) for r in res]
    for k, n in enumerate(SMALL):
        out[n] = tuple(u[k] for u in unpacked)

    grads = [out[n][0] for n in WEIGHTS]
    deltas = [out[n][1] for n in WEIGHTS]
    new_m = [out[n][2] for n in WEIGHTS]
    new_v = [out[n][3] for n in WEIGHTS]
    return (loss, grad_x, *grads, *deltas, *new_m, *new_v)
```

```python
import functools

import jax
import jax.numpy as jnp
from jax import lax
from jax.experimental import pallas as pl
from jax.experimental.pallas import tpu as pltpu

F32 = jnp.float32
BF16 = jnp.bfloat16
MESH = pl.DeviceIdType.MESH
HIGHEST = lax.Precision.HIGHEST

VMEM_LIMIT_BYTES = 48 * 1024 * 1024
LANE = 128
SUBLANE = 8

SSD_STATE = 128
SSD_CHUNK = 128
GRID_W = 64
EPS = 1e-6
N_CHIPS = 4
N_DEV = 8

ADAM_LR = 0.001
ADAM_B1 = 0.9
ADAM_B2 = 0.999
ADAM_EPS = 1e-08
ADAM_WD = 0.01
ADAM_STEP = 10


def _pcall(body, **kw):
    return pl.pallas_call(body, **kw)


def _cparams(n_grid):
    return pltpu.CompilerParams(dimension_semantics=("arbitrary",) * n_grid, vmem_limit_bytes=VMEM_LIMIT_BYTES)


def _cdiv(a, b):
    return -(-a // b)


def _round_up(a, b):
    return _cdiv(a, b) * b


def _tile(n, cap):
    if n <= cap:
        return n
    best = None
    for t in range(LANE, cap + 1, LANE):
        if n % t == 0:
            best = t
    if best is None:
        npad = _round_up(n, LANE)
        for t in range(LANE, cap + 1, LANE):
            if npad % t == 0:
                best = t
    return best


def _row_tile(n, cap, also=()):
    best = None
    for step in (2 * SUBLANE, SUBLANE):
        for t in range(step, min(cap, n) + 1, step):
            if n % t == 0 and all(a % t == 0 for a in also):
                best = t
        if best is not None:
            break
    assert best is not None, (n, cap, also)
    return best


def _silu(v):
    return v * jax.nn.sigmoid(v)


def _mm(a, b, *, name, ta=False, tb=False, precision=None, cap=1024, out_dtype=F32, col_blocks=None):
    M, K = (a.shape[1], a.shape[0]) if ta else a.shape
    N = b.shape[0] if tb else b.shape[1]
    assert K == (b.shape[1] if tb else b.shape[0]), (a.shape, b.shape, ta, tb)
    tm, tk = _tile(M, cap), _tile(K, cap + cap // 2)
    tn = _tile(N if col_blocks is None else N // col_blocks, cap + cap // 2)
    nm, nn, nk = _cdiv(M, tm), _cdiv(N, tn), _cdiv(K, tk)
    k_tail = K % tk
    exact = precision is not None

    def body(a_ref, b_ref, o_ref, acc_ref):
        k = pl.program_id(2)

        @pl.when(k == 0)
        def _():
            acc_ref[...] = jnp.zeros_like(acc_ref)

        av = a_ref[...]
        bv = b_ref[...]
        if k_tail:
            lim = K - k * tk
            ka = lax.broadcasted_iota(jnp.int32, av.shape, 0 if ta else 1)
            kb = lax.broadcasted_iota(jnp.int32, bv.shape, 1 if tb else 0)
            av = jnp.where(ka < lim, av, jnp.zeros_like(av))
            bv = jnp.where(kb < lim, bv, jnp.zeros_like(bv))
        if exact:
            av = av.astype(F32)
            bv = bv.astype(F32)
        else:
            av = av.astype(BF16)
            bv = bv.astype(BF16)
        dn = (((0 if ta else 1,), (1 if tb else 0,)), ((), ()))
        acc_ref[...] += lax.dot_general(av, bv, dn, preferred_element_type=F32, precision=precision)

        @pl.when(k == nk - 1)
        def _():
            o_ref[...] = acc_ref[...].astype(o_ref.dtype)

    a_spec = pl.BlockSpec((tk, tm), lambda i, j, k: (k, i)) if ta else pl.BlockSpec((tm, tk), lambda i, j, k: (i, k))
    b_spec = pl.BlockSpec((tn, tk), lambda i, j, k: (j, k)) if tb else pl.BlockSpec((tk, tn), lambda i, j, k: (k, j))
    if col_blocks is None:
        out_spec = pl.BlockSpec((tm, tn), lambda i, j, k: (i, j))
        out_shape = jax.ShapeDtypeStruct((M, N), out_dtype)
    else:
        per = (N // col_blocks) // tn
        assert per * tn * col_blocks == N, (N, col_blocks, tn)
        out_spec = pl.BlockSpec((None, tm, tn), lambda i, j, k: (j // per, i, j % per))
        out_shape = jax.ShapeDtypeStruct((col_blocks, M, N // col_blocks), out_dtype)
    return _pcall(
        body, name=name, grid=(nm, nn, nk), in_specs=[a_spec, b_spec], out_specs=out_spec, out_shape=out_shape,
        scratch_shapes=[pltpu.VMEM((tm, tn), F32)], compiler_params=_cparams(3),
    )(a, b)


def _norm_rows(rows):
    out = []
    for r in rows:
        if not isinstance(r, tuple):
            r = (r,)
        arr, off, width, roff = (r + (0, None, 0)[len(r) - 1:])
        out.append((arr, off, width if width is not None else arr.shape[1], roff))
    return out


def _rw_plan(T, rows, pars, seg_rows, col_tile, tm_cap):
    widths = [r[2] for r in rows]
    wmax = max(widths + [p.shape[-1] for p in pars] + [1])
    if col_tile is not None:
        assert all(w == widths[0] for w in widths) and all(p.shape[-1] == widths[0] for p in pars)
        ncol = widths[0] // col_tile
        assert ncol * col_tile == widths[0]
        wmax = col_tile
    else:
        ncol = 1
    cap = tm_cap if tm_cap is not None else max(SUBLANE, min(256, (256 * 1024) // wmax))
    tm = _row_tile(T, cap, also=tuple(seg_rows) + tuple(abs(r[3]) for r in rows if r[3]))
    bounds = tuple(s // tm for s in seg_rows)
    return widths, ncol, tm, bounds


def _rw_specs(rows, pars, ncol, tm, bounds, col_tile):
    def seg(i):
        s = 0
        for b in bounds:
            s = s + (i >= b).astype(jnp.int32)
        return s

    specs = []
    for arr, off, w, roff in rows:
        bw = col_tile if col_tile is not None else w
        assert off % bw == 0 and roff % tm == 0, (off, bw, roff, tm)
        specs.append(pl.BlockSpec((tm, bw), functools.partial(lambda j, i, ob, rb: (jnp.maximum(i + rb, 0), ob + j),
                                                              ob=off // bw, rb=roff // tm)))
    for p in pars:
        bw = col_tile if col_tile is not None else p.shape[-1]
        if p.shape[0] > 1:
            specs.append(pl.BlockSpec((None, 1, bw), lambda j, i: (seg(i), 0, j)))
        else:
            specs.append(pl.BlockSpec((None, 1, bw), lambda j, i: (0, 0, j)))
    return specs, seg


def _rw_fwd(name, f, rows, pars, out_widths, *, T=None, seg_rows=(), col_tile=None, tm_cap=None, out_dtypes=None):
    rows = _norm_rows(rows)
    T = rows[0][0].shape[0] if T is None else T
    widths, ncol, tm, bounds = _rw_plan(T, rows, pars, seg_rows, col_tile, tm_cap)
    in_specs, _ = _rw_specs(rows, pars, ncol, tm, bounds, col_tile)
    nr, npar, nout = len(rows), len(pars), len(out_widths)

    def body(*refs):
        vals = [r[...] for r in refs[:nr + npar]]
        outs = f(*vals)
        if not isinstance(outs, (tuple, list)):
            outs = (outs,)
        for o_ref, o in zip(refs[nr + npar:], outs):
            o_ref[...] = o.astype(o_ref.dtype)

    out_specs = [pl.BlockSpec((tm, col_tile if col_tile is not None else w), lambda j, i: (i, j)) for w in out_widths]
    res = _pcall(
        body, name=name, grid=(ncol, T // tm), in_specs=in_specs, out_specs=out_specs,
        out_shape=[jax.ShapeDtypeStruct((T, w), dt) for w, dt in zip(out_widths, out_dtypes or [F32] * nout)],
        compiler_params=_cparams(2),
    )(*[r[0] for r in rows], *pars)
    return res if nout > 1 else res[0]


def _rw_bwd(name, f, rows, pars, cots, *, row_grad, par_grad, T=None, seg_rows=(), col_tile=None, tm_cap=None,
            add=None, cot_fn=None, row_dtypes=None):
    rows = _norm_rows(rows)
    cots = _norm_rows(cots)
    T = rows[0][0].shape[0] if T is None else T
    extra = _norm_rows([add]) if add is not None else []
    all_rows = rows + cots + extra
    widths, ncol, tm, bounds = _rw_plan(T, all_rows, pars, seg_rows, col_tile, tm_cap)
    in_specs, seg = _rw_specs(all_rows, pars, ncol, tm, bounds, col_tile)
    nr, nc, ne, npar = len(rows), len(cots), len(extra), len(pars)
    row_idx = [k for k in range(nr) if row_grad[k]]
    par_idx = [k for k in range(npar) if par_grad[k]]

    def body(*refs):
        i = pl.program_id(1)
        row_vals = [r[...] for r in refs[:nr]]
        cot_vals = [r[...] for r in refs[nr:nr + nc]]
        cot_vals = [jnp.where(i + c[3] // tm >= 0, v, jnp.zeros_like(v)) if c[3] < 0 else v
                    for v, c in zip(cot_vals, cots)]
        add_vals = [r[...] for r in refs[nr + nc:nr + nc + ne]]
        par_vals = [r[...] for r in refs[nr + nc + ne:nr + nc + ne + npar]]
        out_refs = refs[nr + nc + ne + npar:]
        outs, vjp = jax.vjp(f, *row_vals, *par_vals)
        if cot_fn is not None:
            cot_vals = cot_fn(*cot_vals)
            if not isinstance(cot_vals, (tuple, list)):
                cot_vals = (cot_vals,)
        if isinstance(outs, (tuple, list)):
            grads = vjp(tuple(c.astype(o.dtype) for c, o in zip(cot_vals, outs)))
        else:
            grads = vjp(cot_vals[0].astype(outs.dtype))
        first_seg = i == 0
        for b in bounds:
            first_seg = first_seg | (i == b)
        for n, k in enumerate(row_idx):
            g = grads[k]
            if n == 0 and add_vals:
                g = g + add_vals[0]
            out_refs[n][...] = g.astype(out_refs[n].dtype)
        for n, k in enumerate(par_idx):
            g = grads[nr + k]
            o_ref = out_refs[len(row_idx) + n]
            first = first_seg if pars[k].shape[0] > 1 else (i == 0)

            @pl.when(first)
            def _(o_ref=o_ref, g=g):
                o_ref[...] = g

            @pl.when(jnp.logical_not(first))
            def _(o_ref=o_ref, g=g):
                o_ref[...] += g

    out_specs, out_shape = [], []
    for k in row_idx:
        w = widths[k]
        out_specs.append(pl.BlockSpec((tm, col_tile if col_tile is not None else w), lambda j, i: (i, j)))
        out_shape.append(jax.ShapeDtypeStruct((T, w), row_dtypes[len(out_shape)] if row_dtypes else F32))
    for k in par_idx:
        p = pars[k]
        bw = col_tile if col_tile is not None else p.shape[-1]
        if p.shape[0] > 1:
            out_specs.append(pl.BlockSpec((None, 1, bw), lambda j, i: (seg(i), 0, j)))
        else:
            out_specs.append(pl.BlockSpec((None, 1, bw), lambda j, i: (0, 0, j)))
        out_shape.append(jax.ShapeDtypeStruct(p.shape, F32))
    res = _pcall(
        body, name=name, grid=(ncol, T // tm), in_specs=in_specs, out_specs=out_specs, out_shape=out_shape,
        compiler_params=_cparams(2),
    )(*[r[0] for r in all_rows], *pars)
    return list(res[:len(row_idx)]), list(res[len(row_idx):])


def _f_modnorm(h, w, sc, sh):
    y = h * lax.rsqrt(jnp.mean(h * h, axis=-1, keepdims=True) + EPS)
    return (y * w) * (1.0 + sc) + sh


def _f_gate_res(h, y, g):
    return h + g * y


def _f_gate_res_bias(h, y, g, b):
    return h + g * (y + b)


def _f_ffn_act(val, gate):
    return _silu(gate) * val


def _f_softplus(raw, bias):
    v = raw + bias
    return jnp.maximum(v, 0.0) + jnp.log(1.0 + jnp.exp(-jnp.abs(v)))


def _f_ssd_gate(yf, yb, xs, z, d_rep, nw):
    y = (yf + yb + d_rep * xs) * _silu(z)
    return (y * lax.rsqrt(jnp.mean(y * y, axis=-1, keepdims=True) + EPS)) * nw


def _f_glu(a, g, ba, bg):
    return (a + ba) * jax.nn.sigmoid(g + bg)


def _f_ln_silu(h, w, b):
    mu = jnp.mean(h, axis=-1, keepdims=True)
    d = h - mu
    y = d * lax.rsqrt(jnp.mean(d * d, axis=-1, keepdims=True) + EPS)
    return _silu(y * w + b)


def _f_loss_rows(h, t, w):
    y = (h * lax.rsqrt(jnp.mean(h * h, axis=-1, keepdims=True) + EPS)) * w
    e = y - t
    return 0.5 * jnp.mean(e * e, axis=-1, keepdims=True)


def _f_adamw(w, m, v, ga, gb):
    g = ga + gb
    m = ADAM_B1 * m + (1.0 - ADAM_B1) * g
    v = ADAM_B2 * v + (1.0 - ADAM_B2) * (g * g)
    m_hat = m / (1.0 - ADAM_B1 ** ADAM_STEP)
    v_hat = v / (1.0 - ADAM_B2 ** ADAM_STEP)
    delta = -ADAM_LR * (m_hat / (jnp.sqrt(v_hat) + ADAM_EPS) + ADAM_WD * w)
    return g, delta, m, v


def _adamw(name, w, m, v, ga, gb):
    shape = w.shape
    c = shape[-1]
    two_d = [t.reshape(-1, c) for t in (w, m, v, ga, gb)]
    rows = two_d[0].shape[0]
    pad = _round_up(rows, SUBLANE) - rows
    if pad:
        two_d = [jnp.pad(t, ((0, pad), (0, 0))) for t in two_d]
    outs = _rw_fwd(name, _f_adamw, two_d, [], [c] * 4)
    return tuple(o[:rows].reshape(shape) for o in outs)


def _sum_leading(name, x, idxs):
    _, R, C = x.shape
    tm = _row_tile(R, max(SUBLANE, min(512, (512 * 1024) // C)))

    def body(x_ref, o_ref):
        acc = x_ref[idxs[0]].astype(F32)
        for k in idxs[1:]:
            acc = acc + x_ref[k].astype(F32)
        o_ref[...] = acc

    return _pcall(
        body, name=name, grid=(R // tm,), in_specs=[pl.BlockSpec((x.shape[0], tm, C), lambda i: (0, i, 0))],
        out_specs=pl.BlockSpec((tm, C), lambda i: (i, 0)), out_shape=jax.ShapeDtypeStruct((R, C), F32),
        compiler_params=_cparams(1),
    )(x)


def _loss_fwd(h, t, w):
    T, D = h.shape
    tm = _row_tile(T, 256)

    def body(h_ref, t_ref, w_ref, o_ref):
        i = pl.program_id(0)
        part = jnp.sum(_f_loss_rows(h_ref[...], t_ref[...], w_ref[...]), axis=0, keepdims=True)
        part = jnp.broadcast_to(part, (1, LANE))

        @pl.when(i == 0)
        def _():
            o_ref[...] = part

        @pl.when(i > 0)
        def _():
            o_ref[...] += part

    return _pcall(
        body, name="loss_fwd", grid=(T // tm,),
        in_specs=[pl.BlockSpec((tm, D), lambda i: (i, 0)), pl.BlockSpec((tm, D), lambda i: (i, 0)),
                  pl.BlockSpec((1, D), lambda i: (0, 0))],
        out_specs=pl.BlockSpec((1, LANE), lambda i: (0, 0)), out_shape=jax.ShapeDtypeStruct((1, LANE), F32),
        compiler_params=_cparams(1),
    )(h, t, w)


CONV_ROWS = 256
CONV_ROWS_FEW_TAPS = 1024
CONV_ACC_ELEMS = 16384


def _col_mask(arg, t):
    col = jnp.bitwise_and(t, GRID_W - 1)
    return (col != 0) if arg < 0 else (col != GRID_W - 1)


def _conv_plan(T, C, taps):
    seg = [m[1] for _, m in taps if m is not None and m[0] == "seg"]
    boundary = seg[0] if seg else None
    cap = CONV_ROWS_FEW_TAPS if len(taps) <= 9 else CONV_ROWS
    rc = next(r for r in (1024, 768, 512, 256, LANE)
              if r <= cap and T % r == 0 and (boundary is None or boundary % r == 0))
    ct = next((t for t in (512, 256, LANE) if C % t == 0), C)
    reach = max(abs(s) for s, _ in taps)
    hb = next(h for h in (8, 16, 32, 64, 128, 256) if h >= reach and rc % h == 0)
    sub = max(2 * SUBLANE, min(rc, CONV_ACC_ELEMS // ct))
    taps = [(s, None if (m is None or m[0] == "seg") else m[1]) for s, m in taps]
    return rc, ct, hb, sub, T // rc, C // ct, boundary, taps


def _halo_specs(rc, ct, hb, T, off_blocks):
    per = rc // hb
    last = T // hb - 1
    prev = pl.BlockSpec((hb, ct), lambda j, i: (jnp.maximum(i * per - 1, 0), off_blocks + j))
    cur = pl.BlockSpec((rc, ct), lambda j, i: (i, off_blocks + j))
    nxt = pl.BlockSpec((hb, ct), lambda j, i: (jnp.minimum((i + 1) * per, last), off_blocks + j))
    return [prev, cur, nxt]


def _fill_halo(pad_ref, p_ref, c_ref, n_ref, i, nrc, rc, hb, boundary):
    has_prev = i > 0
    has_next = i < nrc - 1
    if boundary is not None:
        has_prev = has_prev & (i * rc != boundary)
        has_next = has_next & ((i + 1) * rc != boundary)
    pad_ref[0:hb, :] = jnp.where(has_prev, p_ref[...], 0.0)
    pad_ref[hb:hb + rc, :] = c_ref[...]
    pad_ref[hb + rc:hb + rc + hb, :] = jnp.where(has_next, n_ref[...], 0.0)


def _shift_plan(keys):
    count = {}
    for s, m in keys:
        k = (s % SUBLANE, m)
        count[k] = count.get(k, 0) + 1
    slots = {}
    for k, n in sorted(count.items(), key=lambda kv: (kv[0][0], str(kv[0][1]))):
        if k != (0, None) and (n >= 2 or k[1] is not None):
            slots[k] = len(slots)
    return slots


def _build_shifted(copies_ref, slots, pad_ref, keys, i, rc, hb, sub):
    for (r, m), slot in slots.items():
        qs = [s - r for s, mk in keys if (s % SUBLANE, mk) == (r, m)]
        lo, hi = hb + min(qs), hb + rc + max(qs)
        for p in range(lo, hi, sub):
            n = min(sub, hi - p)
            v = pad_ref[p + r:p + r + n, :]
            if m is not None:
                t = i * rc - hb + p + r + lax.broadcasted_iota(jnp.int32, (n, 1), 0)
                v = jnp.where(_col_mask(m, t), v, 0.0)
            copies_ref[slot, p:p + n, :] = v


def _read(copies_ref, slots, pad_ref, s, m, row, n):
    k = (s % SUBLANE, m)
    if k in slots:
        q = s - k[0]
        return copies_ref[slots[k], row + q:row + q + n, :]
    return pad_ref[row + s:row + s + n, :]


def _conv_fwd(name, u, col_off, C, w, b, taps, act=False):
    T = u.shape[0]
    rc, ct, hb, sub, nrc, ncc, boundary, taps = _conv_plan(T, C, taps)
    assert col_off % ct == 0
    K = len(taps)
    keys = [(s, None) for s, _ in taps]
    slots = _shift_plan(keys)
    dirs = sorted({m for _, m in taps if m is not None})

    def body(up, uc, un, w_ref, b_ref, *rest):
        y_ref = rest[0]
        pad_ref, copies_ref = rest[-2], rest[-1]
        i = pl.program_id(1)
        _fill_halo(pad_ref, up, uc, un, i, nrc, rc, hb, boundary)
        _build_shifted(copies_ref, slots, pad_ref, keys, i, rc, hb, sub)
        for r0 in range(0, rc, sub):
            acc = jnp.broadcast_to(b_ref[...], (sub, ct))
            for m in [None] + dirs:
                part = None
                for k, (s, mk) in enumerate(taps):
                    if mk != m:
                        continue
                    term = w_ref[k:k + 1, :] * _read(copies_ref, slots, pad_ref, s, None, hb + r0, sub)
                    part = term if part is None else part + term
                if part is None:
                    continue
                if m is not None:
                    t = i * rc + r0 + lax.broadcasted_iota(jnp.int32, (sub, 1), 0)
                    part = jnp.where(_col_mask(m, t), part, 0.0)
                acc = acc + part
            y_ref[r0:r0 + sub, :] = acc
            if act:
                rest[1][r0:r0 + sub, :] = _silu(acc)

    n_out = 2 if act else 1
    res = _pcall(
        body, name=name, grid=(ncc, nrc),
        in_specs=_halo_specs(rc, ct, hb, T, col_off // ct) + [pl.BlockSpec((K, ct), lambda j, i: (0, j)),
                                                              pl.BlockSpec((1, ct), lambda j, i: (0, j))],
        out_specs=[pl.BlockSpec((rc, ct), lambda j, i: (i, j))] * n_out,
        out_shape=[jax.ShapeDtypeStruct((T, C), F32)] * n_out,
        scratch_shapes=[pltpu.VMEM((rc + 2 * hb, ct), F32), pltpu.VMEM((max(len(slots), 1), rc + 2 * hb, ct), F32)],
        compiler_params=_cparams(2),
    )(u, u, u, w, b)
    return res if act else res[0]


def _conv_bwd(name, u, col_off, C, w, g, taps, du_dtype=F32):
    T = u.shape[0]
    rc, ct, hb, sub, nrc, ncc, boundary, taps = _conv_plan(T, C, taps)
    K = len(taps)
    u_keys = [(s, None) for s, _ in taps]
    dirs = sorted({m for _, m in taps if m is not None})
    g_keys = [(-s, m) for s, m in taps] + [(0, m) for m in dirs]
    u_slots, g_slots = _shift_plan(u_keys), _shift_plan(g_keys)

    def body(up, uc, un, gp, gc, gn, w_ref, du_ref, dw_ref, db_ref, upad, gpad, ucopies, gcopies):
        i = pl.program_id(1)
        _fill_halo(upad, up, uc, un, i, nrc, rc, hb, boundary)
        _fill_halo(gpad, gp, gc, gn, i, nrc, rc, hb, boundary)
        _build_shifted(ucopies, u_slots, upad, u_keys, i, rc, hb, sub)
        _build_shifted(gcopies, g_slots, gpad, g_keys, i, rc, hb, sub)

        @pl.when(i == 0)
        def _():
            dw_ref[...] = jnp.zeros_like(dw_ref)
            db_ref[...] = jnp.zeros_like(db_ref)

        def fold(v):
            return jnp.sum(v.reshape(sub // SUBLANE, SUBLANE, ct), axis=0)

        dbs = jnp.zeros((SUBLANE, ct), F32)
        for r0 in range(0, rc, sub):
            dbs = dbs + fold(gpad[hb + r0:hb + r0 + sub, :])
            acc = jnp.zeros((sub, ct), F32)
            for k, (s, m) in enumerate(taps):
                acc = acc + w_ref[k:k + 1, :] * _read(gcopies, g_slots, gpad, -s, m, hb + r0, sub)
            du_ref[r0:r0 + sub, :] = acc.astype(du_ref.dtype)
        db_ref[...] += jnp.sum(dbs, axis=0, keepdims=True)
        for k, (s, m) in enumerate(taps):
            part = jnp.zeros((SUBLANE, ct), F32)
            for r0 in range(0, rc, sub):
                part = part + fold(_read(gcopies, g_slots, gpad, 0, m, hb + r0, sub)
                                   * _read(ucopies, u_slots, upad, s, None, hb + r0, sub))
            dw_ref[k:k + 1, :] += jnp.sum(part, axis=0, keepdims=True)

    halo_u = _halo_specs(rc, ct, hb, T, col_off // ct)
    halo_g = _halo_specs(rc, ct, hb, T, 0)
    rows = rc + 2 * hb
    return _pcall(
        body, name=name, grid=(ncc, nrc),
        in_specs=halo_u + halo_g + [pl.BlockSpec((K, ct), lambda j, i: (0, j))],
        out_specs=[pl.BlockSpec((rc, ct), lambda j, i: (i, j)), pl.BlockSpec((K, ct), lambda j, i: (0, j)),
                   pl.BlockSpec((1, ct), lambda j, i: (0, j))],
        out_shape=[jax.ShapeDtypeStruct((T, C), du_dtype), jax.ShapeDtypeStruct((K, C), F32),
                   jax.ShapeDtypeStruct((1, C), F32)],
        scratch_shapes=[pltpu.VMEM((rows, ct), F32), pltpu.VMEM((rows, ct), F32),
                        pltpu.VMEM((max(len(u_slots), 1), rows, ct), F32),
                        pltpu.VMEM((max(len(g_slots), 1), rows, ct), F32)],
        compiler_params=_cparams(2),
    )(u, u, u, g, g, g, w)


def _ssd_group(xg, bm, cm, s_in, *per_head, reverse, P):
    R = len(per_head) // 2
    dtrs, a_s = per_head[:R], per_head[R:]
    q, rp = xg.shape
    ii = lax.broadcasted_iota(jnp.int32, (q, q), 0)
    jj = lax.broadcasted_iota(jnp.int32, (q, q), 1)
    causal = (jj >= ii) if reverse else (jj <= ii)
    causal_t = (ii >= jj) if reverse else (ii <= jj)
    eye = ii == jj
    lane = lax.broadcasted_iota(jnp.int32, (1, rp), 1)
    row = lax.broadcasted_iota(jnp.int32, (rp, 1), 0)
    nt = (((1,), (1,)), ((), ()))
    tn = (((0,), (0,)), ((), ()))
    cb = lax.dot_general(cm.astype(BF16), bm.astype(BF16), nt, preferred_element_type=F32)
    dt_x = jnp.zeros((q, rp), F32)
    acum_x = jnp.zeros((q, rp), F32)
    tot_row = jnp.zeros((1, rp), F32)
    tot_col = jnp.zeros((rp, 1), F32)
    wts, lane_masks = [], []
    for r in range(R):
        hm = (lane >= r * P) & (lane < (r + 1) * P)
        hc = (row >= r * P) & (row < (r + 1) * P)
        dt_c = jnp.sum(jnp.where(eye, dtrs[r], 0.0), axis=1, keepdims=True)
        dac = dt_c * a_s[r]
        dar = dtrs[r] * a_s[r]
        acum_c = jnp.sum(jnp.where(causal, dar, 0.0), axis=1, keepdims=True)
        acum_r = jnp.sum(jnp.where(causal_t, dac, 0.0), axis=0, keepdims=True)
        decay = jnp.where(causal, jnp.exp(jnp.where(causal, acum_c - acum_r, 0.0)), 0.0)
        tot = jnp.sum(dac, axis=0, keepdims=True)
        dt_x = jnp.where(hm, dt_c, dt_x)
        acum_x = jnp.where(hm, acum_c, acum_x)
        tot_row = jnp.where(hm, tot, tot_row)
        tot_col = jnp.where(hc, tot, tot_col)
        wts.append((cb * decay).astype(BF16))
        lane_masks.append(hm)
    xdt = xg * dt_x
    xdt_b = xdt.astype(BF16)
    y = jnp.zeros((q, rp), F32)
    for r in range(R):
        y = jnp.where(lane_masks[r], jnp.dot(wts[r], xdt_b, preferred_element_type=F32), y)
    dte = jnp.exp(tot_row - acum_x)
    cs = lax.dot_general((xdt * dte).astype(BF16), bm.astype(BF16), tn, preferred_element_type=F32)
    y = y + lax.dot_general(cm.astype(BF16), s_in.astype(BF16), nt, preferred_element_type=F32) * jnp.exp(acum_x)
    s_out = jnp.exp(tot_col) * s_in + cs
    return y, s_out


def _ssd_maps(NC, ncc, reverse_steps):
    def chunk(d, s):
        if reverse_steps:
            s = NC - 1 - s
        return s if d == 0 else jnp.where(s < ncc, ncc - 1 - s, NC - 1 - s + ncc)

    def lat_chunk(d, s):
        c = chunk(d, s) - ncc
        return jnp.where(c < 0, 0 if d == 0 else NC - ncc - 1, c)

    def step(s):
        return NC - 1 - s if reverse_steps else s

    return chunk, lat_chunk, step


def _ssd_specs(chunk, d, R, Q, N, RP, bo, co):
    return [
        pl.BlockSpec((Q, RP), lambda g, s: (chunk(d, s), g)),
        pl.BlockSpec((Q, N), lambda g, s: (chunk(d, s), bo + g)),
        pl.BlockSpec((Q, N), lambda g, s: (chunk(d, s), co + g)),
        pl.BlockSpec((R, 1, Q), lambda g, s: (g, 0, chunk(d, s))),
        pl.BlockSpec((R, 1, 1), lambda g, s: (g, 0, 0)),
    ]


def _ssd_fwd(xbc, b_off, c_off, dtr, a, P, ncc):
    T = xbc.shape[0]
    H = dtr[0].shape[0]
    N, Q = SSD_STATE, SSD_CHUNK
    NC = T // Q
    G = (c_off - b_off) // N
    R = H // G
    RP = R * P
    chunk, lat_chunk, _ = _ssd_maps(NC, ncc, False)

    def body(*refs):
        s = pl.program_id(1)
        s_ref = refs[-1]

        @pl.when(s == 0)
        def _():
            s_ref[...] = jnp.zeros_like(s_ref)

        for d in range(2):
            x_ref, b_ref, c_ref, dtr_ref, a_ref = refs[5 * d:5 * d + 5]
            y_ref, se_ref = refs[10 + 2 * d:12 + 2 * d]
            s_in = s_ref[d]
            se_ref[...] = s_in
            per_head = [dtr_ref[r] for r in range(R)] + [a_ref[r] for r in range(R)]
            y, s_out = _ssd_group(x_ref[...], b_ref[...], c_ref[...], s_in, *per_head, reverse=d == 1, P=P)
            y_ref[...] = y
            s_ref[d] = s_out

    in_specs, out_specs, out_shape, operands = [], [], [], []
    for d in range(2):
        in_specs += _ssd_specs(chunk, d, R, Q, N, RP, b_off // N, c_off // N)
        operands += [xbc, xbc, xbc, dtr[d], a[d]]
        out_specs += [pl.BlockSpec((Q, RP), functools.partial(lambda g, s, d: (lat_chunk(d, s), g), d=d)),
                      pl.BlockSpec((None, None, RP, N), lambda g, s: (g, s, 0, 0))]
        out_shape += [jax.ShapeDtypeStruct((T - ncc * Q, H * P), F32), jax.ShapeDtypeStruct((G, NC, RP, N), F32)]
    y_f, se_f, y_b, se_b = _pcall(
        body, name="ssd_fwd", grid=(G, NC), in_specs=in_specs, out_specs=out_specs, out_shape=out_shape,
        scratch_shapes=[pltpu.VMEM((2, RP, N), F32)], compiler_params=_cparams(2),
    )(*operands)
    return (y_f, y_b), (se_f, se_b)


def _ssd_bwd(xbc, b_off, c_off, dtr, a, s_enter, dy, P, ncc):
    T = xbc.shape[0]
    H = dtr[0].shape[0]
    N, Q = SSD_STATE, SSD_CHUNK
    NC = T // Q
    G = (c_off - b_off) // N
    R = H // G
    RP = R * P
    chunk, lat_chunk, step = _ssd_maps(NC, ncc, True)
    n_in, n_out = 7, 5

    def body(*refs):
        s = pl.program_id(1)
        ds_ref = refs[-1]

        @pl.when(s == 0)
        def _():
            ds_ref[...] = jnp.zeros_like(ds_ref)

        for d in range(2):
            x_ref, b_ref, c_ref, dtr_ref, a_ref, se_ref, dy_ref = refs[n_in * d:n_in * (d + 1)]
            dx_ref, db_ref, dc_ref, ddtr_ref, da_ref = refs[2 * n_in + n_out * d:2 * n_in + n_out * (d + 1)]
            per_head = [dtr_ref[r] for r in range(R)] + [a_ref[r] for r in range(R)]
            f = functools.partial(_ssd_group, reverse=d == 1, P=P)
            _, vjp = jax.vjp(f, x_ref[...], b_ref[...], c_ref[...], se_ref[...], *per_head)
            is_latent = chunk(d, s) >= ncc
            dy_v = jnp.where(is_latent, dy_ref[...], 0.0)
            grads = vjp((dy_v, ds_ref[d]))
            dx_ref[...] = grads[0]
            db_ref[...] = grads[1]
            dc_ref[...] = grads[2]
            ds_ref[d] = grads[3]
            for r in range(R):
                ddtr_ref[r] = grads[4 + r]
                da_ref[r] = jnp.broadcast_to(grads[4 + R + r], (SUBLANE, LANE))

    in_specs, out_specs, out_shape, operands = [], [], [], []
    for d in range(2):
        in_specs += _ssd_specs(chunk, d, R, Q, N, RP, b_off // N, c_off // N) + [
            pl.BlockSpec((None, None, RP, N), lambda g, s: (g, step(s), 0, 0)),
            pl.BlockSpec((Q, RP), functools.partial(lambda g, s, d: (lat_chunk(d, s), g), d=d)),
        ]
        operands += [xbc, xbc, xbc, dtr[d], a[d], s_enter[d], dy]
    for d in range(2):
        at_chunk = functools.partial(lambda g, s, d: (chunk(d, s), g), d=d)
        out_specs += [
            pl.BlockSpec((Q, RP), at_chunk), pl.BlockSpec((Q, N), at_chunk), pl.BlockSpec((Q, N), at_chunk),
            pl.BlockSpec((R, 1, Q), functools.partial(lambda g, s, d: (g, 0, chunk(d, s)), d=d)),
            pl.BlockSpec((R, SUBLANE, LANE), lambda g, s: (g * NC + s, 0, 0)),
        ]
        out_shape += [
            jax.ShapeDtypeStruct((T, H * P), F32), jax.ShapeDtypeStruct((T, G * N), F32),
            jax.ShapeDtypeStruct((T, G * N), F32), jax.ShapeDtypeStruct((H, 1, T), F32),
            jax.ShapeDtypeStruct((G * NC * R, SUBLANE, LANE), F32),
        ]
    res = _pcall(
        body, name="ssd_bwd", grid=(G, NC), in_specs=in_specs, out_specs=out_specs, out_shape=out_shape,
        scratch_shapes=[pltpu.VMEM((2, RP, N), F32)], compiler_params=_cparams(2),
    )(*operands)
    return res[:n_out], res[n_out:]


def _allgather8(name, v):
    R, C = v.shape

    def body(x_ref, out_ref, send_sems, recv_sems, local_sem):
        x, y, c = lax.axis_index("x"), lax.axis_index("y"), lax.axis_index("c")
        me, sibling = (x, y, c), (x, y, 1 - c)
        chips = [(1 - x, y), (x, 1 - y), (1 - x, 1 - y)]

        def slot(px, py, pc):
            return out_ref.at[4 * px + 2 * py + pc]

        def copy(k, block, to, src=None):
            return pltpu.make_async_remote_copy(
                src_ref=slot(*block) if src is None else src, dst_ref=slot(*block),
                send_sem=send_sems.at[k], recv_sem=recv_sems.at[k], device_id=to, device_id_type=MESH)

        mine = pltpu.make_async_copy(x_ref, slot(*me), local_sem)
        mine.start()
        first = [copy(0, me, sibling, src=x_ref)]
        first += [copy(1 + j, me, (*chip, c), src=x_ref) for j, chip in enumerate(chips)]
        for cp in first:
            cp.start()
        passed = [copy(4 + j, (*chip, c), sibling) for j, chip in enumerate(chips)]
        for j, chip in enumerate(chips):
            copy(1 + j, (*chip, c), me).wait_recv()
            passed[j].start()
        copy(0, sibling, me).wait_recv()
        for j, chip in enumerate(chips):
            copy(4 + j, (*chip, 1 - c), me).wait_recv()
        for cp in first + passed:
            cp.wait_send()
        mine.wait()

    return _pcall(
        body, name=name, out_shape=jax.ShapeDtypeStruct((N_DEV, R, C), v.dtype),
        in_specs=[pl.BlockSpec(memory_space=pltpu.VMEM)], out_specs=pl.BlockSpec(memory_space=pltpu.VMEM),
        scratch_shapes=[pltpu.SemaphoreType.DMA((7,)), pltpu.SemaphoreType.DMA((7,)), pltpu.SemaphoreType.DMA],
        compiler_params=pltpu.CompilerParams(vmem_limit_bytes=VMEM_LIMIT_BYTES),
    )(v)


def _exchange4_start(name, srcs, bcast, dep):
    n = len(srcs)
    lands = [lax.empty(((N_CHIPS,) + s.shape) if bcast else s.shape, s.dtype) for s in srcs]

    def body(*refs):
        src, land = refs[:n], refs[n:2 * n]
        send_sems, recv_sems = refs[2 * n + 1], refs[2 * n + 2]
        token = refs[-1]
        x, y, c = lax.axis_index("x"), lax.axis_index("y"), lax.axis_index("c")
        me = 2 * x + y
        for a in range(n):
            for j, (px, py) in enumerate([(1 - x, y), (x, 1 - y), (1 - x, 1 - y)]):
                pltpu.make_async_remote_copy(
                    src_ref=src[a] if bcast else src[a].at[2 * px + py], dst_ref=land[a].at[me],
                    send_sem=send_sems.at[3 * a + j], recv_sem=recv_sems.at[3 * a + j], device_id=(px, py, c),
                    device_id_type=MESH).start()
        token[...] = jnp.zeros_like(token)

    hbm = pl.BlockSpec(memory_space=pltpu.HBM)
    sem = pl.BlockSpec(memory_space=pltpu.SEMAPHORE)
    outs = _pcall(
        body, name=name,
        out_shape=(pltpu.SemaphoreType.DMA((3 * n,)), pltpu.SemaphoreType.DMA((3 * n,)),
                   *[pltpu.HBM(s.shape, s.dtype) for s in srcs], *[pltpu.HBM(l.shape, l.dtype) for l in lands],
                   jax.ShapeDtypeStruct((SUBLANE, LANE), F32)),
        in_specs=[hbm] * (2 * n) + [pl.BlockSpec(memory_space=pl.ANY)],
        out_specs=(sem, sem, *[hbm] * (2 * n), pl.BlockSpec(memory_space=pltpu.VMEM)),
        input_output_aliases={k: 2 + k for k in range(2 * n)},
        compiler_params=pltpu.CompilerParams(has_side_effects=pltpu.SideEffectType.DATAFLOW_SIDE_EFFECTING),
    )(*[pltpu.with_memory_space_constraint(s, pltpu.HBM) for s in srcs],
      *[pltpu.with_memory_space_constraint(l, pltpu.HBM) for l in lands], dep)
    return (n, bcast, outs[0], outs[1], outs[2:2 + n], outs[2 + n:2 + 2 * n]), outs[-1]


def _exchange4_wait(name, handle, after):
    n, bcast, send_sems, recv_sems, src_thru, land_thru = handle

    def body(*refs):
        src, land = refs[:n], refs[n:2 * n]
        send_sems, recv_sems = refs[2 * n], refs[2 * n + 1]
        x, y, c = lax.axis_index("x"), lax.axis_index("y"), lax.axis_index("c")
        for a in range(n):
            for j, (px, py) in enumerate([(1 - x, y), (x, 1 - y), (1 - x, 1 - y)]):
                pk = 2 * px + py
                copy = pltpu.make_async_remote_copy(
                    src_ref=src[a] if bcast else src[a].at[pk], dst_ref=land[a].at[pk],
                    send_sem=send_sems.at[3 * a + j], recv_sem=recv_sems.at[3 * a + j], device_id=(px, py, c),
                    device_id_type=MESH)
                copy.wait_send()
                copy.wait_recv()

    hbm = pl.BlockSpec(memory_space=pltpu.HBM)
    sem = pl.BlockSpec(memory_space=pltpu.SEMAPHORE)
    outs = _pcall(
        body, name=name,
        out_shape=tuple(pltpu.HBM(t.shape, t.dtype) for t in (*src_thru, *land_thru)),
        in_specs=[hbm] * (2 * n) + [sem, sem, pl.BlockSpec(memory_space=pl.ANY)], out_specs=tuple([hbm] * (2 * n)),
        input_output_aliases={k: k for k in range(2 * n)},
        compiler_params=pltpu.CompilerParams(has_side_effects=pltpu.SideEffectType.DATAFLOW_SIDE_EFFECTING),
    )(*src_thru, *land_thru, send_sems, recv_sems, after)
    return list(outs[:n]), list(outs[n:])


def _tie(name, v, token):
    def body(v_ref, token_ref, o_ref):
        del v_ref, token_ref, o_ref

    any_spec = pl.BlockSpec(memory_space=pl.ANY)
    return _pcall(body, name=name, out_shape=jax.ShapeDtypeStruct(v.shape, v.dtype), in_specs=[any_spec, any_spec],
                  out_specs=any_spec, input_output_aliases={0: 0})(v, token)


def _fill_own(landed, own, me, bcast):
    blk = own if bcast else lax.dynamic_index_in_dim(own, me, 0, keepdims=False)
    return lax.dynamic_update_index_in_dim(landed, blk, me, 0)


def _swap_sibling(name, srcs):
    n = len(srcs)

    def body(*refs):
        src, out = refs[:n], refs[n:2 * n]
        send_sems, recv_sems = refs[2 * n:]
        x, y, c = lax.axis_index("x"), lax.axis_index("y"), lax.axis_index("c")
        copies = []
        for a in range(n):
            rc = pltpu.make_async_remote_copy(
                src_ref=src[a], dst_ref=out[a], send_sem=send_sems.at[a], recv_sem=recv_sems.at[a],
                device_id=(x, y, 1 - c), device_id_type=MESH)
            rc.start()
            copies.append(rc)
        for cp in copies:
            cp.wait()

    any_spec = pl.BlockSpec(memory_space=pl.ANY)
    return _pcall(
        body, name=name, out_shape=[jax.ShapeDtypeStruct(s.shape, s.dtype) for s in srcs],
        in_specs=[any_spec] * n, out_specs=[any_spec] * n,
        scratch_shapes=[pltpu.SemaphoreType.DMA((n,)), pltpu.SemaphoreType.DMA((n,))],
    )(*srcs)


def _mod_fwd(c16, mod_w, mod_b_shard):
    nl, D, S = mod_w.shape

    def body(c_ref, w_ref, b_ref, o_ref):
        s = _silu(c_ref[...]).astype(BF16)
        o_ref[...] = jnp.dot(s, w_ref[...].astype(BF16), preferred_element_type=F32) + b_ref[...]

    return _pcall(
        body, name="mod_fwd", grid=(nl,),
        in_specs=[pl.BlockSpec((16, D), lambda l: (0, 0)), pl.BlockSpec((None, D, S), lambda l: (l, 0, 0)),
                  pl.BlockSpec((None, 1, S), lambda l: (l, 0, 0))],
        out_specs=pl.BlockSpec((None, 16, S), lambda l: (l, 0, 0)),
        out_shape=jax.ShapeDtypeStruct((nl, 16, S), F32), compiler_params=_cparams(1),
    )(c16, mod_w, mod_b_shard)


def _mod_w_update(s16t, dm16, w, m, v):
    nl, D, S = w.shape
    tm = _row_tile(D, 256)

    def body(s_ref, dm_ref, w_ref, m_ref, v_ref, g_ref, dl_ref, nm_ref, nv_ref):
        g = jnp.dot(s_ref[...], dm_ref[...], preferred_element_type=F32, precision=HIGHEST)
        g, dl, nm, nv = _f_adamw(w_ref[...], m_ref[...], v_ref[...], g, jnp.zeros_like(g))
        g_ref[...] = g
        dl_ref[...] = dl
        nm_ref[...] = nm
        nv_ref[...] = nv

    big = pl.BlockSpec((None, tm, S), lambda l, i: (l, i, 0))
    return _pcall(
        body, name="mod_w_update", grid=(nl, D // tm),
        in_specs=[pl.BlockSpec((tm, 16), lambda l, i: (i, 0)), pl.BlockSpec((None, 16, S), lambda l, i: (l, 0, 0)),
                  big, big, big],
        out_specs=[big] * 4, out_shape=[jax.ShapeDtypeStruct(w.shape, F32)] * 4, compiler_params=_cparams(2),
    )(s16t, dm16, w, m, v)


def _size(shape):
    n = 1
    for d in shape:
        n *= d
    return n


def _pack(arrs):
    pieces = []
    for a in arrs:
        flat = a.reshape(-1).astype(F32)
        pieces.append(jnp.pad(flat, (0, _round_up(flat.shape[0], LANE) - flat.shape[0])).reshape(-1, LANE))
    buf = jnp.concatenate(pieces, axis=0)
    return jnp.pad(buf, ((0, _round_up(buf.shape[0], SUBLANE) - buf.shape[0]), (0, 0)))


def _unpack(buf, shapes):
    lead = buf.shape[:-2]
    out, row = [], 0
    for s in shapes:
        n = _size(s)
        rows = _cdiv(n, LANE)
        piece = buf[..., row:row + rows, :].reshape(lead + (rows * LANE,))
        out.append(piece[..., :n].reshape(lead + tuple(s)))
        row += rows
    return out


def _adamw_many(name, ws, ms, vs, gs):
    n = len(ws)

    def body(*refs):
        for k in range(n):
            res = _f_adamw(refs[k][...], refs[n + k][...], refs[2 * n + k][...], refs[3 * n + k][...], 0.0)
            for j in range(4):
                refs[(4 + j) * n + k][...] = res[j]

    vmem = pl.BlockSpec(memory_space=pltpu.VMEM)
    res = _pcall(body, name=name, out_shape=[jax.ShapeDtypeStruct(w.shape, F32) for _ in range(4) for w in ws],
                 in_specs=[vmem] * (4 * n), out_specs=[vmem] * (4 * n))(*ws, *ms, *vs, *gs)
    return [tuple(res[j * n + k] for j in range(4)) for k in range(n)]


SHARD_AXIS = {
    "mod_w": 2, "ssd_w_in": 2, "ssd_conv_w": 2, "ssd_w_out": 1, "conf_w_pw1": 2, "conf_b_pw1": 1, "conf_w_dw": 2,
    "conf_b_dw": 1, "conf_ln_w": 1, "conf_ln_b": 1, "conf_w_pw2": 1, "conf_b_pw2": 1, "ffn_w_up": 2,
    "ffn_conv_w": 3, "ffn_w_down": 1,
}
BIG = ("ssd_w_in", "ssd_w_out", "conf_w_pw1", "conf_w_pw2", "ffn_w_up", "ffn_w_down")
WEIGHTS = ("c_ctx", "mod_w", "mod_b", "norm1_w", "norm2_w", "ssd_w_in", "ssd_conv_w", "ssd_conv_b", "ssd_dt_bias",
           "ssd_a_log", "ssd_d", "ssd_norm_w", "ssd_w_out", "conf_w_pw1", "conf_b_pw1", "conf_w_dw", "conf_b_dw",
           "conf_ln_w", "conf_ln_b", "conf_w_pw2", "conf_b_pw2", "ffn_w_up", "ffn_conv_w", "ffn_conv_b",
           "ffn_w_down", "final_norm_w")
SMALL = tuple(n for n in WEIGHTS if n not in BIG and n != "mod_w")
SMALL_SHARDED = tuple(n for n in SMALL if n in SHARD_AXIS)


def _unshard(stacked, axis):
    return jnp.concatenate([stacked[k] for k in range(N_CHIPS)], axis=axis)


def _to_blocks(full, axis):
    return jnp.stack(jnp.split(full, N_CHIPS, axis=axis))


def _par(v):
    v = v.reshape(-1, v.shape[-1])
    return v[:, None, :]


def kernel(x, c, ctx, c_ctx, mod_w, mod_b, norm1_w, norm2_w, ssd_w_in, ssd_conv_w, ssd_conv_b, ssd_dt_bias, ssd_a_log, ssd_d, ssd_norm_w, ssd_w_out, conf_w_pw1, conf_b_pw1, conf_w_dw, conf_b_dw, conf_ln_w, conf_ln_b, conf_w_pw2, conf_b_pw2, ffn_w_up, ffn_conv_w, ffn_conv_b, ffn_w_down, final_norm_w, loss_target, m_c_ctx, m_mod_w, m_mod_b, m_norm1_w, m_norm2_w, m_ssd_w_in, m_ssd_conv_w, m_ssd_conv_b, m_ssd_dt_bias, m_ssd_a_log, m_ssd_d, m_ssd_norm_w, m_ssd_w_out, m_conf_w_pw1, m_conf_b_pw1, m_conf_w_dw, m_conf_b_dw, m_conf_ln_w, m_conf_ln_b, m_conf_w_pw2, m_conf_b_pw2, m_ffn_w_up, m_ffn_conv_w, m_ffn_conv_b, m_ffn_w_down, m_final_norm_w, v_c_ctx, v_mod_w, v_mod_b, v_norm1_w, v_norm2_w, v_ssd_w_in, v_ssd_conv_w, v_ssd_conv_b, v_ssd_dt_bias, v_ssd_a_log, v_ssd_d, v_ssd_norm_w, v_ssd_w_out, v_conf_w_pw1, v_conf_b_pw1, v_conf_w_dw, v_conf_b_dw, v_conf_ln_w, v_conf_ln_b, v_conf_w_pw2, v_conf_b_pw2, v_ffn_w_up, v_ffn_conv_w, v_ffn_conv_b, v_ffn_w_down, v_final_norm_w):
    given = dict(locals())
    W = {n: given[n] for n in WEIGHTS}
    Mo = {n: given["m_" + n] for n in WEIGHTS}
    Vo = {n: given["v_" + n] for n in WEIGHTS}

    ax, ay, ac = lax.axis_index("x"), lax.axis_index("y"), lax.axis_index("c")
    chip = 2 * ax + ay
    dev = 4 * ax + 2 * ay + ac

    D = x.shape[-1]
    L, Lc = x.shape[1], ctx.shape[1]
    T0 = L + Lc
    H = ssd_a_log.shape[-1]
    DI = ssd_norm_w.shape[-1]
    P = DI // H
    CD = ssd_conv_b.shape[-1]
    N = SSD_STATE
    G = (CD - DI) // (2 * N)
    FH = ffn_conv_b.shape[-1]
    KS = ssd_conv_w.shape[1]
    KC = conf_w_dw.shape[1]
    ncc = Lc // SSD_CHUNK

    shard_b = {n: W[n].astype(BF16) for n in BIG}
    gather_a, token = _exchange4_start("gather_w_in_start", [shard_b["ssd_w_in"]], True, x)
    c = _tie("tie_gather_w_in", c, token)

    small_shard_shapes = [W[n].shape for n in SMALL_SHARDED]
    f1 = _allgather8("gather_small", _pack([c] + [W[n] for n in SMALL_SHARDED]))
    parts = _unpack(f1, [c.shape] + small_shard_shapes)
    Wf = dict(W)
    for n, p in zip(SMALL_SHARDED, parts[1:]):
        Wf[n] = _unshard(p[::2], SHARD_AXIS[n])
    c16 = jnp.concatenate([parts[0].reshape(N_DEV, D), c_ctx[None, :], jnp.zeros((16 - N_DEV - 1, D), F32)], axis=0)

    S_mod = mod_w.shape[-1]
    mod_b_shard = lax.dynamic_slice_in_dim(mod_b, chip * S_mod, S_mod, axis=1)[:, None, :]
    mod_part = _mod_fwd(c16, mod_w, mod_b_shard)
    f2 = _allgather8("gather_mod", mod_part.reshape(2 * 16, S_mod))
    mods = jnp.concatenate([f2[2 * k].reshape(2, 16, S_mod) for k in range(N_CHIPS)], axis=-1)
    my = lax.dynamic_slice_in_dim(mods, dev, 1, axis=1)[:, 0]
    sh1, sc1, g1, sh2, sc2, g2 = [[my[l, k * D:(k + 1) * D] for l in range(2)] for k in range(6)]
    csh1, csc1 = mods[0, N_DEV, 0:D], mods[0, N_DEV, D:2 * D]

    def full_weight(n, own, landed):
        return _unshard(_fill_own(landed, own, chip, True), SHARD_AXIS[n])

    xl = x[0]
    hcat = jnp.concatenate([ctx[0], xl], axis=0)
    n1w0, n2w0, n1w1, n2w1 = _par(norm1_w[0]), _par(norm2_w[0]), _par(norm1_w[1]), _par(norm2_w[1])
    sc_seg = jnp.stack([csc1, sc1[0]])[:, None, :]
    sh_seg = jnp.stack([csh1, sh1[0]])[:, None, :]

    a0 = _rw_fwd("l0_modnorm1", _f_modnorm, [hcat], [n1w0, sc_seg, sh_seg], [D], seg_rows=(Lc,), out_dtypes=[BF16])
    (own_in,), (landed_in,) = _exchange4_wait("gather_w_in_wait", gather_a, a0)
    w_in = full_weight("ssd_w_in", own_in, landed_in)[0]
    rest = [n for n in BIG if n != "ssd_w_in"]
    gather_b, token = _exchange4_start("gather_rest_start", [shard_b[n] for n in rest], True, landed_in)
    a0 = _tie("tie_gather_rest", a0, token)
    proj = _mm(a0, w_in, name="l0_w_in")
    seg_taps = [(k - KS // 2, ("seg", Lc)) for k in range(KS)]
    xbc_pre, xbc = _conv_fwd("l0_conv", proj, DI, CD, Wf["ssd_conv_w"][0], ssd_conv_b, seg_taps, act=True)
    dt_raw = proj[:, DI + CD:]
    dt_bias = _par(ssd_dt_bias.reshape(1, 2 * H))
    dt = _rw_fwd("l0_softplus", _f_softplus, [dt_raw], [dt_bias], [2 * H])
    dt_t = dt.T
    dtr = (dt_t[:H, None, :], dt_t[H:, None, :])
    a_all = -jnp.exp(ssd_a_log.reshape(2, H, 1, 1))
    a_neg = (a_all[0], a_all[1])
    (y_f, y_b), s_enter = _ssd_fwd(xbc, DI, DI + G * N, dtr, a_neg, P, ncc)
    gate_rows = [y_f, y_b, (xbc, 0, DI, Lc), (proj, 0, DI, Lc)]
    d_rep = _par(jnp.repeat(ssd_d[0], P))
    ssd_nw = _par(ssd_norm_w[0])
    yn = _rw_fwd("l0_ssd_gate", _f_ssd_gate, gate_rows, [d_rep, ssd_nw], [DI], T=L, out_dtypes=[BF16])
    Wb = {n: full_weight(n, own, g) for n, own, g in zip(rest, *_exchange4_wait("gather_rest_wait", gather_b, yn))}
    w_out, w_pw1, w_pw2 = Wb["ssd_w_out"][0], Wb["conf_w_pw1"][0], Wb["conf_w_pw2"][0]
    w_up, w_dn = Wb["ffn_w_up"], Wb["ffn_w_down"]
    mix0 = _mm(yn, w_out, name="l0_w_out")
    g1_0, g2_0, g1_1, g2_1 = _par(g1[0]), _par(g2[0]), _par(g1[1]), _par(g2[1])
    h1 = _rw_fwd("l0_res1", _f_gate_res, [xl, mix0], [g1_0], [D])

    grid_taps = [((i - 1) * GRID_W + (j - 1), (None if j == 1 else ("col", j - 1))) for i in range(3) for j in range(3)]

    def ffn_fwd(l, h, tag):
        a = _rw_fwd(tag + "_modnorm2", _f_modnorm, [h], [_par(norm2_w[l]), _par(sc2[l]), _par(sh2[l])], [D],
                    out_dtypes=[BF16])
        hh = _mm(a, w_up[l], name=tag + "_w_up")
        gc = _conv_fwd(tag + "_ffn_conv", hh, FH, FH, Wf["ffn_conv_w"][l].reshape(9, FH), ffn_conv_b[l][None, :],
                       grid_taps)
        act = _rw_fwd(tag + "_act", _f_ffn_act, [(hh, 0, FH), gc], [], [FH], col_tile=_tile(FH, 1536),
                      out_dtypes=[BF16])
        dn = _mm(act, w_dn[l], name=tag + "_w_down")
        return a, hh, gc, act, dn

    a1, hh0, gc0, act0, dn0 = ffn_fwd(0, h1, "l0")
    h2 = _rw_fwd("l0_res2", _f_gate_res, [h1, dn0], [g2_0], [D])

    a2 = _rw_fwd("l1_modnorm1", _f_modnorm, [h2], [n1w1, _par(sc1[1]), _par(sh1[1])], [D], out_dtypes=[BF16])
    pw = _mm(a2, w_pw1, name="l1_pw1")
    b_pw1 = Wf["conf_b_pw1"][0]
    glu = _rw_fwd("l1_glu", _f_glu, [(pw, 0, D), (pw, D, D)], [_par(b_pw1[:D]), _par(b_pw1[D:])], [D])
    conf_taps = [(k - KC // 2, None) for k in range(KC)]
    cv = _conv_fwd("l1_conv", glu, 0, D, Wf["conf_w_dw"][0], Wf["conf_b_dw"], conf_taps)
    ln_w, ln_b = _par(Wf["conf_ln_w"][0]), _par(Wf["conf_ln_b"][0])
    ls = _rw_fwd("l1_ln_silu", _f_ln_silu, [cv], [ln_w, ln_b], [D], out_dtypes=[BF16])
    p2 = _mm(ls, w_pw2, name="l1_pw2")
    b_pw2 = _par(Wf["conf_b_pw2"][0])
    h3 = _rw_fwd("l1_res1", _f_gate_res_bias, [h2, p2], [g1_1, b_pw2], [D])
    a3, hh1, gc1, act1, dn1 = ffn_fwd(1, h3, "l1")
    h4 = _rw_fwd("l1_res2", _f_gate_res, [h3, dn1], [g2_1], [D])

    fnw = final_norm_w[None, :]
    tgt = loss_target[0]
    loss_local = _loss_fwd(h4, tgt, fnw)[0, 0]
    loss = lax.psum(loss_local, ("x", "y", "c"))

    G_full = {}
    reduces = {}

    def start_reduce(tag, items, dep):
        def blocks_of(g, ax):
            if g.ndim == 3:
                return g
            return g.reshape(N_CHIPS, g.shape[0] // N_CHIPS, g.shape[1]) if ax == 0 else _to_blocks(g, ax)

        blocks = [blocks_of(g, ax).astype(BF16) for _, g, ax in items]
        handle, tok = _exchange4_start("reduce_" + tag + "_start", blocks, False, dep)
        reduces[tag] = ([n for n, _, _ in items], handle)
        return tok
    ones = jnp.ones((L, 1), F32)
    (dh4,), (dfnw,) = _rw_bwd("loss_bwd", _f_loss_rows, [h4, tgt], [_par(final_norm_w)], [ones],
                              row_grad=[True, False], par_grad=[True])
    G_full["final_norm_w"] = dfnw.reshape(D)

    def ffn_bwd(l, h, saved, g2_l, dh_out, tag):
        a, hh, gc, act, dn = saved
        (ddn,), (dg2,) = _rw_bwd(tag + "_res2_bwd", _f_gate_res, [h, dn], [g2_l], [dh_out],
                                 row_grad=[False, True], par_grad=[True], row_dtypes=[BF16])
        dact = _mm(ddn, w_dn[l], tb=True, name=tag + "_w_down_dx")
        dwdn = _mm(act, ddn, ta=True, name=tag + "_w_down_dw", out_dtype=BF16)
        (dval, dgc), _ = _rw_bwd(tag + "_act_bwd", _f_ffn_act, [(hh, 0, FH), gc], [], [dact],
                                 row_grad=[True, True], par_grad=[], col_tile=_tile(FH, 1536), row_dtypes=[BF16, F32])
        dgin, dcw, dcb = _conv_bwd(tag + "_ffn_conv_bwd", hh, FH, FH, Wf["ffn_conv_w"][l].reshape(9, FH), dgc,
                                   grid_taps, du_dtype=BF16)
        dhh = jnp.concatenate([dval, dgin], axis=1)
        da = _mm(dhh, w_up[l], tb=True, name=tag + "_w_up_dx")
        dwup = _mm(a, dhh, ta=True, name=tag + "_w_up_dw", out_dtype=BF16, col_blocks=N_CHIPS)
        (dh,), (dn2w, dsc2, dsh2) = _rw_bwd(
            tag + "_modnorm2_bwd", _f_modnorm, [h], [_par(norm2_w[l]), _par(sc2[l]), _par(sh2[l])], [da],
            row_grad=[True], par_grad=[True, True, True], add=dh_out)
        return dh, dict(w_down=dwdn, w_up=dwup, conv_w=dcw.reshape(3, 3, FH), conv_b=dcb.reshape(FH),
                        n2w=dn2w.reshape(D), sc2=dsc2.reshape(D), sh2=dsh2.reshape(D), g2=dg2.reshape(D))

    dh3, gf1 = ffn_bwd(1, h3, (a3, hh1, gc1, act1, dn1), g2_1, dh4, "l1")
    (dp2,), (dg1_1, db_pw2) = _rw_bwd("l1_res1_bwd", _f_gate_res_bias, [h2, p2], [g1_1, b_pw2], [dh3],
                                      row_grad=[False, True], par_grad=[True, True], row_dtypes=[BF16])
    dls = _mm(dp2, w_pw2, tb=True, name="l1_pw2_dx")
    dw_pw2 = _mm(ls, dp2, ta=True, name="l1_pw2_dw", out_dtype=BF16)
    (dcv,), (dln_w, dln_b) = _rw_bwd("l1_ln_silu_bwd", _f_ln_silu, [cv], [ln_w, ln_b], [dls],
                                     row_grad=[True], par_grad=[True, True])
    dglu, dw_dw, db_dw = _conv_bwd("l1_conv_bwd", glu, 0, D, Wf["conf_w_dw"][0], dcv, conf_taps)
    (dpa, dpg), (dba, dbg) = _rw_bwd("l1_glu_bwd", _f_glu, [(pw, 0, D), (pw, D, D)],
                                     [_par(b_pw1[:D]), _par(b_pw1[D:])], [dglu],
                                     row_grad=[True, True], par_grad=[True, True], row_dtypes=[BF16, BF16])
    dpw = jnp.concatenate([dpa, dpg], axis=1)
    da2 = _mm(dpw, w_pw1, tb=True, name="l1_pw1_dx")
    dw_pw1 = _mm(a2, dpw, ta=True, name="l1_pw1_dw", out_dtype=BF16, col_blocks=N_CHIPS)
    (dh2,), (dn1w1, dsc1_1, dsh1_1) = _rw_bwd(
        "l1_modnorm1_bwd", _f_modnorm, [h2], [n1w1, _par(sc1[1]), _par(sh1[1])], [da2],
        row_grad=[True], par_grad=[True, True, True], add=dh3)
    G_full["conf_b_pw2"] = db_pw2.reshape(1, D)
    G_full["conf_ln_w"], G_full["conf_ln_b"] = dln_w.reshape(1, D), dln_b.reshape(1, D)
    G_full["conf_w_dw"], G_full["conf_b_dw"] = dw_dw[None], db_dw.reshape(1, D)
    G_full["conf_b_pw1"] = jnp.concatenate([dba.reshape(1, D), dbg.reshape(1, D)], axis=1)

    token = start_reduce("l1", [("conf_w_pw2", dw_pw2, 0), ("conf_w_pw1", dw_pw1, 1), ("ffn_w_up1", gf1["w_up"], 1),
                                ("ffn_w_down1", gf1["w_down"], 0)], dw_pw2)
    dh2 = _tie("tie_reduce_l1", dh2, token)
    dh1, gf0 = ffn_bwd(0, h1, (a1, hh0, gc0, act0, dn0), g2_0, dh2, "l0")
    G_full["ffn_conv_w"] = jnp.stack([gf0["conv_w"], gf1["conv_w"]])
    G_full["ffn_conv_b"] = jnp.stack([gf0["conv_b"], gf1["conv_b"]])

    (dmix,), (dg1_0,) = _rw_bwd("l0_res1_bwd", _f_gate_res, [xl, mix0], [g1_0], [dh1],
                                row_grad=[False, True], par_grad=[True], row_dtypes=[BF16])
    dyn = _mm(dmix, w_out, tb=True, name="l0_w_out_dx")
    dw_out = _mm(yn, dmix, ta=True, name="l0_w_out_dw", out_dtype=BF16)
    token = start_reduce("l0", [("ffn_w_up0", gf0["w_up"], 1), ("ffn_w_down0", gf0["w_down"], 0),
                                ("ssd_w_out", dw_out, 0)], dw_out)
    dyn = _tie("tie_reduce_l0", dyn, token)
    (dy_lat, dxs_gate, dz_lat), (dd_rep, dssd_nw) = _rw_bwd(
        "l0_ssd_gate_bwd", _f_ssd_gate, gate_rows, [d_rep, ssd_nw], [dyn],
        row_grad=[True, False, True, True], par_grad=[True, True], T=L, row_dtypes=[F32, F32, BF16])
    g_f, g_b = _ssd_bwd(xbc, DI, DI + G * N, dtr, a_neg, s_enter, dy_lat, P, ncc)
    silu_bwd = functools.partial(_rw_bwd, f=_silu, pars=[], row_grad=[True], par_grad=[], T=T0)
    (dxs_pre,), _ = silu_bwd("l0_silu_bwd_x", rows=[(xbc_pre, 0, DI)], cot_fn=lambda p, q, r: p + q + r,
                             cots=[g_f[0], g_b[0], (dxs_gate, 0, DI, -Lc)],
                             col_tile=_tile(DI, 1024))
    (db_pre,), _ = silu_bwd("l0_silu_bwd_b", rows=[(xbc_pre, DI, G * N)], cot_fn=lambda p, q: p + q,
                            cots=[g_f[1], g_b[1]], col_tile=_tile(G * N, 1024))
    (dc_pre,), _ = silu_bwd("l0_silu_bwd_c", rows=[(xbc_pre, DI + G * N, G * N)], cot_fn=lambda p, q: p + q,
                            cots=[g_f[2], g_b[2]], col_tile=_tile(G * N, 1024))
    conv_w0 = Wf["ssd_conv_w"][0]
    pieces = []
    for tag, off, width, g_pre in (("x", 0, DI, dxs_pre), ("b", DI, G * N, db_pre), ("c", DI + G * N, G * N, dc_pre)):
        pieces.append(_conv_bwd("l0_conv_bwd_" + tag, proj, DI + off, width, conv_w0[:, off:off + width], g_pre,
                                seg_taps, du_dtype=BF16))
    dconv_in = [p[0] for p in pieces]
    dcw0 = jnp.concatenate([p[1] for p in pieces], axis=1)
    dcb0 = jnp.concatenate([p[2] for p in pieces], axis=1)
    ddt = jnp.concatenate([g_f[3][:, 0, :].T, g_b[3][:, 0, :].T], axis=1)
    (ddt_raw,), (ddt_bias,) = _rw_bwd("l0_softplus_bwd", _f_softplus, [dt_raw], [dt_bias], [ddt],
                                      row_grad=[True], par_grad=[True], row_dtypes=[BF16])
    dproj = jnp.concatenate([jnp.pad(dz_lat, ((Lc, 0), (0, 0))), *dconv_in, ddt_raw], axis=1)
    da0 = _mm(dproj, w_in, tb=True, name="l0_w_in_dx")
    dw_in = _mm(a0, dproj, ta=True, name="l0_w_in_dw", out_dtype=BF16)
    token = start_reduce("in", [("ssd_w_in", dw_in, 1)], dw_in)
    da0 = _tie("tie_reduce_in", da0, token)
    (dhcat,), (dn1w0, dsc_seg, dsh_seg) = _rw_bwd(
        "l0_modnorm1_bwd", _f_modnorm, [hcat], [n1w0, sc_seg, sh_seg], [da0],
        row_grad=[True], par_grad=[True, True, True], seg_rows=(Lc,))
    grad_x = (dhcat[Lc:] + dh1)[None]

    da_heads = jnp.stack([g[4][:, 0, 0].reshape(G, T0 // SSD_CHUNK, H // G).sum(axis=1).reshape(H)
                          for g in (g_f, g_b)])[None]
    G_full["ssd_a_log"] = da_heads * (-jnp.exp(ssd_a_log))
    G_full["ssd_dt_bias"] = ddt_bias.reshape(1, 2, H)
    G_full["ssd_d"] = dd_rep.reshape(H, P).sum(axis=1)[None]
    G_full["ssd_norm_w"] = dssd_nw.reshape(1, DI)
    G_full["ssd_conv_w"], G_full["ssd_conv_b"] = dcw0[None], dcb0.reshape(1, CD)
    G_full["norm1_w"] = jnp.stack([dn1w0.reshape(D), dn1w1.reshape(D)])
    G_full["norm2_w"] = jnp.stack([gf0["n2w"], gf1["n2w"]])

    zD = jnp.zeros((D,), F32)
    dm_own = jnp.stack([
        jnp.concatenate([dsh_seg[1, 0], dsc_seg[1, 0], dg1_0.reshape(D), gf0["sh2"], gf0["sc2"], gf0["g2"]]),
        jnp.concatenate([dsh1_1.reshape(D), dsc1_1.reshape(D), dg1_1.reshape(D), gf1["sh2"], gf1["sc2"], gf1["g2"]]),
    ])
    dmc_own = jnp.concatenate([dsh_seg[0, 0], dsc_seg[0, 0], zD, zD, zD, zD])

    out = {}

    def finish_reduce(tags, after, swap_name):
        partial = {}
        for tag in tags:
            names, handle = reduces[tag]
            blocks, landed = _exchange4_wait("reduce_" + tag + "_wait", handle, after)
            for n, blk, own in zip(names, landed, blocks):
                r = _fill_own(blk, own, chip, False)
                partial[n] = _sum_leading("sum4_" + n, r.reshape(N_CHIPS, -1, r.shape[-1]),
                                          (0, 1, 2, 3)).reshape(r.shape[1:])
        for n in ("ffn_w_up", "ffn_w_down"):
            if n + "0" in partial:
                partial[n] = jnp.stack([partial.pop(n + "0"), partial.pop(n + "1")])
        names = [n for n in BIG if n in partial]
        mine = [partial[n].reshape(W[n].shape) for n in names]
        for n, own, sib in zip(names, mine, _swap_sibling(swap_name, mine)):
            out[n] = _adamw("adamw_" + n, W[n], Mo[n], Vo[n], own, sib)
        return names

    early = finish_reduce(["l1", "l0"], dhcat, "swap_grads_early")

    small_sum_names = [n for n in SMALL if n not in ("c_ctx", "mod_b")]
    sum_part = [G_full[n] for n in small_sum_names] + [dmc_own]
    packed = _tie("tie_small_grads", _pack(sum_part + [dm_own]), out[early[-1]][1])
    gat = _allgather8("gather_small_grads", packed)
    total = _sum_leading("sum_small_grads", gat, tuple(range(N_DEV)))
    summed = _unpack(total, [a.shape for a in sum_part])
    Gs = dict(zip(small_sum_names, summed[:-1]))
    dmc_tot = summed[-1]
    dm_all = _unpack(gat, [a.shape for a in sum_part] + [dm_own.shape])[-1].transpose(1, 0, 2)
    dm16 = jnp.concatenate([dm_all, jnp.stack([dmc_tot, jnp.zeros_like(dmc_tot)])[:, None, :],
                            jnp.zeros((2, 16 - N_DEV - 1, 6 * D), F32)], axis=1)
    Gs["mod_b"] = _sum_leading("sum_mod_b", dm16.transpose(1, 0, 2).reshape(16, 2 * 6 * D // LANE, LANE),
                               tuple(range(N_DEV + 1))).reshape(2, 6 * D)

    dm16_shard = lax.dynamic_slice_in_dim(dm16, chip * S_mod, S_mod, axis=2)
    ds16 = _mm(dm16_shard[0], mod_w[0], tb=True, precision=HIGHEST, name="c_ctx_dx")
    sig = jax.nn.sigmoid(c_ctx)
    dcc_part = ds16[N_DEV] * (sig * (1.0 + c_ctx * (1.0 - sig)))
    gat_cc = _allgather8("gather_c_ctx_grad", _pack([dcc_part]))
    Gs["c_ctx"] = _sum_leading("sum_c_ctx_grad", gat_cc, (0, 2, 4, 6)).reshape(-1)[:D]

    s16t = _silu(c16).T
    out["mod_w"] = _mod_w_update(s16t, dm16_shard, mod_w, m_mod_w, v_mod_w)
    finish_reduce(["in"], out["mod_w"][0], "swap_grads_late")

    def own(n, full):
        if n in SHARD_AXIS:
            size = W[n].shape[SHARD_AXIS[n]]
            return lax.dynamic_slice_in_dim(full, chip * size, size, axis=SHARD_AXIS[n])
        return full

    def two_d(a):
        return a.reshape(1, -1) if a.ndim == 1 else a

    g_small = [own(n, Gs[n].reshape(Wf[n].shape)) for n in SMALL]
    res = _adamw_many("adamw_small", [two_d(W[n]) for n in SMALL], [two_d(Mo[n]) for n in SMALL],
                      [two_d(Vo[n]) for n in SMALL], [two_d(g) for g in g_small])
    for n, r in zip(SMALL, res):
        out[n] = tuple(t.reshape(W[n].shape) for t in r)

    grads = [out[n][0] for n in WEIGHTS]
    deltas = [out[n][1] for n in WEIGHTS]
    new_m = [out[n][2] for n in WEIGHTS]
    new_v = [out[n][3] for n in WEIGHTS]
    return (loss, grad_x, *grads, *deltas, *new_m, *new_v)
```

```python
import functools

import jax
import jax.numpy as jnp
from jax import lax
from jax.experimental import pallas as pl
from jax.experimental.pallas import tpu as pltpu

F32 = jnp.float32
BF16 = jnp.bfloat16
MESH = pl.DeviceIdType.MESH
HIGHEST = lax.Precision.HIGHEST

VMEM_LIMIT_BYTES = 48 * 1024 * 1024
LANE = 128
SUBLANE = 8

SSD_STATE = 128
SSD_CHUNK = 128
GRID_W = 64
EPS = 1e-6
N_CHIPS = 4
N_DEV = 8

ADAM_LR = 0.001
ADAM_B1 = 0.9
ADAM_B2 = 0.999
ADAM_EPS = 1e-08
ADAM_WD = 0.01
ADAM_STEP = 10


def _pcall(body, **kw):
    return pl.pallas_call(body, **kw)


def _cparams(n_grid):
    return pltpu.CompilerParams(dimension_semantics=("arbitrary",) * n_grid, vmem_limit_bytes=VMEM_LIMIT_BYTES)


def _cdiv(a, b):
    return -(-a // b)


def _round_up(a, b):
    return _cdiv(a, b) * b


def _tile(n, cap):
    if n <= cap:
        return n
    best = None
    for t in range(LANE, cap + 1, LANE):
        if n % t == 0:
            best = t
    if best is None:
        npad = _round_up(n, LANE)
        for t in range(LANE, cap + 1, LANE):
            if npad % t == 0:
                best = t
    return best


def _row_tile(n, cap, also=()):
    best = None
    for step in (2 * SUBLANE, SUBLANE):
        for t in range(step, min(cap, n) + 1, step):
            if n % t == 0 and all(a % t == 0 for a in also):
                best = t
        if best is not None:
            break
    assert best is not None, (n, cap, also)
    return best


def _silu(v):
    return v * jax.nn.sigmoid(v)


def _mm(a, b, *, name, ta=False, tb=False, precision=None, cap=1024, out_dtype=F32, col_blocks=None):
    M, K = (a.shape[1], a.shape[0]) if ta else a.shape
    N = b.shape[0] if tb else b.shape[1]
    assert K == (b.shape[1] if tb else b.shape[0]), (a.shape, b.shape, ta, tb)
    tm, tk = _tile(M, cap), _tile(K, cap + cap // 2)
    tn = _tile(N if col_blocks is None else N // col_blocks, cap + cap // 2)
    nm, nn, nk = _cdiv(M, tm), _cdiv(N, tn), _cdiv(K, tk)
    k_tail = K % tk
    exact = precision is not None

    def body(a_ref, b_ref, o_ref, acc_ref):
        k = pl.program_id(2)

        @pl.when(k == 0)
        def _():
            acc_ref[...] = jnp.zeros_like(acc_ref)

        av = a_ref[...]
        bv = b_ref[...]
        if k_tail:
            lim = K - k * tk
            ka = lax.broadcasted_iota(jnp.int32, av.shape, 0 if ta else 1)
            kb = lax.broadcasted_iota(jnp.int32, bv.shape, 1 if tb else 0)
            av = jnp.where(ka < lim, av, jnp.zeros_like(av))
            bv = jnp.where(kb < lim, bv, jnp.zeros_like(bv))
        if exact:
            av = av.astype(F32)
            bv = bv.astype(F32)
        else:
            av = av.astype(BF16)
            bv = bv.astype(BF16)
        dn = (((0 if ta else 1,), (1 if tb else 0,)), ((), ()))
        acc_ref[...] += lax.dot_general(av, bv, dn, preferred_element_type=F32, precision=precision)

        @pl.when(k == nk - 1)
        def _():
            o_ref[...] = acc_ref[...].astype(o_ref.dtype)

    a_spec = pl.BlockSpec((tk, tm), lambda i, j, k: (k, i)) if ta else pl.BlockSpec((tm, tk), lambda i, j, k: (i, k))
    b_spec = pl.BlockSpec((tn, tk), lambda i, j, k: (j, k)) if tb else pl.BlockSpec((tk, tn), lambda i, j, k: (k, j))
    if col_blocks is None:
        out_spec = pl.BlockSpec((tm, tn), lambda i, j, k: (i, j))
        out_shape = jax.ShapeDtypeStruct((M, N), out_dtype)
    else:
        per = (N // col_blocks) // tn
        assert per * tn * col_blocks == N, (N, col_blocks, tn)
        out_spec = pl.BlockSpec((None, tm, tn), lambda i, j, k: (j // per, i, j % per))
        out_shape = jax.ShapeDtypeStruct((col_blocks, M, N // col_blocks), out_dtype)
    return _pcall(
        body, name=name, grid=(nm, nn, nk), in_specs=[a_spec, b_spec], out_specs=out_spec, out_shape=out_shape,
        scratch_shapes=[pltpu.VMEM((tm, tn), F32)], compiler_params=_cparams(3),
    )(a, b)


def _norm_rows(rows):
    out = []
    for r in rows:
        if not isinstance(r, tuple):
            r = (r,)
        arr, off, width, roff = (r + (0, None, 0)[len(r) - 1:])
        out.append((arr, off, width if width is not None else arr.shape[1], roff))
    return out


def _rw_plan(T, rows, pars, seg_rows, col_tile, tm_cap):
    widths = [r[2] for r in rows]
    wmax = max(widths + [p.shape[-1] for p in pars] + [1])
    if col_tile is not None:
        assert all(w == widths[0] for w in widths) and all(p.shape[-1] == widths[0] for p in pars)
        ncol = widths[0] // col_tile
        assert ncol * col_tile == widths[0]
        wmax = col_tile
    else:
        ncol = 1
    cap = tm_cap if tm_cap is not None else max(SUBLANE, min(256, (256 * 1024) // wmax))
    tm = _row_tile(T, cap, also=tuple(seg_rows) + tuple(abs(r[3]) for r in rows if r[3]))
    bounds = tuple(s // tm for s in seg_rows)
    return widths, ncol, tm, bounds


def _rw_specs(rows, pars, ncol, tm, bounds, col_tile):
    def seg(i):
        s = 0
        for b in bounds:
            s = s + (i >= b).astype(jnp.int32)
        return s

    specs = []
    for arr, off, w, roff in rows:
        bw = col_tile if col_tile is not None else w
        assert off % bw == 0 and roff % tm == 0, (off, bw, roff, tm)
        specs.append(pl.BlockSpec((tm, bw), functools.partial(lambda j, i, ob, rb: (jnp.maximum(i + rb, 0), ob + j),
                                                              ob=off // bw, rb=roff // tm)))
    for p in pars:
        bw = col_tile if col_tile is not None else p.shape[-1]
        if p.shape[0] > 1:
            specs.append(pl.BlockSpec((None, 1, bw), lambda j, i: (seg(i), 0, j)))
        else:
            specs.append(pl.BlockSpec((None, 1, bw), lambda j, i: (0, 0, j)))
    return specs, seg


def _rw_fwd(name, f, rows, pars, out_widths, *, T=None, seg_rows=(), col_tile=None, tm_cap=None, out_dtypes=None):
    rows = _norm_rows(rows)
    T = rows[0][0].shape[0] if T is None else T
    widths, ncol, tm, bounds = _rw_plan(T, rows, pars, seg_rows, col_tile, tm_cap)
    in_specs, _ = _rw_specs(rows, pars, ncol, tm, bounds, col_tile)
    nr, npar, nout = len(rows), len(pars), len(out_widths)

    def body(*refs):
        vals = [r[...] for r in refs[:nr + npar]]
        outs = f(*vals)
        if not isinstance(outs, (tuple, list)):
            outs = (outs,)
        for o_ref, o in zip(refs[nr + npar:], outs):
            o_ref[...] = o.astype(o_ref.dtype)

    out_specs = [pl.BlockSpec((tm, col_tile if col_tile is not None else w), lambda j, i: (i, j)) for w in out_widths]
    res = _pcall(
        body, name=name, grid=(ncol, T // tm), in_specs=in_specs, out_specs=out_specs,
        out_shape=[jax.ShapeDtypeStruct((T, w), dt) for w, dt in zip(out_widths, out_dtypes or [F32] * nout)],
        compiler_params=_cparams(2),
    )(*[r[0] for r in rows], *pars)
    return res if nout > 1 else res[0]


def _rw_bwd(name, f, rows, pars, cots, *, row_grad, par_grad, T=None, seg_rows=(), col_tile=None, tm_cap=None,
            add=None, cot_fn=None, row_dtypes=None):
    rows = _norm_rows(rows)
    cots = _norm_rows(cots)
    T = rows[0][0].shape[0] if T is None else T
    extra = _norm_rows([add]) if add is not None else []
    all_rows = rows + cots + extra
    widths, ncol, tm, bounds = _rw_plan(T, all_rows, pars, seg_rows, col_tile, tm_cap)
    in_specs, seg = _rw_specs(all_rows, pars, ncol, tm, bounds, col_tile)
    nr, nc, ne, npar = len(rows), len(cots), len(extra), len(pars)
    row_idx = [k for k in range(nr) if row_grad[k]]
    par_idx = [k for k in range(npar) if par_grad[k]]

    def body(*refs):
        i = pl.program_id(1)
        row_vals = [r[...] for r in refs[:nr]]
        cot_vals = [r[...] for r in refs[nr:nr + nc]]
        cot_vals = [jnp.where(i + c[3] // tm >= 0, v, jnp.zeros_like(v)) if c[3] < 0 else v
                    for v, c in zip(cot_vals, cots)]
        add_vals = [r[...] for r in refs[nr + nc:nr + nc + ne]]
        par_vals = [r[...] for r in refs[nr + nc + ne:nr + nc + ne + npar]]
        out_refs = refs[nr + nc + ne + npar:]
        outs, vjp = jax.vjp(f, *row_vals, *par_vals)
        if cot_fn is not None:
            cot_vals = cot_fn(*cot_vals)
            if not isinstance(cot_vals, (tuple, list)):
                cot_vals = (cot_vals,)
        if isinstance(outs, (tuple, list)):
            grads = vjp(tuple(c.astype(o.dtype) for c, o in zip(cot_vals, outs)))
        else:
            grads = vjp(cot_vals[0].astype(outs.dtype))
        first_seg = i == 0
        for b in bounds:
            first_seg = first_seg | (i == b)
        for n, k in enumerate(row_idx):
            g = grads[k]
            if n == 0 and add_vals:
                g = g + add_vals[0]
            out_refs[n][...] = g.astype(out_refs[n].dtype)
        for n, k in enumerate(par_idx):
            g = grads[nr + k]
            o_ref = out_refs[len(row_idx) + n]
            first = first_seg if pars[k].shape[0] > 1 else (i == 0)

            @pl.when(first)
            def _(o_ref=o_ref, g=g):
                o_ref[...] = g

            @pl.when(jnp.logical_not(first))
            def _(o_ref=o_ref, g=g):
                o_ref[...] += g

    out_specs, out_shape = [], []
    for k in row_idx:
        w = widths[k]
        out_specs.append(pl.BlockSpec((tm, col_tile if col_tile is not None else w), lambda j, i: (i, j)))
        out_shape.append(jax.ShapeDtypeStruct((T, w), row_dtypes[len(out_shape)] if row_dtypes else F32))
    for k in par_idx:
        p = pars[k]
        bw = col_tile if col_tile is not None else p.shape[-1]
        if p.shape[0] > 1:
            out_specs.append(pl.BlockSpec((None, 1, bw), lambda j, i: (seg(i), 0, j)))
        else:
            out_specs.append(pl.BlockSpec((None, 1, bw), lambda j, i: (0, 0, j)))
        out_shape.append(jax.ShapeDtypeStruct(p.shape, F32))
    res = _pcall(
        body, name=name, grid=(ncol, T // tm), in_specs=in_specs, out_specs=out_specs, out_shape=out_shape,
        compiler_params=_cparams(2),
    )(*[r[0] for r in all_rows], *pars)
    return list(res[:len(row_idx)]), list(res[len(row_idx):])


def _f_modnorm(h, w, sc, sh):
    y = h * lax.rsqrt(jnp.mean(h * h, axis=-1, keepdims=True) + EPS)
    return (y * w) * (1.0 + sc) + sh


def _f_gate_res(h, y, g):
    return h + g * y


def _f_gate_res_bias(h, y, g, b):
    return h + g * (y + b)


def _f_gate(y, g):
    return g * y


def _f_gate_bias(y, g, b):
    return g * (y + b)


def _f_ffn_act(val, gate):
    return _silu(gate) * val


def _f_softplus(raw, bias):
    v = raw + bias
    return jnp.maximum(v, 0.0) + jnp.log(1.0 + jnp.exp(-jnp.abs(v)))


def _f_ssd_gate(yf, yb, xs, z, d_rep, nw):
    y = (yf + yb + d_rep * xs) * _silu(z)
    return (y * lax.rsqrt(jnp.mean(y * y, axis=-1, keepdims=True) + EPS)) * nw


def _f_glu(a, g, ba, bg):
    return (a + ba) * jax.nn.sigmoid(g + bg)


def _f_ln_silu(h, w, b):
    mu = jnp.mean(h, axis=-1, keepdims=True)
    d = h - mu
    y = d * lax.rsqrt(jnp.mean(d * d, axis=-1, keepdims=True) + EPS)
    return _silu(y * w + b)


def _f_loss_rows(h, t, w):
    y = (h * lax.rsqrt(jnp.mean(h * h, axis=-1, keepdims=True) + EPS)) * w
    e = y - t
    return 0.5 * jnp.mean(e * e, axis=-1, keepdims=True)


def _f_adamw(w, m, v, ga, gb):
    g = ga + gb
    m = ADAM_B1 * m + (1.0 - ADAM_B1) * g
    v = ADAM_B2 * v + (1.0 - ADAM_B2) * (g * g)
    m_hat = m / (1.0 - ADAM_B1 ** ADAM_STEP)
    v_hat = v / (1.0 - ADAM_B2 ** ADAM_STEP)
    delta = -ADAM_LR * (m_hat / (jnp.sqrt(v_hat) + ADAM_EPS) + ADAM_WD * w)
    return g, delta, m, v


def _adamw(name, w, m, v, ga, gb):
    shape = w.shape
    c = shape[-1]
    two_d = [t.reshape(-1, c) for t in (w, m, v, ga, gb)]
    rows = two_d[0].shape[0]
    pad = _round_up(rows, SUBLANE) - rows
    if pad:
        two_d = [jnp.pad(t, ((0, pad), (0, 0))) for t in two_d]
    outs = _rw_fwd(name, _f_adamw, two_d, [], [c] * 4)
    return tuple(o[:rows].reshape(shape) for o in outs)


def _sum_leading(name, x, idxs):
    _, R, C = x.shape
    tm = _row_tile(R, max(SUBLANE, min(512, (512 * 1024) // C)))

    def body(x_ref, o_ref):
        acc = x_ref[idxs[0]].astype(F32)
        for k in idxs[1:]:
            acc = acc + x_ref[k].astype(F32)
        o_ref[...] = acc

    return _pcall(
        body, name=name, grid=(R // tm,), in_specs=[pl.BlockSpec((x.shape[0], tm, C), lambda i: (0, i, 0))],
        out_specs=pl.BlockSpec((tm, C), lambda i: (i, 0)), out_shape=jax.ShapeDtypeStruct((R, C), F32),
        compiler_params=_cparams(1),
    )(x)


def _loss_fwd(h, t, w):
    T, D = h.shape
    tm = _row_tile(T, 256)

    def body(h_ref, t_ref, w_ref, o_ref):
        i = pl.program_id(0)
        part = jnp.sum(_f_loss_rows(h_ref[...], t_ref[...], w_ref[...]), axis=0, keepdims=True)
        part = jnp.broadcast_to(part, (1, LANE))

        @pl.when(i == 0)
        def _():
            o_ref[...] = part

        @pl.when(i > 0)
        def _():
            o_ref[...] += part

    return _pcall(
        body, name="loss_fwd", grid=(T // tm,),
        in_specs=[pl.BlockSpec((tm, D), lambda i: (i, 0)), pl.BlockSpec((tm, D), lambda i: (i, 0)),
                  pl.BlockSpec((1, D), lambda i: (0, 0))],
        out_specs=pl.BlockSpec((1, LANE), lambda i: (0, 0)), out_shape=jax.ShapeDtypeStruct((1, LANE), F32),
        compiler_params=_cparams(1),
    )(h, t, w)


CONV_ROWS = 256
CONV_ROWS_FEW_TAPS = 1024
CONV_ACC_ELEMS = 16384


def _col_mask(arg, t):
    col = jnp.bitwise_and(t, GRID_W - 1)
    return (col != 0) if arg < 0 else (col != GRID_W - 1)


def _conv_plan(T, C, taps):
    seg = [m[1] for _, m in taps if m is not None and m[0] == "seg"]
    boundary = seg[0] if seg else None
    cap = CONV_ROWS_FEW_TAPS if len(taps) <= 9 else CONV_ROWS
    rc = next(r for r in (1024, 768, 512, 256, LANE)
              if r <= cap and T % r == 0 and (boundary is None or boundary % r == 0))
    ct = next((t for t in (512, 256, LANE) if C % t == 0), C)
    reach = max(abs(s) for s, _ in taps)
    hb = next(h for h in (8, 16, 32, 64, 128, 256) if h >= reach and rc % h == 0)
    sub = max(2 * SUBLANE, min(rc, CONV_ACC_ELEMS // ct))
    taps = [(s, None if (m is None or m[0] == "seg") else m[1]) for s, m in taps]
    return rc, ct, hb, sub, T // rc, C // ct, boundary, taps


def _halo_specs(rc, ct, hb, T, off_blocks):
    per = rc // hb
    last = T // hb - 1
    prev = pl.BlockSpec((hb, ct), lambda j, i: (jnp.maximum(i * per - 1, 0), off_blocks + j))
    cur = pl.BlockSpec((rc, ct), lambda j, i: (i, off_blocks + j))
    nxt = pl.BlockSpec((hb, ct), lambda j, i: (jnp.minimum((i + 1) * per, last), off_blocks + j))
    return [prev, cur, nxt]


def _fill_halo(pad_ref, p_ref, c_ref, n_ref, i, nrc, rc, hb, boundary):
    has_prev = i > 0
    has_next = i < nrc - 1
    if boundary is not None:
        has_prev = has_prev & (i * rc != boundary)
        has_next = has_next & ((i + 1) * rc != boundary)
    pad_ref[0:hb, :] = jnp.where(has_prev, p_ref[...], 0.0)
    pad_ref[hb:hb + rc, :] = c_ref[...]
    pad_ref[hb + rc:hb + rc + hb, :] = jnp.where(has_next, n_ref[...], 0.0)


def _shift_plan(keys):
    count = {}
    for s, m in keys:
        k = (s % SUBLANE, m)
        count[k] = count.get(k, 0) + 1
    slots = {}
    for k, n in sorted(count.items(), key=lambda kv: (kv[0][0], str(kv[0][1]))):
        if k != (0, None) and (n >= 2 or k[1] is not None):
            slots[k] = len(slots)
    return slots


def _build_shifted(copies_ref, slots, pad_ref, keys, i, rc, hb, sub):
    for (r, m), slot in slots.items():
        qs = [s - r for s, mk in keys if (s % SUBLANE, mk) == (r, m)]
        lo, hi = hb + min(qs), hb + rc + max(qs)
        for p in range(lo, hi, sub):
            n = min(sub, hi - p)
            v = pad_ref[p + r:p + r + n, :]
            if m is not None:
                t = i * rc - hb + p + r + lax.broadcasted_iota(jnp.int32, (n, 1), 0)
                v = jnp.where(_col_mask(m, t), v, 0.0)
            copies_ref[slot, p:p + n, :] = v


def _read(copies_ref, slots, pad_ref, s, m, row, n):
    k = (s % SUBLANE, m)
    if k in slots:
        q = s - k[0]
        return copies_ref[slots[k], row + q:row + q + n, :]
    return pad_ref[row + s:row + s + n, :]


def _conv_fwd(name, u, col_off, C, w, b, taps, act=False):
    T = u.shape[0]
    rc, ct, hb, sub, nrc, ncc, boundary, taps = _conv_plan(T, C, taps)
    assert col_off % ct == 0
    K = len(taps)
    keys = [(s, None) for s, _ in taps]
    slots = _shift_plan(keys)
    dirs = sorted({m for _, m in taps if m is not None})

    def body(up, uc, un, w_ref, b_ref, *rest):
        y_ref = rest[0]
        pad_ref, copies_ref = rest[-2], rest[-1]
        i = pl.program_id(1)
        _fill_halo(pad_ref, up, uc, un, i, nrc, rc, hb, boundary)
        _build_shifted(copies_ref, slots, pad_ref, keys, i, rc, hb, sub)
        for r0 in range(0, rc, sub):
            acc = jnp.broadcast_to(b_ref[...], (sub, ct))
            for m in [None] + dirs:
                part = None
                for k, (s, mk) in enumerate(taps):
                    if mk != m:
                        continue
                    term = w_ref[k:k + 1, :] * _read(copies_ref, slots, pad_ref, s, None, hb + r0, sub)
                    part = term if part is None else part + term
                if part is None:
                    continue
                if m is not None:
                    t = i * rc + r0 + lax.broadcasted_iota(jnp.int32, (sub, 1), 0)
                    part = jnp.where(_col_mask(m, t), part, 0.0)
                acc = acc + part
            y_ref[r0:r0 + sub, :] = acc
            if act:
                rest[1][r0:r0 + sub, :] = _silu(acc)

    n_out = 2 if act else 1
    res = _pcall(
        body, name=name, grid=(ncc, nrc),
        in_specs=_halo_specs(rc, ct, hb, T, col_off // ct) + [pl.BlockSpec((K, ct), lambda j, i: (0, j)),
                                                              pl.BlockSpec((1, ct), lambda j, i: (0, j))],
        out_specs=[pl.BlockSpec((rc, ct), lambda j, i: (i, j))] * n_out,
        out_shape=[jax.ShapeDtypeStruct((T, C), F32)] * n_out,
        scratch_shapes=[pltpu.VMEM((rc + 2 * hb, ct), F32), pltpu.VMEM((max(len(slots), 1), rc + 2 * hb, ct), F32)],
        compiler_params=_cparams(2),
    )(u, u, u, w, b)
    return res if act else res[0]


def _conv_bwd(name, u, col_off, C, w, g, taps, du_dtype=F32):
    T = u.shape[0]
    rc, ct, hb, sub, nrc, ncc, boundary, taps = _conv_plan(T, C, taps)
    K = len(taps)
    u_keys = [(s, None) for s, _ in taps]
    dirs = sorted({m for _, m in taps if m is not None})
    g_keys = [(-s, m) for s, m in taps] + [(0, m) for m in dirs]
    u_slots, g_slots = _shift_plan(u_keys), _shift_plan(g_keys)

    def body(up, uc, un, gp, gc, gn, w_ref, du_ref, dw_ref, db_ref, upad, gpad, ucopies, gcopies):
        i = pl.program_id(1)
        _fill_halo(upad, up, uc, un, i, nrc, rc, hb, boundary)
        _fill_halo(gpad, gp, gc, gn, i, nrc, rc, hb, boundary)
        _build_shifted(ucopies, u_slots, upad, u_keys, i, rc, hb, sub)
        _build_shifted(gcopies, g_slots, gpad, g_keys, i, rc, hb, sub)

        @pl.when(i == 0)
        def _():
            dw_ref[...] = jnp.zeros_like(dw_ref)
            db_ref[...] = jnp.zeros_like(db_ref)

        def fold(v):
            return jnp.sum(v.reshape(sub // SUBLANE, SUBLANE, ct), axis=0)

        dbs = jnp.zeros((SUBLANE, ct), F32)
        for r0 in range(0, rc, sub):
            dbs = dbs + fold(gpad[hb + r0:hb + r0 + sub, :])
            acc = jnp.zeros((sub, ct), F32)
            for k, (s, m) in enumerate(taps):
                acc = acc + w_ref[k:k + 1, :] * _read(gcopies, g_slots, gpad, -s, m, hb + r0, sub)
            du_ref[r0:r0 + sub, :] = acc.astype(du_ref.dtype)
        db_ref[...] += jnp.sum(dbs, axis=0, keepdims=True)
        for k, (s, m) in enumerate(taps):
            part = jnp.zeros((SUBLANE, ct), F32)
            for r0 in range(0, rc, sub):
                part = part + fold(_read(gcopies, g_slots, gpad, 0, m, hb + r0, sub)
                                   * _read(ucopies, u_slots, upad, s, None, hb + r0, sub))
            dw_ref[k:k + 1, :] += jnp.sum(part, axis=0, keepdims=True)

    halo_u = _halo_specs(rc, ct, hb, T, col_off // ct)
    halo_g = _halo_specs(rc, ct, hb, T, 0)
    rows = rc + 2 * hb
    return _pcall(
        body, name=name, grid=(ncc, nrc),
        in_specs=halo_u + halo_g + [pl.BlockSpec((K, ct), lambda j, i: (0, j))],
        out_specs=[pl.BlockSpec((rc, ct), lambda j, i: (i, j)), pl.BlockSpec((K, ct), lambda j, i: (0, j)),
                   pl.BlockSpec((1, ct), lambda j, i: (0, j))],
        out_shape=[jax.ShapeDtypeStruct((T, C), du_dtype), jax.ShapeDtypeStruct((K, C), F32),
                   jax.ShapeDtypeStruct((1, C), F32)],
        scratch_shapes=[pltpu.VMEM((rows, ct), F32), pltpu.VMEM((rows, ct), F32),
                        pltpu.VMEM((max(len(u_slots), 1), rows, ct), F32),
                        pltpu.VMEM((max(len(g_slots), 1), rows, ct), F32)],
        compiler_params=_cparams(2),
    )(u, u, u, g, g, g, w)


def _ssd_group(xg, bm, cm, s_in, *per_head, reverse, P):
    R = len(per_head) // 2
    dtrs, a_s = per_head[:R], per_head[R:]
    q, rp = xg.shape
    ii = lax.broadcasted_iota(jnp.int32, (q, q), 0)
    jj = lax.broadcasted_iota(jnp.int32, (q, q), 1)
    causal = (jj >= ii) if reverse else (jj <= ii)
    causal_t = (ii >= jj) if reverse else (ii <= jj)
    eye = ii == jj
    lane = lax.broadcasted_iota(jnp.int32, (1, rp), 1)
    row = lax.broadcasted_iota(jnp.int32, (rp, 1), 0)
    nt = (((1,), (1,)), ((), ()))
    tn = (((0,), (0,)), ((), ()))
    cb = lax.dot_general(cm.astype(BF16), bm.astype(BF16), nt, preferred_element_type=F32)
    dt_x = jnp.zeros((q, rp), F32)
    acum_x = jnp.zeros((q, rp), F32)
    tot_row = jnp.zeros((1, rp), F32)
    tot_col = jnp.zeros((rp, 1), F32)
    wts, lane_masks = [], []
    for r in range(R):
        hm = (lane >= r * P) & (lane < (r + 1) * P)
        hc = (row >= r * P) & (row < (r + 1) * P)
        dt_c = jnp.sum(jnp.where(eye, dtrs[r], 0.0), axis=1, keepdims=True)
        dac = dt_c * a_s[r]
        dar = dtrs[r] * a_s[r]
        acum_c = jnp.sum(jnp.where(causal, dar, 0.0), axis=1, keepdims=True)
        acum_r = jnp.sum(jnp.where(causal_t, dac, 0.0), axis=0, keepdims=True)
        decay = jnp.where(causal, jnp.exp(jnp.where(causal, acum_c - acum_r, 0.0)), 0.0)
        tot = jnp.sum(dac, axis=0, keepdims=True)
        dt_x = jnp.where(hm, dt_c, dt_x)
        acum_x = jnp.where(hm, acum_c, acum_x)
        tot_row = jnp.where(hm, tot, tot_row)
        tot_col = jnp.where(hc, tot, tot_col)
        wts.append((cb * decay).astype(BF16))
        lane_masks.append(hm)
    xdt = xg * dt_x
    xdt_b = xdt.astype(BF16)
    y = jnp.zeros((q, rp), F32)
    for r in range(R):
        y = jnp.where(lane_masks[r], jnp.dot(wts[r], xdt_b, preferred_element_type=F32), y)
    dte = jnp.exp(tot_row - acum_x)
    cs = lax.dot_general((xdt * dte).astype(BF16), bm.astype(BF16), tn, preferred_element_type=F32)
    y = y + lax.dot_general(cm.astype(BF16), s_in.astype(BF16), nt, preferred_element_type=F32) * jnp.exp(acum_x)
    s_out = jnp.exp(tot_col) * s_in + cs
    return y, s_out


def _ssd_maps(NC, ncc, reverse_steps):
    def chunk(d, s):
        if reverse_steps:
            s = NC - 1 - s
        return s if d == 0 else jnp.where(s < ncc, ncc - 1 - s, NC - 1 - s + ncc)

    def lat_chunk(d, s):
        c = chunk(d, s) - ncc
        return jnp.where(c < 0, 0 if d == 0 else NC - ncc - 1, c)

    def step(s):
        return NC - 1 - s if reverse_steps else s

    return chunk, lat_chunk, step


def _ssd_specs(chunk, d, R, Q, N, RP, bo, co):
    return [
        pl.BlockSpec((Q, RP), lambda g, s: (chunk(d, s), g)),
        pl.BlockSpec((Q, N), lambda g, s: (chunk(d, s), bo + g)),
        pl.BlockSpec((Q, N), lambda g, s: (chunk(d, s), co + g)),
        pl.BlockSpec((R, 1, Q), lambda g, s: (g, 0, chunk(d, s))),
        pl.BlockSpec((R, 1, 1), lambda g, s: (g, 0, 0)),
    ]


def _ssd_fwd(xbc, b_off, c_off, dtr, a, P, ncc):
    T = xbc.shape[0]
    H = dtr[0].shape[0]
    N, Q = SSD_STATE, SSD_CHUNK
    NC = T // Q
    G = (c_off - b_off) // N
    R = H // G
    RP = R * P
    chunk, lat_chunk, _ = _ssd_maps(NC, ncc, False)

    def body(*refs):
        s = pl.program_id(1)
        s_ref = refs[-1]

        @pl.when(s == 0)
        def _():
            s_ref[...] = jnp.zeros_like(s_ref)

        for d in range(2):
            x_ref, b_ref, c_ref, dtr_ref, a_ref = refs[5 * d:5 * d + 5]
            y_ref, se_ref = refs[10 + 2 * d:12 + 2 * d]
            s_in = s_ref[d]
            se_ref[...] = s_in
            per_head = [dtr_ref[r] for r in range(R)] + [a_ref[r] for r in range(R)]
            y, s_out = _ssd_group(x_ref[...], b_ref[...], c_ref[...], s_in, *per_head, reverse=d == 1, P=P)
            y_ref[...] = y
            s_ref[d] = s_out

    in_specs, out_specs, out_shape, operands = [], [], [], []
    for d in range(2):
        in_specs += _ssd_specs(chunk, d, R, Q, N, RP, b_off // N, c_off // N)
        operands += [xbc, xbc, xbc, dtr[d], a[d]]
        out_specs += [pl.BlockSpec((Q, RP), functools.partial(lambda g, s, d: (lat_chunk(d, s), g), d=d)),
                      pl.BlockSpec((None, None, RP, N), lambda g, s: (g, s, 0, 0))]
        out_shape += [jax.ShapeDtypeStruct((T - ncc * Q, H * P), F32), jax.ShapeDtypeStruct((G, NC, RP, N), F32)]
    y_f, se_f, y_b, se_b = _pcall(
        body, name="ssd_fwd", grid=(G, NC), in_specs=in_specs, out_specs=out_specs, out_shape=out_shape,
        scratch_shapes=[pltpu.VMEM((2, RP, N), F32)], compiler_params=_cparams(2),
    )(*operands)
    return (y_f, y_b), (se_f, se_b)


def _ssd_bwd(xbc, b_off, c_off, dtr, a, s_enter, dy, P, ncc):
    T = xbc.shape[0]
    H = dtr[0].shape[0]
    N, Q = SSD_STATE, SSD_CHUNK
    NC = T // Q
    G = (c_off - b_off) // N
    R = H // G
    RP = R * P
    chunk, lat_chunk, step = _ssd_maps(NC, ncc, True)
    n_in, n_out = 7, 5

    def body(*refs):
        s = pl.program_id(1)
        ds_ref = refs[-1]

        @pl.when(s == 0)
        def _():
            ds_ref[...] = jnp.zeros_like(ds_ref)

        for d in range(2):
            x_ref, b_ref, c_ref, dtr_ref, a_ref, se_ref, dy_ref = refs[n_in * d:n_in * (d + 1)]
            dx_ref, db_ref, dc_ref, ddtr_ref, da_ref = refs[2 * n_in + n_out * d:2 * n_in + n_out * (d + 1)]
            per_head = [dtr_ref[r] for r in range(R)] + [a_ref[r] for r in range(R)]
            f = functools.partial(_ssd_group, reverse=d == 1, P=P)
            _, vjp = jax.vjp(f, x_ref[...], b_ref[...], c_ref[...], se_ref[...], *per_head)
            is_latent = chunk(d, s) >= ncc
            dy_v = jnp.where(is_latent, dy_ref[...], 0.0)
            grads = vjp((dy_v, ds_ref[d]))
            dx_ref[...] = grads[0]
            db_ref[...] = grads[1]
            dc_ref[...] = grads[2]
            ds_ref[d] = grads[3]
            for r in range(R):
                ddtr_ref[r] = grads[4 + r]
                da_ref[r] = jnp.broadcast_to(grads[4 + R + r], (SUBLANE, LANE))

    in_specs, out_specs, out_shape, operands = [], [], [], []
    for d in range(2):
        in_specs += _ssd_specs(chunk, d, R, Q, N, RP, b_off // N, c_off // N) + [
            pl.BlockSpec((None, None, RP, N), lambda g, s: (g, step(s), 0, 0)),
            pl.BlockSpec((Q, RP), functools.partial(lambda g, s, d: (lat_chunk(d, s), g), d=d)),
        ]
        operands += [xbc, xbc, xbc, dtr[d], a[d], s_enter[d], dy]
    for d in range(2):
        at_chunk = functools.partial(lambda g, s, d: (chunk(d, s), g), d=d)
        out_specs += [
            pl.BlockSpec((Q, RP), at_chunk), pl.BlockSpec((Q, N), at_chunk), pl.BlockSpec((Q, N), at_chunk),
            pl.BlockSpec((R, 1, Q), functools.partial(lambda g, s, d: (g, 0, chunk(d, s)), d=d)),
            pl.BlockSpec((R, SUBLANE, LANE), lambda g, s: (g * NC + s, 0, 0)),
        ]
        out_shape += [
            jax.ShapeDtypeStruct((T, H * P), F32), jax.ShapeDtypeStruct((T, G * N), F32),
            jax.ShapeDtypeStruct((T, G * N), F32), jax.ShapeDtypeStruct((H, 1, T), F32),
            jax.ShapeDtypeStruct((G * NC * R, SUBLANE, LANE), F32),
        ]
    res = _pcall(
        body, name="ssd_bwd", grid=(G, NC), in_specs=in_specs, out_specs=out_specs, out_shape=out_shape,
        scratch_shapes=[pltpu.VMEM((2, RP, N), F32)], compiler_params=_cparams(2),
    )(*operands)
    return res[:n_out], res[n_out:]


def _allgather8(name, v):
    R, C = v.shape

    def body(x_ref, out_ref, send_sems, recv_sems, local_sem):
        x, y, c = lax.axis_index("x"), lax.axis_index("y"), lax.axis_index("c")
        me, sibling = (x, y, c), (x, y, 1 - c)
        chips = [(1 - x, y), (x, 1 - y), (1 - x, 1 - y)]

        def slot(px, py, pc):
            return out_ref.at[4 * px + 2 * py + pc]

        def copy(k, block, to, src=None):
            return pltpu.make_async_remote_copy(
                src_ref=slot(*block) if src is None else src, dst_ref=slot(*block),
                send_sem=send_sems.at[k], recv_sem=recv_sems.at[k], device_id=to, device_id_type=MESH)

        mine = pltpu.make_async_copy(x_ref, slot(*me), local_sem)
        mine.start()
        first = [copy(0, me, sibling, src=x_ref)]
        first += [copy(1 + j, me, (*chip, c), src=x_ref) for j, chip in enumerate(chips)]
        for cp in first:
            cp.start()
        passed = [copy(4 + j, (*chip, c), sibling) for j, chip in enumerate(chips)]
        for j, chip in enumerate(chips):
            copy(1 + j, (*chip, c), me).wait_recv()
            passed[j].start()
        copy(0, sibling, me).wait_recv()
        for j, chip in enumerate(chips):
            copy(4 + j, (*chip, 1 - c), me).wait_recv()
        for cp in first + passed:
            cp.wait_send()
        mine.wait()

    return _pcall(
        body, name=name, out_shape=jax.ShapeDtypeStruct((N_DEV, R, C), v.dtype),
        in_specs=[pl.BlockSpec(memory_space=pltpu.VMEM)], out_specs=pl.BlockSpec(memory_space=pltpu.VMEM),
        scratch_shapes=[pltpu.SemaphoreType.DMA((7,)), pltpu.SemaphoreType.DMA((7,)), pltpu.SemaphoreType.DMA],
        compiler_params=pltpu.CompilerParams(vmem_limit_bytes=VMEM_LIMIT_BYTES),
    )(v)


def _slot(ref, k, axis, size):
    if axis is None:
        return ref.at[k]
    align = LANE if size % LANE == 0 else 2 * SUBLANE
    assert size % align == 0
    return ref.at[(slice(None),) * axis + (pl.ds(pl.multiple_of(k * size, align), size),)]


def _exchange4_start(name, srcs, bcast, dep, axes=None):
    n = len(srcs)
    axes = list(axes) if axes is not None else [None] * n
    sizes = [None if ax is None else s.shape[ax] for s, ax in zip(srcs, axes)]

    def land_shape(s, ax):
        if not bcast:
            return s.shape
        if ax is None:
            return (N_CHIPS,) + s.shape
        return s.shape[:ax] + (N_CHIPS * s.shape[ax],) + s.shape[ax + 1:]

    lands = [lax.empty(land_shape(s, ax), s.dtype) for s, ax in zip(srcs, axes)]

    def body(*refs):
        src, land = refs[:n], refs[n:2 * n]
        send_sems, recv_sems = refs[2 * n + 1], refs[2 * n + 2]
        token = refs[-1]
        x, y, c = lax.axis_index("x"), lax.axis_index("y"), lax.axis_index("c")
        me = 2 * x + y
        for a in range(n):
            for j, (px, py) in enumerate([(1 - x, y), (x, 1 - y), (1 - x, 1 - y)]):
                pltpu.make_async_remote_copy(
                    src_ref=src[a] if bcast else src[a].at[2 * px + py], dst_ref=_slot(land[a], me, axes[a], sizes[a]),
                    send_sem=send_sems.at[3 * a + j], recv_sem=recv_sems.at[3 * a + j], device_id=(px, py, c),
                    device_id_type=MESH).start()
        token[...] = jnp.zeros_like(token)

    hbm = pl.BlockSpec(memory_space=pltpu.HBM)
    sem = pl.BlockSpec(memory_space=pltpu.SEMAPHORE)
    outs = _pcall(
        body, name=name,
        out_shape=(pltpu.SemaphoreType.DMA((3 * n,)), pltpu.SemaphoreType.DMA((3 * n,)),
                   *[pltpu.HBM(s.shape, s.dtype) for s in srcs], *[pltpu.HBM(l.shape, l.dtype) for l in lands],
                   jax.ShapeDtypeStruct((SUBLANE, LANE), F32)),
        in_specs=[hbm] * (2 * n) + [pl.BlockSpec(memory_space=pl.ANY)],
        out_specs=(sem, sem, *[hbm] * (2 * n), pl.BlockSpec(memory_space=pltpu.VMEM)),
        input_output_aliases={k: 2 + k for k in range(2 * n)},
        compiler_params=pltpu.CompilerParams(has_side_effects=pltpu.SideEffectType.DATAFLOW_SIDE_EFFECTING),
    )(*[pltpu.with_memory_space_constraint(s, pltpu.HBM) for s in srcs],
      *[pltpu.with_memory_space_constraint(l, pltpu.HBM) for l in lands], dep)
    return (n, bcast, axes, sizes, outs[0], outs[1], outs[2:2 + n], outs[2 + n:2 + 2 * n]), outs[-1]


def _exchange4_wait(name, handle, after):
    n, bcast, axes, sizes, send_sems, recv_sems, src_thru, land_thru = handle

    def body(*refs):
        src, land = refs[:n], refs[n:2 * n]
        send_sems, recv_sems = refs[2 * n], refs[2 * n + 1]
        x, y, c = lax.axis_index("x"), lax.axis_index("y"), lax.axis_index("c")
        for a in range(n):
            for j, (px, py) in enumerate([(1 - x, y), (x, 1 - y), (1 - x, 1 - y)]):
                pk = 2 * px + py
                copy = pltpu.make_async_remote_copy(
                    src_ref=src[a] if bcast else src[a].at[pk], dst_ref=_slot(land[a], pk, axes[a], sizes[a]),
                    send_sem=send_sems.at[3 * a + j], recv_sem=recv_sems.at[3 * a + j], device_id=(px, py, c),
                    device_id_type=MESH)
                copy.wait_send()
                copy.wait_recv()

    hbm = pl.BlockSpec(memory_space=pltpu.HBM)
    sem = pl.BlockSpec(memory_space=pltpu.SEMAPHORE)
    outs = _pcall(
        body, name=name,
        out_shape=tuple(pltpu.HBM(t.shape, t.dtype) for t in (*src_thru, *land_thru)),
        in_specs=[hbm] * (2 * n) + [sem, sem, pl.BlockSpec(memory_space=pl.ANY)], out_specs=tuple([hbm] * (2 * n)),
        input_output_aliases={k: k for k in range(2 * n)},
        compiler_params=pltpu.CompilerParams(has_side_effects=pltpu.SideEffectType.DATAFLOW_SIDE_EFFECTING),
    )(*src_thru, *land_thru, send_sems, recv_sems, after)
    return list(outs[:n]), list(outs[n:])


def _tie(name, v, token):
    def body(v_ref, token_ref, o_ref):
        del v_ref, token_ref, o_ref

    any_spec = pl.BlockSpec(memory_space=pl.ANY)
    return _pcall(body, name=name, out_shape=jax.ShapeDtypeStruct(v.shape, v.dtype), in_specs=[any_spec, any_spec],
                  out_specs=any_spec, input_output_aliases={0: 0})(v, token)


def _fill_own(landed, own, me, bcast):
    blk = own if bcast else lax.dynamic_index_in_dim(own, me, 0, keepdims=False)
    return lax.dynamic_update_index_in_dim(landed, blk, me, 0)


def _swap_sibling(name, srcs):
    n = len(srcs)

    def body(*refs):
        src, out = refs[:n], refs[n:2 * n]
        send_sems, recv_sems = refs[2 * n:]
        x, y, c = lax.axis_index("x"), lax.axis_index("y"), lax.axis_index("c")
        copies = []
        for a in range(n):
            rc = pltpu.make_async_remote_copy(
                src_ref=src[a], dst_ref=out[a], send_sem=send_sems.at[a], recv_sem=recv_sems.at[a],
                device_id=(x, y, 1 - c), device_id_type=MESH)
            rc.start()
            copies.append(rc)
        for cp in copies:
            cp.wait()

    any_spec = pl.BlockSpec(memory_space=pl.ANY)
    return _pcall(
        body, name=name, out_shape=[jax.ShapeDtypeStruct(s.shape, s.dtype) for s in srcs],
        in_specs=[any_spec] * n, out_specs=[any_spec] * n,
        scratch_shapes=[pltpu.SemaphoreType.DMA((n,)), pltpu.SemaphoreType.DMA((n,))],
    )(*srcs)


def _mod_fwd(c16, mod_w, mod_b_shard):
    nl, D, S = mod_w.shape

    def body(c_ref, w_ref, b_ref, o_ref):
        s = _silu(c_ref[...]).astype(BF16)
        o_ref[...] = jnp.dot(s, w_ref[...].astype(BF16), preferred_element_type=F32) + b_ref[...]

    return _pcall(
        body, name="mod_fwd", grid=(nl,),
        in_specs=[pl.BlockSpec((16, D), lambda l: (0, 0)), pl.BlockSpec((None, D, S), lambda l: (l, 0, 0)),
                  pl.BlockSpec((None, 1, S), lambda l: (l, 0, 0))],
        out_specs=pl.BlockSpec((None, 16, S), lambda l: (l, 0, 0)),
        out_shape=jax.ShapeDtypeStruct((nl, 16, S), F32), compiler_params=_cparams(1),
    )(c16, mod_w, mod_b_shard)


def _mod_w_update(s16t, dm16, w, m, v):
    nl, D, S = w.shape
    tm = _row_tile(D, 256)

    def body(s_ref, dm_ref, w_ref, m_ref, v_ref, g_ref, dl_ref, nm_ref, nv_ref):
        g = jnp.dot(s_ref[...], dm_ref[...], preferred_element_type=F32, precision=HIGHEST)
        g, dl, nm, nv = _f_adamw(w_ref[...], m_ref[...], v_ref[...], g, jnp.zeros_like(g))
        g_ref[...] = g
        dl_ref[...] = dl
        nm_ref[...] = nm
        nv_ref[...] = nv

    big = pl.BlockSpec((None, tm, S), lambda l, i: (l, i, 0))
    return _pcall(
        body, name="mod_w_update", grid=(nl, D // tm),
        in_specs=[pl.BlockSpec((tm, 16), lambda l, i: (i, 0)), pl.BlockSpec((None, 16, S), lambda l, i: (l, 0, 0)),
                  big, big, big],
        out_specs=[big] * 4, out_shape=[jax.ShapeDtypeStruct(w.shape, F32)] * 4, compiler_params=_cparams(2),
    )(s16t, dm16, w, m, v)


def _size(shape):
    n = 1
    for d in shape:
        n *= d
    return n


def _pack(arrs):
    pieces = []
    for a in arrs:
        flat = a.reshape(-1).astype(F32)
        pieces.append(jnp.pad(flat, (0, _round_up(flat.shape[0], LANE) - flat.shape[0])).reshape(-1, LANE))
    buf = jnp.concatenate(pieces, axis=0)
    return jnp.pad(buf, ((0, _round_up(buf.shape[0], SUBLANE) - buf.shape[0]), (0, 0)))


def _unpack(buf, shapes):
    lead = buf.shape[:-2]
    out, row = [], 0
    for s in shapes:
        n = _size(s)
        rows = _cdiv(n, LANE)
        piece = buf[..., row:row + rows, :].reshape(lead + (rows * LANE,))
        out.append(piece[..., :n].reshape(lead + tuple(s)))
        row += rows
    return out


def _adamw_many(name, ws, ms, vs, gs):
    n = len(ws)

    def body(*refs):
        for k in range(n):
            res = _f_adamw(refs[k][...], refs[n + k][...], refs[2 * n + k][...], refs[3 * n + k][...], 0.0)
            for j in range(4):
                refs[(4 + j) * n + k][...] = res[j]

    vmem = pl.BlockSpec(memory_space=pltpu.VMEM)
    res = _pcall(body, name=name, out_shape=[jax.ShapeDtypeStruct(w.shape, F32) for _ in range(4) for w in ws],
                 in_specs=[vmem] * (4 * n), out_specs=[vmem] * (4 * n))(*ws, *ms, *vs, *gs)
    return [tuple(res[j * n + k] for j in range(4)) for k in range(n)]


SHARD_AXIS = {
    "mod_w": 2, "ssd_w_in": 2, "ssd_conv_w": 2, "ssd_w_out": 1, "conf_w_pw1": 2, "conf_b_pw1": 1, "conf_w_dw": 2,
    "conf_b_dw": 1, "conf_ln_w": 1, "conf_ln_b": 1, "conf_w_pw2": 1, "conf_b_pw2": 1, "ffn_w_up": 2,
    "ffn_conv_w": 3, "ffn_w_down": 1,
}
BIG = ("ssd_w_in", "ssd_w_out", "conf_w_pw1", "conf_w_pw2", "ffn_w_up", "ffn_w_down")
WEIGHTS = ("c_ctx", "mod_w", "mod_b", "norm1_w", "norm2_w", "ssd_w_in", "ssd_conv_w", "ssd_conv_b", "ssd_dt_bias",
           "ssd_a_log", "ssd_d", "ssd_norm_w", "ssd_w_out", "conf_w_pw1", "conf_b_pw1", "conf_w_dw", "conf_b_dw",
           "conf_ln_w", "conf_ln_b", "conf_w_pw2", "conf_b_pw2", "ffn_w_up", "ffn_conv_w", "ffn_conv_b",
           "ffn_w_down", "final_norm_w")
SMALL = tuple(n for n in WEIGHTS if n not in BIG and n != "mod_w")
SMALL_SHARDED = tuple(n for n in SMALL if n in SHARD_AXIS)


def _unshard(stacked, axis):
    return jnp.concatenate([stacked[k] for k in range(N_CHIPS)], axis=axis)


def _to_blocks(full, axis):
    return jnp.stack(jnp.split(full, N_CHIPS, axis=axis))


def _par(v):
    v = v.reshape(-1, v.shape[-1])
    return v[:, None, :]


def kernel(x, c, ctx, c_ctx, mod_w, mod_b, norm1_w, norm2_w, ssd_w_in, ssd_conv_w, ssd_conv_b, ssd_dt_bias, ssd_a_log, ssd_d, ssd_norm_w, ssd_w_out, conf_w_pw1, conf_b_pw1, conf_w_dw, conf_b_dw, conf_ln_w, conf_ln_b, conf_w_pw2, conf_b_pw2, ffn_w_up, ffn_conv_w, ffn_conv_b, ffn_w_down, final_norm_w, loss_target, m_c_ctx, m_mod_w, m_mod_b, m_norm1_w, m_norm2_w, m_ssd_w_in, m_ssd_conv_w, m_ssd_conv_b, m_ssd_dt_bias, m_ssd_a_log, m_ssd_d, m_ssd_norm_w, m_ssd_w_out, m_conf_w_pw1, m_conf_b_pw1, m_conf_w_dw, m_conf_b_dw, m_conf_ln_w, m_conf_ln_b, m_conf_w_pw2, m_conf_b_pw2, m_ffn_w_up, m_ffn_conv_w, m_ffn_conv_b, m_ffn_w_down, m_final_norm_w, v_c_ctx, v_mod_w, v_mod_b, v_norm1_w, v_norm2_w, v_ssd_w_in, v_ssd_conv_w, v_ssd_conv_b, v_ssd_dt_bias, v_ssd_a_log, v_ssd_d, v_ssd_norm_w, v_ssd_w_out, v_conf_w_pw1, v_conf_b_pw1, v_conf_w_dw, v_conf_b_dw, v_conf_ln_w, v_conf_ln_b, v_conf_w_pw2, v_conf_b_pw2, v_ffn_w_up, v_ffn_conv_w, v_ffn_conv_b, v_ffn_w_down, v_final_norm_w):
    given = dict(locals())
    W = {n: given[n] for n in WEIGHTS}
    Mo = {n: given["m_" + n] for n in WEIGHTS}
    Vo = {n: given["v_" + n] for n in WEIGHTS}

    ax, ay, ac = lax.axis_index("x"), lax.axis_index("y"), lax.axis_index("c")
    chip = 2 * ax + ay
    dev = 4 * ax + 2 * ay + ac

    D = x.shape[-1]
    L, Lc = x.shape[1], ctx.shape[1]
    T0 = L + Lc
    H = ssd_a_log.shape[-1]
    DI = ssd_norm_w.shape[-1]
    P = DI // H
    CD = ssd_conv_b.shape[-1]
    N = SSD_STATE
    G = (CD - DI) // (2 * N)
    FH = ffn_conv_b.shape[-1]
    KS = ssd_conv_w.shape[1]
    KC = conf_w_dw.shape[1]
    ncc = Lc // SSD_CHUNK

    shard_b = {n: W[n].astype(BF16) for n in BIG}
    gather_a, token = _exchange4_start("gather_w_in_start", [shard_b["ssd_w_in"]], True, x)
    c = _tie("tie_gather_w_in", c, token)

    small_shard_shapes = [W[n].shape for n in SMALL_SHARDED]
    f1 = _allgather8("gather_small", _pack([c] + [W[n] for n in SMALL_SHARDED]))
    parts = _unpack(f1, [c.shape] + small_shard_shapes)
    Wf = dict(W)
    for n, p in zip(SMALL_SHARDED, parts[1:]):
        Wf[n] = _unshard(p[::2], SHARD_AXIS[n])
    c16 = jnp.concatenate([parts[0].reshape(N_DEV, D), c_ctx[None, :], jnp.zeros((16 - N_DEV - 1, D), F32)], axis=0)

    S_mod = mod_w.shape[-1]
    mod_b_shard = lax.dynamic_slice_in_dim(mod_b, chip * S_mod, S_mod, axis=1)[:, None, :]
    mod_part = _mod_fwd(c16, mod_w, mod_b_shard)
    f2 = _allgather8("gather_mod", mod_part.reshape(2 * 16, S_mod))
    mods = jnp.concatenate([f2[2 * k].reshape(2, 16, S_mod) for k in range(N_CHIPS)], axis=-1)
    my = lax.dynamic_slice_in_dim(mods, dev, 1, axis=1)[:, 0]
    sh1, sc1, g1, sh2, sc2, g2 = [[my[l, k * D:(k + 1) * D] for l in range(2)] for k in range(6)]
    csh1, csc1 = mods[0, N_DEV, 0:D], mods[0, N_DEV, D:2 * D]

    def full_weight(n, own, landed):
        if landed.ndim == own.ndim:
            ax = SHARD_AXIS[n]
            return lax.dynamic_update_slice_in_dim(landed, own, chip * own.shape[ax], ax)
        return _unshard(_fill_own(landed, own, chip, True), SHARD_AXIS[n])

    xl = x[0]
    hcat = jnp.concatenate([ctx[0], xl], axis=0)
    n1w0, n2w0, n1w1, n2w1 = _par(norm1_w[0]), _par(norm2_w[0]), _par(norm1_w[1]), _par(norm2_w[1])
    sc_seg = jnp.stack([csc1, sc1[0]])[:, None, :]
    sh_seg = jnp.stack([csh1, sh1[0]])[:, None, :]

    a0 = _rw_fwd("l0_modnorm1", _f_modnorm, [hcat], [n1w0, sc_seg, sh_seg], [D], seg_rows=(Lc,), out_dtypes=[BF16])
    (own_in,), (landed_in,) = _exchange4_wait("gather_w_in_wait", gather_a, a0)
    w_in = full_weight("ssd_w_in", own_in, landed_in)[0]
    rest = [n for n in BIG if n != "ssd_w_in"]
    gather_b, token = _exchange4_start("gather_rest_start", [shard_b[n] for n in rest], True, landed_in,
                                       axes=[SHARD_AXIS[n] for n in rest])
    a0 = _tie("tie_gather_rest", a0, token)
    proj = _mm(a0, w_in, name="l0_w_in")
    seg_taps = [(k - KS // 2, ("seg", Lc)) for k in range(KS)]
    xbc_pre, xbc = _conv_fwd("l0_conv", proj, DI, CD, Wf["ssd_conv_w"][0], ssd_conv_b, seg_taps, act=True)
    dt_raw = proj[:, DI + CD:]
    dt_bias = _par(ssd_dt_bias.reshape(1, 2 * H))
    dt = _rw_fwd("l0_softplus", _f_softplus, [dt_raw], [dt_bias], [2 * H])
    dt_t = dt.T
    dtr = (dt_t[:H, None, :], dt_t[H:, None, :])
    a_all = -jnp.exp(ssd_a_log.reshape(2, H, 1, 1))
    a_neg = (a_all[0], a_all[1])
    (y_f, y_b), s_enter = _ssd_fwd(xbc, DI, DI + G * N, dtr, a_neg, P, ncc)
    gate_rows = [y_f, y_b, (xbc, 0, DI, Lc), (proj, 0, DI, Lc)]
    d_rep = _par(jnp.repeat(ssd_d[0], P))
    ssd_nw = _par(ssd_norm_w[0])
    yn = _rw_fwd("l0_ssd_gate", _f_ssd_gate, gate_rows, [d_rep, ssd_nw], [DI], T=L, out_dtypes=[BF16])
    Wb = {n: full_weight(n, own, g) for n, own, g in zip(rest, *_exchange4_wait("gather_rest_wait", gather_b, yn))}
    w_out, w_pw1, w_pw2 = Wb["ssd_w_out"][0], Wb["conf_w_pw1"][0], Wb["conf_w_pw2"][0]
    w_up, w_dn = Wb["ffn_w_up"], Wb["ffn_w_down"]
    mix0 = _mm(yn, w_out, name="l0_w_out")
    g1_0, g2_0, g1_1, g2_1 = _par(g1[0]), _par(g2[0]), _par(g1[1]), _par(g2[1])
    h1 = _rw_fwd("l0_res1", _f_gate_res, [xl, mix0], [g1_0], [D])

    grid_taps = [((i - 1) * GRID_W + (j - 1), (None if j == 1 else ("col", j - 1))) for i in range(3) for j in range(3)]

    def ffn_fwd(l, h, tag):
        a = _rw_fwd(tag + "_modnorm2", _f_modnorm, [h], [_par(norm2_w[l]), _par(sc2[l]), _par(sh2[l])], [D],
                    out_dtypes=[BF16])
        hh = _mm(a, w_up[l], name=tag + "_w_up")
        gc = _conv_fwd(tag + "_ffn_conv", hh, FH, FH, Wf["ffn_conv_w"][l].reshape(9, FH), ffn_conv_b[l][None, :],
                       grid_taps)
        act = _rw_fwd(tag + "_act", _f_ffn_act, [(hh, 0, FH), gc], [], [FH], col_tile=_tile(FH, 1536),
                      out_dtypes=[BF16])
        dn = _mm(act, w_dn[l], name=tag + "_w_down")
        return a, hh, gc, act, dn

    a1, hh0, gc0, act0, dn0 = ffn_fwd(0, h1, "l0")
    h2 = _rw_fwd("l0_res2", _f_gate_res, [h1, dn0], [g2_0], [D])

    a2 = _rw_fwd("l1_modnorm1", _f_modnorm, [h2], [n1w1, _par(sc1[1]), _par(sh1[1])], [D], out_dtypes=[BF16])
    pw = _mm(a2, w_pw1, name="l1_pw1")
    b_pw1 = Wf["conf_b_pw1"][0]
    glu = _rw_fwd("l1_glu", _f_glu, [(pw, 0, D), (pw, D, D)], [_par(b_pw1[:D]), _par(b_pw1[D:])], [D])
    conf_taps = [(k - KC // 2, None) for k in range(KC)]
    cv = _conv_fwd("l1_conv", glu, 0, D, Wf["conf_w_dw"][0], Wf["conf_b_dw"], conf_taps)
    ln_w, ln_b = _par(Wf["conf_ln_w"][0]), _par(Wf["conf_ln_b"][0])
    ls = _rw_fwd("l1_ln_silu", _f_ln_silu, [cv], [ln_w, ln_b], [D], out_dtypes=[BF16])
    p2 = _mm(ls, w_pw2, name="l1_pw2")
    b_pw2 = _par(Wf["conf_b_pw2"][0])
    h3 = _rw_fwd("l1_res1", _f_gate_res_bias, [h2, p2], [g1_1, b_pw2], [D])
    a3, hh1, gc1, act1, dn1 = ffn_fwd(1, h3, "l1")
    h4 = _rw_fwd("l1_res2", _f_gate_res, [h3, dn1], [g2_1], [D])

    fnw = final_norm_w[None, :]
    tgt = loss_target[0]
    loss_local = _loss_fwd(h4, tgt, fnw)[0, 0]
    loss = lax.psum(loss_local, ("x", "y", "c"))

    G_full = {}
    reduces = {}

    def start_reduce(tag, items, dep):
        def blocks_of(g, ax):
            if g.ndim == 3:
                return g
            return g.reshape(N_CHIPS, g.shape[0] // N_CHIPS, g.shape[1]) if ax == 0 else _to_blocks(g, ax)

        blocks = [blocks_of(g, ax).astype(BF16) for _, g, ax in items]
        handle, tok = _exchange4_start("reduce_" + tag + "_start", blocks, False, dep)
        reduces[tag] = ([n for n, _, _ in items], handle)
        return tok
    ones = jnp.ones((L, 1), F32)
    (dh4,), (dfnw,) = _rw_bwd("loss_bwd", _f_loss_rows, [h4, tgt], [_par(final_norm_w)], [ones],
                              row_grad=[True, False], par_grad=[True])
    G_full["final_norm_w"] = dfnw.reshape(D)

    def ffn_bwd(l, h, saved, g2_l, dh_out, tag):
        a, hh, gc, act, dn = saved
        (ddn,), (dg2,) = _rw_bwd(tag + "_res2_bwd", _f_gate, [dn], [g2_l], [dh_out],
                                 row_grad=[True], par_grad=[True], row_dtypes=[BF16])
        dact = _mm(ddn, w_dn[l], tb=True, name=tag + "_w_down_dx")
        dwdn = _mm(act, ddn, ta=True, name=tag + "_w_down_dw", out_dtype=BF16)
        (dval, dgc), _ = _rw_bwd(tag + "_act_bwd", _f_ffn_act, [(hh, 0, FH), gc], [], [dact],
                                 row_grad=[True, True], par_grad=[], col_tile=_tile(FH, 1536), row_dtypes=[BF16, F32])
        dgin, dcw, dcb = _conv_bwd(tag + "_ffn_conv_bwd", hh, FH, FH, Wf["ffn_conv_w"][l].reshape(9, FH), dgc,
                                   grid_taps, du_dtype=BF16)
        dhh = jnp.concatenate([dval, dgin], axis=1)
        da = _mm(dhh, w_up[l], tb=True, name=tag + "_w_up_dx")
        dwup = _mm(a, dhh, ta=True, name=tag + "_w_up_dw", out_dtype=BF16, col_blocks=N_CHIPS)
        (dh,), (dn2w, dsc2, dsh2) = _rw_bwd(
            tag + "_modnorm2_bwd", _f_modnorm, [h], [_par(norm2_w[l]), _par(sc2[l]), _par(sh2[l])], [da],
            row_grad=[True], par_grad=[True, True, True], add=dh_out)
        return dh, dict(w_down=dwdn, w_up=dwup, conv_w=dcw.reshape(3, 3, FH), conv_b=dcb.reshape(FH),
                        n2w=dn2w.reshape(D), sc2=dsc2.reshape(D), sh2=dsh2.reshape(D), g2=dg2.reshape(D))

    dh3, gf1 = ffn_bwd(1, h3, (a3, hh1, gc1, act1, dn1), g2_1, dh4, "l1")
    (dp2,), (dg1_1, db_pw2) = _rw_bwd("l1_res1_bwd", _f_gate_bias, [p2], [g1_1, b_pw2], [dh3],
                                      row_grad=[True], par_grad=[True, True], row_dtypes=[BF16])
    dls = _mm(dp2, w_pw2, tb=True, name="l1_pw2_dx")
    dw_pw2 = _mm(ls, dp2, ta=True, name="l1_pw2_dw", out_dtype=BF16)
    (dcv,), (dln_w, dln_b) = _rw_bwd("l1_ln_silu_bwd", _f_ln_silu, [cv], [ln_w, ln_b], [dls],
                                     row_grad=[True], par_grad=[True, True])
    dglu, dw_dw, db_dw = _conv_bwd("l1_conv_bwd", glu, 0, D, Wf["conf_w_dw"][0], dcv, conf_taps)
    (dpa, dpg), (dba, dbg) = _rw_bwd("l1_glu_bwd", _f_glu, [(pw, 0, D), (pw, D, D)],
                                     [_par(b_pw1[:D]), _par(b_pw1[D:])], [dglu],
                                     row_grad=[True, True], par_grad=[True, True], row_dtypes=[BF16, BF16])
    dpw = jnp.concatenate([dpa, dpg], axis=1)
    da2 = _mm(dpw, w_pw1, tb=True, name="l1_pw1_dx")
    dw_pw1 = _mm(a2, dpw, ta=True, name="l1_pw1_dw", out_dtype=BF16, col_blocks=N_CHIPS)
    (dh2,), (dn1w1, dsc1_1, dsh1_1) = _rw_bwd(
        "l1_modnorm1_bwd", _f_modnorm, [h2], [n1w1, _par(sc1[1]), _par(sh1[1])], [da2],
        row_grad=[True], par_grad=[True, True, True], add=dh3)
    G_full["conf_b_pw2"] = db_pw2.reshape(1, D)
    G_full["conf_ln_w"], G_full["conf_ln_b"] = dln_w.reshape(1, D), dln_b.reshape(1, D)
    G_full["conf_w_dw"], G_full["conf_b_dw"] = dw_dw[None], db_dw.reshape(1, D)
    G_full["conf_b_pw1"] = jnp.concatenate([dba.reshape(1, D), dbg.reshape(1, D)], axis=1)

    token = start_reduce("l1", [("conf_w_pw2", dw_pw2, 0), ("conf_w_pw1", dw_pw1, 1), ("ffn_w_up1", gf1["w_up"], 1),
                                ("ffn_w_down1", gf1["w_down"], 0)], dw_pw2)
    dh2 = _tie("tie_reduce_l1", dh2, token)
    dh1, gf0 = ffn_bwd(0, h1, (a1, hh0, gc0, act0, dn0), g2_0, dh2, "l0")
    G_full["ffn_conv_w"] = jnp.stack([gf0["conv_w"], gf1["conv_w"]])
    G_full["ffn_conv_b"] = jnp.stack([gf0["conv_b"], gf1["conv_b"]])

    (dmix,), (dg1_0,) = _rw_bwd("l0_res1_bwd", _f_gate, [mix0], [g1_0], [dh1],
                                row_grad=[True], par_grad=[True], row_dtypes=[BF16])
    dyn = _mm(dmix, w_out, tb=True, name="l0_w_out_dx")
    dw_out = _mm(yn, dmix, ta=True, name="l0_w_out_dw", out_dtype=BF16)
    token = start_reduce("l0", [("ffn_w_up0", gf0["w_up"], 1), ("ffn_w_down0", gf0["w_down"], 0),
                                ("ssd_w_out", dw_out, 0)], dw_out)
    dyn = _tie("tie_reduce_l0", dyn, token)
    (dy_lat, dxs_gate, dz_lat), (dd_rep, dssd_nw) = _rw_bwd(
        "l0_ssd_gate_bwd", _f_ssd_gate, gate_rows, [d_rep, ssd_nw], [dyn],
        row_grad=[True, False, True, True], par_grad=[True, True], T=L, row_dtypes=[F32, F32, BF16])
    g_f, g_b = _ssd_bwd(xbc, DI, DI + G * N, dtr, a_neg, s_enter, dy_lat, P, ncc)
    silu_bwd = functools.partial(_rw_bwd, f=_silu, pars=[], row_grad=[True], par_grad=[], T=T0)
    (dxs_pre,), _ = silu_bwd("l0_silu_bwd_x", rows=[(xbc_pre, 0, DI)], cot_fn=lambda p, q, r: p + q + r,
                             cots=[g_f[0], g_b[0], (dxs_gate, 0, DI, -Lc)],
                             col_tile=_tile(DI, 1024))
    (db_pre,), _ = silu_bwd("l0_silu_bwd_b", rows=[(xbc_pre, DI, G * N)], cot_fn=lambda p, q: p + q,
                            cots=[g_f[1], g_b[1]], col_tile=_tile(G * N, 1024))
    (dc_pre,), _ = silu_bwd("l0_silu_bwd_c", rows=[(xbc_pre, DI + G * N, G * N)], cot_fn=lambda p, q: p + q,
                            cots=[g_f[2], g_b[2]], col_tile=_tile(G * N, 1024))
    conv_w0 = Wf["ssd_conv_w"][0]
    pieces = []
    for tag, off, width, g_pre in (("x", 0, DI, dxs_pre), ("b", DI, G * N, db_pre), ("c", DI + G * N, G * N, dc_pre)):
        pieces.append(_conv_bwd("l0_conv_bwd_" + tag, proj, DI + off, width, conv_w0[:, off:off + width], g_pre,
                                seg_taps, du_dtype=BF16))
    dconv_in = [p[0] for p in pieces]
    dcw0 = jnp.concatenate([p[1] for p in pieces], axis=1)
    dcb0 = jnp.concatenate([p[2] for p in pieces], axis=1)
    ddt = jnp.concatenate([g_f[3][:, 0, :].T, g_b[3][:, 0, :].T], axis=1)
    (ddt_raw,), (ddt_bias,) = _rw_bwd("l0_softplus_bwd", _f_softplus, [dt_raw], [dt_bias], [ddt],
                                      row_grad=[True], par_grad=[True], row_dtypes=[BF16])
    dproj = jnp.concatenate([jnp.pad(dz_lat, ((Lc, 0), (0, 0))), *dconv_in, ddt_raw], axis=1)
    da0 = _mm(dproj, w_in, tb=True, name="l0_w_in_dx")
    dw_in = _mm(a0, dproj, ta=True, name="l0_w_in_dw", out_dtype=BF16)
    token = start_reduce("in", [("ssd_w_in", dw_in, 1)], dw_in)
    da0 = _tie("tie_reduce_in", da0, token)
    (dhcat,), (dn1w0, dsc_seg, dsh_seg) = _rw_bwd(
        "l0_modnorm1_bwd", _f_modnorm, [hcat], [n1w0, sc_seg, sh_seg], [da0],
        row_grad=[True], par_grad=[True, True, True], seg_rows=(Lc,))
    grad_x = (dhcat[Lc:] + dh1)[None]

    da_heads = jnp.stack([g[4][:, 0, 0].reshape(G, T0 // SSD_CHUNK, H // G).sum(axis=1).reshape(H)
                          for g in (g_f, g_b)])[None]
    G_full["ssd_a_log"] = da_heads * (-jnp.exp(ssd_a_log))
    G_full["ssd_dt_bias"] = ddt_bias.reshape(1, 2, H)
    G_full["ssd_d"] = dd_rep.reshape(H, P).sum(axis=1)[None]
    G_full["ssd_norm_w"] = dssd_nw.reshape(1, DI)
    G_full["ssd_conv_w"], G_full["ssd_conv_b"] = dcw0[None], dcb0.reshape(1, CD)
    G_full["norm1_w"] = jnp.stack([dn1w0.reshape(D), dn1w1.reshape(D)])
    G_full["norm2_w"] = jnp.stack([gf0["n2w"], gf1["n2w"]])

    zD = jnp.zeros((D,), F32)
    dm_own = jnp.stack([
        jnp.concatenate([dsh_seg[1, 0], dsc_seg[1, 0], dg1_0.reshape(D), gf0["sh2"], gf0["sc2"], gf0["g2"]]),
        jnp.concatenate([dsh1_1.reshape(D), dsc1_1.reshape(D), dg1_1.reshape(D), gf1["sh2"], gf1["sc2"], gf1["g2"]]),
    ])
    dmc_own = jnp.concatenate([dsh_seg[0, 0], dsc_seg[0, 0], zD, zD, zD, zD])

    out = {}

    def finish_reduce(tags, after, swap_name):
        partial = {}
        for tag in tags:
            names, handle = reduces[tag]
            blocks, landed = _exchange4_wait("reduce_" + tag + "_wait", handle, after)
            for n, blk, own in zip(names, landed, blocks):
                r = _fill_own(blk, own, chip, False)
                partial[n] = _sum_leading("sum4_" + n, r.reshape(N_CHIPS, -1, r.shape[-1]),
                                          (0, 1, 2, 3)).reshape(r.shape[1:])
        for n in ("ffn_w_up", "ffn_w_down"):
            if n + "0" in partial:
                partial[n] = jnp.stack([partial.pop(n + "0"), partial.pop(n + "1")])
        names = [n for n in BIG if n in partial]
        mine = [partial[n].reshape(W[n].shape) for n in names]
        for n, own, sib in zip(names, mine, _swap_sibling(swap_name, mine)):
            out[n] = _adamw("adamw_" + n, W[n], Mo[n], Vo[n], own, sib)
        return names

    early = finish_reduce(["l1", "l0"], dhcat, "swap_grads_early")

    small_sum_names = [n for n in SMALL if n not in ("c_ctx", "mod_b")]
    sum_part = [G_full[n] for n in small_sum_names] + [dmc_own]
    packed = _tie("tie_small_grads", _pack(sum_part + [dm_own]), out[early[-1]][1])
    gat = _allgather8("gather_small_grads", packed)
    total = _sum_leading("sum_small_grads", gat, tuple(range(N_DEV)))
    summed = _unpack(total, [a.shape for a in sum_part])
    Gs = dict(zip(small_sum_names, summed[:-1]))
    dmc_tot = summed[-1]
    dm_all = _unpack(gat, [a.shape for a in sum_part] + [dm_own.shape])[-1].transpose(1, 0, 2)
    dm16 = jnp.concatenate([dm_all, jnp.stack([dmc_tot, jnp.zeros_like(dmc_tot)])[:, None, :],
                            jnp.zeros((2, 16 - N_DEV - 1, 6 * D), F32)], axis=1)
    Gs["mod_b"] = _sum_leading("sum_mod_b", dm16.transpose(1, 0, 2).reshape(16, 2 * 6 * D // LANE, LANE),
                               tuple(range(N_DEV + 1))).reshape(2, 6 * D)

    dm16_shard = lax.dynamic_slice_in_dim(dm16, chip * S_mod, S_mod, axis=2)
    ds16 = _mm(dm16_shard[0], mod_w[0], tb=True, precision=HIGHEST, name="c_ctx_dx")
    sig = jax.nn.sigmoid(c_ctx)
    dcc_part = ds16[N_DEV] * (sig * (1.0 + c_ctx * (1.0 - sig)))
    gat_cc = _allgather8("gather_c_ctx_grad", _pack([dcc_part]))
    Gs["c_ctx"] = _sum_leading("sum_c_ctx_grad", gat_cc, (0, 2, 4, 6)).reshape(-1)[:D]

    s16t = _silu(c16).T
    out["mod_w"] = _mod_w_update(s16t, dm16_shard, mod_w, m_mod_w, v_mod_w)
    finish_reduce(["in"], out["mod_w"][0], "swap_grads_late")

    def own(n, full):
        if n in SHARD_AXIS:
            size = W[n].shape[SHARD_AXIS[n]]
            return lax.dynamic_slice_in_dim(full, chip * size, size, axis=SHARD_AXIS[n])
        return full

    def two_d(a):
        return a.reshape(1, -1) if a.ndim == 1 else a

    g_small = [own(n, Gs[n].reshape(Wf[n].shape)) for n in SMALL]
    res = _adamw_many("adamw_small", [two_d(W[n]) for n in SMALL], [two_d(Mo[n]) for n in SMALL],
                      [two_d(Vo[n]) for n in SMALL], [two_d(g) for g in g_small])
    for n, r in zip(SMALL, res):
        out[n] = tuple(t.reshape(W[n].shape) for t in r)

    grads = [out[n][0] for n in WEIGHTS]
    deltas = [out[n][1] for n in WEIGHTS]
    new_m = [out[n][2] for n in WEIGHTS]
    new_v = [out[n][3] for n in WEIGHTS]
    return (loss, grad_x, *grads, *deltas, *new_m, *new_v)
```

```python
import functools

import jax
import jax.numpy as jnp
from jax import lax
from jax.experimental import pallas as pl
from jax.experimental.pallas import tpu as pltpu

F32 = jnp.float32
BF16 = jnp.bfloat16
MESH = pl.DeviceIdType.MESH
HIGHEST = lax.Precision.HIGHEST

VMEM_LIMIT_BYTES = 48 * 1024 * 1024
LANE = 128
SUBLANE = 8

SSD_STATE = 128
SSD_CHUNK = 128
GRID_W = 64
EPS = 1e-6
N_CHIPS = 4
N_DEV = 8

ADAM_LR = 0.001
ADAM_B1 = 0.9
ADAM_B2 = 0.999
ADAM_EPS = 1e-08
ADAM_WD = 0.01
ADAM_STEP = 10


def _pcall(body, **kw):
    return pl.pallas_call(body, **kw)


def _cparams(n_grid):
    return pltpu.CompilerParams(dimension_semantics=("arbitrary",) * n_grid, vmem_limit_bytes=VMEM_LIMIT_BYTES)


def _cdiv(a, b):
    return -(-a // b)


def _round_up(a, b):
    return _cdiv(a, b) * b


def _tile(n, cap):
    if n <= cap:
        return n
    best = None
    for t in range(LANE, cap + 1, LANE):
        if n % t == 0:
            best = t
    if best is None:
        npad = _round_up(n, LANE)
        for t in range(LANE, cap + 1, LANE):
            if npad % t == 0:
                best = t
    return best


def _row_tile(n, cap, also=()):
    best = None
    for step in (2 * SUBLANE, SUBLANE):
        for t in range(step, min(cap, n) + 1, step):
            if n % t == 0 and all(a % t == 0 for a in also):
                best = t
        if best is not None:
            break
    assert best is not None, (n, cap, also)
    return best


def _silu(v):
    return v * jax.nn.sigmoid(v)


def _mm(a, b, *, name, ta=False, tb=False, precision=None, cap=1024, out_dtype=F32, col_blocks=None):
    M, K = (a.shape[1], a.shape[0]) if ta else a.shape
    N = b.shape[0] if tb else b.shape[1]
    assert K == (b.shape[1] if tb else b.shape[0]), (a.shape, b.shape, ta, tb)
    tm, tk = _tile(M, cap), _tile(K, cap + cap // 2)
    tn = _tile(N if col_blocks is None else N // col_blocks, cap + cap // 2)
    nm, nn, nk = _cdiv(M, tm), _cdiv(N, tn), _cdiv(K, tk)
    k_tail = K % tk
    exact = precision is not None

    def body(a_ref, b_ref, o_ref, acc_ref):
        k = pl.program_id(2)

        @pl.when(k == 0)
        def _():
            acc_ref[...] = jnp.zeros_like(acc_ref)

        av = a_ref[...]
        bv = b_ref[...]
        if k_tail:
            lim = K - k * tk
            ka = lax.broadcasted_iota(jnp.int32, av.shape, 0 if ta else 1)
            kb = lax.broadcasted_iota(jnp.int32, bv.shape, 1 if tb else 0)
            av = jnp.where(ka < lim, av, jnp.zeros_like(av))
            bv = jnp.where(kb < lim, bv, jnp.zeros_like(bv))
        if exact:
            av = av.astype(F32)
            bv = bv.astype(F32)
        else:
            av = av.astype(BF16)
            bv = bv.astype(BF16)
        dn = (((0 if ta else 1,), (1 if tb else 0,)), ((), ()))
        acc_ref[...] += lax.dot_general(av, bv, dn, preferred_element_type=F32, precision=precision)

        @pl.when(k == nk - 1)
        def _():
            o_ref[...] = acc_ref[...].astype(o_ref.dtype)

    a_spec = pl.BlockSpec((tk, tm), lambda i, j, k: (k, i)) if ta else pl.BlockSpec((tm, tk), lambda i, j, k: (i, k))
    b_spec = pl.BlockSpec((tn, tk), lambda i, j, k: (j, k)) if tb else pl.BlockSpec((tk, tn), lambda i, j, k: (k, j))
    if col_blocks is None:
        out_spec = pl.BlockSpec((tm, tn), lambda i, j, k: (i, j))
        out_shape = jax.ShapeDtypeStruct((M, N), out_dtype)
    else:
        per = (N // col_blocks) // tn
        assert per * tn * col_blocks == N, (N, col_blocks, tn)
        out_spec = pl.BlockSpec((None, tm, tn), lambda i, j, k: (j // per, i, j % per))
        out_shape = jax.ShapeDtypeStruct((col_blocks, M, N // col_blocks), out_dtype)
    return _pcall(
        body, name=name, grid=(nm, nn, nk), in_specs=[a_spec, b_spec], out_specs=out_spec, out_shape=out_shape,
        scratch_shapes=[pltpu.VMEM((tm, tn), F32)], compiler_params=_cparams(3),
    )(a, b)


def _norm_rows(rows):
    out = []
    for r in rows:
        if not isinstance(r, tuple):
            r = (r,)
        arr, off, width, roff = (r + (0, None, 0)[len(r) - 1:])
        out.append((arr, off, width if width is not None else arr.shape[1], roff))
    return out


def _rw_plan(T, rows, pars, seg_rows, col_tile, tm_cap):
    widths = [r[2] for r in rows]
    wmax = max(widths + [p.shape[-1] for p in pars] + [1])
    if col_tile is not None:
        assert all(w == widths[0] for w in widths) and all(p.shape[-1] == widths[0] for p in pars)
        ncol = widths[0] // col_tile
        assert ncol * col_tile == widths[0]
        wmax = col_tile
    else:
        ncol = 1
    cap = tm_cap if tm_cap is not None else max(SUBLANE, min(256, (256 * 1024) // wmax))
    tm = _row_tile(T, cap, also=tuple(seg_rows) + tuple(abs(r[3]) for r in rows if r[3]))
    bounds = tuple(s // tm for s in seg_rows)
    return widths, ncol, tm, bounds


def _rw_specs(rows, pars, ncol, tm, bounds, col_tile):
    def seg(i):
        s = 0
        for b in bounds:
            s = s + (i >= b).astype(jnp.int32)
        return s

    specs = []
    for arr, off, w, roff in rows:
        bw = col_tile if col_tile is not None else w
        assert off % bw == 0 and roff % tm == 0, (off, bw, roff, tm)
        specs.append(pl.BlockSpec((tm, bw), functools.partial(
            lambda j, i, ob, rb, last: (jnp.clip(i + rb, 0, last), ob + j),
            ob=off // bw, rb=roff // tm, last=arr.shape[0] // tm - 1)))
    for p in pars:
        bw = col_tile if col_tile is not None else p.shape[-1]
        if p.shape[0] > 1:
            specs.append(pl.BlockSpec((None, 1, bw), lambda j, i: (seg(i), 0, j)))
        else:
            specs.append(pl.BlockSpec((None, 1, bw), lambda j, i: (0, 0, j)))
    return specs, seg


def _head_rows(head):
    top, bottom = head
    return [(top, 0, None, 0), (bottom, 0, None, -top.shape[0])]


def _rw_fwd(name, f, rows, pars, out_widths, *, T=None, seg_rows=(), col_tile=None, tm_cap=None, out_dtypes=None,
            head=None):
    rows = _norm_rows((_head_rows(head) if head else []) + list(rows))
    T = rows[0][0].shape[0] if T is None else T
    widths, ncol, tm, bounds = _rw_plan(T, rows, pars, seg_rows, col_tile, tm_cap)
    in_specs, _ = _rw_specs(rows, pars, ncol, tm, bounds, col_tile)
    nr, npar, nout = len(rows), len(pars), len(out_widths)

    def body(*refs):
        vals = [r[...] for r in refs[:nr + npar]]
        if head:
            vals = [jnp.where(pl.program_id(1) < head[0].shape[0] // tm, vals[0], vals[1])] + vals[2:]
        outs = f(*vals)
        if not isinstance(outs, (tuple, list)):
            outs = (outs,)
        for o_ref, o in zip(refs[nr + npar:], outs):
            o_ref[...] = o.astype(o_ref.dtype)

    out_specs = [pl.BlockSpec((tm, col_tile if col_tile is not None else w), lambda j, i: (i, j)) for w in out_widths]
    res = _pcall(
        body, name=name, grid=(ncol, T // tm), in_specs=in_specs, out_specs=out_specs,
        out_shape=[jax.ShapeDtypeStruct((T, w), dt) for w, dt in zip(out_widths, out_dtypes or [F32] * nout)],
        compiler_params=_cparams(2),
    )(*[r[0] for r in rows], *pars)
    return res if nout > 1 else res[0]


def _rw_bwd(name, f, rows, pars, cots, *, row_grad, par_grad, T=None, seg_rows=(), col_tile=None, tm_cap=None,
            add=None, cot_fn=None, row_dtypes=None, head=None):
    rows = _norm_rows((_head_rows(head) if head else []) + list(rows))
    cots = _norm_rows(cots)
    T = rows[0][0].shape[0] if T is None else T
    extra = _norm_rows([add]) if add is not None else []
    all_rows = rows + cots + extra
    widths, ncol, tm, bounds = _rw_plan(T, all_rows, pars, seg_rows, col_tile, tm_cap)
    in_specs, seg = _rw_specs(all_rows, pars, ncol, tm, bounds, col_tile)
    nr, nc, ne, npar = len(rows), len(cots), len(extra), len(pars)
    skip = 1 if head else 0
    widths = widths[skip:]
    nrf = nr - skip
    row_idx = [k for k in range(nrf) if row_grad[k]]
    par_idx = [k for k in range(npar) if par_grad[k]]

    def body(*refs):
        i = pl.program_id(1)

        def zero_before(vals, ops):
            return [jnp.where(i + c[3] // tm >= 0, v, jnp.zeros_like(v)) if c[3] < 0 else v for v, c in zip(vals, ops)]

        row_vals = [r[...] for r in refs[:nr]]
        if head:
            row_vals = [jnp.where(i < head[0].shape[0] // tm, row_vals[0], row_vals[1])] + row_vals[2:]
        cot_vals = zero_before([r[...] for r in refs[nr:nr + nc]], cots)
        add_vals = zero_before([r[...] for r in refs[nr + nc:nr + nc + ne]], extra)
        par_vals = [r[...] for r in refs[nr + nc + ne:nr + nc + ne + npar]]
        out_refs = refs[nr + nc + ne + npar:]
        outs, vjp = jax.vjp(f, *row_vals, *par_vals)
        if cot_fn is not None:
            cot_vals = cot_fn(*cot_vals)
            if not isinstance(cot_vals, (tuple, list)):
                cot_vals = (cot_vals,)
        if isinstance(outs, (tuple, list)):
            grads = vjp(tuple(c.astype(o.dtype) for c, o in zip(cot_vals, outs)))
        else:
            grads = vjp(cot_vals[0].astype(outs.dtype))
        first_seg = i == 0
        for b in bounds:
            first_seg = first_seg | (i == b)
        for n, k in enumerate(row_idx):
            g = grads[k]
            if n == 0 and add_vals:
                g = g + add_vals[0]
            out_refs[n][...] = g.astype(out_refs[n].dtype)
        for n, k in enumerate(par_idx):
            g = grads[nrf + k]
            o_ref = out_refs[len(row_idx) + n]
            first = first_seg if pars[k].shape[0] > 1 else (i == 0)

            @pl.when(first)
            def _(o_ref=o_ref, g=g):
                o_ref[...] = g

            @pl.when(jnp.logical_not(first))
            def _(o_ref=o_ref, g=g):
                o_ref[...] += g

    out_specs, out_shape = [], []
    for k in row_idx:
        w = widths[k]
        out_specs.append(pl.BlockSpec((tm, col_tile if col_tile is not None else w), lambda j, i: (i, j)))
        out_shape.append(jax.ShapeDtypeStruct((T, w), row_dtypes[len(out_shape)] if row_dtypes else F32))
    for k in par_idx:
        p = pars[k]
        bw = col_tile if col_tile is not None else p.shape[-1]
        if p.shape[0] > 1:
            out_specs.append(pl.BlockSpec((None, 1, bw), lambda j, i: (seg(i), 0, j)))
        else:
            out_specs.append(pl.BlockSpec((None, 1, bw), lambda j, i: (0, 0, j)))
        out_shape.append(jax.ShapeDtypeStruct(p.shape, F32))
    res = _pcall(
        body, name=name, grid=(ncol, T // tm), in_specs=in_specs, out_specs=out_specs, out_shape=out_shape,
        compiler_params=_cparams(2),
    )(*[r[0] for r in all_rows], *pars)
    return list(res[:len(row_idx)]), list(res[len(row_idx):])


def _f_modnorm(h, w, sc, sh):
    y = h * lax.rsqrt(jnp.mean(h * h, axis=-1, keepdims=True) + EPS)
    return (y * w) * (1.0 + sc) + sh


def _f_gate_res(h, y, g):
    return h + g * y


def _f_gate_res_bias(h, y, g, b):
    return h + g * (y + b)


def _f_gate(y, g):
    return g * y


def _f_gate_bias(y, g, b):
    return g * (y + b)


def _f_ffn_act(val, gate):
    return _silu(gate) * val


def _f_softplus(raw, bias):
    v = raw + bias
    return jnp.maximum(v, 0.0) + jnp.log(1.0 + jnp.exp(-jnp.abs(v)))


def _f_ssd_gate(yf, yb, xs, z, d_rep, nw):
    y = (yf + yb + d_rep * xs) * _silu(z)
    return (y * lax.rsqrt(jnp.mean(y * y, axis=-1, keepdims=True) + EPS)) * nw


def _f_glu(a, g, ba, bg):
    return (a + ba) * jax.nn.sigmoid(g + bg)


def _f_ln_silu(h, w, b):
    mu = jnp.mean(h, axis=-1, keepdims=True)
    d = h - mu
    y = d * lax.rsqrt(jnp.mean(d * d, axis=-1, keepdims=True) + EPS)
    return _silu(y * w + b)


def _f_loss_rows(h, t, w):
    y = (h * lax.rsqrt(jnp.mean(h * h, axis=-1, keepdims=True) + EPS)) * w
    e = y - t
    return 0.5 * jnp.mean(e * e, axis=-1, keepdims=True)


def _f_adamw(w, m, v, ga, gb):
    g = ga + gb
    m = ADAM_B1 * m + (1.0 - ADAM_B1) * g
    v = ADAM_B2 * v + (1.0 - ADAM_B2) * (g * g)
    m_hat = m / (1.0 - ADAM_B1 ** ADAM_STEP)
    v_hat = v / (1.0 - ADAM_B2 ** ADAM_STEP)
    delta = -ADAM_LR * (m_hat / (jnp.sqrt(v_hat) + ADAM_EPS) + ADAM_WD * w)
    return g, delta, m, v


def _adamw(name, w, m, v, ga, gb):
    shape = w.shape
    c = shape[-1]
    two_d = [t.reshape(-1, c) for t in (w, m, v, ga, gb)]
    rows = two_d[0].shape[0]
    pad = _round_up(rows, SUBLANE) - rows
    if pad:
        two_d = [jnp.pad(t, ((0, pad), (0, 0))) for t in two_d]
    outs = _rw_fwd(name, _f_adamw, two_d, [], [c] * 4)
    return tuple(o[:rows].reshape(shape) for o in outs)


def _sum_leading(name, x, idxs):
    _, R, C = x.shape
    tm = _row_tile(R, max(SUBLANE, min(512, (512 * 1024) // C)))

    def body(x_ref, o_ref):
        acc = x_ref[idxs[0]].astype(F32)
        for k in idxs[1:]:
            acc = acc + x_ref[k].astype(F32)
        o_ref[...] = acc

    return _pcall(
        body, name=name, grid=(R // tm,), in_specs=[pl.BlockSpec((x.shape[0], tm, C), lambda i: (0, i, 0))],
        out_specs=pl.BlockSpec((tm, C), lambda i: (i, 0)), out_shape=jax.ShapeDtypeStruct((R, C), F32),
        compiler_params=_cparams(1),
    )(x)


def _loss_fwd(h, t, w):
    T, D = h.shape
    tm = _row_tile(T, 256)

    def body(h_ref, t_ref, w_ref, o_ref):
        i = pl.program_id(0)
        part = jnp.sum(_f_loss_rows(h_ref[...], t_ref[...], w_ref[...]), axis=0, keepdims=True)
        part = jnp.broadcast_to(part, (1, LANE))

        @pl.when(i == 0)
        def _():
            o_ref[...] = part

        @pl.when(i > 0)
        def _():
            o_ref[...] += part

    return _pcall(
        body, name="loss_fwd", grid=(T // tm,),
        in_specs=[pl.BlockSpec((tm, D), lambda i: (i, 0)), pl.BlockSpec((tm, D), lambda i: (i, 0)),
                  pl.BlockSpec((1, D), lambda i: (0, 0))],
        out_specs=pl.BlockSpec((1, LANE), lambda i: (0, 0)), out_shape=jax.ShapeDtypeStruct((1, LANE), F32),
        compiler_params=_cparams(1),
    )(h, t, w)


CONV_ROWS = 256
CONV_ROWS_FEW_TAPS = 1024
CONV_ACC_ELEMS = 16384


def _col_mask(arg, t):
    col = jnp.bitwise_and(t, GRID_W - 1)
    return (col != 0) if arg < 0 else (col != GRID_W - 1)


def _conv_plan(T, C, taps):
    seg = [m[1] for _, m in taps if m is not None and m[0] == "seg"]
    cap = CONV_ROWS_FEW_TAPS if len(taps) <= 9 else CONV_ROWS
    rc = next(r for r in (1024, 768, 512, 256, LANE) if r <= cap and T % r == 0)
    ct = next((t for t in (512, 256, LANE) if C % t == 0), C)
    reach = max(abs(s) for s, _ in taps)
    hb = next(h for h in (8, 16, 32, 64, 128, 256) if h >= reach and rc % h == 0)
    sub = max(2 * SUBLANE, min(rc, CONV_ACC_ELEMS // ct))
    boundary = None
    if seg:
        inside = seg[0] % rc
        boundary = (seg[0], (inside - reach, inside + reach) if inside else None)
    taps = [(s, None if (m is None or m[0] == "seg") else m[1]) for s, m in taps]
    return rc, ct, hb, sub, T // rc, C // ct, boundary, taps


def _seg_ok(boundary, i, rc, r0, n, s):
    if boundary is None or boundary[1] is None or s == 0 or r0 + n <= boundary[1][0] or r0 >= boundary[1][1]:
        return None
    t = i * rc + r0 + lax.broadcasted_iota(jnp.int32, (n, 1), 0)
    return (t >= boundary[0]) == ((t + s) >= boundary[0])


def _halo_specs(rc, ct, hb, T, off_blocks):
    per = rc // hb
    last = T // hb - 1
    prev = pl.BlockSpec((hb, ct), lambda j, i: (jnp.maximum(i * per - 1, 0), off_blocks + j))
    cur = pl.BlockSpec((rc, ct), lambda j, i: (i, off_blocks + j))
    nxt = pl.BlockSpec((hb, ct), lambda j, i: (jnp.minimum((i + 1) * per, last), off_blocks + j))
    return [prev, cur, nxt]


def _fill_halo(pad_ref, p_ref, c_ref, n_ref, i, nrc, rc, hb, boundary):
    has_prev = i > 0
    has_next = i < nrc - 1
    if boundary is not None:
        has_prev = has_prev & (i * rc != boundary[0])
        has_next = has_next & ((i + 1) * rc != boundary[0])
    pad_ref[0:hb, :] = jnp.where(has_prev, p_ref[...], 0.0)
    pad_ref[hb:hb + rc, :] = c_ref[...]
    pad_ref[hb + rc:hb + rc + hb, :] = jnp.where(has_next, n_ref[...], 0.0)


def _shift_plan(keys):
    count = {}
    for s, m in keys:
        k = (s % SUBLANE, m)
        count[k] = count.get(k, 0) + 1
    slots = {}
    for k, n in sorted(count.items(), key=lambda kv: (kv[0][0], str(kv[0][1]))):
        if k != (0, None) and (n >= 2 or k[1] is not None):
            slots[k] = len(slots)
    return slots


def _build_shifted(copies_ref, slots, pad_ref, keys, i, rc, hb, sub):
    for (r, m), slot in slots.items():
        qs = [s - r for s, mk in keys if (s % SUBLANE, mk) == (r, m)]
        lo, hi = hb + min(qs), hb + rc + max(qs)
        for p in range(lo, hi, sub):
            n = min(sub, hi - p)
            v = pad_ref[p + r:p + r + n, :]
            if m is not None:
                t = i * rc - hb + p + r + lax.broadcasted_iota(jnp.int32, (n, 1), 0)
                v = jnp.where(_col_mask(m, t), v, 0.0)
            copies_ref[slot, p:p + n, :] = v


def _read(copies_ref, slots, pad_ref, s, m, row, n):
    k = (s % SUBLANE, m)
    if k in slots:
        q = s - k[0]
        return copies_ref[slots[k], row + q:row + q + n, :]
    return pad_ref[row + s:row + s + n, :]


def _conv_fwd(name, u, col_off, C, w, b, taps, act=False):
    T = u.shape[0]
    rc, ct, hb, sub, nrc, ncc, boundary, taps = _conv_plan(T, C, taps)
    assert col_off % ct == 0
    K = len(taps)
    keys = [(s, None) for s, _ in taps]
    slots = _shift_plan(keys)
    dirs = sorted({m for _, m in taps if m is not None})

    def body(up, uc, un, w_ref, b_ref, *rest):
        y_ref = rest[0]
        pad_ref, copies_ref = rest[-2], rest[-1]
        i = pl.program_id(1)
        _fill_halo(pad_ref, up, uc, un, i, nrc, rc, hb, boundary)
        _build_shifted(copies_ref, slots, pad_ref, keys, i, rc, hb, sub)
        for r0 in range(0, rc, sub):
            acc = jnp.broadcast_to(b_ref[...], (sub, ct))
            for m in [None] + dirs:
                part = None
                for k, (s, mk) in enumerate(taps):
                    if mk != m:
                        continue
                    v = _read(copies_ref, slots, pad_ref, s, None, hb + r0, sub)
                    ok = _seg_ok(boundary, i, rc, r0, sub, s)
                    term = w_ref[k:k + 1, :] * (v if ok is None else jnp.where(ok, v, 0.0))
                    part = term if part is None else part + term
                if part is None:
                    continue
                if m is not None:
                    t = i * rc + r0 + lax.broadcasted_iota(jnp.int32, (sub, 1), 0)
                    part = jnp.where(_col_mask(m, t), part, 0.0)
                acc = acc + part
            y_ref[r0:r0 + sub, :] = acc
            if act:
                rest[1][r0:r0 + sub, :] = _silu(acc)

    n_out = 2 if act else 1
    res = _pcall(
        body, name=name, grid=(ncc, nrc),
        in_specs=_halo_specs(rc, ct, hb, T, col_off // ct) + [pl.BlockSpec((K, ct), lambda j, i: (0, j)),
                                                              pl.BlockSpec((1, ct), lambda j, i: (0, j))],
        out_specs=[pl.BlockSpec((rc, ct), lambda j, i: (i, j))] * n_out,
        out_shape=[jax.ShapeDtypeStruct((T, C), F32)] * n_out,
        scratch_shapes=[pltpu.VMEM((rc + 2 * hb, ct), F32), pltpu.VMEM((max(len(slots), 1), rc + 2 * hb, ct), F32)],
        compiler_params=_cparams(2),
    )(u, u, u, w, b)
    return res if act else res[0]


def _conv_bwd(name, u, col_off, C, w, g, taps, du_dtype=F32):
    T = u.shape[0]
    rc, ct, hb, sub, nrc, ncc, boundary, taps = _conv_plan(T, C, taps)
    K = len(taps)
    u_keys = [(s, None) for s, _ in taps]
    dirs = sorted({m for _, m in taps if m is not None})
    g_keys = [(-s, m) for s, m in taps] + [(0, m) for m in dirs]
    u_slots, g_slots = _shift_plan(u_keys), _shift_plan(g_keys)

    def body(up, uc, un, gp, gc, gn, w_ref, du_ref, dw_ref, db_ref, upad, gpad, ucopies, gcopies):
        i = pl.program_id(1)
        _fill_halo(upad, up, uc, un, i, nrc, rc, hb, boundary)
        _fill_halo(gpad, gp, gc, gn, i, nrc, rc, hb, boundary)
        _build_shifted(ucopies, u_slots, upad, u_keys, i, rc, hb, sub)
        _build_shifted(gcopies, g_slots, gpad, g_keys, i, rc, hb, sub)

        @pl.when(i == 0)
        def _():
            dw_ref[...] = jnp.zeros_like(dw_ref)
            db_ref[...] = jnp.zeros_like(db_ref)

        def fold(v):
            return jnp.sum(v.reshape(sub // SUBLANE, SUBLANE, ct), axis=0)

        dbs = jnp.zeros((SUBLANE, ct), F32)
        for r0 in range(0, rc, sub):
            dbs = dbs + fold(gpad[hb + r0:hb + r0 + sub, :])
            acc = jnp.zeros((sub, ct), F32)
            for k, (s, m) in enumerate(taps):
                v = _read(gcopies, g_slots, gpad, -s, m, hb + r0, sub)
                ok = _seg_ok(boundary, i, rc, r0, sub, -s)
                acc = acc + w_ref[k:k + 1, :] * (v if ok is None else jnp.where(ok, v, 0.0))
            du_ref[r0:r0 + sub, :] = acc.astype(du_ref.dtype)
        db_ref[...] += jnp.sum(dbs, axis=0, keepdims=True)
        for k, (s, m) in enumerate(taps):
            part = jnp.zeros((SUBLANE, ct), F32)
            for r0 in range(0, rc, sub):
                v = _read(ucopies, u_slots, upad, s, None, hb + r0, sub)
                ok = _seg_ok(boundary, i, rc, r0, sub, s)
                part = part + fold(_read(gcopies, g_slots, gpad, 0, m, hb + r0, sub)
                                   * (v if ok is None else jnp.where(ok, v, 0.0)))
            dw_ref[k:k + 1, :] += jnp.sum(part, axis=0, keepdims=True)

    halo_u = _halo_specs(rc, ct, hb, T, col_off // ct)
    halo_g = _halo_specs(rc, ct, hb, T, 0)
    rows = rc + 2 * hb
    return _pcall(
        body, name=name, grid=(ncc, nrc),
        in_specs=halo_u + halo_g + [pl.BlockSpec((K, ct), lambda j, i: (0, j))],
        out_specs=[pl.BlockSpec((rc, ct), lambda j, i: (i, j)), pl.BlockSpec((K, ct), lambda j, i: (0, j)),
                   pl.BlockSpec((1, ct), lambda j, i: (0, j))],
        out_shape=[jax.ShapeDtypeStruct((T, C), du_dtype), jax.ShapeDtypeStruct((K, C), F32),
                   jax.ShapeDtypeStruct((1, C), F32)],
        scratch_shapes=[pltpu.VMEM((rows, ct), F32), pltpu.VMEM((rows, ct), F32),
                        pltpu.VMEM((max(len(u_slots), 1), rows, ct), F32),
                        pltpu.VMEM((max(len(g_slots), 1), rows, ct), F32)],
        compiler_params=_cparams(2),
    )(u, u, u, g, g, g, w)


def _ssd_group(xg, bm, cm, s_in, *per_head, reverse, P):
    R = len(per_head) // 2
    dtrs, a_s = per_head[:R], per_head[R:]
    q, rp = xg.shape
    ii = lax.broadcasted_iota(jnp.int32, (q, q), 0)
    jj = lax.broadcasted_iota(jnp.int32, (q, q), 1)
    causal = (jj >= ii) if reverse else (jj <= ii)
    causal_t = (ii >= jj) if reverse else (ii <= jj)
    eye = ii == jj
    lane = lax.broadcasted_iota(jnp.int32, (1, rp), 1)
    row = lax.broadcasted_iota(jnp.int32, (rp, 1), 0)
    nt = (((1,), (1,)), ((), ()))
    tn = (((0,), (0,)), ((), ()))
    cb = lax.dot_general(cm.astype(BF16), bm.astype(BF16), nt, preferred_element_type=F32)
    dt_x = jnp.zeros((q, rp), F32)
    acum_x = jnp.zeros((q, rp), F32)
    tot_row = jnp.zeros((1, rp), F32)
    tot_col = jnp.zeros((rp, 1), F32)
    wts, lane_masks = [], []
    for r in range(R):
        hm = (lane >= r * P) & (lane < (r + 1) * P)
        hc = (row >= r * P) & (row < (r + 1) * P)
        dt_c = jnp.sum(jnp.where(eye, dtrs[r], 0.0), axis=1, keepdims=True)
        dac = dt_c * a_s[r]
        dar = dtrs[r] * a_s[r]
        acum_c = jnp.sum(jnp.where(causal, dar, 0.0), axis=1, keepdims=True)
        acum_r = jnp.sum(jnp.where(causal_t, dac, 0.0), axis=0, keepdims=True)
        decay = jnp.where(causal, jnp.exp(jnp.where(causal, acum_c - acum_r, 0.0)), 0.0)
        tot = jnp.sum(dac, axis=0, keepdims=True)
        dt_x = jnp.where(hm, dt_c, dt_x)
        acum_x = jnp.where(hm, acum_c, acum_x)
        tot_row = jnp.where(hm, tot, tot_row)
        tot_col = jnp.where(hc, tot, tot_col)
        wts.append((cb * decay).astype(BF16))
        lane_masks.append(hm)
    xdt = xg * dt_x
    xdt_b = xdt.astype(BF16)
    y = jnp.zeros((q, rp), F32)
    for r in range(R):
        y = jnp.where(lane_masks[r], jnp.dot(wts[r], xdt_b, preferred_element_type=F32), y)
    dte = jnp.exp(tot_row - acum_x)
    cs = lax.dot_general((xdt * dte).astype(BF16), bm.astype(BF16), tn, preferred_element_type=F32)
    y = y + lax.dot_general(cm.astype(BF16), s_in.astype(BF16), nt, preferred_element_type=F32) * jnp.exp(acum_x)
    s_out = jnp.exp(tot_col) * s_in + cs
    return y, s_out


def _ssd_maps(NC, ncc, reverse_steps):
    def chunk(d, s):
        if reverse_steps:
            s = NC - 1 - s
        return s if d == 0 else jnp.where(s < ncc, ncc - 1 - s, NC - 1 - s + ncc)

    def lat_chunk(d, s):
        c = chunk(d, s) - ncc
        return jnp.where(c < 0, 0 if d == 0 else NC - ncc - 1, c)

    def step(s):
        return NC - 1 - s if reverse_steps else s

    return chunk, lat_chunk, step


def _ssd_specs(chunk, d, R, Q, N, RP, bo, co):
    return [
        pl.BlockSpec((Q, RP), lambda g, s: (chunk(d, s), g)),
        pl.BlockSpec((Q, N), lambda g, s: (chunk(d, s), bo + g)),
        pl.BlockSpec((Q, N), lambda g, s: (chunk(d, s), co + g)),
        pl.BlockSpec((R, 1, Q), lambda g, s: (g, 0, chunk(d, s))),
        pl.BlockSpec((R, 1, 1), lambda g, s: (g, 0, 0)),
    ]


def _ssd_fwd(xbc, b_off, c_off, dtr, a, P, ncc):
    T = xbc.shape[0]
    H = dtr[0].shape[0]
    N, Q = SSD_STATE, SSD_CHUNK
    NC = T // Q
    G = (c_off - b_off) // N
    R = H // G
    RP = R * P
    chunk, lat_chunk, _ = _ssd_maps(NC, ncc, False)

    def body(*refs):
        s = pl.program_id(1)
        s_ref = refs[-1]

        @pl.when(s == 0)
        def _():
            s_ref[...] = jnp.zeros_like(s_ref)

        for d in range(2):
            x_ref, b_ref, c_ref, dtr_ref, a_ref = refs[5 * d:5 * d + 5]
            y_ref, se_ref = refs[10 + 2 * d:12 + 2 * d]
            s_in = s_ref[d]
            se_ref[...] = s_in
            per_head = [dtr_ref[r] for r in range(R)] + [a_ref[r] for r in range(R)]
            y, s_out = _ssd_group(x_ref[...], b_ref[...], c_ref[...], s_in, *per_head, reverse=d == 1, P=P)
            y_ref[...] = y
            s_ref[d] = s_out

    in_specs, out_specs, out_shape, operands = [], [], [], []
    for d in range(2):
        in_specs += _ssd_specs(chunk, d, R, Q, N, RP, b_off // N, c_off // N)
        operands += [xbc, xbc, xbc, dtr[d], a[d]]
        out_specs += [pl.BlockSpec((Q, RP), functools.partial(lambda g, s, d: (lat_chunk(d, s), g), d=d)),
                      pl.BlockSpec((None, None, RP, N), lambda g, s: (g, s, 0, 0))]
        out_shape += [jax.ShapeDtypeStruct((T - ncc * Q, H * P), F32), jax.ShapeDtypeStruct((G, NC, RP, N), F32)]
    y_f, se_f, y_b, se_b = _pcall(
        body, name="ssd_fwd", grid=(G, NC), in_specs=in_specs, out_specs=out_specs, out_shape=out_shape,
        scratch_shapes=[pltpu.VMEM((2, RP, N), F32)], compiler_params=_cparams(2),
    )(*operands)
    return (y_f, y_b), (se_f, se_b)


def _ssd_bwd(xbc, b_off, c_off, dtr, a, s_enter, dy, P, ncc):
    T = xbc.shape[0]
    H = dtr[0].shape[0]
    N, Q = SSD_STATE, SSD_CHUNK
    NC = T // Q
    G = (c_off - b_off) // N
    R = H // G
    RP = R * P
    chunk, lat_chunk, step = _ssd_maps(NC, ncc, True)
    n_in, n_out = 7, 5

    def body(*refs):
        s = pl.program_id(1)
        ds_ref = refs[-1]

        @pl.when(s == 0)
        def _():
            ds_ref[...] = jnp.zeros_like(ds_ref)

        for d in range(2):
            x_ref, b_ref, c_ref, dtr_ref, a_ref, se_ref, dy_ref = refs[n_in * d:n_in * (d + 1)]
            dx_ref, db_ref, dc_ref, ddtr_ref, da_ref = refs[2 * n_in + n_out * d:2 * n_in + n_out * (d + 1)]
            per_head = [dtr_ref[r] for r in range(R)] + [a_ref[r] for r in range(R)]
            f = functools.partial(_ssd_group, reverse=d == 1, P=P)
            _, vjp = jax.vjp(f, x_ref[...], b_ref[...], c_ref[...], se_ref[...], *per_head)
            is_latent = chunk(d, s) >= ncc
            dy_v = jnp.where(is_latent, dy_ref[...], 0.0)
            grads = vjp((dy_v, ds_ref[d]))
            dx_ref[...] = grads[0]
            db_ref[...] = grads[1]
            dc_ref[...] = grads[2]
            ds_ref[d] = grads[3]
            for r in range(R):
                ddtr_ref[r] = grads[4 + r]
                da_ref[r] = jnp.broadcast_to(grads[4 + R + r], (SUBLANE, LANE))

    in_specs, out_specs, out_shape, operands = [], [], [], []
    for d in range(2):
        in_specs += _ssd_specs(chunk, d, R, Q, N, RP, b_off // N, c_off // N) + [
            pl.BlockSpec((None, None, RP, N), lambda g, s: (g, step(s), 0, 0)),
            pl.BlockSpec((Q, RP), functools.partial(lambda g, s, d: (lat_chunk(d, s), g), d=d)),
        ]
        operands += [xbc, xbc, xbc, dtr[d], a[d], s_enter[d], dy]
    for d in range(2):
        at_chunk = functools.partial(lambda g, s, d: (chunk(d, s), g), d=d)
        out_specs += [
            pl.BlockSpec((Q, RP), at_chunk), pl.BlockSpec((Q, N), at_chunk), pl.BlockSpec((Q, N), at_chunk),
            pl.BlockSpec((R, 1, Q), functools.partial(lambda g, s, d: (g, 0, chunk(d, s)), d=d)),
            pl.BlockSpec((R, SUBLANE, LANE), lambda g, s: (g * NC + s, 0, 0)),
        ]
        out_shape += [
            jax.ShapeDtypeStruct((T, H * P), F32), jax.ShapeDtypeStruct((T, G * N), F32),
            jax.ShapeDtypeStruct((T, G * N), F32), jax.ShapeDtypeStruct((H, 1, T), F32),
            jax.ShapeDtypeStruct((G * NC * R, SUBLANE, LANE), F32),
        ]
    res = _pcall(
        body, name="ssd_bwd", grid=(G, NC), in_specs=in_specs, out_specs=out_specs, out_shape=out_shape,
        scratch_shapes=[pltpu.VMEM((2, RP, N), F32)], compiler_params=_cparams(2),
    )(*operands)
    return res[:n_out], res[n_out:]


def _allgather8(name, v):
    R, C = v.shape

    def body(x_ref, out_ref, send_sems, recv_sems, local_sem):
        x, y, c = lax.axis_index("x"), lax.axis_index("y"), lax.axis_index("c")
        me, sibling = (x, y, c), (x, y, 1 - c)
        chips = [(1 - x, y), (x, 1 - y), (1 - x, 1 - y)]

        def slot(px, py, pc):
            return out_ref.at[4 * px + 2 * py + pc]

        def copy(k, block, to, src=None):
            return pltpu.make_async_remote_copy(
                src_ref=slot(*block) if src is None else src, dst_ref=slot(*block),
                send_sem=send_sems.at[k], recv_sem=recv_sems.at[k], device_id=to, device_id_type=MESH)

        mine = pltpu.make_async_copy(x_ref, slot(*me), local_sem)
        mine.start()
        first = [copy(0, me, sibling, src=x_ref)]
        first += [copy(1 + j, me, (*chip, c), src=x_ref) for j, chip in enumerate(chips)]
        for cp in first:
            cp.start()
        passed = [copy(4 + j, (*chip, c), sibling) for j, chip in enumerate(chips)]
        for j, chip in enumerate(chips):
            copy(1 + j, (*chip, c), me).wait_recv()
            passed[j].start()
        copy(0, sibling, me).wait_recv()
        for j, chip in enumerate(chips):
            copy(4 + j, (*chip, 1 - c), me).wait_recv()
        for cp in first + passed:
            cp.wait_send()
        mine.wait()

    return _pcall(
        body, name=name, out_shape=jax.ShapeDtypeStruct((N_DEV, R, C), v.dtype),
        in_specs=[pl.BlockSpec(memory_space=pltpu.VMEM)], out_specs=pl.BlockSpec(memory_space=pltpu.VMEM),
        scratch_shapes=[pltpu.SemaphoreType.DMA((7,)), pltpu.SemaphoreType.DMA((7,)), pltpu.SemaphoreType.DMA],
        compiler_params=pltpu.CompilerParams(vmem_limit_bytes=VMEM_LIMIT_BYTES),
    )(v)


def _slot(ref, k, axis, size):
    if axis is None:
        return ref.at[k]
    align = LANE if size % LANE == 0 else 2 * SUBLANE
    assert size % align == 0
    return ref.at[(slice(None),) * axis + (pl.ds(pl.multiple_of(k * size, align), size),)]


def _exchange4_start(name, srcs, bcast, dep, axes=None):
    n = len(srcs)
    axes = list(axes) if axes is not None else [None] * n
    sizes = [None if ax is None else s.shape[ax] for s, ax in zip(srcs, axes)]

    def land_shape(s, ax):
        if not bcast:
            return s.shape
        if ax is None:
            return (N_CHIPS,) + s.shape
        return s.shape[:ax] + (N_CHIPS * s.shape[ax],) + s.shape[ax + 1:]

    lands = [lax.empty(land_shape(s, ax), s.dtype) for s, ax in zip(srcs, axes)]

    def body(*refs):
        src, land = refs[:n], refs[n:2 * n]
        send_sems, recv_sems = refs[2 * n + 1], refs[2 * n + 2]
        token = refs[-1]
        x, y, c = lax.axis_index("x"), lax.axis_index("y"), lax.axis_index("c")
        me = 2 * x + y
        for a in range(n):
            for j, (px, py) in enumerate([(1 - x, y), (x, 1 - y), (1 - x, 1 - y)]):
                pltpu.make_async_remote_copy(
                    src_ref=src[a] if bcast else src[a].at[2 * px + py], dst_ref=_slot(land[a], me, axes[a], sizes[a]),
                    send_sem=send_sems.at[3 * a + j], recv_sem=recv_sems.at[3 * a + j], device_id=(px, py, c),
                    device_id_type=MESH).start()
        token[...] = jnp.zeros_like(token)

    hbm = pl.BlockSpec(memory_space=pltpu.HBM)
    sem = pl.BlockSpec(memory_space=pltpu.SEMAPHORE)
    outs = _pcall(
        body, name=name,
        out_shape=(pltpu.SemaphoreType.DMA((3 * n,)), pltpu.SemaphoreType.DMA((3 * n,)),
                   *[pltpu.HBM(s.shape, s.dtype) for s in srcs], *[pltpu.HBM(l.shape, l.dtype) for l in lands],
                   jax.ShapeDtypeStruct((SUBLANE, LANE), F32)),
        in_specs=[hbm] * (2 * n) + [pl.BlockSpec(memory_space=pl.ANY)],
        out_specs=(sem, sem, *[hbm] * (2 * n), pl.BlockSpec(memory_space=pltpu.VMEM)),
        input_output_aliases={k: 2 + k for k in range(2 * n)},
        compiler_params=pltpu.CompilerParams(has_side_effects=pltpu.SideEffectType.DATAFLOW_SIDE_EFFECTING),
    )(*[pltpu.with_memory_space_constraint(s, pltpu.HBM) for s in srcs],
      *[pltpu.with_memory_space_constraint(l, pltpu.HBM) for l in lands], dep)
    return (n, bcast, axes, sizes, outs[0], outs[1], outs[2:2 + n], outs[2 + n:2 + 2 * n]), outs[-1]


def _exchange4_wait(name, handle, after):
    n, bcast, axes, sizes, send_sems, recv_sems, src_thru, land_thru = handle

    def body(*refs):
        src, land = refs[:n], refs[n:2 * n]
        send_sems, recv_sems = refs[2 * n], refs[2 * n + 1]
        x, y, c = lax.axis_index("x"), lax.axis_index("y"), lax.axis_index("c")
        for a in range(n):
            for j, (px, py) in enumerate([(1 - x, y), (x, 1 - y), (1 - x, 1 - y)]):
                pk = 2 * px + py
                copy = pltpu.make_async_remote_copy(
                    src_ref=src[a] if bcast else src[a].at[pk], dst_ref=_slot(land[a], pk, axes[a], sizes[a]),
                    send_sem=send_sems.at[3 * a + j], recv_sem=recv_sems.at[3 * a + j], device_id=(px, py, c),
                    device_id_type=MESH)
                copy.wait_send()
                copy.wait_recv()

    hbm = pl.BlockSpec(memory_space=pltpu.HBM)
    sem = pl.BlockSpec(memory_space=pltpu.SEMAPHORE)
    outs = _pcall(
        body, name=name,
        out_shape=tuple(pltpu.HBM(t.shape, t.dtype) for t in (*src_thru, *land_thru)),
        in_specs=[hbm] * (2 * n) + [sem, sem, pl.BlockSpec(memory_space=pl.ANY)], out_specs=tuple([hbm] * (2 * n)),
        input_output_aliases={k: k for k in range(2 * n)},
        compiler_params=pltpu.CompilerParams(has_side_effects=pltpu.SideEffectType.DATAFLOW_SIDE_EFFECTING),
    )(*src_thru, *land_thru, send_sems, recv_sems, after)
    return list(outs[:n]), list(outs[n:])


def _tie(name, v, token):
    def body(v_ref, token_ref, o_ref):
        del v_ref, token_ref, o_ref

    any_spec = pl.BlockSpec(memory_space=pl.ANY)
    return _pcall(body, name=name, out_shape=jax.ShapeDtypeStruct(v.shape, v.dtype), in_specs=[any_spec, any_spec],
                  out_specs=any_spec, input_output_aliases={0: 0})(v, token)


def _fill_own(landed, own, me, bcast):
    blk = own if bcast else lax.dynamic_index_in_dim(own, me, 0, keepdims=False)
    return lax.dynamic_update_index_in_dim(landed, blk, me, 0)


def _swap_sibling(name, srcs):
    n = len(srcs)

    def body(*refs):
        src, out = refs[:n], refs[n:2 * n]
        send_sems, recv_sems = refs[2 * n:]
        x, y, c = lax.axis_index("x"), lax.axis_index("y"), lax.axis_index("c")
        copies = []
        for a in range(n):
            rc = pltpu.make_async_remote_copy(
                src_ref=src[a], dst_ref=out[a], send_sem=send_sems.at[a], recv_sem=recv_sems.at[a],
                device_id=(x, y, 1 - c), device_id_type=MESH)
            rc.start()
            copies.append(rc)
        for cp in copies:
            cp.wait()

    any_spec = pl.BlockSpec(memory_space=pl.ANY)
    return _pcall(
        body, name=name, out_shape=[jax.ShapeDtypeStruct(s.shape, s.dtype) for s in srcs],
        in_specs=[any_spec] * n, out_specs=[any_spec] * n,
        scratch_shapes=[pltpu.SemaphoreType.DMA((n,)), pltpu.SemaphoreType.DMA((n,))],
    )(*srcs)


def _mod_fwd(c16, mod_w, mod_b_shard):
    nl, D, S = mod_w.shape

    def body(c_ref, w_ref, b_ref, o_ref):
        s = _silu(c_ref[...]).astype(BF16)
        o_ref[...] = jnp.dot(s, w_ref[...].astype(BF16), preferred_element_type=F32) + b_ref[...]

    return _pcall(
        body, name="mod_fwd", grid=(nl,),
        in_specs=[pl.BlockSpec((16, D), lambda l: (0, 0)), pl.BlockSpec((None, D, S), lambda l: (l, 0, 0)),
                  pl.BlockSpec((None, 1, S), lambda l: (l, 0, 0))],
        out_specs=pl.BlockSpec((None, 16, S), lambda l: (l, 0, 0)),
        out_shape=jax.ShapeDtypeStruct((nl, 16, S), F32), compiler_params=_cparams(1),
    )(c16, mod_w, mod_b_shard)


def _mod_w_update(s16t, dm16, w, m, v):
    nl, D, S = w.shape
    tm = _row_tile(D, 256)

    def body(s_ref, dm_ref, w_ref, m_ref, v_ref, g_ref, dl_ref, nm_ref, nv_ref):
        g = jnp.dot(s_ref[...], dm_ref[...], preferred_element_type=F32, precision=HIGHEST)
        g, dl, nm, nv = _f_adamw(w_ref[...], m_ref[...], v_ref[...], g, jnp.zeros_like(g))
        g_ref[...] = g
        dl_ref[...] = dl
        nm_ref[...] = nm
        nv_ref[...] = nv

    big = pl.BlockSpec((None, tm, S), lambda l, i: (l, i, 0))
    return _pcall(
        body, name="mod_w_update", grid=(nl, D // tm),
        in_specs=[pl.BlockSpec((tm, 16), lambda l, i: (i, 0)), pl.BlockSpec((None, 16, S), lambda l, i: (l, 0, 0)),
                  big, big, big],
        out_specs=[big] * 4, out_shape=[jax.ShapeDtypeStruct(w.shape, F32)] * 4, compiler_params=_cparams(2),
    )(s16t, dm16, w, m, v)


def _size(shape):
    n = 1
    for d in shape:
        n *= d
    return n


def _pack(arrs):
    pieces = []
    for a in arrs:
        flat = a.reshape(-1).astype(F32)
        pieces.append(jnp.pad(flat, (0, _round_up(flat.shape[0], LANE) - flat.shape[0])).reshape(-1, LANE))
    buf = jnp.concatenate(pieces, axis=0)
    return jnp.pad(buf, ((0, _round_up(buf.shape[0], SUBLANE) - buf.shape[0]), (0, 0)))


def _unpack(buf, shapes):
    lead = buf.shape[:-2]
    out, row = [], 0
    for s in shapes:
        n = _size(s)
        rows = _cdiv(n, LANE)
        piece = buf[..., row:row + rows, :].reshape(lead + (rows * LANE,))
        out.append(piece[..., :n].reshape(lead + tuple(s)))
        row += rows
    return out


def _adamw_many(name, ws, ms, vs, gs):
    n = len(ws)

    def body(*refs):
        for k in range(n):
            res = _f_adamw(refs[k][...], refs[n + k][...], refs[2 * n + k][...], refs[3 * n + k][...], 0.0)
            for j in range(4):
                refs[(4 + j) * n + k][...] = res[j]

    vmem = pl.BlockSpec(memory_space=pltpu.VMEM)
    res = _pcall(body, name=name, out_shape=[jax.ShapeDtypeStruct(w.shape, F32) for _ in range(4) for w in ws],
                 in_specs=[vmem] * (4 * n), out_specs=[vmem] * (4 * n))(*ws, *ms, *vs, *gs)
    return [tuple(res[j * n + k] for j in range(4)) for k in range(n)]


SHARD_AXIS = {
    "mod_w": 2, "ssd_w_in": 2, "ssd_conv_w": 2, "ssd_w_out": 1, "conf_w_pw1": 2, "conf_b_pw1": 1, "conf_w_dw": 2,
    "conf_b_dw": 1, "conf_ln_w": 1, "conf_ln_b": 1, "conf_w_pw2": 1, "conf_b_pw2": 1, "ffn_w_up": 2,
    "ffn_conv_w": 3, "ffn_w_down": 1,
}
BIG = ("ssd_w_in", "ssd_w_out", "conf_w_pw1", "conf_w_pw2", "ffn_w_up", "ffn_w_down")
WEIGHTS = ("c_ctx", "mod_w", "mod_b", "norm1_w", "norm2_w", "ssd_w_in", "ssd_conv_w", "ssd_conv_b", "ssd_dt_bias",
           "ssd_a_log", "ssd_d", "ssd_norm_w", "ssd_w_out", "conf_w_pw1", "conf_b_pw1", "conf_w_dw", "conf_b_dw",
           "conf_ln_w", "conf_ln_b", "conf_w_pw2", "conf_b_pw2", "ffn_w_up", "ffn_conv_w", "ffn_conv_b",
           "ffn_w_down", "final_norm_w")
SMALL = tuple(n for n in WEIGHTS if n not in BIG and n != "mod_w")
SMALL_SHARDED = tuple(n for n in SMALL if n in SHARD_AXIS)


def _unshard(stacked, axis):
    return jnp.concatenate([stacked[k] for k in range(N_CHIPS)], axis=axis)


def _to_blocks(full, axis):
    return jnp.stack(jnp.split(full, N_CHIPS, axis=axis))


def _par(v):
    v = v.reshape(-1, v.shape[-1])
    return v[:, None, :]


def kernel(x, c, ctx, c_ctx, mod_w, mod_b, norm1_w, norm2_w, ssd_w_in, ssd_conv_w, ssd_conv_b, ssd_dt_bias, ssd_a_log, ssd_d, ssd_norm_w, ssd_w_out, conf_w_pw1, conf_b_pw1, conf_w_dw, conf_b_dw, conf_ln_w, conf_ln_b, conf_w_pw2, conf_b_pw2, ffn_w_up, ffn_conv_w, ffn_conv_b, ffn_w_down, final_norm_w, loss_target, m_c_ctx, m_mod_w, m_mod_b, m_norm1_w, m_norm2_w, m_ssd_w_in, m_ssd_conv_w, m_ssd_conv_b, m_ssd_dt_bias, m_ssd_a_log, m_ssd_d, m_ssd_norm_w, m_ssd_w_out, m_conf_w_pw1, m_conf_b_pw1, m_conf_w_dw, m_conf_b_dw, m_conf_ln_w, m_conf_ln_b, m_conf_w_pw2, m_conf_b_pw2, m_ffn_w_up, m_ffn_conv_w, m_ffn_conv_b, m_ffn_w_down, m_final_norm_w, v_c_ctx, v_mod_w, v_mod_b, v_norm1_w, v_norm2_w, v_ssd_w_in, v_ssd_conv_w, v_ssd_conv_b, v_ssd_dt_bias, v_ssd_a_log, v_ssd_d, v_ssd_norm_w, v_ssd_w_out, v_conf_w_pw1, v_conf_b_pw1, v_conf_w_dw, v_conf_b_dw, v_conf_ln_w, v_conf_ln_b, v_conf_w_pw2, v_conf_b_pw2, v_ffn_w_up, v_ffn_conv_w, v_ffn_conv_b, v_ffn_w_down, v_final_norm_w):
    given = dict(locals())
    W = {n: given[n] for n in WEIGHTS}
    Mo = {n: given["m_" + n] for n in WEIGHTS}
    Vo = {n: given["v_" + n] for n in WEIGHTS}

    ax, ay, ac = lax.axis_index("x"), lax.axis_index("y"), lax.axis_index("c")
    chip = 2 * ax + ay
    dev = 4 * ax + 2 * ay + ac

    D = x.shape[-1]
    L, Lc = x.shape[1], ctx.shape[1]
    T0 = L + Lc
    H = ssd_a_log.shape[-1]
    DI = ssd_norm_w.shape[-1]
    P = DI // H
    CD = ssd_conv_b.shape[-1]
    N = SSD_STATE
    G = (CD - DI) // (2 * N)
    FH = ffn_conv_b.shape[-1]
    KS = ssd_conv_w.shape[1]
    KC = conf_w_dw.shape[1]
    ncc = Lc // SSD_CHUNK

    shard_b = {n: W[n].astype(BF16) for n in BIG}

    small_shard_shapes = [W[n].shape for n in SMALL_SHARDED]
    f1 = _allgather8("gather_small", _pack([c] + [W[n] for n in SMALL_SHARDED]))
    parts = _unpack(f1, [c.shape] + small_shard_shapes)
    Wf = dict(W)
    for n, p in zip(SMALL_SHARDED, parts[1:]):
        Wf[n] = _unshard(p[::2], SHARD_AXIS[n])
    c16 = jnp.concatenate([parts[0].reshape(N_DEV, D), c_ctx[None, :], jnp.zeros((16 - N_DEV - 1, D), F32)], axis=0)

    S_mod = mod_w.shape[-1]
    mod_b_shard = lax.dynamic_slice_in_dim(mod_b, chip * S_mod, S_mod, axis=1)[:, None, :]
    mod_part = _mod_fwd(c16, mod_w, mod_b_shard)
    f2 = _allgather8("gather_mod", mod_part.reshape(2 * 16, S_mod))
    mods = jnp.concatenate([f2[2 * k].reshape(2, 16, S_mod) for k in range(N_CHIPS)], axis=-1)
    my = lax.dynamic_slice_in_dim(mods, dev, 1, axis=1)[:, 0]
    sh1, sc1, g1, sh2, sc2, g2 = [[my[l, k * D:(k + 1) * D] for l in range(2)] for k in range(6)]
    csh1, csc1 = mods[0, N_DEV, 0:D], mods[0, N_DEV, D:2 * D]

    gather_a, token = _exchange4_start("gather_w_in_start", [shard_b["ssd_w_in"]], True, mods)
    csc1 = _tie("tie_gather_w_in", csc1, token)

    def full_weight(n, own, landed):
        if landed.ndim == own.ndim:
            ax = SHARD_AXIS[n]
            return lax.dynamic_update_slice_in_dim(landed, own, chip * own.shape[ax], ax)
        return _unshard(_fill_own(landed, own, chip, True), SHARD_AXIS[n])

    xl = x[0]
    rows0 = (ctx[0], xl)
    n1w0, n2w0, n1w1, n2w1 = _par(norm1_w[0]), _par(norm2_w[0]), _par(norm1_w[1]), _par(norm2_w[1])
    sc_seg = jnp.stack([csc1, sc1[0]])[:, None, :]
    sh_seg = jnp.stack([csh1, sh1[0]])[:, None, :]

    a0 = _rw_fwd("l0_modnorm1", _f_modnorm, [], [n1w0, sc_seg, sh_seg], [D], T=T0, seg_rows=(Lc,), head=rows0,
                 out_dtypes=[BF16])
    (own_in,), (landed_in,) = _exchange4_wait("gather_w_in_wait", gather_a, a0)
    w_in = full_weight("ssd_w_in", own_in, landed_in)[0]
    rest = [n for n in BIG if n != "ssd_w_in"]
    gather_b, token = _exchange4_start("gather_rest_start", [shard_b[n] for n in rest], True, landed_in,
                                       axes=[SHARD_AXIS[n] for n in rest])
    a0 = _tie("tie_gather_rest", a0, token)
    proj = _mm(a0, w_in, name="l0_w_in")
    seg_taps = [(k - KS // 2, ("seg", Lc)) for k in range(KS)]
    xbc_pre, xbc = _conv_fwd("l0_conv", proj, DI, CD, Wf["ssd_conv_w"][0], ssd_conv_b, seg_taps, act=True)
    dt_raw = proj[:, DI + CD:]
    dt_bias = _par(ssd_dt_bias.reshape(1, 2 * H))
    dt = _rw_fwd("l0_softplus", _f_softplus, [dt_raw], [dt_bias], [2 * H])
    dt_t = dt.T
    dtr = (dt_t[:H, None, :], dt_t[H:, None, :])
    a_all = -jnp.exp(ssd_a_log.reshape(2, H, 1, 1))
    a_neg = (a_all[0], a_all[1])
    (y_f, y_b), s_enter = _ssd_fwd(xbc, DI, DI + G * N, dtr, a_neg, P, ncc)
    gate_rows = [y_f, y_b, (xbc, 0, DI, Lc), (proj, 0, DI, Lc)]
    d_rep = _par(jnp.repeat(ssd_d[0], P))
    ssd_nw = _par(ssd_norm_w[0])
    yn = _rw_fwd("l0_ssd_gate", _f_ssd_gate, gate_rows, [d_rep, ssd_nw], [DI], T=L, out_dtypes=[BF16])
    Wb = {n: full_weight(n, own, g) for n, own, g in zip(rest, *_exchange4_wait("gather_rest_wait", gather_b, yn))}
    w_out, w_pw1, w_pw2 = Wb["ssd_w_out"][0], Wb["conf_w_pw1"][0], Wb["conf_w_pw2"][0]
    w_up, w_dn = Wb["ffn_w_up"], Wb["ffn_w_down"]
    mix0 = _mm(yn, w_out, name="l0_w_out")
    g1_0, g2_0, g1_1, g2_1 = _par(g1[0]), _par(g2[0]), _par(g1[1]), _par(g2[1])
    h1 = _rw_fwd("l0_res1", _f_gate_res, [xl, mix0], [g1_0], [D])

    grid_taps = [((i - 1) * GRID_W + (j - 1), (None if j == 1 else ("col", j - 1))) for i in range(3) for j in range(3)]

    def ffn_fwd(l, h, tag):
        a = _rw_fwd(tag + "_modnorm2", _f_modnorm, [h], [_par(norm2_w[l]), _par(sc2[l]), _par(sh2[l])], [D],
                    out_dtypes=[BF16])
        hh = _mm(a, w_up[l], name=tag + "_w_up")
        gc = _conv_fwd(tag + "_ffn_conv", hh, FH, FH, Wf["ffn_conv_w"][l].reshape(9, FH), ffn_conv_b[l][None, :],
                       grid_taps)
        act = _rw_fwd(tag + "_act", _f_ffn_act, [(hh, 0, FH), gc], [], [FH], col_tile=_tile(FH, 1536),
                      out_dtypes=[BF16])
        dn = _mm(act, w_dn[l], name=tag + "_w_down")
        return a, hh, gc, act, dn

    a1, hh0, gc0, act0, dn0 = ffn_fwd(0, h1, "l0")
    h2 = _rw_fwd("l0_res2", _f_gate_res, [h1, dn0], [g2_0], [D])

    a2 = _rw_fwd("l1_modnorm1", _f_modnorm, [h2], [n1w1, _par(sc1[1]), _par(sh1[1])], [D], out_dtypes=[BF16])
    pw = _mm(a2, w_pw1, name="l1_pw1")
    b_pw1 = Wf["conf_b_pw1"][0]
    glu = _rw_fwd("l1_glu", _f_glu, [(pw, 0, D), (pw, D, D)], [_par(b_pw1[:D]), _par(b_pw1[D:])], [D])
    conf_taps = [(k - KC // 2, None) for k in range(KC)]
    cv = _conv_fwd("l1_conv", glu, 0, D, Wf["conf_w_dw"][0], Wf["conf_b_dw"], conf_taps)
    ln_w, ln_b = _par(Wf["conf_ln_w"][0]), _par(Wf["conf_ln_b"][0])
    ls = _rw_fwd("l1_ln_silu", _f_ln_silu, [cv], [ln_w, ln_b], [D], out_dtypes=[BF16])
    p2 = _mm(ls, w_pw2, name="l1_pw2")
    b_pw2 = _par(Wf["conf_b_pw2"][0])
    h3 = _rw_fwd("l1_res1", _f_gate_res_bias, [h2, p2], [g1_1, b_pw2], [D])
    a3, hh1, gc1, act1, dn1 = ffn_fwd(1, h3, "l1")
    h4 = _rw_fwd("l1_res2", _f_gate_res, [h3, dn1], [g2_1], [D])

    fnw = final_norm_w[None, :]
    tgt = loss_target[0]
    loss_local = _loss_fwd(h4, tgt, fnw)[0, 0]
    loss = lax.psum(loss_local, ("x", "y", "c"))

    G_full = {}
    reduces = {}

    def start_reduce(tag, items, dep):
        def blocks_of(g, ax):
            if g.ndim == 3:
                return g
            return g.reshape(N_CHIPS, g.shape[0] // N_CHIPS, g.shape[1]) if ax == 0 else _to_blocks(g, ax)

        blocks = [blocks_of(g, ax).astype(BF16) for _, g, ax in items]
        handle, tok = _exchange4_start("reduce_" + tag + "_start", blocks, False, dep)
        reduces[tag] = ([n for n, _, _ in items], handle)
        return tok
    ones = jnp.ones((L, 1), F32)
    (dh4,), (dfnw,) = _rw_bwd("loss_bwd", _f_loss_rows, [h4, tgt], [_par(final_norm_w)], [ones],
                              row_grad=[True, False], par_grad=[True])
    G_full["final_norm_w"] = dfnw.reshape(D)

    def ffn_bwd(l, h, saved, g2_l, dh_out, tag):
        a, hh, gc, act, dn = saved
        (ddn,), (dg2,) = _rw_bwd(tag + "_res2_bwd", _f_gate, [dn], [g2_l], [dh_out],
                                 row_grad=[True], par_grad=[True], row_dtypes=[BF16])
        dact = _mm(ddn, w_dn[l], tb=True, name=tag + "_w_down_dx")
        dwdn = _mm(act, ddn, ta=True, name=tag + "_w_down_dw", out_dtype=BF16)
        (dval, dgc), _ = _rw_bwd(tag + "_act_bwd", _f_ffn_act, [(hh, 0, FH), gc], [], [dact],
                                 row_grad=[True, True], par_grad=[], col_tile=_tile(FH, 1536), row_dtypes=[BF16, F32])
        dgin, dcw, dcb = _conv_bwd(tag + "_ffn_conv_bwd", hh, FH, FH, Wf["ffn_conv_w"][l].reshape(9, FH), dgc,
                                   grid_taps, du_dtype=BF16)
        dhh = jnp.concatenate([dval, dgin], axis=1)
        da = _mm(dhh, w_up[l], tb=True, name=tag + "_w_up_dx")
        dwup = _mm(a, dhh, ta=True, name=tag + "_w_up_dw", out_dtype=BF16, col_blocks=N_CHIPS)
        (dh,), (dn2w, dsc2, dsh2) = _rw_bwd(
            tag + "_modnorm2_bwd", _f_modnorm, [h], [_par(norm2_w[l]), _par(sc2[l]), _par(sh2[l])], [da],
            row_grad=[True], par_grad=[True, True, True], add=dh_out)
        return dh, dict(w_down=dwdn, w_up=dwup, conv_w=dcw.reshape(3, 3, FH), conv_b=dcb.reshape(FH),
                        n2w=dn2w.reshape(D), sc2=dsc2.reshape(D), sh2=dsh2.reshape(D), g2=dg2.reshape(D))

    dh3, gf1 = ffn_bwd(1, h3, (a3, hh1, gc1, act1, dn1), g2_1, dh4, "l1")
    (dp2,), (dg1_1, db_pw2) = _rw_bwd("l1_res1_bwd", _f_gate_bias, [p2], [g1_1, b_pw2], [dh3],
                                      row_grad=[True], par_grad=[True, True], row_dtypes=[BF16])
    dls = _mm(dp2, w_pw2, tb=True, name="l1_pw2_dx")
    dw_pw2 = _mm(ls, dp2, ta=True, name="l1_pw2_dw", out_dtype=BF16)
    (dcv,), (dln_w, dln_b) = _rw_bwd("l1_ln_silu_bwd", _f_ln_silu, [cv], [ln_w, ln_b], [dls],
                                     row_grad=[True], par_grad=[True, True])
    dglu, dw_dw, db_dw = _conv_bwd("l1_conv_bwd", glu, 0, D, Wf["conf_w_dw"][0], dcv, conf_taps)
    (dpa, dpg), (dba, dbg) = _rw_bwd("l1_glu_bwd", _f_glu, [(pw, 0, D), (pw, D, D)],
                                     [_par(b_pw1[:D]), _par(b_pw1[D:])], [dglu],
                                     row_grad=[True, True], par_grad=[True, True], row_dtypes=[BF16, BF16])
    dpw = jnp.concatenate([dpa, dpg], axis=1)
    da2 = _mm(dpw, w_pw1, tb=True, name="l1_pw1_dx")
    dw_pw1 = _mm(a2, dpw, ta=True, name="l1_pw1_dw", out_dtype=BF16, col_blocks=N_CHIPS)
    (dh2,), (dn1w1, dsc1_1, dsh1_1) = _rw_bwd(
        "l1_modnorm1_bwd", _f_modnorm, [h2], [n1w1, _par(sc1[1]), _par(sh1[1])], [da2],
        row_grad=[True], par_grad=[True, True, True], add=dh3)
    G_full["conf_b_pw2"] = db_pw2.reshape(1, D)
    G_full["conf_ln_w"], G_full["conf_ln_b"] = dln_w.reshape(1, D), dln_b.reshape(1, D)
    G_full["conf_w_dw"], G_full["conf_b_dw"] = dw_dw[None], db_dw.reshape(1, D)
    G_full["conf_b_pw1"] = jnp.concatenate([dba.reshape(1, D), dbg.reshape(1, D)], axis=1)

    token = start_reduce("l1", [("conf_w_pw2", dw_pw2, 0), ("conf_w_pw1", dw_pw1, 1), ("ffn_w_up1", gf1["w_up"], 1),
                                ("ffn_w_down1", gf1["w_down"], 0)], dw_pw2)
    dh2 = _tie("tie_reduce_l1", dh2, token)
    dh1, gf0 = ffn_bwd(0, h1, (a1, hh0, gc0, act0, dn0), g2_0, dh2, "l0")
    G_full["ffn_conv_w"] = jnp.stack([gf0["conv_w"], gf1["conv_w"]])
    G_full["ffn_conv_b"] = jnp.stack([gf0["conv_b"], gf1["conv_b"]])

    (dmix,), (dg1_0,) = _rw_bwd("l0_res1_bwd", _f_gate, [mix0], [g1_0], [dh1],
                                row_grad=[True], par_grad=[True], row_dtypes=[BF16])
    dyn = _mm(dmix, w_out, tb=True, name="l0_w_out_dx")
    dw_out = _mm(yn, dmix, ta=True, name="l0_w_out_dw", out_dtype=BF16)
    token = start_reduce("l0", [("ffn_w_up0", gf0["w_up"], 1), ("ffn_w_down0", gf0["w_down"], 0),
                                ("ssd_w_out", dw_out, 0)], dw_out)
    dyn = _tie("tie_reduce_l0", dyn, token)
    (dy_lat, dxs_gate, dz_lat), (dd_rep, dssd_nw) = _rw_bwd(
        "l0_ssd_gate_bwd", _f_ssd_gate, gate_rows, [d_rep, ssd_nw], [dyn],
        row_grad=[True, False, True, True], par_grad=[True, True], T=L, row_dtypes=[F32, F32, BF16])
    g_f, g_b = _ssd_bwd(xbc, DI, DI + G * N, dtr, a_neg, s_enter, dy_lat, P, ncc)
    silu_bwd = functools.partial(_rw_bwd, f=_silu, pars=[], row_grad=[True], par_grad=[], T=T0)
    (dxs_pre,), _ = silu_bwd("l0_silu_bwd_x", rows=[(xbc_pre, 0, DI)], cot_fn=lambda p, q, r: p + q + r,
                             cots=[g_f[0], g_b[0], (dxs_gate, 0, DI, -Lc)],
                             col_tile=_tile(DI, 1024))
    (db_pre,), _ = silu_bwd("l0_silu_bwd_b", rows=[(xbc_pre, DI, G * N)], cot_fn=lambda p, q: p + q,
                            cots=[g_f[1], g_b[1]], col_tile=_tile(G * N, 1024))
    (dc_pre,), _ = silu_bwd("l0_silu_bwd_c", rows=[(xbc_pre, DI + G * N, G * N)], cot_fn=lambda p, q: p + q,
                            cots=[g_f[2], g_b[2]], col_tile=_tile(G * N, 1024))
    conv_w0 = Wf["ssd_conv_w"][0]
    pieces = []
    for tag, off, width, g_pre in (("x", 0, DI, dxs_pre), ("b", DI, G * N, db_pre), ("c", DI + G * N, G * N, dc_pre)):
        pieces.append(_conv_bwd("l0_conv_bwd_" + tag, proj, DI + off, width, conv_w0[:, off:off + width], g_pre,
                                seg_taps, du_dtype=BF16))
    dconv_in = [p[0] for p in pieces]
    dcw0 = jnp.concatenate([p[1] for p in pieces], axis=1)
    dcb0 = jnp.concatenate([p[2] for p in pieces], axis=1)
    ddt = jnp.concatenate([g_f[3][:, 0, :].T, g_b[3][:, 0, :].T], axis=1)
    (ddt_raw,), (ddt_bias,) = _rw_bwd("l0_softplus_bwd", _f_softplus, [dt_raw], [dt_bias], [ddt],
                                      row_grad=[True], par_grad=[True], row_dtypes=[BF16])
    dproj = jnp.concatenate([jnp.pad(dz_lat, ((Lc, 0), (0, 0))), *dconv_in, ddt_raw], axis=1)
    da0 = _mm(dproj, w_in, tb=True, name="l0_w_in_dx")
    dw_in = _mm(a0, dproj, ta=True, name="l0_w_in_dw", out_dtype=BF16)
    token = start_reduce("in", [("ssd_w_in", dw_in, 1)], dw_in)
    da0 = _tie("tie_reduce_in", da0, token)
    (dhcat,), (dn1w0, dsc_seg, dsh_seg) = _rw_bwd(
        "l0_modnorm1_bwd", _f_modnorm, [], [n1w0, sc_seg, sh_seg], [da0], T=T0, head=rows0,
        row_grad=[True], par_grad=[True, True, True], seg_rows=(Lc,), add=(dh1, 0, D, -Lc))
    grad_x = dhcat[Lc:][None]

    da_heads = jnp.stack([g[4][:, 0, 0].reshape(G, T0 // SSD_CHUNK, H // G).sum(axis=1).reshape(H)
                          for g in (g_f, g_b)])[None]
    G_full["ssd_a_log"] = da_heads * (-jnp.exp(ssd_a_log))
    G_full["ssd_dt_bias"] = ddt_bias.reshape(1, 2, H)
    G_full["ssd_d"] = dd_rep.reshape(H, P).sum(axis=1)[None]
    G_full["ssd_norm_w"] = dssd_nw.reshape(1, DI)
    G_full["ssd_conv_w"], G_full["ssd_conv_b"] = dcw0[None], dcb0.reshape(1, CD)
    G_full["norm1_w"] = jnp.stack([dn1w0.reshape(D), dn1w1.reshape(D)])
    G_full["norm2_w"] = jnp.stack([gf0["n2w"], gf1["n2w"]])

    zD = jnp.zeros((D,), F32)
    dm_own = jnp.stack([
        jnp.concatenate([dsh_seg[1, 0], dsc_seg[1, 0], dg1_0.reshape(D), gf0["sh2"], gf0["sc2"], gf0["g2"]]),
        jnp.concatenate([dsh1_1.reshape(D), dsc1_1.reshape(D), dg1_1.reshape(D), gf1["sh2"], gf1["sc2"], gf1["g2"]]),
    ])
    dmc_own = jnp.concatenate([dsh_seg[0, 0], dsc_seg[0, 0], zD, zD, zD, zD])

    out = {}

    def finish_reduce(tags, after, swap_name):
        partial = {}
        for tag in tags:
            names, handle = reduces[tag]
            blocks, landed = _exchange4_wait("reduce_" + tag + "_wait", handle, after)
            for n, blk, own in zip(names, landed, blocks):
                r = _fill_own(blk, own, chip, False)
                partial[n] = _sum_leading("sum4_" + n, r.reshape(N_CHIPS, -1, r.shape[-1]),
                                          (0, 1, 2, 3)).reshape(r.shape[1:])
        for n in ("ffn_w_up", "ffn_w_down"):
            if n + "0" in partial:
                partial[n] = jnp.stack([partial.pop(n + "0"), partial.pop(n + "1")])
        names = [n for n in BIG if n in partial]
        mine = [partial[n].reshape(W[n].shape) for n in names]
        for n, own, sib in zip(names, mine, _swap_sibling(swap_name, mine)):
            out[n] = _adamw("adamw_" + n, W[n], Mo[n], Vo[n], own, sib)
        return names

    early = finish_reduce(["l1", "l0"], dhcat, "swap_grads_early")

    small_sum_names = [n for n in SMALL if n not in ("c_ctx", "mod_b")]
    sum_part = [G_full[n] for n in small_sum_names] + [dmc_own]
    packed = _tie("tie_small_grads", _pack(sum_part + [dm_own]), out[early[-1]][1])
    gat = _allgather8("gather_small_grads", packed)
    total = _sum_leading("sum_small_grads", gat, tuple(range(N_DEV)))
    summed = _unpack(total, [a.shape for a in sum_part])
    Gs = dict(zip(small_sum_names, summed[:-1]))
    dmc_tot = summed[-1]
    dm_all = _unpack(gat, [a.shape for a in sum_part] + [dm_own.shape])[-1].transpose(1, 0, 2)
    dm16 = jnp.concatenate([dm_all, jnp.stack([dmc_tot, jnp.zeros_like(dmc_tot)])[:, None, :],
                            jnp.zeros((2, 16 - N_DEV - 1, 6 * D), F32)], axis=1)
    Gs["mod_b"] = _sum_leading("sum_mod_b", dm16.transpose(1, 0, 2).reshape(16, 2 * 6 * D // LANE, LANE),
                               tuple(range(N_DEV + 1))).reshape(2, 6 * D)

    dm16_shard = lax.dynamic_slice_in_dim(dm16, chip * S_mod, S_mod, axis=2)
    ds16 = _mm(dm16_shard[0], mod_w[0], tb=True, precision=HIGHEST, name="c_ctx_dx")
    sig = jax.nn.sigmoid(c_ctx)
    dcc_part = ds16[N_DEV] * (sig * (1.0 + c_ctx * (1.0 - sig)))
    gat_cc = _allgather8("gather_c_ctx_grad", _pack([dcc_part]))
    Gs["c_ctx"] = _sum_leading("sum_c_ctx_grad", gat_cc, (0, 2, 4, 6)).reshape(-1)[:D]

    s16t = _silu(c16).T
    out["mod_w"] = _mod_w_update(s16t, dm16_shard, mod_w, m_mod_w, v_mod_w)
    finish_reduce(["in"], out["mod_w"][0], "swap_grads_late")

    def own(n, full):
        if n in SHARD_AXIS:
            size = W[n].shape[SHARD_AXIS[n]]
            return lax.dynamic_slice_in_dim(full, chip * size, size, axis=SHARD_AXIS[n])
        return full

    def two_d(a):
        return a.reshape(1, -1) if a.ndim == 1 else a

    g_small = [own(n, Gs[n].reshape(Wf[n].shape)) for n in SMALL]
    res = _adamw_many("adamw_small", [two_d(W[n]) for n in SMALL], [two_d(Mo[n]) for n in SMALL],
                      [two_d(Vo[n]) for n in SMALL], [two_d(g) for g in g_small])
    for n, r in zip(SMALL, res):
        out[n] = tuple(t.reshape(W[n].shape) for t in r)

    grads = [out[n][0] for n in WEIGHTS]
    deltas = [out[n][1] for n in WEIGHTS]
    new_m = [out[n][2] for n in WEIGHTS]
    new_v = [out[n][3] for n in WEIGHTS]
    return (loss, grad_x, *grads, *deltas, *new_m, *new_v)
```

```python
import functools

import jax
import jax.numpy as jnp
from jax import lax
from jax.experimental import pallas as pl
from jax.experimental.pallas import tpu as pltpu

F32 = jnp.float32
BF16 = jnp.bfloat16
MESH = pl.DeviceIdType.MESH
HIGHEST = lax.Precision.HIGHEST

VMEM_LIMIT_BYTES = 48 * 1024 * 1024
LANE = 128
SUBLANE = 8

SSD_STATE = 128
SSD_CHUNK = 128
GRID_W = 64
EPS = 1e-6
N_CHIPS = 4
N_DEV = 8

ADAM_LR = 0.001
ADAM_B1 = 0.9
ADAM_B2 = 0.999
ADAM_EPS = 1e-08
ADAM_WD = 0.01
ADAM_STEP = 10


def _pcall(body, **kw):
    return pl.pallas_call(body, **kw)


def _cparams(n_grid):
    return pltpu.CompilerParams(dimension_semantics=("arbitrary",) * n_grid, vmem_limit_bytes=VMEM_LIMIT_BYTES)


def _cdiv(a, b):
    return -(-a // b)


def _round_up(a, b):
    return _cdiv(a, b) * b


def _tile(n, cap):
    if n <= cap:
        return n
    best = None
    for t in range(LANE, cap + 1, LANE):
        if n % t == 0:
            best = t
    if best is None:
        npad = _round_up(n, LANE)
        for t in range(LANE, cap + 1, LANE):
            if npad % t == 0:
                best = t
    return best


def _row_tile(n, cap, also=()):
    best = None
    for step in (2 * SUBLANE, SUBLANE):
        for t in range(step, min(cap, n) + 1, step):
            if n % t == 0 and all(a % t == 0 for a in also):
                best = t
        if best is not None:
            break
    assert best is not None, (n, cap, also)
    return best


def _silu(v):
    return v * jax.nn.sigmoid(v)


def _mm(a, b, *, name, ta=False, tb=False, precision=None, cap=1024, out_dtype=F32, col_blocks=None):
    M, K = (a.shape[1], a.shape[0]) if ta else a.shape
    N = b.shape[0] if tb else b.shape[1]
    assert K == (b.shape[1] if tb else b.shape[0]), (a.shape, b.shape, ta, tb)
    tm, tk = _tile(M, cap), _tile(K, cap + cap // 2)
    tn = _tile(N if col_blocks is None else N // col_blocks, cap + cap // 2)
    nm, nn, nk = _cdiv(M, tm), _cdiv(N, tn), _cdiv(K, tk)
    k_tail = K % tk
    exact = precision is not None

    def body(a_ref, b_ref, o_ref, acc_ref):
        k = pl.program_id(2)

        @pl.when(k == 0)
        def _():
            acc_ref[...] = jnp.zeros_like(acc_ref)

        av = a_ref[...]
        bv = b_ref[...]
        if k_tail:
            lim = K - k * tk
            ka = lax.broadcasted_iota(jnp.int32, av.shape, 0 if ta else 1)
            kb = lax.broadcasted_iota(jnp.int32, bv.shape, 1 if tb else 0)
            av = jnp.where(ka < lim, av, jnp.zeros_like(av))
            bv = jnp.where(kb < lim, bv, jnp.zeros_like(bv))
        if exact:
            av = av.astype(F32)
            bv = bv.astype(F32)
        else:
            av = av.astype(BF16)
            bv = bv.astype(BF16)
        dn = (((0 if ta else 1,), (1 if tb else 0,)), ((), ()))
        acc_ref[...] += lax.dot_general(av, bv, dn, preferred_element_type=F32, precision=precision)

        @pl.when(k == nk - 1)
        def _():
            o_ref[...] = acc_ref[...].astype(o_ref.dtype)

    a_spec = pl.BlockSpec((tk, tm), lambda i, j, k: (k, i)) if ta else pl.BlockSpec((tm, tk), lambda i, j, k: (i, k))
    b_spec = pl.BlockSpec((tn, tk), lambda i, j, k: (j, k)) if tb else pl.BlockSpec((tk, tn), lambda i, j, k: (k, j))
    if col_blocks is None:
        out_spec = pl.BlockSpec((tm, tn), lambda i, j, k: (i, j))
        out_shape = jax.ShapeDtypeStruct((M, N), out_dtype)
    else:
        per = (N // col_blocks) // tn
        assert per * tn * col_blocks == N, (N, col_blocks, tn)
        out_spec = pl.BlockSpec((None, tm, tn), lambda i, j, k: (j // per, i, j % per))
        out_shape = jax.ShapeDtypeStruct((col_blocks, M, N // col_blocks), out_dtype)
    return _pcall(
        body, name=name, grid=(nm, nn, nk), in_specs=[a_spec, b_spec], out_specs=out_spec, out_shape=out_shape,
        scratch_shapes=[pltpu.VMEM((tm, tn), F32)], compiler_params=_cparams(3),
    )(a, b)


def _norm_rows(rows):
    out = []
    for r in rows:
        if not isinstance(r, tuple):
            r = (r,)
        arr, off, width, roff = (r + (0, None, 0)[len(r) - 1:])
        out.append((arr, off, width if width is not None else arr.shape[1], roff))
    return out


def _rw_plan(T, rows, pars, seg_rows, col_tile, tm_cap):
    widths = [r[2] for r in rows]
    wmax = max(widths + [p.shape[-1] for p in pars] + [1])
    if col_tile is not None:
        assert all(w == widths[0] for w in widths) and all(p.shape[-1] == widths[0] for p in pars)
        ncol = widths[0] // col_tile
        assert ncol * col_tile == widths[0]
        wmax = col_tile
    else:
        ncol = 1
    cap = tm_cap if tm_cap is not None else max(SUBLANE, min(256, (256 * 1024) // wmax))
    tm = _row_tile(T, cap, also=tuple(seg_rows) + tuple(abs(r[3]) for r in rows if r[3]))
    bounds = tuple(s // tm for s in seg_rows)
    return widths, ncol, tm, bounds


def _rw_specs(rows, pars, ncol, tm, bounds, col_tile):
    def seg(i):
        s = 0
        for b in bounds:
            s = s + (i >= b).astype(jnp.int32)
        return s

    specs = []
    for arr, off, w, roff in rows:
        bw = col_tile if col_tile is not None else w
        assert off % bw == 0 and roff % tm == 0, (off, bw, roff, tm)
        specs.append(pl.BlockSpec((tm, bw), functools.partial(
            lambda j, i, ob, rb, last: (jnp.clip(i + rb, 0, last), ob + j),
            ob=off // bw, rb=roff // tm, last=arr.shape[0] // tm - 1)))
    for p in pars:
        bw = col_tile if col_tile is not None else p.shape[-1]
        if p.shape[0] > 1:
            specs.append(pl.BlockSpec((None, 1, bw), lambda j, i: (seg(i), 0, j)))
        else:
            specs.append(pl.BlockSpec((None, 1, bw), lambda j, i: (0, 0, j)))
    return specs, seg


def _head_rows(head):
    top, bottom = head
    return [(top, 0, None, 0), (bottom, 0, None, -top.shape[0])]


def _rw_fwd(name, f, rows, pars, out_widths, *, T=None, seg_rows=(), col_tile=None, tm_cap=None, out_dtypes=None,
            head=None):
    rows = _norm_rows((_head_rows(head) if head else []) + list(rows))
    T = rows[0][0].shape[0] if T is None else T
    widths, ncol, tm, bounds = _rw_plan(T, rows, pars, seg_rows, col_tile, tm_cap)
    in_specs, _ = _rw_specs(rows, pars, ncol, tm, bounds, col_tile)
    nr, npar, nout = len(rows), len(pars), len(out_widths)

    def body(*refs):
        vals = [r[...] for r in refs[:nr + npar]]
        if head:
            vals = [jnp.where(pl.program_id(1) < head[0].shape[0] // tm, vals[0], vals[1])] + vals[2:]
        outs = f(*vals)
        if not isinstance(outs, (tuple, list)):
            outs = (outs,)
        for o_ref, o in zip(refs[nr + npar:], outs):
            o_ref[...] = o.astype(o_ref.dtype)

    out_specs = [pl.BlockSpec((tm, col_tile if col_tile is not None else w), lambda j, i: (i, j)) for w in out_widths]
    res = _pcall(
        body, name=name, grid=(ncol, T // tm), in_specs=in_specs, out_specs=out_specs,
        out_shape=[jax.ShapeDtypeStruct((T, w), dt) for w, dt in zip(out_widths, out_dtypes or [F32] * nout)],
        compiler_params=_cparams(2),
    )(*[r[0] for r in rows], *pars)
    return res if nout > 1 else res[0]


def _rw_bwd(name, f, rows, pars, cots, *, row_grad, par_grad, T=None, seg_rows=(), col_tile=None, tm_cap=None,
            add=None, cot_fn=None, row_dtypes=None, head=None):
    rows = _norm_rows((_head_rows(head) if head else []) + list(rows))
    cots = _norm_rows(cots)
    T = rows[0][0].shape[0] if T is None else T
    extra = _norm_rows([add]) if add is not None else []
    all_rows = rows + cots + extra
    widths, ncol, tm, bounds = _rw_plan(T, all_rows, pars, seg_rows, col_tile, tm_cap)
    in_specs, seg = _rw_specs(all_rows, pars, ncol, tm, bounds, col_tile)
    nr, nc, ne, npar = len(rows), len(cots), len(extra), len(pars)
    skip = 1 if head else 0
    widths = widths[skip:]
    nrf = nr - skip
    row_idx = [k for k in range(nrf) if row_grad[k]]
    par_idx = [k for k in range(npar) if par_grad[k]]

    def body(*refs):
        i = pl.program_id(1)

        def zero_before(vals, ops):
            return [jnp.where(i + c[3] // tm >= 0, v, jnp.zeros_like(v)) if c[3] < 0 else v for v, c in zip(vals, ops)]

        row_vals = [r[...] for r in refs[:nr]]
        if head:
            row_vals = [jnp.where(i < head[0].shape[0] // tm, row_vals[0], row_vals[1])] + row_vals[2:]
        cot_vals = zero_before([r[...] for r in refs[nr:nr + nc]], cots)
        add_vals = zero_before([r[...] for r in refs[nr + nc:nr + nc + ne]], extra)
        par_vals = [r[...] for r in refs[nr + nc + ne:nr + nc + ne + npar]]
        out_refs = refs[nr + nc + ne + npar:]
        outs, vjp = jax.vjp(f, *row_vals, *par_vals)
        if cot_fn is not None:
            cot_vals = cot_fn(*cot_vals)
            if not isinstance(cot_vals, (tuple, list)):
                cot_vals = (cot_vals,)
        if isinstance(outs, (tuple, list)):
            grads = vjp(tuple(c.astype(o.dtype) for c, o in zip(cot_vals, outs)))
        else:
            grads = vjp(cot_vals[0].astype(outs.dtype))
        first_seg = i == 0
        for b in bounds:
            first_seg = first_seg | (i == b)
        for n, k in enumerate(row_idx):
            g = grads[k]
            if n == 0 and add_vals:
                g = g + add_vals[0]
            out_refs[n][...] = g.astype(out_refs[n].dtype)
        for n, k in enumerate(par_idx):
            g = grads[nrf + k]
            o_ref = out_refs[len(row_idx) + n]
            first = first_seg if pars[k].shape[0] > 1 else (i == 0)

            @pl.when(first)
            def _(o_ref=o_ref, g=g):
                o_ref[...] = g

            @pl.when(jnp.logical_not(first))
            def _(o_ref=o_ref, g=g):
                o_ref[...] += g

    out_specs, out_shape = [], []
    for k in row_idx:
        w = widths[k]
        out_specs.append(pl.BlockSpec((tm, col_tile if col_tile is not None else w), lambda j, i: (i, j)))
        out_shape.append(jax.ShapeDtypeStruct((T, w), row_dtypes[len(out_shape)] if row_dtypes else F32))
    for k in par_idx:
        p = pars[k]
        bw = col_tile if col_tile is not None else p.shape[-1]
        if p.shape[0] > 1:
            out_specs.append(pl.BlockSpec((None, 1, bw), lambda j, i: (seg(i), 0, j)))
        else:
            out_specs.append(pl.BlockSpec((None, 1, bw), lambda j, i: (0, 0, j)))
        out_shape.append(jax.ShapeDtypeStruct(p.shape, F32))
    res = _pcall(
        body, name=name, grid=(ncol, T // tm), in_specs=in_specs, out_specs=out_specs, out_shape=out_shape,
        compiler_params=_cparams(2),
    )(*[r[0] for r in all_rows], *pars)
    return list(res[:len(row_idx)]), list(res[len(row_idx):])


def _f_modnorm(h, w, sc, sh):
    y = h * lax.rsqrt(jnp.mean(h * h, axis=-1, keepdims=True) + EPS)
    return (y * w) * (1.0 + sc) + sh


def _f_gate_res(h, y, g):
    return h + g * y


def _f_gate_res_bias(h, y, g, b):
    return h + g * (y + b)


def _f_gate(y, g):
    return g * y


def _f_gate_bias(y, g, b):
    return g * (y + b)


def _f_ffn_act(val, gate):
    return _silu(gate) * val


def _f_softplus(raw, bias):
    v = raw + bias
    return jnp.maximum(v, 0.0) + jnp.log(1.0 + jnp.exp(-jnp.abs(v)))


def _f_ssd_gate(yf, yb, xs, z, d_rep, nw):
    y = (yf + yb + d_rep * xs) * _silu(z)
    return (y * lax.rsqrt(jnp.mean(y * y, axis=-1, keepdims=True) + EPS)) * nw


def _f_glu(a, g, ba, bg):
    return (a + ba) * jax.nn.sigmoid(g + bg)


def _f_ln_silu(h, w, b):
    mu = jnp.mean(h, axis=-1, keepdims=True)
    d = h - mu
    y = d * lax.rsqrt(jnp.mean(d * d, axis=-1, keepdims=True) + EPS)
    return _silu(y * w + b)


def _f_loss_rows(h, t, w):
    y = (h * lax.rsqrt(jnp.mean(h * h, axis=-1, keepdims=True) + EPS)) * w
    e = y - t
    return 0.5 * jnp.mean(e * e, axis=-1, keepdims=True)


def _f_adamw(w, m, v, ga, gb):
    g = ga + gb
    m = ADAM_B1 * m + (1.0 - ADAM_B1) * g
    v = ADAM_B2 * v + (1.0 - ADAM_B2) * (g * g)
    m_hat = m / (1.0 - ADAM_B1 ** ADAM_STEP)
    v_hat = v / (1.0 - ADAM_B2 ** ADAM_STEP)
    delta = -ADAM_LR * (m_hat / (jnp.sqrt(v_hat) + ADAM_EPS) + ADAM_WD * w)
    return g, delta, m, v


def _adamw(name, w, m, v, ga, gb):
    shape = w.shape
    c = shape[-1]
    two_d = [t.reshape(-1, c) for t in (w, m, v, ga, gb)]
    rows = two_d[0].shape[0]
    pad = _round_up(rows, SUBLANE) - rows
    if pad:
        two_d = [jnp.pad(t, ((0, pad), (0, 0))) for t in two_d]
    outs = _rw_fwd(name, _f_adamw, two_d, [], [c] * 4)
    return tuple(o[:rows].reshape(shape) for o in outs)


def _sum_leading(name, x, idxs):
    _, R, C = x.shape
    tm = _row_tile(R, max(SUBLANE, min(512, (512 * 1024) // C)))

    def body(x_ref, o_ref):
        acc = x_ref[idxs[0]].astype(F32)
        for k in idxs[1:]:
            acc = acc + x_ref[k].astype(F32)
        o_ref[...] = acc

    return _pcall(
        body, name=name, grid=(R // tm,), in_specs=[pl.BlockSpec((x.shape[0], tm, C), lambda i: (0, i, 0))],
        out_specs=pl.BlockSpec((tm, C), lambda i: (i, 0)), out_shape=jax.ShapeDtypeStruct((R, C), F32),
        compiler_params=_cparams(1),
    )(x)


def _loss_fwd(h, t, w):
    T, D = h.shape
    tm = _row_tile(T, 256)

    def body(h_ref, t_ref, w_ref, o_ref):
        i = pl.program_id(0)
        part = jnp.sum(_f_loss_rows(h_ref[...], t_ref[...], w_ref[...]), axis=0, keepdims=True)
        part = jnp.broadcast_to(part, (1, LANE))

        @pl.when(i == 0)
        def _():
            o_ref[...] = part

        @pl.when(i > 0)
        def _():
            o_ref[...] += part

    return _pcall(
        body, name="loss_fwd", grid=(T // tm,),
        in_specs=[pl.BlockSpec((tm, D), lambda i: (i, 0)), pl.BlockSpec((tm, D), lambda i: (i, 0)),
                  pl.BlockSpec((1, D), lambda i: (0, 0))],
        out_specs=pl.BlockSpec((1, LANE), lambda i: (0, 0)), out_shape=jax.ShapeDtypeStruct((1, LANE), F32),
        compiler_params=_cparams(1),
    )(h, t, w)


CONV_ROWS = 256
CONV_ROWS_FEW_TAPS = 1024
CONV_ACC_ELEMS = 16384


def _col_mask(arg, t):
    col = jnp.bitwise_and(t, GRID_W - 1)
    return (col != 0) if arg < 0 else (col != GRID_W - 1)


def _conv_plan(T, C, taps):
    seg = [m[1] for _, m in taps if m is not None and m[0] == "seg"]
    cap = CONV_ROWS_FEW_TAPS if len(taps) <= 9 else CONV_ROWS
    rc = next(r for r in (1024, 768, 512, 256, LANE) if r <= cap and T % r == 0)
    ct = next((t for t in (512, 256, LANE) if C % t == 0), C)
    reach = max(abs(s) for s, _ in taps)
    hb = next(h for h in (8, 16, 32, 64, 128, 256) if h >= reach and rc % h == 0)
    sub = max(2 * SUBLANE, min(rc, CONV_ACC_ELEMS // ct))
    boundary = None
    if seg:
        inside = seg[0] % rc
        boundary = (seg[0], (inside - reach, inside + reach) if inside else None)
    taps = [(s, None if (m is None or m[0] == "seg") else m[1]) for s, m in taps]
    return rc, ct, hb, sub, T // rc, C // ct, boundary, taps


def _seg_ok(boundary, i, rc, r0, n, s):
    if boundary is None or boundary[1] is None or s == 0 or r0 + n <= boundary[1][0] or r0 >= boundary[1][1]:
        return None
    t = i * rc + r0 + lax.broadcasted_iota(jnp.int32, (n, 1), 0)
    return (t >= boundary[0]) == ((t + s) >= boundary[0])


def _halo_specs(rc, ct, hb, T, off_blocks):
    per = rc // hb
    last = T // hb - 1
    prev = pl.BlockSpec((hb, ct), lambda j, i: (jnp.maximum(i * per - 1, 0), off_blocks + j))
    cur = pl.BlockSpec((rc, ct), lambda j, i: (i, off_blocks + j))
    nxt = pl.BlockSpec((hb, ct), lambda j, i: (jnp.minimum((i + 1) * per, last), off_blocks + j))
    return [prev, cur, nxt]


def _fill_halo(pad_ref, p_ref, c_ref, n_ref, i, nrc, rc, hb, boundary):
    has_prev = i > 0
    has_next = i < nrc - 1
    if boundary is not None:
        has_prev = has_prev & (i * rc != boundary[0])
        has_next = has_next & ((i + 1) * rc != boundary[0])
    pad_ref[0:hb, :] = jnp.where(has_prev, p_ref[...], 0.0)
    pad_ref[hb:hb + rc, :] = c_ref[...]
    pad_ref[hb + rc:hb + rc + hb, :] = jnp.where(has_next, n_ref[...], 0.0)


def _shift_plan(keys):
    count = {}
    for s, m in keys:
        k = (s % SUBLANE, m)
        count[k] = count.get(k, 0) + 1
    slots = {}
    for k, n in sorted(count.items(), key=lambda kv: (kv[0][0], str(kv[0][1]))):
        if k != (0, None) and (n >= 2 or k[1] is not None):
            slots[k] = len(slots)
    return slots


def _build_shifted(copies_ref, slots, pad_ref, keys, i, rc, hb, sub):
    for (r, m), slot in slots.items():
        qs = [s - r for s, mk in keys if (s % SUBLANE, mk) == (r, m)]
        lo, hi = hb + min(qs), hb + rc + max(qs)
        for p in range(lo, hi, sub):
            n = min(sub, hi - p)
            v = pad_ref[p + r:p + r + n, :]
            if m is not None:
                t = i * rc - hb + p + r + lax.broadcasted_iota(jnp.int32, (n, 1), 0)
                v = jnp.where(_col_mask(m, t), v, 0.0)
            copies_ref[slot, p:p + n, :] = v


def _read(copies_ref, slots, pad_ref, s, m, row, n):
    k = (s % SUBLANE, m)
    if k in slots:
        q = s - k[0]
        return copies_ref[slots[k], row + q:row + q + n, :]
    return pad_ref[row + s:row + s + n, :]


def _conv_fwd(name, u, col_off, C, w, b, taps, act=False):
    T = u.shape[0]
    rc, ct, hb, sub, nrc, ncc, boundary, taps = _conv_plan(T, C, taps)
    assert col_off % ct == 0
    K = len(taps)
    keys = [(s, None) for s, _ in taps]
    slots = _shift_plan(keys)
    dirs = sorted({m for _, m in taps if m is not None})

    def body(up, uc, un, w_ref, b_ref, *rest):
        y_ref = rest[0]
        pad_ref, copies_ref = rest[-2], rest[-1]
        i = pl.program_id(1)
        _fill_halo(pad_ref, up, uc, un, i, nrc, rc, hb, boundary)
        _build_shifted(copies_ref, slots, pad_ref, keys, i, rc, hb, sub)
        for r0 in range(0, rc, sub):
            acc = jnp.broadcast_to(b_ref[...], (sub, ct))
            for m in [None] + dirs:
                part = None
                for k, (s, mk) in enumerate(taps):
                    if mk != m:
                        continue
                    v = _read(copies_ref, slots, pad_ref, s, None, hb + r0, sub)
                    ok = _seg_ok(boundary, i, rc, r0, sub, s)
                    term = w_ref[k:k + 1, :] * (v if ok is None else jnp.where(ok, v, 0.0))
                    part = term if part is None else part + term
                if part is None:
                    continue
                if m is not None:
                    t = i * rc + r0 + lax.broadcasted_iota(jnp.int32, (sub, 1), 0)
                    part = jnp.where(_col_mask(m, t), part, 0.0)
                acc = acc + part
            y_ref[r0:r0 + sub, :] = acc
            if act:
                rest[1][r0:r0 + sub, :] = _silu(acc)

    n_out = 2 if act else 1
    res = _pcall(
        body, name=name, grid=(ncc, nrc),
        in_specs=_halo_specs(rc, ct, hb, T, col_off // ct) + [pl.BlockSpec((K, ct), lambda j, i: (0, j)),
                                                              pl.BlockSpec((1, ct), lambda j, i: (0, j))],
        out_specs=[pl.BlockSpec((rc, ct), lambda j, i: (i, j))] * n_out,
        out_shape=[jax.ShapeDtypeStruct((T, C), F32)] * n_out,
        scratch_shapes=[pltpu.VMEM((rc + 2 * hb, ct), F32), pltpu.VMEM((max(len(slots), 1), rc + 2 * hb, ct), F32)],
        compiler_params=_cparams(2),
    )(u, u, u, w, b)
    return res if act else res[0]


def _conv_bwd(name, u, col_off, C, w, g, taps, du_dtype=F32):
    T = u.shape[0]
    rc, ct, hb, sub, nrc, ncc, boundary, taps = _conv_plan(T, C, taps)
    K = len(taps)
    u_keys = [(s, None) for s, _ in taps]
    dirs = sorted({m for _, m in taps if m is not None})
    g_keys = [(-s, m) for s, m in taps] + [(0, m) for m in dirs]
    u_slots, g_slots = _shift_plan(u_keys), _shift_plan(g_keys)

    def body(up, uc, un, gp, gc, gn, w_ref, du_ref, dw_ref, db_ref, upad, gpad, ucopies, gcopies):
        i = pl.program_id(1)
        _fill_halo(upad, up, uc, un, i, nrc, rc, hb, boundary)
        _fill_halo(gpad, gp, gc, gn, i, nrc, rc, hb, boundary)
        _build_shifted(ucopies, u_slots, upad, u_keys, i, rc, hb, sub)
        _build_shifted(gcopies, g_slots, gpad, g_keys, i, rc, hb, sub)

        @pl.when(i == 0)
        def _():
            dw_ref[...] = jnp.zeros_like(dw_ref)
            db_ref[...] = jnp.zeros_like(db_ref)

        def fold(v):
            return jnp.sum(v.reshape(sub // SUBLANE, SUBLANE, ct), axis=0)

        dbs = jnp.zeros((SUBLANE, ct), F32)
        for r0 in range(0, rc, sub):
            dbs = dbs + fold(gpad[hb + r0:hb + r0 + sub, :])
            acc = jnp.zeros((sub, ct), F32)
            for k, (s, m) in enumerate(taps):
                v = _read(gcopies, g_slots, gpad, -s, m, hb + r0, sub)
                ok = _seg_ok(boundary, i, rc, r0, sub, -s)
                acc = acc + w_ref[k:k + 1, :] * (v if ok is None else jnp.where(ok, v, 0.0))
            du_ref[r0:r0 + sub, :] = acc.astype(du_ref.dtype)
        db_ref[...] += jnp.sum(dbs, axis=0, keepdims=True)
        for k, (s, m) in enumerate(taps):
            part = jnp.zeros((SUBLANE, ct), F32)
            for r0 in range(0, rc, sub):
                v = _read(ucopies, u_slots, upad, s, None, hb + r0, sub)
                ok = _seg_ok(boundary, i, rc, r0, sub, s)
                part = part + fold(_read(gcopies, g_slots, gpad, 0, m, hb + r0, sub)
                                   * (v if ok is None else jnp.where(ok, v, 0.0)))
            dw_ref[k:k + 1, :] += jnp.sum(part, axis=0, keepdims=True)

    halo_u = _halo_specs(rc, ct, hb, T, col_off // ct)
    halo_g = _halo_specs(rc, ct, hb, T, 0)
    rows = rc + 2 * hb
    return _pcall(
        body, name=name, grid=(ncc, nrc),
        in_specs=halo_u + halo_g + [pl.BlockSpec((K, ct), lambda j, i: (0, j))],
        out_specs=[pl.BlockSpec((rc, ct), lambda j, i: (i, j)), pl.BlockSpec((K, ct), lambda j, i: (0, j)),
                   pl.BlockSpec((1, ct), lambda j, i: (0, j))],
        out_shape=[jax.ShapeDtypeStruct((T, C), du_dtype), jax.ShapeDtypeStruct((K, C), F32),
                   jax.ShapeDtypeStruct((1, C), F32)],
        scratch_shapes=[pltpu.VMEM((rows, ct), F32), pltpu.VMEM((rows, ct), F32),
                        pltpu.VMEM((max(len(u_slots), 1), rows, ct), F32),
                        pltpu.VMEM((max(len(g_slots), 1), rows, ct), F32)],
        compiler_params=_cparams(2),
    )(u, u, u, g, g, g, w)


def _ssd_group(xg, bm, cm, s_in, *per_head, reverse, P):
    R = len(per_head) // 2
    dtrs, a_s = per_head[:R], per_head[R:]
    q, rp = xg.shape
    ii = lax.broadcasted_iota(jnp.int32, (q, q), 0)
    jj = lax.broadcasted_iota(jnp.int32, (q, q), 1)
    causal = (jj >= ii) if reverse else (jj <= ii)
    causal_t = (ii >= jj) if reverse else (ii <= jj)
    eye = ii == jj
    lane = lax.broadcasted_iota(jnp.int32, (1, rp), 1)
    row = lax.broadcasted_iota(jnp.int32, (rp, 1), 0)
    nt = (((1,), (1,)), ((), ()))
    tn = (((0,), (0,)), ((), ()))
    cb = lax.dot_general(cm.astype(BF16), bm.astype(BF16), nt, preferred_element_type=F32)
    dt_x = jnp.zeros((q, rp), F32)
    acum_x = jnp.zeros((q, rp), F32)
    tot_row = jnp.zeros((1, rp), F32)
    tot_col = jnp.zeros((rp, 1), F32)
    wts, lane_masks = [], []
    for r in range(R):
        hm = (lane >= r * P) & (lane < (r + 1) * P)
        hc = (row >= r * P) & (row < (r + 1) * P)
        dt_c = jnp.sum(jnp.where(eye, dtrs[r], 0.0), axis=1, keepdims=True)
        dac = dt_c * a_s[r]
        dar = dtrs[r] * a_s[r]
        acum_c = jnp.sum(jnp.where(causal, dar, 0.0), axis=1, keepdims=True)
        acum_r = jnp.sum(jnp.where(causal_t, dac, 0.0), axis=0, keepdims=True)
        decay = jnp.where(causal, jnp.exp(jnp.where(causal, acum_c - acum_r, 0.0)), 0.0)
        tot = jnp.sum(dac, axis=0, keepdims=True)
        dt_x = jnp.where(hm, dt_c, dt_x)
        acum_x = jnp.where(hm, acum_c, acum_x)
        tot_row = jnp.where(hm, tot, tot_row)
        tot_col = jnp.where(hc, tot, tot_col)
        wts.append((cb * decay).astype(BF16))
        lane_masks.append(hm)
    xdt = xg * dt_x
    xdt_b = xdt.astype(BF16)
    y = jnp.zeros((q, rp), F32)
    for r in range(R):
        y = jnp.where(lane_masks[r], jnp.dot(wts[r], xdt_b, preferred_element_type=F32), y)
    dte = jnp.exp(tot_row - acum_x)
    cs = lax.dot_general((xdt * dte).astype(BF16), bm.astype(BF16), tn, preferred_element_type=F32)
    y = y + lax.dot_general(cm.astype(BF16), s_in.astype(BF16), nt, preferred_element_type=F32) * jnp.exp(acum_x)
    s_out = jnp.exp(tot_col) * s_in + cs
    return y, s_out


def _ssd_maps(NC, ncc, reverse_steps):
    def chunk(d, s):
        if reverse_steps:
            s = NC - 1 - s
        return s if d == 0 else jnp.where(s < ncc, ncc - 1 - s, NC - 1 - s + ncc)

    def lat_chunk(d, s):
        c = chunk(d, s) - ncc
        return jnp.where(c < 0, 0 if d == 0 else NC - ncc - 1, c)

    def step(s):
        return NC - 1 - s if reverse_steps else s

    return chunk, lat_chunk, step


def _ssd_specs(chunk, d, R, Q, N, RP, bo, co):
    return [
        pl.BlockSpec((Q, RP), lambda g, s: (chunk(d, s), g)),
        pl.BlockSpec((Q, N), lambda g, s: (chunk(d, s), bo + g)),
        pl.BlockSpec((Q, N), lambda g, s: (chunk(d, s), co + g)),
        pl.BlockSpec((R, 1, Q), lambda g, s: (g, 0, chunk(d, s))),
        pl.BlockSpec((R, 1, 1), lambda g, s: (g, 0, 0)),
    ]


def _ssd_fwd(xbc, b_off, c_off, dtr, a, P, ncc):
    T = xbc.shape[0]
    H = dtr[0].shape[0]
    N, Q = SSD_STATE, SSD_CHUNK
    NC = T // Q
    G = (c_off - b_off) // N
    R = H // G
    RP = R * P
    chunk, lat_chunk, _ = _ssd_maps(NC, ncc, False)

    def body(*refs):
        s = pl.program_id(1)
        s_ref = refs[-1]

        @pl.when(s == 0)
        def _():
            s_ref[...] = jnp.zeros_like(s_ref)

        for d in range(2):
            x_ref, b_ref, c_ref, dtr_ref, a_ref = refs[5 * d:5 * d + 5]
            y_ref, se_ref = refs[10 + 2 * d:12 + 2 * d]
            s_in = s_ref[d]
            se_ref[...] = s_in
            per_head = [dtr_ref[r] for r in range(R)] + [a_ref[r] for r in range(R)]
            y, s_out = _ssd_group(x_ref[...], b_ref[...], c_ref[...], s_in, *per_head, reverse=d == 1, P=P)
            y_ref[...] = y
            s_ref[d] = s_out

    in_specs, out_specs, out_shape, operands = [], [], [], []
    for d in range(2):
        in_specs += _ssd_specs(chunk, d, R, Q, N, RP, b_off // N, c_off // N)
        operands += [xbc, xbc, xbc, dtr[d], a[d]]
        out_specs += [pl.BlockSpec((Q, RP), functools.partial(lambda g, s, d: (lat_chunk(d, s), g), d=d)),
                      pl.BlockSpec((None, None, RP, N), lambda g, s: (g, s, 0, 0))]
        out_shape += [jax.ShapeDtypeStruct((T - ncc * Q, H * P), F32), jax.ShapeDtypeStruct((G, NC, RP, N), F32)]
    y_f, se_f, y_b, se_b = _pcall(
        body, name="ssd_fwd", grid=(G, NC), in_specs=in_specs, out_specs=out_specs, out_shape=out_shape,
        scratch_shapes=[pltpu.VMEM((2, RP, N), F32)], compiler_params=_cparams(2),
    )(*operands)
    return (y_f, y_b), (se_f, se_b)


def _ssd_bwd(xbc, b_off, c_off, dtr, a, s_enter, dy, P, ncc):
    T = xbc.shape[0]
    H = dtr[0].shape[0]
    N, Q = SSD_STATE, SSD_CHUNK
    NC = T // Q
    G = (c_off - b_off) // N
    R = H // G
    RP = R * P
    chunk, lat_chunk, step = _ssd_maps(NC, ncc, True)
    n_in, n_out = 7, 5

    def body(*refs):
        s = pl.program_id(1)
        ds_ref = refs[-1]

        @pl.when(s == 0)
        def _():
            ds_ref[...] = jnp.zeros_like(ds_ref)

        for d in range(2):
            x_ref, b_ref, c_ref, dtr_ref, a_ref, se_ref, dy_ref = refs[n_in * d:n_in * (d + 1)]
            dx_ref, db_ref, dc_ref, ddtr_ref, da_ref = refs[2 * n_in + n_out * d:2 * n_in + n_out * (d + 1)]
            per_head = [dtr_ref[r] for r in range(R)] + [a_ref[r] for r in range(R)]
            f = functools.partial(_ssd_group, reverse=d == 1, P=P)
            _, vjp = jax.vjp(f, x_ref[...], b_ref[...], c_ref[...], se_ref[...], *per_head)
            is_latent = chunk(d, s) >= ncc
            dy_v = jnp.where(is_latent, dy_ref[...], 0.0)
            grads = vjp((dy_v, ds_ref[d]))
            dx_ref[...] = grads[0]
            db_ref[...] = grads[1]
            dc_ref[...] = grads[2]
            ds_ref[d] = grads[3]
            for r in range(R):
                ddtr_ref[r] = grads[4 + r]
                da_ref[r] = jnp.broadcast_to(grads[4 + R + r], (SUBLANE, LANE))

    in_specs, out_specs, out_shape, operands = [], [], [], []
    for d in range(2):
        in_specs += _ssd_specs(chunk, d, R, Q, N, RP, b_off // N, c_off // N) + [
            pl.BlockSpec((None, None, RP, N), lambda g, s: (g, step(s), 0, 0)),
            pl.BlockSpec((Q, RP), functools.partial(lambda g, s, d: (lat_chunk(d, s), g), d=d)),
        ]
        operands += [xbc, xbc, xbc, dtr[d], a[d], s_enter[d], dy]
    for d in range(2):
        at_chunk = functools.partial(lambda g, s, d: (chunk(d, s), g), d=d)
        out_specs += [
            pl.BlockSpec((Q, RP), at_chunk), pl.BlockSpec((Q, N), at_chunk), pl.BlockSpec((Q, N), at_chunk),
            pl.BlockSpec((R, 1, Q), functools.partial(lambda g, s, d: (g, 0, chunk(d, s)), d=d)),
            pl.BlockSpec((R, SUBLANE, LANE), lambda g, s: (g * NC + s, 0, 0)),
        ]
        out_shape += [
            jax.ShapeDtypeStruct((T, H * P), F32), jax.ShapeDtypeStruct((T, G * N), F32),
            jax.ShapeDtypeStruct((T, G * N), F32), jax.ShapeDtypeStruct((H, 1, T), F32),
            jax.ShapeDtypeStruct((G * NC * R, SUBLANE, LANE), F32),
        ]
    res = _pcall(
        body, name="ssd_bwd", grid=(G, NC), in_specs=in_specs, out_specs=out_specs, out_shape=out_shape,
        scratch_shapes=[pltpu.VMEM((2, RP, N), F32)], compiler_params=_cparams(2),
    )(*operands)
    return res[:n_out], res[n_out:]


def _allgather8(name, v):
    R, C = v.shape

    def body(x_ref, out_ref, send_sems, recv_sems, local_sem):
        x, y, c = lax.axis_index("x"), lax.axis_index("y"), lax.axis_index("c")
        me, sibling = (x, y, c), (x, y, 1 - c)
        chips = [(1 - x, y), (x, 1 - y), (1 - x, 1 - y)]

        def slot(px, py, pc):
            return out_ref.at[4 * px + 2 * py + pc]

        def copy(k, block, to, src=None):
            return pltpu.make_async_remote_copy(
                src_ref=slot(*block) if src is None else src, dst_ref=slot(*block),
                send_sem=send_sems.at[k], recv_sem=recv_sems.at[k], device_id=to, device_id_type=MESH)

        mine = pltpu.make_async_copy(x_ref, slot(*me), local_sem)
        mine.start()
        first = [copy(0, me, sibling, src=x_ref)]
        first += [copy(1 + j, me, (*chip, c), src=x_ref) for j, chip in enumerate(chips)]
        for cp in first:
            cp.start()
        passed = [copy(4 + j, (*chip, c), sibling) for j, chip in enumerate(chips)]
        for j, chip in enumerate(chips):
            copy(1 + j, (*chip, c), me).wait_recv()
            passed[j].start()
        copy(0, sibling, me).wait_recv()
        for j, chip in enumerate(chips):
            copy(4 + j, (*chip, 1 - c), me).wait_recv()
        for cp in first + passed:
            cp.wait_send()
        mine.wait()

    return _pcall(
        body, name=name, out_shape=jax.ShapeDtypeStruct((N_DEV, R, C), v.dtype),
        in_specs=[pl.BlockSpec(memory_space=pltpu.VMEM)], out_specs=pl.BlockSpec(memory_space=pltpu.VMEM),
        scratch_shapes=[pltpu.SemaphoreType.DMA((7,)), pltpu.SemaphoreType.DMA((7,)), pltpu.SemaphoreType.DMA],
        compiler_params=pltpu.CompilerParams(vmem_limit_bytes=VMEM_LIMIT_BYTES),
    )(v)


def _slot(ref, k, axis, size):
    if axis is None:
        return ref.at[k]
    align = LANE if size % LANE == 0 else 2 * SUBLANE
    assert size % align == 0
    return ref.at[(slice(None),) * axis + (pl.ds(pl.multiple_of(k * size, align), size),)]


def _exchange4_start(name, srcs, bcast, dep, axes=None):
    n = len(srcs)
    axes = list(axes) if axes is not None else [None] * n
    sizes = [None if ax is None else s.shape[ax] for s, ax in zip(srcs, axes)]

    def land_shape(s, ax):
        if not bcast:
            return s.shape
        if ax is None:
            return (N_CHIPS,) + s.shape
        return s.shape[:ax] + (N_CHIPS * s.shape[ax],) + s.shape[ax + 1:]

    lands = [lax.empty(land_shape(s, ax), s.dtype) for s, ax in zip(srcs, axes)]

    def body(*refs):
        src, land = refs[:n], refs[n:2 * n]
        send_sems, recv_sems = refs[2 * n + 1], refs[2 * n + 2]
        token = refs[-1]
        x, y, c = lax.axis_index("x"), lax.axis_index("y"), lax.axis_index("c")
        me = 2 * x + y
        for a in range(n):
            for j, (px, py) in enumerate([(1 - x, y), (x, 1 - y), (1 - x, 1 - y)]):
                pltpu.make_async_remote_copy(
                    src_ref=src[a] if bcast else src[a].at[2 * px + py], dst_ref=_slot(land[a], me, axes[a], sizes[a]),
                    send_sem=send_sems.at[3 * a + j], recv_sem=recv_sems.at[3 * a + j], device_id=(px, py, c),
                    device_id_type=MESH).start()
        token[...] = jnp.zeros_like(token)

    hbm = pl.BlockSpec(memory_space=pltpu.HBM)
    sem = pl.BlockSpec(memory_space=pltpu.SEMAPHORE)
    outs = _pcall(
        body, name=name,
        out_shape=(pltpu.SemaphoreType.DMA((3 * n,)), pltpu.SemaphoreType.DMA((3 * n,)),
                   *[pltpu.HBM(s.shape, s.dtype) for s in srcs], *[pltpu.HBM(l.shape, l.dtype) for l in lands],
                   jax.ShapeDtypeStruct((SUBLANE, LANE), F32)),
        in_specs=[hbm] * (2 * n) + [pl.BlockSpec(memory_space=pl.ANY)],
        out_specs=(sem, sem, *[hbm] * (2 * n), pl.BlockSpec(memory_space=pltpu.VMEM)),
        input_output_aliases={k: 2 + k for k in range(2 * n)},
        compiler_params=pltpu.CompilerParams(has_side_effects=pltpu.SideEffectType.DATAFLOW_SIDE_EFFECTING),
    )(*[pltpu.with_memory_space_constraint(s, pltpu.HBM) for s in srcs],
      *[pltpu.with_memory_space_constraint(l, pltpu.HBM) for l in lands], dep)
    return (n, bcast, axes, sizes, outs[0], outs[1], outs[2:2 + n], outs[2 + n:2 + 2 * n]), outs[-1]


def _exchange4_wait(name, handle, after):
    n, bcast, axes, sizes, send_sems, recv_sems, src_thru, land_thru = handle

    def body(*refs):
        src, land = refs[:n], refs[n:2 * n]
        send_sems, recv_sems = refs[2 * n], refs[2 * n + 1]
        x, y, c = lax.axis_index("x"), lax.axis_index("y"), lax.axis_index("c")
        for a in range(n):
            for j, (px, py) in enumerate([(1 - x, y), (x, 1 - y), (1 - x, 1 - y)]):
                pk = 2 * px + py
                copy = pltpu.make_async_remote_copy(
                    src_ref=src[a] if bcast else src[a].at[pk], dst_ref=_slot(land[a], pk, axes[a], sizes[a]),
                    send_sem=send_sems.at[3 * a + j], recv_sem=recv_sems.at[3 * a + j], device_id=(px, py, c),
                    device_id_type=MESH)
                copy.wait_send()
                copy.wait_recv()

    hbm = pl.BlockSpec(memory_space=pltpu.HBM)
    sem = pl.BlockSpec(memory_space=pltpu.SEMAPHORE)
    outs = _pcall(
        body, name=name,
        out_shape=tuple(pltpu.HBM(t.shape, t.dtype) for t in (*src_thru, *land_thru)),
        in_specs=[hbm] * (2 * n) + [sem, sem, pl.BlockSpec(memory_space=pl.ANY)], out_specs=tuple([hbm] * (2 * n)),
        input_output_aliases={k: k for k in range(2 * n)},
        compiler_params=pltpu.CompilerParams(has_side_effects=pltpu.SideEffectType.DATAFLOW_SIDE_EFFECTING),
    )(*src_thru, *land_thru, send_sems, recv_sems, after)
    return list(outs[:n]), list(outs[n:])


def _tie(name, v, token):
    def body(v_ref, token_ref, o_ref):
        del v_ref, token_ref, o_ref

    any_spec = pl.BlockSpec(memory_space=pl.ANY)
    return _pcall(body, name=name, out_shape=jax.ShapeDtypeStruct(v.shape, v.dtype), in_specs=[any_spec, any_spec],
                  out_specs=any_spec, input_output_aliases={0: 0})(v, token)


def _fill_own(landed, own, me, bcast):
    blk = own if bcast else lax.dynamic_index_in_dim(own, me, 0, keepdims=False)
    return lax.dynamic_update_index_in_dim(landed, blk, me, 0)


def _swap_sibling(name, srcs):
    n = len(srcs)

    def body(*refs):
        src, out = refs[:n], refs[n:2 * n]
        send_sems, recv_sems = refs[2 * n:]
        x, y, c = lax.axis_index("x"), lax.axis_index("y"), lax.axis_index("c")
        copies = []
        for a in range(n):
            rc = pltpu.make_async_remote_copy(
                src_ref=src[a], dst_ref=out[a], send_sem=send_sems.at[a], recv_sem=recv_sems.at[a],
                device_id=(x, y, 1 - c), device_id_type=MESH)
            rc.start()
            copies.append(rc)
        for cp in copies:
            cp.wait()

    any_spec = pl.BlockSpec(memory_space=pl.ANY)
    return _pcall(
        body, name=name, out_shape=[jax.ShapeDtypeStruct(s.shape, s.dtype) for s in srcs],
        in_specs=[any_spec] * n, out_specs=[any_spec] * n,
        scratch_shapes=[pltpu.SemaphoreType.DMA((n,)), pltpu.SemaphoreType.DMA((n,))],
    )(*srcs)


def _mod_fwd(c16, mod_w, mod_b_shard):
    nl, D, S = mod_w.shape

    def body(c_ref, w_ref, b_ref, o_ref):
        s = _silu(c_ref[...]).astype(BF16)
        o_ref[...] = jnp.dot(s, w_ref[...].astype(BF16), preferred_element_type=F32) + b_ref[...]

    return _pcall(
        body, name="mod_fwd", grid=(nl,),
        in_specs=[pl.BlockSpec((16, D), lambda l: (0, 0)), pl.BlockSpec((None, D, S), lambda l: (l, 0, 0)),
                  pl.BlockSpec((None, 1, S), lambda l: (l, 0, 0))],
        out_specs=pl.BlockSpec((None, 16, S), lambda l: (l, 0, 0)),
        out_shape=jax.ShapeDtypeStruct((nl, 16, S), F32), compiler_params=_cparams(1),
    )(c16, mod_w, mod_b_shard)


def _mod_w_update(s16t, dm16, w, m, v):
    nl, D, S = w.shape
    tm = _row_tile(D, 256)

    def body(s_ref, dm_ref, w_ref, m_ref, v_ref, g_ref, dl_ref, nm_ref, nv_ref):
        g = jnp.dot(s_ref[...], dm_ref[...], preferred_element_type=F32, precision=HIGHEST)
        g, dl, nm, nv = _f_adamw(w_ref[...], m_ref[...], v_ref[...], g, jnp.zeros_like(g))
        g_ref[...] = g
        dl_ref[...] = dl
        nm_ref[...] = nm
        nv_ref[...] = nv

    big = pl.BlockSpec((None, tm, S), lambda l, i: (l, i, 0))
    return _pcall(
        body, name="mod_w_update", grid=(nl, D // tm),
        in_specs=[pl.BlockSpec((tm, 16), lambda l, i: (i, 0)), pl.BlockSpec((None, 16, S), lambda l, i: (l, 0, 0)),
                  big, big, big],
        out_specs=[big] * 4, out_shape=[jax.ShapeDtypeStruct(w.shape, F32)] * 4, compiler_params=_cparams(2),
    )(s16t, dm16, w, m, v)


def _size(shape):
    n = 1
    for d in shape:
        n *= d
    return n


def _pack(arrs):
    pieces = []
    for a in arrs:
        flat = a.reshape(-1).astype(F32)
        pieces.append(jnp.pad(flat, (0, _round_up(flat.shape[0], LANE) - flat.shape[0])).reshape(-1, LANE))
    buf = jnp.concatenate(pieces, axis=0)
    return jnp.pad(buf, ((0, _round_up(buf.shape[0], SUBLANE) - buf.shape[0]), (0, 0)))


def _unpack(buf, shapes):
    lead = buf.shape[:-2]
    out, row = [], 0
    for s in shapes:
        n = _size(s)
        rows = _cdiv(n, LANE)
        piece = buf[..., row:row + rows, :].reshape(lead + (rows * LANE,))
        out.append(piece[..., :n].reshape(lead + tuple(s)))
        row += rows
    return out


def _adamw_many(name, ws, ms, vs, gs):
    n = len(ws)

    def body(*refs):
        for k in range(n):
            res = _f_adamw(refs[k][...], refs[n + k][...], refs[2 * n + k][...], refs[3 * n + k][...], 0.0)
            for j in range(4):
                refs[(4 + j) * n + k][...] = res[j]

    vmem = pl.BlockSpec(memory_space=pltpu.VMEM)
    res = _pcall(body, name=name, out_shape=[jax.ShapeDtypeStruct(w.shape, F32) for _ in range(4) for w in ws],
                 in_specs=[vmem] * (4 * n), out_specs=[vmem] * (4 * n))(*ws, *ms, *vs, *gs)
    return [tuple(res[j * n + k] for j in range(4)) for k in range(n)]


SHARD_AXIS = {
    "mod_w": 2, "ssd_w_in": 2, "ssd_conv_w": 2, "ssd_w_out": 1, "conf_w_pw1": 2, "conf_b_pw1": 1, "conf_w_dw": 2,
    "conf_b_dw": 1, "conf_ln_w": 1, "conf_ln_b": 1, "conf_w_pw2": 1, "conf_b_pw2": 1, "ffn_w_up": 2,
    "ffn_conv_w": 3, "ffn_w_down": 1,
}
BIG = ("ssd_w_in", "ssd_w_out", "conf_w_pw1", "conf_w_pw2", "ffn_w_up", "ffn_w_down")
WEIGHTS = ("c_ctx", "mod_w", "mod_b", "norm1_w", "norm2_w", "ssd_w_in", "ssd_conv_w", "ssd_conv_b", "ssd_dt_bias",
           "ssd_a_log", "ssd_d", "ssd_norm_w", "ssd_w_out", "conf_w_pw1", "conf_b_pw1", "conf_w_dw", "conf_b_dw",
           "conf_ln_w", "conf_ln_b", "conf_w_pw2", "conf_b_pw2", "ffn_w_up", "ffn_conv_w", "ffn_conv_b",
           "ffn_w_down", "final_norm_w")
SMALL = tuple(n for n in WEIGHTS if n not in BIG and n != "mod_w")
SMALL_SHARDED = tuple(n for n in SMALL if n in SHARD_AXIS)


def _unshard(stacked, axis):
    return jnp.concatenate([stacked[k] for k in range(N_CHIPS)], axis=axis)


def _to_blocks(full, axis):
    return jnp.stack(jnp.split(full, N_CHIPS, axis=axis))


def _par(v):
    v = v.reshape(-1, v.shape[-1])
    return v[:, None, :]


def kernel(x, c, ctx, c_ctx, mod_w, mod_b, norm1_w, norm2_w, ssd_w_in, ssd_conv_w, ssd_conv_b, ssd_dt_bias, ssd_a_log, ssd_d, ssd_norm_w, ssd_w_out, conf_w_pw1, conf_b_pw1, conf_w_dw, conf_b_dw, conf_ln_w, conf_ln_b, conf_w_pw2, conf_b_pw2, ffn_w_up, ffn_conv_w, ffn_conv_b, ffn_w_down, final_norm_w, loss_target, m_c_ctx, m_mod_w, m_mod_b, m_norm1_w, m_norm2_w, m_ssd_w_in, m_ssd_conv_w, m_ssd_conv_b, m_ssd_dt_bias, m_ssd_a_log, m_ssd_d, m_ssd_norm_w, m_ssd_w_out, m_conf_w_pw1, m_conf_b_pw1, m_conf_w_dw, m_conf_b_dw, m_conf_ln_w, m_conf_ln_b, m_conf_w_pw2, m_conf_b_pw2, m_ffn_w_up, m_ffn_conv_w, m_ffn_conv_b, m_ffn_w_down, m_final_norm_w, v_c_ctx, v_mod_w, v_mod_b, v_norm1_w, v_norm2_w, v_ssd_w_in, v_ssd_conv_w, v_ssd_conv_b, v_ssd_dt_bias, v_ssd_a_log, v_ssd_d, v_ssd_norm_w, v_ssd_w_out, v_conf_w_pw1, v_conf_b_pw1, v_conf_w_dw, v_conf_b_dw, v_conf_ln_w, v_conf_ln_b, v_conf_w_pw2, v_conf_b_pw2, v_ffn_w_up, v_ffn_conv_w, v_ffn_conv_b, v_ffn_w_down, v_final_norm_w):
    given = dict(locals())
    W = {n: given[n] for n in WEIGHTS}
    Mo = {n: given["m_" + n] for n in WEIGHTS}
    Vo = {n: given["v_" + n] for n in WEIGHTS}

    ax, ay, ac = lax.axis_index("x"), lax.axis_index("y"), lax.axis_index("c")
    chip = 2 * ax + ay
    dev = 4 * ax + 2 * ay + ac

    D = x.shape[-1]
    L, Lc = x.shape[1], ctx.shape[1]
    T0 = L + Lc
    H = ssd_a_log.shape[-1]
    DI = ssd_norm_w.shape[-1]
    P = DI // H
    CD = ssd_conv_b.shape[-1]
    N = SSD_STATE
    G = (CD - DI) // (2 * N)
    FH = ffn_conv_b.shape[-1]
    KS = ssd_conv_w.shape[1]
    KC = conf_w_dw.shape[1]
    ncc = Lc // SSD_CHUNK

    shard_b = {n: W[n].astype(BF16) for n in BIG}

    small_shard_shapes = [W[n].shape for n in SMALL_SHARDED]
    f1 = _allgather8("gather_small", _pack([c] + [W[n] for n in SMALL_SHARDED]))
    parts = _unpack(f1, [c.shape] + small_shard_shapes)
    Wf = dict(W)
    for n, p in zip(SMALL_SHARDED, parts[1:]):
        Wf[n] = _unshard(p[::2], SHARD_AXIS[n])
    c16 = jnp.concatenate([parts[0].reshape(N_DEV, D), c_ctx[None, :], jnp.zeros((16 - N_DEV - 1, D), F32)], axis=0)

    S_mod = mod_w.shape[-1]
    mod_b_shard = lax.dynamic_slice_in_dim(mod_b, chip * S_mod, S_mod, axis=1)[:, None, :]
    mod_part = _mod_fwd(c16, mod_w, mod_b_shard)
    f2 = _allgather8("gather_mod", mod_part.reshape(2 * 16, S_mod))
    mods = jnp.concatenate([f2[2 * k].reshape(2, 16, S_mod) for k in range(N_CHIPS)], axis=-1)
    my = lax.dynamic_slice_in_dim(mods, dev, 1, axis=1)[:, 0]
    sh1, sc1, g1, sh2, sc2, g2 = [[my[l, k * D:(k + 1) * D] for l in range(2)] for k in range(6)]
    csh1, csc1 = mods[0, N_DEV, 0:D], mods[0, N_DEV, D:2 * D]

    gather_a, token = _exchange4_start("gather_w_in_start", [shard_b["ssd_w_in"]], True, mods)
    csc1 = _tie("tie_gather_w_in", csc1, token)

    def full_weight(n, own, landed):
        if landed.ndim == own.ndim:
            ax = SHARD_AXIS[n]
            return lax.dynamic_update_slice_in_dim(landed, own, chip * own.shape[ax], ax)
        return _unshard(_fill_own(landed, own, chip, True), SHARD_AXIS[n])

    xl = x[0]
    rows0 = (ctx[0], xl)
    n1w0, n2w0, n1w1, n2w1 = _par(norm1_w[0]), _par(norm2_w[0]), _par(norm1_w[1]), _par(norm2_w[1])
    sc_seg = jnp.stack([csc1, sc1[0]])[:, None, :]
    sh_seg = jnp.stack([csh1, sh1[0]])[:, None, :]

    a0 = _rw_fwd("l0_modnorm1", _f_modnorm, [], [n1w0, sc_seg, sh_seg], [D], T=T0, seg_rows=(Lc,), head=rows0,
                 out_dtypes=[BF16])
    (own_in,), (landed_in,) = _exchange4_wait("gather_w_in_wait", gather_a, a0)
    w_in = full_weight("ssd_w_in", own_in, landed_in)[0]
    def start_gather(tag, names, dep):
        handle, tok = _exchange4_start("gather_" + tag + "_start", [shard_b[n] for n in names], True, dep,
                                       axes=[1 if SHARD_AXIS[n] == 1 else None for n in names])
        return (names, handle), tok

    def finish_gather(tag, group, after):
        names, handle = group
        return {n: full_weight(n, own, g)
                for n, own, g in zip(names, *_exchange4_wait("gather_" + tag + "_wait", handle, after))}

    gather_b, token = start_gather("mix", ["ssd_w_out", "conf_w_pw1", "conf_w_pw2"], landed_in)
    gather_c, token = start_gather("ffn", ["ffn_w_up", "ffn_w_down"], token)
    a0 = _tie("tie_gather_rest", a0, token)
    proj = _mm(a0, w_in, name="l0_w_in")
    seg_taps = [(k - KS // 2, ("seg", Lc)) for k in range(KS)]
    xbc_pre, xbc = _conv_fwd("l0_conv", proj, DI, CD, Wf["ssd_conv_w"][0], ssd_conv_b, seg_taps, act=True)
    dt_raw = proj[:, DI + CD:]
    dt_bias = _par(ssd_dt_bias.reshape(1, 2 * H))
    dt = _rw_fwd("l0_softplus", _f_softplus, [dt_raw], [dt_bias], [2 * H])
    dt_t = dt.T
    dtr = (dt_t[:H, None, :], dt_t[H:, None, :])
    a_all = -jnp.exp(ssd_a_log.reshape(2, H, 1, 1))
    a_neg = (a_all[0], a_all[1])
    (y_f, y_b), s_enter = _ssd_fwd(xbc, DI, DI + G * N, dtr, a_neg, P, ncc)
    gate_rows = [y_f, y_b, (xbc, 0, DI, Lc), (proj, 0, DI, Lc)]
    d_rep = _par(jnp.repeat(ssd_d[0], P))
    ssd_nw = _par(ssd_norm_w[0])
    yn = _rw_fwd("l0_ssd_gate", _f_ssd_gate, gate_rows, [d_rep, ssd_nw], [DI], T=L, out_dtypes=[BF16])
    Wb = finish_gather("mix", gather_b, yn)
    w_out, w_pw1, w_pw2 = Wb["ssd_w_out"][0], Wb["conf_w_pw1"][0], Wb["conf_w_pw2"][0]
    mix0 = _mm(yn, w_out, name="l0_w_out")
    g1_0, g2_0, g1_1, g2_1 = _par(g1[0]), _par(g2[0]), _par(g1[1]), _par(g2[1])
    h1 = _rw_fwd("l0_res1", _f_gate_res, [xl, mix0], [g1_0], [D])
    Wb = finish_gather("ffn", gather_c, h1)
    w_up, w_dn = Wb["ffn_w_up"], Wb["ffn_w_down"]

    grid_taps = [((i - 1) * GRID_W + (j - 1), (None if j == 1 else ("col", j - 1))) for i in range(3) for j in range(3)]

    def ffn_fwd(l, h, tag):
        a = _rw_fwd(tag + "_modnorm2", _f_modnorm, [h], [_par(norm2_w[l]), _par(sc2[l]), _par(sh2[l])], [D],
                    out_dtypes=[BF16])
        hh = _mm(a, w_up[l], name=tag + "_w_up")
        gc = _conv_fwd(tag + "_ffn_conv", hh, FH, FH, Wf["ffn_conv_w"][l].reshape(9, FH), ffn_conv_b[l][None, :],
                       grid_taps)
        act = _rw_fwd(tag + "_act", _f_ffn_act, [(hh, 0, FH), gc], [], [FH], col_tile=_tile(FH, 1536),
                      out_dtypes=[BF16])
        dn = _mm(act, w_dn[l], name=tag + "_w_down")
        return a, hh, gc, act, dn

    a1, hh0, gc0, act0, dn0 = ffn_fwd(0, h1, "l0")
    h2 = _rw_fwd("l0_res2", _f_gate_res, [h1, dn0], [g2_0], [D])

    a2 = _rw_fwd("l1_modnorm1", _f_modnorm, [h2], [n1w1, _par(sc1[1]), _par(sh1[1])], [D], out_dtypes=[BF16])
    pw = _mm(a2, w_pw1, name="l1_pw1")
    b_pw1 = Wf["conf_b_pw1"][0]
    glu = _rw_fwd("l1_glu", _f_glu, [(pw, 0, D), (pw, D, D)], [_par(b_pw1[:D]), _par(b_pw1[D:])], [D])
    conf_taps = [(k - KC // 2, None) for k in range(KC)]
    cv = _conv_fwd("l1_conv", glu, 0, D, Wf["conf_w_dw"][0], Wf["conf_b_dw"], conf_taps)
    ln_w, ln_b = _par(Wf["conf_ln_w"][0]), _par(Wf["conf_ln_b"][0])
    ls = _rw_fwd("l1_ln_silu", _f_ln_silu, [cv], [ln_w, ln_b], [D], out_dtypes=[BF16])
    p2 = _mm(ls, w_pw2, name="l1_pw2")
    b_pw2 = _par(Wf["conf_b_pw2"][0])
    h3 = _rw_fwd("l1_res1", _f_gate_res_bias, [h2, p2], [g1_1, b_pw2], [D])
    a3, hh1, gc1, act1, dn1 = ffn_fwd(1, h3, "l1")
    h4 = _rw_fwd("l1_res2", _f_gate_res, [h3, dn1], [g2_1], [D])

    fnw = final_norm_w[None, :]
    tgt = loss_target[0]
    loss_local = _loss_fwd(h4, tgt, fnw)[0, 0]
    loss = lax.psum(loss_local, ("x", "y", "c"))

    G_full = {}
    reduces = {}

    def start_reduce(tag, items, dep):
        def blocks_of(g, ax):
            if g.ndim == 3:
                return g
            return g.reshape(N_CHIPS, g.shape[0] // N_CHIPS, g.shape[1]) if ax == 0 else _to_blocks(g, ax)

        blocks = [blocks_of(g, ax).astype(BF16) for _, g, ax in items]
        handle, tok = _exchange4_start("reduce_" + tag + "_start", blocks, False, dep)
        reduces[tag] = ([n for n, _, _ in items], handle)
        return tok
    ones = jnp.ones((L, 1), F32)
    (dh4,), (dfnw,) = _rw_bwd("loss_bwd", _f_loss_rows, [h4, tgt], [_par(final_norm_w)], [ones],
                              row_grad=[True, False], par_grad=[True])
    G_full["final_norm_w"] = dfnw.reshape(D)

    def ffn_bwd(l, h, saved, g2_l, dh_out, tag):
        a, hh, gc, act, dn = saved
        (ddn,), (dg2,) = _rw_bwd(tag + "_res2_bwd", _f_gate, [dn], [g2_l], [dh_out],
                                 row_grad=[True], par_grad=[True], row_dtypes=[BF16])
        dact = _mm(ddn, w_dn[l], tb=True, name=tag + "_w_down_dx")
        dwdn = _mm(act, ddn, ta=True, name=tag + "_w_down_dw", out_dtype=BF16)
        (dval, dgc), _ = _rw_bwd(tag + "_act_bwd", _f_ffn_act, [(hh, 0, FH), gc], [], [dact],
                                 row_grad=[True, True], par_grad=[], col_tile=_tile(FH, 1536), row_dtypes=[BF16, F32])
        dgin, dcw, dcb = _conv_bwd(tag + "_ffn_conv_bwd", hh, FH, FH, Wf["ffn_conv_w"][l].reshape(9, FH), dgc,
                                   grid_taps, du_dtype=BF16)
        dhh = jnp.concatenate([dval, dgin], axis=1)
        da = _mm(dhh, w_up[l], tb=True, name=tag + "_w_up_dx")
        dwup = _mm(a, dhh, ta=True, name=tag + "_w_up_dw", out_dtype=BF16, col_blocks=N_CHIPS)
        (dh,), (dn2w, dsc2, dsh2) = _rw_bwd(
            tag + "_modnorm2_bwd", _f_modnorm, [h], [_par(norm2_w[l]), _par(sc2[l]), _par(sh2[l])], [da],
            row_grad=[True], par_grad=[True, True, True], add=dh_out)
        return dh, dict(w_down=dwdn, w_up=dwup, conv_w=dcw.reshape(3, 3, FH), conv_b=dcb.reshape(FH),
                        n2w=dn2w.reshape(D), sc2=dsc2.reshape(D), sh2=dsh2.reshape(D), g2=dg2.reshape(D))

    dh3, gf1 = ffn_bwd(1, h3, (a3, hh1, gc1, act1, dn1), g2_1, dh4, "l1")
    (dp2,), (dg1_1, db_pw2) = _rw_bwd("l1_res1_bwd", _f_gate_bias, [p2], [g1_1, b_pw2], [dh3],
                                      row_grad=[True], par_grad=[True, True], row_dtypes=[BF16])
    dls = _mm(dp2, w_pw2, tb=True, name="l1_pw2_dx")
    dw_pw2 = _mm(ls, dp2, ta=True, name="l1_pw2_dw", out_dtype=BF16)
    (dcv,), (dln_w, dln_b) = _rw_bwd("l1_ln_silu_bwd", _f_ln_silu, [cv], [ln_w, ln_b], [dls],
                                     row_grad=[True], par_grad=[True, True])
    dglu, dw_dw, db_dw = _conv_bwd("l1_conv_bwd", glu, 0, D, Wf["conf_w_dw"][0], dcv, conf_taps)
    (dpa, dpg), (dba, dbg) = _rw_bwd("l1_glu_bwd", _f_glu, [(pw, 0, D), (pw, D, D)],
                                     [_par(b_pw1[:D]), _par(b_pw1[D:])], [dglu],
                                     row_grad=[True, True], par_grad=[True, True], row_dtypes=[BF16, BF16])
    dpw = jnp.concatenate([dpa, dpg], axis=1)
    da2 = _mm(dpw, w_pw1, tb=True, name="l1_pw1_dx")
    dw_pw1 = _mm(a2, dpw, ta=True, name="l1_pw1_dw", out_dtype=BF16, col_blocks=N_CHIPS)
    (dh2,), (dn1w1, dsc1_1, dsh1_1) = _rw_bwd(
        "l1_modnorm1_bwd", _f_modnorm, [h2], [n1w1, _par(sc1[1]), _par(sh1[1])], [da2],
        row_grad=[True], par_grad=[True, True, True], add=dh3)
    G_full["conf_b_pw2"] = db_pw2.reshape(1, D)
    G_full["conf_ln_w"], G_full["conf_ln_b"] = dln_w.reshape(1, D), dln_b.reshape(1, D)
    G_full["conf_w_dw"], G_full["conf_b_dw"] = dw_dw[None], db_dw.reshape(1, D)
    G_full["conf_b_pw1"] = jnp.concatenate([dba.reshape(1, D), dbg.reshape(1, D)], axis=1)

    token = start_reduce("l1", [("conf_w_pw2", dw_pw2, 0), ("conf_w_pw1", dw_pw1, 1), ("ffn_w_up1", gf1["w_up"], 1),
                                ("ffn_w_down1", gf1["w_down"], 0)], dw_pw2)
    dh2 = _tie("tie_reduce_l1", dh2, token)
    dh1, gf0 = ffn_bwd(0, h1, (a1, hh0, gc0, act0, dn0), g2_0, dh2, "l0")
    G_full["ffn_conv_w"] = jnp.stack([gf0["conv_w"], gf1["conv_w"]])
    G_full["ffn_conv_b"] = jnp.stack([gf0["conv_b"], gf1["conv_b"]])

    (dmix,), (dg1_0,) = _rw_bwd("l0_res1_bwd", _f_gate, [mix0], [g1_0], [dh1],
                                row_grad=[True], par_grad=[True], row_dtypes=[BF16])
    dyn = _mm(dmix, w_out, tb=True, name="l0_w_out_dx")
    dw_out = _mm(yn, dmix, ta=True, name="l0_w_out_dw", out_dtype=BF16)
    token = start_reduce("l0", [("ffn_w_up0", gf0["w_up"], 1), ("ffn_w_down0", gf0["w_down"], 0),
                                ("ssd_w_out", dw_out, 0)], dw_out)
    dyn = _tie("tie_reduce_l0", dyn, token)
    (dy_lat, dxs_gate, dz_lat), (dd_rep, dssd_nw) = _rw_bwd(
        "l0_ssd_gate_bwd", _f_ssd_gate, gate_rows, [d_rep, ssd_nw], [dyn],
        row_grad=[True, False, True, True], par_grad=[True, True], T=L, row_dtypes=[F32, F32, BF16])
    g_f, g_b = _ssd_bwd(xbc, DI, DI + G * N, dtr, a_neg, s_enter, dy_lat, P, ncc)
    silu_bwd = functools.partial(_rw_bwd, f=_silu, pars=[], row_grad=[True], par_grad=[], T=T0)
    (dxs_pre,), _ = silu_bwd("l0_silu_bwd_x", rows=[(xbc_pre, 0, DI)], cot_fn=lambda p, q, r: p + q + r,
                             cots=[g_f[0], g_b[0], (dxs_gate, 0, DI, -Lc)],
                             col_tile=_tile(DI, 1024))
    (db_pre,), _ = silu_bwd("l0_silu_bwd_b", rows=[(xbc_pre, DI, G * N)], cot_fn=lambda p, q: p + q,
                            cots=[g_f[1], g_b[1]], col_tile=_tile(G * N, 1024))
    (dc_pre,), _ = silu_bwd("l0_silu_bwd_c", rows=[(xbc_pre, DI + G * N, G * N)], cot_fn=lambda p, q: p + q,
                            cots=[g_f[2], g_b[2]], col_tile=_tile(G * N, 1024))
    conv_w0 = Wf["ssd_conv_w"][0]
    pieces = []
    for tag, off, width, g_pre in (("x", 0, DI, dxs_pre), ("b", DI, G * N, db_pre), ("c", DI + G * N, G * N, dc_pre)):
        pieces.append(_conv_bwd("l0_conv_bwd_" + tag, proj, DI + off, width, conv_w0[:, off:off + width], g_pre,
                                seg_taps, du_dtype=BF16))
    dconv_in = [p[0] for p in pieces]
    dcw0 = jnp.concatenate([p[1] for p in pieces], axis=1)
    dcb0 = jnp.concatenate([p[2] for p in pieces], axis=1)
    ddt = jnp.concatenate([g_f[3][:, 0, :].T, g_b[3][:, 0, :].T], axis=1)
    (ddt_raw,), (ddt_bias,) = _rw_bwd("l0_softplus_bwd", _f_softplus, [dt_raw], [dt_bias], [ddt],
                                      row_grad=[True], par_grad=[True], row_dtypes=[BF16])
    dproj = jnp.concatenate([jnp.pad(dz_lat, ((Lc, 0), (0, 0))), *dconv_in, ddt_raw], axis=1)
    da0 = _mm(dproj, w_in, tb=True, name="l0_w_in_dx")
    dw_in = _mm(a0, dproj, ta=True, name="l0_w_in_dw", out_dtype=BF16)
    token = start_reduce("in", [("ssd_w_in", dw_in, 1)], dw_in)
    da0 = _tie("tie_reduce_in", da0, token)
    (dhcat,), (dn1w0, dsc_seg, dsh_seg) = _rw_bwd(
        "l0_modnorm1_bwd", _f_modnorm, [], [n1w0, sc_seg, sh_seg], [da0], T=T0, head=rows0,
        row_grad=[True], par_grad=[True, True, True], seg_rows=(Lc,), add=(dh1, 0, D, -Lc))
    grad_x = dhcat[Lc:][None]

    da_heads = jnp.stack([g[4][:, 0, 0].reshape(G, T0 // SSD_CHUNK, H // G).sum(axis=1).reshape(H)
                          for g in (g_f, g_b)])[None]
    G_full["ssd_a_log"] = da_heads * (-jnp.exp(ssd_a_log))
    G_full["ssd_dt_bias"] = ddt_bias.reshape(1, 2, H)
    G_full["ssd_d"] = dd_rep.reshape(H, P).sum(axis=1)[None]
    G_full["ssd_norm_w"] = dssd_nw.reshape(1, DI)
    G_full["ssd_conv_w"], G_full["ssd_conv_b"] = dcw0[None], dcb0.reshape(1, CD)
    G_full["norm1_w"] = jnp.stack([dn1w0.reshape(D), dn1w1.reshape(D)])
    G_full["norm2_w"] = jnp.stack([gf0["n2w"], gf1["n2w"]])

    zD = jnp.zeros((D,), F32)
    dm_own = jnp.stack([
        jnp.concatenate([dsh_seg[1, 0], dsc_seg[1, 0], dg1_0.reshape(D), gf0["sh2"], gf0["sc2"], gf0["g2"]]),
        jnp.concatenate([dsh1_1.reshape(D), dsc1_1.reshape(D), dg1_1.reshape(D), gf1["sh2"], gf1["sc2"], gf1["g2"]]),
    ])
    dmc_own = jnp.concatenate([dsh_seg[0, 0], dsc_seg[0, 0], zD, zD, zD, zD])

    out = {}

    def finish_reduce(tags, after, swap_name):
        partial = {}
        for tag in tags:
            names, handle = reduces[tag]
            blocks, landed = _exchange4_wait("reduce_" + tag + "_wait", handle, after)
            for n, blk, own in zip(names, landed, blocks):
                r = _fill_own(blk, own, chip, False)
                partial[n] = _sum_leading("sum4_" + n, r.reshape(N_CHIPS, -1, r.shape[-1]),
                                          (0, 1, 2, 3)).reshape(r.shape[1:])
        for n in ("ffn_w_up", "ffn_w_down"):
            if n + "0" in partial:
                partial[n] = jnp.stack([partial.pop(n + "0"), partial.pop(n + "1")])
        names = [n for n in BIG if n in partial]
        mine = [partial[n].reshape(W[n].shape) for n in names]
        for n, own, sib in zip(names, mine, _swap_sibling(swap_name, mine)):
            out[n] = _adamw("adamw_" + n, W[n], Mo[n], Vo[n], own, sib)
        return names

    early = finish_reduce(["l1", "l0"], dhcat, "swap_grads_early")

    small_sum_names = [n for n in SMALL if n not in ("c_ctx", "mod_b")]
    sum_part = [G_full[n] for n in small_sum_names] + [dmc_own]
    packed = _tie("tie_small_grads", _pack(sum_part + [dm_own]), out[early[-1]][1])
    gat = _allgather8("gather_small_grads", packed)
    total = _sum_leading("sum_small_grads", gat, tuple(range(N_DEV)))
    summed = _unpack(total, [a.shape for a in sum_part])
    Gs = dict(zip(small_sum_names, summed[:-1]))
    dmc_tot = summed[-1]
    dm_all = _unpack(gat, [a.shape for a in sum_part] + [dm_own.shape])[-1].transpose(1, 0, 2)
    dm16 = jnp.concatenate([dm_all, jnp.stack([dmc_tot, jnp.zeros_like(dmc_tot)])[:, None, :],
                            jnp.zeros((2, 16 - N_DEV - 1, 6 * D), F32)], axis=1)
    Gs["mod_b"] = _sum_leading("sum_mod_b", dm16.transpose(1, 0, 2).reshape(16, 2 * 6 * D // LANE, LANE),
                               tuple(range(N_DEV + 1))).reshape(2, 6 * D)

    dm16_shard = lax.dynamic_slice_in_dim(dm16, chip * S_mod, S_mod, axis=2)
    ds16 = _mm(dm16_shard[0], mod_w[0], tb=True, precision=HIGHEST, name="c_ctx_dx")
    sig = jax.nn.sigmoid(c_ctx)
    dcc_part = ds16[N_DEV] * (sig * (1.0 + c_ctx * (1.0 - sig)))
    gat_cc = _allgather8("gather_c_ctx_grad", _pack([dcc_part]))
    Gs["c_ctx"] = _sum_leading("sum_c_ctx_grad", gat_cc, (0, 2, 4, 6)).reshape(-1)[:D]

    s16t = _silu(c16).T
    out["mod_w"] = _mod_w_update(s16t, dm16_shard, mod_w, m_mod_w, v_mod_w)
    finish_reduce(["in"], out["mod_w"][0], "swap_grads_late")

    def own(n, full):
        if n in SHARD_AXIS:
            size = W[n].shape[SHARD_AXIS[n]]
            return lax.dynamic_slice_in_dim(full, chip * size, size, axis=SHARD_AXIS[n])
        return full

    def two_d(a):
        return a.reshape(1, -1) if a.ndim == 1 else a

    g_small = [own(n, Gs[n].reshape(Wf[n].shape)) for n in SMALL]
    res = _adamw_many("adamw_small", [two_d(W[n]) for n in SMALL], [two_d(Mo[n]) for n in SMALL],
                      [two_d(Vo[n]) for n in SMALL], [two_d(g) for g in g_small])
    for n, r in zip(SMALL, res):
        out[n] = tuple(t.reshape(W[n].shape) for t in r)

    grads = [out[n][0] for n in WEIGHTS]
    deltas = [out[n][1] for n in WEIGHTS]
    new_m = [out[n][2] for n in WEIGHTS]
    new_v = [out[n][3] for n in WEIGHTS]
    return (loss, grad_x, *grads, *deltas, *new_m, *new_v)
```

```python
import functools

import jax
import jax.numpy as jnp
from jax import lax
from jax.experimental import pallas as pl
from jax.experimental.pallas import tpu as pltpu

F32 = jnp.float32
BF16 = jnp.bfloat16
MESH = pl.DeviceIdType.MESH
HIGHEST = lax.Precision.HIGHEST

VMEM_LIMIT_BYTES = 48 * 1024 * 1024
LANE = 128
SUBLANE = 8

SSD_STATE = 128
SSD_CHUNK = 128
GRID_W = 64
EPS = 1e-6
N_CHIPS = 4
N_DEV = 8

ADAM_LR = 0.001
ADAM_B1 = 0.9
ADAM_B2 = 0.999
ADAM_EPS = 1e-08
ADAM_WD = 0.01
ADAM_STEP = 10


def _pcall(body, **kw):
    return pl.pallas_call(body, **kw)


def _cparams(n_grid):
    return pltpu.CompilerParams(dimension_semantics=("arbitrary",) * n_grid, vmem_limit_bytes=VMEM_LIMIT_BYTES)


def _cdiv(a, b):
    return -(-a // b)


def _round_up(a, b):
    return _cdiv(a, b) * b


def _tile(n, cap):
    if n <= cap:
        return n
    best = None
    for t in range(LANE, cap + 1, LANE):
        if n % t == 0:
            best = t
    if best is None:
        npad = _round_up(n, LANE)
        for t in range(LANE, cap + 1, LANE):
            if npad % t == 0:
                best = t
    return best


def _row_tile(n, cap, also=()):
    best = None
    for step in (2 * SUBLANE, SUBLANE):
        for t in range(step, min(cap, n) + 1, step):
            if n % t == 0 and all(a % t == 0 for a in also):
                best = t
        if best is not None:
            break
    assert best is not None, (n, cap, also)
    return best


def _silu(v):
    return v * jax.nn.sigmoid(v)


def _mm(a, b, *, name, ta=False, tb=False, precision=None, cap=1024, out_dtype=F32, col_blocks=None):
    M, K = (a.shape[1], a.shape[0]) if ta else a.shape
    N = b.shape[0] if tb else b.shape[1]
    assert K == (b.shape[1] if tb else b.shape[0]), (a.shape, b.shape, ta, tb)
    tm, tk = _tile(M, cap), _tile(K, cap + cap // 2)
    tn = _tile(N if col_blocks is None else N // col_blocks, cap + cap // 2)
    nm, nn, nk = _cdiv(M, tm), _cdiv(N, tn), _cdiv(K, tk)
    k_tail = K % tk
    exact = precision is not None

    def body(a_ref, b_ref, o_ref, acc_ref):
        k = pl.program_id(2)

        @pl.when(k == 0)
        def _():
            acc_ref[...] = jnp.zeros_like(acc_ref)

        av = a_ref[...]
        bv = b_ref[...]
        if k_tail:
            lim = K - k * tk
            ka = lax.broadcasted_iota(jnp.int32, av.shape, 0 if ta else 1)
            kb = lax.broadcasted_iota(jnp.int32, bv.shape, 1 if tb else 0)
            av = jnp.where(ka < lim, av, jnp.zeros_like(av))
            bv = jnp.where(kb < lim, bv, jnp.zeros_like(bv))
        if exact:
            av = av.astype(F32)
            bv = bv.astype(F32)
        else:
            av = av.astype(BF16)
            bv = bv.astype(BF16)
        dn = (((0 if ta else 1,), (1 if tb else 0,)), ((), ()))
        acc_ref[...] += lax.dot_general(av, bv, dn, preferred_element_type=F32, precision=precision)

        @pl.when(k == nk - 1)
        def _():
            o_ref[...] = acc_ref[...].astype(o_ref.dtype)

    a_spec = pl.BlockSpec((tk, tm), lambda i, j, k: (k, i)) if ta else pl.BlockSpec((tm, tk), lambda i, j, k: (i, k))
    b_spec = pl.BlockSpec((tn, tk), lambda i, j, k: (j, k)) if tb else pl.BlockSpec((tk, tn), lambda i, j, k: (k, j))
    if col_blocks is None:
        out_spec = pl.BlockSpec((tm, tn), lambda i, j, k: (i, j))
        out_shape = jax.ShapeDtypeStruct((M, N), out_dtype)
    else:
        per = (N // col_blocks) // tn
        assert per * tn * col_blocks == N, (N, col_blocks, tn)
        out_spec = pl.BlockSpec((None, tm, tn), lambda i, j, k: (j // per, i, j % per))
        out_shape = jax.ShapeDtypeStruct((col_blocks, M, N // col_blocks), out_dtype)
    return _pcall(
        body, name=name, grid=(nm, nn, nk), in_specs=[a_spec, b_spec], out_specs=out_spec, out_shape=out_shape,
        scratch_shapes=[pltpu.VMEM((tm, tn), F32)], compiler_params=_cparams(3),
    )(a, b)


def _norm_rows(rows):
    out = []
    for r in rows:
        if not isinstance(r, tuple):
            r = (r,)
        arr, off, width, roff = (r + (0, None, 0)[len(r) - 1:])
        out.append((arr, off, width if width is not None else arr.shape[1], roff))
    return out


def _rw_plan(T, rows, pars, seg_rows, col_tile, tm_cap):
    widths = [r[2] for r in rows]
    wmax = max(widths + [p.shape[-1] for p in pars] + [1])
    if col_tile is not None:
        assert all(w == widths[0] for w in widths) and all(p.shape[-1] == widths[0] for p in pars)
        ncol = widths[0] // col_tile
        assert ncol * col_tile == widths[0]
        wmax = col_tile
    else:
        ncol = 1
    cap = tm_cap if tm_cap is not None else max(SUBLANE, min(256, (256 * 1024) // wmax))
    tm = _row_tile(T, cap, also=tuple(seg_rows) + tuple(abs(r[3]) for r in rows if r[3]))
    bounds = tuple(s // tm for s in seg_rows)
    return widths, ncol, tm, bounds


def _rw_specs(rows, pars, ncol, tm, bounds, col_tile):
    def seg(i):
        s = 0
        for b in bounds:
            s = s + (i >= b).astype(jnp.int32)
        return s

    specs = []
    for arr, off, w, roff in rows:
        bw = col_tile if col_tile is not None else w
        assert off % bw == 0 and roff % tm == 0, (off, bw, roff, tm)
        specs.append(pl.BlockSpec((tm, bw), functools.partial(
            lambda j, i, ob, rb, last: (jnp.clip(i + rb, 0, last), ob + j),
            ob=off // bw, rb=roff // tm, last=arr.shape[0] // tm - 1)))
    for p in pars:
        bw = col_tile if col_tile is not None else p.shape[-1]
        if p.shape[0] > 1:
            specs.append(pl.BlockSpec((None, 1, bw), lambda j, i: (seg(i), 0, j)))
        else:
            specs.append(pl.BlockSpec((None, 1, bw), lambda j, i: (0, 0, j)))
    return specs, seg


def _head_rows(head):
    top, bottom = head
    return [(top, 0, None, 0), (bottom, 0, None, -top.shape[0])]


def _rw_fwd(name, f, rows, pars, out_widths, *, T=None, seg_rows=(), col_tile=None, tm_cap=None, out_dtypes=None,
            head=None):
    rows = _norm_rows((_head_rows(head) if head else []) + list(rows))
    T = rows[0][0].shape[0] if T is None else T
    widths, ncol, tm, bounds = _rw_plan(T, rows, pars, seg_rows, col_tile, tm_cap)
    in_specs, _ = _rw_specs(rows, pars, ncol, tm, bounds, col_tile)
    nr, npar, nout = len(rows), len(pars), len(out_widths)

    def body(*refs):
        vals = [r[...] for r in refs[:nr + npar]]
        if head:
            vals = [jnp.where(pl.program_id(1) < head[0].shape[0] // tm, vals[0], vals[1])] + vals[2:]
        outs = f(*vals)
        if not isinstance(outs, (tuple, list)):
            outs = (outs,)
        for o_ref, o in zip(refs[nr + npar:], outs):
            o_ref[...] = o.astype(o_ref.dtype)

    out_specs = [pl.BlockSpec((tm, col_tile if col_tile is not None else w), lambda j, i: (i, j)) for w in out_widths]
    res = _pcall(
        body, name=name, grid=(ncol, T // tm), in_specs=in_specs, out_specs=out_specs,
        out_shape=[jax.ShapeDtypeStruct((T, w), dt) for w, dt in zip(out_widths, out_dtypes or [F32] * nout)],
        compiler_params=_cparams(2),
    )(*[r[0] for r in rows], *pars)
    return res if nout > 1 else res[0]


def _rw_bwd(name, f, rows, pars, cots, *, row_grad, par_grad, T=None, seg_rows=(), col_tile=None, tm_cap=None,
            add=None, cot_fn=None, row_dtypes=None, head=None):
    rows = _norm_rows((_head_rows(head) if head else []) + list(rows))
    cots = _norm_rows(cots)
    T = rows[0][0].shape[0] if T is None else T
    extra = _norm_rows([add]) if add is not None else []
    all_rows = rows + cots + extra
    widths, ncol, tm, bounds = _rw_plan(T, all_rows, pars, seg_rows, col_tile, tm_cap)
    in_specs, seg = _rw_specs(all_rows, pars, ncol, tm, bounds, col_tile)
    nr, nc, ne, npar = len(rows), len(cots), len(extra), len(pars)
    skip = 1 if head else 0
    widths = widths[skip:]
    nrf = nr - skip
    row_idx = [k for k in range(nrf) if row_grad[k]]
    par_idx = [k for k in range(npar) if par_grad[k]]

    def body(*refs):
        i = pl.program_id(1)

        def zero_before(vals, ops):
            return [jnp.where(i + c[3] // tm >= 0, v, jnp.zeros_like(v)) if c[3] < 0 else v for v, c in zip(vals, ops)]

        row_vals = [r[...] for r in refs[:nr]]
        if head:
            row_vals = [jnp.where(i < head[0].shape[0] // tm, row_vals[0], row_vals[1])] + row_vals[2:]
        cot_vals = zero_before([r[...] for r in refs[nr:nr + nc]], cots)
        add_vals = zero_before([r[...] for r in refs[nr + nc:nr + nc + ne]], extra)
        par_vals = [r[...] for r in refs[nr + nc + ne:nr + nc + ne + npar]]
        out_refs = refs[nr + nc + ne + npar:]
        outs, vjp = jax.vjp(f, *row_vals, *par_vals)
        if cot_fn is not None:
            cot_vals = cot_fn(*cot_vals)
            if not isinstance(cot_vals, (tuple, list)):
                cot_vals = (cot_vals,)
        if isinstance(outs, (tuple, list)):
            grads = vjp(tuple(c.astype(o.dtype) for c, o in zip(cot_vals, outs)))
        else:
            grads = vjp(cot_vals[0].astype(outs.dtype))
        first_seg = i == 0
        for b in bounds:
            first_seg = first_seg | (i == b)
        for n, k in enumerate(row_idx):
            g = grads[k]
            if n == 0 and add_vals:
                g = g + add_vals[0]
            out_refs[n][...] = g.astype(out_refs[n].dtype)
        for n, k in enumerate(par_idx):
            g = grads[nrf + k]
            o_ref = out_refs[len(row_idx) + n]
            first = first_seg if pars[k].shape[0] > 1 else (i == 0)

            @pl.when(first)
            def _(o_ref=o_ref, g=g):
                o_ref[...] = g

            @pl.when(jnp.logical_not(first))
            def _(o_ref=o_ref, g=g):
                o_ref[...] += g

    out_specs, out_shape = [], []
    for k in row_idx:
        w = widths[k]
        out_specs.append(pl.BlockSpec((tm, col_tile if col_tile is not None else w), lambda j, i: (i, j)))
        out_shape.append(jax.ShapeDtypeStruct((T, w), row_dtypes[len(out_shape)] if row_dtypes else F32))
    for k in par_idx:
        p = pars[k]
        bw = col_tile if col_tile is not None else p.shape[-1]
        if p.shape[0] > 1:
            out_specs.append(pl.BlockSpec((None, 1, bw), lambda j, i: (seg(i), 0, j)))
        else:
            out_specs.append(pl.BlockSpec((None, 1, bw), lambda j, i: (0, 0, j)))
        out_shape.append(jax.ShapeDtypeStruct(p.shape, F32))
    res = _pcall(
        body, name=name, grid=(ncol, T // tm), in_specs=in_specs, out_specs=out_specs, out_shape=out_shape,
        compiler_params=_cparams(2),
    )(*[r[0] for r in all_rows], *pars)
    return list(res[:len(row_idx)]), list(res[len(row_idx):])


def _f_modnorm(h, w, sc, sh):
    y = h * lax.rsqrt(jnp.mean(h * h, axis=-1, keepdims=True) + EPS)
    return (y * w) * (1.0 + sc) + sh


def _f_gate_res(h, y, g):
    return h + g * y


def _f_gate_res_bias(h, y, g, b):
    return h + g * (y + b)


def _f_gate(y, g):
    return g * y


def _f_gate_bias(y, g, b):
    return g * (y + b)


def _f_ffn_act(val, gate):
    return _silu(gate) * val


def _f_softplus(raw, bias):
    v = raw + bias
    return jnp.maximum(v, 0.0) + jnp.log(1.0 + jnp.exp(-jnp.abs(v)))


def _f_ssd_gate(yf, yb, xs, z, d_rep, nw):
    y = (yf + yb + d_rep * xs) * _silu(z)
    return (y * lax.rsqrt(jnp.mean(y * y, axis=-1, keepdims=True) + EPS)) * nw


def _f_glu(a, g, ba, bg):
    return (a + ba) * jax.nn.sigmoid(g + bg)


def _f_ln_silu(h, w, b):
    mu = jnp.mean(h, axis=-1, keepdims=True)
    d = h - mu
    y = d * lax.rsqrt(jnp.mean(d * d, axis=-1, keepdims=True) + EPS)
    return _silu(y * w + b)


def _f_loss_rows(h, t, w):
    y = (h * lax.rsqrt(jnp.mean(h * h, axis=-1, keepdims=True) + EPS)) * w
    e = y - t
    return 0.5 * jnp.mean(e * e, axis=-1, keepdims=True)


def _f_adamw(w, m, v, ga, gb):
    g = ga + gb
    m = ADAM_B1 * m + (1.0 - ADAM_B1) * g
    v = ADAM_B2 * v + (1.0 - ADAM_B2) * (g * g)
    m_hat = m / (1.0 - ADAM_B1 ** ADAM_STEP)
    v_hat = v / (1.0 - ADAM_B2 ** ADAM_STEP)
    delta = -ADAM_LR * (m_hat / (jnp.sqrt(v_hat) + ADAM_EPS) + ADAM_WD * w)
    return g, delta, m, v


def _adamw(name, w, m, v, ga, gb):
    shape = w.shape
    c = shape[-1]
    two_d = [t.reshape(-1, c) for t in (w, m, v, ga, gb)]
    rows = two_d[0].shape[0]
    pad = _round_up(rows, SUBLANE) - rows
    if pad:
        two_d = [jnp.pad(t, ((0, pad), (0, 0))) for t in two_d]
    outs = _rw_fwd(name, _f_adamw, two_d, [], [c] * 4)
    return tuple(o[:rows].reshape(shape) for o in outs)


def _sum_leading(name, x, idxs):
    _, R, C = x.shape
    tm = _row_tile(R, max(SUBLANE, min(512, (512 * 1024) // C)))

    def body(x_ref, o_ref):
        acc = x_ref[idxs[0]].astype(F32)
        for k in idxs[1:]:
            acc = acc + x_ref[k].astype(F32)
        o_ref[...] = acc

    return _pcall(
        body, name=name, grid=(R // tm,), in_specs=[pl.BlockSpec((x.shape[0], tm, C), lambda i: (0, i, 0))],
        out_specs=pl.BlockSpec((tm, C), lambda i: (i, 0)), out_shape=jax.ShapeDtypeStruct((R, C), F32),
        compiler_params=_cparams(1),
    )(x)


def _loss_fwd(h, t, w):
    T, D = h.shape
    tm = _row_tile(T, 256)

    def body(h_ref, t_ref, w_ref, o_ref):
        i = pl.program_id(0)
        part = jnp.sum(_f_loss_rows(h_ref[...], t_ref[...], w_ref[...]), axis=0, keepdims=True)
        part = jnp.broadcast_to(part, (1, LANE))

        @pl.when(i == 0)
        def _():
            o_ref[...] = part

        @pl.when(i > 0)
        def _():
            o_ref[...] += part

    return _pcall(
        body, name="loss_fwd", grid=(T // tm,),
        in_specs=[pl.BlockSpec((tm, D), lambda i: (i, 0)), pl.BlockSpec((tm, D), lambda i: (i, 0)),
                  pl.BlockSpec((1, D), lambda i: (0, 0))],
        out_specs=pl.BlockSpec((1, LANE), lambda i: (0, 0)), out_shape=jax.ShapeDtypeStruct((1, LANE), F32),
        compiler_params=_cparams(1),
    )(h, t, w)


CONV_ROWS = 256
CONV_ROWS_FEW_TAPS = 1024
CONV_ACC_ELEMS = 16384


def _col_mask(arg, t):
    col = jnp.bitwise_and(t, GRID_W - 1)
    return (col != 0) if arg < 0 else (col != GRID_W - 1)


def _conv_plan(T, C, taps):
    seg = [m[1] for _, m in taps if m is not None and m[0] == "seg"]
    cap = CONV_ROWS_FEW_TAPS if len(taps) <= 9 else CONV_ROWS
    rc = next(r for r in (1024, 768, 512, 256, LANE) if r <= cap and T % r == 0)
    ct = next((t for t in (512, 256, LANE) if C % t == 0), C)
    reach = max(abs(s) for s, _ in taps)
    hb = next(h for h in (8, 16, 32, 64, 128, 256) if h >= reach and rc % h == 0)
    sub = max(2 * SUBLANE, min(rc, CONV_ACC_ELEMS // ct))
    boundary = None
    if seg:
        inside = seg[0] % rc
        boundary = (seg[0], (inside - reach, inside + reach) if inside else None)
    taps = [(s, None if (m is None or m[0] == "seg") else m[1]) for s, m in taps]
    return rc, ct, hb, sub, T // rc, C // ct, boundary, taps


def _seg_ok(boundary, i, rc, r0, n, s):
    if boundary is None or boundary[1] is None or s == 0 or r0 + n <= boundary[1][0] or r0 >= boundary[1][1]:
        return None
    t = i * rc + r0 + lax.broadcasted_iota(jnp.int32, (n, 1), 0)
    return (t >= boundary[0]) == ((t + s) >= boundary[0])


def _halo_specs(rc, ct, hb, T, off_blocks):
    per = rc // hb
    last = T // hb - 1
    prev = pl.BlockSpec((hb, ct), lambda j, i: (jnp.maximum(i * per - 1, 0), off_blocks + j))
    cur = pl.BlockSpec((rc, ct), lambda j, i: (i, off_blocks + j))
    nxt = pl.BlockSpec((hb, ct), lambda j, i: (jnp.minimum((i + 1) * per, last), off_blocks + j))
    return [prev, cur, nxt]


def _fill_halo(pad_ref, p_ref, c_ref, n_ref, i, nrc, rc, hb, boundary):
    has_prev = i > 0
    has_next = i < nrc - 1
    if boundary is not None:
        has_prev = has_prev & (i * rc != boundary[0])
        has_next = has_next & ((i + 1) * rc != boundary[0])
    pad_ref[0:hb, :] = jnp.where(has_prev, p_ref[...], 0.0)
    pad_ref[hb:hb + rc, :] = c_ref[...]
    pad_ref[hb + rc:hb + rc + hb, :] = jnp.where(has_next, n_ref[...], 0.0)


def _shift_plan(keys):
    count = {}
    for s, m in keys:
        k = (s % SUBLANE, m)
        count[k] = count.get(k, 0) + 1
    slots = {}
    for k, n in sorted(count.items(), key=lambda kv: (kv[0][0], str(kv[0][1]))):
        if k != (0, None) and (n >= 2 or k[1] is not None):
            slots[k] = len(slots)
    return slots


def _build_shifted(copies_ref, slots, pad_ref, keys, i, rc, hb, sub):
    for (r, m), slot in slots.items():
        qs = [s - r for s, mk in keys if (s % SUBLANE, mk) == (r, m)]
        lo, hi = hb + min(qs), hb + rc + max(qs)
        for p in range(lo, hi, sub):
            n = min(sub, hi - p)
            v = pad_ref[p + r:p + r + n, :]
            if m is not None:
                t = i * rc - hb + p + r + lax.broadcasted_iota(jnp.int32, (n, 1), 0)
                v = jnp.where(_col_mask(m, t), v, 0.0)
            copies_ref[slot, p:p + n, :] = v


def _read(copies_ref, slots, pad_ref, s, m, row, n):
    k = (s % SUBLANE, m)
    if k in slots:
        q = s - k[0]
        return copies_ref[slots[k], row + q:row + q + n, :]
    return pad_ref[row + s:row + s + n, :]


def _conv_fwd(name, u, col_off, C, w, b, taps, act=False):
    T = u.shape[0]
    rc, ct, hb, sub, nrc, ncc, boundary, taps = _conv_plan(T, C, taps)
    assert col_off % ct == 0
    K = len(taps)
    keys = [(s, None) for s, _ in taps]
    slots = _shift_plan(keys)
    dirs = sorted({m for _, m in taps if m is not None})

    def body(up, uc, un, w_ref, b_ref, *rest):
        y_ref = rest[0]
        pad_ref, copies_ref = rest[-2], rest[-1]
        i = pl.program_id(1)
        _fill_halo(pad_ref, up, uc, un, i, nrc, rc, hb, boundary)
        _build_shifted(copies_ref, slots, pad_ref, keys, i, rc, hb, sub)
        for r0 in range(0, rc, sub):
            acc = jnp.broadcast_to(b_ref[...], (sub, ct))
            for m in [None] + dirs:
                part = None
                for k, (s, mk) in enumerate(taps):
                    if mk != m:
                        continue
                    v = _read(copies_ref, slots, pad_ref, s, None, hb + r0, sub)
                    ok = _seg_ok(boundary, i, rc, r0, sub, s)
                    term = w_ref[k:k + 1, :] * (v if ok is None else jnp.where(ok, v, 0.0))
                    part = term if part is None else part + term
                if part is None:
                    continue
                if m is not None:
                    t = i * rc + r0 + lax.broadcasted_iota(jnp.int32, (sub, 1), 0)
                    part = jnp.where(_col_mask(m, t), part, 0.0)
                acc = acc + part
            y_ref[r0:r0 + sub, :] = acc
            if act:
                rest[1][r0:r0 + sub, :] = _silu(acc)

    n_out = 2 if act else 1
    res = _pcall(
        body, name=name, grid=(ncc, nrc),
        in_specs=_halo_specs(rc, ct, hb, T, col_off // ct) + [pl.BlockSpec((K, ct), lambda j, i: (0, j)),
                                                              pl.BlockSpec((1, ct), lambda j, i: (0, j))],
        out_specs=[pl.BlockSpec((rc, ct), lambda j, i: (i, j))] * n_out,
        out_shape=[jax.ShapeDtypeStruct((T, C), F32)] * n_out,
        scratch_shapes=[pltpu.VMEM((rc + 2 * hb, ct), F32), pltpu.VMEM((max(len(slots), 1), rc + 2 * hb, ct), F32)],
        compiler_params=_cparams(2),
    )(u, u, u, w, b)
    return res if act else res[0]


def _conv_bwd(name, u, col_off, C, w, g, taps, du_dtype=F32):
    T = u.shape[0]
    rc, ct, hb, sub, nrc, ncc, boundary, taps = _conv_plan(T, C, taps)
    K = len(taps)
    u_keys = [(s, None) for s, _ in taps]
    dirs = sorted({m for _, m in taps if m is not None})
    g_keys = [(-s, m) for s, m in taps] + [(0, m) for m in dirs]
    u_slots, g_slots = _shift_plan(u_keys), _shift_plan(g_keys)

    def body(up, uc, un, gp, gc, gn, w_ref, du_ref, dw_ref, db_ref, upad, gpad, ucopies, gcopies):
        i = pl.program_id(1)
        _fill_halo(upad, up, uc, un, i, nrc, rc, hb, boundary)
        _fill_halo(gpad, gp, gc, gn, i, nrc, rc, hb, boundary)
        _build_shifted(ucopies, u_slots, upad, u_keys, i, rc, hb, sub)
        _build_shifted(gcopies, g_slots, gpad, g_keys, i, rc, hb, sub)

        @pl.when(i == 0)
        def _():
            dw_ref[...] = jnp.zeros_like(dw_ref)
            db_ref[...] = jnp.zeros_like(db_ref)

        def fold(v):
            return jnp.sum(v.reshape(sub // SUBLANE, SUBLANE, ct), axis=0)

        dbs = jnp.zeros((SUBLANE, ct), F32)
        for r0 in range(0, rc, sub):
            dbs = dbs + fold(gpad[hb + r0:hb + r0 + sub, :])
            acc = jnp.zeros((sub, ct), F32)
            for k, (s, m) in enumerate(taps):
                v = _read(gcopies, g_slots, gpad, -s, m, hb + r0, sub)
                ok = _seg_ok(boundary, i, rc, r0, sub, -s)
                acc = acc + w_ref[k:k + 1, :] * (v if ok is None else jnp.where(ok, v, 0.0))
            du_ref[r0:r0 + sub, :] = acc.astype(du_ref.dtype)
        db_ref[...] += jnp.sum(dbs, axis=0, keepdims=True)
        for k, (s, m) in enumerate(taps):
            part = jnp.zeros((SUBLANE, ct), F32)
            for r0 in range(0, rc, sub):
                v = _read(ucopies, u_slots, upad, s, None, hb + r0, sub)
                ok = _seg_ok(boundary, i, rc, r0, sub, s)
                part = part + fold(_read(gcopies, g_slots, gpad, 0, m, hb + r0, sub)
                                   * (v if ok is None else jnp.where(ok, v, 0.0)))
            dw_ref[k:k + 1, :] += jnp.sum(part, axis=0, keepdims=True)

    halo_u = _halo_specs(rc, ct, hb, T, col_off // ct)
    halo_g = _halo_specs(rc, ct, hb, T, 0)
    rows = rc + 2 * hb
    return _pcall(
        body, name=name, grid=(ncc, nrc),
        in_specs=halo_u + halo_g + [pl.BlockSpec((K, ct), lambda j, i: (0, j))],
        out_specs=[pl.BlockSpec((rc, ct), lambda j, i: (i, j)), pl.BlockSpec((K, ct), lambda j, i: (0, j)),
                   pl.BlockSpec((1, ct), lambda j, i: (0, j))],
        out_shape=[jax.ShapeDtypeStruct((T, C), du_dtype), jax.ShapeDtypeStruct((K, C), F32),
                   jax.ShapeDtypeStruct((1, C), F32)],
        scratch_shapes=[pltpu.VMEM((rows, ct), F32), pltpu.VMEM((rows, ct), F32),
                        pltpu.VMEM((max(len(u_slots), 1), rows, ct), F32),
                        pltpu.VMEM((max(len(g_slots), 1), rows, ct), F32)],
        compiler_params=_cparams(2),
    )(u, u, u, g, g, g, w)


def _ssd_group(xg, bm, cm, s_in, *per_head, reverse, P):
    R = len(per_head) // 2
    dtrs, a_s = per_head[:R], per_head[R:]
    q, rp = xg.shape
    ii = lax.broadcasted_iota(jnp.int32, (q, q), 0)
    jj = lax.broadcasted_iota(jnp.int32, (q, q), 1)
    causal = (jj >= ii) if reverse else (jj <= ii)
    causal_t = (ii >= jj) if reverse else (ii <= jj)
    eye = ii == jj
    lane = lax.broadcasted_iota(jnp.int32, (1, rp), 1)
    row = lax.broadcasted_iota(jnp.int32, (rp, 1), 0)
    nt = (((1,), (1,)), ((), ()))
    tn = (((0,), (0,)), ((), ()))
    cb = lax.dot_general(cm.astype(BF16), bm.astype(BF16), nt, preferred_element_type=F32)
    dt_x = jnp.zeros((q, rp), F32)
    acum_x = jnp.zeros((q, rp), F32)
    tot_row = jnp.zeros((1, rp), F32)
    tot_col = jnp.zeros((rp, 1), F32)
    wts, lane_masks = [], []
    for r in range(R):
        hm = (lane >= r * P) & (lane < (r + 1) * P)
        hc = (row >= r * P) & (row < (r + 1) * P)
        dt_c = jnp.sum(jnp.where(eye, dtrs[r], 0.0), axis=1, keepdims=True)
        dac = dt_c * a_s[r]
        dar = dtrs[r] * a_s[r]
        acum_c = jnp.sum(jnp.where(causal, dar, 0.0), axis=1, keepdims=True)
        acum_r = jnp.sum(jnp.where(causal_t, dac, 0.0), axis=0, keepdims=True)
        decay = jnp.where(causal, jnp.exp(jnp.where(causal, acum_c - acum_r, 0.0)), 0.0)
        tot = jnp.sum(dac, axis=0, keepdims=True)
        dt_x = jnp.where(hm, dt_c, dt_x)
        acum_x = jnp.where(hm, acum_c, acum_x)
        tot_row = jnp.where(hm, tot, tot_row)
        tot_col = jnp.where(hc, tot, tot_col)
        wts.append((cb * decay).astype(BF16))
        lane_masks.append(hm)
    xdt = xg * dt_x
    xdt_b = xdt.astype(BF16)
    y = jnp.zeros((q, rp), F32)
    for r in range(R):
        y = jnp.where(lane_masks[r], jnp.dot(wts[r], xdt_b, preferred_element_type=F32), y)
    dte = jnp.exp(tot_row - acum_x)
    cs = lax.dot_general((xdt * dte).astype(BF16), bm.astype(BF16), tn, preferred_element_type=F32)
    y = y + lax.dot_general(cm.astype(BF16), s_in.astype(BF16), nt, preferred_element_type=F32) * jnp.exp(acum_x)
    s_out = jnp.exp(tot_col) * s_in + cs
    return y, s_out


def _ssd_maps(NC, ncc, reverse_steps):
    def chunk(d, s):
        if reverse_steps:
            s = NC - 1 - s
        return s if d == 0 else jnp.where(s < ncc, ncc - 1 - s, NC - 1 - s + ncc)

    def lat_chunk(d, s):
        c = chunk(d, s) - ncc
        return jnp.where(c < 0, 0 if d == 0 else NC - ncc - 1, c)

    def step(s):
        return NC - 1 - s if reverse_steps else s

    return chunk, lat_chunk, step


def _ssd_specs(chunk, d, R, Q, N, RP, bo, co):
    return [
        pl.BlockSpec((Q, RP), lambda g, s: (chunk(d, s), g)),
        pl.BlockSpec((Q, N), lambda g, s: (chunk(d, s), bo + g)),
        pl.BlockSpec((Q, N), lambda g, s: (chunk(d, s), co + g)),
        pl.BlockSpec((R, 1, Q), lambda g, s: (g, 0, chunk(d, s))),
        pl.BlockSpec((R, 1, 1), lambda g, s: (g, 0, 0)),
    ]


def _ssd_fwd(xbc, b_off, c_off, dtr, a, P, ncc):
    T = xbc.shape[0]
    H = dtr[0].shape[0]
    N, Q = SSD_STATE, SSD_CHUNK
    NC = T // Q
    G = (c_off - b_off) // N
    R = H // G
    RP = R * P
    chunk, lat_chunk, _ = _ssd_maps(NC, ncc, False)

    def body(*refs):
        s = pl.program_id(1)
        s_ref = refs[-1]

        @pl.when(s == 0)
        def _():
            s_ref[...] = jnp.zeros_like(s_ref)

        for d in range(2):
            x_ref, b_ref, c_ref, dtr_ref, a_ref = refs[5 * d:5 * d + 5]
            y_ref, se_ref = refs[10 + 2 * d:12 + 2 * d]
            s_in = s_ref[d]
            se_ref[...] = s_in
            per_head = [dtr_ref[r] for r in range(R)] + [a_ref[r] for r in range(R)]
            y, s_out = _ssd_group(x_ref[...], b_ref[...], c_ref[...], s_in, *per_head, reverse=d == 1, P=P)
            y_ref[...] = y
            s_ref[d] = s_out

    in_specs, out_specs, out_shape, operands = [], [], [], []
    for d in range(2):
        in_specs += _ssd_specs(chunk, d, R, Q, N, RP, b_off // N, c_off // N)
        operands += [xbc, xbc, xbc, dtr[d], a[d]]
        out_specs += [pl.BlockSpec((Q, RP), functools.partial(lambda g, s, d: (lat_chunk(d, s), g), d=d)),
                      pl.BlockSpec((None, None, RP, N), lambda g, s: (g, s, 0, 0))]
        out_shape += [jax.ShapeDtypeStruct((T - ncc * Q, H * P), F32), jax.ShapeDtypeStruct((G, NC, RP, N), F32)]
    y_f, se_f, y_b, se_b = _pcall(
        body, name="ssd_fwd", grid=(G, NC), in_specs=in_specs, out_specs=out_specs, out_shape=out_shape,
        scratch_shapes=[pltpu.VMEM((2, RP, N), F32)], compiler_params=_cparams(2),
    )(*operands)
    return (y_f, y_b), (se_f, se_b)


def _ssd_bwd(xbc, b_off, c_off, dtr, a, s_enter, dy, P, ncc):
    T = xbc.shape[0]
    H = dtr[0].shape[0]
    N, Q = SSD_STATE, SSD_CHUNK
    NC = T // Q
    G = (c_off - b_off) // N
    R = H // G
    RP = R * P
    chunk, lat_chunk, step = _ssd_maps(NC, ncc, True)
    n_in, n_out = 7, 5

    def body(*refs):
        s = pl.program_id(1)
        ds_ref = refs[-1]

        @pl.when(s == 0)
        def _():
            ds_ref[...] = jnp.zeros_like(ds_ref)

        for d in range(2):
            x_ref, b_ref, c_ref, dtr_ref, a_ref, se_ref, dy_ref = refs[n_in * d:n_in * (d + 1)]
            dx_ref, db_ref, dc_ref, ddtr_ref, da_ref = refs[2 * n_in + n_out * d:2 * n_in + n_out * (d + 1)]
            per_head = [dtr_ref[r] for r in range(R)] + [a_ref[r] for r in range(R)]
            f = functools.partial(_ssd_group, reverse=d == 1, P=P)
            _, vjp = jax.vjp(f, x_ref[...], b_ref[...], c_ref[...], se_ref[...], *per_head)
            is_latent = chunk(d, s) >= ncc
            dy_v = jnp.where(is_latent, dy_ref[...], 0.0)
            grads = vjp((dy_v, ds_ref[d]))
            dx_ref[...] = grads[0]
            db_ref[...] = grads[1]
            dc_ref[...] = grads[2]
            ds_ref[d] = grads[3]
            for r in range(R):
                ddtr_ref[r] = grads[4 + r]
                da_ref[r] = jnp.broadcast_to(grads[4 + R + r], (SUBLANE, LANE))

    in_specs, out_specs, out_shape, operands = [], [], [], []
    for d in range(2):
        in_specs += _ssd_specs(chunk, d, R, Q, N, RP, b_off // N, c_off // N) + [
            pl.BlockSpec((None, None, RP, N), lambda g, s: (g, step(s), 0, 0)),
            pl.BlockSpec((Q, RP), functools.partial(lambda g, s, d: (lat_chunk(d, s), g), d=d)),
        ]
        operands += [xbc, xbc, xbc, dtr[d], a[d], s_enter[d], dy]
    for d in range(2):
        at_chunk = functools.partial(lambda g, s, d: (chunk(d, s), g), d=d)
        out_specs += [
            pl.BlockSpec((Q, RP), at_chunk), pl.BlockSpec((Q, N), at_chunk), pl.BlockSpec((Q, N), at_chunk),
            pl.BlockSpec((R, 1, Q), functools.partial(lambda g, s, d: (g, 0, chunk(d, s)), d=d)),
            pl.BlockSpec((R, SUBLANE, LANE), lambda g, s: (g * NC + s, 0, 0)),
        ]
        out_shape += [
            jax.ShapeDtypeStruct((T, H * P), F32), jax.ShapeDtypeStruct((T, G * N), F32),
            jax.ShapeDtypeStruct((T, G * N), F32), jax.ShapeDtypeStruct((H, 1, T), F32),
            jax.ShapeDtypeStruct((G * NC * R, SUBLANE, LANE), F32),
        ]
    res = _pcall(
        body, name="ssd_bwd", grid=(G, NC), in_specs=in_specs, out_specs=out_specs, out_shape=out_shape,
        scratch_shapes=[pltpu.VMEM((2, RP, N), F32)], compiler_params=_cparams(2),
    )(*operands)
    return res[:n_out], res[n_out:]


def _allgather8(name, v):
    R, C = v.shape

    def body(x_ref, out_ref, send_sems, recv_sems, local_sem):
        x, y, c = lax.axis_index("x"), lax.axis_index("y"), lax.axis_index("c")
        me, sibling = (x, y, c), (x, y, 1 - c)
        chips = [(1 - x, y), (x, 1 - y), (1 - x, 1 - y)]

        def slot(px, py, pc):
            return out_ref.at[4 * px + 2 * py + pc]

        def copy(k, block, to, src=None):
            return pltpu.make_async_remote_copy(
                src_ref=slot(*block) if src is None else src, dst_ref=slot(*block),
                send_sem=send_sems.at[k], recv_sem=recv_sems.at[k], device_id=to, device_id_type=MESH)

        mine = pltpu.make_async_copy(x_ref, slot(*me), local_sem)
        mine.start()
        first = [copy(0, me, sibling, src=x_ref)]
        first += [copy(1 + j, me, (*chip, c), src=x_ref) for j, chip in enumerate(chips)]
        for cp in first:
            cp.start()
        passed = [copy(4 + j, (*chip, c), sibling) for j, chip in enumerate(chips)]
        for j, chip in enumerate(chips):
            copy(1 + j, (*chip, c), me).wait_recv()
            passed[j].start()
        copy(0, sibling, me).wait_recv()
        for j, chip in enumerate(chips):
            copy(4 + j, (*chip, 1 - c), me).wait_recv()
        for cp in first + passed:
            cp.wait_send()
        mine.wait()

    return _pcall(
        body, name=name, out_shape=jax.ShapeDtypeStruct((N_DEV, R, C), v.dtype),
        in_specs=[pl.BlockSpec(memory_space=pltpu.VMEM)], out_specs=pl.BlockSpec(memory_space=pltpu.VMEM),
        scratch_shapes=[pltpu.SemaphoreType.DMA((7,)), pltpu.SemaphoreType.DMA((7,)), pltpu.SemaphoreType.DMA],
        compiler_params=pltpu.CompilerParams(vmem_limit_bytes=VMEM_LIMIT_BYTES),
    )(v)


def _slot(ref, k, axis, size):
    if axis is None:
        return ref.at[k]
    align = LANE if size % LANE == 0 else 2 * SUBLANE
    assert size % align == 0
    return ref.at[(slice(None),) * axis + (pl.ds(pl.multiple_of(k * size, align), size),)]


def _exchange4_start(name, srcs, bcast, dep, axes=None, half=False):
    n = len(srcs)
    axes = list(axes) if axes is not None else [None] * n
    sizes = [None if ax is None else s.shape[ax] for s, ax in zip(srcs, axes)]

    def land_shape(s, ax):
        if not bcast:
            return s.shape
        if half:
            return (N_CHIPS,) + s.shape[1:]
        if ax is None:
            return (N_CHIPS,) + s.shape
        return s.shape[:ax] + (N_CHIPS * s.shape[ax],) + s.shape[ax + 1:]

    lands = [lax.empty(land_shape(s, ax), s.dtype) for s, ax in zip(srcs, axes)]

    def body(*refs):
        src, land = refs[:n], refs[n:2 * n]
        send_sems, recv_sems = refs[2 * n + 1], refs[2 * n + 2]
        token = refs[-1]
        x, y, c = lax.axis_index("x"), lax.axis_index("y"), lax.axis_index("c")
        me = 2 * x + y
        for a in range(n):
            for j, (px, py) in enumerate([(1 - x, y), (x, 1 - y), (1 - x, 1 - y)]):
                pltpu.make_async_remote_copy(
                    src_ref=(src[a].at[c] if half else src[a]) if bcast else src[a].at[2 * px + py],
                    dst_ref=_slot(land[a], me, axes[a], sizes[a]),
                    send_sem=send_sems.at[3 * a + j], recv_sem=recv_sems.at[3 * a + j], device_id=(px, py, c),
                    device_id_type=MESH).start()
        token[...] = jnp.zeros_like(token)

    hbm = pl.BlockSpec(memory_space=pltpu.HBM)
    sem = pl.BlockSpec(memory_space=pltpu.SEMAPHORE)
    outs = _pcall(
        body, name=name,
        out_shape=(pltpu.SemaphoreType.DMA((3 * n,)), pltpu.SemaphoreType.DMA((3 * n,)),
                   *[pltpu.HBM(s.shape, s.dtype) for s in srcs], *[pltpu.HBM(l.shape, l.dtype) for l in lands],
                   jax.ShapeDtypeStruct((SUBLANE, LANE), F32)),
        in_specs=[hbm] * (2 * n) + [pl.BlockSpec(memory_space=pl.ANY)],
        out_specs=(sem, sem, *[hbm] * (2 * n), pl.BlockSpec(memory_space=pltpu.VMEM)),
        input_output_aliases={k: 2 + k for k in range(2 * n)},
        compiler_params=pltpu.CompilerParams(has_side_effects=pltpu.SideEffectType.DATAFLOW_SIDE_EFFECTING),
    )(*[pltpu.with_memory_space_constraint(s, pltpu.HBM) for s in srcs],
      *[pltpu.with_memory_space_constraint(l, pltpu.HBM) for l in lands], dep)
    return (n, bcast, half, axes, sizes, outs[0], outs[1], outs[2:2 + n], outs[2 + n:2 + 2 * n]), outs[-1]


def _exchange4_wait(name, handle, after):
    n, bcast, half, axes, sizes, send_sems, recv_sems, src_thru, land_thru = handle

    def body(*refs):
        src, land = refs[:n], refs[n:2 * n]
        send_sems, recv_sems = refs[2 * n], refs[2 * n + 1]
        x, y, c = lax.axis_index("x"), lax.axis_index("y"), lax.axis_index("c")
        for a in range(n):
            for j, (px, py) in enumerate([(1 - x, y), (x, 1 - y), (1 - x, 1 - y)]):
                pk = 2 * px + py
                copy = pltpu.make_async_remote_copy(
                    src_ref=(src[a].at[c] if half else src[a]) if bcast else src[a].at[pk],
                    dst_ref=_slot(land[a], pk, axes[a], sizes[a]),
                    send_sem=send_sems.at[3 * a + j], recv_sem=recv_sems.at[3 * a + j], device_id=(px, py, c),
                    device_id_type=MESH)
                copy.wait_send()
                copy.wait_recv()

    hbm = pl.BlockSpec(memory_space=pltpu.HBM)
    sem = pl.BlockSpec(memory_space=pltpu.SEMAPHORE)
    outs = _pcall(
        body, name=name,
        out_shape=tuple(pltpu.HBM(t.shape, t.dtype) for t in (*src_thru, *land_thru)),
        in_specs=[hbm] * (2 * n) + [sem, sem, pl.BlockSpec(memory_space=pl.ANY)], out_specs=tuple([hbm] * (2 * n)),
        input_output_aliases={k: k for k in range(2 * n)},
        compiler_params=pltpu.CompilerParams(has_side_effects=pltpu.SideEffectType.DATAFLOW_SIDE_EFFECTING),
    )(*src_thru, *land_thru, send_sems, recv_sems, after)
    return list(outs[:n]), list(outs[n:])


def _tie(name, v, token):
    def body(v_ref, token_ref, o_ref):
        del v_ref, token_ref, o_ref

    any_spec = pl.BlockSpec(memory_space=pl.ANY)
    return _pcall(body, name=name, out_shape=jax.ShapeDtypeStruct(v.shape, v.dtype), in_specs=[any_spec, any_spec],
                  out_specs=any_spec, input_output_aliases={0: 0})(v, token)


def _fill_own(landed, own, me, bcast):
    blk = own if bcast else lax.dynamic_index_in_dim(own, me, 0, keepdims=False)
    return lax.dynamic_update_index_in_dim(landed, blk, me, 0)


def _swap_sibling(name, srcs):
    n = len(srcs)

    def body(*refs):
        src, out = refs[:n], refs[n:2 * n]
        send_sems, recv_sems = refs[2 * n:]
        x, y, c = lax.axis_index("x"), lax.axis_index("y"), lax.axis_index("c")
        copies = []
        for a in range(n):
            rc = pltpu.make_async_remote_copy(
                src_ref=src[a], dst_ref=out[a], send_sem=send_sems.at[a], recv_sem=recv_sems.at[a],
                device_id=(x, y, 1 - c), device_id_type=MESH)
            rc.start()
            copies.append(rc)
        for cp in copies:
            cp.wait()

    any_spec = pl.BlockSpec(memory_space=pl.ANY)
    return _pcall(
        body, name=name, out_shape=[jax.ShapeDtypeStruct(s.shape, s.dtype) for s in srcs],
        in_specs=[any_spec] * n, out_specs=[any_spec] * n,
        scratch_shapes=[pltpu.SemaphoreType.DMA((n,)), pltpu.SemaphoreType.DMA((n,))],
    )(*srcs)


def _mod_fwd(c16, mod_w, mod_b_shard):
    nl, D, S = mod_w.shape

    def body(c_ref, w_ref, b_ref, o_ref):
        s = _silu(c_ref[...]).astype(BF16)
        o_ref[...] = jnp.dot(s, w_ref[...].astype(BF16), preferred_element_type=F32) + b_ref[...]

    return _pcall(
        body, name="mod_fwd", grid=(nl,),
        in_specs=[pl.BlockSpec((16, D), lambda l: (0, 0)), pl.BlockSpec((None, D, S), lambda l: (l, 0, 0)),
                  pl.BlockSpec((None, 1, S), lambda l: (l, 0, 0))],
        out_specs=pl.BlockSpec((None, 16, S), lambda l: (l, 0, 0)),
        out_shape=jax.ShapeDtypeStruct((nl, 16, S), F32), compiler_params=_cparams(1),
    )(c16, mod_w, mod_b_shard)


def _mod_w_update(s16t, dm16, w, m, v):
    nl, D, S = w.shape
    tm = _row_tile(D, 256)

    def body(s_ref, dm_ref, w_ref, m_ref, v_ref, g_ref, dl_ref, nm_ref, nv_ref):
        g = jnp.dot(s_ref[...], dm_ref[...], preferred_element_type=F32, precision=HIGHEST)
        g, dl, nm, nv = _f_adamw(w_ref[...], m_ref[...], v_ref[...], g, jnp.zeros_like(g))
        g_ref[...] = g
        dl_ref[...] = dl
        nm_ref[...] = nm
        nv_ref[...] = nv

    big = pl.BlockSpec((None, tm, S), lambda l, i: (l, i, 0))
    return _pcall(
        body, name="mod_w_update", grid=(nl, D // tm),
        in_specs=[pl.BlockSpec((tm, 16), lambda l, i: (i, 0)), pl.BlockSpec((None, 16, S), lambda l, i: (l, 0, 0)),
                  big, big, big],
        out_specs=[big] * 4, out_shape=[jax.ShapeDtypeStruct(w.shape, F32)] * 4, compiler_params=_cparams(2),
    )(s16t, dm16, w, m, v)


def _size(shape):
    n = 1
    for d in shape:
        n *= d
    return n


def _pack(arrs):
    pieces = []
    for a in arrs:
        flat = a.reshape(-1).astype(F32)
        pieces.append(jnp.pad(flat, (0, _round_up(flat.shape[0], LANE) - flat.shape[0])).reshape(-1, LANE))
    buf = jnp.concatenate(pieces, axis=0)
    return jnp.pad(buf, ((0, _round_up(buf.shape[0], SUBLANE) - buf.shape[0]), (0, 0)))


def _unpack(buf, shapes):
    lead = buf.shape[:-2]
    out, row = [], 0
    for s in shapes:
        n = _size(s)
        rows = _cdiv(n, LANE)
        piece = buf[..., row:row + rows, :].reshape(lead + (rows * LANE,))
        out.append(piece[..., :n].reshape(lead + tuple(s)))
        row += rows
    return out


def _adamw_many(name, ws, ms, vs, gs):
    n = len(ws)

    def body(*refs):
        for k in range(n):
            res = _f_adamw(refs[k][...], refs[n + k][...], refs[2 * n + k][...], refs[3 * n + k][...], 0.0)
            for j in range(4):
                refs[(4 + j) * n + k][...] = res[j]

    vmem = pl.BlockSpec(memory_space=pltpu.VMEM)
    res = _pcall(body, name=name, out_shape=[jax.ShapeDtypeStruct(w.shape, F32) for _ in range(4) for w in ws],
                 in_specs=[vmem] * (4 * n), out_specs=[vmem] * (4 * n))(*ws, *ms, *vs, *gs)
    return [tuple(res[j * n + k] for j in range(4)) for k in range(n)]


SHARD_AXIS = {
    "mod_w": 2, "ssd_w_in": 2, "ssd_conv_w": 2, "ssd_w_out": 1, "conf_w_pw1": 2, "conf_b_pw1": 1, "conf_w_dw": 2,
    "conf_b_dw": 1, "conf_ln_w": 1, "conf_ln_b": 1, "conf_w_pw2": 1, "conf_b_pw2": 1, "ffn_w_up": 2,
    "ffn_conv_w": 3, "ffn_w_down": 1,
}
BIG = ("ssd_w_in", "ssd_w_out", "conf_w_pw1", "conf_w_pw2", "ffn_w_up", "ffn_w_down")
WEIGHTS = ("c_ctx", "mod_w", "mod_b", "norm1_w", "norm2_w", "ssd_w_in", "ssd_conv_w", "ssd_conv_b", "ssd_dt_bias",
           "ssd_a_log", "ssd_d", "ssd_norm_w", "ssd_w_out", "conf_w_pw1", "conf_b_pw1", "conf_w_dw", "conf_b_dw",
           "conf_ln_w", "conf_ln_b", "conf_w_pw2", "conf_b_pw2", "ffn_w_up", "ffn_conv_w", "ffn_conv_b",
           "ffn_w_down", "final_norm_w")
SMALL = tuple(n for n in WEIGHTS if n not in BIG and n != "mod_w")
SMALL_SHARDED = tuple(n for n in SMALL if n in SHARD_AXIS)


def _unshard(stacked, axis):
    return jnp.concatenate([stacked[k] for k in range(N_CHIPS)], axis=axis)


def _to_blocks(full, axis):
    return jnp.stack(jnp.split(full, N_CHIPS, axis=axis))


def _par(v):
    v = v.reshape(-1, v.shape[-1])
    return v[:, None, :]


def kernel(x, c, ctx, c_ctx, mod_w, mod_b, norm1_w, norm2_w, ssd_w_in, ssd_conv_w, ssd_conv_b, ssd_dt_bias, ssd_a_log, ssd_d, ssd_norm_w, ssd_w_out, conf_w_pw1, conf_b_pw1, conf_w_dw, conf_b_dw, conf_ln_w, conf_ln_b, conf_w_pw2, conf_b_pw2, ffn_w_up, ffn_conv_w, ffn_conv_b, ffn_w_down, final_norm_w, loss_target, m_c_ctx, m_mod_w, m_mod_b, m_norm1_w, m_norm2_w, m_ssd_w_in, m_ssd_conv_w, m_ssd_conv_b, m_ssd_dt_bias, m_ssd_a_log, m_ssd_d, m_ssd_norm_w, m_ssd_w_out, m_conf_w_pw1, m_conf_b_pw1, m_conf_w_dw, m_conf_b_dw, m_conf_ln_w, m_conf_ln_b, m_conf_w_pw2, m_conf_b_pw2, m_ffn_w_up, m_ffn_conv_w, m_ffn_conv_b, m_ffn_w_down, m_final_norm_w, v_c_ctx, v_mod_w, v_mod_b, v_norm1_w, v_norm2_w, v_ssd_w_in, v_ssd_conv_w, v_ssd_conv_b, v_ssd_dt_bias, v_ssd_a_log, v_ssd_d, v_ssd_norm_w, v_ssd_w_out, v_conf_w_pw1, v_conf_b_pw1, v_conf_w_dw, v_conf_b_dw, v_conf_ln_w, v_conf_ln_b, v_conf_w_pw2, v_conf_b_pw2, v_ffn_w_up, v_ffn_conv_w, v_ffn_conv_b, v_ffn_w_down, v_final_norm_w):
    given = dict(locals())
    W = {n: given[n] for n in WEIGHTS}
    Mo = {n: given["m_" + n] for n in WEIGHTS}
    Vo = {n: given["v_" + n] for n in WEIGHTS}

    ax, ay, ac = lax.axis_index("x"), lax.axis_index("y"), lax.axis_index("c")
    chip = 2 * ax + ay
    dev = 4 * ax + 2 * ay + ac

    D = x.shape[-1]
    L, Lc = x.shape[1], ctx.shape[1]
    T0 = L + Lc
    H = ssd_a_log.shape[-1]
    DI = ssd_norm_w.shape[-1]
    P = DI // H
    CD = ssd_conv_b.shape[-1]
    N = SSD_STATE
    G = (CD - DI) // (2 * N)
    FH = ffn_conv_b.shape[-1]
    KS = ssd_conv_w.shape[1]
    KC = conf_w_dw.shape[1]
    ncc = Lc // SSD_CHUNK

    shard_b = {n: W[n].astype(BF16) for n in BIG}

    small_shard_shapes = [W[n].shape for n in SMALL_SHARDED]
    f1 = _allgather8("gather_small", _pack([c] + [W[n] for n in SMALL_SHARDED]))
    parts = _unpack(f1, [c.shape] + small_shard_shapes)
    Wf = dict(W)
    for n, p in zip(SMALL_SHARDED, parts[1:]):
        Wf[n] = _unshard(p[::2], SHARD_AXIS[n])
    c16 = jnp.concatenate([parts[0].reshape(N_DEV, D), c_ctx[None, :], jnp.zeros((16 - N_DEV - 1, D), F32)], axis=0)

    S_mod = mod_w.shape[-1]
    mod_b_shard = lax.dynamic_slice_in_dim(mod_b, chip * S_mod, S_mod, axis=1)[:, None, :]
    mod_part = _mod_fwd(c16, mod_w, mod_b_shard)
    f2 = _allgather8("gather_mod", mod_part.reshape(2 * 16, S_mod))
    mods = jnp.concatenate([f2[2 * k].reshape(2, 16, S_mod) for k in range(N_CHIPS)], axis=-1)
    my = lax.dynamic_slice_in_dim(mods, dev, 1, axis=1)[:, 0]
    sh1, sc1, g1, sh2, sc2, g2 = [[my[l, k * D:(k + 1) * D] for l in range(2)] for k in range(6)]
    csh1, csc1 = mods[0, N_DEV, 0:D], mods[0, N_DEV, D:2 * D]

    in_halves = shard_b["ssd_w_in"].reshape(2, D // 2, ssd_w_in.shape[-1])
    gather_a, token = _exchange4_start("gather_w_in_start", [in_halves], True, mods, half=True)
    csc1 = _tie("tie_gather_w_in", csc1, token)

    def full_weight(n, own, landed):
        if landed.ndim == own.ndim:
            ax = SHARD_AXIS[n]
            return lax.dynamic_update_slice_in_dim(landed, own, chip * own.shape[ax], ax)
        return _unshard(_fill_own(landed, own, chip, True), SHARD_AXIS[n])

    xl = x[0]
    rows0 = (ctx[0], xl)
    n1w0, n2w0, n1w1, n2w1 = _par(norm1_w[0]), _par(norm2_w[0]), _par(norm1_w[1]), _par(norm2_w[1])
    sc_seg = jnp.stack([csc1, sc1[0]])[:, None, :]
    sh_seg = jnp.stack([csh1, sh1[0]])[:, None, :]

    a0 = _rw_fwd("l0_modnorm1", _f_modnorm, [], [n1w0, sc_seg, sh_seg], [D], T=T0, seg_rows=(Lc,), head=rows0,
                 out_dtypes=[BF16])
    (own_in,), (landed_in,) = _exchange4_wait("gather_w_in_wait", gather_a, a0)
    mine = _fill_own(landed_in, lax.dynamic_index_in_dim(own_in, ac, 0, keepdims=False), chip, True)
    (theirs,) = _swap_sibling("swap_w_in", [mine])
    top, bottom = jnp.where(ac == 0, mine, theirs), jnp.where(ac == 0, theirs, mine)
    w_in = jnp.concatenate([jnp.concatenate([top[k], bottom[k]], axis=0) for k in range(N_CHIPS)], axis=1)
    landed_in = theirs
    def start_gather(tag, names, dep):
        handle, tok = _exchange4_start("gather_" + tag + "_start", [shard_b[n] for n in names], True, dep,
                                       axes=[1 if SHARD_AXIS[n] == 1 else None for n in names])
        return (names, handle), tok

    def finish_gather(tag, group, after):
        names, handle = group
        return {n: full_weight(n, own, g)
                for n, own, g in zip(names, *_exchange4_wait("gather_" + tag + "_wait", handle, after))}

    gather_b, token = start_gather("mix", ["ssd_w_out", "conf_w_pw1", "conf_w_pw2"], landed_in)
    gather_c, token = start_gather("ffn", ["ffn_w_up", "ffn_w_down"], token)
    a0 = _tie("tie_gather_rest", a0, token)
    proj = _mm(a0, w_in, name="l0_w_in")
    seg_taps = [(k - KS // 2, ("seg", Lc)) for k in range(KS)]
    xbc_pre, xbc = _conv_fwd("l0_conv", proj, DI, CD, Wf["ssd_conv_w"][0], ssd_conv_b, seg_taps, act=True)
    dt_raw = proj[:, DI + CD:]
    dt_bias = _par(ssd_dt_bias.reshape(1, 2 * H))
    dt = _rw_fwd("l0_softplus", _f_softplus, [dt_raw], [dt_bias], [2 * H])
    dt_t = dt.T
    dtr = (dt_t[:H, None, :], dt_t[H:, None, :])
    a_all = -jnp.exp(ssd_a_log.reshape(2, H, 1, 1))
    a_neg = (a_all[0], a_all[1])
    (y_f, y_b), s_enter = _ssd_fwd(xbc, DI, DI + G * N, dtr, a_neg, P, ncc)
    gate_rows = [y_f, y_b, (xbc, 0, DI, Lc), (proj, 0, DI, Lc)]
    d_rep = _par(jnp.repeat(ssd_d[0], P))
    ssd_nw = _par(ssd_norm_w[0])
    yn = _rw_fwd("l0_ssd_gate", _f_ssd_gate, gate_rows, [d_rep, ssd_nw], [DI], T=L, out_dtypes=[BF16])
    Wb = finish_gather("mix", gather_b, yn)
    w_out, w_pw1, w_pw2 = Wb["ssd_w_out"][0], Wb["conf_w_pw1"][0], Wb["conf_w_pw2"][0]
    mix0 = _mm(yn, w_out, name="l0_w_out")
    g1_0, g2_0, g1_1, g2_1 = _par(g1[0]), _par(g2[0]), _par(g1[1]), _par(g2[1])
    h1 = _rw_fwd("l0_res1", _f_gate_res, [xl, mix0], [g1_0], [D])
    Wb = finish_gather("ffn", gather_c, h1)
    w_up, w_dn = Wb["ffn_w_up"], Wb["ffn_w_down"]

    grid_taps = [((i - 1) * GRID_W + (j - 1), (None if j == 1 else ("col", j - 1))) for i in range(3) for j in range(3)]

    def ffn_fwd(l, h, tag):
        a = _rw_fwd(tag + "_modnorm2", _f_modnorm, [h], [_par(norm2_w[l]), _par(sc2[l]), _par(sh2[l])], [D],
                    out_dtypes=[BF16])
        hh = _mm(a, w_up[l], name=tag + "_w_up")
        gc = _conv_fwd(tag + "_ffn_conv", hh, FH, FH, Wf["ffn_conv_w"][l].reshape(9, FH), ffn_conv_b[l][None, :],
                       grid_taps)
        act = _rw_fwd(tag + "_act", _f_ffn_act, [(hh, 0, FH), gc], [], [FH], col_tile=_tile(FH, 1536),
                      out_dtypes=[BF16])
        dn = _mm(act, w_dn[l], name=tag + "_w_down")
        return a, hh, gc, act, dn

    a1, hh0, gc0, act0, dn0 = ffn_fwd(0, h1, "l0")
    h2 = _rw_fwd("l0_res2", _f_gate_res, [h1, dn0], [g2_0], [D])

    a2 = _rw_fwd("l1_modnorm1", _f_modnorm, [h2], [n1w1, _par(sc1[1]), _par(sh1[1])], [D], out_dtypes=[BF16])
    pw = _mm(a2, w_pw1, name="l1_pw1")
    b_pw1 = Wf["conf_b_pw1"][0]
    glu = _rw_fwd("l1_glu", _f_glu, [(pw, 0, D), (pw, D, D)], [_par(b_pw1[:D]), _par(b_pw1[D:])], [D])
    conf_taps = [(k - KC // 2, None) for k in range(KC)]
    cv = _conv_fwd("l1_conv", glu, 0, D, Wf["conf_w_dw"][0], Wf["conf_b_dw"], conf_taps)
    ln_w, ln_b = _par(Wf["conf_ln_w"][0]), _par(Wf["conf_ln_b"][0])
    ls = _rw_fwd("l1_ln_silu", _f_ln_silu, [cv], [ln_w, ln_b], [D], out_dtypes=[BF16])
    p2 = _mm(ls, w_pw2, name="l1_pw2")
    b_pw2 = _par(Wf["conf_b_pw2"][0])
    h3 = _rw_fwd("l1_res1", _f_gate_res_bias, [h2, p2], [g1_1, b_pw2], [D])
    a3, hh1, gc1, act1, dn1 = ffn_fwd(1, h3, "l1")
    h4 = _rw_fwd("l1_res2", _f_gate_res, [h3, dn1], [g2_1], [D])

    fnw = final_norm_w[None, :]
    tgt = loss_target[0]
    loss_local = _loss_fwd(h4, tgt, fnw)[0, 0]
    loss = lax.psum(loss_local, ("x", "y", "c"))

    G_full = {}
    reduces = {}

    def start_reduce(tag, items, dep):
        def blocks_of(g, ax):
            if g.ndim == 3:
                return g
            return g.reshape(N_CHIPS, g.shape[0] // N_CHIPS, g.shape[1]) if ax == 0 else _to_blocks(g, ax)

        blocks = [blocks_of(g, ax).astype(BF16) for _, g, ax in items]
        handle, tok = _exchange4_start("reduce_" + tag + "_start", blocks, False, dep)
        reduces[tag] = ([n for n, _, _ in items], handle)
        return tok
    ones = jnp.ones((L, 1), F32)
    (dh4,), (dfnw,) = _rw_bwd("loss_bwd", _f_loss_rows, [h4, tgt], [_par(final_norm_w)], [ones],
                              row_grad=[True, False], par_grad=[True])
    G_full["final_norm_w"] = dfnw.reshape(D)

    def ffn_bwd(l, h, saved, g2_l, dh_out, tag):
        a, hh, gc, act, dn = saved
        (ddn,), (dg2,) = _rw_bwd(tag + "_res2_bwd", _f_gate, [dn], [g2_l], [dh_out],
                                 row_grad=[True], par_grad=[True], row_dtypes=[BF16])
        dact = _mm(ddn, w_dn[l], tb=True, name=tag + "_w_down_dx")
        dwdn = _mm(act, ddn, ta=True, name=tag + "_w_down_dw", out_dtype=BF16)
        (dval, dgc), _ = _rw_bwd(tag + "_act_bwd", _f_ffn_act, [(hh, 0, FH), gc], [], [dact],
                                 row_grad=[True, True], par_grad=[], col_tile=_tile(FH, 1536), row_dtypes=[BF16, F32])
        dgin, dcw, dcb = _conv_bwd(tag + "_ffn_conv_bwd", hh, FH, FH, Wf["ffn_conv_w"][l].reshape(9, FH), dgc,
                                   grid_taps, du_dtype=BF16)
        dhh = jnp.concatenate([dval, dgin], axis=1)
        da = _mm(dhh, w_up[l], tb=True, name=tag + "_w_up_dx")
        dwup = _mm(a, dhh, ta=True, name=tag + "_w_up_dw", out_dtype=BF16, col_blocks=N_CHIPS)
        (dh,), (dn2w, dsc2, dsh2) = _rw_bwd(
            tag + "_modnorm2_bwd", _f_modnorm, [h], [_par(norm2_w[l]), _par(sc2[l]), _par(sh2[l])], [da],
            row_grad=[True], par_grad=[True, True, True], add=dh_out)
        return dh, dict(w_down=dwdn, w_up=dwup, conv_w=dcw.reshape(3, 3, FH), conv_b=dcb.reshape(FH),
                        n2w=dn2w.reshape(D), sc2=dsc2.reshape(D), sh2=dsh2.reshape(D), g2=dg2.reshape(D))

    dh3, gf1 = ffn_bwd(1, h3, (a3, hh1, gc1, act1, dn1), g2_1, dh4, "l1")
    (dp2,), (dg1_1, db_pw2) = _rw_bwd("l1_res1_bwd", _f_gate_bias, [p2], [g1_1, b_pw2], [dh3],
                                      row_grad=[True], par_grad=[True, True], row_dtypes=[BF16])
    dls = _mm(dp2, w_pw2, tb=True, name="l1_pw2_dx")
    dw_pw2 = _mm(ls, dp2, ta=True, name="l1_pw2_dw", out_dtype=BF16)
    (dcv,), (dln_w, dln_b) = _rw_bwd("l1_ln_silu_bwd", _f_ln_silu, [cv], [ln_w, ln_b], [dls],
                                     row_grad=[True], par_grad=[True, True])
    dglu, dw_dw, db_dw = _conv_bwd("l1_conv_bwd", glu, 0, D, Wf["conf_w_dw"][0], dcv, conf_taps)
    (dpa, dpg), (dba, dbg) = _rw_bwd("l1_glu_bwd", _f_glu, [(pw, 0, D), (pw, D, D)],
                                     [_par(b_pw1[:D]), _par(b_pw1[D:])], [dglu],
                                     row_grad=[True, True], par_grad=[True, True], row_dtypes=[BF16, BF16])
    dpw = jnp.concatenate([dpa, dpg], axis=1)
    da2 = _mm(dpw, w_pw1, tb=True, name="l1_pw1_dx")
    dw_pw1 = _mm(a2, dpw, ta=True, name="l1_pw1_dw", out_dtype=BF16, col_blocks=N_CHIPS)
    (dh2,), (dn1w1, dsc1_1, dsh1_1) = _rw_bwd(
        "l1_modnorm1_bwd", _f_modnorm, [h2], [n1w1, _par(sc1[1]), _par(sh1[1])], [da2],
        row_grad=[True], par_grad=[True, True, True], add=dh3)
    G_full["conf_b_pw2"] = db_pw2.reshape(1, D)
    G_full["conf_ln_w"], G_full["conf_ln_b"] = dln_w.reshape(1, D), dln_b.reshape(1, D)
    G_full["conf_w_dw"], G_full["conf_b_dw"] = dw_dw[None], db_dw.reshape(1, D)
    G_full["conf_b_pw1"] = jnp.concatenate([dba.reshape(1, D), dbg.reshape(1, D)], axis=1)

    token = start_reduce("l1", [("conf_w_pw2", dw_pw2, 0), ("conf_w_pw1", dw_pw1, 1), ("ffn_w_up1", gf1["w_up"], 1),
                                ("ffn_w_down1", gf1["w_down"], 0)], dw_pw2)
    dh2 = _tie("tie_reduce_l1", dh2, token)
    dh1, gf0 = ffn_bwd(0, h1, (a1, hh0, gc0, act0, dn0), g2_0, dh2, "l0")
    G_full["ffn_conv_w"] = jnp.stack([gf0["conv_w"], gf1["conv_w"]])
    G_full["ffn_conv_b"] = jnp.stack([gf0["conv_b"], gf1["conv_b"]])

    (dmix,), (dg1_0,) = _rw_bwd("l0_res1_bwd", _f_gate, [mix0], [g1_0], [dh1],
                                row_grad=[True], par_grad=[True], row_dtypes=[BF16])
    dyn = _mm(dmix, w_out, tb=True, name="l0_w_out_dx")
    dw_out = _mm(yn, dmix, ta=True, name="l0_w_out_dw", out_dtype=BF16)
    token = start_reduce("l0", [("ffn_w_up0", gf0["w_up"], 1), ("ffn_w_down0", gf0["w_down"], 0),
                                ("ssd_w_out", dw_out, 0)], dw_out)
    dyn = _tie("tie_reduce_l0", dyn, token)
    (dy_lat, dxs_gate, dz_lat), (dd_rep, dssd_nw) = _rw_bwd(
        "l0_ssd_gate_bwd", _f_ssd_gate, gate_rows, [d_rep, ssd_nw], [dyn],
        row_grad=[True, False, True, True], par_grad=[True, True], T=L, row_dtypes=[F32, F32, BF16])
    g_f, g_b = _ssd_bwd(xbc, DI, DI + G * N, dtr, a_neg, s_enter, dy_lat, P, ncc)
    silu_bwd = functools.partial(_rw_bwd, f=_silu, pars=[], row_grad=[True], par_grad=[], T=T0)
    (dxs_pre,), _ = silu_bwd("l0_silu_bwd_x", rows=[(xbc_pre, 0, DI)], cot_fn=lambda p, q, r: p + q + r,
                             cots=[g_f[0], g_b[0], (dxs_gate, 0, DI, -Lc)],
                             col_tile=_tile(DI, 1024))
    (db_pre,), _ = silu_bwd("l0_silu_bwd_b", rows=[(xbc_pre, DI, G * N)], cot_fn=lambda p, q: p + q,
                            cots=[g_f[1], g_b[1]], col_tile=_tile(G * N, 1024))
    (dc_pre,), _ = silu_bwd("l0_silu_bwd_c", rows=[(xbc_pre, DI + G * N, G * N)], cot_fn=lambda p, q: p + q,
                            cots=[g_f[2], g_b[2]], col_tile=_tile(G * N, 1024))
    conv_w0 = Wf["ssd_conv_w"][0]
    pieces = []
    for tag, off, width, g_pre in (("x", 0, DI, dxs_pre), ("b", DI, G * N, db_pre), ("c", DI + G * N, G * N, dc_pre)):
        pieces.append(_conv_bwd("l0_conv_bwd_" + tag, proj, DI + off, width, conv_w0[:, off:off + width], g_pre,
                                seg_taps, du_dtype=BF16))
    dconv_in = [p[0] for p in pieces]
    dcw0 = jnp.concatenate([p[1] for p in pieces], axis=1)
    dcb0 = jnp.concatenate([p[2] for p in pieces], axis=1)
    ddt = jnp.concatenate([g_f[3][:, 0, :].T, g_b[3][:, 0, :].T], axis=1)
    (ddt_raw,), (ddt_bias,) = _rw_bwd("l0_softplus_bwd", _f_softplus, [dt_raw], [dt_bias], [ddt],
                                      row_grad=[True], par_grad=[True], row_dtypes=[BF16])
    dproj = jnp.concatenate([jnp.pad(dz_lat, ((Lc, 0), (0, 0))), *dconv_in, ddt_raw], axis=1)
    da0 = _mm(dproj, w_in, tb=True, name="l0_w_in_dx")
    dw_in = _mm(a0, dproj, ta=True, name="l0_w_in_dw", out_dtype=BF16)
    token = start_reduce("in", [("ssd_w_in", dw_in, 1)], dw_in)
    da0 = _tie("tie_reduce_in", da0, token)
    (dhcat,), (dn1w0, dsc_seg, dsh_seg) = _rw_bwd(
        "l0_modnorm1_bwd", _f_modnorm, [], [n1w0, sc_seg, sh_seg], [da0], T=T0, head=rows0,
        row_grad=[True], par_grad=[True, True, True], seg_rows=(Lc,), add=(dh1, 0, D, -Lc))
    grad_x = dhcat[Lc:][None]

    da_heads = jnp.stack([g[4][:, 0, 0].reshape(G, T0 // SSD_CHUNK, H // G).sum(axis=1).reshape(H)
                          for g in (g_f, g_b)])[None]
    G_full["ssd_a_log"] = da_heads * (-jnp.exp(ssd_a_log))
    G_full["ssd_dt_bias"] = ddt_bias.reshape(1, 2, H)
    G_full["ssd_d"] = dd_rep.reshape(H, P).sum(axis=1)[None]
    G_full["ssd_norm_w"] = dssd_nw.reshape(1, DI)
    G_full["ssd_conv_w"], G_full["ssd_conv_b"] = dcw0[None], dcb0.reshape(1, CD)
    G_full["norm1_w"] = jnp.stack([dn1w0.reshape(D), dn1w1.reshape(D)])
    G_full["norm2_w"] = jnp.stack([gf0["n2w"], gf1["n2w"]])

    zD = jnp.zeros((D,), F32)
    dm_own = jnp.stack([
        jnp.concatenate([dsh_seg[1, 0], dsc_seg[1, 0], dg1_0.reshape(D), gf0["sh2"], gf0["sc2"], gf0["g2"]]),
        jnp.concatenate([dsh1_1.reshape(D), dsc1_1.reshape(D), dg1_1.reshape(D), gf1["sh2"], gf1["sc2"], gf1["g2"]]),
    ])
    dmc_own = jnp.concatenate([dsh_seg[0, 0], dsc_seg[0, 0], zD, zD, zD, zD])

    out = {}

    def finish_reduce(tags, after, swap_name):
        partial = {}
        for tag in tags:
            names, handle = reduces[tag]
            blocks, landed = _exchange4_wait("reduce_" + tag + "_wait", handle, after)
            for n, blk, own in zip(names, landed, blocks):
                r = _fill_own(blk, own, chip, False)
                partial[n] = _sum_leading("sum4_" + n, r.reshape(N_CHIPS, -1, r.shape[-1]),
                                          (0, 1, 2, 3)).reshape(r.shape[1:])
        for n in ("ffn_w_up", "ffn_w_down"):
            if n + "0" in partial:
                partial[n] = jnp.stack([partial.pop(n + "0"), partial.pop(n + "1")])
        names = [n for n in BIG if n in partial]
        mine = [partial[n].reshape(W[n].shape) for n in names]
        for n, own, sib in zip(names, mine, _swap_sibling(swap_name, mine)):
            out[n] = _adamw("adamw_" + n, W[n], Mo[n], Vo[n], own, sib)
        return names

    early = finish_reduce(["l1", "l0"], dhcat, "swap_grads_early")

    small_sum_names = [n for n in SMALL if n not in ("c_ctx", "mod_b")]
    sum_part = [G_full[n] for n in small_sum_names] + [dmc_own]
    packed = _tie("tie_small_grads", _pack(sum_part + [dm_own]), out[early[-1]][1])
    gat = _allgather8("gather_small_grads", packed)
    total = _sum_leading("sum_small_grads", gat, tuple(range(N_DEV)))
    summed = _unpack(total, [a.shape for a in sum_part])
    Gs = dict(zip(small_sum_names, summed[:-1]))
    dmc_tot = summed[-1]
    dm_all = _unpack(gat, [a.shape for a in sum_part] + [dm_own.shape])[-1].transpose(1, 0, 2)
    dm16 = jnp.concatenate([dm_all, jnp.stack([dmc_tot, jnp.zeros_like(dmc_tot)])[:, None, :],
                            jnp.zeros((2, 16 - N_DEV - 1, 6 * D), F32)], axis=1)
    Gs["mod_b"] = _sum_leading("sum_mod_b", dm16.transpose(1, 0, 2).reshape(16, 2 * 6 * D // LANE, LANE),
                               tuple(range(N_DEV + 1))).reshape(2, 6 * D)

    dm16_shard = lax.dynamic_slice_in_dim(dm16, chip * S_mod, S_mod, axis=2)
    ds16 = _mm(dm16_shard[0], mod_w[0], tb=True, precision=HIGHEST, name="c_ctx_dx")
    sig = jax.nn.sigmoid(c_ctx)
    dcc_part = ds16[N_DEV] * (sig * (1.0 + c_ctx * (1.0 - sig)))
    gat_cc = _allgather8("gather_c_ctx_grad", _pack([dcc_part]))
    Gs["c_ctx"] = _sum_leading("sum_c_ctx_grad", gat_cc, (0, 2, 4, 6)).reshape(-1)[:D]

    s16t = _silu(c16).T
    out["mod_w"] = _mod_w_update(s16t, dm16_shard, mod_w, m_mod_w, v_mod_w)
    finish_reduce(["in"], out["mod_w"][0], "swap_grads_late")

    def own(n, full):
        if n in SHARD_AXIS:
            size = W[n].shape[SHARD_AXIS[n]]
            return lax.dynamic_slice_in_dim(full, chip * size, size, axis=SHARD_AXIS[n])
        return full

    def two_d(a):
        return a.reshape(1, -1) if a.ndim == 1 else a

    g_small = [own(n, Gs[n].reshape(Wf[n].shape)) for n in SMALL]
    res = _adamw_many("adamw_small", [two_d(W[n]) for n in SMALL], [two_d(Mo[n]) for n in SMALL],
                      [two_d(Vo[n]) for n in SMALL], [two_d(g) for g in g_small])
    for n, r in zip(SMALL, res):
        out[n] = tuple(t.reshape(W[n].shape) for t in r)

    grads = [out[n][0] for n in WEIGHTS]
    deltas = [out[n][1] for n in WEIGHTS]
    new_m = [out[n][2] for n in WEIGHTS]
    new_v = [out[n][3] for n in WEIGHTS]
    return (loss, grad_x, *grads, *deltas, *new_m, *new_v)
```

```python
import functools

import jax
import jax.numpy as jnp
from jax import lax
from jax.experimental import pallas as pl
from jax.experimental.pallas import tpu as pltpu

F32 = jnp.float32
BF16 = jnp.bfloat16
MESH = pl.DeviceIdType.MESH
HIGHEST = lax.Precision.HIGHEST

VMEM_LIMIT_BYTES = 48 * 1024 * 1024
LANE = 128
SUBLANE = 8

SSD_STATE = 128
SSD_CHUNK = 128
GRID_W = 64
EPS = 1e-6
N_CHIPS = 4
N_DEV = 8

ADAM_LR = 0.001
ADAM_B1 = 0.9
ADAM_B2 = 0.999
ADAM_EPS = 1e-08
ADAM_WD = 0.01
ADAM_STEP = 10


def _pcall(body, **kw):
    return pl.pallas_call(body, **kw)


def _cparams(n_grid):
    return pltpu.CompilerParams(dimension_semantics=("arbitrary",) * n_grid, vmem_limit_bytes=VMEM_LIMIT_BYTES)


def _cdiv(a, b):
    return -(-a // b)


def _round_up(a, b):
    return _cdiv(a, b) * b


def _tile(n, cap):
    if n <= cap:
        return n
    best = None
    for t in range(LANE, cap + 1, LANE):
        if n % t == 0:
            best = t
    if best is None:
        npad = _round_up(n, LANE)
        for t in range(LANE, cap + 1, LANE):
            if npad % t == 0:
                best = t
    return best


def _row_tile(n, cap, also=()):
    best = None
    for step in (2 * SUBLANE, SUBLANE):
        for t in range(step, min(cap, n) + 1, step):
            if n % t == 0 and all(a % t == 0 for a in also):
                best = t
        if best is not None:
            break
    assert best is not None, (n, cap, also)
    return best


def _silu(v):
    return v * jax.nn.sigmoid(v)


def _mm(a, b, *, name, ta=False, tb=False, precision=None, cap=1024, out_dtype=F32, col_blocks=None):
    M, K = (a.shape[1], a.shape[0]) if ta else a.shape
    N = b.shape[0] if tb else b.shape[1]
    assert K == (b.shape[1] if tb else b.shape[0]), (a.shape, b.shape, ta, tb)
    tm, tk = _tile(M, cap), _tile(K, cap + cap // 2)
    tn = _tile(N if col_blocks is None else N // col_blocks, cap + cap // 2)
    nm, nn, nk = _cdiv(M, tm), _cdiv(N, tn), _cdiv(K, tk)
    k_tail = K % tk
    exact = precision is not None

    def body(a_ref, b_ref, o_ref, acc_ref):
        k = pl.program_id(2)

        @pl.when(k == 0)
        def _():
            acc_ref[...] = jnp.zeros_like(acc_ref)

        av = a_ref[...]
        bv = b_ref[...]
        if k_tail:
            lim = K - k * tk
            ka = lax.broadcasted_iota(jnp.int32, av.shape, 0 if ta else 1)
            kb = lax.broadcasted_iota(jnp.int32, bv.shape, 1 if tb else 0)
            av = jnp.where(ka < lim, av, jnp.zeros_like(av))
            bv = jnp.where(kb < lim, bv, jnp.zeros_like(bv))
        if exact:
            av = av.astype(F32)
            bv = bv.astype(F32)
        else:
            av = av.astype(BF16)
            bv = bv.astype(BF16)
        dn = (((0 if ta else 1,), (1 if tb else 0,)), ((), ()))
        acc_ref[...] += lax.dot_general(av, bv, dn, preferred_element_type=F32, precision=precision)

        @pl.when(k == nk - 1)
        def _():
            o_ref[...] = acc_ref[...].astype(o_ref.dtype)

    a_spec = pl.BlockSpec((tk, tm), lambda i, j, k: (k, i)) if ta else pl.BlockSpec((tm, tk), lambda i, j, k: (i, k))
    b_spec = pl.BlockSpec((tn, tk), lambda i, j, k: (j, k)) if tb else pl.BlockSpec((tk, tn), lambda i, j, k: (k, j))
    if col_blocks is None:
        out_spec = pl.BlockSpec((tm, tn), lambda i, j, k: (i, j))
        out_shape = jax.ShapeDtypeStruct((M, N), out_dtype)
    else:
        per = (N // col_blocks) // tn
        assert per * tn * col_blocks == N, (N, col_blocks, tn)
        out_spec = pl.BlockSpec((None, tm, tn), lambda i, j, k: (j // per, i, j % per))
        out_shape = jax.ShapeDtypeStruct((col_blocks, M, N // col_blocks), out_dtype)
    return _pcall(
        body, name=name, grid=(nm, nn, nk), in_specs=[a_spec, b_spec], out_specs=out_spec, out_shape=out_shape,
        scratch_shapes=[pltpu.VMEM((tm, tn), F32)], compiler_params=_cparams(3),
    )(a, b)


def _norm_rows(rows):
    out = []
    for r in rows:
        if not isinstance(r, tuple):
            r = (r,)
        arr, off, width, roff = (r + (0, None, 0)[len(r) - 1:])
        out.append((arr, off, width if width is not None else arr.shape[1], roff))
    return out


def _rw_plan(T, rows, pars, seg_rows, col_tile, tm_cap):
    widths = [r[2] for r in rows]
    wmax = max(widths + [p.shape[-1] for p in pars] + [1])
    if col_tile is not None:
        assert all(w == widths[0] for w in widths) and all(p.shape[-1] == widths[0] for p in pars)
        ncol = widths[0] // col_tile
        assert ncol * col_tile == widths[0]
        wmax = col_tile
    else:
        ncol = 1
    cap = tm_cap if tm_cap is not None else max(SUBLANE, min(512, (512 * 1024) // wmax))
    tm = _row_tile(T, cap, also=tuple(seg_rows) + tuple(abs(r[3]) for r in rows if r[3]))
    bounds = tuple(s // tm for s in seg_rows)
    return widths, ncol, tm, bounds


def _rw_specs(rows, pars, ncol, tm, bounds, col_tile):
    def seg(i):
        s = 0
        for b in bounds:
            s = s + (i >= b).astype(jnp.int32)
        return s

    specs = []
    for arr, off, w, roff in rows:
        bw = col_tile if col_tile is not None else w
        assert off % bw == 0 and roff % tm == 0, (off, bw, roff, tm)
        specs.append(pl.BlockSpec((tm, bw), functools.partial(
            lambda j, i, ob, rb, last: (jnp.clip(i + rb, 0, last), ob + j),
            ob=off // bw, rb=roff // tm, last=arr.shape[0] // tm - 1)))
    for p in pars:
        bw = col_tile if col_tile is not None else p.shape[-1]
        if p.shape[0] > 1:
            specs.append(pl.BlockSpec((None, 1, bw), lambda j, i: (seg(i), 0, j)))
        else:
            specs.append(pl.BlockSpec((None, 1, bw), lambda j, i: (0, 0, j)))
    return specs, seg


def _head_rows(head):
    top, bottom = head
    return [(top, 0, None, 0), (bottom, 0, None, -top.shape[0])]


def _rw_fwd(name, f, rows, pars, out_widths, *, T=None, seg_rows=(), col_tile=None, tm_cap=None, out_dtypes=None,
            head=None):
    rows = _norm_rows((_head_rows(head) if head else []) + list(rows))
    T = rows[0][0].shape[0] if T is None else T
    widths, ncol, tm, bounds = _rw_plan(T, rows, pars, seg_rows, col_tile, tm_cap)
    in_specs, _ = _rw_specs(rows, pars, ncol, tm, bounds, col_tile)
    nr, npar, nout = len(rows), len(pars), len(out_widths)

    def body(*refs):
        vals = [r[...] for r in refs[:nr + npar]]
        if head:
            vals = [jnp.where(pl.program_id(1) < head[0].shape[0] // tm, vals[0], vals[1])] + vals[2:]
        outs = f(*vals)
        if not isinstance(outs, (tuple, list)):
            outs = (outs,)
        for o_ref, o in zip(refs[nr + npar:], outs):
            o_ref[...] = o.astype(o_ref.dtype)

    out_specs = [pl.BlockSpec((tm, col_tile if col_tile is not None else w), lambda j, i: (i, j)) for w in out_widths]
    res = _pcall(
        body, name=name, grid=(ncol, T // tm), in_specs=in_specs, out_specs=out_specs,
        out_shape=[jax.ShapeDtypeStruct((T, w), dt) for w, dt in zip(out_widths, out_dtypes or [F32] * nout)],
        compiler_params=_cparams(2),
    )(*[r[0] for r in rows], *pars)
    return res if nout > 1 else res[0]


def _rw_bwd(name, f, rows, pars, cots, *, row_grad, par_grad, T=None, seg_rows=(), col_tile=None, tm_cap=None,
            add=None, cot_fn=None, row_dtypes=None, head=None):
    rows = _norm_rows((_head_rows(head) if head else []) + list(rows))
    cots = _norm_rows(cots)
    T = rows[0][0].shape[0] if T is None else T
    extra = _norm_rows([add]) if add is not None else []
    all_rows = rows + cots + extra
    widths, ncol, tm, bounds = _rw_plan(T, all_rows, pars, seg_rows, col_tile, tm_cap)
    in_specs, seg = _rw_specs(all_rows, pars, ncol, tm, bounds, col_tile)
    nr, nc, ne, npar = len(rows), len(cots), len(extra), len(pars)
    skip = 1 if head else 0
    widths = widths[skip:]
    nrf = nr - skip
    row_idx = [k for k in range(nrf) if row_grad[k]]
    par_idx = [k for k in range(npar) if par_grad[k]]

    def body(*refs):
        i = pl.program_id(1)

        def zero_before(vals, ops):
            return [jnp.where(i + c[3] // tm >= 0, v, jnp.zeros_like(v)) if c[3] < 0 else v for v, c in zip(vals, ops)]

        row_vals = [r[...] for r in refs[:nr]]
        if head:
            row_vals = [jnp.where(i < head[0].shape[0] // tm, row_vals[0], row_vals[1])] + row_vals[2:]
        cot_vals = zero_before([r[...] for r in refs[nr:nr + nc]], cots)
        add_vals = zero_before([r[...] for r in refs[nr + nc:nr + nc + ne]], extra)
        par_vals = [r[...] for r in refs[nr + nc + ne:nr + nc + ne + npar]]
        out_refs = refs[nr + nc + ne + npar:]
        outs, vjp = jax.vjp(f, *row_vals, *par_vals)
        if cot_fn is not None:
            cot_vals = cot_fn(*cot_vals)
            if not isinstance(cot_vals, (tuple, list)):
                cot_vals = (cot_vals,)
        if isinstance(outs, (tuple, list)):
            grads = vjp(tuple(c.astype(o.dtype) for c, o in zip(cot_vals, outs)))
        else:
            grads = vjp(cot_vals[0].astype(outs.dtype))
        first_seg = i == 0
        for b in bounds:
            first_seg = first_seg | (i == b)
        for n, k in enumerate(row_idx):
            g = grads[k]
            if n == 0 and add_vals:
                g = g + add_vals[0]
            out_refs[n][...] = g.astype(out_refs[n].dtype)
        for n, k in enumerate(par_idx):
            g = grads[nrf + k]
            o_ref = out_refs[len(row_idx) + n]
            first = first_seg if pars[k].shape[0] > 1 else (i == 0)

            @pl.when(first)
            def _(o_ref=o_ref, g=g):
                o_ref[...] = g

            @pl.when(jnp.logical_not(first))
            def _(o_ref=o_ref, g=g):
                o_ref[...] += g

    out_specs, out_shape = [], []
    for k in row_idx:
        w = widths[k]
        out_specs.append(pl.BlockSpec((tm, col_tile if col_tile is not None else w), lambda j, i: (i, j)))
        out_shape.append(jax.ShapeDtypeStruct((T, w), row_dtypes[len(out_shape)] if row_dtypes else F32))
    for k in par_idx:
        p = pars[k]
        bw = col_tile if col_tile is not None else p.shape[-1]
        if p.shape[0] > 1:
            out_specs.append(pl.BlockSpec((None, 1, bw), lambda j, i: (seg(i), 0, j)))
        else:
            out_specs.append(pl.BlockSpec((None, 1, bw), lambda j, i: (0, 0, j)))
        out_shape.append(jax.ShapeDtypeStruct(p.shape, F32))
    res = _pcall(
        body, name=name, grid=(ncol, T // tm), in_specs=in_specs, out_specs=out_specs, out_shape=out_shape,
        compiler_params=_cparams(2),
    )(*[r[0] for r in all_rows], *pars)
    return list(res[:len(row_idx)]), list(res[len(row_idx):])


def _f_modnorm(h, w, sc, sh):
    y = h * lax.rsqrt(jnp.mean(h * h, axis=-1, keepdims=True) + EPS)
    return (y * w) * (1.0 + sc) + sh


def _f_gate_res(h, y, g):
    return h + g * y


def _f_gate_res_bias(h, y, g, b):
    return h + g * (y + b)


def _f_gate(y, g):
    return g * y


def _f_gate_bias(y, g, b):
    return g * (y + b)


def _f_ffn_act(val, gate):
    return _silu(gate) * val


def _f_softplus(raw, bias):
    v = raw + bias
    return jnp.maximum(v, 0.0) + jnp.log(1.0 + jnp.exp(-jnp.abs(v)))


def _f_ssd_gate(yf, yb, xs, z, d_rep, nw):
    y = (yf + yb + d_rep * xs) * _silu(z)
    return (y * lax.rsqrt(jnp.mean(y * y, axis=-1, keepdims=True) + EPS)) * nw


def _f_glu(a, g, ba, bg):
    return (a + ba) * jax.nn.sigmoid(g + bg)


def _f_ln_silu(h, w, b):
    mu = jnp.mean(h, axis=-1, keepdims=True)
    d = h - mu
    y = d * lax.rsqrt(jnp.mean(d * d, axis=-1, keepdims=True) + EPS)
    return _silu(y * w + b)


def _f_loss_rows(h, t, w):
    y = (h * lax.rsqrt(jnp.mean(h * h, axis=-1, keepdims=True) + EPS)) * w
    e = y - t
    return 0.5 * jnp.mean(e * e, axis=-1, keepdims=True)


def _f_adamw(w, m, v, ga, gb):
    g = ga + gb
    m = ADAM_B1 * m + (1.0 - ADAM_B1) * g
    v = ADAM_B2 * v + (1.0 - ADAM_B2) * (g * g)
    m_hat = m / (1.0 - ADAM_B1 ** ADAM_STEP)
    v_hat = v / (1.0 - ADAM_B2 ** ADAM_STEP)
    delta = -ADAM_LR * (m_hat / (jnp.sqrt(v_hat) + ADAM_EPS) + ADAM_WD * w)
    return g, delta, m, v


def _adamw(name, w, m, v, ga, gb):
    shape = w.shape
    c = shape[-1]
    two_d = [t.reshape(-1, c) for t in (w, m, v, ga, gb)]
    rows = two_d[0].shape[0]
    pad = _round_up(rows, SUBLANE) - rows
    if pad:
        two_d = [jnp.pad(t, ((0, pad), (0, 0))) for t in two_d]
    outs = _rw_fwd(name, _f_adamw, two_d, [], [c] * 4)
    return tuple(o[:rows].reshape(shape) for o in outs)


def _sum_leading(name, x, idxs):
    _, R, C = x.shape
    tm = _row_tile(R, max(SUBLANE, min(512, (512 * 1024) // C)))

    def body(x_ref, o_ref):
        acc = x_ref[idxs[0]].astype(F32)
        for k in idxs[1:]:
            acc = acc + x_ref[k].astype(F32)
        o_ref[...] = acc

    return _pcall(
        body, name=name, grid=(R // tm,), in_specs=[pl.BlockSpec((x.shape[0], tm, C), lambda i: (0, i, 0))],
        out_specs=pl.BlockSpec((tm, C), lambda i: (i, 0)), out_shape=jax.ShapeDtypeStruct((R, C), F32),
        compiler_params=_cparams(1),
    )(x)


def _loss_fwd(h, t, w):
    T, D = h.shape
    tm = _row_tile(T, 256)

    def body(h_ref, t_ref, w_ref, o_ref):
        i = pl.program_id(0)
        part = jnp.sum(_f_loss_rows(h_ref[...], t_ref[...], w_ref[...]), axis=0, keepdims=True)
        part = jnp.broadcast_to(part, (1, LANE))

        @pl.when(i == 0)
        def _():
            o_ref[...] = part

        @pl.when(i > 0)
        def _():
            o_ref[...] += part

    return _pcall(
        body, name="loss_fwd", grid=(T // tm,),
        in_specs=[pl.BlockSpec((tm, D), lambda i: (i, 0)), pl.BlockSpec((tm, D), lambda i: (i, 0)),
                  pl.BlockSpec((1, D), lambda i: (0, 0))],
        out_specs=pl.BlockSpec((1, LANE), lambda i: (0, 0)), out_shape=jax.ShapeDtypeStruct((1, LANE), F32),
        compiler_params=_cparams(1),
    )(h, t, w)


CONV_ROWS = 256
CONV_ROWS_FEW_TAPS = 1024
CONV_ACC_ELEMS = 16384


def _col_mask(arg, t):
    col = jnp.bitwise_and(t, GRID_W - 1)
    return (col != 0) if arg < 0 else (col != GRID_W - 1)


def _conv_plan(T, C, taps):
    seg = [m[1] for _, m in taps if m is not None and m[0] == "seg"]
    cap = CONV_ROWS_FEW_TAPS if len(taps) <= 9 else CONV_ROWS
    rc = next(r for r in (1024, 768, 512, 256, LANE) if r <= cap and T % r == 0)
    ct = next((t for t in (512, 256, LANE) if C % t == 0), C)
    reach = max(abs(s) for s, _ in taps)
    hb = next(h for h in (8, 16, 32, 64, 128, 256) if h >= reach and rc % h == 0)
    sub = max(2 * SUBLANE, min(rc, CONV_ACC_ELEMS // ct))
    boundary = None
    if seg:
        inside = seg[0] % rc
        boundary = (seg[0], (inside - reach, inside + reach) if inside else None)
    taps = [(s, None if (m is None or m[0] == "seg") else m[1]) for s, m in taps]
    return rc, ct, hb, sub, T // rc, C // ct, boundary, taps


def _seg_ok(boundary, i, rc, r0, n, s):
    if boundary is None or boundary[1] is None or s == 0 or r0 + n <= boundary[1][0] or r0 >= boundary[1][1]:
        return None
    t = i * rc + r0 + lax.broadcasted_iota(jnp.int32, (n, 1), 0)
    return (t >= boundary[0]) == ((t + s) >= boundary[0])


def _halo_specs(rc, ct, hb, T, off_blocks):
    per = rc // hb
    last = T // hb - 1
    prev = pl.BlockSpec((hb, ct), lambda j, i: (jnp.maximum(i * per - 1, 0), off_blocks + j))
    cur = pl.BlockSpec((rc, ct), lambda j, i: (i, off_blocks + j))
    nxt = pl.BlockSpec((hb, ct), lambda j, i: (jnp.minimum((i + 1) * per, last), off_blocks + j))
    return [prev, cur, nxt]


def _fill_halo(pad_ref, p_ref, c_ref, n_ref, i, nrc, rc, hb, boundary):
    has_prev = i > 0
    has_next = i < nrc - 1
    if boundary is not None:
        has_prev = has_prev & (i * rc != boundary[0])
        has_next = has_next & ((i + 1) * rc != boundary[0])
    pad_ref[0:hb, :] = jnp.where(has_prev, p_ref[...], 0.0)
    pad_ref[hb:hb + rc, :] = c_ref[...]
    pad_ref[hb + rc:hb + rc + hb, :] = jnp.where(has_next, n_ref[...], 0.0)


def _shift_plan(keys):
    count = {}
    for s, m in keys:
        k = (s % SUBLANE, m)
        count[k] = count.get(k, 0) + 1
    slots = {}
    for k, n in sorted(count.items(), key=lambda kv: (kv[0][0], str(kv[0][1]))):
        if k != (0, None) and (n >= 2 or k[1] is not None):
            slots[k] = len(slots)
    return slots


def _build_shifted(copies_ref, slots, pad_ref, keys, i, rc, hb, sub):
    for (r, m), slot in slots.items():
        qs = [s - r for s, mk in keys if (s % SUBLANE, mk) == (r, m)]
        lo, hi = hb + min(qs), hb + rc + max(qs)
        for p in range(lo, hi, sub):
            n = min(sub, hi - p)
            v = pad_ref[p + r:p + r + n, :]
            if m is not None:
                t = i * rc - hb + p + r + lax.broadcasted_iota(jnp.int32, (n, 1), 0)
                v = jnp.where(_col_mask(m, t), v, 0.0)
            copies_ref[slot, p:p + n, :] = v


def _read(copies_ref, slots, pad_ref, s, m, row, n):
    k = (s % SUBLANE, m)
    if k in slots:
        q = s - k[0]
        return copies_ref[slots[k], row + q:row + q + n, :]
    return pad_ref[row + s:row + s + n, :]


def _conv_fwd(name, u, col_off, C, w, b, taps, act=False):
    T = u.shape[0]
    rc, ct, hb, sub, nrc, ncc, boundary, taps = _conv_plan(T, C, taps)
    assert col_off % ct == 0
    K = len(taps)
    keys = [(s, None) for s, _ in taps]
    slots = _shift_plan(keys)
    dirs = sorted({m for _, m in taps if m is not None})

    def body(up, uc, un, w_ref, b_ref, *rest):
        y_ref = rest[0]
        pad_ref, copies_ref = rest[-2], rest[-1]
        i = pl.program_id(1)
        _fill_halo(pad_ref, up, uc, un, i, nrc, rc, hb, boundary)
        _build_shifted(copies_ref, slots, pad_ref, keys, i, rc, hb, sub)
        for r0 in range(0, rc, sub):
            acc = jnp.broadcast_to(b_ref[...], (sub, ct))
            for m in [None] + dirs:
                part = None
                for k, (s, mk) in enumerate(taps):
                    if mk != m:
                        continue
                    v = _read(copies_ref, slots, pad_ref, s, None, hb + r0, sub)
                    ok = _seg_ok(boundary, i, rc, r0, sub, s)
                    term = w_ref[k:k + 1, :] * (v if ok is None else jnp.where(ok, v, 0.0))
                    part = term if part is None else part + term
                if part is None:
                    continue
                if m is not None:
                    t = i * rc + r0 + lax.broadcasted_iota(jnp.int32, (sub, 1), 0)
                    part = jnp.where(_col_mask(m, t), part, 0.0)
                acc = acc + part
            y_ref[r0:r0 + sub, :] = acc
            if act:
                rest[1][r0:r0 + sub, :] = _silu(acc)

    n_out = 2 if act else 1
    res = _pcall(
        body, name=name, grid=(ncc, nrc),
        in_specs=_halo_specs(rc, ct, hb, T, col_off // ct) + [pl.BlockSpec((K, ct), lambda j, i: (0, j)),
                                                              pl.BlockSpec((1, ct), lambda j, i: (0, j))],
        out_specs=[pl.BlockSpec((rc, ct), lambda j, i: (i, j))] * n_out,
        out_shape=[jax.ShapeDtypeStruct((T, C), F32)] * n_out,
        scratch_shapes=[pltpu.VMEM((rc + 2 * hb, ct), F32), pltpu.VMEM((max(len(slots), 1), rc + 2 * hb, ct), F32)],
        compiler_params=_cparams(2),
    )(u, u, u, w, b)
    return res if act else res[0]


def _conv_bwd(name, u, col_off, C, w, g, taps, du_dtype=F32):
    T = u.shape[0]
    rc, ct, hb, sub, nrc, ncc, boundary, taps = _conv_plan(T, C, taps)
    K = len(taps)
    u_keys = [(s, None) for s, _ in taps]
    dirs = sorted({m for _, m in taps if m is not None})
    g_keys = [(-s, m) for s, m in taps] + [(0, m) for m in dirs]
    u_slots, g_slots = _shift_plan(u_keys), _shift_plan(g_keys)

    def body(up, uc, un, gp, gc, gn, w_ref, du_ref, dw_ref, db_ref, upad, gpad, ucopies, gcopies):
        i = pl.program_id(1)
        _fill_halo(upad, up, uc, un, i, nrc, rc, hb, boundary)
        _fill_halo(gpad, gp, gc, gn, i, nrc, rc, hb, boundary)
        _build_shifted(ucopies, u_slots, upad, u_keys, i, rc, hb, sub)
        _build_shifted(gcopies, g_slots, gpad, g_keys, i, rc, hb, sub)

        @pl.when(i == 0)
        def _():
            dw_ref[...] = jnp.zeros_like(dw_ref)
            db_ref[...] = jnp.zeros_like(db_ref)

        def fold(v):
            return jnp.sum(v.reshape(sub // SUBLANE, SUBLANE, ct), axis=0)

        dbs = jnp.zeros((SUBLANE, ct), F32)
        for r0 in range(0, rc, sub):
            dbs = dbs + fold(gpad[hb + r0:hb + r0 + sub, :])
            acc = jnp.zeros((sub, ct), F32)
            for k, (s, m) in enumerate(taps):
                v = _read(gcopies, g_slots, gpad, -s, m, hb + r0, sub)
                ok = _seg_ok(boundary, i, rc, r0, sub, -s)
                acc = acc + w_ref[k:k + 1, :] * (v if ok is None else jnp.where(ok, v, 0.0))
            du_ref[r0:r0 + sub, :] = acc.astype(du_ref.dtype)
        db_ref[...] += jnp.sum(dbs, axis=0, keepdims=True)
        for k, (s, m) in enumerate(taps):
            part = jnp.zeros((SUBLANE, ct), F32)
            for r0 in range(0, rc, sub):
                v = _read(ucopies, u_slots, upad, s, None, hb + r0, sub)
                ok = _seg_ok(boundary, i, rc, r0, sub, s)
                part = part + fold(_read(gcopies, g_slots, gpad, 0, m, hb + r0, sub)
                                   * (v if ok is None else jnp.where(ok, v, 0.0)))
            dw_ref[k:k + 1, :] += jnp.sum(part, axis=0, keepdims=True)

    halo_u = _halo_specs(rc, ct, hb, T, col_off // ct)
    halo_g = _halo_specs(rc, ct, hb, T, 0)
    rows = rc + 2 * hb
    return _pcall(
        body, name=name, grid=(ncc, nrc),
        in_specs=halo_u + halo_g + [pl.BlockSpec((K, ct), lambda j, i: (0, j))],
        out_specs=[pl.BlockSpec((rc, ct), lambda j, i: (i, j)), pl.BlockSpec((K, ct), lambda j, i: (0, j)),
                   pl.BlockSpec((1, ct), lambda j, i: (0, j))],
        out_shape=[jax.ShapeDtypeStruct((T, C), du_dtype), jax.ShapeDtypeStruct((K, C), F32),
                   jax.ShapeDtypeStruct((1, C), F32)],
        scratch_shapes=[pltpu.VMEM((rows, ct), F32), pltpu.VMEM((rows, ct), F32),
                        pltpu.VMEM((max(len(u_slots), 1), rows, ct), F32),
                        pltpu.VMEM((max(len(g_slots), 1), rows, ct), F32)],
        compiler_params=_cparams(2),
    )(u, u, u, g, g, g, w)


def _ssd_group(xg, bm, cm, s_in, *per_head, reverse, P):
    R = len(per_head) // 2
    dtrs, a_s = per_head[:R], per_head[R:]
    q, rp = xg.shape
    ii = lax.broadcasted_iota(jnp.int32, (q, q), 0)
    jj = lax.broadcasted_iota(jnp.int32, (q, q), 1)
    causal = (jj >= ii) if reverse else (jj <= ii)
    causal_t = (ii >= jj) if reverse else (ii <= jj)
    eye = ii == jj
    lane = lax.broadcasted_iota(jnp.int32, (1, rp), 1)
    row = lax.broadcasted_iota(jnp.int32, (rp, 1), 0)
    nt = (((1,), (1,)), ((), ()))
    tn = (((0,), (0,)), ((), ()))
    cb = lax.dot_general(cm.astype(BF16), bm.astype(BF16), nt, preferred_element_type=F32)
    dt_x = jnp.zeros((q, rp), F32)
    acum_x = jnp.zeros((q, rp), F32)
    tot_row = jnp.zeros((1, rp), F32)
    tot_col = jnp.zeros((rp, 1), F32)
    wts, lane_masks = [], []
    for r in range(R):
        hm = (lane >= r * P) & (lane < (r + 1) * P)
        hc = (row >= r * P) & (row < (r + 1) * P)
        dt_c = jnp.sum(jnp.where(eye, dtrs[r], 0.0), axis=1, keepdims=True)
        dac = dt_c * a_s[r]
        dar = dtrs[r] * a_s[r]
        acum_c = jnp.sum(jnp.where(causal, dar, 0.0), axis=1, keepdims=True)
        acum_r = jnp.sum(jnp.where(causal_t, dac, 0.0), axis=0, keepdims=True)
        decay = jnp.where(causal, jnp.exp(jnp.where(causal, acum_c - acum_r, 0.0)), 0.0)
        tot = jnp.sum(dac, axis=0, keepdims=True)
        dt_x = jnp.where(hm, dt_c, dt_x)
        acum_x = jnp.where(hm, acum_c, acum_x)
        tot_row = jnp.where(hm, tot, tot_row)
        tot_col = jnp.where(hc, tot, tot_col)
        wts.append((cb * decay).astype(BF16))
        lane_masks.append(hm)
    xdt = xg * dt_x
    xdt_b = xdt.astype(BF16)
    y = jnp.zeros((q, rp), F32)
    for r in range(R):
        y = jnp.where(lane_masks[r], jnp.dot(wts[r], xdt_b, preferred_element_type=F32), y)
    dte = jnp.exp(tot_row - acum_x)
    cs = lax.dot_general((xdt * dte).astype(BF16), bm.astype(BF16), tn, preferred_element_type=F32)
    y = y + lax.dot_general(cm.astype(BF16), s_in.astype(BF16), nt, preferred_element_type=F32) * jnp.exp(acum_x)
    s_out = jnp.exp(tot_col) * s_in + cs
    return y, s_out


def _ssd_maps(NC, ncc, reverse_steps):
    def chunk(d, s):
        if reverse_steps:
            s = NC - 1 - s
        return s if d == 0 else jnp.where(s < ncc, ncc - 1 - s, NC - 1 - s + ncc)

    def lat_chunk(d, s):
        c = chunk(d, s) - ncc
        return jnp.where(c < 0, 0 if d == 0 else NC - ncc - 1, c)

    def step(s):
        return NC - 1 - s if reverse_steps else s

    return chunk, lat_chunk, step


def _ssd_specs(chunk, d, R, Q, N, RP, bo, co):
    return [
        pl.BlockSpec((Q, RP), lambda g, s: (chunk(d, s), g)),
        pl.BlockSpec((Q, N), lambda g, s: (chunk(d, s), bo + g)),
        pl.BlockSpec((Q, N), lambda g, s: (chunk(d, s), co + g)),
        pl.BlockSpec((R, 1, Q), lambda g, s: (g, 0, chunk(d, s))),
        pl.BlockSpec((R, 1, 1), lambda g, s: (g, 0, 0)),
    ]


def _ssd_fwd(xbc, b_off, c_off, dtr, a, P, ncc):
    T = xbc.shape[0]
    H = dtr[0].shape[0]
    N, Q = SSD_STATE, SSD_CHUNK
    NC = T // Q
    G = (c_off - b_off) // N
    R = H // G
    RP = R * P
    chunk, lat_chunk, _ = _ssd_maps(NC, ncc, False)

    def body(*refs):
        s = pl.program_id(1)
        s_ref = refs[-1]

        @pl.when(s == 0)
        def _():
            s_ref[...] = jnp.zeros_like(s_ref)

        for d in range(2):
            x_ref, b_ref, c_ref, dtr_ref, a_ref = refs[5 * d:5 * d + 5]
            y_ref, se_ref = refs[10 + 2 * d:12 + 2 * d]
            s_in = s_ref[d]
            se_ref[...] = s_in
            per_head = [dtr_ref[r] for r in range(R)] + [a_ref[r] for r in range(R)]
            y, s_out = _ssd_group(x_ref[...], b_ref[...], c_ref[...], s_in, *per_head, reverse=d == 1, P=P)
            y_ref[...] = y
            s_ref[d] = s_out

    in_specs, out_specs, out_shape, operands = [], [], [], []
    for d in range(2):
        in_specs += _ssd_specs(chunk, d, R, Q, N, RP, b_off // N, c_off // N)
        operands += [xbc, xbc, xbc, dtr[d], a[d]]
        out_specs += [pl.BlockSpec((Q, RP), functools.partial(lambda g, s, d: (lat_chunk(d, s), g), d=d)),
                      pl.BlockSpec((None, None, RP, N), lambda g, s: (g, s, 0, 0))]
        out_shape += [jax.ShapeDtypeStruct((T - ncc * Q, H * P), F32), jax.ShapeDtypeStruct((G, NC, RP, N), F32)]
    y_f, se_f, y_b, se_b = _pcall(
        body, name="ssd_fwd", grid=(G, NC), in_specs=in_specs, out_specs=out_specs, out_shape=out_shape,
        scratch_shapes=[pltpu.VMEM((2, RP, N), F32)], compiler_params=_cparams(2),
    )(*operands)
    return (y_f, y_b), (se_f, se_b)


def _ssd_bwd(xbc, b_off, c_off, dtr, a, s_enter, dy, P, ncc):
    T = xbc.shape[0]
    H = dtr[0].shape[0]
    N, Q = SSD_STATE, SSD_CHUNK
    NC = T // Q
    G = (c_off - b_off) // N
    R = H // G
    RP = R * P
    chunk, lat_chunk, step = _ssd_maps(NC, ncc, True)
    n_in, n_out = 7, 5

    def body(*refs):
        s = pl.program_id(1)
        ds_ref = refs[-1]

        @pl.when(s == 0)
        def _():
            ds_ref[...] = jnp.zeros_like(ds_ref)

        for d in range(2):
            x_ref, b_ref, c_ref, dtr_ref, a_ref, se_ref, dy_ref = refs[n_in * d:n_in * (d + 1)]
            dx_ref, db_ref, dc_ref, ddtr_ref, da_ref = refs[2 * n_in + n_out * d:2 * n_in + n_out * (d + 1)]
            per_head = [dtr_ref[r] for r in range(R)] + [a_ref[r] for r in range(R)]
            f = functools.partial(_ssd_group, reverse=d == 1, P=P)
            _, vjp = jax.vjp(f, x_ref[...], b_ref[...], c_ref[...], se_ref[...], *per_head)
            is_latent = chunk(d, s) >= ncc
            dy_v = jnp.where(is_latent, dy_ref[...], 0.0)
            grads = vjp((dy_v, ds_ref[d]))
            dx_ref[...] = grads[0]
            db_ref[...] = grads[1]
            dc_ref[...] = grads[2]
            ds_ref[d] = grads[3]
            for r in range(R):
                ddtr_ref[r] = grads[4 + r]
                da_ref[r] = jnp.broadcast_to(grads[4 + R + r], (SUBLANE, LANE))

    in_specs, out_specs, out_shape, operands = [], [], [], []
    for d in range(2):
        in_specs += _ssd_specs(chunk, d, R, Q, N, RP, b_off // N, c_off // N) + [
            pl.BlockSpec((None, None, RP, N), lambda g, s: (g, step(s), 0, 0)),
            pl.BlockSpec((Q, RP), functools.partial(lambda g, s, d: (lat_chunk(d, s), g), d=d)),
        ]
        operands += [xbc, xbc, xbc, dtr[d], a[d], s_enter[d], dy]
    for d in range(2):
        at_chunk = functools.partial(lambda g, s, d: (chunk(d, s), g), d=d)
        out_specs += [
            pl.BlockSpec((Q, RP), at_chunk), pl.BlockSpec((Q, N), at_chunk), pl.BlockSpec((Q, N), at_chunk),
            pl.BlockSpec((R, 1, Q), functools.partial(lambda g, s, d: (g, 0, chunk(d, s)), d=d)),
            pl.BlockSpec((R, SUBLANE, LANE), lambda g, s: (g * NC + s, 0, 0)),
        ]
        out_shape += [
            jax.ShapeDtypeStruct((T, H * P), F32), jax.ShapeDtypeStruct((T, G * N), F32),
            jax.ShapeDtypeStruct((T, G * N), F32), jax.ShapeDtypeStruct((H, 1, T), F32),
            jax.ShapeDtypeStruct((G * NC * R, SUBLANE, LANE), F32),
        ]
    res = _pcall(
        body, name="ssd_bwd", grid=(G, NC), in_specs=in_specs, out_specs=out_specs, out_shape=out_shape,
        scratch_shapes=[pltpu.VMEM((2, RP, N), F32)], compiler_params=_cparams(2),
    )(*operands)
    return res[:n_out], res[n_out:]


def _allgather8(name, v):
    R, C = v.shape

    def body(x_ref, out_ref, send_sems, recv_sems, local_sem):
        x, y, c = lax.axis_index("x"), lax.axis_index("y"), lax.axis_index("c")
        me, sibling = (x, y, c), (x, y, 1 - c)
        chips = [(1 - x, y), (x, 1 - y), (1 - x, 1 - y)]

        def slot(px, py, pc):
            return out_ref.at[4 * px + 2 * py + pc]

        def copy(k, block, to, src=None):
            return pltpu.make_async_remote_copy(
                src_ref=slot(*block) if src is None else src, dst_ref=slot(*block),
                send_sem=send_sems.at[k], recv_sem=recv_sems.at[k], device_id=to, device_id_type=MESH)

        mine = pltpu.make_async_copy(x_ref, slot(*me), local_sem)
        mine.start()
        first = [copy(0, me, sibling, src=x_ref)]
        first += [copy(1 + j, me, (*chip, c), src=x_ref) for j, chip in enumerate(chips)]
        for cp in first:
            cp.start()
        passed = [copy(4 + j, (*chip, c), sibling) for j, chip in enumerate(chips)]
        for j, chip in enumerate(chips):
            copy(1 + j, (*chip, c), me).wait_recv()
            passed[j].start()
        copy(0, sibling, me).wait_recv()
        for j, chip in enumerate(chips):
            copy(4 + j, (*chip, 1 - c), me).wait_recv()
        for cp in first + passed:
            cp.wait_send()
        mine.wait()

    return _pcall(
        body, name=name, out_shape=jax.ShapeDtypeStruct((N_DEV, R, C), v.dtype),
        in_specs=[pl.BlockSpec(memory_space=pltpu.VMEM)], out_specs=pl.BlockSpec(memory_space=pltpu.VMEM),
        scratch_shapes=[pltpu.SemaphoreType.DMA((7,)), pltpu.SemaphoreType.DMA((7,)), pltpu.SemaphoreType.DMA],
        compiler_params=pltpu.CompilerParams(vmem_limit_bytes=VMEM_LIMIT_BYTES),
    )(v)


def _slot(ref, k, axis, size):
    if axis is None:
        return ref.at[k]
    align = LANE if size % LANE == 0 else 2 * SUBLANE
    assert size % align == 0
    return ref.at[(slice(None),) * axis + (pl.ds(pl.multiple_of(k * size, align), size),)]


def _exchange4_start(name, srcs, bcast, dep, axes=None, half=False):
    n = len(srcs)
    axes = list(axes) if axes is not None else [None] * n
    sizes = [None if ax is None else s.shape[ax] for s, ax in zip(srcs, axes)]

    def land_shape(s, ax):
        if not bcast:
            return s.shape
        if half:
            return (N_CHIPS,) + s.shape[1:]
        if ax is None:
            return (N_CHIPS,) + s.shape
        return s.shape[:ax] + (N_CHIPS * s.shape[ax],) + s.shape[ax + 1:]

    lands = [lax.empty(land_shape(s, ax), s.dtype) for s, ax in zip(srcs, axes)]

    def body(*refs):
        src, land = refs[:n], refs[n:2 * n]
        send_sems, recv_sems = refs[2 * n + 1], refs[2 * n + 2]
        token = refs[-1]
        x, y, c = lax.axis_index("x"), lax.axis_index("y"), lax.axis_index("c")
        me = 2 * x + y
        for a in range(n):
            for j, (px, py) in enumerate([(1 - x, y), (x, 1 - y), (1 - x, 1 - y)]):
                pltpu.make_async_remote_copy(
                    src_ref=(src[a].at[c] if half else src[a]) if bcast else src[a].at[2 * px + py],
                    dst_ref=_slot(land[a], me, axes[a], sizes[a]),
                    send_sem=send_sems.at[3 * a + j], recv_sem=recv_sems.at[3 * a + j], device_id=(px, py, c),
                    device_id_type=MESH).start()
        token[...] = jnp.zeros_like(token)

    hbm = pl.BlockSpec(memory_space=pltpu.HBM)
    sem = pl.BlockSpec(memory_space=pltpu.SEMAPHORE)
    outs = _pcall(
        body, name=name,
        out_shape=(pltpu.SemaphoreType.DMA((3 * n,)), pltpu.SemaphoreType.DMA((3 * n,)),
                   *[pltpu.HBM(s.shape, s.dtype) for s in srcs], *[pltpu.HBM(l.shape, l.dtype) for l in lands],
                   jax.ShapeDtypeStruct((SUBLANE, LANE), F32)),
        in_specs=[hbm] * (2 * n) + [pl.BlockSpec(memory_space=pl.ANY)],
        out_specs=(sem, sem, *[hbm] * (2 * n), pl.BlockSpec(memory_space=pltpu.VMEM)),
        input_output_aliases={k: 2 + k for k in range(2 * n)},
        compiler_params=pltpu.CompilerParams(has_side_effects=pltpu.SideEffectType.DATAFLOW_SIDE_EFFECTING),
    )(*[pltpu.with_memory_space_constraint(s, pltpu.HBM) for s in srcs],
      *[pltpu.with_memory_space_constraint(l, pltpu.HBM) for l in lands], dep)
    return (n, bcast, half, axes, sizes, outs[0], outs[1], outs[2:2 + n], outs[2 + n:2 + 2 * n]), outs[-1]


def _exchange4_wait(name, handle, after):
    n, bcast, half, axes, sizes, send_sems, recv_sems, src_thru, land_thru = handle

    def body(*refs):
        src, land = refs[:n], refs[n:2 * n]
        send_sems, recv_sems = refs[2 * n], refs[2 * n + 1]
        x, y, c = lax.axis_index("x"), lax.axis_index("y"), lax.axis_index("c")
        for a in range(n):
            for j, (px, py) in enumerate([(1 - x, y), (x, 1 - y), (1 - x, 1 - y)]):
                pk = 2 * px + py
                copy = pltpu.make_async_remote_copy(
                    src_ref=(src[a].at[c] if half else src[a]) if bcast else src[a].at[pk],
                    dst_ref=_slot(land[a], pk, axes[a], sizes[a]),
                    send_sem=send_sems.at[3 * a + j], recv_sem=recv_sems.at[3 * a + j], device_id=(px, py, c),
                    device_id_type=MESH)
                copy.wait_send()
                copy.wait_recv()

    hbm = pl.BlockSpec(memory_space=pltpu.HBM)
    sem = pl.BlockSpec(memory_space=pltpu.SEMAPHORE)
    outs = _pcall(
        body, name=name,
        out_shape=tuple(pltpu.HBM(t.shape, t.dtype) for t in (*src_thru, *land_thru)),
        in_specs=[hbm] * (2 * n) + [sem, sem, pl.BlockSpec(memory_space=pl.ANY)], out_specs=tuple([hbm] * (2 * n)),
        input_output_aliases={k: k for k in range(2 * n)},
        compiler_params=pltpu.CompilerParams(has_side_effects=pltpu.SideEffectType.DATAFLOW_SIDE_EFFECTING),
    )(*src_thru, *land_thru, send_sems, recv_sems, after)
    return list(outs[:n]), list(outs[n:])


def _tie(name, v, token):
    def body(v_ref, token_ref, o_ref):
        del v_ref, token_ref, o_ref

    any_spec = pl.BlockSpec(memory_space=pl.ANY)
    return _pcall(body, name=name, out_shape=jax.ShapeDtypeStruct(v.shape, v.dtype), in_specs=[any_spec, any_spec],
                  out_specs=any_spec, input_output_aliases={0: 0})(v, token)


def _fill_own(landed, own, me, bcast):
    blk = own if bcast else lax.dynamic_index_in_dim(own, me, 0, keepdims=False)
    return lax.dynamic_update_index_in_dim(landed, blk, me, 0)


def _swap_sibling(name, srcs):
    n = len(srcs)

    def body(*refs):
        src, out = refs[:n], refs[n:2 * n]
        send_sems, recv_sems = refs[2 * n:]
        x, y, c = lax.axis_index("x"), lax.axis_index("y"), lax.axis_index("c")
        copies = []
        for a in range(n):
            rc = pltpu.make_async_remote_copy(
                src_ref=src[a], dst_ref=out[a], send_sem=send_sems.at[a], recv_sem=recv_sems.at[a],
                device_id=(x, y, 1 - c), device_id_type=MESH)
            rc.start()
            copies.append(rc)
        for cp in copies:
            cp.wait()

    any_spec = pl.BlockSpec(memory_space=pl.ANY)
    return _pcall(
        body, name=name, out_shape=[jax.ShapeDtypeStruct(s.shape, s.dtype) for s in srcs],
        in_specs=[any_spec] * n, out_specs=[any_spec] * n,
        scratch_shapes=[pltpu.SemaphoreType.DMA((n,)), pltpu.SemaphoreType.DMA((n,))],
    )(*srcs)


def _mod_fwd(c16, mod_w, mod_b_shard):
    nl, D, S = mod_w.shape

    def body(c_ref, w_ref, b_ref, o_ref):
        s = _silu(c_ref[...]).astype(BF16)
        o_ref[...] = jnp.dot(s, w_ref[...].astype(BF16), preferred_element_type=F32) + b_ref[...]

    return _pcall(
        body, name="mod_fwd", grid=(nl,),
        in_specs=[pl.BlockSpec((16, D), lambda l: (0, 0)), pl.BlockSpec((None, D, S), lambda l: (l, 0, 0)),
                  pl.BlockSpec((None, 1, S), lambda l: (l, 0, 0))],
        out_specs=pl.BlockSpec((None, 16, S), lambda l: (l, 0, 0)),
        out_shape=jax.ShapeDtypeStruct((nl, 16, S), F32), compiler_params=_cparams(1),
    )(c16, mod_w, mod_b_shard)


def _mod_w_update(s16t, dm16, w, m, v):
    nl, D, S = w.shape
    tm = _row_tile(D, 256)

    def body(s_ref, dm_ref, w_ref, m_ref, v_ref, g_ref, dl_ref, nm_ref, nv_ref):
        g = jnp.dot(s_ref[...], dm_ref[...], preferred_element_type=F32, precision=HIGHEST)
        g, dl, nm, nv = _f_adamw(w_ref[...], m_ref[...], v_ref[...], g, jnp.zeros_like(g))
        g_ref[...] = g
        dl_ref[...] = dl
        nm_ref[...] = nm
        nv_ref[...] = nv

    big = pl.BlockSpec((None, tm, S), lambda l, i: (l, i, 0))
    return _pcall(
        body, name="mod_w_update", grid=(nl, D // tm),
        in_specs=[pl.BlockSpec((tm, 16), lambda l, i: (i, 0)), pl.BlockSpec((None, 16, S), lambda l, i: (l, 0, 0)),
                  big, big, big],
        out_specs=[big] * 4, out_shape=[jax.ShapeDtypeStruct(w.shape, F32)] * 4, compiler_params=_cparams(2),
    )(s16t, dm16, w, m, v)


def _size(shape):
    n = 1
    for d in shape:
        n *= d
    return n


def _pack(arrs):
    pieces = []
    for a in arrs:
        flat = a.reshape(-1).astype(F32)
        pieces.append(jnp.pad(flat, (0, _round_up(flat.shape[0], LANE) - flat.shape[0])).reshape(-1, LANE))
    buf = jnp.concatenate(pieces, axis=0)
    return jnp.pad(buf, ((0, _round_up(buf.shape[0], SUBLANE) - buf.shape[0]), (0, 0)))


def _unpack(buf, shapes):
    lead = buf.shape[:-2]
    out, row = [], 0
    for s in shapes:
        n = _size(s)
        rows = _cdiv(n, LANE)
        piece = buf[..., row:row + rows, :].reshape(lead + (rows * LANE,))
        out.append(piece[..., :n].reshape(lead + tuple(s)))
        row += rows
    return out


def _adamw_many(name, ws, ms, vs, gs):
    n = len(ws)

    def body(*refs):
        for k in range(n):
            res = _f_adamw(refs[k][...], refs[n + k][...], refs[2 * n + k][...], refs[3 * n + k][...], 0.0)
            for j in range(4):
                refs[(4 + j) * n + k][...] = res[j]

    vmem = pl.BlockSpec(memory_space=pltpu.VMEM)
    res = _pcall(body, name=name, out_shape=[jax.ShapeDtypeStruct(w.shape, F32) for _ in range(4) for w in ws],
                 in_specs=[vmem] * (4 * n), out_specs=[vmem] * (4 * n))(*ws, *ms, *vs, *gs)
    return [tuple(res[j * n + k] for j in range(4)) for k in range(n)]


SHARD_AXIS = {
    "mod_w": 2, "ssd_w_in": 2, "ssd_conv_w": 2, "ssd_w_out": 1, "conf_w_pw1": 2, "conf_b_pw1": 1, "conf_w_dw": 2,
    "conf_b_dw": 1, "conf_ln_w": 1, "conf_ln_b": 1, "conf_w_pw2": 1, "conf_b_pw2": 1, "ffn_w_up": 2,
    "ffn_conv_w": 3, "ffn_w_down": 1,
}
BIG = ("ssd_w_in", "ssd_w_out", "conf_w_pw1", "conf_w_pw2", "ffn_w_up", "ffn_w_down")
WEIGHTS = ("c_ctx", "mod_w", "mod_b", "norm1_w", "norm2_w", "ssd_w_in", "ssd_conv_w", "ssd_conv_b", "ssd_dt_bias",
           "ssd_a_log", "ssd_d", "ssd_norm_w", "ssd_w_out", "conf_w_pw1", "conf_b_pw1", "conf_w_dw", "conf_b_dw",
           "conf_ln_w", "conf_ln_b", "conf_w_pw2", "conf_b_pw2", "ffn_w_up", "ffn_conv_w", "ffn_conv_b",
           "ffn_w_down", "final_norm_w")
SMALL = tuple(n for n in WEIGHTS if n not in BIG and n != "mod_w")
SMALL_SHARDED = tuple(n for n in SMALL if n in SHARD_AXIS)


def _unshard(stacked, axis):
    return jnp.concatenate([stacked[k] for k in range(N_CHIPS)], axis=axis)


def _to_blocks(full, axis):
    return jnp.stack(jnp.split(full, N_CHIPS, axis=axis))


def _par(v):
    v = v.reshape(-1, v.shape[-1])
    return v[:, None, :]


def kernel(x, c, ctx, c_ctx, mod_w, mod_b, norm1_w, norm2_w, ssd_w_in, ssd_conv_w, ssd_conv_b, ssd_dt_bias, ssd_a_log, ssd_d, ssd_norm_w, ssd_w_out, conf_w_pw1, conf_b_pw1, conf_w_dw, conf_b_dw, conf_ln_w, conf_ln_b, conf_w_pw2, conf_b_pw2, ffn_w_up, ffn_conv_w, ffn_conv_b, ffn_w_down, final_norm_w, loss_target, m_c_ctx, m_mod_w, m_mod_b, m_norm1_w, m_norm2_w, m_ssd_w_in, m_ssd_conv_w, m_ssd_conv_b, m_ssd_dt_bias, m_ssd_a_log, m_ssd_d, m_ssd_norm_w, m_ssd_w_out, m_conf_w_pw1, m_conf_b_pw1, m_conf_w_dw, m_conf_b_dw, m_conf_ln_w, m_conf_ln_b, m_conf_w_pw2, m_conf_b_pw2, m_ffn_w_up, m_ffn_conv_w, m_ffn_conv_b, m_ffn_w_down, m_final_norm_w, v_c_ctx, v_mod_w, v_mod_b, v_norm1_w, v_norm2_w, v_ssd_w_in, v_ssd_conv_w, v_ssd_conv_b, v_ssd_dt_bias, v_ssd_a_log, v_ssd_d, v_ssd_norm_w, v_ssd_w_out, v_conf_w_pw1, v_conf_b_pw1, v_conf_w_dw, v_conf_b_dw, v_conf_ln_w, v_conf_ln_b, v_conf_w_pw2, v_conf_b_pw2, v_ffn_w_up, v_ffn_conv_w, v_ffn_conv_b, v_ffn_w_down, v_final_norm_w):
    given = dict(locals())
    W = {n: given[n] for n in WEIGHTS}
    Mo = {n: given["m_" + n] for n in WEIGHTS}
    Vo = {n: given["v_" + n] for n in WEIGHTS}

    ax, ay, ac = lax.axis_index("x"), lax.axis_index("y"), lax.axis_index("c")
    chip = 2 * ax + ay
    dev = 4 * ax + 2 * ay + ac

    D = x.shape[-1]
    L, Lc = x.shape[1], ctx.shape[1]
    T0 = L + Lc
    H = ssd_a_log.shape[-1]
    DI = ssd_norm_w.shape[-1]
    P = DI // H
    CD = ssd_conv_b.shape[-1]
    N = SSD_STATE
    G = (CD - DI) // (2 * N)
    FH = ffn_conv_b.shape[-1]
    KS = ssd_conv_w.shape[1]
    KC = conf_w_dw.shape[1]
    ncc = Lc // SSD_CHUNK

    shard_b = {n: W[n].astype(BF16) for n in BIG}

    small_shard_shapes = [W[n].shape for n in SMALL_SHARDED]
    f1 = _allgather8("gather_small", _pack([c] + [W[n] for n in SMALL_SHARDED]))
    parts = _unpack(f1, [c.shape] + small_shard_shapes)
    Wf = dict(W)
    for n, p in zip(SMALL_SHARDED, parts[1:]):
        Wf[n] = _unshard(p[::2], SHARD_AXIS[n])
    c16 = jnp.concatenate([parts[0].reshape(N_DEV, D), c_ctx[None, :], jnp.zeros((16 - N_DEV - 1, D), F32)], axis=0)

    S_mod = mod_w.shape[-1]
    mod_b_shard = lax.dynamic_slice_in_dim(mod_b, chip * S_mod, S_mod, axis=1)[:, None, :]
    mod_part = _mod_fwd(c16, mod_w, mod_b_shard)
    f2 = _allgather8("gather_mod", mod_part.reshape(2 * 16, S_mod))
    mods = jnp.concatenate([f2[2 * k].reshape(2, 16, S_mod) for k in range(N_CHIPS)], axis=-1)
    my = lax.dynamic_slice_in_dim(mods, dev, 1, axis=1)[:, 0]
    sh1, sc1, g1, sh2, sc2, g2 = [[my[l, k * D:(k + 1) * D] for l in range(2)] for k in range(6)]
    csh1, csc1 = mods[0, N_DEV, 0:D], mods[0, N_DEV, D:2 * D]

    in_halves = shard_b["ssd_w_in"].reshape(2, D // 2, ssd_w_in.shape[-1])
    gather_a, token = _exchange4_start("gather_w_in_start", [in_halves], True, mods, half=True)
    csc1 = _tie("tie_gather_w_in", csc1, token)

    def full_weight(n, own, landed):
        if landed.ndim == own.ndim:
            ax = SHARD_AXIS[n]
            return lax.dynamic_update_slice_in_dim(landed, own, chip * own.shape[ax], ax)
        return _unshard(_fill_own(landed, own, chip, True), SHARD_AXIS[n])

    xl = x[0]
    rows0 = (ctx[0], xl)
    n1w0, n2w0, n1w1, n2w1 = _par(norm1_w[0]), _par(norm2_w[0]), _par(norm1_w[1]), _par(norm2_w[1])
    sc_seg = jnp.stack([csc1, sc1[0]])[:, None, :]
    sh_seg = jnp.stack([csh1, sh1[0]])[:, None, :]

    a0 = _rw_fwd("l0_modnorm1", _f_modnorm, [], [n1w0, sc_seg, sh_seg], [D], T=T0, seg_rows=(Lc,), head=rows0,
                 out_dtypes=[BF16])
    (own_in,), (landed_in,) = _exchange4_wait("gather_w_in_wait", gather_a, a0)
    mine = _fill_own(landed_in, lax.dynamic_index_in_dim(own_in, ac, 0, keepdims=False), chip, True)
    (theirs,) = _swap_sibling("swap_w_in", [mine])
    top, bottom = jnp.where(ac == 0, mine, theirs), jnp.where(ac == 0, theirs, mine)
    w_in = jnp.concatenate([jnp.concatenate([top[k], bottom[k]], axis=0) for k in range(N_CHIPS)], axis=1)
    landed_in = theirs
    def start_gather(tag, names, dep):
        handle, tok = _exchange4_start("gather_" + tag + "_start", [shard_b[n] for n in names], True, dep,
                                       axes=[1 if SHARD_AXIS[n] == 1 else None for n in names])
        return (names, handle), tok

    def finish_gather(tag, group, after):
        names, handle = group
        return {n: full_weight(n, own, g)
                for n, own, g in zip(names, *_exchange4_wait("gather_" + tag + "_wait", handle, after))}

    gather_b, token = start_gather("mix", ["ssd_w_out", "conf_w_pw1", "conf_w_pw2"], landed_in)
    gather_c, token = start_gather("ffn", ["ffn_w_up", "ffn_w_down"], token)
    a0 = _tie("tie_gather_rest", a0, token)
    proj = _mm(a0, w_in, name="l0_w_in")
    seg_taps = [(k - KS // 2, ("seg", Lc)) for k in range(KS)]
    xbc_pre, xbc = _conv_fwd("l0_conv", proj, DI, CD, Wf["ssd_conv_w"][0], ssd_conv_b, seg_taps, act=True)
    dt_raw = proj[:, DI + CD:]
    dt_bias = _par(ssd_dt_bias.reshape(1, 2 * H))
    dt = _rw_fwd("l0_softplus", _f_softplus, [dt_raw], [dt_bias], [2 * H])
    dt_t = dt.T
    dtr = (dt_t[:H, None, :], dt_t[H:, None, :])
    a_all = -jnp.exp(ssd_a_log.reshape(2, H, 1, 1))
    a_neg = (a_all[0], a_all[1])
    (y_f, y_b), s_enter = _ssd_fwd(xbc, DI, DI + G * N, dtr, a_neg, P, ncc)
    gate_rows = [y_f, y_b, (xbc, 0, DI, Lc), (proj, 0, DI, Lc)]
    d_rep = _par(jnp.repeat(ssd_d[0], P))
    ssd_nw = _par(ssd_norm_w[0])
    yn = _rw_fwd("l0_ssd_gate", _f_ssd_gate, gate_rows, [d_rep, ssd_nw], [DI], T=L, out_dtypes=[BF16])
    Wb = finish_gather("mix", gather_b, yn)
    w_out, w_pw1, w_pw2 = Wb["ssd_w_out"][0], Wb["conf_w_pw1"][0], Wb["conf_w_pw2"][0]
    mix0 = _mm(yn, w_out, name="l0_w_out")
    g1_0, g2_0, g1_1, g2_1 = _par(g1[0]), _par(g2[0]), _par(g1[1]), _par(g2[1])
    h1 = _rw_fwd("l0_res1", _f_gate_res, [xl, mix0], [g1_0], [D])
    Wb = finish_gather("ffn", gather_c, h1)
    w_up, w_dn = Wb["ffn_w_up"], Wb["ffn_w_down"]

    grid_taps = [((i - 1) * GRID_W + (j - 1), (None if j == 1 else ("col", j - 1))) for i in range(3) for j in range(3)]

    def ffn_fwd(l, h, tag):
        a = _rw_fwd(tag + "_modnorm2", _f_modnorm, [h], [_par(norm2_w[l]), _par(sc2[l]), _par(sh2[l])], [D],
                    out_dtypes=[BF16])
        hh = _mm(a, w_up[l], name=tag + "_w_up")
        gc = _conv_fwd(tag + "_ffn_conv", hh, FH, FH, Wf["ffn_conv_w"][l].reshape(9, FH), ffn_conv_b[l][None, :],
                       grid_taps)
        act = _rw_fwd(tag + "_act", _f_ffn_act, [(hh, 0, FH), gc], [], [FH], col_tile=_tile(FH, 1536),
                      out_dtypes=[BF16])
        dn = _mm(act, w_dn[l], name=tag + "_w_down")
        return a, hh, gc, act, dn

    a1, hh0, gc0, act0, dn0 = ffn_fwd(0, h1, "l0")
    h2 = _rw_fwd("l0_res2", _f_gate_res, [h1, dn0], [g2_0], [D])

    a2 = _rw_fwd("l1_modnorm1", _f_modnorm, [h2], [n1w1, _par(sc1[1]), _par(sh1[1])], [D], out_dtypes=[BF16])
    pw = _mm(a2, w_pw1, name="l1_pw1")
    b_pw1 = Wf["conf_b_pw1"][0]
    glu = _rw_fwd("l1_glu", _f_glu, [(pw, 0, D), (pw, D, D)], [_par(b_pw1[:D]), _par(b_pw1[D:])], [D])
    conf_taps = [(k - KC // 2, None) for k in range(KC)]
    cv = _conv_fwd("l1_conv", glu, 0, D, Wf["conf_w_dw"][0], Wf["conf_b_dw"], conf_taps)
    ln_w, ln_b = _par(Wf["conf_ln_w"][0]), _par(Wf["conf_ln_b"][0])
    ls = _rw_fwd("l1_ln_silu", _f_ln_silu, [cv], [ln_w, ln_b], [D], out_dtypes=[BF16])
    p2 = _mm(ls, w_pw2, name="l1_pw2")
    b_pw2 = _par(Wf["conf_b_pw2"][0])
    h3 = _rw_fwd("l1_res1", _f_gate_res_bias, [h2, p2], [g1_1, b_pw2], [D])
    a3, hh1, gc1, act1, dn1 = ffn_fwd(1, h3, "l1")
    h4 = _rw_fwd("l1_res2", _f_gate_res, [h3, dn1], [g2_1], [D])

    fnw = final_norm_w[None, :]
    tgt = loss_target[0]
    loss_local = _loss_fwd(h4, tgt, fnw)[0, 0]
    loss = lax.psum(loss_local, ("x", "y", "c"))

    G_full = {}
    reduces = {}

    def start_reduce(tag, items, dep):
        def blocks_of(g, ax):
            if g.ndim == 3:
                return g
            return g.reshape(N_CHIPS, g.shape[0] // N_CHIPS, g.shape[1]) if ax == 0 else _to_blocks(g, ax)

        blocks = [blocks_of(g, ax).astype(BF16) for _, g, ax in items]
        handle, tok = _exchange4_start("reduce_" + tag + "_start", blocks, False, dep)
        reduces[tag] = ([n for n, _, _ in items], handle)
        return tok
    ones = jnp.ones((L, 1), F32)
    (dh4,), (dfnw,) = _rw_bwd("loss_bwd", _f_loss_rows, [h4, tgt], [_par(final_norm_w)], [ones],
                              row_grad=[True, False], par_grad=[True])
    G_full["final_norm_w"] = dfnw.reshape(D)

    def ffn_bwd(l, h, saved, g2_l, dh_out, tag):
        a, hh, gc, act, dn = saved
        (ddn,), (dg2,) = _rw_bwd(tag + "_res2_bwd", _f_gate, [dn], [g2_l], [dh_out],
                                 row_grad=[True], par_grad=[True], row_dtypes=[BF16])
        dact = _mm(ddn, w_dn[l], tb=True, name=tag + "_w_down_dx")
        dwdn = _mm(act, ddn, ta=True, name=tag + "_w_down_dw", out_dtype=BF16)
        (dval, dgc), _ = _rw_bwd(tag + "_act_bwd", _f_ffn_act, [(hh, 0, FH), gc], [], [dact],
                                 row_grad=[True, True], par_grad=[], col_tile=_tile(FH, 1536), row_dtypes=[BF16, F32])
        dgin, dcw, dcb = _conv_bwd(tag + "_ffn_conv_bwd", hh, FH, FH, Wf["ffn_conv_w"][l].reshape(9, FH), dgc,
                                   grid_taps, du_dtype=BF16)
        dhh = jnp.concatenate([dval, dgin], axis=1)
        da = _mm(dhh, w_up[l], tb=True, name=tag + "_w_up_dx")
        dwup = _mm(a, dhh, ta=True, name=tag + "_w_up_dw", out_dtype=BF16, col_blocks=N_CHIPS)
        (dh,), (dn2w, dsc2, dsh2) = _rw_bwd(
            tag + "_modnorm2_bwd", _f_modnorm, [h], [_par(norm2_w[l]), _par(sc2[l]), _par(sh2[l])], [da],
            row_grad=[True], par_grad=[True, True, True], add=dh_out)
        return dh, dict(w_down=dwdn, w_up=dwup, conv_w=dcw.reshape(3, 3, FH), conv_b=dcb.reshape(FH),
                        n2w=dn2w.reshape(D), sc2=dsc2.reshape(D), sh2=dsh2.reshape(D), g2=dg2.reshape(D))

    dh3, gf1 = ffn_bwd(1, h3, (a3, hh1, gc1, act1, dn1), g2_1, dh4, "l1")
    (dp2,), (dg1_1, db_pw2) = _rw_bwd("l1_res1_bwd", _f_gate_bias, [p2], [g1_1, b_pw2], [dh3],
                                      row_grad=[True], par_grad=[True, True], row_dtypes=[BF16])
    dls = _mm(dp2, w_pw2, tb=True, name="l1_pw2_dx")
    dw_pw2 = _mm(ls, dp2, ta=True, name="l1_pw2_dw", out_dtype=BF16)
    (dcv,), (dln_w, dln_b) = _rw_bwd("l1_ln_silu_bwd", _f_ln_silu, [cv], [ln_w, ln_b], [dls],
                                     row_grad=[True], par_grad=[True, True])
    dglu, dw_dw, db_dw = _conv_bwd("l1_conv_bwd", glu, 0, D, Wf["conf_w_dw"][0], dcv, conf_taps)
    (dpa, dpg), (dba, dbg) = _rw_bwd("l1_glu_bwd", _f_glu, [(pw, 0, D), (pw, D, D)],
                                     [_par(b_pw1[:D]), _par(b_pw1[D:])], [dglu],
                                     row_grad=[True, True], par_grad=[True, True], row_dtypes=[BF16, BF16])
    dpw = jnp.concatenate([dpa, dpg], axis=1)
    da2 = _mm(dpw, w_pw1, tb=True, name="l1_pw1_dx")
    dw_pw1 = _mm(a2, dpw, ta=True, name="l1_pw1_dw", out_dtype=BF16, col_blocks=N_CHIPS)
    (dh2,), (dn1w1, dsc1_1, dsh1_1) = _rw_bwd(
        "l1_modnorm1_bwd", _f_modnorm, [h2], [n1w1, _par(sc1[1]), _par(sh1[1])], [da2],
        row_grad=[True], par_grad=[True, True, True], add=dh3)
    G_full["conf_b_pw2"] = db_pw2.reshape(1, D)
    G_full["conf_ln_w"], G_full["conf_ln_b"] = dln_w.reshape(1, D), dln_b.reshape(1, D)
    G_full["conf_w_dw"], G_full["conf_b_dw"] = dw_dw[None], db_dw.reshape(1, D)
    G_full["conf_b_pw1"] = jnp.concatenate([dba.reshape(1, D), dbg.reshape(1, D)], axis=1)

    token = start_reduce("l1", [("conf_w_pw2", dw_pw2, 0), ("conf_w_pw1", dw_pw1, 1), ("ffn_w_up1", gf1["w_up"], 1),
                                ("ffn_w_down1", gf1["w_down"], 0)], dw_pw2)
    dh2 = _tie("tie_reduce_l1", dh2, token)
    dh1, gf0 = ffn_bwd(0, h1, (a1, hh0, gc0, act0, dn0), g2_0, dh2, "l0")
    G_full["ffn_conv_w"] = jnp.stack([gf0["conv_w"], gf1["conv_w"]])
    G_full["ffn_conv_b"] = jnp.stack([gf0["conv_b"], gf1["conv_b"]])

    (dmix,), (dg1_0,) = _rw_bwd("l0_res1_bwd", _f_gate, [mix0], [g1_0], [dh1],
                                row_grad=[True], par_grad=[True], row_dtypes=[BF16])
    dyn = _mm(dmix, w_out, tb=True, name="l0_w_out_dx")
    dw_out = _mm(yn, dmix, ta=True, name="l0_w_out_dw", out_dtype=BF16)
    token = start_reduce("l0", [("ffn_w_up0", gf0["w_up"], 1), ("ffn_w_down0", gf0["w_down"], 0),
                                ("ssd_w_out", dw_out, 0)], dw_out)
    dyn = _tie("tie_reduce_l0", dyn, token)
    (dy_lat, dxs_gate, dz_lat), (dd_rep, dssd_nw) = _rw_bwd(
        "l0_ssd_gate_bwd", _f_ssd_gate, gate_rows, [d_rep, ssd_nw], [dyn],
        row_grad=[True, False, True, True], par_grad=[True, True], T=L, row_dtypes=[F32, F32, BF16])
    g_f, g_b = _ssd_bwd(xbc, DI, DI + G * N, dtr, a_neg, s_enter, dy_lat, P, ncc)
    silu_bwd = functools.partial(_rw_bwd, f=_silu, pars=[], row_grad=[True], par_grad=[], T=T0)
    (dxs_pre,), _ = silu_bwd("l0_silu_bwd_x", rows=[(xbc_pre, 0, DI)], cot_fn=lambda p, q, r: p + q + r,
                             cots=[g_f[0], g_b[0], (dxs_gate, 0, DI, -Lc)],
                             col_tile=_tile(DI, 1024))
    (db_pre,), _ = silu_bwd("l0_silu_bwd_b", rows=[(xbc_pre, DI, G * N)], cot_fn=lambda p, q: p + q,
                            cots=[g_f[1], g_b[1]], col_tile=_tile(G * N, 1024))
    (dc_pre,), _ = silu_bwd("l0_silu_bwd_c", rows=[(xbc_pre, DI + G * N, G * N)], cot_fn=lambda p, q: p + q,
                            cots=[g_f[2], g_b[2]], col_tile=_tile(G * N, 1024))
    conv_w0 = Wf["ssd_conv_w"][0]
    pieces = []
    for tag, off, width, g_pre in (("x", 0, DI, dxs_pre), ("b", DI, G * N, db_pre), ("c", DI + G * N, G * N, dc_pre)):
        pieces.append(_conv_bwd("l0_conv_bwd_" + tag, proj, DI + off, width, conv_w0[:, off:off + width], g_pre,
                                seg_taps, du_dtype=BF16))
    dconv_in = [p[0] for p in pieces]
    dcw0 = jnp.concatenate([p[1] for p in pieces], axis=1)
    dcb0 = jnp.concatenate([p[2] for p in pieces], axis=1)
    ddt = jnp.concatenate([g_f[3][:, 0, :].T, g_b[3][:, 0, :].T], axis=1)
    (ddt_raw,), (ddt_bias,) = _rw_bwd("l0_softplus_bwd", _f_softplus, [dt_raw], [dt_bias], [ddt],
                                      row_grad=[True], par_grad=[True], row_dtypes=[BF16])
    dproj = jnp.concatenate([jnp.pad(dz_lat, ((Lc, 0), (0, 0))), *dconv_in, ddt_raw], axis=1)
    da0 = _mm(dproj, w_in, tb=True, name="l0_w_in_dx")
    dw_in = _mm(a0, dproj, ta=True, name="l0_w_in_dw", out_dtype=BF16)
    token = start_reduce("in", [("ssd_w_in", dw_in, 1)], dw_in)
    da0 = _tie("tie_reduce_in", da0, token)
    (dhcat,), (dn1w0, dsc_seg, dsh_seg) = _rw_bwd(
        "l0_modnorm1_bwd", _f_modnorm, [], [n1w0, sc_seg, sh_seg], [da0], T=T0, head=rows0,
        row_grad=[True], par_grad=[True, True, True], seg_rows=(Lc,), add=(dh1, 0, D, -Lc))
    grad_x = dhcat[Lc:][None]

    da_heads = jnp.stack([g[4][:, 0, 0].reshape(G, T0 // SSD_CHUNK, H // G).sum(axis=1).reshape(H)
                          for g in (g_f, g_b)])[None]
    G_full["ssd_a_log"] = da_heads * (-jnp.exp(ssd_a_log))
    G_full["ssd_dt_bias"] = ddt_bias.reshape(1, 2, H)
    G_full["ssd_d"] = dd_rep.reshape(H, P).sum(axis=1)[None]
    G_full["ssd_norm_w"] = dssd_nw.reshape(1, DI)
    G_full["ssd_conv_w"], G_full["ssd_conv_b"] = dcw0[None], dcb0.reshape(1, CD)
    G_full["norm1_w"] = jnp.stack([dn1w0.reshape(D), dn1w1.reshape(D)])
    G_full["norm2_w"] = jnp.stack([gf0["n2w"], gf1["n2w"]])

    zD = jnp.zeros((D,), F32)
    dm_own = jnp.stack([
        jnp.concatenate([dsh_seg[1, 0], dsc_seg[1, 0], dg1_0.reshape(D), gf0["sh2"], gf0["sc2"], gf0["g2"]]),
        jnp.concatenate([dsh1_1.reshape(D), dsc1_1.reshape(D), dg1_1.reshape(D), gf1["sh2"], gf1["sc2"], gf1["g2"]]),
    ])
    dmc_own = jnp.concatenate([dsh_seg[0, 0], dsc_seg[0, 0], zD, zD, zD, zD])

    out = {}

    def finish_reduce(tags, after, swap_name):
        partial = {}
        for tag in tags:
            names, handle = reduces[tag]
            blocks, landed = _exchange4_wait("reduce_" + tag + "_wait", handle, after)
            for n, blk, own in zip(names, landed, blocks):
                r = _fill_own(blk, own, chip, False)
                partial[n] = _sum_leading("sum4_" + n, r.reshape(N_CHIPS, -1, r.shape[-1]),
                                          (0, 1, 2, 3)).reshape(r.shape[1:])
        for n in ("ffn_w_up", "ffn_w_down"):
            if n + "0" in partial:
                partial[n] = jnp.stack([partial.pop(n + "0"), partial.pop(n + "1")])
        names = [n for n in BIG if n in partial]
        mine = [partial[n].reshape(W[n].shape) for n in names]
        for n, own, sib in zip(names, mine, _swap_sibling(swap_name, mine)):
            out[n] = _adamw("adamw_" + n, W[n], Mo[n], Vo[n], own, sib)
        return names

    early = finish_reduce(["l1", "l0"], dhcat, "swap_grads_early")

    small_sum_names = [n for n in SMALL if n not in ("c_ctx", "mod_b")]
    sum_part = [G_full[n] for n in small_sum_names] + [dmc_own]
    packed = _tie("tie_small_grads", _pack(sum_part + [dm_own]), out[early[-1]][1])
    gat = _allgather8("gather_small_grads", packed)
    total = _sum_leading("sum_small_grads", gat, tuple(range(N_DEV)))
    summed = _unpack(total, [a.shape for a in sum_part])
    Gs = dict(zip(small_sum_names, summed[:-1]))
    dmc_tot = summed[-1]
    dm_all = _unpack(gat, [a.shape for a in sum_part] + [dm_own.shape])[-1].transpose(1, 0, 2)
    dm16 = jnp.concatenate([dm_all, jnp.stack([dmc_tot, jnp.zeros_like(dmc_tot)])[:, None, :],
                            jnp.zeros((2, 16 - N_DEV - 1, 6 * D), F32)], axis=1)
    Gs["mod_b"] = _sum_leading("sum_mod_b", dm16.transpose(1, 0, 2).reshape(16, 2 * 6 * D // LANE, LANE),
                               tuple(range(N_DEV + 1))).reshape(2, 6 * D)

    dm16_shard = lax.dynamic_slice_in_dim(dm16, chip * S_mod, S_mod, axis=2)
    ds16 = _mm(dm16_shard[0], mod_w[0], tb=True, precision=HIGHEST, name="c_ctx_dx")
    sig = jax.nn.sigmoid(c_ctx)
    dcc_part = ds16[N_DEV] * (sig * (1.0 + c_ctx * (1.0 - sig)))
    gat_cc = _allgather8("gather_c_ctx_grad", _pack([dcc_part]))
    Gs["c_ctx"] = _sum_leading("sum_c_ctx_grad", gat_cc, (0, 2, 4, 6)).reshape(-1)[:D]

    s16t = _silu(c16).T
    out["mod_w"] = _mod_w_update(s16t, dm16_shard, mod_w, m_mod_w, v_mod_w)
    finish_reduce(["in"], out["mod_w"][0], "swap_grads_late")

    def own(n, full):
        if n in SHARD_AXIS:
            size = W[n].shape[SHARD_AXIS[n]]
            return lax.dynamic_slice_in_dim(full, chip * size, size, axis=SHARD_AXIS[n])
        return full

    def two_d(a):
        return a.reshape(1, -1) if a.ndim == 1 else a

    g_small = [own(n, Gs[n].reshape(Wf[n].shape)) for n in SMALL]
    res = _adamw_many("adamw_small", [two_d(W[n]) for n in SMALL], [two_d(Mo[n]) for n in SMALL],
                      [two_d(Vo[n]) for n in SMALL], [two_d(g) for g in g_small])
    for n, r in zip(SMALL, res):
        out[n] = tuple(t.reshape(W[n].shape) for t in r)

    grads = [out[n][0] for n in WEIGHTS]
    deltas = [out[n][1] for n in WEIGHTS]
    new_m = [out[n][2] for n in WEIGHTS]
    new_v = [out[n][3] for n in WEIGHTS]
    return (loss, grad_x, *grads, *deltas, *new_m, *new_v)
```

```python
import functools

import jax
import jax.numpy as jnp
from jax import lax
from jax.experimental import pallas as pl
from jax.experimental.pallas import tpu as pltpu

F32 = jnp.float32
BF16 = jnp.bfloat16
MESH = pl.DeviceIdType.MESH
HIGHEST = lax.Precision.HIGHEST

VMEM_LIMIT_BYTES = 48 * 1024 * 1024
LANE = 128
SUBLANE = 8

SSD_STATE = 128
SSD_CHUNK = 128
GRID_W = 64
EPS = 1e-6
N_CHIPS = 4
N_DEV = 8

ADAM_LR = 0.001
ADAM_B1 = 0.9
ADAM_B2 = 0.999
ADAM_EPS = 1e-08
ADAM_WD = 0.01
ADAM_STEP = 10


def _pcall(body, **kw):
    return pl.pallas_call(body, **kw)


def _cparams(n_grid):
    return pltpu.CompilerParams(dimension_semantics=("arbitrary",) * n_grid, vmem_limit_bytes=VMEM_LIMIT_BYTES)


def _cdiv(a, b):
    return -(-a // b)


def _round_up(a, b):
    return _cdiv(a, b) * b


def _tile(n, cap):
    if n <= cap:
        return n
    best = None
    for t in range(LANE, cap + 1, LANE):
        if n % t == 0:
            best = t
    if best is None:
        npad = _round_up(n, LANE)
        for t in range(LANE, cap + 1, LANE):
            if npad % t == 0:
                best = t
    return best


def _row_tile(n, cap, also=()):
    best = None
    for step in (2 * SUBLANE, SUBLANE):
        for t in range(step, min(cap, n) + 1, step):
            if n % t == 0 and all(a % t == 0 for a in also):
                best = t
        if best is not None:
            break
    assert best is not None, (n, cap, also)
    return best


def _silu(v):
    return v * jax.nn.sigmoid(v)


def _mm(a, b, *, name, ta=False, tb=False, precision=None, cap=1024, out_dtype=F32, col_blocks=None):
    M, K = (a.shape[1], a.shape[0]) if ta else a.shape
    N = b.shape[0] if tb else b.shape[1]
    assert K == (b.shape[1] if tb else b.shape[0]), (a.shape, b.shape, ta, tb)
    tm, tk = _tile(M, cap), _tile(K, cap + cap // 2)
    tn = _tile(N if col_blocks is None else N // col_blocks, cap + cap // 2)
    nm, nn, nk = _cdiv(M, tm), _cdiv(N, tn), _cdiv(K, tk)
    k_tail = K % tk
    exact = precision is not None

    def body(a_ref, b_ref, o_ref, acc_ref):
        k = pl.program_id(2)

        @pl.when(k == 0)
        def _():
            acc_ref[...] = jnp.zeros_like(acc_ref)

        av = a_ref[...]
        bv = b_ref[...]
        if k_tail:
            lim = K - k * tk
            ka = lax.broadcasted_iota(jnp.int32, av.shape, 0 if ta else 1)
            kb = lax.broadcasted_iota(jnp.int32, bv.shape, 1 if tb else 0)
            av = jnp.where(ka < lim, av, jnp.zeros_like(av))
            bv = jnp.where(kb < lim, bv, jnp.zeros_like(bv))
        if exact:
            av = av.astype(F32)
            bv = bv.astype(F32)
        else:
            av = av.astype(BF16)
            bv = bv.astype(BF16)
        dn = (((0 if ta else 1,), (1 if tb else 0,)), ((), ()))
        acc_ref[...] += lax.dot_general(av, bv, dn, preferred_element_type=F32, precision=precision)

        @pl.when(k == nk - 1)
        def _():
            o_ref[...] = acc_ref[...].astype(o_ref.dtype)

    a_spec = pl.BlockSpec((tk, tm), lambda i, j, k: (k, i)) if ta else pl.BlockSpec((tm, tk), lambda i, j, k: (i, k))
    b_spec = pl.BlockSpec((tn, tk), lambda i, j, k: (j, k)) if tb else pl.BlockSpec((tk, tn), lambda i, j, k: (k, j))
    if col_blocks is None:
        out_spec = pl.BlockSpec((tm, tn), lambda i, j, k: (i, j))
        out_shape = jax.ShapeDtypeStruct((M, N), out_dtype)
    else:
        per = (N // col_blocks) // tn
        assert per * tn * col_blocks == N, (N, col_blocks, tn)
        out_spec = pl.BlockSpec((None, tm, tn), lambda i, j, k: (j // per, i, j % per))
        out_shape = jax.ShapeDtypeStruct((col_blocks, M, N // col_blocks), out_dtype)
    return _pcall(
        body, name=name, grid=(nm, nn, nk), in_specs=[a_spec, b_spec], out_specs=out_spec, out_shape=out_shape,
        scratch_shapes=[pltpu.VMEM((tm, tn), F32)], compiler_params=_cparams(3),
    )(a, b)


def _norm_rows(rows):
    out = []
    for r in rows:
        if not isinstance(r, tuple):
            r = (r,)
        arr, off, width, roff = (r + (0, None, 0)[len(r) - 1:])
        out.append((arr, off, width if width is not None else arr.shape[1], roff))
    return out


def _rw_plan(T, rows, pars, seg_rows, col_tile, tm_cap):
    widths = [r[2] for r in rows]
    wmax = max(widths + [p.shape[-1] for p in pars] + [1])
    if col_tile is not None:
        assert all(w == widths[0] for w in widths) and all(p.shape[-1] == widths[0] for p in pars)
        ncol = widths[0] // col_tile
        assert ncol * col_tile == widths[0]
        wmax = col_tile
    else:
        ncol = 1
    cap = tm_cap if tm_cap is not None else max(SUBLANE, min(512, (512 * 1024) // wmax))
    tm = _row_tile(T, cap, also=tuple(seg_rows) + tuple(abs(r[3]) for r in rows if r[3]))
    bounds = tuple(s // tm for s in seg_rows)
    return widths, ncol, tm, bounds


def _rw_specs(rows, pars, ncol, tm, bounds, col_tile):
    def seg(i):
        s = 0
        for b in bounds:
            s = s + (i >= b).astype(jnp.int32)
        return s

    specs = []
    for arr, off, w, roff in rows:
        bw = col_tile if col_tile is not None else w
        assert off % bw == 0 and roff % tm == 0, (off, bw, roff, tm)
        specs.append(pl.BlockSpec((tm, bw), functools.partial(
            lambda j, i, ob, rb, last: (jnp.clip(i + rb, 0, last), ob + j),
            ob=off // bw, rb=roff // tm, last=arr.shape[0] // tm - 1)))
    for p in pars:
        bw = col_tile if col_tile is not None else p.shape[-1]
        if p.shape[0] > 1:
            specs.append(pl.BlockSpec((None, 1, bw), lambda j, i: (seg(i), 0, j)))
        else:
            specs.append(pl.BlockSpec((None, 1, bw), lambda j, i: (0, 0, j)))
    return specs, seg


def _head_rows(head):
    top, bottom = head
    return [(top, 0, None, 0), (bottom, 0, None, -top.shape[0])]


def _rw_fwd(name, f, rows, pars, out_widths, *, T=None, seg_rows=(), col_tile=None, tm_cap=None, out_dtypes=None,
            head=None):
    rows = _norm_rows((_head_rows(head) if head else []) + list(rows))
    T = rows[0][0].shape[0] if T is None else T
    widths, ncol, tm, bounds = _rw_plan(T, rows, pars, seg_rows, col_tile, tm_cap)
    in_specs, _ = _rw_specs(rows, pars, ncol, tm, bounds, col_tile)
    nr, npar, nout = len(rows), len(pars), len(out_widths)

    def body(*refs):
        vals = [r[...] for r in refs[:nr + npar]]
        if head:
            vals = [jnp.where(pl.program_id(1) < head[0].shape[0] // tm, vals[0], vals[1])] + vals[2:]
        outs = f(*vals)
        if not isinstance(outs, (tuple, list)):
            outs = (outs,)
        for o_ref, o in zip(refs[nr + npar:], outs):
            o_ref[...] = o.astype(o_ref.dtype)

    out_specs = [pl.BlockSpec((tm, col_tile if col_tile is not None else w), lambda j, i: (i, j)) for w in out_widths]
    res = _pcall(
        body, name=name, grid=(ncol, T // tm), in_specs=in_specs, out_specs=out_specs,
        out_shape=[jax.ShapeDtypeStruct((T, w), dt) for w, dt in zip(out_widths, out_dtypes or [F32] * nout)],
        compiler_params=_cparams(2),
    )(*[r[0] for r in rows], *pars)
    return res if nout > 1 else res[0]


def _rw_bwd(name, f, rows, pars, cots, *, row_grad, par_grad, T=None, seg_rows=(), col_tile=None, tm_cap=None,
            add=None, cot_fn=None, row_dtypes=None, head=None):
    rows = _norm_rows((_head_rows(head) if head else []) + list(rows))
    cots = _norm_rows(cots)
    T = rows[0][0].shape[0] if T is None else T
    extra = _norm_rows([add]) if add is not None else []
    all_rows = rows + cots + extra
    widths, ncol, tm, bounds = _rw_plan(T, all_rows, pars, seg_rows, col_tile, tm_cap)
    in_specs, seg = _rw_specs(all_rows, pars, ncol, tm, bounds, col_tile)
    nr, nc, ne, npar = len(rows), len(cots), len(extra), len(pars)
    skip = 1 if head else 0
    widths = widths[skip:]
    nrf = nr - skip
    row_idx = [k for k in range(nrf) if row_grad[k]]
    par_idx = [k for k in range(npar) if par_grad[k]]

    def body(*refs):
        i = pl.program_id(1)

        def zero_before(vals, ops):
            return [jnp.where(i + c[3] // tm >= 0, v, jnp.zeros_like(v)) if c[3] < 0 else v for v, c in zip(vals, ops)]

        row_vals = [r[...] for r in refs[:nr]]
        if head:
            row_vals = [jnp.where(i < head[0].shape[0] // tm, row_vals[0], row_vals[1])] + row_vals[2:]
        cot_vals = zero_before([r[...] for r in refs[nr:nr + nc]], cots)
        add_vals = zero_before([r[...] for r in refs[nr + nc:nr + nc + ne]], extra)
        par_vals = [r[...] for r in refs[nr + nc + ne:nr + nc + ne + npar]]
        out_refs = refs[nr + nc + ne + npar:]
        outs, vjp = jax.vjp(f, *row_vals, *par_vals)
        if cot_fn is not None:
            cot_vals = cot_fn(*cot_vals)
            if not isinstance(cot_vals, (tuple, list)):
                cot_vals = (cot_vals,)
        if isinstance(outs, (tuple, list)):
            grads = vjp(tuple(c.astype(o.dtype) for c, o in zip(cot_vals, outs)))
        else:
            grads = vjp(cot_vals[0].astype(outs.dtype))
        first_seg = i == 0
        for b in bounds:
            first_seg = first_seg | (i == b)
        for n, k in enumerate(row_idx):
            g = grads[k]
            if n == 0 and add_vals:
                g = g + add_vals[0]
            out_refs[n][...] = g.astype(out_refs[n].dtype)
        for n, k in enumerate(par_idx):
            g = grads[nrf + k]
            o_ref = out_refs[len(row_idx) + n]
            first = first_seg if pars[k].shape[0] > 1 else (i == 0)

            @pl.when(first)
            def _(o_ref=o_ref, g=g):
                o_ref[...] = g

            @pl.when(jnp.logical_not(first))
            def _(o_ref=o_ref, g=g):
                o_ref[...] += g

    out_specs, out_shape = [], []
    for k in row_idx:
        w = widths[k]
        out_specs.append(pl.BlockSpec((tm, col_tile if col_tile is not None else w), lambda j, i: (i, j)))
        out_shape.append(jax.ShapeDtypeStruct((T, w), row_dtypes[len(out_shape)] if row_dtypes else F32))
    for k in par_idx:
        p = pars[k]
        bw = col_tile if col_tile is not None else p.shape[-1]
        if p.shape[0] > 1:
            out_specs.append(pl.BlockSpec((None, 1, bw), lambda j, i: (seg(i), 0, j)))
        else:
            out_specs.append(pl.BlockSpec((None, 1, bw), lambda j, i: (0, 0, j)))
        out_shape.append(jax.ShapeDtypeStruct(p.shape, F32))
    res = _pcall(
        body, name=name, grid=(ncol, T // tm), in_specs=in_specs, out_specs=out_specs, out_shape=out_shape,
        compiler_params=_cparams(2),
    )(*[r[0] for r in all_rows], *pars)
    return list(res[:len(row_idx)]), list(res[len(row_idx):])


def _f_modnorm(h, w, sc, sh):
    y = h * lax.rsqrt(jnp.mean(h * h, axis=-1, keepdims=True) + EPS)
    return (y * w) * (1.0 + sc) + sh


def _f_gate_res(h, y, g):
    return h + g * y


def _f_gate_res_bias(h, y, g, b):
    return h + g * (y + b)


def _f_gate(y, g):
    return g * y


def _f_gate_bias(y, g, b):
    return g * (y + b)


def _f_ffn_act(val, gate):
    return _silu(gate) * val


def _f_softplus(raw, bias):
    v = raw + bias
    return jnp.maximum(v, 0.0) + jnp.log(1.0 + jnp.exp(-jnp.abs(v)))


def _f_ssd_gate(yf, yb, xs, z, d_rep, nw):
    y = (yf + yb + d_rep * xs) * _silu(z)
    return (y * lax.rsqrt(jnp.mean(y * y, axis=-1, keepdims=True) + EPS)) * nw


def _f_glu(a, g, ba, bg):
    return (a + ba) * jax.nn.sigmoid(g + bg)


def _f_ln_silu(h, w, b):
    mu = jnp.mean(h, axis=-1, keepdims=True)
    d = h - mu
    y = d * lax.rsqrt(jnp.mean(d * d, axis=-1, keepdims=True) + EPS)
    return _silu(y * w + b)


def _f_loss_rows(h, t, w):
    y = (h * lax.rsqrt(jnp.mean(h * h, axis=-1, keepdims=True) + EPS)) * w
    e = y - t
    return 0.5 * jnp.mean(e * e, axis=-1, keepdims=True)


def _f_adamw(w, m, v, ga, gb):
    g = ga + gb
    m = ADAM_B1 * m + (1.0 - ADAM_B1) * g
    v = ADAM_B2 * v + (1.0 - ADAM_B2) * (g * g)
    m_hat = m / (1.0 - ADAM_B1 ** ADAM_STEP)
    v_hat = v / (1.0 - ADAM_B2 ** ADAM_STEP)
    delta = -ADAM_LR * (m_hat / (jnp.sqrt(v_hat) + ADAM_EPS) + ADAM_WD * w)
    return g, delta, m, v


def _adamw(name, w, m, v, ga, gb):
    shape = w.shape
    c = shape[-1]
    two_d = [t.reshape(-1, c) for t in (w, m, v, ga, gb)]
    rows = two_d[0].shape[0]
    pad = _round_up(rows, SUBLANE) - rows
    if pad:
        two_d = [jnp.pad(t, ((0, pad), (0, 0))) for t in two_d]
    outs = _rw_fwd(name, _f_adamw, two_d, [], [c] * 4)
    return tuple(o[:rows].reshape(shape) for o in outs)


def _sum_leading(name, x, idxs):
    _, R, C = x.shape
    tm = _row_tile(R, max(SUBLANE, min(512, (512 * 1024) // C)))

    def body(x_ref, o_ref):
        acc = x_ref[idxs[0]].astype(F32)
        for k in idxs[1:]:
            acc = acc + x_ref[k].astype(F32)
        o_ref[...] = acc

    return _pcall(
        body, name=name, grid=(R // tm,), in_specs=[pl.BlockSpec((x.shape[0], tm, C), lambda i: (0, i, 0))],
        out_specs=pl.BlockSpec((tm, C), lambda i: (i, 0)), out_shape=jax.ShapeDtypeStruct((R, C), F32),
        compiler_params=_cparams(1),
    )(x)


def _loss_fwd(h, t, w):
    T, D = h.shape
    tm = _row_tile(T, 256)

    def body(h_ref, t_ref, w_ref, o_ref):
        i = pl.program_id(0)
        part = jnp.sum(_f_loss_rows(h_ref[...], t_ref[...], w_ref[...]), axis=0, keepdims=True)
        part = jnp.broadcast_to(part, (1, LANE))

        @pl.when(i == 0)
        def _():
            o_ref[...] = part

        @pl.when(i > 0)
        def _():
            o_ref[...] += part

    return _pcall(
        body, name="loss_fwd", grid=(T // tm,),
        in_specs=[pl.BlockSpec((tm, D), lambda i: (i, 0)), pl.BlockSpec((tm, D), lambda i: (i, 0)),
                  pl.BlockSpec((1, D), lambda i: (0, 0))],
        out_specs=pl.BlockSpec((1, LANE), lambda i: (0, 0)), out_shape=jax.ShapeDtypeStruct((1, LANE), F32),
        compiler_params=_cparams(1),
    )(h, t, w)


CONV_ROWS = 256
CONV_ROWS_FEW_TAPS = 1024
CONV_ACC_ELEMS = 16384


def _col_mask(arg, t):
    col = jnp.bitwise_and(t, GRID_W - 1)
    return (col != 0) if arg < 0 else (col != GRID_W - 1)


def _conv_plan(T, C, taps):
    seg = [m[1] for _, m in taps if m is not None and m[0] == "seg"]
    cap = CONV_ROWS_FEW_TAPS if len(taps) <= 9 else CONV_ROWS
    rc = next(r for r in (1024, 768, 512, 256, LANE) if r <= cap and T % r == 0)
    ct = next((t for t in (512, 256, LANE) if C % t == 0), C)
    reach = max(abs(s) for s, _ in taps)
    hb = next(h for h in (8, 16, 32, 64, 128, 256) if h >= reach and rc % h == 0)
    sub = max(2 * SUBLANE, min(rc, CONV_ACC_ELEMS // ct))
    boundary = None
    if seg:
        inside = seg[0] % rc
        boundary = (seg[0], (inside - reach, inside + reach) if inside else None)
    taps = [(s, None if (m is None or m[0] == "seg") else m[1]) for s, m in taps]
    return rc, ct, hb, sub, T // rc, C // ct, boundary, taps


def _seg_ok(boundary, i, rc, r0, n, s):
    if boundary is None or boundary[1] is None or s == 0 or r0 + n <= boundary[1][0] or r0 >= boundary[1][1]:
        return None
    t = i * rc + r0 + lax.broadcasted_iota(jnp.int32, (n, 1), 0)
    return (t >= boundary[0]) == ((t + s) >= boundary[0])


def _halo_specs(rc, ct, hb, T, off_blocks):
    per = rc // hb
    last = T // hb - 1
    prev = pl.BlockSpec((hb, ct), lambda j, i: (jnp.maximum(i * per - 1, 0), off_blocks + j))
    cur = pl.BlockSpec((rc, ct), lambda j, i: (i, off_blocks + j))
    nxt = pl.BlockSpec((hb, ct), lambda j, i: (jnp.minimum((i + 1) * per, last), off_blocks + j))
    return [prev, cur, nxt]


def _fill_halo(pad_ref, p_ref, c_ref, n_ref, i, nrc, rc, hb, boundary):
    has_prev = i > 0
    has_next = i < nrc - 1
    if boundary is not None:
        has_prev = has_prev & (i * rc != boundary[0])
        has_next = has_next & ((i + 1) * rc != boundary[0])
    pad_ref[0:hb, :] = jnp.where(has_prev, p_ref[...], 0.0)
    pad_ref[hb:hb + rc, :] = c_ref[...]
    pad_ref[hb + rc:hb + rc + hb, :] = jnp.where(has_next, n_ref[...], 0.0)


def _shift_plan(keys):
    count = {}
    for s, m in keys:
        k = (s % SUBLANE, m)
        count[k] = count.get(k, 0) + 1
    slots = {}
    for k, n in sorted(count.items(), key=lambda kv: (kv[0][0], str(kv[0][1]))):
        if k != (0, None) and (n >= 2 or k[1] is not None):
            slots[k] = len(slots)
    return slots


def _build_shifted(copies_ref, slots, pad_ref, keys, i, rc, hb, sub):
    for (r, m), slot in slots.items():
        qs = [s - r for s, mk in keys if (s % SUBLANE, mk) == (r, m)]
        lo, hi = hb + min(qs), hb + rc + max(qs)
        for p in range(lo, hi, sub):
            n = min(sub, hi - p)
            v = pad_ref[p + r:p + r + n, :]
            if m is not None:
                t = i * rc - hb + p + r + lax.broadcasted_iota(jnp.int32, (n, 1), 0)
                v = jnp.where(_col_mask(m, t), v, 0.0)
            copies_ref[slot, p:p + n, :] = v


def _read(copies_ref, slots, pad_ref, s, m, row, n):
    k = (s % SUBLANE, m)
    if k in slots:
        q = s - k[0]
        return copies_ref[slots[k], row + q:row + q + n, :]
    return pad_ref[row + s:row + s + n, :]


def _conv_fwd(name, u, col_off, C, w, b, taps, act=False):
    T = u.shape[0]
    rc, ct, hb, sub, nrc, ncc, boundary, taps = _conv_plan(T, C, taps)
    assert col_off % ct == 0
    K = len(taps)
    keys = [(s, None) for s, _ in taps]
    slots = _shift_plan(keys)
    dirs = sorted({m for _, m in taps if m is not None})

    def body(up, uc, un, w_ref, b_ref, *rest):
        y_ref = rest[0]
        pad_ref, copies_ref = rest[-2], rest[-1]
        i = pl.program_id(1)
        _fill_halo(pad_ref, up, uc, un, i, nrc, rc, hb, boundary)
        _build_shifted(copies_ref, slots, pad_ref, keys, i, rc, hb, sub)
        for r0 in range(0, rc, sub):
            acc = jnp.broadcast_to(b_ref[...], (sub, ct))
            for m in [None] + dirs:
                part = None
                for k, (s, mk) in enumerate(taps):
                    if mk != m:
                        continue
                    v = _read(copies_ref, slots, pad_ref, s, None, hb + r0, sub)
                    ok = _seg_ok(boundary, i, rc, r0, sub, s)
                    term = w_ref[k:k + 1, :] * (v if ok is None else jnp.where(ok, v, 0.0))
                    part = term if part is None else part + term
                if part is None:
                    continue
                if m is not None:
                    t = i * rc + r0 + lax.broadcasted_iota(jnp.int32, (sub, 1), 0)
                    part = jnp.where(_col_mask(m, t), part, 0.0)
                acc = acc + part
            y_ref[r0:r0 + sub, :] = acc
            if act:
                rest[1][r0:r0 + sub, :] = _silu(acc)

    n_out = 2 if act else 1
    res = _pcall(
        body, name=name, grid=(ncc, nrc),
        in_specs=_halo_specs(rc, ct, hb, T, col_off // ct) + [pl.BlockSpec((K, ct), lambda j, i: (0, j)),
                                                              pl.BlockSpec((1, ct), lambda j, i: (0, j))],
        out_specs=[pl.BlockSpec((rc, ct), lambda j, i: (i, j))] * n_out,
        out_shape=[jax.ShapeDtypeStruct((T, C), F32)] * n_out,
        scratch_shapes=[pltpu.VMEM((rc + 2 * hb, ct), F32), pltpu.VMEM((max(len(slots), 1), rc + 2 * hb, ct), F32)],
        compiler_params=_cparams(2),
    )(u, u, u, w, b)
    return res if act else res[0]


def _conv_bwd(name, u, col_off, C, w, g, taps, du_dtype=F32):
    T = u.shape[0]
    rc, ct, hb, sub, nrc, ncc, boundary, taps = _conv_plan(T, C, taps)
    K = len(taps)
    u_keys = [(s, None) for s, _ in taps]
    dirs = sorted({m for _, m in taps if m is not None})
    g_keys = [(-s, m) for s, m in taps] + [(0, m) for m in dirs]
    u_slots, g_slots = _shift_plan(u_keys), _shift_plan(g_keys)

    def body(up, uc, un, gp, gc, gn, w_ref, du_ref, dw_ref, db_ref, upad, gpad, ucopies, gcopies):
        i = pl.program_id(1)
        _fill_halo(upad, up, uc, un, i, nrc, rc, hb, boundary)
        _fill_halo(gpad, gp, gc, gn, i, nrc, rc, hb, boundary)
        _build_shifted(ucopies, u_slots, upad, u_keys, i, rc, hb, sub)
        _build_shifted(gcopies, g_slots, gpad, g_keys, i, rc, hb, sub)

        @pl.when(i == 0)
        def _():
            dw_ref[...] = jnp.zeros_like(dw_ref)
            db_ref[...] = jnp.zeros_like(db_ref)

        def fold(v):
            return jnp.sum(v.reshape(sub // SUBLANE, SUBLANE, ct), axis=0)

        dbs = jnp.zeros((SUBLANE, ct), F32)
        for r0 in range(0, rc, sub):
            dbs = dbs + fold(gpad[hb + r0:hb + r0 + sub, :])
            acc = jnp.zeros((sub, ct), F32)
            for k, (s, m) in enumerate(taps):
                v = _read(gcopies, g_slots, gpad, -s, m, hb + r0, sub)
                ok = _seg_ok(boundary, i, rc, r0, sub, -s)
                acc = acc + w_ref[k:k + 1, :] * (v if ok is None else jnp.where(ok, v, 0.0))
            du_ref[r0:r0 + sub, :] = acc.astype(du_ref.dtype)
        db_ref[...] += jnp.sum(dbs, axis=0, keepdims=True)
        for k, (s, m) in enumerate(taps):
            part = jnp.zeros((SUBLANE, ct), F32)
            for r0 in range(0, rc, sub):
                v = _read(ucopies, u_slots, upad, s, None, hb + r0, sub)
                ok = _seg_ok(boundary, i, rc, r0, sub, s)
                part = part + fold(_read(gcopies, g_slots, gpad, 0, m, hb + r0, sub)
                                   * (v if ok is None else jnp.where(ok, v, 0.0)))
            dw_ref[k:k + 1, :] += jnp.sum(part, axis=0, keepdims=True)

    halo_u = _halo_specs(rc, ct, hb, T, col_off // ct)
    halo_g = _halo_specs(rc, ct, hb, T, 0)
    rows = rc + 2 * hb
    return _pcall(
        body, name=name, grid=(ncc, nrc),
        in_specs=halo_u + halo_g + [pl.BlockSpec((K, ct), lambda j, i: (0, j))],
        out_specs=[pl.BlockSpec((rc, ct), lambda j, i: (i, j)), pl.BlockSpec((K, ct), lambda j, i: (0, j)),
                   pl.BlockSpec((1, ct), lambda j, i: (0, j))],
        out_shape=[jax.ShapeDtypeStruct((T, C), du_dtype), jax.ShapeDtypeStruct((K, C), F32),
                   jax.ShapeDtypeStruct((1, C), F32)],
        scratch_shapes=[pltpu.VMEM((rows, ct), F32), pltpu.VMEM((rows, ct), F32),
                        pltpu.VMEM((max(len(u_slots), 1), rows, ct), F32),
                        pltpu.VMEM((max(len(g_slots), 1), rows, ct), F32)],
        compiler_params=_cparams(2),
    )(u, u, u, g, g, g, w)


def _ssd_group(xg, bm, cm, s_in, *per_head, reverse, P):
    R = len(per_head) // 2
    dtrs, a_s = per_head[:R], per_head[R:]
    q, rp = xg.shape
    ii = lax.broadcasted_iota(jnp.int32, (q, q), 0)
    jj = lax.broadcasted_iota(jnp.int32, (q, q), 1)
    causal = (jj >= ii) if reverse else (jj <= ii)
    causal_t = (ii >= jj) if reverse else (ii <= jj)
    eye = ii == jj
    lane = lax.broadcasted_iota(jnp.int32, (1, rp), 1)
    row = lax.broadcasted_iota(jnp.int32, (rp, 1), 0)
    nt = (((1,), (1,)), ((), ()))
    tn = (((0,), (0,)), ((), ()))
    cb = lax.dot_general(cm.astype(BF16), bm.astype(BF16), nt, preferred_element_type=F32)
    dt_x = jnp.zeros((q, rp), F32)
    acum_x = jnp.zeros((q, rp), F32)
    tot_row = jnp.zeros((1, rp), F32)
    tot_col = jnp.zeros((rp, 1), F32)
    wts, lane_masks = [], []
    for r in range(R):
        hm = (lane >= r * P) & (lane < (r + 1) * P)
        hc = (row >= r * P) & (row < (r + 1) * P)
        dt_c = jnp.sum(jnp.where(eye, dtrs[r], 0.0), axis=1, keepdims=True)
        dac = dt_c * a_s[r]
        dar = dtrs[r] * a_s[r]
        acum_c = jnp.sum(jnp.where(causal, dar, 0.0), axis=1, keepdims=True)
        acum_r = jnp.sum(jnp.where(causal_t, dac, 0.0), axis=0, keepdims=True)
        decay = jnp.where(causal, jnp.exp(jnp.where(causal, acum_c - acum_r, 0.0)), 0.0)
        tot = jnp.sum(dac, axis=0, keepdims=True)
        dt_x = jnp.where(hm, dt_c, dt_x)
        acum_x = jnp.where(hm, acum_c, acum_x)
        tot_row = jnp.where(hm, tot, tot_row)
        tot_col = jnp.where(hc, tot, tot_col)
        wts.append((cb * decay).astype(BF16))
        lane_masks.append(hm)
    xdt = xg * dt_x
    xdt_b = xdt.astype(BF16)
    y = jnp.zeros((q, rp), F32)
    for r in range(R):
        y = jnp.where(lane_masks[r], jnp.dot(wts[r], xdt_b, preferred_element_type=F32), y)
    dte = jnp.exp(tot_row - acum_x)
    cs = lax.dot_general((xdt * dte).astype(BF16), bm.astype(BF16), tn, preferred_element_type=F32)
    y = y + lax.dot_general(cm.astype(BF16), s_in.astype(BF16), nt, preferred_element_type=F32) * jnp.exp(acum_x)
    s_out = jnp.exp(tot_col) * s_in + cs
    return y, s_out


def _ssd_maps(NC, ncc, reverse_steps):
    def chunk(d, s):
        if reverse_steps:
            s = NC - 1 - s
        return s if d == 0 else jnp.where(s < ncc, ncc - 1 - s, NC - 1 - s + ncc)

    def lat_chunk(d, s):
        c = chunk(d, s) - ncc
        return jnp.where(c < 0, 0 if d == 0 else NC - ncc - 1, c)

    def step(s):
        return NC - 1 - s if reverse_steps else s

    return chunk, lat_chunk, step


def _ssd_specs(chunk, d, R, Q, N, RP, bo, co):
    return [
        pl.BlockSpec((Q, RP), lambda g, s: (chunk(d, s), g)),
        pl.BlockSpec((Q, N), lambda g, s: (chunk(d, s), bo + g)),
        pl.BlockSpec((Q, N), lambda g, s: (chunk(d, s), co + g)),
        pl.BlockSpec((R, 1, Q), lambda g, s: (g, 0, chunk(d, s))),
        pl.BlockSpec((R, 1, 1), lambda g, s: (g, 0, 0)),
    ]


def _ssd_fwd(xbc, b_off, c_off, dtr, a, P, ncc):
    T = xbc.shape[0]
    H = dtr[0].shape[0]
    N, Q = SSD_STATE, SSD_CHUNK
    NC = T // Q
    G = (c_off - b_off) // N
    R = H // G
    RP = R * P
    chunk, lat_chunk, _ = _ssd_maps(NC, ncc, False)

    def body(*refs):
        s = pl.program_id(1)
        s_ref = refs[-1]

        @pl.when(s == 0)
        def _():
            s_ref[...] = jnp.zeros_like(s_ref)

        for d in range(2):
            x_ref, b_ref, c_ref, dtr_ref, a_ref = refs[5 * d:5 * d + 5]
            y_ref, se_ref = refs[10 + 2 * d:12 + 2 * d]
            s_in = s_ref[d]
            se_ref[...] = s_in
            per_head = [dtr_ref[r] for r in range(R)] + [a_ref[r] for r in range(R)]
            y, s_out = _ssd_group(x_ref[...], b_ref[...], c_ref[...], s_in, *per_head, reverse=d == 1, P=P)
            y_ref[...] = y
            s_ref[d] = s_out

    in_specs, out_specs, out_shape, operands = [], [], [], []
    for d in range(2):
        in_specs += _ssd_specs(chunk, d, R, Q, N, RP, b_off // N, c_off // N)
        operands += [xbc, xbc, xbc, dtr[d], a[d]]
        out_specs += [pl.BlockSpec((Q, RP), functools.partial(lambda g, s, d: (lat_chunk(d, s), g), d=d)),
                      pl.BlockSpec((None, None, RP, N), lambda g, s: (g, s, 0, 0))]
        out_shape += [jax.ShapeDtypeStruct((T - ncc * Q, H * P), F32), jax.ShapeDtypeStruct((G, NC, RP, N), F32)]
    y_f, se_f, y_b, se_b = _pcall(
        body, name="ssd_fwd", grid=(G, NC), in_specs=in_specs, out_specs=out_specs, out_shape=out_shape,
        scratch_shapes=[pltpu.VMEM((2, RP, N), F32)], compiler_params=_cparams(2),
    )(*operands)
    return (y_f, y_b), (se_f, se_b)


def _ssd_bwd(xbc, b_off, c_off, dtr, a, s_enter, dy, P, ncc):
    T = xbc.shape[0]
    H = dtr[0].shape[0]
    N, Q = SSD_STATE, SSD_CHUNK
    NC = T // Q
    G = (c_off - b_off) // N
    R = H // G
    RP = R * P
    chunk, lat_chunk, step = _ssd_maps(NC, ncc, True)
    n_in, n_out = 7, 5

    def body(*refs):
        s = pl.program_id(1)
        ds_ref = refs[-1]

        @pl.when(s == 0)
        def _():
            ds_ref[...] = jnp.zeros_like(ds_ref)

        for d in range(2):
            x_ref, b_ref, c_ref, dtr_ref, a_ref, se_ref, dy_ref = refs[n_in * d:n_in * (d + 1)]
            dx_ref, db_ref, dc_ref, ddtr_ref, da_ref = refs[2 * n_in + n_out * d:2 * n_in + n_out * (d + 1)]
            per_head = [dtr_ref[r] for r in range(R)] + [a_ref[r] for r in range(R)]
            f = functools.partial(_ssd_group, reverse=d == 1, P=P)
            _, vjp = jax.vjp(f, x_ref[...], b_ref[...], c_ref[...], se_ref[...], *per_head)
            is_latent = chunk(d, s) >= ncc
            dy_v = jnp.where(is_latent, dy_ref[...], 0.0)
            grads = vjp((dy_v, ds_ref[d]))
            dx_ref[...] = grads[0]
            db_ref[...] = grads[1]
            dc_ref[...] = grads[2]
            ds_ref[d] = grads[3]
            for r in range(R):
                ddtr_ref[r] = grads[4 + r]
                da_ref[r] = jnp.broadcast_to(grads[4 + R + r], (SUBLANE, LANE))

    in_specs, out_specs, out_shape, operands = [], [], [], []
    for d in range(2):
        in_specs += _ssd_specs(chunk, d, R, Q, N, RP, b_off // N, c_off // N) + [
            pl.BlockSpec((None, None, RP, N), lambda g, s: (g, step(s), 0, 0)),
            pl.BlockSpec((Q, RP), functools.partial(lambda g, s, d: (lat_chunk(d, s), g), d=d)),
        ]
        operands += [xbc, xbc, xbc, dtr[d], a[d], s_enter[d], dy]
    for d in range(2):
        at_chunk = functools.partial(lambda g, s, d: (chunk(d, s), g), d=d)
        out_specs += [
            pl.BlockSpec((Q, RP), at_chunk), pl.BlockSpec((Q, N), at_chunk), pl.BlockSpec((Q, N), at_chunk),
            pl.BlockSpec((R, 1, Q), functools.partial(lambda g, s, d: (g, 0, chunk(d, s)), d=d)),
            pl.BlockSpec((R, SUBLANE, LANE), lambda g, s: (g * NC + s, 0, 0)),
        ]
        out_shape += [
            jax.ShapeDtypeStruct((T, H * P), F32), jax.ShapeDtypeStruct((T, G * N), F32),
            jax.ShapeDtypeStruct((T, G * N), F32), jax.ShapeDtypeStruct((H, 1, T), F32),
            jax.ShapeDtypeStruct((G * NC * R, SUBLANE, LANE), F32),
        ]
    res = _pcall(
        body, name="ssd_bwd", grid=(G, NC), in_specs=in_specs, out_specs=out_specs, out_shape=out_shape,
        scratch_shapes=[pltpu.VMEM((2, RP, N), F32)], compiler_params=_cparams(2),
    )(*operands)
    return res[:n_out], res[n_out:]


def _allgather8(name, v):
    R, C = v.shape

    def body(x_ref, out_ref, send_sems, recv_sems, local_sem):
        x, y, c = lax.axis_index("x"), lax.axis_index("y"), lax.axis_index("c")
        me, sibling = (x, y, c), (x, y, 1 - c)
        chips = [(1 - x, y), (x, 1 - y), (1 - x, 1 - y)]

        def slot(px, py, pc):
            return out_ref.at[4 * px + 2 * py + pc]

        def copy(k, block, to, src=None):
            return pltpu.make_async_remote_copy(
                src_ref=slot(*block) if src is None else src, dst_ref=slot(*block),
                send_sem=send_sems.at[k], recv_sem=recv_sems.at[k], device_id=to, device_id_type=MESH)

        mine = pltpu.make_async_copy(x_ref, slot(*me), local_sem)
        mine.start()
        first = [copy(0, me, sibling, src=x_ref)]
        first += [copy(1 + j, me, (*chip, c), src=x_ref) for j, chip in enumerate(chips)]
        for cp in first:
            cp.start()
        passed = [copy(4 + j, (*chip, c), sibling) for j, chip in enumerate(chips)]
        for j, chip in enumerate(chips):
            copy(1 + j, (*chip, c), me).wait_recv()
            passed[j].start()
        copy(0, sibling, me).wait_recv()
        for j, chip in enumerate(chips):
            copy(4 + j, (*chip, 1 - c), me).wait_recv()
        for cp in first + passed:
            cp.wait_send()
        mine.wait()

    return _pcall(
        body, name=name, out_shape=jax.ShapeDtypeStruct((N_DEV, R, C), v.dtype),
        in_specs=[pl.BlockSpec(memory_space=pltpu.VMEM)], out_specs=pl.BlockSpec(memory_space=pltpu.VMEM),
        scratch_shapes=[pltpu.SemaphoreType.DMA((7,)), pltpu.SemaphoreType.DMA((7,)), pltpu.SemaphoreType.DMA],
        compiler_params=pltpu.CompilerParams(vmem_limit_bytes=VMEM_LIMIT_BYTES),
    )(v)


def _slot(ref, k, axis, size):
    if axis is None:
        return ref.at[k]
    align = LANE if size % LANE == 0 else 2 * SUBLANE
    assert size % align == 0
    return ref.at[(slice(None),) * axis + (pl.ds(pl.multiple_of(k * size, align), size),)]


def _exchange4_start(name, srcs, bcast, dep, axes=None, half=False):
    n = len(srcs)
    axes = list(axes) if axes is not None else [None] * n
    sizes = [None if ax is None else s.shape[ax] for s, ax in zip(srcs, axes)]

    def land_shape(s, ax):
        if not bcast:
            return s.shape
        if half:
            return (N_CHIPS,) + s.shape[1:]
        if ax is None:
            return (N_CHIPS,) + s.shape
        return s.shape[:ax] + (N_CHIPS * s.shape[ax],) + s.shape[ax + 1:]

    lands = [lax.empty(land_shape(s, ax), s.dtype) for s, ax in zip(srcs, axes)]

    def body(*refs):
        src, land = refs[:n], refs[n:2 * n]
        send_sems, recv_sems = refs[2 * n + 1], refs[2 * n + 2]
        token = refs[-1]
        x, y, c = lax.axis_index("x"), lax.axis_index("y"), lax.axis_index("c")
        me = 2 * x + y
        for a in range(n):
            for j, (px, py) in enumerate([(1 - x, y), (x, 1 - y), (1 - x, 1 - y)]):
                pltpu.make_async_remote_copy(
                    src_ref=(src[a].at[c] if half else src[a]) if bcast else src[a].at[2 * px + py],
                    dst_ref=_slot(land[a], me, axes[a], sizes[a]),
                    send_sem=send_sems.at[3 * a + j], recv_sem=recv_sems.at[3 * a + j], device_id=(px, py, c),
                    device_id_type=MESH).start()
        token[...] = jnp.zeros_like(token)

    hbm = pl.BlockSpec(memory_space=pltpu.HBM)
    sem = pl.BlockSpec(memory_space=pltpu.SEMAPHORE)
    outs = _pcall(
        body, name=name,
        out_shape=(pltpu.SemaphoreType.DMA((3 * n,)), pltpu.SemaphoreType.DMA((3 * n,)),
                   *[pltpu.HBM(s.shape, s.dtype) for s in srcs], *[pltpu.HBM(l.shape, l.dtype) for l in lands],
                   jax.ShapeDtypeStruct((SUBLANE, LANE), F32)),
        in_specs=[hbm] * (2 * n) + [pl.BlockSpec(memory_space=pl.ANY)],
        out_specs=(sem, sem, *[hbm] * (2 * n), pl.BlockSpec(memory_space=pltpu.VMEM)),
        input_output_aliases={k: 2 + k for k in range(2 * n)},
        compiler_params=pltpu.CompilerParams(has_side_effects=pltpu.SideEffectType.DATAFLOW_SIDE_EFFECTING),
    )(*[pltpu.with_memory_space_constraint(s, pltpu.HBM) for s in srcs],
      *[pltpu.with_memory_space_constraint(l, pltpu.HBM) for l in lands], dep)
    return (n, bcast, half, axes, sizes, outs[0], outs[1], outs[2:2 + n], outs[2 + n:2 + 2 * n]), outs[-1]


def _exchange4_wait(name, handle, after):
    n, bcast, half, axes, sizes, send_sems, recv_sems, src_thru, land_thru = handle

    def body(*refs):
        src, land = refs[:n], refs[n:2 * n]
        send_sems, recv_sems = refs[2 * n], refs[2 * n + 1]
        x, y, c = lax.axis_index("x"), lax.axis_index("y"), lax.axis_index("c")
        for a in range(n):
            for j, (px, py) in enumerate([(1 - x, y), (x, 1 - y), (1 - x, 1 - y)]):
                pk = 2 * px + py
                copy = pltpu.make_async_remote_copy(
                    src_ref=(src[a].at[c] if half else src[a]) if bcast else src[a].at[pk],
                    dst_ref=_slot(land[a], pk, axes[a], sizes[a]),
                    send_sem=send_sems.at[3 * a + j], recv_sem=recv_sems.at[3 * a + j], device_id=(px, py, c),
                    device_id_type=MESH)
                copy.wait_send()
                copy.wait_recv()

    hbm = pl.BlockSpec(memory_space=pltpu.HBM)
    sem = pl.BlockSpec(memory_space=pltpu.SEMAPHORE)
    outs = _pcall(
        body, name=name,
        out_shape=tuple(pltpu.HBM(t.shape, t.dtype) for t in (*src_thru, *land_thru)),
        in_specs=[hbm] * (2 * n) + [sem, sem, pl.BlockSpec(memory_space=pl.ANY)], out_specs=tuple([hbm] * (2 * n)),
        input_output_aliases={k: k for k in range(2 * n)},
        compiler_params=pltpu.CompilerParams(has_side_effects=pltpu.SideEffectType.DATAFLOW_SIDE_EFFECTING),
    )(*src_thru, *land_thru, send_sems, recv_sems, after)
    return list(outs[:n]), list(outs[n:])


def _tie(name, v, token):
    def body(v_ref, token_ref, o_ref):
        del v_ref, token_ref, o_ref

    any_spec = pl.BlockSpec(memory_space=pl.ANY)
    return _pcall(body, name=name, out_shape=jax.ShapeDtypeStruct(v.shape, v.dtype), in_specs=[any_spec, any_spec],
                  out_specs=any_spec, input_output_aliases={0: 0})(v, token)


def _fill_own(landed, own, me, bcast):
    blk = own if bcast else lax.dynamic_index_in_dim(own, me, 0, keepdims=False)
    return lax.dynamic_update_index_in_dim(landed, blk, me, 0)


def _swap_sibling(name, srcs, by_core=False):
    n = len(srcs)

    def body(*refs):
        src, out = refs[:n], refs[n:2 * n]
        send_sems, recv_sems = refs[2 * n:]
        x, y, c = lax.axis_index("x"), lax.axis_index("y"), lax.axis_index("c")
        copies = []
        for a in range(n):
            send = pltpu.make_async_remote_copy(
                src_ref=src[a], dst_ref=out[a].at[c] if by_core else out[a], send_sem=send_sems.at[a],
                recv_sem=recv_sems.at[a], device_id=(x, y, 1 - c), device_id_type=MESH)
            send.start()
            arrive = pltpu.make_async_remote_copy(
                src_ref=src[a], dst_ref=out[a].at[1 - c] if by_core else out[a], send_sem=send_sems.at[a],
                recv_sem=recv_sems.at[a], device_id=(x, y, 1 - c), device_id_type=MESH)
            copies.append((send, arrive))
        for send, arrive in copies:
            send.wait_send()
            arrive.wait_recv()

    any_spec = pl.BlockSpec(memory_space=pl.ANY)
    return _pcall(
        body, name=name,
        out_shape=[jax.ShapeDtypeStruct(((2,) + s.shape) if by_core else s.shape, s.dtype) for s in srcs],
        in_specs=[any_spec] * n, out_specs=[any_spec] * n,
        scratch_shapes=[pltpu.SemaphoreType.DMA((n,)), pltpu.SemaphoreType.DMA((n,))],
    )(*srcs)


def _mod_fwd(c16, mod_w, mod_b_shard):
    nl, D, S = mod_w.shape

    def body(c_ref, w_ref, b_ref, o_ref):
        s = _silu(c_ref[...]).astype(BF16)
        o_ref[...] = jnp.dot(s, w_ref[...].astype(BF16), preferred_element_type=F32) + b_ref[...]

    return _pcall(
        body, name="mod_fwd", grid=(nl,),
        in_specs=[pl.BlockSpec((16, D), lambda l: (0, 0)), pl.BlockSpec((None, D, S), lambda l: (l, 0, 0)),
                  pl.BlockSpec((None, 1, S), lambda l: (l, 0, 0))],
        out_specs=pl.BlockSpec((None, 16, S), lambda l: (l, 0, 0)),
        out_shape=jax.ShapeDtypeStruct((nl, 16, S), F32), compiler_params=_cparams(1),
    )(c16, mod_w, mod_b_shard)


def _mod_w_update(s16t, dm16, w, m, v):
    nl, D, S = w.shape
    tm = _row_tile(D, 256)

    def body(s_ref, dm_ref, w_ref, m_ref, v_ref, g_ref, dl_ref, nm_ref, nv_ref):
        g = jnp.dot(s_ref[...], dm_ref[...], preferred_element_type=F32, precision=HIGHEST)
        g, dl, nm, nv = _f_adamw(w_ref[...], m_ref[...], v_ref[...], g, jnp.zeros_like(g))
        g_ref[...] = g
        dl_ref[...] = dl
        nm_ref[...] = nm
        nv_ref[...] = nv

    big = pl.BlockSpec((None, tm, S), lambda l, i: (l, i, 0))
    return _pcall(
        body, name="mod_w_update", grid=(nl, D // tm),
        in_specs=[pl.BlockSpec((tm, 16), lambda l, i: (i, 0)), pl.BlockSpec((None, 16, S), lambda l, i: (l, 0, 0)),
                  big, big, big],
        out_specs=[big] * 4, out_shape=[jax.ShapeDtypeStruct(w.shape, F32)] * 4, compiler_params=_cparams(2),
    )(s16t, dm16, w, m, v)


def _size(shape):
    n = 1
    for d in shape:
        n *= d
    return n


def _pack(arrs):
    pieces = []
    for a in arrs:
        flat = a.reshape(-1).astype(F32)
        pieces.append(jnp.pad(flat, (0, _round_up(flat.shape[0], LANE) - flat.shape[0])).reshape(-1, LANE))
    buf = jnp.concatenate(pieces, axis=0)
    return jnp.pad(buf, ((0, _round_up(buf.shape[0], SUBLANE) - buf.shape[0]), (0, 0)))


def _unpack(buf, shapes):
    lead = buf.shape[:-2]
    out, row = [], 0
    for s in shapes:
        n = _size(s)
        rows = _cdiv(n, LANE)
        piece = buf[..., row:row + rows, :].reshape(lead + (rows * LANE,))
        out.append(piece[..., :n].reshape(lead + tuple(s)))
        row += rows
    return out


def _adamw_many(name, ws, ms, vs, gs):
    n = len(ws)

    def body(*refs):
        for k in range(n):
            res = _f_adamw(refs[k][...], refs[n + k][...], refs[2 * n + k][...], refs[3 * n + k][...], 0.0)
            for j in range(4):
                refs[(4 + j) * n + k][...] = res[j]

    vmem = pl.BlockSpec(memory_space=pltpu.VMEM)
    res = _pcall(body, name=name, out_shape=[jax.ShapeDtypeStruct(w.shape, F32) for _ in range(4) for w in ws],
                 in_specs=[vmem] * (4 * n), out_specs=[vmem] * (4 * n))(*ws, *ms, *vs, *gs)
    return [tuple(res[j * n + k] for j in range(4)) for k in range(n)]


SHARD_AXIS = {
    "mod_w": 2, "ssd_w_in": 2, "ssd_conv_w": 2, "ssd_w_out": 1, "conf_w_pw1": 2, "conf_b_pw1": 1, "conf_w_dw": 2,
    "conf_b_dw": 1, "conf_ln_w": 1, "conf_ln_b": 1, "conf_w_pw2": 1, "conf_b_pw2": 1, "ffn_w_up": 2,
    "ffn_conv_w": 3, "ffn_w_down": 1,
}
BIG = ("ssd_w_in", "ssd_w_out", "conf_w_pw1", "conf_w_pw2", "ffn_w_up", "ffn_w_down")
WEIGHTS = ("c_ctx", "mod_w", "mod_b", "norm1_w", "norm2_w", "ssd_w_in", "ssd_conv_w", "ssd_conv_b", "ssd_dt_bias",
           "ssd_a_log", "ssd_d", "ssd_norm_w", "ssd_w_out", "conf_w_pw1", "conf_b_pw1", "conf_w_dw", "conf_b_dw",
           "conf_ln_w", "conf_ln_b", "conf_w_pw2", "conf_b_pw2", "ffn_w_up", "ffn_conv_w", "ffn_conv_b",
           "ffn_w_down", "final_norm_w")
SMALL = tuple(n for n in WEIGHTS if n not in BIG and n != "mod_w")
SMALL_SHARDED = tuple(n for n in SMALL if n in SHARD_AXIS)


def _unshard(stacked, axis):
    return jnp.concatenate([stacked[k] for k in range(N_CHIPS)], axis=axis)


def _to_blocks(full, axis):
    return jnp.stack(jnp.split(full, N_CHIPS, axis=axis))


def _par(v):
    v = v.reshape(-1, v.shape[-1])
    return v[:, None, :]


def kernel(x, c, ctx, c_ctx, mod_w, mod_b, norm1_w, norm2_w, ssd_w_in, ssd_conv_w, ssd_conv_b, ssd_dt_bias, ssd_a_log, ssd_d, ssd_norm_w, ssd_w_out, conf_w_pw1, conf_b_pw1, conf_w_dw, conf_b_dw, conf_ln_w, conf_ln_b, conf_w_pw2, conf_b_pw2, ffn_w_up, ffn_conv_w, ffn_conv_b, ffn_w_down, final_norm_w, loss_target, m_c_ctx, m_mod_w, m_mod_b, m_norm1_w, m_norm2_w, m_ssd_w_in, m_ssd_conv_w, m_ssd_conv_b, m_ssd_dt_bias, m_ssd_a_log, m_ssd_d, m_ssd_norm_w, m_ssd_w_out, m_conf_w_pw1, m_conf_b_pw1, m_conf_w_dw, m_conf_b_dw, m_conf_ln_w, m_conf_ln_b, m_conf_w_pw2, m_conf_b_pw2, m_ffn_w_up, m_ffn_conv_w, m_ffn_conv_b, m_ffn_w_down, m_final_norm_w, v_c_ctx, v_mod_w, v_mod_b, v_norm1_w, v_norm2_w, v_ssd_w_in, v_ssd_conv_w, v_ssd_conv_b, v_ssd_dt_bias, v_ssd_a_log, v_ssd_d, v_ssd_norm_w, v_ssd_w_out, v_conf_w_pw1, v_conf_b_pw1, v_conf_w_dw, v_conf_b_dw, v_conf_ln_w, v_conf_ln_b, v_conf_w_pw2, v_conf_b_pw2, v_ffn_w_up, v_ffn_conv_w, v_ffn_conv_b, v_ffn_w_down, v_final_norm_w):
    given = dict(locals())
    W = {n: given[n] for n in WEIGHTS}
    Mo = {n: given["m_" + n] for n in WEIGHTS}
    Vo = {n: given["v_" + n] for n in WEIGHTS}

    ax, ay, ac = lax.axis_index("x"), lax.axis_index("y"), lax.axis_index("c")
    chip = 2 * ax + ay
    dev = 4 * ax + 2 * ay + ac

    D = x.shape[-1]
    L, Lc = x.shape[1], ctx.shape[1]
    T0 = L + Lc
    H = ssd_a_log.shape[-1]
    DI = ssd_norm_w.shape[-1]
    P = DI // H
    CD = ssd_conv_b.shape[-1]
    N = SSD_STATE
    G = (CD - DI) // (2 * N)
    FH = ffn_conv_b.shape[-1]
    KS = ssd_conv_w.shape[1]
    KC = conf_w_dw.shape[1]
    ncc = Lc // SSD_CHUNK

    shard_b = {n: W[n].astype(BF16) for n in BIG}

    small_shard_shapes = [W[n].shape for n in SMALL_SHARDED]
    f1 = _allgather8("gather_small", _pack([c] + [W[n] for n in SMALL_SHARDED]))
    parts = _unpack(f1, [c.shape] + small_shard_shapes)
    Wf = dict(W)
    for n, p in zip(SMALL_SHARDED, parts[1:]):
        Wf[n] = _unshard(p[::2], SHARD_AXIS[n])
    c16 = jnp.concatenate([parts[0].reshape(N_DEV, D), c_ctx[None, :], jnp.zeros((16 - N_DEV - 1, D), F32)], axis=0)

    S_mod = mod_w.shape[-1]
    mod_b_shard = lax.dynamic_slice_in_dim(mod_b, chip * S_mod, S_mod, axis=1)[:, None, :]
    mod_part = _mod_fwd(c16, mod_w, mod_b_shard)
    f2 = _allgather8("gather_mod", mod_part.reshape(2 * 16, S_mod))
    mods = jnp.concatenate([f2[2 * k].reshape(2, 16, S_mod) for k in range(N_CHIPS)], axis=-1)
    my = lax.dynamic_slice_in_dim(mods, dev, 1, axis=1)[:, 0]
    sh1, sc1, g1, sh2, sc2, g2 = [[my[l, k * D:(k + 1) * D] for l in range(2)] for k in range(6)]
    csh1, csc1 = mods[0, N_DEV, 0:D], mods[0, N_DEV, D:2 * D]

    in_halves = shard_b["ssd_w_in"].reshape(2, D // 2, ssd_w_in.shape[-1])
    gather_a, token = _exchange4_start("gather_w_in_start", [in_halves], True, mods, half=True)
    csc1 = _tie("tie_gather_w_in", csc1, token)

    def full_weight(n, own, landed):
        if landed.ndim == own.ndim:
            ax = SHARD_AXIS[n]
            return lax.dynamic_update_slice_in_dim(landed, own, chip * own.shape[ax], ax)
        return _unshard(_fill_own(landed, own, chip, True), SHARD_AXIS[n])

    xl = x[0]
    rows0 = (ctx[0], xl)
    n1w0, n2w0, n1w1, n2w1 = _par(norm1_w[0]), _par(norm2_w[0]), _par(norm1_w[1]), _par(norm2_w[1])
    sc_seg = jnp.stack([csc1, sc1[0]])[:, None, :]
    sh_seg = jnp.stack([csh1, sh1[0]])[:, None, :]

    a0 = _rw_fwd("l0_modnorm1", _f_modnorm, [], [n1w0, sc_seg, sh_seg], [D], T=T0, seg_rows=(Lc,), head=rows0,
                 out_dtypes=[BF16])
    rest = [n for n in BIG if n != "ssd_w_in"]
    for n in rest:
        a0 = _tie("tie_cast_" + n, a0, shard_b[n])
    (own_in,), (landed_in,) = _exchange4_wait("gather_w_in_wait", gather_a, a0)
    mine = _fill_own(landed_in, lax.dynamic_index_in_dim(own_in, ac, 0, keepdims=False), chip, True)
    (halves,) = _swap_sibling("swap_w_in", [mine], by_core=True)
    halves = lax.dynamic_update_index_in_dim(halves, mine, ac, 0)
    w_in = jnp.concatenate([halves[:, k].reshape(D, -1) for k in range(N_CHIPS)], axis=1)
    landed_in = halves
    def start_gather(tag, names, dep):
        handle, tok = _exchange4_start("gather_" + tag + "_start", [shard_b[n] for n in names], True, dep,
                                       axes=[1 if SHARD_AXIS[n] == 1 else None for n in names])
        return (names, handle), tok

    def finish_gather(tag, group, after):
        names, handle = group
        return {n: full_weight(n, own, g)
                for n, own, g in zip(names, *_exchange4_wait("gather_" + tag + "_wait", handle, after))}

    gather_b, token = start_gather("mix", ["ssd_w_out", "conf_w_pw1", "conf_w_pw2"], landed_in)
    gather_c, token = start_gather("ffn", ["ffn_w_up", "ffn_w_down"], token)
    a0 = _tie("tie_gather_rest", a0, token)
    proj = _mm(a0, w_in, name="l0_w_in")
    seg_taps = [(k - KS // 2, ("seg", Lc)) for k in range(KS)]
    xbc_pre, xbc = _conv_fwd("l0_conv", proj, DI, CD, Wf["ssd_conv_w"][0], ssd_conv_b, seg_taps, act=True)
    dt_raw = proj[:, DI + CD:]
    dt_bias = _par(ssd_dt_bias.reshape(1, 2 * H))
    dt = _rw_fwd("l0_softplus", _f_softplus, [dt_raw], [dt_bias], [2 * H])
    dt_t = dt.T
    dtr = (dt_t[:H, None, :], dt_t[H:, None, :])
    a_all = -jnp.exp(ssd_a_log.reshape(2, H, 1, 1))
    a_neg = (a_all[0], a_all[1])
    (y_f, y_b), s_enter = _ssd_fwd(xbc, DI, DI + G * N, dtr, a_neg, P, ncc)
    gate_rows = [y_f, y_b, (xbc, 0, DI, Lc), (proj, 0, DI, Lc)]
    d_rep = _par(jnp.repeat(ssd_d[0], P))
    ssd_nw = _par(ssd_norm_w[0])
    yn = _rw_fwd("l0_ssd_gate", _f_ssd_gate, gate_rows, [d_rep, ssd_nw], [DI], T=L, out_dtypes=[BF16])
    Wb = finish_gather("mix", gather_b, yn)
    w_out, w_pw1, w_pw2 = Wb["ssd_w_out"][0], Wb["conf_w_pw1"][0], Wb["conf_w_pw2"][0]
    mix0 = _mm(yn, w_out, name="l0_w_out")
    g1_0, g2_0, g1_1, g2_1 = _par(g1[0]), _par(g2[0]), _par(g1[1]), _par(g2[1])
    h1 = _rw_fwd("l0_res1", _f_gate_res, [xl, mix0], [g1_0], [D])
    Wb = finish_gather("ffn", gather_c, h1)
    w_up, w_dn = Wb["ffn_w_up"], Wb["ffn_w_down"]

    grid_taps = [((i - 1) * GRID_W + (j - 1), (None if j == 1 else ("col", j - 1))) for i in range(3) for j in range(3)]

    def ffn_fwd(l, h, tag):
        a = _rw_fwd(tag + "_modnorm2", _f_modnorm, [h], [_par(norm2_w[l]), _par(sc2[l]), _par(sh2[l])], [D],
                    out_dtypes=[BF16])
        hh = _mm(a, w_up[l], name=tag + "_w_up")
        gc = _conv_fwd(tag + "_ffn_conv", hh, FH, FH, Wf["ffn_conv_w"][l].reshape(9, FH), ffn_conv_b[l][None, :],
                       grid_taps)
        act = _rw_fwd(tag + "_act", _f_ffn_act, [(hh, 0, FH), gc], [], [FH], col_tile=_tile(FH, 1536),
                      out_dtypes=[BF16])
        dn = _mm(act, w_dn[l], name=tag + "_w_down")
        return a, hh, gc, act, dn

    a1, hh0, gc0, act0, dn0 = ffn_fwd(0, h1, "l0")
    h2 = _rw_fwd("l0_res2", _f_gate_res, [h1, dn0], [g2_0], [D])

    a2 = _rw_fwd("l1_modnorm1", _f_modnorm, [h2], [n1w1, _par(sc1[1]), _par(sh1[1])], [D], out_dtypes=[BF16])
    pw = _mm(a2, w_pw1, name="l1_pw1")
    b_pw1 = Wf["conf_b_pw1"][0]
    glu = _rw_fwd("l1_glu", _f_glu, [(pw, 0, D), (pw, D, D)], [_par(b_pw1[:D]), _par(b_pw1[D:])], [D])
    conf_taps = [(k - KC // 2, None) for k in range(KC)]
    cv = _conv_fwd("l1_conv", glu, 0, D, Wf["conf_w_dw"][0], Wf["conf_b_dw"], conf_taps)
    ln_w, ln_b = _par(Wf["conf_ln_w"][0]), _par(Wf["conf_ln_b"][0])
    ls = _rw_fwd("l1_ln_silu", _f_ln_silu, [cv], [ln_w, ln_b], [D], out_dtypes=[BF16])
    p2 = _mm(ls, w_pw2, name="l1_pw2")
    b_pw2 = _par(Wf["conf_b_pw2"][0])
    h3 = _rw_fwd("l1_res1", _f_gate_res_bias, [h2, p2], [g1_1, b_pw2], [D])
    a3, hh1, gc1, act1, dn1 = ffn_fwd(1, h3, "l1")
    h4 = _rw_fwd("l1_res2", _f_gate_res, [h3, dn1], [g2_1], [D])

    fnw = final_norm_w[None, :]
    tgt = loss_target[0]
    loss_local = _loss_fwd(h4, tgt, fnw)[0, 0]
    loss = lax.psum(loss_local, ("x", "y", "c"))

    G_full = {}
    reduces = {}

    def start_reduce(tag, items, dep):
        def blocks_of(g, ax):
            if g.ndim == 3:
                return g
            return g.reshape(N_CHIPS, g.shape[0] // N_CHIPS, g.shape[1]) if ax == 0 else _to_blocks(g, ax)

        blocks = [blocks_of(g, ax).astype(BF16) for _, g, ax in items]
        handle, tok = _exchange4_start("reduce_" + tag + "_start", blocks, False, dep)
        reduces[tag] = ([n for n, _, _ in items], handle)
        return tok
    ones = jnp.ones((L, 1), F32)
    (dh4,), (dfnw,) = _rw_bwd("loss_bwd", _f_loss_rows, [h4, tgt], [_par(final_norm_w)], [ones],
                              row_grad=[True, False], par_grad=[True])
    G_full["final_norm_w"] = dfnw.reshape(D)

    def ffn_bwd(l, h, saved, g2_l, dh_out, tag):
        a, hh, gc, act, dn = saved
        (ddn,), (dg2,) = _rw_bwd(tag + "_res2_bwd", _f_gate, [dn], [g2_l], [dh_out],
                                 row_grad=[True], par_grad=[True], row_dtypes=[BF16])
        dact = _mm(ddn, w_dn[l], tb=True, name=tag + "_w_down_dx")
        dwdn = _mm(act, ddn, ta=True, name=tag + "_w_down_dw", out_dtype=BF16)
        (dval, dgc), _ = _rw_bwd(tag + "_act_bwd", _f_ffn_act, [(hh, 0, FH), gc], [], [dact],
                                 row_grad=[True, True], par_grad=[], col_tile=_tile(FH, 1536), row_dtypes=[BF16, F32])
        dgin, dcw, dcb = _conv_bwd(tag + "_ffn_conv_bwd", hh, FH, FH, Wf["ffn_conv_w"][l].reshape(9, FH), dgc,
                                   grid_taps, du_dtype=BF16)
        dhh = jnp.concatenate([dval, dgin], axis=1)
        da = _mm(dhh, w_up[l], tb=True, name=tag + "_w_up_dx")
        dwup = _mm(a, dhh, ta=True, name=tag + "_w_up_dw", out_dtype=BF16, col_blocks=N_CHIPS)
        (dh,), (dn2w, dsc2, dsh2) = _rw_bwd(
            tag + "_modnorm2_bwd", _f_modnorm, [h], [_par(norm2_w[l]), _par(sc2[l]), _par(sh2[l])], [da],
            row_grad=[True], par_grad=[True, True, True], add=dh_out)
        return dh, dict(w_down=dwdn, w_up=dwup, conv_w=dcw.reshape(3, 3, FH), conv_b=dcb.reshape(FH),
                        n2w=dn2w.reshape(D), sc2=dsc2.reshape(D), sh2=dsh2.reshape(D), g2=dg2.reshape(D))

    dh3, gf1 = ffn_bwd(1, h3, (a3, hh1, gc1, act1, dn1), g2_1, dh4, "l1")
    (dp2,), (dg1_1, db_pw2) = _rw_bwd("l1_res1_bwd", _f_gate_bias, [p2], [g1_1, b_pw2], [dh3],
                                      row_grad=[True], par_grad=[True, True], row_dtypes=[BF16])
    dls = _mm(dp2, w_pw2, tb=True, name="l1_pw2_dx")
    dw_pw2 = _mm(ls, dp2, ta=True, name="l1_pw2_dw", out_dtype=BF16)
    (dcv,), (dln_w, dln_b) = _rw_bwd("l1_ln_silu_bwd", _f_ln_silu, [cv], [ln_w, ln_b], [dls],
                                     row_grad=[True], par_grad=[True, True])
    dglu, dw_dw, db_dw = _conv_bwd("l1_conv_bwd", glu, 0, D, Wf["conf_w_dw"][0], dcv, conf_taps)
    (dpa, dpg), (dba, dbg) = _rw_bwd("l1_glu_bwd", _f_glu, [(pw, 0, D), (pw, D, D)],
                                     [_par(b_pw1[:D]), _par(b_pw1[D:])], [dglu],
                                     row_grad=[True, True], par_grad=[True, True], row_dtypes=[BF16, BF16])
    dpw = jnp.concatenate([dpa, dpg], axis=1)
    da2 = _mm(dpw, w_pw1, tb=True, name="l1_pw1_dx")
    dw_pw1 = _mm(a2, dpw, ta=True, name="l1_pw1_dw", out_dtype=BF16, col_blocks=N_CHIPS)
    (dh2,), (dn1w1, dsc1_1, dsh1_1) = _rw_bwd(
        "l1_modnorm1_bwd", _f_modnorm, [h2], [n1w1, _par(sc1[1]), _par(sh1[1])], [da2],
        row_grad=[True], par_grad=[True, True, True], add=dh3)
    G_full["conf_b_pw2"] = db_pw2.reshape(1, D)
    G_full["conf_ln_w"], G_full["conf_ln_b"] = dln_w.reshape(1, D), dln_b.reshape(1, D)
    G_full["conf_w_dw"], G_full["conf_b_dw"] = dw_dw[None], db_dw.reshape(1, D)
    G_full["conf_b_pw1"] = jnp.concatenate([dba.reshape(1, D), dbg.reshape(1, D)], axis=1)

    token = start_reduce("l1", [("conf_w_pw2", dw_pw2, 0), ("conf_w_pw1", dw_pw1, 1), ("ffn_w_up1", gf1["w_up"], 1),
                                ("ffn_w_down1", gf1["w_down"], 0)], dw_pw2)
    dh2 = _tie("tie_reduce_l1", dh2, token)
    dh1, gf0 = ffn_bwd(0, h1, (a1, hh0, gc0, act0, dn0), g2_0, dh2, "l0")
    G_full["ffn_conv_w"] = jnp.stack([gf0["conv_w"], gf1["conv_w"]])
    G_full["ffn_conv_b"] = jnp.stack([gf0["conv_b"], gf1["conv_b"]])

    (dmix,), (dg1_0,) = _rw_bwd("l0_res1_bwd", _f_gate, [mix0], [g1_0], [dh1],
                                row_grad=[True], par_grad=[True], row_dtypes=[BF16])
    dyn = _mm(dmix, w_out, tb=True, name="l0_w_out_dx")
    dw_out = _mm(yn, dmix, ta=True, name="l0_w_out_dw", out_dtype=BF16)
    token = start_reduce("l0", [("ffn_w_up0", gf0["w_up"], 1), ("ffn_w_down0", gf0["w_down"], 0),
                                ("ssd_w_out", dw_out, 0)], dw_out)
    dyn = _tie("tie_reduce_l0", dyn, token)
    (dy_lat, dxs_gate, dz_lat), (dd_rep, dssd_nw) = _rw_bwd(
        "l0_ssd_gate_bwd", _f_ssd_gate, gate_rows, [d_rep, ssd_nw], [dyn],
        row_grad=[True, False, True, True], par_grad=[True, True], T=L, row_dtypes=[F32, F32, BF16])
    g_f, g_b = _ssd_bwd(xbc, DI, DI + G * N, dtr, a_neg, s_enter, dy_lat, P, ncc)
    silu_bwd = functools.partial(_rw_bwd, f=_silu, pars=[], row_grad=[True], par_grad=[], T=T0)
    (dxs_pre,), _ = silu_bwd("l0_silu_bwd_x", rows=[(xbc_pre, 0, DI)], cot_fn=lambda p, q, r: p + q + r,
                             cots=[g_f[0], g_b[0], (dxs_gate, 0, DI, -Lc)],
                             col_tile=_tile(DI, 1024))
    (db_pre,), _ = silu_bwd("l0_silu_bwd_b", rows=[(xbc_pre, DI, G * N)], cot_fn=lambda p, q: p + q,
                            cots=[g_f[1], g_b[1]], col_tile=_tile(G * N, 1024))
    (dc_pre,), _ = silu_bwd("l0_silu_bwd_c", rows=[(xbc_pre, DI + G * N, G * N)], cot_fn=lambda p, q: p + q,
                            cots=[g_f[2], g_b[2]], col_tile=_tile(G * N, 1024))
    conv_w0 = Wf["ssd_conv_w"][0]
    pieces = []
    for tag, off, width, g_pre in (("x", 0, DI, dxs_pre), ("b", DI, G * N, db_pre), ("c", DI + G * N, G * N, dc_pre)):
        pieces.append(_conv_bwd("l0_conv_bwd_" + tag, proj, DI + off, width, conv_w0[:, off:off + width], g_pre,
                                seg_taps, du_dtype=BF16))
    dconv_in = [p[0] for p in pieces]
    dcw0 = jnp.concatenate([p[1] for p in pieces], axis=1)
    dcb0 = jnp.concatenate([p[2] for p in pieces], axis=1)
    ddt = jnp.concatenate([g_f[3][:, 0, :].T, g_b[3][:, 0, :].T], axis=1)
    (ddt_raw,), (ddt_bias,) = _rw_bwd("l0_softplus_bwd", _f_softplus, [dt_raw], [dt_bias], [ddt],
                                      row_grad=[True], par_grad=[True], row_dtypes=[BF16])
    dproj = jnp.concatenate([jnp.pad(dz_lat, ((Lc, 0), (0, 0))), *dconv_in, ddt_raw], axis=1)
    da0 = _mm(dproj, w_in, tb=True, name="l0_w_in_dx")
    dw_in = _mm(a0, dproj, ta=True, name="l0_w_in_dw", out_dtype=BF16)
    token = start_reduce("in", [("ssd_w_in", dw_in, 1)], dw_in)
    da0 = _tie("tie_reduce_in", da0, token)
    (dhcat,), (dn1w0, dsc_seg, dsh_seg) = _rw_bwd(
        "l0_modnorm1_bwd", _f_modnorm, [], [n1w0, sc_seg, sh_seg], [da0], T=T0, head=rows0,
        row_grad=[True], par_grad=[True, True, True], seg_rows=(Lc,), add=(dh1, 0, D, -Lc))
    grad_x = dhcat[Lc:][None]

    da_heads = jnp.stack([g[4][:, 0, 0].reshape(G, T0 // SSD_CHUNK, H // G).sum(axis=1).reshape(H)
                          for g in (g_f, g_b)])[None]
    G_full["ssd_a_log"] = da_heads * (-jnp.exp(ssd_a_log))
    G_full["ssd_dt_bias"] = ddt_bias.reshape(1, 2, H)
    G_full["ssd_d"] = dd_rep.reshape(H, P).sum(axis=1)[None]
    G_full["ssd_norm_w"] = dssd_nw.reshape(1, DI)
    G_full["ssd_conv_w"], G_full["ssd_conv_b"] = dcw0[None], dcb0.reshape(1, CD)
    G_full["norm1_w"] = jnp.stack([dn1w0.reshape(D), dn1w1.reshape(D)])
    G_full["norm2_w"] = jnp.stack([gf0["n2w"], gf1["n2w"]])

    zD = jnp.zeros((D,), F32)
    dm_own = jnp.stack([
        jnp.concatenate([dsh_seg[1, 0], dsc_seg[1, 0], dg1_0.reshape(D), gf0["sh2"], gf0["sc2"], gf0["g2"]]),
        jnp.concatenate([dsh1_1.reshape(D), dsc1_1.reshape(D), dg1_1.reshape(D), gf1["sh2"], gf1["sc2"], gf1["g2"]]),
    ])
    dmc_own = jnp.concatenate([dsh_seg[0, 0], dsc_seg[0, 0], zD, zD, zD, zD])

    out = {}

    def finish_reduce(tags, after, swap_name):
        partial = {}
        for tag in tags:
            names, handle = reduces[tag]
            blocks, landed = _exchange4_wait("reduce_" + tag + "_wait", handle, after)
            for n, blk, own in zip(names, landed, blocks):
                r = _fill_own(blk, own, chip, False)
                partial[n] = _sum_leading("sum4_" + n, r.reshape(N_CHIPS, -1, r.shape[-1]),
                                          (0, 1, 2, 3)).reshape(r.shape[1:])
        for n in ("ffn_w_up", "ffn_w_down"):
            if n + "0" in partial:
                partial[n] = jnp.stack([partial.pop(n + "0"), partial.pop(n + "1")])
        names = [n for n in BIG if n in partial]
        mine = [partial[n].reshape(W[n].shape) for n in names]
        for n, own, sib in zip(names, mine, _swap_sibling(swap_name, mine)):
            out[n] = _adamw("adamw_" + n, W[n], Mo[n], Vo[n], own, sib)
        return names

    early = finish_reduce(["l1", "l0"], dhcat, "swap_grads_early")

    small_sum_names = [n for n in SMALL if n not in ("c_ctx", "mod_b")]
    sum_part = [G_full[n] for n in small_sum_names] + [dmc_own]
    packed = _tie("tie_small_grads", _pack(sum_part + [dm_own]), out[early[-1]][1])
    gat = _allgather8("gather_small_grads", packed)
    total = _sum_leading("sum_small_grads", gat, tuple(range(N_DEV)))
    summed = _unpack(total, [a.shape for a in sum_part])
    Gs = dict(zip(small_sum_names, summed[:-1]))
    dmc_tot = summed[-1]
    dm_all = _unpack(gat, [a.shape for a in sum_part] + [dm_own.shape])[-1].transpose(1, 0, 2)
    dm16 = jnp.concatenate([dm_all, jnp.stack([dmc_tot, jnp.zeros_like(dmc_tot)])[:, None, :],
                            jnp.zeros((2, 16 - N_DEV - 1, 6 * D), F32)], axis=1)
    Gs["mod_b"] = _sum_leading("sum_mod_b", dm16.transpose(1, 0, 2).reshape(16, 2 * 6 * D // LANE, LANE),
                               tuple(range(N_DEV + 1))).reshape(2, 6 * D)

    dm16_shard = lax.dynamic_slice_in_dim(dm16, chip * S_mod, S_mod, axis=2)
    ds16 = _mm(dm16_shard[0], mod_w[0], tb=True, precision=HIGHEST, name="c_ctx_dx")
    sig = jax.nn.sigmoid(c_ctx)
    dcc_part = ds16[N_DEV] * (sig * (1.0 + c_ctx * (1.0 - sig)))
    gat_cc = _allgather8("gather_c_ctx_grad", _pack([dcc_part]))
    Gs["c_ctx"] = _sum_leading("sum_c_ctx_grad", gat_cc, (0, 2, 4, 6)).reshape(-1)[:D]

    s16t = _silu(c16).T
    out["mod_w"] = _mod_w_update(s16t, dm16_shard, mod_w, m_mod_w, v_mod_w)
    finish_reduce(["in"], out["mod_w"][0], "swap_grads_late")

    def own(n, full):
        if n in SHARD_AXIS:
            size = W[n].shape[SHARD_AXIS[n]]
            return lax.dynamic_slice_in_dim(full, chip * size, size, axis=SHARD_AXIS[n])
        return full

    def two_d(a):
        return a.reshape(1, -1) if a.ndim == 1 else a

    g_small = [own(n, Gs[n].reshape(Wf[n].shape)) for n in SMALL]
    res = _adamw_many("adamw_small", [two_d(W[n]) for n in SMALL], [two_d(Mo[n]) for n in SMALL],
                      [two_d(Vo[n]) for n in SMALL], [two_d(g) for g in g_small])
    for n, r in zip(SMALL, res):
        out[n] = tuple(t.reshape(W[n].shape) for t in r)

    grads = [out[n][0] for n in WEIGHTS]
    deltas = [out[n][1] for n in WEIGHTS]
    new_m = [out[n][2] for n in WEIGHTS]
    new_v = [out[n][3] for n in WEIGHTS]
    return (loss, grad_x, *grads, *deltas, *new_m, *new_v)
```

```python
import functools

import jax
import jax.numpy as jnp
from jax import lax
from jax.experimental import pallas as pl
from jax.experimental.pallas import tpu as pltpu

F32 = jnp.float32
BF16 = jnp.bfloat16
MESH = pl.DeviceIdType.MESH
HIGHEST = lax.Precision.HIGHEST

VMEM_LIMIT_BYTES = 48 * 1024 * 1024
LANE = 128
SUBLANE = 8

SSD_STATE = 128
SSD_CHUNK = 128
GRID_W = 64
EPS = 1e-6
N_CHIPS = 4
N_DEV = 8

ADAM_LR = 0.001
ADAM_B1 = 0.9
ADAM_B2 = 0.999
ADAM_EPS = 1e-08
ADAM_WD = 0.01
ADAM_STEP = 10


def _pcall(body, **kw):
    return pl.pallas_call(body, **kw)


def _cparams(n_grid):
    return pltpu.CompilerParams(dimension_semantics=("arbitrary",) * n_grid, vmem_limit_bytes=VMEM_LIMIT_BYTES)


def _cdiv(a, b):
    return -(-a // b)


def _round_up(a, b):
    return _cdiv(a, b) * b


def _tile(n, cap):
    if n <= cap:
        return n
    best = None
    for t in range(LANE, cap + 1, LANE):
        if n % t == 0:
            best = t
    if best is None:
        npad = _round_up(n, LANE)
        for t in range(LANE, cap + 1, LANE):
            if npad % t == 0:
                best = t
    return best


def _row_tile(n, cap, also=()):
    best = None
    for step in (2 * SUBLANE, SUBLANE):
        for t in range(step, min(cap, n) + 1, step):
            if n % t == 0 and all(a % t == 0 for a in also):
                best = t
        if best is not None:
            break
    assert best is not None, (n, cap, also)
    return best


def _silu(v):
    return v * jax.nn.sigmoid(v)


def _mm(a, b, *, name, ta=False, tb=False, precision=None, cap=1024, out_dtype=F32, col_blocks=None):
    M, K = (a.shape[1], a.shape[0]) if ta else a.shape
    N = b.shape[0] if tb else b.shape[1]
    assert K == (b.shape[1] if tb else b.shape[0]), (a.shape, b.shape, ta, tb)
    tm, tk = _tile(M, cap), _tile(K, cap + cap // 2)
    tn = _tile(N if col_blocks is None else N // col_blocks, cap + cap // 2)
    nm, nn, nk = _cdiv(M, tm), _cdiv(N, tn), _cdiv(K, tk)
    k_tail = K % tk
    exact = precision is not None

    def body(a_ref, b_ref, o_ref, acc_ref):
        k = pl.program_id(2)

        @pl.when(k == 0)
        def _():
            acc_ref[...] = jnp.zeros_like(acc_ref)

        av = a_ref[...]
        bv = b_ref[...]
        if k_tail:
            lim = K - k * tk
            ka = lax.broadcasted_iota(jnp.int32, av.shape, 0 if ta else 1)
            kb = lax.broadcasted_iota(jnp.int32, bv.shape, 1 if tb else 0)
            av = jnp.where(ka < lim, av, jnp.zeros_like(av))
            bv = jnp.where(kb < lim, bv, jnp.zeros_like(bv))
        if exact:
            av = av.astype(F32)
            bv = bv.astype(F32)
        else:
            av = av.astype(BF16)
            bv = bv.astype(BF16)
        dn = (((0 if ta else 1,), (1 if tb else 0,)), ((), ()))
        acc_ref[...] += lax.dot_general(av, bv, dn, preferred_element_type=F32, precision=precision)

        @pl.when(k == nk - 1)
        def _():
            o_ref[...] = acc_ref[...].astype(o_ref.dtype)

    a_spec = pl.BlockSpec((tk, tm), lambda i, j, k: (k, i)) if ta else pl.BlockSpec((tm, tk), lambda i, j, k: (i, k))
    b_spec = pl.BlockSpec((tn, tk), lambda i, j, k: (j, k)) if tb else pl.BlockSpec((tk, tn), lambda i, j, k: (k, j))
    if col_blocks is None:
        out_spec = pl.BlockSpec((tm, tn), lambda i, j, k: (i, j))
        out_shape = jax.ShapeDtypeStruct((M, N), out_dtype)
    else:
        per = (N // col_blocks) // tn
        assert per * tn * col_blocks == N, (N, col_blocks, tn)
        out_spec = pl.BlockSpec((None, tm, tn), lambda i, j, k: (j // per, i, j % per))
        out_shape = jax.ShapeDtypeStruct((col_blocks, M, N // col_blocks), out_dtype)
    return _pcall(
        body, name=name, grid=(nm, nn, nk), in_specs=[a_spec, b_spec], out_specs=out_spec, out_shape=out_shape,
        scratch_shapes=[pltpu.VMEM((tm, tn), F32)], compiler_params=_cparams(3),
    )(a, b)


def _norm_rows(rows):
    out = []
    for r in rows:
        if not isinstance(r, tuple):
            r = (r,)
        arr, off, width, roff = (r + (0, None, 0)[len(r) - 1:])
        out.append((arr, off, width if width is not None else arr.shape[1], roff))
    return out


def _rw_plan(T, rows, pars, seg_rows, col_tile, tm_cap):
    widths = [r[2] for r in rows]
    wmax = max(widths + [p.shape[-1] for p in pars] + [1])
    if col_tile is not None:
        assert all(w == widths[0] for w in widths) and all(p.shape[-1] == widths[0] for p in pars)
        ncol = widths[0] // col_tile
        assert ncol * col_tile == widths[0]
        wmax = col_tile
    else:
        ncol = 1
    cap = tm_cap if tm_cap is not None else max(SUBLANE, min(512, (512 * 1024) // wmax))
    tm = _row_tile(T, cap, also=tuple(seg_rows) + tuple(abs(r[3]) for r in rows if r[3]))
    bounds = tuple(s // tm for s in seg_rows)
    return widths, ncol, tm, bounds


def _rw_specs(rows, pars, ncol, tm, bounds, col_tile):
    def seg(i):
        s = 0
        for b in bounds:
            s = s + (i >= b).astype(jnp.int32)
        return s

    specs = []
    for arr, off, w, roff in rows:
        bw = col_tile if col_tile is not None else w
        assert off % bw == 0 and roff % tm == 0, (off, bw, roff, tm)
        specs.append(pl.BlockSpec((tm, bw), functools.partial(
            lambda j, i, ob, rb, last: (jnp.clip(i + rb, 0, last), ob + j),
            ob=off // bw, rb=roff // tm, last=arr.shape[0] // tm - 1)))
    for p in pars:
        bw = col_tile if col_tile is not None else p.shape[-1]
        if p.shape[0] > 1:
            specs.append(pl.BlockSpec((None, 1, bw), lambda j, i: (seg(i), 0, j)))
        else:
            specs.append(pl.BlockSpec((None, 1, bw), lambda j, i: (0, 0, j)))
    return specs, seg


def _head_rows(head):
    top, bottom = head
    return [(top, 0, None, 0), (bottom, 0, None, -top.shape[0])]


def _rw_fwd(name, f, rows, pars, out_widths, *, T=None, seg_rows=(), col_tile=None, tm_cap=None, out_dtypes=None,
            head=None):
    rows = _norm_rows((_head_rows(head) if head else []) + list(rows))
    T = rows[0][0].shape[0] if T is None else T
    widths, ncol, tm, bounds = _rw_plan(T, rows, pars, seg_rows, col_tile, tm_cap)
    in_specs, _ = _rw_specs(rows, pars, ncol, tm, bounds, col_tile)
    nr, npar, nout = len(rows), len(pars), len(out_widths)

    def body(*refs):
        vals = [r[...] for r in refs[:nr + npar]]
        if head:
            vals = [jnp.where(pl.program_id(1) < head[0].shape[0] // tm, vals[0], vals[1])] + vals[2:]
        outs = f(*vals)
        if not isinstance(outs, (tuple, list)):
            outs = (outs,)
        for o_ref, o in zip(refs[nr + npar:], outs):
            o_ref[...] = o.astype(o_ref.dtype)

    out_specs = [pl.BlockSpec((tm, col_tile if col_tile is not None else w), lambda j, i: (i, j)) for w in out_widths]
    res = _pcall(
        body, name=name, grid=(ncol, T // tm), in_specs=in_specs, out_specs=out_specs,
        out_shape=[jax.ShapeDtypeStruct((T, w), dt) for w, dt in zip(out_widths, out_dtypes or [F32] * nout)],
        compiler_params=_cparams(2),
    )(*[r[0] for r in rows], *pars)
    return res if nout > 1 else res[0]


def _rw_bwd(name, f, rows, pars, cots, *, row_grad, par_grad, T=None, seg_rows=(), col_tile=None, tm_cap=None,
            add=None, cot_fn=None, row_dtypes=None, head=None):
    rows = _norm_rows((_head_rows(head) if head else []) + list(rows))
    cots = _norm_rows(cots)
    T = rows[0][0].shape[0] if T is None else T
    extra = _norm_rows([add]) if add is not None else []
    all_rows = rows + cots + extra
    widths, ncol, tm, bounds = _rw_plan(T, all_rows, pars, seg_rows, col_tile, tm_cap)
    in_specs, seg = _rw_specs(all_rows, pars, ncol, tm, bounds, col_tile)
    nr, nc, ne, npar = len(rows), len(cots), len(extra), len(pars)
    skip = 1 if head else 0
    widths = widths[skip:]
    nrf = nr - skip
    row_idx = [k for k in range(nrf) if row_grad[k]]
    par_idx = [k for k in range(npar) if par_grad[k]]

    def body(*refs):
        i = pl.program_id(1)

        def zero_before(vals, ops):
            return [jnp.where(i + c[3] // tm >= 0, v, jnp.zeros_like(v)) if c[3] < 0 else v for v, c in zip(vals, ops)]

        row_vals = [r[...] for r in refs[:nr]]
        if head:
            row_vals = [jnp.where(i < head[0].shape[0] // tm, row_vals[0], row_vals[1])] + row_vals[2:]
        cot_vals = zero_before([r[...] for r in refs[nr:nr + nc]], cots)
        add_vals = zero_before([r[...] for r in refs[nr + nc:nr + nc + ne]], extra)
        par_vals = [r[...] for r in refs[nr + nc + ne:nr + nc + ne + npar]]
        out_refs = refs[nr + nc + ne + npar:]
        outs, vjp = jax.vjp(f, *row_vals, *par_vals)
        if cot_fn is not None:
            cot_vals = cot_fn(*cot_vals)
            if not isinstance(cot_vals, (tuple, list)):
                cot_vals = (cot_vals,)
        if isinstance(outs, (tuple, list)):
            grads = vjp(tuple(c.astype(o.dtype) for c, o in zip(cot_vals, outs)))
        else:
            grads = vjp(cot_vals[0].astype(outs.dtype))
        first_seg = i == 0
        for b in bounds:
            first_seg = first_seg | (i == b)
        for n, k in enumerate(row_idx):
            g = grads[k]
            if n == 0 and add_vals:
                g = g + add_vals[0]
            out_refs[n][...] = g.astype(out_refs[n].dtype)
        for n, k in enumerate(par_idx):
            g = grads[nrf + k]
            o_ref = out_refs[len(row_idx) + n]
            first = first_seg if pars[k].shape[0] > 1 else (i == 0)

            @pl.when(first)
            def _(o_ref=o_ref, g=g):
                o_ref[...] = g

            @pl.when(jnp.logical_not(first))
            def _(o_ref=o_ref, g=g):
                o_ref[...] += g

    out_specs, out_shape = [], []
    for k in row_idx:
        w = widths[k]
        out_specs.append(pl.BlockSpec((tm, col_tile if col_tile is not None else w), lambda j, i: (i, j)))
        out_shape.append(jax.ShapeDtypeStruct((T, w), row_dtypes[len(out_shape)] if row_dtypes else F32))
    for k in par_idx:
        p = pars[k]
        bw = col_tile if col_tile is not None else p.shape[-1]
        if p.shape[0] > 1:
            out_specs.append(pl.BlockSpec((None, 1, bw), lambda j, i: (seg(i), 0, j)))
        else:
            out_specs.append(pl.BlockSpec((None, 1, bw), lambda j, i: (0, 0, j)))
        out_shape.append(jax.ShapeDtypeStruct(p.shape, F32))
    res = _pcall(
        body, name=name, grid=(ncol, T // tm), in_specs=in_specs, out_specs=out_specs, out_shape=out_shape,
        compiler_params=_cparams(2),
    )(*[r[0] for r in all_rows], *pars)
    return list(res[:len(row_idx)]), list(res[len(row_idx):])


def _f_modnorm(h, w, sc, sh):
    y = h * lax.rsqrt(jnp.mean(h * h, axis=-1, keepdims=True) + EPS)
    return (y * w) * (1.0 + sc) + sh


def _f_gate_res(h, y, g):
    return h + g * y


def _f_gate_res_bias(h, y, g, b):
    return h + g * (y + b)


def _f_gate(y, g):
    return g * y


def _f_gate_bias(y, g, b):
    return g * (y + b)


def _f_ffn_act(val, gate):
    return _silu(gate) * val


def _f_softplus(raw, bias):
    v = raw + bias
    return jnp.maximum(v, 0.0) + jnp.log(1.0 + jnp.exp(-jnp.abs(v)))


def _f_ssd_gate(yf, yb, xs, z, d_rep, nw):
    y = (yf + yb + d_rep * xs) * _silu(z)
    return (y * lax.rsqrt(jnp.mean(y * y, axis=-1, keepdims=True) + EPS)) * nw


def _f_glu(a, g, ba, bg):
    return (a + ba) * jax.nn.sigmoid(g + bg)


def _f_ln_silu(h, w, b):
    mu = jnp.mean(h, axis=-1, keepdims=True)
    d = h - mu
    y = d * lax.rsqrt(jnp.mean(d * d, axis=-1, keepdims=True) + EPS)
    return _silu(y * w + b)


def _f_loss_rows(h, t, w):
    y = (h * lax.rsqrt(jnp.mean(h * h, axis=-1, keepdims=True) + EPS)) * w
    e = y - t
    return 0.5 * jnp.mean(e * e, axis=-1, keepdims=True)


def _f_adamw(w, m, v, ga, gb):
    g = ga.astype(F32) + gb
    m = ADAM_B1 * m + (1.0 - ADAM_B1) * g
    v = ADAM_B2 * v + (1.0 - ADAM_B2) * (g * g)
    m_hat = m / (1.0 - ADAM_B1 ** ADAM_STEP)
    v_hat = v / (1.0 - ADAM_B2 ** ADAM_STEP)
    delta = -ADAM_LR * (m_hat / (jnp.sqrt(v_hat) + ADAM_EPS) + ADAM_WD * w)
    return g, delta, m, v


def _adamw(name, w, m, v, ga, gb):
    shape = w.shape
    c = shape[-1]
    two_d = [t.reshape(-1, c) for t in (w, m, v, ga, gb)]
    rows = two_d[0].shape[0]
    pad = _round_up(rows, SUBLANE) - rows
    if pad:
        two_d = [jnp.pad(t, ((0, pad), (0, 0))) for t in two_d]
    outs = _rw_fwd(name, _f_adamw, two_d, [], [c] * 4)
    return tuple(o[:rows].reshape(shape) for o in outs)


def _sum_leading(name, x, idxs, out_dtype=F32):
    _, R, C = x.shape
    tm = _row_tile(R, max(SUBLANE, min(512, (512 * 1024) // C)))

    def body(x_ref, o_ref):
        acc = x_ref[idxs[0]].astype(F32)
        for k in idxs[1:]:
            acc = acc + x_ref[k].astype(F32)
        o_ref[...] = acc.astype(o_ref.dtype)

    return _pcall(
        body, name=name, grid=(R // tm,), in_specs=[pl.BlockSpec((x.shape[0], tm, C), lambda i: (0, i, 0))],
        out_specs=pl.BlockSpec((tm, C), lambda i: (i, 0)), out_shape=jax.ShapeDtypeStruct((R, C), out_dtype),
        compiler_params=_cparams(1),
    )(x)


def _loss_fwd(h, t, w):
    T, D = h.shape
    tm = _row_tile(T, 256)

    def body(h_ref, t_ref, w_ref, o_ref):
        i = pl.program_id(0)
        part = jnp.sum(_f_loss_rows(h_ref[...], t_ref[...], w_ref[...]), axis=0, keepdims=True)
        part = jnp.broadcast_to(part, (1, LANE))

        @pl.when(i == 0)
        def _():
            o_ref[...] = part

        @pl.when(i > 0)
        def _():
            o_ref[...] += part

    return _pcall(
        body, name="loss_fwd", grid=(T // tm,),
        in_specs=[pl.BlockSpec((tm, D), lambda i: (i, 0)), pl.BlockSpec((tm, D), lambda i: (i, 0)),
                  pl.BlockSpec((1, D), lambda i: (0, 0))],
        out_specs=pl.BlockSpec((1, LANE), lambda i: (0, 0)), out_shape=jax.ShapeDtypeStruct((1, LANE), F32),
        compiler_params=_cparams(1),
    )(h, t, w)


CONV_ROWS = 256
CONV_ROWS_FEW_TAPS = 1024
CONV_ACC_ELEMS = 16384


def _col_mask(arg, t):
    col = jnp.bitwise_and(t, GRID_W - 1)
    return (col != 0) if arg < 0 else (col != GRID_W - 1)


def _conv_plan(T, C, taps):
    seg = [m[1] for _, m in taps if m is not None and m[0] == "seg"]
    cap = CONV_ROWS_FEW_TAPS if len(taps) <= 9 else CONV_ROWS
    rc = next(r for r in (1024, 768, 512, 256, LANE) if r <= cap and T % r == 0)
    ct = next((t for t in (512, 256, LANE) if C % t == 0), C)
    reach = max(abs(s) for s, _ in taps)
    hb = next(h for h in (8, 16, 32, 64, 128, 256) if h >= reach and rc % h == 0)
    sub = max(2 * SUBLANE, min(rc, CONV_ACC_ELEMS // ct))
    boundary = None
    if seg:
        inside = seg[0] % rc
        boundary = (seg[0], (inside - reach, inside + reach) if inside else None)
    taps = [(s, None if (m is None or m[0] == "seg") else m[1]) for s, m in taps]
    return rc, ct, hb, sub, T // rc, C // ct, boundary, taps


def _seg_ok(boundary, i, rc, r0, n, s):
    if boundary is None or boundary[1] is None or s == 0 or r0 + n <= boundary[1][0] or r0 >= boundary[1][1]:
        return None
    t = i * rc + r0 + lax.broadcasted_iota(jnp.int32, (n, 1), 0)
    return (t >= boundary[0]) == ((t + s) >= boundary[0])


def _halo_specs(rc, ct, hb, T, off_blocks):
    per = rc // hb
    last = T // hb - 1
    prev = pl.BlockSpec((hb, ct), lambda j, i: (jnp.maximum(i * per - 1, 0), off_blocks + j))
    cur = pl.BlockSpec((rc, ct), lambda j, i: (i, off_blocks + j))
    nxt = pl.BlockSpec((hb, ct), lambda j, i: (jnp.minimum((i + 1) * per, last), off_blocks + j))
    return [prev, cur, nxt]


def _fill_halo(pad_ref, p_ref, c_ref, n_ref, i, nrc, rc, hb, boundary):
    has_prev = i > 0
    has_next = i < nrc - 1
    if boundary is not None:
        has_prev = has_prev & (i * rc != boundary[0])
        has_next = has_next & ((i + 1) * rc != boundary[0])
    pad_ref[0:hb, :] = jnp.where(has_prev, p_ref[...], 0.0)
    pad_ref[hb:hb + rc, :] = c_ref[...]
    pad_ref[hb + rc:hb + rc + hb, :] = jnp.where(has_next, n_ref[...], 0.0)


def _shift_plan(keys):
    count = {}
    for s, m in keys:
        k = (s % SUBLANE, m)
        count[k] = count.get(k, 0) + 1
    slots = {}
    for k, n in sorted(count.items(), key=lambda kv: (kv[0][0], str(kv[0][1]))):
        if k != (0, None) and (n >= 2 or k[1] is not None):
            slots[k] = len(slots)
    return slots


def _build_shifted(copies_ref, slots, pad_ref, keys, i, rc, hb, sub):
    for (r, m), slot in slots.items():
        qs = [s - r for s, mk in keys if (s % SUBLANE, mk) == (r, m)]
        lo, hi = hb + min(qs), hb + rc + max(qs)
        for p in range(lo, hi, sub):
            n = min(sub, hi - p)
            v = pad_ref[p + r:p + r + n, :]
            if m is not None:
                t = i * rc - hb + p + r + lax.broadcasted_iota(jnp.int32, (n, 1), 0)
                v = jnp.where(_col_mask(m, t), v, 0.0)
            copies_ref[slot, p:p + n, :] = v


def _read(copies_ref, slots, pad_ref, s, m, row, n):
    k = (s % SUBLANE, m)
    if k in slots:
        q = s - k[0]
        return copies_ref[slots[k], row + q:row + q + n, :]
    return pad_ref[row + s:row + s + n, :]


def _conv_fwd(name, u, col_off, C, w, b, taps, act=False):
    T = u.shape[0]
    rc, ct, hb, sub, nrc, ncc, boundary, taps = _conv_plan(T, C, taps)
    assert col_off % ct == 0
    K = len(taps)
    keys = [(s, None) for s, _ in taps]
    slots = _shift_plan(keys)
    dirs = sorted({m for _, m in taps if m is not None})

    def body(up, uc, un, w_ref, b_ref, *rest):
        y_ref = rest[0]
        pad_ref, copies_ref = rest[-2], rest[-1]
        i = pl.program_id(1)
        _fill_halo(pad_ref, up, uc, un, i, nrc, rc, hb, boundary)
        _build_shifted(copies_ref, slots, pad_ref, keys, i, rc, hb, sub)
        for r0 in range(0, rc, sub):
            acc = jnp.broadcast_to(b_ref[...], (sub, ct))
            for m in [None] + dirs:
                part = None
                for k, (s, mk) in enumerate(taps):
                    if mk != m:
                        continue
                    v = _read(copies_ref, slots, pad_ref, s, None, hb + r0, sub)
                    ok = _seg_ok(boundary, i, rc, r0, sub, s)
                    term = w_ref[k:k + 1, :] * (v if ok is None else jnp.where(ok, v, 0.0))
                    part = term if part is None else part + term
                if part is None:
                    continue
                if m is not None:
                    t = i * rc + r0 + lax.broadcasted_iota(jnp.int32, (sub, 1), 0)
                    part = jnp.where(_col_mask(m, t), part, 0.0)
                acc = acc + part
            y_ref[r0:r0 + sub, :] = acc
            if act:
                rest[1][r0:r0 + sub, :] = _silu(acc)

    n_out = 2 if act else 1
    res = _pcall(
        body, name=name, grid=(ncc, nrc),
        in_specs=_halo_specs(rc, ct, hb, T, col_off // ct) + [pl.BlockSpec((K, ct), lambda j, i: (0, j)),
                                                              pl.BlockSpec((1, ct), lambda j, i: (0, j))],
        out_specs=[pl.BlockSpec((rc, ct), lambda j, i: (i, j))] * n_out,
        out_shape=[jax.ShapeDtypeStruct((T, C), F32)] * n_out,
        scratch_shapes=[pltpu.VMEM((rc + 2 * hb, ct), F32), pltpu.VMEM((max(len(slots), 1), rc + 2 * hb, ct), F32)],
        compiler_params=_cparams(2),
    )(u, u, u, w, b)
    return res if act else res[0]


def _conv_bwd(name, u, col_off, C, w, g, taps, du_dtype=F32):
    T = u.shape[0]
    rc, ct, hb, sub, nrc, ncc, boundary, taps = _conv_plan(T, C, taps)
    K = len(taps)
    u_keys = [(s, None) for s, _ in taps]
    dirs = sorted({m for _, m in taps if m is not None})
    g_keys = [(-s, m) for s, m in taps] + [(0, m) for m in dirs]
    u_slots, g_slots = _shift_plan(u_keys), _shift_plan(g_keys)

    def body(up, uc, un, gp, gc, gn, w_ref, du_ref, dw_ref, db_ref, upad, gpad, ucopies, gcopies):
        i = pl.program_id(1)
        _fill_halo(upad, up, uc, un, i, nrc, rc, hb, boundary)
        _fill_halo(gpad, gp, gc, gn, i, nrc, rc, hb, boundary)
        _build_shifted(ucopies, u_slots, upad, u_keys, i, rc, hb, sub)
        _build_shifted(gcopies, g_slots, gpad, g_keys, i, rc, hb, sub)

        @pl.when(i == 0)
        def _():
            dw_ref[...] = jnp.zeros_like(dw_ref)
            db_ref[...] = jnp.zeros_like(db_ref)

        def fold(v):
            return jnp.sum(v.reshape(sub // SUBLANE, SUBLANE, ct), axis=0)

        dbs = jnp.zeros((SUBLANE, ct), F32)
        for r0 in range(0, rc, sub):
            dbs = dbs + fold(gpad[hb + r0:hb + r0 + sub, :])
            acc = jnp.zeros((sub, ct), F32)
            for k, (s, m) in enumerate(taps):
                v = _read(gcopies, g_slots, gpad, -s, m, hb + r0, sub)
                ok = _seg_ok(boundary, i, rc, r0, sub, -s)
                acc = acc + w_ref[k:k + 1, :] * (v if ok is None else jnp.where(ok, v, 0.0))
            du_ref[r0:r0 + sub, :] = acc.astype(du_ref.dtype)
        db_ref[...] += jnp.sum(dbs, axis=0, keepdims=True)
        for k, (s, m) in enumerate(taps):
            part = jnp.zeros((SUBLANE, ct), F32)
            for r0 in range(0, rc, sub):
                v = _read(ucopies, u_slots, upad, s, None, hb + r0, sub)
                ok = _seg_ok(boundary, i, rc, r0, sub, s)
                part = part + fold(_read(gcopies, g_slots, gpad, 0, m, hb + r0, sub)
                                   * (v if ok is None else jnp.where(ok, v, 0.0)))
            dw_ref[k:k + 1, :] += jnp.sum(part, axis=0, keepdims=True)

    halo_u = _halo_specs(rc, ct, hb, T, col_off // ct)
    halo_g = _halo_specs(rc, ct, hb, T, 0)
    rows = rc + 2 * hb
    return _pcall(
        body, name=name, grid=(ncc, nrc),
        in_specs=halo_u + halo_g + [pl.BlockSpec((K, ct), lambda j, i: (0, j))],
        out_specs=[pl.BlockSpec((rc, ct), lambda j, i: (i, j)), pl.BlockSpec((K, ct), lambda j, i: (0, j)),
                   pl.BlockSpec((1, ct), lambda j, i: (0, j))],
        out_shape=[jax.ShapeDtypeStruct((T, C), du_dtype), jax.ShapeDtypeStruct((K, C), F32),
                   jax.ShapeDtypeStruct((1, C), F32)],
        scratch_shapes=[pltpu.VMEM((rows, ct), F32), pltpu.VMEM((rows, ct), F32),
                        pltpu.VMEM((max(len(u_slots), 1), rows, ct), F32),
                        pltpu.VMEM((max(len(g_slots), 1), rows, ct), F32)],
        compiler_params=_cparams(2),
    )(u, u, u, g, g, g, w)


def _ssd_group(xg, bm, cm, s_in, *per_head, reverse, P):
    R = len(per_head) // 2
    dtrs, a_s = per_head[:R], per_head[R:]
    q, rp = xg.shape
    ii = lax.broadcasted_iota(jnp.int32, (q, q), 0)
    jj = lax.broadcasted_iota(jnp.int32, (q, q), 1)
    causal = (jj >= ii) if reverse else (jj <= ii)
    causal_t = (ii >= jj) if reverse else (ii <= jj)
    eye = ii == jj
    lane = lax.broadcasted_iota(jnp.int32, (1, rp), 1)
    row = lax.broadcasted_iota(jnp.int32, (rp, 1), 0)
    nt = (((1,), (1,)), ((), ()))
    tn = (((0,), (0,)), ((), ()))
    cb = lax.dot_general(cm.astype(BF16), bm.astype(BF16), nt, preferred_element_type=F32)
    dt_x = jnp.zeros((q, rp), F32)
    acum_x = jnp.zeros((q, rp), F32)
    tot_row = jnp.zeros((1, rp), F32)
    tot_col = jnp.zeros((rp, 1), F32)
    wts, lane_masks = [], []
    for r in range(R):
        hm = (lane >= r * P) & (lane < (r + 1) * P)
        hc = (row >= r * P) & (row < (r + 1) * P)
        dt_c = jnp.sum(jnp.where(eye, dtrs[r], 0.0), axis=1, keepdims=True)
        dac = dt_c * a_s[r]
        dar = dtrs[r] * a_s[r]
        acum_c = jnp.sum(jnp.where(causal, dar, 0.0), axis=1, keepdims=True)
        acum_r = jnp.sum(jnp.where(causal_t, dac, 0.0), axis=0, keepdims=True)
        decay = jnp.where(causal, jnp.exp(jnp.where(causal, acum_c - acum_r, 0.0)), 0.0)
        tot = jnp.sum(dac, axis=0, keepdims=True)
        dt_x = jnp.where(hm, dt_c, dt_x)
        acum_x = jnp.where(hm, acum_c, acum_x)
        tot_row = jnp.where(hm, tot, tot_row)
        tot_col = jnp.where(hc, tot, tot_col)
        wts.append((cb * decay).astype(BF16))
        lane_masks.append(hm)
    xdt = xg * dt_x
    xdt_b = xdt.astype(BF16)
    y = jnp.zeros((q, rp), F32)
    for r in range(R):
        y = jnp.where(lane_masks[r], jnp.dot(wts[r], xdt_b, preferred_element_type=F32), y)
    dte = jnp.exp(tot_row - acum_x)
    cs = lax.dot_general((xdt * dte).astype(BF16), bm.astype(BF16), tn, preferred_element_type=F32)
    y = y + lax.dot_general(cm.astype(BF16), s_in.astype(BF16), nt, preferred_element_type=F32) * jnp.exp(acum_x)
    s_out = jnp.exp(tot_col) * s_in + cs
    return y, s_out


def _ssd_maps(NC, ncc, reverse_steps):
    def chunk(d, s):
        if reverse_steps:
            s = NC - 1 - s
        return s if d == 0 else jnp.where(s < ncc, ncc - 1 - s, NC - 1 - s + ncc)

    def lat_chunk(d, s):
        c = chunk(d, s) - ncc
        return jnp.where(c < 0, 0 if d == 0 else NC - ncc - 1, c)

    def step(s):
        return NC - 1 - s if reverse_steps else s

    return chunk, lat_chunk, step


def _ssd_specs(chunk, d, R, Q, N, RP, bo, co):
    return [
        pl.BlockSpec((Q, RP), lambda g, s: (chunk(d, s), g)),
        pl.BlockSpec((Q, N), lambda g, s: (chunk(d, s), bo + g)),
        pl.BlockSpec((Q, N), lambda g, s: (chunk(d, s), co + g)),
        pl.BlockSpec((R, 1, Q), lambda g, s: (g, 0, chunk(d, s))),
        pl.BlockSpec((R, 1, 1), lambda g, s: (g, 0, 0)),
    ]


def _ssd_fwd(xbc, b_off, c_off, dtr, a, P, ncc):
    T = xbc.shape[0]
    H = dtr[0].shape[0]
    N, Q = SSD_STATE, SSD_CHUNK
    NC = T // Q
    G = (c_off - b_off) // N
    R = H // G
    RP = R * P
    chunk, lat_chunk, _ = _ssd_maps(NC, ncc, False)

    def body(*refs):
        s = pl.program_id(1)
        s_ref = refs[-1]

        @pl.when(s == 0)
        def _():
            s_ref[...] = jnp.zeros_like(s_ref)

        for d in range(2):
            x_ref, b_ref, c_ref, dtr_ref, a_ref = refs[5 * d:5 * d + 5]
            y_ref, se_ref = refs[10 + 2 * d:12 + 2 * d]
            s_in = s_ref[d]
            se_ref[...] = s_in
            per_head = [dtr_ref[r] for r in range(R)] + [a_ref[r] for r in range(R)]
            y, s_out = _ssd_group(x_ref[...], b_ref[...], c_ref[...], s_in, *per_head, reverse=d == 1, P=P)
            y_ref[...] = y
            s_ref[d] = s_out

    in_specs, out_specs, out_shape, operands = [], [], [], []
    for d in range(2):
        in_specs += _ssd_specs(chunk, d, R, Q, N, RP, b_off // N, c_off // N)
        operands += [xbc, xbc, xbc, dtr[d], a[d]]
        out_specs += [pl.BlockSpec((Q, RP), functools.partial(lambda g, s, d: (lat_chunk(d, s), g), d=d)),
                      pl.BlockSpec((None, None, RP, N), lambda g, s: (g, s, 0, 0))]
        out_shape += [jax.ShapeDtypeStruct((T - ncc * Q, H * P), F32), jax.ShapeDtypeStruct((G, NC, RP, N), F32)]
    y_f, se_f, y_b, se_b = _pcall(
        body, name="ssd_fwd", grid=(G, NC), in_specs=in_specs, out_specs=out_specs, out_shape=out_shape,
        scratch_shapes=[pltpu.VMEM((2, RP, N), F32)], compiler_params=_cparams(2),
    )(*operands)
    return (y_f, y_b), (se_f, se_b)


def _ssd_bwd(xbc, b_off, c_off, dtr, a, s_enter, dy, P, ncc):
    T = xbc.shape[0]
    H = dtr[0].shape[0]
    N, Q = SSD_STATE, SSD_CHUNK
    NC = T // Q
    G = (c_off - b_off) // N
    R = H // G
    RP = R * P
    chunk, lat_chunk, step = _ssd_maps(NC, ncc, True)
    n_in, n_out = 7, 5

    def body(*refs):
        s = pl.program_id(1)
        ds_ref = refs[-1]

        @pl.when(s == 0)
        def _():
            ds_ref[...] = jnp.zeros_like(ds_ref)

        for d in range(2):
            x_ref, b_ref, c_ref, dtr_ref, a_ref, se_ref, dy_ref = refs[n_in * d:n_in * (d + 1)]
            dx_ref, db_ref, dc_ref, ddtr_ref, da_ref = refs[2 * n_in + n_out * d:2 * n_in + n_out * (d + 1)]
            per_head = [dtr_ref[r] for r in range(R)] + [a_ref[r] for r in range(R)]
            f = functools.partial(_ssd_group, reverse=d == 1, P=P)
            _, vjp = jax.vjp(f, x_ref[...], b_ref[...], c_ref[...], se_ref[...], *per_head)
            is_latent = chunk(d, s) >= ncc
            dy_v = jnp.where(is_latent, dy_ref[...], 0.0)
            grads = vjp((dy_v, ds_ref[d]))
            dx_ref[...] = grads[0]
            db_ref[...] = grads[1]
            dc_ref[...] = grads[2]
            ds_ref[d] = grads[3]
            for r in range(R):
                ddtr_ref[r] = grads[4 + r]
                da_ref[r] = jnp.broadcast_to(grads[4 + R + r], (SUBLANE, LANE))

    in_specs, out_specs, out_shape, operands = [], [], [], []
    for d in range(2):
        in_specs += _ssd_specs(chunk, d, R, Q, N, RP, b_off // N, c_off // N) + [
            pl.BlockSpec((None, None, RP, N), lambda g, s: (g, step(s), 0, 0)),
            pl.BlockSpec((Q, RP), functools.partial(lambda g, s, d: (lat_chunk(d, s), g), d=d)),
        ]
        operands += [xbc, xbc, xbc, dtr[d], a[d], s_enter[d], dy]
    for d in range(2):
        at_chunk = functools.partial(lambda g, s, d: (chunk(d, s), g), d=d)
        out_specs += [
            pl.BlockSpec((Q, RP), at_chunk), pl.BlockSpec((Q, N), at_chunk), pl.BlockSpec((Q, N), at_chunk),
            pl.BlockSpec((R, 1, Q), functools.partial(lambda g, s, d: (g, 0, chunk(d, s)), d=d)),
            pl.BlockSpec((R, SUBLANE, LANE), lambda g, s: (g * NC + s, 0, 0)),
        ]
        out_shape += [
            jax.ShapeDtypeStruct((T, H * P), F32), jax.ShapeDtypeStruct((T, G * N), F32),
            jax.ShapeDtypeStruct((T, G * N), F32), jax.ShapeDtypeStruct((H, 1, T), F32),
            jax.ShapeDtypeStruct((G * NC * R, SUBLANE, LANE), F32),
        ]
    res = _pcall(
        body, name="ssd_bwd", grid=(G, NC), in_specs=in_specs, out_specs=out_specs, out_shape=out_shape,
        scratch_shapes=[pltpu.VMEM((2, RP, N), F32)], compiler_params=_cparams(2),
    )(*operands)
    return res[:n_out], res[n_out:]


def _allgather8(name, v):
    R, C = v.shape

    def body(x_ref, out_ref, send_sems, recv_sems, local_sem):
        x, y, c = lax.axis_index("x"), lax.axis_index("y"), lax.axis_index("c")
        me, sibling = (x, y, c), (x, y, 1 - c)
        chips = [(1 - x, y), (x, 1 - y), (1 - x, 1 - y)]

        def slot(px, py, pc):
            return out_ref.at[4 * px + 2 * py + pc]

        def copy(k, block, to, src=None):
            return pltpu.make_async_remote_copy(
                src_ref=slot(*block) if src is None else src, dst_ref=slot(*block),
                send_sem=send_sems.at[k], recv_sem=recv_sems.at[k], device_id=to, device_id_type=MESH)

        mine = pltpu.make_async_copy(x_ref, slot(*me), local_sem)
        mine.start()
        first = [copy(0, me, sibling, src=x_ref)]
        first += [copy(1 + j, me, (*chip, c), src=x_ref) for j, chip in enumerate(chips)]
        for cp in first:
            cp.start()
        passed = [copy(4 + j, (*chip, c), sibling) for j, chip in enumerate(chips)]
        for j, chip in enumerate(chips):
            copy(1 + j, (*chip, c), me).wait_recv()
            passed[j].start()
        copy(0, sibling, me).wait_recv()
        for j, chip in enumerate(chips):
            copy(4 + j, (*chip, 1 - c), me).wait_recv()
        for cp in first + passed:
            cp.wait_send()
        mine.wait()

    return _pcall(
        body, name=name, out_shape=jax.ShapeDtypeStruct((N_DEV, R, C), v.dtype),
        in_specs=[pl.BlockSpec(memory_space=pltpu.VMEM)], out_specs=pl.BlockSpec(memory_space=pltpu.VMEM),
        scratch_shapes=[pltpu.SemaphoreType.DMA((7,)), pltpu.SemaphoreType.DMA((7,)), pltpu.SemaphoreType.DMA],
        compiler_params=pltpu.CompilerParams(vmem_limit_bytes=VMEM_LIMIT_BYTES),
    )(v)


def _slot(ref, k, axis, size):
    if axis is None:
        return ref.at[k]
    align = LANE if size % LANE == 0 else 2 * SUBLANE
    assert size % align == 0
    return ref.at[(slice(None),) * axis + (pl.ds(pl.multiple_of(k * size, align), size),)]


def _exchange4_start(name, srcs, bcast, dep, axes=None, half=False):
    n = len(srcs)
    axes = list(axes) if axes is not None else [None] * n
    sizes = [None if ax is None else s.shape[ax] for s, ax in zip(srcs, axes)]

    def land_shape(s, ax):
        if not bcast:
            return s.shape
        if half:
            return (N_CHIPS,) + s.shape[1:]
        if ax is None:
            return (N_CHIPS,) + s.shape
        return s.shape[:ax] + (N_CHIPS * s.shape[ax],) + s.shape[ax + 1:]

    lands = [lax.empty(land_shape(s, ax), s.dtype) for s, ax in zip(srcs, axes)]

    def body(*refs):
        src, land = refs[:n], refs[n:2 * n]
        send_sems, recv_sems = refs[2 * n + 1], refs[2 * n + 2]
        token = refs[-1]
        x, y, c = lax.axis_index("x"), lax.axis_index("y"), lax.axis_index("c")
        me = 2 * x + y
        for a in range(n):
            for j, (px, py) in enumerate([(1 - x, y), (x, 1 - y), (1 - x, 1 - y)]):
                pltpu.make_async_remote_copy(
                    src_ref=(src[a].at[c] if half else src[a]) if bcast else src[a].at[2 * px + py],
                    dst_ref=_slot(land[a], me, axes[a], sizes[a]),
                    send_sem=send_sems.at[3 * a + j], recv_sem=recv_sems.at[3 * a + j], device_id=(px, py, c),
                    device_id_type=MESH).start()
        token[...] = jnp.zeros_like(token)

    hbm = pl.BlockSpec(memory_space=pltpu.HBM)
    sem = pl.BlockSpec(memory_space=pltpu.SEMAPHORE)
    outs = _pcall(
        body, name=name,
        out_shape=(pltpu.SemaphoreType.DMA((3 * n,)), pltpu.SemaphoreType.DMA((3 * n,)),
                   *[pltpu.HBM(s.shape, s.dtype) for s in srcs], *[pltpu.HBM(l.shape, l.dtype) for l in lands],
                   jax.ShapeDtypeStruct((SUBLANE, LANE), F32)),
        in_specs=[hbm] * (2 * n) + [pl.BlockSpec(memory_space=pl.ANY)],
        out_specs=(sem, sem, *[hbm] * (2 * n), pl.BlockSpec(memory_space=pltpu.VMEM)),
        input_output_aliases={k: 2 + k for k in range(2 * n)},
        compiler_params=pltpu.CompilerParams(has_side_effects=pltpu.SideEffectType.DATAFLOW_SIDE_EFFECTING),
    )(*[pltpu.with_memory_space_constraint(s, pltpu.HBM) for s in srcs],
      *[pltpu.with_memory_space_constraint(l, pltpu.HBM) for l in lands], dep)
    return (n, bcast, half, axes, sizes, outs[0], outs[1], outs[2:2 + n], outs[2 + n:2 + 2 * n]), outs[-1]


def _exchange4_wait(name, handle, after):
    n, bcast, half, axes, sizes, send_sems, recv_sems, src_thru, land_thru = handle

    def body(*refs):
        src, land = refs[:n], refs[n:2 * n]
        send_sems, recv_sems = refs[2 * n], refs[2 * n + 1]
        x, y, c = lax.axis_index("x"), lax.axis_index("y"), lax.axis_index("c")
        for a in range(n):
            for j, (px, py) in enumerate([(1 - x, y), (x, 1 - y), (1 - x, 1 - y)]):
                pk = 2 * px + py
                copy = pltpu.make_async_remote_copy(
                    src_ref=(src[a].at[c] if half else src[a]) if bcast else src[a].at[pk],
                    dst_ref=_slot(land[a], pk, axes[a], sizes[a]),
                    send_sem=send_sems.at[3 * a + j], recv_sem=recv_sems.at[3 * a + j], device_id=(px, py, c),
                    device_id_type=MESH)
                copy.wait_send()
                copy.wait_recv()

    hbm = pl.BlockSpec(memory_space=pltpu.HBM)
    sem = pl.BlockSpec(memory_space=pltpu.SEMAPHORE)
    outs = _pcall(
        body, name=name,
        out_shape=tuple(pltpu.HBM(t.shape, t.dtype) for t in (*src_thru, *land_thru)),
        in_specs=[hbm] * (2 * n) + [sem, sem, pl.BlockSpec(memory_space=pl.ANY)], out_specs=tuple([hbm] * (2 * n)),
        input_output_aliases={k: k for k in range(2 * n)},
        compiler_params=pltpu.CompilerParams(has_side_effects=pltpu.SideEffectType.DATAFLOW_SIDE_EFFECTING),
    )(*src_thru, *land_thru, send_sems, recv_sems, after)
    return list(outs[:n]), list(outs[n:])


def _tie(name, v, token):
    def body(v_ref, token_ref, o_ref):
        del v_ref, token_ref, o_ref

    any_spec = pl.BlockSpec(memory_space=pl.ANY)
    return _pcall(body, name=name, out_shape=jax.ShapeDtypeStruct(v.shape, v.dtype), in_specs=[any_spec, any_spec],
                  out_specs=any_spec, input_output_aliases={0: 0})(v, token)


def _fill_own(landed, own, me, bcast):
    blk = own if bcast else lax.dynamic_index_in_dim(own, me, 0, keepdims=False)
    return lax.dynamic_update_index_in_dim(landed, blk, me, 0)


def _swap_sibling(name, srcs, by_core=False):
    n = len(srcs)

    def body(*refs):
        src, out = refs[:n], refs[n:2 * n]
        send_sems, recv_sems = refs[2 * n:]
        x, y, c = lax.axis_index("x"), lax.axis_index("y"), lax.axis_index("c")
        copies = []
        for a in range(n):
            send = pltpu.make_async_remote_copy(
                src_ref=src[a], dst_ref=out[a].at[c] if by_core else out[a], send_sem=send_sems.at[a],
                recv_sem=recv_sems.at[a], device_id=(x, y, 1 - c), device_id_type=MESH)
            send.start()
            arrive = pltpu.make_async_remote_copy(
                src_ref=src[a], dst_ref=out[a].at[1 - c] if by_core else out[a], send_sem=send_sems.at[a],
                recv_sem=recv_sems.at[a], device_id=(x, y, 1 - c), device_id_type=MESH)
            copies.append((send, arrive))
        for send, arrive in copies:
            send.wait_send()
            arrive.wait_recv()

    any_spec = pl.BlockSpec(memory_space=pl.ANY)
    return _pcall(
        body, name=name,
        out_shape=[jax.ShapeDtypeStruct(((2,) + s.shape) if by_core else s.shape, s.dtype) for s in srcs],
        in_specs=[any_spec] * n, out_specs=[any_spec] * n,
        scratch_shapes=[pltpu.SemaphoreType.DMA((n,)), pltpu.SemaphoreType.DMA((n,))],
    )(*srcs)


def _mod_fwd(c16, mod_w, mod_b_shard):
    nl, D, S = mod_w.shape

    def body(c_ref, w_ref, b_ref, o_ref):
        s = _silu(c_ref[...]).astype(BF16)
        o_ref[...] = jnp.dot(s, w_ref[...].astype(BF16), preferred_element_type=F32) + b_ref[...]

    return _pcall(
        body, name="mod_fwd", grid=(nl,),
        in_specs=[pl.BlockSpec((16, D), lambda l: (0, 0)), pl.BlockSpec((None, D, S), lambda l: (l, 0, 0)),
                  pl.BlockSpec((None, 1, S), lambda l: (l, 0, 0))],
        out_specs=pl.BlockSpec((None, 16, S), lambda l: (l, 0, 0)),
        out_shape=jax.ShapeDtypeStruct((nl, 16, S), F32), compiler_params=_cparams(1),
    )(c16, mod_w, mod_b_shard)


def _mod_w_update(s16t, dm16, w, m, v):
    nl, D, S = w.shape
    tm = _row_tile(D, 256)

    def body(s_ref, dm_ref, w_ref, m_ref, v_ref, g_ref, dl_ref, nm_ref, nv_ref):
        g = jnp.dot(s_ref[...], dm_ref[...], preferred_element_type=F32, precision=HIGHEST)
        g, dl, nm, nv = _f_adamw(w_ref[...], m_ref[...], v_ref[...], g, jnp.zeros_like(g))
        g_ref[...] = g
        dl_ref[...] = dl
        nm_ref[...] = nm
        nv_ref[...] = nv

    big = pl.BlockSpec((None, tm, S), lambda l, i: (l, i, 0))
    return _pcall(
        body, name="mod_w_update", grid=(nl, D // tm),
        in_specs=[pl.BlockSpec((tm, 16), lambda l, i: (i, 0)), pl.BlockSpec((None, 16, S), lambda l, i: (l, 0, 0)),
                  big, big, big],
        out_specs=[big] * 4, out_shape=[jax.ShapeDtypeStruct(w.shape, F32)] * 4, compiler_params=_cparams(2),
    )(s16t, dm16, w, m, v)


def _size(shape):
    n = 1
    for d in shape:
        n *= d
    return n


def _pack(arrs):
    pieces = []
    for a in arrs:
        flat = a.reshape(-1).astype(F32)
        pieces.append(jnp.pad(flat, (0, _round_up(flat.shape[0], LANE) - flat.shape[0])).reshape(-1, LANE))
    buf = jnp.concatenate(pieces, axis=0)
    return jnp.pad(buf, ((0, _round_up(buf.shape[0], SUBLANE) - buf.shape[0]), (0, 0)))


def _unpack(buf, shapes):
    lead = buf.shape[:-2]
    out, row = [], 0
    for s in shapes:
        n = _size(s)
        rows = _cdiv(n, LANE)
        piece = buf[..., row:row + rows, :].reshape(lead + (rows * LANE,))
        out.append(piece[..., :n].reshape(lead + tuple(s)))
        row += rows
    return out


def _adamw_many(name, ws, ms, vs, gs):
    n = len(ws)

    def body(*refs):
        for k in range(n):
            res = _f_adamw(refs[k][...], refs[n + k][...], refs[2 * n + k][...], refs[3 * n + k][...], 0.0)
            for j in range(4):
                refs[(4 + j) * n + k][...] = res[j]

    vmem = pl.BlockSpec(memory_space=pltpu.VMEM)
    res = _pcall(body, name=name, out_shape=[jax.ShapeDtypeStruct(w.shape, F32) for _ in range(4) for w in ws],
                 in_specs=[vmem] * (4 * n), out_specs=[vmem] * (4 * n))(*ws, *ms, *vs, *gs)
    return [tuple(res[j * n + k] for j in range(4)) for k in range(n)]


SHARD_AXIS = {
    "mod_w": 2, "ssd_w_in": 2, "ssd_conv_w": 2, "ssd_w_out": 1, "conf_w_pw1": 2, "conf_b_pw1": 1, "conf_w_dw": 2,
    "conf_b_dw": 1, "conf_ln_w": 1, "conf_ln_b": 1, "conf_w_pw2": 1, "conf_b_pw2": 1, "ffn_w_up": 2,
    "ffn_conv_w": 3, "ffn_w_down": 1,
}
BIG = ("ssd_w_in", "ssd_w_out", "conf_w_pw1", "conf_w_pw2", "ffn_w_up", "ffn_w_down")
WEIGHTS = ("c_ctx", "mod_w", "mod_b", "norm1_w", "norm2_w", "ssd_w_in", "ssd_conv_w", "ssd_conv_b", "ssd_dt_bias",
           "ssd_a_log", "ssd_d", "ssd_norm_w", "ssd_w_out", "conf_w_pw1", "conf_b_pw1", "conf_w_dw", "conf_b_dw",
           "conf_ln_w", "conf_ln_b", "conf_w_pw2", "conf_b_pw2", "ffn_w_up", "ffn_conv_w", "ffn_conv_b",
           "ffn_w_down", "final_norm_w")
SMALL = tuple(n for n in WEIGHTS if n not in BIG and n != "mod_w")
SMALL_SHARDED = tuple(n for n in SMALL if n in SHARD_AXIS)


def _unshard(stacked, axis):
    return jnp.concatenate([stacked[k] for k in range(N_CHIPS)], axis=axis)


def _to_blocks(full, axis):
    return jnp.stack(jnp.split(full, N_CHIPS, axis=axis))


def _par(v):
    v = v.reshape(-1, v.shape[-1])
    return v[:, None, :]


def kernel(x, c, ctx, c_ctx, mod_w, mod_b, norm1_w, norm2_w, ssd_w_in, ssd_conv_w, ssd_conv_b, ssd_dt_bias, ssd_a_log, ssd_d, ssd_norm_w, ssd_w_out, conf_w_pw1, conf_b_pw1, conf_w_dw, conf_b_dw, conf_ln_w, conf_ln_b, conf_w_pw2, conf_b_pw2, ffn_w_up, ffn_conv_w, ffn_conv_b, ffn_w_down, final_norm_w, loss_target, m_c_ctx, m_mod_w, m_mod_b, m_norm1_w, m_norm2_w, m_ssd_w_in, m_ssd_conv_w, m_ssd_conv_b, m_ssd_dt_bias, m_ssd_a_log, m_ssd_d, m_ssd_norm_w, m_ssd_w_out, m_conf_w_pw1, m_conf_b_pw1, m_conf_w_dw, m_conf_b_dw, m_conf_ln_w, m_conf_ln_b, m_conf_w_pw2, m_conf_b_pw2, m_ffn_w_up, m_ffn_conv_w, m_ffn_conv_b, m_ffn_w_down, m_final_norm_w, v_c_ctx, v_mod_w, v_mod_b, v_norm1_w, v_norm2_w, v_ssd_w_in, v_ssd_conv_w, v_ssd_conv_b, v_ssd_dt_bias, v_ssd_a_log, v_ssd_d, v_ssd_norm_w, v_ssd_w_out, v_conf_w_pw1, v_conf_b_pw1, v_conf_w_dw, v_conf_b_dw, v_conf_ln_w, v_conf_ln_b, v_conf_w_pw2, v_conf_b_pw2, v_ffn_w_up, v_ffn_conv_w, v_ffn_conv_b, v_ffn_w_down, v_final_norm_w):
    given = dict(locals())
    W = {n: given[n] for n in WEIGHTS}
    Mo = {n: given["m_" + n] for n in WEIGHTS}
    Vo = {n: given["v_" + n] for n in WEIGHTS}

    ax, ay, ac = lax.axis_index("x"), lax.axis_index("y"), lax.axis_index("c")
    chip = 2 * ax + ay
    dev = 4 * ax + 2 * ay + ac

    D = x.shape[-1]
    L, Lc = x.shape[1], ctx.shape[1]
    T0 = L + Lc
    H = ssd_a_log.shape[-1]
    DI = ssd_norm_w.shape[-1]
    P = DI // H
    CD = ssd_conv_b.shape[-1]
    N = SSD_STATE
    G = (CD - DI) // (2 * N)
    FH = ffn_conv_b.shape[-1]
    KS = ssd_conv_w.shape[1]
    KC = conf_w_dw.shape[1]
    ncc = Lc // SSD_CHUNK

    shard_b = {n: W[n].astype(BF16) for n in BIG}

    small_shard_shapes = [W[n].shape for n in SMALL_SHARDED]
    f1 = _allgather8("gather_small", _pack([c] + [W[n] for n in SMALL_SHARDED]))
    parts = _unpack(f1, [c.shape] + small_shard_shapes)
    Wf = dict(W)
    for n, p in zip(SMALL_SHARDED, parts[1:]):
        Wf[n] = _unshard(p[::2], SHARD_AXIS[n])
    c16 = jnp.concatenate([parts[0].reshape(N_DEV, D), c_ctx[None, :], jnp.zeros((16 - N_DEV - 1, D), F32)], axis=0)

    S_mod = mod_w.shape[-1]
    mod_b_shard = lax.dynamic_slice_in_dim(mod_b, chip * S_mod, S_mod, axis=1)[:, None, :]
    mod_part = _mod_fwd(c16, mod_w, mod_b_shard)
    f2 = _allgather8("gather_mod", mod_part.reshape(2 * 16, S_mod))
    mods = jnp.concatenate([f2[2 * k].reshape(2, 16, S_mod) for k in range(N_CHIPS)], axis=-1)
    my = lax.dynamic_slice_in_dim(mods, dev, 1, axis=1)[:, 0]
    sh1, sc1, g1, sh2, sc2, g2 = [[my[l, k * D:(k + 1) * D] for l in range(2)] for k in range(6)]
    csh1, csc1 = mods[0, N_DEV, 0:D], mods[0, N_DEV, D:2 * D]

    in_halves = shard_b["ssd_w_in"].reshape(2, D // 2, ssd_w_in.shape[-1])
    gather_a, token = _exchange4_start("gather_w_in_start", [in_halves], True, mods, half=True)
    csc1 = _tie("tie_gather_w_in", csc1, token)

    def full_weight(n, own, landed):
        if landed.ndim == own.ndim:
            ax = SHARD_AXIS[n]
            return lax.dynamic_update_slice_in_dim(landed, own, chip * own.shape[ax], ax)
        return _unshard(_fill_own(landed, own, chip, True), SHARD_AXIS[n])

    xl = x[0]
    rows0 = (ctx[0], xl)
    n1w0, n1w1 = _par(norm1_w[0]), _par(norm1_w[1])
    sc_seg = jnp.stack([csc1, sc1[0]])[:, None, :]
    sh_seg = jnp.stack([csh1, sh1[0]])[:, None, :]

    a0 = _rw_fwd("l0_modnorm1", _f_modnorm, [], [n1w0, sc_seg, sh_seg], [D], T=T0, seg_rows=(Lc,), head=rows0,
                 out_dtypes=[BF16])
    rest = [n for n in BIG if n != "ssd_w_in"]
    for n in rest:
        a0 = _tie("tie_cast_" + n, a0, shard_b[n])
    (own_in,), (landed_in,) = _exchange4_wait("gather_w_in_wait", gather_a, a0)
    mine = _fill_own(landed_in, lax.dynamic_index_in_dim(own_in, ac, 0, keepdims=False), chip, True)
    (halves,) = _swap_sibling("swap_w_in", [mine], by_core=True)
    halves = lax.dynamic_update_index_in_dim(halves, mine, ac, 0)
    w_in = jnp.concatenate([halves[:, k].reshape(D, -1) for k in range(N_CHIPS)], axis=1)
    landed_in = halves
    def start_gather(tag, names, dep):
        handle, tok = _exchange4_start("gather_" + tag + "_start", [shard_b[n] for n in names], True, dep,
                                       axes=[1 if SHARD_AXIS[n] == 1 else None for n in names])
        return (names, handle), tok

    def finish_gather(tag, group, after):
        names, handle = group
        return {n: full_weight(n, own, g)
                for n, own, g in zip(names, *_exchange4_wait("gather_" + tag + "_wait", handle, after))}

    gather_b, token = start_gather("mix", ["ssd_w_out", "conf_w_pw1", "conf_w_pw2"], landed_in)
    gather_c, token = start_gather("ffn", ["ffn_w_up", "ffn_w_down"], token)
    a0 = _tie("tie_gather_rest", a0, token)
    proj = _mm(a0, w_in, name="l0_w_in")
    seg_taps = [(k - KS // 2, ("seg", Lc)) for k in range(KS)]
    xbc_pre, xbc = _conv_fwd("l0_conv", proj, DI, CD, Wf["ssd_conv_w"][0], ssd_conv_b, seg_taps, act=True)
    dt_raw = proj[:, DI + CD:]
    dt_bias = _par(ssd_dt_bias.reshape(1, 2 * H))
    dt = _rw_fwd("l0_softplus", _f_softplus, [dt_raw], [dt_bias], [2 * H])
    dt_t = dt.T
    dtr = (dt_t[:H, None, :], dt_t[H:, None, :])
    a_all = -jnp.exp(ssd_a_log.reshape(2, H, 1, 1))
    a_neg = (a_all[0], a_all[1])
    (y_f, y_b), s_enter = _ssd_fwd(xbc, DI, DI + G * N, dtr, a_neg, P, ncc)
    gate_rows = [y_f, y_b, (xbc, 0, DI, Lc), (proj, 0, DI, Lc)]
    d_rep = _par(jnp.repeat(ssd_d[0], P))
    ssd_nw = _par(ssd_norm_w[0])
    yn = _rw_fwd("l0_ssd_gate", _f_ssd_gate, gate_rows, [d_rep, ssd_nw], [DI], T=L, out_dtypes=[BF16])
    Wb = finish_gather("mix", gather_b, yn)
    w_out, w_pw1, w_pw2 = Wb["ssd_w_out"][0], Wb["conf_w_pw1"][0], Wb["conf_w_pw2"][0]
    mix0 = _mm(yn, w_out, name="l0_w_out")
    g1_0, g2_0, g1_1, g2_1 = _par(g1[0]), _par(g2[0]), _par(g1[1]), _par(g2[1])
    h1 = _rw_fwd("l0_res1", _f_gate_res, [xl, mix0], [g1_0], [D])
    Wb = finish_gather("ffn", gather_c, h1)
    w_up, w_dn = Wb["ffn_w_up"], Wb["ffn_w_down"]

    grid_taps = [((i - 1) * GRID_W + (j - 1), (None if j == 1 else ("col", j - 1))) for i in range(3) for j in range(3)]

    def ffn_fwd(l, h, tag):
        a = _rw_fwd(tag + "_modnorm2", _f_modnorm, [h], [_par(norm2_w[l]), _par(sc2[l]), _par(sh2[l])], [D],
                    out_dtypes=[BF16])
        hh = _mm(a, w_up[l], name=tag + "_w_up")
        gc = _conv_fwd(tag + "_ffn_conv", hh, FH, FH, Wf["ffn_conv_w"][l].reshape(9, FH), ffn_conv_b[l][None, :],
                       grid_taps)
        act = _rw_fwd(tag + "_act", _f_ffn_act, [(hh, 0, FH), gc], [], [FH], col_tile=_tile(FH, 1536),
                      out_dtypes=[BF16])
        dn = _mm(act, w_dn[l], name=tag + "_w_down")
        return a, hh, gc, act, dn

    a1, hh0, gc0, act0, dn0 = ffn_fwd(0, h1, "l0")
    h2 = _rw_fwd("l0_res2", _f_gate_res, [h1, dn0], [g2_0], [D])

    a2 = _rw_fwd("l1_modnorm1", _f_modnorm, [h2], [n1w1, _par(sc1[1]), _par(sh1[1])], [D], out_dtypes=[BF16])
    pw = _mm(a2, w_pw1, name="l1_pw1")
    b_pw1 = Wf["conf_b_pw1"][0]
    glu = _rw_fwd("l1_glu", _f_glu, [(pw, 0, D), (pw, D, D)], [_par(b_pw1[:D]), _par(b_pw1[D:])], [D])
    conf_taps = [(k - KC // 2, None) for k in range(KC)]
    cv = _conv_fwd("l1_conv", glu, 0, D, Wf["conf_w_dw"][0], Wf["conf_b_dw"], conf_taps)
    ln_w, ln_b = _par(Wf["conf_ln_w"][0]), _par(Wf["conf_ln_b"][0])
    ls = _rw_fwd("l1_ln_silu", _f_ln_silu, [cv], [ln_w, ln_b], [D], out_dtypes=[BF16])
    p2 = _mm(ls, w_pw2, name="l1_pw2")
    b_pw2 = _par(Wf["conf_b_pw2"][0])
    h3 = _rw_fwd("l1_res1", _f_gate_res_bias, [h2, p2], [g1_1, b_pw2], [D])
    a3, hh1, gc1, act1, dn1 = ffn_fwd(1, h3, "l1")
    h4 = _rw_fwd("l1_res2", _f_gate_res, [h3, dn1], [g2_1], [D])

    fnw = final_norm_w[None, :]
    tgt = loss_target[0]
    loss_local = _loss_fwd(h4, tgt, fnw)[0, 0]
    loss = lax.psum(loss_local, ("x", "y", "c"))

    G_full = {}
    reduces = {}

    def start_reduce(tag, items, dep):
        def blocks_of(g, ax):
            if g.ndim == 3:
                return g
            return g.reshape(N_CHIPS, g.shape[0] // N_CHIPS, g.shape[1]) if ax == 0 else _to_blocks(g, ax)

        blocks = [blocks_of(g, ax).astype(BF16) for _, g, ax in items]
        handle, tok = _exchange4_start("reduce_" + tag + "_start", blocks, False, dep)
        reduces[tag] = ([n for n, _, _ in items], handle)
        return tok
    ones = jnp.ones((L, 1), F32)
    (dh4,), (dfnw,) = _rw_bwd("loss_bwd", _f_loss_rows, [h4, tgt], [_par(final_norm_w)], [ones],
                              row_grad=[True, False], par_grad=[True])
    G_full["final_norm_w"] = dfnw.reshape(D)

    def ffn_bwd(l, h, saved, g2_l, dh_out, tag):
        a, hh, gc, act, dn = saved
        (ddn,), (dg2,) = _rw_bwd(tag + "_res2_bwd", _f_gate, [dn], [g2_l], [dh_out],
                                 row_grad=[True], par_grad=[True], row_dtypes=[BF16])
        dact = _mm(ddn, w_dn[l], tb=True, name=tag + "_w_down_dx")
        dwdn = _mm(act, ddn, ta=True, name=tag + "_w_down_dw", out_dtype=BF16)
        (dval, dgc), _ = _rw_bwd(tag + "_act_bwd", _f_ffn_act, [(hh, 0, FH), gc], [], [dact],
                                 row_grad=[True, True], par_grad=[], col_tile=_tile(FH, 1536), row_dtypes=[BF16, F32])
        dgin, dcw, dcb = _conv_bwd(tag + "_ffn_conv_bwd", hh, FH, FH, Wf["ffn_conv_w"][l].reshape(9, FH), dgc,
                                   grid_taps, du_dtype=BF16)
        dhh = jnp.concatenate([dval, dgin], axis=1)
        da = _mm(dhh, w_up[l], tb=True, name=tag + "_w_up_dx")
        dwup = _mm(a, dhh, ta=True, name=tag + "_w_up_dw", out_dtype=BF16, col_blocks=N_CHIPS)
        (dh,), (dn2w, dsc2, dsh2) = _rw_bwd(
            tag + "_modnorm2_bwd", _f_modnorm, [h], [_par(norm2_w[l]), _par(sc2[l]), _par(sh2[l])], [da],
            row_grad=[True], par_grad=[True, True, True], add=dh_out)
        return dh, dict(w_down=dwdn, w_up=dwup, conv_w=dcw.reshape(3, 3, FH), conv_b=dcb.reshape(FH),
                        n2w=dn2w.reshape(D), sc2=dsc2.reshape(D), sh2=dsh2.reshape(D), g2=dg2.reshape(D))

    dh3, gf1 = ffn_bwd(1, h3, (a3, hh1, gc1, act1, dn1), g2_1, dh4, "l1")
    (dp2,), (dg1_1, db_pw2) = _rw_bwd("l1_res1_bwd", _f_gate_bias, [p2], [g1_1, b_pw2], [dh3],
                                      row_grad=[True], par_grad=[True, True], row_dtypes=[BF16])
    dls = _mm(dp2, w_pw2, tb=True, name="l1_pw2_dx")
    dw_pw2 = _mm(ls, dp2, ta=True, name="l1_pw2_dw", out_dtype=BF16)
    (dcv,), (dln_w, dln_b) = _rw_bwd("l1_ln_silu_bwd", _f_ln_silu, [cv], [ln_w, ln_b], [dls],
                                     row_grad=[True], par_grad=[True, True])
    dglu, dw_dw, db_dw = _conv_bwd("l1_conv_bwd", glu, 0, D, Wf["conf_w_dw"][0], dcv, conf_taps)
    (dpa, dpg), (dba, dbg) = _rw_bwd("l1_glu_bwd", _f_glu, [(pw, 0, D), (pw, D, D)],
                                     [_par(b_pw1[:D]), _par(b_pw1[D:])], [dglu],
                                     row_grad=[True, True], par_grad=[True, True], row_dtypes=[BF16, BF16])
    dpw = jnp.concatenate([dpa, dpg], axis=1)
    da2 = _mm(dpw, w_pw1, tb=True, name="l1_pw1_dx")
    dw_pw1 = _mm(a2, dpw, ta=True, name="l1_pw1_dw", out_dtype=BF16, col_blocks=N_CHIPS)
    (dh2,), (dn1w1, dsc1_1, dsh1_1) = _rw_bwd(
        "l1_modnorm1_bwd", _f_modnorm, [h2], [n1w1, _par(sc1[1]), _par(sh1[1])], [da2],
        row_grad=[True], par_grad=[True, True, True], add=dh3)
    G_full["conf_b_pw2"] = db_pw2.reshape(1, D)
    G_full["conf_ln_w"], G_full["conf_ln_b"] = dln_w.reshape(1, D), dln_b.reshape(1, D)
    G_full["conf_w_dw"], G_full["conf_b_dw"] = dw_dw[None], db_dw.reshape(1, D)
    G_full["conf_b_pw1"] = jnp.concatenate([dba.reshape(1, D), dbg.reshape(1, D)], axis=1)

    token = start_reduce("l1", [("conf_w_pw2", dw_pw2, 0), ("conf_w_pw1", dw_pw1, 1), ("ffn_w_up1", gf1["w_up"], 1),
                                ("ffn_w_down1", gf1["w_down"], 0)], dw_pw2)
    dh2 = _tie("tie_reduce_l1", dh2, token)
    dh1, gf0 = ffn_bwd(0, h1, (a1, hh0, gc0, act0, dn0), g2_0, dh2, "l0")
    G_full["ffn_conv_w"] = jnp.stack([gf0["conv_w"], gf1["conv_w"]])
    G_full["ffn_conv_b"] = jnp.stack([gf0["conv_b"], gf1["conv_b"]])

    (dmix,), (dg1_0,) = _rw_bwd("l0_res1_bwd", _f_gate, [mix0], [g1_0], [dh1],
                                row_grad=[True], par_grad=[True], row_dtypes=[BF16])
    dyn = _mm(dmix, w_out, tb=True, name="l0_w_out_dx")
    dw_out = _mm(yn, dmix, ta=True, name="l0_w_out_dw", out_dtype=BF16)
    token = start_reduce("l0", [("ffn_w_up0", gf0["w_up"], 1), ("ffn_w_down0", gf0["w_down"], 0),
                                ("ssd_w_out", dw_out, 0)], dw_out)
    dyn = _tie("tie_reduce_l0", dyn, token)
    (dy_lat, dxs_gate, dz_lat), (dd_rep, dssd_nw) = _rw_bwd(
        "l0_ssd_gate_bwd", _f_ssd_gate, gate_rows, [d_rep, ssd_nw], [dyn],
        row_grad=[True, False, True, True], par_grad=[True, True], T=L, row_dtypes=[F32, F32, BF16])
    g_f, g_b = _ssd_bwd(xbc, DI, DI + G * N, dtr, a_neg, s_enter, dy_lat, P, ncc)
    silu_bwd = functools.partial(_rw_bwd, f=_silu, pars=[], row_grad=[True], par_grad=[], T=T0)
    (dxs_pre,), _ = silu_bwd("l0_silu_bwd_x", rows=[(xbc_pre, 0, DI)], cot_fn=lambda p, q, r: p + q + r,
                             cots=[g_f[0], g_b[0], (dxs_gate, 0, DI, -Lc)],
                             col_tile=_tile(DI, 1024))
    (db_pre,), _ = silu_bwd("l0_silu_bwd_b", rows=[(xbc_pre, DI, G * N)], cot_fn=lambda p, q: p + q,
                            cots=[g_f[1], g_b[1]], col_tile=_tile(G * N, 1024))
    (dc_pre,), _ = silu_bwd("l0_silu_bwd_c", rows=[(xbc_pre, DI + G * N, G * N)], cot_fn=lambda p, q: p + q,
                            cots=[g_f[2], g_b[2]], col_tile=_tile(G * N, 1024))
    conv_w0 = Wf["ssd_conv_w"][0]
    pieces = []
    for tag, off, width, g_pre in (("x", 0, DI, dxs_pre), ("b", DI, G * N, db_pre), ("c", DI + G * N, G * N, dc_pre)):
        pieces.append(_conv_bwd("l0_conv_bwd_" + tag, proj, DI + off, width, conv_w0[:, off:off + width], g_pre,
                                seg_taps, du_dtype=BF16))
    dconv_in = [p[0] for p in pieces]
    dcw0 = jnp.concatenate([p[1] for p in pieces], axis=1)
    dcb0 = jnp.concatenate([p[2] for p in pieces], axis=1)
    ddt = jnp.concatenate([g_f[3][:, 0, :].T, g_b[3][:, 0, :].T], axis=1)
    (ddt_raw,), (ddt_bias,) = _rw_bwd("l0_softplus_bwd", _f_softplus, [dt_raw], [dt_bias], [ddt],
                                      row_grad=[True], par_grad=[True], row_dtypes=[BF16])
    dproj = jnp.concatenate([jnp.pad(dz_lat, ((Lc, 0), (0, 0))), *dconv_in, ddt_raw], axis=1)
    da0 = _mm(dproj, w_in, tb=True, name="l0_w_in_dx")
    dw_in = _mm(a0, dproj, ta=True, name="l0_w_in_dw", out_dtype=BF16)
    token = start_reduce("in", [("ssd_w_in", dw_in, 1)], dw_in)
    da0 = _tie("tie_reduce_in", da0, token)
    (dhcat,), (dn1w0, dsc_seg, dsh_seg) = _rw_bwd(
        "l0_modnorm1_bwd", _f_modnorm, [], [n1w0, sc_seg, sh_seg], [da0], T=T0, head=rows0,
        row_grad=[True], par_grad=[True, True, True], seg_rows=(Lc,), add=(dh1, 0, D, -Lc))
    grad_x = dhcat[Lc:][None]

    da_heads = jnp.stack([g[4][:, 0, 0].reshape(G, T0 // SSD_CHUNK, H // G).sum(axis=1).reshape(H)
                          for g in (g_f, g_b)])[None]
    G_full["ssd_a_log"] = da_heads * (-jnp.exp(ssd_a_log))
    G_full["ssd_dt_bias"] = ddt_bias.reshape(1, 2, H)
    G_full["ssd_d"] = dd_rep.reshape(H, P).sum(axis=1)[None]
    G_full["ssd_norm_w"] = dssd_nw.reshape(1, DI)
    G_full["ssd_conv_w"], G_full["ssd_conv_b"] = dcw0[None], dcb0.reshape(1, CD)
    G_full["norm1_w"] = jnp.stack([dn1w0.reshape(D), dn1w1.reshape(D)])
    G_full["norm2_w"] = jnp.stack([gf0["n2w"], gf1["n2w"]])

    zD = jnp.zeros((D,), F32)
    dm_own = jnp.stack([
        jnp.concatenate([dsh_seg[1, 0], dsc_seg[1, 0], dg1_0.reshape(D), gf0["sh2"], gf0["sc2"], gf0["g2"]]),
        jnp.concatenate([dsh1_1.reshape(D), dsc1_1.reshape(D), dg1_1.reshape(D), gf1["sh2"], gf1["sc2"], gf1["g2"]]),
    ])
    dmc_own = jnp.concatenate([dsh_seg[0, 0], dsc_seg[0, 0], zD, zD, zD, zD])

    out = {}

    def finish_reduce(tags, after, swap_name):
        partial = {}
        for tag in tags:
            names, handle = reduces[tag]
            blocks, landed = _exchange4_wait("reduce_" + tag + "_wait", handle, after)
            for n, blk, own in zip(names, landed, blocks):
                r = _fill_own(blk, own, chip, False)
                partial[n] = _sum_leading("sum4_" + n, r.reshape(N_CHIPS, -1, r.shape[-1]),
                                          (0, 1, 2, 3), out_dtype=BF16).reshape(r.shape[1:])
        for n in ("ffn_w_up", "ffn_w_down"):
            if n + "0" in partial:
                partial[n] = jnp.stack([partial.pop(n + "0"), partial.pop(n + "1")])
        names = [n for n in BIG if n in partial]
        mine = [partial[n].reshape(W[n].shape) for n in names]
        for n, own, sib in zip(names, mine, _swap_sibling(swap_name, mine)):
            out[n] = _adamw("adamw_" + n, W[n], Mo[n], Vo[n], own, sib)
        return names

    early = finish_reduce(["l1", "l0"], dhcat, "swap_grads_early")

    small_sum_names = [n for n in SMALL if n not in ("c_ctx", "mod_b")]
    sum_part = [G_full[n] for n in small_sum_names] + [dmc_own]
    packed = _tie("tie_small_grads", _pack(sum_part + [dm_own]), out[early[-1]][1])
    gat = _allgather8("gather_small_grads", packed)
    total = _sum_leading("sum_small_grads", gat, tuple(range(N_DEV)))
    summed = _unpack(total, [a.shape for a in sum_part])
    Gs = dict(zip(small_sum_names, summed[:-1]))
    dmc_tot = summed[-1]
    dm_all = _unpack(gat, [a.shape for a in sum_part] + [dm_own.shape])[-1].transpose(1, 0, 2)
    dm16 = jnp.concatenate([dm_all, jnp.stack([dmc_tot, jnp.zeros_like(dmc_tot)])[:, None, :],
                            jnp.zeros((2, 16 - N_DEV - 1, 6 * D), F32)], axis=1)
    Gs["mod_b"] = _sum_leading("sum_mod_b", dm16.transpose(1, 0, 2).reshape(16, 2 * 6 * D // LANE, LANE),
                               tuple(range(N_DEV + 1))).reshape(2, 6 * D)

    dm16_shard = lax.dynamic_slice_in_dim(dm16, chip * S_mod, S_mod, axis=2)
    ds16 = _mm(dm16_shard[0], mod_w[0], tb=True, precision=HIGHEST, name="c_ctx_dx")
    sig = jax.nn.sigmoid(c_ctx)
    dcc_part = ds16[N_DEV] * (sig * (1.0 + c_ctx * (1.0 - sig)))
    gat_cc = _allgather8("gather_c_ctx_grad", _pack([dcc_part]))
    Gs["c_ctx"] = _sum_leading("sum_c_ctx_grad", gat_cc, (0, 2, 4, 6)).reshape(-1)[:D]

    s16t = _silu(c16).T
    out["mod_w"] = _mod_w_update(s16t, dm16_shard, mod_w, m_mod_w, v_mod_w)
    finish_reduce(["in"], out["mod_w"][0], "swap_grads_late")

    def own(n, full):
        if n in SHARD_AXIS:
            size = W[n].shape[SHARD_AXIS[n]]
            return lax.dynamic_slice_in_dim(full, chip * size, size, axis=SHARD_AXIS[n])
        return full

    def two_d(a):
        return a.reshape(1, -1) if a.ndim == 1 else a

    g_small = [own(n, Gs[n].reshape(Wf[n].shape)) for n in SMALL]
    res = _adamw_many("adamw_small", [two_d(W[n]) for n in SMALL], [two_d(Mo[n]) for n in SMALL],
                      [two_d(Vo[n]) for n in SMALL], [two_d(g) for g in g_small])
    for n, r in zip(SMALL, res):
        out[n] = tuple(t.reshape(W[n].shape) for t in r)

    grads = [out[n][0] for n in WEIGHTS]
    deltas = [out[n][1] for n in WEIGHTS]
    new_m = [out[n][2] for n in WEIGHTS]
    new_v = [out[n][3] for n in WEIGHTS]
    return (loss, grad_x, *grads, *deltas, *new_m, *new_v)
```

```python
import functools

import jax
import jax.numpy as jnp
from jax import lax
from jax.experimental import pallas as pl
from jax.experimental.pallas import tpu as pltpu

F32 = jnp.float32
BF16 = jnp.bfloat16
MESH = pl.DeviceIdType.MESH
HIGHEST = lax.Precision.HIGHEST

VMEM_LIMIT_BYTES = 48 * 1024 * 1024
LANE = 128
SUBLANE = 8

SSD_STATE = 128
SSD_CHUNK = 128
GRID_W = 64
EPS = 1e-6
N_CHIPS = 4
N_DEV = 8

ADAM_LR = 0.001
ADAM_B1 = 0.9
ADAM_B2 = 0.999
ADAM_EPS = 1e-08
ADAM_WD = 0.01
ADAM_STEP = 10


def _pcall(body, **kw):
    return pl.pallas_call(body, **kw)


def _cparams(n_grid):
    return pltpu.CompilerParams(dimension_semantics=("arbitrary",) * n_grid, vmem_limit_bytes=VMEM_LIMIT_BYTES)


def _cdiv(a, b):
    return -(-a // b)


def _round_up(a, b):
    return _cdiv(a, b) * b


def _tile(n, cap):
    if n <= cap:
        return n
    best = None
    for t in range(LANE, cap + 1, LANE):
        if n % t == 0:
            best = t
    if best is None:
        npad = _round_up(n, LANE)
        for t in range(LANE, cap + 1, LANE):
            if npad % t == 0:
                best = t
    return best


def _row_tile(n, cap, also=()):
    best = None
    for step in (2 * SUBLANE, SUBLANE):
        for t in range(step, min(cap, n) + 1, step):
            if n % t == 0 and all(a % t == 0 for a in also):
                best = t
        if best is not None:
            break
    assert best is not None, (n, cap, also)
    return best


def _silu(v):
    return v * jax.nn.sigmoid(v)


def _mm(a, b, *, name, ta=False, tb=False, precision=None, cap=1024, out_dtype=F32, col_blocks=None):
    M, K = (a.shape[1], a.shape[0]) if ta else a.shape
    N = b.shape[0] if tb else b.shape[1]
    assert K == (b.shape[1] if tb else b.shape[0]), (a.shape, b.shape, ta, tb)
    tm, tk = _tile(M, cap), _tile(K, cap + cap // 2)
    tn = _tile(N if col_blocks is None else N // col_blocks, cap + cap // 2)
    nm, nn, nk = _cdiv(M, tm), _cdiv(N, tn), _cdiv(K, tk)
    k_tail = K % tk
    exact = precision is not None

    def body(a_ref, b_ref, o_ref, acc_ref):
        k = pl.program_id(2)

        @pl.when(k == 0)
        def _():
            acc_ref[...] = jnp.zeros_like(acc_ref)

        av = a_ref[...]
        bv = b_ref[...]
        if k_tail:
            lim = K - k * tk
            ka = lax.broadcasted_iota(jnp.int32, av.shape, 0 if ta else 1)
            kb = lax.broadcasted_iota(jnp.int32, bv.shape, 1 if tb else 0)
            av = jnp.where(ka < lim, av, jnp.zeros_like(av))
            bv = jnp.where(kb < lim, bv, jnp.zeros_like(bv))
        if exact:
            av = av.astype(F32)
            bv = bv.astype(F32)
        else:
            av = av.astype(BF16)
            bv = bv.astype(BF16)
        dn = (((0 if ta else 1,), (1 if tb else 0,)), ((), ()))
        acc_ref[...] += lax.dot_general(av, bv, dn, preferred_element_type=F32, precision=precision)

        @pl.when(k == nk - 1)
        def _():
            o_ref[...] = acc_ref[...].astype(o_ref.dtype)

    a_spec = pl.BlockSpec((tk, tm), lambda i, j, k: (k, i)) if ta else pl.BlockSpec((tm, tk), lambda i, j, k: (i, k))
    b_spec = pl.BlockSpec((tn, tk), lambda i, j, k: (j, k)) if tb else pl.BlockSpec((tk, tn), lambda i, j, k: (k, j))
    if col_blocks is None:
        out_spec = pl.BlockSpec((tm, tn), lambda i, j, k: (i, j))
        out_shape = jax.ShapeDtypeStruct((M, N), out_dtype)
    else:
        per = (N // col_blocks) // tn
        assert per * tn * col_blocks == N, (N, col_blocks, tn)
        out_spec = pl.BlockSpec((None, tm, tn), lambda i, j, k: (j // per, i, j % per))
        out_shape = jax.ShapeDtypeStruct((col_blocks, M, N // col_blocks), out_dtype)
    return _pcall(
        body, name=name, grid=(nm, nn, nk), in_specs=[a_spec, b_spec], out_specs=out_spec, out_shape=out_shape,
        scratch_shapes=[pltpu.VMEM((tm, tn), F32)], compiler_params=_cparams(3),
    )(a, b)


def _norm_rows(rows):
    out = []
    for r in rows:
        if not isinstance(r, tuple):
            r = (r,)
        arr, off, width, roff = (r + (0, None, 0)[len(r) - 1:])
        out.append((arr, off, width if width is not None else arr.shape[1], roff))
    return out


def _rw_plan(T, rows, pars, seg_rows, col_tile, tm_cap):
    widths = [r[2] for r in rows]
    wmax = max(widths + [p.shape[-1] for p in pars] + [1])
    if col_tile is not None:
        assert all(w == widths[0] for w in widths) and all(p.shape[-1] == widths[0] for p in pars)
        ncol = widths[0] // col_tile
        assert ncol * col_tile == widths[0]
        wmax = col_tile
    else:
        ncol = 1
    cap = tm_cap if tm_cap is not None else max(SUBLANE, min(512, (512 * 1024) // wmax))
    tm = _row_tile(T, cap, also=tuple(seg_rows) + tuple(abs(r[3]) for r in rows if r[3]))
    bounds = tuple(s // tm for s in seg_rows)
    return widths, ncol, tm, bounds


def _rw_specs(rows, pars, ncol, tm, bounds, col_tile):
    def seg(i):
        s = 0
        for b in bounds:
            s = s + (i >= b).astype(jnp.int32)
        return s

    specs = []
    for arr, off, w, roff in rows:
        bw = col_tile if col_tile is not None else w
        assert off % bw == 0 and roff % tm == 0, (off, bw, roff, tm)
        specs.append(pl.BlockSpec((tm, bw), functools.partial(
            lambda j, i, ob, rb, last: (jnp.clip(i + rb, 0, last), ob + j),
            ob=off // bw, rb=roff // tm, last=arr.shape[0] // tm - 1)))
    for p in pars:
        bw = col_tile if col_tile is not None else p.shape[-1]
        if p.shape[0] > 1:
            specs.append(pl.BlockSpec((None, 1, bw), lambda j, i: (seg(i), 0, j)))
        else:
            specs.append(pl.BlockSpec((None, 1, bw), lambda j, i: (0, 0, j)))
    return specs, seg


def _head_rows(head):
    top, bottom = head
    return [(top, 0, None, 0), (bottom, 0, None, -top.shape[0])]


def _rw_fwd(name, f, rows, pars, out_widths, *, T=None, seg_rows=(), col_tile=None, tm_cap=None, out_dtypes=None,
            head=None):
    rows = _norm_rows((_head_rows(head) if head else []) + list(rows))
    T = rows[0][0].shape[0] if T is None else T
    widths, ncol, tm, bounds = _rw_plan(T, rows, pars, seg_rows, col_tile, tm_cap)
    in_specs, _ = _rw_specs(rows, pars, ncol, tm, bounds, col_tile)
    nr, npar, nout = len(rows), len(pars), len(out_widths)

    def body(*refs):
        vals = [r[...] for r in refs[:nr + npar]]
        if head:
            vals = [jnp.where(pl.program_id(1) < head[0].shape[0] // tm, vals[0], vals[1])] + vals[2:]
        outs = f(*vals)
        if not isinstance(outs, (tuple, list)):
            outs = (outs,)
        for o_ref, o in zip(refs[nr + npar:], outs):
            o_ref[...] = o.astype(o_ref.dtype)

    out_specs = [pl.BlockSpec((tm, col_tile if col_tile is not None else w), lambda j, i: (i, j)) for w in out_widths]
    res = _pcall(
        body, name=name, grid=(ncol, T // tm), in_specs=in_specs, out_specs=out_specs,
        out_shape=[jax.ShapeDtypeStruct((T, w), dt) for w, dt in zip(out_widths, out_dtypes or [F32] * nout)],
        compiler_params=_cparams(2),
    )(*[r[0] for r in rows], *pars)
    return res if nout > 1 else res[0]


def _rw_bwd(name, f, rows, pars, cots, *, row_grad, par_grad, T=None, seg_rows=(), col_tile=None, tm_cap=None,
            add=None, cot_fn=None, row_dtypes=None, head=None):
    rows = _norm_rows((_head_rows(head) if head else []) + list(rows))
    cots = _norm_rows(cots)
    T = rows[0][0].shape[0] if T is None else T
    extra = _norm_rows([add]) if add is not None else []
    all_rows = rows + cots + extra
    widths, ncol, tm, bounds = _rw_plan(T, all_rows, pars, seg_rows, col_tile, tm_cap)
    in_specs, seg = _rw_specs(all_rows, pars, ncol, tm, bounds, col_tile)
    nr, nc, ne, npar = len(rows), len(cots), len(extra), len(pars)
    skip = 1 if head else 0
    widths = widths[skip:]
    nrf = nr - skip
    row_idx = [k for k in range(nrf) if row_grad[k]]
    par_idx = [k for k in range(npar) if par_grad[k]]

    def body(*refs):
        i = pl.program_id(1)

        def zero_before(vals, ops):
            return [jnp.where(i + c[3] // tm >= 0, v, jnp.zeros_like(v)) if c[3] < 0 else v for v, c in zip(vals, ops)]

        row_vals = [r[...] for r in refs[:nr]]
        if head:
            row_vals = [jnp.where(i < head[0].shape[0] // tm, row_vals[0], row_vals[1])] + row_vals[2:]
        cot_vals = zero_before([r[...] for r in refs[nr:nr + nc]], cots)
        add_vals = zero_before([r[...] for r in refs[nr + nc:nr + nc + ne]], extra)
        par_vals = [r[...] for r in refs[nr + nc + ne:nr + nc + ne + npar]]
        out_refs = refs[nr + nc + ne + npar:]
        outs, vjp = jax.vjp(f, *row_vals, *par_vals)
        if cot_fn is not None:
            cot_vals = cot_fn(*cot_vals)
            if not isinstance(cot_vals, (tuple, list)):
                cot_vals = (cot_vals,)
        if isinstance(outs, (tuple, list)):
            grads = vjp(tuple(c.astype(o.dtype) for c, o in zip(cot_vals, outs)))
        else:
            grads = vjp(cot_vals[0].astype(outs.dtype))
        first_seg = i == 0
        for b in bounds:
            first_seg = first_seg | (i == b)
        for n, k in enumerate(row_idx):
            g = grads[k]
            if n == 0 and add_vals:
                g = g + add_vals[0]
            out_refs[n][...] = g.astype(out_refs[n].dtype)
        for n, k in enumerate(par_idx):
            g = grads[nrf + k]
            o_ref = out_refs[len(row_idx) + n]
            first = first_seg if pars[k].shape[0] > 1 else (i == 0)

            @pl.when(first)
            def _(o_ref=o_ref, g=g):
                o_ref[...] = g

            @pl.when(jnp.logical_not(first))
            def _(o_ref=o_ref, g=g):
                o_ref[...] += g

    out_specs, out_shape = [], []
    for k in row_idx:
        w = widths[k]
        out_specs.append(pl.BlockSpec((tm, col_tile if col_tile is not None else w), lambda j, i: (i, j)))
        out_shape.append(jax.ShapeDtypeStruct((T, w), row_dtypes[len(out_shape)] if row_dtypes else F32))
    for k in par_idx:
        p = pars[k]
        bw = col_tile if col_tile is not None else p.shape[-1]
        if p.shape[0] > 1:
            out_specs.append(pl.BlockSpec((None, 1, bw), lambda j, i: (seg(i), 0, j)))
        else:
            out_specs.append(pl.BlockSpec((None, 1, bw), lambda j, i: (0, 0, j)))
        out_shape.append(jax.ShapeDtypeStruct(p.shape, F32))
    res = _pcall(
        body, name=name, grid=(ncol, T // tm), in_specs=in_specs, out_specs=out_specs, out_shape=out_shape,
        compiler_params=_cparams(2),
    )(*[r[0] for r in all_rows], *pars)
    return list(res[:len(row_idx)]), list(res[len(row_idx):])


def _f_modnorm(h, w, sc, sh):
    y = h * lax.rsqrt(jnp.mean(h * h, axis=-1, keepdims=True) + EPS)
    return (y * w) * (1.0 + sc) + sh


def _f_gate_res(h, y, g):
    return h + g * y


def _f_gate_res_bias(h, y, g, b):
    return h + g * (y + b)


def _f_gate(y, g):
    return g * y


def _f_gate_bias(y, g, b):
    return g * (y + b)


def _f_ffn_act(val, gate):
    return _silu(gate) * val


def _f_softplus(raw, bias):
    v = raw + bias
    return jnp.maximum(v, 0.0) + jnp.log(1.0 + jnp.exp(-jnp.abs(v)))


def _f_ssd_gate(yf, yb, xs, z, d_rep, nw):
    y = (yf + yb + d_rep * xs) * _silu(z)
    return (y * lax.rsqrt(jnp.mean(y * y, axis=-1, keepdims=True) + EPS)) * nw


def _f_glu(a, g, ba, bg):
    return (a + ba) * jax.nn.sigmoid(g + bg)


def _f_ln_silu(h, w, b):
    mu = jnp.mean(h, axis=-1, keepdims=True)
    d = h - mu
    y = d * lax.rsqrt(jnp.mean(d * d, axis=-1, keepdims=True) + EPS)
    return _silu(y * w + b)


def _f_loss_rows(h, t, w):
    y = (h * lax.rsqrt(jnp.mean(h * h, axis=-1, keepdims=True) + EPS)) * w
    e = y - t
    return 0.5 * jnp.mean(e * e, axis=-1, keepdims=True)


def _f_adamw(w, m, v, ga, gb):
    g = ga.astype(F32) + gb
    m = ADAM_B1 * m + (1.0 - ADAM_B1) * g
    v = ADAM_B2 * v + (1.0 - ADAM_B2) * (g * g)
    m_hat = m / (1.0 - ADAM_B1 ** ADAM_STEP)
    v_hat = v / (1.0 - ADAM_B2 ** ADAM_STEP)
    delta = -ADAM_LR * (m_hat / (jnp.sqrt(v_hat) + ADAM_EPS) + ADAM_WD * w)
    return g, delta, m, v


def _adamw(name, w, m, v, ga, gb):
    shape = w.shape
    c = shape[-1]
    two_d = [t.reshape(-1, c) for t in (w, m, v, ga, gb)]
    rows = two_d[0].shape[0]
    pad = _round_up(rows, SUBLANE) - rows
    if pad:
        two_d = [jnp.pad(t, ((0, pad), (0, 0))) for t in two_d]
    outs = _rw_fwd(name, _f_adamw, two_d, [], [c] * 4)
    return tuple(o[:rows].reshape(shape) for o in outs)


def _sum_leading(name, x, idxs, out_dtype=F32):
    _, R, C = x.shape
    tm = _row_tile(R, max(SUBLANE, min(512, (512 * 1024) // C)))

    def body(x_ref, o_ref):
        acc = x_ref[idxs[0]].astype(F32)
        for k in idxs[1:]:
            acc = acc + x_ref[k].astype(F32)
        o_ref[...] = acc.astype(o_ref.dtype)

    return _pcall(
        body, name=name, grid=(R // tm,), in_specs=[pl.BlockSpec((x.shape[0], tm, C), lambda i: (0, i, 0))],
        out_specs=pl.BlockSpec((tm, C), lambda i: (i, 0)), out_shape=jax.ShapeDtypeStruct((R, C), out_dtype),
        compiler_params=_cparams(1),
    )(x)


def _loss_fwd(h, t, w):
    T, D = h.shape
    tm = _row_tile(T, 256)

    def body(h_ref, t_ref, w_ref, o_ref):
        i = pl.program_id(0)
        part = jnp.sum(_f_loss_rows(h_ref[...], t_ref[...], w_ref[...]), axis=0, keepdims=True)
        part = jnp.broadcast_to(part, (1, LANE))

        @pl.when(i == 0)
        def _():
            o_ref[...] = part

        @pl.when(i > 0)
        def _():
            o_ref[...] += part

    return _pcall(
        body, name="loss_fwd", grid=(T // tm,),
        in_specs=[pl.BlockSpec((tm, D), lambda i: (i, 0)), pl.BlockSpec((tm, D), lambda i: (i, 0)),
                  pl.BlockSpec((1, D), lambda i: (0, 0))],
        out_specs=pl.BlockSpec((1, LANE), lambda i: (0, 0)), out_shape=jax.ShapeDtypeStruct((1, LANE), F32),
        compiler_params=_cparams(1),
    )(h, t, w)


CONV_ROWS = 256
CONV_ROWS_FEW_TAPS = 1024
CONV_ACC_ELEMS = 16384


def _col_mask(arg, t):
    col = jnp.bitwise_and(t, GRID_W - 1)
    return (col != 0) if arg < 0 else (col != GRID_W - 1)


def _conv_plan(T, C, taps):
    seg = [m[1] for _, m in taps if m is not None and m[0] == "seg"]
    cap = CONV_ROWS_FEW_TAPS if len(taps) <= 9 else CONV_ROWS
    rc = next(r for r in (1024, 768, 512, 256, LANE) if r <= cap and T % r == 0)
    ct = next((t for t in (512, 256, LANE) if C % t == 0), C)
    reach = max(abs(s) for s, _ in taps)
    hb = next(h for h in (8, 16, 32, 64, 128, 256) if h >= reach and rc % h == 0)
    sub = max(2 * SUBLANE, min(rc, CONV_ACC_ELEMS // ct))
    boundary = None
    if seg:
        inside = seg[0] % rc
        boundary = (seg[0], (inside - reach, inside + reach) if inside else None)
    taps = [(s, None if (m is None or m[0] == "seg") else m[1]) for s, m in taps]
    return rc, ct, hb, sub, T // rc, C // ct, boundary, taps


def _seg_ok(boundary, i, rc, r0, n, s):
    if boundary is None or boundary[1] is None or s == 0 or r0 + n <= boundary[1][0] or r0 >= boundary[1][1]:
        return None
    t = i * rc + r0 + lax.broadcasted_iota(jnp.int32, (n, 1), 0)
    return (t >= boundary[0]) == ((t + s) >= boundary[0])


def _halo_specs(rc, ct, hb, T, off_blocks):
    per = rc // hb
    last = T // hb - 1
    prev = pl.BlockSpec((hb, ct), lambda j, i: (jnp.maximum(i * per - 1, 0), off_blocks + j))
    cur = pl.BlockSpec((rc, ct), lambda j, i: (i, off_blocks + j))
    nxt = pl.BlockSpec((hb, ct), lambda j, i: (jnp.minimum((i + 1) * per, last), off_blocks + j))
    return [prev, cur, nxt]


def _fill_halo(pad_ref, p_ref, c_ref, n_ref, i, nrc, rc, hb, boundary):
    has_prev = i > 0
    has_next = i < nrc - 1
    if boundary is not None:
        has_prev = has_prev & (i * rc != boundary[0])
        has_next = has_next & ((i + 1) * rc != boundary[0])
    pad_ref[0:hb, :] = jnp.where(has_prev, p_ref[...], 0.0)
    pad_ref[hb:hb + rc, :] = c_ref[...]
    pad_ref[hb + rc:hb + rc + hb, :] = jnp.where(has_next, n_ref[...], 0.0)


def _shift_plan(keys):
    count = {}
    for s, m in keys:
        k = (s % SUBLANE, m)
        count[k] = count.get(k, 0) + 1
    slots = {}
    for k, n in sorted(count.items(), key=lambda kv: (kv[0][0], str(kv[0][1]))):
        if k != (0, None) and (n >= 2 or k[1] is not None):
            slots[k] = len(slots)
    return slots


def _build_shifted(copies_ref, slots, pad_ref, keys, i, rc, hb, sub):
    for (r, m), slot in slots.items():
        qs = [s - r for s, mk in keys if (s % SUBLANE, mk) == (r, m)]
        lo, hi = hb + min(qs), hb + rc + max(qs)
        for p in range(lo, hi, sub):
            n = min(sub, hi - p)
            v = pad_ref[p + r:p + r + n, :]
            if m is not None:
                t = i * rc - hb + p + r + lax.broadcasted_iota(jnp.int32, (n, 1), 0)
                v = jnp.where(_col_mask(m, t), v, 0.0)
            copies_ref[slot, p:p + n, :] = v


def _read(copies_ref, slots, pad_ref, s, m, row, n):
    k = (s % SUBLANE, m)
    if k in slots:
        q = s - k[0]
        return copies_ref[slots[k], row + q:row + q + n, :]
    return pad_ref[row + s:row + s + n, :]


def _conv_fwd(name, u, col_off, C, w, b, taps, act=False):
    T = u.shape[0]
    rc, ct, hb, sub, nrc, ncc, boundary, taps = _conv_plan(T, C, taps)
    assert col_off % ct == 0
    K = len(taps)
    keys = [(s, None) for s, _ in taps]
    slots = _shift_plan(keys)
    dirs = sorted({m for _, m in taps if m is not None})

    def body(up, uc, un, w_ref, b_ref, *rest):
        y_ref = rest[0]
        pad_ref, copies_ref = rest[-2], rest[-1]
        i = pl.program_id(1)
        _fill_halo(pad_ref, up, uc, un, i, nrc, rc, hb, boundary)
        _build_shifted(copies_ref, slots, pad_ref, keys, i, rc, hb, sub)
        for r0 in range(0, rc, sub):
            acc = jnp.broadcast_to(b_ref[...], (sub, ct))
            for m in [None] + dirs:
                part = None
                for k, (s, mk) in enumerate(taps):
                    if mk != m:
                        continue
                    v = _read(copies_ref, slots, pad_ref, s, None, hb + r0, sub)
                    ok = _seg_ok(boundary, i, rc, r0, sub, s)
                    term = w_ref[k:k + 1, :] * (v if ok is None else jnp.where(ok, v, 0.0))
                    part = term if part is None else part + term
                if part is None:
                    continue
                if m is not None:
                    t = i * rc + r0 + lax.broadcasted_iota(jnp.int32, (sub, 1), 0)
                    part = jnp.where(_col_mask(m, t), part, 0.0)
                acc = acc + part
            y_ref[r0:r0 + sub, :] = acc
            if act:
                rest[1][r0:r0 + sub, :] = _silu(acc)

    n_out = 2 if act else 1
    res = _pcall(
        body, name=name, grid=(ncc, nrc),
        in_specs=_halo_specs(rc, ct, hb, T, col_off // ct) + [pl.BlockSpec((K, ct), lambda j, i: (0, j)),
                                                              pl.BlockSpec((1, ct), lambda j, i: (0, j))],
        out_specs=[pl.BlockSpec((rc, ct), lambda j, i: (i, j))] * n_out,
        out_shape=[jax.ShapeDtypeStruct((T, C), F32)] * n_out,
        scratch_shapes=[pltpu.VMEM((rc + 2 * hb, ct), F32), pltpu.VMEM((max(len(slots), 1), rc + 2 * hb, ct), F32)],
        compiler_params=_cparams(2),
    )(u, u, u, w, b)
    return res if act else res[0]


def _conv_bwd(name, u, col_off, C, w, g, taps, du_dtype=F32):
    T = u.shape[0]
    rc, ct, hb, sub, nrc, ncc, boundary, taps = _conv_plan(T, C, taps)
    K = len(taps)
    u_keys = [(s, None) for s, _ in taps]
    dirs = sorted({m for _, m in taps if m is not None})
    g_keys = [(-s, m) for s, m in taps] + [(0, m) for m in dirs]
    u_slots, g_slots = _shift_plan(u_keys), _shift_plan(g_keys)

    def body(up, uc, un, gp, gc, gn, w_ref, du_ref, dw_ref, db_ref, upad, gpad, ucopies, gcopies):
        i = pl.program_id(1)
        _fill_halo(upad, up, uc, un, i, nrc, rc, hb, boundary)
        _fill_halo(gpad, gp, gc, gn, i, nrc, rc, hb, boundary)
        _build_shifted(ucopies, u_slots, upad, u_keys, i, rc, hb, sub)
        _build_shifted(gcopies, g_slots, gpad, g_keys, i, rc, hb, sub)

        @pl.when(i == 0)
        def _():
            dw_ref[...] = jnp.zeros_like(dw_ref)
            db_ref[...] = jnp.zeros_like(db_ref)

        def fold(v):
            return jnp.sum(v.reshape(sub // SUBLANE, SUBLANE, ct), axis=0)

        dbs = jnp.zeros((SUBLANE, ct), F32)
        for r0 in range(0, rc, sub):
            dbs = dbs + fold(gpad[hb + r0:hb + r0 + sub, :])
            acc = jnp.zeros((sub, ct), F32)
            for k, (s, m) in enumerate(taps):
                v = _read(gcopies, g_slots, gpad, -s, m, hb + r0, sub)
                ok = _seg_ok(boundary, i, rc, r0, sub, -s)
                acc = acc + w_ref[k:k + 1, :] * (v if ok is None else jnp.where(ok, v, 0.0))
            du_ref[r0:r0 + sub, :] = acc.astype(du_ref.dtype)
        db_ref[...] += jnp.sum(dbs, axis=0, keepdims=True)
        for k, (s, m) in enumerate(taps):
            part = jnp.zeros((SUBLANE, ct), F32)
            for r0 in range(0, rc, sub):
                v = _read(ucopies, u_slots, upad, s, None, hb + r0, sub)
                ok = _seg_ok(boundary, i, rc, r0, sub, s)
                part = part + fold(_read(gcopies, g_slots, gpad, 0, m, hb + r0, sub)
                                   * (v if ok is None else jnp.where(ok, v, 0.0)))
            dw_ref[k:k + 1, :] += jnp.sum(part, axis=0, keepdims=True)

    halo_u = _halo_specs(rc, ct, hb, T, col_off // ct)
    halo_g = _halo_specs(rc, ct, hb, T, 0)
    rows = rc + 2 * hb
    return _pcall(
        body, name=name, grid=(ncc, nrc),
        in_specs=halo_u + halo_g + [pl.BlockSpec((K, ct), lambda j, i: (0, j))],
        out_specs=[pl.BlockSpec((rc, ct), lambda j, i: (i, j)), pl.BlockSpec((K, ct), lambda j, i: (0, j)),
                   pl.BlockSpec((1, ct), lambda j, i: (0, j))],
        out_shape=[jax.ShapeDtypeStruct((T, C), du_dtype), jax.ShapeDtypeStruct((K, C), F32),
                   jax.ShapeDtypeStruct((1, C), F32)],
        scratch_shapes=[pltpu.VMEM((rows, ct), F32), pltpu.VMEM((rows, ct), F32),
                        pltpu.VMEM((max(len(u_slots), 1), rows, ct), F32),
                        pltpu.VMEM((max(len(g_slots), 1), rows, ct), F32)],
        compiler_params=_cparams(2),
    )(u, u, u, g, g, g, w)


def _ssd_group(xg, bm, cm, s_in, *per_head, reverse, P):
    R = len(per_head) // 2
    dtrs, a_s = per_head[:R], per_head[R:]
    q, rp = xg.shape
    ii = lax.broadcasted_iota(jnp.int32, (q, q), 0)
    jj = lax.broadcasted_iota(jnp.int32, (q, q), 1)
    causal = (jj >= ii) if reverse else (jj <= ii)
    causal_t = (ii >= jj) if reverse else (ii <= jj)
    eye = ii == jj
    lane = lax.broadcasted_iota(jnp.int32, (1, rp), 1)
    row = lax.broadcasted_iota(jnp.int32, (rp, 1), 0)
    nt = (((1,), (1,)), ((), ()))
    tn = (((0,), (0,)), ((), ()))
    cb = lax.dot_general(cm.astype(BF16), bm.astype(BF16), nt, preferred_element_type=F32)
    dt_x = jnp.zeros((q, rp), F32)
    acum_x = jnp.zeros((q, rp), F32)
    tot_row = jnp.zeros((1, rp), F32)
    tot_col = jnp.zeros((rp, 1), F32)
    wts, lane_masks = [], []
    for r in range(R):
        hm = (lane >= r * P) & (lane < (r + 1) * P)
        hc = (row >= r * P) & (row < (r + 1) * P)
        dt_c = jnp.sum(jnp.where(eye, dtrs[r], 0.0), axis=1, keepdims=True)
        dac = dt_c * a_s[r]
        dar = dtrs[r] * a_s[r]
        acum_c = jnp.sum(jnp.where(causal, dar, 0.0), axis=1, keepdims=True)
        acum_r = jnp.sum(jnp.where(causal_t, dac, 0.0), axis=0, keepdims=True)
        decay = jnp.where(causal, jnp.exp(jnp.where(causal, acum_c - acum_r, 0.0)), 0.0)
        tot = jnp.sum(dac, axis=0, keepdims=True)
        dt_x = jnp.where(hm, dt_c, dt_x)
        acum_x = jnp.where(hm, acum_c, acum_x)
        tot_row = jnp.where(hm, tot, tot_row)
        tot_col = jnp.where(hc, tot, tot_col)
        wts.append((cb * decay).astype(BF16))
        lane_masks.append(hm)
    xdt = xg * dt_x
    xdt_b = xdt.astype(BF16)
    y = jnp.zeros((q, rp), F32)
    for r in range(R):
        y = jnp.where(lane_masks[r], jnp.dot(wts[r], xdt_b, preferred_element_type=F32), y)
    dte = jnp.exp(tot_row - acum_x)
    cs = lax.dot_general((xdt * dte).astype(BF16), bm.astype(BF16), tn, preferred_element_type=F32)
    y = y + lax.dot_general(cm.astype(BF16), s_in.astype(BF16), nt, preferred_element_type=F32) * jnp.exp(acum_x)
    s_out = jnp.exp(tot_col) * s_in + cs
    return y, s_out


def _ssd_group_state(xg, bm, s_in, *per_head, reverse, P):
    R = len(per_head) // 2
    dtrs, a_s = per_head[:R], per_head[R:]
    q, rp = xg.shape
    ii = lax.broadcasted_iota(jnp.int32, (q, q), 0)
    jj = lax.broadcasted_iota(jnp.int32, (q, q), 1)
    causal = (jj >= ii) if reverse else (jj <= ii)
    eye = ii == jj
    lane = lax.broadcasted_iota(jnp.int32, (1, rp), 1)
    row = lax.broadcasted_iota(jnp.int32, (rp, 1), 0)
    dt_x = jnp.zeros((q, rp), F32)
    acum_x = jnp.zeros((q, rp), F32)
    tot_row = jnp.zeros((1, rp), F32)
    tot_col = jnp.zeros((rp, 1), F32)
    for r in range(R):
        hm = (lane >= r * P) & (lane < (r + 1) * P)
        hc = (row >= r * P) & (row < (r + 1) * P)
        dt_c = jnp.sum(jnp.where(eye, dtrs[r], 0.0), axis=1, keepdims=True)
        acum_c = jnp.sum(jnp.where(causal, dtrs[r] * a_s[r], 0.0), axis=1, keepdims=True)
        tot = jnp.sum(dt_c * a_s[r], axis=0, keepdims=True)
        dt_x = jnp.where(hm, dt_c, dt_x)
        acum_x = jnp.where(hm, acum_c, acum_x)
        tot_row = jnp.where(hm, tot, tot_row)
        tot_col = jnp.where(hc, tot, tot_col)
    xe = xg * dt_x * jnp.exp(tot_row - acum_x)
    cs = lax.dot_general(xe.astype(BF16), bm.astype(BF16), (((0,), (0,)), ((), ())), preferred_element_type=F32)
    return jnp.exp(tot_col) * s_in + cs


def _ssd_maps(NC, ncc, reverse_steps):
    def chunk(d, s):
        if reverse_steps:
            s = NC - 1 - s
        return s if d == 0 else jnp.where(s < ncc, ncc - 1 - s, NC - 1 - s + ncc)

    def lat_chunk(d, s):
        c = chunk(d, s) - ncc
        return jnp.where(c < 0, 0 if d == 0 else NC - ncc - 1, c)

    def step(s):
        return NC - 1 - s if reverse_steps else s

    return chunk, lat_chunk, step


def _ssd_specs(chunk, d, R, Q, N, RP, bo, co):
    return [
        pl.BlockSpec((Q, RP), lambda g, s: (chunk(d, s), g)),
        pl.BlockSpec((Q, N), lambda g, s: (chunk(d, s), bo + g)),
        pl.BlockSpec((Q, N), lambda g, s: (chunk(d, s), co + g)),
        pl.BlockSpec((R, 1, Q), lambda g, s: (g, 0, chunk(d, s))),
        pl.BlockSpec((R, 1, 1), lambda g, s: (g, 0, 0)),
    ]


def _ssd_fwd(xbc, b_off, c_off, dtr, a, P, ncc):
    T = xbc.shape[0]
    H = dtr[0].shape[0]
    N, Q = SSD_STATE, SSD_CHUNK
    NC = T // Q
    G = (c_off - b_off) // N
    R = H // G
    RP = R * P
    chunk, lat_chunk, _ = _ssd_maps(NC, ncc, False)

    def body(*refs):
        s = pl.program_id(1)
        s_ref = refs[-1]

        @pl.when(s == 0)
        def _():
            s_ref[...] = jnp.zeros_like(s_ref)

        for d in range(2):
            x_ref, b_ref, c_ref, dtr_ref, a_ref = refs[5 * d:5 * d + 5]
            y_ref, se_ref = refs[10 + 2 * d:12 + 2 * d]
            s_in = s_ref[d]
            se_ref[...] = s_in
            per_head = [dtr_ref[r] for r in range(R)] + [a_ref[r] for r in range(R)]

            @pl.when(s >= ncc)
            def _(d=d, x_ref=x_ref, b_ref=b_ref, c_ref=c_ref, y_ref=y_ref, s_in=s_in, per_head=per_head):
                y, s_out = _ssd_group(x_ref[...], b_ref[...], c_ref[...], s_in, *per_head, reverse=d == 1, P=P)
                y_ref[...] = y
                s_ref[d] = s_out

            @pl.when(s < ncc)
            def _(d=d, x_ref=x_ref, b_ref=b_ref, s_in=s_in, per_head=per_head):
                s_ref[d] = _ssd_group_state(x_ref[...], b_ref[...], s_in, *per_head, reverse=d == 1, P=P)

    in_specs, out_specs, out_shape, operands = [], [], [], []
    for d in range(2):
        in_specs += _ssd_specs(chunk, d, R, Q, N, RP, b_off // N, c_off // N)
        operands += [xbc, xbc, xbc, dtr[d], a[d]]
        out_specs += [pl.BlockSpec((Q, RP), functools.partial(lambda g, s, d: (lat_chunk(d, s), g), d=d)),
                      pl.BlockSpec((None, None, RP, N), lambda g, s: (g, s, 0, 0))]
        out_shape += [jax.ShapeDtypeStruct((T - ncc * Q, H * P), F32), jax.ShapeDtypeStruct((G, NC, RP, N), F32)]
    y_f, se_f, y_b, se_b = _pcall(
        body, name="ssd_fwd", grid=(G, NC), in_specs=in_specs, out_specs=out_specs, out_shape=out_shape,
        scratch_shapes=[pltpu.VMEM((2, RP, N), F32)], compiler_params=_cparams(2),
    )(*operands)
    return (y_f, y_b), (se_f, se_b)


def _ssd_bwd(xbc, b_off, c_off, dtr, a, s_enter, dy, P, ncc):
    T = xbc.shape[0]
    H = dtr[0].shape[0]
    N, Q = SSD_STATE, SSD_CHUNK
    NC = T // Q
    G = (c_off - b_off) // N
    R = H // G
    RP = R * P
    chunk, lat_chunk, step = _ssd_maps(NC, ncc, True)
    n_in, n_out = 7, 5

    def body(*refs):
        s = pl.program_id(1)
        ds_ref = refs[-1]

        @pl.when(s == 0)
        def _():
            ds_ref[...] = jnp.zeros_like(ds_ref)

        for d in range(2):
            x_ref, b_ref, c_ref, dtr_ref, a_ref, se_ref, dy_ref = refs[n_in * d:n_in * (d + 1)]
            dx_ref, db_ref, dc_ref, ddtr_ref, da_ref = refs[2 * n_in + n_out * d:2 * n_in + n_out * (d + 1)]
            per_head = [dtr_ref[r] for r in range(R)] + [a_ref[r] for r in range(R)]

            def store(grads, dx_ref=dx_ref, db_ref=db_ref, ddtr_ref=ddtr_ref, da_ref=da_ref, d=d):
                dx_ref[...] = grads[0]
                db_ref[...] = grads[1]
                ds_ref[d] = grads[2]
                for r in range(R):
                    ddtr_ref[r] = grads[3 + r]
                    da_ref[r] = jnp.broadcast_to(grads[3 + R + r], (SUBLANE, LANE))

            @pl.when(s < NC - ncc)
            def _(d=d, x_ref=x_ref, b_ref=b_ref, c_ref=c_ref, se_ref=se_ref, dy_ref=dy_ref, dc_ref=dc_ref,
                  per_head=per_head, store=store):
                f = functools.partial(_ssd_group, reverse=d == 1, P=P)
                _, vjp = jax.vjp(f, x_ref[...], b_ref[...], c_ref[...], se_ref[...], *per_head)
                grads = vjp((dy_ref[...], ds_ref[d]))
                dc_ref[...] = grads[2]
                store(grads[:2] + grads[3:])

            @pl.when(s >= NC - ncc)
            def _(d=d, x_ref=x_ref, b_ref=b_ref, se_ref=se_ref, dc_ref=dc_ref, per_head=per_head, store=store):
                f = functools.partial(_ssd_group_state, reverse=d == 1, P=P)
                _, vjp = jax.vjp(f, x_ref[...], b_ref[...], se_ref[...], *per_head)
                dc_ref[...] = jnp.zeros_like(dc_ref)
                store(vjp(ds_ref[d]))

    in_specs, out_specs, out_shape, operands = [], [], [], []
    for d in range(2):
        in_specs += _ssd_specs(chunk, d, R, Q, N, RP, b_off // N, c_off // N) + [
            pl.BlockSpec((None, None, RP, N), lambda g, s: (g, step(s), 0, 0)),
            pl.BlockSpec((Q, RP), functools.partial(lambda g, s, d: (lat_chunk(d, s), g), d=d)),
        ]
        operands += [xbc, xbc, xbc, dtr[d], a[d], s_enter[d], dy]
    for d in range(2):
        at_chunk = functools.partial(lambda g, s, d: (chunk(d, s), g), d=d)
        out_specs += [
            pl.BlockSpec((Q, RP), at_chunk), pl.BlockSpec((Q, N), at_chunk), pl.BlockSpec((Q, N), at_chunk),
            pl.BlockSpec((R, 1, Q), functools.partial(lambda g, s, d: (g, 0, chunk(d, s)), d=d)),
            pl.BlockSpec((R, SUBLANE, LANE), lambda g, s: (g * NC + s, 0, 0)),
        ]
        out_shape += [
            jax.ShapeDtypeStruct((T, H * P), F32), jax.ShapeDtypeStruct((T, G * N), F32),
            jax.ShapeDtypeStruct((T, G * N), F32), jax.ShapeDtypeStruct((H, 1, T), F32),
            jax.ShapeDtypeStruct((G * NC * R, SUBLANE, LANE), F32),
        ]
    res = _pcall(
        body, name="ssd_bwd", grid=(G, NC), in_specs=in_specs, out_specs=out_specs, out_shape=out_shape,
        scratch_shapes=[pltpu.VMEM((2, RP, N), F32)], compiler_params=_cparams(2),
    )(*operands)
    return res[:n_out], res[n_out:]


def _allgather8(name, v):
    R, C = v.shape

    def body(x_ref, out_ref, send_sems, recv_sems, local_sem):
        x, y, c = lax.axis_index("x"), lax.axis_index("y"), lax.axis_index("c")
        me, sibling = (x, y, c), (x, y, 1 - c)
        chips = [(1 - x, y), (x, 1 - y), (1 - x, 1 - y)]

        def slot(px, py, pc):
            return out_ref.at[4 * px + 2 * py + pc]

        def copy(k, block, to, src=None):
            return pltpu.make_async_remote_copy(
                src_ref=slot(*block) if src is None else src, dst_ref=slot(*block),
                send_sem=send_sems.at[k], recv_sem=recv_sems.at[k], device_id=to, device_id_type=MESH)

        mine = pltpu.make_async_copy(x_ref, slot(*me), local_sem)
        mine.start()
        first = [copy(0, me, sibling, src=x_ref)]
        first += [copy(1 + j, me, (*chip, c), src=x_ref) for j, chip in enumerate(chips)]
        for cp in first:
            cp.start()
        passed = [copy(4 + j, (*chip, c), sibling) for j, chip in enumerate(chips)]
        for j, chip in enumerate(chips):
            copy(1 + j, (*chip, c), me).wait_recv()
            passed[j].start()
        copy(0, sibling, me).wait_recv()
        for j, chip in enumerate(chips):
            copy(4 + j, (*chip, 1 - c), me).wait_recv()
        for cp in first + passed:
            cp.wait_send()
        mine.wait()

    return _pcall(
        body, name=name, out_shape=jax.ShapeDtypeStruct((N_DEV, R, C), v.dtype),
        in_specs=[pl.BlockSpec(memory_space=pltpu.VMEM)], out_specs=pl.BlockSpec(memory_space=pltpu.VMEM),
        scratch_shapes=[pltpu.SemaphoreType.DMA((7,)), pltpu.SemaphoreType.DMA((7,)), pltpu.SemaphoreType.DMA],
        compiler_params=pltpu.CompilerParams(vmem_limit_bytes=VMEM_LIMIT_BYTES),
    )(v)


def _slot(ref, k, axis, size):
    if axis is None:
        return ref.at[k]
    align = LANE if size % LANE == 0 else 2 * SUBLANE
    assert size % align == 0
    return ref.at[(slice(None),) * axis + (pl.ds(pl.multiple_of(k * size, align), size),)]


def _exchange4_start(name, srcs, bcast, dep, axes=None, half=False):
    n = len(srcs)
    axes = list(axes) if axes is not None else [None] * n
    sizes = [None if ax is None else s.shape[ax] for s, ax in zip(srcs, axes)]

    def land_shape(s, ax):
        if not bcast:
            return s.shape
        if half:
            return (N_CHIPS,) + s.shape[1:]
        if ax is None:
            return (N_CHIPS,) + s.shape
        return s.shape[:ax] + (N_CHIPS * s.shape[ax],) + s.shape[ax + 1:]

    lands = [lax.empty(land_shape(s, ax), s.dtype) for s, ax in zip(srcs, axes)]

    def body(*refs):
        src, land = refs[:n], refs[n:2 * n]
        send_sems, recv_sems = refs[2 * n + 1], refs[2 * n + 2]
        token = refs[-1]
        x, y, c = lax.axis_index("x"), lax.axis_index("y"), lax.axis_index("c")
        me = 2 * x + y
        for a in range(n):
            for j, (px, py) in enumerate([(1 - x, y), (x, 1 - y), (1 - x, 1 - y)]):
                pltpu.make_async_remote_copy(
                    src_ref=(src[a].at[c] if half else src[a]) if bcast else src[a].at[2 * px + py],
                    dst_ref=_slot(land[a], me, axes[a], sizes[a]),
                    send_sem=send_sems.at[3 * a + j], recv_sem=recv_sems.at[3 * a + j], device_id=(px, py, c),
                    device_id_type=MESH).start()
        token[...] = jnp.zeros_like(token)

    hbm = pl.BlockSpec(memory_space=pltpu.HBM)
    sem = pl.BlockSpec(memory_space=pltpu.SEMAPHORE)
    outs = _pcall(
        body, name=name,
        out_shape=(pltpu.SemaphoreType.DMA((3 * n,)), pltpu.SemaphoreType.DMA((3 * n,)),
                   *[pltpu.HBM(s.shape, s.dtype) for s in srcs], *[pltpu.HBM(l.shape, l.dtype) for l in lands],
                   jax.ShapeDtypeStruct((SUBLANE, LANE), F32)),
        in_specs=[hbm] * (2 * n) + [pl.BlockSpec(memory_space=pl.ANY)],
        out_specs=(sem, sem, *[hbm] * (2 * n), pl.BlockSpec(memory_space=pltpu.VMEM)),
        input_output_aliases={k: 2 + k for k in range(2 * n)},
        compiler_params=pltpu.CompilerParams(has_side_effects=pltpu.SideEffectType.DATAFLOW_SIDE_EFFECTING),
    )(*[pltpu.with_memory_space_constraint(s, pltpu.HBM) for s in srcs],
      *[pltpu.with_memory_space_constraint(l, pltpu.HBM) for l in lands], dep)
    return (n, bcast, half, axes, sizes, outs[0], outs[1], outs[2:2 + n], outs[2 + n:2 + 2 * n]), outs[-1]


def _exchange4_wait(name, handle, after):
    n, bcast, half, axes, sizes, send_sems, recv_sems, src_thru, land_thru = handle

    def body(*refs):
        src, land = refs[:n], refs[n:2 * n]
        send_sems, recv_sems = refs[2 * n], refs[2 * n + 1]
        x, y, c = lax.axis_index("x"), lax.axis_index("y"), lax.axis_index("c")
        for a in range(n):
            for j, (px, py) in enumerate([(1 - x, y), (x, 1 - y), (1 - x, 1 - y)]):
                pk = 2 * px + py
                copy = pltpu.make_async_remote_copy(
                    src_ref=(src[a].at[c] if half else src[a]) if bcast else src[a].at[pk],
                    dst_ref=_slot(land[a], pk, axes[a], sizes[a]),
                    send_sem=send_sems.at[3 * a + j], recv_sem=recv_sems.at[3 * a + j], device_id=(px, py, c),
                    device_id_type=MESH)
                copy.wait_send()
                copy.wait_recv()

    hbm = pl.BlockSpec(memory_space=pltpu.HBM)
    sem = pl.BlockSpec(memory_space=pltpu.SEMAPHORE)
    outs = _pcall(
        body, name=name,
        out_shape=tuple(pltpu.HBM(t.shape, t.dtype) for t in (*src_thru, *land_thru)),
        in_specs=[hbm] * (2 * n) + [sem, sem, pl.BlockSpec(memory_space=pl.ANY)], out_specs=tuple([hbm] * (2 * n)),
        input_output_aliases={k: k for k in range(2 * n)},
        compiler_params=pltpu.CompilerParams(has_side_effects=pltpu.SideEffectType.DATAFLOW_SIDE_EFFECTING),
    )(*src_thru, *land_thru, send_sems, recv_sems, after)
    return list(outs[:n]), list(outs[n:])


def _tie(name, v, token):
    def body(v_ref, token_ref, o_ref):
        del v_ref, token_ref, o_ref

    any_spec = pl.BlockSpec(memory_space=pl.ANY)
    return _pcall(body, name=name, out_shape=jax.ShapeDtypeStruct(v.shape, v.dtype), in_specs=[any_spec, any_spec],
                  out_specs=any_spec, input_output_aliases={0: 0})(v, token)


def _fill_own(landed, own, me, bcast):
    blk = own if bcast else lax.dynamic_index_in_dim(own, me, 0, keepdims=False)
    return lax.dynamic_update_index_in_dim(landed, blk, me, 0)


def _swap_sibling(name, srcs, by_core=False):
    n = len(srcs)

    def body(*refs):
        src, out = refs[:n], refs[n:2 * n]
        send_sems, recv_sems = refs[2 * n:]
        x, y, c = lax.axis_index("x"), lax.axis_index("y"), lax.axis_index("c")
        copies = []
        for a in range(n):
            send = pltpu.make_async_remote_copy(
                src_ref=src[a], dst_ref=out[a].at[c] if by_core else out[a], send_sem=send_sems.at[a],
                recv_sem=recv_sems.at[a], device_id=(x, y, 1 - c), device_id_type=MESH)
            send.start()
            arrive = pltpu.make_async_remote_copy(
                src_ref=src[a], dst_ref=out[a].at[1 - c] if by_core else out[a], send_sem=send_sems.at[a],
                recv_sem=recv_sems.at[a], device_id=(x, y, 1 - c), device_id_type=MESH)
            copies.append((send, arrive))
        for send, arrive in copies:
            send.wait_send()
            arrive.wait_recv()

    any_spec = pl.BlockSpec(memory_space=pl.ANY)
    return _pcall(
        body, name=name,
        out_shape=[jax.ShapeDtypeStruct(((2,) + s.shape) if by_core else s.shape, s.dtype) for s in srcs],
        in_specs=[any_spec] * n, out_specs=[any_spec] * n,
        scratch_shapes=[pltpu.SemaphoreType.DMA((n,)), pltpu.SemaphoreType.DMA((n,))],
    )(*srcs)


def _mod_fwd(c16, mod_w, mod_b_shard):
    nl, D, S = mod_w.shape

    def body(c_ref, w_ref, b_ref, o_ref):
        s = _silu(c_ref[...]).astype(BF16)
        o_ref[...] = jnp.dot(s, w_ref[...].astype(BF16), preferred_element_type=F32) + b_ref[...]

    return _pcall(
        body, name="mod_fwd", grid=(nl,),
        in_specs=[pl.BlockSpec((16, D), lambda l: (0, 0)), pl.BlockSpec((None, D, S), lambda l: (l, 0, 0)),
                  pl.BlockSpec((None, 1, S), lambda l: (l, 0, 0))],
        out_specs=pl.BlockSpec((None, 16, S), lambda l: (l, 0, 0)),
        out_shape=jax.ShapeDtypeStruct((nl, 16, S), F32), compiler_params=_cparams(1),
    )(c16, mod_w, mod_b_shard)


def _mod_w_update(s16t, dm16, w, m, v):
    nl, D, S = w.shape
    tm = _row_tile(D, 256)

    def body(s_ref, dm_ref, w_ref, m_ref, v_ref, g_ref, dl_ref, nm_ref, nv_ref):
        g = jnp.dot(s_ref[...], dm_ref[...], preferred_element_type=F32, precision=HIGHEST)
        g, dl, nm, nv = _f_adamw(w_ref[...], m_ref[...], v_ref[...], g, jnp.zeros_like(g))
        g_ref[...] = g
        dl_ref[...] = dl
        nm_ref[...] = nm
        nv_ref[...] = nv

    big = pl.BlockSpec((None, tm, S), lambda l, i: (l, i, 0))
    return _pcall(
        body, name="mod_w_update", grid=(nl, D // tm),
        in_specs=[pl.BlockSpec((tm, 16), lambda l, i: (i, 0)), pl.BlockSpec((None, 16, S), lambda l, i: (l, 0, 0)),
                  big, big, big],
        out_specs=[big] * 4, out_shape=[jax.ShapeDtypeStruct(w.shape, F32)] * 4, compiler_params=_cparams(2),
    )(s16t, dm16, w, m, v)


def _size(shape):
    n = 1
    for d in shape:
        n *= d
    return n


def _pack(arrs):
    pieces = []
    for a in arrs:
        flat = a.reshape(-1).astype(F32)
        pieces.append(jnp.pad(flat, (0, _round_up(flat.shape[0], LANE) - flat.shape[0])).reshape(-1, LANE))
    buf = jnp.concatenate(pieces, axis=0)
    return jnp.pad(buf, ((0, _round_up(buf.shape[0], SUBLANE) - buf.shape[0]), (0, 0)))


def _unpack(buf, shapes):
    lead = buf.shape[:-2]
    out, row = [], 0
    for s in shapes:
        n = _size(s)
        rows = _cdiv(n, LANE)
        piece = buf[..., row:row + rows, :].reshape(lead + (rows * LANE,))
        out.append(piece[..., :n].reshape(lead + tuple(s)))
        row += rows
    return out


def _adamw_many(name, ws, ms, vs, gs):
    n = len(ws)

    def body(*refs):
        for k in range(n):
            res = _f_adamw(refs[k][...], refs[n + k][...], refs[2 * n + k][...], refs[3 * n + k][...], 0.0)
            for j in range(4):
                refs[(4 + j) * n + k][...] = res[j]

    vmem = pl.BlockSpec(memory_space=pltpu.VMEM)
    res = _pcall(body, name=name, out_shape=[jax.ShapeDtypeStruct(w.shape, F32) for _ in range(4) for w in ws],
                 in_specs=[vmem] * (4 * n), out_specs=[vmem] * (4 * n))(*ws, *ms, *vs, *gs)
    return [tuple(res[j * n + k] for j in range(4)) for k in range(n)]


SHARD_AXIS = {
    "mod_w": 2, "ssd_w_in": 2, "ssd_conv_w": 2, "ssd_w_out": 1, "conf_w_pw1": 2, "conf_b_pw1": 1, "conf_w_dw": 2,
    "conf_b_dw": 1, "conf_ln_w": 1, "conf_ln_b": 1, "conf_w_pw2": 1, "conf_b_pw2": 1, "ffn_w_up": 2,
    "ffn_conv_w": 3, "ffn_w_down": 1,
}
BIG = ("ssd_w_in", "ssd_w_out", "conf_w_pw1", "conf_w_pw2", "ffn_w_up", "ffn_w_down")
WEIGHTS = ("c_ctx", "mod_w", "mod_b", "norm1_w", "norm2_w", "ssd_w_in", "ssd_conv_w", "ssd_conv_b", "ssd_dt_bias",
           "ssd_a_log", "ssd_d", "ssd_norm_w", "ssd_w_out", "conf_w_pw1", "conf_b_pw1", "conf_w_dw", "conf_b_dw",
           "conf_ln_w", "conf_ln_b", "conf_w_pw2", "conf_b_pw2", "ffn_w_up", "ffn_conv_w", "ffn_conv_b",
           "ffn_w_down", "final_norm_w")
SMALL = tuple(n for n in WEIGHTS if n not in BIG and n != "mod_w")
SMALL_SHARDED = tuple(n for n in SMALL if n in SHARD_AXIS)


def _unshard(stacked, axis):
    return jnp.concatenate([stacked[k] for k in range(N_CHIPS)], axis=axis)


def _to_blocks(full, axis):
    return jnp.stack(jnp.split(full, N_CHIPS, axis=axis))


def _par(v):
    v = v.reshape(-1, v.shape[-1])
    return v[:, None, :]


def kernel(x, c, ctx, c_ctx, mod_w, mod_b, norm1_w, norm2_w, ssd_w_in, ssd_conv_w, ssd_conv_b, ssd_dt_bias, ssd_a_log, ssd_d, ssd_norm_w, ssd_w_out, conf_w_pw1, conf_b_pw1, conf_w_dw, conf_b_dw, conf_ln_w, conf_ln_b, conf_w_pw2, conf_b_pw2, ffn_w_up, ffn_conv_w, ffn_conv_b, ffn_w_down, final_norm_w, loss_target, m_c_ctx, m_mod_w, m_mod_b, m_norm1_w, m_norm2_w, m_ssd_w_in, m_ssd_conv_w, m_ssd_conv_b, m_ssd_dt_bias, m_ssd_a_log, m_ssd_d, m_ssd_norm_w, m_ssd_w_out, m_conf_w_pw1, m_conf_b_pw1, m_conf_w_dw, m_conf_b_dw, m_conf_ln_w, m_conf_ln_b, m_conf_w_pw2, m_conf_b_pw2, m_ffn_w_up, m_ffn_conv_w, m_ffn_conv_b, m_ffn_w_down, m_final_norm_w, v_c_ctx, v_mod_w, v_mod_b, v_norm1_w, v_norm2_w, v_ssd_w_in, v_ssd_conv_w, v_ssd_conv_b, v_ssd_dt_bias, v_ssd_a_log, v_ssd_d, v_ssd_norm_w, v_ssd_w_out, v_conf_w_pw1, v_conf_b_pw1, v_conf_w_dw, v_conf_b_dw, v_conf_ln_w, v_conf_ln_b, v_conf_w_pw2, v_conf_b_pw2, v_ffn_w_up, v_ffn_conv_w, v_ffn_conv_b, v_ffn_w_down, v_final_norm_w):
    given = dict(locals())
    W = {n: given[n] for n in WEIGHTS}
    Mo = {n: given["m_" + n] for n in WEIGHTS}
    Vo = {n: given["v_" + n] for n in WEIGHTS}

    ax, ay, ac = lax.axis_index("x"), lax.axis_index("y"), lax.axis_index("c")
    chip = 2 * ax + ay
    dev = 4 * ax + 2 * ay + ac

    D = x.shape[-1]
    L, Lc = x.shape[1], ctx.shape[1]
    T0 = L + Lc
    H = ssd_a_log.shape[-1]
    DI = ssd_norm_w.shape[-1]
    P = DI // H
    CD = ssd_conv_b.shape[-1]
    N = SSD_STATE
    G = (CD - DI) // (2 * N)
    FH = ffn_conv_b.shape[-1]
    KS = ssd_conv_w.shape[1]
    KC = conf_w_dw.shape[1]
    ncc = Lc // SSD_CHUNK

    shard_b = {n: W[n].astype(BF16) for n in BIG}

    small_shard_shapes = [W[n].shape for n in SMALL_SHARDED]
    f1 = _allgather8("gather_small", _pack([c] + [W[n] for n in SMALL_SHARDED]))
    parts = _unpack(f1, [c.shape] + small_shard_shapes)
    Wf = dict(W)
    for n, p in zip(SMALL_SHARDED, parts[1:]):
        Wf[n] = _unshard(p[::2], SHARD_AXIS[n])
    c16 = jnp.concatenate([parts[0].reshape(N_DEV, D), c_ctx[None, :], jnp.zeros((16 - N_DEV - 1, D), F32)], axis=0)

    S_mod = mod_w.shape[-1]
    mod_b_shard = lax.dynamic_slice_in_dim(mod_b, chip * S_mod, S_mod, axis=1)[:, None, :]
    mod_part = _mod_fwd(c16, mod_w, mod_b_shard)
    f2 = _allgather8("gather_mod", mod_part.reshape(2 * 16, S_mod))
    mods = jnp.concatenate([f2[2 * k].reshape(2, 16, S_mod) for k in range(N_CHIPS)], axis=-1)
    my = lax.dynamic_slice_in_dim(mods, dev, 1, axis=1)[:, 0]
    sh1, sc1, g1, sh2, sc2, g2 = [[my[l, k * D:(k + 1) * D] for l in range(2)] for k in range(6)]
    csh1, csc1 = mods[0, N_DEV, 0:D], mods[0, N_DEV, D:2 * D]

    in_halves = shard_b["ssd_w_in"].reshape(2, D // 2, ssd_w_in.shape[-1])
    gather_a, token = _exchange4_start("gather_w_in_start", [in_halves], True, mods, half=True)
    csc1 = _tie("tie_gather_w_in", csc1, token)

    def full_weight(n, own, landed):
        if landed.ndim == own.ndim:
            ax = SHARD_AXIS[n]
            return lax.dynamic_update_slice_in_dim(landed, own, chip * own.shape[ax], ax)
        return _unshard(_fill_own(landed, own, chip, True), SHARD_AXIS[n])

    xl = x[0]
    rows0 = (ctx[0], xl)
    n1w0, n1w1 = _par(norm1_w[0]), _par(norm1_w[1])
    sc_seg = jnp.stack([csc1, sc1[0]])[:, None, :]
    sh_seg = jnp.stack([csh1, sh1[0]])[:, None, :]

    a0 = _rw_fwd("l0_modnorm1", _f_modnorm, [], [n1w0, sc_seg, sh_seg], [D], T=T0, seg_rows=(Lc,), head=rows0,
                 out_dtypes=[BF16])
    rest = [n for n in BIG if n != "ssd_w_in"]
    for n in rest:
        a0 = _tie("tie_cast_" + n, a0, shard_b[n])
    (own_in,), (landed_in,) = _exchange4_wait("gather_w_in_wait", gather_a, a0)
    mine = _fill_own(landed_in, lax.dynamic_index_in_dim(own_in, ac, 0, keepdims=False), chip, True)
    (halves,) = _swap_sibling("swap_w_in", [mine], by_core=True)
    halves = lax.dynamic_update_index_in_dim(halves, mine, ac, 0)
    w_in = jnp.concatenate([halves[:, k].reshape(D, -1) for k in range(N_CHIPS)], axis=1)
    landed_in = halves
    def start_gather(tag, names, dep):
        handle, tok = _exchange4_start("gather_" + tag + "_start", [shard_b[n] for n in names], True, dep,
                                       axes=[1 if SHARD_AXIS[n] == 1 else None for n in names])
        return (names, handle), tok

    def finish_gather(tag, group, after):
        names, handle = group
        return {n: full_weight(n, own, g)
                for n, own, g in zip(names, *_exchange4_wait("gather_" + tag + "_wait", handle, after))}

    gather_b, token = start_gather("mix", ["ssd_w_out", "conf_w_pw1", "conf_w_pw2"], landed_in)
    gather_c, token = start_gather("ffn", ["ffn_w_up", "ffn_w_down"], token)
    a0 = _tie("tie_gather_rest", a0, token)
    proj = _mm(a0, w_in, name="l0_w_in")
    seg_taps = [(k - KS // 2, ("seg", Lc)) for k in range(KS)]
    xbc_pre, xbc = _conv_fwd("l0_conv", proj, DI, CD, Wf["ssd_conv_w"][0], ssd_conv_b, seg_taps, act=True)
    dt_raw = proj[:, DI + CD:]
    dt_bias = _par(ssd_dt_bias.reshape(1, 2 * H))
    dt = _rw_fwd("l0_softplus", _f_softplus, [dt_raw], [dt_bias], [2 * H])
    dt_t = dt.T
    dtr = (dt_t[:H, None, :], dt_t[H:, None, :])
    a_all = -jnp.exp(ssd_a_log.reshape(2, H, 1, 1))
    a_neg = (a_all[0], a_all[1])
    (y_f, y_b), s_enter = _ssd_fwd(xbc, DI, DI + G * N, dtr, a_neg, P, ncc)
    gate_rows = [y_f, y_b, (xbc, 0, DI, Lc), (proj, 0, DI, Lc)]
    d_rep = _par(jnp.repeat(ssd_d[0], P))
    ssd_nw = _par(ssd_norm_w[0])
    yn = _rw_fwd("l0_ssd_gate", _f_ssd_gate, gate_rows, [d_rep, ssd_nw], [DI], T=L, out_dtypes=[BF16])
    Wb = finish_gather("mix", gather_b, yn)
    w_out, w_pw1, w_pw2 = Wb["ssd_w_out"][0], Wb["conf_w_pw1"][0], Wb["conf_w_pw2"][0]
    mix0 = _mm(yn, w_out, name="l0_w_out")
    g1_0, g2_0, g1_1, g2_1 = _par(g1[0]), _par(g2[0]), _par(g1[1]), _par(g2[1])
    h1 = _rw_fwd("l0_res1", _f_gate_res, [xl, mix0], [g1_0], [D])
    Wb = finish_gather("ffn", gather_c, h1)
    w_up, w_dn = Wb["ffn_w_up"], Wb["ffn_w_down"]

    grid_taps = [((i - 1) * GRID_W + (j - 1), (None if j == 1 else ("col", j - 1))) for i in range(3) for j in range(3)]

    def ffn_fwd(l, h, tag):
        a = _rw_fwd(tag + "_modnorm2", _f_modnorm, [h], [_par(norm2_w[l]), _par(sc2[l]), _par(sh2[l])], [D],
                    out_dtypes=[BF16])
        hh = _mm(a, w_up[l], name=tag + "_w_up")
        gc = _conv_fwd(tag + "_ffn_conv", hh, FH, FH, Wf["ffn_conv_w"][l].reshape(9, FH), ffn_conv_b[l][None, :],
                       grid_taps)
        act = _rw_fwd(tag + "_act", _f_ffn_act, [(hh, 0, FH), gc], [], [FH], col_tile=_tile(FH, 1536),
                      out_dtypes=[BF16])
        dn = _mm(act, w_dn[l], name=tag + "_w_down")
        return a, hh, gc, act, dn

    a1, hh0, gc0, act0, dn0 = ffn_fwd(0, h1, "l0")
    h2 = _rw_fwd("l0_res2", _f_gate_res, [h1, dn0], [g2_0], [D])

    a2 = _rw_fwd("l1_modnorm1", _f_modnorm, [h2], [n1w1, _par(sc1[1]), _par(sh1[1])], [D], out_dtypes=[BF16])
    pw = _mm(a2, w_pw1, name="l1_pw1")
    b_pw1 = Wf["conf_b_pw1"][0]
    glu = _rw_fwd("l1_glu", _f_glu, [(pw, 0, D), (pw, D, D)], [_par(b_pw1[:D]), _par(b_pw1[D:])], [D])
    conf_taps = [(k - KC // 2, None) for k in range(KC)]
    cv = _conv_fwd("l1_conv", glu, 0, D, Wf["conf_w_dw"][0], Wf["conf_b_dw"], conf_taps)
    ln_w, ln_b = _par(Wf["conf_ln_w"][0]), _par(Wf["conf_ln_b"][0])
    ls = _rw_fwd("l1_ln_silu", _f_ln_silu, [cv], [ln_w, ln_b], [D], out_dtypes=[BF16])
    p2 = _mm(ls, w_pw2, name="l1_pw2")
    b_pw2 = _par(Wf["conf_b_pw2"][0])
    h3 = _rw_fwd("l1_res1", _f_gate_res_bias, [h2, p2], [g1_1, b_pw2], [D])
    a3, hh1, gc1, act1, dn1 = ffn_fwd(1, h3, "l1")
    h4 = _rw_fwd("l1_res2", _f_gate_res, [h3, dn1], [g2_1], [D])

    fnw = final_norm_w[None, :]
    tgt = loss_target[0]
    loss_local = _loss_fwd(h4, tgt, fnw)[0, 0]
    loss = lax.psum(loss_local, ("x", "y", "c"))

    G_full = {}
    reduces = {}

    def start_reduce(tag, items, dep):
        def blocks_of(g, ax):
            if g.ndim == 3:
                return g
            return g.reshape(N_CHIPS, g.shape[0] // N_CHIPS, g.shape[1]) if ax == 0 else _to_blocks(g, ax)

        blocks = [blocks_of(g, ax).astype(BF16) for _, g, ax in items]
        handle, tok = _exchange4_start("reduce_" + tag + "_start", blocks, False, dep)
        reduces[tag] = ([n for n, _, _ in items], handle)
        return tok
    ones = jnp.ones((L, 1), F32)
    (dh4,), (dfnw,) = _rw_bwd("loss_bwd", _f_loss_rows, [h4, tgt], [_par(final_norm_w)], [ones],
                              row_grad=[True, False], par_grad=[True])
    G_full["final_norm_w"] = dfnw.reshape(D)

    def ffn_bwd(l, h, saved, g2_l, dh_out, tag):
        a, hh, gc, act, dn = saved
        (ddn,), (dg2,) = _rw_bwd(tag + "_res2_bwd", _f_gate, [dn], [g2_l], [dh_out],
                                 row_grad=[True], par_grad=[True], row_dtypes=[BF16])
        dact = _mm(ddn, w_dn[l], tb=True, name=tag + "_w_down_dx")
        dwdn = _mm(act, ddn, ta=True, name=tag + "_w_down_dw", out_dtype=BF16)
        (dval, dgc), _ = _rw_bwd(tag + "_act_bwd", _f_ffn_act, [(hh, 0, FH), gc], [], [dact],
                                 row_grad=[True, True], par_grad=[], col_tile=_tile(FH, 1536), row_dtypes=[BF16, F32])
        dgin, dcw, dcb = _conv_bwd(tag + "_ffn_conv_bwd", hh, FH, FH, Wf["ffn_conv_w"][l].reshape(9, FH), dgc,
                                   grid_taps, du_dtype=BF16)
        dhh = jnp.concatenate([dval, dgin], axis=1)
        da = _mm(dhh, w_up[l], tb=True, name=tag + "_w_up_dx")
        dwup = _mm(a, dhh, ta=True, name=tag + "_w_up_dw", out_dtype=BF16, col_blocks=N_CHIPS)
        (dh,), (dn2w, dsc2, dsh2) = _rw_bwd(
            tag + "_modnorm2_bwd", _f_modnorm, [h], [_par(norm2_w[l]), _par(sc2[l]), _par(sh2[l])], [da],
            row_grad=[True], par_grad=[True, True, True], add=dh_out)
        return dh, dict(w_down=dwdn, w_up=dwup, conv_w=dcw.reshape(3, 3, FH), conv_b=dcb.reshape(FH),
                        n2w=dn2w.reshape(D), sc2=dsc2.reshape(D), sh2=dsh2.reshape(D), g2=dg2.reshape(D))

    dh3, gf1 = ffn_bwd(1, h3, (a3, hh1, gc1, act1, dn1), g2_1, dh4, "l1")
    (dp2,), (dg1_1, db_pw2) = _rw_bwd("l1_res1_bwd", _f_gate_bias, [p2], [g1_1, b_pw2], [dh3],
                                      row_grad=[True], par_grad=[True, True], row_dtypes=[BF16])
    dls = _mm(dp2, w_pw2, tb=True, name="l1_pw2_dx")
    dw_pw2 = _mm(ls, dp2, ta=True, name="l1_pw2_dw", out_dtype=BF16)
    (dcv,), (dln_w, dln_b) = _rw_bwd("l1_ln_silu_bwd", _f_ln_silu, [cv], [ln_w, ln_b], [dls],
                                     row_grad=[True], par_grad=[True, True])
    dglu, dw_dw, db_dw = _conv_bwd("l1_conv_bwd", glu, 0, D, Wf["conf_w_dw"][0], dcv, conf_taps)
    (dpa, dpg), (dba, dbg) = _rw_bwd("l1_glu_bwd", _f_glu, [(pw, 0, D), (pw, D, D)],
                                     [_par(b_pw1[:D]), _par(b_pw1[D:])], [dglu],
                                     row_grad=[True, True], par_grad=[True, True], row_dtypes=[BF16, BF16])
    dpw = jnp.concatenate([dpa, dpg], axis=1)
    da2 = _mm(dpw, w_pw1, tb=True, name="l1_pw1_dx")
    dw_pw1 = _mm(a2, dpw, ta=True, name="l1_pw1_dw", out_dtype=BF16, col_blocks=N_CHIPS)
    (dh2,), (dn1w1, dsc1_1, dsh1_1) = _rw_bwd(
        "l1_modnorm1_bwd", _f_modnorm, [h2], [n1w1, _par(sc1[1]), _par(sh1[1])], [da2],
        row_grad=[True], par_grad=[True, True, True], add=dh3)
    G_full["conf_b_pw2"] = db_pw2.reshape(1, D)
    G_full["conf_ln_w"], G_full["conf_ln_b"] = dln_w.reshape(1, D), dln_b.reshape(1, D)
    G_full["conf_w_dw"], G_full["conf_b_dw"] = dw_dw[None], db_dw.reshape(1, D)
    G_full["conf_b_pw1"] = jnp.concatenate([dba.reshape(1, D), dbg.reshape(1, D)], axis=1)

    token = start_reduce("l1", [("conf_w_pw2", dw_pw2, 0), ("conf_w_pw1", dw_pw1, 1), ("ffn_w_up1", gf1["w_up"], 1),
                                ("ffn_w_down1", gf1["w_down"], 0)], dw_pw2)
    dh2 = _tie("tie_reduce_l1", dh2, token)
    dh1, gf0 = ffn_bwd(0, h1, (a1, hh0, gc0, act0, dn0), g2_0, dh2, "l0")
    G_full["ffn_conv_w"] = jnp.stack([gf0["conv_w"], gf1["conv_w"]])
    G_full["ffn_conv_b"] = jnp.stack([gf0["conv_b"], gf1["conv_b"]])

    (dmix,), (dg1_0,) = _rw_bwd("l0_res1_bwd", _f_gate, [mix0], [g1_0], [dh1],
                                row_grad=[True], par_grad=[True], row_dtypes=[BF16])
    dyn = _mm(dmix, w_out, tb=True, name="l0_w_out_dx")
    dw_out = _mm(yn, dmix, ta=True, name="l0_w_out_dw", out_dtype=BF16)
    token = start_reduce("l0", [("ffn_w_up0", gf0["w_up"], 1), ("ffn_w_down0", gf0["w_down"], 0),
                                ("ssd_w_out", dw_out, 0)], dw_out)
    dyn = _tie("tie_reduce_l0", dyn, token)
    (dy_lat, dxs_gate, dz_lat), (dd_rep, dssd_nw) = _rw_bwd(
        "l0_ssd_gate_bwd", _f_ssd_gate, gate_rows, [d_rep, ssd_nw], [dyn],
        row_grad=[True, False, True, True], par_grad=[True, True], T=L, row_dtypes=[F32, F32, BF16])
    g_f, g_b = _ssd_bwd(xbc, DI, DI + G * N, dtr, a_neg, s_enter, dy_lat, P, ncc)
    silu_bwd = functools.partial(_rw_bwd, f=_silu, pars=[], row_grad=[True], par_grad=[], T=T0)
    (dxs_pre,), _ = silu_bwd("l0_silu_bwd_x", rows=[(xbc_pre, 0, DI)], cot_fn=lambda p, q, r: p + q + r,
                             cots=[g_f[0], g_b[0], (dxs_gate, 0, DI, -Lc)],
                             col_tile=_tile(DI, 1024))
    (db_pre,), _ = silu_bwd("l0_silu_bwd_b", rows=[(xbc_pre, DI, G * N)], cot_fn=lambda p, q: p + q,
                            cots=[g_f[1], g_b[1]], col_tile=_tile(G * N, 1024))
    (dc_pre,), _ = silu_bwd("l0_silu_bwd_c", rows=[(xbc_pre, DI + G * N, G * N)], cot_fn=lambda p, q: p + q,
                            cots=[g_f[2], g_b[2]], col_tile=_tile(G * N, 1024))
    conv_w0 = Wf["ssd_conv_w"][0]
    pieces = []
    for tag, off, width, g_pre in (("x", 0, DI, dxs_pre), ("b", DI, G * N, db_pre), ("c", DI + G * N, G * N, dc_pre)):
        pieces.append(_conv_bwd("l0_conv_bwd_" + tag, proj, DI + off, width, conv_w0[:, off:off + width], g_pre,
                                seg_taps, du_dtype=BF16))
    dconv_in = [p[0] for p in pieces]
    dcw0 = jnp.concatenate([p[1] for p in pieces], axis=1)
    dcb0 = jnp.concatenate([p[2] for p in pieces], axis=1)
    ddt = jnp.concatenate([g_f[3][:, 0, :].T, g_b[3][:, 0, :].T], axis=1)
    (ddt_raw,), (ddt_bias,) = _rw_bwd("l0_softplus_bwd", _f_softplus, [dt_raw], [dt_bias], [ddt],
                                      row_grad=[True], par_grad=[True], row_dtypes=[BF16])
    dproj = jnp.concatenate([jnp.pad(dz_lat, ((Lc, 0), (0, 0))), *dconv_in, ddt_raw], axis=1)
    da0 = _mm(dproj, w_in, tb=True, name="l0_w_in_dx")
    dw_in = _mm(a0, dproj, ta=True, name="l0_w_in_dw", out_dtype=BF16)
    token = start_reduce("in", [("ssd_w_in", dw_in, 1)], dw_in)
    da0 = _tie("tie_reduce_in", da0, token)
    (dhcat,), (dn1w0, dsc_seg, dsh_seg) = _rw_bwd(
        "l0_modnorm1_bwd", _f_modnorm, [], [n1w0, sc_seg, sh_seg], [da0], T=T0, head=rows0,
        row_grad=[True], par_grad=[True, True, True], seg_rows=(Lc,), add=(dh1, 0, D, -Lc))
    grad_x = dhcat[Lc:][None]

    da_heads = jnp.stack([g[4][:, 0, 0].reshape(G, T0 // SSD_CHUNK, H // G).sum(axis=1).reshape(H)
                          for g in (g_f, g_b)])[None]
    G_full["ssd_a_log"] = da_heads * (-jnp.exp(ssd_a_log))
    G_full["ssd_dt_bias"] = ddt_bias.reshape(1, 2, H)
    G_full["ssd_d"] = dd_rep.reshape(H, P).sum(axis=1)[None]
    G_full["ssd_norm_w"] = dssd_nw.reshape(1, DI)
    G_full["ssd_conv_w"], G_full["ssd_conv_b"] = dcw0[None], dcb0.reshape(1, CD)
    G_full["norm1_w"] = jnp.stack([dn1w0.reshape(D), dn1w1.reshape(D)])
    G_full["norm2_w"] = jnp.stack([gf0["n2w"], gf1["n2w"]])

    zD = jnp.zeros((D,), F32)
    dm_own = jnp.stack([
        jnp.concatenate([dsh_seg[1, 0], dsc_seg[1, 0], dg1_0.reshape(D), gf0["sh2"], gf0["sc2"], gf0["g2"]]),
        jnp.concatenate([dsh1_1.reshape(D), dsc1_1.reshape(D), dg1_1.reshape(D), gf1["sh2"], gf1["sc2"], gf1["g2"]]),
    ])
    dmc_own = jnp.concatenate([dsh_seg[0, 0], dsc_seg[0, 0], zD, zD, zD, zD])

    out = {}

    def finish_reduce(tags, after, swap_name):
        partial = {}
        for tag in tags:
            names, handle = reduces[tag]
            blocks, landed = _exchange4_wait("reduce_" + tag + "_wait", handle, after)
            for n, blk, own in zip(names, landed, blocks):
                r = _fill_own(blk, own, chip, False)
                partial[n] = _sum_leading("sum4_" + n, r.reshape(N_CHIPS, -1, r.shape[-1]),
                                          (0, 1, 2, 3), out_dtype=BF16).reshape(r.shape[1:])
        for n in ("ffn_w_up", "ffn_w_down"):
            if n + "0" in partial:
                partial[n] = jnp.stack([partial.pop(n + "0"), partial.pop(n + "1")])
        names = [n for n in BIG if n in partial]
        mine = [partial[n].reshape(W[n].shape) for n in names]
        for n, own, sib in zip(names, mine, _swap_sibling(swap_name, mine)):
            out[n] = _adamw("adamw_" + n, W[n], Mo[n], Vo[n], own, sib)
        return names

    early = finish_reduce(["l1", "l0"], dhcat, "swap_grads_early")

    small_sum_names = [n for n in SMALL if n not in ("c_ctx", "mod_b")]
    sum_part = [G_full[n] for n in small_sum_names] + [dmc_own]
    packed = _tie("tie_small_grads", _pack(sum_part + [dm_own]), out[early[-1]][1])
    gat = _allgather8("gather_small_grads", packed)
    total = _sum_leading("sum_small_grads", gat, tuple(range(N_DEV)))
    summed = _unpack(total, [a.shape for a in sum_part])
    Gs = dict(zip(small_sum_names, summed[:-1]))
    dmc_tot = summed[-1]
    dm_all = _unpack(gat, [a.shape for a in sum_part] + [dm_own.shape])[-1].transpose(1, 0, 2)
    dm16 = jnp.concatenate([dm_all, jnp.stack([dmc_tot, jnp.zeros_like(dmc_tot)])[:, None, :],
                            jnp.zeros((2, 16 - N_DEV - 1, 6 * D), F32)], axis=1)
    Gs["mod_b"] = _sum_leading("sum_mod_b", dm16.transpose(1, 0, 2).reshape(16, 2 * 6 * D // LANE, LANE),
                               tuple(range(N_DEV + 1))).reshape(2, 6 * D)

    dm16_shard = lax.dynamic_slice_in_dim(dm16, chip * S_mod, S_mod, axis=2)
    ds16 = _mm(dm16_shard[0], mod_w[0], tb=True, precision=HIGHEST, name="c_ctx_dx")
    sig = jax.nn.sigmoid(c_ctx)
    dcc_part = ds16[N_DEV] * (sig * (1.0 + c_ctx * (1.0 - sig)))
    gat_cc = _allgather8("gather_c_ctx_grad", _pack([dcc_part]))
    Gs["c_ctx"] = _sum_leading("sum_c_ctx_grad", gat_cc, (0, 2, 4, 6)).reshape(-1)[:D]

    s16t = _silu(c16).T
    out["mod_w"] = _mod_w_update(s16t, dm16_shard, mod_w, m_mod_w, v_mod_w)
    finish_reduce(["in"], out["mod_w"][0], "swap_grads_late")

    def own(n, full):
        if n in SHARD_AXIS:
            size = W[n].shape[SHARD_AXIS[n]]
            return lax.dynamic_slice_in_dim(full, chip * size, size, axis=SHARD_AXIS[n])
        return full

    def two_d(a):
        return a.reshape(1, -1) if a.ndim == 1 else a

    g_small = [own(n, Gs[n].reshape(Wf[n].shape)) for n in SMALL]
    res = _adamw_many("adamw_small", [two_d(W[n]) for n in SMALL], [two_d(Mo[n]) for n in SMALL],
                      [two_d(Vo[n]) for n in SMALL], [two_d(g) for g in g_small])
    for n, r in zip(SMALL, res):
        out[n] = tuple(t.reshape(W[n].shape) for t in r)

    grads = [out[n][0] for n in WEIGHTS]
    deltas = [out[n][1] for n in WEIGHTS]
    new_m = [out[n][2] for n in WEIGHTS]
    new_v = [out[n][3] for n in WEIGHTS]
    return (loss, grad_x, *grads, *deltas, *new_m, *new_v)
```

```python
import functools

import jax
import jax.numpy as jnp
from jax import lax
from jax.experimental import pallas as pl
from jax.experimental.pallas import tpu as pltpu

F32 = jnp.float32
BF16 = jnp.bfloat16
MESH = pl.DeviceIdType.MESH
HIGHEST = lax.Precision.HIGHEST

VMEM_LIMIT_BYTES = 48 * 1024 * 1024
LANE = 128
SUBLANE = 8

SSD_STATE = 128
SSD_CHUNK = 128
GRID_W = 64
EPS = 1e-6
N_CHIPS = 4
N_DEV = 8

ADAM_LR = 0.001
ADAM_B1 = 0.9
ADAM_B2 = 0.999
ADAM_EPS = 1e-08
ADAM_WD = 0.01
ADAM_STEP = 10


def _pcall(body, **kw):
    return pl.pallas_call(body, **kw)


def _cparams(n_grid):
    return pltpu.CompilerParams(dimension_semantics=("arbitrary",) * n_grid, vmem_limit_bytes=VMEM_LIMIT_BYTES)


def _cdiv(a, b):
    return -(-a // b)


def _round_up(a, b):
    return _cdiv(a, b) * b


def _tile(n, cap):
    if n <= cap:
        return n
    best = None
    for t in range(LANE, cap + 1, LANE):
        if n % t == 0:
            best = t
    if best is None:
        npad = _round_up(n, LANE)
        for t in range(LANE, cap + 1, LANE):
            if npad % t == 0:
                best = t
    return best


def _row_tile(n, cap, also=()):
    best = None
    for step in (2 * SUBLANE, SUBLANE):
        for t in range(step, min(cap, n) + 1, step):
            if n % t == 0 and all(a % t == 0 for a in also):
                best = t
        if best is not None:
            break
    assert best is not None, (n, cap, also)
    return best


def _silu(v):
    return v * jax.nn.sigmoid(v)


def _mm(a, b, *, name, ta=False, tb=False, precision=None, cap=1024, out_dtype=F32, col_blocks=None):
    M, K = (a.shape[1], a.shape[0]) if ta else a.shape
    N = b.shape[0] if tb else b.shape[1]
    assert K == (b.shape[1] if tb else b.shape[0]), (a.shape, b.shape, ta, tb)
    tm, tk = _tile(M, cap), _tile(K, cap + cap // 2)
    tn = _tile(N if col_blocks is None else N // col_blocks, cap + cap // 2)
    nm, nn, nk = _cdiv(M, tm), _cdiv(N, tn), _cdiv(K, tk)
    k_tail = K % tk
    exact = precision is not None

    def body(a_ref, b_ref, o_ref, acc_ref):
        k = pl.program_id(2)

        @pl.when(k == 0)
        def _():
            acc_ref[...] = jnp.zeros_like(acc_ref)

        av = a_ref[...]
        bv = b_ref[...]
        if k_tail:
            lim = K - k * tk
            ka = lax.broadcasted_iota(jnp.int32, av.shape, 0 if ta else 1)
            kb = lax.broadcasted_iota(jnp.int32, bv.shape, 1 if tb else 0)
            av = jnp.where(ka < lim, av, jnp.zeros_like(av))
            bv = jnp.where(kb < lim, bv, jnp.zeros_like(bv))
        if exact:
            av = av.astype(F32)
            bv = bv.astype(F32)
        else:
            av = av.astype(BF16)
            bv = bv.astype(BF16)
        dn = (((0 if ta else 1,), (1 if tb else 0,)), ((), ()))
        acc_ref[...] += lax.dot_general(av, bv, dn, preferred_element_type=F32, precision=precision)

        @pl.when(k == nk - 1)
        def _():
            o_ref[...] = acc_ref[...].astype(o_ref.dtype)

    a_spec = pl.BlockSpec((tk, tm), lambda i, j, k: (k, i)) if ta else pl.BlockSpec((tm, tk), lambda i, j, k: (i, k))
    b_spec = pl.BlockSpec((tn, tk), lambda i, j, k: (j, k)) if tb else pl.BlockSpec((tk, tn), lambda i, j, k: (k, j))
    if col_blocks is None:
        out_spec = pl.BlockSpec((tm, tn), lambda i, j, k: (i, j))
        out_shape = jax.ShapeDtypeStruct((M, N), out_dtype)
    else:
        per = (N // col_blocks) // tn
        assert per * tn * col_blocks == N, (N, col_blocks, tn)
        out_spec = pl.BlockSpec((None, tm, tn), lambda i, j, k: (j // per, i, j % per))
        out_shape = jax.ShapeDtypeStruct((col_blocks, M, N // col_blocks), out_dtype)
    return _pcall(
        body, name=name, grid=(nm, nn, nk), in_specs=[a_spec, b_spec], out_specs=out_spec, out_shape=out_shape,
        scratch_shapes=[pltpu.VMEM((tm, tn), F32)], compiler_params=_cparams(3),
    )(a, b)


def _norm_rows(rows):
    out = []
    for r in rows:
        if not isinstance(r, tuple):
            r = (r,)
        arr, off, width, roff = (r + (0, None, 0)[len(r) - 1:])
        out.append((arr, off, width if width is not None else arr.shape[1], roff))
    return out


def _rw_plan(T, rows, pars, seg_rows, col_tile, tm_cap):
    widths = [r[2] for r in rows]
    wmax = max(widths + [p.shape[-1] for p in pars] + [1])
    if col_tile is not None:
        assert all(w == widths[0] for w in widths) and all(p.shape[-1] == widths[0] for p in pars)
        ncol = widths[0] // col_tile
        assert ncol * col_tile == widths[0]
        wmax = col_tile
    else:
        ncol = 1
    cap = tm_cap if tm_cap is not None else max(SUBLANE, min(512, (512 * 1024) // wmax))
    tm = _row_tile(T, cap, also=tuple(seg_rows) + tuple(abs(r[3]) for r in rows if r[3]))
    bounds = tuple(s // tm for s in seg_rows)
    return widths, ncol, tm, bounds


def _rw_specs(rows, pars, ncol, tm, bounds, col_tile):
    def seg(i):
        s = 0
        for b in bounds:
            s = s + (i >= b).astype(jnp.int32)
        return s

    specs = []
    for arr, off, w, roff in rows:
        bw = col_tile if col_tile is not None else w
        assert off % bw == 0 and roff % tm == 0, (off, bw, roff, tm)
        specs.append(pl.BlockSpec((tm, bw), functools.partial(
            lambda j, i, ob, rb, last: (jnp.clip(i + rb, 0, last), ob + j),
            ob=off // bw, rb=roff // tm, last=arr.shape[0] // tm - 1)))
    for p in pars:
        bw = col_tile if col_tile is not None else p.shape[-1]
        if p.shape[0] > 1:
            specs.append(pl.BlockSpec((None, 1, bw), lambda j, i: (seg(i), 0, j)))
        else:
            specs.append(pl.BlockSpec((None, 1, bw), lambda j, i: (0, 0, j)))
    return specs, seg


def _head_rows(head):
    top, bottom = head
    return [(top, 0, None, 0), (bottom, 0, None, -top.shape[0])]


def _rw_fwd(name, f, rows, pars, out_widths, *, T=None, seg_rows=(), col_tile=None, tm_cap=None, out_dtypes=None,
            head=None):
    rows = _norm_rows((_head_rows(head) if head else []) + list(rows))
    T = rows[0][0].shape[0] if T is None else T
    widths, ncol, tm, bounds = _rw_plan(T, rows, pars, seg_rows, col_tile, tm_cap)
    in_specs, _ = _rw_specs(rows, pars, ncol, tm, bounds, col_tile)
    nr, npar, nout = len(rows), len(pars), len(out_widths)

    def body(*refs):
        vals = [r[...] for r in refs[:nr + npar]]
        if head:
            vals = [jnp.where(pl.program_id(1) < head[0].shape[0] // tm, vals[0], vals[1])] + vals[2:]
        outs = f(*vals)
        if not isinstance(outs, (tuple, list)):
            outs = (outs,)
        for o_ref, o in zip(refs[nr + npar:], outs):
            o_ref[...] = o.astype(o_ref.dtype)

    out_specs = [pl.BlockSpec((tm, col_tile if col_tile is not None else w), lambda j, i: (i, j)) for w in out_widths]
    res = _pcall(
        body, name=name, grid=(ncol, T // tm), in_specs=in_specs, out_specs=out_specs,
        out_shape=[jax.ShapeDtypeStruct((T, w), dt) for w, dt in zip(out_widths, out_dtypes or [F32] * nout)],
        compiler_params=_cparams(2),
    )(*[r[0] for r in rows], *pars)
    return res if nout > 1 else res[0]


def _rw_bwd(name, f, rows, pars, cots, *, row_grad, par_grad, T=None, seg_rows=(), col_tile=None, tm_cap=None,
            add=None, cot_fn=None, row_dtypes=None, head=None):
    rows = _norm_rows((_head_rows(head) if head else []) + list(rows))
    cots = _norm_rows(cots)
    T = rows[0][0].shape[0] if T is None else T
    extra = _norm_rows([add]) if add is not None else []
    all_rows = rows + cots + extra
    widths, ncol, tm, bounds = _rw_plan(T, all_rows, pars, seg_rows, col_tile, tm_cap)
    in_specs, seg = _rw_specs(all_rows, pars, ncol, tm, bounds, col_tile)
    nr, nc, ne, npar = len(rows), len(cots), len(extra), len(pars)
    skip = 1 if head else 0
    widths = widths[skip:]
    nrf = nr - skip
    row_idx = [k for k in range(nrf) if row_grad[k]]
    par_idx = [k for k in range(npar) if par_grad[k]]

    def body(*refs):
        i = pl.program_id(1)

        def zero_before(vals, ops):
            return [jnp.where(i + c[3] // tm >= 0, v, jnp.zeros_like(v)) if c[3] < 0 else v for v, c in zip(vals, ops)]

        row_vals = [r[...] for r in refs[:nr]]
        if head:
            row_vals = [jnp.where(i < head[0].shape[0] // tm, row_vals[0], row_vals[1])] + row_vals[2:]
        cot_vals = zero_before([r[...] for r in refs[nr:nr + nc]], cots)
        add_vals = zero_before([r[...] for r in refs[nr + nc:nr + nc + ne]], extra)
        par_vals = [r[...] for r in refs[nr + nc + ne:nr + nc + ne + npar]]
        out_refs = refs[nr + nc + ne + npar:]
        outs, vjp = jax.vjp(f, *row_vals, *par_vals)
        if cot_fn is not None:
            cot_vals = cot_fn(*cot_vals)
            if not isinstance(cot_vals, (tuple, list)):
                cot_vals = (cot_vals,)
        if isinstance(outs, (tuple, list)):
            grads = vjp(tuple(c.astype(o.dtype) for c, o in zip(cot_vals, outs)))
        else:
            grads = vjp(cot_vals[0].astype(outs.dtype))
        first_seg = i == 0
        for b in bounds:
            first_seg = first_seg | (i == b)
        for n, k in enumerate(row_idx):
            g = grads[k]
            if n == 0 and add_vals:
                g = g + add_vals[0]
            out_refs[n][...] = g.astype(out_refs[n].dtype)
        for n, k in enumerate(par_idx):
            g = grads[nrf + k]
            o_ref = out_refs[len(row_idx) + n]
            first = first_seg if pars[k].shape[0] > 1 else (i == 0)

            @pl.when(first)
            def _(o_ref=o_ref, g=g):
                o_ref[...] = g

            @pl.when(jnp.logical_not(first))
            def _(o_ref=o_ref, g=g):
                o_ref[...] += g

    out_specs, out_shape = [], []
    for k in row_idx:
        w = widths[k]
        out_specs.append(pl.BlockSpec((tm, col_tile if col_tile is not None else w), lambda j, i: (i, j)))
        out_shape.append(jax.ShapeDtypeStruct((T, w), row_dtypes[len(out_shape)] if row_dtypes else F32))
    for k in par_idx:
        p = pars[k]
        bw = col_tile if col_tile is not None else p.shape[-1]
        if p.shape[0] > 1:
            out_specs.append(pl.BlockSpec((None, 1, bw), lambda j, i: (seg(i), 0, j)))
        else:
            out_specs.append(pl.BlockSpec((None, 1, bw), lambda j, i: (0, 0, j)))
        out_shape.append(jax.ShapeDtypeStruct(p.shape, F32))
    res = _pcall(
        body, name=name, grid=(ncol, T // tm), in_specs=in_specs, out_specs=out_specs, out_shape=out_shape,
        compiler_params=_cparams(2),
    )(*[r[0] for r in all_rows], *pars)
    return list(res[:len(row_idx)]), list(res[len(row_idx):])


def _f_modnorm(h, w, sc, sh):
    y = h * lax.rsqrt(jnp.mean(h * h, axis=-1, keepdims=True) + EPS)
    return (y * w) * (1.0 + sc) + sh


def _f_gate_res(h, y, g):
    return h + g * y


def _f_gate_res_bias(h, y, g, b):
    return h + g * (y + b)


def _f_gate(y, g):
    return g * y


def _f_gate_bias(y, g, b):
    return g * (y + b)


def _f_ffn_act(val, gate):
    return _silu(gate) * val


def _f_softplus(raw, bias):
    v = raw + bias
    return jnp.maximum(v, 0.0) + jnp.log(1.0 + jnp.exp(-jnp.abs(v)))


def _f_ssd_gate(yf, yb, xs, z, d_rep, nw):
    y = (yf + yb + d_rep * xs) * _silu(z)
    return (y * lax.rsqrt(jnp.mean(y * y, axis=-1, keepdims=True) + EPS)) * nw


def _f_glu(a, g, ba, bg):
    return (a + ba) * jax.nn.sigmoid(g + bg)


def _f_ln_silu(h, w, b):
    mu = jnp.mean(h, axis=-1, keepdims=True)
    d = h - mu
    y = d * lax.rsqrt(jnp.mean(d * d, axis=-1, keepdims=True) + EPS)
    return _silu(y * w + b)


def _f_loss_rows(h, t, w):
    y = (h * lax.rsqrt(jnp.mean(h * h, axis=-1, keepdims=True) + EPS)) * w
    e = y - t
    return 0.5 * jnp.mean(e * e, axis=-1, keepdims=True)


def _f_adamw(w, m, v, ga, gb):
    g = ga.astype(F32) + gb
    m = ADAM_B1 * m + (1.0 - ADAM_B1) * g
    v = ADAM_B2 * v + (1.0 - ADAM_B2) * (g * g)
    m_hat = m / (1.0 - ADAM_B1 ** ADAM_STEP)
    v_hat = v / (1.0 - ADAM_B2 ** ADAM_STEP)
    delta = -ADAM_LR * (m_hat / (jnp.sqrt(v_hat) + ADAM_EPS) + ADAM_WD * w)
    return g, delta, m, v


def _adamw(name, w, m, v, ga, gb):
    shape = w.shape
    c = shape[-1]
    two_d = [t.reshape(-1, c) for t in (w, m, v, ga, gb)]
    rows = two_d[0].shape[0]
    pad = _round_up(rows, SUBLANE) - rows
    if pad:
        two_d = [jnp.pad(t, ((0, pad), (0, 0))) for t in two_d]
    outs = _rw_fwd(name, _f_adamw, two_d, [], [c] * 4)
    return tuple(o[:rows].reshape(shape) for o in outs)


def _sum_leading(name, x, idxs, out_dtype=F32):
    _, R, C = x.shape
    tm = _row_tile(R, max(SUBLANE, min(512, (512 * 1024) // C)))

    def body(x_ref, o_ref):
        acc = x_ref[idxs[0]].astype(F32)
        for k in idxs[1:]:
            acc = acc + x_ref[k].astype(F32)
        o_ref[...] = acc.astype(o_ref.dtype)

    return _pcall(
        body, name=name, grid=(R // tm,), in_specs=[pl.BlockSpec((x.shape[0], tm, C), lambda i: (0, i, 0))],
        out_specs=pl.BlockSpec((tm, C), lambda i: (i, 0)), out_shape=jax.ShapeDtypeStruct((R, C), out_dtype),
        compiler_params=_cparams(1),
    )(x)


def _loss_fwd(h, t, w):
    T, D = h.shape
    tm = _row_tile(T, 256)

    def body(h_ref, t_ref, w_ref, o_ref):
        i = pl.program_id(0)
        part = jnp.sum(_f_loss_rows(h_ref[...], t_ref[...], w_ref[...]), axis=0, keepdims=True)
        part = jnp.broadcast_to(part, (1, LANE))

        @pl.when(i == 0)
        def _():
            o_ref[...] = part

        @pl.when(i > 0)
        def _():
            o_ref[...] += part

    return _pcall(
        body, name="loss_fwd", grid=(T // tm,),
        in_specs=[pl.BlockSpec((tm, D), lambda i: (i, 0)), pl.BlockSpec((tm, D), lambda i: (i, 0)),
                  pl.BlockSpec((1, D), lambda i: (0, 0))],
        out_specs=pl.BlockSpec((1, LANE), lambda i: (0, 0)), out_shape=jax.ShapeDtypeStruct((1, LANE), F32),
        compiler_params=_cparams(1),
    )(h, t, w)


CONV_ROWS = 256
CONV_ROWS_FEW_TAPS = 1024
CONV_ACC_ELEMS = 16384


def _col_mask(arg, t):
    col = jnp.bitwise_and(t, GRID_W - 1)
    return (col != 0) if arg < 0 else (col != GRID_W - 1)


def _conv_plan(T, C, taps):
    seg = [m[1] for _, m in taps if m is not None and m[0] == "seg"]
    cap = CONV_ROWS_FEW_TAPS if len(taps) <= 9 else CONV_ROWS
    rc = next(r for r in (1024, 768, 512, 256, LANE) if r <= cap and T % r == 0)
    ct = next((t for t in (512, 256, LANE) if C % t == 0), C)
    reach = max(abs(s) for s, _ in taps)
    hb = next(h for h in (8, 16, 32, 64, 128, 256) if h >= reach and rc % h == 0)
    sub = max(2 * SUBLANE, min(rc, CONV_ACC_ELEMS // ct))
    boundary = None
    if seg:
        inside = seg[0] % rc
        boundary = (seg[0], (inside - reach, inside + reach) if inside else None)
    taps = [(s, None if (m is None or m[0] == "seg") else m[1]) for s, m in taps]
    return rc, ct, hb, sub, T // rc, C // ct, boundary, taps


def _seg_ok(boundary, i, rc, r0, n, s):
    if boundary is None or boundary[1] is None or s == 0 or r0 + n <= boundary[1][0] or r0 >= boundary[1][1]:
        return None
    t = i * rc + r0 + lax.broadcasted_iota(jnp.int32, (n, 1), 0)
    return (t >= boundary[0]) == ((t + s) >= boundary[0])


def _halo_specs(rc, ct, hb, T, off_blocks):
    per = rc // hb
    last = T // hb - 1
    prev = pl.BlockSpec((hb, ct), lambda j, i: (jnp.maximum(i * per - 1, 0), off_blocks + j))
    cur = pl.BlockSpec((rc, ct), lambda j, i: (i, off_blocks + j))
    nxt = pl.BlockSpec((hb, ct), lambda j, i: (jnp.minimum((i + 1) * per, last), off_blocks + j))
    return [prev, cur, nxt]


def _fill_halo(pad_ref, p_ref, c_ref, n_ref, i, nrc, rc, hb, boundary):
    has_prev = i > 0
    has_next = i < nrc - 1
    if boundary is not None:
        has_prev = has_prev & (i * rc != boundary[0])
        has_next = has_next & ((i + 1) * rc != boundary[0])
    pad_ref[0:hb, :] = jnp.where(has_prev, p_ref[...], 0.0)
    pad_ref[hb:hb + rc, :] = c_ref[...]
    pad_ref[hb + rc:hb + rc + hb, :] = jnp.where(has_next, n_ref[...], 0.0)


def _shift_plan(keys):
    count = {}
    for s, m in keys:
        k = (s % SUBLANE, m)
        count[k] = count.get(k, 0) + 1
    slots = {}
    for k, n in sorted(count.items(), key=lambda kv: (kv[0][0], str(kv[0][1]))):
        if k != (0, None) and (n >= 2 or k[1] is not None):
            slots[k] = len(slots)
    return slots


def _build_shifted(copies_ref, slots, pad_ref, keys, i, rc, hb, sub):
    for (r, m), slot in slots.items():
        qs = [s - r for s, mk in keys if (s % SUBLANE, mk) == (r, m)]
        lo, hi = hb + min(qs), hb + rc + max(qs)
        for p in range(lo, hi, sub):
            n = min(sub, hi - p)
            v = pad_ref[p + r:p + r + n, :]
            if m is not None:
                t = i * rc - hb + p + r + lax.broadcasted_iota(jnp.int32, (n, 1), 0)
                v = jnp.where(_col_mask(m, t), v, 0.0)
            copies_ref[slot, p:p + n, :] = v


def _read(copies_ref, slots, pad_ref, s, m, row, n):
    k = (s % SUBLANE, m)
    if k in slots:
        q = s - k[0]
        return copies_ref[slots[k], row + q:row + q + n, :]
    return pad_ref[row + s:row + s + n, :]


def _conv_fwd(name, u, col_off, C, w, b, taps, act=False):
    T = u.shape[0]
    rc, ct, hb, sub, nrc, ncc, boundary, taps = _conv_plan(T, C, taps)
    assert col_off % ct == 0
    K = len(taps)
    keys = [(s, None) for s, _ in taps]
    slots = _shift_plan(keys)
    dirs = sorted({m for _, m in taps if m is not None})

    def body(up, uc, un, w_ref, b_ref, *rest):
        y_ref = rest[0]
        pad_ref, copies_ref = rest[-2], rest[-1]
        i = pl.program_id(1)
        _fill_halo(pad_ref, up, uc, un, i, nrc, rc, hb, boundary)
        _build_shifted(copies_ref, slots, pad_ref, keys, i, rc, hb, sub)
        for r0 in range(0, rc, sub):
            acc = jnp.broadcast_to(b_ref[...], (sub, ct))
            for m in [None] + dirs:
                part = None
                for k, (s, mk) in enumerate(taps):
                    if mk != m:
                        continue
                    v = _read(copies_ref, slots, pad_ref, s, None, hb + r0, sub)
                    ok = _seg_ok(boundary, i, rc, r0, sub, s)
                    term = w_ref[k:k + 1, :] * (v if ok is None else jnp.where(ok, v, 0.0))
                    part = term if part is None else part + term
                if part is None:
                    continue
                if m is not None:
                    t = i * rc + r0 + lax.broadcasted_iota(jnp.int32, (sub, 1), 0)
                    part = jnp.where(_col_mask(m, t), part, 0.0)
                acc = acc + part
            y_ref[r0:r0 + sub, :] = acc
            if act:
                rest[1][r0:r0 + sub, :] = _silu(acc)

    n_out = 2 if act else 1
    res = _pcall(
        body, name=name, grid=(ncc, nrc),
        in_specs=_halo_specs(rc, ct, hb, T, col_off // ct) + [pl.BlockSpec((K, ct), lambda j, i: (0, j)),
                                                              pl.BlockSpec((1, ct), lambda j, i: (0, j))],
        out_specs=[pl.BlockSpec((rc, ct), lambda j, i: (i, j))] * n_out,
        out_shape=[jax.ShapeDtypeStruct((T, C), F32)] * n_out,
        scratch_shapes=[pltpu.VMEM((rc + 2 * hb, ct), F32), pltpu.VMEM((max(len(slots), 1), rc + 2 * hb, ct), F32)],
        compiler_params=_cparams(2),
    )(u, u, u, w, b)
    return res if act else res[0]


def _conv_bwd(name, u, col_off, C, w, g, taps, du_dtype=F32):
    T = u.shape[0]
    rc, ct, hb, sub, nrc, ncc, boundary, taps = _conv_plan(T, C, taps)
    K = len(taps)
    u_keys = [(s, None) for s, _ in taps]
    dirs = sorted({m for _, m in taps if m is not None})
    g_keys = [(-s, m) for s, m in taps] + [(0, m) for m in dirs]
    u_slots, g_slots = _shift_plan(u_keys), _shift_plan(g_keys)

    def body(up, uc, un, gp, gc, gn, w_ref, du_ref, dw_ref, db_ref, upad, gpad, ucopies, gcopies):
        i = pl.program_id(1)
        _fill_halo(upad, up, uc, un, i, nrc, rc, hb, boundary)
        _fill_halo(gpad, gp, gc, gn, i, nrc, rc, hb, boundary)
        _build_shifted(ucopies, u_slots, upad, u_keys, i, rc, hb, sub)
        _build_shifted(gcopies, g_slots, gpad, g_keys, i, rc, hb, sub)

        @pl.when(i == 0)
        def _():
            dw_ref[...] = jnp.zeros_like(dw_ref)
            db_ref[...] = jnp.zeros_like(db_ref)

        def fold(v):
            return jnp.sum(v.reshape(sub // SUBLANE, SUBLANE, ct), axis=0)

        dbs = jnp.zeros((SUBLANE, ct), F32)
        for r0 in range(0, rc, sub):
            dbs = dbs + fold(gpad[hb + r0:hb + r0 + sub, :])
            acc = jnp.zeros((sub, ct), F32)
            for k, (s, m) in enumerate(taps):
                v = _read(gcopies, g_slots, gpad, -s, m, hb + r0, sub)
                ok = _seg_ok(boundary, i, rc, r0, sub, -s)
                acc = acc + w_ref[k:k + 1, :] * (v if ok is None else jnp.where(ok, v, 0.0))
            du_ref[r0:r0 + sub, :] = acc.astype(du_ref.dtype)
        db_ref[...] += jnp.sum(dbs, axis=0, keepdims=True)
        for k, (s, m) in enumerate(taps):
            part = jnp.zeros((SUBLANE, ct), F32)
            for r0 in range(0, rc, sub):
                v = _read(ucopies, u_slots, upad, s, None, hb + r0, sub)
                ok = _seg_ok(boundary, i, rc, r0, sub, s)
                part = part + fold(_read(gcopies, g_slots, gpad, 0, m, hb + r0, sub)
                                   * (v if ok is None else jnp.where(ok, v, 0.0)))
            dw_ref[k:k + 1, :] += jnp.sum(part, axis=0, keepdims=True)

    halo_u = _halo_specs(rc, ct, hb, T, col_off // ct)
    halo_g = _halo_specs(rc, ct, hb, T, 0)
    rows = rc + 2 * hb
    return _pcall(
        body, name=name, grid=(ncc, nrc),
        in_specs=halo_u + halo_g + [pl.BlockSpec((K, ct), lambda j, i: (0, j))],
        out_specs=[pl.BlockSpec((rc, ct), lambda j, i: (i, j)), pl.BlockSpec((K, ct), lambda j, i: (0, j)),
                   pl.BlockSpec((1, ct), lambda j, i: (0, j))],
        out_shape=[jax.ShapeDtypeStruct((T, C), du_dtype), jax.ShapeDtypeStruct((K, C), F32),
                   jax.ShapeDtypeStruct((1, C), F32)],
        scratch_shapes=[pltpu.VMEM((rows, ct), F32), pltpu.VMEM((rows, ct), F32),
                        pltpu.VMEM((max(len(u_slots), 1), rows, ct), F32),
                        pltpu.VMEM((max(len(g_slots), 1), rows, ct), F32)],
        compiler_params=_cparams(2),
    )(u, u, u, g, g, g, w)


def _ssd_group(xg, bm, cm, s_in, *per_head, reverse, P):
    R = len(per_head) // 2
    dtrs, a_s = per_head[:R], per_head[R:]
    q, rp = xg.shape
    ii = lax.broadcasted_iota(jnp.int32, (q, q), 0)
    jj = lax.broadcasted_iota(jnp.int32, (q, q), 1)
    causal = (jj >= ii) if reverse else (jj <= ii)
    causal_t = (ii >= jj) if reverse else (ii <= jj)
    eye = ii == jj
    lane = lax.broadcasted_iota(jnp.int32, (1, rp), 1)
    row = lax.broadcasted_iota(jnp.int32, (rp, 1), 0)
    nt = (((1,), (1,)), ((), ()))
    tn = (((0,), (0,)), ((), ()))
    cb = lax.dot_general(cm.astype(BF16), bm.astype(BF16), nt, preferred_element_type=F32)
    dt_x = jnp.zeros((q, rp), F32)
    acum_x = jnp.zeros((q, rp), F32)
    tot_row = jnp.zeros((1, rp), F32)
    tot_col = jnp.zeros((rp, 1), F32)
    wts, lane_masks = [], []
    for r in range(R):
        hm = (lane >= r * P) & (lane < (r + 1) * P)
        hc = (row >= r * P) & (row < (r + 1) * P)
        dt_c = jnp.sum(jnp.where(eye, dtrs[r], 0.0), axis=1, keepdims=True)
        dac = dt_c * a_s[r]
        dar = dtrs[r] * a_s[r]
        acum_c = jnp.sum(jnp.where(causal, dar, 0.0), axis=1, keepdims=True)
        acum_r = jnp.sum(jnp.where(causal_t, dac, 0.0), axis=0, keepdims=True)
        decay = jnp.where(causal, jnp.exp(jnp.where(causal, acum_c - acum_r, 0.0)), 0.0)
        tot = jnp.sum(dac, axis=0, keepdims=True)
        dt_x = jnp.where(hm, dt_c, dt_x)
        acum_x = jnp.where(hm, acum_c, acum_x)
        tot_row = jnp.where(hm, tot, tot_row)
        tot_col = jnp.where(hc, tot, tot_col)
        wts.append((cb * decay).astype(BF16))
        lane_masks.append(hm)
    xdt = xg * dt_x
    xdt_b = xdt.astype(BF16)
    y = jnp.zeros((q, rp), F32)
    for r in range(R):
        y = jnp.where(lane_masks[r], jnp.dot(wts[r], xdt_b, preferred_element_type=F32), y)
    dte = jnp.exp(tot_row - acum_x)
    cs = lax.dot_general((xdt * dte).astype(BF16), bm.astype(BF16), tn, preferred_element_type=F32)
    y = y + lax.dot_general(cm.astype(BF16), s_in.astype(BF16), nt, preferred_element_type=F32) * jnp.exp(acum_x)
    s_out = jnp.exp(tot_col) * s_in + cs
    return y, s_out


def _ssd_group_state(xg, bm, s_in, *per_head, reverse, P):
    R = len(per_head) // 2
    dtrs, a_s = per_head[:R], per_head[R:]
    q, rp = xg.shape
    ii = lax.broadcasted_iota(jnp.int32, (q, q), 0)
    jj = lax.broadcasted_iota(jnp.int32, (q, q), 1)
    causal = (jj >= ii) if reverse else (jj <= ii)
    eye = ii == jj
    lane = lax.broadcasted_iota(jnp.int32, (1, rp), 1)
    row = lax.broadcasted_iota(jnp.int32, (rp, 1), 0)
    dt_x = jnp.zeros((q, rp), F32)
    acum_x = jnp.zeros((q, rp), F32)
    tot_row = jnp.zeros((1, rp), F32)
    tot_col = jnp.zeros((rp, 1), F32)
    for r in range(R):
        hm = (lane >= r * P) & (lane < (r + 1) * P)
        hc = (row >= r * P) & (row < (r + 1) * P)
        dt_c = jnp.sum(jnp.where(eye, dtrs[r], 0.0), axis=1, keepdims=True)
        acum_c = jnp.sum(jnp.where(causal, dtrs[r] * a_s[r], 0.0), axis=1, keepdims=True)
        tot = jnp.sum(dt_c * a_s[r], axis=0, keepdims=True)
        dt_x = jnp.where(hm, dt_c, dt_x)
        acum_x = jnp.where(hm, acum_c, acum_x)
        tot_row = jnp.where(hm, tot, tot_row)
        tot_col = jnp.where(hc, tot, tot_col)
    xe = xg * dt_x * jnp.exp(tot_row - acum_x)
    cs = lax.dot_general(xe.astype(BF16), bm.astype(BF16), (((0,), (0,)), ((), ())), preferred_element_type=F32)
    return jnp.exp(tot_col) * s_in + cs


def _ssd_maps(NC, ncc, reverse_steps):
    def chunk(d, s):
        if reverse_steps:
            s = NC - 1 - s
        return s if d == 0 else jnp.where(s < ncc, ncc - 1 - s, NC - 1 - s + ncc)

    def lat_chunk(d, s):
        c = chunk(d, s) - ncc
        return jnp.where(c < 0, 0 if d == 0 else NC - ncc - 1, c)

    def step(s):
        return NC - 1 - s if reverse_steps else s

    return chunk, lat_chunk, step


SSD_GROUPS_PER_STEP = 2


def _ssd_specs(chunk, d, GB, R, Q, N, RP, b_off, c_off):
    assert b_off % (GB * N) == 0 and c_off % (GB * N) == 0
    bo, co = b_off // (GB * N), c_off // (GB * N)
    return [
        pl.BlockSpec((Q, GB * RP), lambda g, s: (chunk(d, s), g)),
        pl.BlockSpec((Q, GB * N), lambda g, s: (chunk(d, s), bo + g)),
        pl.BlockSpec((Q, GB * N), lambda g, s: (chunk(d, s), co + g)),
        pl.BlockSpec((GB * R, 1, Q), lambda g, s: (g, 0, chunk(d, s))),
        pl.BlockSpec((GB * R, 1, 1), lambda g, s: (g, 0, 0)),
    ]


def _ssd_fwd(xbc, b_off, c_off, dtr, a, P, ncc):
    T = xbc.shape[0]
    H = dtr[0].shape[0]
    N, Q = SSD_STATE, SSD_CHUNK
    NC = T // Q
    G = (c_off - b_off) // N
    R = H // G
    RP = R * P
    GB = SSD_GROUPS_PER_STEP if G % SSD_GROUPS_PER_STEP == 0 else 1
    chunk, lat_chunk, _ = _ssd_maps(NC, ncc, False)

    def body(*refs):
        s = pl.program_id(1)
        s_ref = refs[-1]

        @pl.when(s == 0)
        def _():
            s_ref[...] = jnp.zeros_like(s_ref)

        for d in range(2):
            x_ref, b_ref, c_ref, dtr_ref, a_ref = refs[5 * d:5 * d + 5]
            y_ref, se_ref = refs[10 + 2 * d:12 + 2 * d]
            for gg in range(GB):
                cols, bcols = slice(gg * RP, (gg + 1) * RP), slice(gg * N, (gg + 1) * N)
                s_in = s_ref[d, gg]
                se_ref[gg] = s_in
                per_head = [dtr_ref[gg * R + r] for r in range(R)] + [a_ref[gg * R + r] for r in range(R)]

                @pl.when(s >= ncc)
                def _(d=d, gg=gg, cols=cols, bcols=bcols, x_ref=x_ref, b_ref=b_ref, c_ref=c_ref, y_ref=y_ref,
                      s_in=s_in, per_head=per_head):
                    y, s_out = _ssd_group(x_ref[:, cols], b_ref[:, bcols], c_ref[:, bcols], s_in, *per_head,
                                          reverse=d == 1, P=P)
                    y_ref[:, cols] = y
                    s_ref[d, gg] = s_out

                @pl.when(s < ncc)
                def _(d=d, gg=gg, cols=cols, bcols=bcols, x_ref=x_ref, b_ref=b_ref, s_in=s_in, per_head=per_head):
                    s_ref[d, gg] = _ssd_group_state(x_ref[:, cols], b_ref[:, bcols], s_in, *per_head,
                                                    reverse=d == 1, P=P)

    in_specs, out_specs, out_shape, operands = [], [], [], []
    for d in range(2):
        in_specs += _ssd_specs(chunk, d, GB, R, Q, N, RP, b_off, c_off)
        operands += [xbc, xbc, xbc, dtr[d], a[d]]
        out_specs += [pl.BlockSpec((Q, GB * RP), functools.partial(lambda g, s, d: (lat_chunk(d, s), g), d=d)),
                      pl.BlockSpec((GB, None, RP, N), lambda g, s: (g, s, 0, 0))]
        out_shape += [jax.ShapeDtypeStruct((T - ncc * Q, H * P), F32), jax.ShapeDtypeStruct((G, NC, RP, N), F32)]
    y_f, se_f, y_b, se_b = _pcall(
        body, name="ssd_fwd", grid=(G // GB, NC), in_specs=in_specs, out_specs=out_specs, out_shape=out_shape,
        scratch_shapes=[pltpu.VMEM((2, GB, RP, N), F32)], compiler_params=_cparams(2),
    )(*operands)
    return (y_f, y_b), (se_f, se_b)


def _ssd_bwd(xbc, b_off, c_off, dtr, a, s_enter, dy, P, ncc):
    T = xbc.shape[0]
    H = dtr[0].shape[0]
    N, Q = SSD_STATE, SSD_CHUNK
    NC = T // Q
    G = (c_off - b_off) // N
    R = H // G
    RP = R * P
    GB = SSD_GROUPS_PER_STEP if G % SSD_GROUPS_PER_STEP == 0 else 1
    chunk, lat_chunk, step = _ssd_maps(NC, ncc, True)
    n_in, n_out = 7, 5

    def body(*refs):
        s = pl.program_id(1)
        ds_ref = refs[-1]

        @pl.when(s == 0)
        def _():
            ds_ref[...] = jnp.zeros_like(ds_ref)

        for d in range(2):
            x_ref, b_ref, c_ref, dtr_ref, a_ref, se_ref, dy_ref = refs[n_in * d:n_in * (d + 1)]
            dx_ref, db_ref, dc_ref, ddtr_ref, da_ref = refs[2 * n_in + n_out * d:2 * n_in + n_out * (d + 1)]
            for gg in range(GB):
                cols, bcols = slice(gg * RP, (gg + 1) * RP), slice(gg * N, (gg + 1) * N)
                per_head = [dtr_ref[gg * R + r] for r in range(R)] + [a_ref[gg * R + r] for r in range(R)]

                def store(grads, dx_ref=dx_ref, db_ref=db_ref, ddtr_ref=ddtr_ref, da_ref=da_ref, d=d, gg=gg,
                          cols=cols, bcols=bcols):
                    dx_ref[:, cols] = grads[0]
                    db_ref[:, bcols] = grads[1]
                    ds_ref[d, gg] = grads[2]
                    for r in range(R):
                        ddtr_ref[gg * R + r] = grads[3 + r]
                        da_ref[gg, r] = jnp.broadcast_to(grads[3 + R + r], (SUBLANE, LANE))

                @pl.when(s < NC - ncc)
                def _(d=d, gg=gg, cols=cols, bcols=bcols, x_ref=x_ref, b_ref=b_ref, c_ref=c_ref, se_ref=se_ref,
                      dy_ref=dy_ref, dc_ref=dc_ref, per_head=per_head, store=store):
                    f = functools.partial(_ssd_group, reverse=d == 1, P=P)
                    _, vjp = jax.vjp(f, x_ref[:, cols], b_ref[:, bcols], c_ref[:, bcols], se_ref[gg], *per_head)
                    grads = vjp((dy_ref[:, cols], ds_ref[d, gg]))
                    dc_ref[:, bcols] = grads[2]
                    store(grads[:2] + grads[3:])

                @pl.when(s >= NC - ncc)
                def _(d=d, gg=gg, cols=cols, bcols=bcols, x_ref=x_ref, b_ref=b_ref, se_ref=se_ref, dc_ref=dc_ref,
                      per_head=per_head, store=store):
                    f = functools.partial(_ssd_group_state, reverse=d == 1, P=P)
                    _, vjp = jax.vjp(f, x_ref[:, cols], b_ref[:, bcols], se_ref[gg], *per_head)
                    dc_ref[:, bcols] = jnp.zeros((Q, N), F32)
                    store(vjp(ds_ref[d, gg]))

    in_specs, out_specs, out_shape, operands = [], [], [], []
    for d in range(2):
        in_specs += _ssd_specs(chunk, d, GB, R, Q, N, RP, b_off, c_off) + [
            pl.BlockSpec((GB, None, RP, N), lambda g, s: (g, step(s), 0, 0)),
            pl.BlockSpec((Q, GB * RP), functools.partial(lambda g, s, d: (lat_chunk(d, s), g), d=d)),
        ]
        operands += [xbc, xbc, xbc, dtr[d], a[d], s_enter[d], dy]
    for d in range(2):
        at_chunk = functools.partial(lambda g, s, d: (chunk(d, s), g), d=d)
        out_specs += [
            pl.BlockSpec((Q, GB * RP), at_chunk), pl.BlockSpec((Q, GB * N), at_chunk),
            pl.BlockSpec((Q, GB * N), at_chunk),
            pl.BlockSpec((GB * R, 1, Q), functools.partial(lambda g, s, d: (g, 0, chunk(d, s)), d=d)),
            pl.BlockSpec((GB, None, R, SUBLANE, LANE), lambda g, s: (g, s, 0, 0, 0)),
        ]
        out_shape += [
            jax.ShapeDtypeStruct((T, H * P), F32), jax.ShapeDtypeStruct((T, G * N), F32),
            jax.ShapeDtypeStruct((T, G * N), F32), jax.ShapeDtypeStruct((H, 1, T), F32),
            jax.ShapeDtypeStruct((G, NC, R, SUBLANE, LANE), F32),
        ]
    res = _pcall(
        body, name="ssd_bwd", grid=(G // GB, NC), in_specs=in_specs, out_specs=out_specs, out_shape=out_shape,
        scratch_shapes=[pltpu.VMEM((2, GB, RP, N), F32)], compiler_params=_cparams(2),
    )(*operands)
    return res[:n_out], res[n_out:]


def _allgather8(name, v):
    R, C = v.shape

    def body(x_ref, out_ref, send_sems, recv_sems, local_sem):
        x, y, c = lax.axis_index("x"), lax.axis_index("y"), lax.axis_index("c")
        me, sibling = (x, y, c), (x, y, 1 - c)
        chips = [(1 - x, y), (x, 1 - y), (1 - x, 1 - y)]

        def slot(px, py, pc):
            return out_ref.at[4 * px + 2 * py + pc]

        def copy(k, block, to, src=None):
            return pltpu.make_async_remote_copy(
                src_ref=slot(*block) if src is None else src, dst_ref=slot(*block),
                send_sem=send_sems.at[k], recv_sem=recv_sems.at[k], device_id=to, device_id_type=MESH)

        mine = pltpu.make_async_copy(x_ref, slot(*me), local_sem)
        mine.start()
        first = [copy(0, me, sibling, src=x_ref)]
        first += [copy(1 + j, me, (*chip, c), src=x_ref) for j, chip in enumerate(chips)]
        for cp in first:
            cp.start()
        passed = [copy(4 + j, (*chip, c), sibling) for j, chip in enumerate(chips)]
        for j, chip in enumerate(chips):
            copy(1 + j, (*chip, c), me).wait_recv()
            passed[j].start()
        copy(0, sibling, me).wait_recv()
        for j, chip in enumerate(chips):
            copy(4 + j, (*chip, 1 - c), me).wait_recv()
        for cp in first + passed:
            cp.wait_send()
        mine.wait()

    return _pcall(
        body, name=name, out_shape=jax.ShapeDtypeStruct((N_DEV, R, C), v.dtype),
        in_specs=[pl.BlockSpec(memory_space=pltpu.VMEM)], out_specs=pl.BlockSpec(memory_space=pltpu.VMEM),
        scratch_shapes=[pltpu.SemaphoreType.DMA((7,)), pltpu.SemaphoreType.DMA((7,)), pltpu.SemaphoreType.DMA],
        compiler_params=pltpu.CompilerParams(vmem_limit_bytes=VMEM_LIMIT_BYTES),
    )(v)


def _slot(ref, k, axis, size):
    if axis is None:
        return ref.at[k]
    align = LANE if size % LANE == 0 else 2 * SUBLANE
    assert size % align == 0
    return ref.at[(slice(None),) * axis + (pl.ds(pl.multiple_of(k * size, align), size),)]


def _exchange4_start(name, srcs, bcast, dep, axes=None, half=False):
    n = len(srcs)
    axes = list(axes) if axes is not None else [None] * n
    sizes = [None if ax is None else s.shape[ax] for s, ax in zip(srcs, axes)]

    def land_shape(s, ax):
        if not bcast:
            return s.shape
        if half:
            return (N_CHIPS,) + s.shape[1:]
        if ax is None:
            return (N_CHIPS,) + s.shape
        return s.shape[:ax] + (N_CHIPS * s.shape[ax],) + s.shape[ax + 1:]

    lands = [lax.empty(land_shape(s, ax), s.dtype) for s, ax in zip(srcs, axes)]

    def body(*refs):
        src, land = refs[:n], refs[n:2 * n]
        send_sems, recv_sems = refs[2 * n + 1], refs[2 * n + 2]
        token = refs[-1]
        x, y, c = lax.axis_index("x"), lax.axis_index("y"), lax.axis_index("c")
        me = 2 * x + y
        for a in range(n):
            for j, (px, py) in enumerate([(1 - x, y), (x, 1 - y), (1 - x, 1 - y)]):
                pltpu.make_async_remote_copy(
                    src_ref=(src[a].at[c] if half else src[a]) if bcast else src[a].at[2 * px + py],
                    dst_ref=_slot(land[a], me, axes[a], sizes[a]),
                    send_sem=send_sems.at[3 * a + j], recv_sem=recv_sems.at[3 * a + j], device_id=(px, py, c),
                    device_id_type=MESH).start()
        token[...] = jnp.zeros_like(token)

    hbm = pl.BlockSpec(memory_space=pltpu.HBM)
    sem = pl.BlockSpec(memory_space=pltpu.SEMAPHORE)
    outs = _pcall(
        body, name=name,
        out_shape=(pltpu.SemaphoreType.DMA((3 * n,)), pltpu.SemaphoreType.DMA((3 * n,)),
                   *[pltpu.HBM(s.shape, s.dtype) for s in srcs], *[pltpu.HBM(l.shape, l.dtype) for l in lands],
                   jax.ShapeDtypeStruct((SUBLANE, LANE), F32)),
        in_specs=[hbm] * (2 * n) + [pl.BlockSpec(memory_space=pl.ANY)],
        out_specs=(sem, sem, *[hbm] * (2 * n), pl.BlockSpec(memory_space=pltpu.VMEM)),
        input_output_aliases={k: 2 + k for k in range(2 * n)},
        compiler_params=pltpu.CompilerParams(has_side_effects=pltpu.SideEffectType.DATAFLOW_SIDE_EFFECTING),
    )(*[pltpu.with_memory_space_constraint(s, pltpu.HBM) for s in srcs],
      *[pltpu.with_memory_space_constraint(l, pltpu.HBM) for l in lands], dep)
    return (n, bcast, half, axes, sizes, outs[0], outs[1], outs[2:2 + n], outs[2 + n:2 + 2 * n]), outs[-1]


def _exchange4_wait(name, handle, after):
    n, bcast, half, axes, sizes, send_sems, recv_sems, src_thru, land_thru = handle

    def body(*refs):
        src, land = refs[:n], refs[n:2 * n]
        send_sems, recv_sems = refs[2 * n], refs[2 * n + 1]
        x, y, c = lax.axis_index("x"), lax.axis_index("y"), lax.axis_index("c")
        for a in range(n):
            for j, (px, py) in enumerate([(1 - x, y), (x, 1 - y), (1 - x, 1 - y)]):
                pk = 2 * px + py
                copy = pltpu.make_async_remote_copy(
                    src_ref=(src[a].at[c] if half else src[a]) if bcast else src[a].at[pk],
                    dst_ref=_slot(land[a], pk, axes[a], sizes[a]),
                    send_sem=send_sems.at[3 * a + j], recv_sem=recv_sems.at[3 * a + j], device_id=(px, py, c),
                    device_id_type=MESH)
                copy.wait_send()
                copy.wait_recv()

    hbm = pl.BlockSpec(memory_space=pltpu.HBM)
    sem = pl.BlockSpec(memory_space=pltpu.SEMAPHORE)
    outs = _pcall(
        body, name=name,
        out_shape=tuple(pltpu.HBM(t.shape, t.dtype) for t in (*src_thru, *land_thru)),
        in_specs=[hbm] * (2 * n) + [sem, sem, pl.BlockSpec(memory_space=pl.ANY)], out_specs=tuple([hbm] * (2 * n)),
        input_output_aliases={k: k for k in range(2 * n)},
        compiler_params=pltpu.CompilerParams(has_side_effects=pltpu.SideEffectType.DATAFLOW_SIDE_EFFECTING),
    )(*src_thru, *land_thru, send_sems, recv_sems, after)
    return list(outs[:n]), list(outs[n:])


def _tie(name, v, token):
    def body(v_ref, token_ref, o_ref):
        del v_ref, token_ref, o_ref

    any_spec = pl.BlockSpec(memory_space=pl.ANY)
    return _pcall(body, name=name, out_shape=jax.ShapeDtypeStruct(v.shape, v.dtype), in_specs=[any_spec, any_spec],
                  out_specs=any_spec, input_output_aliases={0: 0})(v, token)


def _fill_own(landed, own, me, bcast):
    blk = own if bcast else lax.dynamic_index_in_dim(own, me, 0, keepdims=False)
    return lax.dynamic_update_index_in_dim(landed, blk, me, 0)


def _swap_sibling(name, srcs, by_core=False):
    n = len(srcs)

    def body(*refs):
        src, out = refs[:n], refs[n:2 * n]
        send_sems, recv_sems = refs[2 * n:]
        x, y, c = lax.axis_index("x"), lax.axis_index("y"), lax.axis_index("c")
        copies = []
        for a in range(n):
            send = pltpu.make_async_remote_copy(
                src_ref=src[a], dst_ref=out[a].at[c] if by_core else out[a], send_sem=send_sems.at[a],
                recv_sem=recv_sems.at[a], device_id=(x, y, 1 - c), device_id_type=MESH)
            send.start()
            arrive = pltpu.make_async_remote_copy(
                src_ref=src[a], dst_ref=out[a].at[1 - c] if by_core else out[a], send_sem=send_sems.at[a],
                recv_sem=recv_sems.at[a], device_id=(x, y, 1 - c), device_id_type=MESH)
            copies.append((send, arrive))
        for send, arrive in copies:
            send.wait_send()
            arrive.wait_recv()

    any_spec = pl.BlockSpec(memory_space=pl.ANY)
    return _pcall(
        body, name=name,
        out_shape=[jax.ShapeDtypeStruct(((2,) + s.shape) if by_core else s.shape, s.dtype) for s in srcs],
        in_specs=[any_spec] * n, out_specs=[any_spec] * n,
        scratch_shapes=[pltpu.SemaphoreType.DMA((n,)), pltpu.SemaphoreType.DMA((n,))],
    )(*srcs)


def _mod_fwd(c16, mod_w, mod_b_shard):
    nl, D, S = mod_w.shape

    def body(c_ref, w_ref, b_ref, o_ref):
        s = _silu(c_ref[...]).astype(BF16)
        o_ref[...] = jnp.dot(s, w_ref[...].astype(BF16), preferred_element_type=F32) + b_ref[...]

    return _pcall(
        body, name="mod_fwd", grid=(nl,),
        in_specs=[pl.BlockSpec((16, D), lambda l: (0, 0)), pl.BlockSpec((None, D, S), lambda l: (l, 0, 0)),
                  pl.BlockSpec((None, 1, S), lambda l: (l, 0, 0))],
        out_specs=pl.BlockSpec((None, 16, S), lambda l: (l, 0, 0)),
        out_shape=jax.ShapeDtypeStruct((nl, 16, S), F32), compiler_params=_cparams(1),
    )(c16, mod_w, mod_b_shard)


def _mod_w_update(s16t, dm16, w, m, v):
    nl, D, S = w.shape
    tm = _row_tile(D, 256)

    def body(s_ref, dm_ref, w_ref, m_ref, v_ref, g_ref, dl_ref, nm_ref, nv_ref):
        g = jnp.dot(s_ref[...], dm_ref[...], preferred_element_type=F32, precision=HIGHEST)
        g, dl, nm, nv = _f_adamw(w_ref[...], m_ref[...], v_ref[...], g, jnp.zeros_like(g))
        g_ref[...] = g
        dl_ref[...] = dl
        nm_ref[...] = nm
        nv_ref[...] = nv

    big = pl.BlockSpec((None, tm, S), lambda l, i: (l, i, 0))
    return _pcall(
        body, name="mod_w_update", grid=(nl, D // tm),
        in_specs=[pl.BlockSpec((tm, 16), lambda l, i: (i, 0)), pl.BlockSpec((None, 16, S), lambda l, i: (l, 0, 0)),
                  big, big, big],
        out_specs=[big] * 4, out_shape=[jax.ShapeDtypeStruct(w.shape, F32)] * 4, compiler_params=_cparams(2),
    )(s16t, dm16, w, m, v)


def _size(shape):
    n = 1
    for d in shape:
        n *= d
    return n


def _pack(arrs):
    pieces = []
    for a in arrs:
        flat = a.reshape(-1).astype(F32)
        pieces.append(jnp.pad(flat, (0, _round_up(flat.shape[0], LANE) - flat.shape[0])).reshape(-1, LANE))
    buf = jnp.concatenate(pieces, axis=0)
    return jnp.pad(buf, ((0, _round_up(buf.shape[0], SUBLANE) - buf.shape[0]), (0, 0)))


def _unpack(buf, shapes):
    lead = buf.shape[:-2]
    out, row = [], 0
    for s in shapes:
        n = _size(s)
        rows = _cdiv(n, LANE)
        piece = buf[..., row:row + rows, :].reshape(lead + (rows * LANE,))
        out.append(piece[..., :n].reshape(lead + tuple(s)))
        row += rows
    return out


def _adamw_many(name, ws, ms, vs, gs):
    n = len(ws)

    def body(*refs):
        for k in range(n):
            res = _f_adamw(refs[k][...], refs[n + k][...], refs[2 * n + k][...], refs[3 * n + k][...], 0.0)
            for j in range(4):
                refs[(4 + j) * n + k][...] = res[j]

    vmem = pl.BlockSpec(memory_space=pltpu.VMEM)
    res = _pcall(body, name=name, out_shape=[jax.ShapeDtypeStruct(w.shape, F32) for _ in range(4) for w in ws],
                 in_specs=[vmem] * (4 * n), out_specs=[vmem] * (4 * n))(*ws, *ms, *vs, *gs)
    return [tuple(res[j * n + k] for j in range(4)) for k in range(n)]


SHARD_AXIS = {
    "mod_w": 2, "ssd_w_in": 2, "ssd_conv_w": 2, "ssd_w_out": 1, "conf_w_pw1": 2, "conf_b_pw1": 1, "conf_w_dw": 2,
    "conf_b_dw": 1, "conf_ln_w": 1, "conf_ln_b": 1, "conf_w_pw2": 1, "conf_b_pw2": 1, "ffn_w_up": 2,
    "ffn_conv_w": 3, "ffn_w_down": 1,
}
BIG = ("ssd_w_in", "ssd_w_out", "conf_w_pw1", "conf_w_pw2", "ffn_w_up", "ffn_w_down")
WEIGHTS = ("c_ctx", "mod_w", "mod_b", "norm1_w", "norm2_w", "ssd_w_in", "ssd_conv_w", "ssd_conv_b", "ssd_dt_bias",
           "ssd_a_log", "ssd_d", "ssd_norm_w", "ssd_w_out", "conf_w_pw1", "conf_b_pw1", "conf_w_dw", "conf_b_dw",
           "conf_ln_w", "conf_ln_b", "conf_w_pw2", "conf_b_pw2", "ffn_w_up", "ffn_conv_w", "ffn_conv_b",
           "ffn_w_down", "final_norm_w")
SMALL = tuple(n for n in WEIGHTS if n not in BIG and n != "mod_w")
SMALL_SHARDED = tuple(n for n in SMALL if n in SHARD_AXIS)


def _unshard(stacked, axis):
    return jnp.concatenate([stacked[k] for k in range(N_CHIPS)], axis=axis)


def _to_blocks(full, axis):
    return jnp.stack(jnp.split(full, N_CHIPS, axis=axis))


def _par(v):
    v = v.reshape(-1, v.shape[-1])
    return v[:, None, :]


def kernel(x, c, ctx, c_ctx, mod_w, mod_b, norm1_w, norm2_w, ssd_w_in, ssd_conv_w, ssd_conv_b, ssd_dt_bias, ssd_a_log, ssd_d, ssd_norm_w, ssd_w_out, conf_w_pw1, conf_b_pw1, conf_w_dw, conf_b_dw, conf_ln_w, conf_ln_b, conf_w_pw2, conf_b_pw2, ffn_w_up, ffn_conv_w, ffn_conv_b, ffn_w_down, final_norm_w, loss_target, m_c_ctx, m_mod_w, m_mod_b, m_norm1_w, m_norm2_w, m_ssd_w_in, m_ssd_conv_w, m_ssd_conv_b, m_ssd_dt_bias, m_ssd_a_log, m_ssd_d, m_ssd_norm_w, m_ssd_w_out, m_conf_w_pw1, m_conf_b_pw1, m_conf_w_dw, m_conf_b_dw, m_conf_ln_w, m_conf_ln_b, m_conf_w_pw2, m_conf_b_pw2, m_ffn_w_up, m_ffn_conv_w, m_ffn_conv_b, m_ffn_w_down, m_final_norm_w, v_c_ctx, v_mod_w, v_mod_b, v_norm1_w, v_norm2_w, v_ssd_w_in, v_ssd_conv_w, v_ssd_conv_b, v_ssd_dt_bias, v_ssd_a_log, v_ssd_d, v_ssd_norm_w, v_ssd_w_out, v_conf_w_pw1, v_conf_b_pw1, v_conf_w_dw, v_conf_b_dw, v_conf_ln_w, v_conf_ln_b, v_conf_w_pw2, v_conf_b_pw2, v_ffn_w_up, v_ffn_conv_w, v_ffn_conv_b, v_ffn_w_down, v_final_norm_w):
    given = dict(locals())
    W = {n: given[n] for n in WEIGHTS}
    Mo = {n: given["m_" + n] for n in WEIGHTS}
    Vo = {n: given["v_" + n] for n in WEIGHTS}

    ax, ay, ac = lax.axis_index("x"), lax.axis_index("y"), lax.axis_index("c")
    chip = 2 * ax + ay
    dev = 4 * ax + 2 * ay + ac

    D = x.shape[-1]
    L, Lc = x.shape[1], ctx.shape[1]
    T0 = L + Lc
    H = ssd_a_log.shape[-1]
    DI = ssd_norm_w.shape[-1]
    P = DI // H
    CD = ssd_conv_b.shape[-1]
    N = SSD_STATE
    G = (CD - DI) // (2 * N)
    FH = ffn_conv_b.shape[-1]
    KS = ssd_conv_w.shape[1]
    KC = conf_w_dw.shape[1]
    ncc = Lc // SSD_CHUNK

    shard_b = {n: W[n].astype(BF16) for n in BIG}

    small_shard_shapes = [W[n].shape for n in SMALL_SHARDED]
    f1 = _allgather8("gather_small", _pack([c] + [W[n] for n in SMALL_SHARDED]))
    parts = _unpack(f1, [c.shape] + small_shard_shapes)
    Wf = dict(W)
    for n, p in zip(SMALL_SHARDED, parts[1:]):
        Wf[n] = _unshard(p[::2], SHARD_AXIS[n])
    c16 = jnp.concatenate([parts[0].reshape(N_DEV, D), c_ctx[None, :], jnp.zeros((16 - N_DEV - 1, D), F32)], axis=0)

    S_mod = mod_w.shape[-1]
    mod_b_shard = lax.dynamic_slice_in_dim(mod_b, chip * S_mod, S_mod, axis=1)[:, None, :]
    mod_part = _mod_fwd(c16, mod_w, mod_b_shard)
    f2 = _allgather8("gather_mod", mod_part.reshape(2 * 16, S_mod))
    mods = jnp.concatenate([f2[2 * k].reshape(2, 16, S_mod) for k in range(N_CHIPS)], axis=-1)
    my = lax.dynamic_slice_in_dim(mods, dev, 1, axis=1)[:, 0]
    sh1, sc1, g1, sh2, sc2, g2 = [[my[l, k * D:(k + 1) * D] for l in range(2)] for k in range(6)]
    csh1, csc1 = mods[0, N_DEV, 0:D], mods[0, N_DEV, D:2 * D]

    in_halves = shard_b["ssd_w_in"].reshape(2, D // 2, ssd_w_in.shape[-1])
    gather_a, token = _exchange4_start("gather_w_in_start", [in_halves], True, mods, half=True)
    csc1 = _tie("tie_gather_w_in", csc1, token)

    def full_weight(n, own, landed):
        if landed.ndim == own.ndim:
            ax = SHARD_AXIS[n]
            return lax.dynamic_update_slice_in_dim(landed, own, chip * own.shape[ax], ax)
        return _unshard(_fill_own(landed, own, chip, True), SHARD_AXIS[n])

    xl = x[0]
    rows0 = (ctx[0], xl)
    n1w0, n1w1 = _par(norm1_w[0]), _par(norm1_w[1])
    sc_seg = jnp.stack([csc1, sc1[0]])[:, None, :]
    sh_seg = jnp.stack([csh1, sh1[0]])[:, None, :]

    a0 = _rw_fwd("l0_modnorm1", _f_modnorm, [], [n1w0, sc_seg, sh_seg], [D], T=T0, seg_rows=(Lc,), head=rows0,
                 out_dtypes=[BF16])
    rest = [n for n in BIG if n != "ssd_w_in"]
    for n in rest:
        a0 = _tie("tie_cast_" + n, a0, shard_b[n])
    (own_in,), (landed_in,) = _exchange4_wait("gather_w_in_wait", gather_a, a0)
    mine = _fill_own(landed_in, lax.dynamic_index_in_dim(own_in, ac, 0, keepdims=False), chip, True)
    (halves,) = _swap_sibling("swap_w_in", [mine], by_core=True)
    halves = lax.dynamic_update_index_in_dim(halves, mine, ac, 0)
    w_in = jnp.concatenate([halves[:, k].reshape(D, -1) for k in range(N_CHIPS)], axis=1)
    landed_in = halves
    def start_gather(tag, names, dep):
        handle, tok = _exchange4_start("gather_" + tag + "_start", [shard_b[n] for n in names], True, dep,
                                       axes=[1 if SHARD_AXIS[n] == 1 else None for n in names])
        return (names, handle), tok

    def finish_gather(tag, group, after):
        names, handle = group
        return {n: full_weight(n, own, g)
                for n, own, g in zip(names, *_exchange4_wait("gather_" + tag + "_wait", handle, after))}

    gather_b, token = start_gather("mix", ["ssd_w_out", "conf_w_pw1", "conf_w_pw2"], landed_in)
    gather_c, token = start_gather("ffn", ["ffn_w_up", "ffn_w_down"], token)
    a0 = _tie("tie_gather_rest", a0, token)
    proj = _mm(a0, w_in, name="l0_w_in")
    seg_taps = [(k - KS // 2, ("seg", Lc)) for k in range(KS)]
    xbc_pre, xbc = _conv_fwd("l0_conv", proj, DI, CD, Wf["ssd_conv_w"][0], ssd_conv_b, seg_taps, act=True)
    dt_raw = proj[:, DI + CD:]
    dt_bias = _par(ssd_dt_bias.reshape(1, 2 * H))
    dt = _rw_fwd("l0_softplus", _f_softplus, [dt_raw], [dt_bias], [2 * H])
    dt_t = dt.T
    dtr = (dt_t[:H, None, :], dt_t[H:, None, :])
    a_all = -jnp.exp(ssd_a_log.reshape(2, H, 1, 1))
    a_neg = (a_all[0], a_all[1])
    (y_f, y_b), s_enter = _ssd_fwd(xbc, DI, DI + G * N, dtr, a_neg, P, ncc)
    gate_rows = [y_f, y_b, (xbc, 0, DI, Lc), (proj, 0, DI, Lc)]
    d_rep = _par(jnp.repeat(ssd_d[0], P))
    ssd_nw = _par(ssd_norm_w[0])
    yn = _rw_fwd("l0_ssd_gate", _f_ssd_gate, gate_rows, [d_rep, ssd_nw], [DI], T=L, out_dtypes=[BF16])
    Wb = finish_gather("mix", gather_b, yn)
    w_out, w_pw1, w_pw2 = Wb["ssd_w_out"][0], Wb["conf_w_pw1"][0], Wb["conf_w_pw2"][0]
    mix0 = _mm(yn, w_out, name="l0_w_out")
    g1_0, g2_0, g1_1, g2_1 = _par(g1[0]), _par(g2[0]), _par(g1[1]), _par(g2[1])
    h1 = _rw_fwd("l0_res1", _f_gate_res, [xl, mix0], [g1_0], [D])
    Wb = finish_gather("ffn", gather_c, h1)
    w_up, w_dn = Wb["ffn_w_up"], Wb["ffn_w_down"]

    grid_taps = [((i - 1) * GRID_W + (j - 1), (None if j == 1 else ("col", j - 1))) for i in range(3) for j in range(3)]

    def ffn_fwd(l, h, tag):
        a = _rw_fwd(tag + "_modnorm2", _f_modnorm, [h], [_par(norm2_w[l]), _par(sc2[l]), _par(sh2[l])], [D],
                    out_dtypes=[BF16])
        hh = _mm(a, w_up[l], name=tag + "_w_up")
        gc = _conv_fwd(tag + "_ffn_conv", hh, FH, FH, Wf["ffn_conv_w"][l].reshape(9, FH), ffn_conv_b[l][None, :],
                       grid_taps)
        act = _rw_fwd(tag + "_act", _f_ffn_act, [(hh, 0, FH), gc], [], [FH], col_tile=_tile(FH, 1536),
                      out_dtypes=[BF16])
        dn = _mm(act, w_dn[l], name=tag + "_w_down")
        return a, hh, gc, act, dn

    a1, hh0, gc0, act0, dn0 = ffn_fwd(0, h1, "l0")
    h2 = _rw_fwd("l0_res2", _f_gate_res, [h1, dn0], [g2_0], [D])

    a2 = _rw_fwd("l1_modnorm1", _f_modnorm, [h2], [n1w1, _par(sc1[1]), _par(sh1[1])], [D], out_dtypes=[BF16])
    pw = _mm(a2, w_pw1, name="l1_pw1")
    b_pw1 = Wf["conf_b_pw1"][0]
    glu = _rw_fwd("l1_glu", _f_glu, [(pw, 0, D), (pw, D, D)], [_par(b_pw1[:D]), _par(b_pw1[D:])], [D])
    conf_taps = [(k - KC // 2, None) for k in range(KC)]
    cv = _conv_fwd("l1_conv", glu, 0, D, Wf["conf_w_dw"][0], Wf["conf_b_dw"], conf_taps)
    ln_w, ln_b = _par(Wf["conf_ln_w"][0]), _par(Wf["conf_ln_b"][0])
    ls = _rw_fwd("l1_ln_silu", _f_ln_silu, [cv], [ln_w, ln_b], [D], out_dtypes=[BF16])
    p2 = _mm(ls, w_pw2, name="l1_pw2")
    b_pw2 = _par(Wf["conf_b_pw2"][0])
    h3 = _rw_fwd("l1_res1", _f_gate_res_bias, [h2, p2], [g1_1, b_pw2], [D])
    a3, hh1, gc1, act1, dn1 = ffn_fwd(1, h3, "l1")
    h4 = _rw_fwd("l1_res2", _f_gate_res, [h3, dn1], [g2_1], [D])

    fnw = final_norm_w[None, :]
    tgt = loss_target[0]
    loss_local = _loss_fwd(h4, tgt, fnw)[0, 0]
    loss = lax.psum(loss_local, ("x", "y", "c"))

    G_full = {}
    reduces = {}

    def start_reduce(tag, items, dep):
        def blocks_of(g, ax):
            if g.ndim == 3:
                return g
            return g.reshape(N_CHIPS, g.shape[0] // N_CHIPS, g.shape[1]) if ax == 0 else _to_blocks(g, ax)

        blocks = [blocks_of(g, ax).astype(BF16) for _, g, ax in items]
        handle, tok = _exchange4_start("reduce_" + tag + "_start", blocks, False, dep)
        reduces[tag] = ([n for n, _, _ in items], handle)
        return tok
    ones = jnp.ones((L, 1), F32)
    (dh4,), (dfnw,) = _rw_bwd("loss_bwd", _f_loss_rows, [h4, tgt], [_par(final_norm_w)], [ones],
                              row_grad=[True, False], par_grad=[True])
    G_full["final_norm_w"] = dfnw.reshape(D)

    def ffn_bwd(l, h, saved, g2_l, dh_out, tag):
        a, hh, gc, act, dn = saved
        (ddn,), (dg2,) = _rw_bwd(tag + "_res2_bwd", _f_gate, [dn], [g2_l], [dh_out],
                                 row_grad=[True], par_grad=[True], row_dtypes=[BF16])
        dact = _mm(ddn, w_dn[l], tb=True, name=tag + "_w_down_dx")
        dwdn = _mm(act, ddn, ta=True, name=tag + "_w_down_dw", out_dtype=BF16)
        (dval, dgc), _ = _rw_bwd(tag + "_act_bwd", _f_ffn_act, [(hh, 0, FH), gc], [], [dact],
                                 row_grad=[True, True], par_grad=[], col_tile=_tile(FH, 1536), row_dtypes=[BF16, F32])
        dgin, dcw, dcb = _conv_bwd(tag + "_ffn_conv_bwd", hh, FH, FH, Wf["ffn_conv_w"][l].reshape(9, FH), dgc,
                                   grid_taps, du_dtype=BF16)
        dhh = jnp.concatenate([dval, dgin], axis=1)
        da = _mm(dhh, w_up[l], tb=True, name=tag + "_w_up_dx")
        dwup = _mm(a, dhh, ta=True, name=tag + "_w_up_dw", out_dtype=BF16, col_blocks=N_CHIPS)
        (dh,), (dn2w, dsc2, dsh2) = _rw_bwd(
            tag + "_modnorm2_bwd", _f_modnorm, [h], [_par(norm2_w[l]), _par(sc2[l]), _par(sh2[l])], [da],
            row_grad=[True], par_grad=[True, True, True], add=dh_out)
        return dh, dict(w_down=dwdn, w_up=dwup, conv_w=dcw.reshape(3, 3, FH), conv_b=dcb.reshape(FH),
                        n2w=dn2w.reshape(D), sc2=dsc2.reshape(D), sh2=dsh2.reshape(D), g2=dg2.reshape(D))

    dh3, gf1 = ffn_bwd(1, h3, (a3, hh1, gc1, act1, dn1), g2_1, dh4, "l1")
    (dp2,), (dg1_1, db_pw2) = _rw_bwd("l1_res1_bwd", _f_gate_bias, [p2], [g1_1, b_pw2], [dh3],
                                      row_grad=[True], par_grad=[True, True], row_dtypes=[BF16])
    dls = _mm(dp2, w_pw2, tb=True, name="l1_pw2_dx")
    dw_pw2 = _mm(ls, dp2, ta=True, name="l1_pw2_dw", out_dtype=BF16)
    (dcv,), (dln_w, dln_b) = _rw_bwd("l1_ln_silu_bwd", _f_ln_silu, [cv], [ln_w, ln_b], [dls],
                                     row_grad=[True], par_grad=[True, True])
    dglu, dw_dw, db_dw = _conv_bwd("l1_conv_bwd", glu, 0, D, Wf["conf_w_dw"][0], dcv, conf_taps)
    (dpa, dpg), (dba, dbg) = _rw_bwd("l1_glu_bwd", _f_glu, [(pw, 0, D), (pw, D, D)],
                                     [_par(b_pw1[:D]), _par(b_pw1[D:])], [dglu],
                                     row_grad=[True, True], par_grad=[True, True], row_dtypes=[BF16, BF16])
    dpw = jnp.concatenate([dpa, dpg], axis=1)
    da2 = _mm(dpw, w_pw1, tb=True, name="l1_pw1_dx")
    dw_pw1 = _mm(a2, dpw, ta=True, name="l1_pw1_dw", out_dtype=BF16, col_blocks=N_CHIPS)
    (dh2,), (dn1w1, dsc1_1, dsh1_1) = _rw_bwd(
        "l1_modnorm1_bwd", _f_modnorm, [h2], [n1w1, _par(sc1[1]), _par(sh1[1])], [da2],
        row_grad=[True], par_grad=[True, True, True], add=dh3)
    G_full["conf_b_pw2"] = db_pw2.reshape(1, D)
    G_full["conf_ln_w"], G_full["conf_ln_b"] = dln_w.reshape(1, D), dln_b.reshape(1, D)
    G_full["conf_w_dw"], G_full["conf_b_dw"] = dw_dw[None], db_dw.reshape(1, D)
    G_full["conf_b_pw1"] = jnp.concatenate([dba.reshape(1, D), dbg.reshape(1, D)], axis=1)

    token = start_reduce("l1", [("conf_w_pw2", dw_pw2, 0), ("conf_w_pw1", dw_pw1, 1), ("ffn_w_up1", gf1["w_up"], 1),
                                ("ffn_w_down1", gf1["w_down"], 0)], dw_pw2)
    dh2 = _tie("tie_reduce_l1", dh2, token)
    dh1, gf0 = ffn_bwd(0, h1, (a1, hh0, gc0, act0, dn0), g2_0, dh2, "l0")
    G_full["ffn_conv_w"] = jnp.stack([gf0["conv_w"], gf1["conv_w"]])
    G_full["ffn_conv_b"] = jnp.stack([gf0["conv_b"], gf1["conv_b"]])

    (dmix,), (dg1_0,) = _rw_bwd("l0_res1_bwd", _f_gate, [mix0], [g1_0], [dh1],
                                row_grad=[True], par_grad=[True], row_dtypes=[BF16])
    dyn = _mm(dmix, w_out, tb=True, name="l0_w_out_dx")
    dw_out = _mm(yn, dmix, ta=True, name="l0_w_out_dw", out_dtype=BF16)
    token = start_reduce("l0", [("ffn_w_up0", gf0["w_up"], 1), ("ffn_w_down0", gf0["w_down"], 0),
                                ("ssd_w_out", dw_out, 0)], dw_out)
    dyn = _tie("tie_reduce_l0", dyn, token)
    (dy_lat, dxs_gate, dz_lat), (dd_rep, dssd_nw) = _rw_bwd(
        "l0_ssd_gate_bwd", _f_ssd_gate, gate_rows, [d_rep, ssd_nw], [dyn],
        row_grad=[True, False, True, True], par_grad=[True, True], T=L, row_dtypes=[F32, F32, BF16])
    g_f, g_b = _ssd_bwd(xbc, DI, DI + G * N, dtr, a_neg, s_enter, dy_lat, P, ncc)
    silu_bwd = functools.partial(_rw_bwd, f=_silu, pars=[], row_grad=[True], par_grad=[], T=T0)
    (dxs_pre,), _ = silu_bwd("l0_silu_bwd_x", rows=[(xbc_pre, 0, DI)], cot_fn=lambda p, q, r: p + q + r,
                             cots=[g_f[0], g_b[0], (dxs_gate, 0, DI, -Lc)],
                             col_tile=_tile(DI, 1024))
    (db_pre,), _ = silu_bwd("l0_silu_bwd_b", rows=[(xbc_pre, DI, G * N)], cot_fn=lambda p, q: p + q,
                            cots=[g_f[1], g_b[1]], col_tile=_tile(G * N, 1024))
    (dc_pre,), _ = silu_bwd("l0_silu_bwd_c", rows=[(xbc_pre, DI + G * N, G * N)], cot_fn=lambda p, q: p + q,
                            cots=[g_f[2], g_b[2]], col_tile=_tile(G * N, 1024))
    conv_w0 = Wf["ssd_conv_w"][0]
    pieces = []
    for tag, off, width, g_pre in (("x", 0, DI, dxs_pre), ("b", DI, G * N, db_pre), ("c", DI + G * N, G * N, dc_pre)):
        pieces.append(_conv_bwd("l0_conv_bwd_" + tag, proj, DI + off, width, conv_w0[:, off:off + width], g_pre,
                                seg_taps, du_dtype=BF16))
    dconv_in = [p[0] for p in pieces]
    dcw0 = jnp.concatenate([p[1] for p in pieces], axis=1)
    dcb0 = jnp.concatenate([p[2] for p in pieces], axis=1)
    ddt = jnp.concatenate([g_f[3][:, 0, :].T, g_b[3][:, 0, :].T], axis=1)
    (ddt_raw,), (ddt_bias,) = _rw_bwd("l0_softplus_bwd", _f_softplus, [dt_raw], [dt_bias], [ddt],
                                      row_grad=[True], par_grad=[True], row_dtypes=[BF16])
    dproj = jnp.concatenate([jnp.pad(dz_lat, ((Lc, 0), (0, 0))), *dconv_in, ddt_raw], axis=1)
    da0 = _mm(dproj, w_in, tb=True, name="l0_w_in_dx")
    dw_in = _mm(a0, dproj, ta=True, name="l0_w_in_dw", out_dtype=BF16)
    token = start_reduce("in", [("ssd_w_in", dw_in, 1)], dw_in)
    da0 = _tie("tie_reduce_in", da0, token)
    (dhcat,), (dn1w0, dsc_seg, dsh_seg) = _rw_bwd(
        "l0_modnorm1_bwd", _f_modnorm, [], [n1w0, sc_seg, sh_seg], [da0], T=T0, head=rows0,
        row_grad=[True], par_grad=[True, True, True], seg_rows=(Lc,), add=(dh1, 0, D, -Lc))
    grad_x = dhcat[Lc:][None]

    da_heads = jnp.stack([g[4][..., 0, 0].sum(axis=1).reshape(H) for g in (g_f, g_b)])[None]
    G_full["ssd_a_log"] = da_heads * (-jnp.exp(ssd_a_log))
    G_full["ssd_dt_bias"] = ddt_bias.reshape(1, 2, H)
    G_full["ssd_d"] = dd_rep.reshape(H, P).sum(axis=1)[None]
    G_full["ssd_norm_w"] = dssd_nw.reshape(1, DI)
    G_full["ssd_conv_w"], G_full["ssd_conv_b"] = dcw0[None], dcb0.reshape(1, CD)
    G_full["norm1_w"] = jnp.stack([dn1w0.reshape(D), dn1w1.reshape(D)])
    G_full["norm2_w"] = jnp.stack([gf0["n2w"], gf1["n2w"]])

    zD = jnp.zeros((D,), F32)
    dm_own = jnp.stack([
        jnp.concatenate([dsh_seg[1, 0], dsc_seg[1, 0], dg1_0.reshape(D), gf0["sh2"], gf0["sc2"], gf0["g2"]]),
        jnp.concatenate([dsh1_1.reshape(D), dsc1_1.reshape(D), dg1_1.reshape(D), gf1["sh2"], gf1["sc2"], gf1["g2"]]),
    ])
    dmc_own = jnp.concatenate([dsh_seg[0, 0], dsc_seg[0, 0], zD, zD, zD, zD])

    out = {}

    def finish_reduce(tags, after, swap_name):
        partial = {}
        for tag in tags:
            names, handle = reduces[tag]
            blocks, landed = _exchange4_wait("reduce_" + tag + "_wait", handle, after)
            for n, blk, own in zip(names, landed, blocks):
                r = _fill_own(blk, own, chip, False)
                partial[n] = _sum_leading("sum4_" + n, r.reshape(N_CHIPS, -1, r.shape[-1]),
                                          (0, 1, 2, 3), out_dtype=BF16).reshape(r.shape[1:])
        for n in ("ffn_w_up", "ffn_w_down"):
            if n + "0" in partial:
                partial[n] = jnp.stack([partial.pop(n + "0"), partial.pop(n + "1")])
        names = [n for n in BIG if n in partial]
        mine = [partial[n].reshape(W[n].shape) for n in names]
        for n, own, sib in zip(names, mine, _swap_sibling(swap_name, mine)):
            out[n] = _adamw("adamw_" + n, W[n], Mo[n], Vo[n], own, sib)
        return names

    early = finish_reduce(["l1", "l0"], dhcat, "swap_grads_early")

    small_sum_names = [n for n in SMALL if n not in ("c_ctx", "mod_b")]
    sum_part = [G_full[n] for n in small_sum_names] + [dmc_own]
    packed = _tie("tie_small_grads", _pack(sum_part + [dm_own]), out[early[-1]][1])
    gat = _allgather8("gather_small_grads", packed)
    total = _sum_leading("sum_small_grads", gat, tuple(range(N_DEV)))
    summed = _unpack(total, [a.shape for a in sum_part])
    Gs = dict(zip(small_sum_names, summed[:-1]))
    dmc_tot = summed[-1]
    dm_all = _unpack(gat, [a.shape for a in sum_part] + [dm_own.shape])[-1].transpose(1, 0, 2)
    dm16 = jnp.concatenate([dm_all, jnp.stack([dmc_tot, jnp.zeros_like(dmc_tot)])[:, None, :],
                            jnp.zeros((2, 16 - N_DEV - 1, 6 * D), F32)], axis=1)
    Gs["mod_b"] = _sum_leading("sum_mod_b", dm16.transpose(1, 0, 2).reshape(16, 2 * 6 * D // LANE, LANE),
                               tuple(range(N_DEV + 1))).reshape(2, 6 * D)

    dm16_shard = lax.dynamic_slice_in_dim(dm16, chip * S_mod, S_mod, axis=2)
    ds16 = _mm(dm16_shard[0], mod_w[0], tb=True, precision=HIGHEST, name="c_ctx_dx")
    sig = jax.nn.sigmoid(c_ctx)
    dcc_part = ds16[N_DEV] * (sig * (1.0 + c_ctx * (1.0 - sig)))
    gat_cc = _allgather8("gather_c_ctx_grad", _pack([dcc_part]))
    Gs["c_ctx"] = _sum_leading("sum_c_ctx_grad", gat_cc, (0, 2, 4, 6)).reshape(-1)[:D]

    s16t = _silu(c16).T
    out["mod_w"] = _mod_w_update(s16t, dm16_shard, mod_w, m_mod_w, v_mod_w)
    finish_reduce(["in"], out["mod_w"][0], "swap_grads_late")

    def own(n, full):
        if n in SHARD_AXIS:
            size = W[n].shape[SHARD_AXIS[n]]
            return lax.dynamic_slice_in_dim(full, chip * size, size, axis=SHARD_AXIS[n])
        return full

    def two_d(a):
        return a.reshape(1, -1) if a.ndim == 1 else a

    g_small = [own(n, Gs[n].reshape(Wf[n].shape)) for n in SMALL]
    res = _adamw_many("adamw_small", [two_d(W[n]) for n in SMALL], [two_d(Mo[n]) for n in SMALL],
                      [two_d(Vo[n]) for n in SMALL], [two_d(g) for g in g_small])
    for n, r in zip(SMALL, res):
        out[n] = tuple(t.reshape(W[n].shape) for t in r)

    grads = [out[n][0] for n in WEIGHTS]
    deltas = [out[n][1] for n in WEIGHTS]
    new_m = [out[n][2] for n in WEIGHTS]
    new_v = [out[n][3] for n in WEIGHTS]
    return (loss, grad_x, *grads, *deltas, *new_m, *new_v)
```

```python
import functools

import jax
import jax.numpy as jnp
from jax import lax
from jax.experimental import pallas as pl
from jax.experimental.pallas import tpu as pltpu

F32 = jnp.float32
BF16 = jnp.bfloat16
MESH = pl.DeviceIdType.MESH
HIGHEST = lax.Precision.HIGHEST

VMEM_LIMIT_BYTES = 48 * 1024 * 1024
LANE = 128
SUBLANE = 8

SSD_STATE = 128
SSD_CHUNK = 128
GRID_W = 64
EPS = 1e-6
N_CHIPS = 4
N_DEV = 8

ADAM_LR = 0.001
ADAM_B1 = 0.9
ADAM_B2 = 0.999
ADAM_EPS = 1e-08
ADAM_WD = 0.01
ADAM_STEP = 10


def _pcall(body, **kw):
    return pl.pallas_call(body, **kw)


def _cparams(n_grid):
    return pltpu.CompilerParams(dimension_semantics=("arbitrary",) * n_grid, vmem_limit_bytes=VMEM_LIMIT_BYTES)


def _cdiv(a, b):
    return -(-a // b)


def _round_up(a, b):
    return _cdiv(a, b) * b


def _tile(n, cap):
    if n <= cap:
        return n
    best = None
    for t in range(LANE, cap + 1, LANE):
        if n % t == 0:
            best = t
    if best is None:
        npad = _round_up(n, LANE)
        for t in range(LANE, cap + 1, LANE):
            if npad % t == 0:
                best = t
    return best


def _row_tile(n, cap, also=()):
    best = None
    for step in (2 * SUBLANE, SUBLANE):
        for t in range(step, min(cap, n) + 1, step):
            if n % t == 0 and all(a % t == 0 for a in also):
                best = t
        if best is not None:
            break
    assert best is not None, (n, cap, also)
    return best


def _silu(v):
    return v * jax.nn.sigmoid(v)


def _mm(a, b, *, name, ta=False, tb=False, precision=None, cap=1024, out_dtype=F32, col_blocks=None):
    M, K = (a.shape[1], a.shape[0]) if ta else a.shape
    N = b.shape[0] if tb else b.shape[1]
    assert K == (b.shape[1] if tb else b.shape[0]), (a.shape, b.shape, ta, tb)
    tm, tk = _tile(M, cap), _tile(K, cap + cap // 2)
    tn = _tile(N if col_blocks is None else N // col_blocks, cap + cap // 2)
    nm, nn, nk = _cdiv(M, tm), _cdiv(N, tn), _cdiv(K, tk)
    k_tail = K % tk
    exact = precision is not None

    def body(a_ref, b_ref, o_ref, acc_ref):
        k = pl.program_id(2)

        @pl.when(k == 0)
        def _():
            acc_ref[...] = jnp.zeros_like(acc_ref)

        av = a_ref[...]
        bv = b_ref[...]
        if k_tail:
            lim = K - k * tk
            ka = lax.broadcasted_iota(jnp.int32, av.shape, 0 if ta else 1)
            kb = lax.broadcasted_iota(jnp.int32, bv.shape, 1 if tb else 0)
            av = jnp.where(ka < lim, av, jnp.zeros_like(av))
            bv = jnp.where(kb < lim, bv, jnp.zeros_like(bv))
        if exact:
            av = av.astype(F32)
            bv = bv.astype(F32)
        else:
            av = av.astype(BF16)
            bv = bv.astype(BF16)
        dn = (((0 if ta else 1,), (1 if tb else 0,)), ((), ()))
        acc_ref[...] += lax.dot_general(av, bv, dn, preferred_element_type=F32, precision=precision)

        @pl.when(k == nk - 1)
        def _():
            o_ref[...] = acc_ref[...].astype(o_ref.dtype)

    a_spec = pl.BlockSpec((tk, tm), lambda i, j, k: (k, i)) if ta else pl.BlockSpec((tm, tk), lambda i, j, k: (i, k))
    b_spec = pl.BlockSpec((tn, tk), lambda i, j, k: (j, k)) if tb else pl.BlockSpec((tk, tn), lambda i, j, k: (k, j))
    if col_blocks is None:
        out_spec = pl.BlockSpec((tm, tn), lambda i, j, k: (i, j))
        out_shape = jax.ShapeDtypeStruct((M, N), out_dtype)
    else:
        per = (N // col_blocks) // tn
        assert per * tn * col_blocks == N, (N, col_blocks, tn)
        out_spec = pl.BlockSpec((None, tm, tn), lambda i, j, k: (j // per, i, j % per))
        out_shape = jax.ShapeDtypeStruct((col_blocks, M, N // col_blocks), out_dtype)
    return _pcall(
        body, name=name, grid=(nm, nn, nk), in_specs=[a_spec, b_spec], out_specs=out_spec, out_shape=out_shape,
        scratch_shapes=[pltpu.VMEM((tm, tn), F32)], compiler_params=_cparams(3),
    )(a, b)


def _norm_rows(rows):
    out = []
    for r in rows:
        if not isinstance(r, tuple):
            r = (r,)
        arr, off, width, roff = (r + (0, None, 0)[len(r) - 1:])
        out.append((arr, off, width if width is not None else arr.shape[1], roff))
    return out


def _rw_plan(T, rows, pars, seg_rows, col_tile, tm_cap):
    widths = [r[2] for r in rows]
    wmax = max(widths + [p.shape[-1] for p in pars] + [1])
    if col_tile is not None:
        assert all(w == widths[0] for w in widths) and all(p.shape[-1] == widths[0] for p in pars)
        ncol = widths[0] // col_tile
        assert ncol * col_tile == widths[0]
        wmax = col_tile
    else:
        ncol = 1
    cap = tm_cap if tm_cap is not None else max(SUBLANE, min(512, (512 * 1024) // wmax))
    tm = _row_tile(T, cap, also=tuple(seg_rows) + tuple(abs(r[3]) for r in rows if r[3]))
    bounds = tuple(s // tm for s in seg_rows)
    return widths, ncol, tm, bounds


def _rw_specs(rows, pars, ncol, tm, bounds, col_tile):
    def seg(i):
        s = 0
        for b in bounds:
            s = s + (i >= b).astype(jnp.int32)
        return s

    specs = []
    for arr, off, w, roff in rows:
        bw = col_tile if col_tile is not None else w
        assert off % bw == 0 and roff % tm == 0, (off, bw, roff, tm)
        specs.append(pl.BlockSpec((tm, bw), functools.partial(
            lambda j, i, ob, rb, last: (jnp.clip(i + rb, 0, last), ob + j),
            ob=off // bw, rb=roff // tm, last=arr.shape[0] // tm - 1)))
    for p in pars:
        bw = col_tile if col_tile is not None else p.shape[-1]
        if p.shape[0] > 1:
            specs.append(pl.BlockSpec((None, 1, bw), lambda j, i: (seg(i), 0, j)))
        else:
            specs.append(pl.BlockSpec((None, 1, bw), lambda j, i: (0, 0, j)))
    return specs, seg


def _head_rows(head):
    top, bottom = head
    return [(top, 0, None, 0), (bottom, 0, None, -top.shape[0])]


def _rw_fwd(name, f, rows, pars, out_widths, *, T=None, seg_rows=(), col_tile=None, tm_cap=None, out_dtypes=None,
            head=None):
    rows = _norm_rows((_head_rows(head) if head else []) + list(rows))
    T = rows[0][0].shape[0] if T is None else T
    widths, ncol, tm, bounds = _rw_plan(T, rows, pars, seg_rows, col_tile, tm_cap)
    in_specs, _ = _rw_specs(rows, pars, ncol, tm, bounds, col_tile)
    nr, npar, nout = len(rows), len(pars), len(out_widths)

    def body(*refs):
        vals = [r[...] for r in refs[:nr + npar]]
        if head:
            vals = [jnp.where(pl.program_id(1) < head[0].shape[0] // tm, vals[0], vals[1])] + vals[2:]
        outs = f(*vals)
        if not isinstance(outs, (tuple, list)):
            outs = (outs,)
        for o_ref, o in zip(refs[nr + npar:], outs):
            o_ref[...] = o.astype(o_ref.dtype)

    out_specs = [pl.BlockSpec((tm, col_tile if col_tile is not None else w), lambda j, i: (i, j)) for w in out_widths]
    res = _pcall(
        body, name=name, grid=(ncol, T // tm), in_specs=in_specs, out_specs=out_specs,
        out_shape=[jax.ShapeDtypeStruct((T, w), dt) for w, dt in zip(out_widths, out_dtypes or [F32] * nout)],
        compiler_params=_cparams(2),
    )(*[r[0] for r in rows], *pars)
    return res if nout > 1 else res[0]


def _rw_bwd(name, f, rows, pars, cots, *, row_grad, par_grad, T=None, seg_rows=(), col_tile=None, tm_cap=None,
            add=None, cot_fn=None, row_dtypes=None, head=None, skip_rows=0):
    rows = _norm_rows((_head_rows(head) if head else []) + list(rows))
    cots = _norm_rows(cots)
    T = rows[0][0].shape[0] if T is None else T
    extra = _norm_rows([add]) if add is not None else []
    all_rows = rows + cots + extra
    widths, ncol, tm, bounds = _rw_plan(T, all_rows, pars, tuple(seg_rows) + ((skip_rows,) if skip_rows else ()),
                                        col_tile, tm_cap)
    bounds = bounds[:len(seg_rows)]
    in_specs, seg = _rw_specs(all_rows, pars, ncol, tm, bounds, col_tile)
    nr, nc, ne, npar = len(rows), len(cots), len(extra), len(pars)
    skip = 1 if head else 0
    widths = widths[skip:]
    nrf = nr - skip
    row_idx = [k for k in range(nrf) if row_grad[k]]
    par_idx = [k for k in range(npar) if par_grad[k]]

    def body(*refs):
        i = pl.program_id(1)

        def zero_before(vals, ops):
            return [jnp.where(i + c[3] // tm >= 0, v, jnp.zeros_like(v)) if c[3] < 0 else v for v, c in zip(vals, ops)]

        row_vals = [r[...] for r in refs[:nr]]
        if head:
            row_vals = [jnp.where(i < head[0].shape[0] // tm, row_vals[0], row_vals[1])] + row_vals[2:]
        cot_vals = zero_before([r[...] for r in refs[nr:nr + nc]], cots)
        add_vals = zero_before([r[...] for r in refs[nr + nc:nr + nc + ne]], extra)
        par_vals = [r[...] for r in refs[nr + nc + ne:nr + nc + ne + npar]]
        out_refs = refs[nr + nc + ne + npar:]
        outs, vjp = jax.vjp(f, *row_vals, *par_vals)
        if cot_fn is not None:
            cot_vals = cot_fn(*cot_vals)
            if not isinstance(cot_vals, (tuple, list)):
                cot_vals = (cot_vals,)
        if isinstance(outs, (tuple, list)):
            grads = vjp(tuple(c.astype(o.dtype) for c, o in zip(cot_vals, outs)))
        else:
            grads = vjp(cot_vals[0].astype(outs.dtype))
        first_seg = i == 0
        for b in bounds:
            first_seg = first_seg | (i == b)
        for n, k in enumerate(row_idx):
            g = grads[k]
            if n == 0 and add_vals:
                g = g + add_vals[0]
            out_refs[n][...] = g.astype(out_refs[n].dtype)
        for n, k in enumerate(par_idx):
            g = grads[nrf + k]
            o_ref = out_refs[len(row_idx) + n]
            first = first_seg if pars[k].shape[0] > 1 else (i == 0)

            @pl.when(first)
            def _(o_ref=o_ref, g=g):
                o_ref[...] = g

            @pl.when(jnp.logical_not(first))
            def _(o_ref=o_ref, g=g):
                o_ref[...] += g

    out_specs, out_shape = [], []
    for k in row_idx:
        w = widths[k]
        out_specs.append(pl.BlockSpec((tm, col_tile if col_tile is not None else w),
                                      lambda j, i: (jnp.maximum(i - skip_rows // tm, 0), j)))
        out_shape.append(jax.ShapeDtypeStruct((T - skip_rows, w), row_dtypes[len(out_shape)] if row_dtypes else F32))
    for k in par_idx:
        p = pars[k]
        bw = col_tile if col_tile is not None else p.shape[-1]
        if p.shape[0] > 1:
            out_specs.append(pl.BlockSpec((None, 1, bw), lambda j, i: (seg(i), 0, j)))
        else:
            out_specs.append(pl.BlockSpec((None, 1, bw), lambda j, i: (0, 0, j)))
        out_shape.append(jax.ShapeDtypeStruct(p.shape, F32))
    res = _pcall(
        body, name=name, grid=(ncol, T // tm), in_specs=in_specs, out_specs=out_specs, out_shape=out_shape,
        compiler_params=_cparams(2),
    )(*[r[0] for r in all_rows], *pars)
    return list(res[:len(row_idx)]), list(res[len(row_idx):])


def _f_modnorm(h, w, sc, sh):
    y = h * lax.rsqrt(jnp.mean(h * h, axis=-1, keepdims=True) + EPS)
    return (y * w) * (1.0 + sc) + sh


def _f_gate_res(h, y, g):
    return h + g * y


def _f_gate_res_bias(h, y, g, b):
    return h + g * (y + b)


def _f_gate(y, g):
    return g * y


def _f_gate_bias(y, g, b):
    return g * (y + b)


def _f_ffn_act(val, gate):
    return _silu(gate) * val


def _f_softplus(raw, bias):
    v = raw + bias
    return jnp.maximum(v, 0.0) + jnp.log(1.0 + jnp.exp(-jnp.abs(v)))


def _f_ssd_gate(yf, yb, xs, z, d_rep, nw):
    y = (yf + yb + d_rep * xs) * _silu(z)
    return (y * lax.rsqrt(jnp.mean(y * y, axis=-1, keepdims=True) + EPS)) * nw


def _f_glu(a, g, ba, bg):
    return (a + ba) * jax.nn.sigmoid(g + bg)


def _f_ln_silu(h, w, b):
    mu = jnp.mean(h, axis=-1, keepdims=True)
    d = h - mu
    y = d * lax.rsqrt(jnp.mean(d * d, axis=-1, keepdims=True) + EPS)
    return _silu(y * w + b)


def _f_loss_rows(h, t, w):
    y = (h * lax.rsqrt(jnp.mean(h * h, axis=-1, keepdims=True) + EPS)) * w
    e = y - t
    return 0.5 * jnp.mean(e * e, axis=-1, keepdims=True)


def _f_adamw(w, m, v, ga, gb):
    g = ga.astype(F32) + gb
    m = ADAM_B1 * m + (1.0 - ADAM_B1) * g
    v = ADAM_B2 * v + (1.0 - ADAM_B2) * (g * g)
    m_hat = m / (1.0 - ADAM_B1 ** ADAM_STEP)
    v_hat = v / (1.0 - ADAM_B2 ** ADAM_STEP)
    delta = -ADAM_LR * (m_hat / (jnp.sqrt(v_hat) + ADAM_EPS) + ADAM_WD * w)
    return g, delta, m, v


def _adamw(name, w, m, v, ga, gb):
    shape = w.shape
    c = shape[-1]
    two_d = [t.reshape(-1, c) for t in (w, m, v, ga, gb)]
    rows = two_d[0].shape[0]
    pad = _round_up(rows, SUBLANE) - rows
    if pad:
        two_d = [jnp.pad(t, ((0, pad), (0, 0))) for t in two_d]
    outs = _rw_fwd(name, _f_adamw, two_d, [], [c] * 4)
    return tuple(o[:rows].reshape(shape) for o in outs)


def _sum_leading(name, x, idxs, out_dtype=F32):
    _, R, C = x.shape
    tm = _row_tile(R, max(SUBLANE, min(512, (512 * 1024) // C)))

    def body(x_ref, o_ref):
        acc = x_ref[idxs[0]].astype(F32)
        for k in idxs[1:]:
            acc = acc + x_ref[k].astype(F32)
        o_ref[...] = acc.astype(o_ref.dtype)

    return _pcall(
        body, name=name, grid=(R // tm,), in_specs=[pl.BlockSpec((x.shape[0], tm, C), lambda i: (0, i, 0))],
        out_specs=pl.BlockSpec((tm, C), lambda i: (i, 0)), out_shape=jax.ShapeDtypeStruct((R, C), out_dtype),
        compiler_params=_cparams(1),
    )(x)


def _loss_fwd(h, t, w):
    T, D = h.shape
    tm = _row_tile(T, 256)

    def body(h_ref, t_ref, w_ref, o_ref):
        i = pl.program_id(0)
        part = jnp.sum(_f_loss_rows(h_ref[...], t_ref[...], w_ref[...]), axis=0, keepdims=True)
        part = jnp.broadcast_to(part, (1, LANE))

        @pl.when(i == 0)
        def _():
            o_ref[...] = part

        @pl.when(i > 0)
        def _():
            o_ref[...] += part

    return _pcall(
        body, name="loss_fwd", grid=(T // tm,),
        in_specs=[pl.BlockSpec((tm, D), lambda i: (i, 0)), pl.BlockSpec((tm, D), lambda i: (i, 0)),
                  pl.BlockSpec((1, D), lambda i: (0, 0))],
        out_specs=pl.BlockSpec((1, LANE), lambda i: (0, 0)), out_shape=jax.ShapeDtypeStruct((1, LANE), F32),
        compiler_params=_cparams(1),
    )(h, t, w)


CONV_ROWS = 256
CONV_ROWS_FEW_TAPS = 1024
CONV_ACC_ELEMS = 16384


def _col_mask(arg, t):
    col = jnp.bitwise_and(t, GRID_W - 1)
    return (col != 0) if arg < 0 else (col != GRID_W - 1)


def _conv_plan(T, C, taps):
    seg = [m[1] for _, m in taps if m is not None and m[0] == "seg"]
    cap = CONV_ROWS_FEW_TAPS if len(taps) <= 9 else CONV_ROWS
    rc = next(r for r in (1024, 768, 512, 256, LANE) if r <= cap and T % r == 0)
    ct = next((t for t in (512, 256, LANE) if C % t == 0), C)
    reach = max(abs(s) for s, _ in taps)
    hb = next(h for h in (8, 16, 32, 64, 128, 256) if h >= reach and rc % h == 0)
    sub = max(2 * SUBLANE, min(rc, CONV_ACC_ELEMS // ct))
    boundary = None
    if seg:
        inside = seg[0] % rc
        boundary = (seg[0], (inside - reach, inside + reach) if inside else None)
    taps = [(s, None if (m is None or m[0] == "seg") else m[1]) for s, m in taps]
    return rc, ct, hb, sub, T // rc, C // ct, boundary, taps


def _seg_ok(boundary, i, rc, r0, n, s):
    if boundary is None or boundary[1] is None or s == 0 or r0 + n <= boundary[1][0] or r0 >= boundary[1][1]:
        return None
    t = i * rc + r0 + lax.broadcasted_iota(jnp.int32, (n, 1), 0)
    return (t >= boundary[0]) == ((t + s) >= boundary[0])


def _halo_specs(rc, ct, hb, T, off_blocks):
    per = rc // hb
    last = T // hb - 1
    prev = pl.BlockSpec((hb, ct), lambda j, i: (jnp.maximum(i * per - 1, 0), off_blocks + j))
    cur = pl.BlockSpec((rc, ct), lambda j, i: (i, off_blocks + j))
    nxt = pl.BlockSpec((hb, ct), lambda j, i: (jnp.minimum((i + 1) * per, last), off_blocks + j))
    return [prev, cur, nxt]


def _fill_halo(pad_ref, p_ref, c_ref, n_ref, i, nrc, rc, hb, boundary):
    has_prev = i > 0
    has_next = i < nrc - 1
    if boundary is not None:
        has_prev = has_prev & (i * rc != boundary[0])
        has_next = has_next & ((i + 1) * rc != boundary[0])
    pad_ref[0:hb, :] = jnp.where(has_prev, p_ref[...], 0.0)
    pad_ref[hb:hb + rc, :] = c_ref[...]
    pad_ref[hb + rc:hb + rc + hb, :] = jnp.where(has_next, n_ref[...], 0.0)


def _shift_plan(keys):
    count = {}
    for s, m in keys:
        k = (s % SUBLANE, m)
        count[k] = count.get(k, 0) + 1
    slots = {}
    for k, n in sorted(count.items(), key=lambda kv: (kv[0][0], str(kv[0][1]))):
        if k != (0, None) and (n >= 2 or k[1] is not None):
            slots[k] = len(slots)
    return slots


def _build_shifted(copies_ref, slots, pad_ref, keys, i, rc, hb, sub):
    for (r, m), slot in slots.items():
        qs = [s - r for s, mk in keys if (s % SUBLANE, mk) == (r, m)]
        lo, hi = hb + min(qs), hb + rc + max(qs)
        for p in range(lo, hi, sub):
            n = min(sub, hi - p)
            v = pad_ref[p + r:p + r + n, :]
            if m is not None:
                t = i * rc - hb + p + r + lax.broadcasted_iota(jnp.int32, (n, 1), 0)
                v = jnp.where(_col_mask(m, t), v, 0.0)
            copies_ref[slot, p:p + n, :] = v


def _read(copies_ref, slots, pad_ref, s, m, row, n):
    k = (s % SUBLANE, m)
    if k in slots:
        q = s - k[0]
        return copies_ref[slots[k], row + q:row + q + n, :]
    return pad_ref[row + s:row + s + n, :]


def _conv_fwd(name, u, col_off, C, w, b, taps, act=False):
    T = u.shape[0]
    rc, ct, hb, sub, nrc, ncc, boundary, taps = _conv_plan(T, C, taps)
    assert col_off % ct == 0
    K = len(taps)
    keys = [(s, None) for s, _ in taps]
    slots = _shift_plan(keys)
    dirs = sorted({m for _, m in taps if m is not None})

    def body(up, uc, un, w_ref, b_ref, *rest):
        y_ref = rest[0]
        pad_ref, copies_ref = rest[-2], rest[-1]
        i = pl.program_id(1)
        _fill_halo(pad_ref, up, uc, un, i, nrc, rc, hb, boundary)
        _build_shifted(copies_ref, slots, pad_ref, keys, i, rc, hb, sub)
        for r0 in range(0, rc, sub):
            acc = jnp.broadcast_to(b_ref[...], (sub, ct))
            for m in [None] + dirs:
                part = None
                for k, (s, mk) in enumerate(taps):
                    if mk != m:
                        continue
                    v = _read(copies_ref, slots, pad_ref, s, None, hb + r0, sub)
                    ok = _seg_ok(boundary, i, rc, r0, sub, s)
                    term = w_ref[k:k + 1, :] * (v if ok is None else jnp.where(ok, v, 0.0))
                    part = term if part is None else part + term
                if part is None:
                    continue
                if m is not None:
                    t = i * rc + r0 + lax.broadcasted_iota(jnp.int32, (sub, 1), 0)
                    part = jnp.where(_col_mask(m, t), part, 0.0)
                acc = acc + part
            y_ref[r0:r0 + sub, :] = acc
            if act:
                rest[1][r0:r0 + sub, :] = _silu(acc)

    n_out = 2 if act else 1
    res = _pcall(
        body, name=name, grid=(ncc, nrc),
        in_specs=_halo_specs(rc, ct, hb, T, col_off // ct) + [pl.BlockSpec((K, ct), lambda j, i: (0, j)),
                                                              pl.BlockSpec((1, ct), lambda j, i: (0, j))],
        out_specs=[pl.BlockSpec((rc, ct), lambda j, i: (i, j))] * n_out,
        out_shape=[jax.ShapeDtypeStruct((T, C), F32)] * n_out,
        scratch_shapes=[pltpu.VMEM((rc + 2 * hb, ct), F32), pltpu.VMEM((max(len(slots), 1), rc + 2 * hb, ct), F32)],
        compiler_params=_cparams(2),
    )(u, u, u, w, b)
    return res if act else res[0]


def _conv_bwd(name, u, col_off, C, w, g, taps, du_dtype=F32):
    T = u.shape[0]
    rc, ct, hb, sub, nrc, ncc, boundary, taps = _conv_plan(T, C, taps)
    K = len(taps)
    u_keys = [(s, None) for s, _ in taps]
    dirs = sorted({m for _, m in taps if m is not None})
    g_keys = [(-s, m) for s, m in taps] + [(0, m) for m in dirs]
    u_slots, g_slots = _shift_plan(u_keys), _shift_plan(g_keys)

    def body(up, uc, un, gp, gc, gn, w_ref, du_ref, dw_ref, db_ref, upad, gpad, ucopies, gcopies):
        i = pl.program_id(1)
        _fill_halo(upad, up, uc, un, i, nrc, rc, hb, boundary)
        _fill_halo(gpad, gp, gc, gn, i, nrc, rc, hb, boundary)
        _build_shifted(ucopies, u_slots, upad, u_keys, i, rc, hb, sub)
        _build_shifted(gcopies, g_slots, gpad, g_keys, i, rc, hb, sub)

        @pl.when(i == 0)
        def _():
            dw_ref[...] = jnp.zeros_like(dw_ref)
            db_ref[...] = jnp.zeros_like(db_ref)

        def fold(v):
            return jnp.sum(v.reshape(sub // SUBLANE, SUBLANE, ct), axis=0)

        dbs = jnp.zeros((SUBLANE, ct), F32)
        for r0 in range(0, rc, sub):
            dbs = dbs + fold(gpad[hb + r0:hb + r0 + sub, :])
            acc = jnp.zeros((sub, ct), F32)
            for k, (s, m) in enumerate(taps):
                v = _read(gcopies, g_slots, gpad, -s, m, hb + r0, sub)
                ok = _seg_ok(boundary, i, rc, r0, sub, -s)
                acc = acc + w_ref[k:k + 1, :] * (v if ok is None else jnp.where(ok, v, 0.0))
            du_ref[r0:r0 + sub, :] = acc.astype(du_ref.dtype)
        db_ref[...] += jnp.sum(dbs, axis=0, keepdims=True)
        for k, (s, m) in enumerate(taps):
            part = jnp.zeros((SUBLANE, ct), F32)
            for r0 in range(0, rc, sub):
                v = _read(ucopies, u_slots, upad, s, None, hb + r0, sub)
                ok = _seg_ok(boundary, i, rc, r0, sub, s)
                part = part + fold(_read(gcopies, g_slots, gpad, 0, m, hb + r0, sub)
                                   * (v if ok is None else jnp.where(ok, v, 0.0)))
            dw_ref[k:k + 1, :] += jnp.sum(part, axis=0, keepdims=True)

    halo_u = _halo_specs(rc, ct, hb, T, col_off // ct)
    halo_g = _halo_specs(rc, ct, hb, T, 0)
    rows = rc + 2 * hb
    return _pcall(
        body, name=name, grid=(ncc, nrc),
        in_specs=halo_u + halo_g + [pl.BlockSpec((K, ct), lambda j, i: (0, j))],
        out_specs=[pl.BlockSpec((rc, ct), lambda j, i: (i, j)), pl.BlockSpec((K, ct), lambda j, i: (0, j)),
                   pl.BlockSpec((1, ct), lambda j, i: (0, j))],
        out_shape=[jax.ShapeDtypeStruct((T, C), du_dtype), jax.ShapeDtypeStruct((K, C), F32),
                   jax.ShapeDtypeStruct((1, C), F32)],
        scratch_shapes=[pltpu.VMEM((rows, ct), F32), pltpu.VMEM((rows, ct), F32),
                        pltpu.VMEM((max(len(u_slots), 1), rows, ct), F32),
                        pltpu.VMEM((max(len(g_slots), 1), rows, ct), F32)],
        compiler_params=_cparams(2),
    )(u, u, u, g, g, g, w)


def _ssd_group(xg, bm, cm, s_in, *per_head, reverse, P):
    R = len(per_head) // 2
    dtrs, a_s = per_head[:R], per_head[R:]
    q, rp = xg.shape
    ii = lax.broadcasted_iota(jnp.int32, (q, q), 0)
    jj = lax.broadcasted_iota(jnp.int32, (q, q), 1)
    causal = (jj >= ii) if reverse else (jj <= ii)
    causal_t = (ii >= jj) if reverse else (ii <= jj)
    eye = ii == jj
    lane = lax.broadcasted_iota(jnp.int32, (1, rp), 1)
    row = lax.broadcasted_iota(jnp.int32, (rp, 1), 0)
    nt = (((1,), (1,)), ((), ()))
    tn = (((0,), (0,)), ((), ()))
    cb = lax.dot_general(cm.astype(BF16), bm.astype(BF16), nt, preferred_element_type=F32)
    dt_x = jnp.zeros((q, rp), F32)
    acum_x = jnp.zeros((q, rp), F32)
    tot_row = jnp.zeros((1, rp), F32)
    tot_col = jnp.zeros((rp, 1), F32)
    wts, lane_masks = [], []
    for r in range(R):
        hm = (lane >= r * P) & (lane < (r + 1) * P)
        hc = (row >= r * P) & (row < (r + 1) * P)
        dt_c = jnp.sum(jnp.where(eye, dtrs[r], 0.0), axis=1, keepdims=True)
        dac = dt_c * a_s[r]
        dar = dtrs[r] * a_s[r]
        acum_c = jnp.sum(jnp.where(causal, dar, 0.0), axis=1, keepdims=True)
        acum_r = jnp.sum(jnp.where(causal_t, dac, 0.0), axis=0, keepdims=True)
        decay = jnp.where(causal, jnp.exp(jnp.where(causal, acum_c - acum_r, 0.0)), 0.0)
        tot = jnp.sum(dac, axis=0, keepdims=True)
        dt_x = jnp.where(hm, dt_c, dt_x)
        acum_x = jnp.where(hm, acum_c, acum_x)
        tot_row = jnp.where(hm, tot, tot_row)
        tot_col = jnp.where(hc, tot, tot_col)
        wts.append((cb * decay).astype(BF16))
        lane_masks.append(hm)
    xdt = xg * dt_x
    xdt_b = xdt.astype(BF16)
    y = jnp.zeros((q, rp), F32)
    for r in range(R):
        y = jnp.where(lane_masks[r], jnp.dot(wts[r], xdt_b, preferred_element_type=F32), y)
    dte = jnp.exp(tot_row - acum_x)
    cs = lax.dot_general((xdt * dte).astype(BF16), bm.astype(BF16), tn, preferred_element_type=F32)
    y = y + lax.dot_general(cm.astype(BF16), s_in.astype(BF16), nt, preferred_element_type=F32) * jnp.exp(acum_x)
    s_out = jnp.exp(tot_col) * s_in + cs
    return y, s_out


def _ssd_group_state(xg, bm, s_in, *per_head, reverse, P):
    R = len(per_head) // 2
    dtrs, a_s = per_head[:R], per_head[R:]
    q, rp = xg.shape
    ii = lax.broadcasted_iota(jnp.int32, (q, q), 0)
    jj = lax.broadcasted_iota(jnp.int32, (q, q), 1)
    causal = (jj >= ii) if reverse else (jj <= ii)
    eye = ii == jj
    lane = lax.broadcasted_iota(jnp.int32, (1, rp), 1)
    row = lax.broadcasted_iota(jnp.int32, (rp, 1), 0)
    dt_x = jnp.zeros((q, rp), F32)
    acum_x = jnp.zeros((q, rp), F32)
    tot_row = jnp.zeros((1, rp), F32)
    tot_col = jnp.zeros((rp, 1), F32)
    for r in range(R):
        hm = (lane >= r * P) & (lane < (r + 1) * P)
        hc = (row >= r * P) & (row < (r + 1) * P)
        dt_c = jnp.sum(jnp.where(eye, dtrs[r], 0.0), axis=1, keepdims=True)
        acum_c = jnp.sum(jnp.where(causal, dtrs[r] * a_s[r], 0.0), axis=1, keepdims=True)
        tot = jnp.sum(dt_c * a_s[r], axis=0, keepdims=True)
        dt_x = jnp.where(hm, dt_c, dt_x)
        acum_x = jnp.where(hm, acum_c, acum_x)
        tot_row = jnp.where(hm, tot, tot_row)
        tot_col = jnp.where(hc, tot, tot_col)
    xe = xg * dt_x * jnp.exp(tot_row - acum_x)
    cs = lax.dot_general(xe.astype(BF16), bm.astype(BF16), (((0,), (0,)), ((), ())), preferred_element_type=F32)
    return jnp.exp(tot_col) * s_in + cs


def _ssd_maps(NC, ncc, reverse_steps):
    def chunk(d, s):
        if reverse_steps:
            s = NC - 1 - s
        return s if d == 0 else jnp.where(s < ncc, ncc - 1 - s, NC - 1 - s + ncc)

    def lat_chunk(d, s):
        c = chunk(d, s) - ncc
        return jnp.where(c < 0, 0 if d == 0 else NC - ncc - 1, c)

    def step(s):
        return NC - 1 - s if reverse_steps else s

    return chunk, lat_chunk, step


SSD_GROUPS_PER_STEP = 2


def _ssd_specs(chunk, d, GB, R, Q, N, RP, b_off, c_off):
    assert b_off % (GB * N) == 0 and c_off % (GB * N) == 0
    bo, co = b_off // (GB * N), c_off // (GB * N)
    return [
        pl.BlockSpec((Q, GB * RP), lambda g, s: (chunk(d, s), g)),
        pl.BlockSpec((Q, GB * N), lambda g, s: (chunk(d, s), bo + g)),
        pl.BlockSpec((Q, GB * N), lambda g, s: (chunk(d, s), co + g)),
        pl.BlockSpec((GB * R, 1, Q), lambda g, s: (g, 0, chunk(d, s))),
        pl.BlockSpec((GB * R, 1, 1), lambda g, s: (g, 0, 0)),
    ]


def _ssd_fwd(xbc, b_off, c_off, dtr, a, P, ncc):
    T = xbc.shape[0]
    H = dtr[0].shape[0]
    N, Q = SSD_STATE, SSD_CHUNK
    NC = T // Q
    G = (c_off - b_off) // N
    R = H // G
    RP = R * P
    GB = SSD_GROUPS_PER_STEP if G % SSD_GROUPS_PER_STEP == 0 else 1
    chunk, lat_chunk, _ = _ssd_maps(NC, ncc, False)

    def body(*refs):
        s = pl.program_id(1)
        s_ref = refs[-1]

        @pl.when(s == 0)
        def _():
            s_ref[...] = jnp.zeros_like(s_ref)

        for d in range(2):
            x_ref, b_ref, c_ref, dtr_ref, a_ref = refs[5 * d:5 * d + 5]
            y_ref, se_ref = refs[10 + 2 * d:12 + 2 * d]
            for gg in range(GB):
                cols, bcols = slice(gg * RP, (gg + 1) * RP), slice(gg * N, (gg + 1) * N)
                s_in = s_ref[d, gg]
                se_ref[gg] = s_in
                per_head = [dtr_ref[gg * R + r] for r in range(R)] + [a_ref[gg * R + r] for r in range(R)]

                @pl.when(s >= ncc)
                def _(d=d, gg=gg, cols=cols, bcols=bcols, x_ref=x_ref, b_ref=b_ref, c_ref=c_ref, y_ref=y_ref,
                      s_in=s_in, per_head=per_head):
                    y, s_out = _ssd_group(x_ref[:, cols], b_ref[:, bcols], c_ref[:, bcols], s_in, *per_head,
                                          reverse=d == 1, P=P)
                    y_ref[:, cols] = y
                    s_ref[d, gg] = s_out

                @pl.when(s < ncc)
                def _(d=d, gg=gg, cols=cols, bcols=bcols, x_ref=x_ref, b_ref=b_ref, s_in=s_in, per_head=per_head):
                    s_ref[d, gg] = _ssd_group_state(x_ref[:, cols], b_ref[:, bcols], s_in, *per_head,
                                                    reverse=d == 1, P=P)

    in_specs, out_specs, out_shape, operands = [], [], [], []
    for d in range(2):
        in_specs += _ssd_specs(chunk, d, GB, R, Q, N, RP, b_off, c_off)
        operands += [xbc, xbc, xbc, dtr[d], a[d]]
        out_specs += [pl.BlockSpec((Q, GB * RP), functools.partial(lambda g, s, d: (lat_chunk(d, s), g), d=d)),
                      pl.BlockSpec((GB, None, RP, N), lambda g, s: (g, s, 0, 0))]
        out_shape += [jax.ShapeDtypeStruct((T - ncc * Q, H * P), F32), jax.ShapeDtypeStruct((G, NC, RP, N), F32)]
    y_f, se_f, y_b, se_b = _pcall(
        body, name="ssd_fwd", grid=(G // GB, NC), in_specs=in_specs, out_specs=out_specs, out_shape=out_shape,
        scratch_shapes=[pltpu.VMEM((2, GB, RP, N), F32)], compiler_params=_cparams(2),
    )(*operands)
    return (y_f, y_b), (se_f, se_b)


def _ssd_bwd(xbc, b_off, c_off, dtr, a, s_enter, dy, P, ncc):
    T = xbc.shape[0]
    H = dtr[0].shape[0]
    N, Q = SSD_STATE, SSD_CHUNK
    NC = T // Q
    G = (c_off - b_off) // N
    R = H // G
    RP = R * P
    GB = SSD_GROUPS_PER_STEP if G % SSD_GROUPS_PER_STEP == 0 else 1
    chunk, lat_chunk, step = _ssd_maps(NC, ncc, True)
    n_in, n_out = 7, 5

    def body(*refs):
        s = pl.program_id(1)
        ds_ref = refs[-1]

        @pl.when(s == 0)
        def _():
            ds_ref[...] = jnp.zeros_like(ds_ref)

        for d in range(2):
            x_ref, b_ref, c_ref, dtr_ref, a_ref, se_ref, dy_ref = refs[n_in * d:n_in * (d + 1)]
            dx_ref, db_ref, dc_ref, ddtr_ref, da_ref = refs[2 * n_in + n_out * d:2 * n_in + n_out * (d + 1)]
            for gg in range(GB):
                cols, bcols = slice(gg * RP, (gg + 1) * RP), slice(gg * N, (gg + 1) * N)
                per_head = [dtr_ref[gg * R + r] for r in range(R)] + [a_ref[gg * R + r] for r in range(R)]

                def store(grads, dx_ref=dx_ref, db_ref=db_ref, ddtr_ref=ddtr_ref, da_ref=da_ref, d=d, gg=gg,
                          cols=cols, bcols=bcols):
                    dx_ref[:, cols] = grads[0]
                    db_ref[:, bcols] = grads[1]
                    ds_ref[d, gg] = grads[2]
                    for r in range(R):
                        ddtr_ref[gg * R + r] = grads[3 + r]
                        da_ref[gg, r] = jnp.broadcast_to(grads[3 + R + r], (SUBLANE, LANE))

                @pl.when(s < NC - ncc)
                def _(d=d, gg=gg, cols=cols, bcols=bcols, x_ref=x_ref, b_ref=b_ref, c_ref=c_ref, se_ref=se_ref,
                      dy_ref=dy_ref, dc_ref=dc_ref, per_head=per_head, store=store):
                    f = functools.partial(_ssd_group, reverse=d == 1, P=P)
                    _, vjp = jax.vjp(f, x_ref[:, cols], b_ref[:, bcols], c_ref[:, bcols], se_ref[gg], *per_head)
                    grads = vjp((dy_ref[:, cols], ds_ref[d, gg]))
                    dc_ref[:, bcols] = grads[2]
                    store(grads[:2] + grads[3:])

                @pl.when(s >= NC - ncc)
                def _(d=d, gg=gg, cols=cols, bcols=bcols, x_ref=x_ref, b_ref=b_ref, se_ref=se_ref, dc_ref=dc_ref,
                      per_head=per_head, store=store):
                    f = functools.partial(_ssd_group_state, reverse=d == 1, P=P)
                    _, vjp = jax.vjp(f, x_ref[:, cols], b_ref[:, bcols], se_ref[gg], *per_head)
                    dc_ref[:, bcols] = jnp.zeros((Q, N), F32)
                    store(vjp(ds_ref[d, gg]))

    in_specs, out_specs, out_shape, operands = [], [], [], []
    for d in range(2):
        in_specs += _ssd_specs(chunk, d, GB, R, Q, N, RP, b_off, c_off) + [
            pl.BlockSpec((GB, None, RP, N), lambda g, s: (g, step(s), 0, 0)),
            pl.BlockSpec((Q, GB * RP), functools.partial(lambda g, s, d: (lat_chunk(d, s), g), d=d)),
        ]
        operands += [xbc, xbc, xbc, dtr[d], a[d], s_enter[d], dy]
    for d in range(2):
        at_chunk = functools.partial(lambda g, s, d: (chunk(d, s), g), d=d)
        out_specs += [
            pl.BlockSpec((Q, GB * RP), at_chunk), pl.BlockSpec((Q, GB * N), at_chunk),
            pl.BlockSpec((Q, GB * N), at_chunk),
            pl.BlockSpec((GB * R, 1, Q), functools.partial(lambda g, s, d: (g, 0, chunk(d, s)), d=d)),
            pl.BlockSpec((GB, None, R, SUBLANE, LANE), lambda g, s: (g, s, 0, 0, 0)),
        ]
        out_shape += [
            jax.ShapeDtypeStruct((T, H * P), F32), jax.ShapeDtypeStruct((T, G * N), F32),
            jax.ShapeDtypeStruct((T, G * N), F32), jax.ShapeDtypeStruct((H, 1, T), F32),
            jax.ShapeDtypeStruct((G, NC, R, SUBLANE, LANE), F32),
        ]
    res = _pcall(
        body, name="ssd_bwd", grid=(G // GB, NC), in_specs=in_specs, out_specs=out_specs, out_shape=out_shape,
        scratch_shapes=[pltpu.VMEM((2, GB, RP, N), F32)], compiler_params=_cparams(2),
    )(*operands)
    return res[:n_out], res[n_out:]


def _allgather8(name, v):
    R, C = v.shape

    def body(x_ref, out_ref, send_sems, recv_sems, local_sem):
        x, y, c = lax.axis_index("x"), lax.axis_index("y"), lax.axis_index("c")
        me, sibling = (x, y, c), (x, y, 1 - c)
        chips = [(1 - x, y), (x, 1 - y), (1 - x, 1 - y)]

        def slot(px, py, pc):
            return out_ref.at[4 * px + 2 * py + pc]

        def copy(k, block, to, src=None):
            return pltpu.make_async_remote_copy(
                src_ref=slot(*block) if src is None else src, dst_ref=slot(*block),
                send_sem=send_sems.at[k], recv_sem=recv_sems.at[k], device_id=to, device_id_type=MESH)

        mine = pltpu.make_async_copy(x_ref, slot(*me), local_sem)
        mine.start()
        first = [copy(0, me, sibling, src=x_ref)]
        first += [copy(1 + j, me, (*chip, c), src=x_ref) for j, chip in enumerate(chips)]
        for cp in first:
            cp.start()
        passed = [copy(4 + j, (*chip, c), sibling) for j, chip in enumerate(chips)]
        for j, chip in enumerate(chips):
            copy(1 + j, (*chip, c), me).wait_recv()
            passed[j].start()
        copy(0, sibling, me).wait_recv()
        for j, chip in enumerate(chips):
            copy(4 + j, (*chip, 1 - c), me).wait_recv()
        for cp in first + passed:
            cp.wait_send()
        mine.wait()

    return _pcall(
        body, name=name, out_shape=jax.ShapeDtypeStruct((N_DEV, R, C), v.dtype),
        in_specs=[pl.BlockSpec(memory_space=pltpu.VMEM)], out_specs=pl.BlockSpec(memory_space=pltpu.VMEM),
        scratch_shapes=[pltpu.SemaphoreType.DMA((7,)), pltpu.SemaphoreType.DMA((7,)), pltpu.SemaphoreType.DMA],
        compiler_params=pltpu.CompilerParams(vmem_limit_bytes=VMEM_LIMIT_BYTES),
    )(v)


def _slot(ref, k, axis, size):
    if axis is None:
        return ref.at[k]
    align = LANE if size % LANE == 0 else 2 * SUBLANE
    assert size % align == 0
    return ref.at[(slice(None),) * axis + (pl.ds(pl.multiple_of(k * size, align), size),)]


def _exchange4_start(name, srcs, bcast, dep, axes=None, half=False):
    n = len(srcs)
    axes = list(axes) if axes is not None else [None] * n
    sizes = [None if ax is None else s.shape[ax] for s, ax in zip(srcs, axes)]

    def land_shape(s, ax):
        if not bcast:
            return s.shape
        if half:
            return (N_CHIPS,) + s.shape[1:]
        if ax is None:
            return (N_CHIPS,) + s.shape
        return s.shape[:ax] + (N_CHIPS * s.shape[ax],) + s.shape[ax + 1:]

    lands = [lax.empty(land_shape(s, ax), s.dtype) for s, ax in zip(srcs, axes)]

    def body(*refs):
        src, land = refs[:n], refs[n:2 * n]
        send_sems, recv_sems = refs[2 * n + 1], refs[2 * n + 2]
        token = refs[-1]
        x, y, c = lax.axis_index("x"), lax.axis_index("y"), lax.axis_index("c")
        me = 2 * x + y
        for a in range(n):
            for j, (px, py) in enumerate([(1 - x, y), (x, 1 - y), (1 - x, 1 - y)]):
                pltpu.make_async_remote_copy(
                    src_ref=(src[a].at[c] if half else src[a]) if bcast else src[a].at[2 * px + py],
                    dst_ref=_slot(land[a], me, axes[a], sizes[a]),
                    send_sem=send_sems.at[3 * a + j], recv_sem=recv_sems.at[3 * a + j], device_id=(px, py, c),
                    device_id_type=MESH).start()
        token[...] = jnp.zeros_like(token)

    hbm = pl.BlockSpec(memory_space=pltpu.HBM)
    sem = pl.BlockSpec(memory_space=pltpu.SEMAPHORE)
    outs = _pcall(
        body, name=name,
        out_shape=(pltpu.SemaphoreType.DMA((3 * n,)), pltpu.SemaphoreType.DMA((3 * n,)),
                   *[pltpu.HBM(s.shape, s.dtype) for s in srcs], *[pltpu.HBM(l.shape, l.dtype) for l in lands],
                   jax.ShapeDtypeStruct((SUBLANE, LANE), F32)),
        in_specs=[hbm] * (2 * n) + [pl.BlockSpec(memory_space=pl.ANY)],
        out_specs=(sem, sem, *[hbm] * (2 * n), pl.BlockSpec(memory_space=pltpu.VMEM)),
        input_output_aliases={k: 2 + k for k in range(2 * n)},
        compiler_params=pltpu.CompilerParams(has_side_effects=pltpu.SideEffectType.DATAFLOW_SIDE_EFFECTING),
    )(*[pltpu.with_memory_space_constraint(s, pltpu.HBM) for s in srcs],
      *[pltpu.with_memory_space_constraint(l, pltpu.HBM) for l in lands], dep)
    return (n, bcast, half, axes, sizes, outs[0], outs[1], outs[2:2 + n], outs[2 + n:2 + 2 * n]), outs[-1]


def _exchange4_wait(name, handle, after):
    n, bcast, half, axes, sizes, send_sems, recv_sems, src_thru, land_thru = handle

    def body(*refs):
        src, land = refs[:n], refs[n:2 * n]
        send_sems, recv_sems = refs[2 * n], refs[2 * n + 1]
        x, y, c = lax.axis_index("x"), lax.axis_index("y"), lax.axis_index("c")
        for a in range(n):
            for j, (px, py) in enumerate([(1 - x, y), (x, 1 - y), (1 - x, 1 - y)]):
                pk = 2 * px + py
                copy = pltpu.make_async_remote_copy(
                    src_ref=(src[a].at[c] if half else src[a]) if bcast else src[a].at[pk],
                    dst_ref=_slot(land[a], pk, axes[a], sizes[a]),
                    send_sem=send_sems.at[3 * a + j], recv_sem=recv_sems.at[3 * a + j], device_id=(px, py, c),
                    device_id_type=MESH)
                copy.wait_send()
                copy.wait_recv()

    hbm = pl.BlockSpec(memory_space=pltpu.HBM)
    sem = pl.BlockSpec(memory_space=pltpu.SEMAPHORE)
    outs = _pcall(
        body, name=name,
        out_shape=tuple(pltpu.HBM(t.shape, t.dtype) for t in (*src_thru, *land_thru)),
        in_specs=[hbm] * (2 * n) + [sem, sem, pl.BlockSpec(memory_space=pl.ANY)], out_specs=tuple([hbm] * (2 * n)),
        input_output_aliases={k: k for k in range(2 * n)},
        compiler_params=pltpu.CompilerParams(has_side_effects=pltpu.SideEffectType.DATAFLOW_SIDE_EFFECTING),
    )(*src_thru, *land_thru, send_sems, recv_sems, after)
    return list(outs[:n]), list(outs[n:])


def _tie(name, v, token):
    def body(v_ref, token_ref, o_ref):
        del v_ref, token_ref, o_ref

    any_spec = pl.BlockSpec(memory_space=pl.ANY)
    return _pcall(body, name=name, out_shape=jax.ShapeDtypeStruct(v.shape, v.dtype), in_specs=[any_spec, any_spec],
                  out_specs=any_spec, input_output_aliases={0: 0})(v, token)


def _fill_own(landed, own, me, bcast):
    blk = own if bcast else lax.dynamic_index_in_dim(own, me, 0, keepdims=False)
    return lax.dynamic_update_index_in_dim(landed, blk, me, 0)


def _swap_sibling(name, srcs, by_core=False):
    n = len(srcs)

    def body(*refs):
        src, out = refs[:n], refs[n:2 * n]
        send_sems, recv_sems = refs[2 * n:]
        x, y, c = lax.axis_index("x"), lax.axis_index("y"), lax.axis_index("c")
        copies = []
        for a in range(n):
            send = pltpu.make_async_remote_copy(
                src_ref=src[a], dst_ref=out[a].at[c] if by_core else out[a], send_sem=send_sems.at[a],
                recv_sem=recv_sems.at[a], device_id=(x, y, 1 - c), device_id_type=MESH)
            send.start()
            arrive = pltpu.make_async_remote_copy(
                src_ref=src[a], dst_ref=out[a].at[1 - c] if by_core else out[a], send_sem=send_sems.at[a],
                recv_sem=recv_sems.at[a], device_id=(x, y, 1 - c), device_id_type=MESH)
            copies.append((send, arrive))
        for send, arrive in copies:
            send.wait_send()
            arrive.wait_recv()

    any_spec = pl.BlockSpec(memory_space=pl.ANY)
    return _pcall(
        body, name=name,
        out_shape=[jax.ShapeDtypeStruct(((2,) + s.shape) if by_core else s.shape, s.dtype) for s in srcs],
        in_specs=[any_spec] * n, out_specs=[any_spec] * n,
        scratch_shapes=[pltpu.SemaphoreType.DMA((n,)), pltpu.SemaphoreType.DMA((n,))],
    )(*srcs)


def _mod_fwd(c16, mod_w, mod_b_shard):
    nl, D, S = mod_w.shape

    def body(c_ref, w_ref, b_ref, o_ref):
        s = _silu(c_ref[...]).astype(BF16)
        o_ref[...] = jnp.dot(s, w_ref[...].astype(BF16), preferred_element_type=F32) + b_ref[...]

    return _pcall(
        body, name="mod_fwd", grid=(nl,),
        in_specs=[pl.BlockSpec((16, D), lambda l: (0, 0)), pl.BlockSpec((None, D, S), lambda l: (l, 0, 0)),
                  pl.BlockSpec((None, 1, S), lambda l: (l, 0, 0))],
        out_specs=pl.BlockSpec((None, 16, S), lambda l: (l, 0, 0)),
        out_shape=jax.ShapeDtypeStruct((nl, 16, S), F32), compiler_params=_cparams(1),
    )(c16, mod_w, mod_b_shard)


def _mod_w_update(s16t, dm16, w, m, v):
    nl, D, S = w.shape
    tm = _row_tile(D, 256)

    def body(s_ref, dm_ref, w_ref, m_ref, v_ref, g_ref, dl_ref, nm_ref, nv_ref):
        g = jnp.dot(s_ref[...], dm_ref[...], preferred_element_type=F32, precision=HIGHEST)
        g, dl, nm, nv = _f_adamw(w_ref[...], m_ref[...], v_ref[...], g, jnp.zeros_like(g))
        g_ref[...] = g
        dl_ref[...] = dl
        nm_ref[...] = nm
        nv_ref[...] = nv

    big = pl.BlockSpec((None, tm, S), lambda l, i: (l, i, 0))
    return _pcall(
        body, name="mod_w_update", grid=(nl, D // tm),
        in_specs=[pl.BlockSpec((tm, 16), lambda l, i: (i, 0)), pl.BlockSpec((None, 16, S), lambda l, i: (l, 0, 0)),
                  big, big, big],
        out_specs=[big] * 4, out_shape=[jax.ShapeDtypeStruct(w.shape, F32)] * 4, compiler_params=_cparams(2),
    )(s16t, dm16, w, m, v)


def _size(shape):
    n = 1
    for d in shape:
        n *= d
    return n


def _pack(arrs):
    pieces = []
    for a in arrs:
        flat = a.reshape(-1).astype(F32)
        pieces.append(jnp.pad(flat, (0, _round_up(flat.shape[0], LANE) - flat.shape[0])).reshape(-1, LANE))
    buf = jnp.concatenate(pieces, axis=0)
    return jnp.pad(buf, ((0, _round_up(buf.shape[0], SUBLANE) - buf.shape[0]), (0, 0)))


def _unpack(buf, shapes):
    lead = buf.shape[:-2]
    out, row = [], 0
    for s in shapes:
        n = _size(s)
        rows = _cdiv(n, LANE)
        piece = buf[..., row:row + rows, :].reshape(lead + (rows * LANE,))
        out.append(piece[..., :n].reshape(lead + tuple(s)))
        row += rows
    return out


def _adamw_many(name, ws, ms, vs, gs):
    n = len(ws)

    def body(*refs):
        for k in range(n):
            res = _f_adamw(refs[k][...], refs[n + k][...], refs[2 * n + k][...], refs[3 * n + k][...], 0.0)
            for j in range(4):
                refs[(4 + j) * n + k][...] = res[j]

    vmem = pl.BlockSpec(memory_space=pltpu.VMEM)
    res = _pcall(body, name=name, out_shape=[jax.ShapeDtypeStruct(w.shape, F32) for _ in range(4) for w in ws],
                 in_specs=[vmem] * (4 * n), out_specs=[vmem] * (4 * n))(*ws, *ms, *vs, *gs)
    return [tuple(res[j * n + k] for j in range(4)) for k in range(n)]


SHARD_AXIS = {
    "mod_w": 2, "ssd_w_in": 2, "ssd_conv_w": 2, "ssd_w_out": 1, "conf_w_pw1": 2, "conf_b_pw1": 1, "conf_w_dw": 2,
    "conf_b_dw": 1, "conf_ln_w": 1, "conf_ln_b": 1, "conf_w_pw2": 1, "conf_b_pw2": 1, "ffn_w_up": 2,
    "ffn_conv_w": 3, "ffn_w_down": 1,
}
BIG = ("ssd_w_in", "ssd_w_out", "conf_w_pw1", "conf_w_pw2", "ffn_w_up", "ffn_w_down")
WEIGHTS = ("c_ctx", "mod_w", "mod_b", "norm1_w", "norm2_w", "ssd_w_in", "ssd_conv_w", "ssd_conv_b", "ssd_dt_bias",
           "ssd_a_log", "ssd_d", "ssd_norm_w", "ssd_w_out", "conf_w_pw1", "conf_b_pw1", "conf_w_dw", "conf_b_dw",
           "conf_ln_w", "conf_ln_b", "conf_w_pw2", "conf_b_pw2", "ffn_w_up", "ffn_conv_w", "ffn_conv_b",
           "ffn_w_down", "final_norm_w")
SMALL = tuple(n for n in WEIGHTS if n not in BIG and n != "mod_w")
SMALL_SHARDED = tuple(n for n in SMALL if n in SHARD_AXIS)


def _unshard(stacked, axis):
    return jnp.concatenate([stacked[k] for k in range(N_CHIPS)], axis=axis)


def _to_blocks(full, axis):
    return jnp.stack(jnp.split(full, N_CHIPS, axis=axis))


def _par(v):
    v = v.reshape(-1, v.shape[-1])
    return v[:, None, :]


def kernel(x, c, ctx, c_ctx, mod_w, mod_b, norm1_w, norm2_w, ssd_w_in, ssd_conv_w, ssd_conv_b, ssd_dt_bias, ssd_a_log, ssd_d, ssd_norm_w, ssd_w_out, conf_w_pw1, conf_b_pw1, conf_w_dw, conf_b_dw, conf_ln_w, conf_ln_b, conf_w_pw2, conf_b_pw2, ffn_w_up, ffn_conv_w, ffn_conv_b, ffn_w_down, final_norm_w, loss_target, m_c_ctx, m_mod_w, m_mod_b, m_norm1_w, m_norm2_w, m_ssd_w_in, m_ssd_conv_w, m_ssd_conv_b, m_ssd_dt_bias, m_ssd_a_log, m_ssd_d, m_ssd_norm_w, m_ssd_w_out, m_conf_w_pw1, m_conf_b_pw1, m_conf_w_dw, m_conf_b_dw, m_conf_ln_w, m_conf_ln_b, m_conf_w_pw2, m_conf_b_pw2, m_ffn_w_up, m_ffn_conv_w, m_ffn_conv_b, m_ffn_w_down, m_final_norm_w, v_c_ctx, v_mod_w, v_mod_b, v_norm1_w, v_norm2_w, v_ssd_w_in, v_ssd_conv_w, v_ssd_conv_b, v_ssd_dt_bias, v_ssd_a_log, v_ssd_d, v_ssd_norm_w, v_ssd_w_out, v_conf_w_pw1, v_conf_b_pw1, v_conf_w_dw, v_conf_b_dw, v_conf_ln_w, v_conf_ln_b, v_conf_w_pw2, v_conf_b_pw2, v_ffn_w_up, v_ffn_conv_w, v_ffn_conv_b, v_ffn_w_down, v_final_norm_w):
    given = dict(locals())
    W = {n: given[n] for n in WEIGHTS}
    Mo = {n: given["m_" + n] for n in WEIGHTS}
    Vo = {n: given["v_" + n] for n in WEIGHTS}

    ax, ay, ac = lax.axis_index("x"), lax.axis_index("y"), lax.axis_index("c")
    chip = 2 * ax + ay
    dev = 4 * ax + 2 * ay + ac

    D = x.shape[-1]
    L, Lc = x.shape[1], ctx.shape[1]
    T0 = L + Lc
    H = ssd_a_log.shape[-1]
    DI = ssd_norm_w.shape[-1]
    P = DI // H
    CD = ssd_conv_b.shape[-1]
    N = SSD_STATE
    G = (CD - DI) // (2 * N)
    FH = ffn_conv_b.shape[-1]
    KS = ssd_conv_w.shape[1]
    KC = conf_w_dw.shape[1]
    ncc = Lc // SSD_CHUNK

    shard_b = {n: W[n].astype(BF16) for n in BIG}

    small_shard_shapes = [W[n].shape for n in SMALL_SHARDED]
    f1 = _allgather8("gather_small", _pack([c] + [W[n] for n in SMALL_SHARDED]))
    parts = _unpack(f1, [c.shape] + small_shard_shapes)
    Wf = dict(W)
    for n, p in zip(SMALL_SHARDED, parts[1:]):
        Wf[n] = _unshard(p[::2], SHARD_AXIS[n])
    c16 = jnp.concatenate([parts[0].reshape(N_DEV, D), c_ctx[None, :], jnp.zeros((16 - N_DEV - 1, D), F32)], axis=0)

    S_mod = mod_w.shape[-1]
    mod_b_shard = lax.dynamic_slice_in_dim(mod_b, chip * S_mod, S_mod, axis=1)[:, None, :]
    mod_part = _mod_fwd(c16, mod_w, mod_b_shard)
    f2 = _allgather8("gather_mod", mod_part.reshape(2 * 16, S_mod))
    mods = jnp.concatenate([f2[2 * k].reshape(2, 16, S_mod) for k in range(N_CHIPS)], axis=-1)
    my = lax.dynamic_slice_in_dim(mods, dev, 1, axis=1)[:, 0]
    sh1, sc1, g1, sh2, sc2, g2 = [[my[l, k * D:(k + 1) * D] for l in range(2)] for k in range(6)]
    csh1, csc1 = mods[0, N_DEV, 0:D], mods[0, N_DEV, D:2 * D]

    in_halves = shard_b["ssd_w_in"].reshape(2, D // 2, ssd_w_in.shape[-1])
    gather_a, token = _exchange4_start("gather_w_in_start", [in_halves], True, mods, half=True)
    csc1 = _tie("tie_gather_w_in", csc1, token)

    def full_weight(n, own, landed):
        if landed.ndim == own.ndim:
            ax = SHARD_AXIS[n]
            return lax.dynamic_update_slice_in_dim(landed, own, chip * own.shape[ax], ax)
        return _unshard(_fill_own(landed, own, chip, True), SHARD_AXIS[n])

    xl = x[0]
    rows0 = (ctx[0], xl)
    n1w0, n1w1 = _par(norm1_w[0]), _par(norm1_w[1])
    sc_seg = jnp.stack([csc1, sc1[0]])[:, None, :]
    sh_seg = jnp.stack([csh1, sh1[0]])[:, None, :]

    a0 = _rw_fwd("l0_modnorm1", _f_modnorm, [], [n1w0, sc_seg, sh_seg], [D], T=T0, seg_rows=(Lc,), head=rows0,
                 out_dtypes=[BF16])
    rest = [n for n in BIG if n != "ssd_w_in"]
    for n in rest:
        a0 = _tie("tie_cast_" + n, a0, shard_b[n])
    (own_in,), (landed_in,) = _exchange4_wait("gather_w_in_wait", gather_a, a0)
    mine = _fill_own(landed_in, lax.dynamic_index_in_dim(own_in, ac, 0, keepdims=False), chip, True)
    (halves,) = _swap_sibling("swap_w_in", [mine], by_core=True)
    halves = lax.dynamic_update_index_in_dim(halves, mine, ac, 0)
    w_in = jnp.concatenate([halves[:, k].reshape(D, -1) for k in range(N_CHIPS)], axis=1)
    landed_in = halves
    def start_gather(tag, names, dep):
        handle, tok = _exchange4_start("gather_" + tag + "_start", [shard_b[n] for n in names], True, dep,
                                       axes=[1 if SHARD_AXIS[n] == 1 else None for n in names])
        return (names, handle), tok

    def finish_gather(tag, group, after):
        names, handle = group
        return {n: full_weight(n, own, g)
                for n, own, g in zip(names, *_exchange4_wait("gather_" + tag + "_wait", handle, after))}

    gather_b, token = start_gather("mix", ["ssd_w_out", "conf_w_pw1", "conf_w_pw2"], landed_in)
    gather_c, token = start_gather("ffn", ["ffn_w_up", "ffn_w_down"], token)
    a0 = _tie("tie_gather_rest", a0, token)
    proj = _mm(a0, w_in, name="l0_w_in")
    seg_taps = [(k - KS // 2, ("seg", Lc)) for k in range(KS)]
    xbc_pre, xbc = _conv_fwd("l0_conv", proj, DI, CD, Wf["ssd_conv_w"][0], ssd_conv_b, seg_taps, act=True)
    dt_raw = proj[:, DI + CD:]
    dt_bias = _par(ssd_dt_bias.reshape(1, 2 * H))
    dt = _rw_fwd("l0_softplus", _f_softplus, [dt_raw], [dt_bias], [2 * H])
    dt_t = dt.T
    dtr = (dt_t[:H, None, :], dt_t[H:, None, :])
    a_all = -jnp.exp(ssd_a_log.reshape(2, H, 1, 1))
    a_neg = (a_all[0], a_all[1])
    (y_f, y_b), s_enter = _ssd_fwd(xbc, DI, DI + G * N, dtr, a_neg, P, ncc)
    gate_rows = [y_f, y_b, (xbc, 0, DI, Lc), (proj, 0, DI, Lc)]
    d_rep = _par(jnp.repeat(ssd_d[0], P))
    ssd_nw = _par(ssd_norm_w[0])
    yn = _rw_fwd("l0_ssd_gate", _f_ssd_gate, gate_rows, [d_rep, ssd_nw], [DI], T=L, out_dtypes=[BF16])
    Wb = finish_gather("mix", gather_b, yn)
    w_out, w_pw1, w_pw2 = Wb["ssd_w_out"][0], Wb["conf_w_pw1"][0], Wb["conf_w_pw2"][0]
    mix0 = _mm(yn, w_out, name="l0_w_out")
    g1_0, g2_0, g1_1, g2_1 = _par(g1[0]), _par(g2[0]), _par(g1[1]), _par(g2[1])
    h1 = _rw_fwd("l0_res1", _f_gate_res, [xl, mix0], [g1_0], [D])
    Wb = finish_gather("ffn", gather_c, h1)
    w_up, w_dn = Wb["ffn_w_up"], Wb["ffn_w_down"]

    grid_taps = [((i - 1) * GRID_W + (j - 1), (None if j == 1 else ("col", j - 1))) for i in range(3) for j in range(3)]

    def ffn_fwd(l, h, tag):
        a = _rw_fwd(tag + "_modnorm2", _f_modnorm, [h], [_par(norm2_w[l]), _par(sc2[l]), _par(sh2[l])], [D],
                    out_dtypes=[BF16])
        hh = _mm(a, w_up[l], name=tag + "_w_up")
        gc = _conv_fwd(tag + "_ffn_conv", hh, FH, FH, Wf["ffn_conv_w"][l].reshape(9, FH), ffn_conv_b[l][None, :],
                       grid_taps)
        act = _rw_fwd(tag + "_act", _f_ffn_act, [(hh, 0, FH), gc], [], [FH], col_tile=_tile(FH, 1536),
                      out_dtypes=[BF16])
        dn = _mm(act, w_dn[l], name=tag + "_w_down")
        return a, hh, gc, act, dn

    a1, hh0, gc0, act0, dn0 = ffn_fwd(0, h1, "l0")
    h2 = _rw_fwd("l0_res2", _f_gate_res, [h1, dn0], [g2_0], [D])

    a2 = _rw_fwd("l1_modnorm1", _f_modnorm, [h2], [n1w1, _par(sc1[1]), _par(sh1[1])], [D], out_dtypes=[BF16])
    pw = _mm(a2, w_pw1, name="l1_pw1")
    b_pw1 = Wf["conf_b_pw1"][0]
    glu = _rw_fwd("l1_glu", _f_glu, [(pw, 0, D), (pw, D, D)], [_par(b_pw1[:D]), _par(b_pw1[D:])], [D])
    conf_taps = [(k - KC // 2, None) for k in range(KC)]
    cv = _conv_fwd("l1_conv", glu, 0, D, Wf["conf_w_dw"][0], Wf["conf_b_dw"], conf_taps)
    ln_w, ln_b = _par(Wf["conf_ln_w"][0]), _par(Wf["conf_ln_b"][0])
    ls = _rw_fwd("l1_ln_silu", _f_ln_silu, [cv], [ln_w, ln_b], [D], out_dtypes=[BF16])
    p2 = _mm(ls, w_pw2, name="l1_pw2")
    b_pw2 = _par(Wf["conf_b_pw2"][0])
    h3 = _rw_fwd("l1_res1", _f_gate_res_bias, [h2, p2], [g1_1, b_pw2], [D])
    a3, hh1, gc1, act1, dn1 = ffn_fwd(1, h3, "l1")
    h4 = _rw_fwd("l1_res2", _f_gate_res, [h3, dn1], [g2_1], [D])

    fnw = final_norm_w[None, :]
    tgt = loss_target[0]
    loss_local = _loss_fwd(h4, tgt, fnw)[0, 0]

    G_full = {}
    reduces = {}

    def start_reduce(tag, items, dep):
        def blocks_of(g, ax):
            if g.ndim == 3:
                return g
            return g.reshape(N_CHIPS, g.shape[0] // N_CHIPS, g.shape[1]) if ax == 0 else _to_blocks(g, ax)

        blocks = [blocks_of(g, ax).astype(BF16) for _, g, ax in items]
        handle, tok = _exchange4_start("reduce_" + tag + "_start", blocks, False, dep)
        reduces[tag] = ([n for n, _, _ in items], handle)
        return tok
    ones = jnp.ones((L, 1), F32)
    (dh4,), (dfnw,) = _rw_bwd("loss_bwd", _f_loss_rows, [h4, tgt], [_par(final_norm_w)], [ones],
                              row_grad=[True, False], par_grad=[True])
    G_full["final_norm_w"] = dfnw.reshape(D)

    def ffn_bwd(l, h, saved, g2_l, dh_out, tag):
        a, hh, gc, act, dn = saved
        (ddn,), (dg2,) = _rw_bwd(tag + "_res2_bwd", _f_gate, [dn], [g2_l], [dh_out],
                                 row_grad=[True], par_grad=[True], row_dtypes=[BF16])
        dact = _mm(ddn, w_dn[l], tb=True, name=tag + "_w_down_dx")
        dwdn = _mm(act, ddn, ta=True, name=tag + "_w_down_dw", out_dtype=BF16)
        (dval, dgc), _ = _rw_bwd(tag + "_act_bwd", _f_ffn_act, [(hh, 0, FH), gc], [], [dact],
                                 row_grad=[True, True], par_grad=[], col_tile=_tile(FH, 1536), row_dtypes=[BF16, F32])
        dgin, dcw, dcb = _conv_bwd(tag + "_ffn_conv_bwd", hh, FH, FH, Wf["ffn_conv_w"][l].reshape(9, FH), dgc,
                                   grid_taps, du_dtype=BF16)
        dhh = jnp.concatenate([dval, dgin], axis=1)
        da = _mm(dhh, w_up[l], tb=True, name=tag + "_w_up_dx")
        dwup = _mm(a, dhh, ta=True, name=tag + "_w_up_dw", out_dtype=BF16, col_blocks=N_CHIPS)
        (dh,), (dn2w, dsc2, dsh2) = _rw_bwd(
            tag + "_modnorm2_bwd", _f_modnorm, [h], [_par(norm2_w[l]), _par(sc2[l]), _par(sh2[l])], [da],
            row_grad=[True], par_grad=[True, True, True], add=dh_out)
        return dh, dict(w_down=dwdn, w_up=dwup, conv_w=dcw.reshape(3, 3, FH), conv_b=dcb.reshape(FH),
                        n2w=dn2w.reshape(D), sc2=dsc2.reshape(D), sh2=dsh2.reshape(D), g2=dg2.reshape(D))

    dh3, gf1 = ffn_bwd(1, h3, (a3, hh1, gc1, act1, dn1), g2_1, dh4, "l1")
    (dp2,), (dg1_1, db_pw2) = _rw_bwd("l1_res1_bwd", _f_gate_bias, [p2], [g1_1, b_pw2], [dh3],
                                      row_grad=[True], par_grad=[True, True], row_dtypes=[BF16])
    dls = _mm(dp2, w_pw2, tb=True, name="l1_pw2_dx")
    dw_pw2 = _mm(ls, dp2, ta=True, name="l1_pw2_dw", out_dtype=BF16)
    (dcv,), (dln_w, dln_b) = _rw_bwd("l1_ln_silu_bwd", _f_ln_silu, [cv], [ln_w, ln_b], [dls],
                                     row_grad=[True], par_grad=[True, True])
    dglu, dw_dw, db_dw = _conv_bwd("l1_conv_bwd", glu, 0, D, Wf["conf_w_dw"][0], dcv, conf_taps)
    (dpa, dpg), (dba, dbg) = _rw_bwd("l1_glu_bwd", _f_glu, [(pw, 0, D), (pw, D, D)],
                                     [_par(b_pw1[:D]), _par(b_pw1[D:])], [dglu],
                                     row_grad=[True, True], par_grad=[True, True], row_dtypes=[BF16, BF16])
    dpw = jnp.concatenate([dpa, dpg], axis=1)
    da2 = _mm(dpw, w_pw1, tb=True, name="l1_pw1_dx")
    dw_pw1 = _mm(a2, dpw, ta=True, name="l1_pw1_dw", out_dtype=BF16, col_blocks=N_CHIPS)
    (dh2,), (dn1w1, dsc1_1, dsh1_1) = _rw_bwd(
        "l1_modnorm1_bwd", _f_modnorm, [h2], [n1w1, _par(sc1[1]), _par(sh1[1])], [da2],
        row_grad=[True], par_grad=[True, True, True], add=dh3)
    G_full["conf_b_pw2"] = db_pw2.reshape(1, D)
    G_full["conf_ln_w"], G_full["conf_ln_b"] = dln_w.reshape(1, D), dln_b.reshape(1, D)
    G_full["conf_w_dw"], G_full["conf_b_dw"] = dw_dw[None], db_dw.reshape(1, D)
    G_full["conf_b_pw1"] = jnp.concatenate([dba.reshape(1, D), dbg.reshape(1, D)], axis=1)

    token = start_reduce("l1", [("conf_w_pw2", dw_pw2, 0), ("conf_w_pw1", dw_pw1, 1), ("ffn_w_up1", gf1["w_up"], 1),
                                ("ffn_w_down1", gf1["w_down"], 0)], dw_pw2)
    dh2 = _tie("tie_reduce_l1", dh2, token)
    dh1, gf0 = ffn_bwd(0, h1, (a1, hh0, gc0, act0, dn0), g2_0, dh2, "l0")
    G_full["ffn_conv_w"] = jnp.stack([gf0["conv_w"], gf1["conv_w"]])
    G_full["ffn_conv_b"] = jnp.stack([gf0["conv_b"], gf1["conv_b"]])

    (dmix,), (dg1_0,) = _rw_bwd("l0_res1_bwd", _f_gate, [mix0], [g1_0], [dh1],
                                row_grad=[True], par_grad=[True], row_dtypes=[BF16])
    dyn = _mm(dmix, w_out, tb=True, name="l0_w_out_dx")
    dw_out = _mm(yn, dmix, ta=True, name="l0_w_out_dw", out_dtype=BF16)
    token = start_reduce("l0", [("ffn_w_up0", gf0["w_up"], 1), ("ffn_w_down0", gf0["w_down"], 0),
                                ("ssd_w_out", dw_out, 0)], dw_out)
    dyn = _tie("tie_reduce_l0", dyn, token)
    (dy_lat, dxs_gate, dz_lat), (dd_rep, dssd_nw) = _rw_bwd(
        "l0_ssd_gate_bwd", _f_ssd_gate, gate_rows, [d_rep, ssd_nw], [dyn],
        row_grad=[True, False, True, True], par_grad=[True, True], T=L, row_dtypes=[F32, F32, BF16])
    g_f, g_b = _ssd_bwd(xbc, DI, DI + G * N, dtr, a_neg, s_enter, dy_lat, P, ncc)
    silu_bwd = functools.partial(_rw_bwd, f=_silu, pars=[], row_grad=[True], par_grad=[], T=T0)
    (dxs_pre,), _ = silu_bwd("l0_silu_bwd_x", rows=[(xbc_pre, 0, DI)], cot_fn=lambda p, q, r: p + q + r,
                             cots=[g_f[0], g_b[0], (dxs_gate, 0, DI, -Lc)],
                             col_tile=_tile(DI, 1024))
    (db_pre,), _ = silu_bwd("l0_silu_bwd_b", rows=[(xbc_pre, DI, G * N)], cot_fn=lambda p, q: p + q,
                            cots=[g_f[1], g_b[1]], col_tile=_tile(G * N, 1024))
    (dc_pre,), _ = silu_bwd("l0_silu_bwd_c", rows=[(xbc_pre, DI + G * N, G * N)], cot_fn=lambda p, q: p + q,
                            cots=[g_f[2], g_b[2]], col_tile=_tile(G * N, 1024))
    conv_w0 = Wf["ssd_conv_w"][0]
    pieces = []
    for tag, off, width, g_pre in (("x", 0, DI, dxs_pre), ("b", DI, G * N, db_pre), ("c", DI + G * N, G * N, dc_pre)):
        pieces.append(_conv_bwd("l0_conv_bwd_" + tag, proj, DI + off, width, conv_w0[:, off:off + width], g_pre,
                                seg_taps, du_dtype=BF16))
    dconv_in = [p[0] for p in pieces]
    dcw0 = jnp.concatenate([p[1] for p in pieces], axis=1)
    dcb0 = jnp.concatenate([p[2] for p in pieces], axis=1)
    ddt = jnp.concatenate([g_f[3][:, 0, :].T, g_b[3][:, 0, :].T], axis=1)
    (ddt_raw,), (ddt_bias,) = _rw_bwd("l0_softplus_bwd", _f_softplus, [dt_raw], [dt_bias], [ddt],
                                      row_grad=[True], par_grad=[True], row_dtypes=[BF16])
    dproj = jnp.concatenate([jnp.pad(dz_lat, ((Lc, 0), (0, 0))), *dconv_in, ddt_raw], axis=1)
    da0 = _mm(dproj, w_in, tb=True, name="l0_w_in_dx")
    dw_in = _mm(a0, dproj, ta=True, name="l0_w_in_dw", out_dtype=BF16)
    token = start_reduce("in", [("ssd_w_in", dw_in, 1)], dw_in)
    da0 = _tie("tie_reduce_in", da0, token)
    (dhcat,), (dn1w0, dsc_seg, dsh_seg) = _rw_bwd(
        "l0_modnorm1_bwd", _f_modnorm, [], [n1w0, sc_seg, sh_seg], [da0], T=T0, head=rows0,
        row_grad=[True], par_grad=[True, True, True], seg_rows=(Lc,), add=(dh1, 0, D, -Lc), skip_rows=Lc)
    grad_x = dhcat[None]

    da_heads = jnp.stack([g[4][..., 0, 0].sum(axis=1).reshape(H) for g in (g_f, g_b)])[None]
    G_full["ssd_a_log"] = da_heads * (-jnp.exp(ssd_a_log))
    G_full["ssd_dt_bias"] = ddt_bias.reshape(1, 2, H)
    G_full["ssd_d"] = dd_rep.reshape(H, P).sum(axis=1)[None]
    G_full["ssd_norm_w"] = dssd_nw.reshape(1, DI)
    G_full["ssd_conv_w"], G_full["ssd_conv_b"] = dcw0[None], dcb0.reshape(1, CD)
    G_full["norm1_w"] = jnp.stack([dn1w0.reshape(D), dn1w1.reshape(D)])
    G_full["norm2_w"] = jnp.stack([gf0["n2w"], gf1["n2w"]])

    zD = jnp.zeros((D,), F32)
    dm_own = jnp.stack([
        jnp.concatenate([dsh_seg[1, 0], dsc_seg[1, 0], dg1_0.reshape(D), gf0["sh2"], gf0["sc2"], gf0["g2"]]),
        jnp.concatenate([dsh1_1.reshape(D), dsc1_1.reshape(D), dg1_1.reshape(D), gf1["sh2"], gf1["sc2"], gf1["g2"]]),
    ])
    dmc_own = jnp.concatenate([dsh_seg[0, 0], dsc_seg[0, 0], zD, zD, zD, zD])

    out = {}

    def finish_reduce(tags, after, swap_name):
        partial = {}
        for tag in tags:
            names, handle = reduces[tag]
            blocks, landed = _exchange4_wait("reduce_" + tag + "_wait", handle, after)
            for n, blk, own in zip(names, landed, blocks):
                r = _fill_own(blk, own, chip, False)
                partial[n] = _sum_leading("sum4_" + n, r.reshape(N_CHIPS, -1, r.shape[-1]),
                                          (0, 1, 2, 3), out_dtype=BF16).reshape(r.shape[1:])
        for n in ("ffn_w_up", "ffn_w_down"):
            if n + "0" in partial:
                partial[n] = jnp.stack([partial.pop(n + "0"), partial.pop(n + "1")])
        names = [n for n in BIG if n in partial]
        mine = [partial[n].reshape(W[n].shape) for n in names]
        for n, own, sib in zip(names, mine, _swap_sibling(swap_name, mine)):
            out[n] = _adamw("adamw_" + n, W[n], Mo[n], Vo[n], own, sib)
        return names

    early = finish_reduce(["l1", "l0"], dhcat, "swap_grads_early")

    small_sum_names = [n for n in SMALL if n not in ("c_ctx", "mod_b")]
    sum_part = [G_full[n] for n in small_sum_names] + [dmc_own, loss_local.reshape(1)]
    packed = _tie("tie_small_grads", _pack(sum_part + [dm_own]), out[early[-1]][1])
    gat = _allgather8("gather_small_grads", packed)
    total = _sum_leading("sum_small_grads", gat, tuple(range(N_DEV)))
    summed = _unpack(total, [a.shape for a in sum_part])
    Gs = dict(zip(small_sum_names, summed[:-2]))
    dmc_tot, loss = summed[-2], summed[-1][0]
    dm_all = _unpack(gat, [a.shape for a in sum_part] + [dm_own.shape])[-1].transpose(1, 0, 2)
    dm16 = jnp.concatenate([dm_all, jnp.stack([dmc_tot, jnp.zeros_like(dmc_tot)])[:, None, :],
                            jnp.zeros((2, 16 - N_DEV - 1, 6 * D), F32)], axis=1)
    Gs["mod_b"] = _sum_leading("sum_mod_b", dm16.transpose(1, 0, 2).reshape(16, 2 * 6 * D // LANE, LANE),
                               tuple(range(N_DEV + 1))).reshape(2, 6 * D)

    dm16_shard = lax.dynamic_slice_in_dim(dm16, chip * S_mod, S_mod, axis=2)
    ds16 = _mm(dm16_shard[0], mod_w.reshape(2 * D, S_mod), tb=True, precision=HIGHEST, name="c_ctx_dx")
    sig = jax.nn.sigmoid(c_ctx)
    dcc_part = ds16[N_DEV, :D] * (sig * (1.0 + c_ctx * (1.0 - sig)))
    gat_cc = _allgather8("gather_c_ctx_grad", _pack([dcc_part]))
    Gs["c_ctx"] = _sum_leading("sum_c_ctx_grad", gat_cc, (0, 2, 4, 6)).reshape(-1)[:D]

    s16t = _silu(c16).T
    out["mod_w"] = _mod_w_update(s16t, dm16_shard, mod_w, m_mod_w, v_mod_w)
    finish_reduce(["in"], out["mod_w"][0], "swap_grads_late")

    def own(n, full):
        if n in SHARD_AXIS:
            size = W[n].shape[SHARD_AXIS[n]]
            return lax.dynamic_slice_in_dim(full, chip * size, size, axis=SHARD_AXIS[n])
        return full

    def two_d(a):
        return a.reshape(1, -1) if a.ndim == 1 else a

    g_small = [own(n, Gs[n].reshape(Wf[n].shape)) for n in SMALL]
    res = _adamw_many("adamw_small", [two_d(W[n]) for n in SMALL], [two_d(Mo[n]) for n in SMALL],
                      [two_d(Vo[n]) for n in SMALL], [two_d(g) for g in g_small])
    for n, r in zip(SMALL, res):
        out[n] = tuple(t.reshape(W[n].shape) for t in r)

    grads = [out[n][0] for n in WEIGHTS]
    deltas = [out[n][1] for n in WEIGHTS]
    new_m = [out[n][2] for n in WEIGHTS]
    new_v = [out[n][3] for n in WEIGHTS]
    return (loss, grad_x, *grads, *deltas, *new_m, *new_v)
```

```python
import functools

import jax
import jax.numpy as jnp
from jax import lax
from jax.experimental import pallas as pl
from jax.experimental.pallas import tpu as pltpu

F32 = jnp.float32
BF16 = jnp.bfloat16
MESH = pl.DeviceIdType.MESH
HIGHEST = lax.Precision.HIGHEST

VMEM_LIMIT_BYTES = 48 * 1024 * 1024
LANE = 128
SUBLANE = 8

SSD_STATE = 128
SSD_CHUNK = 128
GRID_W = 64
EPS = 1e-6
N_CHIPS = 4
N_DEV = 8

ADAM_LR = 0.001
ADAM_B1 = 0.9
ADAM_B2 = 0.999
ADAM_EPS = 1e-08
ADAM_WD = 0.01
ADAM_STEP = 10


def _pcall(body, **kw):
    return pl.pallas_call(body, **kw)


def _cparams(n_grid):
    return pltpu.CompilerParams(dimension_semantics=("arbitrary",) * n_grid, vmem_limit_bytes=VMEM_LIMIT_BYTES)


def _cdiv(a, b):
    return -(-a // b)


def _round_up(a, b):
    return _cdiv(a, b) * b


def _tile(n, cap):
    if n <= cap:
        return n
    best = None
    for t in range(LANE, cap + 1, LANE):
        if n % t == 0:
            best = t
    if best is None:
        npad = _round_up(n, LANE)
        for t in range(LANE, cap + 1, LANE):
            if npad % t == 0:
                best = t
    return best


def _row_tile(n, cap, also=()):
    best = None
    for step in (2 * SUBLANE, SUBLANE):
        for t in range(step, min(cap, n) + 1, step):
            if n % t == 0 and all(a % t == 0 for a in also):
                best = t
        if best is not None:
            break
    assert best is not None, (n, cap, also)
    return best


def _silu(v):
    return v * jax.nn.sigmoid(v)


def _mm(a, b, *, name, ta=False, tb=False, precision=None, cap=1024, out_dtype=F32, col_blocks=None,
        b_lead=None):
    M, K = (a.shape[1], a.shape[0]) if ta else a.shape
    b_dims = b.shape if b_lead is None else b.shape[1:]
    N = b_dims[0] if tb else b_dims[1]
    assert K == (b_dims[1] if tb else b_dims[0]), (a.shape, b.shape, ta, tb)
    tm, tk = _tile(M, cap), _tile(K, cap + cap // 2)
    tn = _tile(N if col_blocks is None else N // col_blocks, cap + cap // 2)
    nm, nn, nk = _cdiv(M, tm), _cdiv(N, tn), _cdiv(K, tk)
    k_tail = K % tk
    exact = precision is not None

    def body(a_ref, b_ref, o_ref, acc_ref):
        k = pl.program_id(2)

        @pl.when(k == 0)
        def _():
            acc_ref[...] = jnp.zeros_like(acc_ref)

        av = a_ref[...]
        bv = b_ref[...]
        if k_tail:
            lim = K - k * tk
            ka = lax.broadcasted_iota(jnp.int32, av.shape, 0 if ta else 1)
            kb = lax.broadcasted_iota(jnp.int32, bv.shape, 1 if tb else 0)
            av = jnp.where(ka < lim, av, jnp.zeros_like(av))
            bv = jnp.where(kb < lim, bv, jnp.zeros_like(bv))
        if exact:
            av = av.astype(F32)
            bv = bv.astype(F32)
        else:
            av = av.astype(BF16)
            bv = bv.astype(BF16)
        dn = (((0 if ta else 1,), (1 if tb else 0,)), ((), ()))
        acc_ref[...] += lax.dot_general(av, bv, dn, preferred_element_type=F32, precision=precision)

        @pl.when(k == nk - 1)
        def _():
            o_ref[...] = acc_ref[...].astype(o_ref.dtype)

    a_spec = pl.BlockSpec((tk, tm), lambda i, j, k: (k, i)) if ta else pl.BlockSpec((tm, tk), lambda i, j, k: (i, k))
    b_spec = pl.BlockSpec((tn, tk), lambda i, j, k: (j, k)) if tb else pl.BlockSpec((tk, tn), lambda i, j, k: (k, j))
    if b_lead is not None:
        b_blk, b_map = b_spec.block_shape, b_spec.index_map
        b_spec = pl.BlockSpec((None,) + tuple(b_blk), lambda i, j, k: (b_lead,) + tuple(b_map(i, j, k)))
    if col_blocks is None:
        out_spec = pl.BlockSpec((tm, tn), lambda i, j, k: (i, j))
        out_shape = jax.ShapeDtypeStruct((M, N), out_dtype)
    else:
        per = (N // col_blocks) // tn
        assert per * tn * col_blocks == N, (N, col_blocks, tn)
        out_spec = pl.BlockSpec((None, tm, tn), lambda i, j, k: (j // per, i, j % per))
        out_shape = jax.ShapeDtypeStruct((col_blocks, M, N // col_blocks), out_dtype)
    return _pcall(
        body, name=name, grid=(nm, nn, nk), in_specs=[a_spec, b_spec], out_specs=out_spec, out_shape=out_shape,
        scratch_shapes=[pltpu.VMEM((tm, tn), F32)], compiler_params=_cparams(3),
    )(a, b)


def _norm_rows(rows):
    out = []
    for r in rows:
        if not isinstance(r, tuple):
            r = (r,)
        arr, off, width, roff = (r + (0, None, 0)[len(r) - 1:])
        out.append((arr, off, width if width is not None else arr.shape[1], roff))
    return out


def _rw_plan(T, rows, pars, seg_rows, col_tile, tm_cap):
    widths = [r[2] for r in rows]
    wmax = max(widths + [p.shape[-1] for p in pars] + [1])
    if col_tile is not None:
        assert all(w == widths[0] for w in widths) and all(p.shape[-1] == widths[0] for p in pars)
        ncol = widths[0] // col_tile
        assert ncol * col_tile == widths[0]
        wmax = col_tile
    else:
        ncol = 1
    cap = tm_cap if tm_cap is not None else max(SUBLANE, min(512, (512 * 1024) // wmax))
    tm = _row_tile(T, cap, also=tuple(seg_rows) + tuple(abs(r[3]) for r in rows if r[3]))
    bounds = tuple(s // tm for s in seg_rows)
    return widths, ncol, tm, bounds


def _rw_specs(rows, pars, ncol, tm, bounds, col_tile):
    def seg(i):
        s = 0
        for b in bounds:
            s = s + (i >= b).astype(jnp.int32)
        return s

    specs = []
    for arr, off, w, roff in rows:
        bw = col_tile if col_tile is not None else w
        assert off % bw == 0 and roff % tm == 0, (off, bw, roff, tm)
        specs.append(pl.BlockSpec((tm, bw), functools.partial(
            lambda j, i, ob, rb, last: (jnp.clip(i + rb, 0, last), ob + j),
            ob=off // bw, rb=roff // tm, last=arr.shape[0] // tm - 1)))
    for p in pars:
        bw = col_tile if col_tile is not None else p.shape[-1]
        if p.shape[0] > 1:
            specs.append(pl.BlockSpec((None, 1, bw), lambda j, i: (seg(i), 0, j)))
        else:
            specs.append(pl.BlockSpec((None, 1, bw), lambda j, i: (0, 0, j)))
    return specs, seg


def _head_rows(head):
    top, bottom = head
    return [(top, 0, None, 0), (bottom, 0, None, -top.shape[0])]


def _rw_fwd(name, f, rows, pars, out_widths, *, T=None, seg_rows=(), col_tile=None, tm_cap=None, out_dtypes=None,
            head=None):
    rows = _norm_rows((_head_rows(head) if head else []) + list(rows))
    T = rows[0][0].shape[0] if T is None else T
    widths, ncol, tm, bounds = _rw_plan(T, rows, pars, seg_rows, col_tile, tm_cap)
    in_specs, _ = _rw_specs(rows, pars, ncol, tm, bounds, col_tile)
    nr, npar, nout = len(rows), len(pars), len(out_widths)

    def body(*refs):
        vals = [r[...] for r in refs[:nr + npar]]
        if head:
            vals = [jnp.where(pl.program_id(1) < head[0].shape[0] // tm, vals[0], vals[1])] + vals[2:]
        outs = f(*vals)
        if not isinstance(outs, (tuple, list)):
            outs = (outs,)
        for o_ref, o in zip(refs[nr + npar:], outs):
            o_ref[...] = o.astype(o_ref.dtype)

    out_specs = [pl.BlockSpec((tm, col_tile if col_tile is not None else w), lambda j, i: (i, j)) for w in out_widths]
    res = _pcall(
        body, name=name, grid=(ncol, T // tm), in_specs=in_specs, out_specs=out_specs,
        out_shape=[jax.ShapeDtypeStruct((T, w), dt) for w, dt in zip(out_widths, out_dtypes or [F32] * nout)],
        compiler_params=_cparams(2),
    )(*[r[0] for r in rows], *pars)
    return res if nout > 1 else res[0]


def _rw_bwd(name, f, rows, pars, cots, *, row_grad, par_grad, T=None, seg_rows=(), col_tile=None, tm_cap=None,
            add=None, cot_fn=None, row_dtypes=None, head=None, skip_rows=0):
    rows = _norm_rows((_head_rows(head) if head else []) + list(rows))
    cots = _norm_rows(cots)
    T = rows[0][0].shape[0] if T is None else T
    extra = _norm_rows([add]) if add is not None else []
    all_rows = rows + cots + extra
    widths, ncol, tm, bounds = _rw_plan(T, all_rows, pars, tuple(seg_rows) + ((skip_rows,) if skip_rows else ()),
                                        col_tile, tm_cap)
    bounds = bounds[:len(seg_rows)]
    in_specs, seg = _rw_specs(all_rows, pars, ncol, tm, bounds, col_tile)
    nr, nc, ne, npar = len(rows), len(cots), len(extra), len(pars)
    skip = 1 if head else 0
    widths = widths[skip:]
    nrf = nr - skip
    row_idx = [k for k in range(nrf) if row_grad[k]]
    par_idx = [k for k in range(npar) if par_grad[k]]

    def body(*refs):
        i = pl.program_id(1)

        def zero_before(vals, ops):
            return [jnp.where(i + c[3] // tm >= 0, v, jnp.zeros_like(v)) if c[3] < 0 else v for v, c in zip(vals, ops)]

        row_vals = [r[...] for r in refs[:nr]]
        if head:
            row_vals = [jnp.where(i < head[0].shape[0] // tm, row_vals[0], row_vals[1])] + row_vals[2:]
        cot_vals = zero_before([r[...] for r in refs[nr:nr + nc]], cots)
        add_vals = zero_before([r[...] for r in refs[nr + nc:nr + nc + ne]], extra)
        par_vals = [r[...] for r in refs[nr + nc + ne:nr + nc + ne + npar]]
        out_refs = refs[nr + nc + ne + npar:]
        outs, vjp = jax.vjp(f, *row_vals, *par_vals)
        if cot_fn is not None:
            cot_vals = cot_fn(*cot_vals)
            if not isinstance(cot_vals, (tuple, list)):
                cot_vals = (cot_vals,)
        if isinstance(outs, (tuple, list)):
            grads = vjp(tuple(c.astype(o.dtype) for c, o in zip(cot_vals, outs)))
        else:
            grads = vjp(cot_vals[0].astype(outs.dtype))
        first_seg = i == 0
        for b in bounds:
            first_seg = first_seg | (i == b)
        for n, k in enumerate(row_idx):
            g = grads[k]
            if n == 0 and add_vals:
                g = g + add_vals[0]
            out_refs[n][...] = g.astype(out_refs[n].dtype)
        for n, k in enumerate(par_idx):
            g = grads[nrf + k]
            o_ref = out_refs[len(row_idx) + n]
            first = first_seg if pars[k].shape[0] > 1 else (i == 0)

            @pl.when(first)
            def _(o_ref=o_ref, g=g):
                o_ref[...] = g

            @pl.when(jnp.logical_not(first))
            def _(o_ref=o_ref, g=g):
                o_ref[...] += g

    out_specs, out_shape = [], []
    for k in row_idx:
        w = widths[k]
        out_specs.append(pl.BlockSpec((tm, col_tile if col_tile is not None else w),
                                      lambda j, i: (jnp.maximum(i - skip_rows // tm, 0), j)))
        out_shape.append(jax.ShapeDtypeStruct((T - skip_rows, w), row_dtypes[len(out_shape)] if row_dtypes else F32))
    for k in par_idx:
        p = pars[k]
        bw = col_tile if col_tile is not None else p.shape[-1]
        if p.shape[0] > 1:
            out_specs.append(pl.BlockSpec((None, 1, bw), lambda j, i: (seg(i), 0, j)))
        else:
            out_specs.append(pl.BlockSpec((None, 1, bw), lambda j, i: (0, 0, j)))
        out_shape.append(jax.ShapeDtypeStruct(p.shape, F32))
    res = _pcall(
        body, name=name, grid=(ncol, T // tm), in_specs=in_specs, out_specs=out_specs, out_shape=out_shape,
        compiler_params=_cparams(2),
    )(*[r[0] for r in all_rows], *pars)
    return list(res[:len(row_idx)]), list(res[len(row_idx):])


def _f_modnorm(h, w, sc, sh):
    y = h * lax.rsqrt(jnp.mean(h * h, axis=-1, keepdims=True) + EPS)
    return (y * w) * (1.0 + sc) + sh


def _f_gate_res(h, y, g):
    return h + g * y


def _f_gate_res_bias(h, y, g, b):
    return h + g * (y + b)


def _f_gate(y, g):
    return g * y


def _f_gate_bias(y, g, b):
    return g * (y + b)


def _f_ffn_act(val, gate):
    return _silu(gate) * val


def _f_softplus(raw, bias):
    v = raw + bias
    return jnp.maximum(v, 0.0) + jnp.log(1.0 + jnp.exp(-jnp.abs(v)))


def _f_ssd_gate(yf, yb, xs, z, d_rep, nw):
    y = (yf + yb + d_rep * xs) * _silu(z)
    return (y * lax.rsqrt(jnp.mean(y * y, axis=-1, keepdims=True) + EPS)) * nw


def _f_glu(a, g, ba, bg):
    return (a + ba) * jax.nn.sigmoid(g + bg)


def _f_ln_silu(h, w, b):
    mu = jnp.mean(h, axis=-1, keepdims=True)
    d = h - mu
    y = d * lax.rsqrt(jnp.mean(d * d, axis=-1, keepdims=True) + EPS)
    return _silu(y * w + b)


def _f_loss_rows(h, t, w):
    y = (h * lax.rsqrt(jnp.mean(h * h, axis=-1, keepdims=True) + EPS)) * w
    e = y - t
    return 0.5 * jnp.mean(e * e, axis=-1, keepdims=True)


def _f_adamw(w, m, v, ga, gb):
    g = ga.astype(F32) + gb
    m = ADAM_B1 * m + (1.0 - ADAM_B1) * g
    v = ADAM_B2 * v + (1.0 - ADAM_B2) * (g * g)
    m_hat = m / (1.0 - ADAM_B1 ** ADAM_STEP)
    v_hat = v / (1.0 - ADAM_B2 ** ADAM_STEP)
    delta = -ADAM_LR * (m_hat / (jnp.sqrt(v_hat) + ADAM_EPS) + ADAM_WD * w)
    return g, delta, m, v


def _adamw(name, w, m, v, ga, gb):
    shape = w.shape
    c = shape[-1]
    two_d = [t.reshape(-1, c) for t in (w, m, v, ga, gb)]
    rows = two_d[0].shape[0]
    pad = _round_up(rows, SUBLANE) - rows
    if pad:
        two_d = [jnp.pad(t, ((0, pad), (0, 0))) for t in two_d]
    outs = _rw_fwd(name, _f_adamw, two_d, [], [c] * 4)
    return tuple(o[:rows].reshape(shape) for o in outs)


def _sum_leading(name, x, idxs, out_dtype=F32):
    _, R, C = x.shape
    tm = _row_tile(R, max(SUBLANE, min(512, (512 * 1024) // C)))

    def body(x_ref, o_ref):
        acc = x_ref[idxs[0]].astype(F32)
        for k in idxs[1:]:
            acc = acc + x_ref[k].astype(F32)
        o_ref[...] = acc.astype(o_ref.dtype)

    return _pcall(
        body, name=name, grid=(R // tm,), in_specs=[pl.BlockSpec((x.shape[0], tm, C), lambda i: (0, i, 0))],
        out_specs=pl.BlockSpec((tm, C), lambda i: (i, 0)), out_shape=jax.ShapeDtypeStruct((R, C), out_dtype),
        compiler_params=_cparams(1),
    )(x)


def _loss_fwd(h, t, w):
    T, D = h.shape
    tm = _row_tile(T, 256)

    def body(h_ref, t_ref, w_ref, o_ref):
        i = pl.program_id(0)
        part = jnp.sum(_f_loss_rows(h_ref[...], t_ref[...], w_ref[...]), axis=0, keepdims=True)
        part = jnp.broadcast_to(part, (1, LANE))

        @pl.when(i == 0)
        def _():
            o_ref[...] = part

        @pl.when(i > 0)
        def _():
            o_ref[...] += part

    return _pcall(
        body, name="loss_fwd", grid=(T // tm,),
        in_specs=[pl.BlockSpec((tm, D), lambda i: (i, 0)), pl.BlockSpec((tm, D), lambda i: (i, 0)),
                  pl.BlockSpec((1, D), lambda i: (0, 0))],
        out_specs=pl.BlockSpec((1, LANE), lambda i: (0, 0)), out_shape=jax.ShapeDtypeStruct((1, LANE), F32),
        compiler_params=_cparams(1),
    )(h, t, w)


CONV_ROWS = 256
CONV_ROWS_FEW_TAPS = 1024
CONV_ACC_ELEMS = 16384


def _col_mask(arg, t):
    col = jnp.bitwise_and(t, GRID_W - 1)
    return (col != 0) if arg < 0 else (col != GRID_W - 1)


def _conv_plan(T, C, taps):
    seg = [m[1] for _, m in taps if m is not None and m[0] == "seg"]
    cap = CONV_ROWS_FEW_TAPS if len(taps) <= 9 else CONV_ROWS
    rc = next(r for r in (1024, 768, 512, 256, LANE) if r <= cap and T % r == 0)
    ct = next((t for t in (512, 256, LANE) if C % t == 0), C)
    reach = max(abs(s) for s, _ in taps)
    hb = next(h for h in (8, 16, 32, 64, 128, 256) if h >= reach and rc % h == 0)
    sub = max(2 * SUBLANE, min(rc, CONV_ACC_ELEMS // ct))
    boundary = None
    if seg:
        inside = seg[0] % rc
        boundary = (seg[0], (inside - reach, inside + reach) if inside else None)
    taps = [(s, None if (m is None or m[0] == "seg") else m[1]) for s, m in taps]
    return rc, ct, hb, sub, T // rc, C // ct, boundary, taps


def _seg_ok(boundary, i, rc, r0, n, s):
    if boundary is None or boundary[1] is None or s == 0 or r0 + n <= boundary[1][0] or r0 >= boundary[1][1]:
        return None
    t = i * rc + r0 + lax.broadcasted_iota(jnp.int32, (n, 1), 0)
    return (t >= boundary[0]) == ((t + s) >= boundary[0])


def _halo_specs(rc, ct, hb, T, off_blocks):
    per = rc // hb
    last = T // hb - 1
    prev = pl.BlockSpec((hb, ct), lambda j, i: (jnp.maximum(i * per - 1, 0), off_blocks + j))
    cur = pl.BlockSpec((rc, ct), lambda j, i: (i, off_blocks + j))
    nxt = pl.BlockSpec((hb, ct), lambda j, i: (jnp.minimum((i + 1) * per, last), off_blocks + j))
    return [prev, cur, nxt]


def _fill_halo(pad_ref, p_ref, c_ref, n_ref, i, nrc, rc, hb, boundary):
    has_prev = i > 0
    has_next = i < nrc - 1
    if boundary is not None:
        has_prev = has_prev & (i * rc != boundary[0])
        has_next = has_next & ((i + 1) * rc != boundary[0])
    pad_ref[0:hb, :] = jnp.where(has_prev, p_ref[...], 0.0)
    pad_ref[hb:hb + rc, :] = c_ref[...]
    pad_ref[hb + rc:hb + rc + hb, :] = jnp.where(has_next, n_ref[...], 0.0)


def _shift_plan(keys):
    count = {}
    for s, m in keys:
        k = (s % SUBLANE, m)
        count[k] = count.get(k, 0) + 1
    slots = {}
    for k, n in sorted(count.items(), key=lambda kv: (kv[0][0], str(kv[0][1]))):
        if k != (0, None) and (n >= 2 or k[1] is not None):
            slots[k] = len(slots)
    return slots


def _build_shifted(copies_ref, slots, pad_ref, keys, i, rc, hb, sub):
    for (r, m), slot in slots.items():
        qs = [s - r for s, mk in keys if (s % SUBLANE, mk) == (r, m)]
        lo, hi = hb + min(qs), hb + rc + max(qs)
        for p in range(lo, hi, sub):
            n = min(sub, hi - p)
            v = pad_ref[p + r:p + r + n, :]
            if m is not None:
                t = i * rc - hb + p + r + lax.broadcasted_iota(jnp.int32, (n, 1), 0)
                v = jnp.where(_col_mask(m, t), v, 0.0)
            copies_ref[slot, p:p + n, :] = v


def _read(copies_ref, slots, pad_ref, s, m, row, n):
    k = (s % SUBLANE, m)
    if k in slots:
        q = s - k[0]
        return copies_ref[slots[k], row + q:row + q + n, :]
    return pad_ref[row + s:row + s + n, :]


def _conv_fwd(name, u, col_off, C, w, b, taps, act=False):
    T = u.shape[0]
    rc, ct, hb, sub, nrc, ncc, boundary, taps = _conv_plan(T, C, taps)
    assert col_off % ct == 0
    K = len(taps)
    keys = [(s, None) for s, _ in taps]
    slots = _shift_plan(keys)
    dirs = sorted({m for _, m in taps if m is not None})

    def body(up, uc, un, w_ref, b_ref, *rest):
        y_ref = rest[0]
        pad_ref, copies_ref = rest[-2], rest[-1]
        i = pl.program_id(1)
        _fill_halo(pad_ref, up, uc, un, i, nrc, rc, hb, boundary)
        _build_shifted(copies_ref, slots, pad_ref, keys, i, rc, hb, sub)
        for r0 in range(0, rc, sub):
            acc = jnp.broadcast_to(b_ref[...], (sub, ct))
            for m in [None] + dirs:
                part = None
                for k, (s, mk) in enumerate(taps):
                    if mk != m:
                        continue
                    v = _read(copies_ref, slots, pad_ref, s, None, hb + r0, sub)
                    ok = _seg_ok(boundary, i, rc, r0, sub, s)
                    term = w_ref[k:k + 1, :] * (v if ok is None else jnp.where(ok, v, 0.0))
                    part = term if part is None else part + term
                if part is None:
                    continue
                if m is not None:
                    t = i * rc + r0 + lax.broadcasted_iota(jnp.int32, (sub, 1), 0)
                    part = jnp.where(_col_mask(m, t), part, 0.0)
                acc = acc + part
            y_ref[r0:r0 + sub, :] = acc
            if act:
                rest[1][r0:r0 + sub, :] = _silu(acc)

    n_out = 2 if act else 1
    res = _pcall(
        body, name=name, grid=(ncc, nrc),
        in_specs=_halo_specs(rc, ct, hb, T, col_off // ct) + [pl.BlockSpec((K, ct), lambda j, i: (0, j)),
                                                              pl.BlockSpec((1, ct), lambda j, i: (0, j))],
        out_specs=[pl.BlockSpec((rc, ct), lambda j, i: (i, j))] * n_out,
        out_shape=[jax.ShapeDtypeStruct((T, C), F32)] * n_out,
        scratch_shapes=[pltpu.VMEM((rc + 2 * hb, ct), F32), pltpu.VMEM((max(len(slots), 1), rc + 2 * hb, ct), F32)],
        compiler_params=_cparams(2),
    )(u, u, u, w, b)
    return res if act else res[0]


def _conv_bwd(name, u, col_off, C, w, g, taps, du_dtype=F32):
    T = u.shape[0]
    rc, ct, hb, sub, nrc, ncc, boundary, taps = _conv_plan(T, C, taps)
    K = len(taps)
    u_keys = [(s, None) for s, _ in taps]
    dirs = sorted({m for _, m in taps if m is not None})
    g_keys = [(-s, m) for s, m in taps] + [(0, m) for m in dirs]
    u_slots, g_slots = _shift_plan(u_keys), _shift_plan(g_keys)

    def body(up, uc, un, gp, gc, gn, w_ref, du_ref, dw_ref, db_ref, upad, gpad, ucopies, gcopies):
        i = pl.program_id(1)
        _fill_halo(upad, up, uc, un, i, nrc, rc, hb, boundary)
        _fill_halo(gpad, gp, gc, gn, i, nrc, rc, hb, boundary)
        _build_shifted(ucopies, u_slots, upad, u_keys, i, rc, hb, sub)
        _build_shifted(gcopies, g_slots, gpad, g_keys, i, rc, hb, sub)

        @pl.when(i == 0)
        def _():
            dw_ref[...] = jnp.zeros_like(dw_ref)
            db_ref[...] = jnp.zeros_like(db_ref)

        def fold(v):
            return jnp.sum(v.reshape(sub // SUBLANE, SUBLANE, ct), axis=0)

        dbs = jnp.zeros((SUBLANE, ct), F32)
        for r0 in range(0, rc, sub):
            dbs = dbs + fold(gpad[hb + r0:hb + r0 + sub, :])
            acc = jnp.zeros((sub, ct), F32)
            for k, (s, m) in enumerate(taps):
                v = _read(gcopies, g_slots, gpad, -s, m, hb + r0, sub)
                ok = _seg_ok(boundary, i, rc, r0, sub, -s)
                acc = acc + w_ref[k:k + 1, :] * (v if ok is None else jnp.where(ok, v, 0.0))
            du_ref[r0:r0 + sub, :] = acc.astype(du_ref.dtype)
        db_ref[...] += jnp.sum(dbs, axis=0, keepdims=True)
        for k, (s, m) in enumerate(taps):
            part = jnp.zeros((SUBLANE, ct), F32)
            for r0 in range(0, rc, sub):
                v = _read(ucopies, u_slots, upad, s, None, hb + r0, sub)
                ok = _seg_ok(boundary, i, rc, r0, sub, s)
                part = part + fold(_read(gcopies, g_slots, gpad, 0, m, hb + r0, sub)
                                   * (v if ok is None else jnp.where(ok, v, 0.0)))
            dw_ref[k:k + 1, :] += jnp.sum(part, axis=0, keepdims=True)

    halo_u = _halo_specs(rc, ct, hb, T, col_off // ct)
    halo_g = _halo_specs(rc, ct, hb, T, 0)
    rows = rc + 2 * hb
    return _pcall(
        body, name=name, grid=(ncc, nrc),
        in_specs=halo_u + halo_g + [pl.BlockSpec((K, ct), lambda j, i: (0, j))],
        out_specs=[pl.BlockSpec((rc, ct), lambda j, i: (i, j)), pl.BlockSpec((K, ct), lambda j, i: (0, j)),
                   pl.BlockSpec((1, ct), lambda j, i: (0, j))],
        out_shape=[jax.ShapeDtypeStruct((T, C), du_dtype), jax.ShapeDtypeStruct((K, C), F32),
                   jax.ShapeDtypeStruct((1, C), F32)],
        scratch_shapes=[pltpu.VMEM((rows, ct), F32), pltpu.VMEM((rows, ct), F32),
                        pltpu.VMEM((max(len(u_slots), 1), rows, ct), F32),
                        pltpu.VMEM((max(len(g_slots), 1), rows, ct), F32)],
        compiler_params=_cparams(2),
    )(u, u, u, g, g, g, w)


def _ssd_group(xg, bm, cm, s_in, *per_head, reverse, P):
    R = len(per_head) // 2
    dtrs, a_s = per_head[:R], per_head[R:]
    q, rp = xg.shape
    ii = lax.broadcasted_iota(jnp.int32, (q, q), 0)
    jj = lax.broadcasted_iota(jnp.int32, (q, q), 1)
    causal = (jj >= ii) if reverse else (jj <= ii)
    causal_t = (ii >= jj) if reverse else (ii <= jj)
    eye = ii == jj
    lane = lax.broadcasted_iota(jnp.int32, (1, rp), 1)
    row = lax.broadcasted_iota(jnp.int32, (rp, 1), 0)
    nt = (((1,), (1,)), ((), ()))
    tn = (((0,), (0,)), ((), ()))
    cb = lax.dot_general(cm.astype(BF16), bm.astype(BF16), nt, preferred_element_type=F32)
    dt_x = jnp.zeros((q, rp), F32)
    acum_x = jnp.zeros((q, rp), F32)
    tot_row = jnp.zeros((1, rp), F32)
    tot_col = jnp.zeros((rp, 1), F32)
    wts, lane_masks = [], []
    for r in range(R):
        hm = (lane >= r * P) & (lane < (r + 1) * P)
        hc = (row >= r * P) & (row < (r + 1) * P)
        dt_c = jnp.sum(jnp.where(eye, dtrs[r], 0.0), axis=1, keepdims=True)
        dac = dt_c * a_s[r]
        dar = dtrs[r] * a_s[r]
        acum_c = jnp.sum(jnp.where(causal, dar, 0.0), axis=1, keepdims=True)
        acum_r = jnp.sum(jnp.where(causal_t, dac, 0.0), axis=0, keepdims=True)
        decay = jnp.where(causal, jnp.exp(jnp.where(causal, acum_c - acum_r, 0.0)), 0.0)
        tot = jnp.sum(dac, axis=0, keepdims=True)
        dt_x = jnp.where(hm, dt_c, dt_x)
        acum_x = jnp.where(hm, acum_c, acum_x)
        tot_row = jnp.where(hm, tot, tot_row)
        tot_col = jnp.where(hc, tot, tot_col)
        wts.append((cb * decay).astype(BF16))
        lane_masks.append(hm)
    xdt = xg * dt_x
    xdt_b = xdt.astype(BF16)
    y = jnp.zeros((q, rp), F32)
    for r in range(R):
        y = jnp.where(lane_masks[r], jnp.dot(wts[r], xdt_b, preferred_element_type=F32), y)
    dte = jnp.exp(tot_row - acum_x)
    cs = lax.dot_general((xdt * dte).astype(BF16), bm.astype(BF16), tn, preferred_element_type=F32)
    y = y + lax.dot_general(cm.astype(BF16), s_in.astype(BF16), nt, preferred_element_type=F32) * jnp.exp(acum_x)
    s_out = jnp.exp(tot_col) * s_in + cs
    return y, s_out


def _ssd_group_state(xg, bm, s_in, *per_head, reverse, P):
    R = len(per_head) // 2
    dtrs, a_s = per_head[:R], per_head[R:]
    q, rp = xg.shape
    ii = lax.broadcasted_iota(jnp.int32, (q, q), 0)
    jj = lax.broadcasted_iota(jnp.int32, (q, q), 1)
    causal = (jj >= ii) if reverse else (jj <= ii)
    eye = ii == jj
    lane = lax.broadcasted_iota(jnp.int32, (1, rp), 1)
    row = lax.broadcasted_iota(jnp.int32, (rp, 1), 0)
    dt_x = jnp.zeros((q, rp), F32)
    acum_x = jnp.zeros((q, rp), F32)
    tot_row = jnp.zeros((1, rp), F32)
    tot_col = jnp.zeros((rp, 1), F32)
    for r in range(R):
        hm = (lane >= r * P) & (lane < (r + 1) * P)
        hc = (row >= r * P) & (row < (r + 1) * P)
        dt_c = jnp.sum(jnp.where(eye, dtrs[r], 0.0), axis=1, keepdims=True)
        acum_c = jnp.sum(jnp.where(causal, dtrs[r] * a_s[r], 0.0), axis=1, keepdims=True)
        tot = jnp.sum(dt_c * a_s[r], axis=0, keepdims=True)
        dt_x = jnp.where(hm, dt_c, dt_x)
        acum_x = jnp.where(hm, acum_c, acum_x)
        tot_row = jnp.where(hm, tot, tot_row)
        tot_col = jnp.where(hc, tot, tot_col)
    xe = xg * dt_x * jnp.exp(tot_row - acum_x)
    cs = lax.dot_general(xe.astype(BF16), bm.astype(BF16), (((0,), (0,)), ((), ())), preferred_element_type=F32)
    return jnp.exp(tot_col) * s_in + cs


def _ssd_maps(NC, ncc, reverse_steps):
    def chunk(d, s):
        if reverse_steps:
            s = NC - 1 - s
        return s if d == 0 else jnp.where(s < ncc, ncc - 1 - s, NC - 1 - s + ncc)

    def lat_chunk(d, s):
        c = chunk(d, s) - ncc
        return jnp.where(c < 0, 0 if d == 0 else NC - ncc - 1, c)

    def step(s):
        return NC - 1 - s if reverse_steps else s

    return chunk, lat_chunk, step


SSD_GROUPS_PER_STEP = 2


def _ssd_specs(chunk, d, GB, R, Q, N, RP, b_off, c_off):
    assert b_off % (GB * N) == 0 and c_off % (GB * N) == 0
    bo, co = b_off // (GB * N), c_off // (GB * N)
    return [
        pl.BlockSpec((Q, GB * RP), lambda g, s: (chunk(d, s), g)),
        pl.BlockSpec((Q, GB * N), lambda g, s: (chunk(d, s), bo + g)),
        pl.BlockSpec((Q, GB * N), lambda g, s: (chunk(d, s), co + g)),
        pl.BlockSpec((GB * R, 1, Q), lambda g, s: (g, 0, chunk(d, s))),
        pl.BlockSpec((GB * R, 1, 1), lambda g, s: (g, 0, 0)),
    ]


def _ssd_fwd(xbc, b_off, c_off, dtr, a, P, ncc):
    T = xbc.shape[0]
    H = dtr[0].shape[0]
    N, Q = SSD_STATE, SSD_CHUNK
    NC = T // Q
    G = (c_off - b_off) // N
    R = H // G
    RP = R * P
    GB = SSD_GROUPS_PER_STEP if G % SSD_GROUPS_PER_STEP == 0 else 1
    chunk, lat_chunk, _ = _ssd_maps(NC, ncc, False)

    def body(*refs):
        s = pl.program_id(1)
        s_ref = refs[-1]

        @pl.when(s == 0)
        def _():
            s_ref[...] = jnp.zeros_like(s_ref)

        for d in range(2):
            x_ref, b_ref, c_ref, dtr_ref, a_ref = refs[5 * d:5 * d + 5]
            y_ref, se_ref = refs[10 + 2 * d:12 + 2 * d]
            for gg in range(GB):
                cols, bcols = slice(gg * RP, (gg + 1) * RP), slice(gg * N, (gg + 1) * N)
                s_in = s_ref[d, gg]
                se_ref[gg] = s_in
                per_head = [dtr_ref[gg * R + r] for r in range(R)] + [a_ref[gg * R + r] for r in range(R)]

                @pl.when(s >= ncc)
                def _(d=d, gg=gg, cols=cols, bcols=bcols, x_ref=x_ref, b_ref=b_ref, c_ref=c_ref, y_ref=y_ref,
                      s_in=s_in, per_head=per_head):
                    y, s_out = _ssd_group(x_ref[:, cols], b_ref[:, bcols], c_ref[:, bcols], s_in, *per_head,
                                          reverse=d == 1, P=P)
                    y_ref[:, cols] = y
                    s_ref[d, gg] = s_out

                @pl.when(s < ncc)
                def _(d=d, gg=gg, cols=cols, bcols=bcols, x_ref=x_ref, b_ref=b_ref, s_in=s_in, per_head=per_head):
                    s_ref[d, gg] = _ssd_group_state(x_ref[:, cols], b_ref[:, bcols], s_in, *per_head,
                                                    reverse=d == 1, P=P)

    in_specs, out_specs, out_shape, operands = [], [], [], []
    for d in range(2):
        in_specs += _ssd_specs(chunk, d, GB, R, Q, N, RP, b_off, c_off)
        operands += [xbc, xbc, xbc, dtr[d], a[d]]
        out_specs += [pl.BlockSpec((Q, GB * RP), functools.partial(lambda g, s, d: (lat_chunk(d, s), g), d=d)),
                      pl.BlockSpec((GB, None, RP, N), lambda g, s: (g, s, 0, 0))]
        out_shape += [jax.ShapeDtypeStruct((T - ncc * Q, H * P), F32), jax.ShapeDtypeStruct((G, NC, RP, N), F32)]
    y_f, se_f, y_b, se_b = _pcall(
        body, name="ssd_fwd", grid=(G // GB, NC), in_specs=in_specs, out_specs=out_specs, out_shape=out_shape,
        scratch_shapes=[pltpu.VMEM((2, GB, RP, N), F32)], compiler_params=_cparams(2),
    )(*operands)
    return (y_f, y_b), (se_f, se_b)


def _ssd_bwd(xbc, b_off, c_off, dtr, a, s_enter, dy, P, ncc):
    T = xbc.shape[0]
    H = dtr[0].shape[0]
    N, Q = SSD_STATE, SSD_CHUNK
    NC = T // Q
    G = (c_off - b_off) // N
    R = H // G
    RP = R * P
    GB = SSD_GROUPS_PER_STEP if G % SSD_GROUPS_PER_STEP == 0 else 1
    chunk, lat_chunk, step = _ssd_maps(NC, ncc, True)
    n_in, n_out = 7, 5

    def body(*refs):
        s = pl.program_id(1)
        ds_ref = refs[-1]

        @pl.when(s == 0)
        def _():
            ds_ref[...] = jnp.zeros_like(ds_ref)

        for d in range(2):
            x_ref, b_ref, c_ref, dtr_ref, a_ref, se_ref, dy_ref = refs[n_in * d:n_in * (d + 1)]
            dx_ref, db_ref, dc_ref, ddtr_ref, da_ref = refs[2 * n_in + n_out * d:2 * n_in + n_out * (d + 1)]
            for gg in range(GB):
                cols, bcols = slice(gg * RP, (gg + 1) * RP), slice(gg * N, (gg + 1) * N)
                per_head = [dtr_ref[gg * R + r] for r in range(R)] + [a_ref[gg * R + r] for r in range(R)]

                def store(grads, dx_ref=dx_ref, db_ref=db_ref, ddtr_ref=ddtr_ref, da_ref=da_ref, d=d, gg=gg,
                          cols=cols, bcols=bcols):
                    dx_ref[:, cols] = grads[0]
                    db_ref[:, bcols] = grads[1]
                    ds_ref[d, gg] = grads[2]
                    for r in range(R):
                        ddtr_ref[gg * R + r] = grads[3 + r]
                        da_ref[gg, r] = jnp.broadcast_to(grads[3 + R + r], (SUBLANE, LANE))

                @pl.when(s < NC - ncc)
                def _(d=d, gg=gg, cols=cols, bcols=bcols, x_ref=x_ref, b_ref=b_ref, c_ref=c_ref, se_ref=se_ref,
                      dy_ref=dy_ref, dc_ref=dc_ref, per_head=per_head, store=store):
                    f = functools.partial(_ssd_group, reverse=d == 1, P=P)
                    _, vjp = jax.vjp(f, x_ref[:, cols], b_ref[:, bcols], c_ref[:, bcols], se_ref[gg], *per_head)
                    grads = vjp((dy_ref[:, cols], ds_ref[d, gg]))
                    dc_ref[:, bcols] = grads[2]
                    store(grads[:2] + grads[3:])

                @pl.when(s >= NC - ncc)
                def _(d=d, gg=gg, cols=cols, bcols=bcols, x_ref=x_ref, b_ref=b_ref, se_ref=se_ref, dc_ref=dc_ref,
                      per_head=per_head, store=store):
                    f = functools.partial(_ssd_group_state, reverse=d == 1, P=P)
                    _, vjp = jax.vjp(f, x_ref[:, cols], b_ref[:, bcols], se_ref[gg], *per_head)
                    dc_ref[:, bcols] = jnp.zeros((Q, N), F32)
                    store(vjp(ds_ref[d, gg]))

    in_specs, out_specs, out_shape, operands = [], [], [], []
    for d in range(2):
        in_specs += _ssd_specs(chunk, d, GB, R, Q, N, RP, b_off, c_off) + [
            pl.BlockSpec((GB, None, RP, N), lambda g, s: (g, step(s), 0, 0)),
            pl.BlockSpec((Q, GB * RP), functools.partial(lambda g, s, d: (lat_chunk(d, s), g), d=d)),
        ]
        operands += [xbc, xbc, xbc, dtr[d], a[d], s_enter[d], dy]
    for d in range(2):
        at_chunk = functools.partial(lambda g, s, d: (chunk(d, s), g), d=d)
        out_specs += [
            pl.BlockSpec((Q, GB * RP), at_chunk), pl.BlockSpec((Q, GB * N), at_chunk),
            pl.BlockSpec((Q, GB * N), at_chunk),
            pl.BlockSpec((GB * R, 1, Q), functools.partial(lambda g, s, d: (g, 0, chunk(d, s)), d=d)),
            pl.BlockSpec((GB, None, R, SUBLANE, LANE), lambda g, s: (g, s, 0, 0, 0)),
        ]
        out_shape += [
            jax.ShapeDtypeStruct((T, H * P), F32), jax.ShapeDtypeStruct((T, G * N), F32),
            jax.ShapeDtypeStruct((T, G * N), F32), jax.ShapeDtypeStruct((H, 1, T), F32),
            jax.ShapeDtypeStruct((G, NC, R, SUBLANE, LANE), F32),
        ]
    res = _pcall(
        body, name="ssd_bwd", grid=(G // GB, NC), in_specs=in_specs, out_specs=out_specs, out_shape=out_shape,
        scratch_shapes=[pltpu.VMEM((2, GB, RP, N), F32)], compiler_params=_cparams(2),
    )(*operands)
    return res[:n_out], res[n_out:]


def _allgather8(name, v):
    R, C = v.shape

    def body(x_ref, out_ref, send_sems, recv_sems, local_sem):
        x, y, c = lax.axis_index("x"), lax.axis_index("y"), lax.axis_index("c")
        me, sibling = (x, y, c), (x, y, 1 - c)
        chips = [(1 - x, y), (x, 1 - y), (1 - x, 1 - y)]

        def slot(px, py, pc):
            return out_ref.at[4 * px + 2 * py + pc]

        def copy(k, block, to, src=None):
            return pltpu.make_async_remote_copy(
                src_ref=slot(*block) if src is None else src, dst_ref=slot(*block),
                send_sem=send_sems.at[k], recv_sem=recv_sems.at[k], device_id=to, device_id_type=MESH)

        mine = pltpu.make_async_copy(x_ref, slot(*me), local_sem)
        mine.start()
        first = [copy(0, me, sibling, src=x_ref)]
        first += [copy(1 + j, me, (*chip, c), src=x_ref) for j, chip in enumerate(chips)]
        for cp in first:
            cp.start()
        passed = [copy(4 + j, (*chip, c), sibling) for j, chip in enumerate(chips)]
        for j, chip in enumerate(chips):
            copy(1 + j, (*chip, c), me).wait_recv()
            passed[j].start()
        copy(0, sibling, me).wait_recv()
        for j, chip in enumerate(chips):
            copy(4 + j, (*chip, 1 - c), me).wait_recv()
        for cp in first + passed:
            cp.wait_send()
        mine.wait()

    return _pcall(
        body, name=name, out_shape=jax.ShapeDtypeStruct((N_DEV, R, C), v.dtype),
        in_specs=[pl.BlockSpec(memory_space=pltpu.VMEM)], out_specs=pl.BlockSpec(memory_space=pltpu.VMEM),
        scratch_shapes=[pltpu.SemaphoreType.DMA((7,)), pltpu.SemaphoreType.DMA((7,)), pltpu.SemaphoreType.DMA],
        compiler_params=pltpu.CompilerParams(vmem_limit_bytes=VMEM_LIMIT_BYTES),
    )(v)


def _slot(ref, k, axis, size):
    if axis is None:
        return ref.at[k]
    align = LANE if size % LANE == 0 else 2 * SUBLANE
    assert size % align == 0
    return ref.at[(slice(None),) * axis + (pl.ds(pl.multiple_of(k * size, align), size),)]


def _exchange4_start(name, srcs, bcast, dep, axes=None, half=False):
    n = len(srcs)
    axes = list(axes) if axes is not None else [None] * n
    sizes = [None if ax is None else s.shape[ax] for s, ax in zip(srcs, axes)]

    def land_shape(s, ax):
        if not bcast:
            return s.shape
        if half:
            return (N_CHIPS,) + s.shape[1:]
        if ax is None:
            return (N_CHIPS,) + s.shape
        return s.shape[:ax] + (N_CHIPS * s.shape[ax],) + s.shape[ax + 1:]

    lands = [lax.empty(land_shape(s, ax), s.dtype) for s, ax in zip(srcs, axes)]

    def body(*refs):
        src, land = refs[:n], refs[n:2 * n]
        send_sems, recv_sems = refs[2 * n + 1], refs[2 * n + 2]
        token = refs[-1]
        x, y, c = lax.axis_index("x"), lax.axis_index("y"), lax.axis_index("c")
        me = 2 * x + y
        for a in range(n):
            for j, (px, py) in enumerate([(1 - x, y), (x, 1 - y), (1 - x, 1 - y)]):
                pltpu.make_async_remote_copy(
                    src_ref=(src[a].at[c] if half else src[a]) if bcast else src[a].at[2 * px + py],
                    dst_ref=_slot(land[a], me, axes[a], sizes[a]),
                    send_sem=send_sems.at[3 * a + j], recv_sem=recv_sems.at[3 * a + j], device_id=(px, py, c),
                    device_id_type=MESH).start()
        token[...] = jnp.zeros_like(token)

    hbm = pl.BlockSpec(memory_space=pltpu.HBM)
    sem = pl.BlockSpec(memory_space=pltpu.SEMAPHORE)
    outs = _pcall(
        body, name=name,
        out_shape=(pltpu.SemaphoreType.DMA((3 * n,)), pltpu.SemaphoreType.DMA((3 * n,)),
                   *[pltpu.HBM(s.shape, s.dtype) for s in srcs], *[pltpu.HBM(l.shape, l.dtype) for l in lands],
                   jax.ShapeDtypeStruct((SUBLANE, LANE), F32)),
        in_specs=[hbm] * (2 * n) + [pl.BlockSpec(memory_space=pl.ANY)],
        out_specs=(sem, sem, *[hbm] * (2 * n), pl.BlockSpec(memory_space=pltpu.VMEM)),
        input_output_aliases={k: 2 + k for k in range(2 * n)},
        compiler_params=pltpu.CompilerParams(has_side_effects=pltpu.SideEffectType.DATAFLOW_SIDE_EFFECTING),
    )(*[pltpu.with_memory_space_constraint(s, pltpu.HBM) for s in srcs],
      *[pltpu.with_memory_space_constraint(l, pltpu.HBM) for l in lands], dep)
    return (n, bcast, half, axes, sizes, outs[0], outs[1], outs[2:2 + n], outs[2 + n:2 + 2 * n]), outs[-1]


def _exchange4_wait(name, handle, after):
    n, bcast, half, axes, sizes, send_sems, recv_sems, src_thru, land_thru = handle

    def body(*refs):
        src, land = refs[:n], refs[n:2 * n]
        send_sems, recv_sems = refs[2 * n], refs[2 * n + 1]
        x, y, c = lax.axis_index("x"), lax.axis_index("y"), lax.axis_index("c")
        for a in range(n):
            for j, (px, py) in enumerate([(1 - x, y), (x, 1 - y), (1 - x, 1 - y)]):
                pk = 2 * px + py
                copy = pltpu.make_async_remote_copy(
                    src_ref=(src[a].at[c] if half else src[a]) if bcast else src[a].at[pk],
                    dst_ref=_slot(land[a], pk, axes[a], sizes[a]),
                    send_sem=send_sems.at[3 * a + j], recv_sem=recv_sems.at[3 * a + j], device_id=(px, py, c),
                    device_id_type=MESH)
                copy.wait_send()
                copy.wait_recv()

    hbm = pl.BlockSpec(memory_space=pltpu.HBM)
    sem = pl.BlockSpec(memory_space=pltpu.SEMAPHORE)
    outs = _pcall(
        body, name=name,
        out_shape=tuple(pltpu.HBM(t.shape, t.dtype) for t in (*src_thru, *land_thru)),
        in_specs=[hbm] * (2 * n) + [sem, sem, pl.BlockSpec(memory_space=pl.ANY)], out_specs=tuple([hbm] * (2 * n)),
        input_output_aliases={k: k for k in range(2 * n)},
        compiler_params=pltpu.CompilerParams(has_side_effects=pltpu.SideEffectType.DATAFLOW_SIDE_EFFECTING),
    )(*src_thru, *land_thru, send_sems, recv_sems, after)
    return list(outs[:n]), list(outs[n:])


def _tie(name, v, token):
    def body(v_ref, token_ref, o_ref):
        del v_ref, token_ref, o_ref

    any_spec = pl.BlockSpec(memory_space=pl.ANY)
    return _pcall(body, name=name, out_shape=jax.ShapeDtypeStruct(v.shape, v.dtype), in_specs=[any_spec, any_spec],
                  out_specs=any_spec, input_output_aliases={0: 0})(v, token)


def _fill_own(landed, own, me, bcast):
    blk = own if bcast else lax.dynamic_index_in_dim(own, me, 0, keepdims=False)
    return lax.dynamic_update_index_in_dim(landed, blk, me, 0)


def _swap_sibling(name, srcs, by_core=False):
    n = len(srcs)

    def body(*refs):
        src, out = refs[:n], refs[n:2 * n]
        send_sems, recv_sems = refs[2 * n:]
        x, y, c = lax.axis_index("x"), lax.axis_index("y"), lax.axis_index("c")
        copies = []
        for a in range(n):
            send = pltpu.make_async_remote_copy(
                src_ref=src[a], dst_ref=out[a].at[c] if by_core else out[a], send_sem=send_sems.at[a],
                recv_sem=recv_sems.at[a], device_id=(x, y, 1 - c), device_id_type=MESH)
            send.start()
            arrive = pltpu.make_async_remote_copy(
                src_ref=src[a], dst_ref=out[a].at[1 - c] if by_core else out[a], send_sem=send_sems.at[a],
                recv_sem=recv_sems.at[a], device_id=(x, y, 1 - c), device_id_type=MESH)
            copies.append((send, arrive))
        for send, arrive in copies:
            send.wait_send()
            arrive.wait_recv()

    any_spec = pl.BlockSpec(memory_space=pl.ANY)
    return _pcall(
        body, name=name,
        out_shape=[jax.ShapeDtypeStruct(((2,) + s.shape) if by_core else s.shape, s.dtype) for s in srcs],
        in_specs=[any_spec] * n, out_specs=[any_spec] * n,
        scratch_shapes=[pltpu.SemaphoreType.DMA((n,)), pltpu.SemaphoreType.DMA((n,))],
    )(*srcs)


def _mod_fwd(c16, mod_w, mod_b_shard):
    nl, D, S = mod_w.shape

    def body(c_ref, w_ref, b_ref, o_ref):
        s = _silu(c_ref[...]).astype(BF16)
        o_ref[...] = jnp.dot(s, w_ref[...].astype(BF16), preferred_element_type=F32) + b_ref[...]

    return _pcall(
        body, name="mod_fwd", grid=(nl,),
        in_specs=[pl.BlockSpec((16, D), lambda l: (0, 0)), pl.BlockSpec((None, D, S), lambda l: (l, 0, 0)),
                  pl.BlockSpec((None, 1, S), lambda l: (l, 0, 0))],
        out_specs=pl.BlockSpec((None, 16, S), lambda l: (l, 0, 0)),
        out_shape=jax.ShapeDtypeStruct((nl, 16, S), F32), compiler_params=_cparams(1),
    )(c16, mod_w, mod_b_shard)


def _mod_w_update(s16t, dm16, w, m, v):
    nl, D, S = w.shape
    tm = _row_tile(D, 256)

    def body(s_ref, dm_ref, w_ref, m_ref, v_ref, g_ref, dl_ref, nm_ref, nv_ref):
        g = jnp.dot(s_ref[...], dm_ref[...], preferred_element_type=F32, precision=HIGHEST)
        g, dl, nm, nv = _f_adamw(w_ref[...], m_ref[...], v_ref[...], g, jnp.zeros_like(g))
        g_ref[...] = g
        dl_ref[...] = dl
        nm_ref[...] = nm
        nv_ref[...] = nv

    big = pl.BlockSpec((None, tm, S), lambda l, i: (l, i, 0))
    return _pcall(
        body, name="mod_w_update", grid=(nl, D // tm),
        in_specs=[pl.BlockSpec((tm, 16), lambda l, i: (i, 0)), pl.BlockSpec((None, 16, S), lambda l, i: (l, 0, 0)),
                  big, big, big],
        out_specs=[big] * 4, out_shape=[jax.ShapeDtypeStruct(w.shape, F32)] * 4, compiler_params=_cparams(2),
    )(s16t, dm16, w, m, v)


def _size(shape):
    n = 1
    for d in shape:
        n *= d
    return n


def _pack(arrs):
    pieces = []
    for a in arrs:
        flat = a.reshape(-1).astype(F32)
        pieces.append(jnp.pad(flat, (0, _round_up(flat.shape[0], LANE) - flat.shape[0])).reshape(-1, LANE))
    buf = jnp.concatenate(pieces, axis=0)
    return jnp.pad(buf, ((0, _round_up(buf.shape[0], SUBLANE) - buf.shape[0]), (0, 0)))


def _unpack(buf, shapes):
    lead = buf.shape[:-2]
    out, row = [], 0
    for s in shapes:
        n = _size(s)
        rows = _cdiv(n, LANE)
        piece = buf[..., row:row + rows, :].reshape(lead + (rows * LANE,))
        out.append(piece[..., :n].reshape(lead + tuple(s)))
        row += rows
    return out


def _adamw_many(name, ws, ms, vs, gs):
    n = len(ws)

    def body(*refs):
        for k in range(n):
            res = _f_adamw(refs[k][...], refs[n + k][...], refs[2 * n + k][...], refs[3 * n + k][...], 0.0)
            for j in range(4):
                refs[(4 + j) * n + k][...] = res[j]

    vmem = pl.BlockSpec(memory_space=pltpu.VMEM)
    res = _pcall(body, name=name, out_shape=[jax.ShapeDtypeStruct(w.shape, F32) for _ in range(4) for w in ws],
                 in_specs=[vmem] * (4 * n), out_specs=[vmem] * (4 * n))(*ws, *ms, *vs, *gs)
    return [tuple(res[j * n + k] for j in range(4)) for k in range(n)]


SHARD_AXIS = {
    "mod_w": 2, "ssd_w_in": 2, "ssd_conv_w": 2, "ssd_w_out": 1, "conf_w_pw1": 2, "conf_b_pw1": 1, "conf_w_dw": 2,
    "conf_b_dw": 1, "conf_ln_w": 1, "conf_ln_b": 1, "conf_w_pw2": 1, "conf_b_pw2": 1, "ffn_w_up": 2,
    "ffn_conv_w": 3, "ffn_w_down": 1,
}
BIG = ("ssd_w_in", "ssd_w_out", "conf_w_pw1", "conf_w_pw2", "ffn_w_up", "ffn_w_down")
WEIGHTS = ("c_ctx", "mod_w", "mod_b", "norm1_w", "norm2_w", "ssd_w_in", "ssd_conv_w", "ssd_conv_b", "ssd_dt_bias",
           "ssd_a_log", "ssd_d", "ssd_norm_w", "ssd_w_out", "conf_w_pw1", "conf_b_pw1", "conf_w_dw", "conf_b_dw",
           "conf_ln_w", "conf_ln_b", "conf_w_pw2", "conf_b_pw2", "ffn_w_up", "ffn_conv_w", "ffn_conv_b",
           "ffn_w_down", "final_norm_w")
SMALL = tuple(n for n in WEIGHTS if n not in BIG and n != "mod_w")
SMALL_SHARDED = tuple(n for n in SMALL if n in SHARD_AXIS)


def _unshard(stacked, axis):
    return jnp.concatenate([stacked[k] for k in range(N_CHIPS)], axis=axis)


def _to_blocks(full, axis):
    return jnp.stack(jnp.split(full, N_CHIPS, axis=axis))


def _par(v):
    v = v.reshape(-1, v.shape[-1])
    return v[:, None, :]


def kernel(x, c, ctx, c_ctx, mod_w, mod_b, norm1_w, norm2_w, ssd_w_in, ssd_conv_w, ssd_conv_b, ssd_dt_bias, ssd_a_log, ssd_d, ssd_norm_w, ssd_w_out, conf_w_pw1, conf_b_pw1, conf_w_dw, conf_b_dw, conf_ln_w, conf_ln_b, conf_w_pw2, conf_b_pw2, ffn_w_up, ffn_conv_w, ffn_conv_b, ffn_w_down, final_norm_w, loss_target, m_c_ctx, m_mod_w, m_mod_b, m_norm1_w, m_norm2_w, m_ssd_w_in, m_ssd_conv_w, m_ssd_conv_b, m_ssd_dt_bias, m_ssd_a_log, m_ssd_d, m_ssd_norm_w, m_ssd_w_out, m_conf_w_pw1, m_conf_b_pw1, m_conf_w_dw, m_conf_b_dw, m_conf_ln_w, m_conf_ln_b, m_conf_w_pw2, m_conf_b_pw2, m_ffn_w_up, m_ffn_conv_w, m_ffn_conv_b, m_ffn_w_down, m_final_norm_w, v_c_ctx, v_mod_w, v_mod_b, v_norm1_w, v_norm2_w, v_ssd_w_in, v_ssd_conv_w, v_ssd_conv_b, v_ssd_dt_bias, v_ssd_a_log, v_ssd_d, v_ssd_norm_w, v_ssd_w_out, v_conf_w_pw1, v_conf_b_pw1, v_conf_w_dw, v_conf_b_dw, v_conf_ln_w, v_conf_ln_b, v_conf_w_pw2, v_conf_b_pw2, v_ffn_w_up, v_ffn_conv_w, v_ffn_conv_b, v_ffn_w_down, v_final_norm_w):
    given = dict(locals())
    W = {n: given[n] for n in WEIGHTS}
    Mo = {n: given["m_" + n] for n in WEIGHTS}
    Vo = {n: given["v_" + n] for n in WEIGHTS}

    ax, ay, ac = lax.axis_index("x"), lax.axis_index("y"), lax.axis_index("c")
    chip = 2 * ax + ay
    dev = 4 * ax + 2 * ay + ac

    D = x.shape[-1]
    L, Lc = x.shape[1], ctx.shape[1]
    T0 = L + Lc
    H = ssd_a_log.shape[-1]
    DI = ssd_norm_w.shape[-1]
    P = DI // H
    CD = ssd_conv_b.shape[-1]
    N = SSD_STATE
    G = (CD - DI) // (2 * N)
    FH = ffn_conv_b.shape[-1]
    KS = ssd_conv_w.shape[1]
    KC = conf_w_dw.shape[1]
    ncc = Lc // SSD_CHUNK

    shard_b = {n: W[n].astype(BF16) for n in BIG}

    small_shard_shapes = [W[n].shape for n in SMALL_SHARDED]
    f1 = _allgather8("gather_small", _pack([c] + [W[n] for n in SMALL_SHARDED]))
    parts = _unpack(f1, [c.shape] + small_shard_shapes)
    Wf = dict(W)
    for n, p in zip(SMALL_SHARDED, parts[1:]):
        Wf[n] = _unshard(p[::2], SHARD_AXIS[n])
    c16 = jnp.concatenate([parts[0].reshape(N_DEV, D), c_ctx[None, :], jnp.zeros((16 - N_DEV - 1, D), F32)], axis=0)

    S_mod = mod_w.shape[-1]
    mod_b_shard = lax.dynamic_slice_in_dim(mod_b, chip * S_mod, S_mod, axis=1)[:, None, :]
    mod_part = _mod_fwd(c16, mod_w, mod_b_shard)
    f2 = _allgather8("gather_mod", mod_part.reshape(2 * 16, S_mod))
    mods = jnp.concatenate([f2[2 * k].reshape(2, 16, S_mod) for k in range(N_CHIPS)], axis=-1)
    my = lax.dynamic_slice_in_dim(mods, dev, 1, axis=1)[:, 0]
    sh1, sc1, g1, sh2, sc2, g2 = [[my[l, k * D:(k + 1) * D] for l in range(2)] for k in range(6)]
    csh1, csc1 = mods[0, N_DEV, 0:D], mods[0, N_DEV, D:2 * D]

    in_halves = shard_b["ssd_w_in"].reshape(2, D // 2, ssd_w_in.shape[-1])
    gather_a, token = _exchange4_start("gather_w_in_start", [in_halves], True, mods, half=True)
    csc1 = _tie("tie_gather_w_in", csc1, token)

    def full_weight(n, own, landed):
        if landed.ndim == own.ndim:
            ax = SHARD_AXIS[n]
            return lax.dynamic_update_slice_in_dim(landed, own, chip * own.shape[ax], ax)
        return _unshard(_fill_own(landed, own, chip, True), SHARD_AXIS[n])

    xl = x[0]
    rows0 = (ctx[0], xl)
    n1w0, n1w1 = _par(norm1_w[0]), _par(norm1_w[1])
    sc_seg = jnp.stack([csc1, sc1[0]])[:, None, :]
    sh_seg = jnp.stack([csh1, sh1[0]])[:, None, :]

    a0 = _rw_fwd("l0_modnorm1", _f_modnorm, [], [n1w0, sc_seg, sh_seg], [D], T=T0, seg_rows=(Lc,), head=rows0,
                 out_dtypes=[BF16])
    rest = [n for n in BIG if n != "ssd_w_in"]
    for n in rest:
        a0 = _tie("tie_cast_" + n, a0, shard_b[n])
    (own_in,), (landed_in,) = _exchange4_wait("gather_w_in_wait", gather_a, a0)
    mine = _fill_own(landed_in, lax.dynamic_index_in_dim(own_in, ac, 0, keepdims=False), chip, True)
    (halves,) = _swap_sibling("swap_w_in", [mine], by_core=True)
    halves = lax.dynamic_update_index_in_dim(halves, mine, ac, 0)
    w_in = jnp.concatenate([halves[:, k].reshape(D, -1) for k in range(N_CHIPS)], axis=1)
    landed_in = halves
    def start_gather(tag, names, dep):
        handle, tok = _exchange4_start("gather_" + tag + "_start", [shard_b[n] for n in names], True, dep,
                                       axes=[1 if SHARD_AXIS[n] == 1 else None for n in names])
        return (names, handle), tok

    def finish_gather(tag, group, after):
        names, handle = group
        return {n: full_weight(n, own, g)
                for n, own, g in zip(names, *_exchange4_wait("gather_" + tag + "_wait", handle, after))}

    gather_b, token = start_gather("mix", ["ssd_w_out", "conf_w_pw1", "conf_w_pw2"], landed_in)
    gather_c, token = start_gather("ffn", ["ffn_w_up", "ffn_w_down"], token)
    a0 = _tie("tie_gather_rest", a0, token)
    proj = _mm(a0, w_in, name="l0_w_in")
    seg_taps = [(k - KS // 2, ("seg", Lc)) for k in range(KS)]
    xbc_pre, xbc = _conv_fwd("l0_conv", proj, DI, CD, Wf["ssd_conv_w"][0], ssd_conv_b, seg_taps, act=True)
    dt_raw = proj[:, DI + CD:]
    dt_bias = _par(ssd_dt_bias.reshape(1, 2 * H))
    dt = _rw_fwd("l0_softplus", _f_softplus, [dt_raw], [dt_bias], [2 * H])
    dt_t = dt.T
    dtr = (dt_t[:H, None, :], dt_t[H:, None, :])
    a_all = -jnp.exp(ssd_a_log.reshape(2, H, 1, 1))
    a_neg = (a_all[0], a_all[1])
    (y_f, y_b), s_enter = _ssd_fwd(xbc, DI, DI + G * N, dtr, a_neg, P, ncc)
    gate_rows = [y_f, y_b, (xbc, 0, DI, Lc), (proj, 0, DI, Lc)]
    d_rep = _par(jnp.repeat(ssd_d[0], P))
    ssd_nw = _par(ssd_norm_w[0])
    yn = _rw_fwd("l0_ssd_gate", _f_ssd_gate, gate_rows, [d_rep, ssd_nw], [DI], T=L, out_dtypes=[BF16])
    Wb = finish_gather("mix", gather_b, yn)
    w_out, w_pw1, w_pw2 = Wb["ssd_w_out"][0], Wb["conf_w_pw1"][0], Wb["conf_w_pw2"][0]
    mix0 = _mm(yn, w_out, name="l0_w_out")
    g1_0, g2_0, g1_1, g2_1 = _par(g1[0]), _par(g2[0]), _par(g1[1]), _par(g2[1])
    h1 = _rw_fwd("l0_res1", _f_gate_res, [xl, mix0], [g1_0], [D])
    Wb = finish_gather("ffn", gather_c, h1)
    w_up, w_dn = Wb["ffn_w_up"], Wb["ffn_w_down"]

    grid_taps = [((i - 1) * GRID_W + (j - 1), (None if j == 1 else ("col", j - 1))) for i in range(3) for j in range(3)]

    def ffn_fwd(l, h, tag):
        a = _rw_fwd(tag + "_modnorm2", _f_modnorm, [h], [_par(norm2_w[l]), _par(sc2[l]), _par(sh2[l])], [D],
                    out_dtypes=[BF16])
        hh = _mm(a, w_up, b_lead=l, name=tag + "_w_up")
        gc = _conv_fwd(tag + "_ffn_conv", hh, FH, FH, Wf["ffn_conv_w"][l].reshape(9, FH), ffn_conv_b[l][None, :],
                       grid_taps)
        act = _rw_fwd(tag + "_act", _f_ffn_act, [(hh, 0, FH), gc], [], [FH], col_tile=_tile(FH, 1536),
                      out_dtypes=[BF16])
        dn = _mm(act, w_dn, b_lead=l, name=tag + "_w_down")
        return a, hh, gc, act, dn

    a1, hh0, gc0, act0, dn0 = ffn_fwd(0, h1, "l0")
    h2 = _rw_fwd("l0_res2", _f_gate_res, [h1, dn0], [g2_0], [D])

    a2 = _rw_fwd("l1_modnorm1", _f_modnorm, [h2], [n1w1, _par(sc1[1]), _par(sh1[1])], [D], out_dtypes=[BF16])
    pw = _mm(a2, w_pw1, name="l1_pw1")
    b_pw1 = Wf["conf_b_pw1"][0]
    glu = _rw_fwd("l1_glu", _f_glu, [(pw, 0, D), (pw, D, D)], [_par(b_pw1[:D]), _par(b_pw1[D:])], [D])
    conf_taps = [(k - KC // 2, None) for k in range(KC)]
    cv = _conv_fwd("l1_conv", glu, 0, D, Wf["conf_w_dw"][0], Wf["conf_b_dw"], conf_taps)
    ln_w, ln_b = _par(Wf["conf_ln_w"][0]), _par(Wf["conf_ln_b"][0])
    ls = _rw_fwd("l1_ln_silu", _f_ln_silu, [cv], [ln_w, ln_b], [D], out_dtypes=[BF16])
    p2 = _mm(ls, w_pw2, name="l1_pw2")
    b_pw2 = _par(Wf["conf_b_pw2"][0])
    h3 = _rw_fwd("l1_res1", _f_gate_res_bias, [h2, p2], [g1_1, b_pw2], [D])
    a3, hh1, gc1, act1, dn1 = ffn_fwd(1, h3, "l1")
    h4 = _rw_fwd("l1_res2", _f_gate_res, [h3, dn1], [g2_1], [D])

    fnw = final_norm_w[None, :]
    tgt = loss_target[0]
    loss_local = _loss_fwd(h4, tgt, fnw)[0, 0]

    G_full = {}
    reduces = {}

    def start_reduce(tag, items, dep):
        def blocks_of(g, ax):
            if g.ndim == 3:
                return g
            return g.reshape(N_CHIPS, g.shape[0] // N_CHIPS, g.shape[1]) if ax == 0 else _to_blocks(g, ax)

        blocks = [blocks_of(g, ax).astype(BF16) for _, g, ax in items]
        handle, tok = _exchange4_start("reduce_" + tag + "_start", blocks, False, dep)
        reduces[tag] = ([n for n, _, _ in items], handle)
        return tok
    ones = jnp.ones((L, 1), F32)
    (dh4,), (dfnw,) = _rw_bwd("loss_bwd", _f_loss_rows, [h4, tgt], [_par(final_norm_w)], [ones],
                              row_grad=[True, False], par_grad=[True])
    G_full["final_norm_w"] = dfnw.reshape(D)

    def ffn_bwd(l, h, saved, g2_l, dh_out, tag):
        a, hh, gc, act, dn = saved
        (ddn,), (dg2,) = _rw_bwd(tag + "_res2_bwd", _f_gate, [dn], [g2_l], [dh_out],
                                 row_grad=[True], par_grad=[True], row_dtypes=[BF16])
        dact = _mm(ddn, w_dn, b_lead=l, tb=True, name=tag + "_w_down_dx")
        dwdn = _mm(act, ddn, ta=True, name=tag + "_w_down_dw", out_dtype=BF16)
        (dval, dgc), _ = _rw_bwd(tag + "_act_bwd", _f_ffn_act, [(hh, 0, FH), gc], [], [dact],
                                 row_grad=[True, True], par_grad=[], col_tile=_tile(FH, 1536), row_dtypes=[BF16, F32])
        dgin, dcw, dcb = _conv_bwd(tag + "_ffn_conv_bwd", hh, FH, FH, Wf["ffn_conv_w"][l].reshape(9, FH), dgc,
                                   grid_taps, du_dtype=BF16)
        dhh = jnp.concatenate([dval, dgin], axis=1)
        da = _mm(dhh, w_up, b_lead=l, tb=True, name=tag + "_w_up_dx")
        dwup = _mm(a, dhh, ta=True, name=tag + "_w_up_dw", out_dtype=BF16, col_blocks=N_CHIPS)
        (dh,), (dn2w, dsc2, dsh2) = _rw_bwd(
            tag + "_modnorm2_bwd", _f_modnorm, [h], [_par(norm2_w[l]), _par(sc2[l]), _par(sh2[l])], [da],
            row_grad=[True], par_grad=[True, True, True], add=dh_out)
        return dh, dict(w_down=dwdn, w_up=dwup, conv_w=dcw.reshape(3, 3, FH), conv_b=dcb.reshape(FH),
                        n2w=dn2w.reshape(D), sc2=dsc2.reshape(D), sh2=dsh2.reshape(D), g2=dg2.reshape(D))

    dh3, gf1 = ffn_bwd(1, h3, (a3, hh1, gc1, act1, dn1), g2_1, dh4, "l1")
    (dp2,), (dg1_1, db_pw2) = _rw_bwd("l1_res1_bwd", _f_gate_bias, [p2], [g1_1, b_pw2], [dh3],
                                      row_grad=[True], par_grad=[True, True], row_dtypes=[BF16])
    dls = _mm(dp2, w_pw2, tb=True, name="l1_pw2_dx")
    dw_pw2 = _mm(ls, dp2, ta=True, name="l1_pw2_dw", out_dtype=BF16)
    (dcv,), (dln_w, dln_b) = _rw_bwd("l1_ln_silu_bwd", _f_ln_silu, [cv], [ln_w, ln_b], [dls],
                                     row_grad=[True], par_grad=[True, True])
    dglu, dw_dw, db_dw = _conv_bwd("l1_conv_bwd", glu, 0, D, Wf["conf_w_dw"][0], dcv, conf_taps)
    (dpa, dpg), (dba, dbg) = _rw_bwd("l1_glu_bwd", _f_glu, [(pw, 0, D), (pw, D, D)],
                                     [_par(b_pw1[:D]), _par(b_pw1[D:])], [dglu],
                                     row_grad=[True, True], par_grad=[True, True], row_dtypes=[BF16, BF16])
    dpw = jnp.concatenate([dpa, dpg], axis=1)
    da2 = _mm(dpw, w_pw1, tb=True, name="l1_pw1_dx")
    dw_pw1 = _mm(a2, dpw, ta=True, name="l1_pw1_dw", out_dtype=BF16, col_blocks=N_CHIPS)
    (dh2,), (dn1w1, dsc1_1, dsh1_1) = _rw_bwd(
        "l1_modnorm1_bwd", _f_modnorm, [h2], [n1w1, _par(sc1[1]), _par(sh1[1])], [da2],
        row_grad=[True], par_grad=[True, True, True], add=dh3)
    G_full["conf_b_pw2"] = db_pw2.reshape(1, D)
    G_full["conf_ln_w"], G_full["conf_ln_b"] = dln_w.reshape(1, D), dln_b.reshape(1, D)
    G_full["conf_w_dw"], G_full["conf_b_dw"] = dw_dw[None], db_dw.reshape(1, D)
    G_full["conf_b_pw1"] = jnp.concatenate([dba.reshape(1, D), dbg.reshape(1, D)], axis=1)

    token = start_reduce("l1", [("conf_w_pw2", dw_pw2, 0), ("conf_w_pw1", dw_pw1, 1), ("ffn_w_up1", gf1["w_up"], 1),
                                ("ffn_w_down1", gf1["w_down"], 0)], dw_pw2)
    dh2 = _tie("tie_reduce_l1", dh2, token)
    dh1, gf0 = ffn_bwd(0, h1, (a1, hh0, gc0, act0, dn0), g2_0, dh2, "l0")
    G_full["ffn_conv_w"] = jnp.stack([gf0["conv_w"], gf1["conv_w"]])
    G_full["ffn_conv_b"] = jnp.stack([gf0["conv_b"], gf1["conv_b"]])

    (dmix,), (dg1_0,) = _rw_bwd("l0_res1_bwd", _f_gate, [mix0], [g1_0], [dh1],
                                row_grad=[True], par_grad=[True], row_dtypes=[BF16])
    dyn = _mm(dmix, w_out, tb=True, name="l0_w_out_dx")
    dw_out = _mm(yn, dmix, ta=True, name="l0_w_out_dw", out_dtype=BF16)
    token = start_reduce("l0", [("ffn_w_up0", gf0["w_up"], 1), ("ffn_w_down0", gf0["w_down"], 0),
                                ("ssd_w_out", dw_out, 0)], dw_out)
    dyn = _tie("tie_reduce_l0", dyn, token)
    (dy_lat, dxs_gate, dz_lat), (dd_rep, dssd_nw) = _rw_bwd(
        "l0_ssd_gate_bwd", _f_ssd_gate, gate_rows, [d_rep, ssd_nw], [dyn],
        row_grad=[True, False, True, True], par_grad=[True, True], T=L, row_dtypes=[F32, F32, BF16])
    g_f, g_b = _ssd_bwd(xbc, DI, DI + G * N, dtr, a_neg, s_enter, dy_lat, P, ncc)
    silu_bwd = functools.partial(_rw_bwd, f=_silu, pars=[], row_grad=[True], par_grad=[], T=T0)
    (dxs_pre,), _ = silu_bwd("l0_silu_bwd_x", rows=[(xbc_pre, 0, DI)], cot_fn=lambda p, q, r: p + q + r,
                             cots=[g_f[0], g_b[0], (dxs_gate, 0, DI, -Lc)],
                             col_tile=_tile(DI, 1024))
    (db_pre,), _ = silu_bwd("l0_silu_bwd_b", rows=[(xbc_pre, DI, G * N)], cot_fn=lambda p, q: p + q,
                            cots=[g_f[1], g_b[1]], col_tile=_tile(G * N, 1024))
    (dc_pre,), _ = silu_bwd("l0_silu_bwd_c", rows=[(xbc_pre, DI + G * N, G * N)], cot_fn=lambda p, q: p + q,
                            cots=[g_f[2], g_b[2]], col_tile=_tile(G * N, 1024))
    conv_w0 = Wf["ssd_conv_w"][0]
    pieces = []
    for tag, off, width, g_pre in (("x", 0, DI, dxs_pre), ("b", DI, G * N, db_pre), ("c", DI + G * N, G * N, dc_pre)):
        pieces.append(_conv_bwd("l0_conv_bwd_" + tag, proj, DI + off, width, conv_w0[:, off:off + width], g_pre,
                                seg_taps, du_dtype=BF16))
    dconv_in = [p[0] for p in pieces]
    dcw0 = jnp.concatenate([p[1] for p in pieces], axis=1)
    dcb0 = jnp.concatenate([p[2] for p in pieces], axis=1)
    ddt = jnp.concatenate([g_f[3][:, 0, :].T, g_b[3][:, 0, :].T], axis=1)
    (ddt_raw,), (ddt_bias,) = _rw_bwd("l0_softplus_bwd", _f_softplus, [dt_raw], [dt_bias], [ddt],
                                      row_grad=[True], par_grad=[True], row_dtypes=[BF16])
    dproj = jnp.concatenate([jnp.pad(dz_lat, ((Lc, 0), (0, 0))), *dconv_in, ddt_raw], axis=1)
    da0 = _mm(dproj, w_in, tb=True, name="l0_w_in_dx")
    dw_in = _mm(a0, dproj, ta=True, name="l0_w_in_dw", out_dtype=BF16)
    token = start_reduce("in", [("ssd_w_in", dw_in, 1)], dw_in)
    da0 = _tie("tie_reduce_in", da0, token)
    (dhcat,), (dn1w0, dsc_seg, dsh_seg) = _rw_bwd(
        "l0_modnorm1_bwd", _f_modnorm, [], [n1w0, sc_seg, sh_seg], [da0], T=T0, head=rows0,
        row_grad=[True], par_grad=[True, True, True], seg_rows=(Lc,), add=(dh1, 0, D, -Lc), skip_rows=Lc)
    grad_x = dhcat[None]

    da_heads = jnp.stack([g[4][..., 0, 0].sum(axis=1).reshape(H) for g in (g_f, g_b)])[None]
    G_full["ssd_a_log"] = da_heads * (-jnp.exp(ssd_a_log))
    G_full["ssd_dt_bias"] = ddt_bias.reshape(1, 2, H)
    G_full["ssd_d"] = dd_rep.reshape(H, P).sum(axis=1)[None]
    G_full["ssd_norm_w"] = dssd_nw.reshape(1, DI)
    G_full["ssd_conv_w"], G_full["ssd_conv_b"] = dcw0[None], dcb0.reshape(1, CD)
    G_full["norm1_w"] = jnp.stack([dn1w0.reshape(D), dn1w1.reshape(D)])
    G_full["norm2_w"] = jnp.stack([gf0["n2w"], gf1["n2w"]])

    zD = jnp.zeros((D,), F32)
    dm_own = jnp.stack([
        jnp.concatenate([dsh_seg[1, 0], dsc_seg[1, 0], dg1_0.reshape(D), gf0["sh2"], gf0["sc2"], gf0["g2"]]),
        jnp.concatenate([dsh1_1.reshape(D), dsc1_1.reshape(D), dg1_1.reshape(D), gf1["sh2"], gf1["sc2"], gf1["g2"]]),
    ])
    dmc_own = jnp.concatenate([dsh_seg[0, 0], dsc_seg[0, 0], zD, zD, zD, zD])

    out = {}

    def finish_reduce(tags, after, swap_name):
        partial = {}
        for tag in tags:
            names, handle = reduces[tag]
            blocks, landed = _exchange4_wait("reduce_" + tag + "_wait", handle, after)
            for n, blk, own in zip(names, landed, blocks):
                r = _fill_own(blk, own, chip, False)
                partial[n] = _sum_leading("sum4_" + n, r.reshape(N_CHIPS, -1, r.shape[-1]),
                                          (0, 1, 2, 3), out_dtype=BF16).reshape(r.shape[1:])
        for n in ("ffn_w_up", "ffn_w_down"):
            if n + "0" in partial:
                partial[n] = jnp.stack([partial.pop(n + "0"), partial.pop(n + "1")])
        names = [n for n in BIG if n in partial]
        mine = [partial[n].reshape(W[n].shape) for n in names]
        for n, own, sib in zip(names, mine, _swap_sibling(swap_name, mine)):
            out[n] = _adamw("adamw_" + n, W[n], Mo[n], Vo[n], own, sib)
        return names

    early = finish_reduce(["l1", "l0"], dhcat, "swap_grads_early")

    small_sum_names = [n for n in SMALL if n not in ("c_ctx", "mod_b")]
    sum_part = [G_full[n] for n in small_sum_names] + [dmc_own, loss_local.reshape(1)]
    packed = _tie("tie_small_grads", _pack(sum_part + [dm_own]), out[early[-1]][1])
    gat = _allgather8("gather_small_grads", packed)
    total = _sum_leading("sum_small_grads", gat, tuple(range(N_DEV)))
    summed = _unpack(total, [a.shape for a in sum_part])
    Gs = dict(zip(small_sum_names, summed[:-2]))
    dmc_tot, loss = summed[-2], summed[-1][0]
    dm_all = _unpack(gat, [a.shape for a in sum_part] + [dm_own.shape])[-1].transpose(1, 0, 2)
    dm16 = jnp.concatenate([dm_all, jnp.stack([dmc_tot, jnp.zeros_like(dmc_tot)])[:, None, :],
                            jnp.zeros((2, 16 - N_DEV - 1, 6 * D), F32)], axis=1)
    Gs["mod_b"] = _sum_leading("sum_mod_b", dm16.transpose(1, 0, 2).reshape(16, 2 * 6 * D // LANE, LANE),
                               tuple(range(N_DEV + 1))).reshape(2, 6 * D)

    dm16_shard = lax.dynamic_slice_in_dim(dm16, chip * S_mod, S_mod, axis=2)
    ds16 = _mm(dm16_shard[0], mod_w.reshape(2 * D, S_mod), tb=True, precision=HIGHEST, name="c_ctx_dx")
    sig = jax.nn.sigmoid(c_ctx)
    dcc_part = ds16[N_DEV, :D] * (sig * (1.0 + c_ctx * (1.0 - sig)))
    gat_cc = _allgather8("gather_c_ctx_grad", _pack([dcc_part]))
    Gs["c_ctx"] = _sum_leading("sum_c_ctx_grad", gat_cc, (0, 2, 4, 6)).reshape(-1)[:D]

    s16t = _silu(c16).T
    out["mod_w"] = _mod_w_update(s16t, dm16_shard, mod_w, m_mod_w, v_mod_w)
    finish_reduce(["in"], out["mod_w"][0], "swap_grads_late")

    def own(n, full):
        if n in SHARD_AXIS:
            size = W[n].shape[SHARD_AXIS[n]]
            return lax.dynamic_slice_in_dim(full, chip * size, size, axis=SHARD_AXIS[n])
        return full

    def two_d(a):
        return a.reshape(1, -1) if a.ndim == 1 else a

    g_small = [own(n, Gs[n].reshape(Wf[n].shape)) for n in SMALL]
    res = _adamw_many("adamw_small", [two_d(W[n]) for n in SMALL], [two_d(Mo[n]) for n in SMALL],
                      [two_d(Vo[n]) for n in SMALL], [two_d(g) for g in g_small])
    for n, r in zip(SMALL, res):
        out[n] = tuple(t.reshape(W[n].shape) for t in r)

    grads = [out[n][0] for n in WEIGHTS]
    deltas = [out[n][1] for n in WEIGHTS]
    new_m = [out[n][2] for n in WEIGHTS]
    new_v = [out[n][3] for n in WEIGHTS]
    return (loss, grad_x, *grads, *deltas, *new_m, *new_v)
```

```python
import functools

import jax
import jax.numpy as jnp
from jax import lax
from jax.experimental import pallas as pl
from jax.experimental.pallas import tpu as pltpu

F32 = jnp.float32
BF16 = jnp.bfloat16
MESH = pl.DeviceIdType.MESH
HIGHEST = lax.Precision.HIGHEST

VMEM_LIMIT_BYTES = 48 * 1024 * 1024
LANE = 128
SUBLANE = 8

SSD_STATE = 128
SSD_CHUNK = 128
GRID_W = 64
EPS = 1e-6
N_CHIPS = 4
N_DEV = 8

ADAM_LR = 0.001
ADAM_B1 = 0.9
ADAM_B2 = 0.999
ADAM_EPS = 1e-08
ADAM_WD = 0.01
ADAM_STEP = 10


def _pcall(body, **kw):
    return pl.pallas_call(body, **kw)


def _cparams(n_grid):
    return pltpu.CompilerParams(dimension_semantics=("arbitrary",) * n_grid, vmem_limit_bytes=VMEM_LIMIT_BYTES)


def _cdiv(a, b):
    return -(-a // b)


def _round_up(a, b):
    return _cdiv(a, b) * b


def _tile(n, cap):
    if n <= cap:
        return n
    best = None
    for t in range(LANE, cap + 1, LANE):
        if n % t == 0:
            best = t
    if best is None:
        npad = _round_up(n, LANE)
        for t in range(LANE, cap + 1, LANE):
            if npad % t == 0:
                best = t
    return best


def _row_tile(n, cap, also=()):
    best = None
    for step in (2 * SUBLANE, SUBLANE):
        for t in range(step, min(cap, n) + 1, step):
            if n % t == 0 and all(a % t == 0 for a in also):
                best = t
        if best is not None:
            break
    assert best is not None, (n, cap, also)
    return best


def _silu(v):
    return v * jax.nn.sigmoid(v)


def _mm(a, b, *, name, ta=False, tb=False, precision=None, cap=1024, out_dtype=F32, col_blocks=None,
        b_lead=None, b_shards=None):
    M, K = (a.shape[1], a.shape[0]) if ta else a.shape
    b_dims = b.shape[(b_lead is not None) + (b_shards is not None):]
    b_cols = b_dims[1] * (b_shards or 1)
    N, Kb = (b_dims[0], b_cols) if tb else (b_cols, b_dims[0])
    assert K == Kb, (a.shape, b.shape, ta, tb)
    n_cut, k_cut = (1, b_shards or 1) if tb else (b_shards or 1, 1)
    tm, tk = _tile(M, cap), _tile(K // k_cut, cap + cap // 2)
    tn = _tile(N // (col_blocks or n_cut), cap + cap // 2)
    assert b_shards is None or (b_dims[1] % (tk if tb else tn) == 0 and col_blocks is None), (b.shape, tn, tk)
    nm, nn, nk = _cdiv(M, tm), _cdiv(N, tn), _cdiv(K, tk)
    k_tail = K % tk
    exact = precision is not None

    def body(a_ref, b_ref, o_ref, acc_ref):
        k = pl.program_id(2)

        @pl.when(k == 0)
        def _():
            acc_ref[...] = jnp.zeros_like(acc_ref)

        av = a_ref[...]
        bv = b_ref[...]
        if k_tail:
            lim = K - k * tk
            ka = lax.broadcasted_iota(jnp.int32, av.shape, 0 if ta else 1)
            kb = lax.broadcasted_iota(jnp.int32, bv.shape, 1 if tb else 0)
            av = jnp.where(ka < lim, av, jnp.zeros_like(av))
            bv = jnp.where(kb < lim, bv, jnp.zeros_like(bv))
        if exact:
            av = av.astype(F32)
            bv = bv.astype(F32)
        else:
            av = av.astype(BF16)
            bv = bv.astype(BF16)
        dn = (((0 if ta else 1,), (1 if tb else 0,)), ((), ()))
        acc_ref[...] += lax.dot_general(av, bv, dn, preferred_element_type=F32, precision=precision)

        @pl.when(k == nk - 1)
        def _():
            o_ref[...] = acc_ref[...].astype(o_ref.dtype)

    a_spec = pl.BlockSpec((tk, tm), lambda i, j, k: (k, i)) if ta else pl.BlockSpec((tm, tk), lambda i, j, k: (i, k))
    b_spec = pl.BlockSpec((tn, tk), lambda i, j, k: (j, k)) if tb else pl.BlockSpec((tk, tn), lambda i, j, k: (k, j))
    if b_lead is not None or b_shards is not None:
        b_blk, b_map = tuple(b_spec.block_shape), b_spec.index_map
        lead = () if b_lead is None else (b_lead,)
        per = None if b_shards is None else b_dims[1] // b_blk[1]

        def b_index(i, j, k):
            r, c = b_map(i, j, k)
            return lead + (r, c) if per is None else (c // per,) + lead + (r, c % per)

        b_spec = pl.BlockSpec((None,) * (len(lead) + (per is not None)) + b_blk, b_index)
    if col_blocks is None:
        out_spec = pl.BlockSpec((tm, tn), lambda i, j, k: (i, j))
        out_shape = jax.ShapeDtypeStruct((M, N), out_dtype)
    else:
        per = (N // col_blocks) // tn
        assert per * tn * col_blocks == N, (N, col_blocks, tn)
        out_spec = pl.BlockSpec((None, tm, tn), lambda i, j, k: (j // per, i, j % per))
        out_shape = jax.ShapeDtypeStruct((col_blocks, M, N // col_blocks), out_dtype)
    return _pcall(
        body, name=name, grid=(nm, nn, nk), in_specs=[a_spec, b_spec], out_specs=out_spec, out_shape=out_shape,
        scratch_shapes=[pltpu.VMEM((tm, tn), F32)], compiler_params=_cparams(3),
    )(a, b)


def _norm_rows(rows):
    out = []
    for r in rows:
        if not isinstance(r, tuple):
            r = (r,)
        arr, off, width, roff = (r + (0, None, 0)[len(r) - 1:])
        out.append((arr, off, width if width is not None else arr.shape[1], roff))
    return out


def _rw_plan(T, rows, pars, seg_rows, col_tile, tm_cap):
    widths = [r[2] for r in rows]
    wmax = max(widths + [p.shape[-1] for p in pars] + [1])
    if col_tile is not None:
        assert all(w == widths[0] for w in widths) and all(p.shape[-1] == widths[0] for p in pars)
        ncol = widths[0] // col_tile
        assert ncol * col_tile == widths[0]
        wmax = col_tile
    else:
        ncol = 1
    cap = tm_cap if tm_cap is not None else max(SUBLANE, min(512, (512 * 1024) // wmax))
    tm = _row_tile(T, cap, also=tuple(seg_rows) + tuple(abs(r[3]) for r in rows if r[3]))
    bounds = tuple(s // tm for s in seg_rows)
    return widths, ncol, tm, bounds


def _rw_specs(rows, pars, ncol, tm, bounds, col_tile):
    def seg(i):
        s = 0
        for b in bounds:
            s = s + (i >= b).astype(jnp.int32)
        return s

    specs = []
    for arr, off, w, roff in rows:
        bw = col_tile if col_tile is not None else w
        assert off % bw == 0 and roff % tm == 0, (off, bw, roff, tm)
        specs.append(pl.BlockSpec((tm, bw), functools.partial(
            lambda j, i, ob, rb, last: (jnp.clip(i + rb, 0, last), ob + j),
            ob=off // bw, rb=roff // tm, last=arr.shape[0] // tm - 1)))
    for p in pars:
        bw = col_tile if col_tile is not None else p.shape[-1]
        if p.shape[0] > 1:
            specs.append(pl.BlockSpec((None, 1, bw), lambda j, i: (seg(i), 0, j)))
        else:
            specs.append(pl.BlockSpec((None, 1, bw), lambda j, i: (0, 0, j)))
    return specs, seg


def _head_rows(head):
    top, bottom = head
    return [(top, 0, None, 0), (bottom, 0, None, -top.shape[0])]


def _rw_fwd(name, f, rows, pars, out_widths, *, T=None, seg_rows=(), col_tile=None, tm_cap=None, out_dtypes=None,
            head=None):
    rows = _norm_rows((_head_rows(head) if head else []) + list(rows))
    T = rows[0][0].shape[0] if T is None else T
    widths, ncol, tm, bounds = _rw_plan(T, rows, pars, seg_rows, col_tile, tm_cap)
    in_specs, _ = _rw_specs(rows, pars, ncol, tm, bounds, col_tile)
    nr, npar, nout = len(rows), len(pars), len(out_widths)

    def body(*refs):
        vals = [r[...] for r in refs[:nr + npar]]
        if head:
            vals = [jnp.where(pl.program_id(1) < head[0].shape[0] // tm, vals[0], vals[1])] + vals[2:]
        outs = f(*vals)
        if not isinstance(outs, (tuple, list)):
            outs = (outs,)
        for o_ref, o in zip(refs[nr + npar:], outs):
            o_ref[...] = o.astype(o_ref.dtype)

    out_specs = [pl.BlockSpec((tm, col_tile if col_tile is not None else w), lambda j, i: (i, j)) for w in out_widths]
    res = _pcall(
        body, name=name, grid=(ncol, T // tm), in_specs=in_specs, out_specs=out_specs,
        out_shape=[jax.ShapeDtypeStruct((T, w), dt) for w, dt in zip(out_widths, out_dtypes or [F32] * nout)],
        compiler_params=_cparams(2),
    )(*[r[0] for r in rows], *pars)
    return res if nout > 1 else res[0]


def _rw_bwd(name, f, rows, pars, cots, *, row_grad, par_grad, T=None, seg_rows=(), col_tile=None, tm_cap=None,
            add=None, cot_fn=None, row_dtypes=None, head=None, skip_rows=0):
    rows = _norm_rows((_head_rows(head) if head else []) + list(rows))
    cots = _norm_rows(cots)
    T = rows[0][0].shape[0] if T is None else T
    extra = _norm_rows([add]) if add is not None else []
    all_rows = rows + cots + extra
    widths, ncol, tm, bounds = _rw_plan(T, all_rows, pars, tuple(seg_rows) + ((skip_rows,) if skip_rows else ()),
                                        col_tile, tm_cap)
    bounds = bounds[:len(seg_rows)]
    in_specs, seg = _rw_specs(all_rows, pars, ncol, tm, bounds, col_tile)
    nr, nc, ne, npar = len(rows), len(cots), len(extra), len(pars)
    skip = 1 if head else 0
    widths = widths[skip:]
    nrf = nr - skip
    row_idx = [k for k in range(nrf) if row_grad[k]]
    par_idx = [k for k in range(npar) if par_grad[k]]

    def body(*refs):
        i = pl.program_id(1)

        def zero_before(vals, ops):
            return [jnp.where(i + c[3] // tm >= 0, v, jnp.zeros_like(v)) if c[3] < 0 else v for v, c in zip(vals, ops)]

        row_vals = [r[...] for r in refs[:nr]]
        if head:
            row_vals = [jnp.where(i < head[0].shape[0] // tm, row_vals[0], row_vals[1])] + row_vals[2:]
        cot_vals = zero_before([r[...] for r in refs[nr:nr + nc]], cots)
        add_vals = zero_before([r[...] for r in refs[nr + nc:nr + nc + ne]], extra)
        par_vals = [r[...] for r in refs[nr + nc + ne:nr + nc + ne + npar]]
        out_refs = refs[nr + nc + ne + npar:]
        outs, vjp = jax.vjp(f, *row_vals, *par_vals)
        if cot_fn is not None:
            cot_vals = cot_fn(*cot_vals)
            if not isinstance(cot_vals, (tuple, list)):
                cot_vals = (cot_vals,)
        if isinstance(outs, (tuple, list)):
            grads = vjp(tuple(c.astype(o.dtype) for c, o in zip(cot_vals, outs)))
        else:
            grads = vjp(cot_vals[0].astype(outs.dtype))
        first_seg = i == 0
        for b in bounds:
            first_seg = first_seg | (i == b)
        for n, k in enumerate(row_idx):
            g = grads[k]
            if n == 0 and add_vals:
                g = g + add_vals[0]
            out_refs[n][...] = g.astype(out_refs[n].dtype)
        for n, k in enumerate(par_idx):
            g = grads[nrf + k]
            o_ref = out_refs[len(row_idx) + n]
            first = first_seg if pars[k].shape[0] > 1 else (i == 0)

            @pl.when(first)
            def _(o_ref=o_ref, g=g):
                o_ref[...] = g

            @pl.when(jnp.logical_not(first))
            def _(o_ref=o_ref, g=g):
                o_ref[...] += g

    out_specs, out_shape = [], []
    for k in row_idx:
        w = widths[k]
        out_specs.append(pl.BlockSpec((tm, col_tile if col_tile is not None else w),
                                      lambda j, i: (jnp.maximum(i - skip_rows // tm, 0), j)))
        out_shape.append(jax.ShapeDtypeStruct((T - skip_rows, w), row_dtypes[len(out_shape)] if row_dtypes else F32))
    for k in par_idx:
        p = pars[k]
        bw = col_tile if col_tile is not None else p.shape[-1]
        if p.shape[0] > 1:
            out_specs.append(pl.BlockSpec((None, 1, bw), lambda j, i: (seg(i), 0, j)))
        else:
            out_specs.append(pl.BlockSpec((None, 1, bw), lambda j, i: (0, 0, j)))
        out_shape.append(jax.ShapeDtypeStruct(p.shape, F32))
    res = _pcall(
        body, name=name, grid=(ncol, T // tm), in_specs=in_specs, out_specs=out_specs, out_shape=out_shape,
        compiler_params=_cparams(2),
    )(*[r[0] for r in all_rows], *pars)
    return list(res[:len(row_idx)]), list(res[len(row_idx):])


def _f_modnorm(h, w, sc, sh):
    y = h * lax.rsqrt(jnp.mean(h * h, axis=-1, keepdims=True) + EPS)
    return (y * w) * (1.0 + sc) + sh


def _f_gate_res(h, y, g):
    return h + g * y


def _f_gate_res_bias(h, y, g, b):
    return h + g * (y + b)


def _f_gate(y, g):
    return g * y


def _f_gate_bias(y, g, b):
    return g * (y + b)


def _f_ffn_act(val, gate):
    return _silu(gate) * val


def _f_softplus(raw, bias):
    v = raw + bias
    return jnp.maximum(v, 0.0) + jnp.log(1.0 + jnp.exp(-jnp.abs(v)))


def _f_ssd_gate(yf, yb, xs, z, d_rep, nw):
    y = (yf + yb + d_rep * xs) * _silu(z)
    return (y * lax.rsqrt(jnp.mean(y * y, axis=-1, keepdims=True) + EPS)) * nw


def _f_glu(a, g, ba, bg):
    return (a + ba) * jax.nn.sigmoid(g + bg)


def _f_ln_silu(h, w, b):
    mu = jnp.mean(h, axis=-1, keepdims=True)
    d = h - mu
    y = d * lax.rsqrt(jnp.mean(d * d, axis=-1, keepdims=True) + EPS)
    return _silu(y * w + b)


def _f_loss_rows(h, t, w):
    y = (h * lax.rsqrt(jnp.mean(h * h, axis=-1, keepdims=True) + EPS)) * w
    e = y - t
    return 0.5 * jnp.mean(e * e, axis=-1, keepdims=True)


def _f_adamw(w, m, v, ga, gb):
    g = ga.astype(F32) + gb
    m = ADAM_B1 * m + (1.0 - ADAM_B1) * g
    v = ADAM_B2 * v + (1.0 - ADAM_B2) * (g * g)
    m_hat = m / (1.0 - ADAM_B1 ** ADAM_STEP)
    v_hat = v / (1.0 - ADAM_B2 ** ADAM_STEP)
    delta = -ADAM_LR * (m_hat / (jnp.sqrt(v_hat) + ADAM_EPS) + ADAM_WD * w)
    return g, delta, m, v


def _adamw(name, w, m, v, ga, gb):
    shape = w.shape
    c = shape[-1]
    two_d = [t.reshape(-1, c) for t in (w, m, v, ga, gb)]
    rows = two_d[0].shape[0]
    pad = _round_up(rows, SUBLANE) - rows
    if pad:
        two_d = [jnp.pad(t, ((0, pad), (0, 0))) for t in two_d]
    outs = _rw_fwd(name, _f_adamw, two_d, [], [c] * 4)
    return tuple(o[:rows].reshape(shape) for o in outs)


def _sum_leading(name, x, idxs, out_dtype=F32):
    _, R, C = x.shape
    tm = _row_tile(R, max(SUBLANE, min(512, (512 * 1024) // C)))

    def body(x_ref, o_ref):
        acc = x_ref[idxs[0]].astype(F32)
        for k in idxs[1:]:
            acc = acc + x_ref[k].astype(F32)
        o_ref[...] = acc.astype(o_ref.dtype)

    return _pcall(
        body, name=name, grid=(R // tm,), in_specs=[pl.BlockSpec((x.shape[0], tm, C), lambda i: (0, i, 0))],
        out_specs=pl.BlockSpec((tm, C), lambda i: (i, 0)), out_shape=jax.ShapeDtypeStruct((R, C), out_dtype),
        compiler_params=_cparams(1),
    )(x)


def _loss_fwd(h, t, w):
    T, D = h.shape
    tm = _row_tile(T, 256)

    def body(h_ref, t_ref, w_ref, o_ref):
        i = pl.program_id(0)
        part = jnp.sum(_f_loss_rows(h_ref[...], t_ref[...], w_ref[...]), axis=0, keepdims=True)
        part = jnp.broadcast_to(part, (1, LANE))

        @pl.when(i == 0)
        def _():
            o_ref[...] = part

        @pl.when(i > 0)
        def _():
            o_ref[...] += part

    return _pcall(
        body, name="loss_fwd", grid=(T // tm,),
        in_specs=[pl.BlockSpec((tm, D), lambda i: (i, 0)), pl.BlockSpec((tm, D), lambda i: (i, 0)),
                  pl.BlockSpec((1, D), lambda i: (0, 0))],
        out_specs=pl.BlockSpec((1, LANE), lambda i: (0, 0)), out_shape=jax.ShapeDtypeStruct((1, LANE), F32),
        compiler_params=_cparams(1),
    )(h, t, w)


CONV_ROWS = 256
CONV_ROWS_FEW_TAPS = 1024
CONV_ACC_ELEMS = 16384


def _col_mask(arg, t):
    col = jnp.bitwise_and(t, GRID_W - 1)
    return (col != 0) if arg < 0 else (col != GRID_W - 1)


def _conv_plan(T, C, taps):
    seg = [m[1] for _, m in taps if m is not None and m[0] == "seg"]
    cap = CONV_ROWS_FEW_TAPS if len(taps) <= 9 else CONV_ROWS
    rc = next(r for r in (1024, 768, 512, 256, LANE) if r <= cap and T % r == 0)
    ct = next((t for t in (512, 256, LANE) if C % t == 0), C)
    reach = max(abs(s) for s, _ in taps)
    hb = next(h for h in (8, 16, 32, 64, 128, 256) if h >= reach and rc % h == 0)
    sub = max(2 * SUBLANE, min(rc, CONV_ACC_ELEMS // ct))
    boundary = None
    if seg:
        inside = seg[0] % rc
        boundary = (seg[0], (inside - reach, inside + reach) if inside else None)
    taps = [(s, None if (m is None or m[0] == "seg") else m[1]) for s, m in taps]
    return rc, ct, hb, sub, T // rc, C // ct, boundary, taps


def _seg_ok(boundary, i, rc, r0, n, s):
    if boundary is None or boundary[1] is None or s == 0 or r0 + n <= boundary[1][0] or r0 >= boundary[1][1]:
        return None
    t = i * rc + r0 + lax.broadcasted_iota(jnp.int32, (n, 1), 0)
    return (t >= boundary[0]) == ((t + s) >= boundary[0])


def _halo_specs(rc, ct, hb, T, off_blocks):
    per = rc // hb
    last = T // hb - 1
    prev = pl.BlockSpec((hb, ct), lambda j, i: (jnp.maximum(i * per - 1, 0), off_blocks + j))
    cur = pl.BlockSpec((rc, ct), lambda j, i: (i, off_blocks + j))
    nxt = pl.BlockSpec((hb, ct), lambda j, i: (jnp.minimum((i + 1) * per, last), off_blocks + j))
    return [prev, cur, nxt]


def _fill_halo(pad_ref, p_ref, c_ref, n_ref, i, nrc, rc, hb, boundary):
    has_prev = i > 0
    has_next = i < nrc - 1
    if boundary is not None:
        has_prev = has_prev & (i * rc != boundary[0])
        has_next = has_next & ((i + 1) * rc != boundary[0])
    pad_ref[0:hb, :] = jnp.where(has_prev, p_ref[...], 0.0)
    pad_ref[hb:hb + rc, :] = c_ref[...]
    pad_ref[hb + rc:hb + rc + hb, :] = jnp.where(has_next, n_ref[...], 0.0)


def _shift_plan(keys):
    count = {}
    for s, m in keys:
        k = (s % SUBLANE, m)
        count[k] = count.get(k, 0) + 1
    slots = {}
    for k, n in sorted(count.items(), key=lambda kv: (kv[0][0], str(kv[0][1]))):
        if k != (0, None) and (n >= 2 or k[1] is not None):
            slots[k] = len(slots)
    return slots


def _build_shifted(copies_ref, slots, pad_ref, keys, i, rc, hb, sub):
    for (r, m), slot in slots.items():
        qs = [s - r for s, mk in keys if (s % SUBLANE, mk) == (r, m)]
        lo, hi = hb + min(qs), hb + rc + max(qs)
        for p in range(lo, hi, sub):
            n = min(sub, hi - p)
            v = pad_ref[p + r:p + r + n, :]
            if m is not None:
                t = i * rc - hb + p + r + lax.broadcasted_iota(jnp.int32, (n, 1), 0)
                v = jnp.where(_col_mask(m, t), v, 0.0)
            copies_ref[slot, p:p + n, :] = v


def _read(copies_ref, slots, pad_ref, s, m, row, n):
    k = (s % SUBLANE, m)
    if k in slots:
        q = s - k[0]
        return copies_ref[slots[k], row + q:row + q + n, :]
    return pad_ref[row + s:row + s + n, :]


def _conv_fwd(name, u, col_off, C, w, b, taps, act=False):
    T = u.shape[0]
    rc, ct, hb, sub, nrc, ncc, boundary, taps = _conv_plan(T, C, taps)
    assert col_off % ct == 0
    K = len(taps)
    keys = [(s, None) for s, _ in taps]
    slots = _shift_plan(keys)
    dirs = sorted({m for _, m in taps if m is not None})

    def body(up, uc, un, w_ref, b_ref, *rest):
        y_ref = rest[0]
        pad_ref, copies_ref = rest[-2], rest[-1]
        i = pl.program_id(1)
        _fill_halo(pad_ref, up, uc, un, i, nrc, rc, hb, boundary)
        _build_shifted(copies_ref, slots, pad_ref, keys, i, rc, hb, sub)
        for r0 in range(0, rc, sub):
            acc = jnp.broadcast_to(b_ref[...], (sub, ct))
            for m in [None] + dirs:
                part = None
                for k, (s, mk) in enumerate(taps):
                    if mk != m:
                        continue
                    v = _read(copies_ref, slots, pad_ref, s, None, hb + r0, sub)
                    ok = _seg_ok(boundary, i, rc, r0, sub, s)
                    term = w_ref[k:k + 1, :] * (v if ok is None else jnp.where(ok, v, 0.0))
                    part = term if part is None else part + term
                if part is None:
                    continue
                if m is not None:
                    t = i * rc + r0 + lax.broadcasted_iota(jnp.int32, (sub, 1), 0)
                    part = jnp.where(_col_mask(m, t), part, 0.0)
                acc = acc + part
            y_ref[r0:r0 + sub, :] = acc
            if act:
                rest[1][r0:r0 + sub, :] = _silu(acc)

    n_out = 2 if act else 1
    res = _pcall(
        body, name=name, grid=(ncc, nrc),
        in_specs=_halo_specs(rc, ct, hb, T, col_off // ct) + [pl.BlockSpec((K, ct), lambda j, i: (0, j)),
                                                              pl.BlockSpec((1, ct), lambda j, i: (0, j))],
        out_specs=[pl.BlockSpec((rc, ct), lambda j, i: (i, j))] * n_out,
        out_shape=[jax.ShapeDtypeStruct((T, C), F32)] * n_out,
        scratch_shapes=[pltpu.VMEM((rc + 2 * hb, ct), F32), pltpu.VMEM((max(len(slots), 1), rc + 2 * hb, ct), F32)],
        compiler_params=_cparams(2),
    )(u, u, u, w, b)
    return res if act else res[0]


def _conv_bwd(name, u, col_off, C, w, g, taps, du_dtype=F32):
    T = u.shape[0]
    rc, ct, hb, sub, nrc, ncc, boundary, taps = _conv_plan(T, C, taps)
    K = len(taps)
    u_keys = [(s, None) for s, _ in taps]
    dirs = sorted({m for _, m in taps if m is not None})
    g_keys = [(-s, m) for s, m in taps] + [(0, m) for m in dirs]
    u_slots, g_slots = _shift_plan(u_keys), _shift_plan(g_keys)

    def body(up, uc, un, gp, gc, gn, w_ref, du_ref, dw_ref, db_ref, upad, gpad, ucopies, gcopies):
        i = pl.program_id(1)
        _fill_halo(upad, up, uc, un, i, nrc, rc, hb, boundary)
        _fill_halo(gpad, gp, gc, gn, i, nrc, rc, hb, boundary)
        _build_shifted(ucopies, u_slots, upad, u_keys, i, rc, hb, sub)
        _build_shifted(gcopies, g_slots, gpad, g_keys, i, rc, hb, sub)

        @pl.when(i == 0)
        def _():
            dw_ref[...] = jnp.zeros_like(dw_ref)
            db_ref[...] = jnp.zeros_like(db_ref)

        def fold(v):
            return jnp.sum(v.reshape(sub // SUBLANE, SUBLANE, ct), axis=0)

        dbs = jnp.zeros((SUBLANE, ct), F32)
        for r0 in range(0, rc, sub):
            dbs = dbs + fold(gpad[hb + r0:hb + r0 + sub, :])
            acc = jnp.zeros((sub, ct), F32)
            for k, (s, m) in enumerate(taps):
                v = _read(gcopies, g_slots, gpad, -s, m, hb + r0, sub)
                ok = _seg_ok(boundary, i, rc, r0, sub, -s)
                acc = acc + w_ref[k:k + 1, :] * (v if ok is None else jnp.where(ok, v, 0.0))
            du_ref[r0:r0 + sub, :] = acc.astype(du_ref.dtype)
        db_ref[...] += jnp.sum(dbs, axis=0, keepdims=True)
        for k, (s, m) in enumerate(taps):
            part = jnp.zeros((SUBLANE, ct), F32)
            for r0 in range(0, rc, sub):
                v = _read(ucopies, u_slots, upad, s, None, hb + r0, sub)
                ok = _seg_ok(boundary, i, rc, r0, sub, s)
                part = part + fold(_read(gcopies, g_slots, gpad, 0, m, hb + r0, sub)
                                   * (v if ok is None else jnp.where(ok, v, 0.0)))
            dw_ref[k:k + 1, :] += jnp.sum(part, axis=0, keepdims=True)

    halo_u = _halo_specs(rc, ct, hb, T, col_off // ct)
    halo_g = _halo_specs(rc, ct, hb, T, 0)
    rows = rc + 2 * hb
    return _pcall(
        body, name=name, grid=(ncc, nrc),
        in_specs=halo_u + halo_g + [pl.BlockSpec((K, ct), lambda j, i: (0, j))],
        out_specs=[pl.BlockSpec((rc, ct), lambda j, i: (i, j)), pl.BlockSpec((K, ct), lambda j, i: (0, j)),
                   pl.BlockSpec((1, ct), lambda j, i: (0, j))],
        out_shape=[jax.ShapeDtypeStruct((T, C), du_dtype), jax.ShapeDtypeStruct((K, C), F32),
                   jax.ShapeDtypeStruct((1, C), F32)],
        scratch_shapes=[pltpu.VMEM((rows, ct), F32), pltpu.VMEM((rows, ct), F32),
                        pltpu.VMEM((max(len(u_slots), 1), rows, ct), F32),
                        pltpu.VMEM((max(len(g_slots), 1), rows, ct), F32)],
        compiler_params=_cparams(2),
    )(u, u, u, g, g, g, w)


def _ssd_group(xg, bm, cm, s_in, *per_head, reverse, P):
    R = len(per_head) // 2
    dtrs, a_s = per_head[:R], per_head[R:]
    q, rp = xg.shape
    ii = lax.broadcasted_iota(jnp.int32, (q, q), 0)
    jj = lax.broadcasted_iota(jnp.int32, (q, q), 1)
    causal = (jj >= ii) if reverse else (jj <= ii)
    causal_t = (ii >= jj) if reverse else (ii <= jj)
    eye = ii == jj
    lane = lax.broadcasted_iota(jnp.int32, (1, rp), 1)
    row = lax.broadcasted_iota(jnp.int32, (rp, 1), 0)
    nt = (((1,), (1,)), ((), ()))
    tn = (((0,), (0,)), ((), ()))
    cb = lax.dot_general(cm.astype(BF16), bm.astype(BF16), nt, preferred_element_type=F32)
    dt_x = jnp.zeros((q, rp), F32)
    acum_x = jnp.zeros((q, rp), F32)
    tot_row = jnp.zeros((1, rp), F32)
    tot_col = jnp.zeros((rp, 1), F32)
    wts, lane_masks = [], []
    for r in range(R):
        hm = (lane >= r * P) & (lane < (r + 1) * P)
        hc = (row >= r * P) & (row < (r + 1) * P)
        dt_c = jnp.sum(jnp.where(eye, dtrs[r], 0.0), axis=1, keepdims=True)
        dac = dt_c * a_s[r]
        dar = dtrs[r] * a_s[r]
        acum_c = jnp.sum(jnp.where(causal, dar, 0.0), axis=1, keepdims=True)
        acum_r = jnp.sum(jnp.where(causal_t, dac, 0.0), axis=0, keepdims=True)
        decay = jnp.where(causal, jnp.exp(jnp.where(causal, acum_c - acum_r, 0.0)), 0.0)
        tot = jnp.sum(dac, axis=0, keepdims=True)
        dt_x = jnp.where(hm, dt_c, dt_x)
        acum_x = jnp.where(hm, acum_c, acum_x)
        tot_row = jnp.where(hm, tot, tot_row)
        tot_col = jnp.where(hc, tot, tot_col)
        wts.append((cb * decay).astype(BF16))
        lane_masks.append(hm)
    xdt = xg * dt_x
    xdt_b = xdt.astype(BF16)
    y = jnp.zeros((q, rp), F32)
    for r in range(R):
        y = jnp.where(lane_masks[r], jnp.dot(wts[r], xdt_b, preferred_element_type=F32), y)
    dte = jnp.exp(tot_row - acum_x)
    cs = lax.dot_general((xdt * dte).astype(BF16), bm.astype(BF16), tn, preferred_element_type=F32)
    y = y + lax.dot_general(cm.astype(BF16), s_in.astype(BF16), nt, preferred_element_type=F32) * jnp.exp(acum_x)
    s_out = jnp.exp(tot_col) * s_in + cs
    return y, s_out


def _ssd_group_state(xg, bm, s_in, *per_head, reverse, P):
    R = len(per_head) // 2
    dtrs, a_s = per_head[:R], per_head[R:]
    q, rp = xg.shape
    ii = lax.broadcasted_iota(jnp.int32, (q, q), 0)
    jj = lax.broadcasted_iota(jnp.int32, (q, q), 1)
    causal = (jj >= ii) if reverse else (jj <= ii)
    eye = ii == jj
    lane = lax.broadcasted_iota(jnp.int32, (1, rp), 1)
    row = lax.broadcasted_iota(jnp.int32, (rp, 1), 0)
    dt_x = jnp.zeros((q, rp), F32)
    acum_x = jnp.zeros((q, rp), F32)
    tot_row = jnp.zeros((1, rp), F32)
    tot_col = jnp.zeros((rp, 1), F32)
    for r in range(R):
        hm = (lane >= r * P) & (lane < (r + 1) * P)
        hc = (row >= r * P) & (row < (r + 1) * P)
        dt_c = jnp.sum(jnp.where(eye, dtrs[r], 0.0), axis=1, keepdims=True)
        acum_c = jnp.sum(jnp.where(causal, dtrs[r] * a_s[r], 0.0), axis=1, keepdims=True)
        tot = jnp.sum(dt_c * a_s[r], axis=0, keepdims=True)
        dt_x = jnp.where(hm, dt_c, dt_x)
        acum_x = jnp.where(hm, acum_c, acum_x)
        tot_row = jnp.where(hm, tot, tot_row)
        tot_col = jnp.where(hc, tot, tot_col)
    xe = xg * dt_x * jnp.exp(tot_row - acum_x)
    cs = lax.dot_general(xe.astype(BF16), bm.astype(BF16), (((0,), (0,)), ((), ())), preferred_element_type=F32)
    return jnp.exp(tot_col) * s_in + cs


def _ssd_maps(NC, ncc, reverse_steps):
    def chunk(d, s):
        if reverse_steps:
            s = NC - 1 - s
        return s if d == 0 else jnp.where(s < ncc, ncc - 1 - s, NC - 1 - s + ncc)

    def lat_chunk(d, s):
        c = chunk(d, s) - ncc
        return jnp.where(c < 0, 0 if d == 0 else NC - ncc - 1, c)

    def step(s):
        return NC - 1 - s if reverse_steps else s

    return chunk, lat_chunk, step


SSD_GROUPS_PER_STEP = 2


def _ssd_specs(chunk, d, GB, R, Q, N, RP, b_off, c_off):
    assert b_off % (GB * N) == 0 and c_off % (GB * N) == 0
    bo, co = b_off // (GB * N), c_off // (GB * N)
    return [
        pl.BlockSpec((Q, GB * RP), lambda g, s: (chunk(d, s), g)),
        pl.BlockSpec((Q, GB * N), lambda g, s: (chunk(d, s), bo + g)),
        pl.BlockSpec((Q, GB * N), lambda g, s: (chunk(d, s), co + g)),
        pl.BlockSpec((GB * R, 1, Q), lambda g, s: (g, 0, chunk(d, s))),
        pl.BlockSpec((GB * R, 1, 1), lambda g, s: (g, 0, 0)),
    ]


def _ssd_fwd(xbc, b_off, c_off, dtr, a, P, ncc):
    T = xbc.shape[0]
    H = dtr[0].shape[0]
    N, Q = SSD_STATE, SSD_CHUNK
    NC = T // Q
    G = (c_off - b_off) // N
    R = H // G
    RP = R * P
    GB = SSD_GROUPS_PER_STEP if G % SSD_GROUPS_PER_STEP == 0 else 1
    chunk, lat_chunk, _ = _ssd_maps(NC, ncc, False)

    def body(*refs):
        s = pl.program_id(1)
        s_ref = refs[-1]

        @pl.when(s == 0)
        def _():
            s_ref[...] = jnp.zeros_like(s_ref)

        for d in range(2):
            x_ref, b_ref, c_ref, dtr_ref, a_ref = refs[5 * d:5 * d + 5]
            y_ref, se_ref = refs[10 + 2 * d:12 + 2 * d]
            for gg in range(GB):
                cols, bcols = slice(gg * RP, (gg + 1) * RP), slice(gg * N, (gg + 1) * N)
                s_in = s_ref[d, gg]
                se_ref[gg] = s_in
                per_head = [dtr_ref[gg * R + r] for r in range(R)] + [a_ref[gg * R + r] for r in range(R)]

                @pl.when(s >= ncc)
                def _(d=d, gg=gg, cols=cols, bcols=bcols, x_ref=x_ref, b_ref=b_ref, c_ref=c_ref, y_ref=y_ref,
                      s_in=s_in, per_head=per_head):
                    y, s_out = _ssd_group(x_ref[:, cols], b_ref[:, bcols], c_ref[:, bcols], s_in, *per_head,
                                          reverse=d == 1, P=P)
                    y_ref[:, cols] = y
                    s_ref[d, gg] = s_out

                @pl.when(s < ncc)
                def _(d=d, gg=gg, cols=cols, bcols=bcols, x_ref=x_ref, b_ref=b_ref, s_in=s_in, per_head=per_head):
                    s_ref[d, gg] = _ssd_group_state(x_ref[:, cols], b_ref[:, bcols], s_in, *per_head,
                                                    reverse=d == 1, P=P)

    in_specs, out_specs, out_shape, operands = [], [], [], []
    for d in range(2):
        in_specs += _ssd_specs(chunk, d, GB, R, Q, N, RP, b_off, c_off)
        operands += [xbc, xbc, xbc, dtr[d], a[d]]
        out_specs += [pl.BlockSpec((Q, GB * RP), functools.partial(lambda g, s, d: (lat_chunk(d, s), g), d=d)),
                      pl.BlockSpec((GB, None, RP, N), lambda g, s: (g, s, 0, 0))]
        out_shape += [jax.ShapeDtypeStruct((T - ncc * Q, H * P), F32), jax.ShapeDtypeStruct((G, NC, RP, N), F32)]
    y_f, se_f, y_b, se_b = _pcall(
        body, name="ssd_fwd", grid=(G // GB, NC), in_specs=in_specs, out_specs=out_specs, out_shape=out_shape,
        scratch_shapes=[pltpu.VMEM((2, GB, RP, N), F32)], compiler_params=_cparams(2),
    )(*operands)
    return (y_f, y_b), (se_f, se_b)


def _ssd_bwd(xbc, b_off, c_off, dtr, a, s_enter, dy, P, ncc):
    T = xbc.shape[0]
    H = dtr[0].shape[0]
    N, Q = SSD_STATE, SSD_CHUNK
    NC = T // Q
    G = (c_off - b_off) // N
    R = H // G
    RP = R * P
    GB = SSD_GROUPS_PER_STEP if G % SSD_GROUPS_PER_STEP == 0 else 1
    chunk, lat_chunk, step = _ssd_maps(NC, ncc, True)
    n_in, n_out = 7, 5

    def body(*refs):
        s = pl.program_id(1)
        ds_ref = refs[-1]

        @pl.when(s == 0)
        def _():
            ds_ref[...] = jnp.zeros_like(ds_ref)

        for d in range(2):
            x_ref, b_ref, c_ref, dtr_ref, a_ref, se_ref, dy_ref = refs[n_in * d:n_in * (d + 1)]
            dx_ref, db_ref, dc_ref, ddtr_ref, da_ref = refs[2 * n_in + n_out * d:2 * n_in + n_out * (d + 1)]
            for gg in range(GB):
                cols, bcols = slice(gg * RP, (gg + 1) * RP), slice(gg * N, (gg + 1) * N)
                per_head = [dtr_ref[gg * R + r] for r in range(R)] + [a_ref[gg * R + r] for r in range(R)]

                def store(grads, dx_ref=dx_ref, db_ref=db_ref, ddtr_ref=ddtr_ref, da_ref=da_ref, d=d, gg=gg,
                          cols=cols, bcols=bcols):
                    dx_ref[:, cols] = grads[0]
                    db_ref[:, bcols] = grads[1]
                    ds_ref[d, gg] = grads[2]
                    for r in range(R):
                        ddtr_ref[gg * R + r] = grads[3 + r]
                        da_ref[gg, r] = jnp.broadcast_to(grads[3 + R + r], (SUBLANE, LANE))

                @pl.when(s < NC - ncc)
                def _(d=d, gg=gg, cols=cols, bcols=bcols, x_ref=x_ref, b_ref=b_ref, c_ref=c_ref, se_ref=se_ref,
                      dy_ref=dy_ref, dc_ref=dc_ref, per_head=per_head, store=store):
                    f = functools.partial(_ssd_group, reverse=d == 1, P=P)
                    _, vjp = jax.vjp(f, x_ref[:, cols], b_ref[:, bcols], c_ref[:, bcols], se_ref[gg], *per_head)
                    grads = vjp((dy_ref[:, cols], ds_ref[d, gg]))
                    dc_ref[:, bcols] = grads[2]
                    store(grads[:2] + grads[3:])

                @pl.when(s >= NC - ncc)
                def _(d=d, gg=gg, cols=cols, bcols=bcols, x_ref=x_ref, b_ref=b_ref, se_ref=se_ref, dc_ref=dc_ref,
                      per_head=per_head, store=store):
                    f = functools.partial(_ssd_group_state, reverse=d == 1, P=P)
                    _, vjp = jax.vjp(f, x_ref[:, cols], b_ref[:, bcols], se_ref[gg], *per_head)
                    dc_ref[:, bcols] = jnp.zeros((Q, N), F32)
                    store(vjp(ds_ref[d, gg]))

    in_specs, out_specs, out_shape, operands = [], [], [], []
    for d in range(2):
        in_specs += _ssd_specs(chunk, d, GB, R, Q, N, RP, b_off, c_off) + [
            pl.BlockSpec((GB, None, RP, N), lambda g, s: (g, step(s), 0, 0)),
            pl.BlockSpec((Q, GB * RP), functools.partial(lambda g, s, d: (lat_chunk(d, s), g), d=d)),
        ]
        operands += [xbc, xbc, xbc, dtr[d], a[d], s_enter[d], dy]
    for d in range(2):
        at_chunk = functools.partial(lambda g, s, d: (chunk(d, s), g), d=d)
        out_specs += [
            pl.BlockSpec((Q, GB * RP), at_chunk), pl.BlockSpec((Q, GB * N), at_chunk),
            pl.BlockSpec((Q, GB * N), at_chunk),
            pl.BlockSpec((GB * R, 1, Q), functools.partial(lambda g, s, d: (g, 0, chunk(d, s)), d=d)),
            pl.BlockSpec((GB, None, R, SUBLANE, LANE), lambda g, s: (g, s, 0, 0, 0)),
        ]
        out_shape += [
            jax.ShapeDtypeStruct((T, H * P), F32), jax.ShapeDtypeStruct((T, G * N), F32),
            jax.ShapeDtypeStruct((T, G * N), F32), jax.ShapeDtypeStruct((H, 1, T), F32),
            jax.ShapeDtypeStruct((G, NC, R, SUBLANE, LANE), F32),
        ]
    res = _pcall(
        body, name="ssd_bwd", grid=(G // GB, NC), in_specs=in_specs, out_specs=out_specs, out_shape=out_shape,
        scratch_shapes=[pltpu.VMEM((2, GB, RP, N), F32)], compiler_params=_cparams(2),
    )(*operands)
    return res[:n_out], res[n_out:]


def _allgather8(name, v):
    R, C = v.shape

    def body(x_ref, out_ref, send_sems, recv_sems, local_sem):
        x, y, c = lax.axis_index("x"), lax.axis_index("y"), lax.axis_index("c")
        me, sibling = (x, y, c), (x, y, 1 - c)
        chips = [(1 - x, y), (x, 1 - y), (1 - x, 1 - y)]

        def slot(px, py, pc):
            return out_ref.at[4 * px + 2 * py + pc]

        def copy(k, block, to, src=None):
            return pltpu.make_async_remote_copy(
                src_ref=slot(*block) if src is None else src, dst_ref=slot(*block),
                send_sem=send_sems.at[k], recv_sem=recv_sems.at[k], device_id=to, device_id_type=MESH)

        mine = pltpu.make_async_copy(x_ref, slot(*me), local_sem)
        mine.start()
        first = [copy(0, me, sibling, src=x_ref)]
        first += [copy(1 + j, me, (*chip, c), src=x_ref) for j, chip in enumerate(chips)]
        for cp in first:
            cp.start()
        passed = [copy(4 + j, (*chip, c), sibling) for j, chip in enumerate(chips)]
        for j, chip in enumerate(chips):
            copy(1 + j, (*chip, c), me).wait_recv()
            passed[j].start()
        copy(0, sibling, me).wait_recv()
        for j, chip in enumerate(chips):
            copy(4 + j, (*chip, 1 - c), me).wait_recv()
        for cp in first + passed:
            cp.wait_send()
        mine.wait()

    return _pcall(
        body, name=name, out_shape=jax.ShapeDtypeStruct((N_DEV, R, C), v.dtype),
        in_specs=[pl.BlockSpec(memory_space=pltpu.VMEM)], out_specs=pl.BlockSpec(memory_space=pltpu.VMEM),
        scratch_shapes=[pltpu.SemaphoreType.DMA((7,)), pltpu.SemaphoreType.DMA((7,)), pltpu.SemaphoreType.DMA],
        compiler_params=pltpu.CompilerParams(vmem_limit_bytes=VMEM_LIMIT_BYTES),
    )(v)


def _slot(ref, k, axis, size):
    if axis is None:
        return ref.at[k]
    align = LANE if size % LANE == 0 else 2 * SUBLANE
    assert size % align == 0
    return ref.at[(slice(None),) * axis + (pl.ds(pl.multiple_of(k * size, align), size),)]


def _exchange4_start(name, srcs, bcast, dep, axes=None, half=False):
    n = len(srcs)
    axes = list(axes) if axes is not None else [None] * n
    sizes = [None if ax is None else s.shape[ax] for s, ax in zip(srcs, axes)]

    def land_shape(s, ax):
        if not bcast:
            return s.shape
        if half:
            return (N_CHIPS,) + s.shape[1:]
        if ax is None:
            return (N_CHIPS,) + s.shape
        return s.shape[:ax] + (N_CHIPS * s.shape[ax],) + s.shape[ax + 1:]

    lands = [lax.empty(land_shape(s, ax), s.dtype) for s, ax in zip(srcs, axes)]

    def body(*refs):
        src, land = refs[:n], refs[n:2 * n]
        send_sems, recv_sems = refs[2 * n + 1], refs[2 * n + 2]
        token = refs[-1]
        x, y, c = lax.axis_index("x"), lax.axis_index("y"), lax.axis_index("c")
        me = 2 * x + y
        for a in range(n):
            for j, (px, py) in enumerate([(1 - x, y), (x, 1 - y), (1 - x, 1 - y)]):
                pltpu.make_async_remote_copy(
                    src_ref=(src[a].at[c] if half else src[a]) if bcast else src[a].at[2 * px + py],
                    dst_ref=_slot(land[a], me, axes[a], sizes[a]),
                    send_sem=send_sems.at[3 * a + j], recv_sem=recv_sems.at[3 * a + j], device_id=(px, py, c),
                    device_id_type=MESH).start()
        token[...] = jnp.zeros_like(token)

    hbm = pl.BlockSpec(memory_space=pltpu.HBM)
    sem = pl.BlockSpec(memory_space=pltpu.SEMAPHORE)
    outs = _pcall(
        body, name=name,
        out_shape=(pltpu.SemaphoreType.DMA((3 * n,)), pltpu.SemaphoreType.DMA((3 * n,)),
                   *[pltpu.HBM(s.shape, s.dtype) for s in srcs], *[pltpu.HBM(l.shape, l.dtype) for l in lands],
                   jax.ShapeDtypeStruct((SUBLANE, LANE), F32)),
        in_specs=[hbm] * (2 * n) + [pl.BlockSpec(memory_space=pl.ANY)],
        out_specs=(sem, sem, *[hbm] * (2 * n), pl.BlockSpec(memory_space=pltpu.VMEM)),
        input_output_aliases={k: 2 + k for k in range(2 * n)},
        compiler_params=pltpu.CompilerParams(has_side_effects=pltpu.SideEffectType.DATAFLOW_SIDE_EFFECTING),
    )(*[pltpu.with_memory_space_constraint(s, pltpu.HBM) for s in srcs],
      *[pltpu.with_memory_space_constraint(l, pltpu.HBM) for l in lands], dep)
    return (n, bcast, half, axes, sizes, outs[0], outs[1], outs[2:2 + n], outs[2 + n:2 + 2 * n]), outs[-1]


def _exchange4_wait(name, handle, after):
    n, bcast, half, axes, sizes, send_sems, recv_sems, src_thru, land_thru = handle

    def body(*refs):
        src, land = refs[:n], refs[n:2 * n]
        send_sems, recv_sems = refs[2 * n], refs[2 * n + 1]
        x, y, c = lax.axis_index("x"), lax.axis_index("y"), lax.axis_index("c")
        for a in range(n):
            for j, (px, py) in enumerate([(1 - x, y), (x, 1 - y), (1 - x, 1 - y)]):
                pk = 2 * px + py
                copy = pltpu.make_async_remote_copy(
                    src_ref=(src[a].at[c] if half else src[a]) if bcast else src[a].at[pk],
                    dst_ref=_slot(land[a], pk, axes[a], sizes[a]),
                    send_sem=send_sems.at[3 * a + j], recv_sem=recv_sems.at[3 * a + j], device_id=(px, py, c),
                    device_id_type=MESH)
                copy.wait_send()
                copy.wait_recv()

    hbm = pl.BlockSpec(memory_space=pltpu.HBM)
    sem = pl.BlockSpec(memory_space=pltpu.SEMAPHORE)
    outs = _pcall(
        body, name=name,
        out_shape=tuple(pltpu.HBM(t.shape, t.dtype) for t in (*src_thru, *land_thru)),
        in_specs=[hbm] * (2 * n) + [sem, sem, pl.BlockSpec(memory_space=pl.ANY)], out_specs=tuple([hbm] * (2 * n)),
        input_output_aliases={k: k for k in range(2 * n)},
        compiler_params=pltpu.CompilerParams(has_side_effects=pltpu.SideEffectType.DATAFLOW_SIDE_EFFECTING),
    )(*src_thru, *land_thru, send_sems, recv_sems, after)
    return list(outs[:n]), list(outs[n:])


def _tie(name, v, token):
    def body(v_ref, token_ref, o_ref):
        del v_ref, token_ref, o_ref

    any_spec = pl.BlockSpec(memory_space=pl.ANY)
    return _pcall(body, name=name, out_shape=jax.ShapeDtypeStruct(v.shape, v.dtype), in_specs=[any_spec, any_spec],
                  out_specs=any_spec, input_output_aliases={0: 0})(v, token)


def _fill_own(landed, own, me, bcast):
    blk = own if bcast else lax.dynamic_index_in_dim(own, me, 0, keepdims=False)
    return lax.dynamic_update_index_in_dim(landed, blk, me, 0)


def _swap_sibling(name, srcs, by_core=False):
    n = len(srcs)

    def body(*refs):
        src, out = refs[:n], refs[n:2 * n]
        send_sems, recv_sems = refs[2 * n:]
        x, y, c = lax.axis_index("x"), lax.axis_index("y"), lax.axis_index("c")
        copies = []
        for a in range(n):
            send = pltpu.make_async_remote_copy(
                src_ref=src[a], dst_ref=out[a].at[c] if by_core else out[a], send_sem=send_sems.at[a],
                recv_sem=recv_sems.at[a], device_id=(x, y, 1 - c), device_id_type=MESH)
            send.start()
            arrive = pltpu.make_async_remote_copy(
                src_ref=src[a], dst_ref=out[a].at[1 - c] if by_core else out[a], send_sem=send_sems.at[a],
                recv_sem=recv_sems.at[a], device_id=(x, y, 1 - c), device_id_type=MESH)
            copies.append((send, arrive))
        for send, arrive in copies:
            send.wait_send()
            arrive.wait_recv()

    any_spec = pl.BlockSpec(memory_space=pl.ANY)
    return _pcall(
        body, name=name,
        out_shape=[jax.ShapeDtypeStruct(((2,) + s.shape) if by_core else s.shape, s.dtype) for s in srcs],
        in_specs=[any_spec] * n, out_specs=[any_spec] * n,
        scratch_shapes=[pltpu.SemaphoreType.DMA((n,)), pltpu.SemaphoreType.DMA((n,))],
    )(*srcs)


def _mod_fwd(c16, mod_w, mod_b_shard):
    nl, D, S = mod_w.shape

    def body(c_ref, w_ref, b_ref, o_ref):
        s = _silu(c_ref[...]).astype(BF16)
        o_ref[...] = jnp.dot(s, w_ref[...].astype(BF16), preferred_element_type=F32) + b_ref[...]

    return _pcall(
        body, name="mod_fwd", grid=(nl,),
        in_specs=[pl.BlockSpec((16, D), lambda l: (0, 0)), pl.BlockSpec((None, D, S), lambda l: (l, 0, 0)),
                  pl.BlockSpec((None, 1, S), lambda l: (l, 0, 0))],
        out_specs=pl.BlockSpec((None, 16, S), lambda l: (l, 0, 0)),
        out_shape=jax.ShapeDtypeStruct((nl, 16, S), F32), compiler_params=_cparams(1),
    )(c16, mod_w, mod_b_shard)


def _mod_w_update(s16t, dm16, w, m, v):
    nl, D, S = w.shape
    tm = _row_tile(D, 256)

    def body(s_ref, dm_ref, w_ref, m_ref, v_ref, g_ref, dl_ref, nm_ref, nv_ref):
        g = jnp.dot(s_ref[...], dm_ref[...], preferred_element_type=F32, precision=HIGHEST)
        g, dl, nm, nv = _f_adamw(w_ref[...], m_ref[...], v_ref[...], g, jnp.zeros_like(g))
        g_ref[...] = g
        dl_ref[...] = dl
        nm_ref[...] = nm
        nv_ref[...] = nv

    big = pl.BlockSpec((None, tm, S), lambda l, i: (l, i, 0))
    return _pcall(
        body, name="mod_w_update", grid=(nl, D // tm),
        in_specs=[pl.BlockSpec((tm, 16), lambda l, i: (i, 0)), pl.BlockSpec((None, 16, S), lambda l, i: (l, 0, 0)),
                  big, big, big],
        out_specs=[big] * 4, out_shape=[jax.ShapeDtypeStruct(w.shape, F32)] * 4, compiler_params=_cparams(2),
    )(s16t, dm16, w, m, v)


def _size(shape):
    n = 1
    for d in shape:
        n *= d
    return n


def _pack(arrs):
    pieces = []
    for a in arrs:
        flat = a.reshape(-1).astype(F32)
        pieces.append(jnp.pad(flat, (0, _round_up(flat.shape[0], LANE) - flat.shape[0])).reshape(-1, LANE))
    buf = jnp.concatenate(pieces, axis=0)
    return jnp.pad(buf, ((0, _round_up(buf.shape[0], SUBLANE) - buf.shape[0]), (0, 0)))


def _unpack(buf, shapes):
    lead = buf.shape[:-2]
    out, row = [], 0
    for s in shapes:
        n = _size(s)
        rows = _cdiv(n, LANE)
        piece = buf[..., row:row + rows, :].reshape(lead + (rows * LANE,))
        out.append(piece[..., :n].reshape(lead + tuple(s)))
        row += rows
    return out


def _adamw_many(name, ws, ms, vs, gs):
    n = len(ws)

    def body(*refs):
        for k in range(n):
            res = _f_adamw(refs[k][...], refs[n + k][...], refs[2 * n + k][...], refs[3 * n + k][...], 0.0)
            for j in range(4):
                refs[(4 + j) * n + k][...] = res[j]

    vmem = pl.BlockSpec(memory_space=pltpu.VMEM)
    res = _pcall(body, name=name, out_shape=[jax.ShapeDtypeStruct(w.shape, F32) for _ in range(4) for w in ws],
                 in_specs=[vmem] * (4 * n), out_specs=[vmem] * (4 * n))(*ws, *ms, *vs, *gs)
    return [tuple(res[j * n + k] for j in range(4)) for k in range(n)]


SHARD_AXIS = {
    "mod_w": 2, "ssd_w_in": 2, "ssd_conv_w": 2, "ssd_w_out": 1, "conf_w_pw1": 2, "conf_b_pw1": 1, "conf_w_dw": 2,
    "conf_b_dw": 1, "conf_ln_w": 1, "conf_ln_b": 1, "conf_w_pw2": 1, "conf_b_pw2": 1, "ffn_w_up": 2,
    "ffn_conv_w": 3, "ffn_w_down": 1,
}
BIG = ("ssd_w_in", "ssd_w_out", "conf_w_pw1", "conf_w_pw2", "ffn_w_up", "ffn_w_down")
WEIGHTS = ("c_ctx", "mod_w", "mod_b", "norm1_w", "norm2_w", "ssd_w_in", "ssd_conv_w", "ssd_conv_b", "ssd_dt_bias",
           "ssd_a_log", "ssd_d", "ssd_norm_w", "ssd_w_out", "conf_w_pw1", "conf_b_pw1", "conf_w_dw", "conf_b_dw",
           "conf_ln_w", "conf_ln_b", "conf_w_pw2", "conf_b_pw2", "ffn_w_up", "ffn_conv_w", "ffn_conv_b",
           "ffn_w_down", "final_norm_w")
SMALL = tuple(n for n in WEIGHTS if n not in BIG and n != "mod_w")
SMALL_SHARDED = tuple(n for n in SMALL if n in SHARD_AXIS)


def _unshard(stacked, axis):
    return jnp.concatenate([stacked[k] for k in range(N_CHIPS)], axis=axis)


def _to_blocks(full, axis):
    return jnp.stack(jnp.split(full, N_CHIPS, axis=axis))


def _par(v):
    v = v.reshape(-1, v.shape[-1])
    return v[:, None, :]


def kernel(x, c, ctx, c_ctx, mod_w, mod_b, norm1_w, norm2_w, ssd_w_in, ssd_conv_w, ssd_conv_b, ssd_dt_bias, ssd_a_log, ssd_d, ssd_norm_w, ssd_w_out, conf_w_pw1, conf_b_pw1, conf_w_dw, conf_b_dw, conf_ln_w, conf_ln_b, conf_w_pw2, conf_b_pw2, ffn_w_up, ffn_conv_w, ffn_conv_b, ffn_w_down, final_norm_w, loss_target, m_c_ctx, m_mod_w, m_mod_b, m_norm1_w, m_norm2_w, m_ssd_w_in, m_ssd_conv_w, m_ssd_conv_b, m_ssd_dt_bias, m_ssd_a_log, m_ssd_d, m_ssd_norm_w, m_ssd_w_out, m_conf_w_pw1, m_conf_b_pw1, m_conf_w_dw, m_conf_b_dw, m_conf_ln_w, m_conf_ln_b, m_conf_w_pw2, m_conf_b_pw2, m_ffn_w_up, m_ffn_conv_w, m_ffn_conv_b, m_ffn_w_down, m_final_norm_w, v_c_ctx, v_mod_w, v_mod_b, v_norm1_w, v_norm2_w, v_ssd_w_in, v_ssd_conv_w, v_ssd_conv_b, v_ssd_dt_bias, v_ssd_a_log, v_ssd_d, v_ssd_norm_w, v_ssd_w_out, v_conf_w_pw1, v_conf_b_pw1, v_conf_w_dw, v_conf_b_dw, v_conf_ln_w, v_conf_ln_b, v_conf_w_pw2, v_conf_b_pw2, v_ffn_w_up, v_ffn_conv_w, v_ffn_conv_b, v_ffn_w_down, v_final_norm_w):
    given = dict(locals())
    W = {n: given[n] for n in WEIGHTS}
    Mo = {n: given["m_" + n] for n in WEIGHTS}
    Vo = {n: given["v_" + n] for n in WEIGHTS}

    ax, ay, ac = lax.axis_index("x"), lax.axis_index("y"), lax.axis_index("c")
    chip = 2 * ax + ay
    dev = 4 * ax + 2 * ay + ac

    D = x.shape[-1]
    L, Lc = x.shape[1], ctx.shape[1]
    T0 = L + Lc
    H = ssd_a_log.shape[-1]
    DI = ssd_norm_w.shape[-1]
    P = DI // H
    CD = ssd_conv_b.shape[-1]
    N = SSD_STATE
    G = (CD - DI) // (2 * N)
    FH = ffn_conv_b.shape[-1]
    KS = ssd_conv_w.shape[1]
    KC = conf_w_dw.shape[1]
    ncc = Lc // SSD_CHUNK

    shard_b = {n: W[n].astype(BF16) for n in BIG}

    small_shard_shapes = [W[n].shape for n in SMALL_SHARDED]
    f1 = _allgather8("gather_small", _pack([c] + [W[n] for n in SMALL_SHARDED]))
    parts = _unpack(f1, [c.shape] + small_shard_shapes)
    Wf = dict(W)
    for n, p in zip(SMALL_SHARDED, parts[1:]):
        Wf[n] = _unshard(p[::2], SHARD_AXIS[n])
    c16 = jnp.concatenate([parts[0].reshape(N_DEV, D), c_ctx[None, :], jnp.zeros((16 - N_DEV - 1, D), F32)], axis=0)

    S_mod = mod_w.shape[-1]
    mod_b_shard = lax.dynamic_slice_in_dim(mod_b, chip * S_mod, S_mod, axis=1)[:, None, :]
    mod_part = _mod_fwd(c16, mod_w, mod_b_shard)
    f2 = _allgather8("gather_mod", mod_part.reshape(2 * 16, S_mod))
    mods = jnp.concatenate([f2[2 * k].reshape(2, 16, S_mod) for k in range(N_CHIPS)], axis=-1)
    my = lax.dynamic_slice_in_dim(mods, dev, 1, axis=1)[:, 0]
    sh1, sc1, g1, sh2, sc2, g2 = [[my[l, k * D:(k + 1) * D] for l in range(2)] for k in range(6)]
    csh1, csc1 = mods[0, N_DEV, 0:D], mods[0, N_DEV, D:2 * D]

    in_halves = shard_b["ssd_w_in"].reshape(2, D // 2, ssd_w_in.shape[-1])
    gather_a, token = _exchange4_start("gather_w_in_start", [in_halves], True, mods, half=True)
    csc1 = _tie("tie_gather_w_in", csc1, token)

    def full_weight(n, own, landed):
        if landed.ndim == own.ndim:
            ax = SHARD_AXIS[n]
            return lax.dynamic_update_slice_in_dim(landed, own, chip * own.shape[ax], ax)
        return _fill_own(landed, own, chip, True)

    xl = x[0]
    rows0 = (ctx[0], xl)
    n1w0, n1w1 = _par(norm1_w[0]), _par(norm1_w[1])
    sc_seg = jnp.stack([csc1, sc1[0]])[:, None, :]
    sh_seg = jnp.stack([csh1, sh1[0]])[:, None, :]

    a0 = _rw_fwd("l0_modnorm1", _f_modnorm, [], [n1w0, sc_seg, sh_seg], [D], T=T0, seg_rows=(Lc,), head=rows0,
                 out_dtypes=[BF16])
    rest = [n for n in BIG if n != "ssd_w_in"]
    for n in rest:
        a0 = _tie("tie_cast_" + n, a0, shard_b[n])
    (own_in,), (landed_in,) = _exchange4_wait("gather_w_in_wait", gather_a, a0)
    mine = _fill_own(landed_in, lax.dynamic_index_in_dim(own_in, ac, 0, keepdims=False), chip, True)
    (halves,) = _swap_sibling("swap_w_in", [mine], by_core=True)
    halves = lax.dynamic_update_index_in_dim(halves, mine, ac, 0)
    w_in = jnp.concatenate([halves[:, k].reshape(D, -1) for k in range(N_CHIPS)], axis=1)
    landed_in = halves
    def start_gather(tag, names, dep):
        handle, tok = _exchange4_start("gather_" + tag + "_start", [shard_b[n] for n in names], True, dep,
                                       axes=[1 if SHARD_AXIS[n] == 1 else None for n in names])
        return (names, handle), tok

    def finish_gather(tag, group, after):
        names, handle = group
        return {n: full_weight(n, own, g)
                for n, own, g in zip(names, *_exchange4_wait("gather_" + tag + "_wait", handle, after))}

    gather_b, token = start_gather("mix", ["ssd_w_out", "conf_w_pw1", "conf_w_pw2"], landed_in)
    gather_c, token = start_gather("ffn", ["ffn_w_up", "ffn_w_down"], token)
    a0 = _tie("tie_gather_rest", a0, token)
    proj = _mm(a0, w_in, name="l0_w_in")
    seg_taps = [(k - KS // 2, ("seg", Lc)) for k in range(KS)]
    xbc_pre, xbc = _conv_fwd("l0_conv", proj, DI, CD, Wf["ssd_conv_w"][0], ssd_conv_b, seg_taps, act=True)
    dt_raw = proj[:, DI + CD:]
    dt_bias = _par(ssd_dt_bias.reshape(1, 2 * H))
    dt = _rw_fwd("l0_softplus", _f_softplus, [dt_raw], [dt_bias], [2 * H])
    dt_t = dt.T
    dtr = (dt_t[:H, None, :], dt_t[H:, None, :])
    a_all = -jnp.exp(ssd_a_log.reshape(2, H, 1, 1))
    a_neg = (a_all[0], a_all[1])
    (y_f, y_b), s_enter = _ssd_fwd(xbc, DI, DI + G * N, dtr, a_neg, P, ncc)
    gate_rows = [y_f, y_b, (xbc, 0, DI, Lc), (proj, 0, DI, Lc)]
    d_rep = _par(jnp.repeat(ssd_d[0], P))
    ssd_nw = _par(ssd_norm_w[0])
    yn = _rw_fwd("l0_ssd_gate", _f_ssd_gate, gate_rows, [d_rep, ssd_nw], [DI], T=L, out_dtypes=[BF16])
    Wb = finish_gather("mix", gather_b, yn)
    w_out, w_pw1, w_pw2 = Wb["ssd_w_out"][0], Wb["conf_w_pw1"], Wb["conf_w_pw2"][0]
    mix0 = _mm(yn, w_out, name="l0_w_out")
    g1_0, g2_0, g1_1, g2_1 = _par(g1[0]), _par(g2[0]), _par(g1[1]), _par(g2[1])
    h1 = _rw_fwd("l0_res1", _f_gate_res, [xl, mix0], [g1_0], [D])
    Wb = finish_gather("ffn", gather_c, h1)
    w_up, w_dn = Wb["ffn_w_up"], Wb["ffn_w_down"]

    grid_taps = [((i - 1) * GRID_W + (j - 1), (None if j == 1 else ("col", j - 1))) for i in range(3) for j in range(3)]

    def ffn_fwd(l, h, tag):
        a = _rw_fwd(tag + "_modnorm2", _f_modnorm, [h], [_par(norm2_w[l]), _par(sc2[l]), _par(sh2[l])], [D],
                    out_dtypes=[BF16])
        hh = _mm(a, w_up, b_lead=l, b_shards=N_CHIPS, name=tag + "_w_up")
        gc = _conv_fwd(tag + "_ffn_conv", hh, FH, FH, Wf["ffn_conv_w"][l].reshape(9, FH), ffn_conv_b[l][None, :],
                       grid_taps)
        act = _rw_fwd(tag + "_act", _f_ffn_act, [(hh, 0, FH), gc], [], [FH], col_tile=_tile(FH, 1536),
                      out_dtypes=[BF16])
        dn = _mm(act, w_dn, b_lead=l, name=tag + "_w_down")
        return a, hh, gc, act, dn

    a1, hh0, gc0, act0, dn0 = ffn_fwd(0, h1, "l0")
    h2 = _rw_fwd("l0_res2", _f_gate_res, [h1, dn0], [g2_0], [D])

    a2 = _rw_fwd("l1_modnorm1", _f_modnorm, [h2], [n1w1, _par(sc1[1]), _par(sh1[1])], [D], out_dtypes=[BF16])
    pw = _mm(a2, w_pw1, b_lead=0, b_shards=N_CHIPS, name="l1_pw1")
    b_pw1 = Wf["conf_b_pw1"][0]
    glu = _rw_fwd("l1_glu", _f_glu, [(pw, 0, D), (pw, D, D)], [_par(b_pw1[:D]), _par(b_pw1[D:])], [D])
    conf_taps = [(k - KC // 2, None) for k in range(KC)]
    cv = _conv_fwd("l1_conv", glu, 0, D, Wf["conf_w_dw"][0], Wf["conf_b_dw"], conf_taps)
    ln_w, ln_b = _par(Wf["conf_ln_w"][0]), _par(Wf["conf_ln_b"][0])
    ls = _rw_fwd("l1_ln_silu", _f_ln_silu, [cv], [ln_w, ln_b], [D], out_dtypes=[BF16])
    p2 = _mm(ls, w_pw2, name="l1_pw2")
    b_pw2 = _par(Wf["conf_b_pw2"][0])
    h3 = _rw_fwd("l1_res1", _f_gate_res_bias, [h2, p2], [g1_1, b_pw2], [D])
    a3, hh1, gc1, act1, dn1 = ffn_fwd(1, h3, "l1")
    h4 = _rw_fwd("l1_res2", _f_gate_res, [h3, dn1], [g2_1], [D])

    fnw = final_norm_w[None, :]
    tgt = loss_target[0]
    loss_local = _loss_fwd(h4, tgt, fnw)[0, 0]

    G_full = {}
    reduces = {}

    def start_reduce(tag, items, dep):
        def blocks_of(g, ax):
            if g.ndim == 3:
                return g
            return g.reshape(N_CHIPS, g.shape[0] // N_CHIPS, g.shape[1]) if ax == 0 else _to_blocks(g, ax)

        blocks = [blocks_of(g, ax).astype(BF16) for _, g, ax in items]
        handle, tok = _exchange4_start("reduce_" + tag + "_start", blocks, False, dep)
        reduces[tag] = ([n for n, _, _ in items], handle)
        return tok
    ones = jnp.ones((L, 1), F32)
    (dh4,), (dfnw,) = _rw_bwd("loss_bwd", _f_loss_rows, [h4, tgt], [_par(final_norm_w)], [ones],
                              row_grad=[True, False], par_grad=[True])
    G_full["final_norm_w"] = dfnw.reshape(D)

    def ffn_bwd(l, h, saved, g2_l, dh_out, tag):
        a, hh, gc, act, dn = saved
        (ddn,), (dg2,) = _rw_bwd(tag + "_res2_bwd", _f_gate, [dn], [g2_l], [dh_out],
                                 row_grad=[True], par_grad=[True], row_dtypes=[BF16])
        dact = _mm(ddn, w_dn, b_lead=l, tb=True, name=tag + "_w_down_dx")
        dwdn = _mm(act, ddn, ta=True, name=tag + "_w_down_dw", out_dtype=BF16)
        (dval, dgc), _ = _rw_bwd(tag + "_act_bwd", _f_ffn_act, [(hh, 0, FH), gc], [], [dact],
                                 row_grad=[True, True], par_grad=[], col_tile=_tile(FH, 1536), row_dtypes=[BF16, F32])
        dgin, dcw, dcb = _conv_bwd(tag + "_ffn_conv_bwd", hh, FH, FH, Wf["ffn_conv_w"][l].reshape(9, FH), dgc,
                                   grid_taps, du_dtype=BF16)
        dhh = jnp.concatenate([dval, dgin], axis=1)
        da = _mm(dhh, w_up, b_lead=l, b_shards=N_CHIPS, tb=True, name=tag + "_w_up_dx")
        dwup = _mm(a, dhh, ta=True, name=tag + "_w_up_dw", out_dtype=BF16, col_blocks=N_CHIPS)
        (dh,), (dn2w, dsc2, dsh2) = _rw_bwd(
            tag + "_modnorm2_bwd", _f_modnorm, [h], [_par(norm2_w[l]), _par(sc2[l]), _par(sh2[l])], [da],
            row_grad=[True], par_grad=[True, True, True], add=dh_out)
        return dh, dict(w_down=dwdn, w_up=dwup, conv_w=dcw.reshape(3, 3, FH), conv_b=dcb.reshape(FH),
                        n2w=dn2w.reshape(D), sc2=dsc2.reshape(D), sh2=dsh2.reshape(D), g2=dg2.reshape(D))

    dh3, gf1 = ffn_bwd(1, h3, (a3, hh1, gc1, act1, dn1), g2_1, dh4, "l1")
    (dp2,), (dg1_1, db_pw2) = _rw_bwd("l1_res1_bwd", _f_gate_bias, [p2], [g1_1, b_pw2], [dh3],
                                      row_grad=[True], par_grad=[True, True], row_dtypes=[BF16])
    dls = _mm(dp2, w_pw2, tb=True, name="l1_pw2_dx")
    dw_pw2 = _mm(ls, dp2, ta=True, name="l1_pw2_dw", out_dtype=BF16)
    (dcv,), (dln_w, dln_b) = _rw_bwd("l1_ln_silu_bwd", _f_ln_silu, [cv], [ln_w, ln_b], [dls],
                                     row_grad=[True], par_grad=[True, True])
    dglu, dw_dw, db_dw = _conv_bwd("l1_conv_bwd", glu, 0, D, Wf["conf_w_dw"][0], dcv, conf_taps)
    (dpa, dpg), (dba, dbg) = _rw_bwd("l1_glu_bwd", _f_glu, [(pw, 0, D), (pw, D, D)],
                                     [_par(b_pw1[:D]), _par(b_pw1[D:])], [dglu],
                                     row_grad=[True, True], par_grad=[True, True], row_dtypes=[BF16, BF16])
    dpw = jnp.concatenate([dpa, dpg], axis=1)
    da2 = _mm(dpw, w_pw1, b_lead=0, b_shards=N_CHIPS, tb=True, name="l1_pw1_dx")
    dw_pw1 = _mm(a2, dpw, ta=True, name="l1_pw1_dw", out_dtype=BF16, col_blocks=N_CHIPS)
    (dh2,), (dn1w1, dsc1_1, dsh1_1) = _rw_bwd(
        "l1_modnorm1_bwd", _f_modnorm, [h2], [n1w1, _par(sc1[1]), _par(sh1[1])], [da2],
        row_grad=[True], par_grad=[True, True, True], add=dh3)
    G_full["conf_b_pw2"] = db_pw2.reshape(1, D)
    G_full["conf_ln_w"], G_full["conf_ln_b"] = dln_w.reshape(1, D), dln_b.reshape(1, D)
    G_full["conf_w_dw"], G_full["conf_b_dw"] = dw_dw[None], db_dw.reshape(1, D)
    G_full["conf_b_pw1"] = jnp.concatenate([dba.reshape(1, D), dbg.reshape(1, D)], axis=1)

    token = start_reduce("l1", [("conf_w_pw2", dw_pw2, 0), ("conf_w_pw1", dw_pw1, 1), ("ffn_w_up1", gf1["w_up"], 1),
                                ("ffn_w_down1", gf1["w_down"], 0)], dw_pw2)
    dh2 = _tie("tie_reduce_l1", dh2, token)
    dh1, gf0 = ffn_bwd(0, h1, (a1, hh0, gc0, act0, dn0), g2_0, dh2, "l0")
    G_full["ffn_conv_w"] = jnp.stack([gf0["conv_w"], gf1["conv_w"]])
    G_full["ffn_conv_b"] = jnp.stack([gf0["conv_b"], gf1["conv_b"]])

    (dmix,), (dg1_0,) = _rw_bwd("l0_res1_bwd", _f_gate, [mix0], [g1_0], [dh1],
                                row_grad=[True], par_grad=[True], row_dtypes=[BF16])
    dyn = _mm(dmix, w_out, tb=True, name="l0_w_out_dx")
    dw_out = _mm(yn, dmix, ta=True, name="l0_w_out_dw", out_dtype=BF16)
    token = start_reduce("l0", [("ffn_w_up0", gf0["w_up"], 1), ("ffn_w_down0", gf0["w_down"], 0),
                                ("ssd_w_out", dw_out, 0)], dw_out)
    dyn = _tie("tie_reduce_l0", dyn, token)
    (dy_lat, dxs_gate, dz_lat), (dd_rep, dssd_nw) = _rw_bwd(
        "l0_ssd_gate_bwd", _f_ssd_gate, gate_rows, [d_rep, ssd_nw], [dyn],
        row_grad=[True, False, True, True], par_grad=[True, True], T=L, row_dtypes=[F32, F32, BF16])
    g_f, g_b = _ssd_bwd(xbc, DI, DI + G * N, dtr, a_neg, s_enter, dy_lat, P, ncc)
    silu_bwd = functools.partial(_rw_bwd, f=_silu, pars=[], row_grad=[True], par_grad=[], T=T0)
    (dxs_pre,), _ = silu_bwd("l0_silu_bwd_x", rows=[(xbc_pre, 0, DI)], cot_fn=lambda p, q, r: p + q + r,
                             cots=[g_f[0], g_b[0], (dxs_gate, 0, DI, -Lc)],
                             col_tile=_tile(DI, 1024))
    (db_pre,), _ = silu_bwd("l0_silu_bwd_b", rows=[(xbc_pre, DI, G * N)], cot_fn=lambda p, q: p + q,
                            cots=[g_f[1], g_b[1]], col_tile=_tile(G * N, 1024))
    (dc_pre,), _ = silu_bwd("l0_silu_bwd_c", rows=[(xbc_pre, DI + G * N, G * N)], cot_fn=lambda p, q: p + q,
                            cots=[g_f[2], g_b[2]], col_tile=_tile(G * N, 1024))
    conv_w0 = Wf["ssd_conv_w"][0]
    pieces = []
    for tag, off, width, g_pre in (("x", 0, DI, dxs_pre), ("b", DI, G * N, db_pre), ("c", DI + G * N, G * N, dc_pre)):
        pieces.append(_conv_bwd("l0_conv_bwd_" + tag, proj, DI + off, width, conv_w0[:, off:off + width], g_pre,
                                seg_taps, du_dtype=BF16))
    dconv_in = [p[0] for p in pieces]
    dcw0 = jnp.concatenate([p[1] for p in pieces], axis=1)
    dcb0 = jnp.concatenate([p[2] for p in pieces], axis=1)
    ddt = jnp.concatenate([g_f[3][:, 0, :].T, g_b[3][:, 0, :].T], axis=1)
    (ddt_raw,), (ddt_bias,) = _rw_bwd("l0_softplus_bwd", _f_softplus, [dt_raw], [dt_bias], [ddt],
                                      row_grad=[True], par_grad=[True], row_dtypes=[BF16])
    dproj = jnp.concatenate([jnp.pad(dz_lat, ((Lc, 0), (0, 0))), *dconv_in, ddt_raw], axis=1)
    da0 = _mm(dproj, w_in, tb=True, name="l0_w_in_dx")
    dw_in = _mm(a0, dproj, ta=True, name="l0_w_in_dw", out_dtype=BF16)
    token = start_reduce("in", [("ssd_w_in", dw_in, 1)], dw_in)
    da0 = _tie("tie_reduce_in", da0, token)
    (dhcat,), (dn1w0, dsc_seg, dsh_seg) = _rw_bwd(
        "l0_modnorm1_bwd", _f_modnorm, [], [n1w0, sc_seg, sh_seg], [da0], T=T0, head=rows0,
        row_grad=[True], par_grad=[True, True, True], seg_rows=(Lc,), add=(dh1, 0, D, -Lc), skip_rows=Lc)
    grad_x = dhcat[None]

    da_heads = jnp.stack([g[4][..., 0, 0].sum(axis=1).reshape(H) for g in (g_f, g_b)])[None]
    G_full["ssd_a_log"] = da_heads * (-jnp.exp(ssd_a_log))
    G_full["ssd_dt_bias"] = ddt_bias.reshape(1, 2, H)
    G_full["ssd_d"] = dd_rep.reshape(H, P).sum(axis=1)[None]
    G_full["ssd_norm_w"] = dssd_nw.reshape(1, DI)
    G_full["ssd_conv_w"], G_full["ssd_conv_b"] = dcw0[None], dcb0.reshape(1, CD)
    G_full["norm1_w"] = jnp.stack([dn1w0.reshape(D), dn1w1.reshape(D)])
    G_full["norm2_w"] = jnp.stack([gf0["n2w"], gf1["n2w"]])

    zD = jnp.zeros((D,), F32)
    dm_own = jnp.stack([
        jnp.concatenate([dsh_seg[1, 0], dsc_seg[1, 0], dg1_0.reshape(D), gf0["sh2"], gf0["sc2"], gf0["g2"]]),
        jnp.concatenate([dsh1_1.reshape(D), dsc1_1.reshape(D), dg1_1.reshape(D), gf1["sh2"], gf1["sc2"], gf1["g2"]]),
    ])
    dmc_own = jnp.concatenate([dsh_seg[0, 0], dsc_seg[0, 0], zD, zD, zD, zD])

    out = {}

    def finish_reduce(tags, after, swap_name):
        partial = {}
        for tag in tags:
            names, handle = reduces[tag]
            blocks, landed = _exchange4_wait("reduce_" + tag + "_wait", handle, after)
            for n, blk, own in zip(names, landed, blocks):
                r = _fill_own(blk, own, chip, False)
                partial[n] = _sum_leading("sum4_" + n, r.reshape(N_CHIPS, -1, r.shape[-1]),
                                          (0, 1, 2, 3), out_dtype=BF16).reshape(r.shape[1:])
        for n in ("ffn_w_up", "ffn_w_down"):
            if n + "0" in partial:
                partial[n] = jnp.stack([partial.pop(n + "0"), partial.pop(n + "1")])
        names = [n for n in BIG if n in partial]
        mine = [partial[n].reshape(W[n].shape) for n in names]
        for n, own, sib in zip(names, mine, _swap_sibling(swap_name, mine)):
            out[n] = _adamw("adamw_" + n, W[n], Mo[n], Vo[n], own, sib)
        return names

    early = finish_reduce(["l1", "l0"], dhcat, "swap_grads_early")

    small_sum_names = [n for n in SMALL if n not in ("c_ctx", "mod_b")]
    sum_part = [G_full[n] for n in small_sum_names] + [dmc_own, loss_local.reshape(1)]
    packed = _tie("tie_small_grads", _pack(sum_part + [dm_own]), out[early[-1]][1])
    gat = _allgather8("gather_small_grads", packed)
    total = _sum_leading("sum_small_grads", gat, tuple(range(N_DEV)))
    summed = _unpack(total, [a.shape for a in sum_part])
    Gs = dict(zip(small_sum_names, summed[:-2]))
    dmc_tot, loss = summed[-2], summed[-1][0]
    dm_all = _unpack(gat, [a.shape for a in sum_part] + [dm_own.shape])[-1].transpose(1, 0, 2)
    dm16 = jnp.concatenate([dm_all, jnp.stack([dmc_tot, jnp.zeros_like(dmc_tot)])[:, None, :],
                            jnp.zeros((2, 16 - N_DEV - 1, 6 * D), F32)], axis=1)
    Gs["mod_b"] = _sum_leading("sum_mod_b", dm16.transpose(1, 0, 2).reshape(16, 2 * 6 * D // LANE, LANE),
                               tuple(range(N_DEV + 1))).reshape(2, 6 * D)

    dm16_shard = lax.dynamic_slice_in_dim(dm16, chip * S_mod, S_mod, axis=2)
    ds16 = _mm(dm16_shard[0], mod_w.reshape(2 * D, S_mod), tb=True, precision=HIGHEST, name="c_ctx_dx")
    sig = jax.nn.sigmoid(c_ctx)
    dcc_part = ds16[N_DEV, :D] * (sig * (1.0 + c_ctx * (1.0 - sig)))
    gat_cc = _allgather8("gather_c_ctx_grad", _pack([dcc_part]))
    Gs["c_ctx"] = _sum_leading("sum_c_ctx_grad", gat_cc, (0, 2, 4, 6)).reshape(-1)[:D]

    s16t = _silu(c16).T
    out["mod_w"] = _mod_w_update(s16t, dm16_shard, mod_w, m_mod_w, v_mod_w)
    finish_reduce(["in"], out["mod_w"][0], "swap_grads_late")

    def own(n, full):
        if n in SHARD_AXIS:
            size = W[n].shape[SHARD_AXIS[n]]
            return lax.dynamic_slice_in_dim(full, chip * size, size, axis=SHARD_AXIS[n])
        return full

    def two_d(a):
        return a.reshape(1, -1) if a.ndim == 1 else a

    g_small = [own(n, Gs[n].reshape(Wf[n].shape)) for n in SMALL]
    res = _adamw_many("adamw_small", [two_d(W[n]) for n in SMALL], [two_d(Mo[n]) for n in SMALL],
                      [two_d(Vo[n]) for n in SMALL], [two_d(g) for g in g_small])
    for n, r in zip(SMALL, res):
        out[n] = tuple(t.reshape(W[n].shape) for t in r)

    grads = [out[n][0] for n in WEIGHTS]
    deltas = [out[n][1] for n in WEIGHTS]
    new_m = [out[n][2] for n in WEIGHTS]
    new_v = [out[n][3] for n in WEIGHTS]
    return (loss, grad_x, *grads, *deltas, *new_m, *new_v)
```

```python
import functools

import jax
import jax.numpy as jnp
from jax import lax
from jax.experimental import pallas as pl
from jax.experimental.pallas import tpu as pltpu

F32 = jnp.float32
BF16 = jnp.bfloat16
MESH = pl.DeviceIdType.MESH
HIGHEST = lax.Precision.HIGHEST

VMEM_LIMIT_BYTES = 48 * 1024 * 1024
LANE = 128
SUBLANE = 8

SSD_STATE = 128
SSD_CHUNK = 128
GRID_W = 64
EPS = 1e-6
N_CHIPS = 4
N_DEV = 8

ADAM_LR = 0.001
ADAM_B1 = 0.9
ADAM_B2 = 0.999
ADAM_EPS = 1e-08
ADAM_WD = 0.01
ADAM_STEP = 10


def _pcall(body, **kw):
    return pl.pallas_call(body, **kw)


def _cparams(n_grid):
    return pltpu.CompilerParams(dimension_semantics=("arbitrary",) * n_grid, vmem_limit_bytes=VMEM_LIMIT_BYTES)


def _cdiv(a, b):
    return -(-a // b)


def _round_up(a, b):
    return _cdiv(a, b) * b


def _tile(n, cap):
    if n <= cap:
        return n
    best = None
    for t in range(LANE, cap + 1, LANE):
        if n % t == 0:
            best = t
    if best is None:
        npad = _round_up(n, LANE)
        for t in range(LANE, cap + 1, LANE):
            if npad % t == 0:
                best = t
    return best


def _row_tile(n, cap, also=()):
    best = None
    for step in (2 * SUBLANE, SUBLANE):
        for t in range(step, min(cap, n) + 1, step):
            if n % t == 0 and all(a % t == 0 for a in also):
                best = t
        if best is not None:
            break
    assert best is not None, (n, cap, also)
    return best


def _silu(v):
    return v * jax.nn.sigmoid(v)


def _mm(a, b, *, name, ta=False, tb=False, precision=None, cap=1024, out_dtype=F32, col_blocks=None,
        b_lead=None, b_shards=None):
    M, K = (a.shape[1], a.shape[0]) if ta else a.shape
    b_dims = b.shape[(b_lead is not None) + (b_shards is not None):]
    b_cols = b_dims[1] * (b_shards or 1)
    N, Kb = (b_dims[0], b_cols) if tb else (b_cols, b_dims[0])
    assert K == Kb, (a.shape, b.shape, ta, tb)
    n_cut, k_cut = (1, b_shards or 1) if tb else (b_shards or 1, 1)
    tm, tk = _tile(M, cap), _tile(K // k_cut, cap + cap // 2)
    tn = _tile(N // (col_blocks or n_cut), cap + cap // 2)
    assert b_shards is None or (b_dims[1] % (tk if tb else tn) == 0 and col_blocks is None), (b.shape, tn, tk)
    nm, nn, nk = _cdiv(M, tm), _cdiv(N, tn), _cdiv(K, tk)
    k_tail = K % tk
    exact = precision is not None

    def body(a_ref, b_ref, o_ref, acc_ref):
        k = pl.program_id(2)

        @pl.when(k == 0)
        def _():
            acc_ref[...] = jnp.zeros_like(acc_ref)

        av = a_ref[...]
        bv = b_ref[...]
        if k_tail:
            lim = K - k * tk
            ka = lax.broadcasted_iota(jnp.int32, av.shape, 0 if ta else 1)
            kb = lax.broadcasted_iota(jnp.int32, bv.shape, 1 if tb else 0)
            av = jnp.where(ka < lim, av, jnp.zeros_like(av))
            bv = jnp.where(kb < lim, bv, jnp.zeros_like(bv))
        if exact:
            av = av.astype(F32)
            bv = bv.astype(F32)
        else:
            av = av.astype(BF16)
            bv = bv.astype(BF16)
        dn = (((0 if ta else 1,), (1 if tb else 0,)), ((), ()))
        acc_ref[...] += lax.dot_general(av, bv, dn, preferred_element_type=F32, precision=precision)

        @pl.when(k == nk - 1)
        def _():
            o_ref[...] = acc_ref[...].astype(o_ref.dtype)

    a_spec = pl.BlockSpec((tk, tm), lambda i, j, k: (k, i)) if ta else pl.BlockSpec((tm, tk), lambda i, j, k: (i, k))
    b_spec = pl.BlockSpec((tn, tk), lambda i, j, k: (j, k)) if tb else pl.BlockSpec((tk, tn), lambda i, j, k: (k, j))
    if b_lead is not None or b_shards is not None:
        b_blk, b_map = tuple(b_spec.block_shape), b_spec.index_map
        lead = () if b_lead is None else (b_lead,)
        per = None if b_shards is None else b_dims[1] // b_blk[1]

        def b_index(i, j, k):
            r, c = b_map(i, j, k)
            return lead + (r, c) if per is None else (c // per,) + lead + (r, c % per)

        b_spec = pl.BlockSpec((None,) * (len(lead) + (per is not None)) + b_blk, b_index)
    if col_blocks is None:
        out_spec = pl.BlockSpec((tm, tn), lambda i, j, k: (i, j))
        out_shape = jax.ShapeDtypeStruct((M, N), out_dtype)
    else:
        per = (N // col_blocks) // tn
        assert per * tn * col_blocks == N, (N, col_blocks, tn)
        out_spec = pl.BlockSpec((None, tm, tn), lambda i, j, k: (j // per, i, j % per))
        out_shape = jax.ShapeDtypeStruct((col_blocks, M, N // col_blocks), out_dtype)
    return _pcall(
        body, name=name, grid=(nm, nn, nk), in_specs=[a_spec, b_spec], out_specs=out_spec, out_shape=out_shape,
        scratch_shapes=[pltpu.VMEM((tm, tn), F32)], compiler_params=_cparams(3),
    )(a, b)


def _norm_rows(rows):
    out = []
    for r in rows:
        if not isinstance(r, tuple):
            r = (r,)
        arr, off, width, roff = (r + (0, None, 0)[len(r) - 1:])
        out.append((arr, off, width if width is not None else arr.shape[1], roff))
    return out


def _rw_plan(T, rows, pars, seg_rows, col_tile, tm_cap):
    widths = [r[2] for r in rows]
    wmax = max(widths + [p.shape[-1] for p in pars] + [1])
    if col_tile is not None:
        assert all(w == widths[0] for w in widths) and all(p.shape[-1] == widths[0] for p in pars)
        ncol = widths[0] // col_tile
        assert ncol * col_tile == widths[0]
        wmax = col_tile
    else:
        ncol = 1
    cap = tm_cap if tm_cap is not None else max(SUBLANE, min(512, (512 * 1024) // wmax))
    tm = _row_tile(T, cap, also=tuple(seg_rows) + tuple(abs(r[3]) for r in rows if r[3]))
    bounds = tuple(s // tm for s in seg_rows)
    return widths, ncol, tm, bounds


def _rw_specs(rows, pars, ncol, tm, bounds, col_tile):
    def seg(i):
        s = 0
        for b in bounds:
            s = s + (i >= b).astype(jnp.int32)
        return s

    specs = []
    for arr, off, w, roff in rows:
        bw = col_tile if col_tile is not None else w
        assert off % bw == 0 and roff % tm == 0, (off, bw, roff, tm)
        specs.append(pl.BlockSpec((tm, bw), functools.partial(
            lambda j, i, ob, rb, last: (jnp.clip(i + rb, 0, last), ob + j),
            ob=off // bw, rb=roff // tm, last=arr.shape[0] // tm - 1)))
    for p in pars:
        bw = col_tile if col_tile is not None else p.shape[-1]
        if p.shape[0] > 1:
            specs.append(pl.BlockSpec((None, 1, bw), lambda j, i: (seg(i), 0, j)))
        else:
            specs.append(pl.BlockSpec((None, 1, bw), lambda j, i: (0, 0, j)))
    return specs, seg


def _head_rows(head):
    top, bottom = head
    return [(top, 0, None, 0), (bottom, 0, None, -top.shape[0])]


def _rw_fwd(name, f, rows, pars, out_widths, *, T=None, seg_rows=(), col_tile=None, tm_cap=None, out_dtypes=None,
            head=None):
    rows = _norm_rows((_head_rows(head) if head else []) + list(rows))
    T = rows[0][0].shape[0] if T is None else T
    widths, ncol, tm, bounds = _rw_plan(T, rows, pars, seg_rows, col_tile, tm_cap)
    in_specs, _ = _rw_specs(rows, pars, ncol, tm, bounds, col_tile)
    nr, npar, nout = len(rows), len(pars), len(out_widths)

    def body(*refs):
        vals = [r[...] for r in refs[:nr + npar]]
        if head:
            vals = [jnp.where(pl.program_id(1) < head[0].shape[0] // tm, vals[0], vals[1])] + vals[2:]
        outs = f(*vals)
        if not isinstance(outs, (tuple, list)):
            outs = (outs,)
        for o_ref, o in zip(refs[nr + npar:], outs):
            o_ref[...] = o.astype(o_ref.dtype)

    out_specs = [pl.BlockSpec((tm, col_tile if col_tile is not None else w), lambda j, i: (i, j)) for w in out_widths]
    res = _pcall(
        body, name=name, grid=(ncol, T // tm), in_specs=in_specs, out_specs=out_specs,
        out_shape=[jax.ShapeDtypeStruct((T, w), dt) for w, dt in zip(out_widths, out_dtypes or [F32] * nout)],
        compiler_params=_cparams(2),
    )(*[r[0] for r in rows], *pars)
    return res if nout > 1 else res[0]


def _rw_bwd(name, f, rows, pars, cots, *, row_grad, par_grad, T=None, seg_rows=(), col_tile=None, tm_cap=None,
            add=None, cot_fn=None, row_dtypes=None, head=None, skip_rows=0):
    rows = _norm_rows((_head_rows(head) if head else []) + list(rows))
    cots = _norm_rows(cots)
    T = rows[0][0].shape[0] if T is None else T
    extra = _norm_rows([add]) if add is not None else []
    all_rows = rows + cots + extra
    widths, ncol, tm, bounds = _rw_plan(T, all_rows, pars, tuple(seg_rows) + ((skip_rows,) if skip_rows else ()),
                                        col_tile, tm_cap)
    bounds = bounds[:len(seg_rows)]
    in_specs, seg = _rw_specs(all_rows, pars, ncol, tm, bounds, col_tile)
    nr, nc, ne, npar = len(rows), len(cots), len(extra), len(pars)
    skip = 1 if head else 0
    widths = widths[skip:]
    nrf = nr - skip
    row_idx = [k for k in range(nrf) if row_grad[k]]
    par_idx = [k for k in range(npar) if par_grad[k]]

    def body(*refs):
        i = pl.program_id(1)

        def zero_before(vals, ops):
            return [jnp.where(i + c[3] // tm >= 0, v, jnp.zeros_like(v)) if c[3] < 0 else v for v, c in zip(vals, ops)]

        row_vals = [r[...] for r in refs[:nr]]
        if head:
            row_vals = [jnp.where(i < head[0].shape[0] // tm, row_vals[0], row_vals[1])] + row_vals[2:]
        cot_vals = zero_before([r[...] for r in refs[nr:nr + nc]], cots)
        add_vals = zero_before([r[...] for r in refs[nr + nc:nr + nc + ne]], extra)
        par_vals = [r[...] for r in refs[nr + nc + ne:nr + nc + ne + npar]]
        out_refs = refs[nr + nc + ne + npar:]
        outs, vjp = jax.vjp(f, *row_vals, *par_vals)
        if cot_fn is not None:
            cot_vals = cot_fn(*cot_vals)
            if not isinstance(cot_vals, (tuple, list)):
                cot_vals = (cot_vals,)
        if isinstance(outs, (tuple, list)):
            grads = vjp(tuple(c.astype(o.dtype) for c, o in zip(cot_vals, outs)))
        else:
            grads = vjp(cot_vals[0].astype(outs.dtype))
        first_seg = i == 0
        for b in bounds:
            first_seg = first_seg | (i == b)
        for n, k in enumerate(row_idx):
            g = grads[k]
            if n == 0 and add_vals:
                g = g + add_vals[0]
            out_refs[n][...] = g.astype(out_refs[n].dtype)
        for n, k in enumerate(par_idx):
            g = grads[nrf + k]
            o_ref = out_refs[len(row_idx) + n]
            first = first_seg if pars[k].shape[0] > 1 else (i == 0)

            @pl.when(first)
            def _(o_ref=o_ref, g=g):
                o_ref[...] = g

            @pl.when(jnp.logical_not(first))
            def _(o_ref=o_ref, g=g):
                o_ref[...] += g

    out_specs, out_shape = [], []
    for k in row_idx:
        w = widths[k]
        out_specs.append(pl.BlockSpec((tm, col_tile if col_tile is not None else w),
                                      lambda j, i: (jnp.maximum(i - skip_rows // tm, 0), j)))
        out_shape.append(jax.ShapeDtypeStruct((T - skip_rows, w), row_dtypes[len(out_shape)] if row_dtypes else F32))
    for k in par_idx:
        p = pars[k]
        bw = col_tile if col_tile is not None else p.shape[-1]
        if p.shape[0] > 1:
            out_specs.append(pl.BlockSpec((None, 1, bw), lambda j, i: (seg(i), 0, j)))
        else:
            out_specs.append(pl.BlockSpec((None, 1, bw), lambda j, i: (0, 0, j)))
        out_shape.append(jax.ShapeDtypeStruct(p.shape, F32))
    res = _pcall(
        body, name=name, grid=(ncol, T // tm), in_specs=in_specs, out_specs=out_specs, out_shape=out_shape,
        compiler_params=_cparams(2),
    )(*[r[0] for r in all_rows], *pars)
    return list(res[:len(row_idx)]), list(res[len(row_idx):])


def _f_modnorm(h, w, sc, sh):
    y = h * lax.rsqrt(jnp.mean(h * h, axis=-1, keepdims=True) + EPS)
    return (y * w) * (1.0 + sc) + sh


def _f_gate_res(h, y, g):
    return h + g * y


def _f_gate_res_bias(h, y, g, b):
    return h + g * (y + b)


def _f_gate(y, g):
    return g * y


def _f_gate_bias(y, g, b):
    return g * (y + b)


def _f_ffn_act(val, gate):
    return _silu(gate) * val


def _f_softplus(raw, bias):
    v = raw + bias
    return jnp.maximum(v, 0.0) + jnp.log(1.0 + jnp.exp(-jnp.abs(v)))


def _f_ssd_gate(yf, yb, xs, z, d_rep, nw):
    y = (yf + yb + d_rep * xs) * _silu(z)
    return (y * lax.rsqrt(jnp.mean(y * y, axis=-1, keepdims=True) + EPS)) * nw


def _f_glu(a, g, ba, bg):
    return (a + ba) * jax.nn.sigmoid(g + bg)


def _f_ln_silu(h, w, b):
    mu = jnp.mean(h, axis=-1, keepdims=True)
    d = h - mu
    y = d * lax.rsqrt(jnp.mean(d * d, axis=-1, keepdims=True) + EPS)
    return _silu(y * w + b)


def _f_loss_rows(h, t, w):
    y = (h * lax.rsqrt(jnp.mean(h * h, axis=-1, keepdims=True) + EPS)) * w
    e = y - t
    return 0.5 * jnp.mean(e * e, axis=-1, keepdims=True)


def _f_adamw(w, m, v, ga, gb):
    g = ga.astype(F32) + gb
    m = ADAM_B1 * m + (1.0 - ADAM_B1) * g
    v = ADAM_B2 * v + (1.0 - ADAM_B2) * (g * g)
    m_hat = m / (1.0 - ADAM_B1 ** ADAM_STEP)
    v_hat = v / (1.0 - ADAM_B2 ** ADAM_STEP)
    delta = -ADAM_LR * (m_hat / (jnp.sqrt(v_hat) + ADAM_EPS) + ADAM_WD * w)
    return g, delta, m, v


def _adamw(name, w, m, v, ga, gb):
    shape = w.shape
    c = shape[-1]
    two_d = [t.reshape(-1, c) for t in (w, m, v, ga, gb)]
    rows = two_d[0].shape[0]
    pad = _round_up(rows, SUBLANE) - rows
    if pad:
        two_d = [jnp.pad(t, ((0, pad), (0, 0))) for t in two_d]
    outs = _rw_fwd(name, _f_adamw, two_d, [], [c] * 4)
    return tuple(o[:rows].reshape(shape) for o in outs)


def _sum_leading(name, x, idxs, out_dtype=F32):
    _, R, C = x.shape
    tm = _row_tile(R, max(SUBLANE, min(512, (512 * 1024) // C)))

    def body(x_ref, o_ref):
        acc = x_ref[idxs[0]].astype(F32)
        for k in idxs[1:]:
            acc = acc + x_ref[k].astype(F32)
        o_ref[...] = acc.astype(o_ref.dtype)

    return _pcall(
        body, name=name, grid=(R // tm,), in_specs=[pl.BlockSpec((x.shape[0], tm, C), lambda i: (0, i, 0))],
        out_specs=pl.BlockSpec((tm, C), lambda i: (i, 0)), out_shape=jax.ShapeDtypeStruct((R, C), out_dtype),
        compiler_params=_cparams(1),
    )(x)


def _loss_fwd(h, t, w):
    T, D = h.shape
    tm = _row_tile(T, 256)

    def body(h_ref, t_ref, w_ref, o_ref):
        i = pl.program_id(0)
        part = jnp.sum(_f_loss_rows(h_ref[...], t_ref[...], w_ref[...]), axis=0, keepdims=True)
        part = jnp.broadcast_to(part, (1, LANE))

        @pl.when(i == 0)
        def _():
            o_ref[...] = part

        @pl.when(i > 0)
        def _():
            o_ref[...] += part

    return _pcall(
        body, name="loss_fwd", grid=(T // tm,),
        in_specs=[pl.BlockSpec((tm, D), lambda i: (i, 0)), pl.BlockSpec((tm, D), lambda i: (i, 0)),
                  pl.BlockSpec((1, D), lambda i: (0, 0))],
        out_specs=pl.BlockSpec((1, LANE), lambda i: (0, 0)), out_shape=jax.ShapeDtypeStruct((1, LANE), F32),
        compiler_params=_cparams(1),
    )(h, t, w)


CONV_ROWS = 256
CONV_ROWS_FEW_TAPS = 1024
CONV_ACC_ELEMS = 16384


def _col_mask(arg, t):
    col = jnp.bitwise_and(t, GRID_W - 1)
    return (col != 0) if arg < 0 else (col != GRID_W - 1)


def _conv_plan(T, C, taps):
    seg = [m[1] for _, m in taps if m is not None and m[0] == "seg"]
    cap = CONV_ROWS_FEW_TAPS if len(taps) <= 9 else CONV_ROWS
    rc = next(r for r in (1024, 768, 512, 256, LANE) if r <= cap and T % r == 0)
    ct = next((t for t in (512, 256, LANE) if C % t == 0), C)
    reach = max(abs(s) for s, _ in taps)
    hb = next(h for h in (8, 16, 32, 64, 128, 256) if h >= reach and rc % h == 0)
    sub = max(2 * SUBLANE, min(rc, CONV_ACC_ELEMS // ct))
    boundary = None
    if seg:
        inside = seg[0] % rc
        boundary = (seg[0], (inside - reach, inside + reach) if inside else None)
    taps = [(s, None if (m is None or m[0] == "seg") else m[1]) for s, m in taps]
    return rc, ct, hb, sub, T // rc, C // ct, boundary, taps


def _seg_ok(boundary, i, rc, r0, n, s):
    if boundary is None or boundary[1] is None or s == 0 or r0 + n <= boundary[1][0] or r0 >= boundary[1][1]:
        return None
    t = i * rc + r0 + lax.broadcasted_iota(jnp.int32, (n, 1), 0)
    return (t >= boundary[0]) == ((t + s) >= boundary[0])


def _halo_specs(rc, ct, hb, T, off_blocks):
    per = rc // hb
    last = T // hb - 1
    prev = pl.BlockSpec((hb, ct), lambda j, i: (jnp.maximum(i * per - 1, 0), off_blocks + j))
    cur = pl.BlockSpec((rc, ct), lambda j, i: (i, off_blocks + j))
    nxt = pl.BlockSpec((hb, ct), lambda j, i: (jnp.minimum((i + 1) * per, last), off_blocks + j))
    return [prev, cur, nxt]


def _fill_halo(pad_ref, p_ref, c_ref, n_ref, i, nrc, rc, hb, boundary):
    has_prev = i > 0
    has_next = i < nrc - 1
    if boundary is not None:
        has_prev = has_prev & (i * rc != boundary[0])
        has_next = has_next & ((i + 1) * rc != boundary[0])
    pad_ref[0:hb, :] = jnp.where(has_prev, p_ref[...], 0.0)
    pad_ref[hb:hb + rc, :] = c_ref[...]
    pad_ref[hb + rc:hb + rc + hb, :] = jnp.where(has_next, n_ref[...], 0.0)


def _shift_plan(keys):
    count = {}
    for s, m in keys:
        k = (s % SUBLANE, m)
        count[k] = count.get(k, 0) + 1
    slots = {}
    for k, n in sorted(count.items(), key=lambda kv: (kv[0][0], str(kv[0][1]))):
        if k != (0, None) and (n >= 2 or k[1] is not None):
            slots[k] = len(slots)
    return slots


def _build_shifted(copies_ref, slots, pad_ref, keys, i, rc, hb, sub):
    for (r, m), slot in slots.items():
        qs = [s - r for s, mk in keys if (s % SUBLANE, mk) == (r, m)]
        lo, hi = hb + min(qs), hb + rc + max(qs)
        for p in range(lo, hi, sub):
            n = min(sub, hi - p)
            v = pad_ref[p + r:p + r + n, :]
            if m is not None:
                t = i * rc - hb + p + r + lax.broadcasted_iota(jnp.int32, (n, 1), 0)
                v = jnp.where(_col_mask(m, t), v, 0.0)
            copies_ref[slot, p:p + n, :] = v


def _read(copies_ref, slots, pad_ref, s, m, row, n):
    k = (s % SUBLANE, m)
    if k in slots:
        q = s - k[0]
        return copies_ref[slots[k], row + q:row + q + n, :]
    return pad_ref[row + s:row + s + n, :]


def _conv_fwd(name, u, col_off, C, w, b, taps, act=False):
    T = u.shape[0]
    rc, ct, hb, sub, nrc, ncc, boundary, taps = _conv_plan(T, C, taps)
    assert col_off % ct == 0
    K = len(taps)
    keys = [(s, None) for s, _ in taps]
    slots = _shift_plan(keys)
    dirs = sorted({m for _, m in taps if m is not None})

    def body(up, uc, un, w_ref, b_ref, *rest):
        y_ref = rest[0]
        pad_ref, copies_ref = rest[-2], rest[-1]
        i = pl.program_id(1)
        _fill_halo(pad_ref, up, uc, un, i, nrc, rc, hb, boundary)
        _build_shifted(copies_ref, slots, pad_ref, keys, i, rc, hb, sub)
        for r0 in range(0, rc, sub):
            acc = jnp.broadcast_to(b_ref[...], (sub, ct))
            for m in [None] + dirs:
                part = None
                for k, (s, mk) in enumerate(taps):
                    if mk != m:
                        continue
                    v = _read(copies_ref, slots, pad_ref, s, None, hb + r0, sub)
                    ok = _seg_ok(boundary, i, rc, r0, sub, s)
                    term = w_ref[k:k + 1, :] * (v if ok is None else jnp.where(ok, v, 0.0))
                    part = term if part is None else part + term
                if part is None:
                    continue
                if m is not None:
                    t = i * rc + r0 + lax.broadcasted_iota(jnp.int32, (sub, 1), 0)
                    part = jnp.where(_col_mask(m, t), part, 0.0)
                acc = acc + part
            y_ref[r0:r0 + sub, :] = acc
            if act:
                rest[1][r0:r0 + sub, :] = _silu(acc)

    n_out = 2 if act else 1
    res = _pcall(
        body, name=name, grid=(ncc, nrc),
        in_specs=_halo_specs(rc, ct, hb, T, col_off // ct) + [pl.BlockSpec((K, ct), lambda j, i: (0, j)),
                                                              pl.BlockSpec((1, ct), lambda j, i: (0, j))],
        out_specs=[pl.BlockSpec((rc, ct), lambda j, i: (i, j))] * n_out,
        out_shape=[jax.ShapeDtypeStruct((T, C), F32)] * n_out,
        scratch_shapes=[pltpu.VMEM((rc + 2 * hb, ct), F32), pltpu.VMEM((max(len(slots), 1), rc + 2 * hb, ct), F32)],
        compiler_params=_cparams(2),
    )(u, u, u, w, b)
    return res if act else res[0]


def _conv_bwd(name, u, col_off, C, w, g, taps, du_dtype=F32):
    T = u.shape[0]
    rc, ct, hb, sub, nrc, ncc, boundary, taps = _conv_plan(T, C, taps)
    K = len(taps)
    u_keys = [(s, None) for s, _ in taps]
    dirs = sorted({m for _, m in taps if m is not None})
    g_keys = [(-s, m) for s, m in taps] + [(0, m) for m in dirs]
    u_slots, g_slots = _shift_plan(u_keys), _shift_plan(g_keys)

    def body(up, uc, un, gp, gc, gn, w_ref, du_ref, dw_ref, db_ref, upad, gpad, ucopies, gcopies):
        i = pl.program_id(1)
        _fill_halo(upad, up, uc, un, i, nrc, rc, hb, boundary)
        _fill_halo(gpad, gp, gc, gn, i, nrc, rc, hb, boundary)
        _build_shifted(ucopies, u_slots, upad, u_keys, i, rc, hb, sub)
        _build_shifted(gcopies, g_slots, gpad, g_keys, i, rc, hb, sub)

        @pl.when(i == 0)
        def _():
            dw_ref[...] = jnp.zeros_like(dw_ref)
            db_ref[...] = jnp.zeros_like(db_ref)

        def fold(v):
            return jnp.sum(v.reshape(sub // SUBLANE, SUBLANE, ct), axis=0)

        dbs = jnp.zeros((SUBLANE, ct), F32)
        for r0 in range(0, rc, sub):
            dbs = dbs + fold(gpad[hb + r0:hb + r0 + sub, :])
            acc = jnp.zeros((sub, ct), F32)
            for k, (s, m) in enumerate(taps):
                v = _read(gcopies, g_slots, gpad, -s, m, hb + r0, sub)
                ok = _seg_ok(boundary, i, rc, r0, sub, -s)
                acc = acc + w_ref[k:k + 1, :] * (v if ok is None else jnp.where(ok, v, 0.0))
            du_ref[r0:r0 + sub, :] = acc.astype(du_ref.dtype)
        db_ref[...] += jnp.sum(dbs, axis=0, keepdims=True)
        for k, (s, m) in enumerate(taps):
            part = jnp.zeros((SUBLANE, ct), F32)
            for r0 in range(0, rc, sub):
                v = _read(ucopies, u_slots, upad, s, None, hb + r0, sub)
                ok = _seg_ok(boundary, i, rc, r0, sub, s)
                part = part + fold(_read(gcopies, g_slots, gpad, 0, m, hb + r0, sub)
                                   * (v if ok is None else jnp.where(ok, v, 0.0)))
            dw_ref[k:k + 1, :] += jnp.sum(part, axis=0, keepdims=True)

    halo_u = _halo_specs(rc, ct, hb, T, col_off // ct)
    halo_g = _halo_specs(rc, ct, hb, T, 0)
    rows = rc + 2 * hb
    return _pcall(
        body, name=name, grid=(ncc, nrc),
        in_specs=halo_u + halo_g + [pl.BlockSpec((K, ct), lambda j, i: (0, j))],
        out_specs=[pl.BlockSpec((rc, ct), lambda j, i: (i, j)), pl.BlockSpec((K, ct), lambda j, i: (0, j)),
                   pl.BlockSpec((1, ct), lambda j, i: (0, j))],
        out_shape=[jax.ShapeDtypeStruct((T, C), du_dtype), jax.ShapeDtypeStruct((K, C), F32),
                   jax.ShapeDtypeStruct((1, C), F32)],
        scratch_shapes=[pltpu.VMEM((rows, ct), F32), pltpu.VMEM((rows, ct), F32),
                        pltpu.VMEM((max(len(u_slots), 1), rows, ct), F32),
                        pltpu.VMEM((max(len(g_slots), 1), rows, ct), F32)],
        compiler_params=_cparams(2),
    )(u, u, u, g, g, g, w)


def _ssd_group(xg, bm, cm, s_in, *per_head, reverse, P):
    R = len(per_head) // 2
    dtrs, a_s = per_head[:R], per_head[R:]
    q, rp = xg.shape
    ii = lax.broadcasted_iota(jnp.int32, (q, q), 0)
    jj = lax.broadcasted_iota(jnp.int32, (q, q), 1)
    causal = (jj >= ii) if reverse else (jj <= ii)
    causal_t = (ii >= jj) if reverse else (ii <= jj)
    eye = ii == jj
    lane = lax.broadcasted_iota(jnp.int32, (1, rp), 1)
    row = lax.broadcasted_iota(jnp.int32, (rp, 1), 0)
    nt = (((1,), (1,)), ((), ()))
    tn = (((0,), (0,)), ((), ()))
    cb = lax.dot_general(cm.astype(BF16), bm.astype(BF16), nt, preferred_element_type=F32)
    dt_x = jnp.zeros((q, rp), F32)
    acum_x = jnp.zeros((q, rp), F32)
    tot_row = jnp.zeros((1, rp), F32)
    tot_col = jnp.zeros((rp, 1), F32)
    wts, lane_masks = [], []
    for r in range(R):
        hm = (lane >= r * P) & (lane < (r + 1) * P)
        hc = (row >= r * P) & (row < (r + 1) * P)
        dt_c = jnp.sum(jnp.where(eye, dtrs[r], 0.0), axis=1, keepdims=True)
        dac = dt_c * a_s[r]
        dar = dtrs[r] * a_s[r]
        acum_c = jnp.sum(jnp.where(causal, dar, 0.0), axis=1, keepdims=True)
        acum_r = jnp.sum(jnp.where(causal_t, dac, 0.0), axis=0, keepdims=True)
        decay = jnp.where(causal, jnp.exp(jnp.where(causal, acum_c - acum_r, 0.0)), 0.0)
        tot = jnp.sum(dac, axis=0, keepdims=True)
        dt_x = jnp.where(hm, dt_c, dt_x)
        acum_x = jnp.where(hm, acum_c, acum_x)
        tot_row = jnp.where(hm, tot, tot_row)
        tot_col = jnp.where(hc, tot, tot_col)
        wts.append((cb * decay).astype(BF16))
        lane_masks.append(hm)
    xdt = xg * dt_x
    xdt_b = xdt.astype(BF16)
    y = jnp.zeros((q, rp), F32)
    for r in range(R):
        y = jnp.where(lane_masks[r], jnp.dot(wts[r], xdt_b, preferred_element_type=F32), y)
    dte = jnp.exp(tot_row - acum_x)
    cs = lax.dot_general((xdt * dte).astype(BF16), bm.astype(BF16), tn, preferred_element_type=F32)
    y = y + lax.dot_general(cm.astype(BF16), s_in.astype(BF16), nt, preferred_element_type=F32) * jnp.exp(acum_x)
    s_out = jnp.exp(tot_col) * s_in + cs
    return y, s_out


def _ssd_group_state(xg, bm, s_in, *per_head, reverse, P):
    R = len(per_head) // 2
    dtrs, a_s = per_head[:R], per_head[R:]
    q, rp = xg.shape
    ii = lax.broadcasted_iota(jnp.int32, (q, q), 0)
    jj = lax.broadcasted_iota(jnp.int32, (q, q), 1)
    causal = (jj >= ii) if reverse else (jj <= ii)
    eye = ii == jj
    lane = lax.broadcasted_iota(jnp.int32, (1, rp), 1)
    row = lax.broadcasted_iota(jnp.int32, (rp, 1), 0)
    dt_x = jnp.zeros((q, rp), F32)
    acum_x = jnp.zeros((q, rp), F32)
    tot_row = jnp.zeros((1, rp), F32)
    tot_col = jnp.zeros((rp, 1), F32)
    for r in range(R):
        hm = (lane >= r * P) & (lane < (r + 1) * P)
        hc = (row >= r * P) & (row < (r + 1) * P)
        dt_c = jnp.sum(jnp.where(eye, dtrs[r], 0.0), axis=1, keepdims=True)
        acum_c = jnp.sum(jnp.where(causal, dtrs[r] * a_s[r], 0.0), axis=1, keepdims=True)
        tot = jnp.sum(dt_c * a_s[r], axis=0, keepdims=True)
        dt_x = jnp.where(hm, dt_c, dt_x)
        acum_x = jnp.where(hm, acum_c, acum_x)
        tot_row = jnp.where(hm, tot, tot_row)
        tot_col = jnp.where(hc, tot, tot_col)
    xe = xg * dt_x * jnp.exp(tot_row - acum_x)
    cs = lax.dot_general(xe.astype(BF16), bm.astype(BF16), (((0,), (0,)), ((), ())), preferred_element_type=F32)
    return jnp.exp(tot_col) * s_in + cs


def _ssd_maps(NC, ncc, reverse_steps):
    def chunk(d, s):
        if reverse_steps:
            s = NC - 1 - s
        return s if d == 0 else jnp.where(s < ncc, ncc - 1 - s, NC - 1 - s + ncc)

    def lat_chunk(d, s):
        c = chunk(d, s) - ncc
        return jnp.where(c < 0, 0 if d == 0 else NC - ncc - 1, c)

    def step(s):
        return NC - 1 - s if reverse_steps else s

    return chunk, lat_chunk, step


SSD_GROUPS_PER_STEP = 2


def _ssd_specs(chunk, d, GB, R, Q, N, RP, b_off, c_off):
    assert b_off % (GB * N) == 0 and c_off % (GB * N) == 0
    bo, co = b_off // (GB * N), c_off // (GB * N)
    return [
        pl.BlockSpec((Q, GB * RP), lambda g, s: (chunk(d, s), g)),
        pl.BlockSpec((Q, GB * N), lambda g, s: (chunk(d, s), bo + g)),
        pl.BlockSpec((Q, GB * N), lambda g, s: (chunk(d, s), co + g)),
        pl.BlockSpec((GB * R, 1, Q), lambda g, s: (g, 0, chunk(d, s))),
        pl.BlockSpec((GB * R, 1, 1), lambda g, s: (g, 0, 0)),
    ]


def _ssd_fwd(xbc, b_off, c_off, dtr, a, P, ncc):
    T = xbc.shape[0]
    H = dtr[0].shape[0]
    N, Q = SSD_STATE, SSD_CHUNK
    NC = T // Q
    G = (c_off - b_off) // N
    R = H // G
    RP = R * P
    GB = SSD_GROUPS_PER_STEP if G % SSD_GROUPS_PER_STEP == 0 else 1
    chunk, lat_chunk, _ = _ssd_maps(NC, ncc, False)

    def body(*refs):
        s = pl.program_id(1)
        s_ref = refs[-1]

        @pl.when(s == 0)
        def _():
            s_ref[...] = jnp.zeros_like(s_ref)

        for d in range(2):
            x_ref, b_ref, c_ref, dtr_ref, a_ref = refs[5 * d:5 * d + 5]
            y_ref, se_ref = refs[10 + 2 * d:12 + 2 * d]
            for gg in range(GB):
                cols, bcols = slice(gg * RP, (gg + 1) * RP), slice(gg * N, (gg + 1) * N)
                s_in = s_ref[d, gg]
                se_ref[gg] = s_in
                per_head = [dtr_ref[gg * R + r] for r in range(R)] + [a_ref[gg * R + r] for r in range(R)]

                @pl.when(s >= ncc)
                def _(d=d, gg=gg, cols=cols, bcols=bcols, x_ref=x_ref, b_ref=b_ref, c_ref=c_ref, y_ref=y_ref,
                      s_in=s_in, per_head=per_head):
                    y, s_out = _ssd_group(x_ref[:, cols], b_ref[:, bcols], c_ref[:, bcols], s_in, *per_head,
                                          reverse=d == 1, P=P)
                    y_ref[:, cols] = y
                    s_ref[d, gg] = s_out

                @pl.when(s < ncc)
                def _(d=d, gg=gg, cols=cols, bcols=bcols, x_ref=x_ref, b_ref=b_ref, s_in=s_in, per_head=per_head):
                    s_ref[d, gg] = _ssd_group_state(x_ref[:, cols], b_ref[:, bcols], s_in, *per_head,
                                                    reverse=d == 1, P=P)

    in_specs, out_specs, out_shape, operands = [], [], [], []
    for d in range(2):
        in_specs += _ssd_specs(chunk, d, GB, R, Q, N, RP, b_off, c_off)
        operands += [xbc, xbc, xbc, dtr[d], a[d]]
        out_specs += [pl.BlockSpec((Q, GB * RP), functools.partial(lambda g, s, d: (lat_chunk(d, s), g), d=d)),
                      pl.BlockSpec((GB, None, RP, N), lambda g, s: (g, s, 0, 0))]
        out_shape += [jax.ShapeDtypeStruct((T - ncc * Q, H * P), F32), jax.ShapeDtypeStruct((G, NC, RP, N), F32)]
    y_f, se_f, y_b, se_b = _pcall(
        body, name="ssd_fwd", grid=(G // GB, NC), in_specs=in_specs, out_specs=out_specs, out_shape=out_shape,
        scratch_shapes=[pltpu.VMEM((2, GB, RP, N), F32)], compiler_params=_cparams(2),
    )(*operands)
    return (y_f, y_b), (se_f, se_b)


def _ssd_bwd(xbc, b_off, c_off, dtr, a, s_enter, dy, P, ncc):
    T = xbc.shape[0]
    H = dtr[0].shape[0]
    N, Q = SSD_STATE, SSD_CHUNK
    NC = T // Q
    G = (c_off - b_off) // N
    R = H // G
    RP = R * P
    GB = SSD_GROUPS_PER_STEP if G % SSD_GROUPS_PER_STEP == 0 else 1
    chunk, lat_chunk, step = _ssd_maps(NC, ncc, True)
    n_in, n_out = 7, 5

    def body(*refs):
        s = pl.program_id(1)
        ds_ref = refs[-1]

        @pl.when(s == 0)
        def _():
            ds_ref[...] = jnp.zeros_like(ds_ref)

        for d in range(2):
            x_ref, b_ref, c_ref, dtr_ref, a_ref, se_ref, dy_ref = refs[n_in * d:n_in * (d + 1)]
            dx_ref, db_ref, dc_ref, ddtr_ref, da_ref = refs[2 * n_in + n_out * d:2 * n_in + n_out * (d + 1)]
            for gg in range(GB):
                cols, bcols = slice(gg * RP, (gg + 1) * RP), slice(gg * N, (gg + 1) * N)
                per_head = [dtr_ref[gg * R + r] for r in range(R)] + [a_ref[gg * R + r] for r in range(R)]

                def store(grads, dx_ref=dx_ref, db_ref=db_ref, ddtr_ref=ddtr_ref, da_ref=da_ref, d=d, gg=gg,
                          cols=cols, bcols=bcols):
                    dx_ref[:, cols] = grads[0]
                    db_ref[:, bcols] = grads[1]
                    ds_ref[d, gg] = grads[2]
                    for r in range(R):
                        ddtr_ref[gg * R + r] = grads[3 + r]
                        da_ref[gg, r] = jnp.broadcast_to(grads[3 + R + r], (SUBLANE, LANE))

                @pl.when(s < NC - ncc)
                def _(d=d, gg=gg, cols=cols, bcols=bcols, x_ref=x_ref, b_ref=b_ref, c_ref=c_ref, se_ref=se_ref,
                      dy_ref=dy_ref, dc_ref=dc_ref, per_head=per_head, store=store):
                    f = functools.partial(_ssd_group, reverse=d == 1, P=P)
                    _, vjp = jax.vjp(f, x_ref[:, cols], b_ref[:, bcols], c_ref[:, bcols], se_ref[gg], *per_head)
                    grads = vjp((dy_ref[:, cols], ds_ref[d, gg]))
                    dc_ref[:, bcols] = grads[2]
                    store(grads[:2] + grads[3:])

                @pl.when(s >= NC - ncc)
                def _(d=d, gg=gg, cols=cols, bcols=bcols, x_ref=x_ref, b_ref=b_ref, se_ref=se_ref, dc_ref=dc_ref,
                      per_head=per_head, store=store):
                    f = functools.partial(_ssd_group_state, reverse=d == 1, P=P)
                    _, vjp = jax.vjp(f, x_ref[:, cols], b_ref[:, bcols], se_ref[gg], *per_head)
                    dc_ref[:, bcols] = jnp.zeros((Q, N), F32)
                    store(vjp(ds_ref[d, gg]))

    in_specs, out_specs, out_shape, operands = [], [], [], []
    for d in range(2):
        in_specs += _ssd_specs(chunk, d, GB, R, Q, N, RP, b_off, c_off) + [
            pl.BlockSpec((GB, None, RP, N), lambda g, s: (g, step(s), 0, 0)),
            pl.BlockSpec((Q, GB * RP), functools.partial(lambda g, s, d: (lat_chunk(d, s), g), d=d)),
        ]
        operands += [xbc, xbc, xbc, dtr[d], a[d], s_enter[d], dy]
    for d in range(2):
        at_chunk = functools.partial(lambda g, s, d: (chunk(d, s), g), d=d)
        out_specs += [
            pl.BlockSpec((Q, GB * RP), at_chunk), pl.BlockSpec((Q, GB * N), at_chunk),
            pl.BlockSpec((Q, GB * N), at_chunk),
            pl.BlockSpec((GB * R, 1, Q), functools.partial(lambda g, s, d: (g, 0, chunk(d, s)), d=d)),
            pl.BlockSpec((GB, None, R, SUBLANE, LANE), lambda g, s: (g, s, 0, 0, 0)),
        ]
        out_shape += [
            jax.ShapeDtypeStruct((T, H * P), F32), jax.ShapeDtypeStruct((T, G * N), F32),
            jax.ShapeDtypeStruct((T, G * N), F32), jax.ShapeDtypeStruct((H, 1, T), F32),
            jax.ShapeDtypeStruct((G, NC, R, SUBLANE, LANE), F32),
        ]
    res = _pcall(
        body, name="ssd_bwd", grid=(G // GB, NC), in_specs=in_specs, out_specs=out_specs, out_shape=out_shape,
        scratch_shapes=[pltpu.VMEM((2, GB, RP, N), F32)], compiler_params=_cparams(2),
    )(*operands)
    return res[:n_out], res[n_out:]


def _allgather8(name, v):
    R, C = v.shape

    def body(x_ref, out_ref, send_sems, recv_sems, local_sem):
        x, y, c = lax.axis_index("x"), lax.axis_index("y"), lax.axis_index("c")
        me, sibling = (x, y, c), (x, y, 1 - c)
        chips = [(1 - x, y), (x, 1 - y), (1 - x, 1 - y)]

        def slot(px, py, pc):
            return out_ref.at[4 * px + 2 * py + pc]

        def copy(k, block, to, src=None):
            return pltpu.make_async_remote_copy(
                src_ref=slot(*block) if src is None else src, dst_ref=slot(*block),
                send_sem=send_sems.at[k], recv_sem=recv_sems.at[k], device_id=to, device_id_type=MESH)

        mine = pltpu.make_async_copy(x_ref, slot(*me), local_sem)
        mine.start()
        first = [copy(0, me, sibling, src=x_ref)]
        first += [copy(1 + j, me, (*chip, c), src=x_ref) for j, chip in enumerate(chips)]
        for cp in first:
            cp.start()
        passed = [copy(4 + j, (*chip, c), sibling) for j, chip in enumerate(chips)]
        for j, chip in enumerate(chips):
            copy(1 + j, (*chip, c), me).wait_recv()
            passed[j].start()
        copy(0, sibling, me).wait_recv()
        for j, chip in enumerate(chips):
            copy(4 + j, (*chip, 1 - c), me).wait_recv()
        for cp in first + passed:
            cp.wait_send()
        mine.wait()

    return _pcall(
        body, name=name, out_shape=jax.ShapeDtypeStruct((N_DEV, R, C), v.dtype),
        in_specs=[pl.BlockSpec(memory_space=pltpu.VMEM)], out_specs=pl.BlockSpec(memory_space=pltpu.VMEM),
        scratch_shapes=[pltpu.SemaphoreType.DMA((7,)), pltpu.SemaphoreType.DMA((7,)), pltpu.SemaphoreType.DMA],
        compiler_params=pltpu.CompilerParams(vmem_limit_bytes=VMEM_LIMIT_BYTES),
    )(v)


def _slot(ref, k, axis, size):
    if axis is None:
        return ref.at[k]
    align = LANE if size % LANE == 0 else 2 * SUBLANE
    assert size % align == 0
    return ref.at[(slice(None),) * axis + (pl.ds(pl.multiple_of(k * size, align), size),)]


def _exchange4_start(name, srcs, bcast, dep, axes=None, half=False):
    n = len(srcs)
    axes = list(axes) if axes is not None else [None] * n
    sizes = [None if ax is None else s.shape[ax] for s, ax in zip(srcs, axes)]

    def land_shape(s, ax):
        if not bcast:
            return s.shape
        if half:
            return (N_CHIPS,) + s.shape[1:]
        if ax is None:
            return (N_CHIPS,) + s.shape
        return s.shape[:ax] + (N_CHIPS * s.shape[ax],) + s.shape[ax + 1:]

    lands = [lax.empty(land_shape(s, ax), s.dtype) for s, ax in zip(srcs, axes)]

    def body(*refs):
        src, land = refs[:n], refs[n:2 * n]
        send_sems, recv_sems = refs[2 * n + 1], refs[2 * n + 2]
        token = refs[-1]
        x, y, c = lax.axis_index("x"), lax.axis_index("y"), lax.axis_index("c")
        me = 2 * x + y
        for a in range(n):
            for j, (px, py) in enumerate([(1 - x, y), (x, 1 - y), (1 - x, 1 - y)]):
                pltpu.make_async_remote_copy(
                    src_ref=(src[a].at[c] if half else src[a]) if bcast else src[a].at[2 * px + py],
                    dst_ref=_slot(land[a], me, axes[a], sizes[a]),
                    send_sem=send_sems.at[3 * a + j], recv_sem=recv_sems.at[3 * a + j], device_id=(px, py, c),
                    device_id_type=MESH).start()
        token[...] = jnp.zeros_like(token)

    hbm = pl.BlockSpec(memory_space=pltpu.HBM)
    sem = pl.BlockSpec(memory_space=pltpu.SEMAPHORE)
    outs = _pcall(
        body, name=name,
        out_shape=(pltpu.SemaphoreType.DMA((3 * n,)), pltpu.SemaphoreType.DMA((3 * n,)),
                   *[pltpu.HBM(s.shape, s.dtype) for s in srcs], *[pltpu.HBM(l.shape, l.dtype) for l in lands],
                   jax.ShapeDtypeStruct((SUBLANE, LANE), F32)),
        in_specs=[hbm] * (2 * n) + [pl.BlockSpec(memory_space=pl.ANY)],
        out_specs=(sem, sem, *[hbm] * (2 * n), pl.BlockSpec(memory_space=pltpu.VMEM)),
        input_output_aliases={k: 2 + k for k in range(2 * n)},
        compiler_params=pltpu.CompilerParams(has_side_effects=pltpu.SideEffectType.DATAFLOW_SIDE_EFFECTING),
    )(*[pltpu.with_memory_space_constraint(s, pltpu.HBM) for s in srcs],
      *[pltpu.with_memory_space_constraint(l, pltpu.HBM) for l in lands], dep)
    return (n, bcast, half, axes, sizes, outs[0], outs[1], outs[2:2 + n], outs[2 + n:2 + 2 * n]), outs[-1]


def _exchange4_wait(name, handle, after):
    n, bcast, half, axes, sizes, send_sems, recv_sems, src_thru, land_thru = handle

    def body(*refs):
        src, land = refs[:n], refs[n:2 * n]
        send_sems, recv_sems = refs[2 * n], refs[2 * n + 1]
        x, y, c = lax.axis_index("x"), lax.axis_index("y"), lax.axis_index("c")
        for a in range(n):
            for j, (px, py) in enumerate([(1 - x, y), (x, 1 - y), (1 - x, 1 - y)]):
                pk = 2 * px + py
                copy = pltpu.make_async_remote_copy(
                    src_ref=(src[a].at[c] if half else src[a]) if bcast else src[a].at[pk],
                    dst_ref=_slot(land[a], pk, axes[a], sizes[a]),
                    send_sem=send_sems.at[3 * a + j], recv_sem=recv_sems.at[3 * a + j], device_id=(px, py, c),
                    device_id_type=MESH)
                copy.wait_send()
                copy.wait_recv()

    hbm = pl.BlockSpec(memory_space=pltpu.HBM)
    sem = pl.BlockSpec(memory_space=pltpu.SEMAPHORE)
    outs = _pcall(
        body, name=name,
        out_shape=tuple(pltpu.HBM(t.shape, t.dtype) for t in (*src_thru, *land_thru)),
        in_specs=[hbm] * (2 * n) + [sem, sem, pl.BlockSpec(memory_space=pl.ANY)], out_specs=tuple([hbm] * (2 * n)),
        input_output_aliases={k: k for k in range(2 * n)},
        compiler_params=pltpu.CompilerParams(has_side_effects=pltpu.SideEffectType.DATAFLOW_SIDE_EFFECTING),
    )(*src_thru, *land_thru, send_sems, recv_sems, after)
    return list(outs[:n]), list(outs[n:])


def _tie(name, v, token):
    def body(v_ref, token_ref, o_ref):
        del v_ref, token_ref, o_ref

    any_spec = pl.BlockSpec(memory_space=pl.ANY)
    return _pcall(body, name=name, out_shape=jax.ShapeDtypeStruct(v.shape, v.dtype), in_specs=[any_spec, any_spec],
                  out_specs=any_spec, input_output_aliases={0: 0})(v, token)


def _fill_own(landed, own, me, bcast):
    blk = own if bcast else lax.dynamic_index_in_dim(own, me, 0, keepdims=False)
    return lax.dynamic_update_index_in_dim(landed, blk, me, 0)


def _swap_sibling(name, srcs, by_core=False):
    n = len(srcs)

    def body(*refs):
        src, out = refs[:n], refs[n:2 * n]
        send_sems, recv_sems = refs[2 * n:]
        x, y, c = lax.axis_index("x"), lax.axis_index("y"), lax.axis_index("c")
        copies = []
        for a in range(n):
            send = pltpu.make_async_remote_copy(
                src_ref=src[a], dst_ref=out[a].at[c] if by_core else out[a], send_sem=send_sems.at[a],
                recv_sem=recv_sems.at[a], device_id=(x, y, 1 - c), device_id_type=MESH)
            send.start()
            arrive = pltpu.make_async_remote_copy(
                src_ref=src[a], dst_ref=out[a].at[1 - c] if by_core else out[a], send_sem=send_sems.at[a],
                recv_sem=recv_sems.at[a], device_id=(x, y, 1 - c), device_id_type=MESH)
            copies.append((send, arrive))
        for send, arrive in copies:
            send.wait_send()
            arrive.wait_recv()

    any_spec = pl.BlockSpec(memory_space=pl.ANY)
    return _pcall(
        body, name=name,
        out_shape=[jax.ShapeDtypeStruct(((2,) + s.shape) if by_core else s.shape, s.dtype) for s in srcs],
        in_specs=[any_spec] * n, out_specs=[any_spec] * n,
        scratch_shapes=[pltpu.SemaphoreType.DMA((n,)), pltpu.SemaphoreType.DMA((n,))],
    )(*srcs)


def _mod_fwd(c16, mod_w, mod_b_shard):
    nl, D, S = mod_w.shape

    def body(c_ref, w_ref, b_ref, o_ref):
        s = _silu(c_ref[...]).astype(BF16)
        o_ref[...] = jnp.dot(s, w_ref[...].astype(BF16), preferred_element_type=F32) + b_ref[...]

    return _pcall(
        body, name="mod_fwd", grid=(nl,),
        in_specs=[pl.BlockSpec((16, D), lambda l: (0, 0)), pl.BlockSpec((None, D, S), lambda l: (l, 0, 0)),
                  pl.BlockSpec((None, 1, S), lambda l: (l, 0, 0))],
        out_specs=pl.BlockSpec((None, 16, S), lambda l: (l, 0, 0)),
        out_shape=jax.ShapeDtypeStruct((nl, 16, S), F32), compiler_params=_cparams(1),
    )(c16, mod_w, mod_b_shard)


def _mod_w_update(s16t, dm16, w, m, v):
    nl, D, S = w.shape
    tm = _row_tile(D, 256)

    def body(s_ref, dm_ref, w_ref, m_ref, v_ref, g_ref, dl_ref, nm_ref, nv_ref):
        g = jnp.dot(s_ref[...], dm_ref[...], preferred_element_type=F32, precision=HIGHEST)
        g, dl, nm, nv = _f_adamw(w_ref[...], m_ref[...], v_ref[...], g, jnp.zeros_like(g))
        g_ref[...] = g
        dl_ref[...] = dl
        nm_ref[...] = nm
        nv_ref[...] = nv

    big = pl.BlockSpec((None, tm, S), lambda l, i: (l, i, 0))
    return _pcall(
        body, name="mod_w_update", grid=(nl, D // tm),
        in_specs=[pl.BlockSpec((tm, 16), lambda l, i: (i, 0)), pl.BlockSpec((None, 16, S), lambda l, i: (l, 0, 0)),
                  big, big, big],
        out_specs=[big] * 4, out_shape=[jax.ShapeDtypeStruct(w.shape, F32)] * 4, compiler_params=_cparams(2),
    )(s16t, dm16, w, m, v)


def _size(shape):
    n = 1
    for d in shape:
        n *= d
    return n


def _pack(arrs):
    pieces = []
    for a in arrs:
        flat = a.reshape(-1).astype(F32)
        pieces.append(jnp.pad(flat, (0, _round_up(flat.shape[0], LANE) - flat.shape[0])).reshape(-1, LANE))
    buf = jnp.concatenate(pieces, axis=0)
    return jnp.pad(buf, ((0, _round_up(buf.shape[0], SUBLANE) - buf.shape[0]), (0, 0)))


def _unpack(buf, shapes):
    lead = buf.shape[:-2]
    out, row = [], 0
    for s in shapes:
        n = _size(s)
        rows = _cdiv(n, LANE)
        piece = buf[..., row:row + rows, :].reshape(lead + (rows * LANE,))
        out.append(piece[..., :n].reshape(lead + tuple(s)))
        row += rows
    return out


def _adamw_many(name, ws, ms, vs, gs):
    n = len(ws)

    def body(*refs):
        for k in range(n):
            res = _f_adamw(refs[k][...], refs[n + k][...], refs[2 * n + k][...], refs[3 * n + k][...], 0.0)
            for j in range(4):
                refs[(4 + j) * n + k][...] = res[j]

    vmem = pl.BlockSpec(memory_space=pltpu.VMEM)
    res = _pcall(body, name=name, out_shape=[jax.ShapeDtypeStruct(w.shape, F32) for _ in range(4) for w in ws],
                 in_specs=[vmem] * (4 * n), out_specs=[vmem] * (4 * n))(*ws, *ms, *vs, *gs)
    return [tuple(res[j * n + k] for j in range(4)) for k in range(n)]


SHARD_AXIS = {
    "mod_w": 2, "ssd_w_in": 2, "ssd_conv_w": 2, "ssd_w_out": 1, "conf_w_pw1": 2, "conf_b_pw1": 1, "conf_w_dw": 2,
    "conf_b_dw": 1, "conf_ln_w": 1, "conf_ln_b": 1, "conf_w_pw2": 1, "conf_b_pw2": 1, "ffn_w_up": 2,
    "ffn_conv_w": 3, "ffn_w_down": 1,
}
BIG = ("ssd_w_in", "ssd_w_out", "conf_w_pw1", "conf_w_pw2", "ffn_w_up", "ffn_w_down")
WEIGHTS = ("c_ctx", "mod_w", "mod_b", "norm1_w", "norm2_w", "ssd_w_in", "ssd_conv_w", "ssd_conv_b", "ssd_dt_bias",
           "ssd_a_log", "ssd_d", "ssd_norm_w", "ssd_w_out", "conf_w_pw1", "conf_b_pw1", "conf_w_dw", "conf_b_dw",
           "conf_ln_w", "conf_ln_b", "conf_w_pw2", "conf_b_pw2", "ffn_w_up", "ffn_conv_w", "ffn_conv_b",
           "ffn_w_down", "final_norm_w")
SMALL = tuple(n for n in WEIGHTS if n not in BIG and n != "mod_w")
SMALL_SHARDED = tuple(n for n in SMALL if n in SHARD_AXIS)


def _unshard(stacked, axis):
    return jnp.concatenate([stacked[k] for k in range(N_CHIPS)], axis=axis)


def _to_blocks(full, axis):
    return jnp.stack(jnp.split(full, N_CHIPS, axis=axis))


def _par(v):
    v = v.reshape(-1, v.shape[-1])
    return v[:, None, :]


def kernel(x, c, ctx, c_ctx, mod_w, mod_b, norm1_w, norm2_w, ssd_w_in, ssd_conv_w, ssd_conv_b, ssd_dt_bias, ssd_a_log, ssd_d, ssd_norm_w, ssd_w_out, conf_w_pw1, conf_b_pw1, conf_w_dw, conf_b_dw, conf_ln_w, conf_ln_b, conf_w_pw2, conf_b_pw2, ffn_w_up, ffn_conv_w, ffn_conv_b, ffn_w_down, final_norm_w, loss_target, m_c_ctx, m_mod_w, m_mod_b, m_norm1_w, m_norm2_w, m_ssd_w_in, m_ssd_conv_w, m_ssd_conv_b, m_ssd_dt_bias, m_ssd_a_log, m_ssd_d, m_ssd_norm_w, m_ssd_w_out, m_conf_w_pw1, m_conf_b_pw1, m_conf_w_dw, m_conf_b_dw, m_conf_ln_w, m_conf_ln_b, m_conf_w_pw2, m_conf_b_pw2, m_ffn_w_up, m_ffn_conv_w, m_ffn_conv_b, m_ffn_w_down, m_final_norm_w, v_c_ctx, v_mod_w, v_mod_b, v_norm1_w, v_norm2_w, v_ssd_w_in, v_ssd_conv_w, v_ssd_conv_b, v_ssd_dt_bias, v_ssd_a_log, v_ssd_d, v_ssd_norm_w, v_ssd_w_out, v_conf_w_pw1, v_conf_b_pw1, v_conf_w_dw, v_conf_b_dw, v_conf_ln_w, v_conf_ln_b, v_conf_w_pw2, v_conf_b_pw2, v_ffn_w_up, v_ffn_conv_w, v_ffn_conv_b, v_ffn_w_down, v_final_norm_w):
    given = dict(locals())
    W = {n: given[n] for n in WEIGHTS}
    Mo = {n: given["m_" + n] for n in WEIGHTS}
    Vo = {n: given["v_" + n] for n in WEIGHTS}

    ax, ay, ac = lax.axis_index("x"), lax.axis_index("y"), lax.axis_index("c")
    chip = 2 * ax + ay
    dev = 4 * ax + 2 * ay + ac

    D = x.shape[-1]
    L, Lc = x.shape[1], ctx.shape[1]
    T0 = L + Lc
    H = ssd_a_log.shape[-1]
    DI = ssd_norm_w.shape[-1]
    P = DI // H
    CD = ssd_conv_b.shape[-1]
    N = SSD_STATE
    G = (CD - DI) // (2 * N)
    FH = ffn_conv_b.shape[-1]
    KS = ssd_conv_w.shape[1]
    KC = conf_w_dw.shape[1]
    ncc = Lc // SSD_CHUNK

    shard_b = {n: W[n].astype(BF16) for n in BIG}

    small_shard_shapes = [W[n].shape for n in SMALL_SHARDED]
    f1 = _allgather8("gather_small", _pack([c] + [W[n] for n in SMALL_SHARDED]))
    parts = _unpack(f1, [c.shape] + small_shard_shapes)
    Wf = dict(W)
    for n, p in zip(SMALL_SHARDED, parts[1:]):
        Wf[n] = _unshard(p[::2], SHARD_AXIS[n])
    c16 = jnp.concatenate([parts[0].reshape(N_DEV, D), c_ctx[None, :], jnp.zeros((16 - N_DEV - 1, D), F32)], axis=0)

    S_mod = mod_w.shape[-1]
    mod_b_shard = lax.dynamic_slice_in_dim(mod_b, chip * S_mod, S_mod, axis=1)[:, None, :]
    mod_part = _mod_fwd(c16, mod_w, mod_b_shard)
    f2 = _allgather8("gather_mod", mod_part.reshape(2 * 16, S_mod))
    mods = jnp.concatenate([f2[2 * k].reshape(2, 16, S_mod) for k in range(N_CHIPS)], axis=-1)
    my = lax.dynamic_slice_in_dim(mods, dev, 1, axis=1)[:, 0]
    sh1, sc1, g1, sh2, sc2, g2 = [[my[l, k * D:(k + 1) * D] for l in range(2)] for k in range(6)]
    csh1, csc1 = mods[0, N_DEV, 0:D], mods[0, N_DEV, D:2 * D]

    in_halves = shard_b["ssd_w_in"].reshape(2, D // 2, ssd_w_in.shape[-1])
    gather_a, token = _exchange4_start("gather_w_in_start", [in_halves], True, mods, half=True)
    csc1 = _tie("tie_gather_w_in", csc1, token)

    def full_weight(n, own, landed):
        if landed.ndim == own.ndim:
            ax = SHARD_AXIS[n]
            return lax.dynamic_update_slice_in_dim(landed, own, chip * own.shape[ax], ax)
        return _fill_own(landed, own, chip, True)

    xl = x[0]
    rows0 = (ctx[0], xl)
    n1w0, n1w1 = _par(norm1_w[0]), _par(norm1_w[1])
    sc_seg = jnp.stack([csc1, sc1[0]])[:, None, :]
    sh_seg = jnp.stack([csh1, sh1[0]])[:, None, :]

    a0 = _rw_fwd("l0_modnorm1", _f_modnorm, [], [n1w0, sc_seg, sh_seg], [D], T=T0, seg_rows=(Lc,), head=rows0,
                 out_dtypes=[BF16])
    rest = [n for n in BIG if n != "ssd_w_in"]
    for n in rest:
        a0 = _tie("tie_cast_" + n, a0, shard_b[n])
    (own_in,), (landed_in,) = _exchange4_wait("gather_w_in_wait", gather_a, a0)
    mine = _fill_own(landed_in, lax.dynamic_index_in_dim(own_in, ac, 0, keepdims=False), chip, True)
    (halves,) = _swap_sibling("swap_w_in", [mine], by_core=True)
    halves = lax.dynamic_update_index_in_dim(halves, mine, ac, 0)
    w_in = jnp.concatenate([halves[:, k].reshape(D, -1) for k in range(N_CHIPS)], axis=1)
    landed_in = halves
    def start_gather(tag, names, dep):
        handle, tok = _exchange4_start("gather_" + tag + "_start", [shard_b[n] for n in names], True, dep,
                                       axes=[1 if SHARD_AXIS[n] == 1 else None for n in names])
        return (names, handle), tok

    def finish_gather(tag, group, after):
        names, handle = group
        return {n: full_weight(n, own, g)
                for n, own, g in zip(names, *_exchange4_wait("gather_" + tag + "_wait", handle, after))}

    gather_b, token = start_gather("mix", ["ssd_w_out", "conf_w_pw1", "conf_w_pw2"], landed_in)
    gather_c, token = start_gather("ffn_up", ["ffn_w_up"], token)
    gather_d, token = start_gather("ffn_down", ["ffn_w_down"], token)
    a0 = _tie("tie_gather_rest", a0, token)
    proj = _mm(a0, w_in, name="l0_w_in")
    seg_taps = [(k - KS // 2, ("seg", Lc)) for k in range(KS)]
    xbc_pre, xbc = _conv_fwd("l0_conv", proj, DI, CD, Wf["ssd_conv_w"][0], ssd_conv_b, seg_taps, act=True)
    dt_raw = proj[:, DI + CD:]
    dt_bias = _par(ssd_dt_bias.reshape(1, 2 * H))
    dt = _rw_fwd("l0_softplus", _f_softplus, [dt_raw], [dt_bias], [2 * H])
    dt_t = dt.T
    dtr = (dt_t[:H, None, :], dt_t[H:, None, :])
    a_all = -jnp.exp(ssd_a_log.reshape(2, H, 1, 1))
    a_neg = (a_all[0], a_all[1])
    (y_f, y_b), s_enter = _ssd_fwd(xbc, DI, DI + G * N, dtr, a_neg, P, ncc)
    gate_rows = [y_f, y_b, (xbc, 0, DI, Lc), (proj, 0, DI, Lc)]
    d_rep = _par(jnp.repeat(ssd_d[0], P))
    ssd_nw = _par(ssd_norm_w[0])
    yn = _rw_fwd("l0_ssd_gate", _f_ssd_gate, gate_rows, [d_rep, ssd_nw], [DI], T=L, out_dtypes=[BF16])
    Wb = finish_gather("mix", gather_b, yn)
    w_out, w_pw1, w_pw2 = Wb["ssd_w_out"][0], Wb["conf_w_pw1"], Wb["conf_w_pw2"][0]
    mix0 = _mm(yn, w_out, name="l0_w_out")
    g1_0, g2_0, g1_1, g2_1 = _par(g1[0]), _par(g2[0]), _par(g1[1]), _par(g2[1])
    h1 = _rw_fwd("l0_res1", _f_gate_res, [xl, mix0], [g1_0], [D])
    w_up = finish_gather("ffn_up", gather_c, h1)["ffn_w_up"]
    w_dn_landed = []

    grid_taps = [((i - 1) * GRID_W + (j - 1), (None if j == 1 else ("col", j - 1))) for i in range(3) for j in range(3)]

    def ffn_fwd(l, h, tag):
        a = _rw_fwd(tag + "_modnorm2", _f_modnorm, [h], [_par(norm2_w[l]), _par(sc2[l]), _par(sh2[l])], [D],
                    out_dtypes=[BF16])
        hh = _mm(a, w_up, b_lead=l, b_shards=N_CHIPS, name=tag + "_w_up")
        gc = _conv_fwd(tag + "_ffn_conv", hh, FH, FH, Wf["ffn_conv_w"][l].reshape(9, FH), ffn_conv_b[l][None, :],
                       grid_taps)
        act = _rw_fwd(tag + "_act", _f_ffn_act, [(hh, 0, FH), gc], [], [FH], col_tile=_tile(FH, 1536),
                      out_dtypes=[BF16])
        if not w_dn_landed:
            w_dn_landed.append(finish_gather("ffn_down", gather_d, act)["ffn_w_down"])
        dn = _mm(act, w_dn_landed[0], b_lead=l, name=tag + "_w_down")
        return a, hh, gc, act, dn

    a1, hh0, gc0, act0, dn0 = ffn_fwd(0, h1, "l0")
    h2 = _rw_fwd("l0_res2", _f_gate_res, [h1, dn0], [g2_0], [D])

    a2 = _rw_fwd("l1_modnorm1", _f_modnorm, [h2], [n1w1, _par(sc1[1]), _par(sh1[1])], [D], out_dtypes=[BF16])
    pw = _mm(a2, w_pw1, b_lead=0, b_shards=N_CHIPS, name="l1_pw1")
    b_pw1 = Wf["conf_b_pw1"][0]
    glu = _rw_fwd("l1_glu", _f_glu, [(pw, 0, D), (pw, D, D)], [_par(b_pw1[:D]), _par(b_pw1[D:])], [D])
    conf_taps = [(k - KC // 2, None) for k in range(KC)]
    cv = _conv_fwd("l1_conv", glu, 0, D, Wf["conf_w_dw"][0], Wf["conf_b_dw"], conf_taps)
    ln_w, ln_b = _par(Wf["conf_ln_w"][0]), _par(Wf["conf_ln_b"][0])
    ls = _rw_fwd("l1_ln_silu", _f_ln_silu, [cv], [ln_w, ln_b], [D], out_dtypes=[BF16])
    p2 = _mm(ls, w_pw2, name="l1_pw2")
    b_pw2 = _par(Wf["conf_b_pw2"][0])
    h3 = _rw_fwd("l1_res1", _f_gate_res_bias, [h2, p2], [g1_1, b_pw2], [D])
    a3, hh1, gc1, act1, dn1 = ffn_fwd(1, h3, "l1")
    h4 = _rw_fwd("l1_res2", _f_gate_res, [h3, dn1], [g2_1], [D])

    fnw = final_norm_w[None, :]
    tgt = loss_target[0]
    loss_local = _loss_fwd(h4, tgt, fnw)[0, 0]

    G_full = {}
    reduces = {}

    def start_reduce(tag, items, dep):
        def blocks_of(g, ax):
            if g.ndim == 3:
                return g
            return g.reshape(N_CHIPS, g.shape[0] // N_CHIPS, g.shape[1]) if ax == 0 else _to_blocks(g, ax)

        blocks = [blocks_of(g, ax).astype(BF16) for _, g, ax in items]
        handle, tok = _exchange4_start("reduce_" + tag + "_start", blocks, False, dep)
        reduces[tag] = ([n for n, _, _ in items], handle)
        return tok
    ones = jnp.ones((L, 1), F32)
    (dh4,), (dfnw,) = _rw_bwd("loss_bwd", _f_loss_rows, [h4, tgt], [_par(final_norm_w)], [ones],
                              row_grad=[True, False], par_grad=[True])
    G_full["final_norm_w"] = dfnw.reshape(D)

    def ffn_bwd(l, h, saved, g2_l, dh_out, tag):
        a, hh, gc, act, dn = saved
        (ddn,), (dg2,) = _rw_bwd(tag + "_res2_bwd", _f_gate, [dn], [g2_l], [dh_out],
                                 row_grad=[True], par_grad=[True], row_dtypes=[BF16])
        dact = _mm(ddn, w_dn_landed[0], b_lead=l, tb=True, name=tag + "_w_down_dx")
        dwdn = _mm(act, ddn, ta=True, name=tag + "_w_down_dw", out_dtype=BF16)
        (dval, dgc), _ = _rw_bwd(tag + "_act_bwd", _f_ffn_act, [(hh, 0, FH), gc], [], [dact],
                                 row_grad=[True, True], par_grad=[], col_tile=_tile(FH, 1536), row_dtypes=[BF16, F32])
        dgin, dcw, dcb = _conv_bwd(tag + "_ffn_conv_bwd", hh, FH, FH, Wf["ffn_conv_w"][l].reshape(9, FH), dgc,
                                   grid_taps, du_dtype=BF16)
        dhh = jnp.concatenate([dval, dgin], axis=1)
        da = _mm(dhh, w_up, b_lead=l, b_shards=N_CHIPS, tb=True, name=tag + "_w_up_dx")
        dwup = _mm(a, dhh, ta=True, name=tag + "_w_up_dw", out_dtype=BF16, col_blocks=N_CHIPS)
        (dh,), (dn2w, dsc2, dsh2) = _rw_bwd(
            tag + "_modnorm2_bwd", _f_modnorm, [h], [_par(norm2_w[l]), _par(sc2[l]), _par(sh2[l])], [da],
            row_grad=[True], par_grad=[True, True, True], add=dh_out)
        return dh, dict(w_down=dwdn, w_up=dwup, conv_w=dcw.reshape(3, 3, FH), conv_b=dcb.reshape(FH),
                        n2w=dn2w.reshape(D), sc2=dsc2.reshape(D), sh2=dsh2.reshape(D), g2=dg2.reshape(D))

    dh3, gf1 = ffn_bwd(1, h3, (a3, hh1, gc1, act1, dn1), g2_1, dh4, "l1")
    (dp2,), (dg1_1, db_pw2) = _rw_bwd("l1_res1_bwd", _f_gate_bias, [p2], [g1_1, b_pw2], [dh3],
                                      row_grad=[True], par_grad=[True, True], row_dtypes=[BF16])
    dls = _mm(dp2, w_pw2, tb=True, name="l1_pw2_dx")
    dw_pw2 = _mm(ls, dp2, ta=True, name="l1_pw2_dw", out_dtype=BF16)
    (dcv,), (dln_w, dln_b) = _rw_bwd("l1_ln_silu_bwd", _f_ln_silu, [cv], [ln_w, ln_b], [dls],
                                     row_grad=[True], par_grad=[True, True])
    dglu, dw_dw, db_dw = _conv_bwd("l1_conv_bwd", glu, 0, D, Wf["conf_w_dw"][0], dcv, conf_taps)
    (dpa, dpg), (dba, dbg) = _rw_bwd("l1_glu_bwd", _f_glu, [(pw, 0, D), (pw, D, D)],
                                     [_par(b_pw1[:D]), _par(b_pw1[D:])], [dglu],
                                     row_grad=[True, True], par_grad=[True, True], row_dtypes=[BF16, BF16])
    dpw = jnp.concatenate([dpa, dpg], axis=1)
    da2 = _mm(dpw, w_pw1, b_lead=0, b_shards=N_CHIPS, tb=True, name="l1_pw1_dx")
    dw_pw1 = _mm(a2, dpw, ta=True, name="l1_pw1_dw", out_dtype=BF16, col_blocks=N_CHIPS)
    (dh2,), (dn1w1, dsc1_1, dsh1_1) = _rw_bwd(
        "l1_modnorm1_bwd", _f_modnorm, [h2], [n1w1, _par(sc1[1]), _par(sh1[1])], [da2],
        row_grad=[True], par_grad=[True, True, True], add=dh3)
    G_full["conf_b_pw2"] = db_pw2.reshape(1, D)
    G_full["conf_ln_w"], G_full["conf_ln_b"] = dln_w.reshape(1, D), dln_b.reshape(1, D)
    G_full["conf_w_dw"], G_full["conf_b_dw"] = dw_dw[None], db_dw.reshape(1, D)
    G_full["conf_b_pw1"] = jnp.concatenate([dba.reshape(1, D), dbg.reshape(1, D)], axis=1)

    token = start_reduce("l1", [("conf_w_pw2", dw_pw2, 0), ("conf_w_pw1", dw_pw1, 1), ("ffn_w_up1", gf1["w_up"], 1),
                                ("ffn_w_down1", gf1["w_down"], 0)], dw_pw2)
    dh2 = _tie("tie_reduce_l1", dh2, token)
    dh1, gf0 = ffn_bwd(0, h1, (a1, hh0, gc0, act0, dn0), g2_0, dh2, "l0")
    G_full["ffn_conv_w"] = jnp.stack([gf0["conv_w"], gf1["conv_w"]])
    G_full["ffn_conv_b"] = jnp.stack([gf0["conv_b"], gf1["conv_b"]])

    (dmix,), (dg1_0,) = _rw_bwd("l0_res1_bwd", _f_gate, [mix0], [g1_0], [dh1],
                                row_grad=[True], par_grad=[True], row_dtypes=[BF16])
    dyn = _mm(dmix, w_out, tb=True, name="l0_w_out_dx")
    dw_out = _mm(yn, dmix, ta=True, name="l0_w_out_dw", out_dtype=BF16)
    token = start_reduce("l0", [("ffn_w_up0", gf0["w_up"], 1), ("ffn_w_down0", gf0["w_down"], 0),
                                ("ssd_w_out", dw_out, 0)], dw_out)
    dyn = _tie("tie_reduce_l0", dyn, token)
    (dy_lat, dxs_gate, dz_lat), (dd_rep, dssd_nw) = _rw_bwd(
        "l0_ssd_gate_bwd", _f_ssd_gate, gate_rows, [d_rep, ssd_nw], [dyn],
        row_grad=[True, False, True, True], par_grad=[True, True], T=L, row_dtypes=[F32, F32, BF16])
    g_f, g_b = _ssd_bwd(xbc, DI, DI + G * N, dtr, a_neg, s_enter, dy_lat, P, ncc)
    silu_bwd = functools.partial(_rw_bwd, f=_silu, pars=[], row_grad=[True], par_grad=[], T=T0)
    (dxs_pre,), _ = silu_bwd("l0_silu_bwd_x", rows=[(xbc_pre, 0, DI)], cot_fn=lambda p, q, r: p + q + r,
                             cots=[g_f[0], g_b[0], (dxs_gate, 0, DI, -Lc)],
                             col_tile=_tile(DI, 1024))
    (db_pre,), _ = silu_bwd("l0_silu_bwd_b", rows=[(xbc_pre, DI, G * N)], cot_fn=lambda p, q: p + q,
                            cots=[g_f[1], g_b[1]], col_tile=_tile(G * N, 1024))
    (dc_pre,), _ = silu_bwd("l0_silu_bwd_c", rows=[(xbc_pre, DI + G * N, G * N)], cot_fn=lambda p, q: p + q,
                            cots=[g_f[2], g_b[2]], col_tile=_tile(G * N, 1024))
    conv_w0 = Wf["ssd_conv_w"][0]
    pieces = []
    for tag, off, width, g_pre in (("x", 0, DI, dxs_pre), ("b", DI, G * N, db_pre), ("c", DI + G * N, G * N, dc_pre)):
        pieces.append(_conv_bwd("l0_conv_bwd_" + tag, proj, DI + off, width, conv_w0[:, off:off + width], g_pre,
                                seg_taps, du_dtype=BF16))
    dconv_in = [p[0] for p in pieces]
    dcw0 = jnp.concatenate([p[1] for p in pieces], axis=1)
    dcb0 = jnp.concatenate([p[2] for p in pieces], axis=1)
    ddt = jnp.concatenate([g_f[3][:, 0, :].T, g_b[3][:, 0, :].T], axis=1)
    (ddt_raw,), (ddt_bias,) = _rw_bwd("l0_softplus_bwd", _f_softplus, [dt_raw], [dt_bias], [ddt],
                                      row_grad=[True], par_grad=[True], row_dtypes=[BF16])
    dproj = jnp.concatenate([jnp.pad(dz_lat, ((Lc, 0), (0, 0))), *dconv_in, ddt_raw], axis=1)
    da0 = _mm(dproj, w_in, tb=True, name="l0_w_in_dx")
    dw_in = _mm(a0, dproj, ta=True, name="l0_w_in_dw", out_dtype=BF16)
    token = start_reduce("in", [("ssd_w_in", dw_in, 1)], dw_in)
    da0 = _tie("tie_reduce_in", da0, token)
    (dhcat,), (dn1w0, dsc_seg, dsh_seg) = _rw_bwd(
        "l0_modnorm1_bwd", _f_modnorm, [], [n1w0, sc_seg, sh_seg], [da0], T=T0, head=rows0,
        row_grad=[True], par_grad=[True, True, True], seg_rows=(Lc,), add=(dh1, 0, D, -Lc), skip_rows=Lc)
    grad_x = dhcat[None]

    da_heads = jnp.stack([g[4][..., 0, 0].sum(axis=1).reshape(H) for g in (g_f, g_b)])[None]
    G_full["ssd_a_log"] = da_heads * (-jnp.exp(ssd_a_log))
    G_full["ssd_dt_bias"] = ddt_bias.reshape(1, 2, H)
    G_full["ssd_d"] = dd_rep.reshape(H, P).sum(axis=1)[None]
    G_full["ssd_norm_w"] = dssd_nw.reshape(1, DI)
    G_full["ssd_conv_w"], G_full["ssd_conv_b"] = dcw0[None], dcb0.reshape(1, CD)
    G_full["norm1_w"] = jnp.stack([dn1w0.reshape(D), dn1w1.reshape(D)])
    G_full["norm2_w"] = jnp.stack([gf0["n2w"], gf1["n2w"]])

    zD = jnp.zeros((D,), F32)
    dm_own = jnp.stack([
        jnp.concatenate([dsh_seg[1, 0], dsc_seg[1, 0], dg1_0.reshape(D), gf0["sh2"], gf0["sc2"], gf0["g2"]]),
        jnp.concatenate([dsh1_1.reshape(D), dsc1_1.reshape(D), dg1_1.reshape(D), gf1["sh2"], gf1["sc2"], gf1["g2"]]),
    ])
    dmc_own = jnp.concatenate([dsh_seg[0, 0], dsc_seg[0, 0], zD, zD, zD, zD])

    out = {}

    def finish_reduce(tags, after, swap_name):
        partial = {}
        for tag in tags:
            names, handle = reduces[tag]
            blocks, landed = _exchange4_wait("reduce_" + tag + "_wait", handle, after)
            for n, blk, own in zip(names, landed, blocks):
                r = _fill_own(blk, own, chip, False)
                partial[n] = _sum_leading("sum4_" + n, r.reshape(N_CHIPS, -1, r.shape[-1]),
                                          (0, 1, 2, 3), out_dtype=BF16).reshape(r.shape[1:])
        for n in ("ffn_w_up", "ffn_w_down"):
            if n + "0" in partial:
                partial[n] = jnp.stack([partial.pop(n + "0"), partial.pop(n + "1")])
        names = [n for n in BIG if n in partial]
        mine = [partial[n].reshape(W[n].shape) for n in names]
        for n, own, sib in zip(names, mine, _swap_sibling(swap_name, mine)):
            out[n] = _adamw("adamw_" + n, W[n], Mo[n], Vo[n], own, sib)
        return names

    early = finish_reduce(["l1", "l0"], dhcat, "swap_grads_early")

    small_sum_names = [n for n in SMALL if n not in ("c_ctx", "mod_b")]
    sum_part = [G_full[n] for n in small_sum_names] + [dmc_own, loss_local.reshape(1)]
    packed = _tie("tie_small_grads", _pack(sum_part + [dm_own]), out[early[-1]][1])
    gat = _allgather8("gather_small_grads", packed)
    total = _sum_leading("sum_small_grads", gat, tuple(range(N_DEV)))
    summed = _unpack(total, [a.shape for a in sum_part])
    Gs = dict(zip(small_sum_names, summed[:-2]))
    dmc_tot, loss = summed[-2], summed[-1][0]
    dm_all = _unpack(gat, [a.shape for a in sum_part] + [dm_own.shape])[-1].transpose(1, 0, 2)
    dm16 = jnp.concatenate([dm_all, jnp.stack([dmc_tot, jnp.zeros_like(dmc_tot)])[:, None, :],
                            jnp.zeros((2, 16 - N_DEV - 1, 6 * D), F32)], axis=1)
    Gs["mod_b"] = _sum_leading("sum_mod_b", dm16.transpose(1, 0, 2).reshape(16, 2 * 6 * D // LANE, LANE),
                               tuple(range(N_DEV + 1))).reshape(2, 6 * D)

    dm16_shard = lax.dynamic_slice_in_dim(dm16, chip * S_mod, S_mod, axis=2)
    ds16 = _mm(dm16_shard[0], mod_w.reshape(2 * D, S_mod), tb=True, precision=HIGHEST, name="c_ctx_dx")
    sig = jax.nn.sigmoid(c_ctx)
    dcc_part = ds16[N_DEV, :D] * (sig * (1.0 + c_ctx * (1.0 - sig)))
    gat_cc = _allgather8("gather_c_ctx_grad", _pack([dcc_part]))
    Gs["c_ctx"] = _sum_leading("sum_c_ctx_grad", gat_cc, (0, 2, 4, 6)).reshape(-1)[:D]

    s16t = _silu(c16).T
    out["mod_w"] = _mod_w_update(s16t, dm16_shard, mod_w, m_mod_w, v_mod_w)
    finish_reduce(["in"], out["mod_w"][0], "swap_grads_late")

    def own(n, full):
        if n in SHARD_AXIS:
            size = W[n].shape[SHARD_AXIS[n]]
            return lax.dynamic_slice_in_dim(full, chip * size, size, axis=SHARD_AXIS[n])
        return full

    def two_d(a):
        return a.reshape(1, -1) if a.ndim == 1 else a

    g_small = [own(n, Gs[n].reshape(Wf[n].shape)) for n in SMALL]
    res = _adamw_many("adamw_small", [two_d(W[n]) for n in SMALL], [two_d(Mo[n]) for n in SMALL],
                      [two_d(Vo[n]) for n in SMALL], [two_d(g) for g in g_small])
    for n, r in zip(SMALL, res):
        out[n] = tuple(t.reshape(W[n].shape) for t in r)

    grads = [out[n][0] for n in WEIGHTS]
    deltas = [out[n][1] for n in WEIGHTS]
    new_m = [out[n][2] for n in WEIGHTS]
    new_v = [out[n][3] for n in WEIGHTS]
    return (loss, grad_x, *grads, *deltas, *new_m, *new_v)
```

```python
import functools

import jax
import jax.numpy as jnp
from jax import lax
from jax.experimental import pallas as pl
from jax.experimental.pallas import tpu as pltpu

F32 = jnp.float32
BF16 = jnp.bfloat16
MESH = pl.DeviceIdType.MESH
HIGHEST = lax.Precision.HIGHEST

VMEM_LIMIT_BYTES = 48 * 1024 * 1024
LANE = 128
SUBLANE = 8

SSD_STATE = 128
SSD_CHUNK = 128
GRID_W = 64
EPS = 1e-6
N_CHIPS = 4
N_DEV = 8

ADAM_LR = 0.001
ADAM_B1 = 0.9
ADAM_B2 = 0.999
ADAM_EPS = 1e-08
ADAM_WD = 0.01
ADAM_STEP = 10


def _pcall(body, **kw):
    return pl.pallas_call(body, **kw)


def _cparams(n_grid):
    return pltpu.CompilerParams(dimension_semantics=("arbitrary",) * n_grid, vmem_limit_bytes=VMEM_LIMIT_BYTES)


def _cdiv(a, b):
    return -(-a // b)


def _round_up(a, b):
    return _cdiv(a, b) * b


def _tile(n, cap):
    if n <= cap:
        return n
    best = None
    for t in range(LANE, cap + 1, LANE):
        if n % t == 0:
            best = t
    if best is None:
        npad = _round_up(n, LANE)
        for t in range(LANE, cap + 1, LANE):
            if npad % t == 0:
                best = t
    return best


def _row_tile(n, cap, also=()):
    best = None
    for step in (2 * SUBLANE, SUBLANE):
        for t in range(step, min(cap, n) + 1, step):
            if n % t == 0 and all(a % t == 0 for a in also):
                best = t
        if best is not None:
            break
    assert best is not None, (n, cap, also)
    return best


def _silu(v):
    return v * jax.nn.sigmoid(v)


def _mm(a, b, *, name, ta=False, tb=False, precision=None, cap=1024, out_dtype=F32, col_blocks=None,
        b_lead=None, b_shards=None):
    M, K = (a.shape[1], a.shape[0]) if ta else a.shape
    b_dims = b.shape[(b_lead is not None) + (b_shards is not None):]
    b_cols = b_dims[1] * (b_shards or 1)
    N, Kb = (b_dims[0], b_cols) if tb else (b_cols, b_dims[0])
    assert K == Kb, (a.shape, b.shape, ta, tb)
    n_cut, k_cut = (1, b_shards or 1) if tb else (b_shards or 1, 1)
    tm, tk = _tile(M, cap), _tile(K // k_cut, cap + cap // 2)
    tn = _tile(N // (col_blocks or n_cut), cap + cap // 2)
    assert b_shards is None or (b_dims[1] % (tk if tb else tn) == 0 and col_blocks is None), (b.shape, tn, tk)
    nm, nn, nk = _cdiv(M, tm), _cdiv(N, tn), _cdiv(K, tk)
    k_tail = K % tk
    exact = precision is not None

    def body(a_ref, b_ref, o_ref, acc_ref):
        k = pl.program_id(2)

        @pl.when(k == 0)
        def _():
            acc_ref[...] = jnp.zeros_like(acc_ref)

        av = a_ref[...]
        bv = b_ref[...]
        if k_tail:
            lim = K - k * tk
            ka = lax.broadcasted_iota(jnp.int32, av.shape, 0 if ta else 1)
            kb = lax.broadcasted_iota(jnp.int32, bv.shape, 1 if tb else 0)
            av = jnp.where(ka < lim, av, jnp.zeros_like(av))
            bv = jnp.where(kb < lim, bv, jnp.zeros_like(bv))
        if exact:
            av = av.astype(F32)
            bv = bv.astype(F32)
        else:
            av = av.astype(BF16)
            bv = bv.astype(BF16)
        dn = (((0 if ta else 1,), (1 if tb else 0,)), ((), ()))
        acc_ref[...] += lax.dot_general(av, bv, dn, preferred_element_type=F32, precision=precision)

        @pl.when(k == nk - 1)
        def _():
            o_ref[...] = acc_ref[...].astype(o_ref.dtype)

    a_spec = pl.BlockSpec((tk, tm), lambda i, j, k: (k, i)) if ta else pl.BlockSpec((tm, tk), lambda i, j, k: (i, k))
    b_spec = pl.BlockSpec((tn, tk), lambda i, j, k: (j, k)) if tb else pl.BlockSpec((tk, tn), lambda i, j, k: (k, j))
    if b_lead is not None or b_shards is not None:
        b_blk, b_map = tuple(b_spec.block_shape), b_spec.index_map
        lead = () if b_lead is None else (b_lead,)
        per = None if b_shards is None else b_dims[1] // b_blk[1]

        def b_index(i, j, k):
            r, c = b_map(i, j, k)
            return lead + (r, c) if per is None else (c // per,) + lead + (r, c % per)

        b_spec = pl.BlockSpec((None,) * (len(lead) + (per is not None)) + b_blk, b_index)
    if col_blocks is None:
        out_spec = pl.BlockSpec((tm, tn), lambda i, j, k: (i, j))
        out_shape = jax.ShapeDtypeStruct((M, N), out_dtype)
    else:
        per = (N // col_blocks) // tn
        assert per * tn * col_blocks == N, (N, col_blocks, tn)
        out_spec = pl.BlockSpec((None, tm, tn), lambda i, j, k: (j // per, i, j % per))
        out_shape = jax.ShapeDtypeStruct((col_blocks, M, N // col_blocks), out_dtype)
    return _pcall(
        body, name=name, grid=(nm, nn, nk), in_specs=[a_spec, b_spec], out_specs=out_spec, out_shape=out_shape,
        scratch_shapes=[pltpu.VMEM((tm, tn), F32)], compiler_params=_cparams(3),
    )(a, b)


def _norm_rows(rows):
    out = []
    for r in rows:
        if not isinstance(r, tuple):
            r = (r,)
        arr, off, width, roff = (r + (0, None, 0)[len(r) - 1:])
        out.append((arr, off, width if width is not None else arr.shape[1], roff))
    return out


def _rw_plan(T, rows, pars, seg_rows, col_tile, tm_cap):
    widths = [r[2] for r in rows]
    wmax = max(widths + [p.shape[-1] for p in pars] + [1])
    if col_tile is not None:
        assert all(w == widths[0] for w in widths) and all(p.shape[-1] == widths[0] for p in pars)
        ncol = widths[0] // col_tile
        assert ncol * col_tile == widths[0]
        wmax = col_tile
    else:
        ncol = 1
    cap = tm_cap if tm_cap is not None else max(SUBLANE, min(512, (512 * 1024) // wmax))
    tm = _row_tile(T, cap, also=tuple(seg_rows) + tuple(abs(r[3]) for r in rows if r[3]))
    bounds = tuple(s // tm for s in seg_rows)
    return widths, ncol, tm, bounds


def _rw_specs(rows, pars, ncol, tm, bounds, col_tile):
    def seg(i):
        s = 0
        for b in bounds:
            s = s + (i >= b).astype(jnp.int32)
        return s

    specs = []
    for arr, off, w, roff in rows:
        bw = col_tile if col_tile is not None else w
        assert off % bw == 0 and roff % tm == 0, (off, bw, roff, tm)
        specs.append(pl.BlockSpec((tm, bw), functools.partial(
            lambda j, i, ob, rb, last: (jnp.clip(i + rb, 0, last), ob + j),
            ob=off // bw, rb=roff // tm, last=arr.shape[0] // tm - 1)))
    for p in pars:
        bw = col_tile if col_tile is not None else p.shape[-1]
        if p.shape[0] > 1:
            specs.append(pl.BlockSpec((None, 1, bw), lambda j, i: (seg(i), 0, j)))
        else:
            specs.append(pl.BlockSpec((None, 1, bw), lambda j, i: (0, 0, j)))
    return specs, seg


def _head_rows(head):
    top, bottom = head
    return [(top, 0, None, 0), (bottom, 0, None, -top.shape[0])]


def _rw_fwd(name, f, rows, pars, out_widths, *, T=None, seg_rows=(), col_tile=None, tm_cap=None, out_dtypes=None,
            head=None):
    rows = _norm_rows((_head_rows(head) if head else []) + list(rows))
    T = rows[0][0].shape[0] if T is None else T
    widths, ncol, tm, bounds = _rw_plan(T, rows, pars, seg_rows, col_tile, tm_cap)
    in_specs, _ = _rw_specs(rows, pars, ncol, tm, bounds, col_tile)
    nr, npar, nout = len(rows), len(pars), len(out_widths)

    def body(*refs):
        vals = [r[...] for r in refs[:nr + npar]]
        if head:
            vals = [jnp.where(pl.program_id(1) < head[0].shape[0] // tm, vals[0], vals[1])] + vals[2:]
        outs = f(*vals)
        if not isinstance(outs, (tuple, list)):
            outs = (outs,)
        for o_ref, o in zip(refs[nr + npar:], outs):
            o_ref[...] = o.astype(o_ref.dtype)

    out_specs = [pl.BlockSpec((tm, col_tile if col_tile is not None else w), lambda j, i: (i, j)) for w in out_widths]
    res = _pcall(
        body, name=name, grid=(ncol, T // tm), in_specs=in_specs, out_specs=out_specs,
        out_shape=[jax.ShapeDtypeStruct((T, w), dt) for w, dt in zip(out_widths, out_dtypes or [F32] * nout)],
        compiler_params=_cparams(2),
    )(*[r[0] for r in rows], *pars)
    return res if nout > 1 else res[0]


def _rw_bwd(name, f, rows, pars, cots, *, row_grad, par_grad, T=None, seg_rows=(), col_tile=None, tm_cap=None,
            add=None, cot_fn=None, row_dtypes=None, head=None, skip_rows=0):
    rows = _norm_rows((_head_rows(head) if head else []) + list(rows))
    cots = _norm_rows(cots)
    T = rows[0][0].shape[0] if T is None else T
    extra = _norm_rows([add]) if add is not None else []
    all_rows = rows + cots + extra
    widths, ncol, tm, bounds = _rw_plan(T, all_rows, pars, tuple(seg_rows) + ((skip_rows,) if skip_rows else ()),
                                        col_tile, tm_cap)
    bounds = bounds[:len(seg_rows)]
    in_specs, seg = _rw_specs(all_rows, pars, ncol, tm, bounds, col_tile)
    nr, nc, ne, npar = len(rows), len(cots), len(extra), len(pars)
    skip = 1 if head else 0
    widths = widths[skip:]
    nrf = nr - skip
    row_idx = [k for k in range(nrf) if row_grad[k]]
    par_idx = [k for k in range(npar) if par_grad[k]]

    def body(*refs):
        i = pl.program_id(1)

        def zero_before(vals, ops):
            return [jnp.where(i + c[3] // tm >= 0, v, jnp.zeros_like(v)) if c[3] < 0 else v for v, c in zip(vals, ops)]

        row_vals = [r[...] for r in refs[:nr]]
        if head:
            row_vals = [jnp.where(i < head[0].shape[0] // tm, row_vals[0], row_vals[1])] + row_vals[2:]
        cot_vals = zero_before([r[...] for r in refs[nr:nr + nc]], cots)
        add_vals = zero_before([r[...] for r in refs[nr + nc:nr + nc + ne]], extra)
        par_vals = [r[...] for r in refs[nr + nc + ne:nr + nc + ne + npar]]
        out_refs = refs[nr + nc + ne + npar:]
        outs, vjp = jax.vjp(f, *row_vals, *par_vals)
        if cot_fn is not None:
            cot_vals = cot_fn(*cot_vals)
            if not isinstance(cot_vals, (tuple, list)):
                cot_vals = (cot_vals,)
        if isinstance(outs, (tuple, list)):
            grads = vjp(tuple(c.astype(o.dtype) for c, o in zip(cot_vals, outs)))
        else:
            grads = vjp(cot_vals[0].astype(outs.dtype))
        first_seg = i == 0
        for b in bounds:
            first_seg = first_seg | (i == b)
        for n, k in enumerate(row_idx):
            g = grads[k]
            if n == 0 and add_vals:
                g = g + add_vals[0]
            out_refs[n][...] = g.astype(out_refs[n].dtype)
        for n, k in enumerate(par_idx):
            g = grads[nrf + k]
            o_ref = out_refs[len(row_idx) + n]
            first = first_seg if pars[k].shape[0] > 1 else (i == 0)

            @pl.when(first)
            def _(o_ref=o_ref, g=g):
                o_ref[...] = g

            @pl.when(jnp.logical_not(first))
            def _(o_ref=o_ref, g=g):
                o_ref[...] += g

    out_specs, out_shape = [], []
    for k in row_idx:
        w = widths[k]
        out_specs.append(pl.BlockSpec((tm, col_tile if col_tile is not None else w),
                                      lambda j, i: (jnp.maximum(i - skip_rows // tm, 0), j)))
        out_shape.append(jax.ShapeDtypeStruct((T - skip_rows, w), row_dtypes[len(out_shape)] if row_dtypes else F32))
    for k in par_idx:
        p = pars[k]
        bw = col_tile if col_tile is not None else p.shape[-1]
        if p.shape[0] > 1:
            out_specs.append(pl.BlockSpec((None, 1, bw), lambda j, i: (seg(i), 0, j)))
        else:
            out_specs.append(pl.BlockSpec((None, 1, bw), lambda j, i: (0, 0, j)))
        out_shape.append(jax.ShapeDtypeStruct(p.shape, F32))
    res = _pcall(
        body, name=name, grid=(ncol, T // tm), in_specs=in_specs, out_specs=out_specs, out_shape=out_shape,
        compiler_params=_cparams(2),
    )(*[r[0] for r in all_rows], *pars)
    return list(res[:len(row_idx)]), list(res[len(row_idx):])


def _f_modnorm(h, w, sc, sh):
    y = h * lax.rsqrt(jnp.mean(h * h, axis=-1, keepdims=True) + EPS)
    return (y * w) * (1.0 + sc) + sh


def _f_gate_res(h, y, g):
    return h + g * y


def _f_gate_res_bias(h, y, g, b):
    return h + g * (y + b)


def _f_gate(y, g):
    return g * y


def _f_gate_bias(y, g, b):
    return g * (y + b)


def _f_ffn_act(val, gate):
    return _silu(gate) * val


def _f_softplus(raw, bias):
    v = raw + bias
    return jnp.maximum(v, 0.0) + jnp.log(1.0 + jnp.exp(-jnp.abs(v)))


def _f_ssd_gate(yf, yb, xs, z, d_rep, nw):
    y = (yf + yb + d_rep * xs) * _silu(z)
    return (y * lax.rsqrt(jnp.mean(y * y, axis=-1, keepdims=True) + EPS)) * nw


def _f_glu(a, g, ba, bg):
    return (a + ba) * jax.nn.sigmoid(g + bg)


def _f_ln_silu(h, w, b):
    mu = jnp.mean(h, axis=-1, keepdims=True)
    d = h - mu
    y = d * lax.rsqrt(jnp.mean(d * d, axis=-1, keepdims=True) + EPS)
    return _silu(y * w + b)


def _f_loss_rows(h, t, w):
    y = (h * lax.rsqrt(jnp.mean(h * h, axis=-1, keepdims=True) + EPS)) * w
    e = y - t
    return 0.5 * jnp.mean(e * e, axis=-1, keepdims=True)


def _f_adamw(w, m, v, ga, gb):
    g = ga.astype(F32) + gb
    m = ADAM_B1 * m + (1.0 - ADAM_B1) * g
    v = ADAM_B2 * v + (1.0 - ADAM_B2) * (g * g)
    m_hat = m / (1.0 - ADAM_B1 ** ADAM_STEP)
    v_hat = v / (1.0 - ADAM_B2 ** ADAM_STEP)
    delta = -ADAM_LR * (m_hat / (jnp.sqrt(v_hat) + ADAM_EPS) + ADAM_WD * w)
    return g, delta, m, v


def _adamw(name, w, m, v, ga, gb):
    shape = w.shape
    c = shape[-1]
    two_d = [t.reshape(-1, c) for t in (w, m, v, ga, gb)]
    rows = two_d[0].shape[0]
    pad = _round_up(rows, SUBLANE) - rows
    if pad:
        two_d = [jnp.pad(t, ((0, pad), (0, 0))) for t in two_d]
    outs = _rw_fwd(name, _f_adamw, two_d, [], [c] * 4)
    return tuple(o[:rows].reshape(shape) for o in outs)


def _sum_leading(name, x, idxs, out_dtype=F32):
    _, R, C = x.shape
    tm = _row_tile(R, max(SUBLANE, min(512, (512 * 1024) // C)))

    def body(x_ref, o_ref):
        acc = x_ref[idxs[0]].astype(F32)
        for k in idxs[1:]:
            acc = acc + x_ref[k].astype(F32)
        o_ref[...] = acc.astype(o_ref.dtype)

    return _pcall(
        body, name=name, grid=(R // tm,), in_specs=[pl.BlockSpec((x.shape[0], tm, C), lambda i: (0, i, 0))],
        out_specs=pl.BlockSpec((tm, C), lambda i: (i, 0)), out_shape=jax.ShapeDtypeStruct((R, C), out_dtype),
        compiler_params=_cparams(1),
    )(x)


def _loss_fwd(h, t, w):
    T, D = h.shape
    tm = _row_tile(T, 256)

    def body(h_ref, t_ref, w_ref, o_ref):
        i = pl.program_id(0)
        part = jnp.sum(_f_loss_rows(h_ref[...], t_ref[...], w_ref[...]), axis=0, keepdims=True)
        part = jnp.broadcast_to(part, (1, LANE))

        @pl.when(i == 0)
        def _():
            o_ref[...] = part

        @pl.when(i > 0)
        def _():
            o_ref[...] += part

    return _pcall(
        body, name="loss_fwd", grid=(T // tm,),
        in_specs=[pl.BlockSpec((tm, D), lambda i: (i, 0)), pl.BlockSpec((tm, D), lambda i: (i, 0)),
                  pl.BlockSpec((1, D), lambda i: (0, 0))],
        out_specs=pl.BlockSpec((1, LANE), lambda i: (0, 0)), out_shape=jax.ShapeDtypeStruct((1, LANE), F32),
        compiler_params=_cparams(1),
    )(h, t, w)


CONV_ROWS = 256
CONV_ROWS_FEW_TAPS = 1024
CONV_ACC_ELEMS = 16384


def _col_mask(arg, t):
    col = jnp.bitwise_and(t, GRID_W - 1)
    return (col != 0) if arg < 0 else (col != GRID_W - 1)


def _conv_plan(T, C, taps):
    seg = [m[1] for _, m in taps if m is not None and m[0] == "seg"]
    cap = CONV_ROWS_FEW_TAPS if len(taps) <= 9 else CONV_ROWS
    rc = next(r for r in (1024, 768, 512, 256, LANE) if r <= cap and T % r == 0)
    ct = next((t for t in (512, 256, LANE) if C % t == 0), C)
    reach = max(abs(s) for s, _ in taps)
    hb = next(h for h in (8, 16, 32, 64, 128, 256) if h >= reach and rc % h == 0)
    sub = max(2 * SUBLANE, min(rc, CONV_ACC_ELEMS // ct))
    boundary = None
    if seg:
        inside = seg[0] % rc
        boundary = (seg[0], (inside - reach, inside + reach) if inside else None)
    taps = [(s, None if (m is None or m[0] == "seg") else m[1]) for s, m in taps]
    return rc, ct, hb, sub, T // rc, C // ct, boundary, taps


def _seg_ok(boundary, i, rc, r0, n, s):
    if boundary is None or boundary[1] is None or s == 0 or r0 + n <= boundary[1][0] or r0 >= boundary[1][1]:
        return None
    t = i * rc + r0 + lax.broadcasted_iota(jnp.int32, (n, 1), 0)
    return (t >= boundary[0]) == ((t + s) >= boundary[0])


def _halo_specs(rc, ct, hb, T, off_blocks):
    per = rc // hb
    last = T // hb - 1
    prev = pl.BlockSpec((hb, ct), lambda j, i: (jnp.maximum(i * per - 1, 0), off_blocks + j))
    cur = pl.BlockSpec((rc, ct), lambda j, i: (i, off_blocks + j))
    nxt = pl.BlockSpec((hb, ct), lambda j, i: (jnp.minimum((i + 1) * per, last), off_blocks + j))
    return [prev, cur, nxt]


def _fill_halo(pad_ref, p_ref, c_ref, n_ref, i, nrc, rc, hb, boundary):
    has_prev = i > 0
    has_next = i < nrc - 1
    if boundary is not None:
        has_prev = has_prev & (i * rc != boundary[0])
        has_next = has_next & ((i + 1) * rc != boundary[0])
    pad_ref[0:hb, :] = jnp.where(has_prev, p_ref[...], 0.0)
    pad_ref[hb:hb + rc, :] = c_ref[...]
    pad_ref[hb + rc:hb + rc + hb, :] = jnp.where(has_next, n_ref[...], 0.0)


def _shift_plan(keys):
    count = {}
    for s, m in keys:
        k = (s % SUBLANE, m)
        count[k] = count.get(k, 0) + 1
    slots = {}
    for k, n in sorted(count.items(), key=lambda kv: (kv[0][0], str(kv[0][1]))):
        if k != (0, None) and (n >= 2 or k[1] is not None):
            slots[k] = len(slots)
    return slots


def _build_shifted(copies_ref, slots, pad_ref, keys, i, rc, hb, sub):
    for (r, m), slot in slots.items():
        qs = [s - r for s, mk in keys if (s % SUBLANE, mk) == (r, m)]
        lo, hi = hb + min(qs), hb + rc + max(qs)
        for p in range(lo, hi, sub):
            n = min(sub, hi - p)
            v = pad_ref[p + r:p + r + n, :]
            if m is not None:
                t = i * rc - hb + p + r + lax.broadcasted_iota(jnp.int32, (n, 1), 0)
                v = jnp.where(_col_mask(m, t), v, 0.0)
            copies_ref[slot, p:p + n, :] = v


def _read(copies_ref, slots, pad_ref, s, m, row, n):
    k = (s % SUBLANE, m)
    if k in slots:
        q = s - k[0]
        return copies_ref[slots[k], row + q:row + q + n, :]
    return pad_ref[row + s:row + s + n, :]


def _conv_fwd(name, u, col_off, C, w, b, taps, act=False):
    T = u.shape[0]
    rc, ct, hb, sub, nrc, ncc, boundary, taps = _conv_plan(T, C, taps)
    assert col_off % ct == 0
    K = len(taps)
    keys = [(s, None) for s, _ in taps]
    slots = _shift_plan(keys)
    dirs = sorted({m for _, m in taps if m is not None})

    def body(up, uc, un, w_ref, b_ref, *rest):
        y_ref = rest[0]
        pad_ref, copies_ref = rest[-2], rest[-1]
        i = pl.program_id(1)
        _fill_halo(pad_ref, up, uc, un, i, nrc, rc, hb, boundary)
        _build_shifted(copies_ref, slots, pad_ref, keys, i, rc, hb, sub)
        for r0 in range(0, rc, sub):
            acc = jnp.broadcast_to(b_ref[...], (sub, ct))
            for m in [None] + dirs:
                part = None
                for k, (s, mk) in enumerate(taps):
                    if mk != m:
                        continue
                    v = _read(copies_ref, slots, pad_ref, s, None, hb + r0, sub)
                    ok = _seg_ok(boundary, i, rc, r0, sub, s)
                    term = w_ref[k:k + 1, :] * (v if ok is None else jnp.where(ok, v, 0.0))
                    part = term if part is None else part + term
                if part is None:
                    continue
                if m is not None:
                    t = i * rc + r0 + lax.broadcasted_iota(jnp.int32, (sub, 1), 0)
                    part = jnp.where(_col_mask(m, t), part, 0.0)
                acc = acc + part
            y_ref[r0:r0 + sub, :] = acc
            if act:
                rest[1][r0:r0 + sub, :] = _silu(acc)

    n_out = 2 if act else 1
    res = _pcall(
        body, name=name, grid=(ncc, nrc),
        in_specs=_halo_specs(rc, ct, hb, T, col_off // ct) + [pl.BlockSpec((K, ct), lambda j, i: (0, j)),
                                                              pl.BlockSpec((1, ct), lambda j, i: (0, j))],
        out_specs=[pl.BlockSpec((rc, ct), lambda j, i: (i, j))] * n_out,
        out_shape=[jax.ShapeDtypeStruct((T, C), F32)] * n_out,
        scratch_shapes=[pltpu.VMEM((rc + 2 * hb, ct), F32), pltpu.VMEM((max(len(slots), 1), rc + 2 * hb, ct), F32)],
        compiler_params=_cparams(2),
    )(u, u, u, w, b)
    return res if act else res[0]


def _conv_bwd(name, u, col_off, C, w, g, taps, du_dtype=F32):
    T = u.shape[0]
    rc, ct, hb, sub, nrc, ncc, boundary, taps = _conv_plan(T, C, taps)
    K = len(taps)
    u_keys = [(s, None) for s, _ in taps]
    dirs = sorted({m for _, m in taps if m is not None})
    g_keys = [(-s, m) for s, m in taps] + [(0, m) for m in dirs]
    u_slots, g_slots = _shift_plan(u_keys), _shift_plan(g_keys)

    def body(up, uc, un, gp, gc, gn, w_ref, du_ref, dw_ref, db_ref, upad, gpad, ucopies, gcopies):
        i = pl.program_id(1)
        _fill_halo(upad, up, uc, un, i, nrc, rc, hb, boundary)
        _fill_halo(gpad, gp, gc, gn, i, nrc, rc, hb, boundary)
        _build_shifted(ucopies, u_slots, upad, u_keys, i, rc, hb, sub)
        _build_shifted(gcopies, g_slots, gpad, g_keys, i, rc, hb, sub)

        @pl.when(i == 0)
        def _():
            dw_ref[...] = jnp.zeros_like(dw_ref)
            db_ref[...] = jnp.zeros_like(db_ref)

        def fold(v):
            return jnp.sum(v.reshape(sub // SUBLANE, SUBLANE, ct), axis=0)

        dbs = jnp.zeros((SUBLANE, ct), F32)
        for r0 in range(0, rc, sub):
            dbs = dbs + fold(gpad[hb + r0:hb + r0 + sub, :])
            acc = jnp.zeros((sub, ct), F32)
            for k, (s, m) in enumerate(taps):
                v = _read(gcopies, g_slots, gpad, -s, m, hb + r0, sub)
                ok = _seg_ok(boundary, i, rc, r0, sub, -s)
                acc = acc + w_ref[k:k + 1, :] * (v if ok is None else jnp.where(ok, v, 0.0))
            du_ref[r0:r0 + sub, :] = acc.astype(du_ref.dtype)
        db_ref[...] += jnp.sum(dbs, axis=0, keepdims=True)
        for k, (s, m) in enumerate(taps):
            part = jnp.zeros((SUBLANE, ct), F32)
            for r0 in range(0, rc, sub):
                v = _read(ucopies, u_slots, upad, s, None, hb + r0, sub)
                ok = _seg_ok(boundary, i, rc, r0, sub, s)
                part = part + fold(_read(gcopies, g_slots, gpad, 0, m, hb + r0, sub)
                                   * (v if ok is None else jnp.where(ok, v, 0.0)))
            dw_ref[k:k + 1, :] += jnp.sum(part, axis=0, keepdims=True)

    halo_u = _halo_specs(rc, ct, hb, T, col_off // ct)
    halo_g = _halo_specs(rc, ct, hb, T, 0)
    rows = rc + 2 * hb
    return _pcall(
        body, name=name, grid=(ncc, nrc),
        in_specs=halo_u + halo_g + [pl.BlockSpec((K, ct), lambda j, i: (0, j))],
        out_specs=[pl.BlockSpec((rc, ct), lambda j, i: (i, j)), pl.BlockSpec((K, ct), lambda j, i: (0, j)),
                   pl.BlockSpec((1, ct), lambda j, i: (0, j))],
        out_shape=[jax.ShapeDtypeStruct((T, C), du_dtype), jax.ShapeDtypeStruct((K, C), F32),
                   jax.ShapeDtypeStruct((1, C), F32)],
        scratch_shapes=[pltpu.VMEM((rows, ct), F32), pltpu.VMEM((rows, ct), F32),
                        pltpu.VMEM((max(len(u_slots), 1), rows, ct), F32),
                        pltpu.VMEM((max(len(g_slots), 1), rows, ct), F32)],
        compiler_params=_cparams(2),
    )(u, u, u, g, g, g, w)


def _ssd_group(xg, bm, cm, s_in, *per_head, reverse, P):
    R = len(per_head) // 2
    dtrs, a_s = per_head[:R], per_head[R:]
    q, rp = xg.shape
    ii = lax.broadcasted_iota(jnp.int32, (q, q), 0)
    jj = lax.broadcasted_iota(jnp.int32, (q, q), 1)
    causal = (jj >= ii) if reverse else (jj <= ii)
    causal_t = (ii >= jj) if reverse else (ii <= jj)
    eye = ii == jj
    lane = lax.broadcasted_iota(jnp.int32, (1, rp), 1)
    row = lax.broadcasted_iota(jnp.int32, (rp, 1), 0)
    nt = (((1,), (1,)), ((), ()))
    tn = (((0,), (0,)), ((), ()))
    cb = lax.dot_general(cm.astype(BF16), bm.astype(BF16), nt, preferred_element_type=F32)
    dt_x = jnp.zeros((q, rp), F32)
    acum_x = jnp.zeros((q, rp), F32)
    tot_row = jnp.zeros((1, rp), F32)
    tot_col = jnp.zeros((rp, 1), F32)
    wts, lane_masks = [], []
    for r in range(R):
        hm = (lane >= r * P) & (lane < (r + 1) * P)
        hc = (row >= r * P) & (row < (r + 1) * P)
        dt_c = jnp.sum(jnp.where(eye, dtrs[r], 0.0), axis=1, keepdims=True)
        dac = dt_c * a_s[r]
        dar = dtrs[r] * a_s[r]
        acum_c = jnp.sum(jnp.where(causal, dar, 0.0), axis=1, keepdims=True)
        acum_r = jnp.sum(jnp.where(causal_t, dac, 0.0), axis=0, keepdims=True)
        decay = jnp.where(causal, jnp.exp(jnp.where(causal, acum_c - acum_r, 0.0)), 0.0)
        tot = jnp.sum(dac, axis=0, keepdims=True)
        dt_x = jnp.where(hm, dt_c, dt_x)
        acum_x = jnp.where(hm, acum_c, acum_x)
        tot_row = jnp.where(hm, tot, tot_row)
        tot_col = jnp.where(hc, tot, tot_col)
        wts.append((cb * decay).astype(BF16))
        lane_masks.append(hm)
    xdt = xg * dt_x
    xdt_b = xdt.astype(BF16)
    y = jnp.zeros((q, rp), F32)
    for r in range(R):
        y = jnp.where(lane_masks[r], jnp.dot(wts[r], xdt_b, preferred_element_type=F32), y)
    dte = jnp.exp(tot_row - acum_x)
    cs = lax.dot_general((xdt * dte).astype(BF16), bm.astype(BF16), tn, preferred_element_type=F32)
    y = y + lax.dot_general(cm.astype(BF16), s_in.astype(BF16), nt, preferred_element_type=F32) * jnp.exp(acum_x)
    s_out = jnp.exp(tot_col) * s_in + cs
    return y, s_out


def _ssd_group_state(xg, bm, s_in, *per_head, reverse, P):
    R = len(per_head) // 2
    dtrs, a_s = per_head[:R], per_head[R:]
    q, rp = xg.shape
    ii = lax.broadcasted_iota(jnp.int32, (q, q), 0)
    jj = lax.broadcasted_iota(jnp.int32, (q, q), 1)
    causal = (jj >= ii) if reverse else (jj <= ii)
    eye = ii == jj
    lane = lax.broadcasted_iota(jnp.int32, (1, rp), 1)
    row = lax.broadcasted_iota(jnp.int32, (rp, 1), 0)
    dt_x = jnp.zeros((q, rp), F32)
    acum_x = jnp.zeros((q, rp), F32)
    tot_row = jnp.zeros((1, rp), F32)
    tot_col = jnp.zeros((rp, 1), F32)
    for r in range(R):
        hm = (lane >= r * P) & (lane < (r + 1) * P)
        hc = (row >= r * P) & (row < (r + 1) * P)
        dt_c = jnp.sum(jnp.where(eye, dtrs[r], 0.0), axis=1, keepdims=True)
        acum_c = jnp.sum(jnp.where(causal, dtrs[r] * a_s[r], 0.0), axis=1, keepdims=True)
        tot = jnp.sum(dt_c * a_s[r], axis=0, keepdims=True)
        dt_x = jnp.where(hm, dt_c, dt_x)
        acum_x = jnp.where(hm, acum_c, acum_x)
        tot_row = jnp.where(hm, tot, tot_row)
        tot_col = jnp.where(hc, tot, tot_col)
    xe = xg * dt_x * jnp.exp(tot_row - acum_x)
    cs = lax.dot_general(xe.astype(BF16), bm.astype(BF16), (((0,), (0,)), ((), ())), preferred_element_type=F32)
    return jnp.exp(tot_col) * s_in + cs


def _ssd_maps(NC, ncc, reverse_steps):
    def chunk(d, s):
        if reverse_steps:
            s = NC - 1 - s
        return s if d == 0 else jnp.where(s < ncc, ncc - 1 - s, NC - 1 - s + ncc)

    def lat_chunk(d, s):
        c = chunk(d, s) - ncc
        return jnp.where(c < 0, 0 if d == 0 else NC - ncc - 1, c)

    def step(s):
        return NC - 1 - s if reverse_steps else s

    return chunk, lat_chunk, step


SSD_GROUPS_PER_STEP = 2


def _ssd_specs(chunk, d, GB, R, Q, N, RP, b_off, c_off):
    assert b_off % (GB * N) == 0 and c_off % (GB * N) == 0
    bo, co = b_off // (GB * N), c_off // (GB * N)
    return [
        pl.BlockSpec((Q, GB * RP), lambda g, s: (chunk(d, s), g)),
        pl.BlockSpec((Q, GB * N), lambda g, s: (chunk(d, s), bo + g)),
        pl.BlockSpec((Q, GB * N), lambda g, s: (chunk(d, s), co + g)),
        pl.BlockSpec((GB * R, 1, Q), lambda g, s: (g, 0, chunk(d, s))),
        pl.BlockSpec((GB * R, 1, 1), lambda g, s: (g, 0, 0)),
    ]


def _ssd_fwd(xbc, b_off, c_off, dtr, a, P, ncc):
    T = xbc.shape[0]
    H = dtr[0].shape[0]
    N, Q = SSD_STATE, SSD_CHUNK
    NC = T // Q
    G = (c_off - b_off) // N
    R = H // G
    RP = R * P
    GB = SSD_GROUPS_PER_STEP if G % SSD_GROUPS_PER_STEP == 0 else 1
    chunk, lat_chunk, _ = _ssd_maps(NC, ncc, False)

    def body(*refs):
        s = pl.program_id(1)
        s_ref = refs[-1]

        @pl.when(s == 0)
        def _():
            s_ref[...] = jnp.zeros_like(s_ref)

        for d in range(2):
            x_ref, b_ref, c_ref, dtr_ref, a_ref = refs[5 * d:5 * d + 5]
            y_ref, se_ref = refs[10 + 2 * d:12 + 2 * d]
            for gg in range(GB):
                cols, bcols = slice(gg * RP, (gg + 1) * RP), slice(gg * N, (gg + 1) * N)
                s_in = s_ref[d, gg]
                se_ref[gg] = s_in
                per_head = [dtr_ref[gg * R + r] for r in range(R)] + [a_ref[gg * R + r] for r in range(R)]

                @pl.when(s >= ncc)
                def _(d=d, gg=gg, cols=cols, bcols=bcols, x_ref=x_ref, b_ref=b_ref, c_ref=c_ref, y_ref=y_ref,
                      s_in=s_in, per_head=per_head):
                    y, s_out = _ssd_group(x_ref[:, cols], b_ref[:, bcols], c_ref[:, bcols], s_in, *per_head,
                                          reverse=d == 1, P=P)
                    y_ref[:, cols] = y
                    s_ref[d, gg] = s_out

                @pl.when(s < ncc)
                def _(d=d, gg=gg, cols=cols, bcols=bcols, x_ref=x_ref, b_ref=b_ref, s_in=s_in, per_head=per_head):
                    s_ref[d, gg] = _ssd_group_state(x_ref[:, cols], b_ref[:, bcols], s_in, *per_head,
                                                    reverse=d == 1, P=P)

    in_specs, out_specs, out_shape, operands = [], [], [], []
    for d in range(2):
        in_specs += _ssd_specs(chunk, d, GB, R, Q, N, RP, b_off, c_off)
        operands += [xbc, xbc, xbc, dtr[d], a[d]]
        out_specs += [pl.BlockSpec((Q, GB * RP), functools.partial(lambda g, s, d: (lat_chunk(d, s), g), d=d)),
                      pl.BlockSpec((GB, None, RP, N), lambda g, s: (g, s, 0, 0))]
        out_shape += [jax.ShapeDtypeStruct((T - ncc * Q, H * P), F32), jax.ShapeDtypeStruct((G, NC, RP, N), F32)]
    y_f, se_f, y_b, se_b = _pcall(
        body, name="ssd_fwd", grid=(G // GB, NC), in_specs=in_specs, out_specs=out_specs, out_shape=out_shape,
        scratch_shapes=[pltpu.VMEM((2, GB, RP, N), F32)], compiler_params=_cparams(2),
    )(*operands)
    return (y_f, y_b), (se_f, se_b)


def _ssd_bwd(xbc, b_off, c_off, dtr, a, s_enter, dy, P, ncc):
    T = xbc.shape[0]
    H = dtr[0].shape[0]
    N, Q = SSD_STATE, SSD_CHUNK
    NC = T // Q
    G = (c_off - b_off) // N
    R = H // G
    RP = R * P
    GB = SSD_GROUPS_PER_STEP if G % SSD_GROUPS_PER_STEP == 0 else 1
    chunk, lat_chunk, step = _ssd_maps(NC, ncc, True)
    n_in, n_out = 7, 5

    def body(*refs):
        s = pl.program_id(1)
        ds_ref = refs[-1]

        @pl.when(s == 0)
        def _():
            ds_ref[...] = jnp.zeros_like(ds_ref)

        for d in range(2):
            x_ref, b_ref, c_ref, dtr_ref, a_ref, se_ref, dy_ref = refs[n_in * d:n_in * (d + 1)]
            dx_ref, db_ref, dc_ref, ddtr_ref, da_ref = refs[2 * n_in + n_out * d:2 * n_in + n_out * (d + 1)]
            for gg in range(GB):
                cols, bcols = slice(gg * RP, (gg + 1) * RP), slice(gg * N, (gg + 1) * N)
                per_head = [dtr_ref[gg * R + r] for r in range(R)] + [a_ref[gg * R + r] for r in range(R)]

                def store(grads, dx_ref=dx_ref, db_ref=db_ref, ddtr_ref=ddtr_ref, da_ref=da_ref, d=d, gg=gg,
                          cols=cols, bcols=bcols):
                    dx_ref[:, cols] = grads[0]
                    db_ref[:, bcols] = grads[1]
                    ds_ref[d, gg] = grads[2]
                    for r in range(R):
                        ddtr_ref[gg * R + r] = grads[3 + r]
                        da_ref[gg, r] = jnp.broadcast_to(grads[3 + R + r], (SUBLANE, LANE))

                @pl.when(s < NC - ncc)
                def _(d=d, gg=gg, cols=cols, bcols=bcols, x_ref=x_ref, b_ref=b_ref, c_ref=c_ref, se_ref=se_ref,
                      dy_ref=dy_ref, dc_ref=dc_ref, per_head=per_head, store=store):
                    f = functools.partial(_ssd_group, reverse=d == 1, P=P)
                    _, vjp = jax.vjp(f, x_ref[:, cols], b_ref[:, bcols], c_ref[:, bcols], se_ref[gg], *per_head)
                    grads = vjp((dy_ref[:, cols], ds_ref[d, gg]))
                    dc_ref[:, bcols] = grads[2]
                    store(grads[:2] + grads[3:])

                @pl.when(s >= NC - ncc)
                def _(d=d, gg=gg, cols=cols, bcols=bcols, x_ref=x_ref, b_ref=b_ref, se_ref=se_ref, dc_ref=dc_ref,
                      per_head=per_head, store=store):
                    f = functools.partial(_ssd_group_state, reverse=d == 1, P=P)
                    _, vjp = jax.vjp(f, x_ref[:, cols], b_ref[:, bcols], se_ref[gg], *per_head)
                    dc_ref[:, bcols] = jnp.zeros((Q, N), F32)
                    store(vjp(ds_ref[d, gg]))

    in_specs, out_specs, out_shape, operands = [], [], [], []
    for d in range(2):
        in_specs += _ssd_specs(chunk, d, GB, R, Q, N, RP, b_off, c_off) + [
            pl.BlockSpec((GB, None, RP, N), lambda g, s: (g, step(s), 0, 0)),
            pl.BlockSpec((Q, GB * RP), functools.partial(lambda g, s, d: (lat_chunk(d, s), g), d=d)),
        ]
        operands += [xbc, xbc, xbc, dtr[d], a[d], s_enter[d], dy]
    for d in range(2):
        at_chunk = functools.partial(lambda g, s, d: (chunk(d, s), g), d=d)
        out_specs += [
            pl.BlockSpec((Q, GB * RP), at_chunk), pl.BlockSpec((Q, GB * N), at_chunk),
            pl.BlockSpec((Q, GB * N), at_chunk),
            pl.BlockSpec((GB * R, 1, Q), functools.partial(lambda g, s, d: (g, 0, chunk(d, s)), d=d)),
            pl.BlockSpec((GB, None, R, SUBLANE, LANE), lambda g, s: (g, s, 0, 0, 0)),
        ]
        out_shape += [
            jax.ShapeDtypeStruct((T, H * P), F32), jax.ShapeDtypeStruct((T, G * N), F32),
            jax.ShapeDtypeStruct((T, G * N), F32), jax.ShapeDtypeStruct((H, 1, T), F32),
            jax.ShapeDtypeStruct((G, NC, R, SUBLANE, LANE), F32),
        ]
    res = _pcall(
        body, name="ssd_bwd", grid=(G // GB, NC), in_specs=in_specs, out_specs=out_specs, out_shape=out_shape,
        scratch_shapes=[pltpu.VMEM((2, GB, RP, N), F32)], compiler_params=_cparams(2),
    )(*operands)
    return res[:n_out], res[n_out:]


def _allgather8(name, v):
    R, C = v.shape

    def body(x_ref, out_ref, send_sems, recv_sems, local_sem):
        x, y, c = lax.axis_index("x"), lax.axis_index("y"), lax.axis_index("c")
        me, sibling = (x, y, c), (x, y, 1 - c)
        chips = [(1 - x, y), (x, 1 - y), (1 - x, 1 - y)]

        def slot(px, py, pc):
            return out_ref.at[4 * px + 2 * py + pc]

        def copy(k, block, to, src=None):
            return pltpu.make_async_remote_copy(
                src_ref=slot(*block) if src is None else src, dst_ref=slot(*block),
                send_sem=send_sems.at[k], recv_sem=recv_sems.at[k], device_id=to, device_id_type=MESH)

        mine = pltpu.make_async_copy(x_ref, slot(*me), local_sem)
        mine.start()
        first = [copy(0, me, sibling, src=x_ref)]
        first += [copy(1 + j, me, (*chip, c), src=x_ref) for j, chip in enumerate(chips)]
        for cp in first:
            cp.start()
        passed = [copy(4 + j, (*chip, c), sibling) for j, chip in enumerate(chips)]
        for j, chip in enumerate(chips):
            copy(1 + j, (*chip, c), me).wait_recv()
            passed[j].start()
        copy(0, sibling, me).wait_recv()
        for j, chip in enumerate(chips):
            copy(4 + j, (*chip, 1 - c), me).wait_recv()
        for cp in first + passed:
            cp.wait_send()
        mine.wait()

    return _pcall(
        body, name=name, out_shape=jax.ShapeDtypeStruct((N_DEV, R, C), v.dtype),
        in_specs=[pl.BlockSpec(memory_space=pltpu.VMEM)], out_specs=pl.BlockSpec(memory_space=pltpu.VMEM),
        scratch_shapes=[pltpu.SemaphoreType.DMA((7,)), pltpu.SemaphoreType.DMA((7,)), pltpu.SemaphoreType.DMA],
        compiler_params=pltpu.CompilerParams(vmem_limit_bytes=VMEM_LIMIT_BYTES),
    )(v)


def _slot(ref, k, axis, size):
    if axis is None:
        return ref.at[k]
    align = LANE if size % LANE == 0 else 2 * SUBLANE
    assert size % align == 0
    return ref.at[(slice(None),) * axis + (pl.ds(pl.multiple_of(k * size, align), size),)]


def _exchange4_start(name, srcs, bcast, dep, axes=None, half=False):
    n = len(srcs)
    axes = list(axes) if axes is not None else [None] * n
    sizes = [None if ax is None else s.shape[ax] for s, ax in zip(srcs, axes)]

    def land_shape(s, ax):
        if not bcast:
            return s.shape
        if half:
            return (N_CHIPS,) + s.shape[1:]
        if ax is None:
            return (N_CHIPS,) + s.shape
        return s.shape[:ax] + (N_CHIPS * s.shape[ax],) + s.shape[ax + 1:]

    lands = [lax.empty(land_shape(s, ax), s.dtype) for s, ax in zip(srcs, axes)]

    def body(*refs):
        src, land = refs[:n], refs[n:2 * n]
        send_sems, recv_sems = refs[2 * n + 1], refs[2 * n + 2]
        token = refs[-1]
        x, y, c = lax.axis_index("x"), lax.axis_index("y"), lax.axis_index("c")
        me = 2 * x + y
        for a in range(n):
            for j, (px, py) in enumerate([(1 - x, y), (x, 1 - y), (1 - x, 1 - y)]):
                pltpu.make_async_remote_copy(
                    src_ref=(src[a].at[c] if half else src[a]) if bcast else src[a].at[2 * px + py],
                    dst_ref=_slot(land[a], me, axes[a], sizes[a]),
                    send_sem=send_sems.at[3 * a + j], recv_sem=recv_sems.at[3 * a + j], device_id=(px, py, c),
                    device_id_type=MESH).start()
        token[...] = jnp.zeros_like(token)

    hbm = pl.BlockSpec(memory_space=pltpu.HBM)
    sem = pl.BlockSpec(memory_space=pltpu.SEMAPHORE)
    outs = _pcall(
        body, name=name,
        out_shape=(pltpu.SemaphoreType.DMA((3 * n,)), pltpu.SemaphoreType.DMA((3 * n,)),
                   *[pltpu.HBM(s.shape, s.dtype) for s in srcs], *[pltpu.HBM(l.shape, l.dtype) for l in lands],
                   jax.ShapeDtypeStruct((SUBLANE, LANE), F32)),
        in_specs=[hbm] * (2 * n) + [pl.BlockSpec(memory_space=pl.ANY)],
        out_specs=(sem, sem, *[hbm] * (2 * n), pl.BlockSpec(memory_space=pltpu.VMEM)),
        input_output_aliases={k: 2 + k for k in range(2 * n)},
        compiler_params=pltpu.CompilerParams(has_side_effects=pltpu.SideEffectType.DATAFLOW_SIDE_EFFECTING),
    )(*[pltpu.with_memory_space_constraint(s, pltpu.HBM) for s in srcs],
      *[pltpu.with_memory_space_constraint(l, pltpu.HBM) for l in lands], dep)
    return (n, bcast, half, axes, sizes, outs[0], outs[1], outs[2:2 + n], outs[2 + n:2 + 2 * n]), outs[-1]


def _exchange4_wait(name, handle, after):
    n, bcast, half, axes, sizes, send_sems, recv_sems, src_thru, land_thru = handle

    def body(*refs):
        src, land = refs[:n], refs[n:2 * n]
        send_sems, recv_sems = refs[2 * n], refs[2 * n + 1]
        x, y, c = lax.axis_index("x"), lax.axis_index("y"), lax.axis_index("c")
        for a in range(n):
            for j, (px, py) in enumerate([(1 - x, y), (x, 1 - y), (1 - x, 1 - y)]):
                pk = 2 * px + py
                copy = pltpu.make_async_remote_copy(
                    src_ref=(src[a].at[c] if half else src[a]) if bcast else src[a].at[pk],
                    dst_ref=_slot(land[a], pk, axes[a], sizes[a]),
                    send_sem=send_sems.at[3 * a + j], recv_sem=recv_sems.at[3 * a + j], device_id=(px, py, c),
                    device_id_type=MESH)
                copy.wait_send()
                copy.wait_recv()

    hbm = pl.BlockSpec(memory_space=pltpu.HBM)
    sem = pl.BlockSpec(memory_space=pltpu.SEMAPHORE)
    outs = _pcall(
        body, name=name,
        out_shape=tuple(pltpu.HBM(t.shape, t.dtype) for t in (*src_thru, *land_thru)),
        in_specs=[hbm] * (2 * n) + [sem, sem, pl.BlockSpec(memory_space=pl.ANY)], out_specs=tuple([hbm] * (2 * n)),
        input_output_aliases={k: k for k in range(2 * n)},
        compiler_params=pltpu.CompilerParams(has_side_effects=pltpu.SideEffectType.DATAFLOW_SIDE_EFFECTING),
    )(*src_thru, *land_thru, send_sems, recv_sems, after)
    return list(outs[:n]), list(outs[n:])


def _tie(name, v, token):
    def body(v_ref, token_ref, o_ref):
        del v_ref, token_ref, o_ref

    any_spec = pl.BlockSpec(memory_space=pl.ANY)
    return _pcall(body, name=name, out_shape=jax.ShapeDtypeStruct(v.shape, v.dtype), in_specs=[any_spec, any_spec],
                  out_specs=any_spec, input_output_aliases={0: 0})(v, token)


def _fill_own(landed, own, me, bcast):
    blk = own if bcast else lax.dynamic_index_in_dim(own, me, 0, keepdims=False)
    return lax.dynamic_update_index_in_dim(landed, blk, me, 0)


def _swap_sibling(name, srcs, by_core=False):
    n = len(srcs)

    def body(*refs):
        src, out = refs[:n], refs[n:2 * n]
        send_sems, recv_sems = refs[2 * n:]
        x, y, c = lax.axis_index("x"), lax.axis_index("y"), lax.axis_index("c")
        copies = []
        for a in range(n):
            send = pltpu.make_async_remote_copy(
                src_ref=src[a], dst_ref=out[a].at[c] if by_core else out[a], send_sem=send_sems.at[a],
                recv_sem=recv_sems.at[a], device_id=(x, y, 1 - c), device_id_type=MESH)
            send.start()
            arrive = pltpu.make_async_remote_copy(
                src_ref=src[a], dst_ref=out[a].at[1 - c] if by_core else out[a], send_sem=send_sems.at[a],
                recv_sem=recv_sems.at[a], device_id=(x, y, 1 - c), device_id_type=MESH)
            copies.append((send, arrive))
        for send, arrive in copies:
            send.wait_send()
            arrive.wait_recv()

    any_spec = pl.BlockSpec(memory_space=pl.ANY)
    return _pcall(
        body, name=name,
        out_shape=[jax.ShapeDtypeStruct(((2,) + s.shape) if by_core else s.shape, s.dtype) for s in srcs],
        in_specs=[any_spec] * n, out_specs=[any_spec] * n,
        scratch_shapes=[pltpu.SemaphoreType.DMA((n,)), pltpu.SemaphoreType.DMA((n,))],
    )(*srcs)


def _mod_fwd(c16, mod_w, mod_b_shard):
    nl, D, S = mod_w.shape

    def body(c_ref, w_ref, b_ref, o_ref):
        s = _silu(c_ref[...]).astype(BF16)
        o_ref[...] = jnp.dot(s, w_ref[...].astype(BF16), preferred_element_type=F32) + b_ref[...]

    return _pcall(
        body, name="mod_fwd", grid=(nl,),
        in_specs=[pl.BlockSpec((16, D), lambda l: (0, 0)), pl.BlockSpec((None, D, S), lambda l: (l, 0, 0)),
                  pl.BlockSpec((None, 1, S), lambda l: (l, 0, 0))],
        out_specs=pl.BlockSpec((None, 16, S), lambda l: (l, 0, 0)),
        out_shape=jax.ShapeDtypeStruct((nl, 16, S), F32), compiler_params=_cparams(1),
    )(c16, mod_w, mod_b_shard)


def _mod_w_update(s16t, dm16, w, m, v):
    nl, D, S = w.shape
    tm = _row_tile(D, 256)

    def body(s_ref, dm_ref, w_ref, m_ref, v_ref, g_ref, dl_ref, nm_ref, nv_ref):
        g = jnp.dot(s_ref[...], dm_ref[...], preferred_element_type=F32, precision=HIGHEST)
        g, dl, nm, nv = _f_adamw(w_ref[...], m_ref[...], v_ref[...], g, jnp.zeros_like(g))
        g_ref[...] = g
        dl_ref[...] = dl
        nm_ref[...] = nm
        nv_ref[...] = nv

    big = pl.BlockSpec((None, tm, S), lambda l, i: (l, i, 0))
    return _pcall(
        body, name="mod_w_update", grid=(nl, D // tm),
        in_specs=[pl.BlockSpec((tm, 16), lambda l, i: (i, 0)), pl.BlockSpec((None, 16, S), lambda l, i: (l, 0, 0)),
                  big, big, big],
        out_specs=[big] * 4, out_shape=[jax.ShapeDtypeStruct(w.shape, F32)] * 4, compiler_params=_cparams(2),
    )(s16t, dm16, w, m, v)


def _size(shape):
    n = 1
    for d in shape:
        n *= d
    return n


def _pack(arrs):
    pieces = []
    for a in arrs:
        flat = a.reshape(-1).astype(F32)
        pieces.append(jnp.pad(flat, (0, _round_up(flat.shape[0], LANE) - flat.shape[0])).reshape(-1, LANE))
    buf = jnp.concatenate(pieces, axis=0)
    return jnp.pad(buf, ((0, _round_up(buf.shape[0], SUBLANE) - buf.shape[0]), (0, 0)))


def _unpack(buf, shapes):
    lead = buf.shape[:-2]
    out, row = [], 0
    for s in shapes:
        n = _size(s)
        rows = _cdiv(n, LANE)
        piece = buf[..., row:row + rows, :].reshape(lead + (rows * LANE,))
        out.append(piece[..., :n].reshape(lead + tuple(s)))
        row += rows
    return out


def _adamw_many(name, ws, ms, vs, gs):
    n = len(ws)

    def body(*refs):
        for k in range(n):
            res = _f_adamw(refs[k][...], refs[n + k][...], refs[2 * n + k][...], refs[3 * n + k][...], 0.0)
            for j in range(4):
                refs[(4 + j) * n + k][...] = res[j]

    vmem = pl.BlockSpec(memory_space=pltpu.VMEM)
    res = _pcall(body, name=name, out_shape=[jax.ShapeDtypeStruct(w.shape, F32) for _ in range(4) for w in ws],
                 in_specs=[vmem] * (4 * n), out_specs=[vmem] * (4 * n))(*ws, *ms, *vs, *gs)
    return [tuple(res[j * n + k] for j in range(4)) for k in range(n)]


SHARD_AXIS = {
    "mod_w": 2, "ssd_w_in": 2, "ssd_conv_w": 2, "ssd_w_out": 1, "conf_w_pw1": 2, "conf_b_pw1": 1, "conf_w_dw": 2,
    "conf_b_dw": 1, "conf_ln_w": 1, "conf_ln_b": 1, "conf_w_pw2": 1, "conf_b_pw2": 1, "ffn_w_up": 2,
    "ffn_conv_w": 3, "ffn_w_down": 1,
}
BIG = ("ssd_w_in", "ssd_w_out", "conf_w_pw1", "conf_w_pw2", "ffn_w_up", "ffn_w_down")
WEIGHTS = ("c_ctx", "mod_w", "mod_b", "norm1_w", "norm2_w", "ssd_w_in", "ssd_conv_w", "ssd_conv_b", "ssd_dt_bias",
           "ssd_a_log", "ssd_d", "ssd_norm_w", "ssd_w_out", "conf_w_pw1", "conf_b_pw1", "conf_w_dw", "conf_b_dw",
           "conf_ln_w", "conf_ln_b", "conf_w_pw2", "conf_b_pw2", "ffn_w_up", "ffn_conv_w", "ffn_conv_b",
           "ffn_w_down", "final_norm_w")
SMALL = tuple(n for n in WEIGHTS if n not in BIG and n != "mod_w")
SMALL_SHARDED = tuple(n for n in SMALL if n in SHARD_AXIS)


def _unshard(stacked, axis):
    return jnp.concatenate([stacked[k] for k in range(N_CHIPS)], axis=axis)


def _to_blocks(full, axis):
    return jnp.stack(jnp.split(full, N_CHIPS, axis=axis))


def _par(v):
    v = v.reshape(-1, v.shape[-1])
    return v[:, None, :]


def kernel(x, c, ctx, c_ctx, mod_w, mod_b, norm1_w, norm2_w, ssd_w_in, ssd_conv_w, ssd_conv_b, ssd_dt_bias, ssd_a_log, ssd_d, ssd_norm_w, ssd_w_out, conf_w_pw1, conf_b_pw1, conf_w_dw, conf_b_dw, conf_ln_w, conf_ln_b, conf_w_pw2, conf_b_pw2, ffn_w_up, ffn_conv_w, ffn_conv_b, ffn_w_down, final_norm_w, loss_target, m_c_ctx, m_mod_w, m_mod_b, m_norm1_w, m_norm2_w, m_ssd_w_in, m_ssd_conv_w, m_ssd_conv_b, m_ssd_dt_bias, m_ssd_a_log, m_ssd_d, m_ssd_norm_w, m_ssd_w_out, m_conf_w_pw1, m_conf_b_pw1, m_conf_w_dw, m_conf_b_dw, m_conf_ln_w, m_conf_ln_b, m_conf_w_pw2, m_conf_b_pw2, m_ffn_w_up, m_ffn_conv_w, m_ffn_conv_b, m_ffn_w_down, m_final_norm_w, v_c_ctx, v_mod_w, v_mod_b, v_norm1_w, v_norm2_w, v_ssd_w_in, v_ssd_conv_w, v_ssd_conv_b, v_ssd_dt_bias, v_ssd_a_log, v_ssd_d, v_ssd_norm_w, v_ssd_w_out, v_conf_w_pw1, v_conf_b_pw1, v_conf_w_dw, v_conf_b_dw, v_conf_ln_w, v_conf_ln_b, v_conf_w_pw2, v_conf_b_pw2, v_ffn_w_up, v_ffn_conv_w, v_ffn_conv_b, v_ffn_w_down, v_final_norm_w):
    given = dict(locals())
    W = {n: given[n] for n in WEIGHTS}
    Mo = {n: given["m_" + n] for n in WEIGHTS}
    Vo = {n: given["v_" + n] for n in WEIGHTS}

    ax, ay, ac = lax.axis_index("x"), lax.axis_index("y"), lax.axis_index("c")
    chip = 2 * ax + ay
    dev = 4 * ax + 2 * ay + ac

    D = x.shape[-1]
    L, Lc = x.shape[1], ctx.shape[1]
    T0 = L + Lc
    H = ssd_a_log.shape[-1]
    DI = ssd_norm_w.shape[-1]
    P = DI // H
    CD = ssd_conv_b.shape[-1]
    N = SSD_STATE
    G = (CD - DI) // (2 * N)
    FH = ffn_conv_b.shape[-1]
    KS = ssd_conv_w.shape[1]
    KC = conf_w_dw.shape[1]
    ncc = Lc // SSD_CHUNK

    shard_b = {n: W[n].astype(BF16) for n in BIG}

    in_halves = shard_b["ssd_w_in"].reshape(2, D // 2, ssd_w_in.shape[-1])
    gather_a, token = _exchange4_start("gather_w_in_start", [in_halves], True, c, half=True)
    small_shard_shapes = [W[n].shape for n in SMALL_SHARDED]
    f1 = _allgather8("gather_small", _tie("tie_gather_w_in", _pack([c] + [W[n] for n in SMALL_SHARDED]), token))
    parts = _unpack(f1, [c.shape] + small_shard_shapes)
    Wf = dict(W)
    for n, p in zip(SMALL_SHARDED, parts[1:]):
        Wf[n] = _unshard(p[::2], SHARD_AXIS[n])
    c16 = jnp.concatenate([parts[0].reshape(N_DEV, D), c_ctx[None, :], jnp.zeros((16 - N_DEV - 1, D), F32)], axis=0)

    S_mod = mod_w.shape[-1]
    mod_b_shard = lax.dynamic_slice_in_dim(mod_b, chip * S_mod, S_mod, axis=1)[:, None, :]
    mod_part = _mod_fwd(c16, mod_w, mod_b_shard)
    f2 = _allgather8("gather_mod", mod_part.reshape(2 * 16, S_mod))
    mods = jnp.concatenate([f2[2 * k].reshape(2, 16, S_mod) for k in range(N_CHIPS)], axis=-1)
    my = lax.dynamic_slice_in_dim(mods, dev, 1, axis=1)[:, 0]
    sh1, sc1, g1, sh2, sc2, g2 = [[my[l, k * D:(k + 1) * D] for l in range(2)] for k in range(6)]
    csh1, csc1 = mods[0, N_DEV, 0:D], mods[0, N_DEV, D:2 * D]

    def full_weight(n, own, landed):
        if landed.ndim == own.ndim:
            ax = SHARD_AXIS[n]
            return lax.dynamic_update_slice_in_dim(landed, own, chip * own.shape[ax], ax)
        return _fill_own(landed, own, chip, True)

    xl = x[0]
    rows0 = (ctx[0], xl)
    n1w0, n1w1 = _par(norm1_w[0]), _par(norm1_w[1])
    sc_seg = jnp.stack([csc1, sc1[0]])[:, None, :]
    sh_seg = jnp.stack([csh1, sh1[0]])[:, None, :]

    a0 = _rw_fwd("l0_modnorm1", _f_modnorm, [], [n1w0, sc_seg, sh_seg], [D], T=T0, seg_rows=(Lc,), head=rows0,
                 out_dtypes=[BF16])
    rest = [n for n in BIG if n != "ssd_w_in"]
    for n in rest:
        a0 = _tie("tie_cast_" + n, a0, shard_b[n])
    (own_in,), (landed_in,) = _exchange4_wait("gather_w_in_wait", gather_a, a0)
    mine = _fill_own(landed_in, lax.dynamic_index_in_dim(own_in, ac, 0, keepdims=False), chip, True)
    (halves,) = _swap_sibling("swap_w_in", [mine], by_core=True)
    halves = lax.dynamic_update_index_in_dim(halves, mine, ac, 0)
    w_in = jnp.concatenate([halves[:, k].reshape(D, -1) for k in range(N_CHIPS)], axis=1)
    landed_in = halves
    def start_gather(tag, names, dep):
        handle, tok = _exchange4_start("gather_" + tag + "_start", [shard_b[n] for n in names], True, dep,
                                       axes=[1 if SHARD_AXIS[n] == 1 else None for n in names])
        return (names, handle), tok

    def finish_gather(tag, group, after):
        names, handle = group
        return {n: full_weight(n, own, g)
                for n, own, g in zip(names, *_exchange4_wait("gather_" + tag + "_wait", handle, after))}

    gather_b, token = start_gather("mix", ["ssd_w_out", "conf_w_pw1", "conf_w_pw2"], landed_in)
    gather_c, token = start_gather("ffn_up", ["ffn_w_up"], token)
    gather_d, token = start_gather("ffn_down", ["ffn_w_down"], token)
    a0 = _tie("tie_gather_rest", a0, token)
    proj = _mm(a0, w_in, name="l0_w_in")
    seg_taps = [(k - KS // 2, ("seg", Lc)) for k in range(KS)]
    xbc_pre, xbc = _conv_fwd("l0_conv", proj, DI, CD, Wf["ssd_conv_w"][0], ssd_conv_b, seg_taps, act=True)
    dt_raw = proj[:, DI + CD:]
    dt_bias = _par(ssd_dt_bias.reshape(1, 2 * H))
    dt = _rw_fwd("l0_softplus", _f_softplus, [dt_raw], [dt_bias], [2 * H])
    dt_t = dt.T
    dtr = (dt_t[:H, None, :], dt_t[H:, None, :])
    a_all = -jnp.exp(ssd_a_log.reshape(2, H, 1, 1))
    a_neg = (a_all[0], a_all[1])
    (y_f, y_b), s_enter = _ssd_fwd(xbc, DI, DI + G * N, dtr, a_neg, P, ncc)
    gate_rows = [y_f, y_b, (xbc, 0, DI, Lc), (proj, 0, DI, Lc)]
    d_rep = _par(jnp.repeat(ssd_d[0], P))
    ssd_nw = _par(ssd_norm_w[0])
    yn = _rw_fwd("l0_ssd_gate", _f_ssd_gate, gate_rows, [d_rep, ssd_nw], [DI], T=L, out_dtypes=[BF16])
    Wb = finish_gather("mix", gather_b, yn)
    w_out, w_pw1, w_pw2 = Wb["ssd_w_out"][0], Wb["conf_w_pw1"], Wb["conf_w_pw2"][0]
    mix0 = _mm(yn, w_out, name="l0_w_out")
    g1_0, g2_0, g1_1, g2_1 = _par(g1[0]), _par(g2[0]), _par(g1[1]), _par(g2[1])
    h1 = _rw_fwd("l0_res1", _f_gate_res, [xl, mix0], [g1_0], [D])
    w_up = finish_gather("ffn_up", gather_c, h1)["ffn_w_up"]
    w_dn_landed = []

    grid_taps = [((i - 1) * GRID_W + (j - 1), (None if j == 1 else ("col", j - 1))) for i in range(3) for j in range(3)]

    def ffn_fwd(l, h, tag):
        a = _rw_fwd(tag + "_modnorm2", _f_modnorm, [h], [_par(norm2_w[l]), _par(sc2[l]), _par(sh2[l])], [D],
                    out_dtypes=[BF16])
        hh = _mm(a, w_up, b_lead=l, b_shards=N_CHIPS, name=tag + "_w_up")
        gc = _conv_fwd(tag + "_ffn_conv", hh, FH, FH, Wf["ffn_conv_w"][l].reshape(9, FH), ffn_conv_b[l][None, :],
                       grid_taps)
        act = _rw_fwd(tag + "_act", _f_ffn_act, [(hh, 0, FH), gc], [], [FH], col_tile=_tile(FH, 1536),
                      out_dtypes=[BF16])
        if not w_dn_landed:
            w_dn_landed.append(finish_gather("ffn_down", gather_d, act)["ffn_w_down"])
        dn = _mm(act, w_dn_landed[0], b_lead=l, name=tag + "_w_down")
        return a, hh, gc, act, dn

    a1, hh0, gc0, act0, dn0 = ffn_fwd(0, h1, "l0")
    h2 = _rw_fwd("l0_res2", _f_gate_res, [h1, dn0], [g2_0], [D])

    a2 = _rw_fwd("l1_modnorm1", _f_modnorm, [h2], [n1w1, _par(sc1[1]), _par(sh1[1])], [D], out_dtypes=[BF16])
    pw = _mm(a2, w_pw1, b_lead=0, b_shards=N_CHIPS, name="l1_pw1")
    b_pw1 = Wf["conf_b_pw1"][0]
    glu = _rw_fwd("l1_glu", _f_glu, [(pw, 0, D), (pw, D, D)], [_par(b_pw1[:D]), _par(b_pw1[D:])], [D])
    conf_taps = [(k - KC // 2, None) for k in range(KC)]
    cv = _conv_fwd("l1_conv", glu, 0, D, Wf["conf_w_dw"][0], Wf["conf_b_dw"], conf_taps)
    ln_w, ln_b = _par(Wf["conf_ln_w"][0]), _par(Wf["conf_ln_b"][0])
    ls = _rw_fwd("l1_ln_silu", _f_ln_silu, [cv], [ln_w, ln_b], [D], out_dtypes=[BF16])
    p2 = _mm(ls, w_pw2, name="l1_pw2")
    b_pw2 = _par(Wf["conf_b_pw2"][0])
    h3 = _rw_fwd("l1_res1", _f_gate_res_bias, [h2, p2], [g1_1, b_pw2], [D])
    a3, hh1, gc1, act1, dn1 = ffn_fwd(1, h3, "l1")
    h4 = _rw_fwd("l1_res2", _f_gate_res, [h3, dn1], [g2_1], [D])

    fnw = final_norm_w[None, :]
    tgt = loss_target[0]
    loss_local = _loss_fwd(h4, tgt, fnw)[0, 0]

    G_full = {}
    reduces = {}

    def start_reduce(tag, items, dep):
        def blocks_of(g, ax):
            if g.ndim == 3:
                return g
            return g.reshape(N_CHIPS, g.shape[0] // N_CHIPS, g.shape[1]) if ax == 0 else _to_blocks(g, ax)

        blocks = [blocks_of(g, ax).astype(BF16) for _, g, ax in items]
        handle, tok = _exchange4_start("reduce_" + tag + "_start", blocks, False, dep)
        reduces[tag] = ([n for n, _, _ in items], handle)
        return tok
    ones = jnp.ones((L, 1), F32)
    (dh4,), (dfnw,) = _rw_bwd("loss_bwd", _f_loss_rows, [h4, tgt], [_par(final_norm_w)], [ones],
                              row_grad=[True, False], par_grad=[True])
    G_full["final_norm_w"] = dfnw.reshape(D)

    def ffn_bwd(l, h, saved, g2_l, dh_out, tag):
        a, hh, gc, act, dn = saved
        (ddn,), (dg2,) = _rw_bwd(tag + "_res2_bwd", _f_gate, [dn], [g2_l], [dh_out],
                                 row_grad=[True], par_grad=[True], row_dtypes=[BF16])
        dact = _mm(ddn, w_dn_landed[0], b_lead=l, tb=True, name=tag + "_w_down_dx")
        dwdn = _mm(act, ddn, ta=True, name=tag + "_w_down_dw", out_dtype=BF16)
        (dval, dgc), _ = _rw_bwd(tag + "_act_bwd", _f_ffn_act, [(hh, 0, FH), gc], [], [dact],
                                 row_grad=[True, True], par_grad=[], col_tile=_tile(FH, 1536), row_dtypes=[BF16, F32])
        dgin, dcw, dcb = _conv_bwd(tag + "_ffn_conv_bwd", hh, FH, FH, Wf["ffn_conv_w"][l].reshape(9, FH), dgc,
                                   grid_taps, du_dtype=BF16)
        dhh = jnp.concatenate([dval, dgin], axis=1)
        da = _mm(dhh, w_up, b_lead=l, b_shards=N_CHIPS, tb=True, name=tag + "_w_up_dx")
        dwup = _mm(a, dhh, ta=True, name=tag + "_w_up_dw", out_dtype=BF16, col_blocks=N_CHIPS)
        (dh,), (dn2w, dsc2, dsh2) = _rw_bwd(
            tag + "_modnorm2_bwd", _f_modnorm, [h], [_par(norm2_w[l]), _par(sc2[l]), _par(sh2[l])], [da],
            row_grad=[True], par_grad=[True, True, True], add=dh_out)
        return dh, dict(w_down=dwdn, w_up=dwup, conv_w=dcw.reshape(3, 3, FH), conv_b=dcb.reshape(FH),
                        n2w=dn2w.reshape(D), sc2=dsc2.reshape(D), sh2=dsh2.reshape(D), g2=dg2.reshape(D))

    dh3, gf1 = ffn_bwd(1, h3, (a3, hh1, gc1, act1, dn1), g2_1, dh4, "l1")
    (dp2,), (dg1_1, db_pw2) = _rw_bwd("l1_res1_bwd", _f_gate_bias, [p2], [g1_1, b_pw2], [dh3],
                                      row_grad=[True], par_grad=[True, True], row_dtypes=[BF16])
    dls = _mm(dp2, w_pw2, tb=True, name="l1_pw2_dx")
    dw_pw2 = _mm(ls, dp2, ta=True, name="l1_pw2_dw", out_dtype=BF16)
    (dcv,), (dln_w, dln_b) = _rw_bwd("l1_ln_silu_bwd", _f_ln_silu, [cv], [ln_w, ln_b], [dls],
                                     row_grad=[True], par_grad=[True, True])
    dglu, dw_dw, db_dw = _conv_bwd("l1_conv_bwd", glu, 0, D, Wf["conf_w_dw"][0], dcv, conf_taps)
    (dpa, dpg), (dba, dbg) = _rw_bwd("l1_glu_bwd", _f_glu, [(pw, 0, D), (pw, D, D)],
                                     [_par(b_pw1[:D]), _par(b_pw1[D:])], [dglu],
                                     row_grad=[True, True], par_grad=[True, True], row_dtypes=[BF16, BF16])
    dpw = jnp.concatenate([dpa, dpg], axis=1)
    da2 = _mm(dpw, w_pw1, b_lead=0, b_shards=N_CHIPS, tb=True, name="l1_pw1_dx")
    dw_pw1 = _mm(a2, dpw, ta=True, name="l1_pw1_dw", out_dtype=BF16, col_blocks=N_CHIPS)
    (dh2,), (dn1w1, dsc1_1, dsh1_1) = _rw_bwd(
        "l1_modnorm1_bwd", _f_modnorm, [h2], [n1w1, _par(sc1[1]), _par(sh1[1])], [da2],
        row_grad=[True], par_grad=[True, True, True], add=dh3)
    G_full["conf_b_pw2"] = db_pw2.reshape(1, D)
    G_full["conf_ln_w"], G_full["conf_ln_b"] = dln_w.reshape(1, D), dln_b.reshape(1, D)
    G_full["conf_w_dw"], G_full["conf_b_dw"] = dw_dw[None], db_dw.reshape(1, D)
    G_full["conf_b_pw1"] = jnp.concatenate([dba.reshape(1, D), dbg.reshape(1, D)], axis=1)

    token = start_reduce("l1", [("conf_w_pw2", dw_pw2, 0), ("conf_w_pw1", dw_pw1, 1), ("ffn_w_up1", gf1["w_up"], 1),
                                ("ffn_w_down1", gf1["w_down"], 0)], dw_pw2)
    dh2 = _tie("tie_reduce_l1", dh2, token)
    dh1, gf0 = ffn_bwd(0, h1, (a1, hh0, gc0, act0, dn0), g2_0, dh2, "l0")
    G_full["ffn_conv_w"] = jnp.stack([gf0["conv_w"], gf1["conv_w"]])
    G_full["ffn_conv_b"] = jnp.stack([gf0["conv_b"], gf1["conv_b"]])

    (dmix,), (dg1_0,) = _rw_bwd("l0_res1_bwd", _f_gate, [mix0], [g1_0], [dh1],
                                row_grad=[True], par_grad=[True], row_dtypes=[BF16])
    dyn = _mm(dmix, w_out, tb=True, name="l0_w_out_dx")
    dw_out = _mm(yn, dmix, ta=True, name="l0_w_out_dw", out_dtype=BF16)
    token = start_reduce("l0", [("ffn_w_up0", gf0["w_up"], 1), ("ffn_w_down0", gf0["w_down"], 0),
                                ("ssd_w_out", dw_out, 0)], dw_out)
    dyn = _tie("tie_reduce_l0", dyn, token)
    (dy_lat, dxs_gate, dz_lat), (dd_rep, dssd_nw) = _rw_bwd(
        "l0_ssd_gate_bwd", _f_ssd_gate, gate_rows, [d_rep, ssd_nw], [dyn],
        row_grad=[True, False, True, True], par_grad=[True, True], T=L, row_dtypes=[F32, F32, BF16])
    g_f, g_b = _ssd_bwd(xbc, DI, DI + G * N, dtr, a_neg, s_enter, dy_lat, P, ncc)
    silu_bwd = functools.partial(_rw_bwd, f=_silu, pars=[], row_grad=[True], par_grad=[], T=T0)
    (dxs_pre,), _ = silu_bwd("l0_silu_bwd_x", rows=[(xbc_pre, 0, DI)], cot_fn=lambda p, q, r: p + q + r,
                             cots=[g_f[0], g_b[0], (dxs_gate, 0, DI, -Lc)],
                             col_tile=_tile(DI, 1024))
    (db_pre,), _ = silu_bwd("l0_silu_bwd_b", rows=[(xbc_pre, DI, G * N)], cot_fn=lambda p, q: p + q,
                            cots=[g_f[1], g_b[1]], col_tile=_tile(G * N, 1024))
    (dc_pre,), _ = silu_bwd("l0_silu_bwd_c", rows=[(xbc_pre, DI + G * N, G * N)], cot_fn=lambda p, q: p + q,
                            cots=[g_f[2], g_b[2]], col_tile=_tile(G * N, 1024))
    conv_w0 = Wf["ssd_conv_w"][0]
    pieces = []
    for tag, off, width, g_pre in (("x", 0, DI, dxs_pre), ("b", DI, G * N, db_pre), ("c", DI + G * N, G * N, dc_pre)):
        pieces.append(_conv_bwd("l0_conv_bwd_" + tag, proj, DI + off, width, conv_w0[:, off:off + width], g_pre,
                                seg_taps, du_dtype=BF16))
    dconv_in = [p[0] for p in pieces]
    dcw0 = jnp.concatenate([p[1] for p in pieces], axis=1)
    dcb0 = jnp.concatenate([p[2] for p in pieces], axis=1)
    ddt = jnp.concatenate([g_f[3][:, 0, :].T, g_b[3][:, 0, :].T], axis=1)
    (ddt_raw,), (ddt_bias,) = _rw_bwd("l0_softplus_bwd", _f_softplus, [dt_raw], [dt_bias], [ddt],
                                      row_grad=[True], par_grad=[True], row_dtypes=[BF16])
    dproj = jnp.concatenate([jnp.pad(dz_lat, ((Lc, 0), (0, 0))), *dconv_in, ddt_raw], axis=1)
    da0 = _mm(dproj, w_in, tb=True, name="l0_w_in_dx")
    dw_in = _mm(a0, dproj, ta=True, name="l0_w_in_dw", out_dtype=BF16)
    token = start_reduce("in", [("ssd_w_in", dw_in, 1)], dw_in)
    da0 = _tie("tie_reduce_in", da0, token)
    (dhcat,), (dn1w0, dsc_seg, dsh_seg) = _rw_bwd(
        "l0_modnorm1_bwd", _f_modnorm, [], [n1w0, sc_seg, sh_seg], [da0], T=T0, head=rows0,
        row_grad=[True], par_grad=[True, True, True], seg_rows=(Lc,), add=(dh1, 0, D, -Lc), skip_rows=Lc)
    grad_x = dhcat[None]

    da_heads = jnp.stack([g[4][..., 0, 0].sum(axis=1).reshape(H) for g in (g_f, g_b)])[None]
    G_full["ssd_a_log"] = da_heads * (-jnp.exp(ssd_a_log))
    G_full["ssd_dt_bias"] = ddt_bias.reshape(1, 2, H)
    G_full["ssd_d"] = dd_rep.reshape(H, P).sum(axis=1)[None]
    G_full["ssd_norm_w"] = dssd_nw.reshape(1, DI)
    G_full["ssd_conv_w"], G_full["ssd_conv_b"] = dcw0[None], dcb0.reshape(1, CD)
    G_full["norm1_w"] = jnp.stack([dn1w0.reshape(D), dn1w1.reshape(D)])
    G_full["norm2_w"] = jnp.stack([gf0["n2w"], gf1["n2w"]])

    zD = jnp.zeros((D,), F32)
    dm_own = jnp.stack([
        jnp.concatenate([dsh_seg[1, 0], dsc_seg[1, 0], dg1_0.reshape(D), gf0["sh2"], gf0["sc2"], gf0["g2"]]),
        jnp.concatenate([dsh1_1.reshape(D), dsc1_1.reshape(D), dg1_1.reshape(D), gf1["sh2"], gf1["sc2"], gf1["g2"]]),
    ])
    dmc_own = jnp.concatenate([dsh_seg[0, 0], dsc_seg[0, 0], zD, zD, zD, zD])

    out = {}

    def finish_reduce(tags, after, swap_name):
        partial = {}
        for tag in tags:
            names, handle = reduces[tag]
            blocks, landed = _exchange4_wait("reduce_" + tag + "_wait", handle, after)
            for n, blk, own in zip(names, landed, blocks):
                r = _fill_own(blk, own, chip, False)
                partial[n] = _sum_leading("sum4_" + n, r.reshape(N_CHIPS, -1, r.shape[-1]),
                                          (0, 1, 2, 3), out_dtype=BF16).reshape(r.shape[1:])
        for n in ("ffn_w_up", "ffn_w_down"):
            if n + "0" in partial:
                partial[n] = jnp.stack([partial.pop(n + "0"), partial.pop(n + "1")])
        names = [n for n in BIG if n in partial]
        mine = [partial[n].reshape(W[n].shape) for n in names]
        for n, own, sib in zip(names, mine, _swap_sibling(swap_name, mine)):
            out[n] = _adamw("adamw_" + n, W[n], Mo[n], Vo[n], own, sib)
        return names

    early = finish_reduce(["l1", "l0"], dhcat, "swap_grads_early")

    small_sum_names = [n for n in SMALL if n not in ("c_ctx", "mod_b")]
    sum_part = [G_full[n] for n in small_sum_names] + [dmc_own, loss_local.reshape(1)]
    packed = _tie("tie_small_grads", _pack(sum_part + [dm_own]), out[early[-1]][1])
    gat = _allgather8("gather_small_grads", packed)
    total = _sum_leading("sum_small_grads", gat, tuple(range(N_DEV)))
    summed = _unpack(total, [a.shape for a in sum_part])
    Gs = dict(zip(small_sum_names, summed[:-2]))
    dmc_tot, loss = summed[-2], summed[-1][0]
    dm_all = _unpack(gat, [a.shape for a in sum_part] + [dm_own.shape])[-1].transpose(1, 0, 2)
    dm16 = jnp.concatenate([dm_all, jnp.stack([dmc_tot, jnp.zeros_like(dmc_tot)])[:, None, :],
                            jnp.zeros((2, 16 - N_DEV - 1, 6 * D), F32)], axis=1)
    Gs["mod_b"] = _sum_leading("sum_mod_b", dm16.transpose(1, 0, 2).reshape(16, 2 * 6 * D // LANE, LANE),
                               tuple(range(N_DEV + 1))).reshape(2, 6 * D)

    dm16_shard = lax.dynamic_slice_in_dim(dm16, chip * S_mod, S_mod, axis=2)
    ds16 = _mm(dm16_shard[0], mod_w.reshape(2 * D, S_mod), tb=True, precision=HIGHEST, name="c_ctx_dx")
    sig = jax.nn.sigmoid(c_ctx)
    dcc_part = ds16[N_DEV, :D] * (sig * (1.0 + c_ctx * (1.0 - sig)))
    gat_cc = _allgather8("gather_c_ctx_grad", _pack([dcc_part]))
    Gs["c_ctx"] = _sum_leading("sum_c_ctx_grad", gat_cc, (0, 2, 4, 6)).reshape(-1)[:D]

    s16t = _silu(c16).T
    out["mod_w"] = _mod_w_update(s16t, dm16_shard, mod_w, m_mod_w, v_mod_w)
    finish_reduce(["in"], out["mod_w"][0], "swap_grads_late")

    def own(n, full):
        if n in SHARD_AXIS:
            size = W[n].shape[SHARD_AXIS[n]]
            return lax.dynamic_slice_in_dim(full, chip * size, size, axis=SHARD_AXIS[n])
        return full

    def two_d(a):
        return a.reshape(1, -1) if a.ndim == 1 else a

    g_small = [own(n, Gs[n].reshape(Wf[n].shape)) for n in SMALL]
    res = _adamw_many("adamw_small", [two_d(W[n]) for n in SMALL], [two_d(Mo[n]) for n in SMALL],
                      [two_d(Vo[n]) for n in SMALL], [two_d(g) for g in g_small])
    for n, r in zip(SMALL, res):
        out[n] = tuple(t.reshape(W[n].shape) for t in r)

    grads = [out[n][0] for n in WEIGHTS]
    deltas = [out[n][1] for n in WEIGHTS]
    new_m = [out[n][2] for n in WEIGHTS]
    new_v = [out[n][3] for n in WEIGHTS]
    return (loss, grad_x, *grads, *deltas, *new_m, *new_v)
```
